```python
import jax, jax.numpy as jnp
from jax import lax
import numpy as np

D_MODEL = 1024
BATCH = 8
SEQ = 4096
DEPTH = 4

N_META = 16
D_FF = 4 * D_MODEL
D_CONV = D_MODEL // 2
CONV_WIDTH = 31
N_POOL_GROUPS = 4
POOL_WINDOWS = (2, 4, 8, 16)
D_POOL = D_MODEL // 2
POOL_GROUP_DIM = D_POOL // N_POOL_GROUPS
D_EVEN_IN = 2 * D_CONV + D_POOL
HGRN_HEAD_DIM = 128
HGRN_HEADS = D_MODEL // HGRN_HEAD_DIM
D_HGRN = HGRN_HEADS * HGRN_HEAD_DIM
CHUNK = 64
N_EVEN = (DEPTH + 1) // 2
N_ODD = DEPTH // 2
EPS = 1e-6

kernel_name = 'hybrid_conv_pool_hgrn2_trunk'


def _rmsnorm(x, g):
    xf = x.astype(jnp.float32)
    y = xf * lax.rsqrt(jnp.mean(xf * xf, axis=-1, keepdims=True) + EPS)
    return (y * g.astype(jnp.float32)).astype(x.dtype)


def _layernorm(x, g, b):
    xf = x.astype(jnp.float32)
    mu = jnp.mean(xf, axis=-1, keepdims=True)
    xc = xf - mu
    y = xc * lax.rsqrt(jnp.mean(xc * xc, axis=-1, keepdims=True) + EPS)
    return (y * g.astype(jnp.float32) + b.astype(jnp.float32)).astype(x.dtype)


def _conv_mixer(val, gate, conv_w, conv_b, ln_g, ln_b):
    a = val * jax.nn.sigmoid(gate)
    y = lax.conv_general_dilated(
        a, conv_w[:, None, :].astype(a.dtype), window_strides=(1,),
        padding=[(CONV_WIDTH - 1, 0)], dimension_numbers=('NWC', 'WIO', 'NWC'),
        feature_group_count=D_CONV) + conv_b
    return jax.nn.silu(_layernorm(y, ln_g, ln_b))


def _causal_window_mean(x, w):
    L = x.shape[1]
    cs = jnp.cumsum(x.astype(jnp.float32), axis=1)
    cs0 = jnp.pad(cs, ((0, 0), (1, 0), (0, 0)))
    lower = jnp.pad(cs0[:, :L + 1 - w], ((0, 0), (w - 1, 0), (0, 0)))
    count = jnp.minimum(jnp.arange(1, L + 1, dtype=jnp.float32), float(w))
    return ((cs - lower) / count[None, :, None]).astype(x.dtype)


def _pool_mixer(u, pool_w, pool_b, pool_scale):
    Bn, L, _ = u.shape
    ug = u.reshape(Bn, L, N_POOL_GROUPS, POOL_GROUP_DIM)
    pooled = jnp.stack([_causal_window_mean(ug[:, :, gi], w) for gi, w in enumerate(POOL_WINDOWS)], axis=2)
    y = jnp.einsum('blgc,gcd->blgd', pooled - ug, pool_w) + pool_b
    return y.reshape(Bn, L, D_POOL) * pool_scale


def _hgrn2_chunk_scan(q, k, v, logf):
    C = q.shape[3]
    causal = jnp.tril(jnp.ones((C, C), dtype=bool))

    def step(S, inp):
        qc, kc, vc, lfc = inp
        b = jnp.cumsum(lfc, axis=2)
        diff = b[:, :, :, None, :] - b[:, :, None, :, :]
        decay = jnp.exp(jnp.where(causal[:, :, None], diff, -jnp.inf))
        scores = jnp.einsum('bhtk,bhsk,bhtsk->bhts', qc, kc, decay)
        o = (jnp.einsum('bhts,bhsv->bhtv', scores, vc)
             + jnp.einsum('bhtk,bhkv->bhtv', qc * jnp.exp(b), S))
        b_last = b[:, :, -1:, :]
        S = (jnp.exp(b_last[:, :, 0, :])[..., None] * S
             + jnp.einsum('bhsk,bhsv->bhkv', kc * jnp.exp(b_last - b), vc))
        return S, o

    S0 = jnp.zeros((q.shape[1], q.shape[2], q.shape[4], v.shape[4]), jnp.float32)
    _, o = lax.scan(step, S0, (q, k, v, logf))
    return o


def _hgrn2_mixer(u, lb, gnorm_g):
    Bn, L, _ = u.shape
    q, f, i, g = jnp.split(u, 4, axis=-1)
    q = jax.nn.silu(q.astype(jnp.float32))
    forget = lb + (1.0 - lb) * jax.nn.sigmoid(f.astype(jnp.float32))
    k = 1.0 - forget
    logf = jnp.log(forget)
    v = i.astype(jnp.float32)
    pad = CHUNK - N_META
    Lp = L + pad
    n_chunks = Lp // CHUNK

    def to_chunks(t):
        t = jnp.pad(t, ((0, 0), (pad, 0), (0, 0)))
        t = t.reshape(Bn, n_chunks, CHUNK, HGRN_HEADS, HGRN_HEAD_DIM)
        return t.transpose(1, 0, 3, 2, 4)

    o = _hgrn2_chunk_scan(to_chunks(q), to_chunks(k), to_chunks(v), to_chunks(logf))
    o = o.transpose(1, 0, 3, 2, 4).reshape(Bn, Lp, HGRN_HEADS, HGRN_HEAD_DIM)[:, pad:]
    gh = g.reshape(Bn, L, HGRN_HEADS, HGRN_HEAD_DIM).astype(jnp.float32)
    o = _rmsnorm(o, gnorm_g) * jax.nn.silu(gh)
    return o.reshape(Bn, L, D_HGRN).astype(u.dtype)


def _fwd_setup_inputs(seed: int = 0) -> dict:
    key = jax.random.key(seed)
    ks = jax.random.split(key, 24)
    f32 = jnp.float32
    nrm = lambda k, shape, s: jax.random.normal(k, shape, f32) * s
    return {
        'x': nrm(ks[0], (BATCH, SEQ, D_MODEL), 1.0),
        'meta_tokens': nrm(ks[1], (N_META, D_MODEL), 1.0),
        'mix_norm_g': 1.0 + nrm(ks[2], (DEPTH, D_MODEL), 0.02),
        'mlp_norm_g': 1.0 + nrm(ks[3], (DEPTH, D_MODEL), 0.02),
        'final_norm_g': 1.0 + nrm(ks[4], (D_MODEL,), 0.02),
        'ev_w_in': nrm(ks[5], (N_EVEN, D_MODEL, D_EVEN_IN), D_MODEL ** -0.5),
        'ev_conv_w': nrm(ks[6], (N_EVEN, CONV_WIDTH, D_CONV), CONV_WIDTH ** -0.5),
        'ev_conv_b': nrm(ks[7], (N_EVEN, D_CONV), 0.01),
        'ev_ln_g': 1.0 + nrm(ks[8], (N_EVEN, D_CONV), 0.02),
        'ev_ln_b': nrm(ks[9], (N_EVEN, D_CONV), 0.01),
        'ev_pool_w': nrm(ks[10], (N_EVEN, N_POOL_GROUPS, POOL_GROUP_DIM, POOL_GROUP_DIM), POOL_GROUP_DIM ** -0.5),
        'ev_pool_b': nrm(ks[11], (N_EVEN, N_POOL_GROUPS, POOL_GROUP_DIM), 0.01),
        'ev_pool_scale': 1.0 + nrm(ks[12], (N_EVEN, D_POOL), 0.02),
        'ev_w_out': nrm(ks[13], (N_EVEN, D_CONV + D_POOL, D_MODEL), (D_CONV + D_POOL) ** -0.5),
        'od_w_in': nrm(ks[14], (N_ODD, D_MODEL, 4 * D_HGRN), D_MODEL ** -0.5),
        'od_gnorm_g': 1.0 + nrm(ks[15], (N_ODD, HGRN_HEAD_DIM), 0.02),
        'od_w_out': nrm(ks[16], (N_ODD, D_HGRN, D_MODEL), D_HGRN ** -0.5),
        'lb_param': nrm(ks[17], (DEPTH, D_HGRN), 1.0),
        'mlp_w1': nrm(ks[18], (DEPTH, D_MODEL, D_FF), D_MODEL ** -0.5),
        'mlp_w2': nrm(ks[19], (DEPTH, D_FF, D_MODEL), D_FF ** -0.5),
    }


def _fwd_reference(x, meta_tokens, mix_norm_g, mlp_norm_g, final_norm_g,
              ev_w_in, ev_conv_w, ev_conv_b, ev_ln_g, ev_ln_b,
              ev_pool_w, ev_pool_b, ev_pool_scale, ev_w_out,
              od_w_in, od_gnorm_g, od_w_out, lb_param, mlp_w1, mlp_w2):
    Bn = x.shape[0]
    meta = jnp.broadcast_to(meta_tokens[None].astype(x.dtype), (Bn, N_META, D_MODEL))
    h = jnp.concatenate([meta, x], axis=1)
    lb_all = jnp.cumsum(jax.nn.softmax(lb_param.astype(jnp.float32), axis=0), axis=0)
    lb_all = lb_all - lb_all[0]
    for layer in range(DEPTH):
        j = layer // 2
        n = _rmsnorm(h, mix_norm_g[layer])
        if layer % 2 == 0:
            u = n @ ev_w_in[j]
            val, gate, pin = jnp.split(u, [D_CONV, 2 * D_CONV], axis=-1)
            ya = _conv_mixer(val, gate, ev_conv_w[j], ev_conv_b[j], ev_ln_g[j], ev_ln_b[j])
            yb = _pool_mixer(pin, ev_pool_w[j], ev_pool_b[j], ev_pool_scale[j])
            h = h + jnp.concatenate([ya, yb], axis=-1) @ ev_w_out[j]
        else:
            u = n @ od_w_in[j]
            y = _hgrn2_mixer(u, lb_all[layer], od_gnorm_g[j])
            h = h + y @ od_w_out[j]
        n = _rmsnorm(h, mlp_norm_g[layer])
        h = h + jnp.square(jax.nn.relu(n @ mlp_w1[layer])) @ mlp_w2[layer]
    return _rmsnorm(h, final_norm_g)[:, N_META:]


import jax as _jax
import jax.numpy as _jnp

TWIN_FORMAT = 'train_step'
FWD_PARAMS = ['x', 'meta_tokens', 'mix_norm_g', 'mlp_norm_g', 'final_norm_g', 'ev_w_in', 'ev_conv_w', 'ev_conv_b', 'ev_ln_g', 'ev_ln_b', 'ev_pool_w', 'ev_pool_b', 'ev_pool_scale', 'ev_w_out', 'od_w_in', 'od_gnorm_g', 'od_w_out', 'lb_param', 'mlp_w1', 'mlp_w2']
TWIN_WEIGHTS = ['meta_tokens', 'mix_norm_g', 'mlp_norm_g', 'final_norm_g', 'ev_w_in', 'ev_conv_w', 'ev_conv_b', 'ev_ln_g', 'ev_ln_b', 'ev_pool_w', 'ev_pool_b', 'ev_pool_scale', 'ev_w_out', 'od_w_in', 'od_gnorm_g', 'od_w_out', 'lb_param', 'mlp_w1', 'mlp_w2']
TWIN_DIFF_INPUT = 'x'
TWIN_INPUTS = ['x', 'meta_tokens', 'mix_norm_g', 'mlp_norm_g', 'final_norm_g', 'ev_w_in', 'ev_conv_w', 'ev_conv_b', 'ev_ln_g', 'ev_ln_b', 'ev_pool_w', 'ev_pool_b', 'ev_pool_scale', 'ev_w_out', 'od_w_in', 'od_gnorm_g', 'od_w_out', 'lb_param', 'mlp_w1', 'mlp_w2', 'loss_target', 'm_meta_tokens', 'm_mix_norm_g', 'm_mlp_norm_g', 'm_final_norm_g', 'm_ev_w_in', 'm_ev_conv_w', 'm_ev_conv_b', 'm_ev_ln_g', 'm_ev_ln_b', 'm_ev_pool_w', 'm_ev_pool_b', 'm_ev_pool_scale', 'm_ev_w_out', 'm_od_w_in', 'm_od_gnorm_g', 'm_od_w_out', 'm_lb_param', 'm_mlp_w1', 'm_mlp_w2', 'v_meta_tokens', 'v_mix_norm_g', 'v_mlp_norm_g', 'v_final_norm_g', 'v_ev_w_in', 'v_ev_conv_w', 'v_ev_conv_b', 'v_ev_ln_g', 'v_ev_ln_b', 'v_ev_pool_w', 'v_ev_pool_b', 'v_ev_pool_scale', 'v_ev_w_out', 'v_od_w_in', 'v_od_gnorm_g', 'v_od_w_out', 'v_lb_param', 'v_mlp_w1', 'v_mlp_w2']
TWIN_OUTPUTS = ['loss', 'grad_x', 'grad_meta_tokens', 'grad_mix_norm_g', 'grad_mlp_norm_g', 'grad_final_norm_g', 'grad_ev_w_in', 'grad_ev_conv_w', 'grad_ev_conv_b', 'grad_ev_ln_g', 'grad_ev_ln_b', 'grad_ev_pool_w', 'grad_ev_pool_b', 'grad_ev_pool_scale', 'grad_ev_w_out', 'grad_od_w_in', 'grad_od_gnorm_g', 'grad_od_w_out', 'grad_lb_param', 'grad_mlp_w1', 'grad_mlp_w2', 'delta_meta_tokens', 'delta_mix_norm_g', 'delta_mlp_norm_g', 'delta_final_norm_g', 'delta_ev_w_in', 'delta_ev_conv_w', 'delta_ev_conv_b', 'delta_ev_ln_g', 'delta_ev_ln_b', 'delta_ev_pool_w', 'delta_ev_pool_b', 'delta_ev_pool_scale', 'delta_ev_w_out', 'delta_od_w_in', 'delta_od_gnorm_g', 'delta_od_w_out', 'delta_lb_param', 'delta_mlp_w1', 'delta_mlp_w2', 'new_m_meta_tokens', 'new_m_mix_norm_g', 'new_m_mlp_norm_g', 'new_m_final_norm_g', 'new_m_ev_w_in', 'new_m_ev_conv_w', 'new_m_ev_conv_b', 'new_m_ev_ln_g', 'new_m_ev_ln_b', 'new_m_ev_pool_w', 'new_m_ev_pool_b', 'new_m_ev_pool_scale', 'new_m_ev_w_out', 'new_m_od_w_in', 'new_m_od_gnorm_g', 'new_m_od_w_out', 'new_m_lb_param', 'new_m_mlp_w1', 'new_m_mlp_w2', 'new_v_meta_tokens', 'new_v_mix_norm_g', 'new_v_mlp_norm_g', 'new_v_final_norm_g', 'new_v_ev_w_in', 'new_v_ev_conv_w', 'new_v_ev_conv_b', 'new_v_ev_ln_g', 'new_v_ev_ln_b', 'new_v_ev_pool_w', 'new_v_ev_pool_b', 'new_v_ev_pool_scale', 'new_v_ev_w_out', 'new_v_od_w_in', 'new_v_od_gnorm_g', 'new_v_od_w_out', 'new_v_lb_param', 'new_v_mlp_w1', 'new_v_mlp_w2']
TWIN_LEAF_KINDS = {'loss': 'loss', 'grad_x': 'grad_x', 'grad_meta_tokens': 'grad_w', 'grad_mix_norm_g': 'grad_w', 'grad_mlp_norm_g': 'grad_w', 'grad_final_norm_g': 'grad_w', 'grad_ev_w_in': 'grad_w', 'grad_ev_conv_w': 'grad_w', 'grad_ev_conv_b': 'grad_w', 'grad_ev_ln_g': 'grad_w', 'grad_ev_ln_b': 'grad_w', 'grad_ev_pool_w': 'grad_w', 'grad_ev_pool_b': 'grad_w', 'grad_ev_pool_scale': 'grad_w', 'grad_ev_w_out': 'grad_w', 'grad_od_w_in': 'grad_w', 'grad_od_gnorm_g': 'grad_w', 'grad_od_w_out': 'grad_w', 'grad_lb_param': 'grad_w', 'grad_mlp_w1': 'grad_w', 'grad_mlp_w2': 'grad_w', 'delta_meta_tokens': 'delta_w', 'delta_mix_norm_g': 'delta_w', 'delta_mlp_norm_g': 'delta_w', 'delta_final_norm_g': 'delta_w', 'delta_ev_w_in': 'delta_w', 'delta_ev_conv_w': 'delta_w', 'delta_ev_conv_b': 'delta_w', 'delta_ev_ln_g': 'delta_w', 'delta_ev_ln_b': 'delta_w', 'delta_ev_pool_w': 'delta_w', 'delta_ev_pool_b': 'delta_w', 'delta_ev_pool_scale': 'delta_w', 'delta_ev_w_out': 'delta_w', 'delta_od_w_in': 'delta_w', 'delta_od_gnorm_g': 'delta_w', 'delta_od_w_out': 'delta_w', 'delta_lb_param': 'delta_w', 'delta_mlp_w1': 'delta_w', 'delta_mlp_w2': 'delta_w', 'new_m_meta_tokens': 'new_m', 'new_m_mix_norm_g': 'new_m', 'new_m_mlp_norm_g': 'new_m', 'new_m_final_norm_g': 'new_m', 'new_m_ev_w_in': 'new_m', 'new_m_ev_conv_w': 'new_m', 'new_m_ev_conv_b': 'new_m', 'new_m_ev_ln_g': 'new_m', 'new_m_ev_ln_b': 'new_m', 'new_m_ev_pool_w': 'new_m', 'new_m_ev_pool_b': 'new_m', 'new_m_ev_pool_scale': 'new_m', 'new_m_ev_w_out': 'new_m', 'new_m_od_w_in': 'new_m', 'new_m_od_gnorm_g': 'new_m', 'new_m_od_w_out': 'new_m', 'new_m_lb_param': 'new_m', 'new_m_mlp_w1': 'new_m', 'new_m_mlp_w2': 'new_m', 'new_v_meta_tokens': 'new_v', 'new_v_mix_norm_g': 'new_v', 'new_v_mlp_norm_g': 'new_v', 'new_v_final_norm_g': 'new_v', 'new_v_ev_w_in': 'new_v', 'new_v_ev_conv_w': 'new_v', 'new_v_ev_conv_b': 'new_v', 'new_v_ev_ln_g': 'new_v', 'new_v_ev_ln_b': 'new_v', 'new_v_ev_pool_w': 'new_v', 'new_v_ev_pool_b': 'new_v', 'new_v_ev_pool_scale': 'new_v', 'new_v_ev_w_out': 'new_v', 'new_v_od_w_in': 'new_v', 'new_v_od_gnorm_g': 'new_v', 'new_v_od_w_out': 'new_v', 'new_v_lb_param': 'new_v', 'new_v_mlp_w1': 'new_v', 'new_v_mlp_w2': 'new_v'}


def _forward(args):
    return _fwd_reference(*[args[k] for k in FWD_PARAMS])


def _output_shape():
    def fwd():
        inp = _fwd_setup_inputs(0)
        return _fwd_reference(*[inp[k] for k in FWD_PARAMS])
    out = _jax.eval_shape(fwd)
    return out.shape, out.dtype

N_MICROBATCH = 1
ADAM_LR = 0.001
ADAM_B1 = 0.9
ADAM_B2 = 0.999
ADAM_EPS = 1e-08
ADAM_WD = 0.01
ADAM_STEP = 10
PER_EXAMPLE_BATCH_AXIS = {'x': 0, 'loss_target': 0}
SHARED_INPUTS = []
_WEIGHT_DTYPES = {'meta_tokens': _jnp.float32, 'mix_norm_g': _jnp.float32, 'mlp_norm_g': _jnp.float32, 'final_norm_g': _jnp.float32, 'ev_w_in': _jnp.float32, 'ev_conv_w': _jnp.float32, 'ev_conv_b': _jnp.float32, 'ev_ln_g': _jnp.float32, 'ev_ln_b': _jnp.float32, 'ev_pool_w': _jnp.float32, 'ev_pool_b': _jnp.float32, 'ev_pool_scale': _jnp.float32, 'ev_w_out': _jnp.float32, 'od_w_in': _jnp.float32, 'od_gnorm_g': _jnp.float32, 'od_w_out': _jnp.float32, 'lb_param': _jnp.float32, 'mlp_w1': _jnp.float32, 'mlp_w2': _jnp.float32}
MOMENT_SCALE = {'meta_tokens': 6.066529e-03, 'mix_norm_g': 9.718747e-02, 'mlp_norm_g': 1.287842e-01, 'final_norm_g': 3.282971e+01, 'ev_w_in': 8.952116e-02, 'ev_conv_w': 8.947767e-02, 'ev_conv_b': 2.369474e-01, 'ev_ln_g': 1.193342e-01, 'ev_ln_b': 1.291914e-01, 'ev_pool_w': 1.217871e-01, 'ev_pool_b': 2.498034e-01, 'ev_pool_scale': 1.305199e-01, 'ev_w_out': 1.091798e-01, 'od_w_in': 4.275749e-02, 'od_gnorm_g': 1.599079e-01, 'od_w_out': 6.071497e-02, 'lb_param': 2.587844e-03, 'mlp_w1': 6.412550e-02, 'mlp_w2': 1.356333e-01}


def _to_microbatches(a, axis):
    t = _jnp.moveaxis(a, axis, 0)
    t = t.reshape((N_MICROBATCH, t.shape[0] // N_MICROBATCH) + t.shape[1:])
    return _jnp.moveaxis(t, 1, axis + 1)


def setup_inputs(seed: int = 0) -> dict:
    inp = _fwd_setup_inputs(seed)
    key = _jax.random.fold_in(_jax.random.key(seed), 7919)
    shape, _ = _output_shape()
    out = dict(inp)
    out["loss_target"] = _jax.random.normal(_jax.random.fold_in(key, 0), shape, _jnp.float32)
    for i, name in enumerate(TWIN_WEIGHTS):
        w = inp[name].astype(_jnp.float32)
        if MOMENT_SCALE is None:
            s = _jnp.sqrt(_jnp.mean(_jnp.square(w)) + 1e-30)
        else:
            s = MOMENT_SCALE[name]
        km, kv = _jax.random.split(_jax.random.fold_in(key, i + 1))
        out[name] = w
        out["m_" + name] = s * _jax.random.normal(km, w.shape, _jnp.float32)
        out["v_" + name] = (s * s) * _jax.random.uniform(kv, w.shape, _jnp.float32, 0.5, 1.5)
    if N_MICROBATCH > 1:
        for name, axis in PER_EXAMPLE_BATCH_AXIS.items():
            out[name] = _to_microbatches(out[name], axis)
    return {'x': out['x'], 'meta_tokens': out['meta_tokens'], 'mix_norm_g': out['mix_norm_g'], 'mlp_norm_g': out['mlp_norm_g'], 'final_norm_g': out['final_norm_g'], 'ev_w_in': out['ev_w_in'], 'ev_conv_w': out['ev_conv_w'], 'ev_conv_b': out['ev_conv_b'], 'ev_ln_g': out['ev_ln_g'], 'ev_ln_b': out['ev_ln_b'], 'ev_pool_w': out['ev_pool_w'], 'ev_pool_b': out['ev_pool_b'], 'ev_pool_scale': out['ev_pool_scale'], 'ev_w_out': out['ev_w_out'], 'od_w_in': out['od_w_in'], 'od_gnorm_g': out['od_gnorm_g'], 'od_w_out': out['od_w_out'], 'lb_param': out['lb_param'], 'mlp_w1': out['mlp_w1'], 'mlp_w2': out['mlp_w2'], 'loss_target': out['loss_target'], 'm_meta_tokens': out['m_meta_tokens'], 'm_mix_norm_g': out['m_mix_norm_g'], 'm_mlp_norm_g': out['m_mlp_norm_g'], 'm_final_norm_g': out['m_final_norm_g'], 'm_ev_w_in': out['m_ev_w_in'], 'm_ev_conv_w': out['m_ev_conv_w'], 'm_ev_conv_b': out['m_ev_conv_b'], 'm_ev_ln_g': out['m_ev_ln_g'], 'm_ev_ln_b': out['m_ev_ln_b'], 'm_ev_pool_w': out['m_ev_pool_w'], 'm_ev_pool_b': out['m_ev_pool_b'], 'm_ev_pool_scale': out['m_ev_pool_scale'], 'm_ev_w_out': out['m_ev_w_out'], 'm_od_w_in': out['m_od_w_in'], 'm_od_gnorm_g': out['m_od_gnorm_g'], 'm_od_w_out': out['m_od_w_out'], 'm_lb_param': out['m_lb_param'], 'm_mlp_w1': out['m_mlp_w1'], 'm_mlp_w2': out['m_mlp_w2'], 'v_meta_tokens': out['v_meta_tokens'], 'v_mix_norm_g': out['v_mix_norm_g'], 'v_mlp_norm_g': out['v_mlp_norm_g'], 'v_final_norm_g': out['v_final_norm_g'], 'v_ev_w_in': out['v_ev_w_in'], 'v_ev_conv_w': out['v_ev_conv_w'], 'v_ev_conv_b': out['v_ev_conv_b'], 'v_ev_ln_g': out['v_ev_ln_g'], 'v_ev_ln_b': out['v_ev_ln_b'], 'v_ev_pool_w': out['v_ev_pool_w'], 'v_ev_pool_b': out['v_ev_pool_b'], 'v_ev_pool_scale': out['v_ev_pool_scale'], 'v_ev_w_out': out['v_ev_w_out'], 'v_od_w_in': out['v_od_w_in'], 'v_od_gnorm_g': out['v_od_gnorm_g'], 'v_od_w_out': out['v_od_w_out'], 'v_lb_param': out['v_lb_param'], 'v_mlp_w1': out['v_mlp_w1'], 'v_mlp_w2': out['v_mlp_w2']}


def _loss(weights, diff, rest, loss_target):
    with _jax.named_scope("forward"):
        args = {**rest, TWIN_DIFF_INPUT: diff, **{k: w.astype(_WEIGHT_DTYPES[k]) for k, w in weights.items()}}
        y = _forward(args)
    with _jax.named_scope("loss_head"):
        err = _jnp.square(y.astype(_jnp.float32) - loss_target)
        return 0.5 * _jnp.sum(_jnp.mean(err, axis=-1)) if err.ndim else 0.5 * err


def _adamw(w, g, m, v):
    m = ADAM_B1 * m + (1.0 - ADAM_B1) * g
    v = ADAM_B2 * v + (1.0 - ADAM_B2) * _jnp.square(g)
    m_hat = m / (1.0 - ADAM_B1 ** ADAM_STEP)
    v_hat = v / (1.0 - ADAM_B2 ** ADAM_STEP)
    delta = -ADAM_LR * (m_hat / (_jnp.sqrt(v_hat) + ADAM_EPS) + ADAM_WD * w)
    return delta, m, v


def reference(x, meta_tokens, mix_norm_g, mlp_norm_g, final_norm_g, ev_w_in, ev_conv_w, ev_conv_b, ev_ln_g, ev_ln_b, ev_pool_w, ev_pool_b, ev_pool_scale, ev_w_out, od_w_in, od_gnorm_g, od_w_out, lb_param, mlp_w1, mlp_w2, loss_target, m_meta_tokens, m_mix_norm_g, m_mlp_norm_g, m_final_norm_g, m_ev_w_in, m_ev_conv_w, m_ev_conv_b, m_ev_ln_g, m_ev_ln_b, m_ev_pool_w, m_ev_pool_b, m_ev_pool_scale, m_ev_w_out, m_od_w_in, m_od_gnorm_g, m_od_w_out, m_lb_param, m_mlp_w1, m_mlp_w2, v_meta_tokens, v_mix_norm_g, v_mlp_norm_g, v_final_norm_g, v_ev_w_in, v_ev_conv_w, v_ev_conv_b, v_ev_ln_g, v_ev_ln_b, v_ev_pool_w, v_ev_pool_b, v_ev_pool_scale, v_ev_w_out, v_od_w_in, v_od_gnorm_g, v_od_w_out, v_lb_param, v_mlp_w1, v_mlp_w2):
    given = dict(x=x, meta_tokens=meta_tokens, mix_norm_g=mix_norm_g, mlp_norm_g=mlp_norm_g, final_norm_g=final_norm_g, ev_w_in=ev_w_in, ev_conv_w=ev_conv_w, ev_conv_b=ev_conv_b, ev_ln_g=ev_ln_g, ev_ln_b=ev_ln_b, ev_pool_w=ev_pool_w, ev_pool_b=ev_pool_b, ev_pool_scale=ev_pool_scale, ev_w_out=ev_w_out, od_w_in=od_w_in, od_gnorm_g=od_gnorm_g, od_w_out=od_w_out, lb_param=lb_param, mlp_w1=mlp_w1, mlp_w2=mlp_w2, loss_target=loss_target, m_meta_tokens=m_meta_tokens, m_mix_norm_g=m_mix_norm_g, m_mlp_norm_g=m_mlp_norm_g, m_final_norm_g=m_final_norm_g, m_ev_w_in=m_ev_w_in, m_ev_conv_w=m_ev_conv_w, m_ev_conv_b=m_ev_conv_b, m_ev_ln_g=m_ev_ln_g, m_ev_ln_b=m_ev_ln_b, m_ev_pool_w=m_ev_pool_w, m_ev_pool_b=m_ev_pool_b, m_ev_pool_scale=m_ev_pool_scale, m_ev_w_out=m_ev_w_out, m_od_w_in=m_od_w_in, m_od_gnorm_g=m_od_gnorm_g, m_od_w_out=m_od_w_out, m_lb_param=m_lb_param, m_mlp_w1=m_mlp_w1, m_mlp_w2=m_mlp_w2, v_meta_tokens=v_meta_tokens, v_mix_norm_g=v_mix_norm_g, v_mlp_norm_g=v_mlp_norm_g, v_final_norm_g=v_final_norm_g, v_ev_w_in=v_ev_w_in, v_ev_conv_w=v_ev_conv_w, v_ev_conv_b=v_ev_conv_b, v_ev_ln_g=v_ev_ln_g, v_ev_ln_b=v_ev_ln_b, v_ev_pool_w=v_ev_pool_w, v_ev_pool_b=v_ev_pool_b, v_ev_pool_scale=v_ev_pool_scale, v_ev_w_out=v_ev_w_out, v_od_w_in=v_od_w_in, v_od_gnorm_g=v_od_gnorm_g, v_od_w_out=v_od_w_out, v_lb_param=v_lb_param, v_mlp_w1=v_mlp_w1, v_mlp_w2=v_mlp_w2)
    weights = {n: given[n] for n in TWIN_WEIGHTS}
    shared = {n: given[n] for n in SHARED_INPUTS}
    per_example = {n: given[n] for n in ['x']}
    grad_fn = _jax.value_and_grad(_loss, argnums=(0, 1))

    def one_microbatch(ex, loss_target):
        ex = dict(ex)
        diff = ex.pop(TWIN_DIFF_INPUT)
        return grad_fn(weights, diff, {**shared, **ex}, loss_target)

    if N_MICROBATCH == 1:
        loss, (grad_w, grad_x) = one_microbatch(per_example, given["loss_target"])
    else:
        def body(carry, xs):
            loss_sum, grad_sum = carry
            l_k, (gw_k, gx_k) = one_microbatch(xs[0], xs[1])
            with _jax.named_scope("update"):
                return (loss_sum + l_k, _jax.tree.map(_jnp.add, grad_sum, gw_k)), gx_k

        init = (_jnp.zeros((), _jnp.float32), _jax.tree.map(_jnp.zeros_like, weights))
        (loss, grad_w), grad_x = _jax.lax.scan(body, init, (per_example, given["loss_target"]))
    with _jax.named_scope("update"):
        delta_w, new_m, new_v = {}, {}, {}
        for n in TWIN_WEIGHTS:
            delta_w[n], new_m[n], new_v[n] = _adamw(weights[n], grad_w[n], given["m_" + n], given["v_" + n])
    return (loss, grad_x, *[grad_w[n] for n in TWIN_WEIGHTS], *[delta_w[n] for n in TWIN_WEIGHTS],
            *[new_m[n] for n in TWIN_WEIGHTS], *[new_v[n] for n in TWIN_WEIGHTS])
```

```python
import functools

import jax
import jax.numpy as jnp
from jax import lax
from jax.experimental import pallas as pl
from jax.experimental.pallas import tpu as pltpu

F32 = jnp.float32
BF16 = jnp.bfloat16

N_DEV = 8
N_META = 16
CHUNK = 64
PAD = CHUNK - N_META
SUB = 16
HEAD = 128
CONV_WIDTH = 31
HALO = 32
POOL_WINDOWS = (2, 4, 8, 16)
EPS = 1e-6
NEG = -1e30
ADAM_LR, ADAM_B1, ADAM_B2, ADAM_EPS, ADAM_WD, ADAM_STEP = 0.001, 0.9, 0.999, 1e-08, 0.01, 10
VMEM_LIMIT = 56 * 1024 * 1024
EV_ROWS = 416
HGRN_ROWS = 832
MM_ROWS_BIG = 2080
MM_ROWS_MID = 1040
MESH = pl.DeviceIdType.MESH
AXES = ("x", "y", "c")
ANY = pl.BlockSpec(memory_space=pl.ANY)


def _cp(*sem):
    return pltpu.CompilerParams(dimension_semantics=sem, vmem_limit_bytes=VMEM_LIMIT)


def _tile(n, cap, mult):
    best = None
    for d in range(mult, min(n, cap) + 1, mult):
        if n % d == 0:
            best = d
    assert best is not None, (n, cap, mult)
    return best


def _nt(a, b):
    return lax.dot_general(a, b, (((1,), (1,)), ((), ())), preferred_element_type=F32)


def _tn(a, b):
    return lax.dot_general(a, b, (((0,), (0,)), ((), ())), preferred_element_type=F32)


def _nn(a, b):
    return jnp.dot(a, b, preferred_element_type=F32)


def _row_ids(base, n):
    return base + lax.broadcasted_iota(jnp.int32, (n, 1), 0)


def _dsilu(x, s):
    return s * (1.0 + x * (1.0 - s))


def _rms_fwd(h, g):
    T, D = h.shape
    tm = _tile(T, MM_ROWS_MID, 16)

    def body(h_ref, g_ref, n_ref):
        x = h_ref[...]
        r = lax.rsqrt(jnp.mean(x * x, axis=-1, keepdims=True) + EPS)
        n_ref[...] = ((x * r) * g_ref[...]).astype(BF16)

    return pl.pallas_call(
        body, grid=(T // tm,),
        in_specs=[pl.BlockSpec((tm, D), lambda i: (i, 0)), pl.BlockSpec((1, D), lambda i: (0, 0))],
        out_specs=pl.BlockSpec((tm, D), lambda i: (i, 0)),
        out_shape=jax.ShapeDtypeStruct((T, D), BF16),
        compiler_params=_cp("parallel"), name="rms_fwd")(h, g)


def _rms_bwd(dn, h, g, dres):
    T, D = h.shape
    tm = _tile(T, MM_ROWS_MID, 16)

    def body(dn_ref, h_ref, g_ref, dres_ref, dh_ref, dhb_ref, dg_ref):
        i = pl.program_id(0)
        x = h_ref[...]
        dn_v = dn_ref[...]
        r = lax.rsqrt(jnp.mean(x * x, axis=-1, keepdims=True) + EPS)
        xh = x * r
        dxh = dn_v * g_ref[...]
        dx = r * (dxh - xh * jnp.mean(dxh * xh, axis=-1, keepdims=True))
        keep = _row_ids(i * tm, tm) >= PAD
        dh = jnp.where(keep, dres_ref[...] + dx, 0.0)
        dh_ref[...] = dh
        dhb_ref[...] = dh.astype(BF16)

        @pl.when(i == 0)
        def _():
            dg_ref[...] = jnp.zeros_like(dg_ref)

        dg_ref[...] += jnp.sum(dn_v * xh, axis=0, keepdims=True)

    row = pl.BlockSpec((tm, D), lambda i: (i, 0))
    vec = pl.BlockSpec((1, D), lambda i: (0, 0))
    return pl.pallas_call(
        body, grid=(T // tm,),
        in_specs=[row, row, vec, row], out_specs=[row, row, vec],
        out_shape=[jax.ShapeDtypeStruct((T, D), F32), jax.ShapeDtypeStruct((T, D), BF16),
                   jax.ShapeDtypeStruct((1, D), F32)],
        compiler_params=_cp("arbitrary"), name="rms_bwd")(dn, h, g, dres)


def _loss_head(h, g, tgt):
    T, D = h.shape
    tm = _tile(T, MM_ROWS_MID, 16)
    first_x = PAD + N_META

    def body(h_ref, g_ref, t_ref, loss_ref, dh_ref, dhb_ref, dg_ref):
        i = pl.program_id(0)
        x = h_ref[...]
        r = lax.rsqrt(jnp.mean(x * x, axis=-1, keepdims=True) + EPS)
        xh = x * r
        gv = g_ref[...]
        out = xh * gv
        valid = _row_ids(i * tm, tm) >= first_x
        e = jnp.where(valid, out - t_ref[...], 0.0)
        dout = e * (1.0 / D)
        dxh = dout * gv
        dx = r * (dxh - xh * jnp.mean(dxh * xh, axis=-1, keepdims=True))
        dh_ref[...] = dx
        dhb_ref[...] = dx.astype(BF16)

        @pl.when(i == 0)
        def _():
            dg_ref[...] = jnp.zeros_like(dg_ref)
            loss_ref[...] = jnp.zeros_like(loss_ref)

        dg_ref[...] += jnp.sum(dout * xh, axis=0, keepdims=True)
        loss_ref[...] += 0.5 * jnp.sum(jnp.mean(e * e, axis=-1, keepdims=True))

    row = pl.BlockSpec((tm, D), lambda i: (i, 0))
    vec = pl.BlockSpec((1, D), lambda i: (0, 0))
    return pl.pallas_call(
        body, grid=(T // tm,),
        in_specs=[row, vec, row],
        out_specs=[pl.BlockSpec((8, 128), lambda i: (0, 0)), row, row, vec],
        out_shape=[jax.ShapeDtypeStruct((8, 128), F32), jax.ShapeDtypeStruct((T, D), F32),
                   jax.ShapeDtypeStruct((T, D), BF16), jax.ShapeDtypeStruct((1, D), F32)],
        compiler_params=_cp("arbitrary"), name="loss_head")(h, g, tgt)


def _mm_nn(name, a, w, w_spec, M, N, K, tm, tn, tk, mode, extra=None, a_spec=None):
    nk = K // tk
    if a_spec is None:
        a_spec = pl.BlockSpec((tm, tk), lambda i, j, k: (i, k))
    o_spec = pl.BlockSpec((tm, tn), lambda i, j, k: (i, j))

    def body(*refs):
        if mode == "resid":
            a_ref, w_ref, e_ref = refs[:3]
            outs = refs[3:]
        else:
            a_ref, w_ref = refs[:2]
            outs = refs[2:]
        acc_ref = outs[-1] if nk > 1 else None
        part = _nn(a_ref[...], w_ref[...])

        def finish(acc):
            if mode == "f32":
                outs[0][...] = acc
            elif mode == "relu2":
                r = jnp.maximum(acc, 0.0)
                outs[0][...] = r.astype(BF16)
                outs[1][...] = (r * r).astype(BF16)
            else:
                keep = _row_ids(pl.program_id(0) * tm, tm) >= PAD
                outs[0][...] = jnp.where(keep, e_ref[...] + acc, 0.0)

        if nk == 1:
            finish(part)
        else:
            k = pl.program_id(2)

            @pl.when(k == 0)
            def _():
                acc_ref[...] = part

            @pl.when(k > 0)
            def _():
                acc_ref[...] += part

            @pl.when(k == nk - 1)
            def _():
                finish(acc_ref[...])

    in_specs = [a_spec, w_spec(tk, tn)]
    args = [a, w]
    if mode == "resid":
        in_specs.append(o_spec)
        args.append(extra)
    if mode == "relu2":
        out_specs = [o_spec, o_spec]
        out_shape = [jax.ShapeDtypeStruct((M, N), BF16)] * 2
    else:
        out_specs = [o_spec]
        out_shape = [jax.ShapeDtypeStruct((M, N), F32)]
    scratch = [pltpu.VMEM((tm, tn), F32)] if nk > 1 else []
    res = pl.pallas_call(
        body, grid=(M // tm, N // tn, nk), in_specs=in_specs, out_specs=out_specs, out_shape=out_shape,
        scratch_shapes=scratch, compiler_params=_cp("parallel", "parallel", "arbitrary"), name=name)(*args)
    return res if mode == "relu2" else res[0]


def _mm_nt(name, dy, w, dy_spec, w_spec, M, J, N, tm, tj, tn, mode, extra=None):
    nk = N // tn
    o_spec = pl.BlockSpec((tm, tj), lambda i, j, k: (i, j))

    def body(*refs):
        if mode == "dact":
            dy_ref, w_ref, e_ref = refs[:3]
            outs = refs[3:]
        else:
            dy_ref, w_ref = refs[:2]
            outs = refs[2:]
        acc_ref = outs[-1] if nk > 1 else None
        part = _nt(dy_ref[...], w_ref[...])

        def finish(acc):
            if mode == "f32":
                outs[0][...] = acc
            else:
                outs[0][...] = (acc * (2.0 * e_ref[...].astype(F32))).astype(BF16)

        if nk == 1:
            finish(part)
        else:
            k = pl.program_id(2)

            @pl.when(k == 0)
            def _():
                acc_ref[...] = part

            @pl.when(k > 0)
            def _():
                acc_ref[...] += part

            @pl.when(k == nk - 1)
            def _():
                finish(acc_ref[...])

    in_specs = [dy_spec(tm, tn), w_spec(tj, tn)]
    args = [dy, w]
    if mode == "dact":
        in_specs.append(o_spec)
        args.append(extra)
    scratch = [pltpu.VMEM((tm, tj), F32)] if nk > 1 else []
    return pl.pallas_call(
        body, grid=(M // tm, J // tj, nk), in_specs=in_specs, out_specs=[o_spec],
        out_shape=[jax.ShapeDtypeStruct((M, J), BF16 if mode == "dact" else F32)],
        scratch_shapes=scratch, compiler_params=_cp("parallel", "parallel", "arbitrary"), name=name)(*args)[0]


def _mm_tn(name, x, dy, x_spec, dy_spec, o_spec, o_shape, T, K, N, tt, tk, tn):
    nt = T // tt

    def body(x_ref, dy_ref, o_ref, acc_ref):
        t = pl.program_id(2)
        part = _tn(x_ref[...], dy_ref[...])

        @pl.when(t == 0)
        def _():
            acc_ref[...] = part

        @pl.when(t > 0)
        def _():
            acc_ref[...] += part

        @pl.when(t == nt - 1)
        def _():
            o_ref[...] = acc_ref[...].astype(BF16)

    return pl.pallas_call(
        body, grid=(K // tk, N // tn, nt), in_specs=[x_spec(tt, tk), dy_spec(tt, tn)], out_specs=o_spec(tk, tn),
        out_shape=jax.ShapeDtypeStruct(o_shape, BF16), scratch_shapes=[pltpu.VMEM((tk, tn), F32)],
        compiler_params=_cp("parallel", "parallel", "arbitrary"), name=name)(x, dy)


def _pool_counts(base, n, w):
    pos = _row_ids(base, n) - PAD
    return jnp.clip(pos + 1, 1, w).astype(F32)


def _ev_fwd(u, cw, cb, lg, lb, pw, pb, ps):
    T = u.shape[0]
    C = 512
    tm = _tile(T, EV_ROWS, HALO)
    nsub = tm // HALO
    hb = tm // HALO

    def body(val_ref, gate_ref, pin_ref, valh_ref, gateh_ref, pinh_ref, cw_ref, cb_ref, lg_ref, lb_ref, pw_ref,
             pb_ref, ps_ref, yab_ref, yc_ref, a_ext, p_ext, d_buf):
        i = pl.program_id(0)
        nf = (i > 0).astype(F32)
        a_ext[0:HALO, :] = valh_ref[...] * jax.nn.sigmoid(gateh_ref[...]) * nf
        a_ext[HALO:, :] = val_ref[...] * jax.nn.sigmoid(gate_ref[...])
        p_ext[0:HALO, :] = pinh_ref[...] * nf
        p_ext[HALO:, :] = pin_ref[...]

        def sub(s, carry):
            base = pl.multiple_of(s * HALO, HALO)
            win = a_ext[pl.ds(base, 2 * HALO), :]
            acc = jnp.zeros((HALO, C), F32) + cb_ref[...]
            for j in range(CONV_WIDTH):
                acc = acc + cw_ref[pl.ds(j, 1), :] * win[2 + j:2 + j + HALO]
            yc_ref[pl.ds(base, HALO), :] = acc
            mu = jnp.mean(acc, axis=-1, keepdims=True)
            yc = acc - mu
            rstd = lax.rsqrt(jnp.mean(yc * yc, axis=-1, keepdims=True) + EPS)
            z = (yc * rstd) * lg_ref[...] + lb_ref[...]
            yab_ref[pl.ds(base, HALO), 0:C] = (z * jax.nn.sigmoid(z)).astype(BF16)
            pwin = p_ext[pl.ds(base, 2 * HALO), :]
            for gi, w in enumerate(POOL_WINDOWS):
                lo, hi = gi * HEAD, (gi + 1) * HEAD
                x = pwin[HALO:, lo:hi]
                tot = x
                for k in range(1, w):
                    tot = tot + pwin[HALO - k:2 * HALO - k, lo:hi]
                cnt = _pool_counts(i * tm + base, HALO, w)
                d_buf[pl.ds(base, HALO), lo:hi] = (tot / cnt - x).astype(BF16)
            return carry

        lax.fori_loop(0, nsub, sub, 0)
        for gi in range(len(POOL_WINDOWS)):
            lo, hi = gi * HEAD, (gi + 1) * HEAD
            y = _nn(d_buf[:, lo:hi], pw_ref[gi]) + pb_ref[:, lo:hi]
            yab_ref[:, C + lo:C + hi] = (y * ps_ref[:, lo:hi]).astype(BF16)

    def main(c):
        return pl.BlockSpec((tm, C), lambda i: (i, c))

    def halo(c):
        return pl.BlockSpec((HALO, C), lambda i: (jnp.maximum(i * hb - 1, 0), c))

    vec = pl.BlockSpec((1, C), lambda i: (0, 0))
    return pl.pallas_call(
        body, grid=(T // tm,),
        in_specs=[main(0), main(1), main(2), halo(0), halo(1), halo(2),
                  pl.BlockSpec((32, C), lambda i: (0, 0)), vec, vec, vec,
                  pl.BlockSpec((4, HEAD, HEAD), lambda i: (0, 0, 0)), vec, vec],
        out_specs=[pl.BlockSpec((tm, 2 * C), lambda i: (i, 0)), pl.BlockSpec((tm, C), lambda i: (i, 0))],
        out_shape=[jax.ShapeDtypeStruct((T, 2 * C), BF16), jax.ShapeDtypeStruct((T, C), F32)],
        scratch_shapes=[pltpu.VMEM((tm + HALO, C), F32), pltpu.VMEM((tm + HALO, C), F32), pltpu.VMEM((tm, C), BF16)],
        compiler_params=_cp("parallel"), name="ev_fwd")(u, u, u, u, u, u, cw, cb, lg, lb, pw, pb, ps)


def _ev_bwd(dyab, yc, u, cw, lg, lb, pw, pwt, pb, ps):
    T = u.shape[0]
    C = 512
    tm = _tile(T, EV_ROWS, HALO)
    nsub = tm // HALO
    hb = tm // HALO
    nblk = T // tm
    E = tm + HALO

    def body(dya_ref, dyb_ref, dyah_ref, dybh_ref, yc_ref, ych_ref, val_ref, gate_ref, pin_ref, valh_ref, gateh_ref,
             pinh_ref, cw_ref, lg_ref, lb_ref, pw_ref, pwt_ref, pb_ref, ps_ref,
             du_ref, dcw_ref, dvec_ref, dpw_ref,
             dy_ext, a_ext, p_ext, ddc_ext, dd_buf, d_buf, dpre_buf, dcw_acc, vec_acc):
        i = pl.program_id(0)
        nf = (i > 0).astype(F32)
        nl = (i < nblk - 1).astype(F32)

        @pl.when(i == 0)
        def _():
            dcw_ref[...] = jnp.zeros_like(dcw_ref)
            dvec_ref[...] = jnp.zeros_like(dvec_ref)
            dpw_ref[...] = jnp.zeros_like(dpw_ref)

        dcw_acc[...] = jnp.zeros_like(dcw_acc)
        vec_acc[...] = jnp.zeros_like(vec_acc)
        a_ext[0:HALO, :] = valh_ref[...] * jax.nn.sigmoid(gateh_ref[...]) * nf
        a_ext[HALO:, :] = val_ref[...] * jax.nn.sigmoid(gate_ref[...])
        p_ext[0:HALO, :] = pinh_ref[...] * nf
        p_ext[HALO:, :] = pin_ref[...]

        def ln_bwd(y, dya, main):
            mu = jnp.mean(y, axis=-1, keepdims=True)
            ycen = y - mu
            rstd = lax.rsqrt(jnp.mean(ycen * ycen, axis=-1, keepdims=True) + EPS)
            yh = ycen * rstd
            z = yh * lg_ref[...] + lb_ref[...]
            sz = jax.nn.sigmoid(z)
            dz = dya * _dsilu(z, sz)
            dyh = dz * lg_ref[...]
            dy = rstd * (dyh - jnp.mean(dyh, axis=-1, keepdims=True) - yh * jnp.mean(dyh * yh, axis=-1, keepdims=True))
            if main:
                vec_acc[1] += jnp.sum((dz * yh).reshape(HALO // 8, 8, C), axis=0)
                vec_acc[2] += jnp.sum(dz.reshape(HALO // 8, 8, C), axis=0)
                vec_acc[0] += jnp.sum(dy.reshape(HALO // 8, 8, C), axis=0)
            return dy

        def pool_dd(dyb, base, main):
            dpre = dyb * ps_ref[...]
            for gi, w in enumerate(POOL_WINDOWS):
                lo, hi = gi * HEAD, (gi + 1) * HEAD
                dd = _nn(dpre[:, lo:hi].astype(BF16), pwt_ref[gi])
                cnt = _pool_counts(i * tm + base, HALO, w)
                ddc_ext[pl.ds(base, HALO), lo:hi] = dd / cnt
                if main:
                    dd_buf[pl.ds(base, HALO), lo:hi] = dd
            if main:
                dpre_buf[pl.ds(base, HALO), :] = dpre.astype(BF16)
                vec_acc[4] += jnp.sum(dpre.reshape(HALO // 8, 8, C), axis=0)

        def p1(s, carry):
            base = pl.multiple_of(s * HALO, HALO)
            dy_ext[pl.ds(base, HALO), :] = ln_bwd(yc_ref[pl.ds(base, HALO), :], dya_ref[pl.ds(base, HALO), :], True)
            pool_dd(dyb_ref[pl.ds(base, HALO), :], base, True)
            return carry

        lax.fori_loop(0, nsub, p1, 0)
        dy_ext[tm:, :] = ln_bwd(ych_ref[...], dyah_ref[...], False) * nl
        dpre_h = dybh_ref[...] * ps_ref[...] * nl
        for gi, w in enumerate(POOL_WINDOWS):
            lo, hi = gi * HEAD, (gi + 1) * HEAD
            dd = _nn(dpre_h[:, lo:hi].astype(BF16), pwt_ref[gi])
            ddc_ext[tm:, lo:hi] = dd / _pool_counts(i * tm + tm, HALO, w)

        def p2(s, carry):
            base = pl.multiple_of(s * HALO, HALO)
            dwin = dy_ext[pl.ds(base, 2 * HALO), :]
            awin = a_ext[pl.ds(base, 2 * HALO), :]
            dy_m = dwin[0:HALO]
            da = jnp.zeros((HALO, C), F32)
            for j in range(CONV_WIDTH):
                sh = CONV_WIDTH - 1 - j
                da = da + cw_ref[pl.ds(j, 1), :] * dwin[sh:sh + HALO]
                dcw_acc[j] += jnp.sum((dy_m * awin[2 + j:2 + j + HALO]).reshape(HALO // 8, 8, C), axis=0)
            v = val_ref[pl.ds(base, HALO), :]
            g = gate_ref[pl.ds(base, HALO), :]
            sg = jax.nn.sigmoid(g)
            du_ref[pl.ds(base, HALO), 0:C] = (da * sg).astype(BF16)
            du_ref[pl.ds(base, HALO), C:2 * C] = (da * v * sg * (1.0 - sg)).astype(BF16)
            pwin = p_ext[pl.ds(base, 2 * HALO), :]
            cwin = ddc_ext[pl.ds(base, 2 * HALO), :]
            for gi, w in enumerate(POOL_WINDOWS):
                lo, hi = gi * HEAD, (gi + 1) * HEAD
                x = pwin[HALO:, lo:hi]
                tot = x
                back = cwin[0:HALO, lo:hi]
                for k in range(1, w):
                    tot = tot + pwin[HALO - k:2 * HALO - k, lo:hi]
                    back = back + cwin[k:k + HALO, lo:hi]
                cnt = _pool_counts(i * tm + base, HALO, w)
                d_buf[pl.ds(base, HALO), lo:hi] = (tot / cnt - x).astype(BF16)
                du_ref[pl.ds(base, HALO), 2 * C + lo:2 * C + hi] = (back - dd_buf[pl.ds(base, HALO), lo:hi]).astype(BF16)
            return carry

        lax.fori_loop(0, nsub, p2, 0)
        for gi in range(len(POOL_WINDOWS)):
            lo, hi = gi * HEAD, (gi + 1) * HEAD
            pre = _nn(d_buf[:, lo:hi], pw_ref[gi]) + pb_ref[:, lo:hi]
            vec_acc[3, :, lo:hi] += jnp.sum((dyb_ref[:, lo:hi] * pre).reshape(tm // 8, 8, HEAD), axis=0)
            dpw_ref[gi] += _tn(d_buf[:, lo:hi], dpre_buf[:, lo:hi])
        for j in range(CONV_WIDTH):
            dcw_ref[pl.ds(j, 1), :] += jnp.sum(dcw_acc[j], axis=0, keepdims=True)
        for r in range(5):
            dvec_ref[pl.ds(r, 1), :] += jnp.sum(vec_acc[r], axis=0, keepdims=True)

    def main(c, width=C):
        return pl.BlockSpec((tm, width), lambda i: (i, c))

    def prev(c):
        return pl.BlockSpec((HALO, C), lambda i: (jnp.maximum(i * hb - 1, 0), c))

    def nxt(c):
        return pl.BlockSpec((HALO, C), lambda i: (jnp.minimum((i + 1) * hb, T // HALO - 1), c))

    vec = pl.BlockSpec((1, C), lambda i: (0, 0))
    mat = pl.BlockSpec((4, HEAD, HEAD), lambda i: (0, 0, 0))
    return pl.pallas_call(
        body, grid=(nblk,),
        in_specs=[main(0), main(1), nxt(0), nxt(1), main(0), nxt(0), main(0), main(1), main(2), prev(0), prev(1),
                  prev(2), pl.BlockSpec((32, C), lambda i: (0, 0)), vec, vec, mat, mat, vec, vec],
        out_specs=[pl.BlockSpec((tm, 3 * C), lambda i: (i, 0)), pl.BlockSpec((32, C), lambda i: (0, 0)),
                   pl.BlockSpec((8, C), lambda i: (0, 0)), mat],
        out_shape=[jax.ShapeDtypeStruct((T, 3 * C), BF16), jax.ShapeDtypeStruct((32, C), F32),
                   jax.ShapeDtypeStruct((8, C), F32), jax.ShapeDtypeStruct((4, HEAD, HEAD), F32)],
        scratch_shapes=[pltpu.VMEM((E, C), F32), pltpu.VMEM((E, C), F32), pltpu.VMEM((E, C), F32),
                        pltpu.VMEM((E, C), F32), pltpu.VMEM((tm, C), F32), pltpu.VMEM((tm, C), BF16),
                        pltpu.VMEM((tm, C), BF16), pltpu.VMEM((32, 8, C), F32), pltpu.VMEM((8, 8, C), F32)],
        compiler_params=_cp("arbitrary"), name="ev_bwd")(
            dyab, dyab, dyab, dyab, yc, yc, u, u, u, u, u, u, cw, lg, lb, pw, pwt, pb, ps)


def _tri(n, upper):
    r = lax.broadcasted_iota(jnp.int32, (n, n), 0)
    c = lax.broadcasted_iota(jnp.int32, (n, n), 1)
    return ((c >= r) if upper else (c <= r)).astype(F32)


def _hgrn_gates(qr, fr, lbv):
    sq = jax.nn.sigmoid(qr)
    sg = jax.nn.sigmoid(fr)
    fg = lbv + (1.0 - lbv) * sg
    return qr * sq, sq, sg, fg, 1.0 - fg, jnp.log(fg)


def _hgrn_fwd(u, lbv, gn):
    T = u.shape[0]
    H = 8
    RB = _tile(T, HGRN_ROWS, CHUNK)
    NC = RB // CHUNK
    NS = CHUNK // SUB

    def body(q_ref, f_ref, i_ref, g_ref, lb_ref, gn_ref, y_ref, o_ref, s0_ref, st, qs, ks, bs, vs, os_):
        rb = pl.program_id(1)

        @pl.when(rb == 0)
        def _():
            st[...] = jnp.zeros_like(st)

        tri = _tri(CHUNK, False)
        tsub = lax.broadcasted_iota(jnp.int32, (SUB, 1), 0)

        def chunk(c, carry):
            r0 = pl.multiple_of(c * CHUNK, CHUNK)
            rows = pl.ds(r0, CHUNK)
            q, _, _, _, kk, lf = _hgrn_gates(q_ref[rows, :], f_ref[rows, :], lb_ref[...])
            v = i_ref[rows, :]
            b = jnp.dot(tri, lf, preferred_element_type=F32, precision=lax.Precision.HIGHEST)
            qs[...] = q
            ks[...] = kk
            bs[...] = b
            vs[...] = v
            st0 = st[...]
            s0_ref[0, c] = st0
            os_[...] = _nt((q * jnp.exp(b)).astype(BF16), st0.astype(BF16))
            for I in range(NS):
                lo = I * SUB
                qI = qs[lo:lo + SUB, :]
                bI = bs[lo:lo + SUB, :]
                oI = jnp.zeros((SUB, HEAD), F32)
                if I > 0:
                    bprev = bs[pl.ds(lo - 1, 1), :]
                    qt = (qI * jnp.exp(bI - bprev)).astype(BF16)
                    kt = (ks[0:lo, :] * jnp.exp(bprev - bs[0:lo, :])).astype(BF16)
                    A = _nt(qt, kt)
                    oI = oI + _nn(A.astype(BF16), vs[0:lo, :].astype(BF16))
                for s in range(SUB):
                    row = pl.ds(lo + s, 1)
                    Es = jnp.exp(jnp.where(tsub >= s, bI - bs[row, :], NEG))
                    col = jnp.sum(qI * Es * ks[row, :], axis=1, keepdims=True)
                    oI = oI + col * vs[row, :]
                os_[lo:lo + SUB, :] += oI
            blast = bs[pl.ds(CHUNK - 1, 1), :]
            kh = kk * jnp.exp(blast - b)
            st[...] = st0 * jnp.exp(blast) + _tn(v.astype(BF16), kh.astype(BF16))
            o = os_[...]
            o_ref[rows, :] = o
            rr = lax.rsqrt(jnp.mean(o * o, axis=-1, keepdims=True) + EPS)
            gr = g_ref[rows, :]
            y_ref[rows, :] = (((o * rr) * gn_ref[...]) * (gr * jax.nn.sigmoid(gr))).astype(BF16)
            return carry

        lax.fori_loop(0, NC, chunk, 0)

    def blk(q):
        return pl.BlockSpec((RB, HEAD), lambda h, r: (r, q * H + h))

    sc = lambda: pltpu.VMEM((CHUNK, HEAD), F32)
    return pl.pallas_call(
        body, grid=(H, T // RB),
        in_specs=[blk(0), blk(1), blk(2), blk(3), pl.BlockSpec((1, HEAD), lambda h, r: (0, h)),
                  pl.BlockSpec((1, HEAD), lambda h, r: (0, 0))],
        out_specs=[pl.BlockSpec((RB, HEAD), lambda h, r: (r, h)), pl.BlockSpec((RB, HEAD), lambda h, r: (r, h)),
                   pl.BlockSpec((1, NC, HEAD, HEAD), lambda h, r: (h, r, 0, 0))],
        out_shape=[jax.ShapeDtypeStruct((T, H * HEAD), BF16), jax.ShapeDtypeStruct((T, H * HEAD), F32),
                   jax.ShapeDtypeStruct((H, T // CHUNK, HEAD, HEAD), F32)],
        scratch_shapes=[pltpu.VMEM((HEAD, HEAD), F32), sc(), sc(), sc(), sc(), sc()],
        compiler_params=_cp("parallel", "arbitrary"), name="hgrn_fwd")(u, u, u, u, lbv, gn)


def _hgrn_bwd(dy, o, s0, u, lbv, gn):
    T = u.shape[0]
    H = 8
    RB = _tile(T, HGRN_ROWS, CHUNK)
    NB = T // RB
    NC = RB // CHUNK
    NS = CHUNK // SUB

    def body(q_ref, f_ref, i_ref, g_ref, lb_ref, gn_ref, o_ref, dy_ref, s0_ref, du_ref, dlb_ref, dgn_ref,
             dst, qs, ks, bs, vs, dos, dqs, dks, dki, dvs, dbs):
        rb = pl.program_id(1)

        @pl.when(rb == 0)
        def _():
            dst[...] = jnp.zeros_like(dst)
            dlb_ref[...] = jnp.zeros_like(dlb_ref)
            dgn_ref[...] = jnp.zeros_like(dgn_ref)

        tril = _tri(CHUNK, False)
        triu = _tri(CHUNK, True)
        tsub = lax.broadcasted_iota(jnp.int32, (SUB, 1), 0)
        lbv_ = lb_ref[...]
        gnv = gn_ref[...]

        def chunk(cc, carry):
            c = NC - 1 - cc
            r0 = pl.multiple_of(c * CHUNK, CHUNK)
            rows = pl.ds(r0, CHUNK)
            qr = q_ref[rows, :]
            q, sq, sg, fg, kk, lf = _hgrn_gates(qr, f_ref[rows, :], lbv_)
            v = i_ref[rows, :]
            gr = g_ref[rows, :]
            b = jnp.dot(tril, lf, preferred_element_type=F32, precision=lax.Precision.HIGHEST)
            eb = jnp.exp(b)
            ov = o_ref[rows, :]
            dyv = dy_ref[rows, :]
            rr = lax.rsqrt(jnp.mean(ov * ov, axis=-1, keepdims=True) + EPS)
            oh = ov * rr
            gs = jax.nn.sigmoid(gr)
            dgr = dyv * (oh * gnv) * _dsilu(gr, gs)
            dnrm = dyv * (gr * gs)
            dgn_ref[0] += jnp.sum(dnrm * oh, axis=0, keepdims=True)
            t1 = dnrm * gnv
            do = rr * (t1 - oh * jnp.mean(t1 * oh, axis=-1, keepdims=True))
            qs[...] = q
            ks[...] = kk
            bs[...] = b
            vs[...] = v
            dos[...] = do
            st0 = s0_ref[0, c]
            dS = dst[...]
            do_b = do.astype(BF16)
            blast = bs[pl.ds(CHUNK - 1, 1), :]
            elast = jnp.exp(blast - b)
            dq_inter = _nn(do_b, st0.astype(BF16)) * eb
            dqs[...] = dq_inter
            dbs[...] = q * dq_inter
            kh = kk * elast
            dvs[...] = _nt(kh.astype(BF16), dS.astype(BF16))
            dk_inter = _nn(v.astype(BF16), dS.astype(BF16)) * elast
            dki[...] = dk_inter
            dks[...] = jnp.zeros_like(dks)
            for I in range(NS):
                lo = I * SUB
                qI = qs[lo:lo + SUB, :]
                bI = bs[lo:lo + SUB, :]
                doI = dos[lo:lo + SUB, :]
                dqI = jnp.zeros((SUB, HEAD), F32)
                dbI = jnp.zeros((SUB, HEAD), F32)
                if I > 0:
                    bprev = bs[pl.ds(lo - 1, 1), :]
                    eq = jnp.exp(bI - bprev)
                    ek = jnp.exp(bprev - bs[0:lo, :])
                    qt = (qI * eq).astype(BF16)
                    kt = (ks[0:lo, :] * ek).astype(BF16)
                    A = _nt(qt, kt).astype(BF16)
                    doI_b = doI.astype(BF16)
                    dA = _nt(doI_b, vs[0:lo, :].astype(BF16)).astype(BF16)
                    dvs[0:lo, :] += _tn(A, doI_b)
                    dqt = _nn(dA, kt)
                    dkt = _tn(dA, qt)
                    dqI = dqI + dqt * eq
                    dbI = dbI + qt.astype(F32) * dqt
                    dks[0:lo, :] += dkt * ek
                    dbs[0:lo, :] -= kt.astype(F32) * dkt
                for s in range(SUB):
                    row = pl.ds(lo + s, 1)
                    krow = ks[row, :]
                    Es = jnp.exp(jnp.where(tsub >= s, bI - bs[row, :], NEG))
                    qE = qI * Es
                    col = jnp.sum(qE * krow, axis=1, keepdims=True)
                    dcol = jnp.sum(doI * vs[row, :], axis=1, keepdims=True)
                    dvs[row, :] += jnp.sum(col * doI, axis=0, keepdims=True)
                    dq_s = dcol * Es * krow
                    dqI = dqI + dq_s
                    dbI = dbI + qI * dq_s
                    dk_s = jnp.sum(dcol * qE, axis=0, keepdims=True)
                    dks[row, :] += dk_s
                    dbs[row, :] -= krow * dk_s
                dqs[lo:lo + SUB, :] += dqI
                dbs[lo:lo + SUB, :] += dbI
            kdk = kk * dki[...]
            excl = jnp.dot(tril, kdk, preferred_element_type=F32, precision=lax.Precision.HIGHEST) - kdk
            suff = jnp.dot(triu, dbs[...], preferred_element_type=F32, precision=lax.Precision.HIGHEST)
            gdec = jnp.sum(dS * st0, axis=0, keepdims=True) * jnp.exp(blast)
            dlf = suff + excl + gdec
            dk = dks[...] + dki[...]
            dfg = dlf / fg - dk
            dlb_ref[...] += jnp.sum(dfg * (1.0 - sg), axis=0, keepdims=True)
            du_ref[0, rows, :] = (dqs[...] * _dsilu(qr, sq)).astype(BF16)
            du_ref[1, rows, :] = (dfg * (1.0 - lbv_) * sg * (1.0 - sg)).astype(BF16)
            du_ref[2, rows, :] = dvs[...].astype(BF16)
            du_ref[3, rows, :] = dgr.astype(BF16)
            dst[...] = dS * jnp.exp(blast) + _tn(do_b, (q * eb).astype(BF16))
            return carry

        lax.fori_loop(0, NC, chunk, 0)

    def blk(qd):
        return pl.BlockSpec((RB, HEAD), lambda h, r: (NB - 1 - r, qd * H + h))

    hblk = pl.BlockSpec((RB, HEAD), lambda h, r: (NB - 1 - r, h))
    sc = lambda: pltpu.VMEM((CHUNK, HEAD), F32)
    return pl.pallas_call(
        body, grid=(H, NB),
        in_specs=[blk(0), blk(1), blk(2), blk(3), pl.BlockSpec((1, HEAD), lambda h, r: (0, h)),
                  pl.BlockSpec((1, HEAD), lambda h, r: (0, 0)), hblk, hblk,
                  pl.BlockSpec((1, NC, HEAD, HEAD), lambda h, r: (h, NB - 1 - r, 0, 0))],
        out_specs=[pl.BlockSpec((4, RB, HEAD), lambda h, r: (0, NB - 1 - r, h)),
                   pl.BlockSpec((1, HEAD), lambda h, r: (0, h)), pl.BlockSpec((1, 1, HEAD), lambda h, r: (h, 0, 0))],
        out_shape=[jax.ShapeDtypeStruct((4, T, H * HEAD), BF16), jax.ShapeDtypeStruct((1, H * HEAD), F32),
                   jax.ShapeDtypeStruct((H, 1, HEAD), F32)],
        scratch_shapes=[pltpu.VMEM((HEAD, HEAD), F32)] + [sc() for _ in range(10)],
        compiler_params=_cp("parallel", "arbitrary"), name="hgrn_bwd")(u, u, u, u, lbv, gn, o, dy, s0)


def _softmax_rows(p_ref, L):
    rows = [p_ref[pl.ds(l, 1), :] for l in range(L)]
    m = rows[0]
    for r in rows[1:]:
        m = jnp.maximum(m, r)
    e = [jnp.exp(r - m) for r in rows]
    tot = e[0]
    for t in e[1:]:
        tot = tot + t
    return [t / tot for t in e]


def _lb_fwd(lbp):
    L, D = lbp.shape

    def body(p_ref, o_ref):
        sm = _softmax_rows(p_ref, L)
        acc = jnp.zeros((1, D), F32)
        o_ref[pl.ds(0, 1), :] = acc
        for l in range(1, L):
            acc = acc + sm[l]
            o_ref[pl.ds(l, 1), :] = acc

    return pl.pallas_call(body, out_shape=jax.ShapeDtypeStruct((L, D), F32), name="lb_fwd")(lbp)


def _lb_bwd(lbp, dlb):
    L, D = lbp.shape

    def body(p_ref, d_ref, o_ref):
        sm = _softmax_rows(p_ref, L)
        dsm = [jnp.zeros((1, D), F32)]
        for i in range(1, L):
            t = jnp.zeros((1, D), F32)
            for l in range(i, L):
                t = t + d_ref[pl.ds(l, 1), :]
            dsm.append(t)
        dot = jnp.zeros((1, D), F32)
        for i in range(L):
            dot = dot + dsm[i] * sm[i]
        for i in range(L):
            o_ref[pl.ds(i, 1), :] = sm[i] * (dsm[i] - dot)

    return pl.pallas_call(body, out_shape=jax.ShapeDtypeStruct((L, D), F32), name="lb_bwd")(lbp, dlb)


def _my_pos():
    return lax.axis_index("x"), lax.axis_index("y"), lax.axis_index("c")


def _peer(mask):
    x, y, c = _my_pos()
    mx, my, mc = (mask >> 2) & 1, (mask >> 1) & 1, mask & 1
    px = (1 - x) if mx else x
    py = (1 - y) if my else y
    pc = (1 - c) if mc else c
    return (px, py, pc), 4 * px + 2 * py + pc


def _all_gather(shards):
    n = len(shards)

    def body(*refs):
        ins, outs = refs[:n], refs[n:2 * n]
        send_sems, recv_sems, local_sems = refs[2 * n:]
        x, y, c = _my_pos()
        me = 4 * x + 2 * y + c
        local = [pltpu.make_async_copy(ins[a], outs[a].at[:, me], local_sems.at[a]) for a in range(n)]
        for cp in local:
            cp.start()
        sends = []
        for m in range(1, N_DEV):
            peer, _ = _peer(m)
            for a in range(n):
                cp = pltpu.make_async_remote_copy(
                    src_ref=ins[a], dst_ref=outs[a].at[:, me], send_sem=send_sems.at[a, m - 1],
                    recv_sem=recv_sems.at[a, m - 1], device_id=peer, device_id_type=MESH)
                cp.start()
                sends.append(cp)
        for m in range(1, N_DEV):
            peer, pid = _peer(m)
            for a in range(n):
                pltpu.make_async_remote_copy(
                    src_ref=ins[a], dst_ref=outs[a].at[:, pid], send_sem=send_sems.at[a, m - 1],
                    recv_sem=recv_sems.at[a, m - 1], device_id=peer, device_id_type=MESH).wait_recv()
        for cp in sends:
            cp.wait_send()
        for cp in local:
            cp.wait()

    out_shape = [jax.ShapeDtypeStruct((s.shape[0], N_DEV) + s.shape[1:], s.dtype) for s in shards]
    return pl.pallas_call(
        body, in_specs=[ANY] * n, out_specs=[ANY] * n, out_shape=out_shape,
        scratch_shapes=[pltpu.SemaphoreType.DMA((n, N_DEV - 1)), pltpu.SemaphoreType.DMA((n, N_DEV - 1)),
                        pltpu.SemaphoreType.DMA((n,))],
        name="all_gather_weights")(*shards)


def _exchange(grads, groups):
    n = len(grads)
    ng = 1 + max(g for g, _ in groups)
    layers = [1 + max(l for g, l in groups if g == gi) for gi in range(ng)]
    shapes = [None] * ng
    for a, (g, l) in enumerate(groups):
        shapes[g] = grads[a].shape[1:]

    def body(*refs):
        ins, outs = refs[:n], refs[n:n + ng]
        send_sems, recv_sems, local_sems = refs[n + ng:]
        x, y, c = _my_pos()
        me = 4 * x + 2 * y + c
        local = []
        for a, (g, l) in enumerate(groups):
            cp = pltpu.make_async_copy(ins[a].at[me], outs[g].at[me, l], local_sems.at[a])
            cp.start()
            local.append(cp)
        sends = []
        for m in range(1, N_DEV):
            peer, pid = _peer(m)
            for a, (g, l) in enumerate(groups):
                cp = pltpu.make_async_remote_copy(
                    src_ref=ins[a].at[pid], dst_ref=outs[g].at[me, l], send_sem=send_sems.at[a, m - 1],
                    recv_sem=recv_sems.at[a, m - 1], device_id=peer, device_id_type=MESH)
                cp.start()
                sends.append(cp)
        for m in range(1, N_DEV):
            peer, pid = _peer(m)
            for a, (g, l) in enumerate(groups):
                pltpu.make_async_remote_copy(
                    src_ref=ins[a].at[pid], dst_ref=outs[g].at[pid, l], send_sem=send_sems.at[a, m - 1],
                    recv_sem=recv_sems.at[a, m - 1], device_id=peer, device_id_type=MESH).wait_recv()
        for cp in sends:
            cp.wait_send()
        for cp in local:
            cp.wait()

    out_shape = [jax.ShapeDtypeStruct((N_DEV, layers[g]) + shapes[g], grads[[gg for gg, _ in groups].index(g)].dtype)
                 for g in range(ng)]
    return pl.pallas_call(
        body, in_specs=[ANY] * n, out_specs=[ANY] * ng, out_shape=out_shape,
        scratch_shapes=[pltpu.SemaphoreType.DMA((n, N_DEV - 1)), pltpu.SemaphoreType.DMA((n, N_DEV - 1)),
                        pltpu.SemaphoreType.DMA((n,))],
        name="exchange_grads")(*grads)


def _adamw(recv, w, m, v):
    R, C = w.shape
    tr = _tile(R, max(8, (1 << 19) // C), 8) if R % 8 == 0 else R
    bc1 = 1.0 - ADAM_B1 ** ADAM_STEP
    bc2 = 1.0 - ADAM_B2 ** ADAM_STEP

    def body(r_ref, w_ref, m_ref, v_ref, g_ref, d_ref, nm_ref, nv_ref):
        g = r_ref[0].astype(F32)
        for s in range(1, N_DEV):
            g = g + r_ref[s].astype(F32)
        nm = ADAM_B1 * m_ref[...] + (1.0 - ADAM_B1) * g
        nv = ADAM_B2 * v_ref[...] + (1.0 - ADAM_B2) * (g * g)
        mh = nm / bc1
        vh = nv / bc2
        g_ref[...] = g
        d_ref[...] = -ADAM_LR * (mh / (jnp.sqrt(vh) + ADAM_EPS) + ADAM_WD * w_ref[...])
        nm_ref[...] = nm
        nv_ref[...] = nv

    row = pl.BlockSpec((tr, C), lambda i: (i, 0))
    return pl.pallas_call(
        body, grid=(R // tr,),
        in_specs=[pl.BlockSpec((N_DEV, tr, C), lambda i: (0, i, 0)), row, row, row],
        out_specs=[row] * 4, out_shape=[jax.ShapeDtypeStruct((R, C), F32)] * 4,
        compiler_params=_cp("parallel"), name="adamw")(recv, w, m, v)


def _full_w_spec(layer):
    return lambda tk, tn: pl.BlockSpec((None, tk, tn), lambda i, j, k: (layer, k, j))


def _colblk_w_spec(layer, n):
    def spec(tk, tn):
        per = n // tn
        return pl.BlockSpec((None, None, tk, tn), lambda i, j, k: (layer, j // per, k, j % per))
    return spec


def kernel(x, meta_tokens, mix_norm_g, mlp_norm_g, final_norm_g, ev_w_in, ev_conv_w, ev_conv_b, ev_ln_g, ev_ln_b, ev_pool_w, ev_pool_b, ev_pool_scale, ev_w_out, od_w_in, od_gnorm_g, od_w_out, lb_param, mlp_w1, mlp_w2, loss_target, m_meta_tokens, m_mix_norm_g, m_mlp_norm_g, m_final_norm_g, m_ev_w_in, m_ev_conv_w, m_ev_conv_b, m_ev_ln_g, m_ev_ln_b, m_ev_pool_w, m_ev_pool_b, m_ev_pool_scale, m_ev_w_out, m_od_w_in, m_od_gnorm_g, m_od_w_out, m_lb_param, m_mlp_w1, m_mlp_w2, v_meta_tokens, v_mix_norm_g, v_mlp_norm_g, v_final_norm_g, v_ev_w_in, v_ev_conv_w, v_ev_conv_b, v_ev_ln_g, v_ev_ln_b, v_ev_pool_w, v_ev_pool_b, v_ev_pool_scale, v_ev_w_out, v_od_w_in, v_od_gnorm_g, v_od_w_out, v_lb_param, v_mlp_w1, v_mlp_w2):
    S, D = x.shape[1], x.shape[2]
    T = PAD + N_META + S
    DEPTH = mix_norm_g.shape[0]
    DFF = mlp_w1.shape[2] * N_DEV
    dev = 4 * lax.axis_index("x") + 2 * lax.axis_index("y") + lax.axis_index("c")

    shards = [ev_w_in.astype(BF16), ev_w_out.astype(BF16), od_w_in.astype(BF16), od_w_out.astype(BF16),
              mlp_w1.astype(BF16), mlp_w2.astype(BF16), meta_tokens[None], ev_conv_w]
    g_evin, g_evout, g_odin, g_odout, g_w1, g_w2, g_meta, g_cw = _all_gather(shards)
    n_ev = ev_w_in.shape[0]
    n_od = od_w_in.shape[0]
    w_evin = jnp.transpose(g_evin, (0, 2, 1, 3)).reshape(n_ev, D, -1)
    w_evout = g_evout.reshape(n_ev, -1, D)
    w_odout = g_odout.reshape(n_od, -1, D)
    w_w2 = g_w2.reshape(DEPTH, DFF, D)
    meta_full = jnp.transpose(g_meta[0], (1, 0, 2)).reshape(N_META, D)
    cw_full = jnp.transpose(g_cw, (0, 2, 1, 3)).reshape(n_ev, CONV_WIDTH, -1)
    cw_pad = jnp.pad(cw_full, ((0, 0), (0, 32 - CONV_WIDTH), (0, 0)))
    n_in_od = od_w_in.shape[2]
    n_w1 = mlp_w1.shape[2]

    h = jnp.concatenate([jnp.zeros((PAD, D), F32), meta_full, x[0]], axis=0)
    tgt = jnp.pad(loss_target[0], ((PAD + N_META, 0), (0, 0)))
    lb_all = _lb_fwd(lb_param)

    tm_big = _tile(T, MM_ROWS_BIG, 16)
    tm_mid = _tile(T, MM_ROWS_MID, 16)

    saved = []
    for layer in range(DEPTH):
        j = layer // 2
        sv = {"h0": h}
        n = _rms_fwd(h, mix_norm_g[layer][None])
        sv["n"] = n
        if layer % 2 == 0:
            u = _mm_nn("ev_in", n, w_evin, _full_w_spec(j), T, w_evin.shape[2], D, tm_big, 512, D, "f32")
            yab, yc = _ev_fwd(u, cw_pad[j], ev_conv_b[j][None], ev_ln_g[j][None], ev_ln_b[j][None],
                              ev_pool_w[j].astype(BF16), ev_pool_b[j].reshape(1, -1), ev_pool_scale[j][None])
            sv.update(u=u, y=yab, yc=yc)
            h = _mm_nn("ev_out", yab, w_evout, _full_w_spec(j), T, D, D, tm_mid, D, D, "resid", extra=h)
        else:
            u = _mm_nn("od_in", n, g_odin, _colblk_w_spec(j, n_in_od), T, N_DEV * n_in_od, D, tm_big, 512, D, "f32")
            y, o, s0 = _hgrn_fwd(u, lb_all[layer][None], od_gnorm_g[j][None])
            sv.update(u=u, y=y, o=o, s0=s0)
            h = _mm_nn("od_out", y, w_odout, _full_w_spec(j), T, D, D, tm_mid, D, D, "resid", extra=h)
        sv["h1"] = h
        n2 = _rms_fwd(h, mlp_norm_g[layer][None])
        r, act = _mm_nn("mlp_w1", n2, g_w1, _colblk_w_spec(layer, n_w1), T, DFF, D, tm_big, 512, D, "relu2")
        sv.update(n2=n2, r=r, act=act)
        h = _mm_nn("mlp_w2", act, w_w2, _full_w_spec(layer), T, D, DFF, tm_mid, D, 1024, "resid", extra=h)
        saved.append(sv)

    loss_blk, dh, dhb, dg_final = _loss_head(h, final_norm_g[None], tgt)
    loss = lax.psum(loss_blk[0, 0], AXES)

    tt = _tile(T, MM_ROWS_BIG, 16)
    g_mix, g_mlp = [None] * DEPTH, [None] * DEPTH
    gw = {"ev_in": [None] * n_ev, "ev_out": [None] * n_ev, "od_in": [None] * n_od, "od_out": [None] * n_od,
          "w1": [None] * DEPTH, "w2": [None] * DEPTH}
    small = {"cw": [None] * n_ev, "vec": [None] * n_ev, "pw": [None] * n_ev, "gn": [None] * n_od}
    dlb_rows = [jnp.zeros((1, D), F32) for _ in range(DEPTH)]

    def xs2(tt_, tk):
        return pl.BlockSpec((tt_, tk), lambda a, b, t: (t, a))

    def ys2(tt_, tn):
        return pl.BlockSpec((tt_, tn), lambda a, b, t: (t, b))

    def os2(tk, tn):
        return pl.BlockSpec((tk, tn), lambda a, b, t: (a, b))

    def os3(tk, tn):
        return pl.BlockSpec((None, tk, tn), lambda a, b, t: (b, a, 0))

    for layer in reversed(range(DEPTH)):
        j = layer // 2
        sv = saved[layer]
        da1 = _mm_nt("mlp_w2_t", dhb, w_w2, lambda tm, tn: pl.BlockSpec((tm, tn), lambda i, jj, k: (i, k)),
                     lambda tj, tn: pl.BlockSpec((None, tj, tn), lambda i, jj, k: (layer, jj, k)),
                     T, DFF, D, tm_mid, 1024, D, "dact", extra=sv["r"])
        gw["w2"][layer] = _mm_tn("mlp_dw2", sv["act"], dhb, xs2, ys2, os2, (DFF, D), T, DFF, D, tt, 1024, D)
        gw["w1"][layer] = _mm_tn("mlp_dw1", sv["n2"], da1, xs2, ys2, os3, (N_DEV, D, n_w1), T, D, DFF, tt, D, n_w1)
        dn2 = _mm_nt("mlp_w1_t", da1, g_w1, lambda tm, tn: pl.BlockSpec((tm, tn), lambda i, jj, k: (i, k)),
                     lambda tj, tn: pl.BlockSpec((None, None, tj, tn), lambda i, jj, k: (layer, k, jj, 0)),
                     T, D, DFF, tm_mid, D, n_w1, "f32")
        dh, dhb, g_mlp[layer] = _rms_bwd(dn2, sv["h1"], mlp_norm_g[layer][None], dh)
        if layer % 2 == 0:
            dyab = _mm_nt("ev_out_t", dhb, w_evout, lambda tm, tn: pl.BlockSpec((tm, tn), lambda i, jj, k: (i, k)),
                          lambda tj, tn: pl.BlockSpec((None, tj, tn), lambda i, jj, k: (j, jj, k)),
                          T, D, D, tm_mid, D, D, "f32")
            gw["ev_out"][j] = _mm_tn("ev_dwout", sv["y"], dhb, xs2, ys2, os2, (D, D), T, D, D, tt, D, D)
            du, small["cw"][j], small["vec"][j], small["pw"][j] = _ev_bwd(
                dyab, sv["yc"], sv["u"], cw_pad[j], ev_ln_g[j][None], ev_ln_b[j][None], ev_pool_w[j].astype(BF16),
                jnp.transpose(ev_pool_w[j], (0, 2, 1)).astype(BF16), ev_pool_b[j].reshape(1, -1),
                ev_pool_scale[j][None])
            nin = du.shape[1]
            dwin = _mm_tn("ev_dwin", sv["n"], du, xs2, ys2, os2, (D, nin), T, D, nin, tt, D, 512)
            gw["ev_in"][j] = jnp.transpose(dwin.reshape(D, N_DEV, nin // N_DEV), (1, 0, 2))
            dn = _mm_nt("ev_in_t", du, w_evin, lambda tm, tn: pl.BlockSpec((tm, tn), lambda i, jj, k: (i, k)),
                        lambda tj, tn: pl.BlockSpec((None, tj, tn), lambda i, jj, k: (j, jj, k)),
                        T, D, nin, tm_mid, D, 512, "f32")
        else:
            dy = _mm_nt("od_out_t", dhb, w_odout, lambda tm, tn: pl.BlockSpec((tm, tn), lambda i, jj, k: (i, k)),
                        lambda tj, tn: pl.BlockSpec((None, tj, tn), lambda i, jj, k: (j, jj, k)),
                        T, D, D, tm_mid, D, D, "f32")
            gw["od_out"][j] = _mm_tn("od_dwout", sv["y"], dhb, xs2, ys2, os2, (D, D), T, D, D, tt, D, D)
            du3, dlb_rows[layer], small["gn"][j] = _hgrn_bwd(dy, sv["o"], sv["s0"], sv["u"], lb_all[layer][None],
                                                              od_gnorm_g[j][None])
            per = D // n_in_od

            def du_t(tt_, tn):
                return pl.BlockSpec((None, tt_, tn), lambda a, b, t: (b // per, t, b % per))

            gw["od_in"][j] = _mm_tn("od_dwin", sv["n"], du3, xs2, du_t, os3, (N_DEV, D, n_in_od), T, D,
                                    4 * D, tt, D, n_in_od)
            dn = _mm_nt("od_in_t", du3, g_odin,
                        lambda tm, tn: pl.BlockSpec((None, tm, tn), lambda i, jj, k: (k // per, i, k % per)),
                        lambda tj, tn: pl.BlockSpec((None, None, tj, tn), lambda i, jj, k: (j, k, jj, 0)),
                        T, D, 4 * D, tm_mid, D, n_in_od, "f32")
        dh, dhb, g_mix[layer] = _rms_bwd(dn, sv["h0"], mix_norm_g[layer][None], dh)

    dmeta = dh[PAD:PAD + N_META]
    grad_x = dh[PAD + N_META:][None]
    dlb_param = _lb_bwd(lb_param, jnp.concatenate(dlb_rows, axis=0))

    pieces = [
        ("meta", dmeta), ("mix", jnp.concatenate(g_mix, 0)), ("mlp", jnp.concatenate(g_mlp, 0)), ("final", dg_final),
        ("cw", jnp.stack([c[:CONV_WIDTH] for c in small["cw"]])), ("cb", jnp.stack([v_[0] for v_ in small["vec"]])),
        ("lng", jnp.stack([v_[1] for v_ in small["vec"]])), ("lnb", jnp.stack([v_[2] for v_ in small["vec"]])),
        ("pw", jnp.stack(small["pw"])), ("pb", jnp.stack([v_[4] for v_ in small["vec"]])),
        ("ps", jnp.stack([v_[3] for v_ in small["vec"]])), ("gn", jnp.stack([jnp.sum(g_, axis=0)[0] for g_ in small["gn"]])),
        ("lb", dlb_param),
    ]
    flat = jnp.concatenate([p.reshape(-1) for _, p in pieces])
    n_small = flat.shape[0]
    rows_small = -(-n_small // 1024 // 8) * 8
    flat = jnp.pad(flat, (0, rows_small * 1024 - n_small)).reshape(rows_small, 1024)

    grads, groups = [], []
    order = ["ev_in", "ev_out", "od_in", "od_out", "w1", "w2"]
    for gi, key in enumerate(order):
        for l, gmat in enumerate(gw[key]):
            if gmat.ndim == 2:
                gmat = gmat.reshape(N_DEV, gmat.shape[0] // N_DEV, gmat.shape[1])
            grads.append(gmat)
            groups.append((gi, l))
    grads.append(jnp.broadcast_to(flat[None], (N_DEV,) + flat.shape))
    groups.append((len(order), 0))
    recv = _exchange(grads, groups)

    outs = {}
    big = {"ev_in": ("ev_w_in", ev_w_in, m_ev_w_in, v_ev_w_in), "ev_out": ("ev_w_out", ev_w_out, m_ev_w_out, v_ev_w_out),
           "od_in": ("od_w_in", od_w_in, m_od_w_in, v_od_w_in), "od_out": ("od_w_out", od_w_out, m_od_w_out, v_od_w_out),
           "w1": ("mlp_w1", mlp_w1, m_mlp_w1, v_mlp_w1), "w2": ("mlp_w2", mlp_w2, m_mlp_w2, v_mlp_w2)}
    for gi, key in enumerate(order):
        name, w, m, v = big[key]
        C = w.shape[-1]
        res = _adamw(recv[gi].reshape(N_DEV, -1, C), w.reshape(-1, C), m.reshape(-1, C), v.reshape(-1, C))
        outs[name] = [r_.reshape(w.shape) for r_ in res]

    small_params = {
        "meta": ("meta_tokens", None), "mix": ("mix_norm_g", mix_norm_g, m_mix_norm_g, v_mix_norm_g),
        "mlp": ("mlp_norm_g", mlp_norm_g, m_mlp_norm_g, v_mlp_norm_g),
        "final": ("final_norm_g", final_norm_g, m_final_norm_g, v_final_norm_g),
        "cw": ("ev_conv_w", None), "cb": ("ev_conv_b", ev_conv_b, m_ev_conv_b, v_ev_conv_b),
        "lng": ("ev_ln_g", ev_ln_g, m_ev_ln_g, v_ev_ln_g), "lnb": ("ev_ln_b", ev_ln_b, m_ev_ln_b, v_ev_ln_b),
        "pw": ("ev_pool_w", ev_pool_w, m_ev_pool_w, v_ev_pool_w), "pb": ("ev_pool_b", ev_pool_b, m_ev_pool_b, v_ev_pool_b),
        "ps": ("ev_pool_scale", ev_pool_scale, m_ev_pool_scale, v_ev_pool_scale),
        "gn": ("od_gnorm_g", od_gnorm_g, m_od_gnorm_g, v_od_gnorm_g), "lb": ("lb_param", lb_param, m_lb_param, v_lb_param),
    }
    csh = ev_conv_w.shape[2]
    msh = meta_tokens.shape[1]

    def packed(which):
        parts = []
        for key, g_ in pieces:
            ent = small_params[key]
            if key == "meta":
                src = (meta_tokens, m_meta_tokens, v_meta_tokens)[which]
                full = lax.dynamic_update_slice(jnp.zeros((N_META, D), F32), src, (0, dev * msh))
            elif key == "cw":
                src = (ev_conv_w, m_ev_conv_w, v_ev_conv_w)[which]
                full = lax.dynamic_update_slice(jnp.zeros(g_.shape, F32), src, (0, 0, dev * csh))
            else:
                full = ent[1 + which]
            parts.append(full.reshape(-1))
        f = jnp.concatenate(parts)
        return jnp.pad(f, (0, rows_small * 1024 - n_small)).reshape(rows_small, 1024)

    sres = _adamw(recv[len(order)][:, 0], packed(0), packed(1), packed(2))
    off = 0
    for key, g_ in pieces:
        size = g_.size
        vals = [r_.reshape(-1)[off:off + size].reshape(g_.shape) for r_ in sres]
        off += size
        name = small_params[key][0]
        if key == "meta":
            vals = [lax.dynamic_slice(v_, (0, dev * msh), (N_META, msh)) for v_ in vals]
        elif key == "cw":
            vals = [lax.dynamic_slice(v_, (0, 0, dev * csh), v_.shape[:2] + (csh,)) for v_ in vals]
        else:
            vals = [v_.reshape(small_params[key][1].shape) for v_ in vals]
        outs[name] = vals

    names = ["meta_tokens", "mix_norm_g", "mlp_norm_g", "final_norm_g", "ev_w_in", "ev_conv_w", "ev_conv_b", "ev_ln_g",
             "ev_ln_b", "ev_pool_w", "ev_pool_b", "ev_pool_scale", "ev_w_out", "od_w_in", "od_gnorm_g", "od_w_out",
             "lb_param", "mlp_w1", "mlp_w2"]
    result = [loss, grad_x]
    for k in range(4):
        result += [outs[nm][k] for nm in names]
    return tuple(result)
```

```python
import functools

import jax
import jax.numpy as jnp
from jax import lax
from jax.experimental import pallas as pl
from jax.experimental.pallas import tpu as pltpu

F32 = jnp.float32
BF16 = jnp.bfloat16

N_DEV = 8
N_META = 16
CHUNK = 64
PAD = CHUNK - N_META
SUB = 16
HEAD = 128
CONV_WIDTH = 31
HALO = 32
POOL_WINDOWS = (2, 4, 8, 16)
EPS = 1e-6
NEG = -1e30
ADAM_LR, ADAM_B1, ADAM_B2, ADAM_EPS, ADAM_WD, ADAM_STEP = 0.001, 0.9, 0.999, 1e-08, 0.01, 10
VMEM_LIMIT = 56 * 1024 * 1024
EV_ROWS = 416
HGRN_ROWS = 832
MM_ROWS_BIG = 2080
MM_ROWS_MID = 1040
MESH = pl.DeviceIdType.MESH
AXES = ("x", "y", "c")
ANY = pl.BlockSpec(memory_space=pl.ANY)


def _cp(*sem):
    return pltpu.CompilerParams(dimension_semantics=sem, vmem_limit_bytes=VMEM_LIMIT)


def _tile(n, cap, mult):
    best = None
    for d in range(mult, min(n, cap) + 1, mult):
        if n % d == 0:
            best = d
    assert best is not None, (n, cap, mult)
    return best


def _nt(a, b):
    return lax.dot_general(a, b, (((1,), (1,)), ((), ())), preferred_element_type=F32)


def _tn(a, b):
    return lax.dot_general(a, b, (((0,), (0,)), ((), ())), preferred_element_type=F32)


def _nn(a, b):
    return jnp.dot(a, b, preferred_element_type=F32)


def _row_ids(base, n):
    return base + lax.broadcasted_iota(jnp.int32, (n, 1), 0)


def _dsilu(x, s):
    return s * (1.0 + x * (1.0 - s))


def _rms_fwd(h, g):
    T, D = h.shape
    tm = _tile(T, MM_ROWS_MID, 16)

    def body(h_ref, g_ref, n_ref):
        x = h_ref[...]
        r = lax.rsqrt(jnp.mean(x * x, axis=-1, keepdims=True) + EPS)
        n_ref[...] = ((x * r) * g_ref[...]).astype(BF16)

    return pl.pallas_call(
        body, grid=(T // tm,),
        in_specs=[pl.BlockSpec((tm, D), lambda i: (i, 0)), pl.BlockSpec((1, D), lambda i: (0, 0))],
        out_specs=pl.BlockSpec((tm, D), lambda i: (i, 0)),
        out_shape=jax.ShapeDtypeStruct((T, D), BF16),
        compiler_params=_cp("parallel"), name="rms_fwd")(h, g)


def _rms_bwd(dn, h, g, dres):
    T, D = h.shape
    tm = _tile(T, MM_ROWS_MID, 16)

    def body(dn_ref, h_ref, g_ref, dres_ref, dh_ref, dhb_ref, dg_ref):
        i = pl.program_id(0)
        x = h_ref[...]
        dn_v = dn_ref[...]
        r = lax.rsqrt(jnp.mean(x * x, axis=-1, keepdims=True) + EPS)
        xh = x * r
        dxh = dn_v * g_ref[...]
        dx = r * (dxh - xh * jnp.mean(dxh * xh, axis=-1, keepdims=True))
        keep = _row_ids(i * tm, tm) >= PAD
        dh = jnp.where(keep, dres_ref[...] + dx, 0.0)
        dh_ref[...] = dh
        dhb_ref[...] = dh.astype(BF16)

        @pl.when(i == 0)
        def _():
            dg_ref[...] = jnp.zeros_like(dg_ref)

        dg_ref[...] += jnp.sum(dn_v * xh, axis=0, keepdims=True)

    row = pl.BlockSpec((tm, D), lambda i: (i, 0))
    vec = pl.BlockSpec((1, D), lambda i: (0, 0))
    return pl.pallas_call(
        body, grid=(T // tm,),
        in_specs=[row, row, vec, row], out_specs=[row, row, vec],
        out_shape=[jax.ShapeDtypeStruct((T, D), F32), jax.ShapeDtypeStruct((T, D), BF16),
                   jax.ShapeDtypeStruct((1, D), F32)],
        compiler_params=_cp("arbitrary"), name="rms_bwd")(dn, h, g, dres)


def _loss_head(h, g, tgt):
    T, D = h.shape
    tm = _tile(T, MM_ROWS_MID, 16)
    first_x = PAD + N_META

    def body(h_ref, g_ref, t_ref, loss_ref, dh_ref, dhb_ref, dg_ref):
        i = pl.program_id(0)
        x = h_ref[...]
        r = lax.rsqrt(jnp.mean(x * x, axis=-1, keepdims=True) + EPS)
        xh = x * r
        gv = g_ref[...]
        out = xh * gv
        valid = _row_ids(i * tm, tm) >= first_x
        e = jnp.where(valid, out - t_ref[...], 0.0)
        dout = e * (1.0 / D)
        dxh = dout * gv
        dx = r * (dxh - xh * jnp.mean(dxh * xh, axis=-1, keepdims=True))
        dh_ref[...] = dx
        dhb_ref[...] = dx.astype(BF16)

        @pl.when(i == 0)
        def _():
            dg_ref[...] = jnp.zeros_like(dg_ref)
            loss_ref[...] = jnp.zeros_like(loss_ref)

        dg_ref[...] += jnp.sum(dout * xh, axis=0, keepdims=True)
        loss_ref[...] += 0.5 * jnp.sum(jnp.mean(e * e, axis=-1, keepdims=True))

    row = pl.BlockSpec((tm, D), lambda i: (i, 0))
    vec = pl.BlockSpec((1, D), lambda i: (0, 0))
    return pl.pallas_call(
        body, grid=(T // tm,),
        in_specs=[row, vec, row],
        out_specs=[pl.BlockSpec((8, 128), lambda i: (0, 0)), row, row, vec],
        out_shape=[jax.ShapeDtypeStruct((8, 128), F32), jax.ShapeDtypeStruct((T, D), F32),
                   jax.ShapeDtypeStruct((T, D), BF16), jax.ShapeDtypeStruct((1, D), F32)],
        compiler_params=_cp("arbitrary"), name="loss_head")(h, g, tgt)


def _mm_nn(name, a, w, w_spec, M, N, K, tm, tn, tk, mode, extra=None, a_spec=None):
    nk = K // tk
    if a_spec is None:
        a_spec = pl.BlockSpec((tm, tk), lambda i, j, k: (i, k))
    o_spec = pl.BlockSpec((tm, tn), lambda i, j, k: (i, j))

    def body(*refs):
        if mode == "resid":
            a_ref, w_ref, e_ref = refs[:3]
            outs = refs[3:]
        else:
            a_ref, w_ref = refs[:2]
            outs = refs[2:]
        acc_ref = outs[-1] if nk > 1 else None
        part = _nn(a_ref[...], w_ref[...])

        def finish(acc):
            if mode == "f32":
                outs[0][...] = acc
            elif mode == "relu2":
                r = jnp.maximum(acc, 0.0)
                outs[0][...] = r.astype(BF16)
                outs[1][...] = (r * r).astype(BF16)
            else:
                keep = _row_ids(pl.program_id(0) * tm, tm) >= PAD
                outs[0][...] = jnp.where(keep, e_ref[...] + acc, 0.0)

        if nk == 1:
            finish(part)
        else:
            k = pl.program_id(2)

            @pl.when(k == 0)
            def _():
                acc_ref[...] = part

            @pl.when(k > 0)
            def _():
                acc_ref[...] += part

            @pl.when(k == nk - 1)
            def _():
                finish(acc_ref[...])

    in_specs = [a_spec, w_spec(tk, tn)]
    args = [a, w]
    if mode == "resid":
        in_specs.append(o_spec)
        args.append(extra)
    if mode == "relu2":
        out_specs = [o_spec, o_spec]
        out_shape = [jax.ShapeDtypeStruct((M, N), BF16)] * 2
    else:
        out_specs = [o_spec]
        out_shape = [jax.ShapeDtypeStruct((M, N), F32)]
    scratch = [pltpu.VMEM((tm, tn), F32)] if nk > 1 else []
    res = pl.pallas_call(
        body, grid=(M // tm, N // tn, nk), in_specs=in_specs, out_specs=out_specs, out_shape=out_shape,
        scratch_shapes=scratch, compiler_params=_cp("parallel", "parallel", "arbitrary"), name=name)(*args)
    return res if mode == "relu2" else res[0]


def _mm_nt(name, dy, w, dy_spec, w_spec, M, J, N, tm, tj, tn, mode, extra=None):
    nk = N // tn
    o_spec = pl.BlockSpec((tm, tj), lambda i, j, k: (i, j))

    def body(*refs):
        if mode == "dact":
            dy_ref, w_ref, e_ref = refs[:3]
            outs = refs[3:]
        else:
            dy_ref, w_ref = refs[:2]
            outs = refs[2:]
        acc_ref = outs[-1] if nk > 1 else None
        part = _nt(dy_ref[...], w_ref[...])

        def finish(acc):
            if mode == "f32":
                outs[0][...] = acc
            else:
                outs[0][...] = (acc * (2.0 * e_ref[...].astype(F32))).astype(BF16)

        if nk == 1:
            finish(part)
        else:
            k = pl.program_id(2)

            @pl.when(k == 0)
            def _():
                acc_ref[...] = part

            @pl.when(k > 0)
            def _():
                acc_ref[...] += part

            @pl.when(k == nk - 1)
            def _():
                finish(acc_ref[...])

    in_specs = [dy_spec(tm, tn), w_spec(tj, tn)]
    args = [dy, w]
    if mode == "dact":
        in_specs.append(o_spec)
        args.append(extra)
    scratch = [pltpu.VMEM((tm, tj), F32)] if nk > 1 else []
    return pl.pallas_call(
        body, grid=(M // tm, J // tj, nk), in_specs=in_specs, out_specs=[o_spec],
        out_shape=[jax.ShapeDtypeStruct((M, J), BF16 if mode == "dact" else F32)],
        scratch_shapes=scratch, compiler_params=_cp("parallel", "parallel", "arbitrary"), name=name)(*args)[0]


def _mm_tn(name, x, dy, x_spec, dy_spec, o_spec, o_shape, T, K, N, tt, tk, tn):
    nt = T // tt

    def body(x_ref, dy_ref, o_ref, acc_ref):
        t = pl.program_id(2)
        part = _tn(x_ref[...], dy_ref[...])

        @pl.when(t == 0)
        def _():
            acc_ref[...] = part

        @pl.when(t > 0)
        def _():
            acc_ref[...] += part

        @pl.when(t == nt - 1)
        def _():
            o_ref[...] = acc_ref[...].astype(BF16)

    return pl.pallas_call(
        body, grid=(K // tk, N // tn, nt), in_specs=[x_spec(tt, tk), dy_spec(tt, tn)], out_specs=o_spec(tk, tn),
        out_shape=jax.ShapeDtypeStruct(o_shape, BF16), scratch_shapes=[pltpu.VMEM((tk, tn), F32)],
        compiler_params=_cp("parallel", "parallel", "arbitrary"), name=name)(x, dy)


def _pool_counts(base, n, w):
    pos = _row_ids(base, n) - PAD
    return jnp.clip(pos + 1, 1, w).astype(F32)


def _ev_fwd(u, cw, cb, lg, lb, pw, pb, ps):
    T = u.shape[0]
    C = 512
    tm = _tile(T, EV_ROWS, HALO)
    nsub = tm // HALO
    hb = tm // HALO

    def body(val_ref, gate_ref, pin_ref, valh_ref, gateh_ref, pinh_ref, cw_ref, cb_ref, lg_ref, lb_ref, pw_ref,
             pb_ref, ps_ref, yab_ref, yc_ref, a_ext, p_ext, d_buf):
        i = pl.program_id(0)
        nf = (i > 0).astype(F32)
        a_ext[0:HALO, :] = valh_ref[...] * jax.nn.sigmoid(gateh_ref[...]) * nf
        a_ext[HALO:, :] = val_ref[...] * jax.nn.sigmoid(gate_ref[...])
        p_ext[0:HALO, :] = pinh_ref[...] * nf
        p_ext[HALO:, :] = pin_ref[...]

        def sub(s, carry):
            base = pl.multiple_of(s * HALO, HALO)
            win = a_ext[pl.ds(base, 2 * HALO), :]
            acc = jnp.zeros((HALO, C), F32) + cb_ref[...]
            for j in range(CONV_WIDTH):
                acc = acc + cw_ref[pl.ds(j, 1), :] * win[2 + j:2 + j + HALO]
            yc_ref[pl.ds(base, HALO), :] = acc
            mu = jnp.mean(acc, axis=-1, keepdims=True)
            yc = acc - mu
            rstd = lax.rsqrt(jnp.mean(yc * yc, axis=-1, keepdims=True) + EPS)
            z = (yc * rstd) * lg_ref[...] + lb_ref[...]
            yab_ref[pl.ds(base, HALO), 0:C] = (z * jax.nn.sigmoid(z)).astype(BF16)
            pwin = p_ext[pl.ds(base, 2 * HALO), :]
            for gi, w in enumerate(POOL_WINDOWS):
                lo, hi = gi * HEAD, (gi + 1) * HEAD
                x = pwin[HALO:, lo:hi]
                tot = x
                for k in range(1, w):
                    tot = tot + pwin[HALO - k:2 * HALO - k, lo:hi]
                cnt = _pool_counts(i * tm + base, HALO, w)
                d_buf[pl.ds(base, HALO), lo:hi] = (tot / cnt - x).astype(BF16)
            return carry

        lax.fori_loop(0, nsub, sub, 0)
        for gi in range(len(POOL_WINDOWS)):
            lo, hi = gi * HEAD, (gi + 1) * HEAD
            y = _nn(d_buf[:, lo:hi], pw_ref[gi]) + pb_ref[:, lo:hi]
            yab_ref[:, C + lo:C + hi] = (y * ps_ref[:, lo:hi]).astype(BF16)

    def main(c):
        return pl.BlockSpec((tm, C), lambda i: (i, c))

    def halo(c):
        return pl.BlockSpec((HALO, C), lambda i: (jnp.maximum(i * hb - 1, 0), c))

    vec = pl.BlockSpec((1, C), lambda i: (0, 0))
    return pl.pallas_call(
        body, grid=(T // tm,),
        in_specs=[main(0), main(1), main(2), halo(0), halo(1), halo(2),
                  pl.BlockSpec((32, C), lambda i: (0, 0)), vec, vec, vec,
                  pl.BlockSpec((4, HEAD, HEAD), lambda i: (0, 0, 0)), vec, vec],
        out_specs=[pl.BlockSpec((tm, 2 * C), lambda i: (i, 0)), pl.BlockSpec((tm, C), lambda i: (i, 0))],
        out_shape=[jax.ShapeDtypeStruct((T, 2 * C), BF16), jax.ShapeDtypeStruct((T, C), F32)],
        scratch_shapes=[pltpu.VMEM((tm + HALO, C), F32), pltpu.VMEM((tm + HALO, C), F32), pltpu.VMEM((tm, C), BF16)],
        compiler_params=_cp("parallel"), name="ev_fwd")(u, u, u, u, u, u, cw, cb, lg, lb, pw, pb, ps)


def _ev_bwd(dyab, yc, u, cw, lg, lb, pw, pwt, pb, ps):
    T = u.shape[0]
    C = 512
    tm = _tile(T, EV_ROWS, HALO)
    nsub = tm // HALO
    hb = tm // HALO
    nblk = T // tm
    E = tm + HALO

    def body(dya_ref, dyb_ref, dyah_ref, dybh_ref, yc_ref, ych_ref, val_ref, gate_ref, pin_ref, valh_ref, gateh_ref,
             pinh_ref, cw_ref, lg_ref, lb_ref, pw_ref, pwt_ref, pb_ref, ps_ref,
             du_ref, dcw_ref, dvec_ref, dpw_ref,
             dy_ext, a_ext, p_ext, ddc_ext, dd_buf, d_buf, dpre_buf, dcw_acc, vec_acc):
        i = pl.program_id(0)
        nf = (i > 0).astype(F32)
        nl = (i < nblk - 1).astype(F32)

        @pl.when(i == 0)
        def _():
            dcw_ref[...] = jnp.zeros_like(dcw_ref)
            dvec_ref[...] = jnp.zeros_like(dvec_ref)
            dpw_ref[...] = jnp.zeros_like(dpw_ref)

        dcw_acc[...] = jnp.zeros_like(dcw_acc)
        vec_acc[...] = jnp.zeros_like(vec_acc)
        a_ext[0:HALO, :] = valh_ref[...] * jax.nn.sigmoid(gateh_ref[...]) * nf
        a_ext[HALO:, :] = val_ref[...] * jax.nn.sigmoid(gate_ref[...])
        p_ext[0:HALO, :] = pinh_ref[...] * nf
        p_ext[HALO:, :] = pin_ref[...]

        def ln_bwd(y, dya, main):
            mu = jnp.mean(y, axis=-1, keepdims=True)
            ycen = y - mu
            rstd = lax.rsqrt(jnp.mean(ycen * ycen, axis=-1, keepdims=True) + EPS)
            yh = ycen * rstd
            z = yh * lg_ref[...] + lb_ref[...]
            sz = jax.nn.sigmoid(z)
            dz = dya * _dsilu(z, sz)
            dyh = dz * lg_ref[...]
            dy = rstd * (dyh - jnp.mean(dyh, axis=-1, keepdims=True) - yh * jnp.mean(dyh * yh, axis=-1, keepdims=True))
            if main:
                vec_acc[1] += jnp.sum((dz * yh).reshape(HALO // 8, 8, C), axis=0)
                vec_acc[2] += jnp.sum(dz.reshape(HALO // 8, 8, C), axis=0)
                vec_acc[0] += jnp.sum(dy.reshape(HALO // 8, 8, C), axis=0)
            return dy

        def pool_dd(dyb, base, main):
            dpre = dyb * ps_ref[...]
            for gi, w in enumerate(POOL_WINDOWS):
                lo, hi = gi * HEAD, (gi + 1) * HEAD
                dd = _nn(dpre[:, lo:hi].astype(BF16), pwt_ref[gi])
                cnt = _pool_counts(i * tm + base, HALO, w)
                ddc_ext[pl.ds(base, HALO), lo:hi] = dd / cnt
                if main:
                    dd_buf[pl.ds(base, HALO), lo:hi] = dd
            if main:
                dpre_buf[pl.ds(base, HALO), :] = dpre.astype(BF16)
                vec_acc[4] += jnp.sum(dpre.reshape(HALO // 8, 8, C), axis=0)

        def p1(s, carry):
            base = pl.multiple_of(s * HALO, HALO)
            dy_ext[pl.ds(base, HALO), :] = ln_bwd(yc_ref[pl.ds(base, HALO), :], dya_ref[pl.ds(base, HALO), :], True)
            pool_dd(dyb_ref[pl.ds(base, HALO), :], base, True)
            return carry

        lax.fori_loop(0, nsub, p1, 0)
        dy_ext[tm:, :] = ln_bwd(ych_ref[...], dyah_ref[...], False) * nl
        dpre_h = dybh_ref[...] * ps_ref[...] * nl
        for gi, w in enumerate(POOL_WINDOWS):
            lo, hi = gi * HEAD, (gi + 1) * HEAD
            dd = _nn(dpre_h[:, lo:hi].astype(BF16), pwt_ref[gi])
            ddc_ext[tm:, lo:hi] = dd / _pool_counts(i * tm + tm, HALO, w)

        def p2(s, carry):
            base = pl.multiple_of(s * HALO, HALO)
            dwin = dy_ext[pl.ds(base, 2 * HALO), :]
            awin = a_ext[pl.ds(base, 2 * HALO), :]
            dy_m = dwin[0:HALO]
            da = jnp.zeros((HALO, C), F32)
            for j in range(CONV_WIDTH):
                sh = CONV_WIDTH - 1 - j
                da = da + cw_ref[pl.ds(j, 1), :] * dwin[sh:sh + HALO]
                dcw_acc[j] += jnp.sum((dy_m * awin[2 + j:2 + j + HALO]).reshape(HALO // 8, 8, C), axis=0)
            v = val_ref[pl.ds(base, HALO), :]
            g = gate_ref[pl.ds(base, HALO), :]
            sg = jax.nn.sigmoid(g)
            du_ref[pl.ds(base, HALO), 0:C] = (da * sg).astype(BF16)
            du_ref[pl.ds(base, HALO), C:2 * C] = (da * v * sg * (1.0 - sg)).astype(BF16)
            pwin = p_ext[pl.ds(base, 2 * HALO), :]
            cwin = ddc_ext[pl.ds(base, 2 * HALO), :]
            for gi, w in enumerate(POOL_WINDOWS):
                lo, hi = gi * HEAD, (gi + 1) * HEAD
                x = pwin[HALO:, lo:hi]
                tot = x
                back = cwin[0:HALO, lo:hi]
                for k in range(1, w):
                    tot = tot + pwin[HALO - k:2 * HALO - k, lo:hi]
                    back = back + cwin[k:k + HALO, lo:hi]
                cnt = _pool_counts(i * tm + base, HALO, w)
                d_buf[pl.ds(base, HALO), lo:hi] = (tot / cnt - x).astype(BF16)
                du_ref[pl.ds(base, HALO), 2 * C + lo:2 * C + hi] = (back - dd_buf[pl.ds(base, HALO), lo:hi]).astype(BF16)
            return carry

        lax.fori_loop(0, nsub, p2, 0)
        for gi in range(len(POOL_WINDOWS)):
            lo, hi = gi * HEAD, (gi + 1) * HEAD
            pre = _nn(d_buf[:, lo:hi], pw_ref[gi]) + pb_ref[:, lo:hi]
            vec_acc[3, :, lo:hi] += jnp.sum((dyb_ref[:, lo:hi] * pre).reshape(tm // 8, 8, HEAD), axis=0)
            dpw_ref[gi] += _tn(d_buf[:, lo:hi], dpre_buf[:, lo:hi])
        for j in range(CONV_WIDTH):
            dcw_ref[pl.ds(j, 1), :] += jnp.sum(dcw_acc[j], axis=0, keepdims=True)
        for r in range(5):
            dvec_ref[pl.ds(r, 1), :] += jnp.sum(vec_acc[r], axis=0, keepdims=True)

    def main(c, width=C):
        return pl.BlockSpec((tm, width), lambda i: (i, c))

    def prev(c):
        return pl.BlockSpec((HALO, C), lambda i: (jnp.maximum(i * hb - 1, 0), c))

    def nxt(c):
        return pl.BlockSpec((HALO, C), lambda i: (jnp.minimum((i + 1) * hb, T // HALO - 1), c))

    vec = pl.BlockSpec((1, C), lambda i: (0, 0))
    mat = pl.BlockSpec((4, HEAD, HEAD), lambda i: (0, 0, 0))
    return pl.pallas_call(
        body, grid=(nblk,),
        in_specs=[main(0), main(1), nxt(0), nxt(1), main(0), nxt(0), main(0), main(1), main(2), prev(0), prev(1),
                  prev(2), pl.BlockSpec((32, C), lambda i: (0, 0)), vec, vec, mat, mat, vec, vec],
        out_specs=[pl.BlockSpec((tm, 3 * C), lambda i: (i, 0)), pl.BlockSpec((32, C), lambda i: (0, 0)),
                   pl.BlockSpec((8, C), lambda i: (0, 0)), mat],
        out_shape=[jax.ShapeDtypeStruct((T, 3 * C), BF16), jax.ShapeDtypeStruct((32, C), F32),
                   jax.ShapeDtypeStruct((8, C), F32), jax.ShapeDtypeStruct((4, HEAD, HEAD), F32)],
        scratch_shapes=[pltpu.VMEM((E, C), F32), pltpu.VMEM((E, C), F32), pltpu.VMEM((E, C), F32),
                        pltpu.VMEM((E, C), F32), pltpu.VMEM((tm, C), F32), pltpu.VMEM((tm, C), BF16),
                        pltpu.VMEM((tm, C), BF16), pltpu.VMEM((32, 8, C), F32), pltpu.VMEM((8, 8, C), F32)],
        compiler_params=_cp("arbitrary"), name="ev_bwd")(
            dyab, dyab, dyab, dyab, yc, yc, u, u, u, u, u, u, cw, lg, lb, pw, pwt, pb, ps)


def _tri(n, upper):
    r = lax.broadcasted_iota(jnp.int32, (n, n), 0)
    c = lax.broadcasted_iota(jnp.int32, (n, n), 1)
    return ((c >= r) if upper else (c <= r)).astype(F32)


def _hgrn_gates(qr, fr, lbv):
    sq = jax.nn.sigmoid(qr)
    sg = jax.nn.sigmoid(fr)
    fg = lbv + (1.0 - lbv) * sg
    return qr * sq, sq, sg, fg, 1.0 - fg, jnp.log(fg)


def _hgrn_fwd(u, lbv, gn):
    T = u.shape[0]
    H = 8
    RB = _tile(T, HGRN_ROWS, CHUNK)
    NC = RB // CHUNK
    NS = CHUNK // SUB

    def body(q_ref, f_ref, i_ref, g_ref, lb_ref, gn_ref, y_ref, o_ref, s0_ref, st, qs, ks, bs, vs, os_):
        rb = pl.program_id(1)

        @pl.when(rb == 0)
        def _():
            st[...] = jnp.zeros_like(st)

        tri = _tri(CHUNK, False)
        tsub = lax.broadcasted_iota(jnp.int32, (SUB, 1), 0)

        def chunk(c, carry):
            r0 = pl.multiple_of(c * CHUNK, CHUNK)
            rows = pl.ds(r0, CHUNK)
            q, _, _, _, kk, lf = _hgrn_gates(q_ref[rows, :], f_ref[rows, :], lb_ref[...])
            v = i_ref[rows, :]
            b = jnp.dot(tri, lf, preferred_element_type=F32, precision=lax.Precision.HIGHEST)
            qs[...] = q
            ks[...] = kk
            bs[...] = b
            vs[...] = v
            st0 = st[...]
            s0_ref[0, c] = st0
            os_[...] = _nt((q * jnp.exp(b)).astype(BF16), st0.astype(BF16))
            for I in range(NS):
                lo = I * SUB
                qI = qs[lo:lo + SUB, :]
                bI = bs[lo:lo + SUB, :]
                oI = jnp.zeros((SUB, HEAD), F32)
                if I > 0:
                    bprev = bs[pl.ds(lo - 1, 1), :]
                    qt = (qI * jnp.exp(bI - bprev)).astype(BF16)
                    kt = (ks[0:lo, :] * jnp.exp(bprev - bs[0:lo, :])).astype(BF16)
                    A = _nt(qt, kt)
                    oI = oI + _nn(A.astype(BF16), vs[0:lo, :].astype(BF16))
                for s in range(SUB):
                    row = pl.ds(lo + s, 1)
                    Es = jnp.exp(jnp.where(tsub >= s, bI - bs[row, :], NEG))
                    col = jnp.sum(qI * Es * ks[row, :], axis=1, keepdims=True)
                    oI = oI + col * vs[row, :]
                os_[lo:lo + SUB, :] += oI
            blast = bs[pl.ds(CHUNK - 1, 1), :]
            kh = kk * jnp.exp(blast - b)
            st[...] = st0 * jnp.exp(blast) + _tn(v.astype(BF16), kh.astype(BF16))
            o = os_[...]
            o_ref[rows, :] = o
            rr = lax.rsqrt(jnp.mean(o * o, axis=-1, keepdims=True) + EPS)
            gr = g_ref[rows, :]
            y_ref[rows, :] = (((o * rr) * gn_ref[...]) * (gr * jax.nn.sigmoid(gr))).astype(BF16)
            return carry

        lax.fori_loop(0, NC, chunk, 0)

    def blk(q):
        return pl.BlockSpec((RB, HEAD), lambda h, r: (r, q * H + h))

    sc = lambda: pltpu.VMEM((CHUNK, HEAD), F32)
    return pl.pallas_call(
        body, grid=(H, T // RB),
        in_specs=[blk(0), blk(1), blk(2), blk(3), pl.BlockSpec((1, HEAD), lambda h, r: (0, h)),
                  pl.BlockSpec((1, HEAD), lambda h, r: (0, 0))],
        out_specs=[pl.BlockSpec((RB, HEAD), lambda h, r: (r, h)), pl.BlockSpec((RB, HEAD), lambda h, r: (r, h)),
                   pl.BlockSpec((1, NC, HEAD, HEAD), lambda h, r: (h, r, 0, 0))],
        out_shape=[jax.ShapeDtypeStruct((T, H * HEAD), BF16), jax.ShapeDtypeStruct((T, H * HEAD), F32),
                   jax.ShapeDtypeStruct((H, T // CHUNK, HEAD, HEAD), F32)],
        scratch_shapes=[pltpu.VMEM((HEAD, HEAD), F32), sc(), sc(), sc(), sc(), sc()],
        compiler_params=_cp("parallel", "arbitrary"), name="hgrn_fwd")(u, u, u, u, lbv, gn)


def _hgrn_bwd(dy, o, s0, u, lbv, gn):
    T = u.shape[0]
    H = 8
    RB = _tile(T, HGRN_ROWS, CHUNK)
    NB = T // RB
    NC = RB // CHUNK
    NS = CHUNK // SUB

    def body(q_ref, f_ref, i_ref, g_ref, lb_ref, gn_ref, o_ref, dy_ref, s0_ref, du_ref, dlb_ref, dgn_ref,
             dst, qs, ks, bs, vs, dos, dqs, dks, dki, dvs, dbs):
        rb = pl.program_id(1)

        @pl.when(rb == 0)
        def _():
            dst[...] = jnp.zeros_like(dst)
            dlb_ref[...] = jnp.zeros_like(dlb_ref)
            dgn_ref[...] = jnp.zeros_like(dgn_ref)

        tril = _tri(CHUNK, False)
        triu = _tri(CHUNK, True)
        tsub = lax.broadcasted_iota(jnp.int32, (SUB, 1), 0)
        lbv_ = lb_ref[...]
        gnv = gn_ref[...]

        def chunk(cc, carry):
            c = NC - 1 - cc
            r0 = pl.multiple_of(c * CHUNK, CHUNK)
            rows = pl.ds(r0, CHUNK)
            qr = q_ref[rows, :]
            q, sq, sg, fg, kk, lf = _hgrn_gates(qr, f_ref[rows, :], lbv_)
            v = i_ref[rows, :]
            gr = g_ref[rows, :]
            b = jnp.dot(tril, lf, preferred_element_type=F32, precision=lax.Precision.HIGHEST)
            eb = jnp.exp(b)
            ov = o_ref[rows, :]
            dyv = dy_ref[rows, :]
            rr = lax.rsqrt(jnp.mean(ov * ov, axis=-1, keepdims=True) + EPS)
            oh = ov * rr
            gs = jax.nn.sigmoid(gr)
            dgr = dyv * (oh * gnv) * _dsilu(gr, gs)
            dnrm = dyv * (gr * gs)
            dgn_ref[0] += jnp.sum(dnrm * oh, axis=0, keepdims=True)
            t1 = dnrm * gnv
            do = rr * (t1 - oh * jnp.mean(t1 * oh, axis=-1, keepdims=True))
            qs[...] = q
            ks[...] = kk
            bs[...] = b
            vs[...] = v
            dos[...] = do
            st0 = s0_ref[0, c]
            dS = dst[...]
            do_b = do.astype(BF16)
            blast = bs[pl.ds(CHUNK - 1, 1), :]
            elast = jnp.exp(blast - b)
            dq_inter = _nn(do_b, st0.astype(BF16)) * eb
            dqs[...] = dq_inter
            dbs[...] = q * dq_inter
            kh = kk * elast
            dvs[...] = _nt(kh.astype(BF16), dS.astype(BF16))
            dk_inter = _nn(v.astype(BF16), dS.astype(BF16)) * elast
            dki[...] = dk_inter
            dks[...] = jnp.zeros_like(dks)
            for I in range(NS):
                lo = I * SUB
                qI = qs[lo:lo + SUB, :]
                bI = bs[lo:lo + SUB, :]
                doI = dos[lo:lo + SUB, :]
                dqI = jnp.zeros((SUB, HEAD), F32)
                dbI = jnp.zeros((SUB, HEAD), F32)
                if I > 0:
                    bprev = bs[pl.ds(lo - 1, 1), :]
                    eq = jnp.exp(bI - bprev)
                    ek = jnp.exp(bprev - bs[0:lo, :])
                    qt = (qI * eq).astype(BF16)
                    kt = (ks[0:lo, :] * ek).astype(BF16)
                    A = _nt(qt, kt).astype(BF16)
                    doI_b = doI.astype(BF16)
                    dA = _nt(doI_b, vs[0:lo, :].astype(BF16)).astype(BF16)
                    dvs[0:lo, :] += _tn(A, doI_b)
                    dqt = _nn(dA, kt)
                    dkt = _tn(dA, qt)
                    dqI = dqI + dqt * eq
                    dbI = dbI + qt.astype(F32) * dqt
                    dks[0:lo, :] += dkt * ek
                    dbs[0:lo, :] -= kt.astype(F32) * dkt
                for s in range(SUB):
                    row = pl.ds(lo + s, 1)
                    krow = ks[row, :]
                    Es = jnp.exp(jnp.where(tsub >= s, bI - bs[row, :], NEG))
                    qE = qI * Es
                    col = jnp.sum(qE * krow, axis=1, keepdims=True)
                    dcol = jnp.sum(doI * vs[row, :], axis=1, keepdims=True)
                    dvs[row, :] += jnp.sum(col * doI, axis=0, keepdims=True)
                    dq_s = dcol * Es * krow
                    dqI = dqI + dq_s
                    dbI = dbI + qI * dq_s
                    dk_s = jnp.sum(dcol * qE, axis=0, keepdims=True)
                    dks[row, :] += dk_s
                    dbs[row, :] -= krow * dk_s
                dqs[lo:lo + SUB, :] += dqI
                dbs[lo:lo + SUB, :] += dbI
            kdk = kk * dki[...]
            excl = jnp.dot(tril, kdk, preferred_element_type=F32, precision=lax.Precision.HIGHEST) - kdk
            suff = jnp.dot(triu, dbs[...], preferred_element_type=F32, precision=lax.Precision.HIGHEST)
            gdec = jnp.sum(dS * st0, axis=0, keepdims=True) * jnp.exp(blast)
            dlf = suff + excl + gdec
            dk = dks[...] + dki[...]
            dfg = dlf / fg - dk
            dlb_ref[...] += jnp.sum(dfg * (1.0 - sg), axis=0, keepdims=True)
            du_ref[0, rows, :] = (dqs[...] * _dsilu(qr, sq)).astype(BF16)
            du_ref[1, rows, :] = (dfg * (1.0 - lbv_) * sg * (1.0 - sg)).astype(BF16)
            du_ref[2, rows, :] = dvs[...].astype(BF16)
            du_ref[3, rows, :] = dgr.astype(BF16)
            dst[...] = dS * jnp.exp(blast) + _tn(do_b, (q * eb).astype(BF16))
            return carry

        lax.fori_loop(0, NC, chunk, 0)

    def blk(qd):
        return pl.BlockSpec((RB, HEAD), lambda h, r: (NB - 1 - r, qd * H + h))

    hblk = pl.BlockSpec((RB, HEAD), lambda h, r: (NB - 1 - r, h))
    sc = lambda: pltpu.VMEM((CHUNK, HEAD), F32)
    return pl.pallas_call(
        body, grid=(H, NB),
        in_specs=[blk(0), blk(1), blk(2), blk(3), pl.BlockSpec((1, HEAD), lambda h, r: (0, h)),
                  pl.BlockSpec((1, HEAD), lambda h, r: (0, 0)), hblk, hblk,
                  pl.BlockSpec((1, NC, HEAD, HEAD), lambda h, r: (h, NB - 1 - r, 0, 0))],
        out_specs=[pl.BlockSpec((4, RB, HEAD), lambda h, r: (0, NB - 1 - r, h)),
                   pl.BlockSpec((1, HEAD), lambda h, r: (0, h)), pl.BlockSpec((1, 1, HEAD), lambda h, r: (h, 0, 0))],
        out_shape=[jax.ShapeDtypeStruct((4, T, H * HEAD), BF16), jax.ShapeDtypeStruct((1, H * HEAD), F32),
                   jax.ShapeDtypeStruct((H, 1, HEAD), F32)],
        scratch_shapes=[pltpu.VMEM((HEAD, HEAD), F32)] + [sc() for _ in range(10)],
        compiler_params=_cp("parallel", "arbitrary"), name="hgrn_bwd")(u, u, u, u, lbv, gn, o, dy, s0)


def _softmax_rows(p_ref, L):
    rows = [p_ref[pl.ds(l, 1), :] for l in range(L)]
    m = rows[0]
    for r in rows[1:]:
        m = jnp.maximum(m, r)
    e = [jnp.exp(r - m) for r in rows]
    tot = e[0]
    for t in e[1:]:
        tot = tot + t
    return [t / tot for t in e]


def _lb_fwd(lbp):
    L, D = lbp.shape

    def body(p_ref, o_ref):
        sm = _softmax_rows(p_ref, L)
        acc = jnp.zeros((1, D), F32)
        o_ref[pl.ds(0, 1), :] = acc
        for l in range(1, L):
            acc = acc + sm[l]
            o_ref[pl.ds(l, 1), :] = acc

    return pl.pallas_call(body, out_shape=jax.ShapeDtypeStruct((L, D), F32), name="lb_fwd")(lbp)


def _lb_bwd(lbp, dlb):
    L, D = lbp.shape

    def body(p_ref, d_ref, o_ref):
        sm = _softmax_rows(p_ref, L)
        dsm = [jnp.zeros((1, D), F32)]
        for i in range(1, L):
            t = jnp.zeros((1, D), F32)
            for l in range(i, L):
                t = t + d_ref[pl.ds(l, 1), :]
            dsm.append(t)
        dot = jnp.zeros((1, D), F32)
        for i in range(L):
            dot = dot + dsm[i] * sm[i]
        for i in range(L):
            o_ref[pl.ds(i, 1), :] = sm[i] * (dsm[i] - dot)

    return pl.pallas_call(body, out_shape=jax.ShapeDtypeStruct((L, D), F32), name="lb_bwd")(lbp, dlb)


def _my_pos():
    return lax.axis_index("x"), lax.axis_index("y"), lax.axis_index("c")


def _peer(mask):
    x, y, c = _my_pos()
    mx, my, mc = (mask >> 2) & 1, (mask >> 1) & 1, mask & 1
    px = (1 - x) if mx else x
    py = (1 - y) if my else y
    pc = (1 - c) if mc else c
    return (px, py, pc), 4 * px + 2 * py + pc


def _all_gather(shards):
    n = len(shards)

    def body(*refs):
        ins, outs = refs[:n], refs[n:2 * n]
        send_sems, recv_sems, local_sems = refs[2 * n:]
        x, y, c = _my_pos()
        me = 4 * x + 2 * y + c
        local = [pltpu.make_async_copy(ins[a], outs[a].at[:, me], local_sems.at[a]) for a in range(n)]
        for cp in local:
            cp.start()
        sends = []
        for m in range(1, N_DEV):
            peer, _ = _peer(m)
            for a in range(n):
                cp = pltpu.make_async_remote_copy(
                    src_ref=ins[a], dst_ref=outs[a].at[:, me], send_sem=send_sems.at[a, m - 1],
                    recv_sem=recv_sems.at[a, m - 1], device_id=peer, device_id_type=MESH)
                cp.start()
                sends.append(cp)
        for m in range(1, N_DEV):
            peer, pid = _peer(m)
            for a in range(n):
                pltpu.make_async_remote_copy(
                    src_ref=ins[a], dst_ref=outs[a].at[:, pid], send_sem=send_sems.at[a, m - 1],
                    recv_sem=recv_sems.at[a, m - 1], device_id=peer, device_id_type=MESH).wait_recv()
        for cp in sends:
            cp.wait_send()
        for cp in local:
            cp.wait()

    out_shape = [jax.ShapeDtypeStruct((s.shape[0], N_DEV) + s.shape[1:], s.dtype) for s in shards]
    return pl.pallas_call(
        body, in_specs=[ANY] * n, out_specs=[ANY] * n, out_shape=out_shape,
        scratch_shapes=[pltpu.SemaphoreType.DMA((n, N_DEV - 1)), pltpu.SemaphoreType.DMA((n, N_DEV - 1)),
                        pltpu.SemaphoreType.DMA((n,))],
        name="all_gather_weights")(*shards)


def _exchange(grads, groups):
    n = len(grads)
    ng = 1 + max(g for g, _ in groups)
    layers = [1 + max(l for g, l in groups if g == gi) for gi in range(ng)]
    shapes = [None] * ng
    for a, (g, l) in enumerate(groups):
        shapes[g] = grads[a].shape[1:]

    def body(*refs):
        ins, outs = refs[:n], refs[n:n + ng]
        send_sems, recv_sems, local_sems = refs[n + ng:]
        x, y, c = _my_pos()
        me = 4 * x + 2 * y + c
        local = []
        for a, (g, l) in enumerate(groups):
            cp = pltpu.make_async_copy(ins[a].at[me], outs[g].at[me, l], local_sems.at[a])
            cp.start()
            local.append(cp)
        sends = []
        for m in range(1, N_DEV):
            peer, pid = _peer(m)
            for a, (g, l) in enumerate(groups):
                cp = pltpu.make_async_remote_copy(
                    src_ref=ins[a].at[pid], dst_ref=outs[g].at[me, l], send_sem=send_sems.at[a, m - 1],
                    recv_sem=recv_sems.at[a, m - 1], device_id=peer, device_id_type=MESH)
                cp.start()
                sends.append(cp)
        for m in range(1, N_DEV):
            peer, pid = _peer(m)
            for a, (g, l) in enumerate(groups):
                pltpu.make_async_remote_copy(
                    src_ref=ins[a].at[pid], dst_ref=outs[g].at[pid, l], send_sem=send_sems.at[a, m - 1],
                    recv_sem=recv_sems.at[a, m - 1], device_id=peer, device_id_type=MESH).wait_recv()
        for cp in sends:
            cp.wait_send()
        for cp in local:
            cp.wait()

    out_shape = [jax.ShapeDtypeStruct((N_DEV, layers[g]) + shapes[g], grads[[gg for gg, _ in groups].index(g)].dtype)
                 for g in range(ng)]
    return pl.pallas_call(
        body, in_specs=[ANY] * n, out_specs=[ANY] * ng, out_shape=out_shape,
        scratch_shapes=[pltpu.SemaphoreType.DMA((n, N_DEV - 1)), pltpu.SemaphoreType.DMA((n, N_DEV - 1)),
                        pltpu.SemaphoreType.DMA((n,))],
        name="exchange_grads")(*grads)


HBM_SPEC = pl.BlockSpec(memory_space=pltpu.HBM)
SEM_SPEC = pl.BlockSpec(memory_space=pltpu.SEMAPHORE)
EFFECT = pltpu.SideEffectType.DATAFLOW_SIDE_EFFECTING


def _hbm(a):
    return pltpu.with_memory_space_constraint(a, pltpu.HBM)


def _landing(own, dev):
    zone = lax.empty((N_DEV,) + own.shape, own.dtype)
    return lax.dynamic_update_slice(zone, own[None], (dev,) + (0,) * own.ndim)


def _push_start(name, srcs, lands, whole, groups):
    n = len(srcs)
    ng = 1 + max(groups)
    cnt = [groups.count(g) for g in range(ng)]
    idx = [groups[:a].count(groups[a]) for a in range(n)]

    def body(*refs):
        src_refs, land_refs = refs[:n], refs[n:2 * n]
        sems = refs[2 * n:2 * n + 2 * ng]
        token = refs[-1]
        x, y, c = _my_pos()
        me = 4 * x + 2 * y + c
        for a in range(n):
            g = groups[a]
            for m in range(1, N_DEV):
                peer, pid = _peer(m)
                pltpu.make_async_remote_copy(
                    src_ref=src_refs[a] if whole else src_refs[a].at[pid], dst_ref=land_refs[a].at[me],
                    send_sem=sems[2 * g].at[idx[a] * (N_DEV - 1) + m - 1],
                    recv_sem=sems[2 * g + 1].at[idx[a] * (N_DEV - 1) + m - 1],
                    device_id=peer, device_id_type=MESH).start()
        token[...] = jnp.zeros_like(token)

    sem_shapes = []
    for g in range(ng):
        sem_shapes += [pltpu.SemaphoreType.DMA((cnt[g] * (N_DEV - 1),))] * 2
    thru = [pltpu.HBM(s.shape, s.dtype) for s in list(srcs) + list(lands)]
    res = pl.pallas_call(
        body, name=name,
        out_shape=tuple(sem_shapes + thru + [jax.ShapeDtypeStruct((8, 128), F32)]),
        in_specs=tuple([HBM_SPEC] * (2 * n)),
        out_specs=tuple([SEM_SPEC] * (2 * ng) + [HBM_SPEC] * (2 * n) + [pl.BlockSpec(memory_space=pltpu.VMEM)]),
        input_output_aliases={i: 2 * ng + i for i in range(2 * n)},
        compiler_params=pltpu.CompilerParams(has_side_effects=EFFECT),
    )(*[_hbm(s) for s in srcs], *[_hbm(z) for z in lands])
    sems = [(res[2 * g], res[2 * g + 1]) for g in range(ng)]
    srcs_thru = list(res[2 * ng:2 * ng + n])
    lands_thru = list(res[2 * ng + n:2 * ng + 2 * n])
    return sems, srcs_thru, lands_thru, res[-1]


def _push_wait(name, srcs_thru, lands_thru, sems, after, whole):
    n = len(srcs_thru)

    def body(*refs):
        src_refs, land_refs = refs[:n], refs[n:2 * n]
        send_sems, recv_sems = refs[2 * n], refs[2 * n + 1]
        for a in range(n):
            for m in range(1, N_DEV):
                peer, pid = _peer(m)
                cp = pltpu.make_async_remote_copy(
                    src_ref=src_refs[a] if whole else src_refs[a].at[pid], dst_ref=land_refs[a].at[pid],
                    send_sem=send_sems.at[a * (N_DEV - 1) + m - 1], recv_sem=recv_sems.at[a * (N_DEV - 1) + m - 1],
                    device_id=peer, device_id_type=MESH)
                cp.wait_send()
                cp.wait_recv()

    thru = [pltpu.HBM(s.shape, s.dtype) for s in list(srcs_thru) + list(lands_thru)]
    res = pl.pallas_call(
        body, name=name, out_shape=tuple(thru),
        in_specs=tuple([HBM_SPEC] * (2 * n) + [SEM_SPEC, SEM_SPEC, ANY]),
        out_specs=tuple([HBM_SPEC] * (2 * n)),
        input_output_aliases={i: i for i in range(2 * n)},
        compiler_params=pltpu.CompilerParams(has_side_effects=EFFECT),
    )(*srcs_thru, *lands_thru, sems[0], sems[1], after)
    return list(res[n:])


def _adamw(recv, w, m, v, layer=0, prev=None):
    L, R, C = w.shape
    tr = _tile(R, max(8, (1 << 18) // C), 8) if R % 8 == 0 else R
    bc1 = 1.0 - ADAM_B1 ** ADAM_STEP
    bc2 = 1.0 - ADAM_B2 ** ADAM_STEP
    if prev is None:
        prev = [lax.empty((L, R, C), F32) for _ in range(4)]

    def body(r_ref, w_ref, m_ref, v_ref, p0, p1, p2, p3, g_ref, d_ref, nm_ref, nv_ref):
        g = r_ref[0].astype(F32)
        for s in range(1, N_DEV):
            g = g + r_ref[s].astype(F32)
        nm = ADAM_B1 * m_ref[...] + (1.0 - ADAM_B1) * g
        nv = ADAM_B2 * v_ref[...] + (1.0 - ADAM_B2) * (g * g)
        mh = nm / bc1
        vh = nv / bc2
        g_ref[...] = g
        d_ref[...] = -ADAM_LR * (mh / (jnp.sqrt(vh) + ADAM_EPS) + ADAM_WD * w_ref[...])
        nm_ref[...] = nm
        nv_ref[...] = nv

    row = pl.BlockSpec((None, tr, C), lambda i: (layer, i, 0))
    return pl.pallas_call(
        body, grid=(R // tr,),
        in_specs=[pl.BlockSpec((N_DEV, tr, C), lambda i: (0, i, 0)), row, row, row] + [ANY] * 4,
        out_specs=[row] * 4, out_shape=[jax.ShapeDtypeStruct((L, R, C), F32)] * 4,
        input_output_aliases={4: 0, 5: 1, 6: 2, 7: 3},
        compiler_params=_cp("parallel"), name="adamw")(recv, w, m, v, *prev)


def _full_w_spec(tk, tn):
    return pl.BlockSpec((tk, tn), lambda i, j, k: (k, j))


def _colblk_w_spec(n):
    def spec(tk, tn):
        per = n // tn
        return pl.BlockSpec((None, tk, tn), lambda i, j, k: (j // per, k, j % per))
    return spec


def kernel(x, meta_tokens, mix_norm_g, mlp_norm_g, final_norm_g, ev_w_in, ev_conv_w, ev_conv_b, ev_ln_g, ev_ln_b, ev_pool_w, ev_pool_b, ev_pool_scale, ev_w_out, od_w_in, od_gnorm_g, od_w_out, lb_param, mlp_w1, mlp_w2, loss_target, m_meta_tokens, m_mix_norm_g, m_mlp_norm_g, m_final_norm_g, m_ev_w_in, m_ev_conv_w, m_ev_conv_b, m_ev_ln_g, m_ev_ln_b, m_ev_pool_w, m_ev_pool_b, m_ev_pool_scale, m_ev_w_out, m_od_w_in, m_od_gnorm_g, m_od_w_out, m_lb_param, m_mlp_w1, m_mlp_w2, v_meta_tokens, v_mix_norm_g, v_mlp_norm_g, v_final_norm_g, v_ev_w_in, v_ev_conv_w, v_ev_conv_b, v_ev_ln_g, v_ev_ln_b, v_ev_pool_w, v_ev_pool_b, v_ev_pool_scale, v_ev_w_out, v_od_w_in, v_od_gnorm_g, v_od_w_out, v_lb_param, v_mlp_w1, v_mlp_w2):
    S, D = x.shape[1], x.shape[2]
    T = PAD + N_META + S
    DEPTH = mix_norm_g.shape[0]
    DFF = mlp_w1.shape[2] * N_DEV
    dev = 4 * lax.axis_index("x") + 2 * lax.axis_index("y") + lax.axis_index("c")

    g_meta, g_cw = _all_gather([meta_tokens[None], ev_conv_w])
    n_ev = ev_w_in.shape[0]
    n_od = od_w_in.shape[0]
    meta_full = jnp.transpose(g_meta[0], (1, 0, 2)).reshape(N_META, D)
    cw_full = jnp.transpose(g_cw, (0, 2, 1, 3)).reshape(n_ev, CONV_WIDTH, -1)
    cw_pad = jnp.pad(cw_full, ((0, 0), (0, 32 - CONV_WIDTH), (0, 0)))
    n_in_od = od_w_in.shape[2]
    n_w1 = mlp_w1.shape[2]

    ag_src, ag_grp, ag_at = [], [], {}
    for layer in range(DEPTH):
        j = layer // 2
        mixer = [("in", ev_w_in[j]), ("out", ev_w_out[j])] if layer % 2 == 0 else [("in", od_w_in[j]), ("out", od_w_out[j])]
        for pos, (key, arr) in enumerate(mixer + [("w1", mlp_w1[layer]), ("w2", mlp_w2[layer])]):
            ag_at[layer, key] = len(ag_src)
            ag_src.append(arr.astype(BF16))
            ag_grp.append((pos // 2) if layer == 0 else layer + 1)
    ag_sems, ag_s, ag_l, ag_tok = _push_start("ag_start", ag_src, [_landing(s_, dev) for s_ in ag_src], True, ag_grp)

    def ag_wait(group, after):
        ids = [a for a in range(len(ag_src)) if ag_grp[a] == group]
        got = _push_wait(f"ag_wait_{group}", [ag_s[a] for a in ids], [ag_l[a] for a in ids], ag_sems[group], after, True)
        return dict(zip(ids, got))

    h = jnp.concatenate([jnp.zeros((PAD, D), F32), meta_full, x[0]], axis=0) + ag_tok[0, 0]
    tgt = jnp.pad(loss_target[0], ((PAD + N_META, 0), (0, 0)))
    lb_all = _lb_fwd(lb_param)

    tm_big = _tile(T, MM_ROWS_BIG, 16)
    tm_mid = _tile(T, MM_ROWS_MID, 16)

    saved = []
    for layer in range(DEPTH):
        j = layer // 2
        sv = {"h0": h}
        got = ag_wait(0 if layer == 0 else layer + 1, h)
        g_in, g_out = got[ag_at[layer, "in"]], got[ag_at[layer, "out"]]
        w_out = g_out.reshape(-1, D)
        n = _rms_fwd(h, mix_norm_g[layer][None])
        sv["n"] = n
        if layer % 2 == 0:
            w_in = jnp.transpose(g_in, (1, 0, 2)).reshape(D, -1)
            u = _mm_nn("ev_in", n, w_in, _full_w_spec, T, w_in.shape[1], D, tm_big, 512, D, "f32")
            yab, yc = _ev_fwd(u, cw_pad[j], ev_conv_b[j][None], ev_ln_g[j][None], ev_ln_b[j][None],
                              ev_pool_w[j].astype(BF16), ev_pool_b[j].reshape(1, -1), ev_pool_scale[j][None])
            sv.update(u=u, y=yab, yc=yc)
            h = _mm_nn("ev_out", yab, w_out, _full_w_spec, T, D, D, tm_mid, D, D, "resid", extra=h)
        else:
            w_in = g_in
            u = _mm_nn("od_in", n, w_in, _colblk_w_spec(n_in_od), T, N_DEV * n_in_od, D, tm_big, 512, D, "f32")
            y, o, s0 = _hgrn_fwd(u, lb_all[layer][None], od_gnorm_g[j][None])
            sv.update(u=u, y=y, o=o, s0=s0)
            h = _mm_nn("od_out", y, w_out, _full_w_spec, T, D, D, tm_mid, D, D, "resid", extra=h)
        sv["h1"] = h
        if layer == 0:
            got = ag_wait(1, h)
        g_w1 = got[ag_at[layer, "w1"]]
        w_w2 = got[ag_at[layer, "w2"]].reshape(DFF, D)
        sv.update(w_in=w_in, w_out=w_out, g_w1=g_w1, w_w2=w_w2)
        n2 = _rms_fwd(h, mlp_norm_g[layer][None])
        r, act = _mm_nn("mlp_w1", n2, g_w1, _colblk_w_spec(n_w1), T, DFF, D, tm_big, 512, D, "relu2")
        sv.update(n2=n2, r=r, act=act)
        h = _mm_nn("mlp_w2", act, w_w2, _full_w_spec, T, D, DFF, tm_mid, D, 1024, "resid", extra=h)
        saved.append(sv)

    loss_blk, dh, dhb, dg_final = _loss_head(h, final_norm_g[None], tgt)
    loss = lax.psum(loss_blk[0, 0], AXES)

    tt = _tile(T, MM_ROWS_BIG, 16)
    g_mix, g_mlp = [None] * DEPTH, [None] * DEPTH
    small ={"cw": [None] * n_ev, "vec": [None] * n_ev, "pw": [None] * n_ev, "gn": [None] * n_od}
    dlb_rows = [jnp.zeros((1, D), F32) for _ in range(DEPTH)]

    def xs2(tt_, tk):
        return pl.BlockSpec((tt_, tk), lambda a, b, t: (t, a))

    def ys2(tt_, tn):
        return pl.BlockSpec((tt_, tn), lambda a, b, t: (t, b))

    def os2(tk, tn):
        return pl.BlockSpec((tk, tn), lambda a, b, t: (a, b))

    def os3(tk, tn):
        return pl.BlockSpec((None, tk, tn), lambda a, b, t: (b, a, 0))

    def dy2(tm, tn):
        return pl.BlockSpec((tm, tn), lambda i, jj, k: (i, k))

    def w_rows(tj, tn):
        return pl.BlockSpec((tj, tn), lambda i, jj, k: (jj, k))

    def w_colblk(tj, tn):
        return pl.BlockSpec((None, tj, tn), lambda i, jj, k: (k, jj, 0))

    rs_pending = []

    def rs_start(tag, mats):
        blocks = [m_ if m_.ndim == 3 else m_.reshape(N_DEV, m_.shape[0] // N_DEV, m_.shape[1]) for m_ in mats]
        lands = [_landing(lax.dynamic_index_in_dim(b_, dev, 0, keepdims=False), dev) for b_ in blocks]
        sems, s_thru, l_thru, tok = _push_start(f"rs_start_{tag}", blocks, lands, False, [0] * len(blocks))
        rs_pending.append((tag, s_thru, l_thru, sems[0]))
        return tok[0, 0]

    for layer in reversed(range(DEPTH)):
        j = layer // 2
        sv = saved[layer]
        da1 = _mm_nt("mlp_w2_t", dhb, sv["w_w2"], dy2, w_rows, T, DFF, D, tm_mid, 1024, D, "dact", extra=sv["r"])
        dw2 = _mm_tn("mlp_dw2", sv["act"], dhb, xs2, ys2, os2, (DFF, D), T, DFF, D, tt, 1024, D)
        dw1 = _mm_tn("mlp_dw1", sv["n2"], da1, xs2, ys2, os3, (N_DEV, D, n_w1), T, D, DFF, tt, D, n_w1)
        tok = rs_start(f"mlp{layer}", [dw1, dw2])
        dn2 = _mm_nt("mlp_w1_t", da1, sv["g_w1"], dy2, w_colblk, T, D, DFF, tm_mid, D, n_w1, "f32")
        dh, dhb, g_mlp[layer] = _rms_bwd(dn2, sv["h1"], mlp_norm_g[layer][None] + tok, dh)
        if layer % 2 == 0:
            dyab = _mm_nt("ev_out_t", dhb, sv["w_out"], dy2, w_rows, T, D, D, tm_mid, D, D, "f32")
            dwout = _mm_tn("ev_dwout", sv["y"], dhb, xs2, ys2, os2, (D, D), T, D, D, tt, D, D)
            du, small["cw"][j], small["vec"][j], small["pw"][j] = _ev_bwd(
                dyab, sv["yc"], sv["u"], cw_pad[j], ev_ln_g[j][None], ev_ln_b[j][None], ev_pool_w[j].astype(BF16),
                jnp.transpose(ev_pool_w[j], (0, 2, 1)).astype(BF16), ev_pool_b[j].reshape(1, -1),
                ev_pool_scale[j][None])
            nin = du.shape[1]
            dwin = _mm_tn("ev_dwin", sv["n"], du, xs2, ys2, os2, (D, nin), T, D, nin, tt, D, 512)
            dwin = jnp.transpose(dwin.reshape(D, N_DEV, nin // N_DEV), (1, 0, 2))
            dn = _mm_nt("ev_in_t", du, sv["w_in"], dy2, w_rows, T, D, nin, tm_mid, D, 512, "f32")
        else:
            dy = _mm_nt("od_out_t", dhb, sv["w_out"], dy2, w_rows, T, D, D, tm_mid, D, D, "f32")
            dwout = _mm_tn("od_dwout", sv["y"], dhb, xs2, ys2, os2, (D, D), T, D, D, tt, D, D)
            du3, dlb_rows[layer], small["gn"][j] = _hgrn_bwd(dy, sv["o"], sv["s0"], sv["u"], lb_all[layer][None],
                                                              od_gnorm_g[j][None])
            per = D // n_in_od

            def du_t(tt_, tn):
                return pl.BlockSpec((None, tt_, tn), lambda a, b, t: (b // per, t, b % per))

            dwin = _mm_tn("od_dwin", sv["n"], du3, xs2, du_t, os3, (N_DEV, D, n_in_od), T, D, 4 * D, tt, D, n_in_od)
            dn = _mm_nt("od_in_t", du3, sv["w_in"],
                        lambda tm, tn: pl.BlockSpec((None, tm, tn), lambda i, jj, k: (k // per, i, k % per)),
                        w_colblk, T, D, 4 * D, tm_mid, D, n_in_od, "f32")
        tok = rs_start(f"mix{layer}", [dwin, dwout])
        dh, dhb, g_mix[layer] = _rms_bwd(dn, sv["h0"], mix_norm_g[layer][None] + tok, dh)

    dmeta = dh[PAD:PAD + N_META]
    grad_x = dh[PAD + N_META:][None]
    dlb_param = _lb_bwd(lb_param, jnp.concatenate(dlb_rows, axis=0))

    pieces = [
        ("meta", dmeta), ("mix", jnp.concatenate(g_mix, 0)), ("mlp", jnp.concatenate(g_mlp, 0)), ("final", dg_final),
        ("cw", jnp.stack([c[:CONV_WIDTH] for c in small["cw"]])), ("cb", jnp.stack([v_[0] for v_ in small["vec"]])),
        ("lng", jnp.stack([v_[1] for v_ in small["vec"]])), ("lnb", jnp.stack([v_[2] for v_ in small["vec"]])),
        ("pw", jnp.stack(small["pw"])), ("pb", jnp.stack([v_[4] for v_ in small["vec"]])),
        ("ps", jnp.stack([v_[3] for v_ in small["vec"]])), ("gn", jnp.stack([jnp.sum(g_, axis=0)[0] for g_ in small["gn"]])),
        ("lb", dlb_param),
    ]
    flat = jnp.concatenate([p.reshape(-1) for _, p in pieces])
    n_small = flat.shape[0]
    rows_small = -(-n_small // 1024 // 8) * 8
    flat = jnp.pad(flat, (0, rows_small * 1024 - n_small)).reshape(rows_small, 1024)

    recv_small = _exchange([jnp.broadcast_to(flat[None], (N_DEV,) + flat.shape)], [(0, 0)])[0]
    recv = {}
    for tag, s_thru, l_thru, sems in rs_pending:
        got = _push_wait(f"rs_wait_{tag}", s_thru, l_thru, sems, recv_small, False)
        layer = int(tag[3:])
        if tag.startswith("mlp"):
            recv["w1", layer], recv["w2", layer] = got
        else:
            key = "ev" if layer % 2 == 0 else "od"
            recv[key + "_in", layer // 2], recv[key + "_out", layer // 2] = got

    outs = {}
    big = {"ev_in": ("ev_w_in", ev_w_in, m_ev_w_in, v_ev_w_in), "ev_out": ("ev_w_out", ev_w_out, m_ev_w_out, v_ev_w_out),
           "od_in": ("od_w_in", od_w_in, m_od_w_in, v_od_w_in), "od_out": ("od_w_out", od_w_out, m_od_w_out, v_od_w_out),
           "w1": ("mlp_w1", mlp_w1, m_mlp_w1, v_mlp_w1), "w2": ("mlp_w2", mlp_w2, m_mlp_w2, v_mlp_w2)}
    for key, (name, w, m, v) in big.items():
        res = None
        for l in range(w.shape[0]):
            res = _adamw(recv[key, l], w, m, v, layer=l, prev=res)
        outs[name] = res

    small_params = {
        "meta": ("meta_tokens", None), "mix": ("mix_norm_g", mix_norm_g, m_mix_norm_g, v_mix_norm_g),
        "mlp": ("mlp_norm_g", mlp_norm_g, m_mlp_norm_g, v_mlp_norm_g),
        "final": ("final_norm_g", final_norm_g, m_final_norm_g, v_final_norm_g),
        "cw": ("ev_conv_w", None), "cb": ("ev_conv_b", ev_conv_b, m_ev_conv_b, v_ev_conv_b),
        "lng": ("ev_ln_g", ev_ln_g, m_ev_ln_g, v_ev_ln_g), "lnb": ("ev_ln_b", ev_ln_b, m_ev_ln_b, v_ev_ln_b),
        "pw": ("ev_pool_w", ev_pool_w, m_ev_pool_w, v_ev_pool_w), "pb": ("ev_pool_b", ev_pool_b, m_ev_pool_b, v_ev_pool_b),
        "ps": ("ev_pool_scale", ev_pool_scale, m_ev_pool_scale, v_ev_pool_scale),
        "gn": ("od_gnorm_g", od_gnorm_g, m_od_gnorm_g, v_od_gnorm_g), "lb": ("lb_param", lb_param, m_lb_param, v_lb_param),
    }
    csh = ev_conv_w.shape[2]
    msh = meta_tokens.shape[1]

    def packed(which):
        parts = []
        for key, g_ in pieces:
            ent = small_params[key]
            if key == "meta":
                src = (meta_tokens, m_meta_tokens, v_meta_tokens)[which]
                full = lax.dynamic_update_slice(jnp.zeros((N_META, D), F32), src, (0, dev * msh))
            elif key == "cw":
                src = (ev_conv_w, m_ev_conv_w, v_ev_conv_w)[which]
                full = lax.dynamic_update_slice(jnp.zeros(g_.shape, F32), src, (0, 0, dev * csh))
            else:
                full = ent[1 + which]
            parts.append(full.reshape(-1))
        f = jnp.concatenate(parts)
        return jnp.pad(f, (0, rows_small * 1024 - n_small)).reshape(rows_small, 1024)

    sres = [r_[0] for r_ in _adamw(recv_small[:, 0], packed(0)[None], packed(1)[None], packed(2)[None])]
    off = 0
    for key, g_ in pieces:
        size = g_.size
        vals = [r_.reshape(-1)[off:off + size].reshape(g_.shape) for r_ in sres]
        off += size
        name = small_params[key][0]
        if key == "meta":
            vals = [lax.dynamic_slice(v_, (0, dev * msh), (N_META, msh)) for v_ in vals]
        elif key == "cw":
            vals = [lax.dynamic_slice(v_, (0, 0, dev * csh), v_.shape[:2] + (csh,)) for v_ in vals]
        else:
            vals = [v_.reshape(small_params[key][1].shape) for v_ in vals]
        outs[name] = vals

    names = ["meta_tokens", "mix_norm_g", "mlp_norm_g", "final_norm_g", "ev_w_in", "ev_conv_w", "ev_conv_b", "ev_ln_g",
             "ev_ln_b", "ev_pool_w", "ev_pool_b", "ev_pool_scale", "ev_w_out", "od_w_in", "od_gnorm_g", "od_w_out",
             "lb_param", "mlp_w1", "mlp_w2"]
    result = [loss, grad_x]
    for k in range(4):
        result += [outs[nm][k] for nm in names]
    return tuple(result)
```

```python
import functools

import jax
import jax.numpy as jnp
from jax import lax
from jax.experimental import pallas as pl
from jax.experimental.pallas import tpu as pltpu

F32 = jnp.float32
BF16 = jnp.bfloat16

N_DEV = 8
N_META = 16
CHUNK = 64
PAD = CHUNK - N_META
SUB = 16
HEAD = 128
CONV_WIDTH = 31
HALO = 32
POOL_WINDOWS = (2, 4, 8, 16)
EPS = 1e-6
NEG = -1e30
ADAM_LR, ADAM_B1, ADAM_B2, ADAM_EPS, ADAM_WD, ADAM_STEP = 0.001, 0.9, 0.999, 1e-08, 0.01, 10
VMEM_LIMIT = 56 * 1024 * 1024
EV_ROWS = 416
HGRN_ROWS = 832
MM_ROWS_BIG = 2080
MM_ROWS_MID = 1040
MESH = pl.DeviceIdType.MESH
AXES = ("x", "y", "c")
ANY = pl.BlockSpec(memory_space=pl.ANY)


def _cp(*sem):
    return pltpu.CompilerParams(dimension_semantics=sem, vmem_limit_bytes=VMEM_LIMIT)


def _tile(n, cap, mult):
    best = None
    for d in range(mult, min(n, cap) + 1, mult):
        if n % d == 0:
            best = d
    assert best is not None, (n, cap, mult)
    return best


def _nt(a, b):
    return lax.dot_general(a, b, (((1,), (1,)), ((), ())), preferred_element_type=F32)


def _tn(a, b):
    return lax.dot_general(a, b, (((0,), (0,)), ((), ())), preferred_element_type=F32)


def _nn(a, b):
    return jnp.dot(a, b, preferred_element_type=F32)


def _row_ids(base, n):
    return base + lax.broadcasted_iota(jnp.int32, (n, 1), 0)


def _dsilu(x, s):
    return s * (1.0 + x * (1.0 - s))


def _rms_fwd(h, g):
    T, D = h.shape
    tm = _tile(T, MM_ROWS_MID, 16)

    def body(h_ref, g_ref, n_ref):
        x = h_ref[...]
        r = lax.rsqrt(jnp.mean(x * x, axis=-1, keepdims=True) + EPS)
        n_ref[...] = ((x * r) * g_ref[...]).astype(BF16)

    return pl.pallas_call(
        body, grid=(T // tm,),
        in_specs=[pl.BlockSpec((tm, D), lambda i: (i, 0)), pl.BlockSpec((1, D), lambda i: (0, 0))],
        out_specs=pl.BlockSpec((tm, D), lambda i: (i, 0)),
        out_shape=jax.ShapeDtypeStruct((T, D), BF16),
        compiler_params=_cp("parallel"), name="rms_fwd")(h, g)


def _rms_bwd(dn, h, g, dres):
    T, D = h.shape
    tm = _tile(T, MM_ROWS_MID, 16)

    def body(dn_ref, h_ref, g_ref, dres_ref, dh_ref, dhb_ref, dg_ref):
        i = pl.program_id(0)
        x = h_ref[...]
        dn_v = dn_ref[...]
        r = lax.rsqrt(jnp.mean(x * x, axis=-1, keepdims=True) + EPS)
        xh = x * r
        dxh = dn_v * g_ref[...]
        dx = r * (dxh - xh * jnp.mean(dxh * xh, axis=-1, keepdims=True))
        keep = _row_ids(i * tm, tm) >= PAD
        dh = jnp.where(keep, dres_ref[...] + dx, 0.0)
        dh_ref[...] = dh
        dhb_ref[...] = dh.astype(BF16)

        @pl.when(i == 0)
        def _():
            dg_ref[...] = jnp.zeros_like(dg_ref)

        dg_ref[...] += jnp.sum(dn_v * xh, axis=0, keepdims=True)

    row = pl.BlockSpec((tm, D), lambda i: (i, 0))
    vec = pl.BlockSpec((1, D), lambda i: (0, 0))
    return pl.pallas_call(
        body, grid=(T // tm,),
        in_specs=[row, row, vec, row], out_specs=[row, row, vec],
        out_shape=[jax.ShapeDtypeStruct((T, D), F32), jax.ShapeDtypeStruct((T, D), BF16),
                   jax.ShapeDtypeStruct((1, D), F32)],
        compiler_params=_cp("arbitrary"), name="rms_bwd")(dn, h, g, dres)


def _loss_head(h, g, tgt):
    T, D = h.shape
    tm = _tile(T, MM_ROWS_MID, 16)
    first_x = PAD + N_META

    def body(h_ref, g_ref, t_ref, loss_ref, dh_ref, dhb_ref, dg_ref):
        i = pl.program_id(0)
        x = h_ref[...]
        r = lax.rsqrt(jnp.mean(x * x, axis=-1, keepdims=True) + EPS)
        xh = x * r
        gv = g_ref[...]
        out = xh * gv
        valid = _row_ids(i * tm, tm) >= first_x
        e = jnp.where(valid, out - t_ref[...], 0.0)
        dout = e * (1.0 / D)
        dxh = dout * gv
        dx = r * (dxh - xh * jnp.mean(dxh * xh, axis=-1, keepdims=True))
        dh_ref[...] = dx
        dhb_ref[...] = dx.astype(BF16)

        @pl.when(i == 0)
        def _():
            dg_ref[...] = jnp.zeros_like(dg_ref)
            loss_ref[...] = jnp.zeros_like(loss_ref)

        dg_ref[...] += jnp.sum(dout * xh, axis=0, keepdims=True)
        loss_ref[...] += 0.5 * jnp.sum(jnp.mean(e * e, axis=-1, keepdims=True))

    row = pl.BlockSpec((tm, D), lambda i: (i, 0))
    vec = pl.BlockSpec((1, D), lambda i: (0, 0))
    return pl.pallas_call(
        body, grid=(T // tm,),
        in_specs=[row, vec, row],
        out_specs=[pl.BlockSpec((8, 128), lambda i: (0, 0)), row, row, vec],
        out_shape=[jax.ShapeDtypeStruct((8, 128), F32), jax.ShapeDtypeStruct((T, D), F32),
                   jax.ShapeDtypeStruct((T, D), BF16), jax.ShapeDtypeStruct((1, D), F32)],
        compiler_params=_cp("arbitrary"), name="loss_head")(h, g, tgt)


def _mm_nn(name, a, w, w_spec, M, N, K, tm, tn, tk, mode, extra=None, a_spec=None):
    nk = K // tk
    if a_spec is None:
        a_spec = pl.BlockSpec((tm, tk), lambda i, j, k: (i, k))
    o_spec = pl.BlockSpec((tm, tn), lambda i, j, k: (i, j))

    def body(*refs):
        if mode == "resid":
            a_ref, w_ref, e_ref = refs[:3]
            outs = refs[3:]
        else:
            a_ref, w_ref = refs[:2]
            outs = refs[2:]
        acc_ref = outs[-1] if nk > 1 else None
        part = _nn(a_ref[...], w_ref[...])

        def finish(acc):
            if mode == "f32":
                outs[0][...] = acc
            elif mode == "relu2":
                r = jnp.maximum(acc, 0.0)
                outs[0][...] = r.astype(BF16)
                outs[1][...] = (r * r).astype(BF16)
            else:
                keep = _row_ids(pl.program_id(0) * tm, tm) >= PAD
                outs[0][...] = jnp.where(keep, e_ref[...] + acc, 0.0)

        if nk == 1:
            finish(part)
        else:
            k = pl.program_id(2)

            @pl.when(k == 0)
            def _():
                acc_ref[...] = part

            @pl.when(k > 0)
            def _():
                acc_ref[...] += part

            @pl.when(k == nk - 1)
            def _():
                finish(acc_ref[...])

    in_specs = [a_spec, w_spec(tk, tn)]
    args = [a, w]
    if mode == "resid":
        in_specs.append(o_spec)
        args.append(extra)
    if mode == "relu2":
        out_specs = [o_spec, o_spec]
        out_shape = [jax.ShapeDtypeStruct((M, N), BF16)] * 2
    else:
        out_specs = [o_spec]
        out_shape = [jax.ShapeDtypeStruct((M, N), F32)]
    scratch = [pltpu.VMEM((tm, tn), F32)] if nk > 1 else []
    res = pl.pallas_call(
        body, grid=(M // tm, N // tn, nk), in_specs=in_specs, out_specs=out_specs, out_shape=out_shape,
        scratch_shapes=scratch, compiler_params=_cp("parallel", "parallel", "arbitrary"), name=name)(*args)
    return res if mode == "relu2" else res[0]


def _mm_nt(name, dy, w, dy_spec, w_spec, M, J, N, tm, tj, tn, mode, extra=None):
    nk = N // tn
    o_spec = pl.BlockSpec((tm, tj), lambda i, j, k: (i, j))

    def body(*refs):
        if mode == "dact":
            dy_ref, w_ref, e_ref = refs[:3]
            outs = refs[3:]
        else:
            dy_ref, w_ref = refs[:2]
            outs = refs[2:]
        acc_ref = outs[-1] if nk > 1 else None
        part = _nt(dy_ref[...], w_ref[...])

        def finish(acc):
            if mode == "f32":
                outs[0][...] = acc
            else:
                outs[0][...] = (acc * (2.0 * e_ref[...].astype(F32))).astype(BF16)

        if nk == 1:
            finish(part)
        else:
            k = pl.program_id(2)

            @pl.when(k == 0)
            def _():
                acc_ref[...] = part

            @pl.when(k > 0)
            def _():
                acc_ref[...] += part

            @pl.when(k == nk - 1)
            def _():
                finish(acc_ref[...])

    in_specs = [dy_spec(tm, tn), w_spec(tj, tn)]
    args = [dy, w]
    if mode == "dact":
        in_specs.append(o_spec)
        args.append(extra)
    scratch = [pltpu.VMEM((tm, tj), F32)] if nk > 1 else []
    return pl.pallas_call(
        body, grid=(M // tm, J // tj, nk), in_specs=in_specs, out_specs=[o_spec],
        out_shape=[jax.ShapeDtypeStruct((M, J), BF16 if mode == "dact" else F32)],
        scratch_shapes=scratch, compiler_params=_cp("parallel", "parallel", "arbitrary"), name=name)(*args)[0]


def _mm_tn(name, x, dy, x_spec, dy_spec, o_spec, o_shape, T, K, N, tt, tk, tn):
    nt = T // tt

    def body(x_ref, dy_ref, o_ref, acc_ref):
        t = pl.program_id(2)
        part = _tn(x_ref[...], dy_ref[...])

        @pl.when(t == 0)
        def _():
            acc_ref[...] = part

        @pl.when(t > 0)
        def _():
            acc_ref[...] += part

        @pl.when(t == nt - 1)
        def _():
            o_ref[...] = acc_ref[...].astype(BF16)

    return pl.pallas_call(
        body, grid=(K // tk, N // tn, nt), in_specs=[x_spec(tt, tk), dy_spec(tt, tn)], out_specs=o_spec(tk, tn),
        out_shape=jax.ShapeDtypeStruct(o_shape, BF16), scratch_shapes=[pltpu.VMEM((tk, tn), F32)],
        compiler_params=_cp("parallel", "parallel", "arbitrary"), name=name)(x, dy)


def _pool_counts(base, n, w):
    pos = _row_ids(base, n) - PAD
    return jnp.clip(pos + 1, 1, w).astype(F32)


def _ev_fwd(u, cw, cb, lg, lb, pw, pb, ps):
    T = u.shape[0]
    C = 512
    tm = _tile(T, EV_ROWS, HALO)
    nsub = tm // HALO
    hb = tm // HALO

    def body(val_ref, gate_ref, pin_ref, valh_ref, gateh_ref, pinh_ref, cw_ref, cb_ref, lg_ref, lb_ref, pw_ref,
             pb_ref, ps_ref, yab_ref, yc_ref, a_ext, p_ext, d_buf):
        i = pl.program_id(0)
        nf = (i > 0).astype(F32)
        a_ext[0:HALO, :] = valh_ref[...] * jax.nn.sigmoid(gateh_ref[...]) * nf
        a_ext[HALO:, :] = val_ref[...] * jax.nn.sigmoid(gate_ref[...])
        p_ext[0:HALO, :] = pinh_ref[...] * nf
        p_ext[HALO:, :] = pin_ref[...]

        def sub(s, carry):
            base = pl.multiple_of(s * HALO, HALO)
            win = a_ext[pl.ds(base, 2 * HALO), :]
            acc = jnp.zeros((HALO, C), F32) + cb_ref[...]
            for j in range(CONV_WIDTH):
                acc = acc + cw_ref[pl.ds(j, 1), :] * win[2 + j:2 + j + HALO]
            yc_ref[pl.ds(base, HALO), :] = acc
            mu = jnp.mean(acc, axis=-1, keepdims=True)
            yc = acc - mu
            rstd = lax.rsqrt(jnp.mean(yc * yc, axis=-1, keepdims=True) + EPS)
            z = (yc * rstd) * lg_ref[...] + lb_ref[...]
            yab_ref[pl.ds(base, HALO), 0:C] = (z * jax.nn.sigmoid(z)).astype(BF16)
            pwin = p_ext[pl.ds(base, 2 * HALO), :]
            for gi, w in enumerate(POOL_WINDOWS):
                lo, hi = gi * HEAD, (gi + 1) * HEAD
                x = pwin[HALO:, lo:hi]
                tot = x
                for k in range(1, w):
                    tot = tot + pwin[HALO - k:2 * HALO - k, lo:hi]
                cnt = _pool_counts(i * tm + base, HALO, w)
                d_buf[pl.ds(base, HALO), lo:hi] = (tot / cnt - x).astype(BF16)
            return carry

        lax.fori_loop(0, nsub, sub, 0)
        for gi in range(len(POOL_WINDOWS)):
            lo, hi = gi * HEAD, (gi + 1) * HEAD
            y = _nn(d_buf[:, lo:hi], pw_ref[gi]) + pb_ref[:, lo:hi]
            yab_ref[:, C + lo:C + hi] = (y * ps_ref[:, lo:hi]).astype(BF16)

    def main(c):
        return pl.BlockSpec((tm, C), lambda i: (i, c))

    def halo(c):
        return pl.BlockSpec((HALO, C), lambda i: (jnp.maximum(i * hb - 1, 0), c))

    vec = pl.BlockSpec((1, C), lambda i: (0, 0))
    return pl.pallas_call(
        body, grid=(T // tm,),
        in_specs=[main(0), main(1), main(2), halo(0), halo(1), halo(2),
                  pl.BlockSpec((32, C), lambda i: (0, 0)), vec, vec, vec,
                  pl.BlockSpec((4, HEAD, HEAD), lambda i: (0, 0, 0)), vec, vec],
        out_specs=[pl.BlockSpec((tm, 2 * C), lambda i: (i, 0)), pl.BlockSpec((tm, C), lambda i: (i, 0))],
        out_shape=[jax.ShapeDtypeStruct((T, 2 * C), BF16), jax.ShapeDtypeStruct((T, C), F32)],
        scratch_shapes=[pltpu.VMEM((tm + HALO, C), F32), pltpu.VMEM((tm + HALO, C), F32), pltpu.VMEM((tm, C), BF16)],
        compiler_params=_cp("parallel"), name="ev_fwd")(u, u, u, u, u, u, cw, cb, lg, lb, pw, pb, ps)


def _ev_bwd(dyab, yc, u, cw, lg, lb, pw, pwt, pb, ps):
    T = u.shape[0]
    C = 512
    tm = _tile(T, EV_ROWS, HALO)
    nsub = tm // HALO
    hb = tm // HALO
    nblk = T // tm
    E = tm + HALO

    def body(dya_ref, dyb_ref, dyah_ref, dybh_ref, yc_ref, ych_ref, val_ref, gate_ref, pin_ref, valh_ref, gateh_ref,
             pinh_ref, cw_ref, lg_ref, lb_ref, pw_ref, pwt_ref, pb_ref, ps_ref,
             du_ref, dcw_ref, dvec_ref, dpw_ref,
             dy_ext, a_ext, p_ext, ddc_ext, dd_buf, d_buf, dpre_buf, dcw_acc, vec_acc):
        i = pl.program_id(0)
        nf = (i > 0).astype(F32)
        nl = (i < nblk - 1).astype(F32)

        @pl.when(i == 0)
        def _():
            dcw_ref[...] = jnp.zeros_like(dcw_ref)
            dvec_ref[...] = jnp.zeros_like(dvec_ref)
            dpw_ref[...] = jnp.zeros_like(dpw_ref)

        dcw_acc[...] = jnp.zeros_like(dcw_acc)
        vec_acc[...] = jnp.zeros_like(vec_acc)
        a_ext[0:HALO, :] = valh_ref[...] * jax.nn.sigmoid(gateh_ref[...]) * nf
        a_ext[HALO:, :] = val_ref[...] * jax.nn.sigmoid(gate_ref[...])
        p_ext[0:HALO, :] = pinh_ref[...] * nf
        p_ext[HALO:, :] = pin_ref[...]

        def ln_bwd(y, dya, main):
            mu = jnp.mean(y, axis=-1, keepdims=True)
            ycen = y - mu
            rstd = lax.rsqrt(jnp.mean(ycen * ycen, axis=-1, keepdims=True) + EPS)
            yh = ycen * rstd
            z = yh * lg_ref[...] + lb_ref[...]
            sz = jax.nn.sigmoid(z)
            dz = dya * _dsilu(z, sz)
            dyh = dz * lg_ref[...]
            dy = rstd * (dyh - jnp.mean(dyh, axis=-1, keepdims=True) - yh * jnp.mean(dyh * yh, axis=-1, keepdims=True))
            if main:
                vec_acc[1] += jnp.sum((dz * yh).reshape(HALO // 8, 8, C), axis=0)
                vec_acc[2] += jnp.sum(dz.reshape(HALO // 8, 8, C), axis=0)
                vec_acc[0] += jnp.sum(dy.reshape(HALO // 8, 8, C), axis=0)
            return dy

        def pool_dd(dyb, base, main):
            dpre = dyb * ps_ref[...]
            for gi, w in enumerate(POOL_WINDOWS):
                lo, hi = gi * HEAD, (gi + 1) * HEAD
                dd = _nn(dpre[:, lo:hi].astype(BF16), pwt_ref[gi])
                cnt = _pool_counts(i * tm + base, HALO, w)
                ddc_ext[pl.ds(base, HALO), lo:hi] = dd / cnt
                if main:
                    dd_buf[pl.ds(base, HALO), lo:hi] = dd
            if main:
                dpre_buf[pl.ds(base, HALO), :] = dpre.astype(BF16)
                vec_acc[4] += jnp.sum(dpre.reshape(HALO // 8, 8, C), axis=0)

        def p1(s, carry):
            base = pl.multiple_of(s * HALO, HALO)
            dy_ext[pl.ds(base, HALO), :] = ln_bwd(yc_ref[pl.ds(base, HALO), :], dya_ref[pl.ds(base, HALO), :], True)
            pool_dd(dyb_ref[pl.ds(base, HALO), :], base, True)
            return carry

        lax.fori_loop(0, nsub, p1, 0)
        dy_ext[tm:, :] = ln_bwd(ych_ref[...], dyah_ref[...], False) * nl
        dpre_h = dybh_ref[...] * ps_ref[...] * nl
        for gi, w in enumerate(POOL_WINDOWS):
            lo, hi = gi * HEAD, (gi + 1) * HEAD
            dd = _nn(dpre_h[:, lo:hi].astype(BF16), pwt_ref[gi])
            ddc_ext[tm:, lo:hi] = dd / _pool_counts(i * tm + tm, HALO, w)

        def p2(s, carry):
            base = pl.multiple_of(s * HALO, HALO)
            dwin = dy_ext[pl.ds(base, 2 * HALO), :]
            awin = a_ext[pl.ds(base, 2 * HALO), :]
            dy_m = dwin[0:HALO]
            da = jnp.zeros((HALO, C), F32)
            for j in range(CONV_WIDTH):
                sh = CONV_WIDTH - 1 - j
                da = da + cw_ref[pl.ds(j, 1), :] * dwin[sh:sh + HALO]
                dcw_acc[j] += jnp.sum((dy_m * awin[2 + j:2 + j + HALO]).reshape(HALO // 8, 8, C), axis=0)
            v = val_ref[pl.ds(base, HALO), :]
            g = gate_ref[pl.ds(base, HALO), :]
            sg = jax.nn.sigmoid(g)
            du_ref[pl.ds(base, HALO), 0:C] = (da * sg).astype(BF16)
            du_ref[pl.ds(base, HALO), C:2 * C] = (da * v * sg * (1.0 - sg)).astype(BF16)
            pwin = p_ext[pl.ds(base, 2 * HALO), :]
            cwin = ddc_ext[pl.ds(base, 2 * HALO), :]
            for gi, w in enumerate(POOL_WINDOWS):
                lo, hi = gi * HEAD, (gi + 1) * HEAD
                x = pwin[HALO:, lo:hi]
                tot = x
                back = cwin[0:HALO, lo:hi]
                for k in range(1, w):
                    tot = tot + pwin[HALO - k:2 * HALO - k, lo:hi]
                    back = back + cwin[k:k + HALO, lo:hi]
                cnt = _pool_counts(i * tm + base, HALO, w)
                d_buf[pl.ds(base, HALO), lo:hi] = (tot / cnt - x).astype(BF16)
                du_ref[pl.ds(base, HALO), 2 * C + lo:2 * C + hi] = (back - dd_buf[pl.ds(base, HALO), lo:hi]).astype(BF16)
            return carry

        lax.fori_loop(0, nsub, p2, 0)
        for gi in range(len(POOL_WINDOWS)):
            lo, hi = gi * HEAD, (gi + 1) * HEAD
            pre = _nn(d_buf[:, lo:hi], pw_ref[gi]) + pb_ref[:, lo:hi]
            vec_acc[3, :, lo:hi] += jnp.sum((dyb_ref[:, lo:hi] * pre).reshape(tm // 8, 8, HEAD), axis=0)
            dpw_ref[gi] += _tn(d_buf[:, lo:hi], dpre_buf[:, lo:hi])
        for j in range(CONV_WIDTH):
            dcw_ref[pl.ds(j, 1), :] += jnp.sum(dcw_acc[j], axis=0, keepdims=True)
        for r in range(5):
            dvec_ref[pl.ds(r, 1), :] += jnp.sum(vec_acc[r], axis=0, keepdims=True)

    def main(c, width=C):
        return pl.BlockSpec((tm, width), lambda i: (i, c))

    def prev(c):
        return pl.BlockSpec((HALO, C), lambda i: (jnp.maximum(i * hb - 1, 0), c))

    def nxt(c):
        return pl.BlockSpec((HALO, C), lambda i: (jnp.minimum((i + 1) * hb, T // HALO - 1), c))

    vec = pl.BlockSpec((1, C), lambda i: (0, 0))
    mat = pl.BlockSpec((4, HEAD, HEAD), lambda i: (0, 0, 0))
    return pl.pallas_call(
        body, grid=(nblk,),
        in_specs=[main(0), main(1), nxt(0), nxt(1), main(0), nxt(0), main(0), main(1), main(2), prev(0), prev(1),
                  prev(2), pl.BlockSpec((32, C), lambda i: (0, 0)), vec, vec, mat, mat, vec, vec],
        out_specs=[pl.BlockSpec((tm, 3 * C), lambda i: (i, 0)), pl.BlockSpec((32, C), lambda i: (0, 0)),
                   pl.BlockSpec((8, C), lambda i: (0, 0)), mat],
        out_shape=[jax.ShapeDtypeStruct((T, 3 * C), BF16), jax.ShapeDtypeStruct((32, C), F32),
                   jax.ShapeDtypeStruct((8, C), F32), jax.ShapeDtypeStruct((4, HEAD, HEAD), F32)],
        scratch_shapes=[pltpu.VMEM((E, C), F32), pltpu.VMEM((E, C), F32), pltpu.VMEM((E, C), F32),
                        pltpu.VMEM((E, C), F32), pltpu.VMEM((tm, C), F32), pltpu.VMEM((tm, C), BF16),
                        pltpu.VMEM((tm, C), BF16), pltpu.VMEM((32, 8, C), F32), pltpu.VMEM((8, 8, C), F32)],
        compiler_params=_cp("arbitrary"), name="ev_bwd")(
            dyab, dyab, dyab, dyab, yc, yc, u, u, u, u, u, u, cw, lg, lb, pw, pwt, pb, ps)


def _tri(n, upper):
    r = lax.broadcasted_iota(jnp.int32, (n, n), 0)
    c = lax.broadcasted_iota(jnp.int32, (n, n), 1)
    return ((c >= r) if upper else (c <= r)).astype(F32)


def _hgrn_gates(qr, fr, lbv):
    sq = jax.nn.sigmoid(qr)
    sg = jax.nn.sigmoid(fr)
    fg = lbv + (1.0 - lbv) * sg
    return qr * sq, sq, sg, fg, 1.0 - fg, jnp.log(fg)


def _hgrn_fwd(u, lbv, gn):
    T = u.shape[0]
    H = 8
    RB = _tile(T, HGRN_ROWS, CHUNK)
    NC = RB // CHUNK
    NS = CHUNK // SUB

    def body(q_ref, f_ref, i_ref, g_ref, lb_ref, gn_ref, y_ref, o_ref, s0_ref, st, qs, ks, bs, vs, os_):
        rb = pl.program_id(1)

        @pl.when(rb == 0)
        def _():
            st[...] = jnp.zeros_like(st)

        tri = _tri(CHUNK, False)
        tsub = lax.broadcasted_iota(jnp.int32, (SUB, 1), 0)

        def chunk(c, carry):
            r0 = pl.multiple_of(c * CHUNK, CHUNK)
            rows = pl.ds(r0, CHUNK)
            q, _, _, _, kk, lf = _hgrn_gates(q_ref[rows, :], f_ref[rows, :], lb_ref[...])
            v = i_ref[rows, :]
            b = jnp.dot(tri, lf, preferred_element_type=F32, precision=lax.Precision.HIGHEST)
            qs[...] = q
            ks[...] = kk
            bs[...] = b
            vs[...] = v
            st0 = st[...]
            s0_ref[0, c] = st0
            os_[...] = _nt((q * jnp.exp(b)).astype(BF16), st0.astype(BF16))
            for I in range(NS):
                lo = I * SUB
                qI = qs[lo:lo + SUB, :]
                bI = bs[lo:lo + SUB, :]
                oI = jnp.zeros((SUB, HEAD), F32)
                if I > 0:
                    bprev = bs[pl.ds(lo - 1, 1), :]
                    qt = (qI * jnp.exp(bI - bprev)).astype(BF16)
                    kt = (ks[0:lo, :] * jnp.exp(bprev - bs[0:lo, :])).astype(BF16)
                    A = _nt(qt, kt)
                    oI = oI + _nn(A.astype(BF16), vs[0:lo, :].astype(BF16))
                for s in range(SUB):
                    row = pl.ds(lo + s, 1)
                    Es = jnp.exp(jnp.where(tsub >= s, bI - bs[row, :], NEG))
                    col = jnp.sum(qI * Es * ks[row, :], axis=1, keepdims=True)
                    oI = oI + col * vs[row, :]
                os_[lo:lo + SUB, :] += oI
            blast = bs[pl.ds(CHUNK - 1, 1), :]
            kh = kk * jnp.exp(blast - b)
            st[...] = st0 * jnp.exp(blast) + _tn(v.astype(BF16), kh.astype(BF16))
            o = os_[...]
            o_ref[rows, :] = o
            rr = lax.rsqrt(jnp.mean(o * o, axis=-1, keepdims=True) + EPS)
            gr = g_ref[rows, :]
            y_ref[rows, :] = (((o * rr) * gn_ref[...]) * (gr * jax.nn.sigmoid(gr))).astype(BF16)
            return carry

        lax.fori_loop(0, NC, chunk, 0)

    def blk(q):
        return pl.BlockSpec((RB, HEAD), lambda h, r: (r, q * H + h))

    sc = lambda: pltpu.VMEM((CHUNK, HEAD), F32)
    return pl.pallas_call(
        body, grid=(H, T // RB),
        in_specs=[blk(0), blk(1), blk(2), blk(3), pl.BlockSpec((1, HEAD), lambda h, r: (0, h)),
                  pl.BlockSpec((1, HEAD), lambda h, r: (0, 0))],
        out_specs=[pl.BlockSpec((RB, HEAD), lambda h, r: (r, h)), pl.BlockSpec((RB, HEAD), lambda h, r: (r, h)),
                   pl.BlockSpec((1, NC, HEAD, HEAD), lambda h, r: (h, r, 0, 0))],
        out_shape=[jax.ShapeDtypeStruct((T, H * HEAD), BF16), jax.ShapeDtypeStruct((T, H * HEAD), F32),
                   jax.ShapeDtypeStruct((H, T // CHUNK, HEAD, HEAD), F32)],
        scratch_shapes=[pltpu.VMEM((HEAD, HEAD), F32), sc(), sc(), sc(), sc(), sc()],
        compiler_params=_cp("parallel", "arbitrary"), name="hgrn_fwd")(u, u, u, u, lbv, gn)


def _hgrn_bwd(dy, o, s0, u, lbv, gn):
    T = u.shape[0]
    H = 8
    RB = _tile(T, HGRN_ROWS, CHUNK)
    NB = T // RB
    NC = RB // CHUNK
    NS = CHUNK // SUB

    def body(q_ref, f_ref, i_ref, g_ref, lb_ref, gn_ref, o_ref, dy_ref, s0_ref, du_ref, dlb_ref, dgn_ref,
             dst, qs, ks, bs, vs, dos, dqs, dks, dki, dvs, dbs):
        rb = pl.program_id(1)

        @pl.when(rb == 0)
        def _():
            dst[...] = jnp.zeros_like(dst)
            dlb_ref[...] = jnp.zeros_like(dlb_ref)
            dgn_ref[...] = jnp.zeros_like(dgn_ref)

        tril = _tri(CHUNK, False)
        triu = _tri(CHUNK, True)
        tsub = lax.broadcasted_iota(jnp.int32, (SUB, 1), 0)
        lbv_ = lb_ref[...]
        gnv = gn_ref[...]

        def chunk(cc, carry):
            c = NC - 1 - cc
            r0 = pl.multiple_of(c * CHUNK, CHUNK)
            rows = pl.ds(r0, CHUNK)
            qr = q_ref[rows, :]
            q, sq, sg, fg, kk, lf = _hgrn_gates(qr, f_ref[rows, :], lbv_)
            v = i_ref[rows, :]
            gr = g_ref[rows, :]
            b = jnp.dot(tril, lf, preferred_element_type=F32, precision=lax.Precision.HIGHEST)
            eb = jnp.exp(b)
            ov = o_ref[rows, :]
            dyv = dy_ref[rows, :]
            rr = lax.rsqrt(jnp.mean(ov * ov, axis=-1, keepdims=True) + EPS)
            oh = ov * rr
            gs = jax.nn.sigmoid(gr)
            dgr = dyv * (oh * gnv) * _dsilu(gr, gs)
            dnrm = dyv * (gr * gs)
            dgn_ref[0] += jnp.sum(dnrm * oh, axis=0, keepdims=True)
            t1 = dnrm * gnv
            do = rr * (t1 - oh * jnp.mean(t1 * oh, axis=-1, keepdims=True))
            qs[...] = q
            ks[...] = kk
            bs[...] = b
            vs[...] = v
            dos[...] = do
            st0 = s0_ref[0, c]
            dS = dst[...]
            do_b = do.astype(BF16)
            blast = bs[pl.ds(CHUNK - 1, 1), :]
            elast = jnp.exp(blast - b)
            dq_inter = _nn(do_b, st0.astype(BF16)) * eb
            dqs[...] = dq_inter
            dbs[...] = q * dq_inter
            kh = kk * elast
            dvs[...] = _nt(kh.astype(BF16), dS.astype(BF16))
            dk_inter = _nn(v.astype(BF16), dS.astype(BF16)) * elast
            dki[...] = dk_inter
            dks[...] = jnp.zeros_like(dks)
            for I in range(NS):
                lo = I * SUB
                qI = qs[lo:lo + SUB, :]
                bI = bs[lo:lo + SUB, :]
                doI = dos[lo:lo + SUB, :]
                dqI = jnp.zeros((SUB, HEAD), F32)
                dbI = jnp.zeros((SUB, HEAD), F32)
                if I > 0:
                    bprev = bs[pl.ds(lo - 1, 1), :]
                    eq = jnp.exp(bI - bprev)
                    ek = jnp.exp(bprev - bs[0:lo, :])
                    qt = (qI * eq).astype(BF16)
                    kt = (ks[0:lo, :] * ek).astype(BF16)
                    A = _nt(qt, kt).astype(BF16)
                    doI_b = doI.astype(BF16)
                    dA = _nt(doI_b, vs[0:lo, :].astype(BF16)).astype(BF16)
                    dvs[0:lo, :] += _tn(A, doI_b)
                    dqt = _nn(dA, kt)
                    dkt = _tn(dA, qt)
                    dqI = dqI + dqt * eq
                    dbI = dbI + qt.astype(F32) * dqt
                    dks[0:lo, :] += dkt * ek
                    dbs[0:lo, :] -= kt.astype(F32) * dkt
                for s in range(SUB):
                    row = pl.ds(lo + s, 1)
                    krow = ks[row, :]
                    Es = jnp.exp(jnp.where(tsub >= s, bI - bs[row, :], NEG))
                    qE = qI * Es
                    col = jnp.sum(qE * krow, axis=1, keepdims=True)
                    dcol = jnp.sum(doI * vs[row, :], axis=1, keepdims=True)
                    dvs[row, :] += jnp.sum(col * doI, axis=0, keepdims=True)
                    dq_s = dcol * Es * krow
                    dqI = dqI + dq_s
                    dbI = dbI + qI * dq_s
                    dk_s = jnp.sum(dcol * qE, axis=0, keepdims=True)
                    dks[row, :] += dk_s
                    dbs[row, :] -= krow * dk_s
                dqs[lo:lo + SUB, :] += dqI
                dbs[lo:lo + SUB, :] += dbI
            kdk = kk * dki[...]
            excl = jnp.dot(tril, kdk, preferred_element_type=F32, precision=lax.Precision.HIGHEST) - kdk
            suff = jnp.dot(triu, dbs[...], preferred_element_type=F32, precision=lax.Precision.HIGHEST)
            gdec = jnp.sum(dS * st0, axis=0, keepdims=True) * jnp.exp(blast)
            dlf = suff + excl + gdec
            dk = dks[...] + dki[...]
            dfg = dlf / fg - dk
            dlb_ref[...] += jnp.sum(dfg * (1.0 - sg), axis=0, keepdims=True)
            du_ref[0, rows, :] = (dqs[...] * _dsilu(qr, sq)).astype(BF16)
            du_ref[1, rows, :] = (dfg * (1.0 - lbv_) * sg * (1.0 - sg)).astype(BF16)
            du_ref[2, rows, :] = dvs[...].astype(BF16)
            du_ref[3, rows, :] = dgr.astype(BF16)
            dst[...] = dS * jnp.exp(blast) + _tn(do_b, (q * eb).astype(BF16))
            return carry

        lax.fori_loop(0, NC, chunk, 0)

    def blk(qd):
        return pl.BlockSpec((RB, HEAD), lambda h, r: (NB - 1 - r, qd * H + h))

    hblk = pl.BlockSpec((RB, HEAD), lambda h, r: (NB - 1 - r, h))
    sc = lambda: pltpu.VMEM((CHUNK, HEAD), F32)
    return pl.pallas_call(
        body, grid=(H, NB),
        in_specs=[blk(0), blk(1), blk(2), blk(3), pl.BlockSpec((1, HEAD), lambda h, r: (0, h)),
                  pl.BlockSpec((1, HEAD), lambda h, r: (0, 0)), hblk, hblk,
                  pl.BlockSpec((1, NC, HEAD, HEAD), lambda h, r: (h, NB - 1 - r, 0, 0))],
        out_specs=[pl.BlockSpec((4, RB, HEAD), lambda h, r: (0, NB - 1 - r, h)),
                   pl.BlockSpec((1, HEAD), lambda h, r: (0, h)), pl.BlockSpec((1, 1, HEAD), lambda h, r: (h, 0, 0))],
        out_shape=[jax.ShapeDtypeStruct((4, T, H * HEAD), BF16), jax.ShapeDtypeStruct((1, H * HEAD), F32),
                   jax.ShapeDtypeStruct((H, 1, HEAD), F32)],
        scratch_shapes=[pltpu.VMEM((HEAD, HEAD), F32)] + [sc() for _ in range(10)],
        compiler_params=_cp("parallel", "arbitrary"), name="hgrn_bwd")(u, u, u, u, lbv, gn, o, dy, s0)


def _softmax_rows(p_ref, L):
    rows = [p_ref[pl.ds(l, 1), :] for l in range(L)]
    m = rows[0]
    for r in rows[1:]:
        m = jnp.maximum(m, r)
    e = [jnp.exp(r - m) for r in rows]
    tot = e[0]
    for t in e[1:]:
        tot = tot + t
    return [t / tot for t in e]


def _lb_fwd(lbp):
    L, D = lbp.shape

    def body(p_ref, o_ref):
        sm = _softmax_rows(p_ref, L)
        acc = jnp.zeros((1, D), F32)
        o_ref[pl.ds(0, 1), :] = acc
        for l in range(1, L):
            acc = acc + sm[l]
            o_ref[pl.ds(l, 1), :] = acc

    return pl.pallas_call(body, out_shape=jax.ShapeDtypeStruct((L, D), F32), name="lb_fwd")(lbp)


def _lb_bwd(lbp, dlb):
    L, D = lbp.shape

    def body(p_ref, d_ref, o_ref):
        sm = _softmax_rows(p_ref, L)
        dsm = [jnp.zeros((1, D), F32)]
        for i in range(1, L):
            t = jnp.zeros((1, D), F32)
            for l in range(i, L):
                t = t + d_ref[pl.ds(l, 1), :]
            dsm.append(t)
        dot = jnp.zeros((1, D), F32)
        for i in range(L):
            dot = dot + dsm[i] * sm[i]
        for i in range(L):
            o_ref[pl.ds(i, 1), :] = sm[i] * (dsm[i] - dot)

    return pl.pallas_call(body, out_shape=jax.ShapeDtypeStruct((L, D), F32), name="lb_bwd")(lbp, dlb)


def _my_pos():
    return lax.axis_index("x"), lax.axis_index("y"), lax.axis_index("c")


def _peer(mask):
    x, y, c = _my_pos()
    mx, my, mc = (mask >> 2) & 1, (mask >> 1) & 1, mask & 1
    px = (1 - x) if mx else x
    py = (1 - y) if my else y
    pc = (1 - c) if mc else c
    return (px, py, pc), 4 * px + 2 * py + pc


def _all_gather(shards):
    n = len(shards)

    def body(*refs):
        ins, outs = refs[:n], refs[n:2 * n]
        send_sems, recv_sems, local_sems = refs[2 * n:]
        x, y, c = _my_pos()
        me = 4 * x + 2 * y + c
        local = [pltpu.make_async_copy(ins[a], outs[a].at[:, me], local_sems.at[a]) for a in range(n)]
        for cp in local:
            cp.start()
        sends = []
        for m in range(1, N_DEV):
            peer, _ = _peer(m)
            for a in range(n):
                cp = pltpu.make_async_remote_copy(
                    src_ref=ins[a], dst_ref=outs[a].at[:, me], send_sem=send_sems.at[a, m - 1],
                    recv_sem=recv_sems.at[a, m - 1], device_id=peer, device_id_type=MESH)
                cp.start()
                sends.append(cp)
        for m in range(1, N_DEV):
            peer, pid = _peer(m)
            for a in range(n):
                pltpu.make_async_remote_copy(
                    src_ref=ins[a], dst_ref=outs[a].at[:, pid], send_sem=send_sems.at[a, m - 1],
                    recv_sem=recv_sems.at[a, m - 1], device_id=peer, device_id_type=MESH).wait_recv()
        for cp in sends:
            cp.wait_send()
        for cp in local:
            cp.wait()

    out_shape = [jax.ShapeDtypeStruct((s.shape[0], N_DEV) + s.shape[1:], s.dtype) for s in shards]
    return pl.pallas_call(
        body, in_specs=[ANY] * n, out_specs=[ANY] * n, out_shape=out_shape,
        scratch_shapes=[pltpu.SemaphoreType.DMA((n, N_DEV - 1)), pltpu.SemaphoreType.DMA((n, N_DEV - 1)),
                        pltpu.SemaphoreType.DMA((n,))],
        name="all_gather_weights")(*shards)


def _exchange(grads, groups):
    n = len(grads)
    ng = 1 + max(g for g, _ in groups)
    layers = [1 + max(l for g, l in groups if g == gi) for gi in range(ng)]
    shapes = [None] * ng
    for a, (g, l) in enumerate(groups):
        shapes[g] = grads[a].shape[1:]

    def body(*refs):
        ins, outs = refs[:n], refs[n:n + ng]
        send_sems, recv_sems, local_sems = refs[n + ng:]
        x, y, c = _my_pos()
        me = 4 * x + 2 * y + c
        local = []
        for a, (g, l) in enumerate(groups):
            cp = pltpu.make_async_copy(ins[a].at[me], outs[g].at[me, l], local_sems.at[a])
            cp.start()
            local.append(cp)
        sends = []
        for m in range(1, N_DEV):
            peer, pid = _peer(m)
            for a, (g, l) in enumerate(groups):
                cp = pltpu.make_async_remote_copy(
                    src_ref=ins[a].at[pid], dst_ref=outs[g].at[me, l], send_sem=send_sems.at[a, m - 1],
                    recv_sem=recv_sems.at[a, m - 1], device_id=peer, device_id_type=MESH)
                cp.start()
                sends.append(cp)
        for m in range(1, N_DEV):
            peer, pid = _peer(m)
            for a, (g, l) in enumerate(groups):
                pltpu.make_async_remote_copy(
                    src_ref=ins[a].at[pid], dst_ref=outs[g].at[pid, l], send_sem=send_sems.at[a, m - 1],
                    recv_sem=recv_sems.at[a, m - 1], device_id=peer, device_id_type=MESH).wait_recv()
        for cp in sends:
            cp.wait_send()
        for cp in local:
            cp.wait()

    out_shape = [jax.ShapeDtypeStruct((N_DEV, layers[g]) + shapes[g], grads[[gg for gg, _ in groups].index(g)].dtype)
                 for g in range(ng)]
    return pl.pallas_call(
        body, in_specs=[ANY] * n, out_specs=[ANY] * ng, out_shape=out_shape,
        scratch_shapes=[pltpu.SemaphoreType.DMA((n, N_DEV - 1)), pltpu.SemaphoreType.DMA((n, N_DEV - 1)),
                        pltpu.SemaphoreType.DMA((n,))],
        name="exchange_grads")(*grads)


HBM_SPEC = pl.BlockSpec(memory_space=pltpu.HBM)
SEM_SPEC = pl.BlockSpec(memory_space=pltpu.SEMAPHORE)
EFFECT = pltpu.SideEffectType.DATAFLOW_SIDE_EFFECTING


def _hbm(a):
    return pltpu.with_memory_space_constraint(a, pltpu.HBM)


def _landing(own, dev):
    zone = lax.empty((N_DEV,) + own.shape, own.dtype)
    return lax.dynamic_update_slice(zone, own[None], (dev,) + (0,) * own.ndim)


def _push_start(name, srcs, lands, whole, groups):
    n = len(srcs)
    ng = 1 + max(groups)
    cnt = [groups.count(g) for g in range(ng)]
    idx = [groups[:a].count(groups[a]) for a in range(n)]

    def body(*refs):
        src_refs, land_refs = refs[:n], refs[n:2 * n]
        sems = refs[2 * n:2 * n + 2 * ng]
        token = refs[-1]
        x, y, c = _my_pos()
        me = 4 * x + 2 * y + c
        for a in range(n):
            g = groups[a]
            for m in range(1, N_DEV):
                peer, pid = _peer(m)
                pltpu.make_async_remote_copy(
                    src_ref=src_refs[a] if whole else src_refs[a].at[pid], dst_ref=land_refs[a].at[me],
                    send_sem=sems[2 * g].at[idx[a] * (N_DEV - 1) + m - 1],
                    recv_sem=sems[2 * g + 1].at[idx[a] * (N_DEV - 1) + m - 1],
                    device_id=peer, device_id_type=MESH).start()
        token[...] = jnp.zeros_like(token)

    sem_shapes = []
    for g in range(ng):
        sem_shapes += [pltpu.SemaphoreType.DMA((cnt[g] * (N_DEV - 1),))] * 2
    thru = [pltpu.HBM(s.shape, s.dtype) for s in list(srcs) + list(lands)]
    res = pl.pallas_call(
        body, name=name,
        out_shape=tuple(sem_shapes + thru + [jax.ShapeDtypeStruct((8, 128), F32)]),
        in_specs=tuple([HBM_SPEC] * (2 * n)),
        out_specs=tuple([SEM_SPEC] * (2 * ng) + [HBM_SPEC] * (2 * n) + [pl.BlockSpec(memory_space=pltpu.VMEM)]),
        input_output_aliases={i: 2 * ng + i for i in range(2 * n)},
        compiler_params=pltpu.CompilerParams(has_side_effects=EFFECT),
    )(*[_hbm(s) for s in srcs], *[_hbm(z) for z in lands])
    sems = [(res[2 * g], res[2 * g + 1]) for g in range(ng)]
    srcs_thru = list(res[2 * ng:2 * ng + n])
    lands_thru = list(res[2 * ng + n:2 * ng + 2 * n])
    return sems, srcs_thru, lands_thru, res[-1]


def _push_wait(name, srcs_thru, lands_thru, sems, after, whole):
    n = len(srcs_thru)

    def body(*refs):
        src_refs, land_refs = refs[:n], refs[n:2 * n]
        send_sems, recv_sems = refs[2 * n], refs[2 * n + 1]
        for a in range(n):
            for m in range(1, N_DEV):
                peer, pid = _peer(m)
                cp = pltpu.make_async_remote_copy(
                    src_ref=src_refs[a] if whole else src_refs[a].at[pid], dst_ref=land_refs[a].at[pid],
                    send_sem=send_sems.at[a * (N_DEV - 1) + m - 1], recv_sem=recv_sems.at[a * (N_DEV - 1) + m - 1],
                    device_id=peer, device_id_type=MESH)
                cp.wait_send()
                cp.wait_recv()

    thru = [pltpu.HBM(s.shape, s.dtype) for s in list(srcs_thru) + list(lands_thru)]
    res = pl.pallas_call(
        body, name=name, out_shape=tuple(thru),
        in_specs=tuple([HBM_SPEC] * (2 * n) + [SEM_SPEC, SEM_SPEC, ANY]),
        out_specs=tuple([HBM_SPEC] * (2 * n)),
        input_output_aliases={i: i for i in range(2 * n)},
        compiler_params=pltpu.CompilerParams(has_side_effects=EFFECT),
    )(*srcs_thru, *lands_thru, sems[0], sems[1], after)
    return list(res[n:])


def _adamw(recv, w, m, v, layer=0, prev=None):
    L, R, C = w.shape
    tr = _tile(R, max(8, (1 << 18) // C), 8) if R % 8 == 0 else R
    bc1 = 1.0 - ADAM_B1 ** ADAM_STEP
    bc2 = 1.0 - ADAM_B2 ** ADAM_STEP
    if prev is None:
        prev = [lax.empty((L, R, C), F32) for _ in range(4)]

    def body(r_ref, w_ref, m_ref, v_ref, p0, p1, p2, p3, g_ref, d_ref, nm_ref, nv_ref):
        g = r_ref[0].astype(F32)
        for s in range(1, N_DEV):
            g = g + r_ref[s].astype(F32)
        nm = ADAM_B1 * m_ref[...] + (1.0 - ADAM_B1) * g
        nv = ADAM_B2 * v_ref[...] + (1.0 - ADAM_B2) * (g * g)
        mh = nm / bc1
        vh = nv / bc2
        g_ref[...] = g
        d_ref[...] = -ADAM_LR * (mh / (jnp.sqrt(vh) + ADAM_EPS) + ADAM_WD * w_ref[...])
        nm_ref[...] = nm
        nv_ref[...] = nv

    row = pl.BlockSpec((None, tr, C), lambda i: (layer, i, 0))
    return pl.pallas_call(
        body, grid=(R // tr,),
        in_specs=[pl.BlockSpec((N_DEV, tr, C), lambda i: (0, i, 0)), row, row, row] + [ANY] * 4,
        out_specs=[row] * 4, out_shape=[jax.ShapeDtypeStruct((L, R, C), F32)] * 4,
        input_output_aliases={4: 0, 5: 1, 6: 2, 7: 3},
        compiler_params=_cp("parallel"), name="adamw")(recv, w, m, v, *prev)


def _full_w_spec(tk, tn):
    return pl.BlockSpec((tk, tn), lambda i, j, k: (k, j))


def _colblk_w_spec(n):
    def spec(tk, tn):
        per = n // tn
        return pl.BlockSpec((None, tk, tn), lambda i, j, k: (j // per, k, j % per))
    return spec


def kernel(x, meta_tokens, mix_norm_g, mlp_norm_g, final_norm_g, ev_w_in, ev_conv_w, ev_conv_b, ev_ln_g, ev_ln_b, ev_pool_w, ev_pool_b, ev_pool_scale, ev_w_out, od_w_in, od_gnorm_g, od_w_out, lb_param, mlp_w1, mlp_w2, loss_target, m_meta_tokens, m_mix_norm_g, m_mlp_norm_g, m_final_norm_g, m_ev_w_in, m_ev_conv_w, m_ev_conv_b, m_ev_ln_g, m_ev_ln_b, m_ev_pool_w, m_ev_pool_b, m_ev_pool_scale, m_ev_w_out, m_od_w_in, m_od_gnorm_g, m_od_w_out, m_lb_param, m_mlp_w1, m_mlp_w2, v_meta_tokens, v_mix_norm_g, v_mlp_norm_g, v_final_norm_g, v_ev_w_in, v_ev_conv_w, v_ev_conv_b, v_ev_ln_g, v_ev_ln_b, v_ev_pool_w, v_ev_pool_b, v_ev_pool_scale, v_ev_w_out, v_od_w_in, v_od_gnorm_g, v_od_w_out, v_lb_param, v_mlp_w1, v_mlp_w2):
    S, D = x.shape[1], x.shape[2]
    T = PAD + N_META + S
    DEPTH = mix_norm_g.shape[0]
    DFF = mlp_w1.shape[2] * N_DEV
    dev = 4 * lax.axis_index("x") + 2 * lax.axis_index("y") + lax.axis_index("c")

    g_meta, g_cw = _all_gather([meta_tokens[None], ev_conv_w])
    n_ev = ev_w_in.shape[0]
    n_od = od_w_in.shape[0]
    meta_full = jnp.transpose(g_meta[0], (1, 0, 2)).reshape(N_META, D)
    cw_full = jnp.transpose(g_cw, (0, 2, 1, 3)).reshape(n_ev, CONV_WIDTH, -1)
    cw_pad = jnp.pad(cw_full, ((0, 0), (0, 32 - CONV_WIDTH), (0, 0)))
    n_in_od = od_w_in.shape[2]
    n_w1 = mlp_w1.shape[2]

    ag_src, ag_grp, ag_at = [], [], {}
    for layer in range(DEPTH):
        j = layer // 2
        mixer = [("in", ev_w_in[j]), ("out", ev_w_out[j])] if layer % 2 == 0 else [("in", od_w_in[j]), ("out", od_w_out[j])]
        for pos, (key, arr) in enumerate(mixer + [("w1", mlp_w1[layer]), ("w2", mlp_w2[layer])]):
            ag_at[layer, key] = len(ag_src)
            ag_src.append(arr.astype(BF16))
            ag_grp.append((pos // 2) if layer == 0 else layer + 1)
    ag_src, g_meta, g_cw = lax.optimization_barrier((ag_src, g_meta, g_cw))
    ag_sems, ag_s, ag_l, ag_tok = _push_start("ag_start", ag_src, [_landing(s_, dev) for s_ in ag_src], True, ag_grp)

    def ag_wait(group, after):
        ids = [a for a in range(len(ag_src)) if ag_grp[a] == group]
        got = _push_wait(f"ag_wait_{group}", [ag_s[a] for a in ids], [ag_l[a] for a in ids], ag_sems[group], after, True)
        return dict(zip(ids, got))

    h = jnp.concatenate([jnp.zeros((PAD, D), F32), meta_full, x[0]], axis=0) + ag_tok[0, 0]
    tgt = jnp.pad(loss_target[0], ((PAD + N_META, 0), (0, 0)))
    lb_all = _lb_fwd(lb_param)

    tm_big = _tile(T, MM_ROWS_BIG, 16)
    tm_mid = _tile(T, MM_ROWS_MID, 16)

    saved = []
    for layer in range(DEPTH):
        j = layer // 2
        sv = {"h0": h}
        got = ag_wait(0 if layer == 0 else layer + 1, h)
        g_in, g_out = got[ag_at[layer, "in"]], got[ag_at[layer, "out"]]
        w_out = g_out.reshape(-1, D)
        n = _rms_fwd(h, mix_norm_g[layer][None])
        sv["n"] = n
        if layer % 2 == 0:
            w_in = jnp.transpose(g_in, (1, 0, 2)).reshape(D, -1)
            u = _mm_nn("ev_in", n, w_in, _full_w_spec, T, w_in.shape[1], D, tm_big, 512, D, "f32")
            yab, yc = _ev_fwd(u, cw_pad[j], ev_conv_b[j][None], ev_ln_g[j][None], ev_ln_b[j][None],
                              ev_pool_w[j].astype(BF16), ev_pool_b[j].reshape(1, -1), ev_pool_scale[j][None])
            sv.update(u=u, y=yab, yc=yc)
            h = _mm_nn("ev_out", yab, w_out, _full_w_spec, T, D, D, tm_mid, D, D, "resid", extra=h)
        else:
            w_in = g_in
            u = _mm_nn("od_in", n, w_in, _colblk_w_spec(n_in_od), T, N_DEV * n_in_od, D, tm_big, 512, D, "f32")
            y, o, s0 = _hgrn_fwd(u, lb_all[layer][None], od_gnorm_g[j][None])
            sv.update(u=u, y=y, o=o, s0=s0)
            h = _mm_nn("od_out", y, w_out, _full_w_spec, T, D, D, tm_mid, D, D, "resid", extra=h)
        sv["h1"] = h
        if layer == 0:
            got = ag_wait(1, h)
        g_w1 = got[ag_at[layer, "w1"]]
        w_w2 = got[ag_at[layer, "w2"]].reshape(DFF, D)
        sv.update(w_in=w_in, w_out=w_out, g_w1=g_w1, w_w2=w_w2)
        n2 = _rms_fwd(h, mlp_norm_g[layer][None])
        r, act = _mm_nn("mlp_w1", n2, g_w1, _colblk_w_spec(n_w1), T, DFF, D, tm_big, 512, D, "relu2")
        sv.update(n2=n2, r=r, act=act)
        h = _mm_nn("mlp_w2", act, w_w2, _full_w_spec, T, D, DFF, tm_mid, D, 1024, "resid", extra=h)
        saved.append(sv)

    loss_blk, dh, dhb, dg_final = _loss_head(h, final_norm_g[None], tgt)
    loss = lax.psum(loss_blk[0, 0], AXES)

    tt = _tile(T, MM_ROWS_BIG, 16)
    g_mix, g_mlp = [None] * DEPTH, [None] * DEPTH
    small ={"cw": [None] * n_ev, "vec": [None] * n_ev, "pw": [None] * n_ev, "gn": [None] * n_od}
    dlb_rows = [jnp.zeros((1, D), F32) for _ in range(DEPTH)]

    def xs2(tt_, tk):
        return pl.BlockSpec((tt_, tk), lambda a, b, t: (t, a))

    def ys2(tt_, tn):
        return pl.BlockSpec((tt_, tn), lambda a, b, t: (t, b))

    def os2(tk, tn):
        return pl.BlockSpec((tk, tn), lambda a, b, t: (a, b))

    def os3(tk, tn):
        return pl.BlockSpec((None, tk, tn), lambda a, b, t: (b, a, 0))

    def dy2(tm, tn):
        return pl.BlockSpec((tm, tn), lambda i, jj, k: (i, k))

    def w_rows(tj, tn):
        return pl.BlockSpec((tj, tn), lambda i, jj, k: (jj, k))

    def w_colblk(tj, tn):
        return pl.BlockSpec((None, tj, tn), lambda i, jj, k: (k, jj, 0))

    rs_pending = []

    def rs_start(tag, mats):
        blocks = [m_ if m_.ndim == 3 else m_.reshape(N_DEV, m_.shape[0] // N_DEV, m_.shape[1]) for m_ in mats]
        lands = [_landing(lax.dynamic_index_in_dim(b_, dev, 0, keepdims=False), dev) for b_ in blocks]
        sems, s_thru, l_thru, tok = _push_start(f"rs_start_{tag}", blocks, lands, False, [0] * len(blocks))
        rs_pending.append((tag, s_thru, l_thru, sems[0]))
        return tok[0, 0]

    for layer in reversed(range(DEPTH)):
        j = layer // 2
        sv = saved[layer]
        da1 = _mm_nt("mlp_w2_t", dhb, sv["w_w2"], dy2, w_rows, T, DFF, D, tm_mid, 1024, D, "dact", extra=sv["r"])
        dw2 = _mm_tn("mlp_dw2", sv["act"], dhb, xs2, ys2, os2, (DFF, D), T, DFF, D, tt, 1024, D)
        dw1 = _mm_tn("mlp_dw1", sv["n2"], da1, xs2, ys2, os3, (N_DEV, D, n_w1), T, D, DFF, tt, D, n_w1)
        tok = rs_start(f"mlp{layer}", [dw1, dw2])
        dn2 = _mm_nt("mlp_w1_t", da1, sv["g_w1"], dy2, w_colblk, T, D, DFF, tm_mid, D, n_w1, "f32")
        dh, dhb, g_mlp[layer] = _rms_bwd(dn2, sv["h1"], mlp_norm_g[layer][None] + tok, dh)
        if layer % 2 == 0:
            dyab = _mm_nt("ev_out_t", dhb, sv["w_out"], dy2, w_rows, T, D, D, tm_mid, D, D, "f32")
            dwout = _mm_tn("ev_dwout", sv["y"], dhb, xs2, ys2, os2, (D, D), T, D, D, tt, D, D)
            du, small["cw"][j], small["vec"][j], small["pw"][j] = _ev_bwd(
                dyab, sv["yc"], sv["u"], cw_pad[j], ev_ln_g[j][None], ev_ln_b[j][None], ev_pool_w[j].astype(BF16),
                jnp.transpose(ev_pool_w[j], (0, 2, 1)).astype(BF16), ev_pool_b[j].reshape(1, -1),
                ev_pool_scale[j][None])
            nin = du.shape[1]
            dwin = _mm_tn("ev_dwin", sv["n"], du, xs2, ys2, os2, (D, nin), T, D, nin, tt, D, 512)
            dwin = jnp.transpose(dwin.reshape(D, N_DEV, nin // N_DEV), (1, 0, 2))
            dn = _mm_nt("ev_in_t", du, sv["w_in"], dy2, w_rows, T, D, nin, tm_mid, D, 512, "f32")
        else:
            dy = _mm_nt("od_out_t", dhb, sv["w_out"], dy2, w_rows, T, D, D, tm_mid, D, D, "f32")
            dwout = _mm_tn("od_dwout", sv["y"], dhb, xs2, ys2, os2, (D, D), T, D, D, tt, D, D)
            du3, dlb_rows[layer], small["gn"][j] = _hgrn_bwd(dy, sv["o"], sv["s0"], sv["u"], lb_all[layer][None],
                                                              od_gnorm_g[j][None])
            per = D // n_in_od

            def du_t(tt_, tn):
                return pl.BlockSpec((None, tt_, tn), lambda a, b, t: (b // per, t, b % per))

            dwin = _mm_tn("od_dwin", sv["n"], du3, xs2, du_t, os3, (N_DEV, D, n_in_od), T, D, 4 * D, tt, D, n_in_od)
            dn = _mm_nt("od_in_t", du3, sv["w_in"],
                        lambda tm, tn: pl.BlockSpec((None, tm, tn), lambda i, jj, k: (k // per, i, k % per)),
                        w_colblk, T, D, 4 * D, tm_mid, D, n_in_od, "f32")
        tok = rs_start(f"mix{layer}", [dwin, dwout])
        dh, dhb, g_mix[layer] = _rms_bwd(dn, sv["h0"], mix_norm_g[layer][None] + tok, dh)

    dmeta = dh[PAD:PAD + N_META]
    grad_x = dh[PAD + N_META:][None]
    dlb_param = _lb_bwd(lb_param, jnp.concatenate(dlb_rows, axis=0))

    pieces = [
        ("meta", dmeta), ("mix", jnp.concatenate(g_mix, 0)), ("mlp", jnp.concatenate(g_mlp, 0)), ("final", dg_final),
        ("cw", jnp.stack([c[:CONV_WIDTH] for c in small["cw"]])), ("cb", jnp.stack([v_[0] for v_ in small["vec"]])),
        ("lng", jnp.stack([v_[1] for v_ in small["vec"]])), ("lnb", jnp.stack([v_[2] for v_ in small["vec"]])),
        ("pw", jnp.stack(small["pw"])), ("pb", jnp.stack([v_[4] for v_ in small["vec"]])),
        ("ps", jnp.stack([v_[3] for v_ in small["vec"]])), ("gn", jnp.stack([jnp.sum(g_, axis=0)[0] for g_ in small["gn"]])),
        ("lb", dlb_param),
    ]
    flat = jnp.concatenate([p.reshape(-1) for _, p in pieces])
    n_small = flat.shape[0]
    rows_small = -(-n_small // 1024 // 8) * 8
    flat = jnp.pad(flat, (0, rows_small * 1024 - n_small)).reshape(rows_small, 1024)

    recv_small = _exchange([jnp.broadcast_to(flat[None], (N_DEV,) + flat.shape)], [(0, 0)])[0]
    recv = {}
    for tag, s_thru, l_thru, sems in rs_pending:
        got = _push_wait(f"rs_wait_{tag}", s_thru, l_thru, sems, recv_small, False)
        layer = int(tag[3:])
        if tag.startswith("mlp"):
            recv["w1", layer], recv["w2", layer] = got
        else:
            key = "ev" if layer % 2 == 0 else "od"
            recv[key + "_in", layer // 2], recv[key + "_out", layer // 2] = got

    outs = {}
    big = {"ev_in": ("ev_w_in", ev_w_in, m_ev_w_in, v_ev_w_in), "ev_out": ("ev_w_out", ev_w_out, m_ev_w_out, v_ev_w_out),
           "od_in": ("od_w_in", od_w_in, m_od_w_in, v_od_w_in), "od_out": ("od_w_out", od_w_out, m_od_w_out, v_od_w_out),
           "w1": ("mlp_w1", mlp_w1, m_mlp_w1, v_mlp_w1), "w2": ("mlp_w2", mlp_w2, m_mlp_w2, v_mlp_w2)}
    for key, (name, w, m, v) in big.items():
        res = None
        for l in range(w.shape[0]):
            res = _adamw(recv[key, l], w, m, v, layer=l, prev=res)
        outs[name] = res

    small_params = {
        "meta": ("meta_tokens", None), "mix": ("mix_norm_g", mix_norm_g, m_mix_norm_g, v_mix_norm_g),
        "mlp": ("mlp_norm_g", mlp_norm_g, m_mlp_norm_g, v_mlp_norm_g),
        "final": ("final_norm_g", final_norm_g, m_final_norm_g, v_final_norm_g),
        "cw": ("ev_conv_w", None), "cb": ("ev_conv_b", ev_conv_b, m_ev_conv_b, v_ev_conv_b),
        "lng": ("ev_ln_g", ev_ln_g, m_ev_ln_g, v_ev_ln_g), "lnb": ("ev_ln_b", ev_ln_b, m_ev_ln_b, v_ev_ln_b),
        "pw": ("ev_pool_w", ev_pool_w, m_ev_pool_w, v_ev_pool_w), "pb": ("ev_pool_b", ev_pool_b, m_ev_pool_b, v_ev_pool_b),
        "ps": ("ev_pool_scale", ev_pool_scale, m_ev_pool_scale, v_ev_pool_scale),
        "gn": ("od_gnorm_g", od_gnorm_g, m_od_gnorm_g, v_od_gnorm_g), "lb": ("lb_param", lb_param, m_lb_param, v_lb_param),
    }
    csh = ev_conv_w.shape[2]
    msh = meta_tokens.shape[1]

    def packed(which):
        parts = []
        for key, g_ in pieces:
            ent = small_params[key]
            if key == "meta":
                src = (meta_tokens, m_meta_tokens, v_meta_tokens)[which]
                full = lax.dynamic_update_slice(jnp.zeros((N_META, D), F32), src, (0, dev * msh))
            elif key == "cw":
                src = (ev_conv_w, m_ev_conv_w, v_ev_conv_w)[which]
                full = lax.dynamic_update_slice(jnp.zeros(g_.shape, F32), src, (0, 0, dev * csh))
            else:
                full = ent[1 + which]
            parts.append(full.reshape(-1))
        f = jnp.concatenate(parts)
        return jnp.pad(f, (0, rows_small * 1024 - n_small)).reshape(rows_small, 1024)

    sres = [r_[0] for r_ in _adamw(recv_small[:, 0], packed(0)[None], packed(1)[None], packed(2)[None])]
    off = 0
    for key, g_ in pieces:
        size = g_.size
        vals = [r_.reshape(-1)[off:off + size].reshape(g_.shape) for r_ in sres]
        off += size
        name = small_params[key][0]
        if key == "meta":
            vals = [lax.dynamic_slice(v_, (0, dev * msh), (N_META, msh)) for v_ in vals]
        elif key == "cw":
            vals = [lax.dynamic_slice(v_, (0, 0, dev * csh), v_.shape[:2] + (csh,)) for v_ in vals]
        else:
            vals = [v_.reshape(small_params[key][1].shape) for v_ in vals]
        outs[name] = vals

    names = ["meta_tokens", "mix_norm_g", "mlp_norm_g", "final_norm_g", "ev_w_in", "ev_conv_w", "ev_conv_b", "ev_ln_g",
             "ev_ln_b", "ev_pool_w", "ev_pool_b", "ev_pool_scale", "ev_w_out", "od_w_in", "od_gnorm_g", "od_w_out",
             "lb_param", "mlp_w1", "mlp_w2"]
    result = [loss, grad_x]
    for k in range(4):
        result += [outs[nm][k] for nm in names]
    return tuple(result)
```

```python
import functools

import jax
import jax.numpy as jnp
from jax import lax
from jax.experimental import pallas as pl
from jax.experimental.pallas import tpu as pltpu

F32 = jnp.float32
BF16 = jnp.bfloat16

N_DEV = 8
N_META = 16
CHUNK = 64
PAD = CHUNK - N_META
SUB = 16
HEAD = 128
CONV_WIDTH = 31
HALO = 32
POOL_WINDOWS = (2, 4, 8, 16)
EPS = 1e-6
NEG = -1e30
ADAM_LR, ADAM_B1, ADAM_B2, ADAM_EPS, ADAM_WD, ADAM_STEP = 0.001, 0.9, 0.999, 1e-08, 0.01, 10
VMEM_LIMIT = 56 * 1024 * 1024
EV_ROWS = 416
HGRN_ROWS = 832
MM_ROWS_BIG = 2080
MM_ROWS_MID = 1040
HGRN_HEADS_PER_STEP = 2
MESH = pl.DeviceIdType.MESH
AXES = ("x", "y", "c")
ANY = pl.BlockSpec(memory_space=pl.ANY)


def _cp(*sem):
    return pltpu.CompilerParams(dimension_semantics=sem, vmem_limit_bytes=VMEM_LIMIT)


def _tile(n, cap, mult):
    best = None
    for d in range(mult, min(n, cap) + 1, mult):
        if n % d == 0:
            best = d
    assert best is not None, (n, cap, mult)
    return best


def _nt(a, b):
    return lax.dot_general(a, b, (((1,), (1,)), ((), ())), preferred_element_type=F32)


def _tn(a, b):
    return lax.dot_general(a, b, (((0,), (0,)), ((), ())), preferred_element_type=F32)


def _nn(a, b):
    return jnp.dot(a, b, preferred_element_type=F32)


def _r16(x):
    return x.astype(BF16).astype(F32)


def _row_ids(base, n):
    return base + lax.broadcasted_iota(jnp.int32, (n, 1), 0)


def _dsilu(x, s):
    return s * (1.0 + x * (1.0 - s))


def _rms_fwd(h, g):
    T, D = h.shape
    tm = _tile(T, MM_ROWS_MID, 16)

    def body(h_ref, g_ref, n_ref):
        x = h_ref[...]
        r = lax.rsqrt(jnp.mean(x * x, axis=-1, keepdims=True) + EPS)
        n_ref[...] = ((x * r) * g_ref[...]).astype(BF16)

    return pl.pallas_call(
        body, grid=(T // tm,),
        in_specs=[pl.BlockSpec((tm, D), lambda i: (i, 0)), pl.BlockSpec((1, D), lambda i: (0, 0))],
        out_specs=pl.BlockSpec((tm, D), lambda i: (i, 0)),
        out_shape=jax.ShapeDtypeStruct((T, D), BF16),
        compiler_params=_cp("parallel"), name="rms_fwd")(h, g)


def _rms_bwd(dn, h, g, dres):
    T, D = h.shape
    tm = _tile(T, MM_ROWS_MID, 16)

    def body(dn_ref, h_ref, g_ref, dres_ref, dh_ref, dhb_ref, dg_ref):
        i = pl.program_id(0)
        x = h_ref[...]
        dn_v = dn_ref[...]
        r = lax.rsqrt(jnp.mean(x * x, axis=-1, keepdims=True) + EPS)
        xh = x * r
        dxh = dn_v * g_ref[...]
        dx = r * (dxh - xh * jnp.mean(dxh * xh, axis=-1, keepdims=True))
        keep = _row_ids(i * tm, tm) >= PAD
        dh = jnp.where(keep, dres_ref[...] + dx, 0.0)
        dh_ref[...] = dh
        dhb_ref[...] = dh.astype(BF16)

        @pl.when(i == 0)
        def _():
            dg_ref[...] = jnp.zeros_like(dg_ref)

        dg_ref[...] += jnp.sum(dn_v * xh, axis=0, keepdims=True)

    row = pl.BlockSpec((tm, D), lambda i: (i, 0))
    vec = pl.BlockSpec((1, D), lambda i: (0, 0))
    return pl.pallas_call(
        body, grid=(T // tm,),
        in_specs=[row, row, vec, row], out_specs=[row, row, vec],
        out_shape=[jax.ShapeDtypeStruct((T, D), F32), jax.ShapeDtypeStruct((T, D), BF16),
                   jax.ShapeDtypeStruct((1, D), F32)],
        compiler_params=_cp("arbitrary"), name="rms_bwd")(dn, h, g, dres)


def _loss_head(h, g, tgt):
    T, D = h.shape
    tm = _tile(T, MM_ROWS_MID, 16)
    first_x = PAD + N_META

    def body(h_ref, g_ref, t_ref, loss_ref, dh_ref, dhb_ref, dg_ref):
        i = pl.program_id(0)
        x = h_ref[...]
        r = lax.rsqrt(jnp.mean(x * x, axis=-1, keepdims=True) + EPS)
        xh = x * r
        gv = g_ref[...]
        out = xh * gv
        valid = _row_ids(i * tm, tm) >= first_x
        e = jnp.where(valid, out - t_ref[...], 0.0)
        dout = e * (1.0 / D)
        dxh = dout * gv
        dx = r * (dxh - xh * jnp.mean(dxh * xh, axis=-1, keepdims=True))
        dh_ref[...] = dx
        dhb_ref[...] = dx.astype(BF16)

        @pl.when(i == 0)
        def _():
            dg_ref[...] = jnp.zeros_like(dg_ref)
            loss_ref[...] = jnp.zeros_like(loss_ref)

        dg_ref[...] += jnp.sum(dout * xh, axis=0, keepdims=True)
        loss_ref[...] += 0.5 * jnp.sum(jnp.mean(e * e, axis=-1, keepdims=True))

    row = pl.BlockSpec((tm, D), lambda i: (i, 0))
    vec = pl.BlockSpec((1, D), lambda i: (0, 0))
    return pl.pallas_call(
        body, grid=(T // tm,),
        in_specs=[row, vec, row],
        out_specs=[pl.BlockSpec((8, 128), lambda i: (0, 0)), row, row, vec],
        out_shape=[jax.ShapeDtypeStruct((8, 128), F32), jax.ShapeDtypeStruct((T, D), F32),
                   jax.ShapeDtypeStruct((T, D), BF16), jax.ShapeDtypeStruct((1, D), F32)],
        compiler_params=_cp("arbitrary"), name="loss_head")(h, g, tgt)


def _mm_nn(name, a, w, w_spec, M, N, K, tm, tn, tk, mode, extra=None, a_spec=None):
    nk = K // tk
    if a_spec is None:
        a_spec = pl.BlockSpec((tm, tk), lambda i, j, k: (i, k))
    o_spec = pl.BlockSpec((tm, tn), lambda i, j, k: (i, j))

    def body(*refs):
        if mode == "resid":
            a_ref, w_ref, e_ref = refs[:3]
            outs = refs[3:]
        else:
            a_ref, w_ref = refs[:2]
            outs = refs[2:]
        acc_ref = outs[-1] if nk > 1 else None
        part = _nn(a_ref[...], w_ref[...])

        def finish(acc):
            if mode == "f32":
                outs[0][...] = acc
            elif mode == "relu2":
                r = jnp.maximum(acc, 0.0)
                outs[0][...] = r.astype(BF16)
                outs[1][...] = (r * r).astype(BF16)
            else:
                keep = _row_ids(pl.program_id(0) * tm, tm) >= PAD
                outs[0][...] = jnp.where(keep, e_ref[...] + acc, 0.0)

        if nk == 1:
            finish(part)
        else:
            k = pl.program_id(2)

            @pl.when(k == 0)
            def _():
                acc_ref[...] = part

            @pl.when(k > 0)
            def _():
                acc_ref[...] += part

            @pl.when(k == nk - 1)
            def _():
                finish(acc_ref[...])

    in_specs = [a_spec, w_spec(tk, tn)]
    args = [a, w]
    if mode == "resid":
        in_specs.append(o_spec)
        args.append(extra)
    if mode == "relu2":
        out_specs = [o_spec, o_spec]
        out_shape = [jax.ShapeDtypeStruct((M, N), BF16)] * 2
    else:
        out_specs = [o_spec]
        out_shape = [jax.ShapeDtypeStruct((M, N), F32)]
    scratch = [pltpu.VMEM((tm, tn), F32)] if nk > 1 else []
    res = pl.pallas_call(
        body, grid=(M // tm, N // tn, nk), in_specs=in_specs, out_specs=out_specs, out_shape=out_shape,
        scratch_shapes=scratch, compiler_params=_cp("parallel", "parallel", "arbitrary"), name=name)(*args)
    return res if mode == "relu2" else res[0]


def _mm_nt(name, dy, w, dy_spec, w_spec, M, J, N, tm, tj, tn, mode, extra=None):
    nk = N // tn
    o_spec = pl.BlockSpec((tm, tj), lambda i, j, k: (i, j))

    def body(*refs):
        if mode == "dact":
            dy_ref, w_ref, e_ref = refs[:3]
            outs = refs[3:]
        else:
            dy_ref, w_ref = refs[:2]
            outs = refs[2:]
        acc_ref = outs[-1] if nk > 1 else None
        part = _nt(dy_ref[...], w_ref[...])

        def finish(acc):
            if mode == "f32":
                outs[0][...] = acc
            else:
                outs[0][...] = (acc * (2.0 * e_ref[...].astype(F32))).astype(BF16)

        if nk == 1:
            finish(part)
        else:
            k = pl.program_id(2)

            @pl.when(k == 0)
            def _():
                acc_ref[...] = part

            @pl.when(k > 0)
            def _():
                acc_ref[...] += part

            @pl.when(k == nk - 1)
            def _():
                finish(acc_ref[...])

    in_specs = [dy_spec(tm, tn), w_spec(tj, tn)]
    args = [dy, w]
    if mode == "dact":
        in_specs.append(o_spec)
        args.append(extra)
    scratch = [pltpu.VMEM((tm, tj), F32)] if nk > 1 else []
    return pl.pallas_call(
        body, grid=(M // tm, J // tj, nk), in_specs=in_specs, out_specs=[o_spec],
        out_shape=[jax.ShapeDtypeStruct((M, J), BF16 if mode == "dact" else F32)],
        scratch_shapes=scratch, compiler_params=_cp("parallel", "parallel", "arbitrary"), name=name)(*args)[0]


def _mm_tn(name, x, dy, x_spec, dy_spec, o_spec, o_shape, T, K, N, tt, tk, tn):
    nt = T // tt

    def body(x_ref, dy_ref, o_ref, acc_ref):
        t = pl.program_id(2)
        part = _tn(x_ref[...], dy_ref[...])

        @pl.when(t == 0)
        def _():
            acc_ref[...] = part

        @pl.when(t > 0)
        def _():
            acc_ref[...] += part

        @pl.when(t == nt - 1)
        def _():
            o_ref[...] = acc_ref[...].astype(BF16)

    return pl.pallas_call(
        body, grid=(K // tk, N // tn, nt), in_specs=[x_spec(tt, tk), dy_spec(tt, tn)], out_specs=o_spec(tk, tn),
        out_shape=jax.ShapeDtypeStruct(o_shape, BF16), scratch_shapes=[pltpu.VMEM((tk, tn), F32)],
        compiler_params=_cp("parallel", "parallel", "arbitrary"), name=name)(x, dy)


def _pool_counts(base, n, w):
    pos = _row_ids(base, n) - PAD
    return jnp.clip(pos + 1, 1, w).astype(F32)


def _ev_fwd(u, cw, cb, lg, lb, pw, pb, ps):
    T = u.shape[0]
    C = 512
    tm = _tile(T, EV_ROWS, HALO)
    nsub = tm // HALO
    hb = tm // HALO

    def body(val_ref, gate_ref, pin_ref, valh_ref, gateh_ref, pinh_ref, cw_ref, cb_ref, lg_ref, lb_ref, pw_ref,
             pb_ref, ps_ref, yab_ref, yc_ref, a_ext, p_ext, d_buf):
        i = pl.program_id(0)
        nf = (i > 0).astype(F32)
        a_ext[0:HALO, :] = valh_ref[...] * jax.nn.sigmoid(gateh_ref[...]) * nf
        a_ext[HALO:, :] = val_ref[...] * jax.nn.sigmoid(gate_ref[...])
        p_ext[0:HALO, :] = pinh_ref[...] * nf
        p_ext[HALO:, :] = pin_ref[...]

        def sub(s, carry):
            base = pl.multiple_of(s * HALO, HALO)
            win = a_ext[pl.ds(base, 2 * HALO), :]
            acc = jnp.zeros((HALO, C), F32) + cb_ref[...]
            for j in range(CONV_WIDTH):
                acc = acc + cw_ref[pl.ds(j, 1), :] * win[2 + j:2 + j + HALO]
            yc_ref[pl.ds(base, HALO), :] = acc
            mu = jnp.mean(acc, axis=-1, keepdims=True)
            yc = acc - mu
            rstd = lax.rsqrt(jnp.mean(yc * yc, axis=-1, keepdims=True) + EPS)
            z = (yc * rstd) * lg_ref[...] + lb_ref[...]
            yab_ref[pl.ds(base, HALO), 0:C] = (z * jax.nn.sigmoid(z)).astype(BF16)
            pwin = p_ext[pl.ds(base, 2 * HALO), :]
            for gi, w in enumerate(POOL_WINDOWS):
                lo, hi = gi * HEAD, (gi + 1) * HEAD
                x = pwin[HALO:, lo:hi]
                tot = x
                for k in range(1, w):
                    tot = tot + pwin[HALO - k:2 * HALO - k, lo:hi]
                cnt = _pool_counts(i * tm + base, HALO, w)
                d_buf[pl.ds(base, HALO), lo:hi] = (tot / cnt - x).astype(BF16)
            return carry

        lax.fori_loop(0, nsub, sub, 0)
        for gi in range(len(POOL_WINDOWS)):
            lo, hi = gi * HEAD, (gi + 1) * HEAD
            y = _nn(d_buf[:, lo:hi], pw_ref[gi]) + pb_ref[:, lo:hi]
            yab_ref[:, C + lo:C + hi] = (y * ps_ref[:, lo:hi]).astype(BF16)

    def main(c):
        return pl.BlockSpec((tm, C), lambda i: (i, c))

    def halo(c):
        return pl.BlockSpec((HALO, C), lambda i: (jnp.maximum(i * hb - 1, 0), c))

    vec = pl.BlockSpec((1, C), lambda i: (0, 0))
    return pl.pallas_call(
        body, grid=(T // tm,),
        in_specs=[main(0), main(1), main(2), halo(0), halo(1), halo(2),
                  pl.BlockSpec((32, C), lambda i: (0, 0)), vec, vec, vec,
                  pl.BlockSpec((4, HEAD, HEAD), lambda i: (0, 0, 0)), vec, vec],
        out_specs=[pl.BlockSpec((tm, 2 * C), lambda i: (i, 0)), pl.BlockSpec((tm, C), lambda i: (i, 0))],
        out_shape=[jax.ShapeDtypeStruct((T, 2 * C), BF16), jax.ShapeDtypeStruct((T, C), F32)],
        scratch_shapes=[pltpu.VMEM((tm + HALO, C), F32), pltpu.VMEM((tm + HALO, C), F32), pltpu.VMEM((tm, C), BF16)],
        compiler_params=_cp("parallel"), name="ev_fwd")(u, u, u, u, u, u, cw, cb, lg, lb, pw, pb, ps)


def _ev_bwd(dyab, yc, u, cw, lg, lb, pw, pwt, pb, ps):
    T = u.shape[0]
    C = 512
    tm = _tile(T, EV_ROWS, HALO)
    nsub = tm // HALO
    hb = tm // HALO
    nblk = T // tm
    E = tm + HALO

    def body(dya_ref, dyb_ref, dyah_ref, dybh_ref, yc_ref, ych_ref, val_ref, gate_ref, pin_ref, valh_ref, gateh_ref,
             pinh_ref, cw_ref, lg_ref, lb_ref, pw_ref, pwt_ref, pb_ref, ps_ref,
             du_ref, dcw_ref, dvec_ref, dpw_ref,
             dy_ext, a_ext, p_ext, ddc_ext, dd_buf, d_buf, dpre_buf, dcw_acc, vec_acc):
        i = pl.program_id(0)
        nf = (i > 0).astype(F32)
        nl = (i < nblk - 1).astype(F32)

        @pl.when(i == 0)
        def _():
            dcw_ref[...] = jnp.zeros_like(dcw_ref)
            dvec_ref[...] = jnp.zeros_like(dvec_ref)
            dpw_ref[...] = jnp.zeros_like(dpw_ref)

        dcw_acc[...] = jnp.zeros_like(dcw_acc)
        vec_acc[...] = jnp.zeros_like(vec_acc)
        a_ext[0:HALO, :] = valh_ref[...] * jax.nn.sigmoid(gateh_ref[...]) * nf
        a_ext[HALO:, :] = val_ref[...] * jax.nn.sigmoid(gate_ref[...])
        p_ext[0:HALO, :] = pinh_ref[...] * nf
        p_ext[HALO:, :] = pin_ref[...]

        def ln_bwd(y, dya, main):
            mu = jnp.mean(y, axis=-1, keepdims=True)
            ycen = y - mu
            rstd = lax.rsqrt(jnp.mean(ycen * ycen, axis=-1, keepdims=True) + EPS)
            yh = ycen * rstd
            z = yh * lg_ref[...] + lb_ref[...]
            sz = jax.nn.sigmoid(z)
            dz = dya * _dsilu(z, sz)
            dyh = dz * lg_ref[...]
            dy = rstd * (dyh - jnp.mean(dyh, axis=-1, keepdims=True) - yh * jnp.mean(dyh * yh, axis=-1, keepdims=True))
            if main:
                vec_acc[1] += jnp.sum((dz * yh).reshape(HALO // 8, 8, C), axis=0)
                vec_acc[2] += jnp.sum(dz.reshape(HALO // 8, 8, C), axis=0)
                vec_acc[0] += jnp.sum(dy.reshape(HALO // 8, 8, C), axis=0)
            return dy

        def pool_dd(dyb, base, main):
            dpre = dyb * ps_ref[...]
            for gi, w in enumerate(POOL_WINDOWS):
                lo, hi = gi * HEAD, (gi + 1) * HEAD
                dd = _nn(dpre[:, lo:hi].astype(BF16), pwt_ref[gi])
                cnt = _pool_counts(i * tm + base, HALO, w)
                ddc_ext[pl.ds(base, HALO), lo:hi] = dd / cnt
                if main:
                    dd_buf[pl.ds(base, HALO), lo:hi] = dd
            if main:
                dpre_buf[pl.ds(base, HALO), :] = dpre.astype(BF16)
                vec_acc[4] += jnp.sum(dpre.reshape(HALO // 8, 8, C), axis=0)

        def p1(s, carry):
            base = pl.multiple_of(s * HALO, HALO)
            dy_ext[pl.ds(base, HALO), :] = ln_bwd(yc_ref[pl.ds(base, HALO), :], dya_ref[pl.ds(base, HALO), :], True)
            pool_dd(dyb_ref[pl.ds(base, HALO), :], base, True)
            return carry

        lax.fori_loop(0, nsub, p1, 0)
        dy_ext[tm:, :] = ln_bwd(ych_ref[...], dyah_ref[...], False) * nl
        dpre_h = dybh_ref[...] * ps_ref[...] * nl
        for gi, w in enumerate(POOL_WINDOWS):
            lo, hi = gi * HEAD, (gi + 1) * HEAD
            dd = _nn(dpre_h[:, lo:hi].astype(BF16), pwt_ref[gi])
            ddc_ext[tm:, lo:hi] = dd / _pool_counts(i * tm + tm, HALO, w)

        def p2(s, carry):
            base = pl.multiple_of(s * HALO, HALO)
            dwin = dy_ext[pl.ds(base, 2 * HALO), :]
            awin = a_ext[pl.ds(base, 2 * HALO), :]
            dy_m = dwin[0:HALO]
            da = jnp.zeros((HALO, C), F32)
            for j in range(CONV_WIDTH):
                sh = CONV_WIDTH - 1 - j
                da = da + cw_ref[pl.ds(j, 1), :] * dwin[sh:sh + HALO]
                dcw_acc[j] += jnp.sum((dy_m * awin[2 + j:2 + j + HALO]).reshape(HALO // 8, 8, C), axis=0)
            v = val_ref[pl.ds(base, HALO), :]
            g = gate_ref[pl.ds(base, HALO), :]
            sg = jax.nn.sigmoid(g)
            du_ref[pl.ds(base, HALO), 0:C] = (da * sg).astype(BF16)
            du_ref[pl.ds(base, HALO), C:2 * C] = (da * v * sg * (1.0 - sg)).astype(BF16)
            pwin = p_ext[pl.ds(base, 2 * HALO), :]
            cwin = ddc_ext[pl.ds(base, 2 * HALO), :]
            for gi, w in enumerate(POOL_WINDOWS):
                lo, hi = gi * HEAD, (gi + 1) * HEAD
                x = pwin[HALO:, lo:hi]
                tot = x
                back = cwin[0:HALO, lo:hi]
                for k in range(1, w):
                    tot = tot + pwin[HALO - k:2 * HALO - k, lo:hi]
                    back = back + cwin[k:k + HALO, lo:hi]
                cnt = _pool_counts(i * tm + base, HALO, w)
                d_buf[pl.ds(base, HALO), lo:hi] = (tot / cnt - x).astype(BF16)
                du_ref[pl.ds(base, HALO), 2 * C + lo:2 * C + hi] = (back - dd_buf[pl.ds(base, HALO), lo:hi]).astype(BF16)
            return carry

        lax.fori_loop(0, nsub, p2, 0)
        for gi in range(len(POOL_WINDOWS)):
            lo, hi = gi * HEAD, (gi + 1) * HEAD
            pre = _nn(d_buf[:, lo:hi], pw_ref[gi]) + pb_ref[:, lo:hi]
            vec_acc[3, :, lo:hi] += jnp.sum((dyb_ref[:, lo:hi] * pre).reshape(tm // 8, 8, HEAD), axis=0)
            dpw_ref[gi] += _tn(d_buf[:, lo:hi], dpre_buf[:, lo:hi])
        for j in range(CONV_WIDTH):
            dcw_ref[pl.ds(j, 1), :] += jnp.sum(dcw_acc[j], axis=0, keepdims=True)
        for r in range(5):
            dvec_ref[pl.ds(r, 1), :] += jnp.sum(vec_acc[r], axis=0, keepdims=True)

    def main(c, width=C):
        return pl.BlockSpec((tm, width), lambda i: (i, c))

    def prev(c):
        return pl.BlockSpec((HALO, C), lambda i: (jnp.maximum(i * hb - 1, 0), c))

    def nxt(c):
        return pl.BlockSpec((HALO, C), lambda i: (jnp.minimum((i + 1) * hb, T // HALO - 1), c))

    vec = pl.BlockSpec((1, C), lambda i: (0, 0))
    mat = pl.BlockSpec((4, HEAD, HEAD), lambda i: (0, 0, 0))
    return pl.pallas_call(
        body, grid=(nblk,),
        in_specs=[main(0), main(1), nxt(0), nxt(1), main(0), nxt(0), main(0), main(1), main(2), prev(0), prev(1),
                  prev(2), pl.BlockSpec((32, C), lambda i: (0, 0)), vec, vec, mat, mat, vec, vec],
        out_specs=[pl.BlockSpec((tm, 3 * C), lambda i: (i, 0)), pl.BlockSpec((32, C), lambda i: (0, 0)),
                   pl.BlockSpec((8, C), lambda i: (0, 0)), mat],
        out_shape=[jax.ShapeDtypeStruct((T, 3 * C), BF16), jax.ShapeDtypeStruct((32, C), F32),
                   jax.ShapeDtypeStruct((8, C), F32), jax.ShapeDtypeStruct((4, HEAD, HEAD), F32)],
        scratch_shapes=[pltpu.VMEM((E, C), F32), pltpu.VMEM((E, C), F32), pltpu.VMEM((E, C), F32),
                        pltpu.VMEM((E, C), F32), pltpu.VMEM((tm, C), F32), pltpu.VMEM((tm, C), BF16),
                        pltpu.VMEM((tm, C), BF16), pltpu.VMEM((32, 8, C), F32), pltpu.VMEM((8, 8, C), F32)],
        compiler_params=_cp("arbitrary"), name="ev_bwd")(
            dyab, dyab, dyab, dyab, yc, yc, u, u, u, u, u, u, cw, lg, lb, pw, pwt, pb, ps)


def _cumsum_rows(x, reverse=False):
    n = x.shape[0]
    rid = lax.broadcasted_iota(jnp.int32, (n, 1), 0)
    k = 1
    while k < n:
        if reverse:
            sh = jnp.where(rid < n - k, pltpu.roll(x, n - k, 0), 0.0)
        else:
            sh = jnp.where(rid >= k, pltpu.roll(x, k, 0), 0.0)
        x = x + sh
        k *= 2
    return x


def _hgrn_gates(qr, fr, lbv):
    sq = jax.nn.sigmoid(qr)
    sg = jax.nn.sigmoid(fr)
    fg = lbv + (1.0 - lbv) * sg
    return qr * sq, sq, sg, fg, 1.0 - fg, jnp.log(fg)


def _hgrn_fwd(u, lbv, gn):
    T = u.shape[0]
    H = 8
    RB = _tile(T, HGRN_ROWS, CHUNK)
    NC = RB // CHUNK
    NS = CHUNK // SUB

    HP = HGRN_HEADS_PER_STEP
    W = HP * HEAD

    def body(q_ref, f_ref, i_ref, g_ref, lb_ref, gn_ref, y_ref, o_ref, s0_ref, st, qs, ks, bs, vs, os_):
        rb = pl.program_id(1)

        @pl.when(rb == 0)
        def _():
            st[...] = jnp.zeros_like(st)

        tsub = lax.broadcasted_iota(jnp.int32, (SUB, 1), 0)

        def head(hh, c, rows):
            sl = slice(hh * HEAD, (hh + 1) * HEAD)
            q, _, _, _, kk, lf = _hgrn_gates(q_ref[rows, sl], f_ref[rows, sl], lb_ref[:, sl])
            v = i_ref[rows, sl]
            b = _cumsum_rows(lf)
            qs[hh] = q
            ks[hh] = kk
            bs[hh] = b
            vs[hh] = v
            st0 = st[hh]
            s0_ref[hh, c] = st0
            os_[hh] = _nt((q * jnp.exp(b)).astype(BF16), st0.astype(BF16))
            for I in range(NS):
                lo = I * SUB
                qI = qs[hh, lo:lo + SUB, :]
                bI = bs[hh, lo:lo + SUB, :]
                oI = jnp.zeros((SUB, HEAD), F32)
                if I > 0:
                    bprev = bs[hh, pl.ds(lo - 1, 1), :]
                    qt = _r16(qI * jnp.exp(bI - bprev))
                    kt = _r16(ks[hh, 0:lo, :] * jnp.exp(bprev - bs[hh, 0:lo, :]))
                    A = _nt(qt, kt)
                    oI = oI + _nn(_r16(A), _r16(vs[hh, 0:lo, :]))
                for s in range(SUB):
                    row = pl.ds(lo + s, 1)
                    Es = jnp.exp(jnp.where(tsub >= s, bI - bs[hh, row, :], NEG))
                    col = jnp.sum(qI * Es * ks[hh, row, :], axis=1, keepdims=True)
                    oI = oI + col * vs[hh, row, :]
                os_[hh, lo:lo + SUB, :] += oI
            blast = bs[hh, pl.ds(CHUNK - 1, 1), :]
            kh = kk * jnp.exp(blast - b)
            st[hh] = st0 * jnp.exp(blast) + _tn(v.astype(BF16), kh.astype(BF16))
            o = os_[hh]
            o_ref[rows, sl] = o
            rr = lax.rsqrt(jnp.mean(o * o, axis=-1, keepdims=True) + EPS)
            gr = g_ref[rows, sl]
            y_ref[rows, sl] = (((o * rr) * gn_ref[...]) * (gr * jax.nn.sigmoid(gr))).astype(BF16)

        def chunk(c, carry):
            rows = pl.ds(pl.multiple_of(c * CHUNK, CHUNK), CHUNK)
            for hh in range(HP):
                head(hh, c, rows)
            return carry

        lax.fori_loop(0, NC, chunk, 0)

    def blk(q):
        return pl.BlockSpec((RB, W), lambda h, r: (r, q * (H // HP) + h))

    sc = lambda: pltpu.VMEM((HP, CHUNK, HEAD), F32)
    return pl.pallas_call(
        body, grid=(H // HP, T // RB),
        in_specs=[blk(0), blk(1), blk(2), blk(3), pl.BlockSpec((1, W), lambda h, r: (0, h)),
                  pl.BlockSpec((1, HEAD), lambda h, r: (0, 0))],
        out_specs=[pl.BlockSpec((RB, W), lambda h, r: (r, h)), pl.BlockSpec((RB, W), lambda h, r: (r, h)),
                   pl.BlockSpec((HP, NC, HEAD, HEAD), lambda h, r: (h, r, 0, 0))],
        out_shape=[jax.ShapeDtypeStruct((T, H * HEAD), BF16), jax.ShapeDtypeStruct((T, H * HEAD), F32),
                   jax.ShapeDtypeStruct((H, T // CHUNK, HEAD, HEAD), F32)],
        scratch_shapes=[pltpu.VMEM((HP, HEAD, HEAD), F32), sc(), sc(), sc(), sc(), sc()],
        compiler_params=_cp("parallel", "arbitrary"), name="hgrn_fwd")(u, u, u, u, lbv, gn)


def _hgrn_bwd(dy, o, s0, u, lbv, gn):
    T = u.shape[0]
    H = 8
    RB = _tile(T, HGRN_ROWS, CHUNK)
    NB = T // RB
    NC = RB // CHUNK
    NS = CHUNK // SUB

    def body(q_ref, f_ref, i_ref, g_ref, lb_ref, gn_ref, o_ref, dy_ref, s0_ref, du_ref, dlb_ref, dgn_ref,
             dst, qs, ks, bs, vs, dos, dqs, dks, dki, dvs, dbs):
        rb = pl.program_id(1)

        @pl.when(rb == 0)
        def _():
            dst[...] = jnp.zeros_like(dst)
            dlb_ref[...] = jnp.zeros_like(dlb_ref)
            dgn_ref[...] = jnp.zeros_like(dgn_ref)

        tsub = lax.broadcasted_iota(jnp.int32, (SUB, 1), 0)
        lane = lax.broadcasted_iota(jnp.int32, (SUB, HEAD), 1)
        gnv = gn_ref[...]

        def head(hh, c, rows):
            sl = slice(hh * HEAD, (hh + 1) * HEAD)
            lbv_ = lb_ref[:, sl]
            qr = q_ref[rows, sl]
            q, sq, sg, fg, kk, lf = _hgrn_gates(qr, f_ref[rows, sl], lbv_)
            v = i_ref[rows, sl]
            gr = g_ref[rows, sl]
            b = _cumsum_rows(lf)
            eb = jnp.exp(b)
            ov = o_ref[rows, sl]
            dyv = dy_ref[rows, sl]
            rr = lax.rsqrt(jnp.mean(ov * ov, axis=-1, keepdims=True) + EPS)
            oh = ov * rr
            gs = jax.nn.sigmoid(gr)
            dgr = dyv * (oh * gnv) * _dsilu(gr, gs)
            dnrm = dyv * (gr * gs)
            dgn_ref[hh] += jnp.sum(dnrm * oh, axis=0, keepdims=True)
            t1 = dnrm * gnv
            do = rr * (t1 - oh * jnp.mean(t1 * oh, axis=-1, keepdims=True))
            qs[hh] = q
            ks[hh] = kk
            bs[hh] = b
            vs[hh] = v
            dos[hh] = do
            st0 = s0_ref[hh, c]
            dS = dst[hh]
            do_b = do.astype(BF16)
            blast = bs[hh, pl.ds(CHUNK - 1, 1), :]
            elast = jnp.exp(blast - b)
            dq_inter = _nn(do_b, st0.astype(BF16)) * eb
            dqs[hh] = dq_inter
            dbs[hh] = q * dq_inter
            kh = kk * elast
            dvs[hh] = _nt(kh.astype(BF16), dS.astype(BF16))
            dk_inter = _nn(v.astype(BF16), dS.astype(BF16)) * elast
            dki[hh] = dk_inter
            dks[hh] = jnp.zeros((CHUNK, HEAD), F32)
            for I in range(NS):
                lo = I * SUB
                qI = qs[hh, lo:lo + SUB, :]
                bI = bs[hh, lo:lo + SUB, :]
                doI = dos[hh, lo:lo + SUB, :]
                dqI = jnp.zeros((SUB, HEAD), F32)
                dbI = jnp.zeros((SUB, HEAD), F32)
                if I > 0:
                    bprev = bs[hh, pl.ds(lo - 1, 1), :]
                    eq = jnp.exp(bI - bprev)
                    ek = jnp.exp(bprev - bs[hh, 0:lo, :])
                    qt = _r16(qI * eq)
                    kt = _r16(ks[hh, 0:lo, :] * ek)
                    A = _r16(_nt(qt, kt))
                    doI_b = _r16(doI)
                    dA = _r16(_nt(doI_b, _r16(vs[hh, 0:lo, :])))
                    dvs[hh, 0:lo, :] += _tn(A, doI_b)
                    dqt = _nn(dA, kt)
                    dkt = _tn(dA, qt)
                    dqI = dqI + dqt * eq
                    dbI = dbI + qt.astype(F32) * dqt
                    dks[hh, 0:lo, :] += dkt * ek
                    dbs[hh, 0:lo, :] -= kt.astype(F32) * dkt
                dq_d = jnp.zeros((SUB, HEAD), F32)
                a_d = jnp.zeros((SUB, HEAD), F32)
                for s in range(SUB):
                    row = pl.ds(lo + s, 1)
                    krow = ks[hh, row, :]
                    Es = jnp.exp(jnp.where(tsub >= s, bI - bs[hh, row, :], NEG))
                    qE = qI * Es
                    col = jnp.sum(qE * krow, axis=1, keepdims=True)
                    a_d = jnp.where(lane == s, col, a_d)
                    dcol = jnp.sum(doI * vs[hh, row, :], axis=1, keepdims=True)
                    dq_d = dq_d + (dcol * Es) * krow
                    dk_s = jnp.sum(dcol * qE, axis=0, keepdims=True)
                    dks[hh, row, :] += dk_s
                    dbs[hh, row, :] -= krow * dk_s
                dvs[hh, lo:lo + SUB, :] += _tn(a_d, doI)[0:SUB]
                dqI = dqI + dq_d
                dbI = dbI + qI * dq_d
                dqs[hh, lo:lo + SUB, :] += dqI
                dbs[hh, lo:lo + SUB, :] += dbI
            kdk = kk * dki[hh]
            excl = _cumsum_rows(kdk) - kdk
            suff = _cumsum_rows(dbs[hh], reverse=True)
            gdec = jnp.sum(dS * st0, axis=0, keepdims=True) * jnp.exp(blast)
            dlf = suff + excl + gdec
            dk = dks[hh] + dki[hh]
            dfg = dlf / fg - dk
            dlb_ref[:, sl] += jnp.sum(dfg * (1.0 - sg), axis=0, keepdims=True)
            du_ref[0, rows, sl] = (dqs[hh] * _dsilu(qr, sq)).astype(BF16)
            du_ref[1, rows, sl] = (dfg * (1.0 - lbv_) * sg * (1.0 - sg)).astype(BF16)
            du_ref[2, rows, sl] = dvs[hh].astype(BF16)
            du_ref[3, rows, sl] = dgr.astype(BF16)
            dst[hh] = dS * jnp.exp(blast) + _tn(do_b, (q * eb).astype(BF16))

        def chunk(cc, carry):
            c = NC - 1 - cc
            rows = pl.ds(pl.multiple_of(c * CHUNK, CHUNK), CHUNK)
            for hh in range(HP):
                head(hh, c, rows)
            return carry

        lax.fori_loop(0, NC, chunk, 0)

    HP = HGRN_HEADS_PER_STEP
    W = HP * HEAD

    def blk(qd):
        return pl.BlockSpec((RB, W), lambda h, r: (NB - 1 - r, qd * (H // HP) + h))

    hblk = pl.BlockSpec((RB, W), lambda h, r: (NB - 1 - r, h))
    sc = lambda: pltpu.VMEM((HP, CHUNK, HEAD), F32)
    return pl.pallas_call(
        body, grid=(H // HP, NB),
        in_specs=[blk(0), blk(1), blk(2), blk(3), pl.BlockSpec((1, W), lambda h, r: (0, h)),
                  pl.BlockSpec((1, HEAD), lambda h, r: (0, 0)), hblk, hblk,
                  pl.BlockSpec((HP, NC, HEAD, HEAD), lambda h, r: (h, NB - 1 - r, 0, 0))],
        out_specs=[pl.BlockSpec((4, RB, W), lambda h, r: (0, NB - 1 - r, h)),
                   pl.BlockSpec((1, W), lambda h, r: (0, h)), pl.BlockSpec((HP, 1, HEAD), lambda h, r: (h, 0, 0))],
        out_shape=[jax.ShapeDtypeStruct((4, T, H * HEAD), BF16), jax.ShapeDtypeStruct((1, H * HEAD), F32),
                   jax.ShapeDtypeStruct((H, 1, HEAD), F32)],
        scratch_shapes=[pltpu.VMEM((HP, HEAD, HEAD), F32)] + [sc() for _ in range(10)],
        compiler_params=_cp("parallel", "arbitrary"), name="hgrn_bwd")(u, u, u, u, lbv, gn, o, dy, s0)


def _softmax_rows(p_ref, L):
    rows = [p_ref[pl.ds(l, 1), :] for l in range(L)]
    m = rows[0]
    for r in rows[1:]:
        m = jnp.maximum(m, r)
    e = [jnp.exp(r - m) for r in rows]
    tot = e[0]
    for t in e[1:]:
        tot = tot + t
    return [t / tot for t in e]


def _lb_fwd(lbp):
    L, D = lbp.shape

    def body(p_ref, o_ref):
        sm = _softmax_rows(p_ref, L)
        acc = jnp.zeros((1, D), F32)
        o_ref[pl.ds(0, 1), :] = acc
        for l in range(1, L):
            acc = acc + sm[l]
            o_ref[pl.ds(l, 1), :] = acc

    return pl.pallas_call(body, out_shape=jax.ShapeDtypeStruct((L, D), F32), name="lb_fwd")(lbp)


def _lb_bwd(lbp, dlb):
    L, D = lbp.shape

    def body(p_ref, d_ref, o_ref):
        sm = _softmax_rows(p_ref, L)
        dsm = [jnp.zeros((1, D), F32)]
        for i in range(1, L):
            t = jnp.zeros((1, D), F32)
            for l in range(i, L):
                t = t + d_ref[pl.ds(l, 1), :]
            dsm.append(t)
        dot = jnp.zeros((1, D), F32)
        for i in range(L):
            dot = dot + dsm[i] * sm[i]
        for i in range(L):
            o_ref[pl.ds(i, 1), :] = sm[i] * (dsm[i] - dot)

    return pl.pallas_call(body, out_shape=jax.ShapeDtypeStruct((L, D), F32), name="lb_bwd")(lbp, dlb)


def _my_pos():
    return lax.axis_index("x"), lax.axis_index("y"), lax.axis_index("c")


def _peer(mask):
    x, y, c = _my_pos()
    mx, my, mc = (mask >> 2) & 1, (mask >> 1) & 1, mask & 1
    px = (1 - x) if mx else x
    py = (1 - y) if my else y
    pc = (1 - c) if mc else c
    return (px, py, pc), 4 * px + 2 * py + pc


def _all_gather(shards):
    n = len(shards)

    def body(*refs):
        ins, outs = refs[:n], refs[n:2 * n]
        send_sems, recv_sems, local_sems = refs[2 * n:]
        x, y, c = _my_pos()
        me = 4 * x + 2 * y + c
        local = [pltpu.make_async_copy(ins[a], outs[a].at[:, me], local_sems.at[a]) for a in range(n)]
        for cp in local:
            cp.start()
        sends = []
        for m in range(1, N_DEV):
            peer, _ = _peer(m)
            for a in range(n):
                cp = pltpu.make_async_remote_copy(
                    src_ref=ins[a], dst_ref=outs[a].at[:, me], send_sem=send_sems.at[a, m - 1],
                    recv_sem=recv_sems.at[a, m - 1], device_id=peer, device_id_type=MESH)
                cp.start()
                sends.append(cp)
        for m in range(1, N_DEV):
            peer, pid = _peer(m)
            for a in range(n):
                pltpu.make_async_remote_copy(
                    src_ref=ins[a], dst_ref=outs[a].at[:, pid], send_sem=send_sems.at[a, m - 1],
                    recv_sem=recv_sems.at[a, m - 1], device_id=peer, device_id_type=MESH).wait_recv()
        for cp in sends:
            cp.wait_send()
        for cp in local:
            cp.wait()

    out_shape = [jax.ShapeDtypeStruct((s.shape[0], N_DEV) + s.shape[1:], s.dtype) for s in shards]
    return pl.pallas_call(
        body, in_specs=[ANY] * n, out_specs=[ANY] * n, out_shape=out_shape,
        scratch_shapes=[pltpu.SemaphoreType.DMA((n, N_DEV - 1)), pltpu.SemaphoreType.DMA((n, N_DEV - 1)),
                        pltpu.SemaphoreType.DMA((n,))],
        name="all_gather_weights")(*shards)


def _exchange(grads, groups):
    n = len(grads)
    ng = 1 + max(g for g, _ in groups)
    layers = [1 + max(l for g, l in groups if g == gi) for gi in range(ng)]
    shapes = [None] * ng
    for a, (g, l) in enumerate(groups):
        shapes[g] = grads[a].shape[1:]

    def body(*refs):
        ins, outs = refs[:n], refs[n:n + ng]
        send_sems, recv_sems, local_sems = refs[n + ng:]
        x, y, c = _my_pos()
        me = 4 * x + 2 * y + c
        local = []
        for a, (g, l) in enumerate(groups):
            cp = pltpu.make_async_copy(ins[a].at[me], outs[g].at[me, l], local_sems.at[a])
            cp.start()
            local.append(cp)
        sends = []
        for m in range(1, N_DEV):
            peer, pid = _peer(m)
            for a, (g, l) in enumerate(groups):
                cp = pltpu.make_async_remote_copy(
                    src_ref=ins[a].at[pid], dst_ref=outs[g].at[me, l], send_sem=send_sems.at[a, m - 1],
                    recv_sem=recv_sems.at[a, m - 1], device_id=peer, device_id_type=MESH)
                cp.start()
                sends.append(cp)
        for m in range(1, N_DEV):
            peer, pid = _peer(m)
            for a, (g, l) in enumerate(groups):
                pltpu.make_async_remote_copy(
                    src_ref=ins[a].at[pid], dst_ref=outs[g].at[pid, l], send_sem=send_sems.at[a, m - 1],
                    recv_sem=recv_sems.at[a, m - 1], device_id=peer, device_id_type=MESH).wait_recv()
        for cp in sends:
            cp.wait_send()
        for cp in local:
            cp.wait()

    out_shape = [jax.ShapeDtypeStruct((N_DEV, layers[g]) + shapes[g], grads[[gg for gg, _ in groups].index(g)].dtype)
                 for g in range(ng)]
    return pl.pallas_call(
        body, in_specs=[ANY] * n, out_specs=[ANY] * ng, out_shape=out_shape,
        scratch_shapes=[pltpu.SemaphoreType.DMA((n, N_DEV - 1)), pltpu.SemaphoreType.DMA((n, N_DEV - 1)),
                        pltpu.SemaphoreType.DMA((n,))],
        name="exchange_grads")(*grads)


HBM_SPEC = pl.BlockSpec(memory_space=pltpu.HBM)
SEM_SPEC = pl.BlockSpec(memory_space=pltpu.SEMAPHORE)
EFFECT = pltpu.SideEffectType.DATAFLOW_SIDE_EFFECTING


def _hbm(a):
    return pltpu.with_memory_space_constraint(a, pltpu.HBM)


def _landing(own, dev):
    zone = lax.empty((N_DEV,) + own.shape, own.dtype)
    return lax.dynamic_update_slice(zone, own[None], (dev,) + (0,) * own.ndim)


def _push_start(name, srcs, lands, whole, groups):
    n = len(srcs)
    ng = 1 + max(groups)
    cnt = [groups.count(g) for g in range(ng)]
    idx = [groups[:a].count(groups[a]) for a in range(n)]

    def body(*refs):
        src_refs, land_refs = refs[:n], refs[n:2 * n]
        sems = refs[2 * n:2 * n + 2 * ng]
        token = refs[-1]
        x, y, c = _my_pos()
        me = 4 * x + 2 * y + c
        for a in range(n):
            g = groups[a]
            for m in range(1, N_DEV):
                peer, pid = _peer(m)
                pltpu.make_async_remote_copy(
                    src_ref=src_refs[a] if whole else src_refs[a].at[pid], dst_ref=land_refs[a].at[me],
                    send_sem=sems[2 * g].at[idx[a] * (N_DEV - 1) + m - 1],
                    recv_sem=sems[2 * g + 1].at[idx[a] * (N_DEV - 1) + m - 1],
                    device_id=peer, device_id_type=MESH).start()
        token[...] = jnp.zeros_like(token)

    sem_shapes = []
    for g in range(ng):
        sem_shapes += [pltpu.SemaphoreType.DMA((cnt[g] * (N_DEV - 1),))] * 2
    thru = [pltpu.HBM(s.shape, s.dtype) for s in list(srcs) + list(lands)]
    res = pl.pallas_call(
        body, name=name,
        out_shape=tuple(sem_shapes + thru + [jax.ShapeDtypeStruct((8, 128), F32)]),
        in_specs=tuple([HBM_SPEC] * (2 * n)),
        out_specs=tuple([SEM_SPEC] * (2 * ng) + [HBM_SPEC] * (2 * n) + [pl.BlockSpec(memory_space=pltpu.VMEM)]),
        input_output_aliases={i: 2 * ng + i for i in range(2 * n)},
        compiler_params=pltpu.CompilerParams(has_side_effects=EFFECT),
    )(*[_hbm(s) for s in srcs], *[_hbm(z) for z in lands])
    sems = [(res[2 * g], res[2 * g + 1]) for g in range(ng)]
    srcs_thru = list(res[2 * ng:2 * ng + n])
    lands_thru = list(res[2 * ng + n:2 * ng + 2 * n])
    return sems, srcs_thru, lands_thru, res[-1]


def _push_wait(name, srcs_thru, lands_thru, sems, after, whole):
    n = len(srcs_thru)

    def body(*refs):
        src_refs, land_refs = refs[:n], refs[n:2 * n]
        send_sems, recv_sems = refs[2 * n], refs[2 * n + 1]
        for a in range(n):
            for m in range(1, N_DEV):
                peer, pid = _peer(m)
                cp = pltpu.make_async_remote_copy(
                    src_ref=src_refs[a] if whole else src_refs[a].at[pid], dst_ref=land_refs[a].at[pid],
                    send_sem=send_sems.at[a * (N_DEV - 1) + m - 1], recv_sem=recv_sems.at[a * (N_DEV - 1) + m - 1],
                    device_id=peer, device_id_type=MESH)
                cp.wait_send()
                cp.wait_recv()

    thru = [pltpu.HBM(s.shape, s.dtype) for s in list(srcs_thru) + list(lands_thru)]
    res = pl.pallas_call(
        body, name=name, out_shape=tuple(thru),
        in_specs=tuple([HBM_SPEC] * (2 * n) + [SEM_SPEC, SEM_SPEC, ANY]),
        out_specs=tuple([HBM_SPEC] * (2 * n)),
        input_output_aliases={i: i for i in range(2 * n)},
        compiler_params=pltpu.CompilerParams(has_side_effects=EFFECT),
    )(*srcs_thru, *lands_thru, sems[0], sems[1], after)
    return list(res[n:])


def _adamw(recv, w, m, v, layer=0, prev=None):
    L, R, C = w.shape
    tr = _tile(R, max(8, (1 << 18) // C), 8) if R % 8 == 0 else R
    bc1 = 1.0 - ADAM_B1 ** ADAM_STEP
    bc2 = 1.0 - ADAM_B2 ** ADAM_STEP
    if prev is None:
        prev = [lax.empty((L, R, C), F32) for _ in range(4)]

    def body(r_ref, w_ref, m_ref, v_ref, p0, p1, p2, p3, g_ref, d_ref, nm_ref, nv_ref):
        g = r_ref[0].astype(F32)
        for s in range(1, N_DEV):
            g = g + r_ref[s].astype(F32)
        nm = ADAM_B1 * m_ref[...] + (1.0 - ADAM_B1) * g
        nv = ADAM_B2 * v_ref[...] + (1.0 - ADAM_B2) * (g * g)
        mh = nm / bc1
        vh = nv / bc2
        g_ref[...] = g
        d_ref[...] = -ADAM_LR * (mh / (jnp.sqrt(vh) + ADAM_EPS) + ADAM_WD * w_ref[...])
        nm_ref[...] = nm
        nv_ref[...] = nv

    row = pl.BlockSpec((None, tr, C), lambda i: (layer, i, 0))
    return pl.pallas_call(
        body, grid=(R // tr,),
        in_specs=[pl.BlockSpec((N_DEV, tr, C), lambda i: (0, i, 0)), row, row, row] + [ANY] * 4,
        out_specs=[row] * 4, out_shape=[jax.ShapeDtypeStruct((L, R, C), F32)] * 4,
        input_output_aliases={4: 0, 5: 1, 6: 2, 7: 3},
        compiler_params=_cp("parallel"), name="adamw")(recv, w, m, v, *prev)


def _full_w_spec(tk, tn):
    return pl.BlockSpec((tk, tn), lambda i, j, k: (k, j))


def _colblk_w_spec(n):
    def spec(tk, tn):
        per = n // tn
        return pl.BlockSpec((None, tk, tn), lambda i, j, k: (j // per, k, j % per))
    return spec


def kernel(x, meta_tokens, mix_norm_g, mlp_norm_g, final_norm_g, ev_w_in, ev_conv_w, ev_conv_b, ev_ln_g, ev_ln_b, ev_pool_w, ev_pool_b, ev_pool_scale, ev_w_out, od_w_in, od_gnorm_g, od_w_out, lb_param, mlp_w1, mlp_w2, loss_target, m_meta_tokens, m_mix_norm_g, m_mlp_norm_g, m_final_norm_g, m_ev_w_in, m_ev_conv_w, m_ev_conv_b, m_ev_ln_g, m_ev_ln_b, m_ev_pool_w, m_ev_pool_b, m_ev_pool_scale, m_ev_w_out, m_od_w_in, m_od_gnorm_g, m_od_w_out, m_lb_param, m_mlp_w1, m_mlp_w2, v_meta_tokens, v_mix_norm_g, v_mlp_norm_g, v_final_norm_g, v_ev_w_in, v_ev_conv_w, v_ev_conv_b, v_ev_ln_g, v_ev_ln_b, v_ev_pool_w, v_ev_pool_b, v_ev_pool_scale, v_ev_w_out, v_od_w_in, v_od_gnorm_g, v_od_w_out, v_lb_param, v_mlp_w1, v_mlp_w2):
    S, D = x.shape[1], x.shape[2]
    T = PAD + N_META + S
    DEPTH = mix_norm_g.shape[0]
    DFF = mlp_w1.shape[2] * N_DEV
    dev = 4 * lax.axis_index("x") + 2 * lax.axis_index("y") + lax.axis_index("c")

    g_meta, g_cw = _all_gather([meta_tokens[None], ev_conv_w])
    n_ev = ev_w_in.shape[0]
    n_od = od_w_in.shape[0]
    meta_full = jnp.transpose(g_meta[0], (1, 0, 2)).reshape(N_META, D)
    cw_full = jnp.transpose(g_cw, (0, 2, 1, 3)).reshape(n_ev, CONV_WIDTH, -1)
    cw_pad = jnp.pad(cw_full, ((0, 0), (0, 32 - CONV_WIDTH), (0, 0)))
    n_in_od = od_w_in.shape[2]
    n_w1 = mlp_w1.shape[2]

    ag_src, ag_grp, ag_at = [], [], {}
    for layer in range(DEPTH):
        j = layer // 2
        mixer = [("in", ev_w_in[j]), ("out", ev_w_out[j])] if layer % 2 == 0 else [("in", od_w_in[j]), ("out", od_w_out[j])]
        for pos, (key, arr) in enumerate(mixer + [("w1", mlp_w1[layer]), ("w2", mlp_w2[layer])]):
            ag_at[layer, key] = len(ag_src)
            ag_src.append(arr.astype(BF16))
            ag_grp.append(2 * layer + pos // 2)
    ag_src, g_meta, g_cw = lax.optimization_barrier((ag_src, g_meta, g_cw))
    ag_sems, ag_s, ag_l, ag_tok = _push_start("ag_start", ag_src, [_landing(s_, dev) for s_ in ag_src], True, ag_grp)

    def ag_wait(group, after):
        ids = [a for a in range(len(ag_src)) if ag_grp[a] == group]
        got = _push_wait(f"ag_wait_{group}", [ag_s[a] for a in ids], [ag_l[a] for a in ids], ag_sems[group], after, True)
        return dict(zip(ids, got))

    h = jnp.concatenate([jnp.zeros((PAD, D), F32), meta_full, x[0]], axis=0) + ag_tok[0, 0]
    tgt = jnp.pad(loss_target[0], ((PAD + N_META, 0), (0, 0)))
    lb_all = _lb_fwd(lb_param)

    tm_big = _tile(T, MM_ROWS_BIG, 16)
    tm_mid = _tile(T, MM_ROWS_MID, 16)

    saved = []
    for layer in range(DEPTH):
        j = layer // 2
        sv = {"h0": h}
        got = ag_wait(2 * layer, h)
        g_in, g_out = got[ag_at[layer, "in"]], got[ag_at[layer, "out"]]
        w_out = g_out.reshape(-1, D)
        n = _rms_fwd(h, mix_norm_g[layer][None])
        sv["n"] = n
        if layer % 2 == 0:
            w_in = jnp.transpose(g_in, (1, 0, 2)).reshape(D, -1)
            u = _mm_nn("ev_in", n, w_in, _full_w_spec, T, w_in.shape[1], D, tm_big, 512, D, "f32")
            yab, yc = _ev_fwd(u, cw_pad[j], ev_conv_b[j][None], ev_ln_g[j][None], ev_ln_b[j][None],
                              ev_pool_w[j].astype(BF16), ev_pool_b[j].reshape(1, -1), ev_pool_scale[j][None])
            sv.update(u=u, y=yab, yc=yc)
            h = _mm_nn("ev_out", yab, w_out, _full_w_spec, T, D, D, tm_mid, D, D, "resid", extra=h)
        else:
            w_in = g_in
            u = _mm_nn("od_in", n, w_in, _colblk_w_spec(n_in_od), T, N_DEV * n_in_od, D, tm_big, 512, D, "f32")
            y, o, s0 = _hgrn_fwd(u, lb_all[layer][None], od_gnorm_g[j][None])
            sv.update(u=u, y=y, o=o, s0=s0)
            h = _mm_nn("od_out", y, w_out, _full_w_spec, T, D, D, tm_mid, D, D, "resid", extra=h)
        sv["h1"] = h
        got = ag_wait(2 * layer + 1, h)
        g_w1 = got[ag_at[layer, "w1"]]
        w_w2 = got[ag_at[layer, "w2"]].reshape(DFF, D)
        sv.update(w_in=w_in, w_out=w_out, g_w1=g_w1, w_w2=w_w2)
        n2 = _rms_fwd(h, mlp_norm_g[layer][None])
        r, act = _mm_nn("mlp_w1", n2, g_w1, _colblk_w_spec(n_w1), T, DFF, D, tm_big, 512, D, "relu2")
        sv.update(n2=n2, r=r, act=act)
        h = _mm_nn("mlp_w2", act, w_w2, _full_w_spec, T, D, DFF, tm_mid, D, 1024, "resid", extra=h)
        saved.append(sv)

    loss_blk, dh, dhb, dg_final = _loss_head(h, final_norm_g[None], tgt)
    loss = lax.psum(loss_blk[0, 0], AXES)

    tt = _tile(T, MM_ROWS_BIG, 16)
    g_mix, g_mlp = [None] * DEPTH, [None] * DEPTH
    small ={"cw": [None] * n_ev, "vec": [None] * n_ev, "pw": [None] * n_ev, "gn": [None] * n_od}
    dlb_rows = [jnp.zeros((1, D), F32) for _ in range(DEPTH)]

    def xs2(tt_, tk):
        return pl.BlockSpec((tt_, tk), lambda a, b, t: (t, a))

    def ys2(tt_, tn):
        return pl.BlockSpec((tt_, tn), lambda a, b, t: (t, b))

    def os2(tk, tn):
        return pl.BlockSpec((tk, tn), lambda a, b, t: (a, b))

    def os3(tk, tn):
        return pl.BlockSpec((None, tk, tn), lambda a, b, t: (b, a, 0))

    def dy2(tm, tn):
        return pl.BlockSpec((tm, tn), lambda i, jj, k: (i, k))

    def w_rows(tj, tn):
        return pl.BlockSpec((tj, tn), lambda i, jj, k: (jj, k))

    def w_colblk(tj, tn):
        return pl.BlockSpec((None, tj, tn), lambda i, jj, k: (k, jj, 0))

    rs_pending = []

    def rs_start(tag, mats):
        blocks = [m_ if m_.ndim == 3 else m_.reshape(N_DEV, m_.shape[0] // N_DEV, m_.shape[1]) for m_ in mats]
        lands = [_landing(lax.dynamic_index_in_dim(b_, dev, 0, keepdims=False), dev) for b_ in blocks]
        sems, s_thru, l_thru, tok = _push_start(f"rs_start_{tag}", blocks, lands, False, [0] * len(blocks))
        rs_pending.append((tag, s_thru, l_thru, sems[0]))
        return tok[0, 0]

    for layer in reversed(range(DEPTH)):
        j = layer // 2
        sv = saved[layer]
        da1 = _mm_nt("mlp_w2_t", dhb, sv["w_w2"], dy2, w_rows, T, DFF, D, tm_mid, 1024, D, "dact", extra=sv["r"])
        dw2 = _mm_tn("mlp_dw2", sv["act"], dhb, xs2, ys2, os2, (DFF, D), T, DFF, D, tt, 1024, D)
        dw1 = _mm_tn("mlp_dw1", sv["n2"], da1, xs2, ys2, os3, (N_DEV, D, n_w1), T, D, DFF, tt, D, n_w1)
        tok = rs_start(f"mlp{layer}", [dw1, dw2])
        dn2 = _mm_nt("mlp_w1_t", da1, sv["g_w1"], dy2, w_colblk, T, D, DFF, tm_mid, D, n_w1, "f32")
        dh, dhb, g_mlp[layer] = _rms_bwd(dn2, sv["h1"], mlp_norm_g[layer][None] + tok, dh)
        if layer % 2 == 0:
            dyab = _mm_nt("ev_out_t", dhb, sv["w_out"], dy2, w_rows, T, D, D, tm_mid, D, D, "f32")
            dwout = _mm_tn("ev_dwout", sv["y"], dhb, xs2, ys2, os2, (D, D), T, D, D, tt, D, D)
            du, small["cw"][j], small["vec"][j], small["pw"][j] = _ev_bwd(
                dyab, sv["yc"], sv["u"], cw_pad[j], ev_ln_g[j][None], ev_ln_b[j][None], ev_pool_w[j].astype(BF16),
                jnp.transpose(ev_pool_w[j], (0, 2, 1)).astype(BF16), ev_pool_b[j].reshape(1, -1),
                ev_pool_scale[j][None])
            nin = du.shape[1]
            dwin = _mm_tn("ev_dwin", sv["n"], du, xs2, ys2, os2, (D, nin), T, D, nin, tt, D, 512)
            dwin = jnp.transpose(dwin.reshape(D, N_DEV, nin // N_DEV), (1, 0, 2))
            dn = _mm_nt("ev_in_t", du, sv["w_in"], dy2, w_rows, T, D, nin, tm_mid, D, 512, "f32")
        else:
            dy = _mm_nt("od_out_t", dhb, sv["w_out"], dy2, w_rows, T, D, D, tm_mid, D, D, "f32")
            dwout = _mm_tn("od_dwout", sv["y"], dhb, xs2, ys2, os2, (D, D), T, D, D, tt, D, D)
            du3, dlb_rows[layer], small["gn"][j] = _hgrn_bwd(dy, sv["o"], sv["s0"], sv["u"], lb_all[layer][None],
                                                              od_gnorm_g[j][None])
            per = D // n_in_od

            def du_t(tt_, tn):
                return pl.BlockSpec((None, tt_, tn), lambda a, b, t: (b // per, t, b % per))

            dwin = _mm_tn("od_dwin", sv["n"], du3, xs2, du_t, os3, (N_DEV, D, n_in_od), T, D, 4 * D, tt, D, n_in_od)
            dn = _mm_nt("od_in_t", du3, sv["w_in"],
                        lambda tm, tn: pl.BlockSpec((None, tm, tn), lambda i, jj, k: (k // per, i, k % per)),
                        w_colblk, T, D, 4 * D, tm_mid, D, n_in_od, "f32")
        tok = rs_start(f"mix{layer}", [dwin, dwout])
        dh, dhb, g_mix[layer] = _rms_bwd(dn, sv["h0"], mix_norm_g[layer][None] + tok, dh)

    dmeta = dh[PAD:PAD + N_META]
    grad_x = dh[PAD + N_META:][None]
    dlb_param = _lb_bwd(lb_param, jnp.concatenate(dlb_rows, axis=0))

    pieces = [
        ("meta", dmeta), ("mix", jnp.concatenate(g_mix, 0)), ("mlp", jnp.concatenate(g_mlp, 0)), ("final", dg_final),
        ("cw", jnp.stack([c[:CONV_WIDTH] for c in small["cw"]])), ("cb", jnp.stack([v_[0] for v_ in small["vec"]])),
        ("lng", jnp.stack([v_[1] for v_ in small["vec"]])), ("lnb", jnp.stack([v_[2] for v_ in small["vec"]])),
        ("pw", jnp.stack(small["pw"])), ("pb", jnp.stack([v_[4] for v_ in small["vec"]])),
        ("ps", jnp.stack([v_[3] for v_ in small["vec"]])), ("gn", jnp.stack([jnp.sum(g_, axis=0)[0] for g_ in small["gn"]])),
        ("lb", dlb_param),
    ]
    flat = jnp.concatenate([p.reshape(-1) for _, p in pieces])
    n_small = flat.shape[0]
    rows_small = -(-n_small // 1024 // 8) * 8
    flat = jnp.pad(flat, (0, rows_small * 1024 - n_small)).reshape(rows_small, 1024)

    recv_small = _exchange([jnp.broadcast_to(flat[None], (N_DEV,) + flat.shape)], [(0, 0)])[0]
    recv = {}
    for tag, s_thru, l_thru, sems in rs_pending:
        got = _push_wait(f"rs_wait_{tag}", s_thru, l_thru, sems, recv_small, False)
        layer = int(tag[3:])
        if tag.startswith("mlp"):
            recv["w1", layer], recv["w2", layer] = got
        else:
            key = "ev" if layer % 2 == 0 else "od"
            recv[key + "_in", layer // 2], recv[key + "_out", layer // 2] = got

    outs = {}
    big = {"ev_in": ("ev_w_in", ev_w_in, m_ev_w_in, v_ev_w_in), "ev_out": ("ev_w_out", ev_w_out, m_ev_w_out, v_ev_w_out),
           "od_in": ("od_w_in", od_w_in, m_od_w_in, v_od_w_in), "od_out": ("od_w_out", od_w_out, m_od_w_out, v_od_w_out),
           "w1": ("mlp_w1", mlp_w1, m_mlp_w1, v_mlp_w1), "w2": ("mlp_w2", mlp_w2, m_mlp_w2, v_mlp_w2)}
    for key, (name, w, m, v) in big.items():
        res = None
        for l in range(w.shape[0]):
            res = _adamw(recv[key, l], w, m, v, layer=l, prev=res)
        outs[name] = res

    small_params = {
        "meta": ("meta_tokens", None), "mix": ("mix_norm_g", mix_norm_g, m_mix_norm_g, v_mix_norm_g),
        "mlp": ("mlp_norm_g", mlp_norm_g, m_mlp_norm_g, v_mlp_norm_g),
        "final": ("final_norm_g", final_norm_g, m_final_norm_g, v_final_norm_g),
        "cw": ("ev_conv_w", None), "cb": ("ev_conv_b", ev_conv_b, m_ev_conv_b, v_ev_conv_b),
        "lng": ("ev_ln_g", ev_ln_g, m_ev_ln_g, v_ev_ln_g), "lnb": ("ev_ln_b", ev_ln_b, m_ev_ln_b, v_ev_ln_b),
        "pw": ("ev_pool_w", ev_pool_w, m_ev_pool_w, v_ev_pool_w), "pb": ("ev_pool_b", ev_pool_b, m_ev_pool_b, v_ev_pool_b),
        "ps": ("ev_pool_scale", ev_pool_scale, m_ev_pool_scale, v_ev_pool_scale),
        "gn": ("od_gnorm_g", od_gnorm_g, m_od_gnorm_g, v_od_gnorm_g), "lb": ("lb_param", lb_param, m_lb_param, v_lb_param),
    }
    csh = ev_conv_w.shape[2]
    msh = meta_tokens.shape[1]

    def packed(which):
        parts = []
        for key, g_ in pieces:
            ent = small_params[key]
            if key == "meta":
                src = (meta_tokens, m_meta_tokens, v_meta_tokens)[which]
                full = lax.dynamic_update_slice(jnp.zeros((N_META, D), F32), src, (0, dev * msh))
            elif key == "cw":
                src = (ev_conv_w, m_ev_conv_w, v_ev_conv_w)[which]
                full = lax.dynamic_update_slice(jnp.zeros(g_.shape, F32), src, (0, 0, dev * csh))
            else:
                full = ent[1 + which]
            parts.append(full.reshape(-1))
        f = jnp.concatenate(parts)
        return jnp.pad(f, (0, rows_small * 1024 - n_small)).reshape(rows_small, 1024)

    sres = [r_[0] for r_ in _adamw(recv_small[:, 0], packed(0)[None], packed(1)[None], packed(2)[None])]
    off = 0
    for key, g_ in pieces:
        size = g_.size
        vals = [r_.reshape(-1)[off:off + size].reshape(g_.shape) for r_ in sres]
        off += size
        name = small_params[key][0]
        if key == "meta":
            vals = [lax.dynamic_slice(v_, (0, dev * msh), (N_META, msh)) for v_ in vals]
        elif key == "cw":
            vals = [lax.dynamic_slice(v_, (0, 0, dev * csh), v_.shape[:2] + (csh,)) for v_ in vals]
        else:
            vals = [v_.reshape(small_params[key][1].shape) for v_ in vals]
        outs[name] = vals

    names = ["meta_tokens", "mix_norm_g", "mlp_norm_g", "final_norm_g", "ev_w_in", "ev_conv_w", "ev_conv_b", "ev_ln_g",
             "ev_ln_b", "ev_pool_w", "ev_pool_b", "ev_pool_scale", "ev_w_out", "od_w_in", "od_gnorm_g", "od_w_out",
             "lb_param", "mlp_w1", "mlp_w2"]
    result = [loss, grad_x]
    for k in range(4):
        result += [outs[nm][k] for nm in names]
    return tuple(result)
```

```python
import functools

import jax
import jax.numpy as jnp
from jax import lax
from jax.experimental import pallas as pl
from jax.experimental.pallas import tpu as pltpu

F32 = jnp.float32
BF16 = jnp.bfloat16

N_DEV = 8
N_META = 16
CHUNK = 64
PAD = CHUNK - N_META
SUB = 16
HEAD = 128
CONV_WIDTH = 31
HALO = 32
POOL_WINDOWS = (2, 4, 8, 16)
EPS = 1e-6
NEG = -1e30
ADAM_LR, ADAM_B1, ADAM_B2, ADAM_EPS, ADAM_WD, ADAM_STEP = 0.001, 0.9, 0.999, 1e-08, 0.01, 10
VMEM_LIMIT = 56 * 1024 * 1024
EV_ROWS = 416
HGRN_ROWS = 832
MM_ROWS_BIG = 2080
MM_ROWS_MID = 1040
MM_ROWS_K4 = 416
HGRN_HEADS_PER_STEP = 2
MESH = pl.DeviceIdType.MESH
AXES = ("x", "y", "c")
ANY = pl.BlockSpec(memory_space=pl.ANY)


def _cp(*sem):
    return pltpu.CompilerParams(dimension_semantics=sem, vmem_limit_bytes=VMEM_LIMIT)


def _tile(n, cap, mult):
    best = None
    for d in range(mult, min(n, cap) + 1, mult):
        if n % d == 0:
            best = d
    assert best is not None, (n, cap, mult)
    return best


def _nt(a, b):
    return lax.dot_general(a, b, (((1,), (1,)), ((), ())), preferred_element_type=F32)


def _tn(a, b):
    return lax.dot_general(a, b, (((0,), (0,)), ((), ())), preferred_element_type=F32)


def _nn(a, b):
    return jnp.dot(a, b, preferred_element_type=F32)


def _r16(x):
    return x.astype(BF16).astype(F32)


def _row_ids(base, n):
    return base + lax.broadcasted_iota(jnp.int32, (n, 1), 0)


def _dsilu(x, s):
    return s * (1.0 + x * (1.0 - s))


def _rms_fwd(h, g):
    T, D = h.shape
    tm = _tile(T, MM_ROWS_MID, 16)

    def body(h_ref, g_ref, n_ref):
        x = h_ref[...]
        r = lax.rsqrt(jnp.mean(x * x, axis=-1, keepdims=True) + EPS)
        n_ref[...] = ((x * r) * g_ref[...]).astype(BF16)

    return pl.pallas_call(
        body, grid=(T // tm,),
        in_specs=[pl.BlockSpec((tm, D), lambda i: (i, 0)), pl.BlockSpec((1, D), lambda i: (0, 0))],
        out_specs=pl.BlockSpec((tm, D), lambda i: (i, 0)),
        out_shape=jax.ShapeDtypeStruct((T, D), BF16),
        compiler_params=_cp("parallel"), name="rms_fwd")(h, g)


def _rms_bwd(dn, h, g, dres):
    T, D = h.shape
    tm = _tile(T, MM_ROWS_MID, 16)

    def body(dn_ref, h_ref, g_ref, dres_ref, dh_ref, dhb_ref, dg_ref):
        i = pl.program_id(0)
        x = h_ref[...]
        dn_v = dn_ref[...]
        r = lax.rsqrt(jnp.mean(x * x, axis=-1, keepdims=True) + EPS)
        xh = x * r
        dxh = dn_v * g_ref[...]
        dx = r * (dxh - xh * jnp.mean(dxh * xh, axis=-1, keepdims=True))
        keep = _row_ids(i * tm, tm) >= PAD
        dh = jnp.where(keep, dres_ref[...] + dx, 0.0)
        dh_ref[...] = dh
        dhb_ref[...] = dh.astype(BF16)

        @pl.when(i == 0)
        def _():
            dg_ref[...] = jnp.zeros_like(dg_ref)

        dg_ref[...] += jnp.sum(dn_v * xh, axis=0, keepdims=True)

    row = pl.BlockSpec((tm, D), lambda i: (i, 0))
    vec = pl.BlockSpec((1, D), lambda i: (0, 0))
    return pl.pallas_call(
        body, grid=(T // tm,),
        in_specs=[row, row, vec, row], out_specs=[row, row, vec],
        out_shape=[jax.ShapeDtypeStruct((T, D), F32), jax.ShapeDtypeStruct((T, D), BF16),
                   jax.ShapeDtypeStruct((1, D), F32)],
        compiler_params=_cp("arbitrary"), name="rms_bwd")(dn, h, g, dres)


def _loss_head(h, g, tgt):
    T, D = h.shape
    tm = _tile(T, MM_ROWS_MID, 16)
    first_x = PAD + N_META

    def body(h_ref, g_ref, t_ref, loss_ref, dh_ref, dhb_ref, dg_ref):
        i = pl.program_id(0)
        x = h_ref[...]
        r = lax.rsqrt(jnp.mean(x * x, axis=-1, keepdims=True) + EPS)
        xh = x * r
        gv = g_ref[...]
        out = xh * gv
        valid = _row_ids(i * tm, tm) >= first_x
        e = jnp.where(valid, out - t_ref[...], 0.0)
        dout = e * (1.0 / D)
        dxh = dout * gv
        dx = r * (dxh - xh * jnp.mean(dxh * xh, axis=-1, keepdims=True))
        dh_ref[...] = dx
        dhb_ref[...] = dx.astype(BF16)

        @pl.when(i == 0)
        def _():
            dg_ref[...] = jnp.zeros_like(dg_ref)
            loss_ref[...] = jnp.zeros_like(loss_ref)

        dg_ref[...] += jnp.sum(dout * xh, axis=0, keepdims=True)
        loss_ref[...] += 0.5 * jnp.sum(jnp.mean(e * e, axis=-1, keepdims=True))

    row = pl.BlockSpec((tm, D), lambda i: (i, 0))
    vec = pl.BlockSpec((1, D), lambda i: (0, 0))
    return pl.pallas_call(
        body, grid=(T // tm,),
        in_specs=[row, vec, row],
        out_specs=[pl.BlockSpec((8, 128), lambda i: (0, 0)), row, row, vec],
        out_shape=[jax.ShapeDtypeStruct((8, 128), F32), jax.ShapeDtypeStruct((T, D), F32),
                   jax.ShapeDtypeStruct((T, D), BF16), jax.ShapeDtypeStruct((1, D), F32)],
        compiler_params=_cp("arbitrary"), name="loss_head")(h, g, tgt)


def _mm_nn(name, a, w, w_spec, M, N, K, tm, tn, tk, mode, extra=None, a_spec=None):
    nk = K // tk
    if a_spec is None:
        a_spec = pl.BlockSpec((tm, tk), lambda i, j, k: (i, k))
    o_spec = pl.BlockSpec((tm, tn), lambda i, j, k: (i, j))

    def body(*refs):
        if mode == "resid":
            a_ref, w_ref, e_ref = refs[:3]
            outs = refs[3:]
        else:
            a_ref, w_ref = refs[:2]
            outs = refs[2:]
        acc_ref = outs[-1] if nk > 1 else None
        part = _nn(a_ref[...], w_ref[...])

        def finish(acc):
            if mode == "f32":
                outs[0][...] = acc
            elif mode == "relu2":
                r = jnp.maximum(acc, 0.0)
                outs[0][...] = r.astype(BF16)
                outs[1][...] = (r * r).astype(BF16)
            else:
                keep = _row_ids(pl.program_id(0) * tm, tm) >= PAD
                outs[0][...] = jnp.where(keep, e_ref[...] + acc, 0.0)

        if nk == 1:
            finish(part)
        else:
            k = pl.program_id(2)

            @pl.when(k == 0)
            def _():
                acc_ref[...] = part

            @pl.when(k > 0)
            def _():
                acc_ref[...] += part

            @pl.when(k == nk - 1)
            def _():
                finish(acc_ref[...])

    in_specs = [a_spec, w_spec(tk, tn)]
    args = [a, w]
    if mode == "resid":
        in_specs.append(o_spec)
        args.append(extra)
    if mode == "relu2":
        out_specs = [o_spec, o_spec]
        out_shape = [jax.ShapeDtypeStruct((M, N), BF16)] * 2
    else:
        out_specs = [o_spec]
        out_shape = [jax.ShapeDtypeStruct((M, N), F32)]
    scratch = [pltpu.VMEM((tm, tn), F32)] if nk > 1 else []
    res = pl.pallas_call(
        body, grid=(M // tm, N // tn, nk), in_specs=in_specs, out_specs=out_specs, out_shape=out_shape,
        scratch_shapes=scratch, compiler_params=_cp("parallel", "parallel", "arbitrary"), name=name)(*args)
    return res if mode == "relu2" else res[0]


def _mm_nt(name, dy, w, dy_spec, w_spec, M, J, N, tm, tj, tn, mode, extra=None, parts=1):
    nk = N // tn
    o_spec = pl.BlockSpec((tm, tj), lambda i, j, k: (i, j))
    n_extra = {"f32": 0, "dact": 1, "rms": 3}[mode]
    if mode == "rms":
        assert nk == 1 and tj == J

    def body(*refs):
        dy_ref, w_ref = refs[:2]
        ex = refs[2:2 + n_extra]
        outs = refs[2 + n_extra:]
        acc_ref = outs[-1] if nk > 1 else None
        if parts == 1:
            part = _nt(dy_ref[...], w_ref[...])
        else:
            wq = tn // parts
            part = _nt(dy_ref[0], w_ref[:, 0:wq])
            for q in range(1, parts):
                part = part + _nt(dy_ref[q], w_ref[:, q * wq:(q + 1) * wq])

        def finish(acc):
            if mode == "f32":
                outs[0][...] = acc
            elif mode == "dact":
                outs[0][...] = (acc * (2.0 * ex[0][...].astype(F32))).astype(BF16)
            else:
                h_ref, g_ref, dres_ref = ex
                dh_ref, dhb_ref, dg_ref = outs[:3]
                i = pl.program_id(0)
                x = h_ref[...]
                r = lax.rsqrt(jnp.mean(x * x, axis=-1, keepdims=True) + EPS)
                xh = x * r
                dxh = acc * g_ref[...]
                dx = r * (dxh - xh * jnp.mean(dxh * xh, axis=-1, keepdims=True))
                keep = _row_ids(i * tm, tm) >= PAD
                dh = jnp.where(keep, dres_ref[...] + dx, 0.0)
                dh_ref[...] = dh
                dhb_ref[...] = dh.astype(BF16)

                @pl.when(i == 0)
                def _():
                    dg_ref[...] = jnp.zeros_like(dg_ref)

                dg_ref[...] += jnp.sum(acc * xh, axis=0, keepdims=True)

        if nk == 1:
            finish(part)
        else:
            k = pl.program_id(2)

            @pl.when(k == 0)
            def _():
                acc_ref[...] = part

            @pl.when(k > 0)
            def _():
                acc_ref[...] += part

            @pl.when(k == nk - 1)
            def _():
                finish(acc_ref[...])

    in_specs = [dy_spec(tm, tn), w_spec(tj, tn)]
    args = [dy, w]
    scratch = [pltpu.VMEM((tm, tj), F32)] if nk > 1 else []
    if mode == "rms":
        vec = pl.BlockSpec((1, J), lambda i, j, k: (0, 0))
        h, g, dres = extra
        res = pl.pallas_call(
            body, grid=(M // tm, 1, 1), in_specs=in_specs + [o_spec, vec, o_spec], out_specs=[o_spec, o_spec, vec],
            out_shape=[jax.ShapeDtypeStruct((M, J), F32), jax.ShapeDtypeStruct((M, J), BF16),
                       jax.ShapeDtypeStruct((1, J), F32)],
            compiler_params=_cp("arbitrary", "arbitrary", "arbitrary"), name=name)(*args, h, g, dres)
        return res
    if mode == "dact":
        in_specs.append(o_spec)
        args.append(extra)
    return pl.pallas_call(
        body, grid=(M // tm, J // tj, nk), in_specs=in_specs, out_specs=[o_spec],
        out_shape=[jax.ShapeDtypeStruct((M, J), BF16 if mode == "dact" else F32)],
        scratch_shapes=scratch, compiler_params=_cp("parallel", "parallel", "arbitrary"), name=name)(*args)[0]


def _mm_tn(name, x, dy, x_spec, dy_spec, o_spec, o_shape, T, K, N, tt, tk, tn):
    nt = T // tt

    def body(x_ref, dy_ref, o_ref, *acc):
        part = _tn(x_ref[...], dy_ref[...])
        if nt == 1:
            o_ref[...] = part.astype(BF16)
            return
        acc_ref = acc[0]
        t = pl.program_id(2)

        @pl.when(t == 0)
        def _():
            acc_ref[...] = part

        @pl.when(t > 0)
        def _():
            acc_ref[...] += part

        @pl.when(t == nt - 1)
        def _():
            o_ref[...] = acc_ref[...].astype(BF16)

    return pl.pallas_call(
        body, grid=(K // tk, N // tn, nt), in_specs=[x_spec(tt, tk), dy_spec(tt, tn)], out_specs=o_spec(tk, tn),
        out_shape=jax.ShapeDtypeStruct(o_shape, BF16), scratch_shapes=[pltpu.VMEM((tk, tn), F32)] if nt > 1 else [],
        compiler_params=_cp("parallel", "parallel", "arbitrary"), name=name)(x, dy)


def _pool_counts(base, n, w):
    pos = _row_ids(base, n) - PAD
    return jnp.clip(pos + 1, 1, w).astype(F32)


def _ev_fwd(u, cw, cb, lg, lb, pw, pb, ps):
    T = u.shape[0]
    C = 512
    tm = _tile(T, EV_ROWS, HALO)
    nsub = tm // HALO
    hb = tm // HALO

    def body(val_ref, gate_ref, pin_ref, valh_ref, gateh_ref, pinh_ref, cw_ref, cb_ref, lg_ref, lb_ref, pw_ref,
             pb_ref, ps_ref, yab_ref, yc_ref, a_ext, p_ext, d_buf):
        i = pl.program_id(0)
        nf = (i > 0).astype(F32)
        a_ext[0:HALO, :] = valh_ref[...] * jax.nn.sigmoid(gateh_ref[...]) * nf
        a_ext[HALO:, :] = val_ref[...] * jax.nn.sigmoid(gate_ref[...])
        p_ext[0:HALO, :] = pinh_ref[...] * nf
        p_ext[HALO:, :] = pin_ref[...]

        def sub(s, carry):
            base = pl.multiple_of(s * HALO, HALO)
            win = a_ext[pl.ds(base, 2 * HALO), :]
            acc = jnp.zeros((HALO, C), F32) + cb_ref[...]
            for j in range(CONV_WIDTH):
                acc = acc + cw_ref[pl.ds(j, 1), :] * win[2 + j:2 + j + HALO]
            yc_ref[pl.ds(base, HALO), :] = acc
            mu = jnp.mean(acc, axis=-1, keepdims=True)
            yc = acc - mu
            rstd = lax.rsqrt(jnp.mean(yc * yc, axis=-1, keepdims=True) + EPS)
            z = (yc * rstd) * lg_ref[...] + lb_ref[...]
            yab_ref[pl.ds(base, HALO), 0:C] = (z * jax.nn.sigmoid(z)).astype(BF16)
            pwin = p_ext[pl.ds(base, 2 * HALO), :]
            for gi, w in enumerate(POOL_WINDOWS):
                lo, hi = gi * HEAD, (gi + 1) * HEAD
                x = pwin[HALO:, lo:hi]
                tot = x
                for k in range(1, w):
                    tot = tot + pwin[HALO - k:2 * HALO - k, lo:hi]
                cnt = _pool_counts(i * tm + base, HALO, w)
                d_buf[pl.ds(base, HALO), lo:hi] = (tot / cnt - x).astype(BF16)
            return carry

        lax.fori_loop(0, nsub, sub, 0)
        for gi in range(len(POOL_WINDOWS)):
            lo, hi = gi * HEAD, (gi + 1) * HEAD
            y = _nn(d_buf[:, lo:hi], pw_ref[gi]) + pb_ref[:, lo:hi]
            yab_ref[:, C + lo:C + hi] = (y * ps_ref[:, lo:hi]).astype(BF16)

    def main(c):
        return pl.BlockSpec((tm, C), lambda i: (i, c))

    def halo(c):
        return pl.BlockSpec((HALO, C), lambda i: (jnp.maximum(i * hb - 1, 0), c))

    vec = pl.BlockSpec((1, C), lambda i: (0, 0))
    return pl.pallas_call(
        body, grid=(T // tm,),
        in_specs=[main(0), main(1), main(2), halo(0), halo(1), halo(2),
                  pl.BlockSpec((32, C), lambda i: (0, 0)), vec, vec, vec,
                  pl.BlockSpec((4, HEAD, HEAD), lambda i: (0, 0, 0)), vec, vec],
        out_specs=[pl.BlockSpec((tm, 2 * C), lambda i: (i, 0)), pl.BlockSpec((tm, C), lambda i: (i, 0))],
        out_shape=[jax.ShapeDtypeStruct((T, 2 * C), BF16), jax.ShapeDtypeStruct((T, C), F32)],
        scratch_shapes=[pltpu.VMEM((tm + HALO, C), F32), pltpu.VMEM((tm + HALO, C), F32), pltpu.VMEM((tm, C), BF16)],
        compiler_params=_cp("parallel"), name="ev_fwd")(u, u, u, u, u, u, cw, cb, lg, lb, pw, pb, ps)


def _ev_bwd(dyab, yc, u, cw, lg, lb, pw, pwt, pb, ps):
    T = u.shape[0]
    C = 512
    tm = _tile(T, EV_ROWS, HALO)
    nsub = tm // HALO
    hb = tm // HALO
    nblk = T // tm
    E = tm + HALO

    def body(dya_ref, dyb_ref, dyah_ref, dybh_ref, yc_ref, ych_ref, val_ref, gate_ref, pin_ref, valh_ref, gateh_ref,
             pinh_ref, cw_ref, lg_ref, lb_ref, pw_ref, pwt_ref, pb_ref, ps_ref,
             du_ref, dcw_ref, dvec_ref, dpw_ref,
             dy_ext, a_ext, p_ext, ddc_ext, dd_buf, d_buf, dpre_buf, dcw_acc, vec_acc):
        i = pl.program_id(0)
        nf = (i > 0).astype(F32)
        nl = (i < nblk - 1).astype(F32)

        @pl.when(i == 0)
        def _():
            dcw_ref[...] = jnp.zeros_like(dcw_ref)
            dvec_ref[...] = jnp.zeros_like(dvec_ref)
            dpw_ref[...] = jnp.zeros_like(dpw_ref)

        dcw_acc[...] = jnp.zeros_like(dcw_acc)
        vec_acc[...] = jnp.zeros_like(vec_acc)
        a_ext[0:HALO, :] = valh_ref[...] * jax.nn.sigmoid(gateh_ref[...]) * nf
        a_ext[HALO:, :] = val_ref[...] * jax.nn.sigmoid(gate_ref[...])
        p_ext[0:HALO, :] = pinh_ref[...] * nf
        p_ext[HALO:, :] = pin_ref[...]

        def ln_bwd(y, dya, main):
            mu = jnp.mean(y, axis=-1, keepdims=True)
            ycen = y - mu
            rstd = lax.rsqrt(jnp.mean(ycen * ycen, axis=-1, keepdims=True) + EPS)
            yh = ycen * rstd
            z = yh * lg_ref[...] + lb_ref[...]
            sz = jax.nn.sigmoid(z)
            dz = dya * _dsilu(z, sz)
            dyh = dz * lg_ref[...]
            dy = rstd * (dyh - jnp.mean(dyh, axis=-1, keepdims=True) - yh * jnp.mean(dyh * yh, axis=-1, keepdims=True))
            if main:
                vec_acc[1] += jnp.sum((dz * yh).reshape(HALO // 8, 8, C), axis=0)
                vec_acc[2] += jnp.sum(dz.reshape(HALO // 8, 8, C), axis=0)
                vec_acc[0] += jnp.sum(dy.reshape(HALO // 8, 8, C), axis=0)
            return dy

        def pool_dd(dyb, base, main):
            dpre = dyb * ps_ref[...]
            for gi, w in enumerate(POOL_WINDOWS):
                lo, hi = gi * HEAD, (gi + 1) * HEAD
                dd = _nn(dpre[:, lo:hi].astype(BF16), pwt_ref[gi])
                cnt = _pool_counts(i * tm + base, HALO, w)
                ddc_ext[pl.ds(base, HALO), lo:hi] = dd / cnt
                if main:
                    dd_buf[pl.ds(base, HALO), lo:hi] = dd
            if main:
                dpre_buf[pl.ds(base, HALO), :] = dpre.astype(BF16)
                vec_acc[4] += jnp.sum(dpre.reshape(HALO // 8, 8, C), axis=0)

        def p1(s, carry):
            base = pl.multiple_of(s * HALO, HALO)
            dy_ext[pl.ds(base, HALO), :] = ln_bwd(yc_ref[pl.ds(base, HALO), :], dya_ref[pl.ds(base, HALO), :], True)
            pool_dd(dyb_ref[pl.ds(base, HALO), :], base, True)
            return carry

        lax.fori_loop(0, nsub, p1, 0)
        dy_ext[tm:, :] = ln_bwd(ych_ref[...], dyah_ref[...], False) * nl
        dpre_h = dybh_ref[...] * ps_ref[...] * nl
        for gi, w in enumerate(POOL_WINDOWS):
            lo, hi = gi * HEAD, (gi + 1) * HEAD
            dd = _nn(dpre_h[:, lo:hi].astype(BF16), pwt_ref[gi])
            ddc_ext[tm:, lo:hi] = dd / _pool_counts(i * tm + tm, HALO, w)

        def p2(s, carry):
            base = pl.multiple_of(s * HALO, HALO)
            dwin = dy_ext[pl.ds(base, 2 * HALO), :]
            awin = a_ext[pl.ds(base, 2 * HALO), :]
            dy_m = dwin[0:HALO]
            da = jnp.zeros((HALO, C), F32)
            for j in range(CONV_WIDTH):
                sh = CONV_WIDTH - 1 - j
                da = da + cw_ref[pl.ds(j, 1), :] * dwin[sh:sh + HALO]
                dcw_acc[j] += jnp.sum((dy_m * awin[2 + j:2 + j + HALO]).reshape(HALO // 8, 8, C), axis=0)
            v = val_ref[pl.ds(base, HALO), :]
            g = gate_ref[pl.ds(base, HALO), :]
            sg = jax.nn.sigmoid(g)
            du_ref[pl.ds(base, HALO), 0:C] = (da * sg).astype(BF16)
            du_ref[pl.ds(base, HALO), C:2 * C] = (da * v * sg * (1.0 - sg)).astype(BF16)
            pwin = p_ext[pl.ds(base, 2 * HALO), :]
            cwin = ddc_ext[pl.ds(base, 2 * HALO), :]
            for gi, w in enumerate(POOL_WINDOWS):
                lo, hi = gi * HEAD, (gi + 1) * HEAD
                x = pwin[HALO:, lo:hi]
                tot = x
                back = cwin[0:HALO, lo:hi]
                for k in range(1, w):
                    tot = tot + pwin[HALO - k:2 * HALO - k, lo:hi]
                    back = back + cwin[k:k + HALO, lo:hi]
                cnt = _pool_counts(i * tm + base, HALO, w)
                d_buf[pl.ds(base, HALO), lo:hi] = (tot / cnt - x).astype(BF16)
                du_ref[pl.ds(base, HALO), 2 * C + lo:2 * C + hi] = (back - dd_buf[pl.ds(base, HALO), lo:hi]).astype(BF16)
            return carry

        lax.fori_loop(0, nsub, p2, 0)
        for gi in range(len(POOL_WINDOWS)):
            lo, hi = gi * HEAD, (gi + 1) * HEAD
            pre = _nn(d_buf[:, lo:hi], pw_ref[gi]) + pb_ref[:, lo:hi]
            vec_acc[3, :, lo:hi] += jnp.sum((dyb_ref[:, lo:hi] * pre).reshape(tm // 8, 8, HEAD), axis=0)
            dpw_ref[gi] += _tn(d_buf[:, lo:hi], dpre_buf[:, lo:hi])
        for j in range(CONV_WIDTH):
            dcw_ref[pl.ds(j, 1), :] += jnp.sum(dcw_acc[j], axis=0, keepdims=True)
        for r in range(5):
            dvec_ref[pl.ds(r, 1), :] += jnp.sum(vec_acc[r], axis=0, keepdims=True)

    def main(c, width=C):
        return pl.BlockSpec((tm, width), lambda i: (i, c))

    def prev(c):
        return pl.BlockSpec((HALO, C), lambda i: (jnp.maximum(i * hb - 1, 0), c))

    def nxt(c):
        return pl.BlockSpec((HALO, C), lambda i: (jnp.minimum((i + 1) * hb, T // HALO - 1), c))

    vec = pl.BlockSpec((1, C), lambda i: (0, 0))
    mat = pl.BlockSpec((4, HEAD, HEAD), lambda i: (0, 0, 0))
    return pl.pallas_call(
        body, grid=(nblk,),
        in_specs=[main(0), main(1), nxt(0), nxt(1), main(0), nxt(0), main(0), main(1), main(2), prev(0), prev(1),
                  prev(2), pl.BlockSpec((32, C), lambda i: (0, 0)), vec, vec, mat, mat, vec, vec],
        out_specs=[pl.BlockSpec((tm, 3 * C), lambda i: (i, 0)), pl.BlockSpec((32, C), lambda i: (0, 0)),
                   pl.BlockSpec((8, C), lambda i: (0, 0)), mat],
        out_shape=[jax.ShapeDtypeStruct((T, 3 * C), BF16), jax.ShapeDtypeStruct((32, C), F32),
                   jax.ShapeDtypeStruct((8, C), F32), jax.ShapeDtypeStruct((4, HEAD, HEAD), F32)],
        scratch_shapes=[pltpu.VMEM((E, C), F32), pltpu.VMEM((E, C), F32), pltpu.VMEM((E, C), F32),
                        pltpu.VMEM((E, C), F32), pltpu.VMEM((tm, C), F32), pltpu.VMEM((tm, C), BF16),
                        pltpu.VMEM((tm, C), BF16), pltpu.VMEM((32, 8, C), F32), pltpu.VMEM((8, 8, C), F32)],
        compiler_params=_cp("arbitrary"), name="ev_bwd")(
            dyab, dyab, dyab, dyab, yc, yc, u, u, u, u, u, u, cw, lg, lb, pw, pwt, pb, ps)


def _cumsum_rows(x, reverse=False):
    n = x.shape[0]
    rid = lax.broadcasted_iota(jnp.int32, (n, 1), 0)
    k = 1
    while k < n:
        if reverse:
            sh = jnp.where(rid < n - k, pltpu.roll(x, n - k, 0), 0.0)
        else:
            sh = jnp.where(rid >= k, pltpu.roll(x, k, 0), 0.0)
        x = x + sh
        k *= 2
    return x


def _hgrn_gates(qr, fr, lbv):
    sq = jax.nn.sigmoid(qr)
    sg = jax.nn.sigmoid(fr)
    fg = lbv + (1.0 - lbv) * sg
    return qr * sq, sq, sg, fg, 1.0 - fg, jnp.log(fg)


def _hgrn_fwd(u, lbv, gn):
    T = u.shape[0]
    H = 8
    RB = _tile(T, HGRN_ROWS, CHUNK)
    NC = RB // CHUNK
    NS = CHUNK // SUB

    HP = HGRN_HEADS_PER_STEP
    W = HP * HEAD

    def body(q_ref, f_ref, i_ref, g_ref, lb_ref, gn_ref, y_ref, o_ref, s0_ref, st, qs, ks, bs, vs, os_):
        rb = pl.program_id(1)

        @pl.when(rb == 0)
        def _():
            st[...] = jnp.zeros_like(st)

        tsub = lax.broadcasted_iota(jnp.int32, (SUB, 1), 0)

        def head(hh, c, rows):
            sl = slice(hh * HEAD, (hh + 1) * HEAD)
            q, _, _, _, kk, lf = _hgrn_gates(q_ref[rows, sl], f_ref[rows, sl], lb_ref[:, sl])
            v = i_ref[rows, sl]
            b = _cumsum_rows(lf)
            qs[hh] = q
            ks[hh] = kk
            bs[hh] = b
            vs[hh] = v
            st0 = st[hh]
            s0_ref[hh, c] = st0
            os_[hh] = _nt((q * jnp.exp(b)).astype(BF16), st0.astype(BF16))
            for I in range(NS):
                lo = I * SUB
                qI = qs[hh, lo:lo + SUB, :]
                bI = bs[hh, lo:lo + SUB, :]
                oI = jnp.zeros((SUB, HEAD), F32)
                if I > 0:
                    bprev = bs[hh, pl.ds(lo - 1, 1), :]
                    qt = _r16(qI * jnp.exp(bI - bprev))
                    kt = _r16(ks[hh, 0:lo, :] * jnp.exp(bprev - bs[hh, 0:lo, :]))
                    A = _nt(qt, kt)
                    oI = oI + _nn(_r16(A), _r16(vs[hh, 0:lo, :]))
                for s in range(SUB):
                    row = pl.ds(lo + s, 1)
                    Es = jnp.exp(jnp.where(tsub >= s, bI - bs[hh, row, :], NEG))
                    col = jnp.sum(qI * Es * ks[hh, row, :], axis=1, keepdims=True)
                    oI = oI + col * vs[hh, row, :]
                os_[hh, lo:lo + SUB, :] += oI
            blast = bs[hh, pl.ds(CHUNK - 1, 1), :]
            kh = kk * jnp.exp(blast - b)
            st[hh] = st0 * jnp.exp(blast) + _tn(v.astype(BF16), kh.astype(BF16))
            o = os_[hh]
            o_ref[rows, sl] = o
            rr = lax.rsqrt(jnp.mean(o * o, axis=-1, keepdims=True) + EPS)
            gr = g_ref[rows, sl]
            y_ref[rows, sl] = (((o * rr) * gn_ref[...]) * (gr * jax.nn.sigmoid(gr))).astype(BF16)

        def chunk(c, carry):
            rows = pl.ds(pl.multiple_of(c * CHUNK, CHUNK), CHUNK)
            for hh in range(HP):
                head(hh, c, rows)
            return carry

        lax.fori_loop(0, NC, chunk, 0)

    def blk(q):
        return pl.BlockSpec((RB, W), lambda h, r: (r, q * (H // HP) + h))

    sc = lambda: pltpu.VMEM((HP, CHUNK, HEAD), F32)
    return pl.pallas_call(
        body, grid=(H // HP, T // RB),
        in_specs=[blk(0), blk(1), blk(2), blk(3), pl.BlockSpec((1, W), lambda h, r: (0, h)),
                  pl.BlockSpec((1, HEAD), lambda h, r: (0, 0))],
        out_specs=[pl.BlockSpec((RB, W), lambda h, r: (r, h)), pl.BlockSpec((RB, W), lambda h, r: (r, h)),
                   pl.BlockSpec((HP, NC, HEAD, HEAD), lambda h, r: (h, r, 0, 0))],
        out_shape=[jax.ShapeDtypeStruct((T, H * HEAD), BF16), jax.ShapeDtypeStruct((T, H * HEAD), F32),
                   jax.ShapeDtypeStruct((H, T // CHUNK, HEAD, HEAD), F32)],
        scratch_shapes=[pltpu.VMEM((HP, HEAD, HEAD), F32), sc(), sc(), sc(), sc(), sc()],
        compiler_params=_cp("parallel", "arbitrary"), name="hgrn_fwd")(u, u, u, u, lbv, gn)


def _hgrn_bwd(dy, o, s0, u, lbv, gn):
    T = u.shape[0]
    H = 8
    RB = _tile(T, HGRN_ROWS, CHUNK)
    NB = T // RB
    NC = RB // CHUNK
    NS = CHUNK // SUB

    def body(q_ref, f_ref, i_ref, g_ref, lb_ref, gn_ref, o_ref, dy_ref, s0_ref, du_ref, dlb_ref, dgn_ref,
             dst, qs, ks, bs, vs, dos, dqs, dks, dki, dvs, dbs):
        rb = pl.program_id(1)

        @pl.when(rb == 0)
        def _():
            dst[...] = jnp.zeros_like(dst)
            dlb_ref[...] = jnp.zeros_like(dlb_ref)
            dgn_ref[...] = jnp.zeros_like(dgn_ref)

        tsub = lax.broadcasted_iota(jnp.int32, (SUB, 1), 0)
        lane = lax.broadcasted_iota(jnp.int32, (SUB, HEAD), 1)
        gnv = gn_ref[...]

        def head(hh, c, rows):
            sl = slice(hh * HEAD, (hh + 1) * HEAD)
            lbv_ = lb_ref[:, sl]
            qr = q_ref[rows, sl]
            q, sq, sg, fg, kk, lf = _hgrn_gates(qr, f_ref[rows, sl], lbv_)
            v = i_ref[rows, sl]
            gr = g_ref[rows, sl]
            b = _cumsum_rows(lf)
            eb = jnp.exp(b)
            ov = o_ref[rows, sl]
            dyv = dy_ref[rows, sl]
            rr = lax.rsqrt(jnp.mean(ov * ov, axis=-1, keepdims=True) + EPS)
            oh = ov * rr
            gs = jax.nn.sigmoid(gr)
            dgr = dyv * (oh * gnv) * _dsilu(gr, gs)
            dnrm = dyv * (gr * gs)
            dgn_ref[hh] += jnp.sum(dnrm * oh, axis=0, keepdims=True)
            t1 = dnrm * gnv
            do = rr * (t1 - oh * jnp.mean(t1 * oh, axis=-1, keepdims=True))
            qs[hh] = q
            ks[hh] = kk
            bs[hh] = b
            vs[hh] = v
            dos[hh] = do
            st0 = s0_ref[hh, c]
            dS = dst[hh]
            do_b = do.astype(BF16)
            blast = bs[hh, pl.ds(CHUNK - 1, 1), :]
            elast = jnp.exp(blast - b)
            dq_inter = _nn(do_b, st0.astype(BF16)) * eb
            dqs[hh] = dq_inter
            dbs[hh] = q * dq_inter
            kh = kk * elast
            dvs[hh] = _nt(kh.astype(BF16), dS.astype(BF16))
            dk_inter = _nn(v.astype(BF16), dS.astype(BF16)) * elast
            dki[hh] = dk_inter
            dks[hh] = jnp.zeros((CHUNK, HEAD), F32)
            for I in range(NS):
                lo = I * SUB
                qI = qs[hh, lo:lo + SUB, :]
                bI = bs[hh, lo:lo + SUB, :]
                doI = dos[hh, lo:lo + SUB, :]
                dqI = jnp.zeros((SUB, HEAD), F32)
                dbI = jnp.zeros((SUB, HEAD), F32)
                if I > 0:
                    bprev = bs[hh, pl.ds(lo - 1, 1), :]
                    eq = jnp.exp(bI - bprev)
                    ek = jnp.exp(bprev - bs[hh, 0:lo, :])
                    qt = _r16(qI * eq)
                    kt = _r16(ks[hh, 0:lo, :] * ek)
                    A = _r16(_nt(qt, kt))
                    doI_b = _r16(doI)
                    dA = _r16(_nt(doI_b, _r16(vs[hh, 0:lo, :])))
                    dvs[hh, 0:lo, :] += _tn(A, doI_b)
                    dqt = _nn(dA, kt)
                    dkt = _tn(dA, qt)
                    dqI = dqI + dqt * eq
                    dbI = dbI + qt.astype(F32) * dqt
                    dks[hh, 0:lo, :] += dkt * ek
                    dbs[hh, 0:lo, :] -= kt.astype(F32) * dkt
                dq_d = jnp.zeros((SUB, HEAD), F32)
                a_d = jnp.zeros((SUB, HEAD), F32)
                for s in range(SUB):
                    row = pl.ds(lo + s, 1)
                    krow = ks[hh, row, :]
                    Es = jnp.exp(jnp.where(tsub >= s, bI - bs[hh, row, :], NEG))
                    qE = qI * Es
                    col = jnp.sum(qE * krow, axis=1, keepdims=True)
                    a_d = jnp.where(lane == s, col, a_d)
                    dcol = jnp.sum(doI * vs[hh, row, :], axis=1, keepdims=True)
                    dq_d = dq_d + (dcol * Es) * krow
                    dk_s = jnp.sum(dcol * qE, axis=0, keepdims=True)
                    dks[hh, row, :] += dk_s
                    dbs[hh, row, :] -= krow * dk_s
                dvs[hh, lo:lo + SUB, :] += _tn(a_d, doI)[0:SUB]
                dqI = dqI + dq_d
                dbI = dbI + qI * dq_d
                dqs[hh, lo:lo + SUB, :] += dqI
                dbs[hh, lo:lo + SUB, :] += dbI
            kdk = kk * dki[hh]
            excl = _cumsum_rows(kdk) - kdk
            suff = _cumsum_rows(dbs[hh], reverse=True)
            gdec = jnp.sum(dS * st0, axis=0, keepdims=True) * jnp.exp(blast)
            dlf = suff + excl + gdec
            dk = dks[hh] + dki[hh]
            dfg = dlf / fg - dk
            dlb_ref[:, sl] += jnp.sum(dfg * (1.0 - sg), axis=0, keepdims=True)
            du_ref[0, rows, sl] = (dqs[hh] * _dsilu(qr, sq)).astype(BF16)
            du_ref[1, rows, sl] = (dfg * (1.0 - lbv_) * sg * (1.0 - sg)).astype(BF16)
            du_ref[2, rows, sl] = dvs[hh].astype(BF16)
            du_ref[3, rows, sl] = dgr.astype(BF16)
            dst[hh] = dS * jnp.exp(blast) + _tn(do_b, (q * eb).astype(BF16))

        def chunk(cc, carry):
            c = NC - 1 - cc
            rows = pl.ds(pl.multiple_of(c * CHUNK, CHUNK), CHUNK)
            for hh in range(HP):
                head(hh, c, rows)
            return carry

        lax.fori_loop(0, NC, chunk, 0)

    HP = HGRN_HEADS_PER_STEP
    W = HP * HEAD

    def blk(qd):
        return pl.BlockSpec((RB, W), lambda h, r: (NB - 1 - r, qd * (H // HP) + h))

    hblk = pl.BlockSpec((RB, W), lambda h, r: (NB - 1 - r, h))
    sc = lambda: pltpu.VMEM((HP, CHUNK, HEAD), F32)
    return pl.pallas_call(
        body, grid=(H // HP, NB),
        in_specs=[blk(0), blk(1), blk(2), blk(3), pl.BlockSpec((1, W), lambda h, r: (0, h)),
                  pl.BlockSpec((1, HEAD), lambda h, r: (0, 0)), hblk, hblk,
                  pl.BlockSpec((HP, NC, HEAD, HEAD), lambda h, r: (h, NB - 1 - r, 0, 0))],
        out_specs=[pl.BlockSpec((4, RB, W), lambda h, r: (0, NB - 1 - r, h)),
                   pl.BlockSpec((1, W), lambda h, r: (0, h)), pl.BlockSpec((HP, 1, HEAD), lambda h, r: (h, 0, 0))],
        out_shape=[jax.ShapeDtypeStruct((4, T, H * HEAD), BF16), jax.ShapeDtypeStruct((1, H * HEAD), F32),
                   jax.ShapeDtypeStruct((H, 1, HEAD), F32)],
        scratch_shapes=[pltpu.VMEM((HP, HEAD, HEAD), F32)] + [sc() for _ in range(10)],
        compiler_params=_cp("parallel", "arbitrary"), name="hgrn_bwd")(u, u, u, u, lbv, gn, o, dy, s0)


def _softmax_rows(p_ref, L):
    rows = [p_ref[pl.ds(l, 1), :] for l in range(L)]
    m = rows[0]
    for r in rows[1:]:
        m = jnp.maximum(m, r)
    e = [jnp.exp(r - m) for r in rows]
    tot = e[0]
    for t in e[1:]:
        tot = tot + t
    return [t / tot for t in e]


def _lb_fwd(lbp):
    L, D = lbp.shape

    def body(p_ref, o_ref):
        sm = _softmax_rows(p_ref, L)
        acc = jnp.zeros((1, D), F32)
        o_ref[pl.ds(0, 1), :] = acc
        for l in range(1, L):
            acc = acc + sm[l]
            o_ref[pl.ds(l, 1), :] = acc

    return pl.pallas_call(body, out_shape=jax.ShapeDtypeStruct((L, D), F32), name="lb_fwd")(lbp)


def _lb_bwd(lbp, dlb):
    L, D = lbp.shape

    def body(p_ref, d_ref, o_ref):
        sm = _softmax_rows(p_ref, L)
        dsm = [jnp.zeros((1, D), F32)]
        for i in range(1, L):
            t = jnp.zeros((1, D), F32)
            for l in range(i, L):
                t = t + d_ref[pl.ds(l, 1), :]
            dsm.append(t)
        dot = jnp.zeros((1, D), F32)
        for i in range(L):
            dot = dot + dsm[i] * sm[i]
        for i in range(L):
            o_ref[pl.ds(i, 1), :] = sm[i] * (dsm[i] - dot)

    return pl.pallas_call(body, out_shape=jax.ShapeDtypeStruct((L, D), F32), name="lb_bwd")(lbp, dlb)


def _my_pos():
    return lax.axis_index("x"), lax.axis_index("y"), lax.axis_index("c")


def _peer(mask):
    x, y, c = _my_pos()
    mx, my, mc = (mask >> 2) & 1, (mask >> 1) & 1, mask & 1
    px = (1 - x) if mx else x
    py = (1 - y) if my else y
    pc = (1 - c) if mc else c
    return (px, py, pc), 4 * px + 2 * py + pc


def _all_gather(shards):
    n = len(shards)

    def body(*refs):
        ins, outs = refs[:n], refs[n:2 * n]
        send_sems, recv_sems, local_sems = refs[2 * n:]
        x, y, c = _my_pos()
        me = 4 * x + 2 * y + c
        local = [pltpu.make_async_copy(ins[a], outs[a].at[:, me], local_sems.at[a]) for a in range(n)]
        for cp in local:
            cp.start()
        sends = []
        for m in range(1, N_DEV):
            peer, _ = _peer(m)
            for a in range(n):
                cp = pltpu.make_async_remote_copy(
                    src_ref=ins[a], dst_ref=outs[a].at[:, me], send_sem=send_sems.at[a, m - 1],
                    recv_sem=recv_sems.at[a, m - 1], device_id=peer, device_id_type=MESH)
                cp.start()
                sends.append(cp)
        for m in range(1, N_DEV):
            peer, pid = _peer(m)
            for a in range(n):
                pltpu.make_async_remote_copy(
                    src_ref=ins[a], dst_ref=outs[a].at[:, pid], send_sem=send_sems.at[a, m - 1],
                    recv_sem=recv_sems.at[a, m - 1], device_id=peer, device_id_type=MESH).wait_recv()
        for cp in sends:
            cp.wait_send()
        for cp in local:
            cp.wait()

    out_shape = [jax.ShapeDtypeStruct((s.shape[0], N_DEV) + s.shape[1:], s.dtype) for s in shards]
    return pl.pallas_call(
        body, in_specs=[ANY] * n, out_specs=[ANY] * n, out_shape=out_shape,
        scratch_shapes=[pltpu.SemaphoreType.DMA((n, N_DEV - 1)), pltpu.SemaphoreType.DMA((n, N_DEV - 1)),
                        pltpu.SemaphoreType.DMA((n,))],
        name="all_gather_weights")(*shards)


def _exchange(grads, groups):
    n = len(grads)
    ng = 1 + max(g for g, _ in groups)
    layers = [1 + max(l for g, l in groups if g == gi) for gi in range(ng)]
    shapes = [None] * ng
    for a, (g, l) in enumerate(groups):
        shapes[g] = grads[a].shape[1:]

    def body(*refs):
        ins, outs = refs[:n], refs[n:n + ng]
        send_sems, recv_sems, local_sems = refs[n + ng:]
        x, y, c = _my_pos()
        me = 4 * x + 2 * y + c
        local = []
        for a, (g, l) in enumerate(groups):
            cp = pltpu.make_async_copy(ins[a].at[me], outs[g].at[me, l], local_sems.at[a])
            cp.start()
            local.append(cp)
        sends = []
        for m in range(1, N_DEV):
            peer, pid = _peer(m)
            for a, (g, l) in enumerate(groups):
                cp = pltpu.make_async_remote_copy(
                    src_ref=ins[a].at[pid], dst_ref=outs[g].at[me, l], send_sem=send_sems.at[a, m - 1],
                    recv_sem=recv_sems.at[a, m - 1], device_id=peer, device_id_type=MESH)
                cp.start()
                sends.append(cp)
        for m in range(1, N_DEV):
            peer, pid = _peer(m)
            for a, (g, l) in enumerate(groups):
                pltpu.make_async_remote_copy(
                    src_ref=ins[a].at[pid], dst_ref=outs[g].at[pid, l], send_sem=send_sems.at[a, m - 1],
                    recv_sem=recv_sems.at[a, m - 1], device_id=peer, device_id_type=MESH).wait_recv()
        for cp in sends:
            cp.wait_send()
        for cp in local:
            cp.wait()

    out_shape = [jax.ShapeDtypeStruct((N_DEV, layers[g]) + shapes[g], grads[[gg for gg, _ in groups].index(g)].dtype)
                 for g in range(ng)]
    return pl.pallas_call(
        body, in_specs=[ANY] * n, out_specs=[ANY] * ng, out_shape=out_shape,
        scratch_shapes=[pltpu.SemaphoreType.DMA((n, N_DEV - 1)), pltpu.SemaphoreType.DMA((n, N_DEV - 1)),
                        pltpu.SemaphoreType.DMA((n,))],
        name="exchange_grads")(*grads)


HBM_SPEC = pl.BlockSpec(memory_space=pltpu.HBM)
SEM_SPEC = pl.BlockSpec(memory_space=pltpu.SEMAPHORE)
EFFECT = pltpu.SideEffectType.DATAFLOW_SIDE_EFFECTING


def _hbm(a):
    return pltpu.with_memory_space_constraint(a, pltpu.HBM)


def _landing(own, dev):
    zone = lax.empty((N_DEV,) + own.shape, own.dtype)
    return lax.dynamic_update_slice(zone, own[None], (dev,) + (0,) * own.ndim)


def _push_start(name, srcs, lands, whole, groups):
    n = len(srcs)
    ng = 1 + max(groups)
    cnt = [groups.count(g) for g in range(ng)]
    idx = [groups[:a].count(groups[a]) for a in range(n)]

    def body(*refs):
        src_refs, land_refs = refs[:n], refs[n:2 * n]
        sems = refs[2 * n:2 * n + 2 * ng]
        token = refs[-1]
        x, y, c = _my_pos()
        me = 4 * x + 2 * y + c
        for a in range(n):
            g = groups[a]
            for m in range(1, N_DEV):
                peer, pid = _peer(m)
                pltpu.make_async_remote_copy(
                    src_ref=src_refs[a] if whole else src_refs[a].at[pid], dst_ref=land_refs[a].at[me],
                    send_sem=sems[2 * g].at[idx[a] * (N_DEV - 1) + m - 1],
                    recv_sem=sems[2 * g + 1].at[idx[a] * (N_DEV - 1) + m - 1],
                    device_id=peer, device_id_type=MESH).start()
        token[...] = jnp.zeros_like(token)

    sem_shapes = []
    for g in range(ng):
        sem_shapes += [pltpu.SemaphoreType.DMA((cnt[g] * (N_DEV - 1),))] * 2
    thru = [pltpu.HBM(s.shape, s.dtype) for s in list(srcs) + list(lands)]
    res = pl.pallas_call(
        body, name=name,
        out_shape=tuple(sem_shapes + thru + [jax.ShapeDtypeStruct((8, 128), F32)]),
        in_specs=tuple([HBM_SPEC] * (2 * n)),
        out_specs=tuple([SEM_SPEC] * (2 * ng) + [HBM_SPEC] * (2 * n) + [pl.BlockSpec(memory_space=pltpu.VMEM)]),
        input_output_aliases={i: 2 * ng + i for i in range(2 * n)},
        compiler_params=pltpu.CompilerParams(has_side_effects=EFFECT),
    )(*[_hbm(s) for s in srcs], *[_hbm(z) for z in lands])
    sems = [(res[2 * g], res[2 * g + 1]) for g in range(ng)]
    srcs_thru = list(res[2 * ng:2 * ng + n])
    lands_thru = list(res[2 * ng + n:2 * ng + 2 * n])
    return sems, srcs_thru, lands_thru, res[-1]


def _push_wait(name, srcs_thru, lands_thru, sems, after, whole):
    n = len(srcs_thru)

    def body(*refs):
        src_refs, land_refs = refs[:n], refs[n:2 * n]
        send_sems, recv_sems = refs[2 * n], refs[2 * n + 1]
        for a in range(n):
            for m in range(1, N_DEV):
                peer, pid = _peer(m)
                cp = pltpu.make_async_remote_copy(
                    src_ref=src_refs[a] if whole else src_refs[a].at[pid], dst_ref=land_refs[a].at[pid],
                    send_sem=send_sems.at[a * (N_DEV - 1) + m - 1], recv_sem=recv_sems.at[a * (N_DEV - 1) + m - 1],
                    device_id=peer, device_id_type=MESH)
                cp.wait_send()
                cp.wait_recv()

    thru = [pltpu.HBM(s.shape, s.dtype) for s in list(srcs_thru) + list(lands_thru)]
    res = pl.pallas_call(
        body, name=name, out_shape=tuple(thru),
        in_specs=tuple([HBM_SPEC] * (2 * n) + [SEM_SPEC, SEM_SPEC, ANY]),
        out_specs=tuple([HBM_SPEC] * (2 * n)),
        input_output_aliases={i: i for i in range(2 * n)},
        compiler_params=pltpu.CompilerParams(has_side_effects=EFFECT),
    )(*srcs_thru, *lands_thru, sems[0], sems[1], after)
    return list(res[n:])


def _adamw(recv, w, m, v, layer=0, prev=None):
    L, R, C = w.shape
    tr = _tile(R, max(8, (1 << 18) // C), 8) if R % 8 == 0 else R
    bc1 = 1.0 - ADAM_B1 ** ADAM_STEP
    bc2 = 1.0 - ADAM_B2 ** ADAM_STEP
    if prev is None:
        prev = [lax.empty((L, R, C), F32) for _ in range(4)]

    def body(r_ref, w_ref, m_ref, v_ref, p0, p1, p2, p3, g_ref, d_ref, nm_ref, nv_ref):
        g = r_ref[0].astype(F32)
        for s in range(1, N_DEV):
            g = g + r_ref[s].astype(F32)
        nm = ADAM_B1 * m_ref[...] + (1.0 - ADAM_B1) * g
        nv = ADAM_B2 * v_ref[...] + (1.0 - ADAM_B2) * (g * g)
        mh = nm / bc1
        vh = nv / bc2
        g_ref[...] = g
        d_ref[...] = -ADAM_LR * (mh / (jnp.sqrt(vh) + ADAM_EPS) + ADAM_WD * w_ref[...])
        nm_ref[...] = nm
        nv_ref[...] = nv

    row = pl.BlockSpec((None, tr, C), lambda i: (layer, i, 0))
    return pl.pallas_call(
        body, grid=(R // tr,),
        in_specs=[pl.BlockSpec((N_DEV, tr, C), lambda i: (0, i, 0)), row, row, row] + [ANY] * 4,
        out_specs=[row] * 4, out_shape=[jax.ShapeDtypeStruct((L, R, C), F32)] * 4,
        input_output_aliases={4: 0, 5: 1, 6: 2, 7: 3},
        compiler_params=_cp("parallel"), name="adamw")(recv, w, m, v, *prev)


def _full_w_spec(tk, tn):
    return pl.BlockSpec((tk, tn), lambda i, j, k: (k, j))


def _colblk_w_spec(n):
    def spec(tk, tn):
        per = n // tn
        return pl.BlockSpec((None, tk, tn), lambda i, j, k: (j // per, k, j % per))
    return spec


def kernel(x, meta_tokens, mix_norm_g, mlp_norm_g, final_norm_g, ev_w_in, ev_conv_w, ev_conv_b, ev_ln_g, ev_ln_b, ev_pool_w, ev_pool_b, ev_pool_scale, ev_w_out, od_w_in, od_gnorm_g, od_w_out, lb_param, mlp_w1, mlp_w2, loss_target, m_meta_tokens, m_mix_norm_g, m_mlp_norm_g, m_final_norm_g, m_ev_w_in, m_ev_conv_w, m_ev_conv_b, m_ev_ln_g, m_ev_ln_b, m_ev_pool_w, m_ev_pool_b, m_ev_pool_scale, m_ev_w_out, m_od_w_in, m_od_gnorm_g, m_od_w_out, m_lb_param, m_mlp_w1, m_mlp_w2, v_meta_tokens, v_mix_norm_g, v_mlp_norm_g, v_final_norm_g, v_ev_w_in, v_ev_conv_w, v_ev_conv_b, v_ev_ln_g, v_ev_ln_b, v_ev_pool_w, v_ev_pool_b, v_ev_pool_scale, v_ev_w_out, v_od_w_in, v_od_gnorm_g, v_od_w_out, v_lb_param, v_mlp_w1, v_mlp_w2):
    S, D = x.shape[1], x.shape[2]
    T = PAD + N_META + S
    DEPTH = mix_norm_g.shape[0]
    DFF = mlp_w1.shape[2] * N_DEV
    dev = 4 * lax.axis_index("x") + 2 * lax.axis_index("y") + lax.axis_index("c")

    g_meta, g_cw = _all_gather([meta_tokens[None], ev_conv_w])
    n_ev = ev_w_in.shape[0]
    n_od = od_w_in.shape[0]
    meta_full = jnp.transpose(g_meta[0], (1, 0, 2)).reshape(N_META, D)
    cw_full = jnp.transpose(g_cw, (0, 2, 1, 3)).reshape(n_ev, CONV_WIDTH, -1)
    cw_pad = jnp.pad(cw_full, ((0, 0), (0, 32 - CONV_WIDTH), (0, 0)))
    n_in_od = od_w_in.shape[2]
    n_w1 = mlp_w1.shape[2]

    ag_src, ag_grp, ag_at = [], [], {}
    for layer in range(DEPTH):
        j = layer // 2
        mixer = [("in", ev_w_in[j]), ("out", ev_w_out[j])] if layer % 2 == 0 else [("in", od_w_in[j]), ("out", od_w_out[j])]
        for pos, (key, arr) in enumerate(mixer + [("w1", mlp_w1[layer]), ("w2", mlp_w2[layer])]):
            ag_at[layer, key] = len(ag_src)
            ag_src.append(arr.astype(BF16))
            ag_grp.append(2 * layer + pos // 2)
    ag_src, g_meta, g_cw = lax.optimization_barrier((ag_src, g_meta, g_cw))
    ag_sems, ag_s, ag_l, ag_tok = _push_start("ag_start", ag_src, [_landing(s_, dev) for s_ in ag_src], True, ag_grp)

    def ag_wait(group, after):
        ids = [a for a in range(len(ag_src)) if ag_grp[a] == group]
        got = _push_wait(f"ag_wait_{group}", [ag_s[a] for a in ids], [ag_l[a] for a in ids], ag_sems[group], after, True)
        return dict(zip(ids, got))

    h = jnp.concatenate([jnp.zeros((PAD, D), F32), meta_full, x[0]], axis=0) + ag_tok[0, 0]
    tgt = jnp.pad(loss_target[0], ((PAD + N_META, 0), (0, 0)))
    lb_all = _lb_fwd(lb_param)

    tm_big = _tile(T, MM_ROWS_BIG, 16)
    tm_mid = _tile(T, MM_ROWS_MID, 16)
    tm_k4 = _tile(T, MM_ROWS_K4, 16)

    saved = []
    for layer in range(DEPTH):
        j = layer // 2
        sv = {"h0": h}
        got = ag_wait(2 * layer, h)
        g_in, g_out = got[ag_at[layer, "in"]], got[ag_at[layer, "out"]]
        w_out = g_out.reshape(-1, D)
        n = _rms_fwd(h, mix_norm_g[layer][None])
        sv["n"] = n
        w_in = jnp.transpose(g_in, (1, 0, 2)).reshape(D, -1)
        if layer % 2 == 0:
            u = _mm_nn("ev_in", n, w_in, _full_w_spec, T, w_in.shape[1], D, tm_big, 512, D, "f32")
            yab, yc = _ev_fwd(u, cw_pad[j], ev_conv_b[j][None], ev_ln_g[j][None], ev_ln_b[j][None],
                              ev_pool_w[j].astype(BF16), ev_pool_b[j].reshape(1, -1), ev_pool_scale[j][None])
            sv.update(u=u, y=yab, yc=yc)
            h = _mm_nn("ev_out", yab, w_out, _full_w_spec, T, D, D, tm_mid, D, D, "resid", extra=h)
        else:
            u = _mm_nn("od_in", n, w_in, _full_w_spec, T, N_DEV * n_in_od, D, tm_big, 512, D, "f32")
            y, o, s0 = _hgrn_fwd(u, lb_all[layer][None], od_gnorm_g[j][None])
            sv.update(u=u, y=y, o=o, s0=s0)
            h = _mm_nn("od_out", y, w_out, _full_w_spec, T, D, D, tm_mid, D, D, "resid", extra=h)
        sv["h1"] = h
        got = ag_wait(2 * layer + 1, h)
        w_w1 = jnp.transpose(got[ag_at[layer, "w1"]], (1, 0, 2)).reshape(D, DFF)
        w_w2 = got[ag_at[layer, "w2"]].reshape(DFF, D)
        sv.update(w_in=w_in, w_out=w_out, w_w1=w_w1, w_w2=w_w2)
        n2 = _rms_fwd(h, mlp_norm_g[layer][None])
        r, act = _mm_nn("mlp_w1", n2, w_w1, _full_w_spec, T, DFF, D, tm_big, 512, D, "relu2")
        sv.update(n2=n2, r=r, act=act)
        h = _mm_nn("mlp_w2", act, w_w2, _full_w_spec, T, D, DFF, tm_k4, D, DFF, "resid", extra=h)
        saved.append(sv)

    loss_blk, dh, dhb, dg_final = _loss_head(h, final_norm_g[None], tgt)
    loss = lax.psum(loss_blk[0, 0], AXES)

    tt = T
    g_mix, g_mlp = [None] * DEPTH, [None] * DEPTH
    small ={"cw": [None] * n_ev, "vec": [None] * n_ev, "pw": [None] * n_ev, "gn": [None] * n_od}
    dlb_rows = [jnp.zeros((1, D), F32) for _ in range(DEPTH)]

    def xs2(tt_, tk):
        return pl.BlockSpec((tt_, tk), lambda a, b, t: (t, a))

    def ys2(tt_, tn):
        return pl.BlockSpec((tt_, tn), lambda a, b, t: (t, b))

    def os2(tk, tn):
        return pl.BlockSpec((tk, tn), lambda a, b, t: (a, b))

    def os3(tk, tn):
        return pl.BlockSpec((None, tk, tn), lambda a, b, t: (b, a, 0))

    def dy2(tm, tn):
        return pl.BlockSpec((tm, tn), lambda i, jj, k: (i, k))

    def w_rows(tj, tn):
        return pl.BlockSpec((tj, tn), lambda i, jj, k: (jj, k))

    def w_colblk(tj, tn):
        return pl.BlockSpec((None, tj, tn), lambda i, jj, k: (k, jj, 0))

    rs_pending = []

    def rs_start(tag, mats):
        blocks = [m_ if m_.ndim == 3 else m_.reshape(N_DEV, m_.shape[0] // N_DEV, m_.shape[1]) for m_ in mats]
        lands = [_landing(lax.dynamic_index_in_dim(b_, dev, 0, keepdims=False), dev) for b_ in blocks]
        sems, s_thru, l_thru, tok = _push_start(f"rs_start_{tag}", blocks, lands, False, [0] * len(blocks))
        rs_pending.append((tag, s_thru, l_thru, sems[0]))
        return tok[0, 0]

    for layer in reversed(range(DEPTH)):
        j = layer // 2
        sv = saved[layer]
        da1 = _mm_nt("mlp_w2_t", dhb, sv["w_w2"], dy2, w_rows, T, DFF, D, tm_mid, 1024, D, "dact", extra=sv["r"])
        dw2 = _mm_tn("mlp_dw2", sv["act"], dhb, xs2, ys2, os2, (DFF, D), T, DFF, D, tt, 512, D)
        dw1 = _mm_tn("mlp_dw1", sv["n2"], da1, xs2, ys2, os3, (N_DEV, D, n_w1), T, D, DFF, tt, D, n_w1)
        tok = rs_start(f"mlp{layer}", [dw1, dw2])
        dh, dhb, g_mlp[layer] = _mm_nt("mlp_w1_t", da1, sv["w_w1"], dy2, w_rows, T, D, DFF, tm_k4, D, DFF, "rms",
                                       extra=(sv["h1"], mlp_norm_g[layer][None] + tok, dh))
        if layer % 2 == 0:
            dyab = _mm_nt("ev_out_t", dhb, sv["w_out"], dy2, w_rows, T, D, D, tm_mid, D, D, "f32")
            dwout = _mm_tn("ev_dwout", sv["y"], dhb, xs2, ys2, os2, (D, D), T, D, D, tt, 512, D)
            du, small["cw"][j], small["vec"][j], small["pw"][j] = _ev_bwd(
                dyab, sv["yc"], sv["u"], cw_pad[j], ev_ln_g[j][None], ev_ln_b[j][None], ev_pool_w[j].astype(BF16),
                jnp.transpose(ev_pool_w[j], (0, 2, 1)).astype(BF16), ev_pool_b[j].reshape(1, -1),
                ev_pool_scale[j][None])
            nin = du.shape[1]
            dwin = _mm_tn("ev_dwin", sv["n"], du, xs2, ys2, os2, (D, nin), T, D, nin, tt, D, 512)
            dwin = jnp.transpose(dwin.reshape(D, N_DEV, nin // N_DEV), (1, 0, 2))
            tok = rs_start(f"mix{layer}", [dwin, dwout])
            dh, dhb, g_mix[layer] = _mm_nt("ev_in_t", du, sv["w_in"], dy2, w_rows, T, D, nin, tm_k4, D, nin, "rms",
                                           extra=(sv["h0"], mix_norm_g[layer][None] + tok, dh))
        else:
            dy = _mm_nt("od_out_t", dhb, sv["w_out"], dy2, w_rows, T, D, D, tm_mid, D, D, "f32")
            dwout = _mm_tn("od_dwout", sv["y"], dhb, xs2, ys2, os2, (D, D), T, D, D, tt, 512, D)
            du3, dlb_rows[layer], small["gn"][j] = _hgrn_bwd(dy, sv["o"], sv["s0"], sv["u"], lb_all[layer][None],
                                                              od_gnorm_g[j][None])
            per = D // n_in_od

            def du_t(tt_, tn):
                return pl.BlockSpec((None, tt_, tn), lambda a, b, t: (b // per, t, b % per))

            dwin = _mm_tn("od_dwin", sv["n"], du3, xs2, du_t, os3, (N_DEV, D, n_in_od), T, D, 4 * D, tt, D, n_in_od)
            tok = rs_start(f"mix{layer}", [dwin, dwout])
            dh, dhb, g_mix[layer] = _mm_nt(
                "od_in_t", du3, sv["w_in"], lambda tm, tn: pl.BlockSpec((4, tm, tn // 4), lambda i, jj, k: (0, i, 0)),
                w_rows, T, D, 4 * D, tm_k4, D, 4 * D, "rms", extra=(sv["h0"], mix_norm_g[layer][None] + tok, dh),
                parts=4)

    dmeta = dh[PAD:PAD + N_META]
    grad_x = dh[PAD + N_META:][None]
    dlb_param = _lb_bwd(lb_param, jnp.concatenate(dlb_rows, axis=0))

    pieces = [
        ("meta", dmeta), ("mix", jnp.concatenate(g_mix, 0)), ("mlp", jnp.concatenate(g_mlp, 0)), ("final", dg_final),
        ("cw", jnp.stack([c[:CONV_WIDTH] for c in small["cw"]])), ("cb", jnp.stack([v_[0] for v_ in small["vec"]])),
        ("lng", jnp.stack([v_[1] for v_ in small["vec"]])), ("lnb", jnp.stack([v_[2] for v_ in small["vec"]])),
        ("pw", jnp.stack(small["pw"])), ("pb", jnp.stack([v_[4] for v_ in small["vec"]])),
        ("ps", jnp.stack([v_[3] for v_ in small["vec"]])), ("gn", jnp.stack([jnp.sum(g_, axis=0)[0] for g_ in small["gn"]])),
        ("lb", dlb_param),
    ]
    flat = jnp.concatenate([p.reshape(-1) for _, p in pieces])
    n_small = flat.shape[0]
    rows_small = -(-n_small // 1024 // 8) * 8
    flat = jnp.pad(flat, (0, rows_small * 1024 - n_small)).reshape(rows_small, 1024)

    recv_small = _exchange([jnp.broadcast_to(flat[None], (N_DEV,) + flat.shape)], [(0, 0)])[0]
    recv = {}
    for tag, s_thru, l_thru, sems in rs_pending:
        got = _push_wait(f"rs_wait_{tag}", s_thru, l_thru, sems, recv_small, False)
        layer = int(tag[3:])
        if tag.startswith("mlp"):
            recv["w1", layer], recv["w2", layer] = got
        else:
            key = "ev" if layer % 2 == 0 else "od"
            recv[key + "_in", layer // 2], recv[key + "_out", layer // 2] = got

    outs = {}
    big = {"ev_in": ("ev_w_in", ev_w_in, m_ev_w_in, v_ev_w_in), "ev_out": ("ev_w_out", ev_w_out, m_ev_w_out, v_ev_w_out),
           "od_in": ("od_w_in", od_w_in, m_od_w_in, v_od_w_in), "od_out": ("od_w_out", od_w_out, m_od_w_out, v_od_w_out),
           "w1": ("mlp_w1", mlp_w1, m_mlp_w1, v_mlp_w1), "w2": ("mlp_w2", mlp_w2, m_mlp_w2, v_mlp_w2)}
    for key, (name, w, m, v) in big.items():
        res = None
        for l in range(w.shape[0]):
            res = _adamw(recv[key, l], w, m, v, layer=l, prev=res)
        outs[name] = res

    small_params = {
        "meta": ("meta_tokens", None), "mix": ("mix_norm_g", mix_norm_g, m_mix_norm_g, v_mix_norm_g),
        "mlp": ("mlp_norm_g", mlp_norm_g, m_mlp_norm_g, v_mlp_norm_g),
        "final": ("final_norm_g", final_norm_g, m_final_norm_g, v_final_norm_g),
        "cw": ("ev_conv_w", None), "cb": ("ev_conv_b", ev_conv_b, m_ev_conv_b, v_ev_conv_b),
        "lng": ("ev_ln_g", ev_ln_g, m_ev_ln_g, v_ev_ln_g), "lnb": ("ev_ln_b", ev_ln_b, m_ev_ln_b, v_ev_ln_b),
        "pw": ("ev_pool_w", ev_pool_w, m_ev_pool_w, v_ev_pool_w), "pb": ("ev_pool_b", ev_pool_b, m_ev_pool_b, v_ev_pool_b),
        "ps": ("ev_pool_scale", ev_pool_scale, m_ev_pool_scale, v_ev_pool_scale),
        "gn": ("od_gnorm_g", od_gnorm_g, m_od_gnorm_g, v_od_gnorm_g), "lb": ("lb_param", lb_param, m_lb_param, v_lb_param),
    }
    csh = ev_conv_w.shape[2]
    msh = meta_tokens.shape[1]

    def packed(which):
        parts = []
        for key, g_ in pieces:
            ent = small_params[key]
            if key == "meta":
                src = (meta_tokens, m_meta_tokens, v_meta_tokens)[which]
                full = lax.dynamic_update_slice(jnp.zeros((N_META, D), F32), src, (0, dev * msh))
            elif key == "cw":
                src = (ev_conv_w, m_ev_conv_w, v_ev_conv_w)[which]
                full = lax.dynamic_update_slice(jnp.zeros(g_.shape, F32), src, (0, 0, dev * csh))
            else:
                full = ent[1 + which]
            parts.append(full.reshape(-1))
        f = jnp.concatenate(parts)
        return jnp.pad(f, (0, rows_small * 1024 - n_small)).reshape(rows_small, 1024)

    sres = [r_[0] for r_ in _adamw(recv_small[:, 0], packed(0)[None], packed(1)[None], packed(2)[None])]
    off = 0
    for key, g_ in pieces:
        size = g_.size
        vals = [r_.reshape(-1)[off:off + size].reshape(g_.shape) for r_ in sres]
        off += size
        name = small_params[key][0]
        if key == "meta":
            vals = [lax.dynamic_slice(v_, (0, dev * msh), (N_META, msh)) for v_ in vals]
        elif key == "cw":
            vals = [lax.dynamic_slice(v_, (0, 0, dev * csh), v_.shape[:2] + (csh,)) for v_ in vals]
        else:
            vals = [v_.reshape(small_params[key][1].shape) for v_ in vals]
        outs[name] = vals

    names = ["meta_tokens", "mix_norm_g", "mlp_norm_g", "final_norm_g", "ev_w_in", "ev_conv_w", "ev_conv_b", "ev_ln_g",
             "ev_ln_b", "ev_pool_w", "ev_pool_b", "ev_pool_scale", "ev_w_out", "od_w_in", "od_gnorm_g", "od_w_out",
             "lb_param", "mlp_w1", "mlp_w2"]
    result = [loss, grad_x]
    for k in range(4):
        result += [outs[nm][k] for nm in names]
    return tuple(result)
```

```python
import functools

import jax
import jax.numpy as jnp
from jax import lax
from jax.experimental import pallas as pl
from jax.experimental.pallas import tpu as pltpu

F32 = jnp.float32
BF16 = jnp.bfloat16

N_DEV = 8
N_META = 16
CHUNK = 64
PAD = CHUNK - N_META
SUB = 16
HEAD = 128
CONV_WIDTH = 31
HALO = 32
POOL_WINDOWS = (2, 4, 8, 16)
EPS = 1e-6
NEG = -1e30
ADAM_LR, ADAM_B1, ADAM_B2, ADAM_EPS, ADAM_WD, ADAM_STEP = 0.001, 0.9, 0.999, 1e-08, 0.01, 10
VMEM_LIMIT = 56 * 1024 * 1024
EV_ROWS = 416
HGRN_ROWS = 832
MM_ROWS_BIG = 2080
MM_ROWS_MID = 1040
MM_ROWS_K4 = 416
HGRN_HEADS_PER_STEP = 2
MESH = pl.DeviceIdType.MESH
AXES = ("x", "y", "c")
ANY = pl.BlockSpec(memory_space=pl.ANY)


def _cp(*sem):
    return pltpu.CompilerParams(dimension_semantics=sem, vmem_limit_bytes=VMEM_LIMIT)


def _tile(n, cap, mult):
    best = None
    for d in range(mult, min(n, cap) + 1, mult):
        if n % d == 0:
            best = d
    assert best is not None, (n, cap, mult)
    return best


def _nt(a, b):
    return lax.dot_general(a, b, (((1,), (1,)), ((), ())), preferred_element_type=F32)


def _tn(a, b):
    return lax.dot_general(a, b, (((0,), (0,)), ((), ())), preferred_element_type=F32)


def _nn(a, b):
    return jnp.dot(a, b, preferred_element_type=F32)


def _r16(x):
    return x.astype(BF16).astype(F32)


def _row_ids(base, n):
    return base + lax.broadcasted_iota(jnp.int32, (n, 1), 0)


def _dsilu(x, s):
    return s * (1.0 + x * (1.0 - s))


def _rms_fwd(h, g):
    T, D = h.shape
    tm = _tile(T, MM_ROWS_MID, 16)

    def body(h_ref, g_ref, n_ref):
        x = h_ref[...]
        r = lax.rsqrt(jnp.mean(x * x, axis=-1, keepdims=True) + EPS)
        n_ref[...] = ((x * r) * g_ref[...]).astype(BF16)

    return pl.pallas_call(
        body, grid=(T // tm,),
        in_specs=[pl.BlockSpec((tm, D), lambda i: (i, 0)), pl.BlockSpec((1, D), lambda i: (0, 0))],
        out_specs=pl.BlockSpec((tm, D), lambda i: (i, 0)),
        out_shape=jax.ShapeDtypeStruct((T, D), BF16),
        compiler_params=_cp("parallel"), name="rms_fwd")(h, g)


def _rms_bwd(dn, h, g, dres):
    T, D = h.shape
    tm = _tile(T, MM_ROWS_MID, 16)

    def body(dn_ref, h_ref, g_ref, dres_ref, dh_ref, dhb_ref, dg_ref):
        i = pl.program_id(0)
        x = h_ref[...]
        dn_v = dn_ref[...]
        r = lax.rsqrt(jnp.mean(x * x, axis=-1, keepdims=True) + EPS)
        xh = x * r
        dxh = dn_v * g_ref[...]
        dx = r * (dxh - xh * jnp.mean(dxh * xh, axis=-1, keepdims=True))
        keep = _row_ids(i * tm, tm) >= PAD
        dh = jnp.where(keep, dres_ref[...] + dx, 0.0)
        dh_ref[...] = dh
        dhb_ref[...] = dh.astype(BF16)

        @pl.when(i == 0)
        def _():
            dg_ref[...] = jnp.zeros_like(dg_ref)

        dg_ref[...] += jnp.sum(dn_v * xh, axis=0, keepdims=True)

    row = pl.BlockSpec((tm, D), lambda i: (i, 0))
    vec = pl.BlockSpec((1, D), lambda i: (0, 0))
    return pl.pallas_call(
        body, grid=(T // tm,),
        in_specs=[row, row, vec, row], out_specs=[row, row, vec],
        out_shape=[jax.ShapeDtypeStruct((T, D), F32), jax.ShapeDtypeStruct((T, D), BF16),
                   jax.ShapeDtypeStruct((1, D), F32)],
        compiler_params=_cp("arbitrary"), name="rms_bwd")(dn, h, g, dres)


def _loss_head(h, g, tgt):
    T, D = h.shape
    tm = _tile(T, MM_ROWS_MID, 16)
    first_x = PAD + N_META

    def body(h_ref, g_ref, t_ref, loss_ref, dh_ref, dhb_ref, dg_ref):
        i = pl.program_id(0)
        x = h_ref[...]
        r = lax.rsqrt(jnp.mean(x * x, axis=-1, keepdims=True) + EPS)
        xh = x * r
        gv = g_ref[...]
        out = xh * gv
        valid = _row_ids(i * tm, tm) >= first_x
        e = jnp.where(valid, out - t_ref[...], 0.0)
        dout = e * (1.0 / D)
        dxh = dout * gv
        dx = r * (dxh - xh * jnp.mean(dxh * xh, axis=-1, keepdims=True))
        dh_ref[...] = dx
        dhb_ref[...] = dx.astype(BF16)

        @pl.when(i == 0)
        def _():
            dg_ref[...] = jnp.zeros_like(dg_ref)
            loss_ref[...] = jnp.zeros_like(loss_ref)

        dg_ref[...] += jnp.sum(dout * xh, axis=0, keepdims=True)
        loss_ref[...] += 0.5 * jnp.sum(jnp.mean(e * e, axis=-1, keepdims=True))

    row = pl.BlockSpec((tm, D), lambda i: (i, 0))
    vec = pl.BlockSpec((1, D), lambda i: (0, 0))
    return pl.pallas_call(
        body, grid=(T // tm,),
        in_specs=[row, vec, row],
        out_specs=[pl.BlockSpec((8, 128), lambda i: (0, 0)), row, row, vec],
        out_shape=[jax.ShapeDtypeStruct((8, 128), F32), jax.ShapeDtypeStruct((T, D), F32),
                   jax.ShapeDtypeStruct((T, D), BF16), jax.ShapeDtypeStruct((1, D), F32)],
        compiler_params=_cp("arbitrary"), name="loss_head")(h, g, tgt)


def _mm_nn(name, a, w, w_spec, M, N, K, tm, tn, tk, mode, extra=None, a_spec=None):
    nk = K // tk
    if a_spec is None:
        a_spec = pl.BlockSpec((tm, tk), lambda i, j, k: (i, k))
    o_spec = pl.BlockSpec((tm, tn), lambda i, j, k: (i, j))

    def body(*refs):
        if mode == "resid":
            a_ref, w_ref, e_ref = refs[:3]
            outs = refs[3:]
        else:
            a_ref, w_ref = refs[:2]
            outs = refs[2:]
        acc_ref = outs[-1] if nk > 1 else None
        part = _nn(a_ref[...], w_ref[...])

        def finish(acc):
            if mode == "f32":
                outs[0][...] = acc
            elif mode == "relu2":
                r = jnp.maximum(acc, 0.0)
                outs[0][...] = r.astype(BF16)
                outs[1][...] = (r * r).astype(BF16)
            else:
                keep = _row_ids(pl.program_id(0) * tm, tm) >= PAD
                outs[0][...] = jnp.where(keep, e_ref[...] + acc, 0.0)

        if nk == 1:
            finish(part)
        else:
            k = pl.program_id(2)

            @pl.when(k == 0)
            def _():
                acc_ref[...] = part

            @pl.when(k > 0)
            def _():
                acc_ref[...] += part

            @pl.when(k == nk - 1)
            def _():
                finish(acc_ref[...])

    in_specs = [a_spec, w_spec(tk, tn)]
    args = [a, w]
    if mode == "resid":
        in_specs.append(o_spec)
        args.append(extra)
    if mode == "relu2":
        out_specs = [o_spec, o_spec]
        out_shape = [jax.ShapeDtypeStruct((M, N), BF16)] * 2
    else:
        out_specs = [o_spec]
        out_shape = [jax.ShapeDtypeStruct((M, N), F32)]
    scratch = [pltpu.VMEM((tm, tn), F32)] if nk > 1 else []
    res = pl.pallas_call(
        body, grid=(M // tm, N // tn, nk), in_specs=in_specs, out_specs=out_specs, out_shape=out_shape,
        scratch_shapes=scratch, compiler_params=_cp("parallel", "parallel", "arbitrary"), name=name)(*args)
    return res if mode == "relu2" else res[0]


def _mm_nt(name, dy, w, dy_spec, w_spec, M, J, N, tm, tj, tn, mode, extra=None, parts=1):
    nk = N // tn
    o_spec = pl.BlockSpec((tm, tj), lambda i, j, k: (i, j))
    n_extra = {"f32": 0, "dact": 1, "rms": 3}[mode]
    if mode == "rms":
        assert nk == 1 and tj == J

    def body(*refs):
        dy_ref, w_ref = refs[:2]
        ex = refs[2:2 + n_extra]
        outs = refs[2 + n_extra:]
        acc_ref = outs[-1] if nk > 1 else None
        if parts == 1:
            part = _nt(dy_ref[...], w_ref[...])
        else:
            wq = tn // parts
            part = _nt(dy_ref[0], w_ref[:, 0:wq])
            for q in range(1, parts):
                part = part + _nt(dy_ref[q], w_ref[:, q * wq:(q + 1) * wq])

        def finish(acc):
            if mode == "f32":
                outs[0][...] = acc
            elif mode == "dact":
                outs[0][...] = (acc * (2.0 * ex[0][...].astype(F32))).astype(BF16)
            else:
                h_ref, g_ref, dres_ref = ex
                dh_ref, dhb_ref, dg_ref = outs[:3]
                i = pl.program_id(0)
                x = h_ref[...]
                r = lax.rsqrt(jnp.mean(x * x, axis=-1, keepdims=True) + EPS)
                xh = x * r
                dxh = acc * g_ref[...]
                dx = r * (dxh - xh * jnp.mean(dxh * xh, axis=-1, keepdims=True))
                keep = _row_ids(i * tm, tm) >= PAD
                dh = jnp.where(keep, dres_ref[...] + dx, 0.0)
                dh_ref[...] = dh
                dhb_ref[...] = dh.astype(BF16)

                @pl.when(i == 0)
                def _():
                    dg_ref[...] = jnp.zeros_like(dg_ref)

                dg_ref[...] += jnp.sum(acc * xh, axis=0, keepdims=True)

        if nk == 1:
            finish(part)
        else:
            k = pl.program_id(2)

            @pl.when(k == 0)
            def _():
                acc_ref[...] = part

            @pl.when(k > 0)
            def _():
                acc_ref[...] += part

            @pl.when(k == nk - 1)
            def _():
                finish(acc_ref[...])

    in_specs = [dy_spec(tm, tn), w_spec(tj, tn)]
    args = [dy, w]
    scratch = [pltpu.VMEM((tm, tj), F32)] if nk > 1 else []
    if mode == "rms":
        vec = pl.BlockSpec((1, J), lambda i, j, k: (0, 0))
        h, g, dres = extra
        res = pl.pallas_call(
            body, grid=(M // tm, 1, 1), in_specs=in_specs + [o_spec, vec, o_spec], out_specs=[o_spec, o_spec, vec],
            out_shape=[jax.ShapeDtypeStruct((M, J), F32), jax.ShapeDtypeStruct((M, J), BF16),
                       jax.ShapeDtypeStruct((1, J), F32)],
            compiler_params=_cp("arbitrary", "arbitrary", "arbitrary"), name=name)(*args, h, g, dres)
        return res
    if mode == "dact":
        in_specs.append(o_spec)
        args.append(extra)
    return pl.pallas_call(
        body, grid=(M // tm, J // tj, nk), in_specs=in_specs, out_specs=[o_spec],
        out_shape=[jax.ShapeDtypeStruct((M, J), BF16 if mode == "dact" else F32)],
        scratch_shapes=scratch, compiler_params=_cp("parallel", "parallel", "arbitrary"), name=name)(*args)[0]


def _mm_tn(name, x, dy, x_spec, dy_spec, o_spec, o_shape, T, K, N, tt, tk, tn):
    nt = T // tt

    def body(x_ref, dy_ref, o_ref, *acc):
        part = _tn(x_ref[...], dy_ref[...])
        if nt == 1:
            o_ref[...] = part.astype(BF16)
            return
        acc_ref = acc[0]
        t = pl.program_id(2)

        @pl.when(t == 0)
        def _():
            acc_ref[...] = part

        @pl.when(t > 0)
        def _():
            acc_ref[...] += part

        @pl.when(t == nt - 1)
        def _():
            o_ref[...] = acc_ref[...].astype(BF16)

    return pl.pallas_call(
        body, grid=(K // tk, N // tn, nt), in_specs=[x_spec(tt, tk), dy_spec(tt, tn)], out_specs=o_spec(tk, tn),
        out_shape=jax.ShapeDtypeStruct(o_shape, BF16), scratch_shapes=[pltpu.VMEM((tk, tn), F32)] if nt > 1 else [],
        compiler_params=_cp("parallel", "parallel", "arbitrary"), name=name)(x, dy)


def _pool_counts(base, n, w):
    pos = _row_ids(base, n) - PAD
    return jnp.clip(pos + 1, 1, w).astype(F32)


def _shifted_copies(buf, rows):
    buf[0, rows:rows + 8, :] = jnp.zeros((8, buf.shape[2]), F32)

    def blk(s, carry):
        b = pl.multiple_of(s * HALO, HALO)
        win = buf[0, pl.ds(b, HALO + 8), :]
        for r in range(1, 8):
            buf[r, pl.ds(b, HALO), :] = win[r:r + HALO]
        return carry

    lax.fori_loop(0, rows // HALO, blk, 0)


def _ev_fwd(u, cw, cb, lg, lb, pw, pb, ps):
    T = u.shape[0]
    C = 512
    tm = _tile(T, EV_ROWS, HALO)
    nsub = tm // HALO
    hb = tm // HALO

    def body(val_ref, gate_ref, pin_ref, valh_ref, gateh_ref, pinh_ref, cw_ref, cb_ref, lg_ref, lb_ref, pw_ref,
             pb_ref, ps_ref, yab_ref, yc_ref, a_ext, p_ext, d_buf):
        i = pl.program_id(0)
        nf = (i > 0).astype(F32)
        a_ext[0, 0:HALO, :] = valh_ref[...] * jax.nn.sigmoid(gateh_ref[...]) * nf
        a_ext[0, HALO:HALO + tm, :] = val_ref[...] * jax.nn.sigmoid(gate_ref[...])
        p_ext[0:HALO, :] = pinh_ref[...] * nf
        p_ext[HALO:, :] = pin_ref[...]
        _shifted_copies(a_ext, tm + HALO)

        def sub(s, carry):
            base = pl.multiple_of(s * HALO, HALO)
            acc = jnp.zeros((HALO, C), F32) + cb_ref[...]
            for j in range(CONV_WIDTH):
                off = 2 + j
                acc = acc + cw_ref[pl.ds(j, 1), :] * a_ext[off % 8, pl.ds(pl.multiple_of(base + off // 8 * 8, 8), HALO), :]
            yc_ref[pl.ds(base, HALO), :] = acc
            mu = jnp.mean(acc, axis=-1, keepdims=True)
            yc = acc - mu
            rstd = lax.rsqrt(jnp.mean(yc * yc, axis=-1, keepdims=True) + EPS)
            z = (yc * rstd) * lg_ref[...] + lb_ref[...]
            yab_ref[pl.ds(base, HALO), 0:C] = (z * jax.nn.sigmoid(z)).astype(BF16)
            pwin = p_ext[pl.ds(base, 2 * HALO), :]
            for gi, w in enumerate(POOL_WINDOWS):
                lo, hi = gi * HEAD, (gi + 1) * HEAD
                x = pwin[HALO:, lo:hi]
                tot = x
                for k in range(1, w):
                    tot = tot + pwin[HALO - k:2 * HALO - k, lo:hi]
                cnt = _pool_counts(i * tm + base, HALO, w)
                d_buf[pl.ds(base, HALO), lo:hi] = (tot / cnt - x).astype(BF16)
            return carry

        lax.fori_loop(0, nsub, sub, 0)
        for gi in range(len(POOL_WINDOWS)):
            lo, hi = gi * HEAD, (gi + 1) * HEAD
            y = _nn(d_buf[:, lo:hi], pw_ref[gi]) + pb_ref[:, lo:hi]
            yab_ref[:, C + lo:C + hi] = (y * ps_ref[:, lo:hi]).astype(BF16)

    def main(c):
        return pl.BlockSpec((tm, C), lambda i: (i, c))

    def halo(c):
        return pl.BlockSpec((HALO, C), lambda i: (jnp.maximum(i * hb - 1, 0), c))

    vec = pl.BlockSpec((1, C), lambda i: (0, 0))
    return pl.pallas_call(
        body, grid=(T // tm,),
        in_specs=[main(0), main(1), main(2), halo(0), halo(1), halo(2),
                  pl.BlockSpec((32, C), lambda i: (0, 0)), vec, vec, vec,
                  pl.BlockSpec((4, HEAD, HEAD), lambda i: (0, 0, 0)), vec, vec],
        out_specs=[pl.BlockSpec((tm, 2 * C), lambda i: (i, 0)), pl.BlockSpec((tm, C), lambda i: (i, 0))],
        out_shape=[jax.ShapeDtypeStruct((T, 2 * C), BF16), jax.ShapeDtypeStruct((T, C), F32)],
        scratch_shapes=[pltpu.VMEM((8, tm + HALO + 8, C), F32), pltpu.VMEM((tm + HALO, C), F32),
                        pltpu.VMEM((tm, C), BF16)],
        compiler_params=_cp("parallel"), name="ev_fwd")(u, u, u, u, u, u, cw, cb, lg, lb, pw, pb, ps)


def _ev_bwd(dyab, yc, u, cw, lg, lb, pw, pwt, pb, ps):
    T = u.shape[0]
    C = 512
    tm = _tile(T, EV_ROWS, HALO)
    nsub = tm // HALO
    hb = tm // HALO
    nblk = T // tm
    E = tm + HALO

    def body(dya_ref, dyb_ref, dyah_ref, dybh_ref, yc_ref, ych_ref, val_ref, gate_ref, pin_ref, valh_ref, gateh_ref,
             pinh_ref, cw_ref, lg_ref, lb_ref, pw_ref, pwt_ref, pb_ref, ps_ref,
             du_ref, dcw_ref, dvec_ref, dpw_ref,
             dy_ext, a_ext, p_ext, ddc_ext, dd_buf, d_buf, dpre_buf, dcw_acc, vec_acc):
        i = pl.program_id(0)
        nf = (i > 0).astype(F32)
        nl = (i < nblk - 1).astype(F32)

        @pl.when(i == 0)
        def _():
            dcw_ref[...] = jnp.zeros_like(dcw_ref)
            dvec_ref[...] = jnp.zeros_like(dvec_ref)
            dpw_ref[...] = jnp.zeros_like(dpw_ref)

        dcw_acc[...] = jnp.zeros_like(dcw_acc)
        vec_acc[...] = jnp.zeros_like(vec_acc)
        a_ext[0, 0:HALO, :] = valh_ref[...] * jax.nn.sigmoid(gateh_ref[...]) * nf
        a_ext[0, HALO:E, :] = val_ref[...] * jax.nn.sigmoid(gate_ref[...])
        p_ext[0:HALO, :] = pinh_ref[...] * nf
        p_ext[HALO:, :] = pin_ref[...]
        _shifted_copies(a_ext, E)

        def ln_bwd(y, dya, main):
            mu = jnp.mean(y, axis=-1, keepdims=True)
            ycen = y - mu
            rstd = lax.rsqrt(jnp.mean(ycen * ycen, axis=-1, keepdims=True) + EPS)
            yh = ycen * rstd
            z = yh * lg_ref[...] + lb_ref[...]
            sz = jax.nn.sigmoid(z)
            dz = dya * _dsilu(z, sz)
            dyh = dz * lg_ref[...]
            dy = rstd * (dyh - jnp.mean(dyh, axis=-1, keepdims=True) - yh * jnp.mean(dyh * yh, axis=-1, keepdims=True))
            if main:
                vec_acc[1] += jnp.sum((dz * yh).reshape(HALO // 8, 8, C), axis=0)
                vec_acc[2] += jnp.sum(dz.reshape(HALO // 8, 8, C), axis=0)
                vec_acc[0] += jnp.sum(dy.reshape(HALO // 8, 8, C), axis=0)
            return dy

        def pool_dd(dyb, base, main):
            dpre = dyb * ps_ref[...]
            for gi, w in enumerate(POOL_WINDOWS):
                lo, hi = gi * HEAD, (gi + 1) * HEAD
                dd = _nn(dpre[:, lo:hi].astype(BF16), pwt_ref[gi])
                cnt = _pool_counts(i * tm + base, HALO, w)
                ddc_ext[pl.ds(base, HALO), lo:hi] = dd / cnt
                if main:
                    dd_buf[pl.ds(base, HALO), lo:hi] = dd
            if main:
                dpre_buf[pl.ds(base, HALO), :] = dpre.astype(BF16)
                vec_acc[4] += jnp.sum(dpre.reshape(HALO // 8, 8, C), axis=0)

        def p1(s, carry):
            base = pl.multiple_of(s * HALO, HALO)
            dy_ext[0, pl.ds(base, HALO), :] = ln_bwd(yc_ref[pl.ds(base, HALO), :], dya_ref[pl.ds(base, HALO), :], True)
            pool_dd(dyb_ref[pl.ds(base, HALO), :], base, True)
            return carry

        lax.fori_loop(0, nsub, p1, 0)
        dy_ext[0, tm:E, :] = ln_bwd(ych_ref[...], dyah_ref[...], False) * nl
        _shifted_copies(dy_ext, E)
        dpre_h = dybh_ref[...] * ps_ref[...] * nl
        for gi, w in enumerate(POOL_WINDOWS):
            lo, hi = gi * HEAD, (gi + 1) * HEAD
            dd = _nn(dpre_h[:, lo:hi].astype(BF16), pwt_ref[gi])
            ddc_ext[tm:, lo:hi] = dd / _pool_counts(i * tm + tm, HALO, w)

        def p2(s, carry):
            base = pl.multiple_of(s * HALO, HALO)
            dy_m = dy_ext[0, pl.ds(base, HALO), :]
            da = jnp.zeros((HALO, C), F32)
            for j in range(CONV_WIDTH):
                sh = CONV_WIDTH - 1 - j
                off = 2 + j
                da = da + cw_ref[pl.ds(j, 1), :] * dy_ext[sh % 8, pl.ds(pl.multiple_of(base + sh // 8 * 8, 8), HALO), :]
                a_j = a_ext[off % 8, pl.ds(pl.multiple_of(base + off // 8 * 8, 8), HALO), :]
                dcw_acc[j] += jnp.sum((dy_m * a_j).reshape(HALO // 8, 8, C), axis=0)
            v = val_ref[pl.ds(base, HALO), :]
            g = gate_ref[pl.ds(base, HALO), :]
            sg = jax.nn.sigmoid(g)
            du_ref[pl.ds(base, HALO), 0:C] = (da * sg).astype(BF16)
            du_ref[pl.ds(base, HALO), C:2 * C] = (da * v * sg * (1.0 - sg)).astype(BF16)
            pwin = p_ext[pl.ds(base, 2 * HALO), :]
            cwin = ddc_ext[pl.ds(base, 2 * HALO), :]
            for gi, w in enumerate(POOL_WINDOWS):
                lo, hi = gi * HEAD, (gi + 1) * HEAD
                x = pwin[HALO:, lo:hi]
                tot = x
                back = cwin[0:HALO, lo:hi]
                for k in range(1, w):
                    tot = tot + pwin[HALO - k:2 * HALO - k, lo:hi]
                    back = back + cwin[k:k + HALO, lo:hi]
                cnt = _pool_counts(i * tm + base, HALO, w)
                d_buf[pl.ds(base, HALO), lo:hi] = (tot / cnt - x).astype(BF16)
                du_ref[pl.ds(base, HALO), 2 * C + lo:2 * C + hi] = (back - dd_buf[pl.ds(base, HALO), lo:hi]).astype(BF16)
            return carry

        lax.fori_loop(0, nsub, p2, 0)
        for gi in range(len(POOL_WINDOWS)):
            lo, hi = gi * HEAD, (gi + 1) * HEAD
            pre = _nn(d_buf[:, lo:hi], pw_ref[gi]) + pb_ref[:, lo:hi]
            vec_acc[3, :, lo:hi] += jnp.sum((dyb_ref[:, lo:hi] * pre).reshape(tm // 8, 8, HEAD), axis=0)
            dpw_ref[gi] += _tn(d_buf[:, lo:hi], dpre_buf[:, lo:hi])
        for j in range(CONV_WIDTH):
            dcw_ref[pl.ds(j, 1), :] += jnp.sum(dcw_acc[j], axis=0, keepdims=True)
        for r in range(5):
            dvec_ref[pl.ds(r, 1), :] += jnp.sum(vec_acc[r], axis=0, keepdims=True)

    def main(c, width=C):
        return pl.BlockSpec((tm, width), lambda i: (i, c))

    def prev(c):
        return pl.BlockSpec((HALO, C), lambda i: (jnp.maximum(i * hb - 1, 0), c))

    def nxt(c):
        return pl.BlockSpec((HALO, C), lambda i: (jnp.minimum((i + 1) * hb, T // HALO - 1), c))

    vec = pl.BlockSpec((1, C), lambda i: (0, 0))
    mat = pl.BlockSpec((4, HEAD, HEAD), lambda i: (0, 0, 0))
    return pl.pallas_call(
        body, grid=(nblk,),
        in_specs=[main(0), main(1), nxt(0), nxt(1), main(0), nxt(0), main(0), main(1), main(2), prev(0), prev(1),
                  prev(2), pl.BlockSpec((32, C), lambda i: (0, 0)), vec, vec, mat, mat, vec, vec],
        out_specs=[pl.BlockSpec((tm, 3 * C), lambda i: (i, 0)), pl.BlockSpec((32, C), lambda i: (0, 0)),
                   pl.BlockSpec((8, C), lambda i: (0, 0)), mat],
        out_shape=[jax.ShapeDtypeStruct((T, 3 * C), BF16), jax.ShapeDtypeStruct((32, C), F32),
                   jax.ShapeDtypeStruct((8, C), F32), jax.ShapeDtypeStruct((4, HEAD, HEAD), F32)],
        scratch_shapes=[pltpu.VMEM((8, E + 8, C), F32), pltpu.VMEM((8, E + 8, C), F32), pltpu.VMEM((E, C), F32),
                        pltpu.VMEM((E, C), F32), pltpu.VMEM((tm, C), F32), pltpu.VMEM((tm, C), BF16),
                        pltpu.VMEM((tm, C), BF16), pltpu.VMEM((32, 8, C), F32), pltpu.VMEM((8, 8, C), F32)],
        compiler_params=_cp("arbitrary"), name="ev_bwd")(
            dyab, dyab, dyab, dyab, yc, yc, u, u, u, u, u, u, cw, lg, lb, pw, pwt, pb, ps)


def _cumsum_rows(x, reverse=False):
    n = x.shape[0]
    rid = lax.broadcasted_iota(jnp.int32, (n, 1), 0)
    k = 1
    while k < n:
        if reverse:
            sh = jnp.where(rid < n - k, pltpu.roll(x, n - k, 0), 0.0)
        else:
            sh = jnp.where(rid >= k, pltpu.roll(x, k, 0), 0.0)
        x = x + sh
        k *= 2
    return x


def _hgrn_gates(qr, fr, lbv):
    sq = jax.nn.sigmoid(qr)
    sg = jax.nn.sigmoid(fr)
    fg = lbv + (1.0 - lbv) * sg
    return qr * sq, sq, sg, fg, 1.0 - fg, jnp.log(fg)


def _hgrn_fwd(u, lbv, gn):
    T = u.shape[0]
    H = 8
    RB = _tile(T, HGRN_ROWS, CHUNK)
    NC = RB // CHUNK
    NS = CHUNK // SUB

    HP = HGRN_HEADS_PER_STEP
    W = HP * HEAD

    def body(q_ref, f_ref, i_ref, g_ref, lb_ref, gn_ref, y_ref, o_ref, s0_ref, st, qs, ks, bs, vs, os_):
        rb = pl.program_id(1)

        @pl.when(rb == 0)
        def _():
            st[...] = jnp.zeros_like(st)

        tsub = lax.broadcasted_iota(jnp.int32, (SUB, 1), 0)

        def head(hh, c, rows):
            sl = slice(hh * HEAD, (hh + 1) * HEAD)
            q, _, _, _, kk, lf = _hgrn_gates(q_ref[rows, sl], f_ref[rows, sl], lb_ref[:, sl])
            v = i_ref[rows, sl]
            b = _cumsum_rows(lf)
            qs[hh] = q
            ks[hh] = kk
            bs[hh] = b
            vs[hh] = v
            st0 = st[hh]
            s0_ref[hh, c] = st0
            os_[hh] = _nt((q * jnp.exp(b)).astype(BF16), st0.astype(BF16))
            for I in range(NS):
                lo = I * SUB
                qI = qs[hh, lo:lo + SUB, :]
                bI = bs[hh, lo:lo + SUB, :]
                oI = jnp.zeros((SUB, HEAD), F32)
                if I > 0:
                    bprev = bs[hh, pl.ds(lo - 1, 1), :]
                    qt = _r16(qI * jnp.exp(bI - bprev))
                    kt = _r16(ks[hh, 0:lo, :] * jnp.exp(bprev - bs[hh, 0:lo, :]))
                    A = _nt(qt, kt)
                    oI = oI + _nn(_r16(A), _r16(vs[hh, 0:lo, :]))
                for s in range(SUB):
                    row = pl.ds(lo + s, 1)
                    Es = jnp.exp(jnp.where(tsub >= s, bI - bs[hh, row, :], NEG))
                    col = jnp.sum(qI * Es * ks[hh, row, :], axis=1, keepdims=True)
                    oI = oI + col * vs[hh, row, :]
                os_[hh, lo:lo + SUB, :] += oI
            blast = bs[hh, pl.ds(CHUNK - 1, 1), :]
            kh = kk * jnp.exp(blast - b)
            st[hh] = st0 * jnp.exp(blast) + _tn(v.astype(BF16), kh.astype(BF16))
            o = os_[hh]
            o_ref[rows, sl] = o
            rr = lax.rsqrt(jnp.mean(o * o, axis=-1, keepdims=True) + EPS)
            gr = g_ref[rows, sl]
            y_ref[rows, sl] = (((o * rr) * gn_ref[...]) * (gr * jax.nn.sigmoid(gr))).astype(BF16)

        def chunk(c, carry):
            rows = pl.ds(pl.multiple_of(c * CHUNK, CHUNK), CHUNK)
            for hh in range(HP):
                head(hh, c, rows)
            return carry

        lax.fori_loop(0, NC, chunk, 0)

    def blk(q):
        return pl.BlockSpec((RB, W), lambda h, r: (r, q * (H // HP) + h))

    sc = lambda: pltpu.VMEM((HP, CHUNK, HEAD), F32)
    return pl.pallas_call(
        body, grid=(H // HP, T // RB),
        in_specs=[blk(0), blk(1), blk(2), blk(3), pl.BlockSpec((1, W), lambda h, r: (0, h)),
                  pl.BlockSpec((1, HEAD), lambda h, r: (0, 0))],
        out_specs=[pl.BlockSpec((RB, W), lambda h, r: (r, h)), pl.BlockSpec((RB, W), lambda h, r: (r, h)),
                   pl.BlockSpec((HP, NC, HEAD, HEAD), lambda h, r: (h, r, 0, 0))],
        out_shape=[jax.ShapeDtypeStruct((T, H * HEAD), BF16), jax.ShapeDtypeStruct((T, H * HEAD), F32),
                   jax.ShapeDtypeStruct((H, T // CHUNK, HEAD, HEAD), F32)],
        scratch_shapes=[pltpu.VMEM((HP, HEAD, HEAD), F32), sc(), sc(), sc(), sc(), sc()],
        compiler_params=_cp("parallel", "arbitrary"), name="hgrn_fwd")(u, u, u, u, lbv, gn)


def _hgrn_bwd(dy, o, s0, u, lbv, gn):
    T = u.shape[0]
    H = 8
    RB = _tile(T, HGRN_ROWS, CHUNK)
    NB = T // RB
    NC = RB // CHUNK
    NS = CHUNK // SUB

    def body(q_ref, f_ref, i_ref, g_ref, lb_ref, gn_ref, o_ref, dy_ref, s0_ref, du_ref, dlb_ref, dgn_ref,
             dst, qs, ks, bs, vs, dos, dqs, dks, dki, dvs, dbs):
        rb = pl.program_id(1)

        @pl.when(rb == 0)
        def _():
            dst[...] = jnp.zeros_like(dst)
            dlb_ref[...] = jnp.zeros_like(dlb_ref)
            dgn_ref[...] = jnp.zeros_like(dgn_ref)

        tsub = lax.broadcasted_iota(jnp.int32, (SUB, 1), 0)
        lane = lax.broadcasted_iota(jnp.int32, (SUB, HEAD), 1)
        gnv = gn_ref[...]

        def head(hh, c, rows):
            sl = slice(hh * HEAD, (hh + 1) * HEAD)
            lbv_ = lb_ref[:, sl]
            qr = q_ref[rows, sl]
            q, sq, sg, fg, kk, lf = _hgrn_gates(qr, f_ref[rows, sl], lbv_)
            v = i_ref[rows, sl]
            gr = g_ref[rows, sl]
            b = _cumsum_rows(lf)
            eb = jnp.exp(b)
            ov = o_ref[rows, sl]
            dyv = dy_ref[rows, sl]
            rr = lax.rsqrt(jnp.mean(ov * ov, axis=-1, keepdims=True) + EPS)
            oh = ov * rr
            gs = jax.nn.sigmoid(gr)
            dgr = dyv * (oh * gnv) * _dsilu(gr, gs)
            dnrm = dyv * (gr * gs)
            dgn_ref[hh] += jnp.sum(dnrm * oh, axis=0, keepdims=True)
            t1 = dnrm * gnv
            do = rr * (t1 - oh * jnp.mean(t1 * oh, axis=-1, keepdims=True))
            qs[hh] = q
            ks[hh] = kk
            bs[hh] = b
            vs[hh] = v
            dos[hh] = do
            st0 = s0_ref[hh, c]
            dS = dst[hh]
            do_b = do.astype(BF16)
            blast = bs[hh, pl.ds(CHUNK - 1, 1), :]
            elast = jnp.exp(blast - b)
            dq_inter = _nn(do_b, st0.astype(BF16)) * eb
            dqs[hh] = dq_inter
            dbs[hh] = q * dq_inter
            kh = kk * elast
            dvs[hh] = _nt(kh.astype(BF16), dS.astype(BF16))
            dk_inter = _nn(v.astype(BF16), dS.astype(BF16)) * elast
            dki[hh] = dk_inter
            dks[hh] = jnp.zeros((CHUNK, HEAD), F32)
            for I in range(NS):
                lo = I * SUB
                qI = qs[hh, lo:lo + SUB, :]
                bI = bs[hh, lo:lo + SUB, :]
                doI = dos[hh, lo:lo + SUB, :]
                dqI = jnp.zeros((SUB, HEAD), F32)
                dbI = jnp.zeros((SUB, HEAD), F32)
                if I > 0:
                    bprev = bs[hh, pl.ds(lo - 1, 1), :]
                    eq = jnp.exp(bI - bprev)
                    ek = jnp.exp(bprev - bs[hh, 0:lo, :])
                    qt = _r16(qI * eq)
                    kt = _r16(ks[hh, 0:lo, :] * ek)
                    A = _r16(_nt(qt, kt))
                    doI_b = _r16(doI)
                    dA = _r16(_nt(doI_b, _r16(vs[hh, 0:lo, :])))
                    dvs[hh, 0:lo, :] += _tn(A, doI_b)
                    dqt = _nn(dA, kt)
                    dkt = _tn(dA, qt)
                    dqI = dqI + dqt * eq
                    dbI = dbI + qt.astype(F32) * dqt
                    dks[hh, 0:lo, :] += dkt * ek
                    dbs[hh, 0:lo, :] -= kt.astype(F32) * dkt
                dq_d = jnp.zeros((SUB, HEAD), F32)
                a_d = jnp.zeros((SUB, HEAD), F32)
                for s in range(SUB):
                    row = pl.ds(lo + s, 1)
                    krow = ks[hh, row, :]
                    Es = jnp.exp(jnp.where(tsub >= s, bI - bs[hh, row, :], NEG))
                    qE = qI * Es
                    col = jnp.sum(qE * krow, axis=1, keepdims=True)
                    a_d = jnp.where(lane == s, col, a_d)
                    dcol = jnp.sum(doI * vs[hh, row, :], axis=1, keepdims=True)
                    dq_d = dq_d + (dcol * Es) * krow
                    dk_s = jnp.sum(dcol * qE, axis=0, keepdims=True)
                    dks[hh, row, :] += dk_s
                    dbs[hh, row, :] -= krow * dk_s
                dvs[hh, lo:lo + SUB, :] += _tn(a_d, doI)[0:SUB]
                dqI = dqI + dq_d
                dbI = dbI + qI * dq_d
                dqs[hh, lo:lo + SUB, :] += dqI
                dbs[hh, lo:lo + SUB, :] += dbI
            kdk = kk * dki[hh]
            excl = _cumsum_rows(kdk) - kdk
            suff = _cumsum_rows(dbs[hh], reverse=True)
            gdec = jnp.sum(dS * st0, axis=0, keepdims=True) * jnp.exp(blast)
            dlf = suff + excl + gdec
            dk = dks[hh] + dki[hh]
            dfg = dlf / fg - dk
            dlb_ref[:, sl] += jnp.sum(dfg * (1.0 - sg), axis=0, keepdims=True)
            du_ref[0, rows, sl] = (dqs[hh] * _dsilu(qr, sq)).astype(BF16)
            du_ref[1, rows, sl] = (dfg * (1.0 - lbv_) * sg * (1.0 - sg)).astype(BF16)
            du_ref[2, rows, sl] = dvs[hh].astype(BF16)
            du_ref[3, rows, sl] = dgr.astype(BF16)
            dst[hh] = dS * jnp.exp(blast) + _tn(do_b, (q * eb).astype(BF16))

        def chunk(cc, carry):
            c = NC - 1 - cc
            rows = pl.ds(pl.multiple_of(c * CHUNK, CHUNK), CHUNK)
            for hh in range(HP):
                head(hh, c, rows)
            return carry

        lax.fori_loop(0, NC, chunk, 0)

    HP = HGRN_HEADS_PER_STEP
    W = HP * HEAD

    def blk(qd):
        return pl.BlockSpec((RB, W), lambda h, r: (NB - 1 - r, qd * (H // HP) + h))

    hblk = pl.BlockSpec((RB, W), lambda h, r: (NB - 1 - r, h))
    sc = lambda: pltpu.VMEM((HP, CHUNK, HEAD), F32)
    return pl.pallas_call(
        body, grid=(H // HP, NB),
        in_specs=[blk(0), blk(1), blk(2), blk(3), pl.BlockSpec((1, W), lambda h, r: (0, h)),
                  pl.BlockSpec((1, HEAD), lambda h, r: (0, 0)), hblk, hblk,
                  pl.BlockSpec((HP, NC, HEAD, HEAD), lambda h, r: (h, NB - 1 - r, 0, 0))],
        out_specs=[pl.BlockSpec((4, RB, W), lambda h, r: (0, NB - 1 - r, h)),
                   pl.BlockSpec((1, W), lambda h, r: (0, h)), pl.BlockSpec((HP, 1, HEAD), lambda h, r: (h, 0, 0))],
        out_shape=[jax.ShapeDtypeStruct((4, T, H * HEAD), BF16), jax.ShapeDtypeStruct((1, H * HEAD), F32),
                   jax.ShapeDtypeStruct((H, 1, HEAD), F32)],
        scratch_shapes=[pltpu.VMEM((HP, HEAD, HEAD), F32)] + [sc() for _ in range(10)],
        compiler_params=_cp("parallel", "arbitrary"), name="hgrn_bwd")(u, u, u, u, lbv, gn, o, dy, s0)


def _softmax_rows(p_ref, L):
    rows = [p_ref[pl.ds(l, 1), :] for l in range(L)]
    m = rows[0]
    for r in rows[1:]:
        m = jnp.maximum(m, r)
    e = [jnp.exp(r - m) for r in rows]
    tot = e[0]
    for t in e[1:]:
        tot = tot + t
    return [t / tot for t in e]


def _lb_fwd(lbp):
    L, D = lbp.shape

    def body(p_ref, o_ref):
        sm = _softmax_rows(p_ref, L)
        acc = jnp.zeros((1, D), F32)
        o_ref[pl.ds(0, 1), :] = acc
        for l in range(1, L):
            acc = acc + sm[l]
            o_ref[pl.ds(l, 1), :] = acc

    return pl.pallas_call(body, out_shape=jax.ShapeDtypeStruct((L, D), F32), name="lb_fwd")(lbp)


def _lb_bwd(lbp, dlb):
    L, D = lbp.shape

    def body(p_ref, d_ref, o_ref):
        sm = _softmax_rows(p_ref, L)
        dsm = [jnp.zeros((1, D), F32)]
        for i in range(1, L):
            t = jnp.zeros((1, D), F32)
            for l in range(i, L):
                t = t + d_ref[pl.ds(l, 1), :]
            dsm.append(t)
        dot = jnp.zeros((1, D), F32)
        for i in range(L):
            dot = dot + dsm[i] * sm[i]
        for i in range(L):
            o_ref[pl.ds(i, 1), :] = sm[i] * (dsm[i] - dot)

    return pl.pallas_call(body, out_shape=jax.ShapeDtypeStruct((L, D), F32), name="lb_bwd")(lbp, dlb)


def _my_pos():
    return lax.axis_index("x"), lax.axis_index("y"), lax.axis_index("c")


def _peer(mask):
    x, y, c = _my_pos()
    mx, my, mc = (mask >> 2) & 1, (mask >> 1) & 1, mask & 1
    px = (1 - x) if mx else x
    py = (1 - y) if my else y
    pc = (1 - c) if mc else c
    return (px, py, pc), 4 * px + 2 * py + pc


def _all_gather(shards):
    n = len(shards)

    def body(*refs):
        ins, outs = refs[:n], refs[n:2 * n]
        send_sems, recv_sems, local_sems = refs[2 * n:]
        x, y, c = _my_pos()
        me = 4 * x + 2 * y + c
        local = [pltpu.make_async_copy(ins[a], outs[a].at[:, me], local_sems.at[a]) for a in range(n)]
        for cp in local:
            cp.start()
        sends = []
        for m in range(1, N_DEV):
            peer, _ = _peer(m)
            for a in range(n):
                cp = pltpu.make_async_remote_copy(
                    src_ref=ins[a], dst_ref=outs[a].at[:, me], send_sem=send_sems.at[a, m - 1],
                    recv_sem=recv_sems.at[a, m - 1], device_id=peer, device_id_type=MESH)
                cp.start()
                sends.append(cp)
        for m in range(1, N_DEV):
            peer, pid = _peer(m)
            for a in range(n):
                pltpu.make_async_remote_copy(
                    src_ref=ins[a], dst_ref=outs[a].at[:, pid], send_sem=send_sems.at[a, m - 1],
                    recv_sem=recv_sems.at[a, m - 1], device_id=peer, device_id_type=MESH).wait_recv()
        for cp in sends:
            cp.wait_send()
        for cp in local:
            cp.wait()

    out_shape = [jax.ShapeDtypeStruct((s.shape[0], N_DEV) + s.shape[1:], s.dtype) for s in shards]
    return pl.pallas_call(
        body, in_specs=[ANY] * n, out_specs=[ANY] * n, out_shape=out_shape,
        scratch_shapes=[pltpu.SemaphoreType.DMA((n, N_DEV - 1)), pltpu.SemaphoreType.DMA((n, N_DEV - 1)),
                        pltpu.SemaphoreType.DMA((n,))],
        name="all_gather_weights")(*shards)


def _exchange(grads, groups):
    n = len(grads)
    ng = 1 + max(g for g, _ in groups)
    layers = [1 + max(l for g, l in groups if g == gi) for gi in range(ng)]
    shapes = [None] * ng
    for a, (g, l) in enumerate(groups):
        shapes[g] = grads[a].shape[1:]

    def body(*refs):
        ins, outs = refs[:n], refs[n:n + ng]
        send_sems, recv_sems, local_sems = refs[n + ng:]
        x, y, c = _my_pos()
        me = 4 * x + 2 * y + c
        local = []
        for a, (g, l) in enumerate(groups):
            cp = pltpu.make_async_copy(ins[a].at[me], outs[g].at[me, l], local_sems.at[a])
            cp.start()
            local.append(cp)
        sends = []
        for m in range(1, N_DEV):
            peer, pid = _peer(m)
            for a, (g, l) in enumerate(groups):
                cp = pltpu.make_async_remote_copy(
                    src_ref=ins[a].at[pid], dst_ref=outs[g].at[me, l], send_sem=send_sems.at[a, m - 1],
                    recv_sem=recv_sems.at[a, m - 1], device_id=peer, device_id_type=MESH)
                cp.start()
                sends.append(cp)
        for m in range(1, N_DEV):
            peer, pid = _peer(m)
            for a, (g, l) in enumerate(groups):
                pltpu.make_async_remote_copy(
                    src_ref=ins[a].at[pid], dst_ref=outs[g].at[pid, l], send_sem=send_sems.at[a, m - 1],
                    recv_sem=recv_sems.at[a, m - 1], device_id=peer, device_id_type=MESH).wait_recv()
        for cp in sends:
            cp.wait_send()
        for cp in local:
            cp.wait()

    out_shape = [jax.ShapeDtypeStruct((N_DEV, layers[g]) + shapes[g], grads[[gg for gg, _ in groups].index(g)].dtype)
                 for g in range(ng)]
    return pl.pallas_call(
        body, in_specs=[ANY] * n, out_specs=[ANY] * ng, out_shape=out_shape,
        scratch_shapes=[pltpu.SemaphoreType.DMA((n, N_DEV - 1)), pltpu.SemaphoreType.DMA((n, N_DEV - 1)),
                        pltpu.SemaphoreType.DMA((n,))],
        name="exchange_grads")(*grads)


HBM_SPEC = pl.BlockSpec(memory_space=pltpu.HBM)
SEM_SPEC = pl.BlockSpec(memory_space=pltpu.SEMAPHORE)
EFFECT = pltpu.SideEffectType.DATAFLOW_SIDE_EFFECTING


def _hbm(a):
    return pltpu.with_memory_space_constraint(a, pltpu.HBM)


def _landing(own, dev, axis=0):
    if axis == 0:
        return lax.dynamic_update_slice(lax.empty((N_DEV,) + own.shape, own.dtype), own[None], (dev,) + (0,) * own.ndim)
    rows, n = own.shape
    return lax.dynamic_update_slice(lax.empty((rows, N_DEV * n), own.dtype), own, (0, dev * n))


def _slot(ref, i):
    if len(ref.shape) == 2:
        n = ref.shape[1] // N_DEV
        return ref.at[:, pl.ds(i * n, n)]
    return ref.at[i]


def _push_start(name, srcs, lands, whole, groups):
    n = len(srcs)
    ng = 1 + max(groups)
    cnt = [groups.count(g) for g in range(ng)]
    idx = [groups[:a].count(groups[a]) for a in range(n)]

    def body(*refs):
        src_refs, land_refs = refs[:n], refs[n:2 * n]
        sems = refs[2 * n:2 * n + 2 * ng]
        token = refs[-1]
        x, y, c = _my_pos()
        me = 4 * x + 2 * y + c
        for a in range(n):
            g = groups[a]
            for m in range(1, N_DEV):
                peer, pid = _peer(m)
                pltpu.make_async_remote_copy(
                    src_ref=src_refs[a] if whole else src_refs[a].at[pid], dst_ref=_slot(land_refs[a], me),
                    send_sem=sems[2 * g].at[idx[a] * (N_DEV - 1) + m - 1],
                    recv_sem=sems[2 * g + 1].at[idx[a] * (N_DEV - 1) + m - 1],
                    device_id=peer, device_id_type=MESH).start()
        token[...] = jnp.zeros_like(token)

    sem_shapes = []
    for g in range(ng):
        sem_shapes += [pltpu.SemaphoreType.DMA((cnt[g] * (N_DEV - 1),))] * 2
    thru = [pltpu.HBM(s.shape, s.dtype) for s in list(srcs) + list(lands)]
    res = pl.pallas_call(
        body, name=name,
        out_shape=tuple(sem_shapes + thru + [jax.ShapeDtypeStruct((8, 128), F32)]),
        in_specs=tuple([HBM_SPEC] * (2 * n)),
        out_specs=tuple([SEM_SPEC] * (2 * ng) + [HBM_SPEC] * (2 * n) + [pl.BlockSpec(memory_space=pltpu.VMEM)]),
        input_output_aliases={i: 2 * ng + i for i in range(2 * n)},
        compiler_params=pltpu.CompilerParams(has_side_effects=EFFECT),
    )(*[_hbm(s) for s in srcs], *[_hbm(z) for z in lands])
    sems = [(res[2 * g], res[2 * g + 1]) for g in range(ng)]
    srcs_thru = list(res[2 * ng:2 * ng + n])
    lands_thru = list(res[2 * ng + n:2 * ng + 2 * n])
    return sems, srcs_thru, lands_thru, res[-1]


def _push_wait(name, srcs_thru, lands_thru, sems, after, whole):
    n = len(srcs_thru)

    def body(*refs):
        src_refs, land_refs = refs[:n], refs[n:2 * n]
        send_sems, recv_sems = refs[2 * n], refs[2 * n + 1]
        for a in range(n):
            for m in range(1, N_DEV):
                peer, pid = _peer(m)
                cp = pltpu.make_async_remote_copy(
                    src_ref=src_refs[a] if whole else src_refs[a].at[pid], dst_ref=_slot(land_refs[a], pid),
                    send_sem=send_sems.at[a * (N_DEV - 1) + m - 1], recv_sem=recv_sems.at[a * (N_DEV - 1) + m - 1],
                    device_id=peer, device_id_type=MESH)
                cp.wait_send()
                cp.wait_recv()

    thru = [pltpu.HBM(s.shape, s.dtype) for s in list(srcs_thru) + list(lands_thru)]
    res = pl.pallas_call(
        body, name=name, out_shape=tuple(thru),
        in_specs=tuple([HBM_SPEC] * (2 * n) + [SEM_SPEC, SEM_SPEC, ANY]),
        out_specs=tuple([HBM_SPEC] * (2 * n)),
        input_output_aliases={i: i for i in range(2 * n)},
        compiler_params=pltpu.CompilerParams(has_side_effects=EFFECT),
    )(*srcs_thru, *lands_thru, sems[0], sems[1], after)
    return list(res[n:])


def _adamw(recv, w, m, v, layer=0, prev=None):
    L, R, C = w.shape
    tr = _tile(R, max(8, (1 << 18) // C), 8) if R % 8 == 0 else R
    bc1 = 1.0 - ADAM_B1 ** ADAM_STEP
    bc2 = 1.0 - ADAM_B2 ** ADAM_STEP
    if prev is None:
        prev = [lax.empty((L, R, C), F32) for _ in range(4)]

    def body(r_ref, w_ref, m_ref, v_ref, p0, p1, p2, p3, g_ref, d_ref, nm_ref, nv_ref):
        g = r_ref[0].astype(F32)
        for s in range(1, N_DEV):
            g = g + r_ref[s].astype(F32)
        nm = ADAM_B1 * m_ref[...] + (1.0 - ADAM_B1) * g
        nv = ADAM_B2 * v_ref[...] + (1.0 - ADAM_B2) * (g * g)
        mh = nm / bc1
        vh = nv / bc2
        g_ref[...] = g
        d_ref[...] = -ADAM_LR * (mh / (jnp.sqrt(vh) + ADAM_EPS) + ADAM_WD * w_ref[...])
        nm_ref[...] = nm
        nv_ref[...] = nv

    row = pl.BlockSpec((None, tr, C), lambda i: (layer, i, 0))
    return pl.pallas_call(
        body, grid=(R // tr,),
        in_specs=[pl.BlockSpec((N_DEV, tr, C), lambda i: (0, i, 0)), row, row, row] + [ANY] * 4,
        out_specs=[row] * 4, out_shape=[jax.ShapeDtypeStruct((L, R, C), F32)] * 4,
        input_output_aliases={4: 0, 5: 1, 6: 2, 7: 3},
        compiler_params=_cp("parallel"), name="adamw")(recv, w, m, v, *prev)


def _full_w_spec(tk, tn):
    return pl.BlockSpec((tk, tn), lambda i, j, k: (k, j))


def _colblk_w_spec(n):
    def spec(tk, tn):
        per = n // tn
        return pl.BlockSpec((None, tk, tn), lambda i, j, k: (j // per, k, j % per))
    return spec


def kernel(x, meta_tokens, mix_norm_g, mlp_norm_g, final_norm_g, ev_w_in, ev_conv_w, ev_conv_b, ev_ln_g, ev_ln_b, ev_pool_w, ev_pool_b, ev_pool_scale, ev_w_out, od_w_in, od_gnorm_g, od_w_out, lb_param, mlp_w1, mlp_w2, loss_target, m_meta_tokens, m_mix_norm_g, m_mlp_norm_g, m_final_norm_g, m_ev_w_in, m_ev_conv_w, m_ev_conv_b, m_ev_ln_g, m_ev_ln_b, m_ev_pool_w, m_ev_pool_b, m_ev_pool_scale, m_ev_w_out, m_od_w_in, m_od_gnorm_g, m_od_w_out, m_lb_param, m_mlp_w1, m_mlp_w2, v_meta_tokens, v_mix_norm_g, v_mlp_norm_g, v_final_norm_g, v_ev_w_in, v_ev_conv_w, v_ev_conv_b, v_ev_ln_g, v_ev_ln_b, v_ev_pool_w, v_ev_pool_b, v_ev_pool_scale, v_ev_w_out, v_od_w_in, v_od_gnorm_g, v_od_w_out, v_lb_param, v_mlp_w1, v_mlp_w2):
    S, D = x.shape[1], x.shape[2]
    T = PAD + N_META + S
    DEPTH = mix_norm_g.shape[0]
    DFF = mlp_w1.shape[2] * N_DEV
    dev = 4 * lax.axis_index("x") + 2 * lax.axis_index("y") + lax.axis_index("c")

    g_meta, g_cw = _all_gather([meta_tokens[None], ev_conv_w])
    n_ev = ev_w_in.shape[0]
    n_od = od_w_in.shape[0]
    meta_full = jnp.transpose(g_meta[0], (1, 0, 2)).reshape(N_META, D)
    cw_full = jnp.transpose(g_cw, (0, 2, 1, 3)).reshape(n_ev, CONV_WIDTH, -1)
    cw_pad = jnp.pad(cw_full, ((0, 0), (0, 32 - CONV_WIDTH), (0, 0)))
    n_in_od = od_w_in.shape[2]
    n_w1 = mlp_w1.shape[2]

    ag_src, ag_grp, ag_axis, ag_at = [], [], [], {}
    for layer in range(DEPTH):
        j = layer // 2
        mixer = [("in", ev_w_in[j]), ("out", ev_w_out[j])] if layer % 2 == 0 else [("in", od_w_in[j]), ("out", od_w_out[j])]
        for pos, (key, arr) in enumerate(mixer + [("w1", mlp_w1[layer]), ("w2", mlp_w2[layer])]):
            ag_at[layer, key] = len(ag_src)
            ag_src.append(arr.astype(BF16))
            ag_grp.append(2 * layer + pos // 2)
            ag_axis.append(1 if key in ("in", "w1") and arr.shape[1] % 128 == 0 else 0)
    ag_src, g_meta, g_cw = lax.optimization_barrier((ag_src, g_meta, g_cw))
    ag_sems, ag_s, ag_l, ag_tok = _push_start(
        "ag_start", ag_src, [_landing(s_, dev, ax) for s_, ax in zip(ag_src, ag_axis)], True, ag_grp)

    def ag_wait(group, after):
        ids = [a for a in range(len(ag_src)) if ag_grp[a] == group]
        got = _push_wait(f"ag_wait_{group}", [ag_s[a] for a in ids], [ag_l[a] for a in ids], ag_sems[group], after, True)
        return dict(zip(ids, got))

    h = jnp.concatenate([jnp.zeros((PAD, D), F32), meta_full, x[0]], axis=0) + ag_tok[0, 0]
    tgt = jnp.pad(loss_target[0], ((PAD + N_META, 0), (0, 0)))
    lb_all = _lb_fwd(lb_param)

    tm_big = _tile(T, MM_ROWS_BIG, 16)
    tm_mid = _tile(T, MM_ROWS_MID, 16)
    tm_k4 = _tile(T, MM_ROWS_K4, 16)

    saved = []
    for layer in range(DEPTH):
        j = layer // 2
        sv = {"h0": h}
        got = ag_wait(2 * layer, h)
        g_in, g_out = got[ag_at[layer, "in"]], got[ag_at[layer, "out"]]
        w_out = g_out.reshape(-1, D)
        n = _rms_fwd(h, mix_norm_g[layer][None])
        sv["n"] = n
        w_in = g_in if g_in.ndim == 2 else jnp.transpose(g_in, (1, 0, 2)).reshape(D, -1)
        if layer % 2 == 0:
            u = _mm_nn("ev_in", n, w_in, _full_w_spec, T, w_in.shape[1], D, tm_big, 512, D, "f32")
            yab, yc = _ev_fwd(u, cw_pad[j], ev_conv_b[j][None], ev_ln_g[j][None], ev_ln_b[j][None],
                              ev_pool_w[j].astype(BF16), ev_pool_b[j].reshape(1, -1), ev_pool_scale[j][None])
            sv.update(u=u, y=yab, yc=yc)
            h = _mm_nn("ev_out", yab, w_out, _full_w_spec, T, D, D, tm_mid, D, D, "resid", extra=h)
        else:
            u = _mm_nn("od_in", n, w_in, _full_w_spec, T, N_DEV * n_in_od, D, tm_big, 512, D, "f32")
            y, o, s0 = _hgrn_fwd(u, lb_all[layer][None], od_gnorm_g[j][None])
            sv.update(u=u, y=y, o=o, s0=s0)
            h = _mm_nn("od_out", y, w_out, _full_w_spec, T, D, D, tm_mid, D, D, "resid", extra=h)
        sv["h1"] = h
        got = ag_wait(2 * layer + 1, h)
        w_w1 = got[ag_at[layer, "w1"]]
        w_w2 = got[ag_at[layer, "w2"]].reshape(DFF, D)
        sv.update(w_in=w_in, w_out=w_out, w_w1=w_w1, w_w2=w_w2)
        n2 = _rms_fwd(h, mlp_norm_g[layer][None])
        r, act = _mm_nn("mlp_w1", n2, w_w1, _full_w_spec, T, DFF, D, tm_big, 512, D, "relu2")
        sv.update(n2=n2, r=r, act=act)
        h = _mm_nn("mlp_w2", act, w_w2, _full_w_spec, T, D, DFF, tm_k4, D, DFF, "resid", extra=h)
        saved.append(sv)

    loss_blk, dh, dhb, dg_final = _loss_head(h, final_norm_g[None], tgt)
    loss = lax.psum(loss_blk[0, 0], AXES)

    tt = T
    g_mix, g_mlp = [None] * DEPTH, [None] * DEPTH
    small ={"cw": [None] * n_ev, "vec": [None] * n_ev, "pw": [None] * n_ev, "gn": [None] * n_od}
    dlb_rows = [jnp.zeros((1, D), F32) for _ in range(DEPTH)]

    def xs2(tt_, tk):
        return pl.BlockSpec((tt_, tk), lambda a, b, t: (t, a))

    def ys2(tt_, tn):
        return pl.BlockSpec((tt_, tn), lambda a, b, t: (t, b))

    def os2(tk, tn):
        return pl.BlockSpec((tk, tn), lambda a, b, t: (a, b))

    def os3(tk, tn):
        return pl.BlockSpec((None, tk, tn), lambda a, b, t: (b, a, 0))

    def dy2(tm, tn):
        return pl.BlockSpec((tm, tn), lambda i, jj, k: (i, k))

    def w_rows(tj, tn):
        return pl.BlockSpec((tj, tn), lambda i, jj, k: (jj, k))

    def w_colblk(tj, tn):
        return pl.BlockSpec((None, tj, tn), lambda i, jj, k: (k, jj, 0))

    rs_pending = []

    def rs_start(tag, mats):
        blocks = [m_ if m_.ndim == 3 else m_.reshape(N_DEV, m_.shape[0] // N_DEV, m_.shape[1]) for m_ in mats]
        lands = [_landing(lax.dynamic_index_in_dim(b_, dev, 0, keepdims=False), dev) for b_ in blocks]
        sems, s_thru, l_thru, tok = _push_start(f"rs_start_{tag}", blocks, lands, False, [0] * len(blocks))
        rs_pending.append((tag, s_thru, l_thru, sems[0]))
        return tok[0, 0]

    for layer in reversed(range(DEPTH)):
        j = layer // 2
        sv = saved[layer]
        da1 = _mm_nt("mlp_w2_t", dhb, sv["w_w2"], dy2, w_rows, T, DFF, D, tm_mid, 1024, D, "dact", extra=sv["r"])
        dw2 = _mm_tn("mlp_dw2", sv["act"], dhb, xs2, ys2, os2, (DFF, D), T, DFF, D, tt, 512, D)
        dw1 = _mm_tn("mlp_dw1", sv["n2"], da1, xs2, ys2, os3, (N_DEV, D, n_w1), T, D, DFF, tt, D, n_w1)
        tok = rs_start(f"mlp{layer}", [dw1, dw2])
        dh, dhb, g_mlp[layer] = _mm_nt("mlp_w1_t", da1, sv["w_w1"], dy2, w_rows, T, D, DFF, tm_k4, D, DFF, "rms",
                                       extra=(sv["h1"], mlp_norm_g[layer][None] + tok, dh))
        if layer % 2 == 0:
            dyab = _mm_nt("ev_out_t", dhb, sv["w_out"], dy2, w_rows, T, D, D, tm_mid, D, D, "f32")
            dwout = _mm_tn("ev_dwout", sv["y"], dhb, xs2, ys2, os2, (D, D), T, D, D, tt, 512, D)
            du, small["cw"][j], small["vec"][j], small["pw"][j] = _ev_bwd(
                dyab, sv["yc"], sv["u"], cw_pad[j], ev_ln_g[j][None], ev_ln_b[j][None], ev_pool_w[j].astype(BF16),
                jnp.transpose(ev_pool_w[j], (0, 2, 1)).astype(BF16), ev_pool_b[j].reshape(1, -1),
                ev_pool_scale[j][None])
            nin = du.shape[1]
            dwin = _mm_tn("ev_dwin", sv["n"], du, xs2, ys2, os2, (D, nin), T, D, nin, tt, D, 512)
            dwin = jnp.transpose(dwin.reshape(D, N_DEV, nin // N_DEV), (1, 0, 2))
            tok = rs_start(f"mix{layer}", [dwin, dwout])
            dh, dhb, g_mix[layer] = _mm_nt("ev_in_t", du, sv["w_in"], dy2, w_rows, T, D, nin, tm_k4, D, nin, "rms",
                                           extra=(sv["h0"], mix_norm_g[layer][None] + tok, dh))
        else:
            dy = _mm_nt("od_out_t", dhb, sv["w_out"], dy2, w_rows, T, D, D, tm_mid, D, D, "f32")
            dwout = _mm_tn("od_dwout", sv["y"], dhb, xs2, ys2, os2, (D, D), T, D, D, tt, 512, D)
            du3, dlb_rows[layer], small["gn"][j] = _hgrn_bwd(dy, sv["o"], sv["s0"], sv["u"], lb_all[layer][None],
                                                              od_gnorm_g[j][None])
            per = D // n_in_od

            def du_t(tt_, tn):
                return pl.BlockSpec((None, tt_, tn), lambda a, b, t: (b // per, t, b % per))

            dwin = _mm_tn("od_dwin", sv["n"], du3, xs2, du_t, os3, (N_DEV, D, n_in_od), T, D, 4 * D, tt, D, n_in_od)
            tok = rs_start(f"mix{layer}", [dwin, dwout])
            dh, dhb, g_mix[layer] = _mm_nt(
                "od_in_t", du3, sv["w_in"], lambda tm, tn: pl.BlockSpec((4, tm, tn // 4), lambda i, jj, k: (0, i, 0)),
                w_rows, T, D, 4 * D, tm_k4, D, 4 * D, "rms", extra=(sv["h0"], mix_norm_g[layer][None] + tok, dh),
                parts=4)

    dmeta = dh[PAD:PAD + N_META]
    grad_x = dh[PAD + N_META:][None]
    dlb_param = _lb_bwd(lb_param, jnp.concatenate(dlb_rows, axis=0))

    pieces = [
        ("meta", dmeta), ("mix", jnp.concatenate(g_mix, 0)), ("mlp", jnp.concatenate(g_mlp, 0)), ("final", dg_final),
        ("cw", jnp.stack([c[:CONV_WIDTH] for c in small["cw"]])), ("cb", jnp.stack([v_[0] for v_ in small["vec"]])),
        ("lng", jnp.stack([v_[1] for v_ in small["vec"]])), ("lnb", jnp.stack([v_[2] for v_ in small["vec"]])),
        ("pw", jnp.stack(small["pw"])), ("pb", jnp.stack([v_[4] for v_ in small["vec"]])),
        ("ps", jnp.stack([v_[3] for v_ in small["vec"]])), ("gn", jnp.stack([jnp.sum(g_, axis=0)[0] for g_ in small["gn"]])),
        ("lb", dlb_param),
    ]
    flat = jnp.concatenate([p.reshape(-1) for _, p in pieces])
    n_small = flat.shape[0]
    rows_small = -(-n_small // 1024 // 8) * 8
    flat = jnp.pad(flat, (0, rows_small * 1024 - n_small)).reshape(rows_small, 1024)

    recv_small = _exchange([jnp.broadcast_to(flat[None], (N_DEV,) + flat.shape)], [(0, 0)])[0]
    recv = {}
    for tag, s_thru, l_thru, sems in rs_pending:
        got = _push_wait(f"rs_wait_{tag}", s_thru, l_thru, sems, recv_small, False)
        layer = int(tag[3:])
        if tag.startswith("mlp"):
            recv["w1", layer], recv["w2", layer] = got
        else:
            key = "ev" if layer % 2 == 0 else "od"
            recv[key + "_in", layer // 2], recv[key + "_out", layer // 2] = got

    outs = {}
    big = {"ev_in": ("ev_w_in", ev_w_in, m_ev_w_in, v_ev_w_in), "ev_out": ("ev_w_out", ev_w_out, m_ev_w_out, v_ev_w_out),
           "od_in": ("od_w_in", od_w_in, m_od_w_in, v_od_w_in), "od_out": ("od_w_out", od_w_out, m_od_w_out, v_od_w_out),
           "w1": ("mlp_w1", mlp_w1, m_mlp_w1, v_mlp_w1), "w2": ("mlp_w2", mlp_w2, m_mlp_w2, v_mlp_w2)}
    for key, (name, w, m, v) in big.items():
        res = None
        for l in range(w.shape[0]):
            res = _adamw(recv[key, l], w, m, v, layer=l, prev=res)
        outs[name] = res

    small_params = {
        "meta": ("meta_tokens", None), "mix": ("mix_norm_g", mix_norm_g, m_mix_norm_g, v_mix_norm_g),
        "mlp": ("mlp_norm_g", mlp_norm_g, m_mlp_norm_g, v_mlp_norm_g),
        "final": ("final_norm_g", final_norm_g, m_final_norm_g, v_final_norm_g),
        "cw": ("ev_conv_w", None), "cb": ("ev_conv_b", ev_conv_b, m_ev_conv_b, v_ev_conv_b),
        "lng": ("ev_ln_g", ev_ln_g, m_ev_ln_g, v_ev_ln_g), "lnb": ("ev_ln_b", ev_ln_b, m_ev_ln_b, v_ev_ln_b),
        "pw": ("ev_pool_w", ev_pool_w, m_ev_pool_w, v_ev_pool_w), "pb": ("ev_pool_b", ev_pool_b, m_ev_pool_b, v_ev_pool_b),
        "ps": ("ev_pool_scale", ev_pool_scale, m_ev_pool_scale, v_ev_pool_scale),
        "gn": ("od_gnorm_g", od_gnorm_g, m_od_gnorm_g, v_od_gnorm_g), "lb": ("lb_param", lb_param, m_lb_param, v_lb_param),
    }
    csh = ev_conv_w.shape[2]
    msh = meta_tokens.shape[1]

    def packed(which):
        parts = []
        for key, g_ in pieces:
            ent = small_params[key]
            if key == "meta":
                src = (meta_tokens, m_meta_tokens, v_meta_tokens)[which]
                full = lax.dynamic_update_slice(jnp.zeros((N_META, D), F32), src, (0, dev * msh))
            elif key == "cw":
                src = (ev_conv_w, m_ev_conv_w, v_ev_conv_w)[which]
                full = lax.dynamic_update_slice(jnp.zeros(g_.shape, F32), src, (0, 0, dev * csh))
            else:
                full = ent[1 + which]
            parts.append(full.reshape(-1))
        f = jnp.concatenate(parts)
        return jnp.pad(f, (0, rows_small * 1024 - n_small)).reshape(rows_small, 1024)

    sres = [r_[0] for r_ in _adamw(recv_small[:, 0], packed(0)[None], packed(1)[None], packed(2)[None])]
    off = 0
    for key, g_ in pieces:
        size = g_.size
        vals = [r_.reshape(-1)[off:off + size].reshape(g_.shape) for r_ in sres]
        off += size
        name = small_params[key][0]
        if key == "meta":
            vals = [lax.dynamic_slice(v_, (0, dev * msh), (N_META, msh)) for v_ in vals]
        elif key == "cw":
            vals = [lax.dynamic_slice(v_, (0, 0, dev * csh), v_.shape[:2] + (csh,)) for v_ in vals]
        else:
            vals = [v_.reshape(small_params[key][1].shape) for v_ in vals]
        outs[name] = vals

    names = ["meta_tokens", "mix_norm_g", "mlp_norm_g", "final_norm_g", "ev_w_in", "ev_conv_w", "ev_conv_b", "ev_ln_g",
             "ev_ln_b", "ev_pool_w", "ev_pool_b", "ev_pool_scale", "ev_w_out", "od_w_in", "od_gnorm_g", "od_w_out",
             "lb_param", "mlp_w1", "mlp_w2"]
    result = [loss, grad_x]
    for k in range(4):
        result += [outs[nm][k] for nm in names]
    return tuple(result)
```

```python
import functools

import jax
import jax.numpy as jnp
from jax import lax
from jax.experimental import pallas as pl
from jax.experimental.pallas import tpu as pltpu

F32 = jnp.float32
BF16 = jnp.bfloat16

N_DEV = 8
N_META = 16
CHUNK = 64
PAD = CHUNK - N_META
SUB = 16
HEAD = 128
CONV_WIDTH = 31
HALO = 32
POOL_WINDOWS = (2, 4, 8, 16)
EPS = 1e-6
NEG = -1e30
ADAM_LR, ADAM_B1, ADAM_B2, ADAM_EPS, ADAM_WD, ADAM_STEP = 0.001, 0.9, 0.999, 1e-08, 0.01, 10
VMEM_LIMIT = 56 * 1024 * 1024
EV_ROWS = 416
HGRN_ROWS = 832
MM_ROWS_BIG = 2080
MM_ROWS_MID = 1040
MM_ROWS_K4 = 416
HGRN_HEADS_FWD = 4
HGRN_HEADS_BWD = 2
MESH = pl.DeviceIdType.MESH
AXES = ("x", "y", "c")
ANY = pl.BlockSpec(memory_space=pl.ANY)


def _cp(*sem):
    return pltpu.CompilerParams(dimension_semantics=sem, vmem_limit_bytes=VMEM_LIMIT)


def _tile(n, cap, mult):
    best = None
    for d in range(mult, min(n, cap) + 1, mult):
        if n % d == 0:
            best = d
    assert best is not None, (n, cap, mult)
    return best


def _nt(a, b):
    return lax.dot_general(a, b, (((1,), (1,)), ((), ())), preferred_element_type=F32)


def _tn(a, b):
    return lax.dot_general(a, b, (((0,), (0,)), ((), ())), preferred_element_type=F32)


def _nn(a, b):
    return jnp.dot(a, b, preferred_element_type=F32)


def _r16(x):
    return x.astype(BF16).astype(F32)


def _row_ids(base, n):
    return base + lax.broadcasted_iota(jnp.int32, (n, 1), 0)


def _dsilu(x, s):
    return s * (1.0 + x * (1.0 - s))


def _rms_fwd(h, g):
    T, D = h.shape
    tm = _tile(T, MM_ROWS_MID, 16)

    def body(h_ref, g_ref, n_ref):
        x = h_ref[...]
        r = lax.rsqrt(jnp.mean(x * x, axis=-1, keepdims=True) + EPS)
        n_ref[...] = ((x * r) * g_ref[...]).astype(BF16)

    return pl.pallas_call(
        body, grid=(T // tm,),
        in_specs=[pl.BlockSpec((tm, D), lambda i: (i, 0)), pl.BlockSpec((1, D), lambda i: (0, 0))],
        out_specs=pl.BlockSpec((tm, D), lambda i: (i, 0)),
        out_shape=jax.ShapeDtypeStruct((T, D), BF16),
        compiler_params=_cp("parallel"), name="rms_fwd")(h, g)


def _rms_bwd(dn, h, g, dres):
    T, D = h.shape
    tm = _tile(T, MM_ROWS_MID, 16)

    def body(dn_ref, h_ref, g_ref, dres_ref, dh_ref, dhb_ref, dg_ref):
        i = pl.program_id(0)
        x = h_ref[...]
        dn_v = dn_ref[...]
        r = lax.rsqrt(jnp.mean(x * x, axis=-1, keepdims=True) + EPS)
        xh = x * r
        dxh = dn_v * g_ref[...]
        dx = r * (dxh - xh * jnp.mean(dxh * xh, axis=-1, keepdims=True))
        keep = _row_ids(i * tm, tm) >= PAD
        dh = jnp.where(keep, dres_ref[...] + dx, 0.0)
        dh_ref[...] = dh
        dhb_ref[...] = dh.astype(BF16)

        @pl.when(i == 0)
        def _():
            dg_ref[...] = jnp.zeros_like(dg_ref)

        dg_ref[...] += jnp.sum(dn_v * xh, axis=0, keepdims=True)

    row = pl.BlockSpec((tm, D), lambda i: (i, 0))
    vec = pl.BlockSpec((1, D), lambda i: (0, 0))
    return pl.pallas_call(
        body, grid=(T // tm,),
        in_specs=[row, row, vec, row], out_specs=[row, row, vec],
        out_shape=[jax.ShapeDtypeStruct((T, D), F32), jax.ShapeDtypeStruct((T, D), BF16),
                   jax.ShapeDtypeStruct((1, D), F32)],
        compiler_params=_cp("arbitrary"), name="rms_bwd")(dn, h, g, dres)


def _loss_head(h, g, tgt):
    T, D = h.shape
    tm = _tile(T, MM_ROWS_MID, 16)
    first_x = PAD + N_META

    def body(h_ref, g_ref, t_ref, loss_ref, dh_ref, dhb_ref, dg_ref):
        i = pl.program_id(0)
        x = h_ref[...]
        r = lax.rsqrt(jnp.mean(x * x, axis=-1, keepdims=True) + EPS)
        xh = x * r
        gv = g_ref[...]
        out = xh * gv
        valid = _row_ids(i * tm, tm) >= first_x
        e = jnp.where(valid, out - t_ref[...], 0.0)
        dout = e * (1.0 / D)
        dxh = dout * gv
        dx = r * (dxh - xh * jnp.mean(dxh * xh, axis=-1, keepdims=True))
        dh_ref[...] = dx
        dhb_ref[...] = dx.astype(BF16)

        @pl.when(i == 0)
        def _():
            dg_ref[...] = jnp.zeros_like(dg_ref)
            loss_ref[...] = jnp.zeros_like(loss_ref)

        dg_ref[...] += jnp.sum(dout * xh, axis=0, keepdims=True)
        loss_ref[...] += 0.5 * jnp.sum(jnp.mean(e * e, axis=-1, keepdims=True))

    row = pl.BlockSpec((tm, D), lambda i: (i, 0))
    vec = pl.BlockSpec((1, D), lambda i: (0, 0))
    return pl.pallas_call(
        body, grid=(T // tm,),
        in_specs=[row, vec, row],
        out_specs=[pl.BlockSpec((8, 128), lambda i: (0, 0)), row, row, vec],
        out_shape=[jax.ShapeDtypeStruct((8, 128), F32), jax.ShapeDtypeStruct((T, D), F32),
                   jax.ShapeDtypeStruct((T, D), BF16), jax.ShapeDtypeStruct((1, D), F32)],
        compiler_params=_cp("arbitrary"), name="loss_head")(h, g, tgt)


def _mm_nn(name, a, w, w_spec, M, N, K, tm, tn, tk, mode, extra=None, a_spec=None):
    nk = K // tk
    if a_spec is None:
        a_spec = pl.BlockSpec((tm, tk), lambda i, j, k: (i, k))
    o_spec = pl.BlockSpec((tm, tn), lambda i, j, k: (i, j))

    def body(*refs):
        if mode == "resid":
            a_ref, w_ref, e_ref = refs[:3]
            outs = refs[3:]
        else:
            a_ref, w_ref = refs[:2]
            outs = refs[2:]
        acc_ref = outs[-1] if nk > 1 else None
        part = _nn(a_ref[...], w_ref[...])

        def finish(acc):
            if mode == "f32":
                outs[0][...] = acc
            elif mode == "relu2":
                r = jnp.maximum(acc, 0.0)
                outs[0][...] = r.astype(BF16)
                outs[1][...] = (r * r).astype(BF16)
            else:
                keep = _row_ids(pl.program_id(0) * tm, tm) >= PAD
                outs[0][...] = jnp.where(keep, e_ref[...] + acc, 0.0)

        if nk == 1:
            finish(part)
        else:
            k = pl.program_id(2)

            @pl.when(k == 0)
            def _():
                acc_ref[...] = part

            @pl.when(k > 0)
            def _():
                acc_ref[...] += part

            @pl.when(k == nk - 1)
            def _():
                finish(acc_ref[...])

    in_specs = [a_spec, w_spec(tk, tn)]
    args = [a, w]
    if mode == "resid":
        in_specs.append(o_spec)
        args.append(extra)
    if mode == "relu2":
        out_specs = [o_spec, o_spec]
        out_shape = [jax.ShapeDtypeStruct((M, N), BF16)] * 2
    else:
        out_specs = [o_spec]
        out_shape = [jax.ShapeDtypeStruct((M, N), F32)]
    scratch = [pltpu.VMEM((tm, tn), F32)] if nk > 1 else []
    res = pl.pallas_call(
        body, grid=(M // tm, N // tn, nk), in_specs=in_specs, out_specs=out_specs, out_shape=out_shape,
        scratch_shapes=scratch, compiler_params=_cp("parallel", "parallel", "arbitrary"), name=name)(*args)
    return res if mode == "relu2" else res[0]


def _mm_rms_nn(name, h, g, w, tm, tn, mode):
    M, K = h.shape
    N = w.shape[1]

    def body(h_ref, g_ref, w_ref, n_ref, *outs):
        @pl.when(pl.program_id(1) == 0)
        def _():
            x = h_ref[...]
            r = lax.rsqrt(jnp.mean(x * x, axis=-1, keepdims=True) + EPS)
            n_ref[...] = ((x * r) * g_ref[...]).astype(BF16)

        acc = _nn(n_ref[...], w_ref[...])
        if mode == "f32":
            outs[0][...] = acc
        else:
            r = jnp.maximum(acc, 0.0)
            outs[0][...] = r.astype(BF16)
            outs[1][...] = (r * r).astype(BF16)

    row = pl.BlockSpec((tm, K), lambda i, j: (i, 0))
    o_spec = pl.BlockSpec((tm, tn), lambda i, j: (i, j))
    n_out = 1 if mode == "f32" else 2
    return pl.pallas_call(
        body, grid=(M // tm, N // tn),
        in_specs=[row, pl.BlockSpec((1, K), lambda i, j: (0, 0)), pl.BlockSpec((K, tn), lambda i, j: (0, j))],
        out_specs=[row] + [o_spec] * n_out,
        out_shape=[jax.ShapeDtypeStruct((M, K), BF16)] + [jax.ShapeDtypeStruct((M, N), F32 if mode == "f32" else BF16)] * n_out,
        compiler_params=_cp("parallel", "arbitrary"), name=name)(h, g, w)


def _mm_nt(name, dy, w, dy_spec, w_spec, M, J, N, tm, tj, tn, mode, extra=None, parts=1):
    nk = N // tn
    o_spec = pl.BlockSpec((tm, tj), lambda i, j, k: (i, j))
    n_extra = {"f32": 0, "dact": 1, "rms": 3}[mode]
    if mode == "rms":
        assert nk == 1 and tj == J

    def body(*refs):
        dy_ref, w_ref = refs[:2]
        ex = refs[2:2 + n_extra]
        outs = refs[2 + n_extra:]
        acc_ref = outs[-1] if nk > 1 else None
        if parts == 1:
            part = _nt(dy_ref[...], w_ref[...])
        else:
            wq = tn // parts
            part = _nt(dy_ref[0], w_ref[:, 0:wq])
            for q in range(1, parts):
                part = part + _nt(dy_ref[q], w_ref[:, q * wq:(q + 1) * wq])

        def finish(acc):
            if mode == "f32":
                outs[0][...] = acc
            elif mode == "dact":
                outs[0][...] = (acc * (2.0 * ex[0][...].astype(F32))).astype(BF16)
            else:
                h_ref, g_ref, dres_ref = ex
                dh_ref, dhb_ref, dg_ref = outs[:3]
                i = pl.program_id(0)
                x = h_ref[...]
                r = lax.rsqrt(jnp.mean(x * x, axis=-1, keepdims=True) + EPS)
                xh = x * r
                dxh = acc * g_ref[...]
                dx = r * (dxh - xh * jnp.mean(dxh * xh, axis=-1, keepdims=True))
                keep = _row_ids(i * tm, tm) >= PAD
                dh = jnp.where(keep, dres_ref[...] + dx, 0.0)
                dh_ref[...] = dh
                dhb_ref[...] = dh.astype(BF16)

                @pl.when(i == 0)
                def _():
                    dg_ref[...] = jnp.zeros_like(dg_ref)

                dg_ref[...] += jnp.sum(acc * xh, axis=0, keepdims=True)

        if nk == 1:
            finish(part)
        else:
            k = pl.program_id(2)

            @pl.when(k == 0)
            def _():
                acc_ref[...] = part

            @pl.when(k > 0)
            def _():
                acc_ref[...] += part

            @pl.when(k == nk - 1)
            def _():
                finish(acc_ref[...])

    in_specs = [dy_spec(tm, tn), w_spec(tj, tn)]
    args = [dy, w]
    scratch = [pltpu.VMEM((tm, tj), F32)] if nk > 1 else []
    if mode == "rms":
        vec = pl.BlockSpec((1, J), lambda i, j, k: (0, 0))
        h, g, dres = extra
        res = pl.pallas_call(
            body, grid=(M // tm, 1, 1), in_specs=in_specs + [o_spec, vec, o_spec], out_specs=[o_spec, o_spec, vec],
            out_shape=[jax.ShapeDtypeStruct((M, J), F32), jax.ShapeDtypeStruct((M, J), BF16),
                       jax.ShapeDtypeStruct((1, J), F32)],
            compiler_params=_cp("arbitrary", "arbitrary", "arbitrary"), name=name)(*args, h, g, dres)
        return res
    if mode == "dact":
        in_specs.append(o_spec)
        args.append(extra)
    return pl.pallas_call(
        body, grid=(M // tm, J // tj, nk), in_specs=in_specs, out_specs=[o_spec],
        out_shape=[jax.ShapeDtypeStruct((M, J), BF16 if mode == "dact" else F32)],
        scratch_shapes=scratch, compiler_params=_cp("parallel", "parallel", "arbitrary"), name=name)(*args)[0]


def _mm_tn(name, x, dy, x_spec, dy_spec, o_spec, o_shape, T, K, N, tt, tk, tn):
    nt = T // tt

    def body(x_ref, dy_ref, o_ref, *acc):
        part = _tn(x_ref[...], dy_ref[...])
        if nt == 1:
            o_ref[...] = part.astype(BF16)
            return
        acc_ref = acc[0]
        t = pl.program_id(2)

        @pl.when(t == 0)
        def _():
            acc_ref[...] = part

        @pl.when(t > 0)
        def _():
            acc_ref[...] += part

        @pl.when(t == nt - 1)
        def _():
            o_ref[...] = acc_ref[...].astype(BF16)

    return pl.pallas_call(
        body, grid=(K // tk, N // tn, nt), in_specs=[x_spec(tt, tk), dy_spec(tt, tn)], out_specs=o_spec(tk, tn),
        out_shape=jax.ShapeDtypeStruct(o_shape, BF16), scratch_shapes=[pltpu.VMEM((tk, tn), F32)] if nt > 1 else [],
        compiler_params=_cp("parallel", "parallel", "arbitrary"), name=name)(x, dy)


def _pool_counts(base, n, w):
    pos = _row_ids(base, n) - PAD
    return jnp.clip(pos + 1, 1, w).astype(F32)


def _shifted_copies(buf, rows):
    buf[0, rows:rows + 8, :] = jnp.zeros((8, buf.shape[2]), F32)

    def blk(s, carry):
        b = pl.multiple_of(s * HALO, HALO)
        win = buf[0, pl.ds(b, HALO + 8), :]
        for r in range(1, 8):
            buf[r, pl.ds(b, HALO), :] = win[r:r + HALO]
        return carry

    lax.fori_loop(0, rows // HALO, blk, 0)


def _ev_fwd(u, cw, cb, lg, lb, pw, pb, ps):
    T = u.shape[0]
    C = 512
    tm = _tile(T, EV_ROWS, HALO)
    nsub = tm // HALO
    hb = tm // HALO

    def body(val_ref, gate_ref, pin_ref, valh_ref, gateh_ref, pinh_ref, cw_ref, cb_ref, lg_ref, lb_ref, pw_ref,
             pb_ref, ps_ref, yab_ref, yc_ref, a_ext, p_ext, d_buf):
        i = pl.program_id(0)
        nf = (i > 0).astype(F32)
        a_ext[0, 0:HALO, :] = valh_ref[...] * jax.nn.sigmoid(gateh_ref[...]) * nf
        a_ext[0, HALO:HALO + tm, :] = val_ref[...] * jax.nn.sigmoid(gate_ref[...])
        p_ext[0:HALO, :] = pinh_ref[...] * nf
        p_ext[HALO:, :] = pin_ref[...]
        _shifted_copies(a_ext, tm + HALO)

        def sub(s, carry):
            base = pl.multiple_of(s * HALO, HALO)
            acc = jnp.zeros((HALO, C), F32) + cb_ref[...]
            for j in range(CONV_WIDTH):
                off = 2 + j
                acc = acc + cw_ref[pl.ds(j, 1), :] * a_ext[off % 8, pl.ds(pl.multiple_of(base + off // 8 * 8, 8), HALO), :]
            yc_ref[pl.ds(base, HALO), :] = acc
            mu = jnp.mean(acc, axis=-1, keepdims=True)
            yc = acc - mu
            rstd = lax.rsqrt(jnp.mean(yc * yc, axis=-1, keepdims=True) + EPS)
            z = (yc * rstd) * lg_ref[...] + lb_ref[...]
            yab_ref[pl.ds(base, HALO), 0:C] = (z * jax.nn.sigmoid(z)).astype(BF16)
            pwin = p_ext[pl.ds(base, 2 * HALO), :]
            for gi, w in enumerate(POOL_WINDOWS):
                lo, hi = gi * HEAD, (gi + 1) * HEAD
                x = pwin[HALO:, lo:hi]
                tot = x
                for k in range(1, w):
                    tot = tot + pwin[HALO - k:2 * HALO - k, lo:hi]
                cnt = _pool_counts(i * tm + base, HALO, w)
                d_buf[pl.ds(base, HALO), lo:hi] = (tot / cnt - x).astype(BF16)
            return carry

        lax.fori_loop(0, nsub, sub, 0)
        for gi in range(len(POOL_WINDOWS)):
            lo, hi = gi * HEAD, (gi + 1) * HEAD
            y = _nn(d_buf[:, lo:hi], pw_ref[gi]) + pb_ref[:, lo:hi]
            yab_ref[:, C + lo:C + hi] = (y * ps_ref[:, lo:hi]).astype(BF16)

    def main(c):
        return pl.BlockSpec((tm, C), lambda i: (i, c))

    def halo(c):
        return pl.BlockSpec((HALO, C), lambda i: (jnp.maximum(i * hb - 1, 0), c))

    vec = pl.BlockSpec((1, C), lambda i: (0, 0))
    return pl.pallas_call(
        body, grid=(T // tm,),
        in_specs=[main(0), main(1), main(2), halo(0), halo(1), halo(2),
                  pl.BlockSpec((32, C), lambda i: (0, 0)), vec, vec, vec,
                  pl.BlockSpec((4, HEAD, HEAD), lambda i: (0, 0, 0)), vec, vec],
        out_specs=[pl.BlockSpec((tm, 2 * C), lambda i: (i, 0)), pl.BlockSpec((tm, C), lambda i: (i, 0))],
        out_shape=[jax.ShapeDtypeStruct((T, 2 * C), BF16), jax.ShapeDtypeStruct((T, C), F32)],
        scratch_shapes=[pltpu.VMEM((8, tm + HALO + 8, C), F32), pltpu.VMEM((tm + HALO, C), F32),
                        pltpu.VMEM((tm, C), BF16)],
        compiler_params=_cp("parallel"), name="ev_fwd")(u, u, u, u, u, u, cw, cb, lg, lb, pw, pb, ps)


def _ev_bwd(dyab, yc, u, cw, lg, lb, pw, pwt, pb, ps):
    T = u.shape[0]
    C = 512
    tm = _tile(T, EV_ROWS, HALO)
    nsub = tm // HALO
    hb = tm // HALO
    nblk = T // tm
    E = tm + HALO

    def body(dya_ref, dyb_ref, dyah_ref, dybh_ref, yc_ref, ych_ref, val_ref, gate_ref, pin_ref, valh_ref, gateh_ref,
             pinh_ref, cw_ref, lg_ref, lb_ref, pw_ref, pwt_ref, pb_ref, ps_ref,
             du_ref, dcw_ref, dvec_ref, dpw_ref,
             dy_ext, a_ext, p_ext, ddc_ext, dd_buf, d_buf, dpre_buf, dcw_acc, vec_acc):
        i = pl.program_id(0)
        nf = (i > 0).astype(F32)
        nl = (i < nblk - 1).astype(F32)

        @pl.when(i == 0)
        def _():
            dcw_ref[...] = jnp.zeros_like(dcw_ref)
            dvec_ref[...] = jnp.zeros_like(dvec_ref)
            dpw_ref[...] = jnp.zeros_like(dpw_ref)

        dcw_acc[...] = jnp.zeros_like(dcw_acc)
        vec_acc[...] = jnp.zeros_like(vec_acc)
        a_ext[0, 0:HALO, :] = valh_ref[...] * jax.nn.sigmoid(gateh_ref[...]) * nf
        a_ext[0, HALO:E, :] = val_ref[...] * jax.nn.sigmoid(gate_ref[...])
        p_ext[0:HALO, :] = pinh_ref[...] * nf
        p_ext[HALO:, :] = pin_ref[...]
        _shifted_copies(a_ext, E)

        def ln_bwd(y, dya, main):
            mu = jnp.mean(y, axis=-1, keepdims=True)
            ycen = y - mu
            rstd = lax.rsqrt(jnp.mean(ycen * ycen, axis=-1, keepdims=True) + EPS)
            yh = ycen * rstd
            z = yh * lg_ref[...] + lb_ref[...]
            sz = jax.nn.sigmoid(z)
            dz = dya * _dsilu(z, sz)
            dyh = dz * lg_ref[...]
            dy = rstd * (dyh - jnp.mean(dyh, axis=-1, keepdims=True) - yh * jnp.mean(dyh * yh, axis=-1, keepdims=True))
            if main:
                vec_acc[1] += jnp.sum((dz * yh).reshape(HALO // 8, 8, C), axis=0)
                vec_acc[2] += jnp.sum(dz.reshape(HALO // 8, 8, C), axis=0)
                vec_acc[0] += jnp.sum(dy.reshape(HALO // 8, 8, C), axis=0)
            return dy

        def pool_dd(dyb, base, main):
            dpre = dyb * ps_ref[...]
            for gi, w in enumerate(POOL_WINDOWS):
                lo, hi = gi * HEAD, (gi + 1) * HEAD
                dd = _nn(dpre[:, lo:hi].astype(BF16), pwt_ref[gi])
                cnt = _pool_counts(i * tm + base, HALO, w)
                ddc_ext[pl.ds(base, HALO), lo:hi] = dd / cnt
                if main:
                    dd_buf[pl.ds(base, HALO), lo:hi] = dd
            if main:
                dpre_buf[pl.ds(base, HALO), :] = dpre.astype(BF16)
                vec_acc[4] += jnp.sum(dpre.reshape(HALO // 8, 8, C), axis=0)

        def p1(s, carry):
            base = pl.multiple_of(s * HALO, HALO)
            dy_ext[0, pl.ds(base, HALO), :] = ln_bwd(yc_ref[pl.ds(base, HALO), :], dya_ref[pl.ds(base, HALO), :], True)
            pool_dd(dyb_ref[pl.ds(base, HALO), :], base, True)
            return carry

        lax.fori_loop(0, nsub, p1, 0)
        dy_ext[0, tm:E, :] = ln_bwd(ych_ref[...], dyah_ref[...], False) * nl
        _shifted_copies(dy_ext, E)
        dpre_h = dybh_ref[...] * ps_ref[...] * nl
        for gi, w in enumerate(POOL_WINDOWS):
            lo, hi = gi * HEAD, (gi + 1) * HEAD
            dd = _nn(dpre_h[:, lo:hi].astype(BF16), pwt_ref[gi])
            ddc_ext[tm:, lo:hi] = dd / _pool_counts(i * tm + tm, HALO, w)

        def p2(s, carry):
            base = pl.multiple_of(s * HALO, HALO)
            dy_m = dy_ext[0, pl.ds(base, HALO), :]
            da = jnp.zeros((HALO, C), F32)
            for j in range(CONV_WIDTH):
                sh = CONV_WIDTH - 1 - j
                off = 2 + j
                da = da + cw_ref[pl.ds(j, 1), :] * dy_ext[sh % 8, pl.ds(pl.multiple_of(base + sh // 8 * 8, 8), HALO), :]
                a_j = a_ext[off % 8, pl.ds(pl.multiple_of(base + off // 8 * 8, 8), HALO), :]
                dcw_acc[j] += jnp.sum((dy_m * a_j).reshape(HALO // 8, 8, C), axis=0)
            v = val_ref[pl.ds(base, HALO), :]
            g = gate_ref[pl.ds(base, HALO), :]
            sg = jax.nn.sigmoid(g)
            du_ref[pl.ds(base, HALO), 0:C] = (da * sg).astype(BF16)
            du_ref[pl.ds(base, HALO), C:2 * C] = (da * v * sg * (1.0 - sg)).astype(BF16)
            pwin = p_ext[pl.ds(base, 2 * HALO), :]
            cwin = ddc_ext[pl.ds(base, 2 * HALO), :]
            for gi, w in enumerate(POOL_WINDOWS):
                lo, hi = gi * HEAD, (gi + 1) * HEAD
                x = pwin[HALO:, lo:hi]
                tot = x
                back = cwin[0:HALO, lo:hi]
                for k in range(1, w):
                    tot = tot + pwin[HALO - k:2 * HALO - k, lo:hi]
                    back = back + cwin[k:k + HALO, lo:hi]
                cnt = _pool_counts(i * tm + base, HALO, w)
                d_buf[pl.ds(base, HALO), lo:hi] = (tot / cnt - x).astype(BF16)
                du_ref[pl.ds(base, HALO), 2 * C + lo:2 * C + hi] = (back - dd_buf[pl.ds(base, HALO), lo:hi]).astype(BF16)
            return carry

        lax.fori_loop(0, nsub, p2, 0)
        for gi in range(len(POOL_WINDOWS)):
            lo, hi = gi * HEAD, (gi + 1) * HEAD
            pre = _nn(d_buf[:, lo:hi], pw_ref[gi]) + pb_ref[:, lo:hi]
            vec_acc[3, :, lo:hi] += jnp.sum((dyb_ref[:, lo:hi] * pre).reshape(tm // 8, 8, HEAD), axis=0)
            dpw_ref[gi] += _tn(d_buf[:, lo:hi], dpre_buf[:, lo:hi])
        for j in range(CONV_WIDTH):
            dcw_ref[pl.ds(j, 1), :] += jnp.sum(dcw_acc[j], axis=0, keepdims=True)
        for r in range(5):
            dvec_ref[pl.ds(r, 1), :] += jnp.sum(vec_acc[r], axis=0, keepdims=True)

    def main(c, width=C):
        return pl.BlockSpec((tm, width), lambda i: (i, c))

    def prev(c):
        return pl.BlockSpec((HALO, C), lambda i: (jnp.maximum(i * hb - 1, 0), c))

    def nxt(c):
        return pl.BlockSpec((HALO, C), lambda i: (jnp.minimum((i + 1) * hb, T // HALO - 1), c))

    vec = pl.BlockSpec((1, C), lambda i: (0, 0))
    mat = pl.BlockSpec((4, HEAD, HEAD), lambda i: (0, 0, 0))
    return pl.pallas_call(
        body, grid=(nblk,),
        in_specs=[main(0), main(1), nxt(0), nxt(1), main(0), nxt(0), main(0), main(1), main(2), prev(0), prev(1),
                  prev(2), pl.BlockSpec((32, C), lambda i: (0, 0)), vec, vec, mat, mat, vec, vec],
        out_specs=[pl.BlockSpec((tm, 3 * C), lambda i: (i, 0)), pl.BlockSpec((32, C), lambda i: (0, 0)),
                   pl.BlockSpec((8, C), lambda i: (0, 0)), mat],
        out_shape=[jax.ShapeDtypeStruct((T, 3 * C), BF16), jax.ShapeDtypeStruct((32, C), F32),
                   jax.ShapeDtypeStruct((8, C), F32), jax.ShapeDtypeStruct((4, HEAD, HEAD), F32)],
        scratch_shapes=[pltpu.VMEM((8, E + 8, C), F32), pltpu.VMEM((8, E + 8, C), F32), pltpu.VMEM((E, C), F32),
                        pltpu.VMEM((E, C), F32), pltpu.VMEM((tm, C), F32), pltpu.VMEM((tm, C), BF16),
                        pltpu.VMEM((tm, C), BF16), pltpu.VMEM((32, 8, C), F32), pltpu.VMEM((8, 8, C), F32)],
        compiler_params=_cp("arbitrary"), name="ev_bwd")(
            dyab, dyab, dyab, dyab, yc, yc, u, u, u, u, u, u, cw, lg, lb, pw, pwt, pb, ps)


def _cumsum_rows(x, reverse=False):
    n = x.shape[0]
    rid = lax.broadcasted_iota(jnp.int32, (n, 1), 0)
    k = 1
    while k < n:
        if reverse:
            sh = jnp.where(rid < n - k, pltpu.roll(x, n - k, 0), 0.0)
        else:
            sh = jnp.where(rid >= k, pltpu.roll(x, k, 0), 0.0)
        x = x + sh
        k *= 2
    return x


def _hgrn_gates(qr, fr, lbv):
    sq = jax.nn.sigmoid(qr)
    sg = jax.nn.sigmoid(fr)
    fg = lbv + (1.0 - lbv) * sg
    return qr * sq, sq, sg, fg, 1.0 - fg, jnp.log(fg)


def _hgrn_fwd(u, lbv, gn):
    T = u.shape[0]
    H = 8
    RB = _tile(T, HGRN_ROWS, CHUNK)
    NC = RB // CHUNK
    NS = CHUNK // SUB

    HP = HGRN_HEADS_FWD
    W = HP * HEAD

    def body(q_ref, f_ref, i_ref, g_ref, lb_ref, gn_ref, y_ref, o_ref, s0_ref, st, qs, ks, bs, vs, os_):
        rb = pl.program_id(1)

        @pl.when(rb == 0)
        def _():
            st[...] = jnp.zeros_like(st)

        t8 = lax.broadcasted_iota(jnp.int32, (8, 1), 0)

        def head(hh, c, rows):
            sl = slice(hh * HEAD, (hh + 1) * HEAD)
            q, _, _, _, kk, lf = _hgrn_gates(q_ref[rows, sl], f_ref[rows, sl], lb_ref[:, sl])
            v = i_ref[rows, sl]
            b = _cumsum_rows(lf)
            qs[hh] = q
            ks[hh] = kk
            bs[hh] = b
            vs[hh] = v
            st0 = st[hh]
            s0_ref[hh, c] = st0
            os_[hh] = _nt((q * jnp.exp(b)).astype(BF16), st0.astype(BF16))
            for I in range(NS):
                lo = I * SUB
                qI = qs[hh, lo:lo + SUB, :]
                bI = bs[hh, lo:lo + SUB, :]
                oI = jnp.zeros((SUB, HEAD), F32)
                if I > 0:
                    bprev = bs[hh, pl.ds(lo - 1, 1), :]
                    qt = _r16(qI * jnp.exp(bI - bprev))
                    kt = _r16(ks[hh, 0:lo, :] * jnp.exp(bprev - bs[hh, 0:lo, :]))
                    A = _nt(qt, kt)
                    oI = oI + _nn(_r16(A), _r16(vs[hh, 0:lo, :]))
                od = [jnp.zeros((8, HEAD), F32) for _ in range(SUB // 8)]
                for s in range(SUB):
                    row = pl.ds(lo + s, 1)
                    brow, krow, vrow = bs[hh, row, :], ks[hh, row, :], vs[hh, row, :]
                    for ti in range(SUB // 8):
                        o8 = 8 * ti
                        if s > o8 + 7:
                            continue
                        d = bI[o8:o8 + 8] - brow
                        if s > o8:
                            d = jnp.where(t8 >= s - o8, d, NEG)
                        col = jnp.sum(qI[o8:o8 + 8] * jnp.exp(d) * krow, axis=1, keepdims=True)
                        od[ti] = od[ti] + col * vrow
                os_[hh, lo:lo + SUB, :] += oI + jnp.concatenate(od, axis=0)
            blast = bs[hh, pl.ds(CHUNK - 1, 1), :]
            kh = kk * jnp.exp(blast - b)
            st[hh] = st0 * jnp.exp(blast) + _tn(v.astype(BF16), kh.astype(BF16))
            o = os_[hh]
            o_ref[rows, sl] = o
            rr = lax.rsqrt(jnp.mean(o * o, axis=-1, keepdims=True) + EPS)
            gr = g_ref[rows, sl]
            y_ref[rows, sl] = (((o * rr) * gn_ref[...]) * (gr * jax.nn.sigmoid(gr))).astype(BF16)

        def chunk(c, carry):
            rows = pl.ds(pl.multiple_of(c * CHUNK, CHUNK), CHUNK)
            for hh in range(HP):
                head(hh, c, rows)
            return carry

        lax.fori_loop(0, NC, chunk, 0)

    def blk(q):
        return pl.BlockSpec((RB, W), lambda h, r: (r, q * (H // HP) + h))

    sc = lambda: pltpu.VMEM((HP, CHUNK, HEAD), F32)
    return pl.pallas_call(
        body, grid=(H // HP, T // RB),
        in_specs=[blk(0), blk(1), blk(2), blk(3), pl.BlockSpec((1, W), lambda h, r: (0, h)),
                  pl.BlockSpec((1, HEAD), lambda h, r: (0, 0))],
        out_specs=[pl.BlockSpec((RB, W), lambda h, r: (r, h)), pl.BlockSpec((RB, W), lambda h, r: (r, h)),
                   pl.BlockSpec((HP, NC, HEAD, HEAD), lambda h, r: (h, r, 0, 0))],
        out_shape=[jax.ShapeDtypeStruct((T, H * HEAD), BF16), jax.ShapeDtypeStruct((T, H * HEAD), F32),
                   jax.ShapeDtypeStruct((H, T // CHUNK, HEAD, HEAD), F32)],
        scratch_shapes=[pltpu.VMEM((HP, HEAD, HEAD), F32), sc(), sc(), sc(), sc(), sc()],
        compiler_params=_cp("parallel", "arbitrary"), name="hgrn_fwd")(u, u, u, u, lbv, gn)


def _hgrn_bwd(dy, o, s0, u, lbv, gn):
    T = u.shape[0]
    H = 8
    RB = _tile(T, HGRN_ROWS, CHUNK)
    NB = T // RB
    NC = RB // CHUNK
    NS = CHUNK // SUB

    def body(q_ref, f_ref, i_ref, g_ref, lb_ref, gn_ref, o_ref, dy_ref, s0_ref, du_ref, dlb_ref, dgn_ref,
             dst, qs, ks, bs, vs, dos, dqs, dks, dki, dvs, dbs):
        rb = pl.program_id(1)

        @pl.when(rb == 0)
        def _():
            dst[...] = jnp.zeros_like(dst)
            dlb_ref[...] = jnp.zeros_like(dlb_ref)
            dgn_ref[...] = jnp.zeros_like(dgn_ref)

        t8 = lax.broadcasted_iota(jnp.int32, (8, 1), 0)
        lane = lax.broadcasted_iota(jnp.int32, (8, HEAD), 1)
        gnv = gn_ref[...]

        def head(hh, c, rows):
            sl = slice(hh * HEAD, (hh + 1) * HEAD)
            lbv_ = lb_ref[:, sl]
            qr = q_ref[rows, sl]
            q, sq, sg, fg, kk, lf = _hgrn_gates(qr, f_ref[rows, sl], lbv_)
            v = i_ref[rows, sl]
            gr = g_ref[rows, sl]
            b = _cumsum_rows(lf)
            eb = jnp.exp(b)
            ov = o_ref[rows, sl]
            dyv = dy_ref[rows, sl]
            rr = lax.rsqrt(jnp.mean(ov * ov, axis=-1, keepdims=True) + EPS)
            oh = ov * rr
            gs = jax.nn.sigmoid(gr)
            dgr = dyv * (oh * gnv) * _dsilu(gr, gs)
            dnrm = dyv * (gr * gs)
            dgn_ref[hh] += jnp.sum(dnrm * oh, axis=0, keepdims=True)
            t1 = dnrm * gnv
            do = rr * (t1 - oh * jnp.mean(t1 * oh, axis=-1, keepdims=True))
            qs[hh] = q
            ks[hh] = kk
            bs[hh] = b
            vs[hh] = v
            dos[hh] = do
            st0 = s0_ref[hh, c]
            dS = dst[hh]
            do_b = do.astype(BF16)
            blast = bs[hh, pl.ds(CHUNK - 1, 1), :]
            elast = jnp.exp(blast - b)
            dq_inter = _nn(do_b, st0.astype(BF16)) * eb
            dqs[hh] = dq_inter
            dbs[hh] = q * dq_inter
            kh = kk * elast
            dvs[hh] = _nt(kh.astype(BF16), dS.astype(BF16))
            dk_inter = _nn(v.astype(BF16), dS.astype(BF16)) * elast
            dki[hh] = dk_inter
            dks[hh] = jnp.zeros((CHUNK, HEAD), F32)
            for I in range(NS):
                lo = I * SUB
                qI = qs[hh, lo:lo + SUB, :]
                bI = bs[hh, lo:lo + SUB, :]
                doI = dos[hh, lo:lo + SUB, :]
                dqI = jnp.zeros((SUB, HEAD), F32)
                dbI = jnp.zeros((SUB, HEAD), F32)
                if I > 0:
                    bprev = bs[hh, pl.ds(lo - 1, 1), :]
                    eq = jnp.exp(bI - bprev)
                    ek = jnp.exp(bprev - bs[hh, 0:lo, :])
                    qt = _r16(qI * eq)
                    kt = _r16(ks[hh, 0:lo, :] * ek)
                    A = _r16(_nt(qt, kt))
                    doI_b = _r16(doI)
                    dA = _r16(_nt(doI_b, _r16(vs[hh, 0:lo, :])))
                    dvs[hh, 0:lo, :] += _tn(A, doI_b)
                    dqt = _nn(dA, kt)
                    dkt = _tn(dA, qt)
                    dqI = dqI + dqt * eq
                    dbI = dbI + qt.astype(F32) * dqt
                    dks[hh, 0:lo, :] += dkt * ek
                    dbs[hh, 0:lo, :] -= kt.astype(F32) * dkt
                dq_t = [jnp.zeros((8, HEAD), F32) for _ in range(SUB // 8)]
                a_t = [jnp.zeros((8, HEAD), F32) for _ in range(SUB // 8)]
                for s in range(SUB):
                    row = pl.ds(lo + s, 1)
                    brow, krow, vrow = bs[hh, row, :], ks[hh, row, :], vs[hh, row, :]
                    dk_s = None
                    for ti in range(SUB // 8):
                        o8 = 8 * ti
                        if s > o8 + 7:
                            continue
                        d = bI[o8:o8 + 8] - brow
                        if s > o8:
                            d = jnp.where(t8 >= s - o8, d, NEG)
                        Es = jnp.exp(d)
                        qE = qI[o8:o8 + 8] * Es
                        col = jnp.sum(qE * krow, axis=1, keepdims=True)
                        a_t[ti] = jnp.where(lane == s, col, a_t[ti])
                        dcol = jnp.sum(doI[o8:o8 + 8] * vrow, axis=1, keepdims=True)
                        dq_t[ti] = dq_t[ti] + (dcol * Es) * krow
                        part = jnp.sum(dcol * qE, axis=0, keepdims=True)
                        dk_s = part if dk_s is None else dk_s + part
                    dks[hh, row, :] += dk_s
                    dbs[hh, row, :] -= krow * dk_s
                a_d = jnp.concatenate(a_t, axis=0)
                dq_d = jnp.concatenate(dq_t, axis=0)
                dvs[hh, lo:lo + SUB, :] += _tn(a_d, doI)[0:SUB]
                dqI = dqI + dq_d
                dbI = dbI + qI * dq_d
                dqs[hh, lo:lo + SUB, :] += dqI
                dbs[hh, lo:lo + SUB, :] += dbI
            kdk = kk * dki[hh]
            excl = _cumsum_rows(kdk) - kdk
            suff = _cumsum_rows(dbs[hh], reverse=True)
            gdec = jnp.sum(dS * st0, axis=0, keepdims=True) * jnp.exp(blast)
            dlf = suff + excl + gdec
            dk = dks[hh] + dki[hh]
            dfg = dlf / fg - dk
            dlb_ref[:, sl] += jnp.sum(dfg * (1.0 - sg), axis=0, keepdims=True)
            du_ref[0, rows, sl] = (dqs[hh] * _dsilu(qr, sq)).astype(BF16)
            du_ref[1, rows, sl] = (dfg * (1.0 - lbv_) * sg * (1.0 - sg)).astype(BF16)
            du_ref[2, rows, sl] = dvs[hh].astype(BF16)
            du_ref[3, rows, sl] = dgr.astype(BF16)
            dst[hh] = dS * jnp.exp(blast) + _tn(do_b, (q * eb).astype(BF16))

        def chunk(cc, carry):
            c = NC - 1 - cc
            rows = pl.ds(pl.multiple_of(c * CHUNK, CHUNK), CHUNK)
            for hh in range(HP):
                head(hh, c, rows)
            return carry

        lax.fori_loop(0, NC, chunk, 0)

    HP = HGRN_HEADS_BWD
    W = HP * HEAD

    def blk(qd):
        return pl.BlockSpec((RB, W), lambda h, r: (NB - 1 - r, qd * (H // HP) + h))

    hblk = pl.BlockSpec((RB, W), lambda h, r: (NB - 1 - r, h))
    sc = lambda: pltpu.VMEM((HP, CHUNK, HEAD), F32)
    return pl.pallas_call(
        body, grid=(H // HP, NB),
        in_specs=[blk(0), blk(1), blk(2), blk(3), pl.BlockSpec((1, W), lambda h, r: (0, h)),
                  pl.BlockSpec((1, HEAD), lambda h, r: (0, 0)), hblk, hblk,
                  pl.BlockSpec((HP, NC, HEAD, HEAD), lambda h, r: (h, NB - 1 - r, 0, 0))],
        out_specs=[pl.BlockSpec((4, RB, W), lambda h, r: (0, NB - 1 - r, h)),
                   pl.BlockSpec((1, W), lambda h, r: (0, h)), pl.BlockSpec((HP, 1, HEAD), lambda h, r: (h, 0, 0))],
        out_shape=[jax.ShapeDtypeStruct((4, T, H * HEAD), BF16), jax.ShapeDtypeStruct((1, H * HEAD), F32),
                   jax.ShapeDtypeStruct((H, 1, HEAD), F32)],
        scratch_shapes=[pltpu.VMEM((HP, HEAD, HEAD), F32)] + [sc() for _ in range(10)],
        compiler_params=_cp("parallel", "arbitrary"), name="hgrn_bwd")(u, u, u, u, lbv, gn, o, dy, s0)


def _softmax_rows(p_ref, L):
    rows = [p_ref[pl.ds(l, 1), :] for l in range(L)]
    m = rows[0]
    for r in rows[1:]:
        m = jnp.maximum(m, r)
    e = [jnp.exp(r - m) for r in rows]
    tot = e[0]
    for t in e[1:]:
        tot = tot + t
    return [t / tot for t in e]


def _lb_fwd(lbp):
    L, D = lbp.shape

    def body(p_ref, o_ref):
        sm = _softmax_rows(p_ref, L)
        acc = jnp.zeros((1, D), F32)
        o_ref[pl.ds(0, 1), :] = acc
        for l in range(1, L):
            acc = acc + sm[l]
            o_ref[pl.ds(l, 1), :] = acc

    return pl.pallas_call(body, out_shape=jax.ShapeDtypeStruct((L, D), F32), name="lb_fwd")(lbp)


def _lb_bwd(lbp, dlb):
    L, D = lbp.shape

    def body(p_ref, d_ref, o_ref):
        sm = _softmax_rows(p_ref, L)
        dsm = [jnp.zeros((1, D), F32)]
        for i in range(1, L):
            t = jnp.zeros((1, D), F32)
            for l in range(i, L):
                t = t + d_ref[pl.ds(l, 1), :]
            dsm.append(t)
        dot = jnp.zeros((1, D), F32)
        for i in range(L):
            dot = dot + dsm[i] * sm[i]
        for i in range(L):
            o_ref[pl.ds(i, 1), :] = sm[i] * (dsm[i] - dot)

    return pl.pallas_call(body, out_shape=jax.ShapeDtypeStruct((L, D), F32), name="lb_bwd")(lbp, dlb)


def _my_pos():
    return lax.axis_index("x"), lax.axis_index("y"), lax.axis_index("c")


def _peer(mask):
    x, y, c = _my_pos()
    mx, my, mc = (mask >> 2) & 1, (mask >> 1) & 1, mask & 1
    px = (1 - x) if mx else x
    py = (1 - y) if my else y
    pc = (1 - c) if mc else c
    return (px, py, pc), 4 * px + 2 * py + pc


def _all_gather(shards):
    n = len(shards)

    def body(*refs):
        ins, outs = refs[:n], refs[n:2 * n]
        send_sems, recv_sems, local_sems = refs[2 * n:]
        x, y, c = _my_pos()
        me = 4 * x + 2 * y + c
        local = [pltpu.make_async_copy(ins[a], outs[a].at[:, me], local_sems.at[a]) for a in range(n)]
        for cp in local:
            cp.start()
        sends = []
        for m in range(1, N_DEV):
            peer, _ = _peer(m)
            for a in range(n):
                cp = pltpu.make_async_remote_copy(
                    src_ref=ins[a], dst_ref=outs[a].at[:, me], send_sem=send_sems.at[a, m - 1],
                    recv_sem=recv_sems.at[a, m - 1], device_id=peer, device_id_type=MESH)
                cp.start()
                sends.append(cp)
        for m in range(1, N_DEV):
            peer, pid = _peer(m)
            for a in range(n):
                pltpu.make_async_remote_copy(
                    src_ref=ins[a], dst_ref=outs[a].at[:, pid], send_sem=send_sems.at[a, m - 1],
                    recv_sem=recv_sems.at[a, m - 1], device_id=peer, device_id_type=MESH).wait_recv()
        for cp in sends:
            cp.wait_send()
        for cp in local:
            cp.wait()

    out_shape = [jax.ShapeDtypeStruct((s.shape[0], N_DEV) + s.shape[1:], s.dtype) for s in shards]
    return pl.pallas_call(
        body, in_specs=[ANY] * n, out_specs=[ANY] * n, out_shape=out_shape,
        scratch_shapes=[pltpu.SemaphoreType.DMA((n, N_DEV - 1)), pltpu.SemaphoreType.DMA((n, N_DEV - 1)),
                        pltpu.SemaphoreType.DMA((n,))],
        name="all_gather_weights")(*shards)


def _exchange(grads, groups):
    n = len(grads)
    ng = 1 + max(g for g, _ in groups)
    layers = [1 + max(l for g, l in groups if g == gi) for gi in range(ng)]
    shapes = [None] * ng
    for a, (g, l) in enumerate(groups):
        shapes[g] = grads[a].shape[1:]

    def body(*refs):
        ins, outs = refs[:n], refs[n:n + ng]
        send_sems, recv_sems, local_sems = refs[n + ng:]
        x, y, c = _my_pos()
        me = 4 * x + 2 * y + c
        local = []
        for a, (g, l) in enumerate(groups):
            cp = pltpu.make_async_copy(ins[a].at[me], outs[g].at[me, l], local_sems.at[a])
            cp.start()
            local.append(cp)
        sends = []
        for m in range(1, N_DEV):
            peer, pid = _peer(m)
            for a, (g, l) in enumerate(groups):
                cp = pltpu.make_async_remote_copy(
                    src_ref=ins[a].at[pid], dst_ref=outs[g].at[me, l], send_sem=send_sems.at[a, m - 1],
                    recv_sem=recv_sems.at[a, m - 1], device_id=peer, device_id_type=MESH)
                cp.start()
                sends.append(cp)
        for m in range(1, N_DEV):
            peer, pid = _peer(m)
            for a, (g, l) in enumerate(groups):
                pltpu.make_async_remote_copy(
                    src_ref=ins[a].at[pid], dst_ref=outs[g].at[pid, l], send_sem=send_sems.at[a, m - 1],
                    recv_sem=recv_sems.at[a, m - 1], device_id=peer, device_id_type=MESH).wait_recv()
        for cp in sends:
            cp.wait_send()
        for cp in local:
            cp.wait()

    out_shape = [jax.ShapeDtypeStruct((N_DEV, layers[g]) + shapes[g], grads[[gg for gg, _ in groups].index(g)].dtype)
                 for g in range(ng)]
    return pl.pallas_call(
        body, in_specs=[ANY] * n, out_specs=[ANY] * ng, out_shape=out_shape,
        scratch_shapes=[pltpu.SemaphoreType.DMA((n, N_DEV - 1)), pltpu.SemaphoreType.DMA((n, N_DEV - 1)),
                        pltpu.SemaphoreType.DMA((n,))],
        name="exchange_grads")(*grads)


HBM_SPEC = pl.BlockSpec(memory_space=pltpu.HBM)
SEM_SPEC = pl.BlockSpec(memory_space=pltpu.SEMAPHORE)
EFFECT = pltpu.SideEffectType.DATAFLOW_SIDE_EFFECTING


def _hbm(a):
    return pltpu.with_memory_space_constraint(a, pltpu.HBM)


def _landing(own, dev, axis=0):
    if axis == 0:
        return lax.dynamic_update_slice(lax.empty((N_DEV,) + own.shape, own.dtype), own[None], (dev,) + (0,) * own.ndim)
    rows, n = own.shape
    return lax.dynamic_update_slice(lax.empty((rows, N_DEV * n), own.dtype), own, (0, dev * n))


def _slot(ref, i):
    if len(ref.shape) == 2:
        n = ref.shape[1] // N_DEV
        return ref.at[:, pl.ds(i * n, n)]
    return ref.at[i]


def _push_start(name, srcs, lands, whole, groups):
    n = len(srcs)
    ng = 1 + max(groups)
    cnt = [groups.count(g) for g in range(ng)]
    idx = [groups[:a].count(groups[a]) for a in range(n)]

    def body(*refs):
        src_refs, land_refs = refs[:n], refs[n:2 * n]
        sems = refs[2 * n:2 * n + 2 * ng]
        token = refs[-1]
        x, y, c = _my_pos()
        me = 4 * x + 2 * y + c
        for a in range(n):
            g = groups[a]
            for m in range(1, N_DEV):
                peer, pid = _peer(m)
                pltpu.make_async_remote_copy(
                    src_ref=src_refs[a] if whole else src_refs[a].at[pid], dst_ref=_slot(land_refs[a], me),
                    send_sem=sems[2 * g].at[idx[a] * (N_DEV - 1) + m - 1],
                    recv_sem=sems[2 * g + 1].at[idx[a] * (N_DEV - 1) + m - 1],
                    device_id=peer, device_id_type=MESH).start()
        token[...] = jnp.zeros_like(token)

    sem_shapes = []
    for g in range(ng):
        sem_shapes += [pltpu.SemaphoreType.DMA((cnt[g] * (N_DEV - 1),))] * 2
    thru = [pltpu.HBM(s.shape, s.dtype) for s in list(srcs) + list(lands)]
    res = pl.pallas_call(
        body, name=name,
        out_shape=tuple(sem_shapes + thru + [jax.ShapeDtypeStruct((8, 128), F32)]),
        in_specs=tuple([HBM_SPEC] * (2 * n)),
        out_specs=tuple([SEM_SPEC] * (2 * ng) + [HBM_SPEC] * (2 * n) + [pl.BlockSpec(memory_space=pltpu.VMEM)]),
        input_output_aliases={i: 2 * ng + i for i in range(2 * n)},
        compiler_params=pltpu.CompilerParams(has_side_effects=EFFECT),
    )(*[_hbm(s) for s in srcs], *[_hbm(z) for z in lands])
    sems = [(res[2 * g], res[2 * g + 1]) for g in range(ng)]
    srcs_thru = list(res[2 * ng:2 * ng + n])
    lands_thru = list(res[2 * ng + n:2 * ng + 2 * n])
    return sems, srcs_thru, lands_thru, res[-1]


def _push_wait(name, srcs_thru, lands_thru, sems, after, whole):
    n = len(srcs_thru)

    def body(*refs):
        src_refs, land_refs = refs[:n], refs[n:2 * n]
        send_sems, recv_sems = refs[2 * n], refs[2 * n + 1]
        for a in range(n):
            for m in range(1, N_DEV):
                peer, pid = _peer(m)
                cp = pltpu.make_async_remote_copy(
                    src_ref=src_refs[a] if whole else src_refs[a].at[pid], dst_ref=_slot(land_refs[a], pid),
                    send_sem=send_sems.at[a * (N_DEV - 1) + m - 1], recv_sem=recv_sems.at[a * (N_DEV - 1) + m - 1],
                    device_id=peer, device_id_type=MESH)
                cp.wait_send()
                cp.wait_recv()

    thru = [pltpu.HBM(s.shape, s.dtype) for s in list(srcs_thru) + list(lands_thru)]
    res = pl.pallas_call(
        body, name=name, out_shape=tuple(thru),
        in_specs=tuple([HBM_SPEC] * (2 * n) + [SEM_SPEC, SEM_SPEC, ANY]),
        out_specs=tuple([HBM_SPEC] * (2 * n)),
        input_output_aliases={i: i for i in range(2 * n)},
        compiler_params=pltpu.CompilerParams(has_side_effects=EFFECT),
    )(*srcs_thru, *lands_thru, sems[0], sems[1], after)
    return list(res[n:])


def _adamw(recv, w, m, v, layer=0, prev=None):
    L, R, C = w.shape
    tr = _tile(R, max(8, (1 << 18) // C), 8) if R % 8 == 0 else R
    bc1 = 1.0 - ADAM_B1 ** ADAM_STEP
    bc2 = 1.0 - ADAM_B2 ** ADAM_STEP
    if prev is None:
        prev = [lax.empty((L, R, C), F32) for _ in range(4)]

    def body(r_ref, w_ref, m_ref, v_ref, p0, p1, p2, p3, g_ref, d_ref, nm_ref, nv_ref):
        g = r_ref[0].astype(F32)
        for s in range(1, N_DEV):
            g = g + r_ref[s].astype(F32)
        nm = ADAM_B1 * m_ref[...] + (1.0 - ADAM_B1) * g
        nv = ADAM_B2 * v_ref[...] + (1.0 - ADAM_B2) * (g * g)
        mh = nm / bc1
        vh = nv / bc2
        g_ref[...] = g
        d_ref[...] = -ADAM_LR * (mh / (jnp.sqrt(vh) + ADAM_EPS) + ADAM_WD * w_ref[...])
        nm_ref[...] = nm
        nv_ref[...] = nv

    row = pl.BlockSpec((None, tr, C), lambda i: (layer, i, 0))
    return pl.pallas_call(
        body, grid=(R // tr,),
        in_specs=[pl.BlockSpec((N_DEV, tr, C), lambda i: (0, i, 0)), row, row, row] + [ANY] * 4,
        out_specs=[row] * 4, out_shape=[jax.ShapeDtypeStruct((L, R, C), F32)] * 4,
        input_output_aliases={4: 0, 5: 1, 6: 2, 7: 3},
        compiler_params=_cp("parallel"), name="adamw")(recv, w, m, v, *prev)


def _full_w_spec(tk, tn):
    return pl.BlockSpec((tk, tn), lambda i, j, k: (k, j))


def _colblk_w_spec(n):
    def spec(tk, tn):
        per = n // tn
        return pl.BlockSpec((None, tk, tn), lambda i, j, k: (j // per, k, j % per))
    return spec


def kernel(x, meta_tokens, mix_norm_g, mlp_norm_g, final_norm_g, ev_w_in, ev_conv_w, ev_conv_b, ev_ln_g, ev_ln_b, ev_pool_w, ev_pool_b, ev_pool_scale, ev_w_out, od_w_in, od_gnorm_g, od_w_out, lb_param, mlp_w1, mlp_w2, loss_target, m_meta_tokens, m_mix_norm_g, m_mlp_norm_g, m_final_norm_g, m_ev_w_in, m_ev_conv_w, m_ev_conv_b, m_ev_ln_g, m_ev_ln_b, m_ev_pool_w, m_ev_pool_b, m_ev_pool_scale, m_ev_w_out, m_od_w_in, m_od_gnorm_g, m_od_w_out, m_lb_param, m_mlp_w1, m_mlp_w2, v_meta_tokens, v_mix_norm_g, v_mlp_norm_g, v_final_norm_g, v_ev_w_in, v_ev_conv_w, v_ev_conv_b, v_ev_ln_g, v_ev_ln_b, v_ev_pool_w, v_ev_pool_b, v_ev_pool_scale, v_ev_w_out, v_od_w_in, v_od_gnorm_g, v_od_w_out, v_lb_param, v_mlp_w1, v_mlp_w2):
    S, D = x.shape[1], x.shape[2]
    T = PAD + N_META + S
    DEPTH = mix_norm_g.shape[0]
    DFF = mlp_w1.shape[2] * N_DEV
    dev = 4 * lax.axis_index("x") + 2 * lax.axis_index("y") + lax.axis_index("c")

    g_meta, g_cw = _all_gather([meta_tokens[None], ev_conv_w])
    n_ev = ev_w_in.shape[0]
    n_od = od_w_in.shape[0]
    meta_full = jnp.transpose(g_meta[0], (1, 0, 2)).reshape(N_META, D)
    cw_full = jnp.transpose(g_cw, (0, 2, 1, 3)).reshape(n_ev, CONV_WIDTH, -1)
    cw_pad = jnp.pad(cw_full, ((0, 0), (0, 32 - CONV_WIDTH), (0, 0)))
    n_in_od = od_w_in.shape[2]
    n_w1 = mlp_w1.shape[2]

    ag_src, ag_grp, ag_axis, ag_at = [], [], [], {}
    for layer in range(DEPTH):
        j = layer // 2
        mixer = [("in", ev_w_in[j]), ("out", ev_w_out[j])] if layer % 2 == 0 else [("in", od_w_in[j]), ("out", od_w_out[j])]
        for pos, (key, arr) in enumerate(mixer + [("w1", mlp_w1[layer]), ("w2", mlp_w2[layer])]):
            ag_at[layer, key] = len(ag_src)
            ag_src.append(arr.astype(BF16))
            ag_grp.append(2 * layer + pos // 2)
            ag_axis.append(1 if key in ("in", "w1") and arr.shape[1] % 128 == 0 else 0)
    ag_src, g_meta, g_cw = lax.optimization_barrier((ag_src, g_meta, g_cw))
    ag_sems, ag_s, ag_l, ag_tok = _push_start(
        "ag_start", ag_src, [_landing(s_, dev, ax) for s_, ax in zip(ag_src, ag_axis)], True, ag_grp)

    def ag_wait(group, after):
        ids = [a for a in range(len(ag_src)) if ag_grp[a] == group]
        got = _push_wait(f"ag_wait_{group}", [ag_s[a] for a in ids], [ag_l[a] for a in ids], ag_sems[group], after, True)
        return dict(zip(ids, got))

    h = jnp.concatenate([jnp.zeros((PAD, D), F32), meta_full, x[0]], axis=0) + ag_tok[0, 0]
    tgt = jnp.pad(loss_target[0], ((PAD + N_META, 0), (0, 0)))
    lb_all = _lb_fwd(lb_param)

    tm_big = _tile(T, MM_ROWS_BIG, 16)
    tm_mid = _tile(T, MM_ROWS_MID, 16)
    tm_k4 = _tile(T, MM_ROWS_K4, 16)

    saved = []
    for layer in range(DEPTH):
        j = layer // 2
        sv = {"h0": h}
        got = ag_wait(2 * layer, h)
        g_in, g_out = got[ag_at[layer, "in"]], got[ag_at[layer, "out"]]
        w_out = g_out.reshape(-1, D)
        w_in = g_in if g_in.ndim == 2 else jnp.transpose(g_in, (1, 0, 2)).reshape(D, -1)
        if layer % 2 == 0:
            sv["n"], u = _mm_rms_nn("ev_in", h, mix_norm_g[layer][None], w_in, tm_mid, 512, "f32")
            yab, yc = _ev_fwd(u, cw_pad[j], ev_conv_b[j][None], ev_ln_g[j][None], ev_ln_b[j][None],
                              ev_pool_w[j].astype(BF16), ev_pool_b[j].reshape(1, -1), ev_pool_scale[j][None])
            sv.update(u=u, y=yab, yc=yc)
            h = _mm_nn("ev_out", yab, w_out, _full_w_spec, T, D, D, tm_mid, D, D, "resid", extra=h)
        else:
            sv["n"], u = _mm_rms_nn("od_in", h, mix_norm_g[layer][None], w_in, tm_mid, 1024, "f32")
            y, o, s0 = _hgrn_fwd(u, lb_all[layer][None], od_gnorm_g[j][None])
            sv.update(u=u, y=y, o=o, s0=s0)
            h = _mm_nn("od_out", y, w_out, _full_w_spec, T, D, D, tm_mid, D, D, "resid", extra=h)
        sv["h1"] = h
        got = ag_wait(2 * layer + 1, h)
        w_w1 = got[ag_at[layer, "w1"]]
        w_w2 = got[ag_at[layer, "w2"]].reshape(DFF, D)
        sv.update(w_in=w_in, w_out=w_out, w_w1=w_w1, w_w2=w_w2)
        n2, r, act = _mm_rms_nn("mlp_w1", h, mlp_norm_g[layer][None], w_w1, tm_mid, 1024, "relu2")
        sv.update(n2=n2, r=r, act=act)
        h = _mm_nn("mlp_w2", act, w_w2, _full_w_spec, T, D, DFF, tm_k4, D, DFF, "resid", extra=h)
        saved.append(sv)

    loss_blk, dh, dhb, dg_final = _loss_head(h, final_norm_g[None], tgt)
    loss = lax.psum(loss_blk[0, 0], AXES)

    tt = T
    g_mix, g_mlp = [None] * DEPTH, [None] * DEPTH
    small ={"cw": [None] * n_ev, "vec": [None] * n_ev, "pw": [None] * n_ev, "gn": [None] * n_od}
    dlb_rows = [jnp.zeros((1, D), F32) for _ in range(DEPTH)]

    def xs2(tt_, tk):
        return pl.BlockSpec((tt_, tk), lambda a, b, t: (t, a))

    def ys2(tt_, tn):
        return pl.BlockSpec((tt_, tn), lambda a, b, t: (t, b))

    def os2(tk, tn):
        return pl.BlockSpec((tk, tn), lambda a, b, t: (a, b))

    def os3(tk, tn):
        return pl.BlockSpec((None, tk, tn), lambda a, b, t: (b, a, 0))

    def dy2(tm, tn):
        return pl.BlockSpec((tm, tn), lambda i, jj, k: (i, k))

    def w_rows(tj, tn):
        return pl.BlockSpec((tj, tn), lambda i, jj, k: (jj, k))

    def w_colblk(tj, tn):
        return pl.BlockSpec((None, tj, tn), lambda i, jj, k: (k, jj, 0))

    rs_pending = []

    def rs_start(tag, mats):
        blocks = [m_ if m_.ndim == 3 else m_.reshape(N_DEV, m_.shape[0] // N_DEV, m_.shape[1]) for m_ in mats]
        lands = [_landing(lax.dynamic_index_in_dim(b_, dev, 0, keepdims=False), dev) for b_ in blocks]
        sems, s_thru, l_thru, tok = _push_start(f"rs_start_{tag}", blocks, lands, False, [0] * len(blocks))
        rs_pending.append((tag, s_thru, l_thru, sems[0]))
        return tok[0, 0]

    for layer in reversed(range(DEPTH)):
        j = layer // 2
        sv = saved[layer]
        da1 = _mm_nt("mlp_w2_t", dhb, sv["w_w2"], dy2, w_rows, T, DFF, D, tm_mid, 1024, D, "dact", extra=sv["r"])
        dw2 = _mm_tn("mlp_dw2", sv["act"], dhb, xs2, ys2, os2, (DFF, D), T, DFF, D, tt, 512, D)
        dw1 = _mm_tn("mlp_dw1", sv["n2"], da1, xs2, ys2, os3, (N_DEV, D, n_w1), T, D, DFF, tt, D, n_w1)
        tok = rs_start(f"mlp{layer}", [dw1, dw2])
        dh, dhb, g_mlp[layer] = _mm_nt("mlp_w1_t", da1, sv["w_w1"], dy2, w_rows, T, D, DFF, tm_k4, D, DFF, "rms",
                                       extra=(sv["h1"], mlp_norm_g[layer][None] + tok, dh))
        if layer % 2 == 0:
            dyab = _mm_nt("ev_out_t", dhb, sv["w_out"], dy2, w_rows, T, D, D, tm_mid, D, D, "f32")
            dwout = _mm_tn("ev_dwout", sv["y"], dhb, xs2, ys2, os2, (D, D), T, D, D, tt, 512, D)
            du, small["cw"][j], small["vec"][j], small["pw"][j] = _ev_bwd(
                dyab, sv["yc"], sv["u"], cw_pad[j], ev_ln_g[j][None], ev_ln_b[j][None], ev_pool_w[j].astype(BF16),
                jnp.transpose(ev_pool_w[j], (0, 2, 1)).astype(BF16), ev_pool_b[j].reshape(1, -1),
                ev_pool_scale[j][None])
            nin = du.shape[1]
            dwin = _mm_tn("ev_dwin", sv["n"], du, xs2, ys2, os2, (D, nin), T, D, nin, tt, D, 512)
            dwin = jnp.transpose(dwin.reshape(D, N_DEV, nin // N_DEV), (1, 0, 2))
            tok = rs_start(f"mix{layer}", [dwin, dwout])
            dh, dhb, g_mix[layer] = _mm_nt("ev_in_t", du, sv["w_in"], dy2, w_rows, T, D, nin, tm_k4, D, nin, "rms",
                                           extra=(sv["h0"], mix_norm_g[layer][None] + tok, dh))
        else:
            dy = _mm_nt("od_out_t", dhb, sv["w_out"], dy2, w_rows, T, D, D, tm_mid, D, D, "f32")
            dwout = _mm_tn("od_dwout", sv["y"], dhb, xs2, ys2, os2, (D, D), T, D, D, tt, 512, D)
            du3, dlb_rows[layer], small["gn"][j] = _hgrn_bwd(dy, sv["o"], sv["s0"], sv["u"], lb_all[layer][None],
                                                              od_gnorm_g[j][None])
            per = D // n_in_od

            def du_t(tt_, tn):
                return pl.BlockSpec((None, tt_, tn), lambda a, b, t: (b // per, t, b % per))

            dwin = _mm_tn("od_dwin", sv["n"], du3, xs2, du_t, os3, (N_DEV, D, n_in_od), T, D, 4 * D, tt, D, n_in_od)
            tok = rs_start(f"mix{layer}", [dwin, dwout])
            dh, dhb, g_mix[layer] = _mm_nt(
                "od_in_t", du3, sv["w_in"], lambda tm, tn: pl.BlockSpec((4, tm, tn // 4), lambda i, jj, k: (0, i, 0)),
                w_rows, T, D, 4 * D, tm_k4, D, 4 * D, "rms", extra=(sv["h0"], mix_norm_g[layer][None] + tok, dh),
                parts=4)

    dmeta = dh[PAD:PAD + N_META]
    grad_x = dh[PAD + N_META:][None]
    dlb_param = _lb_bwd(lb_param, jnp.concatenate(dlb_rows, axis=0))

    pieces = [
        ("meta", dmeta), ("mix", jnp.concatenate(g_mix, 0)), ("mlp", jnp.concatenate(g_mlp, 0)), ("final", dg_final),
        ("cw", jnp.stack([c[:CONV_WIDTH] for c in small["cw"]])), ("cb", jnp.stack([v_[0] for v_ in small["vec"]])),
        ("lng", jnp.stack([v_[1] for v_ in small["vec"]])), ("lnb", jnp.stack([v_[2] for v_ in small["vec"]])),
        ("pw", jnp.stack(small["pw"])), ("pb", jnp.stack([v_[4] for v_ in small["vec"]])),
        ("ps", jnp.stack([v_[3] for v_ in small["vec"]])), ("gn", jnp.stack([jnp.sum(g_, axis=0)[0] for g_ in small["gn"]])),
        ("lb", dlb_param),
    ]
    flat = jnp.concatenate([p.reshape(-1) for _, p in pieces])
    n_small = flat.shape[0]
    rows_small = -(-n_small // 1024 // 8) * 8
    flat = jnp.pad(flat, (0, rows_small * 1024 - n_small)).reshape(rows_small, 1024)

    recv_small = _exchange([jnp.broadcast_to(flat[None], (N_DEV,) + flat.shape)], [(0, 0)])[0]
    recv = {}
    for tag, s_thru, l_thru, sems in rs_pending:
        got = _push_wait(f"rs_wait_{tag}", s_thru, l_thru, sems, recv_small, False)
        layer = int(tag[3:])
        if tag.startswith("mlp"):
            recv["w1", layer], recv["w2", layer] = got
        else:
            key = "ev" if layer % 2 == 0 else "od"
            recv[key + "_in", layer // 2], recv[key + "_out", layer // 2] = got

    outs = {}
    big = {"ev_in": ("ev_w_in", ev_w_in, m_ev_w_in, v_ev_w_in), "ev_out": ("ev_w_out", ev_w_out, m_ev_w_out, v_ev_w_out),
           "od_in": ("od_w_in", od_w_in, m_od_w_in, v_od_w_in), "od_out": ("od_w_out", od_w_out, m_od_w_out, v_od_w_out),
           "w1": ("mlp_w1", mlp_w1, m_mlp_w1, v_mlp_w1), "w2": ("mlp_w2", mlp_w2, m_mlp_w2, v_mlp_w2)}
    for key, (name, w, m, v) in big.items():
        res = None
        for l in range(w.shape[0]):
            res = _adamw(recv[key, l], w, m, v, layer=l, prev=res)
        outs[name] = res

    small_params = {
        "meta": ("meta_tokens", None), "mix": ("mix_norm_g", mix_norm_g, m_mix_norm_g, v_mix_norm_g),
        "mlp": ("mlp_norm_g", mlp_norm_g, m_mlp_norm_g, v_mlp_norm_g),
        "final": ("final_norm_g", final_norm_g, m_final_norm_g, v_final_norm_g),
        "cw": ("ev_conv_w", None), "cb": ("ev_conv_b", ev_conv_b, m_ev_conv_b, v_ev_conv_b),
        "lng": ("ev_ln_g", ev_ln_g, m_ev_ln_g, v_ev_ln_g), "lnb": ("ev_ln_b", ev_ln_b, m_ev_ln_b, v_ev_ln_b),
        "pw": ("ev_pool_w", ev_pool_w, m_ev_pool_w, v_ev_pool_w), "pb": ("ev_pool_b", ev_pool_b, m_ev_pool_b, v_ev_pool_b),
        "ps": ("ev_pool_scale", ev_pool_scale, m_ev_pool_scale, v_ev_pool_scale),
        "gn": ("od_gnorm_g", od_gnorm_g, m_od_gnorm_g, v_od_gnorm_g), "lb": ("lb_param", lb_param, m_lb_param, v_lb_param),
    }
    csh = ev_conv_w.shape[2]
    msh = meta_tokens.shape[1]

    def packed(which):
        parts = []
        for key, g_ in pieces:
            ent = small_params[key]
            if key == "meta":
                src = (meta_tokens, m_meta_tokens, v_meta_tokens)[which]
                full = lax.dynamic_update_slice(jnp.zeros((N_META, D), F32), src, (0, dev * msh))
            elif key == "cw":
                src = (ev_conv_w, m_ev_conv_w, v_ev_conv_w)[which]
                full = lax.dynamic_update_slice(jnp.zeros(g_.shape, F32), src, (0, 0, dev * csh))
            else:
                full = ent[1 + which]
            parts.append(full.reshape(-1))
        f = jnp.concatenate(parts)
        return jnp.pad(f, (0, rows_small * 1024 - n_small)).reshape(rows_small, 1024)

    sres = [r_[0] for r_ in _adamw(recv_small[:, 0], packed(0)[None], packed(1)[None], packed(2)[None])]
    off = 0
    for key, g_ in pieces:
        size = g_.size
        vals = [r_.reshape(-1)[off:off + size].reshape(g_.shape) for r_ in sres]
        off += size
        name = small_params[key][0]
        if key == "meta":
            vals = [lax.dynamic_slice(v_, (0, dev * msh), (N_META, msh)) for v_ in vals]
        elif key == "cw":
            vals = [lax.dynamic_slice(v_, (0, 0, dev * csh), v_.shape[:2] + (csh,)) for v_ in vals]
        else:
            vals = [v_.reshape(small_params[key][1].shape) for v_ in vals]
        outs[name] = vals

    names = ["meta_tokens", "mix_norm_g", "mlp_norm_g", "final_norm_g", "ev_w_in", "ev_conv_w", "ev_conv_b", "ev_ln_g",
             "ev_ln_b", "ev_pool_w", "ev_pool_b", "ev_pool_scale", "ev_w_out", "od_w_in", "od_gnorm_g", "od_w_out",
             "lb_param", "mlp_w1", "mlp_w2"]
    result = [loss, grad_x]
    for k in range(4):
        result += [outs[nm][k] for nm in names]
    return tuple(result)
```

```python
import functools

import jax
import jax.numpy as jnp
from jax import lax
from jax.experimental import pallas as pl
from jax.experimental.pallas import tpu as pltpu

F32 = jnp.float32
BF16 = jnp.bfloat16

N_DEV = 8
N_META = 16
CHUNK = 64
PAD = CHUNK - N_META
SUB = 16
HEAD = 128
CONV_WIDTH = 31
HALO = 32
POOL_WINDOWS = (2, 4, 8, 16)
EPS = 1e-6
NEG = -1e30
ADAM_LR, ADAM_B1, ADAM_B2, ADAM_EPS, ADAM_WD, ADAM_STEP = 0.001, 0.9, 0.999, 1e-08, 0.01, 10
VMEM_LIMIT = 56 * 1024 * 1024
EV_ROWS = 416
HGRN_ROWS = 832
MM_ROWS_BIG = 2080
MM_ROWS_MID = 1040
MM_ROWS_K4 = 416
HGRN_HEADS_FWD = 4
HGRN_HEADS_BWD = 2
MESH = pl.DeviceIdType.MESH
AXES = ("x", "y", "c")
ANY = pl.BlockSpec(memory_space=pl.ANY)


def _cp(*sem):
    return pltpu.CompilerParams(dimension_semantics=sem, vmem_limit_bytes=VMEM_LIMIT)


def _tile(n, cap, mult):
    best = None
    for d in range(mult, min(n, cap) + 1, mult):
        if n % d == 0:
            best = d
    assert best is not None, (n, cap, mult)
    return best


def _nt(a, b):
    return lax.dot_general(a, b, (((1,), (1,)), ((), ())), preferred_element_type=F32)


def _tn(a, b):
    return lax.dot_general(a, b, (((0,), (0,)), ((), ())), preferred_element_type=F32)


def _nn(a, b):
    return jnp.dot(a, b, preferred_element_type=F32)


def _r16(x):
    return x.astype(BF16).astype(F32)


def _row_ids(base, n):
    return base + lax.broadcasted_iota(jnp.int32, (n, 1), 0)


def _dsilu(x, s):
    return s * (1.0 + x * (1.0 - s))


def _rms_fwd(h, g):
    T, D = h.shape
    tm = _tile(T, MM_ROWS_MID, 16)

    def body(h_ref, g_ref, n_ref):
        x = h_ref[...]
        r = lax.rsqrt(jnp.mean(x * x, axis=-1, keepdims=True) + EPS)
        n_ref[...] = ((x * r) * g_ref[...]).astype(BF16)

    return pl.pallas_call(
        body, grid=(T // tm,),
        in_specs=[pl.BlockSpec((tm, D), lambda i: (i, 0)), pl.BlockSpec((1, D), lambda i: (0, 0))],
        out_specs=pl.BlockSpec((tm, D), lambda i: (i, 0)),
        out_shape=jax.ShapeDtypeStruct((T, D), BF16),
        compiler_params=_cp("parallel"), name="rms_fwd")(h, g)


def _rms_bwd(dn, h, g, dres):
    T, D = h.shape
    tm = _tile(T, MM_ROWS_MID, 16)

    def body(dn_ref, h_ref, g_ref, dres_ref, dh_ref, dhb_ref, dg_ref):
        i = pl.program_id(0)
        x = h_ref[...]
        dn_v = dn_ref[...]
        r = lax.rsqrt(jnp.mean(x * x, axis=-1, keepdims=True) + EPS)
        xh = x * r
        dxh = dn_v * g_ref[...]
        dx = r * (dxh - xh * jnp.mean(dxh * xh, axis=-1, keepdims=True))
        keep = _row_ids(i * tm, tm) >= PAD
        dh = jnp.where(keep, dres_ref[...] + dx, 0.0)
        dh_ref[...] = dh
        dhb_ref[...] = dh.astype(BF16)

        @pl.when(i == 0)
        def _():
            dg_ref[...] = jnp.zeros_like(dg_ref)

        dg_ref[...] += jnp.sum(dn_v * xh, axis=0, keepdims=True)

    row = pl.BlockSpec((tm, D), lambda i: (i, 0))
    vec = pl.BlockSpec((1, D), lambda i: (0, 0))
    return pl.pallas_call(
        body, grid=(T // tm,),
        in_specs=[row, row, vec, row], out_specs=[row, row, vec],
        out_shape=[jax.ShapeDtypeStruct((T, D), F32), jax.ShapeDtypeStruct((T, D), BF16),
                   jax.ShapeDtypeStruct((1, D), F32)],
        compiler_params=_cp("arbitrary"), name="rms_bwd")(dn, h, g, dres)


def _loss_head(h, g, tgt):
    T, D = h.shape
    tm = _tile(T, MM_ROWS_MID, 16)
    first_x = PAD + N_META

    def body(h_ref, g_ref, t_ref, loss_ref, dh_ref, dhb_ref, dg_ref):
        i = pl.program_id(0)
        x = h_ref[...]
        r = lax.rsqrt(jnp.mean(x * x, axis=-1, keepdims=True) + EPS)
        xh = x * r
        gv = g_ref[...]
        out = xh * gv
        valid = _row_ids(i * tm, tm) >= first_x
        e = jnp.where(valid, out - t_ref[...], 0.0)
        dout = e * (1.0 / D)
        dxh = dout * gv
        dx = r * (dxh - xh * jnp.mean(dxh * xh, axis=-1, keepdims=True))
        dh_ref[...] = dx
        dhb_ref[...] = dx.astype(BF16)

        @pl.when(i == 0)
        def _():
            dg_ref[...] = jnp.zeros_like(dg_ref)
            loss_ref[...] = jnp.zeros_like(loss_ref)

        dg_ref[...] += jnp.sum(dout * xh, axis=0, keepdims=True)
        loss_ref[...] += 0.5 * jnp.sum(jnp.mean(e * e, axis=-1, keepdims=True))

    row = pl.BlockSpec((tm, D), lambda i: (i, 0))
    vec = pl.BlockSpec((1, D), lambda i: (0, 0))
    return pl.pallas_call(
        body, grid=(T // tm,),
        in_specs=[row, vec, row],
        out_specs=[pl.BlockSpec((8, 128), lambda i: (0, 0)), row, row, vec],
        out_shape=[jax.ShapeDtypeStruct((8, 128), F32), jax.ShapeDtypeStruct((T, D), F32),
                   jax.ShapeDtypeStruct((T, D), BF16), jax.ShapeDtypeStruct((1, D), F32)],
        compiler_params=_cp("arbitrary"), name="loss_head")(h, g, tgt)


def _mm_nn(name, a, w, w_spec, M, N, K, tm, tn, tk, mode, extra=None, a_spec=None):
    nk = K // tk
    if a_spec is None:
        a_spec = pl.BlockSpec((tm, tk), lambda i, j, k: (i, k))
    o_spec = pl.BlockSpec((tm, tn), lambda i, j, k: (i, j))

    def body(*refs):
        if mode == "resid":
            a_ref, w_ref, e_ref = refs[:3]
            outs = refs[3:]
        else:
            a_ref, w_ref = refs[:2]
            outs = refs[2:]
        acc_ref = outs[-1] if nk > 1 else None
        part = _nn(a_ref[...], w_ref[...])

        def finish(acc):
            if mode == "f32":
                outs[0][...] = acc
            elif mode == "relu2":
                r = jnp.maximum(acc, 0.0)
                outs[0][...] = r.astype(BF16)
                outs[1][...] = (r * r).astype(BF16)
            else:
                keep = _row_ids(pl.program_id(0) * tm, tm) >= PAD
                outs[0][...] = jnp.where(keep, e_ref[...] + acc, 0.0)

        if nk == 1:
            finish(part)
        else:
            k = pl.program_id(2)

            @pl.when(k == 0)
            def _():
                acc_ref[...] = part

            @pl.when(k > 0)
            def _():
                acc_ref[...] += part

            @pl.when(k == nk - 1)
            def _():
                finish(acc_ref[...])

    in_specs = [a_spec, w_spec(tk, tn)]
    args = [a, w]
    if mode == "resid":
        in_specs.append(o_spec)
        args.append(extra)
    if mode == "relu2":
        out_specs = [o_spec, o_spec]
        out_shape = [jax.ShapeDtypeStruct((M, N), BF16)] * 2
    else:
        out_specs = [o_spec]
        out_shape = [jax.ShapeDtypeStruct((M, N), F32)]
    scratch = [pltpu.VMEM((tm, tn), F32)] if nk > 1 else []
    res = pl.pallas_call(
        body, grid=(M // tm, N // tn, nk), in_specs=in_specs, out_specs=out_specs, out_shape=out_shape,
        scratch_shapes=scratch, compiler_params=_cp("parallel", "parallel", "arbitrary"), name=name)(*args)
    return res if mode == "relu2" else res[0]


def _mm_rms_nn(name, h, g, w, tm, tn, mode):
    M, K = h.shape
    N = w.shape[1]

    def body(h_ref, g_ref, w_ref, n_ref, *outs):
        @pl.when(pl.program_id(1) == 0)
        def _():
            x = h_ref[...]
            r = lax.rsqrt(jnp.mean(x * x, axis=-1, keepdims=True) + EPS)
            n_ref[...] = ((x * r) * g_ref[...]).astype(BF16)

        acc = _nn(n_ref[...], w_ref[...])
        if mode == "f32":
            outs[0][...] = acc
        else:
            r = jnp.maximum(acc, 0.0)
            outs[0][...] = r.astype(BF16)
            outs[1][...] = (r * r).astype(BF16)

    row = pl.BlockSpec((tm, K), lambda i, j: (i, 0))
    o_spec = pl.BlockSpec((tm, tn), lambda i, j: (i, j))
    n_out = 1 if mode == "f32" else 2
    return pl.pallas_call(
        body, grid=(M // tm, N // tn),
        in_specs=[row, pl.BlockSpec((1, K), lambda i, j: (0, 0)), pl.BlockSpec((K, tn), lambda i, j: (0, j))],
        out_specs=[row] + [o_spec] * n_out,
        out_shape=[jax.ShapeDtypeStruct((M, K), BF16)] + [jax.ShapeDtypeStruct((M, N), F32 if mode == "f32" else BF16)] * n_out,
        compiler_params=_cp("parallel", "arbitrary"), name=name)(h, g, w)


def _mm_nt(name, dy, w, dy_spec, w_spec, M, J, N, tm, tj, tn, mode, extra=None, parts=1):
    nk = N // tn
    o_spec = pl.BlockSpec((tm, tj), lambda i, j, k: (i, j))
    n_extra = {"f32": 0, "dact": 1, "rms": 3}[mode]
    if mode == "rms":
        assert nk == 1 and tj == J

    def body(*refs):
        dy_ref, w_ref = refs[:2]
        ex = refs[2:2 + n_extra]
        outs = refs[2 + n_extra:]
        acc_ref = outs[-1] if nk > 1 else None
        if parts == 1:
            part = _nt(dy_ref[...], w_ref[...])
        else:
            wq = tn // parts
            part = _nt(dy_ref[0], w_ref[:, 0:wq])
            for q in range(1, parts):
                part = part + _nt(dy_ref[q], w_ref[:, q * wq:(q + 1) * wq])

        def finish(acc):
            if mode == "f32":
                outs[0][...] = acc
            elif mode == "dact":
                outs[0][...] = (acc * (2.0 * ex[0][...].astype(F32))).astype(BF16)
            else:
                h_ref, g_ref, dres_ref = ex
                dh_ref, dhb_ref, dg_ref = outs[:3]
                i = pl.program_id(0)
                x = h_ref[...]
                r = lax.rsqrt(jnp.mean(x * x, axis=-1, keepdims=True) + EPS)
                xh = x * r
                dxh = acc * g_ref[...]
                dx = r * (dxh - xh * jnp.mean(dxh * xh, axis=-1, keepdims=True))
                keep = _row_ids(i * tm, tm) >= PAD
                dh = jnp.where(keep, dres_ref[...] + dx, 0.0)
                dh_ref[...] = dh
                dhb_ref[...] = dh.astype(BF16)

                @pl.when(i == 0)
                def _():
                    dg_ref[...] = jnp.zeros_like(dg_ref)

                dg_ref[...] += jnp.sum(acc * xh, axis=0, keepdims=True)

        if nk == 1:
            finish(part)
        else:
            k = pl.program_id(2)

            @pl.when(k == 0)
            def _():
                acc_ref[...] = part

            @pl.when(k > 0)
            def _():
                acc_ref[...] += part

            @pl.when(k == nk - 1)
            def _():
                finish(acc_ref[...])

    in_specs = [dy_spec(tm, tn), w_spec(tj, tn)]
    args = [dy, w]
    scratch = [pltpu.VMEM((tm, tj), F32)] if nk > 1 else []
    if mode == "rms":
        vec = pl.BlockSpec((1, J), lambda i, j, k: (0, 0))
        h, g, dres = extra
        res = pl.pallas_call(
            body, grid=(M // tm, 1, 1), in_specs=in_specs + [o_spec, vec, o_spec], out_specs=[o_spec, o_spec, vec],
            out_shape=[jax.ShapeDtypeStruct((M, J), F32), jax.ShapeDtypeStruct((M, J), BF16),
                       jax.ShapeDtypeStruct((1, J), F32)],
            compiler_params=_cp("arbitrary", "arbitrary", "arbitrary"), name=name)(*args, h, g, dres)
        return res
    if mode == "dact":
        in_specs.append(o_spec)
        args.append(extra)
    return pl.pallas_call(
        body, grid=(M // tm, J // tj, nk), in_specs=in_specs, out_specs=[o_spec],
        out_shape=[jax.ShapeDtypeStruct((M, J), BF16 if mode == "dact" else F32)],
        scratch_shapes=scratch, compiler_params=_cp("parallel", "parallel", "arbitrary"), name=name)(*args)[0]


def _mm_tn(name, x, dy, x_spec, dy_spec, o_spec, o_shape, T, K, N, tt, tk, tn):
    nt = T // tt

    def body(x_ref, dy_ref, o_ref, *acc):
        part = _tn(x_ref[...], dy_ref[...])
        if nt == 1:
            o_ref[...] = part.astype(BF16)
            return
        acc_ref = acc[0]
        t = pl.program_id(2)

        @pl.when(t == 0)
        def _():
            acc_ref[...] = part

        @pl.when(t > 0)
        def _():
            acc_ref[...] += part

        @pl.when(t == nt - 1)
        def _():
            o_ref[...] = acc_ref[...].astype(BF16)

    return pl.pallas_call(
        body, grid=(K // tk, N // tn, nt), in_specs=[x_spec(tt, tk), dy_spec(tt, tn)], out_specs=o_spec(tk, tn),
        out_shape=jax.ShapeDtypeStruct(o_shape, BF16), scratch_shapes=[pltpu.VMEM((tk, tn), F32)] if nt > 1 else [],
        compiler_params=_cp("parallel", "parallel", "arbitrary"), name=name)(x, dy)


def _pool_counts(base, n, w):
    pos = _row_ids(base, n) - PAD
    return jnp.clip(pos + 1, 1, w).astype(F32)


def _shifted_copies(buf, rows):
    buf[0, rows:rows + 8, :] = jnp.zeros((8, buf.shape[2]), F32)

    def blk(s, carry):
        b = pl.multiple_of(s * HALO, HALO)
        win = buf[0, pl.ds(b, HALO + 8), :]
        for r in range(1, 8):
            buf[r, pl.ds(b, HALO), :] = win[r:r + HALO]
        return carry

    lax.fori_loop(0, rows // HALO, blk, 0)


def _ev_fwd(u, cw, cb, lg, lb, pw, pb, ps):
    T = u.shape[0]
    C = 512
    tm = _tile(T, EV_ROWS, HALO)
    nsub = tm // HALO
    hb = tm // HALO

    def body(val_ref, gate_ref, pin_ref, valh_ref, gateh_ref, pinh_ref, cw_ref, cb_ref, lg_ref, lb_ref, pw_ref,
             pb_ref, ps_ref, yab_ref, yc_ref, a_ext, p_ext, d_buf):
        i = pl.program_id(0)
        nf = (i > 0).astype(F32)
        a_ext[0, 0:HALO, :] = valh_ref[...] * jax.nn.sigmoid(gateh_ref[...]) * nf
        a_ext[0, HALO:HALO + tm, :] = val_ref[...] * jax.nn.sigmoid(gate_ref[...])
        p_ext[0:HALO, :] = pinh_ref[...] * nf
        p_ext[HALO:, :] = pin_ref[...]
        _shifted_copies(a_ext, tm + HALO)

        def sub(s, carry):
            base = pl.multiple_of(s * HALO, HALO)
            acc = jnp.zeros((HALO, C), F32) + cb_ref[...]
            for j in range(CONV_WIDTH):
                off = 2 + j
                acc = acc + cw_ref[pl.ds(j, 1), :] * a_ext[off % 8, pl.ds(pl.multiple_of(base + off // 8 * 8, 8), HALO), :]
            yc_ref[pl.ds(base, HALO), :] = acc
            mu = jnp.mean(acc, axis=-1, keepdims=True)
            yc = acc - mu
            rstd = lax.rsqrt(jnp.mean(yc * yc, axis=-1, keepdims=True) + EPS)
            z = (yc * rstd) * lg_ref[...] + lb_ref[...]
            yab_ref[pl.ds(base, HALO), 0:C] = (z * jax.nn.sigmoid(z)).astype(BF16)
            pwin = p_ext[pl.ds(base, 2 * HALO), :]
            for gi, w in enumerate(POOL_WINDOWS):
                lo, hi = gi * HEAD, (gi + 1) * HEAD
                x = pwin[HALO:, lo:hi]
                tot = x
                for k in range(1, w):
                    tot = tot + pwin[HALO - k:2 * HALO - k, lo:hi]
                cnt = _pool_counts(i * tm + base, HALO, w)
                d_buf[pl.ds(base, HALO), lo:hi] = (tot / cnt - x).astype(BF16)
            return carry

        lax.fori_loop(0, nsub, sub, 0)
        for gi in range(len(POOL_WINDOWS)):
            lo, hi = gi * HEAD, (gi + 1) * HEAD
            y = _nn(d_buf[:, lo:hi], pw_ref[gi]) + pb_ref[:, lo:hi]
            yab_ref[:, C + lo:C + hi] = (y * ps_ref[:, lo:hi]).astype(BF16)

    def main(c):
        return pl.BlockSpec((tm, C), lambda i: (i, c))

    def halo(c):
        return pl.BlockSpec((HALO, C), lambda i: (jnp.maximum(i * hb - 1, 0), c))

    vec = pl.BlockSpec((1, C), lambda i: (0, 0))
    return pl.pallas_call(
        body, grid=(T // tm,),
        in_specs=[main(0), main(1), main(2), halo(0), halo(1), halo(2),
                  pl.BlockSpec((32, C), lambda i: (0, 0)), vec, vec, vec,
                  pl.BlockSpec((4, HEAD, HEAD), lambda i: (0, 0, 0)), vec, vec],
        out_specs=[pl.BlockSpec((tm, 2 * C), lambda i: (i, 0)), pl.BlockSpec((tm, C), lambda i: (i, 0))],
        out_shape=[jax.ShapeDtypeStruct((T, 2 * C), BF16), jax.ShapeDtypeStruct((T, C), F32)],
        scratch_shapes=[pltpu.VMEM((8, tm + HALO + 8, C), F32), pltpu.VMEM((tm + HALO, C), F32),
                        pltpu.VMEM((tm, C), BF16)],
        compiler_params=_cp("parallel"), name="ev_fwd")(u, u, u, u, u, u, cw, cb, lg, lb, pw, pb, ps)


def _ev_bwd(dyab, yc, u, cw, lg, lb, pw, pwt, pb, ps):
    T = u.shape[0]
    C = 512
    tm = _tile(T, EV_ROWS, HALO)
    nsub = tm // HALO
    hb = tm // HALO
    nblk = T // tm
    E = tm + HALO

    def body(dya_ref, dyb_ref, dyah_ref, dybh_ref, yc_ref, ych_ref, val_ref, gate_ref, pin_ref, valh_ref, gateh_ref,
             pinh_ref, cw_ref, lg_ref, lb_ref, pw_ref, pwt_ref, pb_ref, ps_ref,
             du_ref, dcw_ref, dvec_ref, dpw_ref,
             dy_ext, a_ext, p_ext, ddc_ext, dd_buf, d_buf, dpre_buf, dcw_acc, vec_acc):
        i = pl.program_id(0)
        nf = (i > 0).astype(F32)
        nl = (i < nblk - 1).astype(F32)

        @pl.when(i == 0)
        def _():
            dcw_ref[...] = jnp.zeros_like(dcw_ref)
            dvec_ref[...] = jnp.zeros_like(dvec_ref)
            dpw_ref[...] = jnp.zeros_like(dpw_ref)

        dcw_acc[...] = jnp.zeros_like(dcw_acc)
        vec_acc[...] = jnp.zeros_like(vec_acc)
        a_ext[0, 0:HALO, :] = valh_ref[...] * jax.nn.sigmoid(gateh_ref[...]) * nf
        a_ext[0, HALO:E, :] = val_ref[...] * jax.nn.sigmoid(gate_ref[...])
        p_ext[0:HALO, :] = pinh_ref[...] * nf
        p_ext[HALO:, :] = pin_ref[...]
        _shifted_copies(a_ext, E)

        def ln_bwd(y, dya, main):
            mu = jnp.mean(y, axis=-1, keepdims=True)
            ycen = y - mu
            rstd = lax.rsqrt(jnp.mean(ycen * ycen, axis=-1, keepdims=True) + EPS)
            yh = ycen * rstd
            z = yh * lg_ref[...] + lb_ref[...]
            sz = jax.nn.sigmoid(z)
            dz = dya * _dsilu(z, sz)
            dyh = dz * lg_ref[...]
            dy = rstd * (dyh - jnp.mean(dyh, axis=-1, keepdims=True) - yh * jnp.mean(dyh * yh, axis=-1, keepdims=True))
            if main:
                vec_acc[1] += jnp.sum((dz * yh).reshape(HALO // 8, 8, C), axis=0)
                vec_acc[2] += jnp.sum(dz.reshape(HALO // 8, 8, C), axis=0)
                vec_acc[0] += jnp.sum(dy.reshape(HALO // 8, 8, C), axis=0)
            return dy

        def pool_dd(dyb, base, main):
            dpre = dyb * ps_ref[...]
            for gi, w in enumerate(POOL_WINDOWS):
                lo, hi = gi * HEAD, (gi + 1) * HEAD
                dd = _nn(dpre[:, lo:hi].astype(BF16), pwt_ref[gi])
                cnt = _pool_counts(i * tm + base, HALO, w)
                ddc_ext[pl.ds(base, HALO), lo:hi] = dd / cnt
                if main:
                    dd_buf[pl.ds(base, HALO), lo:hi] = dd
            if main:
                dpre_buf[pl.ds(base, HALO), :] = dpre.astype(BF16)
                vec_acc[4] += jnp.sum(dpre.reshape(HALO // 8, 8, C), axis=0)

        def p1(s, carry):
            base = pl.multiple_of(s * HALO, HALO)
            dy_ext[0, pl.ds(base, HALO), :] = ln_bwd(yc_ref[pl.ds(base, HALO), :], dya_ref[pl.ds(base, HALO), :], True)
            pool_dd(dyb_ref[pl.ds(base, HALO), :], base, True)
            return carry

        lax.fori_loop(0, nsub, p1, 0)
        dy_ext[0, tm:E, :] = ln_bwd(ych_ref[...], dyah_ref[...], False) * nl
        _shifted_copies(dy_ext, E)
        dpre_h = dybh_ref[...] * ps_ref[...] * nl
        for gi, w in enumerate(POOL_WINDOWS):
            lo, hi = gi * HEAD, (gi + 1) * HEAD
            dd = _nn(dpre_h[:, lo:hi].astype(BF16), pwt_ref[gi])
            ddc_ext[tm:, lo:hi] = dd / _pool_counts(i * tm + tm, HALO, w)

        def p2(s, carry):
            base = pl.multiple_of(s * HALO, HALO)
            dy_m = dy_ext[0, pl.ds(base, HALO), :]
            da = jnp.zeros((HALO, C), F32)
            for j in range(CONV_WIDTH):
                sh = CONV_WIDTH - 1 - j
                off = 2 + j
                da = da + cw_ref[pl.ds(j, 1), :] * dy_ext[sh % 8, pl.ds(pl.multiple_of(base + sh // 8 * 8, 8), HALO), :]
                a_j = a_ext[off % 8, pl.ds(pl.multiple_of(base + off // 8 * 8, 8), HALO), :]
                dcw_acc[j] += jnp.sum((dy_m * a_j).reshape(HALO // 8, 8, C), axis=0)
            v = val_ref[pl.ds(base, HALO), :]
            g = gate_ref[pl.ds(base, HALO), :]
            sg = jax.nn.sigmoid(g)
            du_ref[pl.ds(base, HALO), 0:C] = (da * sg).astype(BF16)
            du_ref[pl.ds(base, HALO), C:2 * C] = (da * v * sg * (1.0 - sg)).astype(BF16)
            pwin = p_ext[pl.ds(base, 2 * HALO), :]
            cwin = ddc_ext[pl.ds(base, 2 * HALO), :]
            for gi, w in enumerate(POOL_WINDOWS):
                lo, hi = gi * HEAD, (gi + 1) * HEAD
                x = pwin[HALO:, lo:hi]
                tot = x
                back = cwin[0:HALO, lo:hi]
                for k in range(1, w):
                    tot = tot + pwin[HALO - k:2 * HALO - k, lo:hi]
                    back = back + cwin[k:k + HALO, lo:hi]
                cnt = _pool_counts(i * tm + base, HALO, w)
                d_buf[pl.ds(base, HALO), lo:hi] = (tot / cnt - x).astype(BF16)
                du_ref[pl.ds(base, HALO), 2 * C + lo:2 * C + hi] = (back - dd_buf[pl.ds(base, HALO), lo:hi]).astype(BF16)
            return carry

        lax.fori_loop(0, nsub, p2, 0)
        for gi in range(len(POOL_WINDOWS)):
            lo, hi = gi * HEAD, (gi + 1) * HEAD
            pre = _nn(d_buf[:, lo:hi], pw_ref[gi]) + pb_ref[:, lo:hi]
            vec_acc[3, :, lo:hi] += jnp.sum((dyb_ref[:, lo:hi] * pre).reshape(tm // 8, 8, HEAD), axis=0)
            dpw_ref[gi] += _tn(d_buf[:, lo:hi], dpre_buf[:, lo:hi])
        for j in range(CONV_WIDTH):
            dcw_ref[pl.ds(j, 1), :] += jnp.sum(dcw_acc[j], axis=0, keepdims=True)
        for r in range(5):
            dvec_ref[pl.ds(r, 1), :] += jnp.sum(vec_acc[r], axis=0, keepdims=True)

    def main(c, width=C):
        return pl.BlockSpec((tm, width), lambda i: (i, c))

    def prev(c):
        return pl.BlockSpec((HALO, C), lambda i: (jnp.maximum(i * hb - 1, 0), c))

    def nxt(c):
        return pl.BlockSpec((HALO, C), lambda i: (jnp.minimum((i + 1) * hb, T // HALO - 1), c))

    vec = pl.BlockSpec((1, C), lambda i: (0, 0))
    mat = pl.BlockSpec((4, HEAD, HEAD), lambda i: (0, 0, 0))
    return pl.pallas_call(
        body, grid=(nblk,),
        in_specs=[main(0), main(1), nxt(0), nxt(1), main(0), nxt(0), main(0), main(1), main(2), prev(0), prev(1),
                  prev(2), pl.BlockSpec((32, C), lambda i: (0, 0)), vec, vec, mat, mat, vec, vec],
        out_specs=[pl.BlockSpec((tm, 3 * C), lambda i: (i, 0)), pl.BlockSpec((32, C), lambda i: (0, 0)),
                   pl.BlockSpec((8, C), lambda i: (0, 0)), mat],
        out_shape=[jax.ShapeDtypeStruct((T, 3 * C), BF16), jax.ShapeDtypeStruct((32, C), F32),
                   jax.ShapeDtypeStruct((8, C), F32), jax.ShapeDtypeStruct((4, HEAD, HEAD), F32)],
        scratch_shapes=[pltpu.VMEM((8, E + 8, C), F32), pltpu.VMEM((8, E + 8, C), F32), pltpu.VMEM((E, C), F32),
                        pltpu.VMEM((E, C), F32), pltpu.VMEM((tm, C), F32), pltpu.VMEM((tm, C), BF16),
                        pltpu.VMEM((tm, C), BF16), pltpu.VMEM((32, 8, C), F32), pltpu.VMEM((8, 8, C), F32)],
        compiler_params=_cp("arbitrary"), name="ev_bwd")(
            dyab, dyab, dyab, dyab, yc, yc, u, u, u, u, u, u, cw, lg, lb, pw, pwt, pb, ps)


def _cumsum_rows(x, reverse=False):
    n = x.shape[0]
    rid = lax.broadcasted_iota(jnp.int32, (n, 1), 0)
    k = 1
    while k < n:
        if reverse:
            sh = jnp.where(rid < n - k, pltpu.roll(x, n - k, 0), 0.0)
        else:
            sh = jnp.where(rid >= k, pltpu.roll(x, k, 0), 0.0)
        x = x + sh
        k *= 2
    return x


def _hgrn_gates(qr, fr, lbv):
    sq = jax.nn.sigmoid(qr)
    sg = jax.nn.sigmoid(fr)
    fg = lbv + (1.0 - lbv) * sg
    return qr * sq, sq, sg, fg, 1.0 - fg, jnp.log(fg)


def _hgrn_fwd(u, lbv, gn):
    T = u.shape[0]
    H = 8
    RB = _tile(T, HGRN_ROWS, CHUNK)
    NC = RB // CHUNK
    NS = CHUNK // SUB

    HP = HGRN_HEADS_FWD
    W = HP * HEAD

    def body(q_ref, f_ref, i_ref, g_ref, lb_ref, gn_ref, y_ref, o_ref, s0_ref, st, qs, ks, bs, vs, os_):
        rb = pl.program_id(1)

        @pl.when(rb == 0)
        def _():
            st[...] = jnp.zeros_like(st)

        t8 = lax.broadcasted_iota(jnp.int32, (8, 1), 0)

        def head(hh, c, rows):
            sl = slice(hh * HEAD, (hh + 1) * HEAD)
            q, _, _, _, kk, lf = _hgrn_gates(q_ref[rows, sl], f_ref[rows, sl], lb_ref[:, sl])
            v = i_ref[rows, sl]
            b = _cumsum_rows(lf)
            qs[hh] = q
            ks[hh] = kk
            bs[hh] = b
            vs[hh] = v
            st0 = st[hh]
            s0_ref[hh, c] = st0
            os_[hh] = _nt((q * jnp.exp(b)).astype(BF16), st0.astype(BF16))
            for I in range(NS):
                lo = I * SUB
                qI = qs[hh, lo:lo + SUB, :]
                bI = bs[hh, lo:lo + SUB, :]
                oI = jnp.zeros((SUB, HEAD), F32)
                if I > 0:
                    bprev = bs[hh, pl.ds(lo - 1, 1), :]
                    qt = _r16(qI * jnp.exp(bI - bprev))
                    kt = _r16(ks[hh, 0:lo, :] * jnp.exp(bprev - bs[hh, 0:lo, :]))
                    A = _nt(qt, kt)
                    oI = oI + _nn(_r16(A), _r16(vs[hh, 0:lo, :]))
                od = [jnp.zeros((8, HEAD), F32) for _ in range(SUB // 8)]
                for s in range(SUB):
                    row = pl.ds(lo + s, 1)
                    brow, krow, vrow = bs[hh, row, :], ks[hh, row, :], vs[hh, row, :]
                    for ti in range(SUB // 8):
                        o8 = 8 * ti
                        if s > o8 + 7:
                            continue
                        d = bI[o8:o8 + 8] - brow
                        if s > o8:
                            d = jnp.where(t8 >= s - o8, d, NEG)
                        col = jnp.sum(qI[o8:o8 + 8] * jnp.exp(d) * krow, axis=1, keepdims=True)
                        od[ti] = od[ti] + col * vrow
                os_[hh, lo:lo + SUB, :] += oI + jnp.concatenate(od, axis=0)
            blast = bs[hh, pl.ds(CHUNK - 1, 1), :]
            kh = kk * jnp.exp(blast - b)
            st[hh] = st0 * jnp.exp(blast) + _tn(v.astype(BF16), kh.astype(BF16))
            o = os_[hh]
            o_ref[rows, sl] = o
            rr = lax.rsqrt(jnp.mean(o * o, axis=-1, keepdims=True) + EPS)
            gr = g_ref[rows, sl]
            y_ref[rows, sl] = (((o * rr) * gn_ref[...]) * (gr * jax.nn.sigmoid(gr))).astype(BF16)

        def chunk(c, carry):
            rows = pl.ds(pl.multiple_of(c * CHUNK, CHUNK), CHUNK)
            for hh in range(HP):
                head(hh, c, rows)
            return carry

        lax.fori_loop(0, NC, chunk, 0)

    def blk(q):
        return pl.BlockSpec((RB, W), lambda h, r: (r, q * (H // HP) + h))

    sc = lambda: pltpu.VMEM((HP, CHUNK, HEAD), F32)
    return pl.pallas_call(
        body, grid=(H // HP, T // RB),
        in_specs=[blk(0), blk(1), blk(2), blk(3), pl.BlockSpec((1, W), lambda h, r: (0, h)),
                  pl.BlockSpec((1, HEAD), lambda h, r: (0, 0))],
        out_specs=[pl.BlockSpec((RB, W), lambda h, r: (r, h)), pl.BlockSpec((RB, W), lambda h, r: (r, h)),
                   pl.BlockSpec((HP, NC, HEAD, HEAD), lambda h, r: (h, r, 0, 0))],
        out_shape=[jax.ShapeDtypeStruct((T, H * HEAD), BF16), jax.ShapeDtypeStruct((T, H * HEAD), F32),
                   jax.ShapeDtypeStruct((H, T // CHUNK, HEAD, HEAD), F32)],
        scratch_shapes=[pltpu.VMEM((HP, HEAD, HEAD), F32), sc(), sc(), sc(), sc(), sc()],
        compiler_params=_cp("parallel", "arbitrary"), name="hgrn_fwd")(u, u, u, u, lbv, gn)


def _hgrn_bwd(dy, o, s0, u, lbv, gn):
    T = u.shape[0]
    H = 8
    RB = _tile(T, HGRN_ROWS, CHUNK)
    NB = T // RB
    NC = RB // CHUNK
    NS = CHUNK // SUB

    def body(q_ref, f_ref, i_ref, g_ref, lb_ref, gn_ref, o_ref, dy_ref, s0_ref, du_ref, dlb_ref, dgn_ref,
             dst, qs, ks, bs, vs, dos, dqs, dks, dki, dvs, dbs):
        rb = pl.program_id(1)

        @pl.when(rb == 0)
        def _():
            dst[...] = jnp.zeros_like(dst)
            dlb_ref[...] = jnp.zeros_like(dlb_ref)
            dgn_ref[...] = jnp.zeros_like(dgn_ref)

        t8 = lax.broadcasted_iota(jnp.int32, (8, 1), 0)
        lane = lax.broadcasted_iota(jnp.int32, (8, HEAD), 1)
        gnv = gn_ref[...]

        def head(hh, c, rows):
            sl = slice(hh * HEAD, (hh + 1) * HEAD)
            lbv_ = lb_ref[:, sl]
            qr = q_ref[rows, sl]
            q, sq, sg, fg, kk, lf = _hgrn_gates(qr, f_ref[rows, sl], lbv_)
            v = i_ref[rows, sl]
            gr = g_ref[rows, sl]
            b = _cumsum_rows(lf)
            eb = jnp.exp(b)
            ov = o_ref[rows, sl]
            dyv = dy_ref[rows, sl]
            rr = lax.rsqrt(jnp.mean(ov * ov, axis=-1, keepdims=True) + EPS)
            oh = ov * rr
            gs = jax.nn.sigmoid(gr)
            dgr = dyv * (oh * gnv) * _dsilu(gr, gs)
            dnrm = dyv * (gr * gs)
            dgn_ref[hh] += jnp.sum(dnrm * oh, axis=0, keepdims=True)
            t1 = dnrm * gnv
            do = rr * (t1 - oh * jnp.mean(t1 * oh, axis=-1, keepdims=True))
            qs[hh] = q
            ks[hh] = kk
            bs[hh] = b
            vs[hh] = v
            dos[hh] = do
            st0 = s0_ref[hh, c]
            dS = dst[hh]
            do_b = do.astype(BF16)
            blast = bs[hh, pl.ds(CHUNK - 1, 1), :]
            elast = jnp.exp(blast - b)
            dq_inter = _nn(do_b, st0.astype(BF16)) * eb
            dqs[hh] = dq_inter
            dbs[hh] = q * dq_inter
            kh = kk * elast
            dvs[hh] = _nt(kh.astype(BF16), dS.astype(BF16))
            dk_inter = _nn(v.astype(BF16), dS.astype(BF16)) * elast
            dki[hh] = dk_inter
            dks[hh] = jnp.zeros((CHUNK, HEAD), F32)
            for I in range(NS):
                lo = I * SUB
                qI = qs[hh, lo:lo + SUB, :]
                bI = bs[hh, lo:lo + SUB, :]
                doI = dos[hh, lo:lo + SUB, :]
                dqI = jnp.zeros((SUB, HEAD), F32)
                dbI = jnp.zeros((SUB, HEAD), F32)
                if I > 0:
                    bprev = bs[hh, pl.ds(lo - 1, 1), :]
                    eq = jnp.exp(bI - bprev)
                    ek = jnp.exp(bprev - bs[hh, 0:lo, :])
                    qt = _r16(qI * eq)
                    kt = _r16(ks[hh, 0:lo, :] * ek)
                    A = _r16(_nt(qt, kt))
                    doI_b = _r16(doI)
                    dA = _r16(_nt(doI_b, _r16(vs[hh, 0:lo, :])))
                    dvs[hh, 0:lo, :] += _tn(A, doI_b)
                    dqt = _nn(dA, kt)
                    dkt = _tn(dA, qt)
                    dqI = dqI + dqt * eq
                    dbI = dbI + qt.astype(F32) * dqt
                    dks[hh, 0:lo, :] += dkt * ek
                    dbs[hh, 0:lo, :] -= kt.astype(F32) * dkt
                dq_t = [jnp.zeros((8, HEAD), F32) for _ in range(SUB // 8)]
                a_t = [jnp.zeros((8, HEAD), F32) for _ in range(SUB // 8)]
                for s in range(SUB):
                    row = pl.ds(lo + s, 1)
                    brow, krow, vrow = bs[hh, row, :], ks[hh, row, :], vs[hh, row, :]
                    dk_s = None
                    for ti in range(SUB // 8):
                        o8 = 8 * ti
                        if s > o8 + 7:
                            continue
                        d = bI[o8:o8 + 8] - brow
                        if s > o8:
                            d = jnp.where(t8 >= s - o8, d, NEG)
                        Es = jnp.exp(d)
                        qE = qI[o8:o8 + 8] * Es
                        col = jnp.sum(qE * krow, axis=1, keepdims=True)
                        a_t[ti] = jnp.where(lane == s, col, a_t[ti])
                        dcol = jnp.sum(doI[o8:o8 + 8] * vrow, axis=1, keepdims=True)
                        dq_t[ti] = dq_t[ti] + (dcol * Es) * krow
                        part = jnp.sum(dcol * qE, axis=0, keepdims=True)
                        dk_s = part if dk_s is None else dk_s + part
                    dks[hh, row, :] += dk_s
                    dbs[hh, row, :] -= krow * dk_s
                a_d = jnp.concatenate(a_t, axis=0)
                dq_d = jnp.concatenate(dq_t, axis=0)
                dvs[hh, lo:lo + SUB, :] += _tn(a_d, doI)[0:SUB]
                dqI = dqI + dq_d
                dbI = dbI + qI * dq_d
                dqs[hh, lo:lo + SUB, :] += dqI
                dbs[hh, lo:lo + SUB, :] += dbI
            kdk = kk * dki[hh]
            excl = _cumsum_rows(kdk) - kdk
            suff = _cumsum_rows(dbs[hh], reverse=True)
            gdec = jnp.sum(dS * st0, axis=0, keepdims=True) * jnp.exp(blast)
            dlf = suff + excl + gdec
            dk = dks[hh] + dki[hh]
            dfg = dlf / fg - dk
            dlb_ref[:, sl] += jnp.sum(dfg * (1.0 - sg), axis=0, keepdims=True)
            du_ref[0, rows, sl] = (dqs[hh] * _dsilu(qr, sq)).astype(BF16)
            du_ref[1, rows, sl] = (dfg * (1.0 - lbv_) * sg * (1.0 - sg)).astype(BF16)
            du_ref[2, rows, sl] = dvs[hh].astype(BF16)
            du_ref[3, rows, sl] = dgr.astype(BF16)
            dst[hh] = dS * jnp.exp(blast) + _tn(do_b, (q * eb).astype(BF16))

        def chunk(cc, carry):
            c = NC - 1 - cc
            rows = pl.ds(pl.multiple_of(c * CHUNK, CHUNK), CHUNK)
            for hh in range(HP):
                head(hh, c, rows)
            return carry

        lax.fori_loop(0, NC, chunk, 0)

    HP = HGRN_HEADS_BWD
    W = HP * HEAD

    def blk(qd):
        return pl.BlockSpec((RB, W), lambda h, r: (NB - 1 - r, qd * (H // HP) + h))

    hblk = pl.BlockSpec((RB, W), lambda h, r: (NB - 1 - r, h))
    sc = lambda: pltpu.VMEM((HP, CHUNK, HEAD), F32)
    return pl.pallas_call(
        body, grid=(H // HP, NB),
        in_specs=[blk(0), blk(1), blk(2), blk(3), pl.BlockSpec((1, W), lambda h, r: (0, h)),
                  pl.BlockSpec((1, HEAD), lambda h, r: (0, 0)), hblk, hblk,
                  pl.BlockSpec((HP, NC, HEAD, HEAD), lambda h, r: (h, NB - 1 - r, 0, 0))],
        out_specs=[pl.BlockSpec((4, RB, W), lambda h, r: (0, NB - 1 - r, h)),
                   pl.BlockSpec((1, W), lambda h, r: (0, h)), pl.BlockSpec((HP, 1, HEAD), lambda h, r: (h, 0, 0))],
        out_shape=[jax.ShapeDtypeStruct((4, T, H * HEAD), BF16), jax.ShapeDtypeStruct((1, H * HEAD), F32),
                   jax.ShapeDtypeStruct((H, 1, HEAD), F32)],
        scratch_shapes=[pltpu.VMEM((HP, HEAD, HEAD), F32)] + [sc() for _ in range(10)],
        compiler_params=_cp("parallel", "arbitrary"), name="hgrn_bwd")(u, u, u, u, lbv, gn, o, dy, s0)


def _softmax_rows(p_ref, L):
    rows = [p_ref[pl.ds(l, 1), :] for l in range(L)]
    m = rows[0]
    for r in rows[1:]:
        m = jnp.maximum(m, r)
    e = [jnp.exp(r - m) for r in rows]
    tot = e[0]
    for t in e[1:]:
        tot = tot + t
    return [t / tot for t in e]


def _lb_fwd(lbp):
    L, D = lbp.shape

    def body(p_ref, o_ref):
        sm = _softmax_rows(p_ref, L)
        acc = jnp.zeros((1, D), F32)
        o_ref[pl.ds(0, 1), :] = acc
        for l in range(1, L):
            acc = acc + sm[l]
            o_ref[pl.ds(l, 1), :] = acc

    return pl.pallas_call(body, out_shape=jax.ShapeDtypeStruct((L, D), F32), name="lb_fwd")(lbp)


def _lb_bwd(lbp, dlb):
    L, D = lbp.shape

    def body(p_ref, d_ref, o_ref):
        sm = _softmax_rows(p_ref, L)
        dsm = [jnp.zeros((1, D), F32)]
        for i in range(1, L):
            t = jnp.zeros((1, D), F32)
            for l in range(i, L):
                t = t + d_ref[pl.ds(l, 1), :]
            dsm.append(t)
        dot = jnp.zeros((1, D), F32)
        for i in range(L):
            dot = dot + dsm[i] * sm[i]
        for i in range(L):
            o_ref[pl.ds(i, 1), :] = sm[i] * (dsm[i] - dot)

    return pl.pallas_call(body, out_shape=jax.ShapeDtypeStruct((L, D), F32), name="lb_bwd")(lbp, dlb)


def _my_pos():
    return lax.axis_index("x"), lax.axis_index("y"), lax.axis_index("c")


def _peer(mask):
    x, y, c = _my_pos()
    mx, my, mc = (mask >> 2) & 1, (mask >> 1) & 1, mask & 1
    px = (1 - x) if mx else x
    py = (1 - y) if my else y
    pc = (1 - c) if mc else c
    return (px, py, pc), 4 * px + 2 * py + pc


def _all_gather(shards):
    n = len(shards)

    def body(*refs):
        ins, outs = refs[:n], refs[n:2 * n]
        send_sems, recv_sems, local_sems = refs[2 * n:]
        x, y, c = _my_pos()
        me = 4 * x + 2 * y + c
        local = [pltpu.make_async_copy(ins[a], outs[a].at[:, me], local_sems.at[a]) for a in range(n)]
        for cp in local:
            cp.start()
        sends = []
        for m in range(1, N_DEV):
            peer, _ = _peer(m)
            for a in range(n):
                cp = pltpu.make_async_remote_copy(
                    src_ref=ins[a], dst_ref=outs[a].at[:, me], send_sem=send_sems.at[a, m - 1],
                    recv_sem=recv_sems.at[a, m - 1], device_id=peer, device_id_type=MESH)
                cp.start()
                sends.append(cp)
        for m in range(1, N_DEV):
            peer, pid = _peer(m)
            for a in range(n):
                pltpu.make_async_remote_copy(
                    src_ref=ins[a], dst_ref=outs[a].at[:, pid], send_sem=send_sems.at[a, m - 1],
                    recv_sem=recv_sems.at[a, m - 1], device_id=peer, device_id_type=MESH).wait_recv()
        for cp in sends:
            cp.wait_send()
        for cp in local:
            cp.wait()

    out_shape = [jax.ShapeDtypeStruct((s.shape[0], N_DEV) + s.shape[1:], s.dtype) for s in shards]
    return pl.pallas_call(
        body, in_specs=[ANY] * n, out_specs=[ANY] * n, out_shape=out_shape,
        scratch_shapes=[pltpu.SemaphoreType.DMA((n, N_DEV - 1)), pltpu.SemaphoreType.DMA((n, N_DEV - 1)),
                        pltpu.SemaphoreType.DMA((n,))],
        name="all_gather_weights")(*shards)


def _exchange(grads, groups):
    n = len(grads)
    ng = 1 + max(g for g, _ in groups)
    layers = [1 + max(l for g, l in groups if g == gi) for gi in range(ng)]
    shapes = [None] * ng
    for a, (g, l) in enumerate(groups):
        shapes[g] = grads[a].shape[1:]

    def body(*refs):
        ins, outs = refs[:n], refs[n:n + ng]
        send_sems, recv_sems, local_sems = refs[n + ng:]
        x, y, c = _my_pos()
        me = 4 * x + 2 * y + c
        local = []
        for a, (g, l) in enumerate(groups):
            cp = pltpu.make_async_copy(ins[a].at[me], outs[g].at[me, l], local_sems.at[a])
            cp.start()
            local.append(cp)
        sends = []
        for m in range(1, N_DEV):
            peer, pid = _peer(m)
            for a, (g, l) in enumerate(groups):
                cp = pltpu.make_async_remote_copy(
                    src_ref=ins[a].at[pid], dst_ref=outs[g].at[me, l], send_sem=send_sems.at[a, m - 1],
                    recv_sem=recv_sems.at[a, m - 1], device_id=peer, device_id_type=MESH)
                cp.start()
                sends.append(cp)
        for m in range(1, N_DEV):
            peer, pid = _peer(m)
            for a, (g, l) in enumerate(groups):
                pltpu.make_async_remote_copy(
                    src_ref=ins[a].at[pid], dst_ref=outs[g].at[pid, l], send_sem=send_sems.at[a, m - 1],
                    recv_sem=recv_sems.at[a, m - 1], device_id=peer, device_id_type=MESH).wait_recv()
        for cp in sends:
            cp.wait_send()
        for cp in local:
            cp.wait()

    out_shape = [jax.ShapeDtypeStruct((N_DEV, layers[g]) + shapes[g], grads[[gg for gg, _ in groups].index(g)].dtype)
                 for g in range(ng)]
    return pl.pallas_call(
        body, in_specs=[ANY] * n, out_specs=[ANY] * ng, out_shape=out_shape,
        scratch_shapes=[pltpu.SemaphoreType.DMA((n, N_DEV - 1)), pltpu.SemaphoreType.DMA((n, N_DEV - 1)),
                        pltpu.SemaphoreType.DMA((n,))],
        name="exchange_grads")(*grads)


HBM_SPEC = pl.BlockSpec(memory_space=pltpu.HBM)
SEM_SPEC = pl.BlockSpec(memory_space=pltpu.SEMAPHORE)
EFFECT = pltpu.SideEffectType.DATAFLOW_SIDE_EFFECTING


def _hbm(a):
    return pltpu.with_memory_space_constraint(a, pltpu.HBM)


def _landing(block_shape, dtype, axis=0):
    if axis == 0:
        return lax.empty((N_DEV,) + tuple(block_shape), dtype)
    rows, n = block_shape
    return lax.empty((rows, N_DEV * n), dtype)


def _slot(ref, i):
    if len(ref.shape) == 2:
        n = ref.shape[1] // N_DEV
        return ref.at[:, pl.ds(i * n, n)]
    return ref.at[i]


def _push_start(name, srcs, lands, whole, groups):
    n = len(srcs)
    ng = 1 + max(groups)
    cnt = [groups.count(g) for g in range(ng)]
    idx = [groups[:a].count(groups[a]) for a in range(n)]

    def body(*refs):
        src_refs, land_refs = refs[:n], refs[n:2 * n]
        sems = refs[2 * n:2 * n + 3 * ng]
        token = refs[-1]
        x, y, c = _my_pos()
        me = 4 * x + 2 * y + c
        for a in range(n):
            g = groups[a]
            for m in range(1, N_DEV):
                peer, pid = _peer(m)
                pltpu.make_async_remote_copy(
                    src_ref=src_refs[a] if whole else src_refs[a].at[pid], dst_ref=_slot(land_refs[a], me),
                    send_sem=sems[3 * g].at[idx[a] * (N_DEV - 1) + m - 1],
                    recv_sem=sems[3 * g + 1].at[idx[a] * (N_DEV - 1) + m - 1],
                    device_id=peer, device_id_type=MESH).start()
            pltpu.make_async_copy(src_refs[a] if whole else src_refs[a].at[me], _slot(land_refs[a], me),
                                  sems[3 * g + 2].at[idx[a]]).start()
        token[...] = jnp.zeros_like(token)

    sem_shapes = []
    for g in range(ng):
        sem_shapes += [pltpu.SemaphoreType.DMA((cnt[g] * (N_DEV - 1),))] * 2 + [pltpu.SemaphoreType.DMA((cnt[g],))]
    thru = [pltpu.HBM(s.shape, s.dtype) for s in list(srcs) + list(lands)]
    res = pl.pallas_call(
        body, name=name,
        out_shape=tuple(sem_shapes + thru + [jax.ShapeDtypeStruct((8, 128), F32)]),
        in_specs=tuple([HBM_SPEC] * (2 * n)),
        out_specs=tuple([SEM_SPEC] * (3 * ng) + [HBM_SPEC] * (2 * n) + [pl.BlockSpec(memory_space=pltpu.VMEM)]),
        input_output_aliases={i: 3 * ng + i for i in range(2 * n)},
        compiler_params=pltpu.CompilerParams(has_side_effects=EFFECT),
    )(*[_hbm(s) for s in srcs], *[_hbm(z) for z in lands])
    sems = [(res[3 * g], res[3 * g + 1], res[3 * g + 2]) for g in range(ng)]
    srcs_thru = list(res[3 * ng:3 * ng + n])
    lands_thru = list(res[3 * ng + n:3 * ng + 2 * n])
    return sems, srcs_thru, lands_thru, res[-1]


def _push_wait(name, srcs_thru, lands_thru, sems, after, whole):
    n = len(srcs_thru)

    def body(*refs):
        src_refs, land_refs = refs[:n], refs[n:2 * n]
        send_sems, recv_sems, own_sems = refs[2 * n], refs[2 * n + 1], refs[2 * n + 2]
        x, y, c = _my_pos()
        me = 4 * x + 2 * y + c
        for a in range(n):
            pltpu.make_async_copy(src_refs[a] if whole else src_refs[a].at[me], _slot(land_refs[a], me),
                                  own_sems.at[a]).wait()
            for m in range(1, N_DEV):
                peer, pid = _peer(m)
                cp = pltpu.make_async_remote_copy(
                    src_ref=src_refs[a] if whole else src_refs[a].at[pid], dst_ref=_slot(land_refs[a], pid),
                    send_sem=send_sems.at[a * (N_DEV - 1) + m - 1], recv_sem=recv_sems.at[a * (N_DEV - 1) + m - 1],
                    device_id=peer, device_id_type=MESH)
                cp.wait_send()
                cp.wait_recv()

    thru = [pltpu.HBM(s.shape, s.dtype) for s in list(srcs_thru) + list(lands_thru)]
    res = pl.pallas_call(
        body, name=name, out_shape=tuple(thru),
        in_specs=tuple([HBM_SPEC] * (2 * n) + [SEM_SPEC, SEM_SPEC, SEM_SPEC, ANY]),
        out_specs=tuple([HBM_SPEC] * (2 * n)),
        input_output_aliases={i: i for i in range(2 * n)},
        compiler_params=pltpu.CompilerParams(has_side_effects=EFFECT),
    )(*srcs_thru, *lands_thru, sems[0], sems[1], sems[2], after)
    return list(res[n:])


def _adamw(recv, w, m, v, layer=0, prev=None):
    L, R, C = w.shape
    tr = _tile(R, max(8, (1 << 18) // C), 8) if R % 8 == 0 else R
    bc1 = 1.0 - ADAM_B1 ** ADAM_STEP
    bc2 = 1.0 - ADAM_B2 ** ADAM_STEP
    if prev is None:
        prev = [lax.empty((L, R, C), F32) for _ in range(4)]

    def body(r_ref, w_ref, m_ref, v_ref, p0, p1, p2, p3, g_ref, d_ref, nm_ref, nv_ref):
        g = r_ref[0].astype(F32)
        for s in range(1, N_DEV):
            g = g + r_ref[s].astype(F32)
        nm = ADAM_B1 * m_ref[...] + (1.0 - ADAM_B1) * g
        nv = ADAM_B2 * v_ref[...] + (1.0 - ADAM_B2) * (g * g)
        mh = nm / bc1
        vh = nv / bc2
        g_ref[...] = g
        d_ref[...] = -ADAM_LR * (mh / (jnp.sqrt(vh) + ADAM_EPS) + ADAM_WD * w_ref[...])
        nm_ref[...] = nm
        nv_ref[...] = nv

    row = pl.BlockSpec((None, tr, C), lambda i: (layer, i, 0))
    return pl.pallas_call(
        body, grid=(R // tr,),
        in_specs=[pl.BlockSpec((N_DEV, tr, C), lambda i: (0, i, 0)), row, row, row] + [ANY] * 4,
        out_specs=[row] * 4, out_shape=[jax.ShapeDtypeStruct((L, R, C), F32)] * 4,
        input_output_aliases={4: 0, 5: 1, 6: 2, 7: 3},
        compiler_params=_cp("parallel"), name="adamw")(recv, w, m, v, *prev)


def _full_w_spec(tk, tn):
    return pl.BlockSpec((tk, tn), lambda i, j, k: (k, j))


def _colblk_w_spec(n):
    def spec(tk, tn):
        per = n // tn
        return pl.BlockSpec((None, tk, tn), lambda i, j, k: (j // per, k, j % per))
    return spec


def kernel(x, meta_tokens, mix_norm_g, mlp_norm_g, final_norm_g, ev_w_in, ev_conv_w, ev_conv_b, ev_ln_g, ev_ln_b, ev_pool_w, ev_pool_b, ev_pool_scale, ev_w_out, od_w_in, od_gnorm_g, od_w_out, lb_param, mlp_w1, mlp_w2, loss_target, m_meta_tokens, m_mix_norm_g, m_mlp_norm_g, m_final_norm_g, m_ev_w_in, m_ev_conv_w, m_ev_conv_b, m_ev_ln_g, m_ev_ln_b, m_ev_pool_w, m_ev_pool_b, m_ev_pool_scale, m_ev_w_out, m_od_w_in, m_od_gnorm_g, m_od_w_out, m_lb_param, m_mlp_w1, m_mlp_w2, v_meta_tokens, v_mix_norm_g, v_mlp_norm_g, v_final_norm_g, v_ev_w_in, v_ev_conv_w, v_ev_conv_b, v_ev_ln_g, v_ev_ln_b, v_ev_pool_w, v_ev_pool_b, v_ev_pool_scale, v_ev_w_out, v_od_w_in, v_od_gnorm_g, v_od_w_out, v_lb_param, v_mlp_w1, v_mlp_w2):
    S, D = x.shape[1], x.shape[2]
    T = PAD + N_META + S
    DEPTH = mix_norm_g.shape[0]
    DFF = mlp_w1.shape[2] * N_DEV
    dev = 4 * lax.axis_index("x") + 2 * lax.axis_index("y") + lax.axis_index("c")

    g_meta, g_cw = _all_gather([meta_tokens[None], ev_conv_w])
    n_ev = ev_w_in.shape[0]
    n_od = od_w_in.shape[0]
    meta_full = jnp.transpose(g_meta[0], (1, 0, 2)).reshape(N_META, D)
    cw_full = jnp.transpose(g_cw, (0, 2, 1, 3)).reshape(n_ev, CONV_WIDTH, -1)
    cw_pad = jnp.pad(cw_full, ((0, 0), (0, 32 - CONV_WIDTH), (0, 0)))
    n_in_od = od_w_in.shape[2]
    n_w1 = mlp_w1.shape[2]

    ag_src, ag_grp, ag_axis, ag_at = [], [], [], {}
    for layer in range(DEPTH):
        j = layer // 2
        mixer = [("in", ev_w_in[j]), ("out", ev_w_out[j])] if layer % 2 == 0 else [("in", od_w_in[j]), ("out", od_w_out[j])]
        for pos, (key, arr) in enumerate(mixer + [("w1", mlp_w1[layer]), ("w2", mlp_w2[layer])]):
            ag_at[layer, key] = len(ag_src)
            ag_src.append(arr.astype(BF16))
            ag_grp.append(len(ag_grp))
            ag_axis.append(1 if key in ("in", "w1") and arr.shape[1] % 128 == 0 else 0)
    ag_src, g_meta, g_cw = lax.optimization_barrier((ag_src, g_meta, g_cw))
    ag_sems, ag_s, ag_l, ag_tok = _push_start(
        "ag_start", ag_src, [_landing(s_.shape, s_.dtype, ax) for s_, ax in zip(ag_src, ag_axis)], True, ag_grp)

    def ag_wait(layer, key, after):
        a = ag_at[layer, key]
        return _push_wait(f"ag_wait_{a}", [ag_s[a]], [ag_l[a]], ag_sems[a], after, True)[0]

    h = jnp.concatenate([jnp.zeros((PAD, D), F32), meta_full, x[0]], axis=0) + ag_tok[0, 0]
    tgt = jnp.pad(loss_target[0], ((PAD + N_META, 0), (0, 0)))
    lb_all = _lb_fwd(lb_param)

    tm_big = _tile(T, MM_ROWS_BIG, 16)
    tm_mid = _tile(T, MM_ROWS_MID, 16)
    tm_k4 = _tile(T, MM_ROWS_K4, 16)

    saved = []
    for layer in range(DEPTH):
        j = layer // 2
        sv = {"h0": h}
        g_in = ag_wait(layer, "in", h)
        w_in = g_in if g_in.ndim == 2 else jnp.transpose(g_in, (1, 0, 2)).reshape(D, -1)
        if layer % 2 == 0:
            sv["n"], u = _mm_rms_nn("ev_in", h, mix_norm_g[layer][None], w_in, tm_mid, 512, "f32")
            yab, yc = _ev_fwd(u, cw_pad[j], ev_conv_b[j][None], ev_ln_g[j][None], ev_ln_b[j][None],
                              ev_pool_w[j].astype(BF16), ev_pool_b[j].reshape(1, -1), ev_pool_scale[j][None])
            sv.update(u=u, y=yab, yc=yc)
            w_out = ag_wait(layer, "out", yab).reshape(-1, D)
            h = _mm_nn("ev_out", yab, w_out, _full_w_spec, T, D, D, tm_mid, D, D, "resid", extra=h)
        else:
            sv["n"], u = _mm_rms_nn("od_in", h, mix_norm_g[layer][None], w_in, tm_mid, 1024, "f32")
            y, o, s0 = _hgrn_fwd(u, lb_all[layer][None], od_gnorm_g[j][None])
            sv.update(u=u, y=y, o=o, s0=s0)
            w_out = ag_wait(layer, "out", y).reshape(-1, D)
            h = _mm_nn("od_out", y, w_out, _full_w_spec, T, D, D, tm_mid, D, D, "resid", extra=h)
        sv["h1"] = h
        w_w1 = ag_wait(layer, "w1", h)
        n2, r, act = _mm_rms_nn("mlp_w1", h, mlp_norm_g[layer][None], w_w1, tm_mid, 1024, "relu2")
        w_w2 = ag_wait(layer, "w2", act).reshape(DFF, D)
        sv.update(w_in=w_in, w_out=w_out, w_w1=w_w1, w_w2=w_w2)
        sv.update(n2=n2, r=r, act=act)
        h = _mm_nn("mlp_w2", act, w_w2, _full_w_spec, T, D, DFF, tm_k4, D, DFF, "resid", extra=h)
        saved.append(sv)

    loss_blk, dh, dhb, dg_final = _loss_head(h, final_norm_g[None], tgt)
    loss = lax.psum(loss_blk[0, 0], AXES)

    tt = T
    g_mix, g_mlp = [None] * DEPTH, [None] * DEPTH
    small ={"cw": [None] * n_ev, "vec": [None] * n_ev, "pw": [None] * n_ev, "gn": [None] * n_od}
    dlb_rows = [jnp.zeros((1, D), F32) for _ in range(DEPTH)]

    def xs2(tt_, tk):
        return pl.BlockSpec((tt_, tk), lambda a, b, t: (t, a))

    def ys2(tt_, tn):
        return pl.BlockSpec((tt_, tn), lambda a, b, t: (t, b))

    def os2(tk, tn):
        return pl.BlockSpec((tk, tn), lambda a, b, t: (a, b))

    def os3(tk, tn):
        return pl.BlockSpec((None, tk, tn), lambda a, b, t: (b, a, 0))

    def dy2(tm, tn):
        return pl.BlockSpec((tm, tn), lambda i, jj, k: (i, k))

    def w_rows(tj, tn):
        return pl.BlockSpec((tj, tn), lambda i, jj, k: (jj, k))

    def w_colblk(tj, tn):
        return pl.BlockSpec((None, tj, tn), lambda i, jj, k: (k, jj, 0))

    rs_pending = []

    def rs_start(tag, mats):
        blocks = [m_ if m_.ndim == 3 else m_.reshape(N_DEV, m_.shape[0] // N_DEV, m_.shape[1]) for m_ in mats]
        lands = [_landing(b_.shape[1:], b_.dtype) for b_ in blocks]
        sems, s_thru, l_thru, tok = _push_start(f"rs_start_{tag}", blocks, lands, False, [0] * len(blocks))
        rs_pending.append((tag, s_thru, l_thru, sems[0]))
        return tok[0, 0]

    for layer in reversed(range(DEPTH)):
        j = layer // 2
        sv = saved[layer]
        da1 = _mm_nt("mlp_w2_t", dhb, sv["w_w2"], dy2, w_rows, T, DFF, D, tm_mid, 1024, D, "dact", extra=sv["r"])
        dw2 = _mm_tn("mlp_dw2", sv["act"], dhb, xs2, ys2, os2, (DFF, D), T, DFF, D, tt, 512, D)
        dw1 = _mm_tn("mlp_dw1", sv["n2"], da1, xs2, ys2, os3, (N_DEV, D, n_w1), T, D, DFF, tt, D, n_w1)
        tok = rs_start(f"mlp{layer}", [dw1, dw2])
        dh, dhb, g_mlp[layer] = _mm_nt("mlp_w1_t", da1, sv["w_w1"], dy2, w_rows, T, D, DFF, tm_k4, D, DFF, "rms",
                                       extra=(sv["h1"], mlp_norm_g[layer][None] + tok, dh))
        if layer % 2 == 0:
            dyab = _mm_nt("ev_out_t", dhb, sv["w_out"], dy2, w_rows, T, D, D, tm_mid, D, D, "f32")
            dwout = _mm_tn("ev_dwout", sv["y"], dhb, xs2, ys2, os2, (D, D), T, D, D, tt, 512, D)
            du, small["cw"][j], small["vec"][j], small["pw"][j] = _ev_bwd(
                dyab, sv["yc"], sv["u"], cw_pad[j], ev_ln_g[j][None], ev_ln_b[j][None], ev_pool_w[j].astype(BF16),
                jnp.transpose(ev_pool_w[j], (0, 2, 1)).astype(BF16), ev_pool_b[j].reshape(1, -1),
                ev_pool_scale[j][None])
            nin = du.shape[1]
            dwin = _mm_tn("ev_dwin", sv["n"], du, xs2, ys2, os2, (D, nin), T, D, nin, tt, D, 512)
            dwin = jnp.transpose(dwin.reshape(D, N_DEV, nin // N_DEV), (1, 0, 2))
            tok = rs_start(f"mix{layer}", [dwin, dwout])
            dh, dhb, g_mix[layer] = _mm_nt("ev_in_t", du, sv["w_in"], dy2, w_rows, T, D, nin, tm_k4, D, nin, "rms",
                                           extra=(sv["h0"], mix_norm_g[layer][None] + tok, dh))
        else:
            dy = _mm_nt("od_out_t", dhb, sv["w_out"], dy2, w_rows, T, D, D, tm_mid, D, D, "f32")
            dwout = _mm_tn("od_dwout", sv["y"], dhb, xs2, ys2, os2, (D, D), T, D, D, tt, 512, D)
            du3, dlb_rows[layer], small["gn"][j] = _hgrn_bwd(dy, sv["o"], sv["s0"], sv["u"], lb_all[layer][None],
                                                              od_gnorm_g[j][None])
            per = D // n_in_od

            def du_t(tt_, tn):
                return pl.BlockSpec((None, tt_, tn), lambda a, b, t: (b // per, t, b % per))

            dwin = _mm_tn("od_dwin", sv["n"], du3, xs2, du_t, os3, (N_DEV, D, n_in_od), T, D, 4 * D, tt, D, n_in_od)
            tok = rs_start(f"mix{layer}", [dwin, dwout])
            dh, dhb, g_mix[layer] = _mm_nt(
                "od_in_t", du3, sv["w_in"], lambda tm, tn: pl.BlockSpec((4, tm, tn // 4), lambda i, jj, k: (0, i, 0)),
                w_rows, T, D, 4 * D, tm_k4, D, 4 * D, "rms", extra=(sv["h0"], mix_norm_g[layer][None] + tok, dh),
                parts=4)

    dmeta = dh[PAD:PAD + N_META]
    grad_x = dh[PAD + N_META:][None]
    dlb_param = _lb_bwd(lb_param, jnp.concatenate(dlb_rows, axis=0))

    pieces = [
        ("meta", dmeta), ("mix", jnp.concatenate(g_mix, 0)), ("mlp", jnp.concatenate(g_mlp, 0)), ("final", dg_final),
        ("cw", jnp.stack([c[:CONV_WIDTH] for c in small["cw"]])), ("cb", jnp.stack([v_[0] for v_ in small["vec"]])),
        ("lng", jnp.stack([v_[1] for v_ in small["vec"]])), ("lnb", jnp.stack([v_[2] for v_ in small["vec"]])),
        ("pw", jnp.stack(small["pw"])), ("pb", jnp.stack([v_[4] for v_ in small["vec"]])),
        ("ps", jnp.stack([v_[3] for v_ in small["vec"]])), ("gn", jnp.stack([jnp.sum(g_, axis=0)[0] for g_ in small["gn"]])),
        ("lb", dlb_param),
    ]
    flat = jnp.concatenate([p.reshape(-1) for _, p in pieces])
    n_small = flat.shape[0]
    rows_small = -(-n_small // 1024 // 8) * 8
    flat = jnp.pad(flat, (0, rows_small * 1024 - n_small)).reshape(rows_small, 1024)

    recv_small = _exchange([jnp.broadcast_to(flat[None], (N_DEV,) + flat.shape)], [(0, 0)])[0]
    recv = {}
    for tag, s_thru, l_thru, sems in rs_pending:
        got = _push_wait(f"rs_wait_{tag}", s_thru, l_thru, sems, recv_small, False)
        layer = int(tag[3:])
        if tag.startswith("mlp"):
            recv["w1", layer], recv["w2", layer] = got
        else:
            key = "ev" if layer % 2 == 0 else "od"
            recv[key + "_in", layer // 2], recv[key + "_out", layer // 2] = got

    outs = {}
    big = {"ev_in": ("ev_w_in", ev_w_in, m_ev_w_in, v_ev_w_in), "ev_out": ("ev_w_out", ev_w_out, m_ev_w_out, v_ev_w_out),
           "od_in": ("od_w_in", od_w_in, m_od_w_in, v_od_w_in), "od_out": ("od_w_out", od_w_out, m_od_w_out, v_od_w_out),
           "w1": ("mlp_w1", mlp_w1, m_mlp_w1, v_mlp_w1), "w2": ("mlp_w2", mlp_w2, m_mlp_w2, v_mlp_w2)}
    for key, (name, w, m, v) in big.items():
        res = None
        for l in range(w.shape[0]):
            res = _adamw(recv[key, l], w, m, v, layer=l, prev=res)
        outs[name] = res

    small_params = {
        "meta": ("meta_tokens", None), "mix": ("mix_norm_g", mix_norm_g, m_mix_norm_g, v_mix_norm_g),
        "mlp": ("mlp_norm_g", mlp_norm_g, m_mlp_norm_g, v_mlp_norm_g),
        "final": ("final_norm_g", final_norm_g, m_final_norm_g, v_final_norm_g),
        "cw": ("ev_conv_w", None), "cb": ("ev_conv_b", ev_conv_b, m_ev_conv_b, v_ev_conv_b),
        "lng": ("ev_ln_g", ev_ln_g, m_ev_ln_g, v_ev_ln_g), "lnb": ("ev_ln_b", ev_ln_b, m_ev_ln_b, v_ev_ln_b),
        "pw": ("ev_pool_w", ev_pool_w, m_ev_pool_w, v_ev_pool_w), "pb": ("ev_pool_b", ev_pool_b, m_ev_pool_b, v_ev_pool_b),
        "ps": ("ev_pool_scale", ev_pool_scale, m_ev_pool_scale, v_ev_pool_scale),
        "gn": ("od_gnorm_g", od_gnorm_g, m_od_gnorm_g, v_od_gnorm_g), "lb": ("lb_param", lb_param, m_lb_param, v_lb_param),
    }
    csh = ev_conv_w.shape[2]
    msh = meta_tokens.shape[1]

    def packed(which):
        parts = []
        for key, g_ in pieces:
            ent = small_params[key]
            if key == "meta":
                src = (meta_tokens, m_meta_tokens, v_meta_tokens)[which]
                full = lax.dynamic_update_slice(jnp.zeros((N_META, D), F32), src, (0, dev * msh))
            elif key == "cw":
                src = (ev_conv_w, m_ev_conv_w, v_ev_conv_w)[which]
                full = lax.dynamic_update_slice(jnp.zeros(g_.shape, F32), src, (0, 0, dev * csh))
            else:
                full = ent[1 + which]
            parts.append(full.reshape(-1))
        f = jnp.concatenate(parts)
        return jnp.pad(f, (0, rows_small * 1024 - n_small)).reshape(rows_small, 1024)

    sres = [r_[0] for r_ in _adamw(recv_small[:, 0], packed(0)[None], packed(1)[None], packed(2)[None])]
    off = 0
    for key, g_ in pieces:
        size = g_.size
        vals = [r_.reshape(-1)[off:off + size].reshape(g_.shape) for r_ in sres]
        off += size
        name = small_params[key][0]
        if key == "meta":
            vals = [lax.dynamic_slice(v_, (0, dev * msh), (N_META, msh)) for v_ in vals]
        elif key == "cw":
            vals = [lax.dynamic_slice(v_, (0, 0, dev * csh), v_.shape[:2] + (csh,)) for v_ in vals]
        else:
            vals = [v_.reshape(small_params[key][1].shape) for v_ in vals]
        outs[name] = vals

    names = ["meta_tokens", "mix_norm_g", "mlp_norm_g", "final_norm_g", "ev_w_in", "ev_conv_w", "ev_conv_b", "ev_ln_g",
             "ev_ln_b", "ev_pool_w", "ev_pool_b", "ev_pool_scale", "ev_w_out", "od_w_in", "od_gnorm_g", "od_w_out",
             "lb_param", "mlp_w1", "mlp_w2"]
    result = [loss, grad_x]
    for k in range(4):
        result += [outs[nm][k] for nm in names]
    return tuple(result)
```

```python
import functools

import jax
import jax.numpy as jnp
from jax import lax
from jax.experimental import pallas as pl
from jax.experimental.pallas import tpu as pltpu

F32 = jnp.float32
BF16 = jnp.bfloat16

N_DEV = 8
N_META = 16
CHUNK = 64
PAD = CHUNK - N_META
SUB = 16
HEAD = 128
CONV_WIDTH = 31
HALO = 32
POOL_WINDOWS = (2, 4, 8, 16)
EPS = 1e-6
NEG = -1e30
ADAM_LR, ADAM_B1, ADAM_B2, ADAM_EPS, ADAM_WD, ADAM_STEP = 0.001, 0.9, 0.999, 1e-08, 0.01, 10
VMEM_LIMIT = 56 * 1024 * 1024
EV_ROWS = 416
HGRN_ROWS = 832
MM_ROWS_BIG = 2080
MM_ROWS_MID = 1040
MM_ROWS_K4 = 416
SMALL_F32_ROWS = 8
HGRN_HEADS_FWD = 4
HGRN_HEADS_BWD = 2
MESH = pl.DeviceIdType.MESH
AXES = ("x", "y", "c")
ANY = pl.BlockSpec(memory_space=pl.ANY)


def _cp(*sem):
    return pltpu.CompilerParams(dimension_semantics=sem, vmem_limit_bytes=VMEM_LIMIT)


def _tile(n, cap, mult):
    best = None
    for d in range(mult, min(n, cap) + 1, mult):
        if n % d == 0:
            best = d
    assert best is not None, (n, cap, mult)
    return best


def _nt(a, b):
    return lax.dot_general(a, b, (((1,), (1,)), ((), ())), preferred_element_type=F32)


def _tn(a, b):
    return lax.dot_general(a, b, (((0,), (0,)), ((), ())), preferred_element_type=F32)


def _nn(a, b):
    return jnp.dot(a, b, preferred_element_type=F32)


def _r16(x):
    return x.astype(BF16).astype(F32)


def _row_ids(base, n):
    return base + lax.broadcasted_iota(jnp.int32, (n, 1), 0)


def _dsilu(x, s):
    return s * (1.0 + x * (1.0 - s))


def _rms_fwd(h, g):
    T, D = h.shape
    tm = _tile(T, MM_ROWS_MID, 16)

    def body(h_ref, g_ref, n_ref):
        x = h_ref[...]
        r = lax.rsqrt(jnp.mean(x * x, axis=-1, keepdims=True) + EPS)
        n_ref[...] = ((x * r) * g_ref[...]).astype(BF16)

    return pl.pallas_call(
        body, grid=(T // tm,),
        in_specs=[pl.BlockSpec((tm, D), lambda i: (i, 0)), pl.BlockSpec((1, D), lambda i: (0, 0))],
        out_specs=pl.BlockSpec((tm, D), lambda i: (i, 0)),
        out_shape=jax.ShapeDtypeStruct((T, D), BF16),
        compiler_params=_cp("parallel"), name="rms_fwd")(h, g)


def _rms_bwd(dn, h, g, dres):
    T, D = h.shape
    tm = _tile(T, MM_ROWS_MID, 16)

    def body(dn_ref, h_ref, g_ref, dres_ref, dh_ref, dhb_ref, dg_ref):
        i = pl.program_id(0)
        x = h_ref[...]
        dn_v = dn_ref[...]
        r = lax.rsqrt(jnp.mean(x * x, axis=-1, keepdims=True) + EPS)
        xh = x * r
        dxh = dn_v * g_ref[...]
        dx = r * (dxh - xh * jnp.mean(dxh * xh, axis=-1, keepdims=True))
        keep = _row_ids(i * tm, tm) >= PAD
        dh = jnp.where(keep, dres_ref[...] + dx, 0.0)
        dh_ref[...] = dh
        dhb_ref[...] = dh.astype(BF16)

        @pl.when(i == 0)
        def _():
            dg_ref[...] = jnp.zeros_like(dg_ref)

        dg_ref[...] += jnp.sum(dn_v * xh, axis=0, keepdims=True)

    row = pl.BlockSpec((tm, D), lambda i: (i, 0))
    vec = pl.BlockSpec((1, D), lambda i: (0, 0))
    return pl.pallas_call(
        body, grid=(T // tm,),
        in_specs=[row, row, vec, row], out_specs=[row, row, vec],
        out_shape=[jax.ShapeDtypeStruct((T, D), F32), jax.ShapeDtypeStruct((T, D), BF16),
                   jax.ShapeDtypeStruct((1, D), F32)],
        compiler_params=_cp("arbitrary"), name="rms_bwd")(dn, h, g, dres)


def _loss_head(h, g, tgt):
    T, D = h.shape
    tm = _tile(T, MM_ROWS_MID, 16)
    first_x = PAD + N_META

    def body(h_ref, g_ref, t_ref, loss_ref, dh_ref, dhb_ref, dg_ref):
        i = pl.program_id(0)
        x = h_ref[...]
        r = lax.rsqrt(jnp.mean(x * x, axis=-1, keepdims=True) + EPS)
        xh = x * r
        gv = g_ref[...]
        out = xh * gv
        valid = _row_ids(i * tm, tm) >= first_x
        e = jnp.where(valid, out - t_ref[...], 0.0)
        dout = e * (1.0 / D)
        dxh = dout * gv
        dx = r * (dxh - xh * jnp.mean(dxh * xh, axis=-1, keepdims=True))
        dh_ref[...] = dx
        dhb_ref[...] = dx.astype(BF16)

        @pl.when(i == 0)
        def _():
            dg_ref[...] = jnp.zeros_like(dg_ref)
            loss_ref[...] = jnp.zeros_like(loss_ref)

        dg_ref[...] += jnp.sum(dout * xh, axis=0, keepdims=True)
        loss_ref[...] += 0.5 * jnp.sum(jnp.mean(e * e, axis=-1, keepdims=True))

    row = pl.BlockSpec((tm, D), lambda i: (i, 0))
    vec = pl.BlockSpec((1, D), lambda i: (0, 0))
    return pl.pallas_call(
        body, grid=(T // tm,),
        in_specs=[row, vec, row],
        out_specs=[pl.BlockSpec((8, 128), lambda i: (0, 0)), row, row, vec],
        out_shape=[jax.ShapeDtypeStruct((8, 128), F32), jax.ShapeDtypeStruct((T, D), F32),
                   jax.ShapeDtypeStruct((T, D), BF16), jax.ShapeDtypeStruct((1, D), F32)],
        compiler_params=_cp("arbitrary"), name="loss_head")(h, g, tgt)


def _mm_nn(name, a, w, w_spec, M, N, K, tm, tn, tk, mode, extra=None, a_spec=None):
    nk = K // tk
    if a_spec is None:
        a_spec = pl.BlockSpec((tm, tk), lambda i, j, k: (i, k))
    o_spec = pl.BlockSpec((tm, tn), lambda i, j, k: (i, j))

    def body(*refs):
        if mode == "resid":
            a_ref, w_ref, e_ref = refs[:3]
            outs = refs[3:]
        else:
            a_ref, w_ref = refs[:2]
            outs = refs[2:]
        acc_ref = outs[-1] if nk > 1 else None
        part = _nn(a_ref[...], w_ref[...])

        def finish(acc):
            if mode == "f32":
                outs[0][...] = acc
            elif mode == "relu2":
                r = jnp.maximum(acc, 0.0)
                outs[0][...] = r.astype(BF16)
                outs[1][...] = (r * r).astype(BF16)
            else:
                keep = _row_ids(pl.program_id(0) * tm, tm) >= PAD
                outs[0][...] = jnp.where(keep, e_ref[...] + acc, 0.0)

        if nk == 1:
            finish(part)
        else:
            k = pl.program_id(2)

            @pl.when(k == 0)
            def _():
                acc_ref[...] = part

            @pl.when(k > 0)
            def _():
                acc_ref[...] += part

            @pl.when(k == nk - 1)
            def _():
                finish(acc_ref[...])

    in_specs = [a_spec, w_spec(tk, tn)]
    args = [a, w]
    if mode == "resid":
        in_specs.append(o_spec)
        args.append(extra)
    if mode == "relu2":
        out_specs = [o_spec, o_spec]
        out_shape = [jax.ShapeDtypeStruct((M, N), BF16)] * 2
    else:
        out_specs = [o_spec]
        out_shape = [jax.ShapeDtypeStruct((M, N), F32)]
    scratch = [pltpu.VMEM((tm, tn), F32)] if nk > 1 else []
    res = pl.pallas_call(
        body, grid=(M // tm, N // tn, nk), in_specs=in_specs, out_specs=out_specs, out_shape=out_shape,
        scratch_shapes=scratch, compiler_params=_cp("parallel", "parallel", "arbitrary"), name=name)(*args)
    return res if mode == "relu2" else res[0]


def _mm_rms_nn(name, h, g, w, tm, tn, mode):
    M, K = h.shape
    N = w.shape[1]

    def body(h_ref, g_ref, w_ref, n_ref, *outs):
        @pl.when(pl.program_id(1) == 0)
        def _():
            x = h_ref[...]
            r = lax.rsqrt(jnp.mean(x * x, axis=-1, keepdims=True) + EPS)
            n_ref[...] = ((x * r) * g_ref[...]).astype(BF16)

        acc = _nn(n_ref[...], w_ref[...])
        if mode == "f32":
            outs[0][...] = acc
        else:
            r = jnp.maximum(acc, 0.0)
            outs[0][...] = r.astype(BF16)
            outs[1][...] = (r * r).astype(BF16)

    row = pl.BlockSpec((tm, K), lambda i, j: (i, 0))
    o_spec = pl.BlockSpec((tm, tn), lambda i, j: (i, j))
    n_out = 1 if mode == "f32" else 2
    return pl.pallas_call(
        body, grid=(M // tm, N // tn),
        in_specs=[row, pl.BlockSpec((1, K), lambda i, j: (0, 0)), pl.BlockSpec((K, tn), lambda i, j: (0, j))],
        out_specs=[row] + [o_spec] * n_out,
        out_shape=[jax.ShapeDtypeStruct((M, K), BF16)] + [jax.ShapeDtypeStruct((M, N), F32 if mode == "f32" else BF16)] * n_out,
        compiler_params=_cp("parallel", "arbitrary"), name=name)(h, g, w)


def _mm_nt(name, dy, w, dy_spec, w_spec, M, J, N, tm, tj, tn, mode, extra=None, parts=1):
    nk = N // tn
    o_spec = pl.BlockSpec((tm, tj), lambda i, j, k: (i, j))
    n_extra = {"f32": 0, "dact": 1, "rms": 3}[mode]
    if mode == "rms":
        assert nk == 1 and tj == J

    def body(*refs):
        dy_ref, w_ref = refs[:2]
        ex = refs[2:2 + n_extra]
        outs = refs[2 + n_extra:]
        acc_ref = outs[-1] if nk > 1 else None
        if parts == 1:
            part = _nt(dy_ref[...], w_ref[...])
        else:
            wq = tn // parts
            part = _nt(dy_ref[0], w_ref[:, 0:wq])
            for q in range(1, parts):
                part = part + _nt(dy_ref[q], w_ref[:, q * wq:(q + 1) * wq])

        def finish(acc):
            if mode == "f32":
                outs[0][...] = acc
            elif mode == "dact":
                outs[0][...] = (acc * (2.0 * ex[0][...].astype(F32))).astype(BF16)
            else:
                h_ref, g_ref, dres_ref = ex
                dh_ref, dhb_ref, dg_ref = outs[:3]
                i = pl.program_id(0)
                x = h_ref[...]
                r = lax.rsqrt(jnp.mean(x * x, axis=-1, keepdims=True) + EPS)
                xh = x * r
                dxh = acc * g_ref[...]
                dx = r * (dxh - xh * jnp.mean(dxh * xh, axis=-1, keepdims=True))
                keep = _row_ids(i * tm, tm) >= PAD
                dh = jnp.where(keep, dres_ref[...] + dx, 0.0)
                dh_ref[...] = dh
                dhb_ref[...] = dh.astype(BF16)

                @pl.when(i == 0)
                def _():
                    dg_ref[...] = jnp.zeros_like(dg_ref)

                dg_ref[...] += jnp.sum(acc * xh, axis=0, keepdims=True)

        if nk == 1:
            finish(part)
        else:
            k = pl.program_id(2)

            @pl.when(k == 0)
            def _():
                acc_ref[...] = part

            @pl.when(k > 0)
            def _():
                acc_ref[...] += part

            @pl.when(k == nk - 1)
            def _():
                finish(acc_ref[...])

    in_specs = [dy_spec(tm, tn), w_spec(tj, tn)]
    args = [dy, w]
    scratch = [pltpu.VMEM((tm, tj), F32)] if nk > 1 else []
    if mode == "rms":
        vec = pl.BlockSpec((1, J), lambda i, j, k: (0, 0))
        h, g, dres = extra
        res = pl.pallas_call(
            body, grid=(M // tm, 1, 1), in_specs=in_specs + [o_spec, vec, o_spec], out_specs=[o_spec, o_spec, vec],
            out_shape=[jax.ShapeDtypeStruct((M, J), F32), jax.ShapeDtypeStruct((M, J), BF16),
                       jax.ShapeDtypeStruct((1, J), F32)],
            compiler_params=_cp("arbitrary", "arbitrary", "arbitrary"), name=name)(*args, h, g, dres)
        return res
    if mode == "dact":
        in_specs.append(o_spec)
        args.append(extra)
    return pl.pallas_call(
        body, grid=(M // tm, J // tj, nk), in_specs=in_specs, out_specs=[o_spec],
        out_shape=[jax.ShapeDtypeStruct((M, J), BF16 if mode == "dact" else F32)],
        scratch_shapes=scratch, compiler_params=_cp("parallel", "parallel", "arbitrary"), name=name)(*args)[0]


def _mm_tn(name, x, dy, x_spec, dy_spec, o_spec, o_shape, T, K, N, tt, tk, tn):
    nt = T // tt

    def body(x_ref, dy_ref, o_ref, *acc):
        part = _tn(x_ref[...], dy_ref[...])
        if nt == 1:
            o_ref[...] = part.astype(BF16)
            return
        acc_ref = acc[0]
        t = pl.program_id(2)

        @pl.when(t == 0)
        def _():
            acc_ref[...] = part

        @pl.when(t > 0)
        def _():
            acc_ref[...] += part

        @pl.when(t == nt - 1)
        def _():
            o_ref[...] = acc_ref[...].astype(BF16)

    return pl.pallas_call(
        body, grid=(K // tk, N // tn, nt), in_specs=[x_spec(tt, tk), dy_spec(tt, tn)], out_specs=o_spec(tk, tn),
        out_shape=jax.ShapeDtypeStruct(o_shape, BF16), scratch_shapes=[pltpu.VMEM((tk, tn), F32)] if nt > 1 else [],
        compiler_params=_cp("parallel", "parallel", "arbitrary"), name=name)(x, dy)


def _pool_counts(base, n, w):
    pos = _row_ids(base, n) - PAD
    return jnp.clip(pos + 1, 1, w).astype(F32)


def _shifted_copies(buf, rows):
    buf[0, rows:rows + 8, :] = jnp.zeros((8, buf.shape[2]), F32)

    def blk(s, carry):
        b = pl.multiple_of(s * HALO, HALO)
        win = buf[0, pl.ds(b, HALO + 8), :]
        for r in range(1, 8):
            buf[r, pl.ds(b, HALO), :] = win[r:r + HALO]
        return carry

    lax.fori_loop(0, rows // HALO, blk, 0)


def _ev_fwd(u, cw, cb, lg, lb, pw, pb, ps):
    T = u.shape[0]
    C = 512
    tm = _tile(T, EV_ROWS, HALO)
    nsub = tm // HALO
    hb = tm // HALO

    def body(val_ref, gate_ref, pin_ref, valh_ref, gateh_ref, pinh_ref, cw_ref, cb_ref, lg_ref, lb_ref, pw_ref,
             pb_ref, ps_ref, yab_ref, yc_ref, a_ext, p_ext, d_buf):
        i = pl.program_id(0)
        nf = (i > 0).astype(F32)
        a_ext[0, 0:HALO, :] = valh_ref[...] * jax.nn.sigmoid(gateh_ref[...]) * nf
        a_ext[0, HALO:HALO + tm, :] = val_ref[...] * jax.nn.sigmoid(gate_ref[...])
        p_ext[0:HALO, :] = pinh_ref[...] * nf
        p_ext[HALO:, :] = pin_ref[...]
        _shifted_copies(a_ext, tm + HALO)

        def sub(s, carry):
            base = pl.multiple_of(s * HALO, HALO)
            acc = jnp.zeros((HALO, C), F32) + cb_ref[...]
            for j in range(CONV_WIDTH):
                off = 2 + j
                acc = acc + cw_ref[pl.ds(j, 1), :] * a_ext[off % 8, pl.ds(pl.multiple_of(base + off // 8 * 8, 8), HALO), :]
            yc_ref[pl.ds(base, HALO), :] = acc
            mu = jnp.mean(acc, axis=-1, keepdims=True)
            yc = acc - mu
            rstd = lax.rsqrt(jnp.mean(yc * yc, axis=-1, keepdims=True) + EPS)
            z = (yc * rstd) * lg_ref[...] + lb_ref[...]
            yab_ref[pl.ds(base, HALO), 0:C] = (z * jax.nn.sigmoid(z)).astype(BF16)
            pwin = p_ext[pl.ds(base, 2 * HALO), :]
            for gi, w in enumerate(POOL_WINDOWS):
                lo, hi = gi * HEAD, (gi + 1) * HEAD
                x = pwin[HALO:, lo:hi]
                tot = x
                for k in range(1, w):
                    tot = tot + pwin[HALO - k:2 * HALO - k, lo:hi]
                cnt = _pool_counts(i * tm + base, HALO, w)
                d_buf[pl.ds(base, HALO), lo:hi] = (tot / cnt - x).astype(BF16)
            return carry

        lax.fori_loop(0, nsub, sub, 0)
        for gi in range(len(POOL_WINDOWS)):
            lo, hi = gi * HEAD, (gi + 1) * HEAD
            y = _nn(d_buf[:, lo:hi], pw_ref[gi]) + pb_ref[:, lo:hi]
            yab_ref[:, C + lo:C + hi] = (y * ps_ref[:, lo:hi]).astype(BF16)

    def main(c):
        return pl.BlockSpec((tm, C), lambda i: (i, c))

    def halo(c):
        return pl.BlockSpec((HALO, C), lambda i: (jnp.maximum(i * hb - 1, 0), c))

    vec = pl.BlockSpec((1, C), lambda i: (0, 0))
    return pl.pallas_call(
        body, grid=(T // tm,),
        in_specs=[main(0), main(1), main(2), halo(0), halo(1), halo(2),
                  pl.BlockSpec((32, C), lambda i: (0, 0)), vec, vec, vec,
                  pl.BlockSpec((4, HEAD, HEAD), lambda i: (0, 0, 0)), vec, vec],
        out_specs=[pl.BlockSpec((tm, 2 * C), lambda i: (i, 0)), pl.BlockSpec((tm, C), lambda i: (i, 0))],
        out_shape=[jax.ShapeDtypeStruct((T, 2 * C), BF16), jax.ShapeDtypeStruct((T, C), F32)],
        scratch_shapes=[pltpu.VMEM((8, tm + HALO + 8, C), F32), pltpu.VMEM((tm + HALO, C), F32),
                        pltpu.VMEM((tm, C), BF16)],
        compiler_params=_cp("parallel"), name="ev_fwd")(u, u, u, u, u, u, cw, cb, lg, lb, pw, pb, ps)


def _ev_bwd(dyab, yc, u, cw, lg, lb, pw, pwt, pb, ps):
    T = u.shape[0]
    C = 512
    tm = _tile(T, EV_ROWS, HALO)
    nsub = tm // HALO
    hb = tm // HALO
    nblk = T // tm
    E = tm + HALO

    def body(dya_ref, dyb_ref, dyah_ref, dybh_ref, yc_ref, ych_ref, val_ref, gate_ref, pin_ref, valh_ref, gateh_ref,
             pinh_ref, cw_ref, lg_ref, lb_ref, pw_ref, pwt_ref, pb_ref, ps_ref,
             du_ref, dcw_ref, dvec_ref, dpw_ref,
             dy_ext, a_ext, p_ext, ddc_ext, dd_buf, d_buf, dpre_buf, dcw_acc, vec_acc):
        i = pl.program_id(0)
        nf = (i > 0).astype(F32)
        nl = (i < nblk - 1).astype(F32)

        @pl.when(i == 0)
        def _():
            dcw_ref[...] = jnp.zeros_like(dcw_ref)
            dvec_ref[...] = jnp.zeros_like(dvec_ref)
            dpw_ref[...] = jnp.zeros_like(dpw_ref)

        dcw_acc[...] = jnp.zeros_like(dcw_acc)
        vec_acc[...] = jnp.zeros_like(vec_acc)
        a_ext[0, 0:HALO, :] = valh_ref[...] * jax.nn.sigmoid(gateh_ref[...]) * nf
        a_ext[0, HALO:E, :] = val_ref[...] * jax.nn.sigmoid(gate_ref[...])
        p_ext[0:HALO, :] = pinh_ref[...] * nf
        p_ext[HALO:, :] = pin_ref[...]
        _shifted_copies(a_ext, E)

        def ln_bwd(y, dya, main):
            mu = jnp.mean(y, axis=-1, keepdims=True)
            ycen = y - mu
            rstd = lax.rsqrt(jnp.mean(ycen * ycen, axis=-1, keepdims=True) + EPS)
            yh = ycen * rstd
            z = yh * lg_ref[...] + lb_ref[...]
            sz = jax.nn.sigmoid(z)
            dz = dya * _dsilu(z, sz)
            dyh = dz * lg_ref[...]
            dy = rstd * (dyh - jnp.mean(dyh, axis=-1, keepdims=True) - yh * jnp.mean(dyh * yh, axis=-1, keepdims=True))
            if main:
                vec_acc[1] += jnp.sum((dz * yh).reshape(HALO // 8, 8, C), axis=0)
                vec_acc[2] += jnp.sum(dz.reshape(HALO // 8, 8, C), axis=0)
                vec_acc[0] += jnp.sum(dy.reshape(HALO // 8, 8, C), axis=0)
            return dy

        def pool_dd(dyb, base, main):
            dpre = dyb * ps_ref[...]
            for gi, w in enumerate(POOL_WINDOWS):
                lo, hi = gi * HEAD, (gi + 1) * HEAD
                dd = _nn(dpre[:, lo:hi].astype(BF16), pwt_ref[gi])
                cnt = _pool_counts(i * tm + base, HALO, w)
                ddc_ext[pl.ds(base, HALO), lo:hi] = dd / cnt
                if main:
                    dd_buf[pl.ds(base, HALO), lo:hi] = dd
            if main:
                dpre_buf[pl.ds(base, HALO), :] = dpre.astype(BF16)
                vec_acc[4] += jnp.sum(dpre.reshape(HALO // 8, 8, C), axis=0)

        def p1(s, carry):
            base = pl.multiple_of(s * HALO, HALO)
            dy_ext[0, pl.ds(base, HALO), :] = ln_bwd(yc_ref[pl.ds(base, HALO), :], dya_ref[pl.ds(base, HALO), :], True)
            pool_dd(dyb_ref[pl.ds(base, HALO), :], base, True)
            return carry

        lax.fori_loop(0, nsub, p1, 0)
        dy_ext[0, tm:E, :] = ln_bwd(ych_ref[...], dyah_ref[...], False) * nl
        _shifted_copies(dy_ext, E)
        dpre_h = dybh_ref[...] * ps_ref[...] * nl
        for gi, w in enumerate(POOL_WINDOWS):
            lo, hi = gi * HEAD, (gi + 1) * HEAD
            dd = _nn(dpre_h[:, lo:hi].astype(BF16), pwt_ref[gi])
            ddc_ext[tm:, lo:hi] = dd / _pool_counts(i * tm + tm, HALO, w)

        def p2(s, carry):
            base = pl.multiple_of(s * HALO, HALO)
            dy_m = dy_ext[0, pl.ds(base, HALO), :]
            da = jnp.zeros((HALO, C), F32)
            for j in range(CONV_WIDTH):
                sh = CONV_WIDTH - 1 - j
                off = 2 + j
                da = da + cw_ref[pl.ds(j, 1), :] * dy_ext[sh % 8, pl.ds(pl.multiple_of(base + sh // 8 * 8, 8), HALO), :]
                a_j = a_ext[off % 8, pl.ds(pl.multiple_of(base + off // 8 * 8, 8), HALO), :]
                dcw_acc[j] += jnp.sum((dy_m * a_j).reshape(HALO // 8, 8, C), axis=0)
            v = val_ref[pl.ds(base, HALO), :]
            g = gate_ref[pl.ds(base, HALO), :]
            sg = jax.nn.sigmoid(g)
            du_ref[pl.ds(base, HALO), 0:C] = (da * sg).astype(BF16)
            du_ref[pl.ds(base, HALO), C:2 * C] = (da * v * sg * (1.0 - sg)).astype(BF16)
            pwin = p_ext[pl.ds(base, 2 * HALO), :]
            cwin = ddc_ext[pl.ds(base, 2 * HALO), :]
            for gi, w in enumerate(POOL_WINDOWS):
                lo, hi = gi * HEAD, (gi + 1) * HEAD
                x = pwin[HALO:, lo:hi]
                tot = x
                back = cwin[0:HALO, lo:hi]
                for k in range(1, w):
                    tot = tot + pwin[HALO - k:2 * HALO - k, lo:hi]
                    back = back + cwin[k:k + HALO, lo:hi]
                cnt = _pool_counts(i * tm + base, HALO, w)
                d_buf[pl.ds(base, HALO), lo:hi] = (tot / cnt - x).astype(BF16)
                du_ref[pl.ds(base, HALO), 2 * C + lo:2 * C + hi] = (back - dd_buf[pl.ds(base, HALO), lo:hi]).astype(BF16)
            return carry

        lax.fori_loop(0, nsub, p2, 0)
        for gi in range(len(POOL_WINDOWS)):
            lo, hi = gi * HEAD, (gi + 1) * HEAD
            pre = _nn(d_buf[:, lo:hi], pw_ref[gi]) + pb_ref[:, lo:hi]
            vec_acc[3, :, lo:hi] += jnp.sum((dyb_ref[:, lo:hi] * pre).reshape(tm // 8, 8, HEAD), axis=0)
            dpw_ref[gi] += _tn(d_buf[:, lo:hi], dpre_buf[:, lo:hi])
        for j in range(CONV_WIDTH):
            dcw_ref[pl.ds(j, 1), :] += jnp.sum(dcw_acc[j], axis=0, keepdims=True)
        for r in range(5):
            dvec_ref[pl.ds(r, 1), :] += jnp.sum(vec_acc[r], axis=0, keepdims=True)

    def main(c, width=C):
        return pl.BlockSpec((tm, width), lambda i: (i, c))

    def prev(c):
        return pl.BlockSpec((HALO, C), lambda i: (jnp.maximum(i * hb - 1, 0), c))

    def nxt(c):
        return pl.BlockSpec((HALO, C), lambda i: (jnp.minimum((i + 1) * hb, T // HALO - 1), c))

    vec = pl.BlockSpec((1, C), lambda i: (0, 0))
    mat = pl.BlockSpec((4, HEAD, HEAD), lambda i: (0, 0, 0))
    return pl.pallas_call(
        body, grid=(nblk,),
        in_specs=[main(0), main(1), nxt(0), nxt(1), main(0), nxt(0), main(0), main(1), main(2), prev(0), prev(1),
                  prev(2), pl.BlockSpec((32, C), lambda i: (0, 0)), vec, vec, mat, mat, vec, vec],
        out_specs=[pl.BlockSpec((tm, 3 * C), lambda i: (i, 0)), pl.BlockSpec((32, C), lambda i: (0, 0)),
                   pl.BlockSpec((8, C), lambda i: (0, 0)), mat],
        out_shape=[jax.ShapeDtypeStruct((T, 3 * C), BF16), jax.ShapeDtypeStruct((32, C), F32),
                   jax.ShapeDtypeStruct((8, C), F32), jax.ShapeDtypeStruct((4, HEAD, HEAD), F32)],
        scratch_shapes=[pltpu.VMEM((8, E + 8, C), F32), pltpu.VMEM((8, E + 8, C), F32), pltpu.VMEM((E, C), F32),
                        pltpu.VMEM((E, C), F32), pltpu.VMEM((tm, C), F32), pltpu.VMEM((tm, C), BF16),
                        pltpu.VMEM((tm, C), BF16), pltpu.VMEM((32, 8, C), F32), pltpu.VMEM((8, 8, C), F32)],
        compiler_params=_cp("arbitrary"), name="ev_bwd")(
            dyab, dyab, dyab, dyab, yc, yc, u, u, u, u, u, u, cw, lg, lb, pw, pwt, pb, ps)


def _cumsum_rows(x, reverse=False):
    n = x.shape[0]
    rid = lax.broadcasted_iota(jnp.int32, (n, 1), 0)
    k = 1
    while k < n:
        if reverse:
            sh = jnp.where(rid < n - k, pltpu.roll(x, n - k, 0), 0.0)
        else:
            sh = jnp.where(rid >= k, pltpu.roll(x, k, 0), 0.0)
        x = x + sh
        k *= 2
    return x


def _hgrn_gates(qr, fr, lbv):
    sq = jax.nn.sigmoid(qr)
    sg = jax.nn.sigmoid(fr)
    fg = lbv + (1.0 - lbv) * sg
    return qr * sq, sq, sg, fg, 1.0 - fg, jnp.log(fg)


def _hgrn_fwd(u, lbv, gn):
    T = u.shape[0]
    H = 8
    RB = _tile(T, HGRN_ROWS, CHUNK)
    NC = RB // CHUNK
    NS = CHUNK // SUB

    HP = HGRN_HEADS_FWD
    W = HP * HEAD

    def body(q_ref, f_ref, i_ref, g_ref, lb_ref, gn_ref, y_ref, o_ref, s0_ref, st, qs, ks, bs, vs, os_):
        rb = pl.program_id(1)

        @pl.when(rb == 0)
        def _():
            st[...] = jnp.zeros_like(st)

        t8 = lax.broadcasted_iota(jnp.int32, (8, 1), 0)

        def head(hh, c, rows):
            sl = slice(hh * HEAD, (hh + 1) * HEAD)
            q, _, _, _, kk, lf = _hgrn_gates(q_ref[rows, sl], f_ref[rows, sl], lb_ref[:, sl])
            v = i_ref[rows, sl]
            b = _cumsum_rows(lf)
            qs[hh] = q
            ks[hh] = kk
            bs[hh] = b
            vs[hh] = v
            st0 = st[hh]
            s0_ref[hh, c] = st0
            os_[hh] = _nt((q * jnp.exp(b)).astype(BF16), st0.astype(BF16))
            for I in range(NS):
                lo = I * SUB
                qI = qs[hh, lo:lo + SUB, :]
                bI = bs[hh, lo:lo + SUB, :]
                oI = jnp.zeros((SUB, HEAD), F32)
                if I > 0:
                    bprev = bs[hh, pl.ds(lo - 1, 1), :]
                    qt = _r16(qI * jnp.exp(bI - bprev))
                    kt = _r16(ks[hh, 0:lo, :] * jnp.exp(bprev - bs[hh, 0:lo, :]))
                    A = _nt(qt, kt)
                    oI = oI + _nn(_r16(A), _r16(vs[hh, 0:lo, :]))
                od = [jnp.zeros((8, HEAD), F32) for _ in range(SUB // 8)]
                for s in range(SUB):
                    row = pl.ds(lo + s, 1)
                    brow, krow, vrow = bs[hh, row, :], ks[hh, row, :], vs[hh, row, :]
                    for ti in range(SUB // 8):
                        o8 = 8 * ti
                        if s > o8 + 7:
                            continue
                        d = bI[o8:o8 + 8] - brow
                        if s > o8:
                            d = jnp.where(t8 >= s - o8, d, NEG)
                        col = jnp.sum(qI[o8:o8 + 8] * jnp.exp(d) * krow, axis=1, keepdims=True)
                        od[ti] = od[ti] + col * vrow
                os_[hh, lo:lo + SUB, :] += oI + jnp.concatenate(od, axis=0)
            blast = bs[hh, pl.ds(CHUNK - 1, 1), :]
            kh = kk * jnp.exp(blast - b)
            st[hh] = st0 * jnp.exp(blast) + _tn(v.astype(BF16), kh.astype(BF16))
            o = os_[hh]
            o_ref[rows, sl] = o
            rr = lax.rsqrt(jnp.mean(o * o, axis=-1, keepdims=True) + EPS)
            gr = g_ref[rows, sl]
            y_ref[rows, sl] = (((o * rr) * gn_ref[...]) * (gr * jax.nn.sigmoid(gr))).astype(BF16)

        def chunk(c, carry):
            rows = pl.ds(pl.multiple_of(c * CHUNK, CHUNK), CHUNK)
            for hh in range(HP):
                head(hh, c, rows)
            return carry

        lax.fori_loop(0, NC, chunk, 0)

    def blk(q):
        return pl.BlockSpec((RB, W), lambda h, r: (r, q * (H // HP) + h))

    sc = lambda: pltpu.VMEM((HP, CHUNK, HEAD), F32)
    return pl.pallas_call(
        body, grid=(H // HP, T // RB),
        in_specs=[blk(0), blk(1), blk(2), blk(3), pl.BlockSpec((1, W), lambda h, r: (0, h)),
                  pl.BlockSpec((1, HEAD), lambda h, r: (0, 0))],
        out_specs=[pl.BlockSpec((RB, W), lambda h, r: (r, h)), pl.BlockSpec((RB, W), lambda h, r: (r, h)),
                   pl.BlockSpec((HP, NC, HEAD, HEAD), lambda h, r: (h, r, 0, 0))],
        out_shape=[jax.ShapeDtypeStruct((T, H * HEAD), BF16), jax.ShapeDtypeStruct((T, H * HEAD), F32),
                   jax.ShapeDtypeStruct((H, T // CHUNK, HEAD, HEAD), F32)],
        scratch_shapes=[pltpu.VMEM((HP, HEAD, HEAD), F32), sc(), sc(), sc(), sc(), sc()],
        compiler_params=_cp("parallel", "arbitrary"), name="hgrn_fwd")(u, u, u, u, lbv, gn)


def _hgrn_bwd(dy, o, s0, u, lbv, gn):
    T = u.shape[0]
    H = 8
    RB = _tile(T, HGRN_ROWS, CHUNK)
    NB = T // RB
    NC = RB // CHUNK
    NS = CHUNK // SUB

    def body(q_ref, f_ref, i_ref, g_ref, lb_ref, gn_ref, o_ref, dy_ref, s0_ref, du_ref, dlb_ref, dgn_ref,
             dst, qs, ks, bs, vs, dos, dqs, dks, dki, dvs, dbs):
        rb = pl.program_id(1)

        @pl.when(rb == 0)
        def _():
            dst[...] = jnp.zeros_like(dst)
            dlb_ref[...] = jnp.zeros_like(dlb_ref)
            dgn_ref[...] = jnp.zeros_like(dgn_ref)

        t8 = lax.broadcasted_iota(jnp.int32, (8, 1), 0)
        lane = lax.broadcasted_iota(jnp.int32, (8, HEAD), 1)
        gnv = gn_ref[...]

        def head(hh, c, rows):
            sl = slice(hh * HEAD, (hh + 1) * HEAD)
            lbv_ = lb_ref[:, sl]
            qr = q_ref[rows, sl]
            q, sq, sg, fg, kk, lf = _hgrn_gates(qr, f_ref[rows, sl], lbv_)
            v = i_ref[rows, sl]
            gr = g_ref[rows, sl]
            b = _cumsum_rows(lf)
            eb = jnp.exp(b)
            ov = o_ref[rows, sl]
            dyv = dy_ref[rows, sl]
            rr = lax.rsqrt(jnp.mean(ov * ov, axis=-1, keepdims=True) + EPS)
            oh = ov * rr
            gs = jax.nn.sigmoid(gr)
            dgr = dyv * (oh * gnv) * _dsilu(gr, gs)
            dnrm = dyv * (gr * gs)
            dgn_ref[hh] += jnp.sum(dnrm * oh, axis=0, keepdims=True)
            t1 = dnrm * gnv
            do = rr * (t1 - oh * jnp.mean(t1 * oh, axis=-1, keepdims=True))
            qs[hh] = q
            ks[hh] = kk
            bs[hh] = b
            vs[hh] = v
            dos[hh] = do
            st0 = s0_ref[hh, c]
            dS = dst[hh]
            do_b = do.astype(BF16)
            blast = bs[hh, pl.ds(CHUNK - 1, 1), :]
            elast = jnp.exp(blast - b)
            dq_inter = _nn(do_b, st0.astype(BF16)) * eb
            dqs[hh] = dq_inter
            dbs[hh] = q * dq_inter
            kh = kk * elast
            dvs[hh] = _nt(kh.astype(BF16), dS.astype(BF16))
            dk_inter = _nn(v.astype(BF16), dS.astype(BF16)) * elast
            dki[hh] = dk_inter
            dks[hh] = jnp.zeros((CHUNK, HEAD), F32)
            for I in range(NS):
                lo = I * SUB
                qI = qs[hh, lo:lo + SUB, :]
                bI = bs[hh, lo:lo + SUB, :]
                doI = dos[hh, lo:lo + SUB, :]
                dqI = jnp.zeros((SUB, HEAD), F32)
                dbI = jnp.zeros((SUB, HEAD), F32)
                if I > 0:
                    bprev = bs[hh, pl.ds(lo - 1, 1), :]
                    eq = jnp.exp(bI - bprev)
                    ek = jnp.exp(bprev - bs[hh, 0:lo, :])
                    qt = _r16(qI * eq)
                    kt = _r16(ks[hh, 0:lo, :] * ek)
                    A = _r16(_nt(qt, kt))
                    doI_b = _r16(doI)
                    dA = _r16(_nt(doI_b, _r16(vs[hh, 0:lo, :])))
                    dvs[hh, 0:lo, :] += _tn(A, doI_b)
                    dqt = _nn(dA, kt)
                    dkt = _tn(dA, qt)
                    dqI = dqI + dqt * eq
                    dbI = dbI + qt.astype(F32) * dqt
                    dks[hh, 0:lo, :] += dkt * ek
                    dbs[hh, 0:lo, :] -= kt.astype(F32) * dkt
                dq_t = [jnp.zeros((8, HEAD), F32) for _ in range(SUB // 8)]
                a_t = [jnp.zeros((8, HEAD), F32) for _ in range(SUB // 8)]
                for s in range(SUB):
                    row = pl.ds(lo + s, 1)
                    brow, krow, vrow = bs[hh, row, :], ks[hh, row, :], vs[hh, row, :]
                    dk_s = None
                    for ti in range(SUB // 8):
                        o8 = 8 * ti
                        if s > o8 + 7:
                            continue
                        d = bI[o8:o8 + 8] - brow
                        if s > o8:
                            d = jnp.where(t8 >= s - o8, d, NEG)
                        Es = jnp.exp(d)
                        qE = qI[o8:o8 + 8] * Es
                        col = jnp.sum(qE * krow, axis=1, keepdims=True)
                        a_t[ti] = jnp.where(lane == s, col, a_t[ti])
                        dcol = jnp.sum(doI[o8:o8 + 8] * vrow, axis=1, keepdims=True)
                        dq_t[ti] = dq_t[ti] + (dcol * Es) * krow
                        part = jnp.sum(dcol * qE, axis=0, keepdims=True)
                        dk_s = part if dk_s is None else dk_s + part
                    dks[hh, row, :] += dk_s
                    dbs[hh, row, :] -= krow * dk_s
                a_d = jnp.concatenate(a_t, axis=0)
                dq_d = jnp.concatenate(dq_t, axis=0)
                dvs[hh, lo:lo + SUB, :] += _tn(a_d, doI)[0:SUB]
                dqI = dqI + dq_d
                dbI = dbI + qI * dq_d
                dqs[hh, lo:lo + SUB, :] += dqI
                dbs[hh, lo:lo + SUB, :] += dbI
            kdk = kk * dki[hh]
            excl = _cumsum_rows(kdk) - kdk
            suff = _cumsum_rows(dbs[hh], reverse=True)
            gdec = jnp.sum(dS * st0, axis=0, keepdims=True) * jnp.exp(blast)
            dlf = suff + excl + gdec
            dk = dks[hh] + dki[hh]
            dfg = dlf / fg - dk
            dlb_ref[:, sl] += jnp.sum(dfg * (1.0 - sg), axis=0, keepdims=True)
            du_ref[0, rows, sl] = (dqs[hh] * _dsilu(qr, sq)).astype(BF16)
            du_ref[1, rows, sl] = (dfg * (1.0 - lbv_) * sg * (1.0 - sg)).astype(BF16)
            du_ref[2, rows, sl] = dvs[hh].astype(BF16)
            du_ref[3, rows, sl] = dgr.astype(BF16)
            dst[hh] = dS * jnp.exp(blast) + _tn(do_b, (q * eb).astype(BF16))

        def chunk(cc, carry):
            c = NC - 1 - cc
            rows = pl.ds(pl.multiple_of(c * CHUNK, CHUNK), CHUNK)
            for hh in range(HP):
                head(hh, c, rows)
            return carry

        lax.fori_loop(0, NC, chunk, 0)

    HP = HGRN_HEADS_BWD
    W = HP * HEAD

    def blk(qd):
        return pl.BlockSpec((RB, W), lambda h, r: (NB - 1 - r, qd * (H // HP) + h))

    hblk = pl.BlockSpec((RB, W), lambda h, r: (NB - 1 - r, h))
    sc = lambda: pltpu.VMEM((HP, CHUNK, HEAD), F32)
    return pl.pallas_call(
        body, grid=(H // HP, NB),
        in_specs=[blk(0), blk(1), blk(2), blk(3), pl.BlockSpec((1, W), lambda h, r: (0, h)),
                  pl.BlockSpec((1, HEAD), lambda h, r: (0, 0)), hblk, hblk,
                  pl.BlockSpec((HP, NC, HEAD, HEAD), lambda h, r: (h, NB - 1 - r, 0, 0))],
        out_specs=[pl.BlockSpec((4, RB, W), lambda h, r: (0, NB - 1 - r, h)),
                   pl.BlockSpec((1, W), lambda h, r: (0, h)), pl.BlockSpec((HP, 1, HEAD), lambda h, r: (h, 0, 0))],
        out_shape=[jax.ShapeDtypeStruct((4, T, H * HEAD), BF16), jax.ShapeDtypeStruct((1, H * HEAD), F32),
                   jax.ShapeDtypeStruct((H, 1, HEAD), F32)],
        scratch_shapes=[pltpu.VMEM((HP, HEAD, HEAD), F32)] + [sc() for _ in range(10)],
        compiler_params=_cp("parallel", "arbitrary"), name="hgrn_bwd")(u, u, u, u, lbv, gn, o, dy, s0)


def _softmax_rows(p_ref, L):
    rows = [p_ref[pl.ds(l, 1), :] for l in range(L)]
    m = rows[0]
    for r in rows[1:]:
        m = jnp.maximum(m, r)
    e = [jnp.exp(r - m) for r in rows]
    tot = e[0]
    for t in e[1:]:
        tot = tot + t
    return [t / tot for t in e]


def _lb_fwd(lbp):
    L, D = lbp.shape

    def body(p_ref, o_ref):
        sm = _softmax_rows(p_ref, L)
        acc = jnp.zeros((1, D), F32)
        o_ref[pl.ds(0, 1), :] = acc
        for l in range(1, L):
            acc = acc + sm[l]
            o_ref[pl.ds(l, 1), :] = acc

    return pl.pallas_call(body, out_shape=jax.ShapeDtypeStruct((L, D), F32), name="lb_fwd")(lbp)


def _lb_bwd(lbp, dlb):
    L, D = lbp.shape

    def body(p_ref, d_ref, o_ref):
        sm = _softmax_rows(p_ref, L)
        dsm = [jnp.zeros((1, D), F32)]
        for i in range(1, L):
            t = jnp.zeros((1, D), F32)
            for l in range(i, L):
                t = t + d_ref[pl.ds(l, 1), :]
            dsm.append(t)
        dot = jnp.zeros((1, D), F32)
        for i in range(L):
            dot = dot + dsm[i] * sm[i]
        for i in range(L):
            o_ref[pl.ds(i, 1), :] = sm[i] * (dsm[i] - dot)

    return pl.pallas_call(body, out_shape=jax.ShapeDtypeStruct((L, D), F32), name="lb_bwd")(lbp, dlb)


def _my_pos():
    return lax.axis_index("x"), lax.axis_index("y"), lax.axis_index("c")


def _peer(mask):
    x, y, c = _my_pos()
    mx, my, mc = (mask >> 2) & 1, (mask >> 1) & 1, mask & 1
    px = (1 - x) if mx else x
    py = (1 - y) if my else y
    pc = (1 - c) if mc else c
    return (px, py, pc), 4 * px + 2 * py + pc


def _all_gather(shards):
    n = len(shards)

    def body(*refs):
        ins, outs = refs[:n], refs[n:2 * n]
        send_sems, recv_sems, local_sems = refs[2 * n:]
        x, y, c = _my_pos()
        me = 4 * x + 2 * y + c
        local = [pltpu.make_async_copy(ins[a], outs[a].at[:, me], local_sems.at[a]) for a in range(n)]
        for cp in local:
            cp.start()
        sends = []
        for m in range(1, N_DEV):
            peer, _ = _peer(m)
            for a in range(n):
                cp = pltpu.make_async_remote_copy(
                    src_ref=ins[a], dst_ref=outs[a].at[:, me], send_sem=send_sems.at[a, m - 1],
                    recv_sem=recv_sems.at[a, m - 1], device_id=peer, device_id_type=MESH)
                cp.start()
                sends.append(cp)
        for m in range(1, N_DEV):
            peer, pid = _peer(m)
            for a in range(n):
                pltpu.make_async_remote_copy(
                    src_ref=ins[a], dst_ref=outs[a].at[:, pid], send_sem=send_sems.at[a, m - 1],
                    recv_sem=recv_sems.at[a, m - 1], device_id=peer, device_id_type=MESH).wait_recv()
        for cp in sends:
            cp.wait_send()
        for cp in local:
            cp.wait()

    out_shape = [jax.ShapeDtypeStruct((s.shape[0], N_DEV) + s.shape[1:], s.dtype) for s in shards]
    return pl.pallas_call(
        body, in_specs=[ANY] * n, out_specs=[ANY] * n, out_shape=out_shape,
        scratch_shapes=[pltpu.SemaphoreType.DMA((n, N_DEV - 1)), pltpu.SemaphoreType.DMA((n, N_DEV - 1)),
                        pltpu.SemaphoreType.DMA((n,))],
        name="all_gather_weights")(*shards)


def _exchange(grads, groups):
    n = len(grads)
    ng = 1 + max(g for g, _ in groups)
    layers = [1 + max(l for g, l in groups if g == gi) for gi in range(ng)]
    shapes = [None] * ng
    for a, (g, l) in enumerate(groups):
        shapes[g] = grads[a].shape[1:]

    def body(*refs):
        ins, outs = refs[:n], refs[n:n + ng]
        send_sems, recv_sems, local_sems = refs[n + ng:]
        x, y, c = _my_pos()
        me = 4 * x + 2 * y + c
        local = []
        for a, (g, l) in enumerate(groups):
            cp = pltpu.make_async_copy(ins[a].at[me], outs[g].at[me, l], local_sems.at[a])
            cp.start()
            local.append(cp)
        sends = []
        for m in range(1, N_DEV):
            peer, pid = _peer(m)
            for a, (g, l) in enumerate(groups):
                cp = pltpu.make_async_remote_copy(
                    src_ref=ins[a].at[pid], dst_ref=outs[g].at[me, l], send_sem=send_sems.at[a, m - 1],
                    recv_sem=recv_sems.at[a, m - 1], device_id=peer, device_id_type=MESH)
                cp.start()
                sends.append(cp)
        for m in range(1, N_DEV):
            peer, pid = _peer(m)
            for a, (g, l) in enumerate(groups):
                pltpu.make_async_remote_copy(
                    src_ref=ins[a].at[pid], dst_ref=outs[g].at[pid, l], send_sem=send_sems.at[a, m - 1],
                    recv_sem=recv_sems.at[a, m - 1], device_id=peer, device_id_type=MESH).wait_recv()
        for cp in sends:
            cp.wait_send()
        for cp in local:
            cp.wait()

    out_shape = [jax.ShapeDtypeStruct((N_DEV, layers[g]) + shapes[g], grads[[gg for gg, _ in groups].index(g)].dtype)
                 for g in range(ng)]
    return pl.pallas_call(
        body, in_specs=[ANY] * n, out_specs=[ANY] * ng, out_shape=out_shape,
        scratch_shapes=[pltpu.SemaphoreType.DMA((n, N_DEV - 1)), pltpu.SemaphoreType.DMA((n, N_DEV - 1)),
                        pltpu.SemaphoreType.DMA((n,))],
        name="exchange_grads")(*grads)


HBM_SPEC = pl.BlockSpec(memory_space=pltpu.HBM)
SEM_SPEC = pl.BlockSpec(memory_space=pltpu.SEMAPHORE)
EFFECT = pltpu.SideEffectType.DATAFLOW_SIDE_EFFECTING


def _hbm(a):
    return pltpu.with_memory_space_constraint(a, pltpu.HBM)


def _landing(block_shape, dtype, axis=0):
    if axis == 0:
        return lax.empty((N_DEV,) + tuple(block_shape), dtype)
    rows, n = block_shape
    return lax.empty((rows, N_DEV * n), dtype)


def _slot(ref, i):
    if len(ref.shape) == 2:
        n = ref.shape[1] // N_DEV
        return ref.at[:, pl.ds(i * n, n)]
    return ref.at[i]


def _push_start(name, srcs, lands, whole, groups):
    n = len(srcs)
    ng = 1 + max(groups)
    cnt = [groups.count(g) for g in range(ng)]
    idx = [groups[:a].count(groups[a]) for a in range(n)]

    def body(*refs):
        src_refs, land_refs = refs[:n], refs[n:2 * n]
        sems = refs[2 * n:2 * n + 3 * ng]
        token = refs[-1]
        x, y, c = _my_pos()
        me = 4 * x + 2 * y + c
        for a in range(n):
            g = groups[a]
            for m in range(1, N_DEV):
                peer, pid = _peer(m)
                pltpu.make_async_remote_copy(
                    src_ref=src_refs[a] if whole else src_refs[a].at[pid], dst_ref=_slot(land_refs[a], me),
                    send_sem=sems[3 * g].at[idx[a] * (N_DEV - 1) + m - 1],
                    recv_sem=sems[3 * g + 1].at[idx[a] * (N_DEV - 1) + m - 1],
                    device_id=peer, device_id_type=MESH).start()
            pltpu.make_async_copy(src_refs[a] if whole else src_refs[a].at[me], _slot(land_refs[a], me),
                                  sems[3 * g + 2].at[idx[a]]).start()
        token[...] = jnp.zeros_like(token)

    sem_shapes = []
    for g in range(ng):
        sem_shapes += [pltpu.SemaphoreType.DMA((cnt[g] * (N_DEV - 1),))] * 2 + [pltpu.SemaphoreType.DMA((cnt[g],))]
    thru = [pltpu.HBM(s.shape, s.dtype) for s in list(srcs) + list(lands)]
    res = pl.pallas_call(
        body, name=name,
        out_shape=tuple(sem_shapes + thru + [jax.ShapeDtypeStruct((8, 128), F32)]),
        in_specs=tuple([HBM_SPEC] * (2 * n)),
        out_specs=tuple([SEM_SPEC] * (3 * ng) + [HBM_SPEC] * (2 * n) + [pl.BlockSpec(memory_space=pltpu.VMEM)]),
        input_output_aliases={i: 3 * ng + i for i in range(2 * n)},
        compiler_params=pltpu.CompilerParams(has_side_effects=EFFECT),
    )(*[_hbm(s) for s in srcs], *[_hbm(z) for z in lands])
    sems = [(res[3 * g], res[3 * g + 1], res[3 * g + 2]) for g in range(ng)]
    srcs_thru = list(res[3 * ng:3 * ng + n])
    lands_thru = list(res[3 * ng + n:3 * ng + 2 * n])
    return sems, srcs_thru, lands_thru, res[-1]


def _push_wait(name, srcs_thru, lands_thru, sems, after, whole):
    n = len(srcs_thru)

    def body(*refs):
        src_refs, land_refs = refs[:n], refs[n:2 * n]
        send_sems, recv_sems, own_sems = refs[2 * n], refs[2 * n + 1], refs[2 * n + 2]
        x, y, c = _my_pos()
        me = 4 * x + 2 * y + c
        for a in range(n):
            pltpu.make_async_copy(src_refs[a] if whole else src_refs[a].at[me], _slot(land_refs[a], me),
                                  own_sems.at[a]).wait()
            for m in range(1, N_DEV):
                peer, pid = _peer(m)
                cp = pltpu.make_async_remote_copy(
                    src_ref=src_refs[a] if whole else src_refs[a].at[pid], dst_ref=_slot(land_refs[a], pid),
                    send_sem=send_sems.at[a * (N_DEV - 1) + m - 1], recv_sem=recv_sems.at[a * (N_DEV - 1) + m - 1],
                    device_id=peer, device_id_type=MESH)
                cp.wait_send()
                cp.wait_recv()

    thru = [pltpu.HBM(s.shape, s.dtype) for s in list(srcs_thru) + list(lands_thru)]
    res = pl.pallas_call(
        body, name=name, out_shape=tuple(thru),
        in_specs=tuple([HBM_SPEC] * (2 * n) + [SEM_SPEC, SEM_SPEC, SEM_SPEC, ANY]),
        out_specs=tuple([HBM_SPEC] * (2 * n)),
        input_output_aliases={i: i for i in range(2 * n)},
        compiler_params=pltpu.CompilerParams(has_side_effects=EFFECT),
    )(*srcs_thru, *lands_thru, sems[0], sems[1], sems[2], after)
    return list(res[n:])


def _adamw(recv, w, m, v, layer=0, prev=None):
    L, R, C = w.shape
    tr = _tile(R, max(8, (1 << 18) // C), 8) if R % 8 == 0 else R
    bc1 = 1.0 - ADAM_B1 ** ADAM_STEP
    bc2 = 1.0 - ADAM_B2 ** ADAM_STEP
    if prev is None:
        prev = [lax.empty((L, R, C), F32) for _ in range(4)]

    def body(r_ref, w_ref, m_ref, v_ref, p0, p1, p2, p3, g_ref, d_ref, nm_ref, nv_ref):
        g = r_ref[0].astype(F32)
        for s in range(1, N_DEV):
            g = g + r_ref[s].astype(F32)
        nm = ADAM_B1 * m_ref[...] + (1.0 - ADAM_B1) * g
        nv = ADAM_B2 * v_ref[...] + (1.0 - ADAM_B2) * (g * g)
        mh = nm / bc1
        vh = nv / bc2
        g_ref[...] = g
        d_ref[...] = -ADAM_LR * (mh / (jnp.sqrt(vh) + ADAM_EPS) + ADAM_WD * w_ref[...])
        nm_ref[...] = nm
        nv_ref[...] = nv

    row = pl.BlockSpec((None, tr, C), lambda i: (layer, i, 0))
    return pl.pallas_call(
        body, grid=(R // tr,),
        in_specs=[pl.BlockSpec((N_DEV, tr, C), lambda i: (0, i, 0)), row, row, row] + [ANY] * 4,
        out_specs=[row] * 4, out_shape=[jax.ShapeDtypeStruct((L, R, C), F32)] * 4,
        input_output_aliases={4: 0, 5: 1, 6: 2, 7: 3},
        compiler_params=_cp("parallel"), name="adamw")(recv, w, m, v, *prev)


def _full_w_spec(tk, tn):
    return pl.BlockSpec((tk, tn), lambda i, j, k: (k, j))


def _colblk_w_spec(n):
    def spec(tk, tn):
        per = n // tn
        return pl.BlockSpec((None, tk, tn), lambda i, j, k: (j // per, k, j % per))
    return spec


def kernel(x, meta_tokens, mix_norm_g, mlp_norm_g, final_norm_g, ev_w_in, ev_conv_w, ev_conv_b, ev_ln_g, ev_ln_b, ev_pool_w, ev_pool_b, ev_pool_scale, ev_w_out, od_w_in, od_gnorm_g, od_w_out, lb_param, mlp_w1, mlp_w2, loss_target, m_meta_tokens, m_mix_norm_g, m_mlp_norm_g, m_final_norm_g, m_ev_w_in, m_ev_conv_w, m_ev_conv_b, m_ev_ln_g, m_ev_ln_b, m_ev_pool_w, m_ev_pool_b, m_ev_pool_scale, m_ev_w_out, m_od_w_in, m_od_gnorm_g, m_od_w_out, m_lb_param, m_mlp_w1, m_mlp_w2, v_meta_tokens, v_mix_norm_g, v_mlp_norm_g, v_final_norm_g, v_ev_w_in, v_ev_conv_w, v_ev_conv_b, v_ev_ln_g, v_ev_ln_b, v_ev_pool_w, v_ev_pool_b, v_ev_pool_scale, v_ev_w_out, v_od_w_in, v_od_gnorm_g, v_od_w_out, v_lb_param, v_mlp_w1, v_mlp_w2):
    S, D = x.shape[1], x.shape[2]
    T = PAD + N_META + S
    DEPTH = mix_norm_g.shape[0]
    DFF = mlp_w1.shape[2] * N_DEV
    dev = 4 * lax.axis_index("x") + 2 * lax.axis_index("y") + lax.axis_index("c")

    g_meta, g_cw = _all_gather([meta_tokens[None], ev_conv_w])
    n_ev = ev_w_in.shape[0]
    n_od = od_w_in.shape[0]
    meta_full = jnp.transpose(g_meta[0], (1, 0, 2)).reshape(N_META, D)
    cw_full = jnp.transpose(g_cw, (0, 2, 1, 3)).reshape(n_ev, CONV_WIDTH, -1)
    cw_pad = jnp.pad(cw_full, ((0, 0), (0, 32 - CONV_WIDTH), (0, 0)))
    n_in_od = od_w_in.shape[2]
    n_w1 = mlp_w1.shape[2]

    ag_src, ag_grp, ag_axis, ag_at = [], [], [], {}
    for layer in range(DEPTH):
        j = layer // 2
        mixer = [("in", ev_w_in[j]), ("out", ev_w_out[j])] if layer % 2 == 0 else [("in", od_w_in[j]), ("out", od_w_out[j])]
        for pos, (key, arr) in enumerate(mixer + [("w1", mlp_w1[layer]), ("w2", mlp_w2[layer])]):
            ag_at[layer, key] = len(ag_src)
            ag_src.append(arr.astype(BF16))
            ag_grp.append(len(ag_grp))
            ag_axis.append(1 if key in ("in", "w1") and arr.shape[1] % 128 == 0 else 0)
    ag_src, g_meta, g_cw = lax.optimization_barrier((ag_src, g_meta, g_cw))
    ag_sems, ag_s, ag_l, ag_tok = _push_start(
        "ag_start", ag_src, [_landing(s_.shape, s_.dtype, ax) for s_, ax in zip(ag_src, ag_axis)], True, ag_grp)

    def ag_wait(layer, key, after):
        a = ag_at[layer, key]
        return _push_wait(f"ag_wait_{a}", [ag_s[a]], [ag_l[a]], ag_sems[a], after, True)[0]

    h = jnp.concatenate([jnp.zeros((PAD, D), F32), meta_full, x[0]], axis=0) + ag_tok[0, 0]
    tgt = jnp.pad(loss_target[0], ((PAD + N_META, 0), (0, 0)))
    lb_all = _lb_fwd(lb_param)

    tm_big = _tile(T, MM_ROWS_BIG, 16)
    tm_mid = _tile(T, MM_ROWS_MID, 16)
    tm_k4 = _tile(T, MM_ROWS_K4, 16)

    saved = []
    for layer in range(DEPTH):
        j = layer // 2
        sv = {"h0": h}
        g_in = ag_wait(layer, "in", h)
        w_in = g_in if g_in.ndim == 2 else jnp.transpose(g_in, (1, 0, 2)).reshape(D, -1)
        if layer % 2 == 0:
            sv["n"], u = _mm_rms_nn("ev_in", h, mix_norm_g[layer][None], w_in, tm_mid, 512, "f32")
            yab, yc = _ev_fwd(u, cw_pad[j], ev_conv_b[j][None], ev_ln_g[j][None], ev_ln_b[j][None],
                              ev_pool_w[j].astype(BF16), ev_pool_b[j].reshape(1, -1), ev_pool_scale[j][None])
            sv.update(u=u, y=yab, yc=yc)
            w_out = ag_wait(layer, "out", yab).reshape(-1, D)
            h = _mm_nn("ev_out", yab, w_out, _full_w_spec, T, D, D, tm_mid, D, D, "resid", extra=h)
        else:
            sv["n"], u = _mm_rms_nn("od_in", h, mix_norm_g[layer][None], w_in, tm_mid, 1024, "f32")
            y, o, s0 = _hgrn_fwd(u, lb_all[layer][None], od_gnorm_g[j][None])
            sv.update(u=u, y=y, o=o, s0=s0)
            w_out = ag_wait(layer, "out", y).reshape(-1, D)
            h = _mm_nn("od_out", y, w_out, _full_w_spec, T, D, D, tm_mid, D, D, "resid", extra=h)
        sv["h1"] = h
        w_w1 = ag_wait(layer, "w1", h)
        n2, r, act = _mm_rms_nn("mlp_w1", h, mlp_norm_g[layer][None], w_w1, tm_mid, 1024, "relu2")
        w_w2 = ag_wait(layer, "w2", act).reshape(DFF, D)
        sv.update(w_in=w_in, w_out=w_out, w_w1=w_w1, w_w2=w_w2)
        sv.update(n2=n2, r=r, act=act)
        h = _mm_nn("mlp_w2", act, w_w2, _full_w_spec, T, D, DFF, tm_k4, D, DFF, "resid", extra=h)
        saved.append(sv)

    loss_blk, dh, dhb, dg_final = _loss_head(h, final_norm_g[None], tgt)
    loss = lax.psum(loss_blk[0, 0], AXES)

    tt = T
    g_mix, g_mlp = [None] * DEPTH, [None] * DEPTH
    small ={"cw": [None] * n_ev, "vec": [None] * n_ev, "pw": [None] * n_ev, "gn": [None] * n_od}
    dlb_rows = [jnp.zeros((1, D), F32) for _ in range(DEPTH)]

    def xs2(tt_, tk):
        return pl.BlockSpec((tt_, tk), lambda a, b, t: (t, a))

    def ys2(tt_, tn):
        return pl.BlockSpec((tt_, tn), lambda a, b, t: (t, b))

    def os2(tk, tn):
        return pl.BlockSpec((tk, tn), lambda a, b, t: (a, b))

    def os3(tk, tn):
        return pl.BlockSpec((None, tk, tn), lambda a, b, t: (b, a, 0))

    def dy2(tm, tn):
        return pl.BlockSpec((tm, tn), lambda i, jj, k: (i, k))

    def w_rows(tj, tn):
        return pl.BlockSpec((tj, tn), lambda i, jj, k: (jj, k))

    def w_colblk(tj, tn):
        return pl.BlockSpec((None, tj, tn), lambda i, jj, k: (k, jj, 0))

    rs_pending = []

    def rs_start(tag, mats):
        blocks = [m_ if m_.ndim == 3 else m_.reshape(N_DEV, m_.shape[0] // N_DEV, m_.shape[1]) for m_ in mats]
        lands = [_landing(b_.shape[1:], b_.dtype) for b_ in blocks]
        sems, s_thru, l_thru, tok = _push_start(f"rs_start_{tag}", blocks, lands, False, [0] * len(blocks))
        rs_pending.append((tag, s_thru, l_thru, sems[0]))
        return tok[0, 0]

    for layer in reversed(range(DEPTH)):
        j = layer // 2
        sv = saved[layer]
        da1 = _mm_nt("mlp_w2_t", dhb, sv["w_w2"], dy2, w_rows, T, DFF, D, tm_mid, 1024, D, "dact", extra=sv["r"])
        dw2 = _mm_tn("mlp_dw2", sv["act"], dhb, xs2, ys2, os2, (DFF, D), T, DFF, D, tt, 512, D)
        dw1 = _mm_tn("mlp_dw1", sv["n2"], da1, xs2, ys2, os3, (N_DEV, D, n_w1), T, D, DFF, tt, D, n_w1)
        tok = rs_start(f"mlp{layer}", [dw1, dw2])
        dh, dhb, g_mlp[layer] = _mm_nt("mlp_w1_t", da1, sv["w_w1"], dy2, w_rows, T, D, DFF, tm_k4, D, DFF, "rms",
                                       extra=(sv["h1"], mlp_norm_g[layer][None] + tok, dh))
        if layer % 2 == 0:
            dyab = _mm_nt("ev_out_t", dhb, sv["w_out"], dy2, w_rows, T, D, D, tm_mid, D, D, "f32")
            dwout = _mm_tn("ev_dwout", sv["y"], dhb, xs2, ys2, os2, (D, D), T, D, D, tt, 512, D)
            du, small["cw"][j], small["vec"][j], small["pw"][j] = _ev_bwd(
                dyab, sv["yc"], sv["u"], cw_pad[j], ev_ln_g[j][None], ev_ln_b[j][None], ev_pool_w[j].astype(BF16),
                jnp.transpose(ev_pool_w[j], (0, 2, 1)).astype(BF16), ev_pool_b[j].reshape(1, -1),
                ev_pool_scale[j][None])
            nin = du.shape[1]
            dwin = _mm_tn("ev_dwin", sv["n"], du, xs2, ys2, os2, (D, nin), T, D, nin, tt, D, 512)
            dwin = jnp.transpose(dwin.reshape(D, N_DEV, nin // N_DEV), (1, 0, 2))
            tok = rs_start(f"mix{layer}", [dwin, dwout])
            dh, dhb, g_mix[layer] = _mm_nt("ev_in_t", du, sv["w_in"], dy2, w_rows, T, D, nin, tm_k4, D, nin, "rms",
                                           extra=(sv["h0"], mix_norm_g[layer][None] + tok, dh))
        else:
            dy = _mm_nt("od_out_t", dhb, sv["w_out"], dy2, w_rows, T, D, D, tm_mid, D, D, "f32")
            dwout = _mm_tn("od_dwout", sv["y"], dhb, xs2, ys2, os2, (D, D), T, D, D, tt, 512, D)
            du3, dlb_rows[layer], small["gn"][j] = _hgrn_bwd(dy, sv["o"], sv["s0"], sv["u"], lb_all[layer][None],
                                                              od_gnorm_g[j][None])
            per = D // n_in_od

            def du_t(tt_, tn):
                return pl.BlockSpec((None, tt_, tn), lambda a, b, t: (b // per, t, b % per))

            dwin = _mm_tn("od_dwin", sv["n"], du3, xs2, du_t, os3, (N_DEV, D, n_in_od), T, D, 4 * D, tt, D, n_in_od)
            tok = rs_start(f"mix{layer}", [dwin, dwout])
            dh, dhb, g_mix[layer] = _mm_nt(
                "od_in_t", du3, sv["w_in"], lambda tm, tn: pl.BlockSpec((4, tm, tn // 4), lambda i, jj, k: (0, i, 0)),
                w_rows, T, D, 4 * D, tm_k4, D, 4 * D, "rms", extra=(sv["h0"], mix_norm_g[layer][None] + tok, dh),
                parts=4)

    dmeta = dh[PAD:PAD + N_META]
    grad_x = dh[PAD + N_META:][None]
    dlb_param = _lb_bwd(lb_param, jnp.concatenate(dlb_rows, axis=0))

    pieces = [
        ("final", dg_final), ("pad", jnp.zeros((SMALL_F32_ROWS - 1, D), F32)),
        ("meta", dmeta), ("mix", jnp.concatenate(g_mix, 0)), ("mlp", jnp.concatenate(g_mlp, 0)),
        ("cw", jnp.stack([c[:CONV_WIDTH] for c in small["cw"]])), ("cb", jnp.stack([v_[0] for v_ in small["vec"]])),
        ("lng", jnp.stack([v_[1] for v_ in small["vec"]])), ("lnb", jnp.stack([v_[2] for v_ in small["vec"]])),
        ("pw", jnp.stack(small["pw"])), ("pb", jnp.stack([v_[4] for v_ in small["vec"]])),
        ("ps", jnp.stack([v_[3] for v_ in small["vec"]])), ("gn", jnp.stack([jnp.sum(g_, axis=0)[0] for g_ in small["gn"]])),
        ("lb", dlb_param),
    ]
    flat = jnp.concatenate([p.reshape(-1) for _, p in pieces])
    n_small = flat.shape[0]
    rows_small = SMALL_F32_ROWS + -(-(n_small // 1024 + 1 - SMALL_F32_ROWS) // 16) * 16
    flat = jnp.pad(flat, (0, rows_small * 1024 - n_small)).reshape(rows_small, 1024)

    to_all = lambda a_: jnp.broadcast_to(a_[None], (N_DEV,) + a_.shape)
    got_f32, got_bf16 = _exchange([to_all(flat[:SMALL_F32_ROWS]), to_all(flat[SMALL_F32_ROWS:].astype(BF16))],
                                  [(0, 0), (1, 0)])
    recv_small = jnp.concatenate([got_f32, got_bf16.astype(F32)], axis=2)
    recv = {}
    for tag, s_thru, l_thru, sems in rs_pending:
        got = _push_wait(f"rs_wait_{tag}", s_thru, l_thru, sems, recv_small, False)
        layer = int(tag[3:])
        if tag.startswith("mlp"):
            recv["w1", layer], recv["w2", layer] = got
        else:
            key = "ev" if layer % 2 == 0 else "od"
            recv[key + "_in", layer // 2], recv[key + "_out", layer // 2] = got

    outs = {}
    big = {"ev_in": ("ev_w_in", ev_w_in, m_ev_w_in, v_ev_w_in), "ev_out": ("ev_w_out", ev_w_out, m_ev_w_out, v_ev_w_out),
           "od_in": ("od_w_in", od_w_in, m_od_w_in, v_od_w_in), "od_out": ("od_w_out", od_w_out, m_od_w_out, v_od_w_out),
           "w1": ("mlp_w1", mlp_w1, m_mlp_w1, v_mlp_w1), "w2": ("mlp_w2", mlp_w2, m_mlp_w2, v_mlp_w2)}
    for key, (name, w, m, v) in big.items():
        res = None
        for l in range(w.shape[0]):
            res = _adamw(recv[key, l], w, m, v, layer=l, prev=res)
        outs[name] = res

    small_params = {
        "meta": ("meta_tokens", None), "mix": ("mix_norm_g", mix_norm_g, m_mix_norm_g, v_mix_norm_g),
        "mlp": ("mlp_norm_g", mlp_norm_g, m_mlp_norm_g, v_mlp_norm_g),
        "final": ("final_norm_g", final_norm_g, m_final_norm_g, v_final_norm_g),
        "cw": ("ev_conv_w", None), "cb": ("ev_conv_b", ev_conv_b, m_ev_conv_b, v_ev_conv_b),
        "lng": ("ev_ln_g", ev_ln_g, m_ev_ln_g, v_ev_ln_g), "lnb": ("ev_ln_b", ev_ln_b, m_ev_ln_b, v_ev_ln_b),
        "pw": ("ev_pool_w", ev_pool_w, m_ev_pool_w, v_ev_pool_w), "pb": ("ev_pool_b", ev_pool_b, m_ev_pool_b, v_ev_pool_b),
        "ps": ("ev_pool_scale", ev_pool_scale, m_ev_pool_scale, v_ev_pool_scale),
        "gn": ("od_gnorm_g", od_gnorm_g, m_od_gnorm_g, v_od_gnorm_g), "lb": ("lb_param", lb_param, m_lb_param, v_lb_param),
    }
    csh = ev_conv_w.shape[2]
    msh = meta_tokens.shape[1]

    def packed(which):
        parts = []
        for key, g_ in pieces:
            ent = small_params.get(key)
            if key == "pad":
                full = g_
            elif key == "meta":
                src = (meta_tokens, m_meta_tokens, v_meta_tokens)[which]
                full = lax.dynamic_update_slice(jnp.zeros((N_META, D), F32), src, (0, dev * msh))
            elif key == "cw":
                src = (ev_conv_w, m_ev_conv_w, v_ev_conv_w)[which]
                full = lax.dynamic_update_slice(jnp.zeros(g_.shape, F32), src, (0, 0, dev * csh))
            else:
                full = ent[1 + which]
            parts.append(full.reshape(-1))
        f = jnp.concatenate(parts)
        return jnp.pad(f, (0, rows_small * 1024 - n_small)).reshape(rows_small, 1024)

    sres = [r_[0] for r_ in _adamw(recv_small[:, 0], packed(0)[None], packed(1)[None], packed(2)[None])]
    off = 0
    for key, g_ in pieces:
        size = g_.size
        vals = [r_.reshape(-1)[off:off + size].reshape(g_.shape) for r_ in sres]
        off += size
        if key == "pad":
            continue
        name = small_params[key][0]
        if key == "meta":
            vals = [lax.dynamic_slice(v_, (0, dev * msh), (N_META, msh)) for v_ in vals]
        elif key == "cw":
            vals = [lax.dynamic_slice(v_, (0, 0, dev * csh), v_.shape[:2] + (csh,)) for v_ in vals]
        else:
            vals = [v_.reshape(small_params[key][1].shape) for v_ in vals]
        outs[name] = vals

    names = ["meta_tokens", "mix_norm_g", "mlp_norm_g", "final_norm_g", "ev_w_in", "ev_conv_w", "ev_conv_b", "ev_ln_g",
             "ev_ln_b", "ev_pool_w", "ev_pool_b", "ev_pool_scale", "ev_w_out", "od_w_in", "od_gnorm_g", "od_w_out",
             "lb_param", "mlp_w1", "mlp_w2"]
    result = [loss, grad_x]
    for k in range(4):
        result += [outs[nm][k] for nm in names]
    return tuple(result)
```

```python
import functools

import jax
import jax.numpy as jnp
from jax import lax
from jax.experimental import pallas as pl
from jax.experimental.pallas import tpu as pltpu

F32 = jnp.float32
BF16 = jnp.bfloat16

N_DEV = 8
N_META = 16
CHUNK = 64
PAD = CHUNK - N_META
SUB = 16
HEAD = 128
CONV_WIDTH = 31
HALO = 32
POOL_WINDOWS = (2, 4, 8, 16)
EPS = 1e-6
NEG = -1e30
ADAM_LR, ADAM_B1, ADAM_B2, ADAM_EPS, ADAM_WD, ADAM_STEP = 0.001, 0.9, 0.999, 1e-08, 0.01, 10
VMEM_LIMIT = 56 * 1024 * 1024
EV_ROWS = 416
HGRN_ROWS = 832
MM_ROWS_BIG = 2080
MM_ROWS_MID = 1040
MM_ROWS_K4 = 832
SMALL_F32_ROWS = 8
HGRN_HEADS_FWD = 4
HGRN_HEADS_BWD = 2
MESH = pl.DeviceIdType.MESH
AXES = ("x", "y", "c")
ANY = pl.BlockSpec(memory_space=pl.ANY)


def _cp(*sem):
    return pltpu.CompilerParams(dimension_semantics=sem, vmem_limit_bytes=VMEM_LIMIT)


def _tile(n, cap, mult):
    best = None
    for d in range(mult, min(n, cap) + 1, mult):
        if n % d == 0:
            best = d
    assert best is not None, (n, cap, mult)
    return best


def _nt(a, b):
    return lax.dot_general(a, b, (((1,), (1,)), ((), ())), preferred_element_type=F32)


def _tn(a, b):
    return lax.dot_general(a, b, (((0,), (0,)), ((), ())), preferred_element_type=F32)


def _nn(a, b):
    return jnp.dot(a, b, preferred_element_type=F32)


def _r16(x):
    return x.astype(BF16).astype(F32)


def _row_ids(base, n):
    return base + lax.broadcasted_iota(jnp.int32, (n, 1), 0)


def _dsilu(x, s):
    return s * (1.0 + x * (1.0 - s))


def _rms_fwd(h, g):
    T, D = h.shape
    tm = _tile(T, MM_ROWS_MID, 16)

    def body(h_ref, g_ref, n_ref):
        x = h_ref[...]
        r = lax.rsqrt(jnp.mean(x * x, axis=-1, keepdims=True) + EPS)
        n_ref[...] = ((x * r) * g_ref[...]).astype(BF16)

    return pl.pallas_call(
        body, grid=(T // tm,),
        in_specs=[pl.BlockSpec((tm, D), lambda i: (i, 0)), pl.BlockSpec((1, D), lambda i: (0, 0))],
        out_specs=pl.BlockSpec((tm, D), lambda i: (i, 0)),
        out_shape=jax.ShapeDtypeStruct((T, D), BF16),
        compiler_params=_cp("parallel"), name="rms_fwd")(h, g)


def _rms_bwd(dn, h, g, dres):
    T, D = h.shape
    tm = _tile(T, MM_ROWS_MID, 16)

    def body(dn_ref, h_ref, g_ref, dres_ref, dh_ref, dhb_ref, dg_ref):
        i = pl.program_id(0)
        x = h_ref[...]
        dn_v = dn_ref[...]
        r = lax.rsqrt(jnp.mean(x * x, axis=-1, keepdims=True) + EPS)
        xh = x * r
        dxh = dn_v * g_ref[...]
        dx = r * (dxh - xh * jnp.mean(dxh * xh, axis=-1, keepdims=True))
        keep = _row_ids(i * tm, tm) >= PAD
        dh = jnp.where(keep, dres_ref[...] + dx, 0.0)
        dh_ref[...] = dh
        dhb_ref[...] = dh.astype(BF16)

        @pl.when(i == 0)
        def _():
            dg_ref[...] = jnp.zeros_like(dg_ref)

        dg_ref[...] += jnp.sum(dn_v * xh, axis=0, keepdims=True)

    row = pl.BlockSpec((tm, D), lambda i: (i, 0))
    vec = pl.BlockSpec((1, D), lambda i: (0, 0))
    return pl.pallas_call(
        body, grid=(T // tm,),
        in_specs=[row, row, vec, row], out_specs=[row, row, vec],
        out_shape=[jax.ShapeDtypeStruct((T, D), F32), jax.ShapeDtypeStruct((T, D), BF16),
                   jax.ShapeDtypeStruct((1, D), F32)],
        compiler_params=_cp("arbitrary"), name="rms_bwd")(dn, h, g, dres)


def _loss_head(h, g, tgt):
    T, D = h.shape
    tm = _tile(T, MM_ROWS_MID, 16)
    first_x = PAD + N_META

    def body(h_ref, g_ref, t_ref, loss_ref, dh_ref, dhb_ref, dg_ref):
        i = pl.program_id(0)
        x = h_ref[...]
        r = lax.rsqrt(jnp.mean(x * x, axis=-1, keepdims=True) + EPS)
        xh = x * r
        gv = g_ref[...]
        out = xh * gv
        valid = _row_ids(i * tm, tm) >= first_x
        e = jnp.where(valid, out - t_ref[...], 0.0)
        dout = e * (1.0 / D)
        dxh = dout * gv
        dx = r * (dxh - xh * jnp.mean(dxh * xh, axis=-1, keepdims=True))
        dh_ref[...] = dx
        dhb_ref[...] = dx.astype(BF16)

        @pl.when(i == 0)
        def _():
            dg_ref[...] = jnp.zeros_like(dg_ref)
            loss_ref[...] = jnp.zeros_like(loss_ref)

        dg_ref[...] += jnp.sum(dout * xh, axis=0, keepdims=True)
        loss_ref[...] += 0.5 * jnp.sum(jnp.mean(e * e, axis=-1, keepdims=True))

    row = pl.BlockSpec((tm, D), lambda i: (i, 0))
    vec = pl.BlockSpec((1, D), lambda i: (0, 0))
    return pl.pallas_call(
        body, grid=(T // tm,),
        in_specs=[row, vec, row],
        out_specs=[pl.BlockSpec((8, 128), lambda i: (0, 0)), row, row, vec],
        out_shape=[jax.ShapeDtypeStruct((8, 128), F32), jax.ShapeDtypeStruct((T, D), F32),
                   jax.ShapeDtypeStruct((T, D), BF16), jax.ShapeDtypeStruct((1, D), F32)],
        compiler_params=_cp("arbitrary"), name="loss_head")(h, g, tgt)


def _mm_nn(name, a, w, w_spec, M, N, K, tm, tn, tk, mode, extra=None, a_spec=None):
    nk = K // tk
    if a_spec is None:
        a_spec = pl.BlockSpec((tm, tk), lambda i, j, k: (i, k))
    o_spec = pl.BlockSpec((tm, tn), lambda i, j, k: (i, j))

    def body(*refs):
        if mode == "resid":
            a_ref, w_ref, e_ref = refs[:3]
            outs = refs[3:]
        else:
            a_ref, w_ref = refs[:2]
            outs = refs[2:]
        acc_ref = outs[-1] if nk > 1 else None
        part = _nn(a_ref[...], w_ref[...])

        def finish(acc):
            if mode == "f32":
                outs[0][...] = acc
            elif mode == "relu2":
                r = jnp.maximum(acc, 0.0)
                outs[0][...] = r.astype(BF16)
                outs[1][...] = (r * r).astype(BF16)
            else:
                keep = _row_ids(pl.program_id(0) * tm, tm) >= PAD
                outs[0][...] = jnp.where(keep, e_ref[...] + acc, 0.0)

        if nk == 1:
            finish(part)
        else:
            k = pl.program_id(2)

            @pl.when(k == 0)
            def _():
                acc_ref[...] = part

            @pl.when(k > 0)
            def _():
                acc_ref[...] += part

            @pl.when(k == nk - 1)
            def _():
                finish(acc_ref[...])

    in_specs = [a_spec, w_spec(tk, tn)]
    args = [a, w]
    if mode == "resid":
        in_specs.append(o_spec)
        args.append(extra)
    if mode == "relu2":
        out_specs = [o_spec, o_spec]
        out_shape = [jax.ShapeDtypeStruct((M, N), BF16)] * 2
    else:
        out_specs = [o_spec]
        out_shape = [jax.ShapeDtypeStruct((M, N), F32)]
    scratch = [pltpu.VMEM((tm, tn), F32)] if nk > 1 else []
    res = pl.pallas_call(
        body, grid=(M // tm, N // tn, nk), in_specs=in_specs, out_specs=out_specs, out_shape=out_shape,
        scratch_shapes=scratch, compiler_params=_cp("parallel", "parallel", "arbitrary"), name=name)(*args)
    return res if mode == "relu2" else res[0]


def _mm_rms_nn(name, h, g, w, tm, tn, mode):
    M, K = h.shape
    N = w.shape[1]

    def body(h_ref, g_ref, w_ref, n_ref, *outs):
        @pl.when(pl.program_id(1) == 0)
        def _():
            ch = _tile(tm, 256, 16)

            def chunk(c, carry):
                rows = pl.ds(pl.multiple_of(c * ch, ch), ch)
                x = h_ref[rows, :]
                r = lax.rsqrt(jnp.mean(x * x, axis=-1, keepdims=True) + EPS)
                n_ref[rows, :] = ((x * r) * g_ref[...]).astype(BF16)
                return carry

            lax.fori_loop(0, tm // ch, chunk, 0)

        acc = _nn(n_ref[...], w_ref[...])
        if mode == "f32":
            outs[0][...] = acc
        else:
            r = jnp.maximum(acc, 0.0)
            outs[0][...] = r.astype(BF16)
            outs[1][...] = (r * r).astype(BF16)

    row = pl.BlockSpec((tm, K), lambda i, j: (i, 0))
    o_spec = pl.BlockSpec((tm, tn), lambda i, j: (i, j))
    n_out = 1 if mode == "f32" else 2
    return pl.pallas_call(
        body, grid=(M // tm, N // tn),
        in_specs=[row, pl.BlockSpec((1, K), lambda i, j: (0, 0)), pl.BlockSpec((K, tn), lambda i, j: (0, j))],
        out_specs=[row] + [o_spec] * n_out,
        out_shape=[jax.ShapeDtypeStruct((M, K), BF16)] + [jax.ShapeDtypeStruct((M, N), F32 if mode == "f32" else BF16)] * n_out,
        compiler_params=_cp("parallel", "arbitrary"), name=name)(h, g, w)


def _mm_nt(name, dy, w, dy_spec, w_spec, M, J, N, tm, tj, tn, mode, extra=None, parts=1):
    nk = N // tn
    o_spec = pl.BlockSpec((tm, tj), lambda i, j, k: (i, j))
    n_extra = {"f32": 0, "dact": 1, "rms": 3}[mode]
    if mode == "rms":
        assert nk == 1 and tj == J

    def body(*refs):
        dy_ref, w_ref = refs[:2]
        ex = refs[2:2 + n_extra]
        outs = refs[2 + n_extra:]
        acc_ref = outs[-1] if nk > 1 else None
        if parts == 1:
            part = _nt(dy_ref[...], w_ref[...])
        else:
            wq = tn // parts
            part = _nt(dy_ref[0], w_ref[:, 0:wq])
            for q in range(1, parts):
                part = part + _nt(dy_ref[q], w_ref[:, q * wq:(q + 1) * wq])

        def finish(acc):
            if mode == "f32":
                outs[0][...] = acc
            elif mode == "dact":
                outs[0][...] = (acc * (2.0 * ex[0][...].astype(F32))).astype(BF16)
            else:
                h_ref, g_ref, dres_ref = ex
                dh_ref, dhb_ref, dg_ref = outs[:3]
                i = pl.program_id(0)

                @pl.when(i == 0)
                def _():
                    dg_ref[...] = jnp.zeros_like(dg_ref)

                ch = _tile(tm, 256, 16)
                for c0 in range(0, tm, ch):
                    a_c = acc[c0:c0 + ch]
                    x = h_ref[c0:c0 + ch, :]
                    r = lax.rsqrt(jnp.mean(x * x, axis=-1, keepdims=True) + EPS)
                    xh = x * r
                    dxh = a_c * g_ref[...]
                    dx = r * (dxh - xh * jnp.mean(dxh * xh, axis=-1, keepdims=True))
                    keep = _row_ids(i * tm + c0, ch) >= PAD
                    dh = jnp.where(keep, dres_ref[c0:c0 + ch, :] + dx, 0.0)
                    dh_ref[c0:c0 + ch, :] = dh
                    dhb_ref[c0:c0 + ch, :] = dh.astype(BF16)
                    dg_ref[...] += jnp.sum(a_c * xh, axis=0, keepdims=True)

        if nk == 1:
            finish(part)
        else:
            k = pl.program_id(2)

            @pl.when(k == 0)
            def _():
                acc_ref[...] = part

            @pl.when(k > 0)
            def _():
                acc_ref[...] += part

            @pl.when(k == nk - 1)
            def _():
                finish(acc_ref[...])

    in_specs = [dy_spec(tm, tn), w_spec(tj, tn)]
    args = [dy, w]
    scratch = [pltpu.VMEM((tm, tj), F32)] if nk > 1 else []
    if mode == "rms":
        vec = pl.BlockSpec((1, J), lambda i, j, k: (0, 0))
        h, g, dres = extra
        res = pl.pallas_call(
            body, grid=(M // tm, 1, 1), in_specs=in_specs + [o_spec, vec, o_spec], out_specs=[o_spec, o_spec, vec],
            out_shape=[jax.ShapeDtypeStruct((M, J), F32), jax.ShapeDtypeStruct((M, J), BF16),
                       jax.ShapeDtypeStruct((1, J), F32)],
            compiler_params=_cp("arbitrary", "arbitrary", "arbitrary"), name=name)(*args, h, g, dres)
        return res
    if mode == "dact":
        in_specs.append(o_spec)
        args.append(extra)
    return pl.pallas_call(
        body, grid=(M // tm, J // tj, nk), in_specs=in_specs, out_specs=[o_spec],
        out_shape=[jax.ShapeDtypeStruct((M, J), BF16 if mode == "dact" else F32)],
        scratch_shapes=scratch, compiler_params=_cp("parallel", "parallel", "arbitrary"), name=name)(*args)[0]


def _mm_tn(name, x, dy, x_spec, dy_spec, o_spec, o_shape, T, K, N, tt, tk, tn):
    nt = T // tt

    def body(x_ref, dy_ref, o_ref, *acc):
        part = _tn(x_ref[...], dy_ref[...])
        if nt == 1:
            o_ref[...] = part.astype(BF16)
            return
        acc_ref = acc[0]
        t = pl.program_id(2)

        @pl.when(t == 0)
        def _():
            acc_ref[...] = part

        @pl.when(t > 0)
        def _():
            acc_ref[...] += part

        @pl.when(t == nt - 1)
        def _():
            o_ref[...] = acc_ref[...].astype(BF16)

    return pl.pallas_call(
        body, grid=(K // tk, N // tn, nt), in_specs=[x_spec(tt, tk), dy_spec(tt, tn)], out_specs=o_spec(tk, tn),
        out_shape=jax.ShapeDtypeStruct(o_shape, BF16), scratch_shapes=[pltpu.VMEM((tk, tn), F32)] if nt > 1 else [],
        compiler_params=_cp("parallel", "parallel", "arbitrary"), name=name)(x, dy)


def _pool_counts(base, n, w):
    pos = _row_ids(base, n) - PAD
    return jnp.clip(pos + 1, 1, w).astype(F32)


def _shifted_copies(buf, rows):
    buf[0, rows:rows + 8, :] = jnp.zeros((8, buf.shape[2]), F32)

    def blk(s, carry):
        b = pl.multiple_of(s * HALO, HALO)
        win = buf[0, pl.ds(b, HALO + 8), :]
        for r in range(1, 8):
            buf[r, pl.ds(b, HALO), :] = win[r:r + HALO]
        return carry

    lax.fori_loop(0, rows // HALO, blk, 0)


def _ev_fwd(u, cw, cb, lg, lb, pw, pb, ps):
    T = u.shape[0]
    C = 512
    tm = _tile(T, EV_ROWS, HALO)
    nsub = tm // HALO
    hb = tm // HALO

    def body(val_ref, gate_ref, pin_ref, valh_ref, gateh_ref, pinh_ref, cw_ref, cb_ref, lg_ref, lb_ref, pw_ref,
             pb_ref, ps_ref, yab_ref, yc_ref, a_ext, p_ext, d_buf):
        i = pl.program_id(0)
        nf = (i > 0).astype(F32)
        a_ext[0, 0:HALO, :] = valh_ref[...] * jax.nn.sigmoid(gateh_ref[...]) * nf
        a_ext[0, HALO:HALO + tm, :] = val_ref[...] * jax.nn.sigmoid(gate_ref[...])
        p_ext[0:HALO, :] = pinh_ref[...] * nf
        p_ext[HALO:, :] = pin_ref[...]
        _shifted_copies(a_ext, tm + HALO)

        def sub(s, carry):
            base = pl.multiple_of(s * HALO, HALO)
            acc = jnp.zeros((HALO, C), F32) + cb_ref[...]
            for j in range(CONV_WIDTH):
                off = 2 + j
                acc = acc + cw_ref[pl.ds(j, 1), :] * a_ext[off % 8, pl.ds(pl.multiple_of(base + off // 8 * 8, 8), HALO), :]
            yc_ref[pl.ds(base, HALO), :] = acc
            mu = jnp.mean(acc, axis=-1, keepdims=True)
            yc = acc - mu
            rstd = lax.rsqrt(jnp.mean(yc * yc, axis=-1, keepdims=True) + EPS)
            z = (yc * rstd) * lg_ref[...] + lb_ref[...]
            yab_ref[pl.ds(base, HALO), 0:C] = (z * jax.nn.sigmoid(z)).astype(BF16)
            pwin = p_ext[pl.ds(base, 2 * HALO), :]
            for gi, w in enumerate(POOL_WINDOWS):
                lo, hi = gi * HEAD, (gi + 1) * HEAD
                x = pwin[HALO:, lo:hi]
                tot = x
                for k in range(1, w):
                    tot = tot + pwin[HALO - k:2 * HALO - k, lo:hi]
                cnt = _pool_counts(i * tm + base, HALO, w)
                d_buf[pl.ds(base, HALO), lo:hi] = (tot / cnt - x).astype(BF16)
            return carry

        lax.fori_loop(0, nsub, sub, 0)
        for gi in range(len(POOL_WINDOWS)):
            lo, hi = gi * HEAD, (gi + 1) * HEAD
            y = _nn(d_buf[:, lo:hi], pw_ref[gi]) + pb_ref[:, lo:hi]
            yab_ref[:, C + lo:C + hi] = (y * ps_ref[:, lo:hi]).astype(BF16)

    def main(c):
        return pl.BlockSpec((tm, C), lambda i: (i, c))

    def halo(c):
        return pl.BlockSpec((HALO, C), lambda i: (jnp.maximum(i * hb - 1, 0), c))

    vec = pl.BlockSpec((1, C), lambda i: (0, 0))
    return pl.pallas_call(
        body, grid=(T // tm,),
        in_specs=[main(0), main(1), main(2), halo(0), halo(1), halo(2),
                  pl.BlockSpec((32, C), lambda i: (0, 0)), vec, vec, vec,
                  pl.BlockSpec((4, HEAD, HEAD), lambda i: (0, 0, 0)), vec, vec],
        out_specs=[pl.BlockSpec((tm, 2 * C), lambda i: (i, 0)), pl.BlockSpec((tm, C), lambda i: (i, 0))],
        out_shape=[jax.ShapeDtypeStruct((T, 2 * C), BF16), jax.ShapeDtypeStruct((T, C), F32)],
        scratch_shapes=[pltpu.VMEM((8, tm + HALO + 8, C), F32), pltpu.VMEM((tm + HALO, C), F32),
                        pltpu.VMEM((tm, C), BF16)],
        compiler_params=_cp("parallel"), name="ev_fwd")(u, u, u, u, u, u, cw, cb, lg, lb, pw, pb, ps)


def _ev_bwd(dyab, yc, u, cw, lg, lb, pw, pwt, pb, ps):
    T = u.shape[0]
    C = 512
    tm = _tile(T, EV_ROWS, HALO)
    nsub = tm // HALO
    hb = tm // HALO
    nblk = T // tm
    E = tm + HALO

    def body(dya_ref, dyb_ref, dyah_ref, dybh_ref, yc_ref, ych_ref, val_ref, gate_ref, pin_ref, valh_ref, gateh_ref,
             pinh_ref, cw_ref, lg_ref, lb_ref, pw_ref, pwt_ref, pb_ref, ps_ref,
             du_ref, dcw_ref, dvec_ref, dpw_ref,
             dy_ext, a_ext, p_ext, ddc_ext, dd_buf, d_buf, dpre_buf, dcw_acc, vec_acc):
        i = pl.program_id(0)
        nf = (i > 0).astype(F32)
        nl = (i < nblk - 1).astype(F32)

        @pl.when(i == 0)
        def _():
            dcw_ref[...] = jnp.zeros_like(dcw_ref)
            dvec_ref[...] = jnp.zeros_like(dvec_ref)
            dpw_ref[...] = jnp.zeros_like(dpw_ref)

        dcw_acc[...] = jnp.zeros_like(dcw_acc)
        vec_acc[...] = jnp.zeros_like(vec_acc)
        a_ext[0, 0:HALO, :] = valh_ref[...] * jax.nn.sigmoid(gateh_ref[...]) * nf
        a_ext[0, HALO:E, :] = val_ref[...] * jax.nn.sigmoid(gate_ref[...])
        p_ext[0:HALO, :] = pinh_ref[...] * nf
        p_ext[HALO:, :] = pin_ref[...]
        _shifted_copies(a_ext, E)

        def ln_bwd(y, dya, main):
            mu = jnp.mean(y, axis=-1, keepdims=True)
            ycen = y - mu
            rstd = lax.rsqrt(jnp.mean(ycen * ycen, axis=-1, keepdims=True) + EPS)
            yh = ycen * rstd
            z = yh * lg_ref[...] + lb_ref[...]
            sz = jax.nn.sigmoid(z)
            dz = dya * _dsilu(z, sz)
            dyh = dz * lg_ref[...]
            dy = rstd * (dyh - jnp.mean(dyh, axis=-1, keepdims=True) - yh * jnp.mean(dyh * yh, axis=-1, keepdims=True))
            if main:
                vec_acc[1] += jnp.sum((dz * yh).reshape(HALO // 8, 8, C), axis=0)
                vec_acc[2] += jnp.sum(dz.reshape(HALO // 8, 8, C), axis=0)
                vec_acc[0] += jnp.sum(dy.reshape(HALO // 8, 8, C), axis=0)
            return dy

        def pool_dd(dyb, base, main):
            dpre = dyb * ps_ref[...]
            for gi, w in enumerate(POOL_WINDOWS):
                lo, hi = gi * HEAD, (gi + 1) * HEAD
                dd = _nn(dpre[:, lo:hi].astype(BF16), pwt_ref[gi])
                cnt = _pool_counts(i * tm + base, HALO, w)
                ddc_ext[pl.ds(base, HALO), lo:hi] = dd / cnt
                if main:
                    dd_buf[pl.ds(base, HALO), lo:hi] = dd
            if main:
                dpre_buf[pl.ds(base, HALO), :] = dpre.astype(BF16)
                vec_acc[4] += jnp.sum(dpre.reshape(HALO // 8, 8, C), axis=0)

        def p1(s, carry):
            base = pl.multiple_of(s * HALO, HALO)
            dy_ext[0, pl.ds(base, HALO), :] = ln_bwd(yc_ref[pl.ds(base, HALO), :], dya_ref[pl.ds(base, HALO), :], True)
            pool_dd(dyb_ref[pl.ds(base, HALO), :], base, True)
            return carry

        lax.fori_loop(0, nsub, p1, 0)
        dy_ext[0, tm:E, :] = ln_bwd(ych_ref[...], dyah_ref[...], False) * nl
        _shifted_copies(dy_ext, E)
        dpre_h = dybh_ref[...] * ps_ref[...] * nl
        for gi, w in enumerate(POOL_WINDOWS):
            lo, hi = gi * HEAD, (gi + 1) * HEAD
            dd = _nn(dpre_h[:, lo:hi].astype(BF16), pwt_ref[gi])
            ddc_ext[tm:, lo:hi] = dd / _pool_counts(i * tm + tm, HALO, w)

        def p2(s, carry):
            base = pl.multiple_of(s * HALO, HALO)
            dy_m = dy_ext[0, pl.ds(base, HALO), :]
            da = jnp.zeros((HALO, C), F32)
            for j in range(CONV_WIDTH):
                sh = CONV_WIDTH - 1 - j
                off = 2 + j
                da = da + cw_ref[pl.ds(j, 1), :] * dy_ext[sh % 8, pl.ds(pl.multiple_of(base + sh // 8 * 8, 8), HALO), :]
                a_j = a_ext[off % 8, pl.ds(pl.multiple_of(base + off // 8 * 8, 8), HALO), :]
                dcw_acc[j] += jnp.sum((dy_m * a_j).reshape(HALO // 8, 8, C), axis=0)
            v = val_ref[pl.ds(base, HALO), :]
            g = gate_ref[pl.ds(base, HALO), :]
            sg = jax.nn.sigmoid(g)
            du_ref[pl.ds(base, HALO), 0:C] = (da * sg).astype(BF16)
            du_ref[pl.ds(base, HALO), C:2 * C] = (da * v * sg * (1.0 - sg)).astype(BF16)
            pwin = p_ext[pl.ds(base, 2 * HALO), :]
            cwin = ddc_ext[pl.ds(base, 2 * HALO), :]
            for gi, w in enumerate(POOL_WINDOWS):
                lo, hi = gi * HEAD, (gi + 1) * HEAD
                x = pwin[HALO:, lo:hi]
                tot = x
                back = cwin[0:HALO, lo:hi]
                for k in range(1, w):
                    tot = tot + pwin[HALO - k:2 * HALO - k, lo:hi]
                    back = back + cwin[k:k + HALO, lo:hi]
                cnt = _pool_counts(i * tm + base, HALO, w)
                d_buf[pl.ds(base, HALO), lo:hi] = (tot / cnt - x).astype(BF16)
                du_ref[pl.ds(base, HALO), 2 * C + lo:2 * C + hi] = (back - dd_buf[pl.ds(base, HALO), lo:hi]).astype(BF16)
            return carry

        lax.fori_loop(0, nsub, p2, 0)
        for gi in range(len(POOL_WINDOWS)):
            lo, hi = gi * HEAD, (gi + 1) * HEAD
            pre = _nn(d_buf[:, lo:hi], pw_ref[gi]) + pb_ref[:, lo:hi]
            vec_acc[3, :, lo:hi] += jnp.sum((dyb_ref[:, lo:hi] * pre).reshape(tm // 8, 8, HEAD), axis=0)
            dpw_ref[gi] += _tn(d_buf[:, lo:hi], dpre_buf[:, lo:hi])
        for j in range(CONV_WIDTH):
            dcw_ref[pl.ds(j, 1), :] += jnp.sum(dcw_acc[j], axis=0, keepdims=True)
        for r in range(5):
            dvec_ref[pl.ds(r, 1), :] += jnp.sum(vec_acc[r], axis=0, keepdims=True)

    def main(c, width=C):
        return pl.BlockSpec((tm, width), lambda i: (i, c))

    def prev(c):
        return pl.BlockSpec((HALO, C), lambda i: (jnp.maximum(i * hb - 1, 0), c))

    def nxt(c):
        return pl.BlockSpec((HALO, C), lambda i: (jnp.minimum((i + 1) * hb, T // HALO - 1), c))

    vec = pl.BlockSpec((1, C), lambda i: (0, 0))
    mat = pl.BlockSpec((4, HEAD, HEAD), lambda i: (0, 0, 0))
    return pl.pallas_call(
        body, grid=(nblk,),
        in_specs=[main(0), main(1), nxt(0), nxt(1), main(0), nxt(0), main(0), main(1), main(2), prev(0), prev(1),
                  prev(2), pl.BlockSpec((32, C), lambda i: (0, 0)), vec, vec, mat, mat, vec, vec],
        out_specs=[pl.BlockSpec((tm, 3 * C), lambda i: (i, 0)), pl.BlockSpec((32, C), lambda i: (0, 0)),
                   pl.BlockSpec((8, C), lambda i: (0, 0)), mat],
        out_shape=[jax.ShapeDtypeStruct((T, 3 * C), BF16), jax.ShapeDtypeStruct((32, C), F32),
                   jax.ShapeDtypeStruct((8, C), F32), jax.ShapeDtypeStruct((4, HEAD, HEAD), F32)],
        scratch_shapes=[pltpu.VMEM((8, E + 8, C), F32), pltpu.VMEM((8, E + 8, C), F32), pltpu.VMEM((E, C), F32),
                        pltpu.VMEM((E, C), F32), pltpu.VMEM((tm, C), F32), pltpu.VMEM((tm, C), BF16),
                        pltpu.VMEM((tm, C), BF16), pltpu.VMEM((32, 8, C), F32), pltpu.VMEM((8, 8, C), F32)],
        compiler_params=_cp("arbitrary"), name="ev_bwd")(
            dyab, dyab, dyab, dyab, yc, yc, u, u, u, u, u, u, cw, lg, lb, pw, pwt, pb, ps)


def _cumsum_rows(x, reverse=False):
    n = x.shape[0]
    rid = lax.broadcasted_iota(jnp.int32, (n, 1), 0)
    k = 1
    while k < n:
        if reverse:
            sh = jnp.where(rid < n - k, pltpu.roll(x, n - k, 0), 0.0)
        else:
            sh = jnp.where(rid >= k, pltpu.roll(x, k, 0), 0.0)
        x = x + sh
        k *= 2
    return x


def _hgrn_gates(qr, fr, lbv):
    sq = jax.nn.sigmoid(qr)
    sg = jax.nn.sigmoid(fr)
    fg = lbv + (1.0 - lbv) * sg
    return qr * sq, sq, sg, fg, 1.0 - fg, jnp.log(fg)


def _hgrn_fwd(u, lbv, gn):
    T = u.shape[0]
    H = 8
    RB = _tile(T, HGRN_ROWS, CHUNK)
    NC = RB // CHUNK
    NS = CHUNK // SUB

    HP = HGRN_HEADS_FWD
    W = HP * HEAD

    def body(q_ref, f_ref, i_ref, g_ref, lb_ref, gn_ref, y_ref, o_ref, s0_ref, st, qs, ks, bs, vs, os_):
        rb = pl.program_id(1)

        @pl.when(rb == 0)
        def _():
            st[...] = jnp.zeros_like(st)

        t8 = lax.broadcasted_iota(jnp.int32, (8, 1), 0)

        def head(hh, c, rows):
            sl = slice(hh * HEAD, (hh + 1) * HEAD)
            q, _, _, _, kk, lf = _hgrn_gates(q_ref[rows, sl], f_ref[rows, sl], lb_ref[:, sl])
            v = i_ref[rows, sl]
            b = _cumsum_rows(lf)
            qs[hh] = q
            ks[hh] = kk
            bs[hh] = b
            vs[hh] = v
            st0 = st[hh]
            s0_ref[hh, c] = st0
            os_[hh] = _nt((q * jnp.exp(b)).astype(BF16), st0.astype(BF16))
            for I in range(NS):
                lo = I * SUB
                qI = qs[hh, lo:lo + SUB, :]
                bI = bs[hh, lo:lo + SUB, :]
                oI = jnp.zeros((SUB, HEAD), F32)
                if I > 0:
                    bprev = bs[hh, pl.ds(lo - 1, 1), :]
                    qt = _r16(qI * jnp.exp(bI - bprev))
                    kt = _r16(ks[hh, 0:lo, :] * jnp.exp(bprev - bs[hh, 0:lo, :]))
                    A = _nt(qt, kt)
                    oI = oI + _nn(_r16(A), _r16(vs[hh, 0:lo, :]))
                od = [jnp.zeros((8, HEAD), F32) for _ in range(SUB // 8)]
                for s in range(SUB):
                    row = pl.ds(lo + s, 1)
                    brow, krow, vrow = bs[hh, row, :], ks[hh, row, :], vs[hh, row, :]
                    for ti in range(SUB // 8):
                        o8 = 8 * ti
                        if s > o8 + 7:
                            continue
                        d = bI[o8:o8 + 8] - brow
                        if s > o8:
                            d = jnp.where(t8 >= s - o8, d, NEG)
                        col = jnp.sum(qI[o8:o8 + 8] * jnp.exp(d) * krow, axis=1, keepdims=True)
                        od[ti] = od[ti] + col * vrow
                os_[hh, lo:lo + SUB, :] += oI + jnp.concatenate(od, axis=0)
            blast = bs[hh, pl.ds(CHUNK - 1, 1), :]
            kh = kk * jnp.exp(blast - b)
            st[hh] = st0 * jnp.exp(blast) + _tn(v.astype(BF16), kh.astype(BF16))
            o = os_[hh]
            o_ref[rows, sl] = o
            rr = lax.rsqrt(jnp.mean(o * o, axis=-1, keepdims=True) + EPS)
            gr = g_ref[rows, sl]
            y_ref[rows, sl] = (((o * rr) * gn_ref[...]) * (gr * jax.nn.sigmoid(gr))).astype(BF16)

        def chunk(c, carry):
            rows = pl.ds(pl.multiple_of(c * CHUNK, CHUNK), CHUNK)
            for hh in range(HP):
                head(hh, c, rows)
            return carry

        lax.fori_loop(0, NC, chunk, 0)

    def blk(q):
        return pl.BlockSpec((RB, W), lambda h, r: (r, q * (H // HP) + h))

    sc = lambda: pltpu.VMEM((HP, CHUNK, HEAD), F32)
    return pl.pallas_call(
        body, grid=(H // HP, T // RB),
        in_specs=[blk(0), blk(1), blk(2), blk(3), pl.BlockSpec((1, W), lambda h, r: (0, h)),
                  pl.BlockSpec((1, HEAD), lambda h, r: (0, 0))],
        out_specs=[pl.BlockSpec((RB, W), lambda h, r: (r, h)), pl.BlockSpec((RB, W), lambda h, r: (r, h)),
                   pl.BlockSpec((HP, NC, HEAD, HEAD), lambda h, r: (h, r, 0, 0))],
        out_shape=[jax.ShapeDtypeStruct((T, H * HEAD), BF16), jax.ShapeDtypeStruct((T, H * HEAD), F32),
                   jax.ShapeDtypeStruct((H, T // CHUNK, HEAD, HEAD), F32)],
        scratch_shapes=[pltpu.VMEM((HP, HEAD, HEAD), F32), sc(), sc(), sc(), sc(), sc()],
        compiler_params=_cp("parallel", "arbitrary"), name="hgrn_fwd")(u, u, u, u, lbv, gn)


def _hgrn_bwd(dy, o, s0, u, lbv, gn):
    T = u.shape[0]
    H = 8
    RB = _tile(T, HGRN_ROWS, CHUNK)
    NB = T // RB
    NC = RB // CHUNK
    NS = CHUNK // SUB

    def body(q_ref, f_ref, i_ref, g_ref, lb_ref, gn_ref, o_ref, dy_ref, s0_ref, du_ref, dlb_ref, dgn_ref,
             dst, qs, ks, bs, vs, dos, dqs, dks, dki, dvs, dbs):
        rb = pl.program_id(1)

        @pl.when(rb == 0)
        def _():
            dst[...] = jnp.zeros_like(dst)
            dlb_ref[...] = jnp.zeros_like(dlb_ref)
            dgn_ref[...] = jnp.zeros_like(dgn_ref)

        t8 = lax.broadcasted_iota(jnp.int32, (8, 1), 0)
        lane = lax.broadcasted_iota(jnp.int32, (8, HEAD), 1)
        gnv = gn_ref[...]

        def head(hh, c, rows):
            sl = slice(hh * HEAD, (hh + 1) * HEAD)
            lbv_ = lb_ref[:, sl]
            qr = q_ref[rows, sl]
            q, sq, sg, fg, kk, lf = _hgrn_gates(qr, f_ref[rows, sl], lbv_)
            v = i_ref[rows, sl]
            gr = g_ref[rows, sl]
            b = _cumsum_rows(lf)
            eb = jnp.exp(b)
            ov = o_ref[rows, sl]
            dyv = dy_ref[rows, sl]
            rr = lax.rsqrt(jnp.mean(ov * ov, axis=-1, keepdims=True) + EPS)
            oh = ov * rr
            gs = jax.nn.sigmoid(gr)
            dgr = dyv * (oh * gnv) * _dsilu(gr, gs)
            dnrm = dyv * (gr * gs)
            dgn_ref[hh] += jnp.sum(dnrm * oh, axis=0, keepdims=True)
            t1 = dnrm * gnv
            do = rr * (t1 - oh * jnp.mean(t1 * oh, axis=-1, keepdims=True))
            qs[hh] = q
            ks[hh] = kk
            bs[hh] = b
            vs[hh] = v
            dos[hh] = do
            st0 = s0_ref[hh, c]
            dS = dst[hh]
            do_b = do.astype(BF16)
            blast = bs[hh, pl.ds(CHUNK - 1, 1), :]
            elast = jnp.exp(blast - b)
            dq_inter = _nn(do_b, st0.astype(BF16)) * eb
            dqs[hh] = dq_inter
            dbs[hh] = q * dq_inter
            kh = kk * elast
            dvs[hh] = _nt(kh.astype(BF16), dS.astype(BF16))
            dk_inter = _nn(v.astype(BF16), dS.astype(BF16)) * elast
            dki[hh] = dk_inter
            dks[hh] = jnp.zeros((CHUNK, HEAD), F32)
            for I in range(NS):
                lo = I * SUB
                qI = qs[hh, lo:lo + SUB, :]
                bI = bs[hh, lo:lo + SUB, :]
                doI = dos[hh, lo:lo + SUB, :]
                dqI = jnp.zeros((SUB, HEAD), F32)
                dbI = jnp.zeros((SUB, HEAD), F32)
                if I > 0:
                    bprev = bs[hh, pl.ds(lo - 1, 1), :]
                    eq = jnp.exp(bI - bprev)
                    ek = jnp.exp(bprev - bs[hh, 0:lo, :])
                    qt = _r16(qI * eq)
                    kt = _r16(ks[hh, 0:lo, :] * ek)
                    A = _r16(_nt(qt, kt))
                    doI_b = _r16(doI)
                    dA = _r16(_nt(doI_b, _r16(vs[hh, 0:lo, :])))
                    dvs[hh, 0:lo, :] += _tn(A, doI_b)
                    dqt = _nn(dA, kt)
                    dkt = _tn(dA, qt)
                    dqI = dqI + dqt * eq
                    dbI = dbI + qt.astype(F32) * dqt
                    dks[hh, 0:lo, :] += dkt * ek
                    dbs[hh, 0:lo, :] -= kt.astype(F32) * dkt
                dq_t = [jnp.zeros((8, HEAD), F32) for _ in range(SUB // 8)]
                a_t = [jnp.zeros((8, HEAD), F32) for _ in range(SUB // 8)]
                for s in range(SUB):
                    row = pl.ds(lo + s, 1)
                    brow, krow, vrow = bs[hh, row, :], ks[hh, row, :], vs[hh, row, :]
                    dk_s = None
                    for ti in range(SUB // 8):
                        o8 = 8 * ti
                        if s > o8 + 7:
                            continue
                        d = bI[o8:o8 + 8] - brow
                        if s > o8:
                            d = jnp.where(t8 >= s - o8, d, NEG)
                        Es = jnp.exp(d)
                        qE = qI[o8:o8 + 8] * Es
                        col = jnp.sum(qE * krow, axis=1, keepdims=True)
                        a_t[ti] = jnp.where(lane == s, col, a_t[ti])
                        dcol = jnp.sum(doI[o8:o8 + 8] * vrow, axis=1, keepdims=True)
                        dq_t[ti] = dq_t[ti] + (dcol * Es) * krow
                        part = jnp.sum(dcol * qE, axis=0, keepdims=True)
                        dk_s = part if dk_s is None else dk_s + part
                    dks[hh, row, :] += dk_s
                    dbs[hh, row, :] -= krow * dk_s
                a_d = jnp.concatenate(a_t, axis=0)
                dq_d = jnp.concatenate(dq_t, axis=0)
                dvs[hh, lo:lo + SUB, :] += _tn(a_d, doI)[0:SUB]
                dqI = dqI + dq_d
                dbI = dbI + qI * dq_d
                dqs[hh, lo:lo + SUB, :] += dqI
                dbs[hh, lo:lo + SUB, :] += dbI
            kdk = kk * dki[hh]
            excl = _cumsum_rows(kdk) - kdk
            suff = _cumsum_rows(dbs[hh], reverse=True)
            gdec = jnp.sum(dS * st0, axis=0, keepdims=True) * jnp.exp(blast)
            dlf = suff + excl + gdec
            dk = dks[hh] + dki[hh]
            dfg = dlf / fg - dk
            dlb_ref[:, sl] += jnp.sum(dfg * (1.0 - sg), axis=0, keepdims=True)
            du_ref[0, rows, sl] = (dqs[hh] * _dsilu(qr, sq)).astype(BF16)
            du_ref[1, rows, sl] = (dfg * (1.0 - lbv_) * sg * (1.0 - sg)).astype(BF16)
            du_ref[2, rows, sl] = dvs[hh].astype(BF16)
            du_ref[3, rows, sl] = dgr.astype(BF16)
            dst[hh] = dS * jnp.exp(blast) + _tn(do_b, (q * eb).astype(BF16))

        def chunk(cc, carry):
            c = NC - 1 - cc
            rows = pl.ds(pl.multiple_of(c * CHUNK, CHUNK), CHUNK)
            for hh in range(HP):
                head(hh, c, rows)
            return carry

        lax.fori_loop(0, NC, chunk, 0)

    HP = HGRN_HEADS_BWD
    W = HP * HEAD

    def blk(qd):
        return pl.BlockSpec((RB, W), lambda h, r: (NB - 1 - r, qd * (H // HP) + h))

    hblk = pl.BlockSpec((RB, W), lambda h, r: (NB - 1 - r, h))
    sc = lambda: pltpu.VMEM((HP, CHUNK, HEAD), F32)
    return pl.pallas_call(
        body, grid=(H // HP, NB),
        in_specs=[blk(0), blk(1), blk(2), blk(3), pl.BlockSpec((1, W), lambda h, r: (0, h)),
                  pl.BlockSpec((1, HEAD), lambda h, r: (0, 0)), hblk, hblk,
                  pl.BlockSpec((HP, NC, HEAD, HEAD), lambda h, r: (h, NB - 1 - r, 0, 0))],
        out_specs=[pl.BlockSpec((4, RB, W), lambda h, r: (0, NB - 1 - r, h)),
                   pl.BlockSpec((1, W), lambda h, r: (0, h)), pl.BlockSpec((HP, 1, HEAD), lambda h, r: (h, 0, 0))],
        out_shape=[jax.ShapeDtypeStruct((4, T, H * HEAD), BF16), jax.ShapeDtypeStruct((1, H * HEAD), F32),
                   jax.ShapeDtypeStruct((H, 1, HEAD), F32)],
        scratch_shapes=[pltpu.VMEM((HP, HEAD, HEAD), F32)] + [sc() for _ in range(10)],
        compiler_params=_cp("parallel", "arbitrary"), name="hgrn_bwd")(u, u, u, u, lbv, gn, o, dy, s0)


def _softmax_rows(p_ref, L):
    rows = [p_ref[pl.ds(l, 1), :] for l in range(L)]
    m = rows[0]
    for r in rows[1:]:
        m = jnp.maximum(m, r)
    e = [jnp.exp(r - m) for r in rows]
    tot = e[0]
    for t in e[1:]:
        tot = tot + t
    return [t / tot for t in e]


def _lb_fwd(lbp):
    L, D = lbp.shape

    def body(p_ref, o_ref):
        sm = _softmax_rows(p_ref, L)
        acc = jnp.zeros((1, D), F32)
        o_ref[pl.ds(0, 1), :] = acc
        for l in range(1, L):
            acc = acc + sm[l]
            o_ref[pl.ds(l, 1), :] = acc

    return pl.pallas_call(body, out_shape=jax.ShapeDtypeStruct((L, D), F32), name="lb_fwd")(lbp)


def _lb_bwd(lbp, dlb):
    L, D = lbp.shape

    def body(p_ref, d_ref, o_ref):
        sm = _softmax_rows(p_ref, L)
        dsm = [jnp.zeros((1, D), F32)]
        for i in range(1, L):
            t = jnp.zeros((1, D), F32)
            for l in range(i, L):
                t = t + d_ref[pl.ds(l, 1), :]
            dsm.append(t)
        dot = jnp.zeros((1, D), F32)
        for i in range(L):
            dot = dot + dsm[i] * sm[i]
        for i in range(L):
            o_ref[pl.ds(i, 1), :] = sm[i] * (dsm[i] - dot)

    return pl.pallas_call(body, out_shape=jax.ShapeDtypeStruct((L, D), F32), name="lb_bwd")(lbp, dlb)


def _my_pos():
    return lax.axis_index("x"), lax.axis_index("y"), lax.axis_index("c")


def _peer(mask):
    x, y, c = _my_pos()
    mx, my, mc = (mask >> 2) & 1, (mask >> 1) & 1, mask & 1
    px = (1 - x) if mx else x
    py = (1 - y) if my else y
    pc = (1 - c) if mc else c
    return (px, py, pc), 4 * px + 2 * py + pc


def _all_gather(shards):
    n = len(shards)

    def body(*refs):
        ins, outs = refs[:n], refs[n:2 * n]
        send_sems, recv_sems, local_sems = refs[2 * n:]
        x, y, c = _my_pos()
        me = 4 * x + 2 * y + c
        local = [pltpu.make_async_copy(ins[a], outs[a].at[:, me], local_sems.at[a]) for a in range(n)]
        for cp in local:
            cp.start()
        sends = []
        for m in range(1, N_DEV):
            peer, _ = _peer(m)
            for a in range(n):
                cp = pltpu.make_async_remote_copy(
                    src_ref=ins[a], dst_ref=outs[a].at[:, me], send_sem=send_sems.at[a, m - 1],
                    recv_sem=recv_sems.at[a, m - 1], device_id=peer, device_id_type=MESH)
                cp.start()
                sends.append(cp)
        for m in range(1, N_DEV):
            peer, pid = _peer(m)
            for a in range(n):
                pltpu.make_async_remote_copy(
                    src_ref=ins[a], dst_ref=outs[a].at[:, pid], send_sem=send_sems.at[a, m - 1],
                    recv_sem=recv_sems.at[a, m - 1], device_id=peer, device_id_type=MESH).wait_recv()
        for cp in sends:
            cp.wait_send()
        for cp in local:
            cp.wait()

    out_shape = [jax.ShapeDtypeStruct((s.shape[0], N_DEV) + s.shape[1:], s.dtype) for s in shards]
    return pl.pallas_call(
        body, in_specs=[ANY] * n, out_specs=[ANY] * n, out_shape=out_shape,
        scratch_shapes=[pltpu.SemaphoreType.DMA((n, N_DEV - 1)), pltpu.SemaphoreType.DMA((n, N_DEV - 1)),
                        pltpu.SemaphoreType.DMA((n,))],
        name="all_gather_weights")(*shards)


def _exchange(grads, groups):
    n = len(grads)
    ng = 1 + max(g for g, _ in groups)
    layers = [1 + max(l for g, l in groups if g == gi) for gi in range(ng)]
    shapes = [None] * ng
    for a, (g, l) in enumerate(groups):
        shapes[g] = grads[a].shape[1:]

    def body(*refs):
        ins, outs = refs[:n], refs[n:n + ng]
        send_sems, recv_sems, local_sems = refs[n + ng:]
        x, y, c = _my_pos()
        me = 4 * x + 2 * y + c
        local = []
        for a, (g, l) in enumerate(groups):
            cp = pltpu.make_async_copy(ins[a].at[me], outs[g].at[me, l], local_sems.at[a])
            cp.start()
            local.append(cp)
        sends = []
        for m in range(1, N_DEV):
            peer, pid = _peer(m)
            for a, (g, l) in enumerate(groups):
                cp = pltpu.make_async_remote_copy(
                    src_ref=ins[a].at[pid], dst_ref=outs[g].at[me, l], send_sem=send_sems.at[a, m - 1],
                    recv_sem=recv_sems.at[a, m - 1], device_id=peer, device_id_type=MESH)
                cp.start()
                sends.append(cp)
        for m in range(1, N_DEV):
            peer, pid = _peer(m)
            for a, (g, l) in enumerate(groups):
                pltpu.make_async_remote_copy(
                    src_ref=ins[a].at[pid], dst_ref=outs[g].at[pid, l], send_sem=send_sems.at[a, m - 1],
                    recv_sem=recv_sems.at[a, m - 1], device_id=peer, device_id_type=MESH).wait_recv()
        for cp in sends:
            cp.wait_send()
        for cp in local:
            cp.wait()

    out_shape = [jax.ShapeDtypeStruct((N_DEV, layers[g]) + shapes[g], grads[[gg for gg, _ in groups].index(g)].dtype)
                 for g in range(ng)]
    return pl.pallas_call(
        body, in_specs=[ANY] * n, out_specs=[ANY] * ng, out_shape=out_shape,
        scratch_shapes=[pltpu.SemaphoreType.DMA((n, N_DEV - 1)), pltpu.SemaphoreType.DMA((n, N_DEV - 1)),
                        pltpu.SemaphoreType.DMA((n,))],
        name="exchange_grads")(*grads)


HBM_SPEC = pl.BlockSpec(memory_space=pltpu.HBM)
SEM_SPEC = pl.BlockSpec(memory_space=pltpu.SEMAPHORE)
EFFECT = pltpu.SideEffectType.DATAFLOW_SIDE_EFFECTING


def _hbm(a):
    return pltpu.with_memory_space_constraint(a, pltpu.HBM)


def _landing(block_shape, dtype, axis=0):
    if axis == 0:
        return lax.empty((N_DEV,) + tuple(block_shape), dtype)
    rows, n = block_shape
    return lax.empty((rows, N_DEV * n), dtype)


def _slot(ref, i):
    if len(ref.shape) == 2:
        n = ref.shape[1] // N_DEV
        return ref.at[:, pl.ds(i * n, n)]
    return ref.at[i]


def _push_start(name, srcs, lands, whole, groups):
    n = len(srcs)
    ng = 1 + max(groups)
    cnt = [groups.count(g) for g in range(ng)]
    idx = [groups[:a].count(groups[a]) for a in range(n)]

    def body(*refs):
        src_refs, land_refs = refs[:n], refs[n:2 * n]
        sems = refs[2 * n:2 * n + 3 * ng]
        token = refs[-1]
        x, y, c = _my_pos()
        me = 4 * x + 2 * y + c
        for a in range(n):
            g = groups[a]
            for m in range(1, N_DEV):
                peer, pid = _peer(m)
                pltpu.make_async_remote_copy(
                    src_ref=src_refs[a] if whole else src_refs[a].at[pid], dst_ref=_slot(land_refs[a], me),
                    send_sem=sems[3 * g].at[idx[a] * (N_DEV - 1) + m - 1],
                    recv_sem=sems[3 * g + 1].at[idx[a] * (N_DEV - 1) + m - 1],
                    device_id=peer, device_id_type=MESH).start()
            pltpu.make_async_copy(src_refs[a] if whole else src_refs[a].at[me], _slot(land_refs[a], me),
                                  sems[3 * g + 2].at[idx[a]]).start()
        token[...] = jnp.zeros_like(token)

    sem_shapes = []
    for g in range(ng):
        sem_shapes += [pltpu.SemaphoreType.DMA((cnt[g] * (N_DEV - 1),))] * 2 + [pltpu.SemaphoreType.DMA((cnt[g],))]
    thru = [pltpu.HBM(s.shape, s.dtype) for s in list(srcs) + list(lands)]
    res = pl.pallas_call(
        body, name=name,
        out_shape=tuple(sem_shapes + thru + [jax.ShapeDtypeStruct((8, 128), F32)]),
        in_specs=tuple([HBM_SPEC] * (2 * n)),
        out_specs=tuple([SEM_SPEC] * (3 * ng) + [HBM_SPEC] * (2 * n) + [pl.BlockSpec(memory_space=pltpu.VMEM)]),
        input_output_aliases={i: 3 * ng + i for i in range(2 * n)},
        compiler_params=pltpu.CompilerParams(has_side_effects=EFFECT),
    )(*[_hbm(s) for s in srcs], *[_hbm(z) for z in lands])
    sems = [(res[3 * g], res[3 * g + 1], res[3 * g + 2]) for g in range(ng)]
    srcs_thru = list(res[3 * ng:3 * ng + n])
    lands_thru = list(res[3 * ng + n:3 * ng + 2 * n])
    return sems, srcs_thru, lands_thru, res[-1]


def _push_wait(name, srcs_thru, lands_thru, sems, after, whole):
    n = len(srcs_thru)

    def body(*refs):
        src_refs, land_refs = refs[:n], refs[n:2 * n]
        send_sems, recv_sems, own_sems = refs[2 * n], refs[2 * n + 1], refs[2 * n + 2]
        x, y, c = _my_pos()
        me = 4 * x + 2 * y + c
        for a in range(n):
            pltpu.make_async_copy(src_refs[a] if whole else src_refs[a].at[me], _slot(land_refs[a], me),
                                  own_sems.at[a]).wait()
            for m in range(1, N_DEV):
                peer, pid = _peer(m)
                cp = pltpu.make_async_remote_copy(
                    src_ref=src_refs[a] if whole else src_refs[a].at[pid], dst_ref=_slot(land_refs[a], pid),
                    send_sem=send_sems.at[a * (N_DEV - 1) + m - 1], recv_sem=recv_sems.at[a * (N_DEV - 1) + m - 1],
                    device_id=peer, device_id_type=MESH)
                cp.wait_send()
                cp.wait_recv()

    thru = [pltpu.HBM(s.shape, s.dtype) for s in list(srcs_thru) + list(lands_thru)]
    res = pl.pallas_call(
        body, name=name, out_shape=tuple(thru),
        in_specs=tuple([HBM_SPEC] * (2 * n) + [SEM_SPEC, SEM_SPEC, SEM_SPEC, ANY]),
        out_specs=tuple([HBM_SPEC] * (2 * n)),
        input_output_aliases={i: i for i in range(2 * n)},
        compiler_params=pltpu.CompilerParams(has_side_effects=EFFECT),
    )(*srcs_thru, *lands_thru, sems[0], sems[1], sems[2], after)
    return list(res[n:])


def _adamw(recv, w, m, v, layer=0, prev=None):
    L, R, C = w.shape
    tr = _tile(R, max(8, (1 << 18) // C), 8) if R % 8 == 0 else R
    bc1 = 1.0 - ADAM_B1 ** ADAM_STEP
    bc2 = 1.0 - ADAM_B2 ** ADAM_STEP
    if prev is None:
        prev = [lax.empty((L, R, C), F32) for _ in range(4)]

    def body(r_ref, w_ref, m_ref, v_ref, p0, p1, p2, p3, g_ref, d_ref, nm_ref, nv_ref):
        g = r_ref[0].astype(F32)
        for s in range(1, N_DEV):
            g = g + r_ref[s].astype(F32)
        nm = ADAM_B1 * m_ref[...] + (1.0 - ADAM_B1) * g
        nv = ADAM_B2 * v_ref[...] + (1.0 - ADAM_B2) * (g * g)
        mh = nm / bc1
        vh = nv / bc2
        g_ref[...] = g
        d_ref[...] = -ADAM_LR * (mh / (jnp.sqrt(vh) + ADAM_EPS) + ADAM_WD * w_ref[...])
        nm_ref[...] = nm
        nv_ref[...] = nv

    row = pl.BlockSpec((None, tr, C), lambda i: (layer, i, 0))
    return pl.pallas_call(
        body, grid=(R // tr,),
        in_specs=[pl.BlockSpec((N_DEV, tr, C), lambda i: (0, i, 0)), row, row, row] + [ANY] * 4,
        out_specs=[row] * 4, out_shape=[jax.ShapeDtypeStruct((L, R, C), F32)] * 4,
        input_output_aliases={4: 0, 5: 1, 6: 2, 7: 3},
        compiler_params=_cp("parallel"), name="adamw")(recv, w, m, v, *prev)


def _full_w_spec(tk, tn):
    return pl.BlockSpec((tk, tn), lambda i, j, k: (k, j))


def _colblk_w_spec(n):
    def spec(tk, tn):
        per = n // tn
        return pl.BlockSpec((None, tk, tn), lambda i, j, k: (j // per, k, j % per))
    return spec


def kernel(x, meta_tokens, mix_norm_g, mlp_norm_g, final_norm_g, ev_w_in, ev_conv_w, ev_conv_b, ev_ln_g, ev_ln_b, ev_pool_w, ev_pool_b, ev_pool_scale, ev_w_out, od_w_in, od_gnorm_g, od_w_out, lb_param, mlp_w1, mlp_w2, loss_target, m_meta_tokens, m_mix_norm_g, m_mlp_norm_g, m_final_norm_g, m_ev_w_in, m_ev_conv_w, m_ev_conv_b, m_ev_ln_g, m_ev_ln_b, m_ev_pool_w, m_ev_pool_b, m_ev_pool_scale, m_ev_w_out, m_od_w_in, m_od_gnorm_g, m_od_w_out, m_lb_param, m_mlp_w1, m_mlp_w2, v_meta_tokens, v_mix_norm_g, v_mlp_norm_g, v_final_norm_g, v_ev_w_in, v_ev_conv_w, v_ev_conv_b, v_ev_ln_g, v_ev_ln_b, v_ev_pool_w, v_ev_pool_b, v_ev_pool_scale, v_ev_w_out, v_od_w_in, v_od_gnorm_g, v_od_w_out, v_lb_param, v_mlp_w1, v_mlp_w2):
    S, D = x.shape[1], x.shape[2]
    T = PAD + N_META + S
    DEPTH = mix_norm_g.shape[0]
    DFF = mlp_w1.shape[2] * N_DEV
    dev = 4 * lax.axis_index("x") + 2 * lax.axis_index("y") + lax.axis_index("c")

    g_meta, g_cw = _all_gather([meta_tokens[None], ev_conv_w])
    n_ev = ev_w_in.shape[0]
    n_od = od_w_in.shape[0]
    meta_full = jnp.transpose(g_meta[0], (1, 0, 2)).reshape(N_META, D)
    cw_full = jnp.transpose(g_cw, (0, 2, 1, 3)).reshape(n_ev, CONV_WIDTH, -1)
    cw_pad = jnp.pad(cw_full, ((0, 0), (0, 32 - CONV_WIDTH), (0, 0)))
    n_in_od = od_w_in.shape[2]
    n_w1 = mlp_w1.shape[2]

    ag_src, ag_grp, ag_axis, ag_at = [], [], [], {}
    for layer in range(DEPTH):
        j = layer // 2
        mixer = [("in", ev_w_in[j]), ("out", ev_w_out[j])] if layer % 2 == 0 else [("in", od_w_in[j]), ("out", od_w_out[j])]
        for pos, (key, arr) in enumerate(mixer + [("w1", mlp_w1[layer]), ("w2", mlp_w2[layer])]):
            ag_at[layer, key] = len(ag_src)
            ag_src.append(arr.astype(BF16))
            ag_grp.append(len(ag_grp))
            ag_axis.append(1 if key in ("in", "w1") and arr.shape[1] % 128 == 0 else 0)
    ag_src, g_meta, g_cw = lax.optimization_barrier((ag_src, g_meta, g_cw))
    ag_sems, ag_s, ag_l, ag_tok = _push_start(
        "ag_start", ag_src, [_landing(s_.shape, s_.dtype, ax) for s_, ax in zip(ag_src, ag_axis)], True, ag_grp)

    def ag_wait(layer, key, after):
        a = ag_at[layer, key]
        return _push_wait(f"ag_wait_{a}", [ag_s[a]], [ag_l[a]], ag_sems[a], after, True)[0]

    h = jnp.concatenate([jnp.zeros((PAD, D), F32), meta_full, x[0]], axis=0) + ag_tok[0, 0]
    tgt = jnp.pad(loss_target[0], ((PAD + N_META, 0), (0, 0)))
    lb_all = _lb_fwd(lb_param)

    tm_big = _tile(T, MM_ROWS_BIG, 16)
    tm_mid = _tile(T, MM_ROWS_MID, 16)
    tm_k4 = _tile(T, MM_ROWS_K4, 16)

    saved = []
    for layer in range(DEPTH):
        j = layer // 2
        sv = {"h0": h}
        g_in = ag_wait(layer, "in", h)
        w_in = g_in if g_in.ndim == 2 else jnp.transpose(g_in, (1, 0, 2)).reshape(D, -1)
        if layer % 2 == 0:
            sv["n"], u = _mm_rms_nn("ev_in", h, mix_norm_g[layer][None], w_in, tm_big, 512, "f32")
            yab, yc = _ev_fwd(u, cw_pad[j], ev_conv_b[j][None], ev_ln_g[j][None], ev_ln_b[j][None],
                              ev_pool_w[j].astype(BF16), ev_pool_b[j].reshape(1, -1), ev_pool_scale[j][None])
            sv.update(u=u, y=yab, yc=yc)
            w_out = ag_wait(layer, "out", yab).reshape(-1, D)
            h = _mm_nn("ev_out", yab, w_out, _full_w_spec, T, D, D, tm_mid, D, D, "resid", extra=h)
        else:
            sv["n"], u = _mm_rms_nn("od_in", h, mix_norm_g[layer][None], w_in, tm_big, 512, "f32")
            y, o, s0 = _hgrn_fwd(u, lb_all[layer][None], od_gnorm_g[j][None])
            sv.update(u=u, y=y, o=o, s0=s0)
            w_out = ag_wait(layer, "out", y).reshape(-1, D)
            h = _mm_nn("od_out", y, w_out, _full_w_spec, T, D, D, tm_mid, D, D, "resid", extra=h)
        sv["h1"] = h
        w_w1 = ag_wait(layer, "w1", h)
        n2, r, act = _mm_rms_nn("mlp_w1", h, mlp_norm_g[layer][None], w_w1, tm_big, 512, "relu2")
        w_w2 = ag_wait(layer, "w2", act).reshape(DFF, D)
        sv.update(w_in=w_in, w_out=w_out, w_w1=w_w1, w_w2=w_w2)
        sv.update(n2=n2, r=r, act=act)
        h = _mm_nn("mlp_w2", act, w_w2, _full_w_spec, T, D, DFF, tm_mid, 512, DFF, "resid", extra=h)
        saved.append(sv)

    loss_blk, dh, dhb, dg_final = _loss_head(h, final_norm_g[None], tgt)
    loss = lax.psum(loss_blk[0, 0], AXES)

    tt = T
    g_mix, g_mlp = [None] * DEPTH, [None] * DEPTH
    small ={"cw": [None] * n_ev, "vec": [None] * n_ev, "pw": [None] * n_ev, "gn": [None] * n_od}
    dlb_rows = [jnp.zeros((1, D), F32) for _ in range(DEPTH)]

    def xs2(tt_, tk):
        return pl.BlockSpec((tt_, tk), lambda a, b, t: (t, a))

    def ys2(tt_, tn):
        return pl.BlockSpec((tt_, tn), lambda a, b, t: (t, b))

    def os2(tk, tn):
        return pl.BlockSpec((tk, tn), lambda a, b, t: (a, b))

    def os3(tk, tn):
        return pl.BlockSpec((None, tk, tn), lambda a, b, t: (b, a, 0))

    def dy2(tm, tn):
        return pl.BlockSpec((tm, tn), lambda i, jj, k: (i, k))

    def w_rows(tj, tn):
        return pl.BlockSpec((tj, tn), lambda i, jj, k: (jj, k))

    def w_whole(tj, tn):
        return pl.BlockSpec((tj, tn), lambda i, jj, k: (0, 0), pipeline_mode=pl.Buffered(1))

    def w_colblk(tj, tn):
        return pl.BlockSpec((None, tj, tn), lambda i, jj, k: (k, jj, 0))

    rs_pending = []

    def rs_start(tag, mats):
        blocks = [m_ if m_.ndim == 3 else m_.reshape(N_DEV, m_.shape[0] // N_DEV, m_.shape[1]) for m_ in mats]
        lands = [_landing(b_.shape[1:], b_.dtype) for b_ in blocks]
        sems, s_thru, l_thru, tok = _push_start(f"rs_start_{tag}", blocks, lands, False, [0] * len(blocks))
        rs_pending.append((tag, s_thru, l_thru, sems[0]))
        return tok[0, 0]

    for layer in reversed(range(DEPTH)):
        j = layer // 2
        sv = saved[layer]
        da1 = _mm_nt("mlp_w2_t", dhb, sv["w_w2"], dy2, w_rows, T, DFF, D, tm_mid, 1024, D, "dact", extra=sv["r"])
        dw2 = _mm_tn("mlp_dw2", sv["act"], dhb, xs2, ys2, os2, (DFF, D), T, DFF, D, tt, 512, D)
        dw1 = _mm_tn("mlp_dw1", sv["n2"], da1, xs2, ys2, os3, (N_DEV, D, n_w1), T, D, DFF, tt, D, n_w1)
        tok = rs_start(f"mlp{layer}", [dw1, dw2])
        dh, dhb, g_mlp[layer] = _mm_nt("mlp_w1_t", da1, sv["w_w1"], dy2, w_whole, T, D, DFF, tm_k4, D, DFF, "rms",
                                       extra=(sv["h1"], mlp_norm_g[layer][None] + tok, dh))
        if layer % 2 == 0:
            dyab = _mm_nt("ev_out_t", dhb, sv["w_out"], dy2, w_rows, T, D, D, tm_mid, D, D, "f32")
            dwout = _mm_tn("ev_dwout", sv["y"], dhb, xs2, ys2, os2, (D, D), T, D, D, tt, 512, D)
            du, small["cw"][j], small["vec"][j], small["pw"][j] = _ev_bwd(
                dyab, sv["yc"], sv["u"], cw_pad[j], ev_ln_g[j][None], ev_ln_b[j][None], ev_pool_w[j].astype(BF16),
                jnp.transpose(ev_pool_w[j], (0, 2, 1)).astype(BF16), ev_pool_b[j].reshape(1, -1),
                ev_pool_scale[j][None])
            nin = du.shape[1]
            dwin = _mm_tn("ev_dwin", sv["n"], du, xs2, ys2, os2, (D, nin), T, D, nin, tt, D, 512)
            dwin = jnp.transpose(dwin.reshape(D, N_DEV, nin // N_DEV), (1, 0, 2))
            tok = rs_start(f"mix{layer}", [dwin, dwout])
            dh, dhb, g_mix[layer] = _mm_nt("ev_in_t", du, sv["w_in"], dy2, w_whole, T, D, nin, tm_k4, D, nin, "rms",
                                           extra=(sv["h0"], mix_norm_g[layer][None] + tok, dh))
        else:
            dy = _mm_nt("od_out_t", dhb, sv["w_out"], dy2, w_rows, T, D, D, tm_mid, D, D, "f32")
            dwout = _mm_tn("od_dwout", sv["y"], dhb, xs2, ys2, os2, (D, D), T, D, D, tt, 512, D)
            du3, dlb_rows[layer], small["gn"][j] = _hgrn_bwd(dy, sv["o"], sv["s0"], sv["u"], lb_all[layer][None],
                                                              od_gnorm_g[j][None])
            per = D // n_in_od

            def du_t(tt_, tn):
                return pl.BlockSpec((None, tt_, tn), lambda a, b, t: (b // per, t, b % per))

            dwin = _mm_tn("od_dwin", sv["n"], du3, xs2, du_t, os3, (N_DEV, D, n_in_od), T, D, 4 * D, tt, D, n_in_od)
            tok = rs_start(f"mix{layer}", [dwin, dwout])
            dh, dhb, g_mix[layer] = _mm_nt(
                "od_in_t", du3, sv["w_in"], lambda tm, tn: pl.BlockSpec((4, tm, tn // 4), lambda i, jj, k: (0, i, 0)),
                w_whole, T, D, 4 * D, tm_k4, D, 4 * D, "rms", extra=(sv["h0"], mix_norm_g[layer][None] + tok, dh),
                parts=4)

    dmeta = dh[PAD:PAD + N_META]
    grad_x = dh[PAD + N_META:][None]
    dlb_param = _lb_bwd(lb_param, jnp.concatenate(dlb_rows, axis=0))

    pieces = [
        ("final", dg_final), ("pad", jnp.zeros((SMALL_F32_ROWS - 1, D), F32)),
        ("meta", dmeta), ("mix", jnp.concatenate(g_mix, 0)), ("mlp", jnp.concatenate(g_mlp, 0)),
        ("cw", jnp.stack([c[:CONV_WIDTH] for c in small["cw"]])), ("cb", jnp.stack([v_[0] for v_ in small["vec"]])),
        ("lng", jnp.stack([v_[1] for v_ in small["vec"]])), ("lnb", jnp.stack([v_[2] for v_ in small["vec"]])),
        ("pw", jnp.stack(small["pw"])), ("pb", jnp.stack([v_[4] for v_ in small["vec"]])),
        ("ps", jnp.stack([v_[3] for v_ in small["vec"]])), ("gn", jnp.stack([jnp.sum(g_, axis=0)[0] for g_ in small["gn"]])),
        ("lb", dlb_param),
    ]
    flat = jnp.concatenate([p.reshape(-1) for _, p in pieces])
    n_small = flat.shape[0]
    rows_small = SMALL_F32_ROWS + -(-(n_small // 1024 + 1 - SMALL_F32_ROWS) // 16) * 16
    flat = jnp.pad(flat, (0, rows_small * 1024 - n_small)).reshape(rows_small, 1024)

    to_all = lambda a_: jnp.broadcast_to(a_[None], (N_DEV,) + a_.shape)
    got_f32, got_bf16 = _exchange([to_all(flat[:SMALL_F32_ROWS]), to_all(flat[SMALL_F32_ROWS:].astype(BF16))],
                                  [(0, 0), (1, 0)])
    recv_small = jnp.concatenate([got_f32, got_bf16.astype(F32)], axis=2)
    recv = {}
    for tag, s_thru, l_thru, sems in rs_pending:
        got = _push_wait(f"rs_wait_{tag}", s_thru, l_thru, sems, recv_small, False)
        layer = int(tag[3:])
        if tag.startswith("mlp"):
            recv["w1", layer], recv["w2", layer] = got
        else:
            key = "ev" if layer % 2 == 0 else "od"
            recv[key + "_in", layer // 2], recv[key + "_out", layer // 2] = got

    outs = {}
    big = {"ev_in": ("ev_w_in", ev_w_in, m_ev_w_in, v_ev_w_in), "ev_out": ("ev_w_out", ev_w_out, m_ev_w_out, v_ev_w_out),
           "od_in": ("od_w_in", od_w_in, m_od_w_in, v_od_w_in), "od_out": ("od_w_out", od_w_out, m_od_w_out, v_od_w_out),
           "w1": ("mlp_w1", mlp_w1, m_mlp_w1, v_mlp_w1), "w2": ("mlp_w2", mlp_w2, m_mlp_w2, v_mlp_w2)}
    for key, (name, w, m, v) in big.items():
        res = None
        for l in range(w.shape[0]):
            res = _adamw(recv[key, l], w, m, v, layer=l, prev=res)
        outs[name] = res

    small_params = {
        "meta": ("meta_tokens", None), "mix": ("mix_norm_g", mix_norm_g, m_mix_norm_g, v_mix_norm_g),
        "mlp": ("mlp_norm_g", mlp_norm_g, m_mlp_norm_g, v_mlp_norm_g),
        "final": ("final_norm_g", final_norm_g, m_final_norm_g, v_final_norm_g),
        "cw": ("ev_conv_w", None), "cb": ("ev_conv_b", ev_conv_b, m_ev_conv_b, v_ev_conv_b),
        "lng": ("ev_ln_g", ev_ln_g, m_ev_ln_g, v_ev_ln_g), "lnb": ("ev_ln_b", ev_ln_b, m_ev_ln_b, v_ev_ln_b),
        "pw": ("ev_pool_w", ev_pool_w, m_ev_pool_w, v_ev_pool_w), "pb": ("ev_pool_b", ev_pool_b, m_ev_pool_b, v_ev_pool_b),
        "ps": ("ev_pool_scale", ev_pool_scale, m_ev_pool_scale, v_ev_pool_scale),
        "gn": ("od_gnorm_g", od_gnorm_g, m_od_gnorm_g, v_od_gnorm_g), "lb": ("lb_param", lb_param, m_lb_param, v_lb_param),
    }
    csh = ev_conv_w.shape[2]
    msh = meta_tokens.shape[1]

    def packed(which):
        parts = []
        for key, g_ in pieces:
            ent = small_params.get(key)
            if key == "pad":
                full = g_
            elif key == "meta":
                src = (meta_tokens, m_meta_tokens, v_meta_tokens)[which]
                full = lax.dynamic_update_slice(jnp.zeros((N_META, D), F32), src, (0, dev * msh))
            elif key == "cw":
                src = (ev_conv_w, m_ev_conv_w, v_ev_conv_w)[which]
                full = lax.dynamic_update_slice(jnp.zeros(g_.shape, F32), src, (0, 0, dev * csh))
            else:
                full = ent[1 + which]
            parts.append(full.reshape(-1))
        f = jnp.concatenate(parts)
        return jnp.pad(f, (0, rows_small * 1024 - n_small)).reshape(rows_small, 1024)

    sres = [r_[0] for r_ in _adamw(recv_small[:, 0], packed(0)[None], packed(1)[None], packed(2)[None])]
    off = 0
    for key, g_ in pieces:
        size = g_.size
        vals = [r_.reshape(-1)[off:off + size].reshape(g_.shape) for r_ in sres]
        off += size
        if key == "pad":
            continue
        name = small_params[key][0]
        if key == "meta":
            vals = [lax.dynamic_slice(v_, (0, dev * msh), (N_META, msh)) for v_ in vals]
        elif key == "cw":
            vals = [lax.dynamic_slice(v_, (0, 0, dev * csh), v_.shape[:2] + (csh,)) for v_ in vals]
        else:
            vals = [v_.reshape(small_params[key][1].shape) for v_ in vals]
        outs[name] = vals

    names = ["meta_tokens", "mix_norm_g", "mlp_norm_g", "final_norm_g", "ev_w_in", "ev_conv_w", "ev_conv_b", "ev_ln_g",
             "ev_ln_b", "ev_pool_w", "ev_pool_b", "ev_pool_scale", "ev_w_out", "od_w_in", "od_gnorm_g", "od_w_out",
             "lb_param", "mlp_w1", "mlp_w2"]
    result = [loss, grad_x]
    for k in range(4):
        result += [outs[nm][k] for nm in names]
    return tuple(result)
```

```python
import functools

import jax
import jax.numpy as jnp
from jax import lax
from jax.experimental import pallas as pl
from jax.experimental.pallas import tpu as pltpu

F32 = jnp.float32
BF16 = jnp.bfloat16

N_DEV = 8
N_META = 16
CHUNK = 64
PAD = CHUNK - N_META
SUB = 16
HEAD = 128
CONV_WIDTH = 31
HALO = 32
POOL_WINDOWS = (2, 4, 8, 16)
EPS = 1e-6
NEG = -1e30
ADAM_LR, ADAM_B1, ADAM_B2, ADAM_EPS, ADAM_WD, ADAM_STEP = 0.001, 0.9, 0.999, 1e-08, 0.01, 10
VMEM_LIMIT = 56 * 1024 * 1024
EV_ROWS = 416
HGRN_ROWS = 832
MM_ROWS_BIG = 2080
MM_ROWS_MID = 1040
MM_ROWS_K4 = 416
SMALL_F32_ROWS = 8
HGRN_HEADS_FWD = 4
HGRN_HEADS_BWD = 2
MESH = pl.DeviceIdType.MESH
AXES = ("x", "y", "c")
ANY = pl.BlockSpec(memory_space=pl.ANY)


def _cp(*sem):
    return pltpu.CompilerParams(dimension_semantics=sem, vmem_limit_bytes=VMEM_LIMIT)


def _tile(n, cap, mult):
    best = None
    for d in range(mult, min(n, cap) + 1, mult):
        if n % d == 0:
            best = d
    assert best is not None, (n, cap, mult)
    return best


def _nt(a, b):
    return lax.dot_general(a, b, (((1,), (1,)), ((), ())), preferred_element_type=F32)


def _tn(a, b):
    return lax.dot_general(a, b, (((0,), (0,)), ((), ())), preferred_element_type=F32)


def _nn(a, b):
    return jnp.dot(a, b, preferred_element_type=F32)


def _r16(x):
    return x.astype(BF16).astype(F32)


def _row_ids(base, n):
    return base + lax.broadcasted_iota(jnp.int32, (n, 1), 0)


def _dsilu(x, s):
    return s * (1.0 + x * (1.0 - s))


def _rms_fwd(h, g):
    T, D = h.shape
    tm = _tile(T, MM_ROWS_MID, 16)

    def body(h_ref, g_ref, n_ref):
        x = h_ref[...]
        r = lax.rsqrt(jnp.mean(x * x, axis=-1, keepdims=True) + EPS)
        n_ref[...] = ((x * r) * g_ref[...]).astype(BF16)

    return pl.pallas_call(
        body, grid=(T // tm,),
        in_specs=[pl.BlockSpec((tm, D), lambda i: (i, 0)), pl.BlockSpec((1, D), lambda i: (0, 0))],
        out_specs=pl.BlockSpec((tm, D), lambda i: (i, 0)),
        out_shape=jax.ShapeDtypeStruct((T, D), BF16),
        compiler_params=_cp("parallel"), name="rms_fwd")(h, g)


def _rms_bwd(dn, h, g, dres):
    T, D = h.shape
    tm = _tile(T, MM_ROWS_MID, 16)

    def body(dn_ref, h_ref, g_ref, dres_ref, dh_ref, dhb_ref, dg_ref):
        i = pl.program_id(0)
        x = h_ref[...]
        dn_v = dn_ref[...]
        r = lax.rsqrt(jnp.mean(x * x, axis=-1, keepdims=True) + EPS)
        xh = x * r
        dxh = dn_v * g_ref[...]
        dx = r * (dxh - xh * jnp.mean(dxh * xh, axis=-1, keepdims=True))
        keep = _row_ids(i * tm, tm) >= PAD
        dh = jnp.where(keep, dres_ref[...] + dx, 0.0)
        dh_ref[...] = dh
        dhb_ref[...] = dh.astype(BF16)

        @pl.when(i == 0)
        def _():
            dg_ref[...] = jnp.zeros_like(dg_ref)

        dg_ref[...] += jnp.sum(dn_v * xh, axis=0, keepdims=True)

    row = pl.BlockSpec((tm, D), lambda i: (i, 0))
    vec = pl.BlockSpec((1, D), lambda i: (0, 0))
    return pl.pallas_call(
        body, grid=(T // tm,),
        in_specs=[row, row, vec, row], out_specs=[row, row, vec],
        out_shape=[jax.ShapeDtypeStruct((T, D), F32), jax.ShapeDtypeStruct((T, D), BF16),
                   jax.ShapeDtypeStruct((1, D), F32)],
        compiler_params=_cp("arbitrary"), name="rms_bwd")(dn, h, g, dres)


def _loss_head(h, g, tgt):
    T, D = h.shape
    tm = _tile(T, MM_ROWS_MID, 16)
    first_x = PAD + N_META

    def body(h_ref, g_ref, t_ref, loss_ref, dh_ref, dhb_ref, dg_ref):
        i = pl.program_id(0)
        x = h_ref[...]
        r = lax.rsqrt(jnp.mean(x * x, axis=-1, keepdims=True) + EPS)
        xh = x * r
        gv = g_ref[...]
        out = xh * gv
        valid = _row_ids(i * tm, tm) >= first_x
        e = jnp.where(valid, out - t_ref[...], 0.0)
        dout = e * (1.0 / D)
        dxh = dout * gv
        dx = r * (dxh - xh * jnp.mean(dxh * xh, axis=-1, keepdims=True))
        dh_ref[...] = dx
        dhb_ref[...] = dx.astype(BF16)

        @pl.when(i == 0)
        def _():
            dg_ref[...] = jnp.zeros_like(dg_ref)
            loss_ref[...] = jnp.zeros_like(loss_ref)

        dg_ref[...] += jnp.sum(dout * xh, axis=0, keepdims=True)
        loss_ref[...] += 0.5 * jnp.sum(jnp.mean(e * e, axis=-1, keepdims=True))

    row = pl.BlockSpec((tm, D), lambda i: (i, 0))
    vec = pl.BlockSpec((1, D), lambda i: (0, 0))
    return pl.pallas_call(
        body, grid=(T // tm,),
        in_specs=[row, vec, row],
        out_specs=[pl.BlockSpec((8, 128), lambda i: (0, 0)), row, row, vec],
        out_shape=[jax.ShapeDtypeStruct((8, 128), F32), jax.ShapeDtypeStruct((T, D), F32),
                   jax.ShapeDtypeStruct((T, D), BF16), jax.ShapeDtypeStruct((1, D), F32)],
        compiler_params=_cp("arbitrary"), name="loss_head")(h, g, tgt)


def _mm_nn(name, a, w, w_spec, M, N, K, tm, tn, tk, mode, extra=None, a_spec=None):
    nk = K // tk
    if a_spec is None:
        a_spec = pl.BlockSpec((tm, tk), lambda i, j, k: (i, k))
    o_spec = pl.BlockSpec((tm, tn), lambda i, j, k: (i, j))

    def body(*refs):
        if mode == "resid":
            a_ref, w_ref, e_ref = refs[:3]
            outs = refs[3:]
        else:
            a_ref, w_ref = refs[:2]
            outs = refs[2:]
        acc_ref = outs[-1] if nk > 1 else None
        part = _nn(a_ref[...], w_ref[...])

        def finish(acc):
            if mode == "f32":
                outs[0][...] = acc
            elif mode == "relu2":
                r = jnp.maximum(acc, 0.0)
                outs[0][...] = r.astype(BF16)
                outs[1][...] = (r * r).astype(BF16)
            else:
                keep = _row_ids(pl.program_id(0) * tm, tm) >= PAD
                outs[0][...] = jnp.where(keep, e_ref[...] + acc, 0.0)

        if nk == 1:
            finish(part)
        else:
            k = pl.program_id(2)

            @pl.when(k == 0)
            def _():
                acc_ref[...] = part

            @pl.when(k > 0)
            def _():
                acc_ref[...] += part

            @pl.when(k == nk - 1)
            def _():
                finish(acc_ref[...])

    in_specs = [a_spec, w_spec(tk, tn)]
    args = [a, w]
    if mode == "resid":
        in_specs.append(o_spec)
        args.append(extra)
    if mode == "relu2":
        out_specs = [o_spec, o_spec]
        out_shape = [jax.ShapeDtypeStruct((M, N), BF16)] * 2
    else:
        out_specs = [o_spec]
        out_shape = [jax.ShapeDtypeStruct((M, N), F32)]
    scratch = [pltpu.VMEM((tm, tn), F32)] if nk > 1 else []
    res = pl.pallas_call(
        body, grid=(M // tm, N // tn, nk), in_specs=in_specs, out_specs=out_specs, out_shape=out_shape,
        scratch_shapes=scratch, compiler_params=_cp("parallel", "parallel", "arbitrary"), name=name)(*args)
    return res if mode == "relu2" else res[0]


def _mm_rms_nn(name, h, g, w, tm, tn, mode):
    M, K = h.shape
    N = w.shape[1]

    def body(h_ref, g_ref, w_ref, n_ref, *outs):
        @pl.when(pl.program_id(1) == 0)
        def _():
            ch = _tile(tm, 256, 16)

            def chunk(c, carry):
                rows = pl.ds(pl.multiple_of(c * ch, ch), ch)
                x = h_ref[rows, :]
                r = lax.rsqrt(jnp.mean(x * x, axis=-1, keepdims=True) + EPS)
                n_ref[rows, :] = ((x * r) * g_ref[...]).astype(BF16)
                return carry

            lax.fori_loop(0, tm // ch, chunk, 0)

        acc = _nn(n_ref[...], w_ref[...])
        if mode == "f32":
            outs[0][...] = acc
        else:
            r = jnp.maximum(acc, 0.0)
            outs[0][...] = r.astype(BF16)
            outs[1][...] = (r * r).astype(BF16)

    row = pl.BlockSpec((tm, K), lambda i, j: (i, 0))
    o_spec = pl.BlockSpec((tm, tn), lambda i, j: (i, j))
    n_out = 1 if mode == "f32" else 2
    return pl.pallas_call(
        body, grid=(M // tm, N // tn),
        in_specs=[row, pl.BlockSpec((1, K), lambda i, j: (0, 0)), pl.BlockSpec((K, tn), lambda i, j: (0, j))],
        out_specs=[row] + [o_spec] * n_out,
        out_shape=[jax.ShapeDtypeStruct((M, K), BF16)] + [jax.ShapeDtypeStruct((M, N), F32 if mode == "f32" else BF16)] * n_out,
        compiler_params=_cp("parallel", "arbitrary"), name=name)(h, g, w)


def _mm_nt(name, dy, w, dy_spec, w_spec, M, J, N, tm, tj, tn, mode, extra=None, parts=1):
    nk = N // tn
    o_spec = pl.BlockSpec((tm, tj), lambda i, j, k: (i, j))
    n_extra = {"f32": 0, "dact": 1, "rms": 3}[mode]
    if mode == "rms":
        assert nk == 1 and tj == J

    def body(*refs):
        dy_ref, w_ref = refs[:2]
        ex = refs[2:2 + n_extra]
        outs = refs[2 + n_extra:]
        acc_ref = outs[-1] if nk > 1 else None
        if parts == 1:
            part = _nt(dy_ref[...], w_ref[...])
        else:
            wq = tn // parts
            part = _nt(dy_ref[0], w_ref[:, 0:wq])
            for q in range(1, parts):
                part = part + _nt(dy_ref[q], w_ref[:, q * wq:(q + 1) * wq])

        def finish(acc):
            if mode == "f32":
                outs[0][...] = acc
            elif mode == "dact":
                outs[0][...] = (acc * (2.0 * ex[0][...].astype(F32))).astype(BF16)
            else:
                h_ref, g_ref, dres_ref = ex
                dh_ref, dhb_ref, dg_ref = outs[:3]
                i = pl.program_id(0)

                @pl.when(i == 0)
                def _():
                    dg_ref[...] = jnp.zeros_like(dg_ref)

                ch = _tile(tm, 256, 16)
                for c0 in range(0, tm, ch):
                    a_c = acc[c0:c0 + ch]
                    x = h_ref[c0:c0 + ch, :]
                    r = lax.rsqrt(jnp.mean(x * x, axis=-1, keepdims=True) + EPS)
                    xh = x * r
                    dxh = a_c * g_ref[...]
                    dx = r * (dxh - xh * jnp.mean(dxh * xh, axis=-1, keepdims=True))
                    keep = _row_ids(i * tm + c0, ch) >= PAD
                    dh = jnp.where(keep, dres_ref[c0:c0 + ch, :] + dx, 0.0)
                    dh_ref[c0:c0 + ch, :] = dh
                    dhb_ref[c0:c0 + ch, :] = dh.astype(BF16)
                    dg_ref[...] += jnp.sum(a_c * xh, axis=0, keepdims=True)

        if nk == 1:
            finish(part)
        else:
            k = pl.program_id(2)

            @pl.when(k == 0)
            def _():
                acc_ref[...] = part

            @pl.when(k > 0)
            def _():
                acc_ref[...] += part

            @pl.when(k == nk - 1)
            def _():
                finish(acc_ref[...])

    in_specs = [dy_spec(tm, tn), w_spec(tj, tn)]
    args = [dy, w]
    scratch = [pltpu.VMEM((tm, tj), F32)] if nk > 1 else []
    if mode == "rms":
        vec = pl.BlockSpec((1, J), lambda i, j, k: (0, 0))
        h, g, dres = extra
        res = pl.pallas_call(
            body, grid=(M // tm, 1, 1), in_specs=in_specs + [o_spec, vec, o_spec], out_specs=[o_spec, o_spec, vec],
            out_shape=[jax.ShapeDtypeStruct((M, J), F32), jax.ShapeDtypeStruct((M, J), BF16),
                       jax.ShapeDtypeStruct((1, J), F32)],
            compiler_params=_cp("arbitrary", "arbitrary", "arbitrary"), name=name)(*args, h, g, dres)
        return res
    if mode == "dact":
        in_specs.append(o_spec)
        args.append(extra)
    return pl.pallas_call(
        body, grid=(M // tm, J // tj, nk), in_specs=in_specs, out_specs=[o_spec],
        out_shape=[jax.ShapeDtypeStruct((M, J), BF16 if mode == "dact" else F32)],
        scratch_shapes=scratch, compiler_params=_cp("parallel", "parallel", "arbitrary"), name=name)(*args)[0]


def _mm_tn(name, x, dy, x_spec, dy_spec, o_spec, o_shape, T, K, N, tt, tk, tn):
    nt = T // tt

    def body(x_ref, dy_ref, o_ref, *acc):
        part = _tn(x_ref[...], dy_ref[...])
        if nt == 1:
            o_ref[...] = part.astype(BF16)
            return
        acc_ref = acc[0]
        t = pl.program_id(2)

        @pl.when(t == 0)
        def _():
            acc_ref[...] = part

        @pl.when(t > 0)
        def _():
            acc_ref[...] += part

        @pl.when(t == nt - 1)
        def _():
            o_ref[...] = acc_ref[...].astype(BF16)

    return pl.pallas_call(
        body, grid=(K // tk, N // tn, nt), in_specs=[x_spec(tt, tk), dy_spec(tt, tn)], out_specs=o_spec(tk, tn),
        out_shape=jax.ShapeDtypeStruct(o_shape, BF16), scratch_shapes=[pltpu.VMEM((tk, tn), F32)] if nt > 1 else [],
        compiler_params=_cp("parallel", "parallel", "arbitrary"), name=name)(x, dy)


def _pool_counts(base, n, w):
    pos = _row_ids(base, n) - PAD
    return jnp.clip(pos + 1, 1, w).astype(F32)


def _shifted_copies(buf, rows):
    buf[0, rows:rows + 8, :] = jnp.zeros((8, buf.shape[2]), F32)

    def blk(s, carry):
        b = pl.multiple_of(s * HALO, HALO)
        win = buf[0, pl.ds(b, HALO + 8), :]
        for r in range(1, 8):
            buf[r, pl.ds(b, HALO), :] = win[r:r + HALO]
        return carry

    lax.fori_loop(0, rows // HALO, blk, 0)


def _ev_fwd(u, cw, cb, lg, lb, pw, pb, ps):
    T = u.shape[0]
    C = 512
    tm = _tile(T, EV_ROWS, HALO)
    nsub = tm // HALO
    hb = tm // HALO

    def body(val_ref, gate_ref, pin_ref, valh_ref, gateh_ref, pinh_ref, cw_ref, cb_ref, lg_ref, lb_ref, pw_ref,
             pb_ref, ps_ref, yab_ref, yc_ref, a_ext, p_ext, d_buf):
        i = pl.program_id(0)
        nf = (i > 0).astype(F32)
        a_ext[0, 0:HALO, :] = valh_ref[...] * jax.nn.sigmoid(gateh_ref[...]) * nf
        a_ext[0, HALO:HALO + tm, :] = val_ref[...] * jax.nn.sigmoid(gate_ref[...])
        p_ext[0:HALO, :] = pinh_ref[...] * nf
        p_ext[HALO:, :] = pin_ref[...]
        _shifted_copies(a_ext, tm + HALO)

        def sub(s, carry):
            base = pl.multiple_of(s * HALO, HALO)
            acc = jnp.zeros((HALO, C), F32) + cb_ref[...]
            for j in range(CONV_WIDTH):
                off = 2 + j
                acc = acc + cw_ref[pl.ds(j, 1), :] * a_ext[off % 8, pl.ds(pl.multiple_of(base + off // 8 * 8, 8), HALO), :]
            yc_ref[pl.ds(base, HALO), :] = acc
            mu = jnp.mean(acc, axis=-1, keepdims=True)
            yc = acc - mu
            rstd = lax.rsqrt(jnp.mean(yc * yc, axis=-1, keepdims=True) + EPS)
            z = (yc * rstd) * lg_ref[...] + lb_ref[...]
            yab_ref[pl.ds(base, HALO), 0:C] = (z * jax.nn.sigmoid(z)).astype(BF16)
            pwin = p_ext[pl.ds(base, 2 * HALO), :]
            for gi, w in enumerate(POOL_WINDOWS):
                lo, hi = gi * HEAD, (gi + 1) * HEAD
                x = pwin[HALO:, lo:hi]
                tot = x
                for k in range(1, w):
                    tot = tot + pwin[HALO - k:2 * HALO - k, lo:hi]
                cnt = _pool_counts(i * tm + base, HALO, w)
                d_buf[pl.ds(base, HALO), lo:hi] = (tot / cnt - x).astype(BF16)
            return carry

        lax.fori_loop(0, nsub, sub, 0)
        for gi in range(len(POOL_WINDOWS)):
            lo, hi = gi * HEAD, (gi + 1) * HEAD
            y = _nn(d_buf[:, lo:hi], pw_ref[gi]) + pb_ref[:, lo:hi]
            yab_ref[:, C + lo:C + hi] = (y * ps_ref[:, lo:hi]).astype(BF16)

    def main(c):
        return pl.BlockSpec((tm, C), lambda i: (i, c))

    def halo(c):
        return pl.BlockSpec((HALO, C), lambda i: (jnp.maximum(i * hb - 1, 0), c))

    vec = pl.BlockSpec((1, C), lambda i: (0, 0))
    return pl.pallas_call(
        body, grid=(T // tm,),
        in_specs=[main(0), main(1), main(2), halo(0), halo(1), halo(2),
                  pl.BlockSpec((32, C), lambda i: (0, 0)), vec, vec, vec,
                  pl.BlockSpec((4, HEAD, HEAD), lambda i: (0, 0, 0)), vec, vec],
        out_specs=[pl.BlockSpec((tm, 2 * C), lambda i: (i, 0)), pl.BlockSpec((tm, C), lambda i: (i, 0))],
        out_shape=[jax.ShapeDtypeStruct((T, 2 * C), BF16), jax.ShapeDtypeStruct((T, C), F32)],
        scratch_shapes=[pltpu.VMEM((8, tm + HALO + 8, C), F32), pltpu.VMEM((tm + HALO, C), F32),
                        pltpu.VMEM((tm, C), BF16)],
        compiler_params=_cp("parallel"), name="ev_fwd")(u, u, u, u, u, u, cw, cb, lg, lb, pw, pb, ps)


def _ev_bwd(dyab, yc, u, cw, lg, lb, pw, pwt, pb, ps):
    T = u.shape[0]
    C = 512
    tm = _tile(T, EV_ROWS, HALO)
    nsub = tm // HALO
    hb = tm // HALO
    nblk = T // tm
    E = tm + HALO

    def body(dya_ref, dyb_ref, dyah_ref, dybh_ref, yc_ref, ych_ref, val_ref, gate_ref, pin_ref, valh_ref, gateh_ref,
             pinh_ref, cw_ref, lg_ref, lb_ref, pw_ref, pwt_ref, pb_ref, ps_ref,
             du_ref, dcw_ref, dvec_ref, dpw_ref,
             dy_ext, a_ext, p_ext, ddc_ext, dd_buf, d_buf, dpre_buf, dcw_acc, vec_acc):
        i = pl.program_id(0)
        nf = (i > 0).astype(F32)
        nl = (i < nblk - 1).astype(F32)

        @pl.when(i == 0)
        def _():
            dcw_ref[...] = jnp.zeros_like(dcw_ref)
            dvec_ref[...] = jnp.zeros_like(dvec_ref)
            dpw_ref[...] = jnp.zeros_like(dpw_ref)

        dcw_acc[...] = jnp.zeros_like(dcw_acc)
        vec_acc[...] = jnp.zeros_like(vec_acc)
        a_ext[0, 0:HALO, :] = valh_ref[...] * jax.nn.sigmoid(gateh_ref[...]) * nf
        a_ext[0, HALO:E, :] = val_ref[...] * jax.nn.sigmoid(gate_ref[...])
        p_ext[0:HALO, :] = pinh_ref[...] * nf
        p_ext[HALO:, :] = pin_ref[...]
        _shifted_copies(a_ext, E)

        def ln_bwd(y, dya, main):
            mu = jnp.mean(y, axis=-1, keepdims=True)
            ycen = y - mu
            rstd = lax.rsqrt(jnp.mean(ycen * ycen, axis=-1, keepdims=True) + EPS)
            yh = ycen * rstd
            z = yh * lg_ref[...] + lb_ref[...]
            sz = jax.nn.sigmoid(z)
            dz = dya * _dsilu(z, sz)
            dyh = dz * lg_ref[...]
            dy = rstd * (dyh - jnp.mean(dyh, axis=-1, keepdims=True) - yh * jnp.mean(dyh * yh, axis=-1, keepdims=True))
            if main:
                vec_acc[1] += jnp.sum((dz * yh).reshape(HALO // 8, 8, C), axis=0)
                vec_acc[2] += jnp.sum(dz.reshape(HALO // 8, 8, C), axis=0)
                vec_acc[0] += jnp.sum(dy.reshape(HALO // 8, 8, C), axis=0)
            return dy

        def pool_dd(dyb, base, main):
            dpre = dyb * ps_ref[...]
            for gi, w in enumerate(POOL_WINDOWS):
                lo, hi = gi * HEAD, (gi + 1) * HEAD
                dd = _nn(dpre[:, lo:hi].astype(BF16), pwt_ref[gi])
                cnt = _pool_counts(i * tm + base, HALO, w)
                ddc_ext[pl.ds(base, HALO), lo:hi] = dd / cnt
                if main:
                    dd_buf[pl.ds(base, HALO), lo:hi] = dd
            if main:
                dpre_buf[pl.ds(base, HALO), :] = dpre.astype(BF16)
                vec_acc[4] += jnp.sum(dpre.reshape(HALO // 8, 8, C), axis=0)

        def p1(s, carry):
            base = pl.multiple_of(s * HALO, HALO)
            dy_ext[0, pl.ds(base, HALO), :] = ln_bwd(yc_ref[pl.ds(base, HALO), :], dya_ref[pl.ds(base, HALO), :], True)
            pool_dd(dyb_ref[pl.ds(base, HALO), :], base, True)
            return carry

        lax.fori_loop(0, nsub, p1, 0)
        dy_ext[0, tm:E, :] = ln_bwd(ych_ref[...], dyah_ref[...], False) * nl
        _shifted_copies(dy_ext, E)
        dpre_h = dybh_ref[...] * ps_ref[...] * nl
        for gi, w in enumerate(POOL_WINDOWS):
            lo, hi = gi * HEAD, (gi + 1) * HEAD
            dd = _nn(dpre_h[:, lo:hi].astype(BF16), pwt_ref[gi])
            ddc_ext[tm:, lo:hi] = dd / _pool_counts(i * tm + tm, HALO, w)

        def p2(s, carry):
            base = pl.multiple_of(s * HALO, HALO)
            dy_m = dy_ext[0, pl.ds(base, HALO), :]
            da = jnp.zeros((HALO, C), F32)
            for j in range(CONV_WIDTH):
                sh = CONV_WIDTH - 1 - j
                off = 2 + j
                da = da + cw_ref[pl.ds(j, 1), :] * dy_ext[sh % 8, pl.ds(pl.multiple_of(base + sh // 8 * 8, 8), HALO), :]
                a_j = a_ext[off % 8, pl.ds(pl.multiple_of(base + off // 8 * 8, 8), HALO), :]
                dcw_acc[j] += jnp.sum((dy_m * a_j).reshape(HALO // 8, 8, C), axis=0)
            v = val_ref[pl.ds(base, HALO), :]
            g = gate_ref[pl.ds(base, HALO), :]
            sg = jax.nn.sigmoid(g)
            du_ref[pl.ds(base, HALO), 0:C] = (da * sg).astype(BF16)
            du_ref[pl.ds(base, HALO), C:2 * C] = (da * v * sg * (1.0 - sg)).astype(BF16)
            pwin = p_ext[pl.ds(base, 2 * HALO), :]
            cwin = ddc_ext[pl.ds(base, 2 * HALO), :]
            for gi, w in enumerate(POOL_WINDOWS):
                lo, hi = gi * HEAD, (gi + 1) * HEAD
                x = pwin[HALO:, lo:hi]
                tot = x
                back = cwin[0:HALO, lo:hi]
                for k in range(1, w):
                    tot = tot + pwin[HALO - k:2 * HALO - k, lo:hi]
                    back = back + cwin[k:k + HALO, lo:hi]
                cnt = _pool_counts(i * tm + base, HALO, w)
                d_buf[pl.ds(base, HALO), lo:hi] = (tot / cnt - x).astype(BF16)
                du_ref[pl.ds(base, HALO), 2 * C + lo:2 * C + hi] = (back - dd_buf[pl.ds(base, HALO), lo:hi]).astype(BF16)
            return carry

        lax.fori_loop(0, nsub, p2, 0)
        for gi in range(len(POOL_WINDOWS)):
            lo, hi = gi * HEAD, (gi + 1) * HEAD
            pre = _nn(d_buf[:, lo:hi], pw_ref[gi]) + pb_ref[:, lo:hi]
            vec_acc[3, :, lo:hi] += jnp.sum((dyb_ref[:, lo:hi] * pre).reshape(tm // 8, 8, HEAD), axis=0)
            dpw_ref[gi] += _tn(d_buf[:, lo:hi], dpre_buf[:, lo:hi])
        for j in range(CONV_WIDTH):
            dcw_ref[pl.ds(j, 1), :] += jnp.sum(dcw_acc[j], axis=0, keepdims=True)
        for r in range(5):
            dvec_ref[pl.ds(r, 1), :] += jnp.sum(vec_acc[r], axis=0, keepdims=True)

    def main(c, width=C):
        return pl.BlockSpec((tm, width), lambda i: (i, c))

    def prev(c):
        return pl.BlockSpec((HALO, C), lambda i: (jnp.maximum(i * hb - 1, 0), c))

    def nxt(c):
        return pl.BlockSpec((HALO, C), lambda i: (jnp.minimum((i + 1) * hb, T // HALO - 1), c))

    vec = pl.BlockSpec((1, C), lambda i: (0, 0))
    mat = pl.BlockSpec((4, HEAD, HEAD), lambda i: (0, 0, 0))
    return pl.pallas_call(
        body, grid=(nblk,),
        in_specs=[main(0), main(1), nxt(0), nxt(1), main(0), nxt(0), main(0), main(1), main(2), prev(0), prev(1),
                  prev(2), pl.BlockSpec((32, C), lambda i: (0, 0)), vec, vec, mat, mat, vec, vec],
        out_specs=[pl.BlockSpec((tm, 3 * C), lambda i: (i, 0)), pl.BlockSpec((32, C), lambda i: (0, 0)),
                   pl.BlockSpec((8, C), lambda i: (0, 0)), mat],
        out_shape=[jax.ShapeDtypeStruct((T, 3 * C), BF16), jax.ShapeDtypeStruct((32, C), F32),
                   jax.ShapeDtypeStruct((8, C), F32), jax.ShapeDtypeStruct((4, HEAD, HEAD), F32)],
        scratch_shapes=[pltpu.VMEM((8, E + 8, C), F32), pltpu.VMEM((8, E + 8, C), F32), pltpu.VMEM((E, C), F32),
                        pltpu.VMEM((E, C), F32), pltpu.VMEM((tm, C), F32), pltpu.VMEM((tm, C), BF16),
                        pltpu.VMEM((tm, C), BF16), pltpu.VMEM((32, 8, C), F32), pltpu.VMEM((8, 8, C), F32)],
        compiler_params=_cp("arbitrary"), name="ev_bwd")(
            dyab, dyab, dyab, dyab, yc, yc, u, u, u, u, u, u, cw, lg, lb, pw, pwt, pb, ps)


def _cumsum_rows(x, reverse=False):
    n = x.shape[0]
    rid = lax.broadcasted_iota(jnp.int32, (n, 1), 0)
    k = 1
    while k < n:
        if reverse:
            sh = jnp.where(rid < n - k, pltpu.roll(x, n - k, 0), 0.0)
        else:
            sh = jnp.where(rid >= k, pltpu.roll(x, k, 0), 0.0)
        x = x + sh
        k *= 2
    return x


def _hgrn_gates(qr, fr, lbv):
    sq = jax.nn.sigmoid(qr)
    sg = jax.nn.sigmoid(fr)
    fg = lbv + (1.0 - lbv) * sg
    return qr * sq, sq, sg, fg, 1.0 - fg, jnp.log(fg)


def _hgrn_fwd(u, lbv, gn):
    T = u.shape[0]
    H = 8
    RB = _tile(T, HGRN_ROWS, CHUNK)
    NC = RB // CHUNK
    NS = CHUNK // SUB

    HP = HGRN_HEADS_FWD
    W = HP * HEAD

    def body(q_ref, f_ref, i_ref, g_ref, lb_ref, gn_ref, y_ref, o_ref, s0_ref, st, qs, ks, bs, vs, os_):
        rb = pl.program_id(1)

        @pl.when(rb == 0)
        def _():
            st[...] = jnp.zeros_like(st)

        t8 = lax.broadcasted_iota(jnp.int32, (8, 1), 0)

        def head(hh, c, rows):
            sl = slice(hh * HEAD, (hh + 1) * HEAD)
            q, _, _, _, kk, lf = _hgrn_gates(q_ref[rows, sl], f_ref[rows, sl], lb_ref[:, sl])
            v = i_ref[rows, sl]
            b = _cumsum_rows(lf)
            qs[hh] = q
            ks[hh] = kk
            bs[hh] = b
            vs[hh] = v
            st0 = st[hh]
            s0_ref[hh, c] = st0
            os_[hh] = _nt((q * jnp.exp(b)).astype(BF16), st0.astype(BF16))
            for I in range(NS):
                lo = I * SUB
                qI = qs[hh, lo:lo + SUB, :]
                bI = bs[hh, lo:lo + SUB, :]
                oI = jnp.zeros((SUB, HEAD), F32)
                if I > 0:
                    bprev = bs[hh, pl.ds(lo - 1, 1), :]
                    qt = _r16(qI * jnp.exp(bI - bprev))
                    kt = _r16(ks[hh, 0:lo, :] * jnp.exp(bprev - bs[hh, 0:lo, :]))
                    A = _nt(qt, kt)
                    oI = oI + _nn(_r16(A), _r16(vs[hh, 0:lo, :]))
                od = [jnp.zeros((8, HEAD), F32) for _ in range(SUB // 8)]
                for s in range(SUB):
                    row = pl.ds(lo + s, 1)
                    brow, krow, vrow = bs[hh, row, :], ks[hh, row, :], vs[hh, row, :]
                    for ti in range(SUB // 8):
                        o8 = 8 * ti
                        if s > o8 + 7:
                            continue
                        d = bI[o8:o8 + 8] - brow
                        if s > o8:
                            d = jnp.where(t8 >= s - o8, d, NEG)
                        col = jnp.sum(qI[o8:o8 + 8] * jnp.exp(d) * krow, axis=1, keepdims=True)
                        od[ti] = od[ti] + col * vrow
                os_[hh, lo:lo + SUB, :] += oI + jnp.concatenate(od, axis=0)
            blast = bs[hh, pl.ds(CHUNK - 1, 1), :]
            kh = kk * jnp.exp(blast - b)
            st[hh] = st0 * jnp.exp(blast) + _tn(v.astype(BF16), kh.astype(BF16))
            o = os_[hh]
            o_ref[rows, sl] = o
            rr = lax.rsqrt(jnp.mean(o * o, axis=-1, keepdims=True) + EPS)
            gr = g_ref[rows, sl]
            y_ref[rows, sl] = (((o * rr) * gn_ref[...]) * (gr * jax.nn.sigmoid(gr))).astype(BF16)

        def chunk(c, carry):
            rows = pl.ds(pl.multiple_of(c * CHUNK, CHUNK), CHUNK)
            for hh in range(HP):
                head(hh, c, rows)
            return carry

        lax.fori_loop(0, NC, chunk, 0)

    def blk(q):
        return pl.BlockSpec((RB, W), lambda h, r: (r, q * (H // HP) + h))

    sc = lambda: pltpu.VMEM((HP, CHUNK, HEAD), F32)
    return pl.pallas_call(
        body, grid=(H // HP, T // RB),
        in_specs=[blk(0), blk(1), blk(2), blk(3), pl.BlockSpec((1, W), lambda h, r: (0, h)),
                  pl.BlockSpec((1, HEAD), lambda h, r: (0, 0))],
        out_specs=[pl.BlockSpec((RB, W), lambda h, r: (r, h)), pl.BlockSpec((RB, W), lambda h, r: (r, h)),
                   pl.BlockSpec((HP, NC, HEAD, HEAD), lambda h, r: (h, r, 0, 0))],
        out_shape=[jax.ShapeDtypeStruct((T, H * HEAD), BF16), jax.ShapeDtypeStruct((T, H * HEAD), F32),
                   jax.ShapeDtypeStruct((H, T // CHUNK, HEAD, HEAD), F32)],
        scratch_shapes=[pltpu.VMEM((HP, HEAD, HEAD), F32), sc(), sc(), sc(), sc(), sc()],
        compiler_params=_cp("parallel", "arbitrary"), name="hgrn_fwd")(u, u, u, u, lbv, gn)


def _hgrn_bwd(dy, o, s0, u, lbv, gn):
    T = u.shape[0]
    H = 8
    RB = _tile(T, HGRN_ROWS, CHUNK)
    NB = T // RB
    NC = RB // CHUNK
    NS = CHUNK // SUB

    def body(q_ref, f_ref, i_ref, g_ref, lb_ref, gn_ref, o_ref, dy_ref, s0_ref, du_ref, dlb_ref, dgn_ref,
             dst, qs, ks, bs, vs, dos, dqs, dks, dki, dvs, dbs):
        rb = pl.program_id(1)

        @pl.when(rb == 0)
        def _():
            dst[...] = jnp.zeros_like(dst)
            dlb_ref[...] = jnp.zeros_like(dlb_ref)
            dgn_ref[...] = jnp.zeros_like(dgn_ref)

        t8 = lax.broadcasted_iota(jnp.int32, (8, 1), 0)
        lane = lax.broadcasted_iota(jnp.int32, (8, HEAD), 1)
        gnv = gn_ref[...]

        def head(hh, c, rows):
            sl = slice(hh * HEAD, (hh + 1) * HEAD)
            lbv_ = lb_ref[:, sl]
            qr = q_ref[rows, sl]
            q, sq, sg, fg, kk, lf = _hgrn_gates(qr, f_ref[rows, sl], lbv_)
            v = i_ref[rows, sl]
            gr = g_ref[rows, sl]
            b = _cumsum_rows(lf)
            eb = jnp.exp(b)
            ov = o_ref[rows, sl]
            dyv = dy_ref[rows, sl]
            rr = lax.rsqrt(jnp.mean(ov * ov, axis=-1, keepdims=True) + EPS)
            oh = ov * rr
            gs = jax.nn.sigmoid(gr)
            dgr = dyv * (oh * gnv) * _dsilu(gr, gs)
            dnrm = dyv * (gr * gs)
            dgn_ref[hh] += jnp.sum(dnrm * oh, axis=0, keepdims=True)
            t1 = dnrm * gnv
            do = rr * (t1 - oh * jnp.mean(t1 * oh, axis=-1, keepdims=True))
            qs[hh] = q
            ks[hh] = kk
            bs[hh] = b
            vs[hh] = v
            dos[hh] = do
            st0 = s0_ref[hh, c]
            dS = dst[hh]
            do_b = do.astype(BF16)
            blast = bs[hh, pl.ds(CHUNK - 1, 1), :]
            elast = jnp.exp(blast - b)
            dq_inter = _nn(do_b, st0.astype(BF16)) * eb
            dqs[hh] = dq_inter
            dbs[hh] = q * dq_inter
            kh = kk * elast
            dvs[hh] = _nt(kh.astype(BF16), dS.astype(BF16))
            dk_inter = _nn(v.astype(BF16), dS.astype(BF16)) * elast
            dki[hh] = dk_inter
            dks[hh] = jnp.zeros((CHUNK, HEAD), F32)
            for I in range(NS):
                lo = I * SUB
                qI = qs[hh, lo:lo + SUB, :]
                bI = bs[hh, lo:lo + SUB, :]
                doI = dos[hh, lo:lo + SUB, :]
                dqI = jnp.zeros((SUB, HEAD), F32)
                dbI = jnp.zeros((SUB, HEAD), F32)
                if I > 0:
                    bprev = bs[hh, pl.ds(lo - 1, 1), :]
                    eq = jnp.exp(bI - bprev)
                    ek = jnp.exp(bprev - bs[hh, 0:lo, :])
                    qt = _r16(qI * eq)
                    kt = _r16(ks[hh, 0:lo, :] * ek)
                    A = _r16(_nt(qt, kt))
                    doI_b = _r16(doI)
                    dA = _r16(_nt(doI_b, _r16(vs[hh, 0:lo, :])))
                    dvs[hh, 0:lo, :] += _tn(A, doI_b)
                    dqt = _nn(dA, kt)
                    dkt = _tn(dA, qt)
                    dqI = dqI + dqt * eq
                    dbI = dbI + qt.astype(F32) * dqt
                    dks[hh, 0:lo, :] += dkt * ek
                    dbs[hh, 0:lo, :] -= kt.astype(F32) * dkt
                dq_t = [jnp.zeros((8, HEAD), F32) for _ in range(SUB // 8)]
                a_t = [jnp.zeros((8, HEAD), F32) for _ in range(SUB // 8)]
                for s in range(SUB):
                    row = pl.ds(lo + s, 1)
                    brow, krow, vrow = bs[hh, row, :], ks[hh, row, :], vs[hh, row, :]
                    dk_s = None
                    for ti in range(SUB // 8):
                        o8 = 8 * ti
                        if s > o8 + 7:
                            continue
                        d = bI[o8:o8 + 8] - brow
                        if s > o8:
                            d = jnp.where(t8 >= s - o8, d, NEG)
                        Es = jnp.exp(d)
                        qE = qI[o8:o8 + 8] * Es
                        col = jnp.sum(qE * krow, axis=1, keepdims=True)
                        a_t[ti] = jnp.where(lane == s, col, a_t[ti])
                        dcol = jnp.sum(doI[o8:o8 + 8] * vrow, axis=1, keepdims=True)
                        dq_t[ti] = dq_t[ti] + (dcol * Es) * krow
                        part = jnp.sum(dcol * qE, axis=0, keepdims=True)
                        dk_s = part if dk_s is None else dk_s + part
                    dks[hh, row, :] += dk_s
                    dbs[hh, row, :] -= krow * dk_s
                a_d = jnp.concatenate(a_t, axis=0)
                dq_d = jnp.concatenate(dq_t, axis=0)
                dvs[hh, lo:lo + SUB, :] += _tn(a_d, doI)[0:SUB]
                dqI = dqI + dq_d
                dbI = dbI + qI * dq_d
                dqs[hh, lo:lo + SUB, :] += dqI
                dbs[hh, lo:lo + SUB, :] += dbI
            kdk = kk * dki[hh]
            excl = _cumsum_rows(kdk) - kdk
            suff = _cumsum_rows(dbs[hh], reverse=True)
            gdec = jnp.sum(dS * st0, axis=0, keepdims=True) * jnp.exp(blast)
            dlf = suff + excl + gdec
            dk = dks[hh] + dki[hh]
            dfg = dlf / fg - dk
            dlb_ref[:, sl] += jnp.sum(dfg * (1.0 - sg), axis=0, keepdims=True)
            du_ref[0, rows, sl] = (dqs[hh] * _dsilu(qr, sq)).astype(BF16)
            du_ref[1, rows, sl] = (dfg * (1.0 - lbv_) * sg * (1.0 - sg)).astype(BF16)
            du_ref[2, rows, sl] = dvs[hh].astype(BF16)
            du_ref[3, rows, sl] = dgr.astype(BF16)
            dst[hh] = dS * jnp.exp(blast) + _tn(do_b, (q * eb).astype(BF16))

        def chunk(cc, carry):
            c = NC - 1 - cc
            rows = pl.ds(pl.multiple_of(c * CHUNK, CHUNK), CHUNK)
            for hh in range(HP):
                head(hh, c, rows)
            return carry

        lax.fori_loop(0, NC, chunk, 0)

    HP = HGRN_HEADS_BWD
    W = HP * HEAD

    def blk(qd):
        return pl.BlockSpec((RB, W), lambda h, r: (NB - 1 - r, qd * (H // HP) + h))

    hblk = pl.BlockSpec((RB, W), lambda h, r: (NB - 1 - r, h))
    sc = lambda: pltpu.VMEM((HP, CHUNK, HEAD), F32)
    return pl.pallas_call(
        body, grid=(H // HP, NB),
        in_specs=[blk(0), blk(1), blk(2), blk(3), pl.BlockSpec((1, W), lambda h, r: (0, h)),
                  pl.BlockSpec((1, HEAD), lambda h, r: (0, 0)), hblk, hblk,
                  pl.BlockSpec((HP, NC, HEAD, HEAD), lambda h, r: (h, NB - 1 - r, 0, 0))],
        out_specs=[pl.BlockSpec((4, RB, W), lambda h, r: (0, NB - 1 - r, h)),
                   pl.BlockSpec((1, W), lambda h, r: (0, h)), pl.BlockSpec((HP, 1, HEAD), lambda h, r: (h, 0, 0))],
        out_shape=[jax.ShapeDtypeStruct((4, T, H * HEAD), BF16), jax.ShapeDtypeStruct((1, H * HEAD), F32),
                   jax.ShapeDtypeStruct((H, 1, HEAD), F32)],
        scratch_shapes=[pltpu.VMEM((HP, HEAD, HEAD), F32)] + [sc() for _ in range(10)],
        compiler_params=_cp("parallel", "arbitrary"), name="hgrn_bwd")(u, u, u, u, lbv, gn, o, dy, s0)


def _softmax_rows(p_ref, L):
    rows = [p_ref[pl.ds(l, 1), :] for l in range(L)]
    m = rows[0]
    for r in rows[1:]:
        m = jnp.maximum(m, r)
    e = [jnp.exp(r - m) for r in rows]
    tot = e[0]
    for t in e[1:]:
        tot = tot + t
    return [t / tot for t in e]


def _lb_fwd(lbp):
    L, D = lbp.shape

    def body(p_ref, o_ref):
        sm = _softmax_rows(p_ref, L)
        acc = jnp.zeros((1, D), F32)
        o_ref[pl.ds(0, 1), :] = acc
        for l in range(1, L):
            acc = acc + sm[l]
            o_ref[pl.ds(l, 1), :] = acc

    return pl.pallas_call(body, out_shape=jax.ShapeDtypeStruct((L, D), F32), name="lb_fwd")(lbp)


def _lb_bwd(lbp, dlb):
    L, D = lbp.shape

    def body(p_ref, d_ref, o_ref):
        sm = _softmax_rows(p_ref, L)
        dsm = [jnp.zeros((1, D), F32)]
        for i in range(1, L):
            t = jnp.zeros((1, D), F32)
            for l in range(i, L):
                t = t + d_ref[pl.ds(l, 1), :]
            dsm.append(t)
        dot = jnp.zeros((1, D), F32)
        for i in range(L):
            dot = dot + dsm[i] * sm[i]
        for i in range(L):
            o_ref[pl.ds(i, 1), :] = sm[i] * (dsm[i] - dot)

    return pl.pallas_call(body, out_shape=jax.ShapeDtypeStruct((L, D), F32), name="lb_bwd")(lbp, dlb)


def _my_pos():
    return lax.axis_index("x"), lax.axis_index("y"), lax.axis_index("c")


def _peer(mask):
    x, y, c = _my_pos()
    mx, my, mc = (mask >> 2) & 1, (mask >> 1) & 1, mask & 1
    px = (1 - x) if mx else x
    py = (1 - y) if my else y
    pc = (1 - c) if mc else c
    return (px, py, pc), 4 * px + 2 * py + pc


def _all_gather(shards):
    n = len(shards)

    def body(*refs):
        ins, outs = refs[:n], refs[n:2 * n]
        send_sems, recv_sems, local_sems = refs[2 * n:]
        x, y, c = _my_pos()
        me = 4 * x + 2 * y + c
        local = [pltpu.make_async_copy(ins[a], outs[a].at[:, me], local_sems.at[a]) for a in range(n)]
        for cp in local:
            cp.start()
        sends = []
        for m in range(1, N_DEV):
            peer, _ = _peer(m)
            for a in range(n):
                cp = pltpu.make_async_remote_copy(
                    src_ref=ins[a], dst_ref=outs[a].at[:, me], send_sem=send_sems.at[a, m - 1],
                    recv_sem=recv_sems.at[a, m - 1], device_id=peer, device_id_type=MESH)
                cp.start()
                sends.append(cp)
        for m in range(1, N_DEV):
            peer, pid = _peer(m)
            for a in range(n):
                pltpu.make_async_remote_copy(
                    src_ref=ins[a], dst_ref=outs[a].at[:, pid], send_sem=send_sems.at[a, m - 1],
                    recv_sem=recv_sems.at[a, m - 1], device_id=peer, device_id_type=MESH).wait_recv()
        for cp in sends:
            cp.wait_send()
        for cp in local:
            cp.wait()

    out_shape = [jax.ShapeDtypeStruct((s.shape[0], N_DEV) + s.shape[1:], s.dtype) for s in shards]
    return pl.pallas_call(
        body, in_specs=[ANY] * n, out_specs=[ANY] * n, out_shape=out_shape,
        scratch_shapes=[pltpu.SemaphoreType.DMA((n, N_DEV - 1)), pltpu.SemaphoreType.DMA((n, N_DEV - 1)),
                        pltpu.SemaphoreType.DMA((n,))],
        name="all_gather_weights")(*shards)


def _exchange(grads, groups):
    n = len(grads)
    ng = 1 + max(g for g, _ in groups)
    layers = [1 + max(l for g, l in groups if g == gi) for gi in range(ng)]
    shapes = [None] * ng
    for a, (g, l) in enumerate(groups):
        shapes[g] = grads[a].shape[1:]

    def body(*refs):
        ins, outs = refs[:n], refs[n:n + ng]
        send_sems, recv_sems, local_sems = refs[n + ng:]
        x, y, c = _my_pos()
        me = 4 * x + 2 * y + c
        local = []
        for a, (g, l) in enumerate(groups):
            cp = pltpu.make_async_copy(ins[a].at[me], outs[g].at[me, l], local_sems.at[a])
            cp.start()
            local.append(cp)
        sends = []
        for m in range(1, N_DEV):
            peer, pid = _peer(m)
            for a, (g, l) in enumerate(groups):
                cp = pltpu.make_async_remote_copy(
                    src_ref=ins[a].at[pid], dst_ref=outs[g].at[me, l], send_sem=send_sems.at[a, m - 1],
                    recv_sem=recv_sems.at[a, m - 1], device_id=peer, device_id_type=MESH)
                cp.start()
                sends.append(cp)
        for m in range(1, N_DEV):
            peer, pid = _peer(m)
            for a, (g, l) in enumerate(groups):
                pltpu.make_async_remote_copy(
                    src_ref=ins[a].at[pid], dst_ref=outs[g].at[pid, l], send_sem=send_sems.at[a, m - 1],
                    recv_sem=recv_sems.at[a, m - 1], device_id=peer, device_id_type=MESH).wait_recv()
        for cp in sends:
            cp.wait_send()
        for cp in local:
            cp.wait()

    out_shape = [jax.ShapeDtypeStruct((N_DEV, layers[g]) + shapes[g], grads[[gg for gg, _ in groups].index(g)].dtype)
                 for g in range(ng)]
    return pl.pallas_call(
        body, in_specs=[ANY] * n, out_specs=[ANY] * ng, out_shape=out_shape,
        scratch_shapes=[pltpu.SemaphoreType.DMA((n, N_DEV - 1)), pltpu.SemaphoreType.DMA((n, N_DEV - 1)),
                        pltpu.SemaphoreType.DMA((n,))],
        name="exchange_grads")(*grads)


HBM_SPEC = pl.BlockSpec(memory_space=pltpu.HBM)
SEM_SPEC = pl.BlockSpec(memory_space=pltpu.SEMAPHORE)
EFFECT = pltpu.SideEffectType.DATAFLOW_SIDE_EFFECTING


def _hbm(a):
    return pltpu.with_memory_space_constraint(a, pltpu.HBM)


def _landing(block_shape, dtype, axis=0):
    if axis == 0:
        return lax.empty((N_DEV,) + tuple(block_shape), dtype)
    rows, n = block_shape
    return lax.empty((rows, N_DEV * n), dtype)


def _slot(ref, i):
    if len(ref.shape) == 2:
        n = ref.shape[1] // N_DEV
        return ref.at[:, pl.ds(i * n, n)]
    return ref.at[i]


def _push_start(name, srcs, lands, whole, groups):
    n = len(srcs)
    ng = 1 + max(groups)
    cnt = [groups.count(g) for g in range(ng)]
    idx = [groups[:a].count(groups[a]) for a in range(n)]

    def body(*refs):
        src_refs, land_refs = refs[:n], refs[n:2 * n]
        sems = refs[2 * n:2 * n + 3 * ng]
        token = refs[-1]
        x, y, c = _my_pos()
        me = 4 * x + 2 * y + c
        for a in range(n):
            g = groups[a]
            for m in range(1, N_DEV):
                peer, pid = _peer(m)
                pltpu.make_async_remote_copy(
                    src_ref=src_refs[a] if whole else src_refs[a].at[pid], dst_ref=_slot(land_refs[a], me),
                    send_sem=sems[3 * g].at[idx[a] * (N_DEV - 1) + m - 1],
                    recv_sem=sems[3 * g + 1].at[idx[a] * (N_DEV - 1) + m - 1],
                    device_id=peer, device_id_type=MESH).start()
            pltpu.make_async_copy(src_refs[a] if whole else src_refs[a].at[me], _slot(land_refs[a], me),
                                  sems[3 * g + 2].at[idx[a]]).start()
        token[...] = jnp.zeros_like(token)

    sem_shapes = []
    for g in range(ng):
        sem_shapes += [pltpu.SemaphoreType.DMA((cnt[g] * (N_DEV - 1),))] * 2 + [pltpu.SemaphoreType.DMA((cnt[g],))]
    thru = [pltpu.HBM(s.shape, s.dtype) for s in list(srcs) + list(lands)]
    res = pl.pallas_call(
        body, name=name,
        out_shape=tuple(sem_shapes + thru + [jax.ShapeDtypeStruct((8, 128), F32)]),
        in_specs=tuple([HBM_SPEC] * (2 * n)),
        out_specs=tuple([SEM_SPEC] * (3 * ng) + [HBM_SPEC] * (2 * n) + [pl.BlockSpec(memory_space=pltpu.VMEM)]),
        input_output_aliases={i: 3 * ng + i for i in range(2 * n)},
        compiler_params=pltpu.CompilerParams(has_side_effects=EFFECT),
    )(*[_hbm(s) for s in srcs], *[_hbm(z) for z in lands])
    sems = [(res[3 * g], res[3 * g + 1], res[3 * g + 2]) for g in range(ng)]
    srcs_thru = list(res[3 * ng:3 * ng + n])
    lands_thru = list(res[3 * ng + n:3 * ng + 2 * n])
    return sems, srcs_thru, lands_thru, res[-1]


def _push_wait(name, srcs_thru, lands_thru, sems, after, whole):
    n = len(srcs_thru)

    def body(*refs):
        src_refs, land_refs = refs[:n], refs[n:2 * n]
        send_sems, recv_sems, own_sems = refs[2 * n], refs[2 * n + 1], refs[2 * n + 2]
        x, y, c = _my_pos()
        me = 4 * x + 2 * y + c
        for a in range(n):
            pltpu.make_async_copy(src_refs[a] if whole else src_refs[a].at[me], _slot(land_refs[a], me),
                                  own_sems.at[a]).wait()
            for m in range(1, N_DEV):
                peer, pid = _peer(m)
                cp = pltpu.make_async_remote_copy(
                    src_ref=src_refs[a] if whole else src_refs[a].at[pid], dst_ref=_slot(land_refs[a], pid),
                    send_sem=send_sems.at[a * (N_DEV - 1) + m - 1], recv_sem=recv_sems.at[a * (N_DEV - 1) + m - 1],
                    device_id=peer, device_id_type=MESH)
                cp.wait_send()
                cp.wait_recv()

    thru = [pltpu.HBM(s.shape, s.dtype) for s in list(srcs_thru) + list(lands_thru)]
    res = pl.pallas_call(
        body, name=name, out_shape=tuple(thru),
        in_specs=tuple([HBM_SPEC] * (2 * n) + [SEM_SPEC, SEM_SPEC, SEM_SPEC, ANY]),
        out_specs=tuple([HBM_SPEC] * (2 * n)),
        input_output_aliases={i: i for i in range(2 * n)},
        compiler_params=pltpu.CompilerParams(has_side_effects=EFFECT),
    )(*srcs_thru, *lands_thru, sems[0], sems[1], sems[2], after)
    return list(res[n:])


def _adamw(recv, w, m, v, layer=0, prev=None):
    L, R, C = w.shape
    tr = _tile(R, max(8, (1 << 18) // C), 8) if R % 8 == 0 else R
    bc1 = 1.0 - ADAM_B1 ** ADAM_STEP
    bc2 = 1.0 - ADAM_B2 ** ADAM_STEP
    if prev is None:
        prev = [lax.empty((L, R, C), F32) for _ in range(4)]

    def body(r_ref, w_ref, m_ref, v_ref, p0, p1, p2, p3, g_ref, d_ref, nm_ref, nv_ref):
        g = r_ref[0].astype(F32)
        for s in range(1, N_DEV):
            g = g + r_ref[s].astype(F32)
        nm = ADAM_B1 * m_ref[...] + (1.0 - ADAM_B1) * g
        nv = ADAM_B2 * v_ref[...] + (1.0 - ADAM_B2) * (g * g)
        mh = nm / bc1
        vh = nv / bc2
        g_ref[...] = g
        d_ref[...] = -ADAM_LR * (mh / (jnp.sqrt(vh) + ADAM_EPS) + ADAM_WD * w_ref[...])
        nm_ref[...] = nm
        nv_ref[...] = nv

    row = pl.BlockSpec((None, tr, C), lambda i: (layer, i, 0))
    return pl.pallas_call(
        body, grid=(R // tr,),
        in_specs=[pl.BlockSpec((N_DEV, tr, C), lambda i: (0, i, 0)), row, row, row] + [ANY] * 4,
        out_specs=[row] * 4, out_shape=[jax.ShapeDtypeStruct((L, R, C), F32)] * 4,
        input_output_aliases={4: 0, 5: 1, 6: 2, 7: 3},
        compiler_params=_cp("parallel"), name="adamw")(recv, w, m, v, *prev)


def _full_w_spec(tk, tn):
    return pl.BlockSpec((tk, tn), lambda i, j, k: (k, j))


def _colblk_w_spec(n):
    def spec(tk, tn):
        per = n // tn
        return pl.BlockSpec((None, tk, tn), lambda i, j, k: (j // per, k, j % per))
    return spec


def kernel(x, meta_tokens, mix_norm_g, mlp_norm_g, final_norm_g, ev_w_in, ev_conv_w, ev_conv_b, ev_ln_g, ev_ln_b, ev_pool_w, ev_pool_b, ev_pool_scale, ev_w_out, od_w_in, od_gnorm_g, od_w_out, lb_param, mlp_w1, mlp_w2, loss_target, m_meta_tokens, m_mix_norm_g, m_mlp_norm_g, m_final_norm_g, m_ev_w_in, m_ev_conv_w, m_ev_conv_b, m_ev_ln_g, m_ev_ln_b, m_ev_pool_w, m_ev_pool_b, m_ev_pool_scale, m_ev_w_out, m_od_w_in, m_od_gnorm_g, m_od_w_out, m_lb_param, m_mlp_w1, m_mlp_w2, v_meta_tokens, v_mix_norm_g, v_mlp_norm_g, v_final_norm_g, v_ev_w_in, v_ev_conv_w, v_ev_conv_b, v_ev_ln_g, v_ev_ln_b, v_ev_pool_w, v_ev_pool_b, v_ev_pool_scale, v_ev_w_out, v_od_w_in, v_od_gnorm_g, v_od_w_out, v_lb_param, v_mlp_w1, v_mlp_w2):
    S, D = x.shape[1], x.shape[2]
    T = PAD + N_META + S
    DEPTH = mix_norm_g.shape[0]
    DFF = mlp_w1.shape[2] * N_DEV
    dev = 4 * lax.axis_index("x") + 2 * lax.axis_index("y") + lax.axis_index("c")

    g_meta, g_cw = _all_gather([meta_tokens[None], ev_conv_w])
    n_ev = ev_w_in.shape[0]
    n_od = od_w_in.shape[0]
    meta_full = jnp.transpose(g_meta[0], (1, 0, 2)).reshape(N_META, D)
    cw_full = jnp.transpose(g_cw, (0, 2, 1, 3)).reshape(n_ev, CONV_WIDTH, -1)
    cw_pad = jnp.pad(cw_full, ((0, 0), (0, 32 - CONV_WIDTH), (0, 0)))
    n_in_od = od_w_in.shape[2]
    n_w1 = mlp_w1.shape[2]

    ag_src, ag_grp, ag_axis, ag_at = [], [], [], {}
    for layer in range(DEPTH):
        j = layer // 2
        mixer = [("in", ev_w_in[j]), ("out", ev_w_out[j])] if layer % 2 == 0 else [("in", od_w_in[j]), ("out", od_w_out[j])]
        for pos, (key, arr) in enumerate(mixer + [("w1", mlp_w1[layer]), ("w2", mlp_w2[layer])]):
            ag_at[layer, key] = len(ag_src)
            ag_src.append(arr.astype(BF16))
            ag_grp.append(len(ag_grp))
            ag_axis.append(1 if key in ("in", "w1") and arr.shape[1] % 128 == 0 else 0)
    ag_src, g_meta, g_cw = lax.optimization_barrier((ag_src, g_meta, g_cw))
    ag_sems, ag_s, ag_l, ag_tok = _push_start(
        "ag_start", ag_src, [_landing(s_.shape, s_.dtype, ax) for s_, ax in zip(ag_src, ag_axis)], True, ag_grp)

    def ag_wait(layer, key, after):
        a = ag_at[layer, key]
        return _push_wait(f"ag_wait_{a}", [ag_s[a]], [ag_l[a]], ag_sems[a], after, True)[0]

    h = jnp.concatenate([jnp.zeros((PAD, D), F32), meta_full, x[0]], axis=0) + ag_tok[0, 0]
    tgt = jnp.pad(loss_target[0], ((PAD + N_META, 0), (0, 0)))
    lb_all = _lb_fwd(lb_param)

    tm_big = _tile(T, MM_ROWS_BIG, 16)
    tm_mid = _tile(T, MM_ROWS_MID, 16)
    tm_k4 = _tile(T, MM_ROWS_K4, 16)

    saved = []
    for layer in range(DEPTH):
        j = layer // 2
        sv = {"h0": h}
        g_in = ag_wait(layer, "in", h)
        w_in = g_in if g_in.ndim == 2 else jnp.transpose(g_in, (1, 0, 2)).reshape(D, -1)
        if layer % 2 == 0:
            sv["n"], u = _mm_rms_nn("ev_in", h, mix_norm_g[layer][None], w_in, tm_big, 512, "f32")
            yab, yc = _ev_fwd(u, cw_pad[j], ev_conv_b[j][None], ev_ln_g[j][None], ev_ln_b[j][None],
                              ev_pool_w[j].astype(BF16), ev_pool_b[j].reshape(1, -1), ev_pool_scale[j][None])
            sv.update(u=u, y=yab, yc=yc)
            w_out = ag_wait(layer, "out", yab).reshape(-1, D)
            h = _mm_nn("ev_out", yab, w_out, _full_w_spec, T, D, D, tm_mid, D, D, "resid", extra=h)
        else:
            sv["n"], u = _mm_rms_nn("od_in", h, mix_norm_g[layer][None], w_in, tm_big, 512, "f32")
            y, o, s0 = _hgrn_fwd(u, lb_all[layer][None], od_gnorm_g[j][None])
            sv.update(u=u, y=y, o=o, s0=s0)
            w_out = ag_wait(layer, "out", y).reshape(-1, D)
            h = _mm_nn("od_out", y, w_out, _full_w_spec, T, D, D, tm_mid, D, D, "resid", extra=h)
        sv["h1"] = h
        w_w1 = ag_wait(layer, "w1", h)
        n2, r, act = _mm_rms_nn("mlp_w1", h, mlp_norm_g[layer][None], w_w1, tm_big, 512, "relu2")
        w_w2 = ag_wait(layer, "w2", act).reshape(DFF, D)
        sv.update(w_in=w_in, w_out=w_out, w_w1=w_w1, w_w2=w_w2)
        sv.update(n2=n2, r=r, act=act)
        h = _mm_nn("mlp_w2", act, w_w2, _full_w_spec, T, D, DFF, tm_k4, D, DFF, "resid", extra=h)
        saved.append(sv)

    loss_blk, dh, dhb, dg_final = _loss_head(h, final_norm_g[None], tgt)
    loss = lax.psum(loss_blk[0, 0], AXES)

    tt = T
    g_mix, g_mlp = [None] * DEPTH, [None] * DEPTH
    small ={"cw": [None] * n_ev, "vec": [None] * n_ev, "pw": [None] * n_ev, "gn": [None] * n_od}
    dlb_rows = [jnp.zeros((1, D), F32) for _ in range(DEPTH)]

    def xs2(tt_, tk):
        return pl.BlockSpec((tt_, tk), lambda a, b, t: (t, a))

    def ys2(tt_, tn):
        return pl.BlockSpec((tt_, tn), lambda a, b, t: (t, b))

    def os2(tk, tn):
        return pl.BlockSpec((tk, tn), lambda a, b, t: (a, b))

    def os3(tk, tn):
        return pl.BlockSpec((None, tk, tn), lambda a, b, t: (b, a, 0))

    def dy2(tm, tn):
        return pl.BlockSpec((tm, tn), lambda i, jj, k: (i, k))

    def w_rows(tj, tn):
        return pl.BlockSpec((tj, tn), lambda i, jj, k: (jj, k))

    def w_whole(tj, tn):
        return pl.BlockSpec((tj, tn), lambda i, jj, k: (0, 0), pipeline_mode=pl.Buffered(1))

    def w_colblk(tj, tn):
        return pl.BlockSpec((None, tj, tn), lambda i, jj, k: (k, jj, 0))

    rs_pending = []

    def rs_start(tag, mats):
        blocks = [m_ if m_.ndim == 3 else m_.reshape(N_DEV, m_.shape[0] // N_DEV, m_.shape[1]) for m_ in mats]
        lands = [_landing(b_.shape[1:], b_.dtype) for b_ in blocks]
        sems, s_thru, l_thru, tok = _push_start(f"rs_start_{tag}", blocks, lands, False, [0] * len(blocks))
        rs_pending.append((tag, s_thru, l_thru, sems[0]))
        return tok[0, 0]

    for layer in reversed(range(DEPTH)):
        j = layer // 2
        sv = saved[layer]
        da1 = _mm_nt("mlp_w2_t", dhb, sv["w_w2"], dy2, w_rows, T, DFF, D, tm_mid, 1024, D, "dact", extra=sv["r"])
        dw2 = _mm_tn("mlp_dw2", sv["act"], dhb, xs2, ys2, os2, (DFF, D), T, DFF, D, tt, 512, D)
        dw1 = _mm_tn("mlp_dw1", sv["n2"], da1, xs2, ys2, os3, (N_DEV, D, n_w1), T, D, DFF, tt, D, n_w1)
        tok = rs_start(f"mlp{layer}", [dw1, dw2])
        dh, dhb, g_mlp[layer] = _mm_nt("mlp_w1_t", da1, sv["w_w1"], dy2, w_whole, T, D, DFF, tm_k4, D, DFF, "rms",
                                       extra=(sv["h1"], mlp_norm_g[layer][None] + tok, dh))
        if layer % 2 == 0:
            dyab = _mm_nt("ev_out_t", dhb, sv["w_out"], dy2, w_rows, T, D, D, tm_mid, D, D, "f32")
            dwout = _mm_tn("ev_dwout", sv["y"], dhb, xs2, ys2, os2, (D, D), T, D, D, tt, 512, D)
            du, small["cw"][j], small["vec"][j], small["pw"][j] = _ev_bwd(
                dyab, sv["yc"], sv["u"], cw_pad[j], ev_ln_g[j][None], ev_ln_b[j][None], ev_pool_w[j].astype(BF16),
                jnp.transpose(ev_pool_w[j], (0, 2, 1)).astype(BF16), ev_pool_b[j].reshape(1, -1),
                ev_pool_scale[j][None])
            nin = du.shape[1]
            dwin = _mm_tn("ev_dwin", sv["n"], du, xs2, ys2, os2, (D, nin), T, D, nin, tt, D, 512)
            dwin = jnp.transpose(dwin.reshape(D, N_DEV, nin // N_DEV), (1, 0, 2))
            tok = rs_start(f"mix{layer}", [dwin, dwout])
            dh, dhb, g_mix[layer] = _mm_nt("ev_in_t", du, sv["w_in"], dy2, w_whole, T, D, nin, tm_k4, D, nin, "rms",
                                           extra=(sv["h0"], mix_norm_g[layer][None] + tok, dh))
        else:
            dy = _mm_nt("od_out_t", dhb, sv["w_out"], dy2, w_rows, T, D, D, tm_mid, D, D, "f32")
            dwout = _mm_tn("od_dwout", sv["y"], dhb, xs2, ys2, os2, (D, D), T, D, D, tt, 512, D)
            du3, dlb_rows[layer], small["gn"][j] = _hgrn_bwd(dy, sv["o"], sv["s0"], sv["u"], lb_all[layer][None],
                                                              od_gnorm_g[j][None])
            per = D // n_in_od

            def du_t(tt_, tn):
                return pl.BlockSpec((None, tt_, tn), lambda a, b, t: (b // per, t, b % per))

            dwin = _mm_tn("od_dwin", sv["n"], du3, xs2, du_t, os3, (N_DEV, D, n_in_od), T, D, 4 * D, tt, D, n_in_od)
            tok = rs_start(f"mix{layer}", [dwin, dwout])
            dh, dhb, g_mix[layer] = _mm_nt(
                "od_in_t", du3, sv["w_in"], lambda tm, tn: pl.BlockSpec((4, tm, tn // 4), lambda i, jj, k: (0, i, 0)),
                w_whole, T, D, 4 * D, tm_k4, D, 4 * D, "rms", extra=(sv["h0"], mix_norm_g[layer][None] + tok, dh),
                parts=4)

    dmeta = dh[PAD:PAD + N_META]
    grad_x = dh[PAD + N_META:][None]
    dlb_param = _lb_bwd(lb_param, jnp.concatenate(dlb_rows, axis=0))

    pieces = [
        ("final", dg_final), ("pad", jnp.zeros((SMALL_F32_ROWS - 1, D), F32)),
        ("meta", dmeta), ("mix", jnp.concatenate(g_mix, 0)), ("mlp", jnp.concatenate(g_mlp, 0)),
        ("cw", jnp.stack([c[:CONV_WIDTH] for c in small["cw"]])), ("cb", jnp.stack([v_[0] for v_ in small["vec"]])),
        ("lng", jnp.stack([v_[1] for v_ in small["vec"]])), ("lnb", jnp.stack([v_[2] for v_ in small["vec"]])),
        ("pw", jnp.stack(small["pw"])), ("pb", jnp.stack([v_[4] for v_ in small["vec"]])),
        ("ps", jnp.stack([v_[3] for v_ in small["vec"]])), ("gn", jnp.stack([jnp.sum(g_, axis=0)[0] for g_ in small["gn"]])),
        ("lb", dlb_param),
    ]
    flat = jnp.concatenate([p.reshape(-1) for _, p in pieces])
    n_small = flat.shape[0]
    rows_small = SMALL_F32_ROWS + -(-(n_small // 1024 + 1 - SMALL_F32_ROWS) // 16) * 16
    flat = jnp.pad(flat, (0, rows_small * 1024 - n_small)).reshape(rows_small, 1024)

    sm_src = [flat[:SMALL_F32_ROWS], flat[SMALL_F32_ROWS:].astype(BF16)]
    sm_sems, sm_s, sm_l, sm_tok = _push_start("small_start", sm_src, [_landing(a_.shape, a_.dtype) for a_ in sm_src],
                                              True, [0, 0])
    recv = {}
    for tag, s_thru, l_thru, sems in rs_pending:
        got = _push_wait(f"rs_wait_{tag}", s_thru, l_thru, sems, sm_tok, False)
        layer = int(tag[3:])
        if tag.startswith("mlp"):
            recv["w1", layer], recv["w2", layer] = got
        else:
            key = "ev" if layer % 2 == 0 else "od"
            recv[key + "_in", layer // 2], recv[key + "_out", layer // 2] = got

    outs = {}
    big = {"ev_in": ("ev_w_in", ev_w_in, m_ev_w_in, v_ev_w_in), "ev_out": ("ev_w_out", ev_w_out, m_ev_w_out, v_ev_w_out),
           "od_in": ("od_w_in", od_w_in, m_od_w_in, v_od_w_in), "od_out": ("od_w_out", od_w_out, m_od_w_out, v_od_w_out),
           "w1": ("mlp_w1", mlp_w1, m_mlp_w1, v_mlp_w1), "w2": ("mlp_w2", mlp_w2, m_mlp_w2, v_mlp_w2)}
    for key, (name, w, m, v) in big.items():
        res = None
        for l in range(w.shape[0]):
            res = _adamw(recv[key, l], w, m, v, layer=l, prev=res)
        outs[name] = res

    small_params = {
        "meta": ("meta_tokens", None), "mix": ("mix_norm_g", mix_norm_g, m_mix_norm_g, v_mix_norm_g),
        "mlp": ("mlp_norm_g", mlp_norm_g, m_mlp_norm_g, v_mlp_norm_g),
        "final": ("final_norm_g", final_norm_g, m_final_norm_g, v_final_norm_g),
        "cw": ("ev_conv_w", None), "cb": ("ev_conv_b", ev_conv_b, m_ev_conv_b, v_ev_conv_b),
        "lng": ("ev_ln_g", ev_ln_g, m_ev_ln_g, v_ev_ln_g), "lnb": ("ev_ln_b", ev_ln_b, m_ev_ln_b, v_ev_ln_b),
        "pw": ("ev_pool_w", ev_pool_w, m_ev_pool_w, v_ev_pool_w), "pb": ("ev_pool_b", ev_pool_b, m_ev_pool_b, v_ev_pool_b),
        "ps": ("ev_pool_scale", ev_pool_scale, m_ev_pool_scale, v_ev_pool_scale),
        "gn": ("od_gnorm_g", od_gnorm_g, m_od_gnorm_g, v_od_gnorm_g), "lb": ("lb_param", lb_param, m_lb_param, v_lb_param),
    }
    csh = ev_conv_w.shape[2]
    msh = meta_tokens.shape[1]

    def packed(which):
        parts = []
        for key, g_ in pieces:
            ent = small_params.get(key)
            if key == "pad":
                full = g_
            elif key == "meta":
                src = (meta_tokens, m_meta_tokens, v_meta_tokens)[which]
                full = lax.dynamic_update_slice(jnp.zeros((N_META, D), F32), src, (0, dev * msh))
            elif key == "cw":
                src = (ev_conv_w, m_ev_conv_w, v_ev_conv_w)[which]
                full = lax.dynamic_update_slice(jnp.zeros(g_.shape, F32), src, (0, 0, dev * csh))
            else:
                full = ent[1 + which]
            parts.append(full.reshape(-1))
        f = jnp.concatenate(parts)
        return jnp.pad(f, (0, rows_small * 1024 - n_small)).reshape(rows_small, 1024)

    got_f32, got_bf16 = _push_wait("small_wait", sm_s, sm_l, sm_sems[0], outs["mlp_w2"][0], True)
    recv_small = jnp.concatenate([got_f32, got_bf16.astype(F32)], axis=1)
    sres = [r_[0] for r_ in _adamw(recv_small, packed(0)[None], packed(1)[None], packed(2)[None])]
    off = 0
    for key, g_ in pieces:
        size = g_.size
        vals = [r_.reshape(-1)[off:off + size].reshape(g_.shape) for r_ in sres]
        off += size
        if key == "pad":
            continue
        name = small_params[key][0]
        if key == "meta":
            vals = [lax.dynamic_slice(v_, (0, dev * msh), (N_META, msh)) for v_ in vals]
        elif key == "cw":
            vals = [lax.dynamic_slice(v_, (0, 0, dev * csh), v_.shape[:2] + (csh,)) for v_ in vals]
        else:
            vals = [v_.reshape(small_params[key][1].shape) for v_ in vals]
        outs[name] = vals

    names = ["meta_tokens", "mix_norm_g", "mlp_norm_g", "final_norm_g", "ev_w_in", "ev_conv_w", "ev_conv_b", "ev_ln_g",
             "ev_ln_b", "ev_pool_w", "ev_pool_b", "ev_pool_scale", "ev_w_out", "od_w_in", "od_gnorm_g", "od_w_out",
             "lb_param", "mlp_w1", "mlp_w2"]
    result = [loss, grad_x]
    for k in range(4):
        result += [outs[nm][k] for nm in names]
    return tuple(result)
```

```python
import functools

import jax
import jax.numpy as jnp
from jax import lax
from jax.experimental import pallas as pl
from jax.experimental.pallas import tpu as pltpu

F32 = jnp.float32
BF16 = jnp.bfloat16

N_DEV = 8
N_META = 16
CHUNK = 64
PAD = CHUNK - N_META
SUB = 16
HEAD = 128
CONV_WIDTH = 31
HALO = 32
POOL_WINDOWS = (2, 4, 8, 16)
EPS = 1e-6
NEG = -1e30
ADAM_LR, ADAM_B1, ADAM_B2, ADAM_EPS, ADAM_WD, ADAM_STEP = 0.001, 0.9, 0.999, 1e-08, 0.01, 10
VMEM_LIMIT = 56 * 1024 * 1024
EV_ROWS = 416
HGRN_ROWS = 832
MM_ROWS_BIG = 2080
MM_ROWS_MID = 1040
MM_ROWS_K4 = 416
SMALL_F32_ROWS = 8
HGRN_HEADS_FWD = 8
HGRN_HEADS_BWD = 2
MESH = pl.DeviceIdType.MESH
AXES = ("x", "y", "c")
ANY = pl.BlockSpec(memory_space=pl.ANY)


def _cp(*sem):
    return pltpu.CompilerParams(dimension_semantics=sem, vmem_limit_bytes=VMEM_LIMIT)


def _tile(n, cap, mult):
    best = None
    for d in range(mult, min(n, cap) + 1, mult):
        if n % d == 0:
            best = d
    assert best is not None, (n, cap, mult)
    return best


def _nt(a, b):
    return lax.dot_general(a, b, (((1,), (1,)), ((), ())), preferred_element_type=F32)


def _tn(a, b):
    return lax.dot_general(a, b, (((0,), (0,)), ((), ())), preferred_element_type=F32)


def _nn(a, b):
    return jnp.dot(a, b, preferred_element_type=F32)


def _r16(x):
    return x.astype(BF16).astype(F32)


def _row_ids(base, n):
    return base + lax.broadcasted_iota(jnp.int32, (n, 1), 0)


def _dsilu(x, s):
    return s * (1.0 + x * (1.0 - s))


def _loss_head(h, g, tgt):
    T, D = h.shape
    tm = _tile(T, MM_ROWS_MID, 16)
    first_x = PAD + N_META

    def body(h_ref, g_ref, t_ref, loss_ref, dh_ref, dhb_ref, dg_ref):
        i = pl.program_id(0)
        x = h_ref[...]
        r = lax.rsqrt(jnp.mean(x * x, axis=-1, keepdims=True) + EPS)
        xh = x * r
        gv = g_ref[...]
        out = xh * gv
        valid = _row_ids(i * tm, tm) >= first_x
        e = jnp.where(valid, out - t_ref[...], 0.0)
        dout = e * (1.0 / D)
        dxh = dout * gv
        dx = r * (dxh - xh * jnp.mean(dxh * xh, axis=-1, keepdims=True))
        dh_ref[...] = dx
        dhb_ref[...] = dx.astype(BF16)

        @pl.when(i == 0)
        def _():
            dg_ref[...] = jnp.zeros_like(dg_ref)
            loss_ref[...] = jnp.zeros_like(loss_ref)

        dg_ref[...] += jnp.sum(dout * xh, axis=0, keepdims=True)
        loss_ref[...] += 0.5 * jnp.sum(jnp.mean(e * e, axis=-1, keepdims=True))

    row = pl.BlockSpec((tm, D), lambda i: (i, 0))
    vec = pl.BlockSpec((1, D), lambda i: (0, 0))
    return pl.pallas_call(
        body, grid=(T // tm,),
        in_specs=[row, vec, row],
        out_specs=[pl.BlockSpec((8, 128), lambda i: (0, 0)), row, row, vec],
        out_shape=[jax.ShapeDtypeStruct((8, 128), F32), jax.ShapeDtypeStruct((T, D), F32),
                   jax.ShapeDtypeStruct((T, D), BF16), jax.ShapeDtypeStruct((1, D), F32)],
        compiler_params=_cp("arbitrary"), name="loss_head")(h, g, tgt)


def _mm_nn(name, a, w, w_spec, M, N, K, tm, tn, tk, mode, extra=None, a_spec=None):
    nk = K // tk
    if a_spec is None:
        a_spec = pl.BlockSpec((tm, tk), lambda i, j, k: (i, k))
    o_spec = pl.BlockSpec((tm, tn), lambda i, j, k: (i, j))

    def body(*refs):
        if mode == "resid":
            a_ref, w_ref, e_ref = refs[:3]
            outs = refs[3:]
        else:
            a_ref, w_ref = refs[:2]
            outs = refs[2:]
        acc_ref = outs[-1] if nk > 1 else None
        part = _nn(a_ref[...], w_ref[...])

        def finish(acc):
            if mode == "f32":
                outs[0][...] = acc
            elif mode == "relu2":
                r = jnp.maximum(acc, 0.0)
                outs[0][...] = r.astype(BF16)
                outs[1][...] = (r * r).astype(BF16)
            else:
                keep = _row_ids(pl.program_id(0) * tm, tm) >= PAD
                outs[0][...] = jnp.where(keep, e_ref[...] + acc, 0.0)

        if nk == 1:
            finish(part)
        else:
            k = pl.program_id(2)

            @pl.when(k == 0)
            def _():
                acc_ref[...] = part

            @pl.when(k > 0)
            def _():
                acc_ref[...] += part

            @pl.when(k == nk - 1)
            def _():
                finish(acc_ref[...])

    in_specs = [a_spec, w_spec(tk, tn)]
    args = [a, w]
    if mode == "resid":
        in_specs.append(o_spec)
        args.append(extra)
    if mode == "relu2":
        out_specs = [o_spec, o_spec]
        out_shape = [jax.ShapeDtypeStruct((M, N), BF16)] * 2
    else:
        out_specs = [o_spec]
        out_shape = [jax.ShapeDtypeStruct((M, N), F32)]
    scratch = [pltpu.VMEM((tm, tn), F32)] if nk > 1 else []
    res = pl.pallas_call(
        body, grid=(M // tm, N // tn, nk), in_specs=in_specs, out_specs=out_specs, out_shape=out_shape,
        scratch_shapes=scratch, compiler_params=_cp("parallel", "parallel", "arbitrary"), name=name)(*args)
    return res if mode == "relu2" else res[0]


def _mm_rms_nn(name, h, g, w, tm, tn, mode):
    M, K = h.shape
    N = w.shape[1]

    def body(h_ref, g_ref, w_ref, n_ref, *outs):
        @pl.when(pl.program_id(1) == 0)
        def _():
            ch = _tile(tm, 256, 16)

            def chunk(c, carry):
                rows = pl.ds(pl.multiple_of(c * ch, ch), ch)
                x = h_ref[rows, :]
                r = lax.rsqrt(jnp.mean(x * x, axis=-1, keepdims=True) + EPS)
                n_ref[rows, :] = ((x * r) * g_ref[...]).astype(BF16)
                return carry

            lax.fori_loop(0, tm // ch, chunk, 0)

        acc = _nn(n_ref[...], w_ref[...])
        if mode == "f32":
            outs[0][...] = acc
        else:
            r = jnp.maximum(acc, 0.0)
            outs[0][...] = r.astype(BF16)
            outs[1][...] = (r * r).astype(BF16)

    row = pl.BlockSpec((tm, K), lambda i, j: (i, 0))
    o_spec = pl.BlockSpec((tm, tn), lambda i, j: (i, j))
    n_out = 1 if mode == "f32" else 2
    return pl.pallas_call(
        body, grid=(M // tm, N // tn),
        in_specs=[row, pl.BlockSpec((1, K), lambda i, j: (0, 0)), pl.BlockSpec((K, tn), lambda i, j: (0, j))],
        out_specs=[row] + [o_spec] * n_out,
        out_shape=[jax.ShapeDtypeStruct((M, K), BF16)] + [jax.ShapeDtypeStruct((M, N), F32 if mode == "f32" else BF16)] * n_out,
        compiler_params=_cp("parallel", "arbitrary"), name=name)(h, g, w)


def _mm_nt(name, dy, w, dy_spec, w_spec, M, J, N, tm, tj, tn, mode, extra=None, parts=1):
    nk = N // tn
    o_spec = pl.BlockSpec((tm, tj), lambda i, j, k: (i, j))
    n_extra = {"f32": 0, "dact": 1, "rms": 3}[mode]
    if mode == "rms":
        assert nk == 1 and tj == J

    def body(*refs):
        dy_ref, w_ref = refs[:2]
        ex = refs[2:2 + n_extra]
        outs = refs[2 + n_extra:]
        acc_ref = outs[-1] if nk > 1 else None
        if parts == 1:
            part = _nt(dy_ref[...], w_ref[...])
        else:
            wq = tn // parts
            part = _nt(dy_ref[0], w_ref[:, 0:wq])
            for q in range(1, parts):
                part = part + _nt(dy_ref[q], w_ref[:, q * wq:(q + 1) * wq])

        def finish(acc):
            if mode == "f32":
                outs[0][...] = acc
            elif mode == "dact":
                outs[0][...] = (acc * (2.0 * ex[0][...].astype(F32))).astype(BF16)
            else:
                h_ref, g_ref, dres_ref = ex
                dh_ref, dhb_ref, dg_ref = outs[:3]
                i = pl.program_id(0)

                @pl.when(i == 0)
                def _():
                    dg_ref[...] = jnp.zeros_like(dg_ref)

                ch = _tile(tm, 256, 16)
                for c0 in range(0, tm, ch):
                    a_c = acc[c0:c0 + ch]
                    x = h_ref[c0:c0 + ch, :]
                    r = lax.rsqrt(jnp.mean(x * x, axis=-1, keepdims=True) + EPS)
                    xh = x * r
                    dxh = a_c * g_ref[...]
                    dx = r * (dxh - xh * jnp.mean(dxh * xh, axis=-1, keepdims=True))
                    keep = _row_ids(i * tm + c0, ch) >= PAD
                    dh = jnp.where(keep, dres_ref[c0:c0 + ch, :] + dx, 0.0)
                    dh_ref[c0:c0 + ch, :] = dh
                    dhb_ref[c0:c0 + ch, :] = dh.astype(BF16)
                    dg_ref[...] += jnp.sum(a_c * xh, axis=0, keepdims=True)

        if nk == 1:
            finish(part)
        else:
            k = pl.program_id(2)

            @pl.when(k == 0)
            def _():
                acc_ref[...] = part

            @pl.when(k > 0)
            def _():
                acc_ref[...] += part

            @pl.when(k == nk - 1)
            def _():
                finish(acc_ref[...])

    in_specs = [dy_spec(tm, tn), w_spec(tj, tn)]
    args = [dy, w]
    scratch = [pltpu.VMEM((tm, tj), F32)] if nk > 1 else []
    if mode == "rms":
        vec = pl.BlockSpec((1, J), lambda i, j, k: (0, 0))
        h, g, dres = extra
        res = pl.pallas_call(
            body, grid=(M // tm, 1, 1), in_specs=in_specs + [o_spec, vec, o_spec], out_specs=[o_spec, o_spec, vec],
            out_shape=[jax.ShapeDtypeStruct((M, J), F32), jax.ShapeDtypeStruct((M, J), BF16),
                       jax.ShapeDtypeStruct((1, J), F32)],
            compiler_params=_cp("arbitrary", "arbitrary", "arbitrary"), name=name)(*args, h, g, dres)
        return res
    if mode == "dact":
        in_specs.append(o_spec)
        args.append(extra)
    return pl.pallas_call(
        body, grid=(M // tm, J // tj, nk), in_specs=in_specs, out_specs=[o_spec],
        out_shape=[jax.ShapeDtypeStruct((M, J), BF16 if mode == "dact" else F32)],
        scratch_shapes=scratch, compiler_params=_cp("parallel", "parallel", "arbitrary"), name=name)(*args)[0]


def _mm_tn(name, x, dy, x_spec, dy_spec, o_spec, o_shape, T, K, N, tt, tk, tn):
    nt = T // tt

    def body(x_ref, dy_ref, o_ref, *acc):
        part = _tn(x_ref[...], dy_ref[...])
        if nt == 1:
            o_ref[...] = part.astype(BF16)
            return
        acc_ref = acc[0]
        t = pl.program_id(2)

        @pl.when(t == 0)
        def _():
            acc_ref[...] = part

        @pl.when(t > 0)
        def _():
            acc_ref[...] += part

        @pl.when(t == nt - 1)
        def _():
            o_ref[...] = acc_ref[...].astype(BF16)

    return pl.pallas_call(
        body, grid=(K // tk, N // tn, nt), in_specs=[x_spec(tt, tk), dy_spec(tt, tn)], out_specs=o_spec(tk, tn),
        out_shape=jax.ShapeDtypeStruct(o_shape, BF16), scratch_shapes=[pltpu.VMEM((tk, tn), F32)] if nt > 1 else [],
        compiler_params=_cp("parallel", "parallel", "arbitrary"), name=name)(x, dy)


def _pool_counts(base, n, w):
    pos = _row_ids(base, n) - PAD
    return jnp.clip(pos + 1, 1, w).astype(F32)


def _shifted_copies(buf, rows):
    buf[0, rows:rows + 8, :] = jnp.zeros((8, buf.shape[2]), F32)

    def blk(s, carry):
        b = pl.multiple_of(s * HALO, HALO)
        win = buf[0, pl.ds(b, HALO + 8), :]
        for r in range(1, 8):
            buf[r, pl.ds(b, HALO), :] = win[r:r + HALO]
        return carry

    lax.fori_loop(0, rows // HALO, blk, 0)


def _ev_fwd(u, cw, cb, lg, lb, pw, pb, ps):
    T = u.shape[0]
    C = 512
    tm = _tile(T, EV_ROWS, HALO)
    nsub = tm // HALO
    hb = tm // HALO

    def body(val_ref, gate_ref, pin_ref, valh_ref, gateh_ref, pinh_ref, cw_ref, cb_ref, lg_ref, lb_ref, pw_ref,
             pb_ref, ps_ref, yab_ref, yc_ref, a_ext, p_ext, d_buf):
        i = pl.program_id(0)
        nf = (i > 0).astype(F32)
        a_ext[0, 0:HALO, :] = valh_ref[...] * jax.nn.sigmoid(gateh_ref[...]) * nf
        a_ext[0, HALO:HALO + tm, :] = val_ref[...] * jax.nn.sigmoid(gate_ref[...])
        p_ext[0:HALO, :] = pinh_ref[...] * nf
        p_ext[HALO:, :] = pin_ref[...]
        _shifted_copies(a_ext, tm + HALO)

        def sub(s, carry):
            base = pl.multiple_of(s * HALO, HALO)
            acc = jnp.zeros((HALO, C), F32) + cb_ref[...]
            for j in range(CONV_WIDTH):
                off = 2 + j
                acc = acc + cw_ref[pl.ds(j, 1), :] * a_ext[off % 8, pl.ds(pl.multiple_of(base + off // 8 * 8, 8), HALO), :]
            yc_ref[pl.ds(base, HALO), :] = acc
            mu = jnp.mean(acc, axis=-1, keepdims=True)
            yc = acc - mu
            rstd = lax.rsqrt(jnp.mean(yc * yc, axis=-1, keepdims=True) + EPS)
            z = (yc * rstd) * lg_ref[...] + lb_ref[...]
            yab_ref[pl.ds(base, HALO), 0:C] = (z * jax.nn.sigmoid(z)).astype(BF16)
            pwin = p_ext[pl.ds(base, 2 * HALO), :]
            for gi, w in enumerate(POOL_WINDOWS):
                lo, hi = gi * HEAD, (gi + 1) * HEAD
                x = pwin[HALO:, lo:hi]
                tot = x
                for k in range(1, w):
                    tot = tot + pwin[HALO - k:2 * HALO - k, lo:hi]
                cnt = _pool_counts(i * tm + base, HALO, w)
                d_buf[pl.ds(base, HALO), lo:hi] = (tot / cnt - x).astype(BF16)
            return carry

        lax.fori_loop(0, nsub, sub, 0)
        for gi in range(len(POOL_WINDOWS)):
            lo, hi = gi * HEAD, (gi + 1) * HEAD
            y = _nn(d_buf[:, lo:hi], pw_ref[gi]) + pb_ref[:, lo:hi]
            yab_ref[:, C + lo:C + hi] = (y * ps_ref[:, lo:hi]).astype(BF16)

    def main(c):
        return pl.BlockSpec((tm, C), lambda i: (i, c))

    def halo(c):
        return pl.BlockSpec((HALO, C), lambda i: (jnp.maximum(i * hb - 1, 0), c))

    vec = pl.BlockSpec((1, C), lambda i: (0, 0))
    return pl.pallas_call(
        body, grid=(T // tm,),
        in_specs=[main(0), main(1), main(2), halo(0), halo(1), halo(2),
                  pl.BlockSpec((32, C), lambda i: (0, 0)), vec, vec, vec,
                  pl.BlockSpec((4, HEAD, HEAD), lambda i: (0, 0, 0)), vec, vec],
        out_specs=[pl.BlockSpec((tm, 2 * C), lambda i: (i, 0)), pl.BlockSpec((tm, C), lambda i: (i, 0))],
        out_shape=[jax.ShapeDtypeStruct((T, 2 * C), BF16), jax.ShapeDtypeStruct((T, C), F32)],
        scratch_shapes=[pltpu.VMEM((8, tm + HALO + 8, C), F32), pltpu.VMEM((tm + HALO, C), F32),
                        pltpu.VMEM((tm, C), BF16)],
        compiler_params=_cp("parallel"), name="ev_fwd")(u, u, u, u, u, u, cw, cb, lg, lb, pw, pb, ps)


def _ev_bwd(dyab, yc, u, cw, lg, lb, pw, pwt, pb, ps):
    T = u.shape[0]
    C = 512
    tm = _tile(T, EV_ROWS, HALO)
    nsub = tm // HALO
    hb = tm // HALO
    nblk = T // tm
    E = tm + HALO

    def body(dya_ref, dyb_ref, dyah_ref, dybh_ref, yc_ref, ych_ref, val_ref, gate_ref, pin_ref, valh_ref, gateh_ref,
             pinh_ref, cw_ref, lg_ref, lb_ref, pw_ref, pwt_ref, pb_ref, ps_ref,
             du_ref, dcw_ref, dvec_ref, dpw_ref,
             dy_ext, a_ext, p_ext, ddc_ext, dd_buf, d_buf, dpre_buf, dcw_acc, vec_acc):
        i = pl.program_id(0)
        nf = (i > 0).astype(F32)
        nl = (i < nblk - 1).astype(F32)

        @pl.when(i == 0)
        def _():
            dcw_ref[...] = jnp.zeros_like(dcw_ref)
            dvec_ref[...] = jnp.zeros_like(dvec_ref)
            dpw_ref[...] = jnp.zeros_like(dpw_ref)

        dcw_acc[...] = jnp.zeros_like(dcw_acc)
        vec_acc[...] = jnp.zeros_like(vec_acc)
        a_ext[0, 0:HALO, :] = valh_ref[...] * jax.nn.sigmoid(gateh_ref[...]) * nf
        a_ext[0, HALO:E, :] = val_ref[...] * jax.nn.sigmoid(gate_ref[...])
        p_ext[0:HALO, :] = pinh_ref[...] * nf
        p_ext[HALO:, :] = pin_ref[...]
        _shifted_copies(a_ext, E)

        def ln_bwd(y, dya, main):
            mu = jnp.mean(y, axis=-1, keepdims=True)
            ycen = y - mu
            rstd = lax.rsqrt(jnp.mean(ycen * ycen, axis=-1, keepdims=True) + EPS)
            yh = ycen * rstd
            z = yh * lg_ref[...] + lb_ref[...]
            sz = jax.nn.sigmoid(z)
            dz = dya * _dsilu(z, sz)
            dyh = dz * lg_ref[...]
            dy = rstd * (dyh - jnp.mean(dyh, axis=-1, keepdims=True) - yh * jnp.mean(dyh * yh, axis=-1, keepdims=True))
            if main:
                vec_acc[1] += jnp.sum((dz * yh).reshape(HALO // 8, 8, C), axis=0)
                vec_acc[2] += jnp.sum(dz.reshape(HALO // 8, 8, C), axis=0)
                vec_acc[0] += jnp.sum(dy.reshape(HALO // 8, 8, C), axis=0)
            return dy

        def pool_dd(dyb, base, main):
            dpre = dyb * ps_ref[...]
            for gi, w in enumerate(POOL_WINDOWS):
                lo, hi = gi * HEAD, (gi + 1) * HEAD
                dd = _nn(dpre[:, lo:hi].astype(BF16), pwt_ref[gi])
                cnt = _pool_counts(i * tm + base, HALO, w)
                ddc_ext[pl.ds(base, HALO), lo:hi] = dd / cnt
                if main:
                    dd_buf[pl.ds(base, HALO), lo:hi] = dd
            if main:
                dpre_buf[pl.ds(base, HALO), :] = dpre.astype(BF16)
                vec_acc[4] += jnp.sum(dpre.reshape(HALO // 8, 8, C), axis=0)

        def p1(s, carry):
            base = pl.multiple_of(s * HALO, HALO)
            dy_ext[0, pl.ds(base, HALO), :] = ln_bwd(yc_ref[pl.ds(base, HALO), :], dya_ref[pl.ds(base, HALO), :], True)
            pool_dd(dyb_ref[pl.ds(base, HALO), :], base, True)
            return carry

        lax.fori_loop(0, nsub, p1, 0)
        dy_ext[0, tm:E, :] = ln_bwd(ych_ref[...], dyah_ref[...], False) * nl
        _shifted_copies(dy_ext, E)
        dpre_h = dybh_ref[...] * ps_ref[...] * nl
        for gi, w in enumerate(POOL_WINDOWS):
            lo, hi = gi * HEAD, (gi + 1) * HEAD
            dd = _nn(dpre_h[:, lo:hi].astype(BF16), pwt_ref[gi])
            ddc_ext[tm:, lo:hi] = dd / _pool_counts(i * tm + tm, HALO, w)

        def p2(s, carry):
            base = pl.multiple_of(s * HALO, HALO)
            dy_m = dy_ext[0, pl.ds(base, HALO), :]
            da = jnp.zeros((HALO, C), F32)
            for j in range(CONV_WIDTH):
                sh = CONV_WIDTH - 1 - j
                off = 2 + j
                da = da + cw_ref[pl.ds(j, 1), :] * dy_ext[sh % 8, pl.ds(pl.multiple_of(base + sh // 8 * 8, 8), HALO), :]
                a_j = a_ext[off % 8, pl.ds(pl.multiple_of(base + off // 8 * 8, 8), HALO), :]
                dcw_acc[j] += jnp.sum((dy_m * a_j).reshape(HALO // 8, 8, C), axis=0)
            v = val_ref[pl.ds(base, HALO), :]
            g = gate_ref[pl.ds(base, HALO), :]
            sg = jax.nn.sigmoid(g)
            du_ref[pl.ds(base, HALO), 0:C] = (da * sg).astype(BF16)
            du_ref[pl.ds(base, HALO), C:2 * C] = (da * v * sg * (1.0 - sg)).astype(BF16)
            pwin = p_ext[pl.ds(base, 2 * HALO), :]
            cwin = ddc_ext[pl.ds(base, 2 * HALO), :]
            for gi, w in enumerate(POOL_WINDOWS):
                lo, hi = gi * HEAD, (gi + 1) * HEAD
                x = pwin[HALO:, lo:hi]
                tot = x
                back = cwin[0:HALO, lo:hi]
                for k in range(1, w):
                    tot = tot + pwin[HALO - k:2 * HALO - k, lo:hi]
                    back = back + cwin[k:k + HALO, lo:hi]
                cnt = _pool_counts(i * tm + base, HALO, w)
                d_buf[pl.ds(base, HALO), lo:hi] = (tot / cnt - x).astype(BF16)
                du_ref[pl.ds(base, HALO), 2 * C + lo:2 * C + hi] = (back - dd_buf[pl.ds(base, HALO), lo:hi]).astype(BF16)
            return carry

        lax.fori_loop(0, nsub, p2, 0)
        for gi in range(len(POOL_WINDOWS)):
            lo, hi = gi * HEAD, (gi + 1) * HEAD
            pre = _nn(d_buf[:, lo:hi], pw_ref[gi]) + pb_ref[:, lo:hi]
            vec_acc[3, :, lo:hi] += jnp.sum((dyb_ref[:, lo:hi] * pre).reshape(tm // 8, 8, HEAD), axis=0)
            dpw_ref[gi] += _tn(d_buf[:, lo:hi], dpre_buf[:, lo:hi])
        for j in range(CONV_WIDTH):
            dcw_ref[pl.ds(j, 1), :] += jnp.sum(dcw_acc[j], axis=0, keepdims=True)
        for r in range(5):
            dvec_ref[pl.ds(r, 1), :] += jnp.sum(vec_acc[r], axis=0, keepdims=True)

    def main(c, width=C):
        return pl.BlockSpec((tm, width), lambda i: (i, c))

    def prev(c):
        return pl.BlockSpec((HALO, C), lambda i: (jnp.maximum(i * hb - 1, 0), c))

    def nxt(c):
        return pl.BlockSpec((HALO, C), lambda i: (jnp.minimum((i + 1) * hb, T // HALO - 1), c))

    vec = pl.BlockSpec((1, C), lambda i: (0, 0))
    mat = pl.BlockSpec((4, HEAD, HEAD), lambda i: (0, 0, 0))
    return pl.pallas_call(
        body, grid=(nblk,),
        in_specs=[main(0), main(1), nxt(0), nxt(1), main(0), nxt(0), main(0), main(1), main(2), prev(0), prev(1),
                  prev(2), pl.BlockSpec((32, C), lambda i: (0, 0)), vec, vec, mat, mat, vec, vec],
        out_specs=[pl.BlockSpec((tm, 3 * C), lambda i: (i, 0)), pl.BlockSpec((32, C), lambda i: (0, 0)),
                   pl.BlockSpec((8, C), lambda i: (0, 0)), mat],
        out_shape=[jax.ShapeDtypeStruct((T, 3 * C), BF16), jax.ShapeDtypeStruct((32, C), F32),
                   jax.ShapeDtypeStruct((8, C), F32), jax.ShapeDtypeStruct((4, HEAD, HEAD), F32)],
        scratch_shapes=[pltpu.VMEM((8, E + 8, C), F32), pltpu.VMEM((8, E + 8, C), F32), pltpu.VMEM((E, C), F32),
                        pltpu.VMEM((E, C), F32), pltpu.VMEM((tm, C), F32), pltpu.VMEM((tm, C), BF16),
                        pltpu.VMEM((tm, C), BF16), pltpu.VMEM((32, 8, C), F32), pltpu.VMEM((8, 8, C), F32)],
        compiler_params=_cp("arbitrary"), name="ev_bwd")(
            dyab, dyab, dyab, dyab, yc, yc, u, u, u, u, u, u, cw, lg, lb, pw, pwt, pb, ps)


def _cumsum_rows(x, reverse=False):
    n = x.shape[0]
    rid = lax.broadcasted_iota(jnp.int32, (n, 1), 0)
    k = 1
    while k < n:
        if reverse:
            sh = jnp.where(rid < n - k, pltpu.roll(x, n - k, 0), 0.0)
        else:
            sh = jnp.where(rid >= k, pltpu.roll(x, k, 0), 0.0)
        x = x + sh
        k *= 2
    return x


def _hgrn_gates(qr, fr, lbv):
    sq = jax.nn.sigmoid(qr)
    sg = jax.nn.sigmoid(fr)
    fg = lbv + (1.0 - lbv) * sg
    return qr * sq, sq, sg, fg, 1.0 - fg, jnp.log(fg)


def _hgrn_fwd(u, lbv, gn):
    T = u.shape[0]
    H = 8
    RB = _tile(T, HGRN_ROWS, CHUNK)
    NC = RB // CHUNK
    NS = CHUNK // SUB

    HP = HGRN_HEADS_FWD
    W = HP * HEAD

    def body(q_ref, f_ref, i_ref, g_ref, lb_ref, gn_ref, y_ref, o_ref, s0_ref, st, qs, ks, bs, vs, os_):
        rb = pl.program_id(1)

        @pl.when(rb == 0)
        def _():
            st[...] = jnp.zeros_like(st)

        t8 = lax.broadcasted_iota(jnp.int32, (8, 1), 0)

        def head(hh, c, rows):
            sl = slice(hh * HEAD, (hh + 1) * HEAD)
            q, _, _, _, kk, lf = _hgrn_gates(q_ref[rows, sl], f_ref[rows, sl], lb_ref[:, sl])
            v = i_ref[rows, sl]
            b = _cumsum_rows(lf)
            qs[hh] = q
            ks[hh] = kk
            bs[hh] = b
            vs[hh] = v
            st0 = st[hh]
            s0_ref[hh, c] = st0
            os_[hh] = _nt((q * jnp.exp(b)).astype(BF16), st0.astype(BF16))
            for I in range(NS):
                lo = I * SUB
                qI = qs[hh, lo:lo + SUB, :]
                bI = bs[hh, lo:lo + SUB, :]
                oI = jnp.zeros((SUB, HEAD), F32)
                if I > 0:
                    bprev = bs[hh, pl.ds(lo - 1, 1), :]
                    qt = _r16(qI * jnp.exp(bI - bprev))
                    kt = _r16(ks[hh, 0:lo, :] * jnp.exp(bprev - bs[hh, 0:lo, :]))
                    A = _nt(qt, kt)
                    oI = oI + _nn(_r16(A), _r16(vs[hh, 0:lo, :]))
                od = [jnp.zeros((8, HEAD), F32) for _ in range(SUB // 8)]
                for s in range(SUB):
                    row = pl.ds(lo + s, 1)
                    brow, krow, vrow = bs[hh, row, :], ks[hh, row, :], vs[hh, row, :]
                    for ti in range(SUB // 8):
                        o8 = 8 * ti
                        if s > o8 + 7:
                            continue
                        d = bI[o8:o8 + 8] - brow
                        if s > o8:
                            d = jnp.where(t8 >= s - o8, d, NEG)
                        col = jnp.sum(qI[o8:o8 + 8] * jnp.exp(d) * krow, axis=1, keepdims=True)
                        od[ti] = od[ti] + col * vrow
                os_[hh, lo:lo + SUB, :] += oI + jnp.concatenate(od, axis=0)
            blast = bs[hh, pl.ds(CHUNK - 1, 1), :]
            kh = kk * jnp.exp(blast - b)
            st[hh] = st0 * jnp.exp(blast) + _tn(v.astype(BF16), kh.astype(BF16))
            o = os_[hh]
            o_ref[rows, sl] = o
            rr = lax.rsqrt(jnp.mean(o * o, axis=-1, keepdims=True) + EPS)
            gr = g_ref[rows, sl]
            y_ref[rows, sl] = (((o * rr) * gn_ref[...]) * (gr * jax.nn.sigmoid(gr))).astype(BF16)

        def chunk(c, carry):
            rows = pl.ds(pl.multiple_of(c * CHUNK, CHUNK), CHUNK)
            for hh in range(HP):
                head(hh, c, rows)
            return carry

        lax.fori_loop(0, NC, chunk, 0)

    def blk(q):
        return pl.BlockSpec((RB, W), lambda h, r: (r, q * (H // HP) + h))

    sc = lambda: pltpu.VMEM((HP, CHUNK, HEAD), F32)
    return pl.pallas_call(
        body, grid=(H // HP, T // RB),
        in_specs=[blk(0), blk(1), blk(2), blk(3), pl.BlockSpec((1, W), lambda h, r: (0, h)),
                  pl.BlockSpec((1, HEAD), lambda h, r: (0, 0))],
        out_specs=[pl.BlockSpec((RB, W), lambda h, r: (r, h)), pl.BlockSpec((RB, W), lambda h, r: (r, h)),
                   pl.BlockSpec((HP, NC, HEAD, HEAD), lambda h, r: (h, r, 0, 0))],
        out_shape=[jax.ShapeDtypeStruct((T, H * HEAD), BF16), jax.ShapeDtypeStruct((T, H * HEAD), F32),
                   jax.ShapeDtypeStruct((H, T // CHUNK, HEAD, HEAD), F32)],
        scratch_shapes=[pltpu.VMEM((HP, HEAD, HEAD), F32), sc(), sc(), sc(), sc(), sc()],
        compiler_params=_cp("parallel", "arbitrary"), name="hgrn_fwd")(u, u, u, u, lbv, gn)


def _hgrn_bwd(dy, o, s0, u, lbv, gn):
    T = u.shape[0]
    H = 8
    RB = _tile(T, HGRN_ROWS, CHUNK)
    NB = T // RB
    NC = RB // CHUNK
    NS = CHUNK // SUB

    def body(q_ref, f_ref, i_ref, g_ref, lb_ref, gn_ref, o_ref, dy_ref, s0_ref, du_ref, dlb_ref, dgn_ref,
             dst, qs, ks, bs, vs, dos, dqs, dks, dki, dvs, dbs):
        rb = pl.program_id(1)

        @pl.when(rb == 0)
        def _():
            dst[...] = jnp.zeros_like(dst)
            dlb_ref[...] = jnp.zeros_like(dlb_ref)
            dgn_ref[...] = jnp.zeros_like(dgn_ref)

        t8 = lax.broadcasted_iota(jnp.int32, (8, 1), 0)
        lane = lax.broadcasted_iota(jnp.int32, (8, HEAD), 1)
        gnv = gn_ref[...]

        def head(hh, c, rows):
            sl = slice(hh * HEAD, (hh + 1) * HEAD)
            lbv_ = lb_ref[:, sl]
            qr = q_ref[rows, sl]
            q, sq, sg, fg, kk, lf = _hgrn_gates(qr, f_ref[rows, sl], lbv_)
            v = i_ref[rows, sl]
            gr = g_ref[rows, sl]
            b = _cumsum_rows(lf)
            eb = jnp.exp(b)
            ov = o_ref[rows, sl]
            dyv = dy_ref[rows, sl]
            rr = lax.rsqrt(jnp.mean(ov * ov, axis=-1, keepdims=True) + EPS)
            oh = ov * rr
            gs = jax.nn.sigmoid(gr)
            dgr = dyv * (oh * gnv) * _dsilu(gr, gs)
            dnrm = dyv * (gr * gs)
            dgn_ref[hh] += jnp.sum(dnrm * oh, axis=0, keepdims=True)
            t1 = dnrm * gnv
            do = rr * (t1 - oh * jnp.mean(t1 * oh, axis=-1, keepdims=True))
            qs[hh] = q
            ks[hh] = kk
            bs[hh] = b
            vs[hh] = v
            dos[hh] = do
            st0 = s0_ref[hh, c]
            dS = dst[hh]
            do_b = do.astype(BF16)
            blast = bs[hh, pl.ds(CHUNK - 1, 1), :]
            elast = jnp.exp(blast - b)
            dq_inter = _nn(do_b, st0.astype(BF16)) * eb
            dqs[hh] = dq_inter
            dbs[hh] = q * dq_inter
            kh = kk * elast
            dvs[hh] = _nt(kh.astype(BF16), dS.astype(BF16))
            dk_inter = _nn(v.astype(BF16), dS.astype(BF16)) * elast
            dki[hh] = dk_inter
            dks[hh] = jnp.zeros((CHUNK, HEAD), F32)
            for I in range(NS):
                lo = I * SUB
                qI = qs[hh, lo:lo + SUB, :]
                bI = bs[hh, lo:lo + SUB, :]
                doI = dos[hh, lo:lo + SUB, :]
                dqI = jnp.zeros((SUB, HEAD), F32)
                dbI = jnp.zeros((SUB, HEAD), F32)
                if I > 0:
                    bprev = bs[hh, pl.ds(lo - 1, 1), :]
                    eq = jnp.exp(bI - bprev)
                    ek = jnp.exp(bprev - bs[hh, 0:lo, :])
                    qt = _r16(qI * eq)
                    kt = _r16(ks[hh, 0:lo, :] * ek)
                    A = _r16(_nt(qt, kt))
                    doI_b = _r16(doI)
                    dA = _r16(_nt(doI_b, _r16(vs[hh, 0:lo, :])))
                    dvs[hh, 0:lo, :] += _tn(A, doI_b)
                    dqt = _nn(dA, kt)
                    dkt = _tn(dA, qt)
                    dqI = dqI + dqt * eq
                    dbI = dbI + qt.astype(F32) * dqt
                    dks[hh, 0:lo, :] += dkt * ek
                    dbs[hh, 0:lo, :] -= kt.astype(F32) * dkt
                dq_t = [jnp.zeros((8, HEAD), F32) for _ in range(SUB // 8)]
                a_t = [jnp.zeros((8, HEAD), F32) for _ in range(SUB // 8)]
                for s in range(SUB):
                    row = pl.ds(lo + s, 1)
                    brow, krow, vrow = bs[hh, row, :], ks[hh, row, :], vs[hh, row, :]
                    dk_s = None
                    for ti in range(SUB // 8):
                        o8 = 8 * ti
                        if s > o8 + 7:
                            continue
                        d = bI[o8:o8 + 8] - brow
                        if s > o8:
                            d = jnp.where(t8 >= s - o8, d, NEG)
                        Es = jnp.exp(d)
                        qE = qI[o8:o8 + 8] * Es
                        col = jnp.sum(qE * krow, axis=1, keepdims=True)
                        a_t[ti] = jnp.where(lane == s, col, a_t[ti])
                        dcol = jnp.sum(doI[o8:o8 + 8] * vrow, axis=1, keepdims=True)
                        dq_t[ti] = dq_t[ti] + (dcol * Es) * krow
                        part = jnp.sum(dcol * qE, axis=0, keepdims=True)
                        dk_s = part if dk_s is None else dk_s + part
                    dks[hh, row, :] += dk_s
                    dbs[hh, row, :] -= krow * dk_s
                a_d = jnp.concatenate(a_t, axis=0)
                dq_d = jnp.concatenate(dq_t, axis=0)
                dvs[hh, lo:lo + SUB, :] += _tn(a_d, doI)[0:SUB]
                dqI = dqI + dq_d
                dbI = dbI + qI * dq_d
                dqs[hh, lo:lo + SUB, :] += dqI
                dbs[hh, lo:lo + SUB, :] += dbI
            kdk = kk * dki[hh]
            excl = _cumsum_rows(kdk) - kdk
            suff = _cumsum_rows(dbs[hh], reverse=True)
            gdec = jnp.sum(dS * st0, axis=0, keepdims=True) * jnp.exp(blast)
            dlf = suff + excl + gdec
            dk = dks[hh] + dki[hh]
            dfg = dlf / fg - dk
            dlb_ref[:, sl] += jnp.sum(dfg * (1.0 - sg), axis=0, keepdims=True)
            du_ref[0, rows, sl] = (dqs[hh] * _dsilu(qr, sq)).astype(BF16)
            du_ref[1, rows, sl] = (dfg * (1.0 - lbv_) * sg * (1.0 - sg)).astype(BF16)
            du_ref[2, rows, sl] = dvs[hh].astype(BF16)
            du_ref[3, rows, sl] = dgr.astype(BF16)
            dst[hh] = dS * jnp.exp(blast) + _tn(do_b, (q * eb).astype(BF16))

        def chunk(cc, carry):
            c = NC - 1 - cc
            rows = pl.ds(pl.multiple_of(c * CHUNK, CHUNK), CHUNK)
            for hh in range(HP):
                head(hh, c, rows)
            return carry

        lax.fori_loop(0, NC, chunk, 0)

    HP = HGRN_HEADS_BWD
    W = HP * HEAD

    def blk(qd):
        return pl.BlockSpec((RB, W), lambda h, r: (NB - 1 - r, qd * (H // HP) + h))

    hblk = pl.BlockSpec((RB, W), lambda h, r: (NB - 1 - r, h))
    sc = lambda: pltpu.VMEM((HP, CHUNK, HEAD), F32)
    return pl.pallas_call(
        body, grid=(H // HP, NB),
        in_specs=[blk(0), blk(1), blk(2), blk(3), pl.BlockSpec((1, W), lambda h, r: (0, h)),
                  pl.BlockSpec((1, HEAD), lambda h, r: (0, 0)), hblk, hblk,
                  pl.BlockSpec((HP, NC, HEAD, HEAD), lambda h, r: (h, NB - 1 - r, 0, 0))],
        out_specs=[pl.BlockSpec((4, RB, W), lambda h, r: (0, NB - 1 - r, h)),
                   pl.BlockSpec((1, W), lambda h, r: (0, h)), pl.BlockSpec((HP, 1, HEAD), lambda h, r: (h, 0, 0))],
        out_shape=[jax.ShapeDtypeStruct((4, T, H * HEAD), BF16), jax.ShapeDtypeStruct((1, H * HEAD), F32),
                   jax.ShapeDtypeStruct((H, 1, HEAD), F32)],
        scratch_shapes=[pltpu.VMEM((HP, HEAD, HEAD), F32)] + [sc() for _ in range(10)],
        compiler_params=_cp("parallel", "arbitrary"), name="hgrn_bwd")(u, u, u, u, lbv, gn, o, dy, s0)


def _softmax_rows(p_ref, L):
    rows = [p_ref[pl.ds(l, 1), :] for l in range(L)]
    m = rows[0]
    for r in rows[1:]:
        m = jnp.maximum(m, r)
    e = [jnp.exp(r - m) for r in rows]
    tot = e[0]
    for t in e[1:]:
        tot = tot + t
    return [t / tot for t in e]


def _lb_fwd(lbp):
    L, D = lbp.shape

    def body(p_ref, o_ref):
        sm = _softmax_rows(p_ref, L)
        acc = jnp.zeros((1, D), F32)
        o_ref[pl.ds(0, 1), :] = acc
        for l in range(1, L):
            acc = acc + sm[l]
            o_ref[pl.ds(l, 1), :] = acc

    return pl.pallas_call(body, out_shape=jax.ShapeDtypeStruct((L, D), F32), name="lb_fwd")(lbp)


def _lb_bwd(lbp, dlb):
    L, D = lbp.shape

    def body(p_ref, d_ref, o_ref):
        sm = _softmax_rows(p_ref, L)
        dsm = [jnp.zeros((1, D), F32)]
        for i in range(1, L):
            t = jnp.zeros((1, D), F32)
            for l in range(i, L):
                t = t + d_ref[pl.ds(l, 1), :]
            dsm.append(t)
        dot = jnp.zeros((1, D), F32)
        for i in range(L):
            dot = dot + dsm[i] * sm[i]
        for i in range(L):
            o_ref[pl.ds(i, 1), :] = sm[i] * (dsm[i] - dot)

    return pl.pallas_call(body, out_shape=jax.ShapeDtypeStruct((L, D), F32), name="lb_bwd")(lbp, dlb)


def _my_pos():
    return lax.axis_index("x"), lax.axis_index("y"), lax.axis_index("c")


def _peer(mask):
    x, y, c = _my_pos()
    mx, my, mc = (mask >> 2) & 1, (mask >> 1) & 1, mask & 1
    px = (1 - x) if mx else x
    py = (1 - y) if my else y
    pc = (1 - c) if mc else c
    return (px, py, pc), 4 * px + 2 * py + pc


def _all_gather(shards):
    n = len(shards)

    def body(*refs):
        ins, outs = refs[:n], refs[n:2 * n]
        send_sems, recv_sems, local_sems = refs[2 * n:]
        x, y, c = _my_pos()
        me = 4 * x + 2 * y + c
        local = [pltpu.make_async_copy(ins[a], outs[a].at[:, me], local_sems.at[a]) for a in range(n)]
        for cp in local:
            cp.start()
        sends = []
        for m in range(1, N_DEV):
            peer, _ = _peer(m)
            for a in range(n):
                cp = pltpu.make_async_remote_copy(
                    src_ref=ins[a], dst_ref=outs[a].at[:, me], send_sem=send_sems.at[a, m - 1],
                    recv_sem=recv_sems.at[a, m - 1], device_id=peer, device_id_type=MESH)
                cp.start()
                sends.append(cp)
        for m in range(1, N_DEV):
            peer, pid = _peer(m)
            for a in range(n):
                pltpu.make_async_remote_copy(
                    src_ref=ins[a], dst_ref=outs[a].at[:, pid], send_sem=send_sems.at[a, m - 1],
                    recv_sem=recv_sems.at[a, m - 1], device_id=peer, device_id_type=MESH).wait_recv()
        for cp in sends:
            cp.wait_send()
        for cp in local:
            cp.wait()

    out_shape = [jax.ShapeDtypeStruct((s.shape[0], N_DEV) + s.shape[1:], s.dtype) for s in shards]
    return pl.pallas_call(
        body, in_specs=[ANY] * n, out_specs=[ANY] * n, out_shape=out_shape,
        scratch_shapes=[pltpu.SemaphoreType.DMA((n, N_DEV - 1)), pltpu.SemaphoreType.DMA((n, N_DEV - 1)),
                        pltpu.SemaphoreType.DMA((n,))],
        name="all_gather_small")(*shards)


HBM_SPEC =pl.BlockSpec(memory_space=pltpu.HBM)
SEM_SPEC = pl.BlockSpec(memory_space=pltpu.SEMAPHORE)
EFFECT = pltpu.SideEffectType.DATAFLOW_SIDE_EFFECTING


def _hbm(a):
    return pltpu.with_memory_space_constraint(a, pltpu.HBM)


def _landing(block_shape, dtype, axis=0):
    if axis == 0:
        return lax.empty((N_DEV,) + tuple(block_shape), dtype)
    rows, n = block_shape
    return lax.empty((rows, N_DEV * n), dtype)


def _slot(ref, i):
    if len(ref.shape) == 2:
        n = ref.shape[1] // N_DEV
        return ref.at[:, pl.ds(i * n, n)]
    return ref.at[i]


def _push_start(name, srcs, lands, whole, groups):
    n = len(srcs)
    ng = 1 + max(groups)
    cnt = [groups.count(g) for g in range(ng)]
    idx = [groups[:a].count(groups[a]) for a in range(n)]

    def body(*refs):
        src_refs, land_refs = refs[:n], refs[n:2 * n]
        sems = refs[2 * n:2 * n + 3 * ng]
        token = refs[-1]
        x, y, c = _my_pos()
        me = 4 * x + 2 * y + c
        for a in range(n):
            g = groups[a]
            for m in range(1, N_DEV):
                peer, pid = _peer(m)
                pltpu.make_async_remote_copy(
                    src_ref=src_refs[a] if whole else src_refs[a].at[pid], dst_ref=_slot(land_refs[a], me),
                    send_sem=sems[3 * g].at[idx[a] * (N_DEV - 1) + m - 1],
                    recv_sem=sems[3 * g + 1].at[idx[a] * (N_DEV - 1) + m - 1],
                    device_id=peer, device_id_type=MESH).start()
            pltpu.make_async_copy(src_refs[a] if whole else src_refs[a].at[me], _slot(land_refs[a], me),
                                  sems[3 * g + 2].at[idx[a]]).start()
        token[...] = jnp.zeros_like(token)

    sem_shapes = []
    for g in range(ng):
        sem_shapes += [pltpu.SemaphoreType.DMA((cnt[g] * (N_DEV - 1),))] * 2 + [pltpu.SemaphoreType.DMA((cnt[g],))]
    thru = [pltpu.HBM(s.shape, s.dtype) for s in list(srcs) + list(lands)]
    res = pl.pallas_call(
        body, name=name,
        out_shape=tuple(sem_shapes + thru + [jax.ShapeDtypeStruct((8, 128), F32)]),
        in_specs=tuple([HBM_SPEC] * (2 * n)),
        out_specs=tuple([SEM_SPEC] * (3 * ng) + [HBM_SPEC] * (2 * n) + [pl.BlockSpec(memory_space=pltpu.VMEM)]),
        input_output_aliases={i: 3 * ng + i for i in range(2 * n)},
        compiler_params=pltpu.CompilerParams(has_side_effects=EFFECT),
    )(*[_hbm(s) for s in srcs], *[_hbm(z) for z in lands])
    sems = [(res[3 * g], res[3 * g + 1], res[3 * g + 2]) for g in range(ng)]
    srcs_thru = list(res[3 * ng:3 * ng + n])
    lands_thru = list(res[3 * ng + n:3 * ng + 2 * n])
    return sems, srcs_thru, lands_thru, res[-1]


def _push_wait(name, srcs_thru, lands_thru, sems, after, whole):
    n = len(srcs_thru)

    def body(*refs):
        src_refs, land_refs = refs[:n], refs[n:2 * n]
        send_sems, recv_sems, own_sems = refs[2 * n], refs[2 * n + 1], refs[2 * n + 2]
        x, y, c = _my_pos()
        me = 4 * x + 2 * y + c
        for a in range(n):
            pltpu.make_async_copy(src_refs[a] if whole else src_refs[a].at[me], _slot(land_refs[a], me),
                                  own_sems.at[a]).wait()
            for m in range(1, N_DEV):
                peer, pid = _peer(m)
                cp = pltpu.make_async_remote_copy(
                    src_ref=src_refs[a] if whole else src_refs[a].at[pid], dst_ref=_slot(land_refs[a], pid),
                    send_sem=send_sems.at[a * (N_DEV - 1) + m - 1], recv_sem=recv_sems.at[a * (N_DEV - 1) + m - 1],
                    device_id=peer, device_id_type=MESH)
                cp.wait_send()
                cp.wait_recv()

    thru = [pltpu.HBM(s.shape, s.dtype) for s in list(srcs_thru) + list(lands_thru)]
    res = pl.pallas_call(
        body, name=name, out_shape=tuple(thru),
        in_specs=tuple([HBM_SPEC] * (2 * n) + [SEM_SPEC, SEM_SPEC, SEM_SPEC, ANY]),
        out_specs=tuple([HBM_SPEC] * (2 * n)),
        input_output_aliases={i: i for i in range(2 * n)},
        compiler_params=pltpu.CompilerParams(has_side_effects=EFFECT),
    )(*srcs_thru, *lands_thru, sems[0], sems[1], sems[2], after)
    return list(res[n:])


def _adamw(recv, w, m, v, layer=0, prev=None):
    L, R, C = w.shape
    tr = _tile(R, max(8, (1 << 18) // C), 8) if R % 8 == 0 else R
    bc1 = 1.0 - ADAM_B1 ** ADAM_STEP
    bc2 = 1.0 - ADAM_B2 ** ADAM_STEP
    if prev is None:
        prev = [lax.empty((L, R, C), F32) for _ in range(4)]

    def body(r_ref, w_ref, m_ref, v_ref, p0, p1, p2, p3, g_ref, d_ref, nm_ref, nv_ref):
        g = r_ref[0].astype(F32)
        for s in range(1, N_DEV):
            g = g + r_ref[s].astype(F32)
        nm = ADAM_B1 * m_ref[...] + (1.0 - ADAM_B1) * g
        nv = ADAM_B2 * v_ref[...] + (1.0 - ADAM_B2) * (g * g)
        mh = nm / bc1
        vh = nv / bc2
        g_ref[...] = g
        d_ref[...] = -ADAM_LR * (mh / (jnp.sqrt(vh) + ADAM_EPS) + ADAM_WD * w_ref[...])
        nm_ref[...] = nm
        nv_ref[...] = nv

    row = pl.BlockSpec((None, tr, C), lambda i: (layer, i, 0))
    return pl.pallas_call(
        body, grid=(R // tr,),
        in_specs=[pl.BlockSpec((N_DEV, tr, C), lambda i: (0, i, 0)), row, row, row] + [ANY] * 4,
        out_specs=[row] * 4, out_shape=[jax.ShapeDtypeStruct((L, R, C), F32)] * 4,
        input_output_aliases={4: 0, 5: 1, 6: 2, 7: 3},
        compiler_params=_cp("parallel"), name="adamw")(recv, w, m, v, *prev)


def _full_w_spec(tk, tn):
    return pl.BlockSpec((tk, tn), lambda i, j, k: (k, j))


def kernel(x, meta_tokens, mix_norm_g, mlp_norm_g, final_norm_g, ev_w_in, ev_conv_w, ev_conv_b, ev_ln_g, ev_ln_b, ev_pool_w, ev_pool_b, ev_pool_scale, ev_w_out, od_w_in, od_gnorm_g, od_w_out, lb_param, mlp_w1, mlp_w2, loss_target, m_meta_tokens, m_mix_norm_g, m_mlp_norm_g, m_final_norm_g, m_ev_w_in, m_ev_conv_w, m_ev_conv_b, m_ev_ln_g, m_ev_ln_b, m_ev_pool_w, m_ev_pool_b, m_ev_pool_scale, m_ev_w_out, m_od_w_in, m_od_gnorm_g, m_od_w_out, m_lb_param, m_mlp_w1, m_mlp_w2, v_meta_tokens, v_mix_norm_g, v_mlp_norm_g, v_final_norm_g, v_ev_w_in, v_ev_conv_w, v_ev_conv_b, v_ev_ln_g, v_ev_ln_b, v_ev_pool_w, v_ev_pool_b, v_ev_pool_scale, v_ev_w_out, v_od_w_in, v_od_gnorm_g, v_od_w_out, v_lb_param, v_mlp_w1, v_mlp_w2):
    S, D = x.shape[1], x.shape[2]
    T = PAD + N_META + S
    DEPTH = mix_norm_g.shape[0]
    DFF = mlp_w1.shape[2] * N_DEV
    dev = 4 * lax.axis_index("x") + 2 * lax.axis_index("y") + lax.axis_index("c")

    g_meta, g_cw = _all_gather([meta_tokens[None], ev_conv_w])
    n_ev = ev_w_in.shape[0]
    n_od = od_w_in.shape[0]
    meta_full = jnp.transpose(g_meta[0], (1, 0, 2)).reshape(N_META, D)
    cw_full = jnp.transpose(g_cw, (0, 2, 1, 3)).reshape(n_ev, CONV_WIDTH, -1)
    cw_pad = jnp.pad(cw_full, ((0, 0), (0, 32 - CONV_WIDTH), (0, 0)))
    n_in_od = od_w_in.shape[2]
    n_w1 = mlp_w1.shape[2]

    ag_src, ag_grp, ag_axis, ag_at = [], [], [], {}
    for layer in range(DEPTH):
        j = layer // 2
        mixer = [("in", ev_w_in[j]), ("out", ev_w_out[j])] if layer % 2 == 0 else [("in", od_w_in[j]), ("out", od_w_out[j])]
        for pos, (key, arr) in enumerate(mixer + [("w1", mlp_w1[layer]), ("w2", mlp_w2[layer])]):
            ag_at[layer, key] = len(ag_src)
            ag_src.append(arr.astype(BF16))
            ag_grp.append(len(ag_grp))
            ag_axis.append(1 if key in ("in", "w1") and arr.shape[1] % 128 == 0 else 0)
    ag_src, g_meta, g_cw = lax.optimization_barrier((ag_src, g_meta, g_cw))
    ag_sems, ag_s, ag_l, ag_tok = _push_start(
        "ag_start", ag_src, [_landing(s_.shape, s_.dtype, ax) for s_, ax in zip(ag_src, ag_axis)], True, ag_grp)

    def ag_wait(layer, key, after):
        a = ag_at[layer, key]
        return _push_wait(f"ag_wait_{a}", [ag_s[a]], [ag_l[a]], ag_sems[a], after, True)[0]

    h = jnp.concatenate([jnp.zeros((PAD, D), F32), meta_full, x[0]], axis=0) + ag_tok[0, 0]
    tgt = jnp.pad(loss_target[0], ((PAD + N_META, 0), (0, 0)))
    lb_all = _lb_fwd(lb_param)

    tm_big = _tile(T, MM_ROWS_BIG, 16)
    tm_mid = _tile(T, MM_ROWS_MID, 16)
    tm_k4 = _tile(T, MM_ROWS_K4, 16)

    saved = []
    for layer in range(DEPTH):
        j = layer // 2
        sv = {"h0": h}
        g_in = ag_wait(layer, "in", h)
        w_in = g_in if g_in.ndim == 2 else jnp.transpose(g_in, (1, 0, 2)).reshape(D, -1)
        if layer % 2 == 0:
            sv["n"], u = _mm_rms_nn("ev_in", h, mix_norm_g[layer][None], w_in, tm_big, 512, "f32")
            yab, yc = _ev_fwd(u, cw_pad[j], ev_conv_b[j][None], ev_ln_g[j][None], ev_ln_b[j][None],
                              ev_pool_w[j].astype(BF16), ev_pool_b[j].reshape(1, -1), ev_pool_scale[j][None])
            sv.update(u=u, y=yab, yc=yc)
            w_out = ag_wait(layer, "out", yab).reshape(-1, D)
            h = _mm_nn("ev_out", yab, w_out, _full_w_spec, T, D, D, tm_mid, D, D, "resid", extra=h)
        else:
            sv["n"], u = _mm_rms_nn("od_in", h, mix_norm_g[layer][None], w_in, tm_big, 512, "f32")
            y, o, s0 = _hgrn_fwd(u, lb_all[layer][None], od_gnorm_g[j][None])
            sv.update(u=u, y=y, o=o, s0=s0)
            w_out = ag_wait(layer, "out", y).reshape(-1, D)
            h = _mm_nn("od_out", y, w_out, _full_w_spec, T, D, D, tm_mid, D, D, "resid", extra=h)
        sv["h1"] = h
        w_w1 = ag_wait(layer, "w1", h)
        n2, r, act = _mm_rms_nn("mlp_w1", h, mlp_norm_g[layer][None], w_w1, tm_big, 512, "relu2")
        w_w2 = ag_wait(layer, "w2", act).reshape(DFF, D)
        sv.update(w_in=w_in, w_out=w_out, w_w1=w_w1, w_w2=w_w2)
        sv.update(n2=n2, r=r, act=act)
        h = _mm_nn("mlp_w2", act, w_w2, _full_w_spec, T, D, DFF, tm_k4, D, DFF, "resid", extra=h)
        saved.append(sv)

    loss_blk, dh, dhb, dg_final = _loss_head(h, final_norm_g[None], tgt)
    loss = lax.psum(loss_blk[0, 0], AXES)

    tt = T
    g_mix, g_mlp = [None] * DEPTH, [None] * DEPTH
    small ={"cw": [None] * n_ev, "vec": [None] * n_ev, "pw": [None] * n_ev, "gn": [None] * n_od}
    dlb_rows = [jnp.zeros((1, D), F32) for _ in range(DEPTH)]

    def xs2(tt_, tk):
        return pl.BlockSpec((tt_, tk), lambda a, b, t: (t, a))

    def ys2(tt_, tn):
        return pl.BlockSpec((tt_, tn), lambda a, b, t: (t, b))

    def os2(tk, tn):
        return pl.BlockSpec((tk, tn), lambda a, b, t: (a, b))

    def os3(tk, tn):
        return pl.BlockSpec((None, tk, tn), lambda a, b, t: (b, a, 0))

    def dy2(tm, tn):
        return pl.BlockSpec((tm, tn), lambda i, jj, k: (i, k))

    def w_rows(tj, tn):
        return pl.BlockSpec((tj, tn), lambda i, jj, k: (jj, k))

    def w_whole(tj, tn):
        return pl.BlockSpec((tj, tn), lambda i, jj, k: (0, 0), pipeline_mode=pl.Buffered(1))

    rs_pending = []

    def rs_start(tag, mats):
        blocks = [m_ if m_.ndim == 3 else m_.reshape(N_DEV, m_.shape[0] // N_DEV, m_.shape[1]) for m_ in mats]
        lands = [_landing(b_.shape[1:], b_.dtype) for b_ in blocks]
        sems, s_thru, l_thru, tok = _push_start(f"rs_start_{tag}", blocks, lands, False, [0] * len(blocks))
        rs_pending.append((tag, s_thru, l_thru, sems[0]))
        return tok[0, 0]

    for layer in reversed(range(DEPTH)):
        j = layer // 2
        sv = saved[layer]
        da1 = _mm_nt("mlp_w2_t", dhb, sv["w_w2"], dy2, w_rows, T, DFF, D, tm_mid, 1024, D, "dact", extra=sv["r"])
        dw2 = _mm_tn("mlp_dw2", sv["act"], dhb, xs2, ys2, os2, (DFF, D), T, DFF, D, tt, 512, D)
        dw1 = _mm_tn("mlp_dw1", sv["n2"], da1, xs2, ys2, os3, (N_DEV, D, n_w1), T, D, DFF, tt, D, n_w1)
        tok = rs_start(f"mlp{layer}", [dw1, dw2])
        dh, dhb, g_mlp[layer] = _mm_nt("mlp_w1_t", da1, sv["w_w1"], dy2, w_whole, T, D, DFF, tm_k4, D, DFF, "rms",
                                       extra=(sv["h1"], mlp_norm_g[layer][None] + tok, dh))
        if layer % 2 == 0:
            dyab = _mm_nt("ev_out_t", dhb, sv["w_out"], dy2, w_rows, T, D, D, tm_mid, D, D, "f32")
            dwout = _mm_tn("ev_dwout", sv["y"], dhb, xs2, ys2, os2, (D, D), T, D, D, tt, 512, D)
            du, small["cw"][j], small["vec"][j], small["pw"][j] = _ev_bwd(
                dyab, sv["yc"], sv["u"], cw_pad[j], ev_ln_g[j][None], ev_ln_b[j][None], ev_pool_w[j].astype(BF16),
                jnp.transpose(ev_pool_w[j], (0, 2, 1)).astype(BF16), ev_pool_b[j].reshape(1, -1),
                ev_pool_scale[j][None])
            nin = du.shape[1]
            dwin = _mm_tn("ev_dwin", sv["n"], du, xs2, ys2, os2, (D, nin), T, D, nin, tt, D, 512)
            dwin = jnp.transpose(dwin.reshape(D, N_DEV, nin // N_DEV), (1, 0, 2))
            tok = rs_start(f"mix{layer}", [dwin, dwout])
            dh, dhb, g_mix[layer] = _mm_nt("ev_in_t", du, sv["w_in"], dy2, w_whole, T, D, nin, tm_k4, D, nin, "rms",
                                           extra=(sv["h0"], mix_norm_g[layer][None] + tok, dh))
        else:
            dy = _mm_nt("od_out_t", dhb, sv["w_out"], dy2, w_rows, T, D, D, tm_mid, D, D, "f32")
            dwout = _mm_tn("od_dwout", sv["y"], dhb, xs2, ys2, os2, (D, D), T, D, D, tt, 512, D)
            du3, dlb_rows[layer], small["gn"][j] = _hgrn_bwd(dy, sv["o"], sv["s0"], sv["u"], lb_all[layer][None],
                                                              od_gnorm_g[j][None])
            per = D // n_in_od

            def du_t(tt_, tn):
                return pl.BlockSpec((None, tt_, tn), lambda a, b, t: (b // per, t, b % per))

            dwin = _mm_tn("od_dwin", sv["n"], du3, xs2, du_t, os3, (N_DEV, D, n_in_od), T, D, 4 * D, tt, D, n_in_od)
            tok = rs_start(f"mix{layer}", [dwin, dwout])
            dh, dhb, g_mix[layer] = _mm_nt(
                "od_in_t", du3, sv["w_in"], lambda tm, tn: pl.BlockSpec((4, tm, tn // 4), lambda i, jj, k: (0, i, 0)),
                w_whole, T, D, 4 * D, tm_k4, D, 4 * D, "rms", extra=(sv["h0"], mix_norm_g[layer][None] + tok, dh),
                parts=4)

    dmeta = dh[PAD:PAD + N_META]
    grad_x = dh[PAD + N_META:][None]
    dlb_param = _lb_bwd(lb_param, jnp.concatenate(dlb_rows, axis=0))

    pieces = [
        ("final", dg_final), ("pad", jnp.zeros((SMALL_F32_ROWS - 1, D), F32)),
        ("meta", dmeta), ("mix", jnp.concatenate(g_mix, 0)), ("mlp", jnp.concatenate(g_mlp, 0)),
        ("cw", jnp.stack([c[:CONV_WIDTH] for c in small["cw"]])), ("cb", jnp.stack([v_[0] for v_ in small["vec"]])),
        ("lng", jnp.stack([v_[1] for v_ in small["vec"]])), ("lnb", jnp.stack([v_[2] for v_ in small["vec"]])),
        ("pw", jnp.stack(small["pw"])), ("pb", jnp.stack([v_[4] for v_ in small["vec"]])),
        ("ps", jnp.stack([v_[3] for v_ in small["vec"]])), ("gn", jnp.stack([jnp.sum(g_, axis=0)[0] for g_ in small["gn"]])),
        ("lb", dlb_param),
    ]
    flat = jnp.concatenate([p.reshape(-1) for _, p in pieces])
    n_small = flat.shape[0]
    rows_small = SMALL_F32_ROWS + -(-(n_small // 1024 + 1 - SMALL_F32_ROWS) // 16) * 16
    flat = jnp.pad(flat, (0, rows_small * 1024 - n_small)).reshape(rows_small, 1024)

    sm_src = [flat[:SMALL_F32_ROWS], flat[SMALL_F32_ROWS:].astype(BF16)]
    sm_sems, sm_s, sm_l, sm_tok = _push_start("small_start", sm_src, [_landing(a_.shape, a_.dtype) for a_ in sm_src],
                                              True, [0, 0])
    recv = {}
    for tag, s_thru, l_thru, sems in rs_pending:
        got = _push_wait(f"rs_wait_{tag}", s_thru, l_thru, sems, sm_tok, False)
        layer = int(tag[3:])
        if tag.startswith("mlp"):
            recv["w1", layer], recv["w2", layer] = got
        else:
            key = "ev" if layer % 2 == 0 else "od"
            recv[key + "_in", layer // 2], recv[key + "_out", layer // 2] = got

    outs = {}
    big = {"ev_in": ("ev_w_in", ev_w_in, m_ev_w_in, v_ev_w_in), "ev_out": ("ev_w_out", ev_w_out, m_ev_w_out, v_ev_w_out),
           "od_in": ("od_w_in", od_w_in, m_od_w_in, v_od_w_in), "od_out": ("od_w_out", od_w_out, m_od_w_out, v_od_w_out),
           "w1": ("mlp_w1", mlp_w1, m_mlp_w1, v_mlp_w1), "w2": ("mlp_w2", mlp_w2, m_mlp_w2, v_mlp_w2)}
    for key, (name, w, m, v) in big.items():
        res = None
        for l in range(w.shape[0]):
            res = _adamw(recv[key, l], w, m, v, layer=l, prev=res)
        outs[name] = res

    small_params = {
        "meta": ("meta_tokens", None), "mix": ("mix_norm_g", mix_norm_g, m_mix_norm_g, v_mix_norm_g),
        "mlp": ("mlp_norm_g", mlp_norm_g, m_mlp_norm_g, v_mlp_norm_g),
        "final": ("final_norm_g", final_norm_g, m_final_norm_g, v_final_norm_g),
        "cw": ("ev_conv_w", None), "cb": ("ev_conv_b", ev_conv_b, m_ev_conv_b, v_ev_conv_b),
        "lng": ("ev_ln_g", ev_ln_g, m_ev_ln_g, v_ev_ln_g), "lnb": ("ev_ln_b", ev_ln_b, m_ev_ln_b, v_ev_ln_b),
        "pw": ("ev_pool_w", ev_pool_w, m_ev_pool_w, v_ev_pool_w), "pb": ("ev_pool_b", ev_pool_b, m_ev_pool_b, v_ev_pool_b),
        "ps": ("ev_pool_scale", ev_pool_scale, m_ev_pool_scale, v_ev_pool_scale),
        "gn": ("od_gnorm_g", od_gnorm_g, m_od_gnorm_g, v_od_gnorm_g), "lb": ("lb_param", lb_param, m_lb_param, v_lb_param),
    }
    csh = ev_conv_w.shape[2]
    msh = meta_tokens.shape[1]

    def packed(which):
        parts = []
        for key, g_ in pieces:
            ent = small_params.get(key)
            if key == "pad":
                full = g_
            elif key == "meta":
                src = (meta_tokens, m_meta_tokens, v_meta_tokens)[which]
                full = lax.dynamic_update_slice(jnp.zeros((N_META, D), F32), src, (0, dev * msh))
            elif key == "cw":
                src = (ev_conv_w, m_ev_conv_w, v_ev_conv_w)[which]
                full = lax.dynamic_update_slice(jnp.zeros(g_.shape, F32), src, (0, 0, dev * csh))
            else:
                full = ent[1 + which]
            parts.append(full.reshape(-1))
        f = jnp.concatenate(parts)
        return jnp.pad(f, (0, rows_small * 1024 - n_small)).reshape(rows_small, 1024)

    got_f32, got_bf16 = _push_wait("small_wait", sm_s, sm_l, sm_sems[0], outs["mlp_w2"][0], True)
    recv_small = jnp.concatenate([got_f32, got_bf16.astype(F32)], axis=1)
    sres = [r_[0] for r_ in _adamw(recv_small, packed(0)[None], packed(1)[None], packed(2)[None])]
    off = 0
    for key, g_ in pieces:
        size = g_.size
        vals = [r_.reshape(-1)[off:off + size].reshape(g_.shape) for r_ in sres]
        off += size
        if key == "pad":
            continue
        name = small_params[key][0]
        if key == "meta":
            vals = [lax.dynamic_slice(v_, (0, dev * msh), (N_META, msh)) for v_ in vals]
        elif key == "cw":
            vals = [lax.dynamic_slice(v_, (0, 0, dev * csh), v_.shape[:2] + (csh,)) for v_ in vals]
        else:
            vals = [v_.reshape(small_params[key][1].shape) for v_ in vals]
        outs[name] = vals

    names = ["meta_tokens", "mix_norm_g", "mlp_norm_g", "final_norm_g", "ev_w_in", "ev_conv_w", "ev_conv_b", "ev_ln_g",
             "ev_ln_b", "ev_pool_w", "ev_pool_b", "ev_pool_scale", "ev_w_out", "od_w_in", "od_gnorm_g", "od_w_out",
             "lb_param", "mlp_w1", "mlp_w2"]
    result = [loss, grad_x]
    for k in range(4):
        result += [outs[nm][k] for nm in names]
    return tuple(result)
```

```python
import functools

import jax
import jax.numpy as jnp
from jax import lax
from jax.experimental import pallas as pl
from jax.experimental.pallas import tpu as pltpu

F32 = jnp.float32
BF16 = jnp.bfloat16

N_DEV = 8
N_META = 16
CHUNK = 64
PAD = CHUNK - N_META
SUB = 16
HEAD = 128
CONV_WIDTH = 31
HALO = 32
POOL_WINDOWS = (2, 4, 8, 16)
EPS = 1e-6
NEG = -1e30
ADAM_LR, ADAM_B1, ADAM_B2, ADAM_EPS, ADAM_WD, ADAM_STEP = 0.001, 0.9, 0.999, 1e-08, 0.01, 10
VMEM_LIMIT = 56 * 1024 * 1024
EV_ROWS = 416
HGRN_ROWS = 832
MM_ROWS_BIG = 2080
MM_ROWS_MID = 1040
MM_ROWS_K4 = 416
SMALL_F32_ROWS = 8
HGRN_HEADS_FWD = 8
HGRN_HEADS_BWD = 2
MESH = pl.DeviceIdType.MESH
AXES = ("x", "y", "c")
ANY = pl.BlockSpec(memory_space=pl.ANY)


def _cp(*sem):
    return pltpu.CompilerParams(dimension_semantics=sem, vmem_limit_bytes=VMEM_LIMIT)


def _tile(n, cap, mult):
    best = None
    for d in range(mult, min(n, cap) + 1, mult):
        if n % d == 0:
            best = d
    assert best is not None, (n, cap, mult)
    return best


def _nt(a, b):
    return lax.dot_general(a, b, (((1,), (1,)), ((), ())), preferred_element_type=F32)


def _tn(a, b):
    return lax.dot_general(a, b, (((0,), (0,)), ((), ())), preferred_element_type=F32)


def _nn(a, b):
    return jnp.dot(a, b, preferred_element_type=F32)


def _r16(x):
    return x.astype(BF16).astype(F32)


def _row_ids(base, n):
    return base + lax.broadcasted_iota(jnp.int32, (n, 1), 0)


def _dsilu(x, s):
    return s * (1.0 + x * (1.0 - s))


def _loss_head(h, g, tgt):
    T, D = h.shape
    tm = _tile(T, MM_ROWS_MID, 16)
    first_x = PAD + N_META

    def body(h_ref, g_ref, t_ref, loss_ref, dh_ref, dhb_ref, dg_ref):
        i = pl.program_id(0)
        x = h_ref[...]
        r = lax.rsqrt(jnp.mean(x * x, axis=-1, keepdims=True) + EPS)
        xh = x * r
        gv = g_ref[...]
        out = xh * gv
        valid = _row_ids(i * tm, tm) >= first_x
        e = jnp.where(valid, out - t_ref[...], 0.0)
        dout = e * (1.0 / D)
        dxh = dout * gv
        dx = r * (dxh - xh * jnp.mean(dxh * xh, axis=-1, keepdims=True))
        dh_ref[...] = dx
        dhb_ref[...] = dx.astype(BF16)

        @pl.when(i == 0)
        def _():
            dg_ref[...] = jnp.zeros_like(dg_ref)
            loss_ref[...] = jnp.zeros_like(loss_ref)

        dg_ref[...] += jnp.sum(dout * xh, axis=0, keepdims=True)
        loss_ref[...] += 0.5 * jnp.sum(jnp.mean(e * e, axis=-1, keepdims=True))

    row = pl.BlockSpec((tm, D), lambda i: (i, 0))
    vec = pl.BlockSpec((1, D), lambda i: (0, 0))
    return pl.pallas_call(
        body, grid=(T // tm,),
        in_specs=[row, vec, row],
        out_specs=[pl.BlockSpec((8, 128), lambda i: (0, 0)), row, row, vec],
        out_shape=[jax.ShapeDtypeStruct((8, 128), F32), jax.ShapeDtypeStruct((T, D), F32),
                   jax.ShapeDtypeStruct((T, D), BF16), jax.ShapeDtypeStruct((1, D), F32)],
        compiler_params=_cp("arbitrary"), name="loss_head")(h, g, tgt)


def _mm_nn(name, a, w, w_spec, M, N, K, tm, tn, tk, mode, extra=None, a_spec=None):
    nk = K // tk
    if a_spec is None:
        a_spec = pl.BlockSpec((tm, tk), lambda i, j, k: (i, k))
    o_spec = pl.BlockSpec((tm, tn), lambda i, j, k: (i, j))

    def body(*refs):
        if mode == "resid":
            a_ref, w_ref, e_ref = refs[:3]
            outs = refs[3:]
        else:
            a_ref, w_ref = refs[:2]
            outs = refs[2:]
        acc_ref = outs[-1] if nk > 1 else None
        part = _nn(a_ref[...], w_ref[...])

        def finish(acc):
            if mode == "f32":
                outs[0][...] = acc
            elif mode == "relu2":
                r = jnp.maximum(acc, 0.0)
                outs[0][...] = r.astype(BF16)
                outs[1][...] = (r * r).astype(BF16)
            else:
                keep = _row_ids(pl.program_id(0) * tm, tm) >= PAD
                outs[0][...] = jnp.where(keep, e_ref[...] + acc, 0.0)

        if nk == 1:
            finish(part)
        else:
            k = pl.program_id(2)

            @pl.when(k == 0)
            def _():
                acc_ref[...] = part

            @pl.when(k > 0)
            def _():
                acc_ref[...] += part

            @pl.when(k == nk - 1)
            def _():
                finish(acc_ref[...])

    in_specs = [a_spec, w_spec(tk, tn)]
    args = [a, w]
    if mode == "resid":
        in_specs.append(o_spec)
        args.append(extra)
    if mode == "relu2":
        out_specs = [o_spec, o_spec]
        out_shape = [jax.ShapeDtypeStruct((M, N), BF16)] * 2
    else:
        out_specs = [o_spec]
        out_shape = [jax.ShapeDtypeStruct((M, N), F32)]
    scratch = [pltpu.VMEM((tm, tn), F32)] if nk > 1 else []
    res = pl.pallas_call(
        body, grid=(M // tm, N // tn, nk), in_specs=in_specs, out_specs=out_specs, out_shape=out_shape,
        scratch_shapes=scratch, compiler_params=_cp("parallel", "parallel", "arbitrary"), name=name)(*args)
    return res if mode == "relu2" else res[0]


def _mm_rms_nn(name, h, g, w, tm, tn, mode):
    M, K = h.shape
    N = w.shape[1]

    def body(h_ref, g_ref, w_ref, n_ref, *outs):
        @pl.when(pl.program_id(1) == 0)
        def _():
            ch = _tile(tm, 256, 16)

            def chunk(c, carry):
                rows = pl.ds(pl.multiple_of(c * ch, ch), ch)
                x = h_ref[rows, :]
                r = lax.rsqrt(jnp.mean(x * x, axis=-1, keepdims=True) + EPS)
                n_ref[rows, :] = ((x * r) * g_ref[...]).astype(BF16)
                return carry

            lax.fori_loop(0, tm // ch, chunk, 0)

        acc = _nn(n_ref[...], w_ref[...])
        if mode == "f32":
            outs[0][...] = acc
        else:
            r = jnp.maximum(acc, 0.0)
            outs[0][...] = r.astype(BF16)
            outs[1][...] = (r * r).astype(BF16)

    row = pl.BlockSpec((tm, K), lambda i, j: (i, 0))
    o_spec = pl.BlockSpec((tm, tn), lambda i, j: (i, j))
    n_out = 1 if mode == "f32" else 2
    return pl.pallas_call(
        body, grid=(M // tm, N // tn),
        in_specs=[row, pl.BlockSpec((1, K), lambda i, j: (0, 0)), pl.BlockSpec((K, tn), lambda i, j: (0, j))],
        out_specs=[row] + [o_spec] * n_out,
        out_shape=[jax.ShapeDtypeStruct((M, K), BF16)] + [jax.ShapeDtypeStruct((M, N), F32 if mode == "f32" else BF16)] * n_out,
        compiler_params=_cp("parallel", "arbitrary"), name=name)(h, g, w)


def _mm_nt(name, dy, w, dy_spec, w_spec, M, J, N, tm, tj, tn, mode, extra=None, parts=1):
    nk = N // tn
    o_spec = pl.BlockSpec((tm, tj), lambda i, j, k: (i, j))
    n_extra = {"f32": 0, "dact": 1, "rms": 3}[mode]
    if mode == "rms":
        assert nk == 1 and tj == J

    def body(*refs):
        dy_ref, w_ref = refs[:2]
        ex = refs[2:2 + n_extra]
        outs = refs[2 + n_extra:]
        acc_ref = outs[-1] if nk > 1 else None
        if parts == 1:
            part = _nt(dy_ref[...], w_ref[...])
        else:
            wq = tn // parts
            part = _nt(dy_ref[0], w_ref[:, 0:wq])
            for q in range(1, parts):
                part = part + _nt(dy_ref[q], w_ref[:, q * wq:(q + 1) * wq])

        def finish(acc):
            if mode == "f32":
                outs[0][...] = acc
            elif mode == "dact":
                outs[0][...] = (acc * (2.0 * ex[0][...].astype(F32))).astype(BF16)
            else:
                h_ref, g_ref, dres_ref = ex
                dh_ref, dhb_ref, dg_ref = outs[:3]
                i = pl.program_id(0)

                @pl.when(i == 0)
                def _():
                    dg_ref[...] = jnp.zeros_like(dg_ref)

                ch = _tile(tm, 256, 16)
                for c0 in range(0, tm, ch):
                    a_c = acc[c0:c0 + ch]
                    x = h_ref[c0:c0 + ch, :]
                    r = lax.rsqrt(jnp.mean(x * x, axis=-1, keepdims=True) + EPS)
                    xh = x * r
                    dxh = a_c * g_ref[...]
                    dx = r * (dxh - xh * jnp.mean(dxh * xh, axis=-1, keepdims=True))
                    keep = _row_ids(i * tm + c0, ch) >= PAD
                    dh = jnp.where(keep, dres_ref[c0:c0 + ch, :] + dx, 0.0)
                    dh_ref[c0:c0 + ch, :] = dh
                    dhb_ref[c0:c0 + ch, :] = dh.astype(BF16)
                    dg_ref[...] += jnp.sum(a_c * xh, axis=0, keepdims=True)

        if nk == 1:
            finish(part)
        else:
            k = pl.program_id(2)

            @pl.when(k == 0)
            def _():
                acc_ref[...] = part

            @pl.when(k > 0)
            def _():
                acc_ref[...] += part

            @pl.when(k == nk - 1)
            def _():
                finish(acc_ref[...])

    in_specs = [dy_spec(tm, tn), w_spec(tj, tn)]
    args = [dy, w]
    scratch = [pltpu.VMEM((tm, tj), F32)] if nk > 1 else []
    if mode == "rms":
        vec = pl.BlockSpec((1, J), lambda i, j, k: (0, 0))
        h, g, dres = extra
        res = pl.pallas_call(
            body, grid=(M // tm, 1, 1), in_specs=in_specs + [o_spec, vec, o_spec], out_specs=[o_spec, o_spec, vec],
            out_shape=[jax.ShapeDtypeStruct((M, J), F32), jax.ShapeDtypeStruct((M, J), BF16),
                       jax.ShapeDtypeStruct((1, J), F32)],
            compiler_params=_cp("arbitrary", "arbitrary", "arbitrary"), name=name)(*args, h, g, dres)
        return res
    if mode == "dact":
        in_specs.append(o_spec)
        args.append(extra)
    return pl.pallas_call(
        body, grid=(M // tm, J // tj, nk), in_specs=in_specs, out_specs=[o_spec],
        out_shape=[jax.ShapeDtypeStruct((M, J), BF16 if mode == "dact" else F32)],
        scratch_shapes=scratch, compiler_params=_cp("parallel", "parallel", "arbitrary"), name=name)(*args)[0]


def _mm_tn(name, x, dy, x_spec, dy_spec, o_spec, o_shape, T, K, N, tt, tk, tn):
    nt = T // tt

    def body(x_ref, dy_ref, o_ref, *acc):
        part = _tn(x_ref[...], dy_ref[...])
        if nt == 1:
            o_ref[...] = part.astype(BF16)
            return
        acc_ref = acc[0]
        t = pl.program_id(2)

        @pl.when(t == 0)
        def _():
            acc_ref[...] = part

        @pl.when(t > 0)
        def _():
            acc_ref[...] += part

        @pl.when(t == nt - 1)
        def _():
            o_ref[...] = acc_ref[...].astype(BF16)

    return pl.pallas_call(
        body, grid=(K // tk, N // tn, nt), in_specs=[x_spec(tt, tk), dy_spec(tt, tn)], out_specs=o_spec(tk, tn),
        out_shape=jax.ShapeDtypeStruct(o_shape, BF16), scratch_shapes=[pltpu.VMEM((tk, tn), F32)] if nt > 1 else [],
        compiler_params=_cp("parallel", "parallel", "arbitrary"), name=name)(x, dy)


def _pool_counts(base, n, w):
    pos = _row_ids(base, n) - PAD
    return jnp.clip(pos + 1, 1, w).astype(F32)


def _shifted_copies(buf, rows):
    buf[0, rows:rows + 8, :] = jnp.zeros((8, buf.shape[2]), F32)

    def blk(s, carry):
        b = pl.multiple_of(s * HALO, HALO)
        win = buf[0, pl.ds(b, HALO + 8), :]
        for r in range(1, 8):
            buf[r, pl.ds(b, HALO), :] = win[r:r + HALO]
        return carry

    lax.fori_loop(0, rows // HALO, blk, 0)


def _ev_fwd(u, cw, cb, lg, lb, pw, pb, ps):
    T = u.shape[0]
    C = 512
    tm = _tile(T, EV_ROWS, HALO)
    nsub = tm // HALO
    hb = tm // HALO

    def body(val_ref, gate_ref, pin_ref, valh_ref, gateh_ref, pinh_ref, cw_ref, cb_ref, lg_ref, lb_ref, pw_ref,
             pb_ref, ps_ref, yab_ref, yc_ref, a_ext, p_ext, d_buf):
        i = pl.program_id(0)
        nf = (i > 0).astype(F32)
        a_ext[0, 0:HALO, :] = valh_ref[...] * jax.nn.sigmoid(gateh_ref[...]) * nf
        a_ext[0, HALO:HALO + tm, :] = val_ref[...] * jax.nn.sigmoid(gate_ref[...])
        p_ext[0:HALO, :] = pinh_ref[...] * nf
        p_ext[HALO:, :] = pin_ref[...]
        _shifted_copies(a_ext, tm + HALO)

        def sub(s, carry):
            base = pl.multiple_of(s * HALO, HALO)
            acc = jnp.zeros((HALO, C), F32) + cb_ref[...]
            for j in range(CONV_WIDTH):
                off = 2 + j
                acc = acc + cw_ref[pl.ds(j, 1), :] * a_ext[off % 8, pl.ds(pl.multiple_of(base + off // 8 * 8, 8), HALO), :]
            yc_ref[pl.ds(base, HALO), :] = acc
            mu = jnp.mean(acc, axis=-1, keepdims=True)
            yc = acc - mu
            rstd = lax.rsqrt(jnp.mean(yc * yc, axis=-1, keepdims=True) + EPS)
            z = (yc * rstd) * lg_ref[...] + lb_ref[...]
            yab_ref[pl.ds(base, HALO), 0:C] = (z * jax.nn.sigmoid(z)).astype(BF16)
            pwin = p_ext[pl.ds(base, 2 * HALO), :]
            for gi, w in enumerate(POOL_WINDOWS):
                lo, hi = gi * HEAD, (gi + 1) * HEAD
                x = pwin[HALO:, lo:hi]
                tot = x
                for k in range(1, w):
                    tot = tot + pwin[HALO - k:2 * HALO - k, lo:hi]
                cnt = _pool_counts(i * tm + base, HALO, w)
                d_buf[pl.ds(base, HALO), lo:hi] = (tot / cnt - x).astype(BF16)
            return carry

        lax.fori_loop(0, nsub, sub, 0, unroll=2)
        for gi in range(len(POOL_WINDOWS)):
            lo, hi = gi * HEAD, (gi + 1) * HEAD
            y = _nn(d_buf[:, lo:hi], pw_ref[gi]) + pb_ref[:, lo:hi]
            yab_ref[:, C + lo:C + hi] = (y * ps_ref[:, lo:hi]).astype(BF16)

    def main(c):
        return pl.BlockSpec((tm, C), lambda i: (i, c))

    def halo(c):
        return pl.BlockSpec((HALO, C), lambda i: (jnp.maximum(i * hb - 1, 0), c))

    vec = pl.BlockSpec((1, C), lambda i: (0, 0))
    return pl.pallas_call(
        body, grid=(T // tm,),
        in_specs=[main(0), main(1), main(2), halo(0), halo(1), halo(2),
                  pl.BlockSpec((32, C), lambda i: (0, 0)), vec, vec, vec,
                  pl.BlockSpec((4, HEAD, HEAD), lambda i: (0, 0, 0)), vec, vec],
        out_specs=[pl.BlockSpec((tm, 2 * C), lambda i: (i, 0)), pl.BlockSpec((tm, C), lambda i: (i, 0))],
        out_shape=[jax.ShapeDtypeStruct((T, 2 * C), BF16), jax.ShapeDtypeStruct((T, C), F32)],
        scratch_shapes=[pltpu.VMEM((8, tm + HALO + 8, C), F32), pltpu.VMEM((tm + HALO, C), F32),
                        pltpu.VMEM((tm, C), BF16)],
        compiler_params=_cp("parallel"), name="ev_fwd")(u, u, u, u, u, u, cw, cb, lg, lb, pw, pb, ps)


def _ev_bwd(dyab, yc, u, cw, lg, lb, pw, pwt, pb, ps):
    T = u.shape[0]
    C = 512
    tm = _tile(T, EV_ROWS, HALO)
    nsub = tm // HALO
    hb = tm // HALO
    nblk = T // tm
    E = tm + HALO

    def body(dya_ref, dyb_ref, dyah_ref, dybh_ref, yc_ref, ych_ref, val_ref, gate_ref, pin_ref, valh_ref, gateh_ref,
             pinh_ref, cw_ref, lg_ref, lb_ref, pw_ref, pwt_ref, pb_ref, ps_ref,
             du_ref, dcw_ref, dvec_ref, dpw_ref,
             dy_ext, a_ext, p_ext, ddc_ext, dd_buf, d_buf, dpre_buf, dcw_acc, vec_acc):
        i = pl.program_id(0)
        nf = (i > 0).astype(F32)
        nl = (i < nblk - 1).astype(F32)

        @pl.when(i == 0)
        def _():
            dcw_ref[...] = jnp.zeros_like(dcw_ref)
            dvec_ref[...] = jnp.zeros_like(dvec_ref)
            dpw_ref[...] = jnp.zeros_like(dpw_ref)

        dcw_acc[...] = jnp.zeros_like(dcw_acc)
        vec_acc[...] = jnp.zeros_like(vec_acc)
        a_ext[0, 0:HALO, :] = valh_ref[...] * jax.nn.sigmoid(gateh_ref[...]) * nf
        a_ext[0, HALO:E, :] = val_ref[...] * jax.nn.sigmoid(gate_ref[...])
        p_ext[0:HALO, :] = pinh_ref[...] * nf
        p_ext[HALO:, :] = pin_ref[...]
        _shifted_copies(a_ext, E)

        def ln_bwd(y, dya, main):
            mu = jnp.mean(y, axis=-1, keepdims=True)
            ycen = y - mu
            rstd = lax.rsqrt(jnp.mean(ycen * ycen, axis=-1, keepdims=True) + EPS)
            yh = ycen * rstd
            z = yh * lg_ref[...] + lb_ref[...]
            sz = jax.nn.sigmoid(z)
            dz = dya * _dsilu(z, sz)
            dyh = dz * lg_ref[...]
            dy = rstd * (dyh - jnp.mean(dyh, axis=-1, keepdims=True) - yh * jnp.mean(dyh * yh, axis=-1, keepdims=True))
            if main:
                vec_acc[1] += jnp.sum((dz * yh).reshape(HALO // 8, 8, C), axis=0)
                vec_acc[2] += jnp.sum(dz.reshape(HALO // 8, 8, C), axis=0)
                vec_acc[0] += jnp.sum(dy.reshape(HALO // 8, 8, C), axis=0)
            return dy

        def pool_dd(dyb, base, main):
            dpre = dyb * ps_ref[...]
            for gi, w in enumerate(POOL_WINDOWS):
                lo, hi = gi * HEAD, (gi + 1) * HEAD
                dd = _nn(dpre[:, lo:hi].astype(BF16), pwt_ref[gi])
                cnt = _pool_counts(i * tm + base, HALO, w)
                ddc_ext[pl.ds(base, HALO), lo:hi] = dd / cnt
                if main:
                    dd_buf[pl.ds(base, HALO), lo:hi] = dd
            if main:
                dpre_buf[pl.ds(base, HALO), :] = dpre.astype(BF16)
                vec_acc[4] += jnp.sum(dpre.reshape(HALO // 8, 8, C), axis=0)

        def p1(s, carry):
            base = pl.multiple_of(s * HALO, HALO)
            dy_ext[0, pl.ds(base, HALO), :] = ln_bwd(yc_ref[pl.ds(base, HALO), :], dya_ref[pl.ds(base, HALO), :], True)
            pool_dd(dyb_ref[pl.ds(base, HALO), :], base, True)
            return carry

        lax.fori_loop(0, nsub, p1, 0, unroll=2)
        dy_ext[0, tm:E, :] = ln_bwd(ych_ref[...], dyah_ref[...], False) * nl
        _shifted_copies(dy_ext, E)
        dpre_h = dybh_ref[...] * ps_ref[...] * nl
        for gi, w in enumerate(POOL_WINDOWS):
            lo, hi = gi * HEAD, (gi + 1) * HEAD
            dd = _nn(dpre_h[:, lo:hi].astype(BF16), pwt_ref[gi])
            ddc_ext[tm:, lo:hi] = dd / _pool_counts(i * tm + tm, HALO, w)

        def p2(s, carry):
            base = pl.multiple_of(s * HALO, HALO)
            dy_m = dy_ext[0, pl.ds(base, HALO), :]
            da = jnp.zeros((HALO, C), F32)
            for j in range(CONV_WIDTH):
                sh = CONV_WIDTH - 1 - j
                off = 2 + j
                da = da + cw_ref[pl.ds(j, 1), :] * dy_ext[sh % 8, pl.ds(pl.multiple_of(base + sh // 8 * 8, 8), HALO), :]
                a_j = a_ext[off % 8, pl.ds(pl.multiple_of(base + off // 8 * 8, 8), HALO), :]
                dcw_acc[j] += jnp.sum((dy_m * a_j).reshape(HALO // 8, 8, C), axis=0)
            v = val_ref[pl.ds(base, HALO), :]
            g = gate_ref[pl.ds(base, HALO), :]
            sg = jax.nn.sigmoid(g)
            du_ref[pl.ds(base, HALO), 0:C] = (da * sg).astype(BF16)
            du_ref[pl.ds(base, HALO), C:2 * C] = (da * v * sg * (1.0 - sg)).astype(BF16)
            pwin = p_ext[pl.ds(base, 2 * HALO), :]
            cwin = ddc_ext[pl.ds(base, 2 * HALO), :]
            for gi, w in enumerate(POOL_WINDOWS):
                lo, hi = gi * HEAD, (gi + 1) * HEAD
                x = pwin[HALO:, lo:hi]
                tot = x
                back = cwin[0:HALO, lo:hi]
                for k in range(1, w):
                    tot = tot + pwin[HALO - k:2 * HALO - k, lo:hi]
                    back = back + cwin[k:k + HALO, lo:hi]
                cnt = _pool_counts(i * tm + base, HALO, w)
                d_buf[pl.ds(base, HALO), lo:hi] = (tot / cnt - x).astype(BF16)
                du_ref[pl.ds(base, HALO), 2 * C + lo:2 * C + hi] = (back - dd_buf[pl.ds(base, HALO), lo:hi]).astype(BF16)
            return carry

        lax.fori_loop(0, nsub, p2, 0)
        for gi in range(len(POOL_WINDOWS)):
            lo, hi = gi * HEAD, (gi + 1) * HEAD
            pre = _nn(d_buf[:, lo:hi], pw_ref[gi]) + pb_ref[:, lo:hi]
            vec_acc[3, :, lo:hi] += jnp.sum((dyb_ref[:, lo:hi] * pre).reshape(tm // 8, 8, HEAD), axis=0)
            dpw_ref[gi] += _tn(d_buf[:, lo:hi], dpre_buf[:, lo:hi])
        for j in range(CONV_WIDTH):
            dcw_ref[pl.ds(j, 1), :] += jnp.sum(dcw_acc[j], axis=0, keepdims=True)
        for r in range(5):
            dvec_ref[pl.ds(r, 1), :] += jnp.sum(vec_acc[r], axis=0, keepdims=True)

    def main(c, width=C):
        return pl.BlockSpec((tm, width), lambda i: (i, c))

    def prev(c):
        return pl.BlockSpec((HALO, C), lambda i: (jnp.maximum(i * hb - 1, 0), c))

    def nxt(c):
        return pl.BlockSpec((HALO, C), lambda i: (jnp.minimum((i + 1) * hb, T // HALO - 1), c))

    vec = pl.BlockSpec((1, C), lambda i: (0, 0))
    mat = pl.BlockSpec((4, HEAD, HEAD), lambda i: (0, 0, 0))
    return pl.pallas_call(
        body, grid=(nblk,),
        in_specs=[main(0), main(1), nxt(0), nxt(1), main(0), nxt(0), main(0), main(1), main(2), prev(0), prev(1),
                  prev(2), pl.BlockSpec((32, C), lambda i: (0, 0)), vec, vec, mat, mat, vec, vec],
        out_specs=[pl.BlockSpec((tm, 3 * C), lambda i: (i, 0)), pl.BlockSpec((32, C), lambda i: (0, 0)),
                   pl.BlockSpec((8, C), lambda i: (0, 0)), mat],
        out_shape=[jax.ShapeDtypeStruct((T, 3 * C), BF16), jax.ShapeDtypeStruct((32, C), F32),
                   jax.ShapeDtypeStruct((8, C), F32), jax.ShapeDtypeStruct((4, HEAD, HEAD), F32)],
        scratch_shapes=[pltpu.VMEM((8, E + 8, C), F32), pltpu.VMEM((8, E + 8, C), F32), pltpu.VMEM((E, C), F32),
                        pltpu.VMEM((E, C), F32), pltpu.VMEM((tm, C), F32), pltpu.VMEM((tm, C), BF16),
                        pltpu.VMEM((tm, C), BF16), pltpu.VMEM((32, 8, C), F32), pltpu.VMEM((8, 8, C), F32)],
        compiler_params=_cp("arbitrary"), name="ev_bwd")(
            dyab, dyab, dyab, dyab, yc, yc, u, u, u, u, u, u, cw, lg, lb, pw, pwt, pb, ps)


def _cumsum_rows(x, reverse=False):
    n = x.shape[0]
    rid = lax.broadcasted_iota(jnp.int32, (n, 1), 0)
    k = 1
    while k < n:
        if reverse:
            sh = jnp.where(rid < n - k, pltpu.roll(x, n - k, 0), 0.0)
        else:
            sh = jnp.where(rid >= k, pltpu.roll(x, k, 0), 0.0)
        x = x + sh
        k *= 2
    return x


def _hgrn_gates(qr, fr, lbv):
    sq = jax.nn.sigmoid(qr)
    sg = jax.nn.sigmoid(fr)
    fg = lbv + (1.0 - lbv) * sg
    return qr * sq, sq, sg, fg, 1.0 - fg, jnp.log(fg)


def _hgrn_fwd(u, lbv, gn):
    T = u.shape[0]
    H = 8
    RB = _tile(T, HGRN_ROWS, CHUNK)
    NC = RB // CHUNK
    NS = CHUNK // SUB

    HP = HGRN_HEADS_FWD
    W = HP * HEAD

    def body(q_ref, f_ref, i_ref, g_ref, lb_ref, gn_ref, y_ref, o_ref, s0_ref, st, qs, ks, bs, vs, os_):
        rb = pl.program_id(1)

        @pl.when(rb == 0)
        def _():
            st[...] = jnp.zeros_like(st)

        t8 = lax.broadcasted_iota(jnp.int32, (8, 1), 0)

        def head(hh, c, rows):
            sl = slice(hh * HEAD, (hh + 1) * HEAD)
            q, _, _, _, kk, lf = _hgrn_gates(q_ref[rows, sl], f_ref[rows, sl], lb_ref[:, sl])
            v = i_ref[rows, sl]
            b = _cumsum_rows(lf)
            qs[hh] = q
            ks[hh] = kk
            bs[hh] = b
            vs[hh] = v
            st0 = st[hh]
            s0_ref[hh, c] = st0
            os_[hh] = _nt((q * jnp.exp(b)).astype(BF16), st0.astype(BF16))
            for I in range(NS):
                lo = I * SUB
                qI = qs[hh, lo:lo + SUB, :]
                bI = bs[hh, lo:lo + SUB, :]
                oI = jnp.zeros((SUB, HEAD), F32)
                if I > 0:
                    bprev = bs[hh, pl.ds(lo - 1, 1), :]
                    qt = _r16(qI * jnp.exp(bI - bprev))
                    kt = _r16(ks[hh, 0:lo, :] * jnp.exp(bprev - bs[hh, 0:lo, :]))
                    A = _nt(qt, kt)
                    oI = oI + _nn(_r16(A), _r16(vs[hh, 0:lo, :]))
                od = [jnp.zeros((8, HEAD), F32) for _ in range(SUB // 8)]
                for s in range(SUB):
                    row = pl.ds(lo + s, 1)
                    brow, krow, vrow = bs[hh, row, :], ks[hh, row, :], vs[hh, row, :]
                    for ti in range(SUB // 8):
                        o8 = 8 * ti
                        if s > o8 + 7:
                            continue
                        d = bI[o8:o8 + 8] - brow
                        if s > o8:
                            d = jnp.where(t8 >= s - o8, d, NEG)
                        col = jnp.sum(qI[o8:o8 + 8] * jnp.exp(d) * krow, axis=1, keepdims=True)
                        od[ti] = od[ti] + col * vrow
                os_[hh, lo:lo + SUB, :] += oI + jnp.concatenate(od, axis=0)
            blast = bs[hh, pl.ds(CHUNK - 1, 1), :]
            kh = kk * jnp.exp(blast - b)
            st[hh] = st0 * jnp.exp(blast) + _tn(v.astype(BF16), kh.astype(BF16))
            o = os_[hh]
            o_ref[rows, sl] = o
            rr = lax.rsqrt(jnp.mean(o * o, axis=-1, keepdims=True) + EPS)
            gr = g_ref[rows, sl]
            y_ref[rows, sl] = (((o * rr) * gn_ref[...]) * (gr * jax.nn.sigmoid(gr))).astype(BF16)

        def chunk(c, carry):
            rows = pl.ds(pl.multiple_of(c * CHUNK, CHUNK), CHUNK)
            for hh in range(HP):
                head(hh, c, rows)
            return carry

        lax.fori_loop(0, NC, chunk, 0)

    def blk(q):
        return pl.BlockSpec((RB, W), lambda h, r: (r, q * (H // HP) + h))

    sc = lambda: pltpu.VMEM((HP, CHUNK, HEAD), F32)
    return pl.pallas_call(
        body, grid=(H // HP, T // RB),
        in_specs=[blk(0), blk(1), blk(2), blk(3), pl.BlockSpec((1, W), lambda h, r: (0, h)),
                  pl.BlockSpec((1, HEAD), lambda h, r: (0, 0))],
        out_specs=[pl.BlockSpec((RB, W), lambda h, r: (r, h)), pl.BlockSpec((RB, W), lambda h, r: (r, h)),
                   pl.BlockSpec((HP, NC, HEAD, HEAD), lambda h, r: (h, r, 0, 0))],
        out_shape=[jax.ShapeDtypeStruct((T, H * HEAD), BF16), jax.ShapeDtypeStruct((T, H * HEAD), F32),
                   jax.ShapeDtypeStruct((H, T // CHUNK, HEAD, HEAD), F32)],
        scratch_shapes=[pltpu.VMEM((HP, HEAD, HEAD), F32), sc(), sc(), sc(), sc(), sc()],
        compiler_params=_cp("parallel", "arbitrary"), name="hgrn_fwd")(u, u, u, u, lbv, gn)


def _hgrn_bwd(dy, o, s0, u, lbv, gn):
    T = u.shape[0]
    H = 8
    RB = _tile(T, HGRN_ROWS, CHUNK)
    NB = T // RB
    NC = RB // CHUNK
    NS = CHUNK // SUB

    def body(q_ref, f_ref, i_ref, g_ref, lb_ref, gn_ref, o_ref, dy_ref, s0_ref, du_ref, dlb_ref, dgn_ref,
             dst, qs, ks, bs, vs, dos, dqs, dks, dki, dvs, dbs):
        rb = pl.program_id(1)

        @pl.when(rb == 0)
        def _():
            dst[...] = jnp.zeros_like(dst)
            dlb_ref[...] = jnp.zeros_like(dlb_ref)
            dgn_ref[...] = jnp.zeros_like(dgn_ref)

        t8 = lax.broadcasted_iota(jnp.int32, (8, 1), 0)
        lane = lax.broadcasted_iota(jnp.int32, (8, HEAD), 1)
        gnv = gn_ref[...]

        def head(hh, c, rows):
            sl = slice(hh * HEAD, (hh + 1) * HEAD)
            lbv_ = lb_ref[:, sl]
            qr = q_ref[rows, sl]
            q, sq, sg, fg, kk, lf = _hgrn_gates(qr, f_ref[rows, sl], lbv_)
            v = i_ref[rows, sl]
            gr = g_ref[rows, sl]
            b = _cumsum_rows(lf)
            eb = jnp.exp(b)
            ov = o_ref[rows, sl]
            dyv = dy_ref[rows, sl]
            rr = lax.rsqrt(jnp.mean(ov * ov, axis=-1, keepdims=True) + EPS)
            oh = ov * rr
            gs = jax.nn.sigmoid(gr)
            dgr = dyv * (oh * gnv) * _dsilu(gr, gs)
            dnrm = dyv * (gr * gs)
            dgn_ref[hh] += jnp.sum(dnrm * oh, axis=0, keepdims=True)
            t1 = dnrm * gnv
            do = rr * (t1 - oh * jnp.mean(t1 * oh, axis=-1, keepdims=True))
            qs[hh] = q
            ks[hh] = kk
            bs[hh] = b
            vs[hh] = v
            dos[hh] = do
            st0 = s0_ref[hh, c]
            dS = dst[hh]
            do_b = do.astype(BF16)
            blast = bs[hh, pl.ds(CHUNK - 1, 1), :]
            elast = jnp.exp(blast - b)
            dq_inter = _nn(do_b, st0.astype(BF16)) * eb
            dqs[hh] = dq_inter
            dbs[hh] = q * dq_inter
            kh = kk * elast
            dvs[hh] = _nt(kh.astype(BF16), dS.astype(BF16))
            dk_inter = _nn(v.astype(BF16), dS.astype(BF16)) * elast
            dki[hh] = dk_inter
            dks[hh] = jnp.zeros((CHUNK, HEAD), F32)
            for I in range(NS):
                lo = I * SUB
                qI = qs[hh, lo:lo + SUB, :]
                bI = bs[hh, lo:lo + SUB, :]
                doI = dos[hh, lo:lo + SUB, :]
                dqI = jnp.zeros((SUB, HEAD), F32)
                dbI = jnp.zeros((SUB, HEAD), F32)
                if I > 0:
                    bprev = bs[hh, pl.ds(lo - 1, 1), :]
                    eq = jnp.exp(bI - bprev)
                    ek = jnp.exp(bprev - bs[hh, 0:lo, :])
                    qt = _r16(qI * eq)
                    kt = _r16(ks[hh, 0:lo, :] * ek)
                    A = _r16(_nt(qt, kt))
                    doI_b = _r16(doI)
                    dA = _r16(_nt(doI_b, _r16(vs[hh, 0:lo, :])))
                    dvs[hh, 0:lo, :] += _tn(A, doI_b)
                    dqt = _nn(dA, kt)
                    dkt = _tn(dA, qt)
                    dqI = dqI + dqt * eq
                    dbI = dbI + qt.astype(F32) * dqt
                    dks[hh, 0:lo, :] += dkt * ek
                    dbs[hh, 0:lo, :] -= kt.astype(F32) * dkt
                dq_t = [jnp.zeros((8, HEAD), F32) for _ in range(SUB // 8)]
                a_t = [jnp.zeros((8, HEAD), F32) for _ in range(SUB // 8)]
                for s in range(SUB):
                    row = pl.ds(lo + s, 1)
                    brow, krow, vrow = bs[hh, row, :], ks[hh, row, :], vs[hh, row, :]
                    dk_s = None
                    for ti in range(SUB // 8):
                        o8 = 8 * ti
                        if s > o8 + 7:
                            continue
                        d = bI[o8:o8 + 8] - brow
                        if s > o8:
                            d = jnp.where(t8 >= s - o8, d, NEG)
                        Es = jnp.exp(d)
                        qE = qI[o8:o8 + 8] * Es
                        col = jnp.sum(qE * krow, axis=1, keepdims=True)
                        a_t[ti] = jnp.where(lane == s, col, a_t[ti])
                        dcol = jnp.sum(doI[o8:o8 + 8] * vrow, axis=1, keepdims=True)
                        dq_t[ti] = dq_t[ti] + (dcol * Es) * krow
                        part = jnp.sum(dcol * qE, axis=0, keepdims=True)
                        dk_s = part if dk_s is None else dk_s + part
                    dks[hh, row, :] += dk_s
                    dbs[hh, row, :] -= krow * dk_s
                a_d = jnp.concatenate(a_t, axis=0)
                dq_d = jnp.concatenate(dq_t, axis=0)
                dvs[hh, lo:lo + SUB, :] += _tn(a_d, doI)[0:SUB]
                dqI = dqI + dq_d
                dbI = dbI + qI * dq_d
                dqs[hh, lo:lo + SUB, :] += dqI
                dbs[hh, lo:lo + SUB, :] += dbI
            kdk = kk * dki[hh]
            excl = _cumsum_rows(kdk) - kdk
            suff = _cumsum_rows(dbs[hh], reverse=True)
            gdec = jnp.sum(dS * st0, axis=0, keepdims=True) * jnp.exp(blast)
            dlf = suff + excl + gdec
            dk = dks[hh] + dki[hh]
            dfg = dlf / fg - dk
            dlb_ref[:, sl] += jnp.sum(dfg * (1.0 - sg), axis=0, keepdims=True)
            du_ref[0, rows, sl] = (dqs[hh] * _dsilu(qr, sq)).astype(BF16)
            du_ref[1, rows, sl] = (dfg * (1.0 - lbv_) * sg * (1.0 - sg)).astype(BF16)
            du_ref[2, rows, sl] = dvs[hh].astype(BF16)
            du_ref[3, rows, sl] = dgr.astype(BF16)
            dst[hh] = dS * jnp.exp(blast) + _tn(do_b, (q * eb).astype(BF16))

        def chunk(cc, carry):
            c = NC - 1 - cc
            rows = pl.ds(pl.multiple_of(c * CHUNK, CHUNK), CHUNK)
            for hh in range(HP):
                head(hh, c, rows)
            return carry

        lax.fori_loop(0, NC, chunk, 0)

    HP = HGRN_HEADS_BWD
    W = HP * HEAD

    def blk(qd):
        return pl.BlockSpec((RB, W), lambda h, r: (NB - 1 - r, qd * (H // HP) + h))

    hblk = pl.BlockSpec((RB, W), lambda h, r: (NB - 1 - r, h))
    sc = lambda: pltpu.VMEM((HP, CHUNK, HEAD), F32)
    return pl.pallas_call(
        body, grid=(H // HP, NB),
        in_specs=[blk(0), blk(1), blk(2), blk(3), pl.BlockSpec((1, W), lambda h, r: (0, h)),
                  pl.BlockSpec((1, HEAD), lambda h, r: (0, 0)), hblk, hblk,
                  pl.BlockSpec((HP, NC, HEAD, HEAD), lambda h, r: (h, NB - 1 - r, 0, 0))],
        out_specs=[pl.BlockSpec((4, RB, W), lambda h, r: (0, NB - 1 - r, h)),
                   pl.BlockSpec((1, W), lambda h, r: (0, h)), pl.BlockSpec((HP, 1, HEAD), lambda h, r: (h, 0, 0))],
        out_shape=[jax.ShapeDtypeStruct((4, T, H * HEAD), BF16), jax.ShapeDtypeStruct((1, H * HEAD), F32),
                   jax.ShapeDtypeStruct((H, 1, HEAD), F32)],
        scratch_shapes=[pltpu.VMEM((HP, HEAD, HEAD), F32)] + [sc() for _ in range(10)],
        compiler_params=_cp("parallel", "arbitrary"), name="hgrn_bwd")(u, u, u, u, lbv, gn, o, dy, s0)


def _softmax_rows(p_ref, L):
    rows = [p_ref[pl.ds(l, 1), :] for l in range(L)]
    m = rows[0]
    for r in rows[1:]:
        m = jnp.maximum(m, r)
    e = [jnp.exp(r - m) for r in rows]
    tot = e[0]
    for t in e[1:]:
        tot = tot + t
    return [t / tot for t in e]


def _lb_fwd(lbp):
    L, D = lbp.shape

    def body(p_ref, o_ref):
        sm = _softmax_rows(p_ref, L)
        acc = jnp.zeros((1, D), F32)
        o_ref[pl.ds(0, 1), :] = acc
        for l in range(1, L):
            acc = acc + sm[l]
            o_ref[pl.ds(l, 1), :] = acc

    return pl.pallas_call(body, out_shape=jax.ShapeDtypeStruct((L, D), F32), name="lb_fwd")(lbp)


def _lb_bwd(lbp, dlb):
    L, D = lbp.shape

    def body(p_ref, d_ref, o_ref):
        sm = _softmax_rows(p_ref, L)
        dsm = [jnp.zeros((1, D), F32)]
        for i in range(1, L):
            t = jnp.zeros((1, D), F32)
            for l in range(i, L):
                t = t + d_ref[pl.ds(l, 1), :]
            dsm.append(t)
        dot = jnp.zeros((1, D), F32)
        for i in range(L):
            dot = dot + dsm[i] * sm[i]
        for i in range(L):
            o_ref[pl.ds(i, 1), :] = sm[i] * (dsm[i] - dot)

    return pl.pallas_call(body, out_shape=jax.ShapeDtypeStruct((L, D), F32), name="lb_bwd")(lbp, dlb)


def _my_pos():
    return lax.axis_index("x"), lax.axis_index("y"), lax.axis_index("c")


def _peer(mask):
    x, y, c = _my_pos()
    mx, my, mc = (mask >> 2) & 1, (mask >> 1) & 1, mask & 1
    px = (1 - x) if mx else x
    py = (1 - y) if my else y
    pc = (1 - c) if mc else c
    return (px, py, pc), 4 * px + 2 * py + pc


def _all_gather(shards):
    n = len(shards)

    def body(*refs):
        ins, outs = refs[:n], refs[n:2 * n]
        send_sems, recv_sems, local_sems = refs[2 * n:]
        x, y, c = _my_pos()
        me = 4 * x + 2 * y + c
        local = [pltpu.make_async_copy(ins[a], outs[a].at[:, me], local_sems.at[a]) for a in range(n)]
        for cp in local:
            cp.start()
        sends = []
        for m in range(1, N_DEV):
            peer, _ = _peer(m)
            for a in range(n):
                cp = pltpu.make_async_remote_copy(
                    src_ref=ins[a], dst_ref=outs[a].at[:, me], send_sem=send_sems.at[a, m - 1],
                    recv_sem=recv_sems.at[a, m - 1], device_id=peer, device_id_type=MESH)
                cp.start()
                sends.append(cp)
        for m in range(1, N_DEV):
            peer, pid = _peer(m)
            for a in range(n):
                pltpu.make_async_remote_copy(
                    src_ref=ins[a], dst_ref=outs[a].at[:, pid], send_sem=send_sems.at[a, m - 1],
                    recv_sem=recv_sems.at[a, m - 1], device_id=peer, device_id_type=MESH).wait_recv()
        for cp in sends:
            cp.wait_send()
        for cp in local:
            cp.wait()

    out_shape = [jax.ShapeDtypeStruct((s.shape[0], N_DEV) + s.shape[1:], s.dtype) for s in shards]
    return pl.pallas_call(
        body, in_specs=[ANY] * n, out_specs=[ANY] * n, out_shape=out_shape,
        scratch_shapes=[pltpu.SemaphoreType.DMA((n, N_DEV - 1)), pltpu.SemaphoreType.DMA((n, N_DEV - 1)),
                        pltpu.SemaphoreType.DMA((n,))],
        name="all_gather_small")(*shards)


HBM_SPEC =pl.BlockSpec(memory_space=pltpu.HBM)
SEM_SPEC = pl.BlockSpec(memory_space=pltpu.SEMAPHORE)
EFFECT = pltpu.SideEffectType.DATAFLOW_SIDE_EFFECTING


def _hbm(a):
    return pltpu.with_memory_space_constraint(a, pltpu.HBM)


def _landing(block_shape, dtype, axis=0):
    if axis == 0:
        return lax.empty((N_DEV,) + tuple(block_shape), dtype)
    rows, n = block_shape
    return lax.empty((rows, N_DEV * n), dtype)


def _slot(ref, i):
    if len(ref.shape) == 2:
        n = ref.shape[1] // N_DEV
        return ref.at[:, pl.ds(i * n, n)]
    return ref.at[i]


def _push_start(name, srcs, lands, whole, groups):
    n = len(srcs)
    ng = 1 + max(groups)
    cnt = [groups.count(g) for g in range(ng)]
    idx = [groups[:a].count(groups[a]) for a in range(n)]

    def body(*refs):
        src_refs, land_refs = refs[:n], refs[n:2 * n]
        sems = refs[2 * n:2 * n + 3 * ng]
        token = refs[-1]
        x, y, c = _my_pos()
        me = 4 * x + 2 * y + c
        for a in range(n):
            g = groups[a]
            for m in range(1, N_DEV):
                peer, pid = _peer(m)
                pltpu.make_async_remote_copy(
                    src_ref=src_refs[a] if whole else src_refs[a].at[pid], dst_ref=_slot(land_refs[a], me),
                    send_sem=sems[3 * g].at[idx[a] * (N_DEV - 1) + m - 1],
                    recv_sem=sems[3 * g + 1].at[idx[a] * (N_DEV - 1) + m - 1],
                    device_id=peer, device_id_type=MESH).start()
            pltpu.make_async_copy(src_refs[a] if whole else src_refs[a].at[me], _slot(land_refs[a], me),
                                  sems[3 * g + 2].at[idx[a]]).start()
        token[...] = jnp.zeros_like(token)

    sem_shapes = []
    for g in range(ng):
        sem_shapes += [pltpu.SemaphoreType.DMA((cnt[g] * (N_DEV - 1),))] * 2 + [pltpu.SemaphoreType.DMA((cnt[g],))]
    thru = [pltpu.HBM(s.shape, s.dtype) for s in list(srcs) + list(lands)]
    res = pl.pallas_call(
        body, name=name,
        out_shape=tuple(sem_shapes + thru + [jax.ShapeDtypeStruct((8, 128), F32)]),
        in_specs=tuple([HBM_SPEC] * (2 * n)),
        out_specs=tuple([SEM_SPEC] * (3 * ng) + [HBM_SPEC] * (2 * n) + [pl.BlockSpec(memory_space=pltpu.VMEM)]),
        input_output_aliases={i: 3 * ng + i for i in range(2 * n)},
        compiler_params=pltpu.CompilerParams(has_side_effects=EFFECT),
    )(*[_hbm(s) for s in srcs], *[_hbm(z) for z in lands])
    sems = [(res[3 * g], res[3 * g + 1], res[3 * g + 2]) for g in range(ng)]
    srcs_thru = list(res[3 * ng:3 * ng + n])
    lands_thru = list(res[3 * ng + n:3 * ng + 2 * n])
    return sems, srcs_thru, lands_thru, res[-1]


def _push_wait(name, srcs_thru, lands_thru, sems, after, whole):
    n = len(srcs_thru)

    def body(*refs):
        src_refs, land_refs = refs[:n], refs[n:2 * n]
        send_sems, recv_sems, own_sems = refs[2 * n], refs[2 * n + 1], refs[2 * n + 2]
        x, y, c = _my_pos()
        me = 4 * x + 2 * y + c
        for a in range(n):
            pltpu.make_async_copy(src_refs[a] if whole else src_refs[a].at[me], _slot(land_refs[a], me),
                                  own_sems.at[a]).wait()
            for m in range(1, N_DEV):
                peer, pid = _peer(m)
                cp = pltpu.make_async_remote_copy(
                    src_ref=src_refs[a] if whole else src_refs[a].at[pid], dst_ref=_slot(land_refs[a], pid),
                    send_sem=send_sems.at[a * (N_DEV - 1) + m - 1], recv_sem=recv_sems.at[a * (N_DEV - 1) + m - 1],
                    device_id=peer, device_id_type=MESH)
                cp.wait_send()
                cp.wait_recv()

    thru = [pltpu.HBM(s.shape, s.dtype) for s in list(srcs_thru) + list(lands_thru)]
    res = pl.pallas_call(
        body, name=name, out_shape=tuple(thru),
        in_specs=tuple([HBM_SPEC] * (2 * n) + [SEM_SPEC, SEM_SPEC, SEM_SPEC, ANY]),
        out_specs=tuple([HBM_SPEC] * (2 * n)),
        input_output_aliases={i: i for i in range(2 * n)},
        compiler_params=pltpu.CompilerParams(has_side_effects=EFFECT),
    )(*srcs_thru, *lands_thru, sems[0], sems[1], sems[2], after)
    return list(res[n:])


def _adamw(recv, w, m, v, layer=0, prev=None):
    L, R, C = w.shape
    tr = _tile(R, max(8, (1 << 18) // C), 8) if R % 8 == 0 else R
    bc1 = 1.0 - ADAM_B1 ** ADAM_STEP
    bc2 = 1.0 - ADAM_B2 ** ADAM_STEP
    if prev is None:
        prev = [lax.empty((L, R, C), F32) for _ in range(4)]

    def body(r_ref, w_ref, m_ref, v_ref, p0, p1, p2, p3, g_ref, d_ref, nm_ref, nv_ref):
        g = r_ref[0].astype(F32)
        for s in range(1, N_DEV):
            g = g + r_ref[s].astype(F32)
        nm = ADAM_B1 * m_ref[...] + (1.0 - ADAM_B1) * g
        nv = ADAM_B2 * v_ref[...] + (1.0 - ADAM_B2) * (g * g)
        mh = nm / bc1
        vh = nv / bc2
        g_ref[...] = g
        d_ref[...] = -ADAM_LR * (mh / (jnp.sqrt(vh) + ADAM_EPS) + ADAM_WD * w_ref[...])
        nm_ref[...] = nm
        nv_ref[...] = nv

    row = pl.BlockSpec((None, tr, C), lambda i: (layer, i, 0))
    return pl.pallas_call(
        body, grid=(R // tr,),
        in_specs=[pl.BlockSpec((N_DEV, tr, C), lambda i: (0, i, 0)), row, row, row] + [ANY] * 4,
        out_specs=[row] * 4, out_shape=[jax.ShapeDtypeStruct((L, R, C), F32)] * 4,
        input_output_aliases={4: 0, 5: 1, 6: 2, 7: 3},
        compiler_params=_cp("parallel"), name="adamw")(recv, w, m, v, *prev)


def _full_w_spec(tk, tn):
    return pl.BlockSpec((tk, tn), lambda i, j, k: (k, j))


def kernel(x, meta_tokens, mix_norm_g, mlp_norm_g, final_norm_g, ev_w_in, ev_conv_w, ev_conv_b, ev_ln_g, ev_ln_b, ev_pool_w, ev_pool_b, ev_pool_scale, ev_w_out, od_w_in, od_gnorm_g, od_w_out, lb_param, mlp_w1, mlp_w2, loss_target, m_meta_tokens, m_mix_norm_g, m_mlp_norm_g, m_final_norm_g, m_ev_w_in, m_ev_conv_w, m_ev_conv_b, m_ev_ln_g, m_ev_ln_b, m_ev_pool_w, m_ev_pool_b, m_ev_pool_scale, m_ev_w_out, m_od_w_in, m_od_gnorm_g, m_od_w_out, m_lb_param, m_mlp_w1, m_mlp_w2, v_meta_tokens, v_mix_norm_g, v_mlp_norm_g, v_final_norm_g, v_ev_w_in, v_ev_conv_w, v_ev_conv_b, v_ev_ln_g, v_ev_ln_b, v_ev_pool_w, v_ev_pool_b, v_ev_pool_scale, v_ev_w_out, v_od_w_in, v_od_gnorm_g, v_od_w_out, v_lb_param, v_mlp_w1, v_mlp_w2):
    S, D = x.shape[1], x.shape[2]
    T = PAD + N_META + S
    DEPTH = mix_norm_g.shape[0]
    DFF = mlp_w1.shape[2] * N_DEV
    dev = 4 * lax.axis_index("x") + 2 * lax.axis_index("y") + lax.axis_index("c")

    g_meta, g_cw = _all_gather([meta_tokens[None], ev_conv_w])
    n_ev = ev_w_in.shape[0]
    n_od = od_w_in.shape[0]
    meta_full = jnp.transpose(g_meta[0], (1, 0, 2)).reshape(N_META, D)
    cw_full = jnp.transpose(g_cw, (0, 2, 1, 3)).reshape(n_ev, CONV_WIDTH, -1)
    cw_pad = jnp.pad(cw_full, ((0, 0), (0, 32 - CONV_WIDTH), (0, 0)))
    n_in_od = od_w_in.shape[2]
    n_w1 = mlp_w1.shape[2]

    ag_src, ag_grp, ag_axis, ag_at = [], [], [], {}
    for layer in range(DEPTH):
        j = layer // 2
        mixer = [("in", ev_w_in[j]), ("out", ev_w_out[j])] if layer % 2 == 0 else [("in", od_w_in[j]), ("out", od_w_out[j])]
        for pos, (key, arr) in enumerate(mixer + [("w1", mlp_w1[layer]), ("w2", mlp_w2[layer])]):
            ag_at[layer, key] = len(ag_src)
            ag_src.append(arr.astype(BF16))
            ag_grp.append(len(ag_grp))
            ag_axis.append(1 if key in ("in", "w1") and arr.shape[1] % 128 == 0 else 0)
    ag_src, g_meta, g_cw = lax.optimization_barrier((ag_src, g_meta, g_cw))
    ag_sems, ag_s, ag_l, ag_tok = _push_start(
        "ag_start", ag_src, [_landing(s_.shape, s_.dtype, ax) for s_, ax in zip(ag_src, ag_axis)], True, ag_grp)

    def ag_wait(layer, key, after):
        a = ag_at[layer, key]
        return _push_wait(f"ag_wait_{a}", [ag_s[a]], [ag_l[a]], ag_sems[a], after, True)[0]

    h = jnp.concatenate([jnp.zeros((PAD, D), F32), meta_full, x[0]], axis=0) + ag_tok[0, 0]
    tgt = jnp.pad(loss_target[0], ((PAD + N_META, 0), (0, 0)))
    lb_all = _lb_fwd(lb_param)

    tm_big = _tile(T, MM_ROWS_BIG, 16)
    tm_mid = _tile(T, MM_ROWS_MID, 16)
    tm_k4 = _tile(T, MM_ROWS_K4, 16)

    saved = []
    for layer in range(DEPTH):
        j = layer // 2
        sv = {"h0": h}
        g_in = ag_wait(layer, "in", h)
        w_in = g_in if g_in.ndim == 2 else jnp.transpose(g_in, (1, 0, 2)).reshape(D, -1)
        if layer % 2 == 0:
            sv["n"], u = _mm_rms_nn("ev_in", h, mix_norm_g[layer][None], w_in, tm_big, 512, "f32")
            yab, yc = _ev_fwd(u, cw_pad[j], ev_conv_b[j][None], ev_ln_g[j][None], ev_ln_b[j][None],
                              ev_pool_w[j].astype(BF16), ev_pool_b[j].reshape(1, -1), ev_pool_scale[j][None])
            sv.update(u=u, y=yab, yc=yc)
            w_out = ag_wait(layer, "out", yab).reshape(-1, D)
            h = _mm_nn("ev_out", yab, w_out, _full_w_spec, T, D, D, tm_mid, D, D, "resid", extra=h)
        else:
            sv["n"], u = _mm_rms_nn("od_in", h, mix_norm_g[layer][None], w_in, tm_big, 512, "f32")
            y, o, s0 = _hgrn_fwd(u, lb_all[layer][None], od_gnorm_g[j][None])
            sv.update(u=u, y=y, o=o, s0=s0)
            w_out = ag_wait(layer, "out", y).reshape(-1, D)
            h = _mm_nn("od_out", y, w_out, _full_w_spec, T, D, D, tm_mid, D, D, "resid", extra=h)
        sv["h1"] = h
        w_w1 = ag_wait(layer, "w1", h)
        n2, r, act = _mm_rms_nn("mlp_w1", h, mlp_norm_g[layer][None], w_w1, tm_big, 512, "relu2")
        w_w2 = ag_wait(layer, "w2", act).reshape(DFF, D)
        sv.update(w_in=w_in, w_out=w_out, w_w1=w_w1, w_w2=w_w2)
        sv.update(n2=n2, r=r, act=act)
        h = _mm_nn("mlp_w2", act, w_w2, _full_w_spec, T, D, DFF, tm_k4, D, DFF, "resid", extra=h)
        saved.append(sv)

    loss_blk, dh, dhb, dg_final = _loss_head(h, final_norm_g[None], tgt)
    loss = lax.psum(loss_blk[0, 0], AXES)

    tt = T
    g_mix, g_mlp = [None] * DEPTH, [None] * DEPTH
    small ={"cw": [None] * n_ev, "vec": [None] * n_ev, "pw": [None] * n_ev, "gn": [None] * n_od}
    dlb_rows = [jnp.zeros((1, D), F32) for _ in range(DEPTH)]

    def xs2(tt_, tk):
        return pl.BlockSpec((tt_, tk), lambda a, b, t: (t, a))

    def ys2(tt_, tn):
        return pl.BlockSpec((tt_, tn), lambda a, b, t: (t, b))

    def os2(tk, tn):
        return pl.BlockSpec((tk, tn), lambda a, b, t: (a, b))

    def os3(tk, tn):
        return pl.BlockSpec((None, tk, tn), lambda a, b, t: (b, a, 0))

    def dy2(tm, tn):
        return pl.BlockSpec((tm, tn), lambda i, jj, k: (i, k))

    def w_rows(tj, tn):
        return pl.BlockSpec((tj, tn), lambda i, jj, k: (jj, k))

    def w_whole(tj, tn):
        return pl.BlockSpec((tj, tn), lambda i, jj, k: (0, 0), pipeline_mode=pl.Buffered(1))

    rs_pending = []

    def rs_start(tag, mats):
        blocks = [m_ if m_.ndim == 3 else m_.reshape(N_DEV, m_.shape[0] // N_DEV, m_.shape[1]) for m_ in mats]
        lands = [_landing(b_.shape[1:], b_.dtype) for b_ in blocks]
        sems, s_thru, l_thru, tok = _push_start(f"rs_start_{tag}", blocks, lands, False, [0] * len(blocks))
        rs_pending.append((tag, s_thru, l_thru, sems[0]))
        return tok[0, 0]

    for layer in reversed(range(DEPTH)):
        j = layer // 2
        sv = saved[layer]
        da1 = _mm_nt("mlp_w2_t", dhb, sv["w_w2"], dy2, w_rows, T, DFF, D, tm_mid, 1024, D, "dact", extra=sv["r"])
        dw2 = _mm_tn("mlp_dw2", sv["act"], dhb, xs2, ys2, os2, (DFF, D), T, DFF, D, tt, 512, D)
        dw1 = _mm_tn("mlp_dw1", sv["n2"], da1, xs2, ys2, os3, (N_DEV, D, n_w1), T, D, DFF, tt, D, n_w1)
        tok = rs_start(f"mlp{layer}", [dw1, dw2])
        dh, dhb, g_mlp[layer] = _mm_nt("mlp_w1_t", da1, sv["w_w1"], dy2, w_whole, T, D, DFF, tm_k4, D, DFF, "rms",
                                       extra=(sv["h1"], mlp_norm_g[layer][None] + tok, dh))
        if layer % 2 == 0:
            dyab = _mm_nt("ev_out_t", dhb, sv["w_out"], dy2, w_rows, T, D, D, tm_mid, D, D, "f32")
            dwout = _mm_tn("ev_dwout", sv["y"], dhb, xs2, ys2, os2, (D, D), T, D, D, tt, 512, D)
            du, small["cw"][j], small["vec"][j], small["pw"][j] = _ev_bwd(
                dyab, sv["yc"], sv["u"], cw_pad[j], ev_ln_g[j][None], ev_ln_b[j][None], ev_pool_w[j].astype(BF16),
                jnp.transpose(ev_pool_w[j], (0, 2, 1)).astype(BF16), ev_pool_b[j].reshape(1, -1),
                ev_pool_scale[j][None])
            nin = du.shape[1]
            dwin = _mm_tn("ev_dwin", sv["n"], du, xs2, ys2, os2, (D, nin), T, D, nin, tt, D, 512)
            dwin = jnp.transpose(dwin.reshape(D, N_DEV, nin // N_DEV), (1, 0, 2))
            tok = rs_start(f"mix{layer}", [dwin, dwout])
            dh, dhb, g_mix[layer] = _mm_nt("ev_in_t", du, sv["w_in"], dy2, w_whole, T, D, nin, tm_k4, D, nin, "rms",
                                           extra=(sv["h0"], mix_norm_g[layer][None] + tok, dh))
        else:
            dy = _mm_nt("od_out_t", dhb, sv["w_out"], dy2, w_rows, T, D, D, tm_mid, D, D, "f32")
            dwout = _mm_tn("od_dwout", sv["y"], dhb, xs2, ys2, os2, (D, D), T, D, D, tt, 512, D)
            du3, dlb_rows[layer], small["gn"][j] = _hgrn_bwd(dy, sv["o"], sv["s0"], sv["u"], lb_all[layer][None],
                                                              od_gnorm_g[j][None])
            per = D // n_in_od

            def du_t(tt_, tn):
                return pl.BlockSpec((None, tt_, tn), lambda a, b, t: (b // per, t, b % per))

            dwin = _mm_tn("od_dwin", sv["n"], du3, xs2, du_t, os3, (N_DEV, D, n_in_od), T, D, 4 * D, tt, D, n_in_od)
            tok = rs_start(f"mix{layer}", [dwin, dwout])
            dh, dhb, g_mix[layer] = _mm_nt(
                "od_in_t", du3, sv["w_in"], lambda tm, tn: pl.BlockSpec((4, tm, tn // 4), lambda i, jj, k: (0, i, 0)),
                w_whole, T, D, 4 * D, tm_k4, D, 4 * D, "rms", extra=(sv["h0"], mix_norm_g[layer][None] + tok, dh),
                parts=4)

    dmeta = dh[PAD:PAD + N_META]
    grad_x = dh[PAD + N_META:][None]
    dlb_param = _lb_bwd(lb_param, jnp.concatenate(dlb_rows, axis=0))

    pieces = [
        ("final", dg_final), ("pad", jnp.zeros((SMALL_F32_ROWS - 1, D), F32)),
        ("meta", dmeta), ("mix", jnp.concatenate(g_mix, 0)), ("mlp", jnp.concatenate(g_mlp, 0)),
        ("cw", jnp.stack([c[:CONV_WIDTH] for c in small["cw"]])), ("cb", jnp.stack([v_[0] for v_ in small["vec"]])),
        ("lng", jnp.stack([v_[1] for v_ in small["vec"]])), ("lnb", jnp.stack([v_[2] for v_ in small["vec"]])),
        ("pw", jnp.stack(small["pw"])), ("pb", jnp.stack([v_[4] for v_ in small["vec"]])),
        ("ps", jnp.stack([v_[3] for v_ in small["vec"]])), ("gn", jnp.stack([jnp.sum(g_, axis=0)[0] for g_ in small["gn"]])),
        ("lb", dlb_param),
    ]
    flat = jnp.concatenate([p.reshape(-1) for _, p in pieces])
    n_small = flat.shape[0]
    rows_small = SMALL_F32_ROWS + -(-(n_small // 1024 + 1 - SMALL_F32_ROWS) // 16) * 16
    flat = jnp.pad(flat, (0, rows_small * 1024 - n_small)).reshape(rows_small, 1024)

    sm_src = [flat[:SMALL_F32_ROWS], flat[SMALL_F32_ROWS:].astype(BF16)]
    sm_sems, sm_s, sm_l, sm_tok = _push_start("small_start", sm_src, [_landing(a_.shape, a_.dtype) for a_ in sm_src],
                                              True, [0, 0])
    recv = {}
    for tag, s_thru, l_thru, sems in rs_pending:
        got = _push_wait(f"rs_wait_{tag}", s_thru, l_thru, sems, sm_tok, False)
        layer = int(tag[3:])
        if tag.startswith("mlp"):
            recv["w1", layer], recv["w2", layer] = got
        else:
            key = "ev" if layer % 2 == 0 else "od"
            recv[key + "_in", layer // 2], recv[key + "_out", layer // 2] = got

    outs = {}
    big = {"ev_in": ("ev_w_in", ev_w_in, m_ev_w_in, v_ev_w_in), "ev_out": ("ev_w_out", ev_w_out, m_ev_w_out, v_ev_w_out),
           "od_in": ("od_w_in", od_w_in, m_od_w_in, v_od_w_in), "od_out": ("od_w_out", od_w_out, m_od_w_out, v_od_w_out),
           "w1": ("mlp_w1", mlp_w1, m_mlp_w1, v_mlp_w1), "w2": ("mlp_w2", mlp_w2, m_mlp_w2, v_mlp_w2)}
    for key, (name, w, m, v) in big.items():
        res = None
        for l in range(w.shape[0]):
            res = _adamw(recv[key, l], w, m, v, layer=l, prev=res)
        outs[name] = res

    small_params = {
        "meta": ("meta_tokens", None), "mix": ("mix_norm_g", mix_norm_g, m_mix_norm_g, v_mix_norm_g),
        "mlp": ("mlp_norm_g", mlp_norm_g, m_mlp_norm_g, v_mlp_norm_g),
        "final": ("final_norm_g", final_norm_g, m_final_norm_g, v_final_norm_g),
        "cw": ("ev_conv_w", None), "cb": ("ev_conv_b", ev_conv_b, m_ev_conv_b, v_ev_conv_b),
        "lng": ("ev_ln_g", ev_ln_g, m_ev_ln_g, v_ev_ln_g), "lnb": ("ev_ln_b", ev_ln_b, m_ev_ln_b, v_ev_ln_b),
        "pw": ("ev_pool_w", ev_pool_w, m_ev_pool_w, v_ev_pool_w), "pb": ("ev_pool_b", ev_pool_b, m_ev_pool_b, v_ev_pool_b),
        "ps": ("ev_pool_scale", ev_pool_scale, m_ev_pool_scale, v_ev_pool_scale),
        "gn": ("od_gnorm_g", od_gnorm_g, m_od_gnorm_g, v_od_gnorm_g), "lb": ("lb_param", lb_param, m_lb_param, v_lb_param),
    }
    csh = ev_conv_w.shape[2]
    msh = meta_tokens.shape[1]

    def packed(which):
        parts = []
        for key, g_ in pieces:
            ent = small_params.get(key)
            if key == "pad":
                full = g_
            elif key == "meta":
                src = (meta_tokens, m_meta_tokens, v_meta_tokens)[which]
                full = lax.dynamic_update_slice(jnp.zeros((N_META, D), F32), src, (0, dev * msh))
            elif key == "cw":
                src = (ev_conv_w, m_ev_conv_w, v_ev_conv_w)[which]
                full = lax.dynamic_update_slice(jnp.zeros(g_.shape, F32), src, (0, 0, dev * csh))
            else:
                full = ent[1 + which]
            parts.append(full.reshape(-1))
        f = jnp.concatenate(parts)
        return jnp.pad(f, (0, rows_small * 1024 - n_small)).reshape(rows_small, 1024)

    got_f32, got_bf16 = _push_wait("small_wait", sm_s, sm_l, sm_sems[0], outs["mlp_w2"][0], True)
    recv_small = jnp.concatenate([got_f32, got_bf16.astype(F32)], axis=1)
    sres = [r_[0] for r_ in _adamw(recv_small, packed(0)[None], packed(1)[None], packed(2)[None])]
    off = 0
    for key, g_ in pieces:
        size = g_.size
        vals = [r_.reshape(-1)[off:off + size].reshape(g_.shape) for r_ in sres]
        off += size
        if key == "pad":
            continue
        name = small_params[key][0]
        if key == "meta":
            vals = [lax.dynamic_slice(v_, (0, dev * msh), (N_META, msh)) for v_ in vals]
        elif key == "cw":
            vals = [lax.dynamic_slice(v_, (0, 0, dev * csh), v_.shape[:2] + (csh,)) for v_ in vals]
        else:
            vals = [v_.reshape(small_params[key][1].shape) for v_ in vals]
        outs[name] = vals

    names = ["meta_tokens", "mix_norm_g", "mlp_norm_g", "final_norm_g", "ev_w_in", "ev_conv_w", "ev_conv_b", "ev_ln_g",
             "ev_ln_b", "ev_pool_w", "ev_pool_b", "ev_pool_scale", "ev_w_out", "od_w_in", "od_gnorm_g", "od_w_out",
             "lb_param", "mlp_w1", "mlp_w2"]
    result = [loss, grad_x]
    for k in range(4):
        result += [outs[nm][k] for nm in names]
    return tuple(result)
```

```python
import functools

import jax
import jax.numpy as jnp
from jax import lax
from jax.experimental import pallas as pl
from jax.experimental.pallas import tpu as pltpu

F32 = jnp.float32
BF16 = jnp.bfloat16

N_DEV = 8
N_META = 16
CHUNK = 64
PAD = CHUNK - N_META
SUB = 16
HEAD = 128
CONV_WIDTH = 31
HALO = 32
POOL_WINDOWS = (2, 4, 8, 16)
EPS = 1e-6
NEG = -1e30
ADAM_LR, ADAM_B1, ADAM_B2, ADAM_EPS, ADAM_WD, ADAM_STEP = 0.001, 0.9, 0.999, 1e-08, 0.01, 10
VMEM_LIMIT = 56 * 1024 * 1024
EV_ROWS = 416
HGRN_ROWS = 832
MM_ROWS_BIG = 2080
MM_ROWS_MID = 1040
MM_ROWS_K4 = 416
SMALL_F32_ROWS = 8
HGRN_HEADS_FWD = 8
HGRN_HEADS_BWD = 2
MESH = pl.DeviceIdType.MESH
AXES = ("x", "y", "c")
ANY = pl.BlockSpec(memory_space=pl.ANY)


def _cp(*sem):
    return pltpu.CompilerParams(dimension_semantics=sem, vmem_limit_bytes=VMEM_LIMIT)


def _tile(n, cap, mult):
    best = None
    for d in range(mult, min(n, cap) + 1, mult):
        if n % d == 0:
            best = d
    assert best is not None, (n, cap, mult)
    return best


def _nt(a, b):
    return lax.dot_general(a, b, (((1,), (1,)), ((), ())), preferred_element_type=F32)


def _tn(a, b):
    return lax.dot_general(a, b, (((0,), (0,)), ((), ())), preferred_element_type=F32)


def _nn(a, b):
    return jnp.dot(a, b, preferred_element_type=F32)


def _r16(x):
    return x.astype(BF16).astype(F32)


def _row_ids(base, n):
    return base + lax.broadcasted_iota(jnp.int32, (n, 1), 0)


def _dsilu(x, s):
    return s * (1.0 + x * (1.0 - s))


def _loss_head(h, g, tgt):
    T, D = h.shape
    tm = _tile(T, MM_ROWS_MID, 16)
    first_x = PAD + N_META

    def body(h_ref, g_ref, t_ref, loss_ref, dh_ref, dhb_ref, dg_ref):
        i = pl.program_id(0)
        x = h_ref[...]
        r = lax.rsqrt(jnp.mean(x * x, axis=-1, keepdims=True) + EPS)
        xh = x * r
        gv = g_ref[...]
        out = xh * gv
        valid = _row_ids(i * tm, tm) >= first_x
        e = jnp.where(valid, out - t_ref[...], 0.0)
        dout = e * (1.0 / D)
        dxh = dout * gv
        dx = r * (dxh - xh * jnp.mean(dxh * xh, axis=-1, keepdims=True))
        dh_ref[...] = dx
        dhb_ref[...] = dx.astype(BF16)

        @pl.when(i == 0)
        def _():
            dg_ref[...] = jnp.zeros_like(dg_ref)
            loss_ref[...] = jnp.zeros_like(loss_ref)

        dg_ref[...] += jnp.sum(dout * xh, axis=0, keepdims=True)
        loss_ref[...] += 0.5 * jnp.sum(jnp.mean(e * e, axis=-1, keepdims=True))

    row = pl.BlockSpec((tm, D), lambda i: (i, 0))
    vec = pl.BlockSpec((1, D), lambda i: (0, 0))
    return pl.pallas_call(
        body, grid=(T // tm,),
        in_specs=[row, vec, row],
        out_specs=[pl.BlockSpec((8, 128), lambda i: (0, 0)), row, row, vec],
        out_shape=[jax.ShapeDtypeStruct((8, 128), F32), jax.ShapeDtypeStruct((T, D), F32),
                   jax.ShapeDtypeStruct((T, D), BF16), jax.ShapeDtypeStruct((1, D), F32)],
        compiler_params=_cp("arbitrary"), name="loss_head")(h, g, tgt)


def _mm_nn(name, a, w, w_spec, M, N, K, tm, tn, tk, mode, extra=None, a_spec=None):
    nk = K // tk
    if a_spec is None:
        a_spec = pl.BlockSpec((tm, tk), lambda i, j, k: (i, k))
    o_spec = pl.BlockSpec((tm, tn), lambda i, j, k: (i, j))

    def body(*refs):
        if mode == "resid":
            a_ref, w_ref, e_ref = refs[:3]
            outs = refs[3:]
        else:
            a_ref, w_ref = refs[:2]
            outs = refs[2:]
        acc_ref = outs[-1] if nk > 1 else None
        part = _nn(a_ref[...], w_ref[...])

        def finish(acc):
            if mode == "f32":
                outs[0][...] = acc
            elif mode == "relu2":
                r = jnp.maximum(acc, 0.0)
                outs[0][...] = r.astype(BF16)
                outs[1][...] = (r * r).astype(BF16)
            else:
                keep = _row_ids(pl.program_id(0) * tm, tm) >= PAD
                outs[0][...] = jnp.where(keep, e_ref[...] + acc, 0.0)

        if nk == 1:
            finish(part)
        else:
            k = pl.program_id(2)

            @pl.when(k == 0)
            def _():
                acc_ref[...] = part

            @pl.when(k > 0)
            def _():
                acc_ref[...] += part

            @pl.when(k == nk - 1)
            def _():
                finish(acc_ref[...])

    in_specs = [a_spec, w_spec(tk, tn)]
    args = [a, w]
    if mode == "resid":
        in_specs.append(o_spec)
        args.append(extra)
    if mode == "relu2":
        out_specs = [o_spec, o_spec]
        out_shape = [jax.ShapeDtypeStruct((M, N), BF16)] * 2
    else:
        out_specs = [o_spec]
        out_shape = [jax.ShapeDtypeStruct((M, N), F32)]
    scratch = [pltpu.VMEM((tm, tn), F32)] if nk > 1 else []
    res = pl.pallas_call(
        body, grid=(M // tm, N // tn, nk), in_specs=in_specs, out_specs=out_specs, out_shape=out_shape,
        scratch_shapes=scratch, compiler_params=_cp("parallel", "parallel", "arbitrary"), name=name)(*args)
    return res if mode == "relu2" else res[0]


def _mm_rms_nn(name, h, g, w, tm, tn, mode):
    M, K = h.shape
    N = w.shape[1]

    def body(h_ref, g_ref, w_ref, n_ref, *outs):
        @pl.when(pl.program_id(1) == 0)
        def _():
            ch = _tile(tm, 256, 16)

            def chunk(c, carry):
                rows = pl.ds(pl.multiple_of(c * ch, ch), ch)
                x = h_ref[rows, :]
                r = lax.rsqrt(jnp.mean(x * x, axis=-1, keepdims=True) + EPS)
                n_ref[rows, :] = ((x * r) * g_ref[...]).astype(BF16)
                return carry

            lax.fori_loop(0, tm // ch, chunk, 0)

        acc = _nn(n_ref[...], w_ref[...])
        if mode == "f32":
            outs[0][...] = acc
        else:
            r = jnp.maximum(acc, 0.0)
            outs[0][...] = r.astype(BF16)
            outs[1][...] = (r * r).astype(BF16)

    row = pl.BlockSpec((tm, K), lambda i, j: (i, 0))
    o_spec = pl.BlockSpec((tm, tn), lambda i, j: (i, j))
    n_out = 1 if mode == "f32" else 2
    return pl.pallas_call(
        body, grid=(M // tm, N // tn),
        in_specs=[row, pl.BlockSpec((1, K), lambda i, j: (0, 0)), pl.BlockSpec((K, tn), lambda i, j: (0, j))],
        out_specs=[row] + [o_spec] * n_out,
        out_shape=[jax.ShapeDtypeStruct((M, K), BF16)] + [jax.ShapeDtypeStruct((M, N), F32 if mode == "f32" else BF16)] * n_out,
        compiler_params=_cp("parallel", "arbitrary"), name=name)(h, g, w)


def _mm_nt(name, dy, w, dy_spec, w_spec, M, J, N, tm, tj, tn, mode, extra=None, parts=1):
    nk = N // tn
    o_spec = pl.BlockSpec((tm, tj), lambda i, j, k: (i, j))
    n_extra = {"f32": 0, "dact": 1, "rms": 3}[mode]
    if mode == "rms":
        assert nk == 1 and tj == J

    def body(*refs):
        dy_ref, w_ref = refs[:2]
        ex = refs[2:2 + n_extra]
        outs = refs[2 + n_extra:]
        acc_ref = outs[-1] if nk > 1 else None
        if parts == 1:
            part = _nt(dy_ref[...], w_ref[...])
        else:
            wq = tn // parts
            part = _nt(dy_ref[0], w_ref[:, 0:wq])
            for q in range(1, parts):
                part = part + _nt(dy_ref[q], w_ref[:, q * wq:(q + 1) * wq])

        def finish(acc):
            if mode == "f32":
                outs[0][...] = acc
            elif mode == "dact":
                outs[0][...] = (acc * (2.0 * ex[0][...].astype(F32))).astype(BF16)
            else:
                h_ref, g_ref, dres_ref = ex
                dh_ref, dhb_ref, dg_ref = outs[:3]
                i = pl.program_id(0)

                @pl.when(i == 0)
                def _():
                    dg_ref[...] = jnp.zeros_like(dg_ref)

                ch = _tile(tm, 256, 16)
                for c0 in range(0, tm, ch):
                    a_c = acc[c0:c0 + ch]
                    x = h_ref[c0:c0 + ch, :]
                    r = lax.rsqrt(jnp.mean(x * x, axis=-1, keepdims=True) + EPS)
                    xh = x * r
                    dxh = a_c * g_ref[...]
                    dx = r * (dxh - xh * jnp.mean(dxh * xh, axis=-1, keepdims=True))
                    keep = _row_ids(i * tm + c0, ch) >= PAD
                    dh = jnp.where(keep, dres_ref[c0:c0 + ch, :] + dx, 0.0)
                    dh_ref[c0:c0 + ch, :] = dh
                    dhb_ref[c0:c0 + ch, :] = dh.astype(BF16)
                    dg_ref[...] += jnp.sum(a_c * xh, axis=0, keepdims=True)

        if nk == 1:
            finish(part)
        else:
            k = pl.program_id(2)

            @pl.when(k == 0)
            def _():
                acc_ref[...] = part

            @pl.when(k > 0)
            def _():
                acc_ref[...] += part

            @pl.when(k == nk - 1)
            def _():
                finish(acc_ref[...])

    in_specs = [dy_spec(tm, tn), w_spec(tj, tn)]
    args = [dy, w]
    scratch = [pltpu.VMEM((tm, tj), F32)] if nk > 1 else []
    if mode == "rms":
        vec = pl.BlockSpec((1, J), lambda i, j, k: (0, 0))
        h, g, dres = extra
        res = pl.pallas_call(
            body, grid=(M // tm, 1, 1), in_specs=in_specs + [o_spec, vec, o_spec], out_specs=[o_spec, o_spec, vec],
            out_shape=[jax.ShapeDtypeStruct((M, J), F32), jax.ShapeDtypeStruct((M, J), BF16),
                       jax.ShapeDtypeStruct((1, J), F32)],
            compiler_params=_cp("arbitrary", "arbitrary", "arbitrary"), name=name)(*args, h, g, dres)
        return res
    if mode == "dact":
        in_specs.append(o_spec)
        args.append(extra)
    return pl.pallas_call(
        body, grid=(M // tm, J // tj, nk), in_specs=in_specs, out_specs=[o_spec],
        out_shape=[jax.ShapeDtypeStruct((M, J), BF16 if mode == "dact" else F32)],
        scratch_shapes=scratch, compiler_params=_cp("parallel", "parallel", "arbitrary"), name=name)(*args)[0]


def _mm_tn(name, x, dy, x_spec, dy_spec, o_spec, o_shape, T, K, N, tt, tk, tn):
    nt = T // tt

    def body(x_ref, dy_ref, o_ref, *acc):
        part = _tn(x_ref[...], dy_ref[...])
        if nt == 1:
            o_ref[...] = part.astype(BF16)
            return
        acc_ref = acc[0]
        t = pl.program_id(2)

        @pl.when(t == 0)
        def _():
            acc_ref[...] = part

        @pl.when(t > 0)
        def _():
            acc_ref[...] += part

        @pl.when(t == nt - 1)
        def _():
            o_ref[...] = acc_ref[...].astype(BF16)

    return pl.pallas_call(
        body, grid=(K // tk, N // tn, nt), in_specs=[x_spec(tt, tk), dy_spec(tt, tn)], out_specs=o_spec(tk, tn),
        out_shape=jax.ShapeDtypeStruct(o_shape, BF16), scratch_shapes=[pltpu.VMEM((tk, tn), F32)] if nt > 1 else [],
        compiler_params=_cp("parallel", "parallel", "arbitrary"), name=name)(x, dy)


def _pool_counts(base, n, w):
    pos = _row_ids(base, n) - PAD
    return jnp.clip(pos + 1, 1, w).astype(F32)


def _shifted_copies(buf, rows):
    buf[0, rows:rows + 8, :] = jnp.zeros((8, buf.shape[2]), F32)

    def blk(s, carry):
        b = pl.multiple_of(s * HALO, HALO)
        win = buf[0, pl.ds(b, HALO + 8), :]
        for r in range(1, 8):
            buf[r, pl.ds(b, HALO), :] = win[r:r + HALO]
        return carry

    lax.fori_loop(0, rows // HALO, blk, 0)


def _ev_fwd(u, cw, cb, lg, lb, pw, pb, ps):
    T = u.shape[0]
    C = 512
    tm = _tile(T, EV_ROWS, HALO)
    nsub = tm // HALO
    hb = tm // HALO

    def body(val_ref, gate_ref, pin_ref, valh_ref, gateh_ref, pinh_ref, cw_ref, cb_ref, lg_ref, lb_ref, pw_ref,
             pb_ref, ps_ref, yab_ref, yc_ref, a_ext, p_ext, d_buf):
        i = pl.program_id(0)
        nf = (i > 0).astype(F32)
        a_ext[0, 0:HALO, :] = valh_ref[...] * jax.nn.sigmoid(gateh_ref[...]) * nf
        a_ext[0, HALO:HALO + tm, :] = val_ref[...] * jax.nn.sigmoid(gate_ref[...])
        p_ext[0:HALO, :] = pinh_ref[...] * nf
        p_ext[HALO:, :] = pin_ref[...]
        _shifted_copies(a_ext, tm + HALO)

        def sub(s, carry):
            base = pl.multiple_of(s * HALO, HALO)
            acc = jnp.zeros((HALO, C), F32) + cb_ref[...]
            for j in range(CONV_WIDTH):
                off = 2 + j
                acc = acc + cw_ref[pl.ds(j, 1), :] * a_ext[off % 8, pl.ds(pl.multiple_of(base + off // 8 * 8, 8), HALO), :]
            yc_ref[pl.ds(base, HALO), :] = acc
            mu = jnp.mean(acc, axis=-1, keepdims=True)
            yc = acc - mu
            rstd = lax.rsqrt(jnp.mean(yc * yc, axis=-1, keepdims=True) + EPS)
            z = (yc * rstd) * lg_ref[...] + lb_ref[...]
            yab_ref[pl.ds(base, HALO), 0:C] = (z * jax.nn.sigmoid(z)).astype(BF16)
            pwin = p_ext[pl.ds(base, 2 * HALO), :]
            for gi, w in enumerate(POOL_WINDOWS):
                lo, hi = gi * HEAD, (gi + 1) * HEAD
                x = pwin[HALO:, lo:hi]
                tot = x
                for k in range(1, w):
                    tot = tot + pwin[HALO - k:2 * HALO - k, lo:hi]
                cnt = _pool_counts(i * tm + base, HALO, w)
                d_buf[pl.ds(base, HALO), lo:hi] = (tot / cnt - x).astype(BF16)
            return carry

        lax.fori_loop(0, nsub, sub, 0, unroll=2)
        for gi in range(len(POOL_WINDOWS)):
            lo, hi = gi * HEAD, (gi + 1) * HEAD
            y = _nn(d_buf[:, lo:hi], pw_ref[gi]) + pb_ref[:, lo:hi]
            yab_ref[:, C + lo:C + hi] = (y * ps_ref[:, lo:hi]).astype(BF16)

    def main(c):
        return pl.BlockSpec((tm, C), lambda i: (i, c))

    def halo(c):
        return pl.BlockSpec((HALO, C), lambda i: (jnp.maximum(i * hb - 1, 0), c))

    vec = pl.BlockSpec((1, C), lambda i: (0, 0))
    return pl.pallas_call(
        body, grid=(T // tm,),
        in_specs=[main(0), main(1), main(2), halo(0), halo(1), halo(2),
                  pl.BlockSpec((32, C), lambda i: (0, 0)), vec, vec, vec,
                  pl.BlockSpec((4, HEAD, HEAD), lambda i: (0, 0, 0)), vec, vec],
        out_specs=[pl.BlockSpec((tm, 2 * C), lambda i: (i, 0)), pl.BlockSpec((tm, C), lambda i: (i, 0))],
        out_shape=[jax.ShapeDtypeStruct((T, 2 * C), BF16), jax.ShapeDtypeStruct((T, C), F32)],
        scratch_shapes=[pltpu.VMEM((8, tm + HALO + 8, C), F32), pltpu.VMEM((tm + HALO, C), F32),
                        pltpu.VMEM((tm, C), BF16)],
        compiler_params=_cp("parallel"), name="ev_fwd")(u, u, u, u, u, u, cw, cb, lg, lb, pw, pb, ps)


def _ev_bwd(dyab, yc, u, cw, lg, lb, pw, pwt, pb, ps):
    T = u.shape[0]
    C = 512
    tm = _tile(T, EV_ROWS, HALO)
    nsub = tm // HALO
    hb = tm // HALO
    nblk = T // tm
    E = tm + HALO

    def body(dya_ref, dyb_ref, dyah_ref, dybh_ref, yc_ref, ych_ref, val_ref, gate_ref, pin_ref, valh_ref, gateh_ref,
             pinh_ref, cw_ref, lg_ref, lb_ref, pw_ref, pwt_ref, pb_ref, ps_ref,
             du_ref, dcw_ref, dvec_ref, dpw_ref,
             dy_ext, a_ext, p_ext, ddc_ext, dd_buf, d_buf, dpre_buf, dcw_acc, vec_acc):
        i = pl.program_id(0)
        nf = (i > 0).astype(F32)
        nl = (i < nblk - 1).astype(F32)

        @pl.when(i == 0)
        def _():
            dcw_ref[...] = jnp.zeros_like(dcw_ref)
            dvec_ref[...] = jnp.zeros_like(dvec_ref)
            dpw_ref[...] = jnp.zeros_like(dpw_ref)

        dcw_acc[...] = jnp.zeros_like(dcw_acc)
        vec_acc[...] = jnp.zeros_like(vec_acc)
        a_ext[0, 0:HALO, :] = valh_ref[...] * jax.nn.sigmoid(gateh_ref[...]) * nf
        a_ext[0, HALO:E, :] = val_ref[...] * jax.nn.sigmoid(gate_ref[...])
        p_ext[0:HALO, :] = pinh_ref[...] * nf
        p_ext[HALO:, :] = pin_ref[...]
        _shifted_copies(a_ext, E)

        def ln_bwd(y, dya, main):
            mu = jnp.mean(y, axis=-1, keepdims=True)
            ycen = y - mu
            rstd = lax.rsqrt(jnp.mean(ycen * ycen, axis=-1, keepdims=True) + EPS)
            yh = ycen * rstd
            z = yh * lg_ref[...] + lb_ref[...]
            sz = jax.nn.sigmoid(z)
            dz = dya * _dsilu(z, sz)
            dyh = dz * lg_ref[...]
            dy = rstd * (dyh - jnp.mean(dyh, axis=-1, keepdims=True) - yh * jnp.mean(dyh * yh, axis=-1, keepdims=True))
            if main:
                vec_acc[1] += jnp.sum((dz * yh).reshape(HALO // 8, 8, C), axis=0)
                vec_acc[2] += jnp.sum(dz.reshape(HALO // 8, 8, C), axis=0)
                vec_acc[0] += jnp.sum(dy.reshape(HALO // 8, 8, C), axis=0)
            return dy

        def pool_dd(dyb, base, main):
            dpre = dyb * ps_ref[...]
            for gi, w in enumerate(POOL_WINDOWS):
                lo, hi = gi * HEAD, (gi + 1) * HEAD
                dd = _nn(dpre[:, lo:hi].astype(BF16), pwt_ref[gi])
                cnt = _pool_counts(i * tm + base, HALO, w)
                ddc_ext[pl.ds(base, HALO), lo:hi] = dd / cnt
                if main:
                    dd_buf[pl.ds(base, HALO), lo:hi] = dd
            if main:
                dpre_buf[pl.ds(base, HALO), :] = dpre.astype(BF16)
                vec_acc[4] += jnp.sum(dpre.reshape(HALO // 8, 8, C), axis=0)

        def p1(s, carry):
            base = pl.multiple_of(s * HALO, HALO)
            dy_ext[0, pl.ds(base, HALO), :] = ln_bwd(yc_ref[pl.ds(base, HALO), :], dya_ref[pl.ds(base, HALO), :], True)
            pool_dd(dyb_ref[pl.ds(base, HALO), :], base, True)
            return carry

        lax.fori_loop(0, nsub, p1, 0, unroll=2)
        dy_ext[0, tm:E, :] = ln_bwd(ych_ref[...], dyah_ref[...], False) * nl
        _shifted_copies(dy_ext, E)
        dpre_h = dybh_ref[...] * ps_ref[...] * nl
        for gi, w in enumerate(POOL_WINDOWS):
            lo, hi = gi * HEAD, (gi + 1) * HEAD
            dd = _nn(dpre_h[:, lo:hi].astype(BF16), pwt_ref[gi])
            ddc_ext[tm:, lo:hi] = dd / _pool_counts(i * tm + tm, HALO, w)

        def p2(s, carry):
            base = pl.multiple_of(s * HALO, HALO)
            dy_m = dy_ext[0, pl.ds(base, HALO), :]
            da = jnp.zeros((HALO, C), F32)
            for j in range(CONV_WIDTH):
                sh = CONV_WIDTH - 1 - j
                off = 2 + j
                da = da + cw_ref[pl.ds(j, 1), :] * dy_ext[sh % 8, pl.ds(pl.multiple_of(base + sh // 8 * 8, 8), HALO), :]
                a_j = a_ext[off % 8, pl.ds(pl.multiple_of(base + off // 8 * 8, 8), HALO), :]
                dcw_acc[j] += jnp.sum((dy_m * a_j).reshape(HALO // 8, 8, C), axis=0)
            v = val_ref[pl.ds(base, HALO), :]
            g = gate_ref[pl.ds(base, HALO), :]
            sg = jax.nn.sigmoid(g)
            du_ref[pl.ds(base, HALO), 0:C] = (da * sg).astype(BF16)
            du_ref[pl.ds(base, HALO), C:2 * C] = (da * v * sg * (1.0 - sg)).astype(BF16)
            pwin = p_ext[pl.ds(base, 2 * HALO), :]
            cwin = ddc_ext[pl.ds(base, 2 * HALO), :]
            for gi, w in enumerate(POOL_WINDOWS):
                lo, hi = gi * HEAD, (gi + 1) * HEAD
                x = pwin[HALO:, lo:hi]
                tot = x
                back = cwin[0:HALO, lo:hi]
                for k in range(1, w):
                    tot = tot + pwin[HALO - k:2 * HALO - k, lo:hi]
                    back = back + cwin[k:k + HALO, lo:hi]
                cnt = _pool_counts(i * tm + base, HALO, w)
                d_buf[pl.ds(base, HALO), lo:hi] = (tot / cnt - x).astype(BF16)
                du_ref[pl.ds(base, HALO), 2 * C + lo:2 * C + hi] = (back - dd_buf[pl.ds(base, HALO), lo:hi]).astype(BF16)
            return carry

        lax.fori_loop(0, nsub, p2, 0)
        for gi in range(len(POOL_WINDOWS)):
            lo, hi = gi * HEAD, (gi + 1) * HEAD
            pre = _nn(d_buf[:, lo:hi], pw_ref[gi]) + pb_ref[:, lo:hi]
            vec_acc[3, :, lo:hi] += jnp.sum((dyb_ref[:, lo:hi] * pre).reshape(tm // 8, 8, HEAD), axis=0)
            dpw_ref[gi] += _tn(d_buf[:, lo:hi], dpre_buf[:, lo:hi])
        for j in range(CONV_WIDTH):
            dcw_ref[pl.ds(j, 1), :] += jnp.sum(dcw_acc[j], axis=0, keepdims=True)
        for r in range(5):
            dvec_ref[pl.ds(r, 1), :] += jnp.sum(vec_acc[r], axis=0, keepdims=True)

    def main(c, width=C):
        return pl.BlockSpec((tm, width), lambda i: (i, c))

    def prev(c):
        return pl.BlockSpec((HALO, C), lambda i: (jnp.maximum(i * hb - 1, 0), c))

    def nxt(c):
        return pl.BlockSpec((HALO, C), lambda i: (jnp.minimum((i + 1) * hb, T // HALO - 1), c))

    vec = pl.BlockSpec((1, C), lambda i: (0, 0))
    mat = pl.BlockSpec((4, HEAD, HEAD), lambda i: (0, 0, 0))
    return pl.pallas_call(
        body, grid=(nblk,),
        in_specs=[main(0), main(1), nxt(0), nxt(1), main(0), nxt(0), main(0), main(1), main(2), prev(0), prev(1),
                  prev(2), pl.BlockSpec((32, C), lambda i: (0, 0)), vec, vec, mat, mat, vec, vec],
        out_specs=[pl.BlockSpec((tm, 3 * C), lambda i: (i, 0)), pl.BlockSpec((32, C), lambda i: (0, 0)),
                   pl.BlockSpec((8, C), lambda i: (0, 0)), mat],
        out_shape=[jax.ShapeDtypeStruct((T, 3 * C), BF16), jax.ShapeDtypeStruct((32, C), F32),
                   jax.ShapeDtypeStruct((8, C), F32), jax.ShapeDtypeStruct((4, HEAD, HEAD), F32)],
        scratch_shapes=[pltpu.VMEM((8, E + 8, C), F32), pltpu.VMEM((8, E + 8, C), F32), pltpu.VMEM((E, C), F32),
                        pltpu.VMEM((E, C), F32), pltpu.VMEM((tm, C), F32), pltpu.VMEM((tm, C), BF16),
                        pltpu.VMEM((tm, C), BF16), pltpu.VMEM((32, 8, C), F32), pltpu.VMEM((8, 8, C), F32)],
        compiler_params=_cp("arbitrary"), name="ev_bwd")(
            dyab, dyab, dyab, dyab, yc, yc, u, u, u, u, u, u, cw, lg, lb, pw, pwt, pb, ps)


def _cumsum_rows(x, reverse=False):
    n = x.shape[0]
    rid = lax.broadcasted_iota(jnp.int32, (n, 1), 0)
    k = 1
    while k < n:
        if reverse:
            sh = jnp.where(rid < n - k, pltpu.roll(x, n - k, 0), 0.0)
        else:
            sh = jnp.where(rid >= k, pltpu.roll(x, k, 0), 0.0)
        x = x + sh
        k *= 2
    return x


def _hgrn_gates(qr, fr, lbv):
    sq = jax.nn.sigmoid(qr)
    sg = jax.nn.sigmoid(fr)
    fg = lbv + (1.0 - lbv) * sg
    return qr * sq, sq, sg, fg, 1.0 - fg, jnp.log(fg)


def _hgrn_fwd(u, lbv, gn):
    T = u.shape[0]
    H = 8
    RB = _tile(T, HGRN_ROWS, CHUNK)
    NC = RB // CHUNK
    NS = CHUNK // SUB

    HP = HGRN_HEADS_FWD
    W = HP * HEAD

    def body(q_ref, f_ref, i_ref, g_ref, lb_ref, gn_ref, y_ref, o_ref, s0_ref, st, qs, ks, bs, vs, os_):
        rb = pl.program_id(1)

        @pl.when(rb == 0)
        def _():
            st[...] = jnp.zeros_like(st)

        t8 = lax.broadcasted_iota(jnp.int32, (8, 1), 0)

        def head(hh, c, rows):
            sl = slice(hh * HEAD, (hh + 1) * HEAD)
            q, _, _, _, kk, lf = _hgrn_gates(q_ref[rows, sl], f_ref[rows, sl], lb_ref[:, sl])
            v = i_ref[rows, sl]
            b = _cumsum_rows(lf)
            qs[hh] = q
            ks[hh] = kk
            bs[hh] = b
            vs[hh] = v
            st0 = st[hh]
            s0_ref[hh, c] = st0
            os_[hh] = _nt((q * jnp.exp(b)).astype(BF16), st0.astype(BF16))
            for I in range(NS):
                lo = I * SUB
                qI = qs[hh, lo:lo + SUB, :]
                bI = bs[hh, lo:lo + SUB, :]
                oI = jnp.zeros((SUB, HEAD), F32)
                if I > 0:
                    bprev = bs[hh, pl.ds(lo - 1, 1), :]
                    qt = _r16(qI * jnp.exp(bI - bprev))
                    kt = _r16(ks[hh, 0:lo, :] * jnp.exp(bprev - bs[hh, 0:lo, :]))
                    A = _nt(qt, kt)
                    oI = oI + _nn(_r16(A), _r16(vs[hh, 0:lo, :]))
                od = [jnp.zeros((8, HEAD), F32) for _ in range(SUB // 8)]
                for s in range(SUB):
                    row = pl.ds(lo + s, 1)
                    brow, krow, vrow = bs[hh, row, :], ks[hh, row, :], vs[hh, row, :]
                    for ti in range(SUB // 8):
                        o8 = 8 * ti
                        if s > o8 + 7:
                            continue
                        d = bI[o8:o8 + 8] - brow
                        if s > o8:
                            d = jnp.where(t8 >= s - o8, d, NEG)
                        col = jnp.sum(qI[o8:o8 + 8] * jnp.exp(d) * krow, axis=1, keepdims=True)
                        od[ti] = od[ti] + col * vrow
                os_[hh, lo:lo + SUB, :] += oI + jnp.concatenate(od, axis=0)
            blast = bs[hh, pl.ds(CHUNK - 1, 1), :]
            kh = kk * jnp.exp(blast - b)
            st[hh] = st0 * jnp.exp(blast) + _tn(v.astype(BF16), kh.astype(BF16))
            o = os_[hh]
            o_ref[rows, sl] = o
            rr = lax.rsqrt(jnp.mean(o * o, axis=-1, keepdims=True) + EPS)
            gr = g_ref[rows, sl]
            y_ref[rows, sl] = (((o * rr) * gn_ref[...]) * (gr * jax.nn.sigmoid(gr))).astype(BF16)

        def chunk(c, carry):
            rows = pl.ds(pl.multiple_of(c * CHUNK, CHUNK), CHUNK)
            for hh in range(HP):
                head(hh, c, rows)
            return carry

        lax.fori_loop(0, NC, chunk, 0)

    def blk(q):
        return pl.BlockSpec((RB, W), lambda h, r: (r, q * (H // HP) + h))

    sc = lambda: pltpu.VMEM((HP, CHUNK, HEAD), F32)
    return pl.pallas_call(
        body, grid=(H // HP, T // RB),
        in_specs=[blk(0), blk(1), blk(2), blk(3), pl.BlockSpec((1, W), lambda h, r: (0, h)),
                  pl.BlockSpec((1, HEAD), lambda h, r: (0, 0))],
        out_specs=[pl.BlockSpec((RB, W), lambda h, r: (r, h)), pl.BlockSpec((RB, W), lambda h, r: (r, h)),
                   pl.BlockSpec((HP, NC, HEAD, HEAD), lambda h, r: (h, r, 0, 0))],
        out_shape=[jax.ShapeDtypeStruct((T, H * HEAD), BF16), jax.ShapeDtypeStruct((T, H * HEAD), F32),
                   jax.ShapeDtypeStruct((H, T // CHUNK, HEAD, HEAD), F32)],
        scratch_shapes=[pltpu.VMEM((HP, HEAD, HEAD), F32), sc(), sc(), sc(), sc(), sc()],
        compiler_params=_cp("parallel", "arbitrary"), name="hgrn_fwd")(u, u, u, u, lbv, gn)


def _hgrn_bwd(dy, o, s0, u, lbv, gn):
    T = u.shape[0]
    H = 8
    RB = _tile(T, HGRN_ROWS, CHUNK)
    NB = T // RB
    NC = RB // CHUNK
    NS = CHUNK // SUB

    def body(q_ref, f_ref, i_ref, g_ref, lb_ref, gn_ref, o_ref, dy_ref, s0_ref, du_ref, dlb_ref, dgn_ref,
             dst, qs, ks, bs, vs, dos, dqs, dks, dki, dvs, dbs):
        rb = pl.program_id(1)

        @pl.when(rb == 0)
        def _():
            dst[...] = jnp.zeros_like(dst)
            dlb_ref[...] = jnp.zeros_like(dlb_ref)
            dgn_ref[...] = jnp.zeros_like(dgn_ref)

        t8 = lax.broadcasted_iota(jnp.int32, (8, 1), 0)
        lane = lax.broadcasted_iota(jnp.int32, (8, HEAD), 1)
        gnv = gn_ref[...]

        def head(hh, c, rows):
            sl = slice(hh * HEAD, (hh + 1) * HEAD)
            lbv_ = lb_ref[:, sl]
            qr = q_ref[rows, sl]
            q, sq, sg, fg, kk, lf = _hgrn_gates(qr, f_ref[rows, sl], lbv_)
            v = i_ref[rows, sl]
            gr = g_ref[rows, sl]
            b = _cumsum_rows(lf)
            eb = jnp.exp(b)
            ov = o_ref[rows, sl]
            dyv = dy_ref[rows, sl]
            rr = lax.rsqrt(jnp.mean(ov * ov, axis=-1, keepdims=True) + EPS)
            oh = ov * rr
            gs = jax.nn.sigmoid(gr)
            dgr = dyv * (oh * gnv) * _dsilu(gr, gs)
            dnrm = dyv * (gr * gs)
            dgn_ref[hh] += jnp.sum(dnrm * oh, axis=0, keepdims=True)
            t1 = dnrm * gnv
            do = rr * (t1 - oh * jnp.mean(t1 * oh, axis=-1, keepdims=True))
            qs[hh] = q
            ks[hh] = kk
            bs[hh] = b
            vs[hh] = v
            dos[hh] = do
            st0 = s0_ref[hh, c]
            dS = dst[hh]
            do_b = do.astype(BF16)
            blast = bs[hh, pl.ds(CHUNK - 1, 1), :]
            elast = jnp.exp(blast - b)
            dq_inter = _nn(do_b, st0.astype(BF16)) * eb
            dqs[hh] = dq_inter
            dbs[hh] = q * dq_inter
            kh = kk * elast
            dvs[hh] = _nt(kh.astype(BF16), dS.astype(BF16))
            dk_inter = _nn(v.astype(BF16), dS.astype(BF16)) * elast
            dki[hh] = dk_inter
            dks[hh] = jnp.zeros((CHUNK, HEAD), F32)
            for I in range(NS):
                lo = I * SUB
                qI = qs[hh, lo:lo + SUB, :]
                bI = bs[hh, lo:lo + SUB, :]
                doI = dos[hh, lo:lo + SUB, :]
                dqI = jnp.zeros((SUB, HEAD), F32)
                dbI = jnp.zeros((SUB, HEAD), F32)
                if I > 0:
                    bprev = bs[hh, pl.ds(lo - 1, 1), :]
                    eq = jnp.exp(bI - bprev)
                    ek = jnp.exp(bprev - bs[hh, 0:lo, :])
                    qt = _r16(qI * eq)
                    kt = _r16(ks[hh, 0:lo, :] * ek)
                    A = _r16(_nt(qt, kt))
                    doI_b = _r16(doI)
                    dA = _r16(_nt(doI_b, _r16(vs[hh, 0:lo, :])))
                    dvs[hh, 0:lo, :] += _tn(A, doI_b)
                    dqt = _nn(dA, kt)
                    dkt = _tn(dA, qt)
                    dqI = dqI + dqt * eq
                    dbI = dbI + qt.astype(F32) * dqt
                    dks[hh, 0:lo, :] += dkt * ek
                    dbs[hh, 0:lo, :] -= kt.astype(F32) * dkt
                dq_t = [jnp.zeros((8, HEAD), F32) for _ in range(SUB // 8)]
                a_t = [jnp.zeros((8, HEAD), F32) for _ in range(SUB // 8)]
                for s in range(SUB):
                    row = pl.ds(lo + s, 1)
                    brow, krow, vrow = bs[hh, row, :], ks[hh, row, :], vs[hh, row, :]
                    dk_s = None
                    for ti in range(SUB // 8):
                        o8 = 8 * ti
                        if s > o8 + 7:
                            continue
                        d = bI[o8:o8 + 8] - brow
                        if s > o8:
                            d = jnp.where(t8 >= s - o8, d, NEG)
                        Es = jnp.exp(d)
                        qE = qI[o8:o8 + 8] * Es
                        col = jnp.sum(qE * krow, axis=1, keepdims=True)
                        a_t[ti] = jnp.where(lane == s, col, a_t[ti])
                        dcol = jnp.sum(doI[o8:o8 + 8] * vrow, axis=1, keepdims=True)
                        dq_t[ti] = dq_t[ti] + (dcol * Es) * krow
                        part = jnp.sum(dcol * qE, axis=0, keepdims=True)
                        dk_s = part if dk_s is None else dk_s + part
                    dks[hh, row, :] += dk_s
                    dbs[hh, row, :] -= krow * dk_s
                a_d = jnp.concatenate(a_t, axis=0)
                dq_d = jnp.concatenate(dq_t, axis=0)
                dvs[hh, lo:lo + SUB, :] += _tn(a_d, doI)[0:SUB]
                dqI = dqI + dq_d
                dbI = dbI + qI * dq_d
                dqs[hh, lo:lo + SUB, :] += dqI
                dbs[hh, lo:lo + SUB, :] += dbI
            kdk = kk * dki[hh]
            excl = _cumsum_rows(kdk) - kdk
            suff = _cumsum_rows(dbs[hh], reverse=True)
            gdec = jnp.sum(dS * st0, axis=0, keepdims=True) * jnp.exp(blast)
            dlf = suff + excl + gdec
            dk = dks[hh] + dki[hh]
            dfg = dlf / fg - dk
            dlb_ref[:, sl] += jnp.sum(dfg * (1.0 - sg), axis=0, keepdims=True)
            du_ref[0, rows, sl] = (dqs[hh] * _dsilu(qr, sq)).astype(BF16)
            du_ref[1, rows, sl] = (dfg * (1.0 - lbv_) * sg * (1.0 - sg)).astype(BF16)
            du_ref[2, rows, sl] = dvs[hh].astype(BF16)
            du_ref[3, rows, sl] = dgr.astype(BF16)
            dst[hh] = dS * jnp.exp(blast) + _tn(do_b, (q * eb).astype(BF16))

        def chunk(cc, carry):
            c = NC - 1 - cc
            rows = pl.ds(pl.multiple_of(c * CHUNK, CHUNK), CHUNK)
            for hh in range(HP):
                head(hh, c, rows)
            return carry

        lax.fori_loop(0, NC, chunk, 0)

    HP = HGRN_HEADS_BWD
    W = HP * HEAD

    def blk(qd):
        return pl.BlockSpec((RB, W), lambda h, r: (NB - 1 - r, qd * (H // HP) + h))

    hblk = pl.BlockSpec((RB, W), lambda h, r: (NB - 1 - r, h))
    sc = lambda: pltpu.VMEM((HP, CHUNK, HEAD), F32)
    return pl.pallas_call(
        body, grid=(H // HP, NB),
        in_specs=[blk(0), blk(1), blk(2), blk(3), pl.BlockSpec((1, W), lambda h, r: (0, h)),
                  pl.BlockSpec((1, HEAD), lambda h, r: (0, 0)), hblk, hblk,
                  pl.BlockSpec((HP, NC, HEAD, HEAD), lambda h, r: (h, NB - 1 - r, 0, 0))],
        out_specs=[pl.BlockSpec((4, RB, W), lambda h, r: (0, NB - 1 - r, h)),
                   pl.BlockSpec((1, W), lambda h, r: (0, h)), pl.BlockSpec((HP, 1, HEAD), lambda h, r: (h, 0, 0))],
        out_shape=[jax.ShapeDtypeStruct((4, T, H * HEAD), BF16), jax.ShapeDtypeStruct((1, H * HEAD), F32),
                   jax.ShapeDtypeStruct((H, 1, HEAD), F32)],
        scratch_shapes=[pltpu.VMEM((HP, HEAD, HEAD), F32)] + [sc() for _ in range(10)],
        compiler_params=_cp("parallel", "arbitrary"), name="hgrn_bwd")(u, u, u, u, lbv, gn, o, dy, s0)


def _softmax_rows(p_ref, L):
    rows = [p_ref[pl.ds(l, 1), :] for l in range(L)]
    m = rows[0]
    for r in rows[1:]:
        m = jnp.maximum(m, r)
    e = [jnp.exp(r - m) for r in rows]
    tot = e[0]
    for t in e[1:]:
        tot = tot + t
    return [t / tot for t in e]


def _lb_fwd(lbp):
    L, D = lbp.shape

    def body(p_ref, o_ref):
        sm = _softmax_rows(p_ref, L)
        acc = jnp.zeros((1, D), F32)
        o_ref[pl.ds(0, 1), :] = acc
        for l in range(1, L):
            acc = acc + sm[l]
            o_ref[pl.ds(l, 1), :] = acc

    return pl.pallas_call(body, out_shape=jax.ShapeDtypeStruct((L, D), F32), name="lb_fwd")(lbp)


def _lb_bwd(lbp, dlb):
    L, D = lbp.shape

    def body(p_ref, d_ref, o_ref):
        sm = _softmax_rows(p_ref, L)
        dsm = [jnp.zeros((1, D), F32)]
        for i in range(1, L):
            t = jnp.zeros((1, D), F32)
            for l in range(i, L):
                t = t + d_ref[pl.ds(l, 1), :]
            dsm.append(t)
        dot = jnp.zeros((1, D), F32)
        for i in range(L):
            dot = dot + dsm[i] * sm[i]
        for i in range(L):
            o_ref[pl.ds(i, 1), :] = sm[i] * (dsm[i] - dot)

    return pl.pallas_call(body, out_shape=jax.ShapeDtypeStruct((L, D), F32), name="lb_bwd")(lbp, dlb)


def _my_pos():
    return lax.axis_index("x"), lax.axis_index("y"), lax.axis_index("c")


def _peer(mask):
    x, y, c = _my_pos()
    mx, my, mc = (mask >> 2) & 1, (mask >> 1) & 1, mask & 1
    px = (1 - x) if mx else x
    py = (1 - y) if my else y
    pc = (1 - c) if mc else c
    return (px, py, pc), 4 * px + 2 * py + pc


def _all_gather(shards):
    n = len(shards)

    def body(*refs):
        ins, outs = refs[:n], refs[n:2 * n]
        send_sems, recv_sems, local_sems = refs[2 * n:]
        x, y, c = _my_pos()
        me = 4 * x + 2 * y + c
        local = [pltpu.make_async_copy(ins[a], outs[a].at[:, me], local_sems.at[a]) for a in range(n)]
        for cp in local:
            cp.start()
        sends = []
        for m in range(1, N_DEV):
            peer, _ = _peer(m)
            for a in range(n):
                cp = pltpu.make_async_remote_copy(
                    src_ref=ins[a], dst_ref=outs[a].at[:, me], send_sem=send_sems.at[a, m - 1],
                    recv_sem=recv_sems.at[a, m - 1], device_id=peer, device_id_type=MESH)
                cp.start()
                sends.append(cp)
        for m in range(1, N_DEV):
            peer, pid = _peer(m)
            for a in range(n):
                pltpu.make_async_remote_copy(
                    src_ref=ins[a], dst_ref=outs[a].at[:, pid], send_sem=send_sems.at[a, m - 1],
                    recv_sem=recv_sems.at[a, m - 1], device_id=peer, device_id_type=MESH).wait_recv()
        for cp in sends:
            cp.wait_send()
        for cp in local:
            cp.wait()

    out_shape = [jax.ShapeDtypeStruct((s.shape[0], N_DEV) + s.shape[1:], s.dtype) for s in shards]
    return pl.pallas_call(
        body, in_specs=[ANY] * n, out_specs=[ANY] * n, out_shape=out_shape,
        scratch_shapes=[pltpu.SemaphoreType.DMA((n, N_DEV - 1)), pltpu.SemaphoreType.DMA((n, N_DEV - 1)),
                        pltpu.SemaphoreType.DMA((n,))],
        name="all_gather_small")(*shards)


HBM_SPEC =pl.BlockSpec(memory_space=pltpu.HBM)
SEM_SPEC = pl.BlockSpec(memory_space=pltpu.SEMAPHORE)
EFFECT = pltpu.SideEffectType.DATAFLOW_SIDE_EFFECTING


def _hbm(a):
    return pltpu.with_memory_space_constraint(a, pltpu.HBM)


def _landing(block_shape, dtype, axis=0):
    if axis == 0:
        return lax.empty((N_DEV,) + tuple(block_shape), dtype)
    rows, n = block_shape
    return lax.empty((rows, N_DEV * n), dtype)


def _slot(ref, i):
    if len(ref.shape) == 2:
        n = ref.shape[1] // N_DEV
        return ref.at[:, pl.ds(i * n, n)]
    return ref.at[i]


def _push_start(name, srcs, lands, whole, groups):
    n = len(srcs)
    ng = 1 + max(groups)
    cnt = [groups.count(g) for g in range(ng)]
    idx = [groups[:a].count(groups[a]) for a in range(n)]

    def body(*refs):
        src_refs, land_refs = refs[:n], refs[n:2 * n]
        sems = refs[2 * n:2 * n + 3 * ng]
        token = refs[-1]
        x, y, c = _my_pos()
        me = 4 * x + 2 * y + c
        for a in range(n):
            g = groups[a]
            for m in range(1, N_DEV):
                peer, pid = _peer(m)
                pltpu.make_async_remote_copy(
                    src_ref=src_refs[a] if whole else src_refs[a].at[pid], dst_ref=_slot(land_refs[a], me),
                    send_sem=sems[3 * g].at[idx[a] * (N_DEV - 1) + m - 1],
                    recv_sem=sems[3 * g + 1].at[idx[a] * (N_DEV - 1) + m - 1],
                    device_id=peer, device_id_type=MESH).start()
            pltpu.make_async_copy(src_refs[a] if whole else src_refs[a].at[me], _slot(land_refs[a], me),
                                  sems[3 * g + 2].at[idx[a]]).start()
        token[...] = jnp.zeros_like(token)

    sem_shapes = []
    for g in range(ng):
        sem_shapes += [pltpu.SemaphoreType.DMA((cnt[g] * (N_DEV - 1),))] * 2 + [pltpu.SemaphoreType.DMA((cnt[g],))]
    thru = [pltpu.HBM(s.shape, s.dtype) for s in list(srcs) + list(lands)]
    res = pl.pallas_call(
        body, name=name,
        out_shape=tuple(sem_shapes + thru + [jax.ShapeDtypeStruct((8, 128), F32)]),
        in_specs=tuple([HBM_SPEC] * (2 * n)),
        out_specs=tuple([SEM_SPEC] * (3 * ng) + [HBM_SPEC] * (2 * n) + [pl.BlockSpec(memory_space=pltpu.VMEM)]),
        input_output_aliases={i: 3 * ng + i for i in range(2 * n)},
        compiler_params=pltpu.CompilerParams(has_side_effects=EFFECT),
    )(*[_hbm(s) for s in srcs], *[_hbm(z) for z in lands])
    sems = [(res[3 * g], res[3 * g + 1], res[3 * g + 2]) for g in range(ng)]
    srcs_thru = list(res[3 * ng:3 * ng + n])
    lands_thru = list(res[3 * ng + n:3 * ng + 2 * n])
    return sems, srcs_thru, lands_thru, res[-1]


def _push_wait(name, srcs_thru, lands_thru, sems, after, whole):
    n = len(srcs_thru)

    def body(*refs):
        src_refs, land_refs = refs[:n], refs[n:2 * n]
        send_sems, recv_sems, own_sems = refs[2 * n], refs[2 * n + 1], refs[2 * n + 2]
        x, y, c = _my_pos()
        me = 4 * x + 2 * y + c
        for a in range(n):
            pltpu.make_async_copy(src_refs[a] if whole else src_refs[a].at[me], _slot(land_refs[a], me),
                                  own_sems.at[a]).wait()
            for m in range(1, N_DEV):
                peer, pid = _peer(m)
                cp = pltpu.make_async_remote_copy(
                    src_ref=src_refs[a] if whole else src_refs[a].at[pid], dst_ref=_slot(land_refs[a], pid),
                    send_sem=send_sems.at[a * (N_DEV - 1) + m - 1], recv_sem=recv_sems.at[a * (N_DEV - 1) + m - 1],
                    device_id=peer, device_id_type=MESH)
                cp.wait_send()
                cp.wait_recv()

    thru = [pltpu.HBM(s.shape, s.dtype) for s in list(srcs_thru) + list(lands_thru)]
    res = pl.pallas_call(
        body, name=name, out_shape=tuple(thru),
        in_specs=tuple([HBM_SPEC] * (2 * n) + [SEM_SPEC, SEM_SPEC, SEM_SPEC, ANY]),
        out_specs=tuple([HBM_SPEC] * (2 * n)),
        input_output_aliases={i: i for i in range(2 * n)},
        compiler_params=pltpu.CompilerParams(has_side_effects=EFFECT),
    )(*srcs_thru, *lands_thru, sems[0], sems[1], sems[2], after)
    return list(res[n:])


def _adamw(recv, w, m, v, layer=0, prev=None):
    L, R, C = w.shape
    tr = _tile(R, max(8, (1 << 18) // C), 8) if R % 8 == 0 else R
    bc1 = 1.0 - ADAM_B1 ** ADAM_STEP
    bc2 = 1.0 - ADAM_B2 ** ADAM_STEP
    if prev is None:
        prev = [lax.empty((L, R, C), F32) for _ in range(4)]

    def body(r_ref, w_ref, m_ref, v_ref, p0, p1, p2, p3, g_ref, d_ref, nm_ref, nv_ref):
        g = r_ref[0].astype(F32)
        for s in range(1, N_DEV):
            g = g + r_ref[s].astype(F32)
        nm = ADAM_B1 * m_ref[...] + (1.0 - ADAM_B1) * g
        nv = ADAM_B2 * v_ref[...] + (1.0 - ADAM_B2) * (g * g)
        mh = nm / bc1
        vh = nv / bc2
        g_ref[...] = g
        d_ref[...] = -ADAM_LR * (mh / (jnp.sqrt(vh) + ADAM_EPS) + ADAM_WD * w_ref[...])
        nm_ref[...] = nm
        nv_ref[...] = nv

    row = pl.BlockSpec((None, tr, C), lambda i: (layer, i, 0))
    return pl.pallas_call(
        body, grid=(R // tr,),
        in_specs=[pl.BlockSpec((N_DEV, tr, C), lambda i: (0, i, 0)), row, row, row] + [ANY] * 4,
        out_specs=[row] * 4, out_shape=[jax.ShapeDtypeStruct((L, R, C), F32)] * 4,
        input_output_aliases={4: 0, 5: 1, 6: 2, 7: 3},
        compiler_params=_cp("parallel"), name="adamw")(recv, w, m, v, *prev)


def _full_w_spec(tk, tn):
    return pl.BlockSpec((tk, tn), lambda i, j, k: (k, j))


def kernel(x, meta_tokens, mix_norm_g, mlp_norm_g, final_norm_g, ev_w_in, ev_conv_w, ev_conv_b, ev_ln_g, ev_ln_b, ev_pool_w, ev_pool_b, ev_pool_scale, ev_w_out, od_w_in, od_gnorm_g, od_w_out, lb_param, mlp_w1, mlp_w2, loss_target, m_meta_tokens, m_mix_norm_g, m_mlp_norm_g, m_final_norm_g, m_ev_w_in, m_ev_conv_w, m_ev_conv_b, m_ev_ln_g, m_ev_ln_b, m_ev_pool_w, m_ev_pool_b, m_ev_pool_scale, m_ev_w_out, m_od_w_in, m_od_gnorm_g, m_od_w_out, m_lb_param, m_mlp_w1, m_mlp_w2, v_meta_tokens, v_mix_norm_g, v_mlp_norm_g, v_final_norm_g, v_ev_w_in, v_ev_conv_w, v_ev_conv_b, v_ev_ln_g, v_ev_ln_b, v_ev_pool_w, v_ev_pool_b, v_ev_pool_scale, v_ev_w_out, v_od_w_in, v_od_gnorm_g, v_od_w_out, v_lb_param, v_mlp_w1, v_mlp_w2):
    S, D = x.shape[1], x.shape[2]
    T = PAD + N_META + S
    DEPTH = mix_norm_g.shape[0]
    DFF = mlp_w1.shape[2] * N_DEV
    dev = 4 * lax.axis_index("x") + 2 * lax.axis_index("y") + lax.axis_index("c")

    g_meta, g_cw = _all_gather([meta_tokens[None], ev_conv_w])
    n_ev = ev_w_in.shape[0]
    n_od = od_w_in.shape[0]
    meta_full = jnp.transpose(g_meta[0], (1, 0, 2)).reshape(N_META, D)
    cw_full = jnp.transpose(g_cw, (0, 2, 1, 3)).reshape(n_ev, CONV_WIDTH, -1)
    cw_pad = jnp.pad(cw_full, ((0, 0), (0, 32 - CONV_WIDTH), (0, 0)))
    n_in_od = od_w_in.shape[2]
    n_w1 = mlp_w1.shape[2]

    ag_src, ag_grp, ag_axis, ag_at = [], [], [], {}
    for layer in range(DEPTH):
        j = layer // 2
        mixer = [("in", ev_w_in[j]), ("out", ev_w_out[j])] if layer % 2 == 0 else [("in", od_w_in[j]), ("out", od_w_out[j])]
        for pos, (key, arr) in enumerate(mixer + [("w1", mlp_w1[layer]), ("w2", mlp_w2[layer])]):
            ag_at[layer, key] = len(ag_src)
            ag_src.append(arr.astype(BF16))
            ag_grp.append(len(ag_grp))
            ag_axis.append(1 if key in ("in", "w1") and arr.shape[1] % 128 == 0 else 0)
    ag_src, g_meta, g_cw = lax.optimization_barrier((ag_src, g_meta, g_cw))
    ag_sems, ag_s, ag_l, ag_tok = _push_start(
        "ag_start", ag_src, [_landing(s_.shape, s_.dtype, ax) for s_, ax in zip(ag_src, ag_axis)], True, ag_grp)

    def ag_wait(layer, key, after):
        a = ag_at[layer, key]
        return _push_wait(f"ag_wait_{a}", [ag_s[a]], [ag_l[a]], ag_sems[a], after, True)[0]

    h = jnp.concatenate([jnp.zeros((PAD, D), F32), meta_full, x[0]], axis=0) + ag_tok[0, 0]
    tgt = jnp.pad(loss_target[0], ((PAD + N_META, 0), (0, 0)))
    lb_all = _lb_fwd(lb_param)
    h, tgt, lb_all = lax.optimization_barrier((h, tgt, lb_all))

    tm_big = _tile(T, MM_ROWS_BIG, 16)
    tm_mid = _tile(T, MM_ROWS_MID, 16)
    tm_k4 = _tile(T, MM_ROWS_K4, 16)

    saved = []
    for layer in range(DEPTH):
        j = layer // 2
        sv = {"h0": h}
        g_in = ag_wait(layer, "in", h)
        w_in = g_in if g_in.ndim == 2 else jnp.transpose(g_in, (1, 0, 2)).reshape(D, -1)
        if layer % 2 == 0:
            sv["n"], u = _mm_rms_nn("ev_in", h, mix_norm_g[layer][None], w_in, tm_big, 512, "f32")
            yab, yc = _ev_fwd(u, cw_pad[j], ev_conv_b[j][None], ev_ln_g[j][None], ev_ln_b[j][None],
                              ev_pool_w[j].astype(BF16), ev_pool_b[j].reshape(1, -1), ev_pool_scale[j][None])
            sv.update(u=u, y=yab, yc=yc)
            w_out = ag_wait(layer, "out", yab).reshape(-1, D)
            h = _mm_nn("ev_out", yab, w_out, _full_w_spec, T, D, D, tm_mid, D, D, "resid", extra=h)
        else:
            sv["n"], u = _mm_rms_nn("od_in", h, mix_norm_g[layer][None], w_in, tm_big, 512, "f32")
            y, o, s0 = _hgrn_fwd(u, lb_all[layer][None], od_gnorm_g[j][None])
            sv.update(u=u, y=y, o=o, s0=s0)
            w_out = ag_wait(layer, "out", y).reshape(-1, D)
            h = _mm_nn("od_out", y, w_out, _full_w_spec, T, D, D, tm_mid, D, D, "resid", extra=h)
        sv["h1"] = h
        w_w1 = ag_wait(layer, "w1", h)
        n2, r, act = _mm_rms_nn("mlp_w1", h, mlp_norm_g[layer][None], w_w1, tm_big, 512, "relu2")
        w_w2 = ag_wait(layer, "w2", act).reshape(DFF, D)
        sv.update(w_in=w_in, w_out=w_out, w_w1=w_w1, w_w2=w_w2)
        sv.update(n2=n2, r=r, act=act)
        h = _mm_nn("mlp_w2", act, w_w2, _full_w_spec, T, D, DFF, tm_k4, D, DFF, "resid", extra=h)
        saved.append(sv)

    loss_blk, dh, dhb, dg_final = _loss_head(h, final_norm_g[None], tgt)
    loss = lax.psum(loss_blk[0, 0], AXES)

    tt = T
    g_mix, g_mlp = [None] * DEPTH, [None] * DEPTH
    small ={"cw": [None] * n_ev, "vec": [None] * n_ev, "pw": [None] * n_ev, "gn": [None] * n_od}
    dlb_rows = [jnp.zeros((1, D), F32) for _ in range(DEPTH)]

    def xs2(tt_, tk):
        return pl.BlockSpec((tt_, tk), lambda a, b, t: (t, a))

    def ys2(tt_, tn):
        return pl.BlockSpec((tt_, tn), lambda a, b, t: (t, b))

    def os2(tk, tn):
        return pl.BlockSpec((tk, tn), lambda a, b, t: (a, b))

    def os3(tk, tn):
        return pl.BlockSpec((None, tk, tn), lambda a, b, t: (b, a, 0))

    def dy2(tm, tn):
        return pl.BlockSpec((tm, tn), lambda i, jj, k: (i, k))

    def w_rows(tj, tn):
        return pl.BlockSpec((tj, tn), lambda i, jj, k: (jj, k))

    def w_whole(tj, tn):
        return pl.BlockSpec((tj, tn), lambda i, jj, k: (0, 0), pipeline_mode=pl.Buffered(1))

    rs_pending = []

    def rs_start(tag, mats):
        blocks = [m_ if m_.ndim == 3 else m_.reshape(N_DEV, m_.shape[0] // N_DEV, m_.shape[1]) for m_ in mats]
        lands = [_landing(b_.shape[1:], b_.dtype) for b_ in blocks]
        sems, s_thru, l_thru, tok = _push_start(f"rs_start_{tag}", blocks, lands, False, [0] * len(blocks))
        rs_pending.append((tag, s_thru, l_thru, sems[0]))
        return tok[0, 0]

    for layer in reversed(range(DEPTH)):
        j = layer // 2
        sv = saved[layer]
        da1 = _mm_nt("mlp_w2_t", dhb, sv["w_w2"], dy2, w_rows, T, DFF, D, tm_mid, 1024, D, "dact", extra=sv["r"])
        dw2 = _mm_tn("mlp_dw2", sv["act"], dhb, xs2, ys2, os2, (DFF, D), T, DFF, D, tt, 512, D)
        dw1 = _mm_tn("mlp_dw1", sv["n2"], da1, xs2, ys2, os3, (N_DEV, D, n_w1), T, D, DFF, tt, D, n_w1)
        tok = rs_start(f"mlp{layer}", [dw1, dw2])
        dh, dhb, g_mlp[layer] = _mm_nt("mlp_w1_t", da1, sv["w_w1"], dy2, w_whole, T, D, DFF, tm_k4, D, DFF, "rms",
                                       extra=(sv["h1"], mlp_norm_g[layer][None] + tok, dh))
        if layer % 2 == 0:
            dyab = _mm_nt("ev_out_t", dhb, sv["w_out"], dy2, w_rows, T, D, D, tm_mid, D, D, "f32")
            dwout = _mm_tn("ev_dwout", sv["y"], dhb, xs2, ys2, os2, (D, D), T, D, D, tt, 512, D)
            du, small["cw"][j], small["vec"][j], small["pw"][j] = _ev_bwd(
                dyab, sv["yc"], sv["u"], cw_pad[j], ev_ln_g[j][None], ev_ln_b[j][None], ev_pool_w[j].astype(BF16),
                jnp.transpose(ev_pool_w[j], (0, 2, 1)).astype(BF16), ev_pool_b[j].reshape(1, -1),
                ev_pool_scale[j][None])
            nin = du.shape[1]
            dwin = _mm_tn("ev_dwin", sv["n"], du, xs2, ys2, os2, (D, nin), T, D, nin, tt, D, 512)
            dwin = jnp.transpose(dwin.reshape(D, N_DEV, nin // N_DEV), (1, 0, 2))
            tok = rs_start(f"mix{layer}", [dwin, dwout])
            dh, dhb, g_mix[layer] = _mm_nt("ev_in_t", du, sv["w_in"], dy2, w_whole, T, D, nin, tm_k4, D, nin, "rms",
                                           extra=(sv["h0"], mix_norm_g[layer][None] + tok, dh))
        else:
            dy = _mm_nt("od_out_t", dhb, sv["w_out"], dy2, w_rows, T, D, D, tm_mid, D, D, "f32")
            dwout = _mm_tn("od_dwout", sv["y"], dhb, xs2, ys2, os2, (D, D), T, D, D, tt, 512, D)
            du3, dlb_rows[layer], small["gn"][j] = _hgrn_bwd(dy, sv["o"], sv["s0"], sv["u"], lb_all[layer][None],
                                                              od_gnorm_g[j][None])
            per = D // n_in_od

            def du_t(tt_, tn):
                return pl.BlockSpec((None, tt_, tn), lambda a, b, t: (b // per, t, b % per))

            dwin = _mm_tn("od_dwin", sv["n"], du3, xs2, du_t, os3, (N_DEV, D, n_in_od), T, D, 4 * D, tt, D, n_in_od)
            tok = rs_start(f"mix{layer}", [dwin, dwout])
            dh, dhb, g_mix[layer] = _mm_nt(
                "od_in_t", du3, sv["w_in"], lambda tm, tn: pl.BlockSpec((4, tm, tn // 4), lambda i, jj, k: (0, i, 0)),
                w_whole, T, D, 4 * D, tm_k4, D, 4 * D, "rms", extra=(sv["h0"], mix_norm_g[layer][None] + tok, dh),
                parts=4)

    dmeta = dh[PAD:PAD + N_META]
    grad_x = dh[PAD + N_META:][None]
    dlb_param = _lb_bwd(lb_param, jnp.concatenate(dlb_rows, axis=0))

    pieces = [
        ("final", dg_final), ("pad", jnp.zeros((SMALL_F32_ROWS - 1, D), F32)),
        ("meta", dmeta), ("mix", jnp.concatenate(g_mix, 0)), ("mlp", jnp.concatenate(g_mlp, 0)),
        ("cw", jnp.stack([c[:CONV_WIDTH] for c in small["cw"]])), ("cb", jnp.stack([v_[0] for v_ in small["vec"]])),
        ("lng", jnp.stack([v_[1] for v_ in small["vec"]])), ("lnb", jnp.stack([v_[2] for v_ in small["vec"]])),
        ("pw", jnp.stack(small["pw"])), ("pb", jnp.stack([v_[4] for v_ in small["vec"]])),
        ("ps", jnp.stack([v_[3] for v_ in small["vec"]])), ("gn", jnp.stack([jnp.sum(g_, axis=0)[0] for g_ in small["gn"]])),
        ("lb", dlb_param),
    ]
    flat = jnp.concatenate([p.reshape(-1) for _, p in pieces])
    n_small = flat.shape[0]
    rows_small = SMALL_F32_ROWS + -(-(n_small // 1024 + 1 - SMALL_F32_ROWS) // 16) * 16
    flat = jnp.pad(flat, (0, rows_small * 1024 - n_small)).reshape(rows_small, 1024)

    sm_src = [flat[:SMALL_F32_ROWS], flat[SMALL_F32_ROWS:].astype(BF16)]
    sm_sems, sm_s, sm_l, sm_tok = _push_start("small_start", sm_src, [_landing(a_.shape, a_.dtype) for a_ in sm_src],
                                              True, [0, 0])
    recv = {}
    for tag, s_thru, l_thru, sems in rs_pending:
        got = _push_wait(f"rs_wait_{tag}", s_thru, l_thru, sems, sm_tok, False)
        layer = int(tag[3:])
        if tag.startswith("mlp"):
            recv["w1", layer], recv["w2", layer] = got
        else:
            key = "ev" if layer % 2 == 0 else "od"
            recv[key + "_in", layer // 2], recv[key + "_out", layer // 2] = got

    outs = {}
    big = {"ev_in": ("ev_w_in", ev_w_in, m_ev_w_in, v_ev_w_in), "ev_out": ("ev_w_out", ev_w_out, m_ev_w_out, v_ev_w_out),
           "od_in": ("od_w_in", od_w_in, m_od_w_in, v_od_w_in), "od_out": ("od_w_out", od_w_out, m_od_w_out, v_od_w_out),
           "w1": ("mlp_w1", mlp_w1, m_mlp_w1, v_mlp_w1), "w2": ("mlp_w2", mlp_w2, m_mlp_w2, v_mlp_w2)}
    for key, (name, w, m, v) in big.items():
        res = None
        for l in range(w.shape[0]):
            res = _adamw(recv[key, l], w, m, v, layer=l, prev=res)
        outs[name] = res

    small_params = {
        "meta": ("meta_tokens", None), "mix": ("mix_norm_g", mix_norm_g, m_mix_norm_g, v_mix_norm_g),
        "mlp": ("mlp_norm_g", mlp_norm_g, m_mlp_norm_g, v_mlp_norm_g),
        "final": ("final_norm_g", final_norm_g, m_final_norm_g, v_final_norm_g),
        "cw": ("ev_conv_w", None), "cb": ("ev_conv_b", ev_conv_b, m_ev_conv_b, v_ev_conv_b),
        "lng": ("ev_ln_g", ev_ln_g, m_ev_ln_g, v_ev_ln_g), "lnb": ("ev_ln_b", ev_ln_b, m_ev_ln_b, v_ev_ln_b),
        "pw": ("ev_pool_w", ev_pool_w, m_ev_pool_w, v_ev_pool_w), "pb": ("ev_pool_b", ev_pool_b, m_ev_pool_b, v_ev_pool_b),
        "ps": ("ev_pool_scale", ev_pool_scale, m_ev_pool_scale, v_ev_pool_scale),
        "gn": ("od_gnorm_g", od_gnorm_g, m_od_gnorm_g, v_od_gnorm_g), "lb": ("lb_param", lb_param, m_lb_param, v_lb_param),
    }
    csh = ev_conv_w.shape[2]
    msh = meta_tokens.shape[1]

    def packed(which):
        parts = []
        for key, g_ in pieces:
            ent = small_params.get(key)
            if key == "pad":
                full = g_
            elif key == "meta":
                src = (meta_tokens, m_meta_tokens, v_meta_tokens)[which]
                full = lax.dynamic_update_slice(jnp.zeros((N_META, D), F32), src, (0, dev * msh))
            elif key == "cw":
                src = (ev_conv_w, m_ev_conv_w, v_ev_conv_w)[which]
                full = lax.dynamic_update_slice(jnp.zeros(g_.shape, F32), src, (0, 0, dev * csh))
            else:
                full = ent[1 + which]
            parts.append(full.reshape(-1))
        f = jnp.concatenate(parts)
        return jnp.pad(f, (0, rows_small * 1024 - n_small)).reshape(rows_small, 1024)

    got_f32, got_bf16 = _push_wait("small_wait", sm_s, sm_l, sm_sems[0], outs["mlp_w2"][0], True)
    recv_small = jnp.concatenate([got_f32, got_bf16.astype(F32)], axis=1)
    sres = [r_[0] for r_ in _adamw(recv_small, packed(0)[None], packed(1)[None], packed(2)[None])]
    off = 0
    for key, g_ in pieces:
        size = g_.size
        vals = [r_.reshape(-1)[off:off + size].reshape(g_.shape) for r_ in sres]
        off += size
        if key == "pad":
            continue
        name = small_params[key][0]
        if key == "meta":
            vals = [lax.dynamic_slice(v_, (0, dev * msh), (N_META, msh)) for v_ in vals]
        elif key == "cw":
            vals = [lax.dynamic_slice(v_, (0, 0, dev * csh), v_.shape[:2] + (csh,)) for v_ in vals]
        else:
            vals = [v_.reshape(small_params[key][1].shape) for v_ in vals]
        outs[name] = vals

    names = ["meta_tokens", "mix_norm_g", "mlp_norm_g", "final_norm_g", "ev_w_in", "ev_conv_w", "ev_conv_b", "ev_ln_g",
             "ev_ln_b", "ev_pool_w", "ev_pool_b", "ev_pool_scale", "ev_w_out", "od_w_in", "od_gnorm_g", "od_w_out",
             "lb_param", "mlp_w1", "mlp_w2"]
    result = [loss, grad_x]
    for k in range(4):
        result += [outs[nm][k] for nm in names]
    return tuple(result)
```

```python
import functools

import jax
import jax.numpy as jnp
from jax import lax
from jax.experimental import pallas as pl
from jax.experimental.pallas import tpu as pltpu

F32 = jnp.float32
BF16 = jnp.bfloat16

N_DEV = 8
N_META = 16
CHUNK = 64
PAD = CHUNK - N_META
SUB = 16
HEAD = 128
CONV_WIDTH = 31
HALO = 32
POOL_WINDOWS = (2, 4, 8, 16)
EPS = 1e-6
NEG = -1e30
ADAM_LR, ADAM_B1, ADAM_B2, ADAM_EPS, ADAM_WD, ADAM_STEP = 0.001, 0.9, 0.999, 1e-08, 0.01, 10
VMEM_LIMIT = 56 * 1024 * 1024
EV_ROWS = 416
HGRN_ROWS = 832
MM_ROWS_BIG = 2080
MM_ROWS_MID = 1040
MM_ROWS_K4 = 416
SMALL_F32_ROWS = 8
HGRN_HEADS_FWD = 8
HGRN_HEADS_BWD = 2
MESH = pl.DeviceIdType.MESH
AXES = ("x", "y", "c")
ANY = pl.BlockSpec(memory_space=pl.ANY)


def _cp(*sem):
    return pltpu.CompilerParams(dimension_semantics=sem, vmem_limit_bytes=VMEM_LIMIT)


def _tile(n, cap, mult):
    best = None
    for d in range(mult, min(n, cap) + 1, mult):
        if n % d == 0:
            best = d
    assert best is not None, (n, cap, mult)
    return best


def _nt(a, b):
    return lax.dot_general(a, b, (((1,), (1,)), ((), ())), preferred_element_type=F32)


def _tn(a, b):
    return lax.dot_general(a, b, (((0,), (0,)), ((), ())), preferred_element_type=F32)


def _nn(a, b):
    return jnp.dot(a, b, preferred_element_type=F32)


def _r16(x):
    return x.astype(BF16).astype(F32)


def _row_ids(base, n):
    return base + lax.broadcasted_iota(jnp.int32, (n, 1), 0)


def _dsilu(x, s):
    return s * (1.0 + x * (1.0 - s))


def _loss_head(h, g, tgt):
    T, D = h.shape
    tm = _tile(T, MM_ROWS_MID, 16)
    first_x = PAD + N_META

    def body(h_ref, g_ref, t_ref, loss_ref, dh_ref, dhb_ref, dg_ref):
        i = pl.program_id(0)
        x = h_ref[...]
        r = lax.rsqrt(jnp.mean(x * x, axis=-1, keepdims=True) + EPS)
        xh = x * r
        gv = g_ref[...]
        out = xh * gv
        valid = _row_ids(i * tm, tm) >= first_x
        e = jnp.where(valid, out - t_ref[...], 0.0)
        dout = e * (1.0 / D)
        dxh = dout * gv
        dx = r * (dxh - xh * jnp.mean(dxh * xh, axis=-1, keepdims=True))
        dh_ref[...] = dx
        dhb_ref[...] = dx.astype(BF16)

        @pl.when(i == 0)
        def _():
            dg_ref[...] = jnp.zeros_like(dg_ref)
            loss_ref[...] = jnp.zeros_like(loss_ref)

        dg_ref[...] += jnp.sum(dout * xh, axis=0, keepdims=True)
        loss_ref[...] += 0.5 * jnp.sum(jnp.mean(e * e, axis=-1, keepdims=True))

    row = pl.BlockSpec((tm, D), lambda i: (i, 0))
    vec = pl.BlockSpec((1, D), lambda i: (0, 0))
    return pl.pallas_call(
        body, grid=(T // tm,),
        in_specs=[row, vec, row],
        out_specs=[pl.BlockSpec((8, 128), lambda i: (0, 0)), row, row, vec],
        out_shape=[jax.ShapeDtypeStruct((8, 128), F32), jax.ShapeDtypeStruct((T, D), F32),
                   jax.ShapeDtypeStruct((T, D), BF16), jax.ShapeDtypeStruct((1, D), F32)],
        compiler_params=_cp("arbitrary"), name="loss_head")(h, g, tgt)


def _mm_nn(name, a, w, w_spec, M, N, K, tm, tn, tk, mode, extra=None, a_spec=None):
    nk = K // tk
    if a_spec is None:
        a_spec = pl.BlockSpec((tm, tk), lambda i, j, k: (i, k))
    o_spec = pl.BlockSpec((tm, tn), lambda i, j, k: (i, j))

    def body(*refs):
        if mode == "resid":
            a_ref, w_ref, e_ref = refs[:3]
            outs = refs[3:]
        else:
            a_ref, w_ref = refs[:2]
            outs = refs[2:]
        acc_ref = outs[-1] if nk > 1 else None
        part = _nn(a_ref[...], w_ref[...])

        def finish(acc):
            if mode == "f32":
                outs[0][...] = acc
            elif mode == "relu2":
                r = jnp.maximum(acc, 0.0)
                outs[0][...] = r.astype(BF16)
                outs[1][...] = (r * r).astype(BF16)
            else:
                keep = _row_ids(pl.program_id(0) * tm, tm) >= PAD
                outs[0][...] = jnp.where(keep, e_ref[...] + acc, 0.0)

        if nk == 1:
            finish(part)
        else:
            k = pl.program_id(2)

            @pl.when(k == 0)
            def _():
                acc_ref[...] = part

            @pl.when(k > 0)
            def _():
                acc_ref[...] += part

            @pl.when(k == nk - 1)
            def _():
                finish(acc_ref[...])

    in_specs = [a_spec, w_spec(tk, tn)]
    args = [a, w]
    if mode == "resid":
        in_specs.append(o_spec)
        args.append(extra)
    if mode == "relu2":
        out_specs = [o_spec, o_spec]
        out_shape = [jax.ShapeDtypeStruct((M, N), BF16)] * 2
    else:
        out_specs = [o_spec]
        out_shape = [jax.ShapeDtypeStruct((M, N), F32)]
    scratch = [pltpu.VMEM((tm, tn), F32)] if nk > 1 else []
    res = pl.pallas_call(
        body, grid=(M // tm, N // tn, nk), in_specs=in_specs, out_specs=out_specs, out_shape=out_shape,
        scratch_shapes=scratch, compiler_params=_cp("parallel", "parallel", "arbitrary"), name=name)(*args)
    return res if mode == "relu2" else res[0]


def _mm_rms_nn(name, h, g, w, tm, tn, mode):
    M, K = h.shape
    N = w.shape[1]

    def body(h_ref, g_ref, w_ref, n_ref, *outs):
        @pl.when(pl.program_id(1) == 0)
        def _():
            ch = _tile(tm, 256, 16)

            def chunk(c, carry):
                rows = pl.ds(pl.multiple_of(c * ch, ch), ch)
                x = h_ref[rows, :]
                r = lax.rsqrt(jnp.mean(x * x, axis=-1, keepdims=True) + EPS)
                n_ref[rows, :] = ((x * r) * g_ref[...]).astype(BF16)
                return carry

            lax.fori_loop(0, tm // ch, chunk, 0)

        acc = _nn(n_ref[...], w_ref[...])
        if mode == "f32":
            outs[0][...] = acc
        else:
            r = jnp.maximum(acc, 0.0)
            outs[0][...] = r.astype(BF16)
            outs[1][...] = (r * r).astype(BF16)

    row = pl.BlockSpec((tm, K), lambda i, j: (i, 0))
    o_spec = pl.BlockSpec((tm, tn), lambda i, j: (i, j))
    n_out = 1 if mode == "f32" else 2
    return pl.pallas_call(
        body, grid=(M // tm, N // tn),
        in_specs=[row, pl.BlockSpec((1, K), lambda i, j: (0, 0)), pl.BlockSpec((K, tn), lambda i, j: (0, j))],
        out_specs=[row] + [o_spec] * n_out,
        out_shape=[jax.ShapeDtypeStruct((M, K), BF16)] + [jax.ShapeDtypeStruct((M, N), F32 if mode == "f32" else BF16)] * n_out,
        compiler_params=_cp("parallel", "arbitrary"), name=name)(h, g, w)


def _mm_nt(name, dy, w, dy_spec, w_spec, M, J, N, tm, tj, tn, mode, extra=None, parts=1):
    nk = N // tn
    o_spec = pl.BlockSpec((tm, tj), lambda i, j, k: (i, j))
    n_extra = {"f32": 0, "dact": 1, "rms": 3}[mode]
    if mode == "rms":
        assert nk == 1 and tj == J

    def body(*refs):
        dy_ref, w_ref = refs[:2]
        ex = refs[2:2 + n_extra]
        outs = refs[2 + n_extra:]
        acc_ref = outs[-1] if nk > 1 else None
        if parts == 1:
            part = _nt(dy_ref[...], w_ref[...])
        else:
            wq = tn // parts
            part = _nt(dy_ref[0], w_ref[:, 0:wq])
            for q in range(1, parts):
                part = part + _nt(dy_ref[q], w_ref[:, q * wq:(q + 1) * wq])

        def finish(acc):
            if mode == "f32":
                outs[0][...] = acc
            elif mode == "dact":
                outs[0][...] = (acc * (2.0 * ex[0][...].astype(F32))).astype(BF16)
            else:
                h_ref, g_ref, dres_ref = ex
                dh_ref, dhb_ref, dg_ref = outs[:3]
                i = pl.program_id(0)

                @pl.when(i == 0)
                def _():
                    dg_ref[...] = jnp.zeros_like(dg_ref)

                ch = _tile(tm, 256, 16)
                for c0 in range(0, tm, ch):
                    a_c = acc[c0:c0 + ch]
                    x = h_ref[c0:c0 + ch, :]
                    r = lax.rsqrt(jnp.mean(x * x, axis=-1, keepdims=True) + EPS)
                    xh = x * r
                    dxh = a_c * g_ref[...]
                    dx = r * (dxh - xh * jnp.mean(dxh * xh, axis=-1, keepdims=True))
                    keep = _row_ids(i * tm + c0, ch) >= PAD
                    dh = jnp.where(keep, dres_ref[c0:c0 + ch, :] + dx, 0.0)
                    dh_ref[c0:c0 + ch, :] = dh
                    dhb_ref[c0:c0 + ch, :] = dh.astype(BF16)
                    dg_ref[...] += jnp.sum(a_c * xh, axis=0, keepdims=True)

        if nk == 1:
            finish(part)
        else:
            k = pl.program_id(2)

            @pl.when(k == 0)
            def _():
                acc_ref[...] = part

            @pl.when(k > 0)
            def _():
                acc_ref[...] += part

            @pl.when(k == nk - 1)
            def _():
                finish(acc_ref[...])

    in_specs = [dy_spec(tm, tn), w_spec(tj, tn)]
    args = [dy, w]
    scratch = [pltpu.VMEM((tm, tj), F32)] if nk > 1 else []
    if mode == "rms":
        vec = pl.BlockSpec((1, J), lambda i, j, k: (0, 0))
        h, g, dres = extra
        res = pl.pallas_call(
            body, grid=(M // tm, 1, 1), in_specs=in_specs + [o_spec, vec, o_spec], out_specs=[o_spec, o_spec, vec],
            out_shape=[jax.ShapeDtypeStruct((M, J), F32), jax.ShapeDtypeStruct((M, J), BF16),
                       jax.ShapeDtypeStruct((1, J), F32)],
            compiler_params=_cp("arbitrary", "arbitrary", "arbitrary"), name=name)(*args, h, g, dres)
        return res
    if mode == "dact":
        in_specs.append(o_spec)
        args.append(extra)
    return pl.pallas_call(
        body, grid=(M // tm, J // tj, nk), in_specs=in_specs, out_specs=[o_spec],
        out_shape=[jax.ShapeDtypeStruct((M, J), BF16 if mode == "dact" else F32)],
        scratch_shapes=scratch, compiler_params=_cp("parallel", "parallel", "arbitrary"), name=name)(*args)[0]


def _mm_tn(name, x, dy, x_spec, dy_spec, o_spec, o_shape, T, K, N, tt, tk, tn):
    nt = T // tt

    def body(x_ref, dy_ref, o_ref, *acc):
        part = _tn(x_ref[...], dy_ref[...])
        if nt == 1:
            o_ref[...] = part.astype(BF16)
            return
        acc_ref = acc[0]
        t = pl.program_id(2)

        @pl.when(t == 0)
        def _():
            acc_ref[...] = part

        @pl.when(t > 0)
        def _():
            acc_ref[...] += part

        @pl.when(t == nt - 1)
        def _():
            o_ref[...] = acc_ref[...].astype(BF16)

    return pl.pallas_call(
        body, grid=(K // tk, N // tn, nt), in_specs=[x_spec(tt, tk), dy_spec(tt, tn)], out_specs=o_spec(tk, tn),
        out_shape=jax.ShapeDtypeStruct(o_shape, BF16), scratch_shapes=[pltpu.VMEM((tk, tn), F32)] if nt > 1 else [],
        compiler_params=_cp("parallel", "parallel", "arbitrary"), name=name)(x, dy)


def _pool_counts(base, n, w):
    pos = _row_ids(base, n) - PAD
    return jnp.clip(pos + 1, 1, w).astype(F32)


def _shifted_copies(buf, rows):
    buf[0, rows:rows + 8, :] = jnp.zeros((8, buf.shape[2]), F32)

    def blk(s, carry):
        b = pl.multiple_of(s * HALO, HALO)
        win = buf[0, pl.ds(b, HALO + 8), :]
        for r in range(1, 8):
            buf[r, pl.ds(b, HALO), :] = win[r:r + HALO]
        return carry

    lax.fori_loop(0, rows // HALO, blk, 0)


def _ev_fwd(u, cw, cb, lg, lb, pw, pb, ps):
    T = u.shape[0]
    C = 512
    tm = _tile(T, EV_ROWS, HALO)
    nsub = tm // HALO
    hb = tm // HALO

    def body(val_ref, gate_ref, pin_ref, valh_ref, gateh_ref, pinh_ref, cw_ref, cb_ref, lg_ref, lb_ref, pw_ref,
             pb_ref, ps_ref, yab_ref, yc_ref, a_ext, p_ext, d_buf):
        i = pl.program_id(0)
        nf = (i > 0).astype(F32)
        a_ext[0, 0:HALO, :] = valh_ref[...] * jax.nn.sigmoid(gateh_ref[...]) * nf
        a_ext[0, HALO:HALO + tm, :] = val_ref[...] * jax.nn.sigmoid(gate_ref[...])
        p_ext[0:HALO, :] = pinh_ref[...] * nf
        p_ext[HALO:, :] = pin_ref[...]
        _shifted_copies(a_ext, tm + HALO)

        def sub(s, carry):
            base = pl.multiple_of(s * HALO, HALO)
            acc = jnp.zeros((HALO, C), F32) + cb_ref[...]
            for j in range(CONV_WIDTH):
                off = 2 + j
                acc = acc + cw_ref[pl.ds(j, 1), :] * a_ext[off % 8, pl.ds(pl.multiple_of(base + off // 8 * 8, 8), HALO), :]
            yc_ref[pl.ds(base, HALO), :] = acc
            mu = jnp.mean(acc, axis=-1, keepdims=True)
            yc = acc - mu
            rstd = lax.rsqrt(jnp.mean(yc * yc, axis=-1, keepdims=True) + EPS)
            z = (yc * rstd) * lg_ref[...] + lb_ref[...]
            yab_ref[pl.ds(base, HALO), 0:C] = (z * jax.nn.sigmoid(z)).astype(BF16)
            pwin = p_ext[pl.ds(base, 2 * HALO), :]
            for gi, w in enumerate(POOL_WINDOWS):
                lo, hi = gi * HEAD, (gi + 1) * HEAD
                x = pwin[HALO:, lo:hi]
                tot = x
                for k in range(1, w):
                    tot = tot + pwin[HALO - k:2 * HALO - k, lo:hi]
                cnt = _pool_counts(i * tm + base, HALO, w)
                d_buf[pl.ds(base, HALO), lo:hi] = (tot / cnt - x).astype(BF16)
            return carry

        lax.fori_loop(0, nsub, sub, 0, unroll=2)
        for gi in range(len(POOL_WINDOWS)):
            lo, hi = gi * HEAD, (gi + 1) * HEAD
            y = _nn(d_buf[:, lo:hi], pw_ref[gi]) + pb_ref[:, lo:hi]
            yab_ref[:, C + lo:C + hi] = (y * ps_ref[:, lo:hi]).astype(BF16)

    def main(c):
        return pl.BlockSpec((tm, C), lambda i: (i, c))

    def halo(c):
        return pl.BlockSpec((HALO, C), lambda i: (jnp.maximum(i * hb - 1, 0), c))

    vec = pl.BlockSpec((1, C), lambda i: (0, 0))
    return pl.pallas_call(
        body, grid=(T // tm,),
        in_specs=[main(0), main(1), main(2), halo(0), halo(1), halo(2),
                  pl.BlockSpec((32, C), lambda i: (0, 0)), vec, vec, vec,
                  pl.BlockSpec((4, HEAD, HEAD), lambda i: (0, 0, 0)), vec, vec],
        out_specs=[pl.BlockSpec((tm, 2 * C), lambda i: (i, 0)), pl.BlockSpec((tm, C), lambda i: (i, 0))],
        out_shape=[jax.ShapeDtypeStruct((T, 2 * C), BF16), jax.ShapeDtypeStruct((T, C), F32)],
        scratch_shapes=[pltpu.VMEM((8, tm + HALO + 8, C), F32), pltpu.VMEM((tm + HALO, C), F32),
                        pltpu.VMEM((tm, C), BF16)],
        compiler_params=_cp("parallel"), name="ev_fwd")(u, u, u, u, u, u, cw, cb, lg, lb, pw, pb, ps)


def _ev_bwd(dyab, yc, u, cw, lg, lb, pw, pwt, pb, ps):
    T = u.shape[0]
    C = 512
    tm = _tile(T, EV_ROWS, HALO)
    nsub = tm // HALO
    hb = tm // HALO
    nblk = T // tm
    E = tm + HALO

    def body(dya_ref, dyb_ref, dyah_ref, dybh_ref, yc_ref, ych_ref, val_ref, gate_ref, pin_ref, valh_ref, gateh_ref,
             pinh_ref, cw_ref, lg_ref, lb_ref, pw_ref, pwt_ref, pb_ref, ps_ref,
             du_ref, dcw_ref, dvec_ref, dpw_ref,
             dy_ext, a_ext, p_ext, ddc_ext, dd_buf, d_buf, dpre_buf, dcw_acc, vec_acc):
        i = pl.program_id(0)
        nf = (i > 0).astype(F32)
        nl = (i < nblk - 1).astype(F32)

        @pl.when(i == 0)
        def _():
            dcw_ref[...] = jnp.zeros_like(dcw_ref)
            dvec_ref[...] = jnp.zeros_like(dvec_ref)
            dpw_ref[...] = jnp.zeros_like(dpw_ref)

        dcw_acc[...] = jnp.zeros_like(dcw_acc)
        vec_acc[...] = jnp.zeros_like(vec_acc)
        a_ext[0, 0:HALO, :] = valh_ref[...] * jax.nn.sigmoid(gateh_ref[...]) * nf
        a_ext[0, HALO:E, :] = val_ref[...] * jax.nn.sigmoid(gate_ref[...])
        p_ext[0:HALO, :] = pinh_ref[...] * nf
        p_ext[HALO:, :] = pin_ref[...]
        _shifted_copies(a_ext, E)

        def ln_bwd(y, dya, main):
            mu = jnp.mean(y, axis=-1, keepdims=True)
            ycen = y - mu
            rstd = lax.rsqrt(jnp.mean(ycen * ycen, axis=-1, keepdims=True) + EPS)
            yh = ycen * rstd
            z = yh * lg_ref[...] + lb_ref[...]
            sz = jax.nn.sigmoid(z)
            dz = dya * _dsilu(z, sz)
            dyh = dz * lg_ref[...]
            dy = rstd * (dyh - jnp.mean(dyh, axis=-1, keepdims=True) - yh * jnp.mean(dyh * yh, axis=-1, keepdims=True))
            if main:
                vec_acc[1] += jnp.sum((dz * yh).reshape(HALO // 8, 8, C), axis=0)
                vec_acc[2] += jnp.sum(dz.reshape(HALO // 8, 8, C), axis=0)
                vec_acc[0] += jnp.sum(dy.reshape(HALO // 8, 8, C), axis=0)
            return dy

        def pool_dd(dyb, base, main):
            dpre = dyb * ps_ref[...]
            for gi, w in enumerate(POOL_WINDOWS):
                lo, hi = gi * HEAD, (gi + 1) * HEAD
                dd = _nn(dpre[:, lo:hi].astype(BF16), pwt_ref[gi])
                cnt = _pool_counts(i * tm + base, HALO, w)
                ddc_ext[pl.ds(base, HALO), lo:hi] = dd / cnt
                if main:
                    dd_buf[pl.ds(base, HALO), lo:hi] = dd
            if main:
                dpre_buf[pl.ds(base, HALO), :] = dpre.astype(BF16)
                vec_acc[4] += jnp.sum(dpre.reshape(HALO // 8, 8, C), axis=0)

        def p1(s, carry):
            base = pl.multiple_of(s * HALO, HALO)
            dy_ext[0, pl.ds(base, HALO), :] = ln_bwd(yc_ref[pl.ds(base, HALO), :], dya_ref[pl.ds(base, HALO), :], True)
            pool_dd(dyb_ref[pl.ds(base, HALO), :], base, True)
            return carry

        lax.fori_loop(0, nsub, p1, 0, unroll=2)
        dy_ext[0, tm:E, :] = ln_bwd(ych_ref[...], dyah_ref[...], False) * nl
        _shifted_copies(dy_ext, E)
        dpre_h = dybh_ref[...] * ps_ref[...] * nl
        for gi, w in enumerate(POOL_WINDOWS):
            lo, hi = gi * HEAD, (gi + 1) * HEAD
            dd = _nn(dpre_h[:, lo:hi].astype(BF16), pwt_ref[gi])
            ddc_ext[tm:, lo:hi] = dd / _pool_counts(i * tm + tm, HALO, w)

        def p2(s, carry):
            base = pl.multiple_of(s * HALO, HALO)
            dy_m = dy_ext[0, pl.ds(base, HALO), :]
            da = jnp.zeros((HALO, C), F32)
            for j in range(CONV_WIDTH):
                sh = CONV_WIDTH - 1 - j
                off = 2 + j
                da = da + cw_ref[pl.ds(j, 1), :] * dy_ext[sh % 8, pl.ds(pl.multiple_of(base + sh // 8 * 8, 8), HALO), :]
                a_j = a_ext[off % 8, pl.ds(pl.multiple_of(base + off // 8 * 8, 8), HALO), :]
                dcw_acc[j] += jnp.sum((dy_m * a_j).reshape(HALO // 8, 8, C), axis=0)
            v = val_ref[pl.ds(base, HALO), :]
            g = gate_ref[pl.ds(base, HALO), :]
            sg = jax.nn.sigmoid(g)
            du_ref[pl.ds(base, HALO), 0:C] = (da * sg).astype(BF16)
            du_ref[pl.ds(base, HALO), C:2 * C] = (da * v * sg * (1.0 - sg)).astype(BF16)
            pwin = p_ext[pl.ds(base, 2 * HALO), :]
            cwin = ddc_ext[pl.ds(base, 2 * HALO), :]
            for gi, w in enumerate(POOL_WINDOWS):
                lo, hi = gi * HEAD, (gi + 1) * HEAD
                x = pwin[HALO:, lo:hi]
                tot = x
                back = cwin[0:HALO, lo:hi]
                for k in range(1, w):
                    tot = tot + pwin[HALO - k:2 * HALO - k, lo:hi]
                    back = back + cwin[k:k + HALO, lo:hi]
                cnt = _pool_counts(i * tm + base, HALO, w)
                d_buf[pl.ds(base, HALO), lo:hi] = (tot / cnt - x).astype(BF16)
                du_ref[pl.ds(base, HALO), 2 * C + lo:2 * C + hi] = (back - dd_buf[pl.ds(base, HALO), lo:hi]).astype(BF16)
            return carry

        lax.fori_loop(0, nsub, p2, 0)
        for gi in range(len(POOL_WINDOWS)):
            lo, hi = gi * HEAD, (gi + 1) * HEAD
            pre = _nn(d_buf[:, lo:hi], pw_ref[gi]) + pb_ref[:, lo:hi]
            vec_acc[3, :, lo:hi] += jnp.sum((dyb_ref[:, lo:hi] * pre).reshape(tm // 8, 8, HEAD), axis=0)
            dpw_ref[gi] += _tn(d_buf[:, lo:hi], dpre_buf[:, lo:hi])
        for j in range(CONV_WIDTH):
            dcw_ref[pl.ds(j, 1), :] += jnp.sum(dcw_acc[j], axis=0, keepdims=True)
        for r in range(5):
            dvec_ref[pl.ds(r, 1), :] += jnp.sum(vec_acc[r], axis=0, keepdims=True)

    def main(c, width=C):
        return pl.BlockSpec((tm, width), lambda i: (i, c))

    def prev(c):
        return pl.BlockSpec((HALO, C), lambda i: (jnp.maximum(i * hb - 1, 0), c))

    def nxt(c):
        return pl.BlockSpec((HALO, C), lambda i: (jnp.minimum((i + 1) * hb, T // HALO - 1), c))

    vec = pl.BlockSpec((1, C), lambda i: (0, 0))
    mat = pl.BlockSpec((4, HEAD, HEAD), lambda i: (0, 0, 0))
    return pl.pallas_call(
        body, grid=(nblk,),
        in_specs=[main(0), main(1), nxt(0), nxt(1), main(0), nxt(0), main(0), main(1), main(2), prev(0), prev(1),
                  prev(2), pl.BlockSpec((32, C), lambda i: (0, 0)), vec, vec, mat, mat, vec, vec],
        out_specs=[pl.BlockSpec((tm, 3 * C), lambda i: (i, 0)), pl.BlockSpec((32, C), lambda i: (0, 0)),
                   pl.BlockSpec((8, C), lambda i: (0, 0)), mat],
        out_shape=[jax.ShapeDtypeStruct((T, 3 * C), BF16), jax.ShapeDtypeStruct((32, C), F32),
                   jax.ShapeDtypeStruct((8, C), F32), jax.ShapeDtypeStruct((4, HEAD, HEAD), F32)],
        scratch_shapes=[pltpu.VMEM((8, E + 8, C), F32), pltpu.VMEM((8, E + 8, C), F32), pltpu.VMEM((E, C), F32),
                        pltpu.VMEM((E, C), F32), pltpu.VMEM((tm, C), F32), pltpu.VMEM((tm, C), BF16),
                        pltpu.VMEM((tm, C), BF16), pltpu.VMEM((32, 8, C), F32), pltpu.VMEM((8, 8, C), F32)],
        compiler_params=_cp("arbitrary"), name="ev_bwd")(
            dyab, dyab, dyab, dyab, yc, yc, u, u, u, u, u, u, cw, lg, lb, pw, pwt, pb, ps)


def _cumsum_rows(x, reverse=False):
    n = x.shape[0]
    rid = lax.broadcasted_iota(jnp.int32, (n, 1), 0)
    k = 1
    while k < n:
        if reverse:
            sh = jnp.where(rid < n - k, pltpu.roll(x, n - k, 0), 0.0)
        else:
            sh = jnp.where(rid >= k, pltpu.roll(x, k, 0), 0.0)
        x = x + sh
        k *= 2
    return x


def _hgrn_gates(qr, fr, lbv):
    sq = jax.nn.sigmoid(qr)
    sg = jax.nn.sigmoid(fr)
    fg = lbv + (1.0 - lbv) * sg
    return qr * sq, sq, sg, fg, 1.0 - fg, jnp.log(fg)


def _hgrn_fwd(u, lbv, gn):
    T = u.shape[0]
    H = 8
    RB = _tile(T, HGRN_ROWS, CHUNK)
    NC = RB // CHUNK
    NS = CHUNK // SUB

    HP = HGRN_HEADS_FWD
    W = HP * HEAD

    def body(q_ref, f_ref, i_ref, g_ref, lb_ref, gn_ref, y_ref, o_ref, s0_ref, st, qs, ks, bs, vs, os_):
        rb = pl.program_id(1)

        @pl.when(rb == 0)
        def _():
            st[...] = jnp.zeros_like(st)

        t8 = lax.broadcasted_iota(jnp.int32, (8, 1), 0)

        def head(hh, c, rows):
            sl = slice(hh * HEAD, (hh + 1) * HEAD)
            q, _, _, _, kk, lf = _hgrn_gates(q_ref[rows, sl], f_ref[rows, sl], lb_ref[:, sl])
            v = i_ref[rows, sl]
            b = _cumsum_rows(lf)
            qs[hh] = q
            ks[hh] = kk
            bs[hh] = b
            vs[hh] = v
            st0 = st[hh]
            s0_ref[hh, c] = st0
            os_[hh] = _nt((q * jnp.exp(b)).astype(BF16), st0.astype(BF16))
            for I in range(NS):
                lo = I * SUB
                qI = qs[hh, lo:lo + SUB, :]
                bI = bs[hh, lo:lo + SUB, :]
                oI = jnp.zeros((SUB, HEAD), F32)
                if I > 0:
                    bprev = bs[hh, pl.ds(lo - 1, 1), :]
                    qt = _r16(qI * jnp.exp(bI - bprev))
                    kt = _r16(ks[hh, 0:lo, :] * jnp.exp(bprev - bs[hh, 0:lo, :]))
                    A = _nt(qt, kt)
                    oI = oI + _nn(_r16(A), _r16(vs[hh, 0:lo, :]))
                od = [jnp.zeros((8, HEAD), F32) for _ in range(SUB // 8)]
                for s in range(SUB):
                    row = pl.ds(lo + s, 1)
                    brow, krow, vrow = bs[hh, row, :], ks[hh, row, :], vs[hh, row, :]
                    for ti in range(SUB // 8):
                        o8 = 8 * ti
                        if s > o8 + 7:
                            continue
                        d = bI[o8:o8 + 8] - brow
                        if s > o8:
                            d = jnp.where(t8 >= s - o8, d, NEG)
                        col = jnp.sum(qI[o8:o8 + 8] * jnp.exp(d) * krow, axis=1, keepdims=True)
                        od[ti] = od[ti] + col * vrow
                os_[hh, lo:lo + SUB, :] += oI + jnp.concatenate(od, axis=0)
            blast = bs[hh, pl.ds(CHUNK - 1, 1), :]
            kh = kk * jnp.exp(blast - b)
            st[hh] = st0 * jnp.exp(blast) + _tn(v.astype(BF16), kh.astype(BF16))
            o = os_[hh]
            o_ref[rows, sl] = o
            rr = lax.rsqrt(jnp.mean(o * o, axis=-1, keepdims=True) + EPS)
            gr = g_ref[rows, sl]
            y_ref[rows, sl] = (((o * rr) * gn_ref[...]) * (gr * jax.nn.sigmoid(gr))).astype(BF16)

        def chunk(c, carry):
            rows = pl.ds(pl.multiple_of(c * CHUNK, CHUNK), CHUNK)
            for hh in range(HP):
                head(hh, c, rows)
            return carry

        lax.fori_loop(0, NC, chunk, 0)

    def blk(q):
        return pl.BlockSpec((RB, W), lambda h, r: (r, q * (H // HP) + h))

    sc = lambda: pltpu.VMEM((HP, CHUNK, HEAD), F32)
    return pl.pallas_call(
        body, grid=(H // HP, T // RB),
        in_specs=[blk(0), blk(1), blk(2), blk(3), pl.BlockSpec((1, W), lambda h, r: (0, h)),
                  pl.BlockSpec((1, HEAD), lambda h, r: (0, 0))],
        out_specs=[pl.BlockSpec((RB, W), lambda h, r: (r, h)), pl.BlockSpec((RB, W), lambda h, r: (r, h)),
                   pl.BlockSpec((HP, NC, HEAD, HEAD), lambda h, r: (h, r, 0, 0))],
        out_shape=[jax.ShapeDtypeStruct((T, H * HEAD), BF16), jax.ShapeDtypeStruct((T, H * HEAD), F32),
                   jax.ShapeDtypeStruct((H, T // CHUNK, HEAD, HEAD), F32)],
        scratch_shapes=[pltpu.VMEM((HP, HEAD, HEAD), F32), sc(), sc(), sc(), sc(), sc()],
        compiler_params=_cp("parallel", "arbitrary"), name="hgrn_fwd")(u, u, u, u, lbv, gn)


def _hgrn_bwd(dy, o, s0, u, lbv, gn):
    T = u.shape[0]
    H = 8
    RB = _tile(T, HGRN_ROWS, CHUNK)
    NB = T // RB
    NC = RB // CHUNK
    NS = CHUNK // SUB

    def body(q_ref, f_ref, i_ref, g_ref, lb_ref, gn_ref, o_ref, dy_ref, s0_ref, du_ref, dlb_ref, dgn_ref,
             dst, qs, ks, bs, vs, dos, dqs, dks, dki, dvs, dbs):
        rb = pl.program_id(1)

        @pl.when(rb == 0)
        def _():
            dst[...] = jnp.zeros_like(dst)
            dlb_ref[...] = jnp.zeros_like(dlb_ref)
            dgn_ref[...] = jnp.zeros_like(dgn_ref)

        t8 = lax.broadcasted_iota(jnp.int32, (8, 1), 0)
        lane = lax.broadcasted_iota(jnp.int32, (8, HEAD), 1)
        gnv = gn_ref[...]

        def head(hh, c, rows):
            sl = slice(hh * HEAD, (hh + 1) * HEAD)
            lbv_ = lb_ref[:, sl]
            qr = q_ref[rows, sl]
            q, sq, sg, fg, kk, lf = _hgrn_gates(qr, f_ref[rows, sl], lbv_)
            v = i_ref[rows, sl]
            gr = g_ref[rows, sl]
            b = _cumsum_rows(lf)
            eb = jnp.exp(b)
            ov = o_ref[rows, sl]
            dyv = dy_ref[rows, sl]
            rr = lax.rsqrt(jnp.mean(ov * ov, axis=-1, keepdims=True) + EPS)
            oh = ov * rr
            gs = jax.nn.sigmoid(gr)
            dgr = dyv * (oh * gnv) * _dsilu(gr, gs)
            dnrm = dyv * (gr * gs)
            dgn_ref[hh] += jnp.sum(dnrm * oh, axis=0, keepdims=True)
            t1 = dnrm * gnv
            do = rr * (t1 - oh * jnp.mean(t1 * oh, axis=-1, keepdims=True))
            qs[hh] = q
            ks[hh] = kk
            bs[hh] = b
            vs[hh] = v
            dos[hh] = do
            st0 = s0_ref[hh, c]
            dS = dst[hh]
            do_b = do.astype(BF16)
            blast = bs[hh, pl.ds(CHUNK - 1, 1), :]
            elast = jnp.exp(blast - b)
            dq_inter = _nn(do_b, st0.astype(BF16)) * eb
            dqs[hh] = dq_inter
            dbs[hh] = q * dq_inter
            kh = kk * elast
            dvs[hh] = _nt(kh.astype(BF16), dS.astype(BF16))
            dk_inter = _nn(v.astype(BF16), dS.astype(BF16)) * elast
            dki[hh] = dk_inter
            dks[hh] = jnp.zeros((CHUNK, HEAD), F32)
            for I in range(NS):
                lo = I * SUB
                qI = qs[hh, lo:lo + SUB, :]
                bI = bs[hh, lo:lo + SUB, :]
                doI = dos[hh, lo:lo + SUB, :]
                dqI = jnp.zeros((SUB, HEAD), F32)
                dbI = jnp.zeros((SUB, HEAD), F32)
                if I > 0:
                    bprev = bs[hh, pl.ds(lo - 1, 1), :]
                    eq = jnp.exp(bI - bprev)
                    ek = jnp.exp(bprev - bs[hh, 0:lo, :])
                    qt = _r16(qI * eq)
                    kt = _r16(ks[hh, 0:lo, :] * ek)
                    A = _r16(_nt(qt, kt))
                    doI_b = _r16(doI)
                    dA = _r16(_nt(doI_b, _r16(vs[hh, 0:lo, :])))
                    dvs[hh, 0:lo, :] += _tn(A, doI_b)
                    dqt = _nn(dA, kt)
                    dkt = _tn(dA, qt)
                    dqI = dqI + dqt * eq
                    dbI = dbI + qt.astype(F32) * dqt
                    dks[hh, 0:lo, :] += dkt * ek
                    dbs[hh, 0:lo, :] -= kt.astype(F32) * dkt
                dq_t = [jnp.zeros((8, HEAD), F32) for _ in range(SUB // 8)]
                a_t = [jnp.zeros((8, HEAD), F32) for _ in range(SUB // 8)]
                for s in range(SUB):
                    row = pl.ds(lo + s, 1)
                    brow, krow, vrow = bs[hh, row, :], ks[hh, row, :], vs[hh, row, :]
                    dk_s = None
                    for ti in range(SUB // 8):
                        o8 = 8 * ti
                        if s > o8 + 7:
                            continue
                        d = bI[o8:o8 + 8] - brow
                        if s > o8:
                            d = jnp.where(t8 >= s - o8, d, NEG)
                        Es = jnp.exp(d)
                        qE = qI[o8:o8 + 8] * Es
                        col = jnp.sum(qE * krow, axis=1, keepdims=True)
                        a_t[ti] = jnp.where(lane == s, col, a_t[ti])
                        dcol = jnp.sum(doI[o8:o8 + 8] * vrow, axis=1, keepdims=True)
                        dq_t[ti] = dq_t[ti] + (dcol * Es) * krow
                        part = jnp.sum(dcol * qE, axis=0, keepdims=True)
                        dk_s = part if dk_s is None else dk_s + part
                    dks[hh, row, :] += dk_s
                    dbs[hh, row, :] -= krow * dk_s
                a_d = jnp.concatenate(a_t, axis=0)
                dq_d = jnp.concatenate(dq_t, axis=0)
                dvs[hh, lo:lo + SUB, :] += _tn(a_d, doI)[0:SUB]
                dqI = dqI + dq_d
                dbI = dbI + qI * dq_d
                dqs[hh, lo:lo + SUB, :] += dqI
                dbs[hh, lo:lo + SUB, :] += dbI
            kdk = kk * dki[hh]
            excl = _cumsum_rows(kdk) - kdk
            suff = _cumsum_rows(dbs[hh], reverse=True)
            gdec = jnp.sum(dS * st0, axis=0, keepdims=True) * jnp.exp(blast)
            dlf = suff + excl + gdec
            dk = dks[hh] + dki[hh]
            dfg = dlf / fg - dk
            dlb_ref[:, sl] += jnp.sum(dfg * (1.0 - sg), axis=0, keepdims=True)
            du_ref[0, rows, sl] = (dqs[hh] * _dsilu(qr, sq)).astype(BF16)
            du_ref[1, rows, sl] = (dfg * (1.0 - lbv_) * sg * (1.0 - sg)).astype(BF16)
            du_ref[2, rows, sl] = dvs[hh].astype(BF16)
            du_ref[3, rows, sl] = dgr.astype(BF16)
            dst[hh] = dS * jnp.exp(blast) + _tn(do_b, (q * eb).astype(BF16))

        def chunk(cc, carry):
            c = NC - 1 - cc
            rows = pl.ds(pl.multiple_of(c * CHUNK, CHUNK), CHUNK)
            for hh in range(HP):
                head(hh, c, rows)
            return carry

        lax.fori_loop(0, NC, chunk, 0)

    HP = HGRN_HEADS_BWD
    W = HP * HEAD

    def blk(qd):
        return pl.BlockSpec((RB, W), lambda h, r: (NB - 1 - r, qd * (H // HP) + h))

    hblk = pl.BlockSpec((RB, W), lambda h, r: (NB - 1 - r, h))
    sc = lambda: pltpu.VMEM((HP, CHUNK, HEAD), F32)
    return pl.pallas_call(
        body, grid=(H // HP, NB),
        in_specs=[blk(0), blk(1), blk(2), blk(3), pl.BlockSpec((1, W), lambda h, r: (0, h)),
                  pl.BlockSpec((1, HEAD), lambda h, r: (0, 0)), hblk, hblk,
                  pl.BlockSpec((HP, NC, HEAD, HEAD), lambda h, r: (h, NB - 1 - r, 0, 0))],
        out_specs=[pl.BlockSpec((4, RB, W), lambda h, r: (0, NB - 1 - r, h)),
                   pl.BlockSpec((1, W), lambda h, r: (0, h)), pl.BlockSpec((HP, 1, HEAD), lambda h, r: (h, 0, 0))],
        out_shape=[jax.ShapeDtypeStruct((4, T, H * HEAD), BF16), jax.ShapeDtypeStruct((1, H * HEAD), F32),
                   jax.ShapeDtypeStruct((H, 1, HEAD), F32)],
        scratch_shapes=[pltpu.VMEM((HP, HEAD, HEAD), F32)] + [sc() for _ in range(10)],
        compiler_params=_cp("parallel", "arbitrary"), name="hgrn_bwd")(u, u, u, u, lbv, gn, o, dy, s0)


def _softmax_rows(p_ref, L):
    rows = [p_ref[pl.ds(l, 1), :] for l in range(L)]
    m = rows[0]
    for r in rows[1:]:
        m = jnp.maximum(m, r)
    e = [jnp.exp(r - m) for r in rows]
    tot = e[0]
    for t in e[1:]:
        tot = tot + t
    return [t / tot for t in e]


def _lb_fwd(lbp):
    L, D = lbp.shape

    def body(p_ref, o_ref):
        sm = _softmax_rows(p_ref, L)
        acc = jnp.zeros((1, D), F32)
        o_ref[pl.ds(0, 1), :] = acc
        for l in range(1, L):
            acc = acc + sm[l]
            o_ref[pl.ds(l, 1), :] = acc

    return pl.pallas_call(body, out_shape=jax.ShapeDtypeStruct((L, D), F32), name="lb_fwd")(lbp)


def _lb_bwd(lbp, dlb):
    L, D = lbp.shape

    def body(p_ref, d_ref, o_ref):
        sm = _softmax_rows(p_ref, L)
        dsm = [jnp.zeros((1, D), F32)]
        for i in range(1, L):
            t = jnp.zeros((1, D), F32)
            for l in range(i, L):
                t = t + d_ref[pl.ds(l, 1), :]
            dsm.append(t)
        dot = jnp.zeros((1, D), F32)
        for i in range(L):
            dot = dot + dsm[i] * sm[i]
        for i in range(L):
            o_ref[pl.ds(i, 1), :] = sm[i] * (dsm[i] - dot)

    return pl.pallas_call(body, out_shape=jax.ShapeDtypeStruct((L, D), F32), name="lb_bwd")(lbp, dlb)


def _my_pos():
    return lax.axis_index("x"), lax.axis_index("y"), lax.axis_index("c")


def _peer(mask):
    x, y, c = _my_pos()
    mx, my, mc = (mask >> 2) & 1, (mask >> 1) & 1, mask & 1
    px = (1 - x) if mx else x
    py = (1 - y) if my else y
    pc = (1 - c) if mc else c
    return (px, py, pc), 4 * px + 2 * py + pc


def _all_gather(shards):
    n = len(shards)

    def body(*refs):
        ins, outs = refs[:n], refs[n:2 * n]
        send_sems, recv_sems, local_sems = refs[2 * n:]
        x, y, c = _my_pos()
        me = 4 * x + 2 * y + c
        local = [pltpu.make_async_copy(ins[a], outs[a].at[:, me], local_sems.at[a]) for a in range(n)]
        for cp in local:
            cp.start()
        sends = []
        for m in range(1, N_DEV):
            peer, _ = _peer(m)
            for a in range(n):
                cp = pltpu.make_async_remote_copy(
                    src_ref=ins[a], dst_ref=outs[a].at[:, me], send_sem=send_sems.at[a, m - 1],
                    recv_sem=recv_sems.at[a, m - 1], device_id=peer, device_id_type=MESH)
                cp.start()
                sends.append(cp)
        for m in range(1, N_DEV):
            peer, pid = _peer(m)
            for a in range(n):
                pltpu.make_async_remote_copy(
                    src_ref=ins[a], dst_ref=outs[a].at[:, pid], send_sem=send_sems.at[a, m - 1],
                    recv_sem=recv_sems.at[a, m - 1], device_id=peer, device_id_type=MESH).wait_recv()
        for cp in sends:
            cp.wait_send()
        for cp in local:
            cp.wait()

    out_shape = [jax.ShapeDtypeStruct((s.shape[0], N_DEV) + s.shape[1:], s.dtype) for s in shards]
    return pl.pallas_call(
        body, in_specs=[ANY] * n, out_specs=[ANY] * n, out_shape=out_shape,
        scratch_shapes=[pltpu.SemaphoreType.DMA((n, N_DEV - 1)), pltpu.SemaphoreType.DMA((n, N_DEV - 1)),
                        pltpu.SemaphoreType.DMA((n,))],
        name="all_gather_small")(*shards)


HBM_SPEC =pl.BlockSpec(memory_space=pltpu.HBM)
SEM_SPEC = pl.BlockSpec(memory_space=pltpu.SEMAPHORE)
EFFECT = pltpu.SideEffectType.DATAFLOW_SIDE_EFFECTING


def _hbm(a):
    return pltpu.with_memory_space_constraint(a, pltpu.HBM)


def _landing(block_shape, dtype, axis=0):
    if axis == 0:
        return lax.empty((N_DEV,) + tuple(block_shape), dtype)
    rows, n = block_shape
    return lax.empty((rows, N_DEV * n), dtype)


def _slot(ref, i):
    if len(ref.shape) == 2:
        n = ref.shape[1] // N_DEV
        return ref.at[:, pl.ds(i * n, n)]
    return ref.at[i]


def _push_start(name, srcs, lands, whole, groups):
    n = len(srcs)
    ng = 1 + max(groups)
    cnt = [groups.count(g) for g in range(ng)]
    idx = [groups[:a].count(groups[a]) for a in range(n)]

    def body(*refs):
        src_refs, land_refs = refs[:n], refs[n:2 * n]
        sems = refs[2 * n:2 * n + 3 * ng]
        token = refs[-1]
        x, y, c = _my_pos()
        me = 4 * x + 2 * y + c
        for a in range(n):
            g = groups[a]
            for m in range(1, N_DEV):
                peer, pid = _peer(m)
                pltpu.make_async_remote_copy(
                    src_ref=src_refs[a] if whole else src_refs[a].at[pid], dst_ref=_slot(land_refs[a], me),
                    send_sem=sems[3 * g].at[idx[a] * (N_DEV - 1) + m - 1],
                    recv_sem=sems[3 * g + 1].at[idx[a] * (N_DEV - 1) + m - 1],
                    device_id=peer, device_id_type=MESH).start()
            pltpu.make_async_copy(src_refs[a] if whole else src_refs[a].at[me], _slot(land_refs[a], me),
                                  sems[3 * g + 2].at[idx[a]]).start()
        token[...] = jnp.zeros_like(token)

    sem_shapes = []
    for g in range(ng):
        sem_shapes += [pltpu.SemaphoreType.DMA((cnt[g] * (N_DEV - 1),))] * 2 + [pltpu.SemaphoreType.DMA((cnt[g],))]
    thru = [pltpu.HBM(s.shape, s.dtype) for s in list(srcs) + list(lands)]
    res = pl.pallas_call(
        body, name=name,
        out_shape=tuple(sem_shapes + thru + [jax.ShapeDtypeStruct((8, 128), F32)]),
        in_specs=tuple([HBM_SPEC] * (2 * n)),
        out_specs=tuple([SEM_SPEC] * (3 * ng) + [HBM_SPEC] * (2 * n) + [pl.BlockSpec(memory_space=pltpu.VMEM)]),
        input_output_aliases={i: 3 * ng + i for i in range(2 * n)},
        compiler_params=pltpu.CompilerParams(has_side_effects=EFFECT),
    )(*[_hbm(s) for s in srcs], *[_hbm(z) for z in lands])
    sems = [(res[3 * g], res[3 * g + 1], res[3 * g + 2]) for g in range(ng)]
    srcs_thru = list(res[3 * ng:3 * ng + n])
    lands_thru = list(res[3 * ng + n:3 * ng + 2 * n])
    return sems, srcs_thru, lands_thru, res[-1]


def _push_wait(name, srcs_thru, lands_thru, sems, after, whole):
    n = len(srcs_thru)

    def body(*refs):
        src_refs, land_refs = refs[:n], refs[n:2 * n]
        send_sems, recv_sems, own_sems = refs[2 * n], refs[2 * n + 1], refs[2 * n + 2]
        x, y, c = _my_pos()
        me = 4 * x + 2 * y + c
        for a in range(n):
            pltpu.make_async_copy(src_refs[a] if whole else src_refs[a].at[me], _slot(land_refs[a], me),
                                  own_sems.at[a]).wait()
            for m in range(1, N_DEV):
                peer, pid = _peer(m)
                cp = pltpu.make_async_remote_copy(
                    src_ref=src_refs[a] if whole else src_refs[a].at[pid], dst_ref=_slot(land_refs[a], pid),
                    send_sem=send_sems.at[a * (N_DEV - 1) + m - 1], recv_sem=recv_sems.at[a * (N_DEV - 1) + m - 1],
                    device_id=peer, device_id_type=MESH)
                cp.wait_send()
                cp.wait_recv()

    thru = [pltpu.HBM(s.shape, s.dtype) for s in list(srcs_thru) + list(lands_thru)]
    res = pl.pallas_call(
        body, name=name, out_shape=tuple(thru),
        in_specs=tuple([HBM_SPEC] * (2 * n) + [SEM_SPEC, SEM_SPEC, SEM_SPEC, ANY]),
        out_specs=tuple([HBM_SPEC] * (2 * n)),
        input_output_aliases={i: i for i in range(2 * n)},
        compiler_params=pltpu.CompilerParams(has_side_effects=EFFECT),
    )(*srcs_thru, *lands_thru, sems[0], sems[1], sems[2], after)
    return list(res[n:])


def _adamw(recv, w, m, v, layer=0, prev=None):
    L, R, C = w.shape
    tr = _tile(R, max(8, (1 << 18) // C), 8) if R % 8 == 0 else R
    bc1 = 1.0 - ADAM_B1 ** ADAM_STEP
    bc2 = 1.0 - ADAM_B2 ** ADAM_STEP
    if prev is None:
        prev = [lax.empty((L, R, C), F32) for _ in range(4)]

    def body(r_ref, w_ref, m_ref, v_ref, p0, p1, p2, p3, g_ref, d_ref, nm_ref, nv_ref):
        g = r_ref[0].astype(F32)
        for s in range(1, N_DEV):
            g = g + r_ref[s].astype(F32)
        nm = ADAM_B1 * m_ref[...] + (1.0 - ADAM_B1) * g
        nv = ADAM_B2 * v_ref[...] + (1.0 - ADAM_B2) * (g * g)
        mh = nm / bc1
        vh = nv / bc2
        g_ref[...] = g
        d_ref[...] = -ADAM_LR * (mh / (jnp.sqrt(vh) + ADAM_EPS) + ADAM_WD * w_ref[...])
        nm_ref[...] = nm
        nv_ref[...] = nv

    row = pl.BlockSpec((None, tr, C), lambda i: (layer, i, 0))
    return pl.pallas_call(
        body, grid=(R // tr,),
        in_specs=[pl.BlockSpec((N_DEV, tr, C), lambda i: (0, i, 0)), row, row, row] + [ANY] * 4,
        out_specs=[row] * 4, out_shape=[jax.ShapeDtypeStruct((L, R, C), F32)] * 4,
        input_output_aliases={4: 0, 5: 1, 6: 2, 7: 3},
        compiler_params=_cp("parallel"), name="adamw")(recv, w, m, v, *prev)


def _full_w_spec(tk, tn):
    return pl.BlockSpec((tk, tn), lambda i, j, k: (k, j))


def kernel(x, meta_tokens, mix_norm_g, mlp_norm_g, final_norm_g, ev_w_in, ev_conv_w, ev_conv_b, ev_ln_g, ev_ln_b, ev_pool_w, ev_pool_b, ev_pool_scale, ev_w_out, od_w_in, od_gnorm_g, od_w_out, lb_param, mlp_w1, mlp_w2, loss_target, m_meta_tokens, m_mix_norm_g, m_mlp_norm_g, m_final_norm_g, m_ev_w_in, m_ev_conv_w, m_ev_conv_b, m_ev_ln_g, m_ev_ln_b, m_ev_pool_w, m_ev_pool_b, m_ev_pool_scale, m_ev_w_out, m_od_w_in, m_od_gnorm_g, m_od_w_out, m_lb_param, m_mlp_w1, m_mlp_w2, v_meta_tokens, v_mix_norm_g, v_mlp_norm_g, v_final_norm_g, v_ev_w_in, v_ev_conv_w, v_ev_conv_b, v_ev_ln_g, v_ev_ln_b, v_ev_pool_w, v_ev_pool_b, v_ev_pool_scale, v_ev_w_out, v_od_w_in, v_od_gnorm_g, v_od_w_out, v_lb_param, v_mlp_w1, v_mlp_w2):
    S, D = x.shape[1], x.shape[2]
    T = PAD + N_META + S
    DEPTH = mix_norm_g.shape[0]
    DFF = mlp_w1.shape[2] * N_DEV
    dev = 4 * lax.axis_index("x") + 2 * lax.axis_index("y") + lax.axis_index("c")

    g_meta, g_cw = _all_gather([meta_tokens[None], ev_conv_w])
    n_ev = ev_w_in.shape[0]
    n_od = od_w_in.shape[0]
    meta_full = jnp.transpose(g_meta[0], (1, 0, 2)).reshape(N_META, D)
    cw_full = jnp.transpose(g_cw, (0, 2, 1, 3)).reshape(n_ev, CONV_WIDTH, -1)
    cw_pad = jnp.pad(cw_full, ((0, 0), (0, 32 - CONV_WIDTH), (0, 0)))
    n_in_od = od_w_in.shape[2]
    n_w1 = mlp_w1.shape[2]

    ag_src, ag_grp, ag_axis, ag_at = [], [], [], {}
    for layer in range(DEPTH):
        j = layer // 2
        mixer = [("in", ev_w_in[j]), ("out", ev_w_out[j])] if layer % 2 == 0 else [("in", od_w_in[j]), ("out", od_w_out[j])]
        for pos, (key, arr) in enumerate(mixer + [("w1", mlp_w1[layer]), ("w2", mlp_w2[layer])]):
            ag_at[layer, key] = len(ag_src)
            ag_src.append(arr.astype(BF16))
            ag_grp.append(len(ag_grp))
            ag_axis.append(1 if key in ("in", "w1") and arr.shape[1] % 128 == 0 else 0)
    ag_src, g_meta, g_cw = lax.optimization_barrier((ag_src, g_meta, g_cw))
    ag_sems, ag_s, ag_l, ag_tok = _push_start(
        "ag_start", ag_src, [_landing(s_.shape, s_.dtype, ax) for s_, ax in zip(ag_src, ag_axis)], True, ag_grp)

    def ag_wait(layer, key, after):
        a = ag_at[layer, key]
        return _push_wait(f"ag_wait_{a}", [ag_s[a]], [ag_l[a]], ag_sems[a], after, True)[0]

    h = jnp.concatenate([jnp.zeros((PAD, D), F32), meta_full, x[0]], axis=0)
    h = lax.optimization_barrier((h, ag_tok))[0]
    tgt = jnp.pad(loss_target[0], ((PAD + N_META, 0), (0, 0)))
    lb_all = _lb_fwd(lb_param)
    h, tgt, lb_all = lax.optimization_barrier((h, tgt, lb_all))

    tm_big = _tile(T, MM_ROWS_BIG, 16)
    tm_mid = _tile(T, MM_ROWS_MID, 16)
    tm_k4 = _tile(T, MM_ROWS_K4, 16)

    saved = []
    for layer in range(DEPTH):
        j = layer // 2
        sv = {"h0": h}
        g_in = ag_wait(layer, "in", h)
        w_in = g_in if g_in.ndim == 2 else jnp.transpose(g_in, (1, 0, 2)).reshape(D, -1)
        if layer % 2 == 0:
            sv["n"], u = _mm_rms_nn("ev_in", h, mix_norm_g[layer][None], w_in, tm_big, 512, "f32")
            yab, yc = _ev_fwd(u, cw_pad[j], ev_conv_b[j][None], ev_ln_g[j][None], ev_ln_b[j][None],
                              ev_pool_w[j].astype(BF16), ev_pool_b[j].reshape(1, -1), ev_pool_scale[j][None])
            sv.update(u=u, y=yab, yc=yc)
            w_out = ag_wait(layer, "out", yab).reshape(-1, D)
            h = _mm_nn("ev_out", yab, w_out, _full_w_spec, T, D, D, tm_mid, D, D, "resid", extra=h)
        else:
            sv["n"], u = _mm_rms_nn("od_in", h, mix_norm_g[layer][None], w_in, tm_big, 512, "f32")
            y, o, s0 = _hgrn_fwd(u, lb_all[layer][None], od_gnorm_g[j][None])
            sv.update(u=u, y=y, o=o, s0=s0)
            w_out = ag_wait(layer, "out", y).reshape(-1, D)
            h = _mm_nn("od_out", y, w_out, _full_w_spec, T, D, D, tm_mid, D, D, "resid", extra=h)
        sv["h1"] = h
        w_w1 = ag_wait(layer, "w1", h)
        n2, r, act = _mm_rms_nn("mlp_w1", h, mlp_norm_g[layer][None], w_w1, tm_big, 512, "relu2")
        w_w2 = ag_wait(layer, "w2", act).reshape(DFF, D)
        sv.update(w_in=w_in, w_out=w_out, w_w1=w_w1, w_w2=w_w2)
        sv.update(n2=n2, r=r, act=act)
        h = _mm_nn("mlp_w2", act, w_w2, _full_w_spec, T, D, DFF, tm_k4, D, DFF, "resid", extra=h)
        saved.append(sv)

    loss_blk, dh, dhb, dg_final = _loss_head(h, final_norm_g[None], tgt)
    loss = lax.psum(loss_blk[0, 0], AXES)

    tt = T
    g_mix, g_mlp = [None] * DEPTH, [None] * DEPTH
    small ={"cw": [None] * n_ev, "vec": [None] * n_ev, "pw": [None] * n_ev, "gn": [None] * n_od}
    dlb_rows = [jnp.zeros((1, D), F32) for _ in range(DEPTH)]

    def xs2(tt_, tk):
        return pl.BlockSpec((tt_, tk), lambda a, b, t: (t, a))

    def ys2(tt_, tn):
        return pl.BlockSpec((tt_, tn), lambda a, b, t: (t, b))

    def os2(tk, tn):
        return pl.BlockSpec((tk, tn), lambda a, b, t: (a, b))

    def os3(tk, tn):
        return pl.BlockSpec((None, tk, tn), lambda a, b, t: (b, a, 0))

    def dy2(tm, tn):
        return pl.BlockSpec((tm, tn), lambda i, jj, k: (i, k))

    def w_rows(tj, tn):
        return pl.BlockSpec((tj, tn), lambda i, jj, k: (jj, k))

    def w_whole(tj, tn):
        return pl.BlockSpec((tj, tn), lambda i, jj, k: (0, 0), pipeline_mode=pl.Buffered(1))

    rs_pending = []

    def rs_start(tag, mats):
        blocks = [m_ if m_.ndim == 3 else m_.reshape(N_DEV, m_.shape[0] // N_DEV, m_.shape[1]) for m_ in mats]
        lands = [_landing(b_.shape[1:], b_.dtype) for b_ in blocks]
        sems, s_thru, l_thru, tok = _push_start(f"rs_start_{tag}", blocks, lands, False, [0] * len(blocks))
        rs_pending.append((tag, s_thru, l_thru, sems[0]))
        return tok

    def after_start(gain, tok):
        return lax.optimization_barrier((gain, tok))[0]

    for layer in reversed(range(DEPTH)):
        j = layer // 2
        sv = saved[layer]
        da1 = _mm_nt("mlp_w2_t", dhb, sv["w_w2"], dy2, w_rows, T, DFF, D, tm_mid, 1024, D, "dact", extra=sv["r"])
        dw2 = _mm_tn("mlp_dw2", sv["act"], dhb, xs2, ys2, os2, (DFF, D), T, DFF, D, tt, 512, D)
        dw1 = _mm_tn("mlp_dw1", sv["n2"], da1, xs2, ys2, os3, (N_DEV, D, n_w1), T, D, DFF, tt, D, n_w1)
        tok = rs_start(f"mlp{layer}", [dw1, dw2])
        dh, dhb, g_mlp[layer] = _mm_nt("mlp_w1_t", da1, sv["w_w1"], dy2, w_whole, T, D, DFF, tm_k4, D, DFF, "rms",
                                       extra=(sv["h1"], after_start(mlp_norm_g[layer][None], tok), dh))
        if layer % 2 == 0:
            dyab = _mm_nt("ev_out_t", dhb, sv["w_out"], dy2, w_rows, T, D, D, tm_mid, D, D, "f32")
            dwout = _mm_tn("ev_dwout", sv["y"], dhb, xs2, ys2, os2, (D, D), T, D, D, tt, 512, D)
            du, small["cw"][j], small["vec"][j], small["pw"][j] = _ev_bwd(
                dyab, sv["yc"], sv["u"], cw_pad[j], ev_ln_g[j][None], ev_ln_b[j][None], ev_pool_w[j].astype(BF16),
                jnp.transpose(ev_pool_w[j], (0, 2, 1)).astype(BF16), ev_pool_b[j].reshape(1, -1),
                ev_pool_scale[j][None])
            nin = du.shape[1]
            dwin = _mm_tn("ev_dwin", sv["n"], du, xs2, ys2, os2, (D, nin), T, D, nin, tt, D, 512)
            dwin = jnp.transpose(dwin.reshape(D, N_DEV, nin // N_DEV), (1, 0, 2))
            tok = rs_start(f"mix{layer}", [dwin, dwout])
            dh, dhb, g_mix[layer] = _mm_nt("ev_in_t", du, sv["w_in"], dy2, w_whole, T, D, nin, tm_k4, D, nin, "rms",
                                           extra=(sv["h0"], after_start(mix_norm_g[layer][None], tok), dh))
        else:
            dy = _mm_nt("od_out_t", dhb, sv["w_out"], dy2, w_rows, T, D, D, tm_mid, D, D, "f32")
            dwout = _mm_tn("od_dwout", sv["y"], dhb, xs2, ys2, os2, (D, D), T, D, D, tt, 512, D)
            du3, dlb_rows[layer], small["gn"][j] = _hgrn_bwd(dy, sv["o"], sv["s0"], sv["u"], lb_all[layer][None],
                                                              od_gnorm_g[j][None])
            per = D // n_in_od

            def du_t(tt_, tn):
                return pl.BlockSpec((None, tt_, tn), lambda a, b, t: (b // per, t, b % per))

            dwin = _mm_tn("od_dwin", sv["n"], du3, xs2, du_t, os3, (N_DEV, D, n_in_od), T, D, 4 * D, tt, D, n_in_od)
            tok = rs_start(f"mix{layer}", [dwin, dwout])
            dh, dhb, g_mix[layer] = _mm_nt(
                "od_in_t", du3, sv["w_in"], lambda tm, tn: pl.BlockSpec((4, tm, tn // 4), lambda i, jj, k: (0, i, 0)),
                w_whole, T, D, 4 * D, tm_k4, D, 4 * D, "rms",
                extra=(sv["h0"], after_start(mix_norm_g[layer][None], tok), dh),
                parts=4)

    dmeta = dh[PAD:PAD + N_META]
    grad_x = dh[PAD + N_META:][None]
    dlb_param = _lb_bwd(lb_param, jnp.concatenate(dlb_rows, axis=0))

    pieces = [
        ("final", dg_final), ("pad", jnp.zeros((SMALL_F32_ROWS - 1, D), F32)),
        ("meta", dmeta), ("mix", jnp.concatenate(g_mix, 0)), ("mlp", jnp.concatenate(g_mlp, 0)),
        ("cw", jnp.stack([c[:CONV_WIDTH] for c in small["cw"]])), ("cb", jnp.stack([v_[0] for v_ in small["vec"]])),
        ("lng", jnp.stack([v_[1] for v_ in small["vec"]])), ("lnb", jnp.stack([v_[2] for v_ in small["vec"]])),
        ("pw", jnp.stack(small["pw"])), ("pb", jnp.stack([v_[4] for v_ in small["vec"]])),
        ("ps", jnp.stack([v_[3] for v_ in small["vec"]])), ("gn", jnp.stack([jnp.sum(g_, axis=0)[0] for g_ in small["gn"]])),
        ("lb", dlb_param),
    ]
    flat = jnp.concatenate([p.reshape(-1) for _, p in pieces])
    n_small = flat.shape[0]
    rows_small = SMALL_F32_ROWS + -(-(n_small // 1024 + 1 - SMALL_F32_ROWS) // 16) * 16
    flat = jnp.pad(flat, (0, rows_small * 1024 - n_small)).reshape(rows_small, 1024)

    sm_src = [flat[:SMALL_F32_ROWS], flat[SMALL_F32_ROWS:].astype(BF16)]
    sm_sems, sm_s, sm_l, sm_tok = _push_start("small_start", sm_src, [_landing(a_.shape, a_.dtype) for a_ in sm_src],
                                              True, [0, 0])
    recv = {}
    for tag, s_thru, l_thru, sems in rs_pending:
        got = _push_wait(f"rs_wait_{tag}", s_thru, l_thru, sems, sm_tok, False)
        layer = int(tag[3:])
        if tag.startswith("mlp"):
            recv["w1", layer], recv["w2", layer] = got
        else:
            key = "ev" if layer % 2 == 0 else "od"
            recv[key + "_in", layer // 2], recv[key + "_out", layer // 2] = got

    outs = {}
    big = {"ev_in": ("ev_w_in", ev_w_in, m_ev_w_in, v_ev_w_in), "ev_out": ("ev_w_out", ev_w_out, m_ev_w_out, v_ev_w_out),
           "od_in": ("od_w_in", od_w_in, m_od_w_in, v_od_w_in), "od_out": ("od_w_out", od_w_out, m_od_w_out, v_od_w_out),
           "w1": ("mlp_w1", mlp_w1, m_mlp_w1, v_mlp_w1), "w2": ("mlp_w2", mlp_w2, m_mlp_w2, v_mlp_w2)}
    for key, (name, w, m, v) in big.items():
        res = None
        for l in range(w.shape[0]):
            res = _adamw(recv[key, l], w, m, v, layer=l, prev=res)
        outs[name] = res

    small_params = {
        "meta": ("meta_tokens", None), "mix": ("mix_norm_g", mix_norm_g, m_mix_norm_g, v_mix_norm_g),
        "mlp": ("mlp_norm_g", mlp_norm_g, m_mlp_norm_g, v_mlp_norm_g),
        "final": ("final_norm_g", final_norm_g, m_final_norm_g, v_final_norm_g),
        "cw": ("ev_conv_w", None), "cb": ("ev_conv_b", ev_conv_b, m_ev_conv_b, v_ev_conv_b),
        "lng": ("ev_ln_g", ev_ln_g, m_ev_ln_g, v_ev_ln_g), "lnb": ("ev_ln_b", ev_ln_b, m_ev_ln_b, v_ev_ln_b),
        "pw": ("ev_pool_w", ev_pool_w, m_ev_pool_w, v_ev_pool_w), "pb": ("ev_pool_b", ev_pool_b, m_ev_pool_b, v_ev_pool_b),
        "ps": ("ev_pool_scale", ev_pool_scale, m_ev_pool_scale, v_ev_pool_scale),
        "gn": ("od_gnorm_g", od_gnorm_g, m_od_gnorm_g, v_od_gnorm_g), "lb": ("lb_param", lb_param, m_lb_param, v_lb_param),
    }
    csh = ev_conv_w.shape[2]
    msh = meta_tokens.shape[1]

    def packed(which):
        parts = []
        for key, g_ in pieces:
            ent = small_params.get(key)
            if key == "pad":
                full = g_
            elif key == "meta":
                src = (meta_tokens, m_meta_tokens, v_meta_tokens)[which]
                full = lax.dynamic_update_slice(jnp.zeros((N_META, D), F32), src, (0, dev * msh))
            elif key == "cw":
                src = (ev_conv_w, m_ev_conv_w, v_ev_conv_w)[which]
                full = lax.dynamic_update_slice(jnp.zeros(g_.shape, F32), src, (0, 0, dev * csh))
            else:
                full = ent[1 + which]
            parts.append(full.reshape(-1))
        f = jnp.concatenate(parts)
        return jnp.pad(f, (0, rows_small * 1024 - n_small)).reshape(rows_small, 1024)

    got_f32, got_bf16 = _push_wait("small_wait", sm_s, sm_l, sm_sems[0], outs["mlp_w2"][0], True)
    recv_small = jnp.concatenate([got_f32, got_bf16.astype(F32)], axis=1)
    sres = [r_[0] for r_ in _adamw(recv_small, packed(0)[None], packed(1)[None], packed(2)[None])]
    off = 0
    for key, g_ in pieces:
        size = g_.size
        vals = [r_.reshape(-1)[off:off + size].reshape(g_.shape) for r_ in sres]
        off += size
        if key == "pad":
            continue
        name = small_params[key][0]
        if key == "meta":
            vals = [lax.dynamic_slice(v_, (0, dev * msh), (N_META, msh)) for v_ in vals]
        elif key == "cw":
            vals = [lax.dynamic_slice(v_, (0, 0, dev * csh), v_.shape[:2] + (csh,)) for v_ in vals]
        else:
            vals = [v_.reshape(small_params[key][1].shape) for v_ in vals]
        outs[name] = vals

    names = ["meta_tokens", "mix_norm_g", "mlp_norm_g", "final_norm_g", "ev_w_in", "ev_conv_w", "ev_conv_b", "ev_ln_g",
             "ev_ln_b", "ev_pool_w", "ev_pool_b", "ev_pool_scale", "ev_w_out", "od_w_in", "od_gnorm_g", "od_w_out",
             "lb_param", "mlp_w1", "mlp_w2"]
    result = [loss, grad_x]
    for k in range(4):
        result += [outs[nm][k] for nm in names]
    return tuple(result)
```

```python
import functools

import jax
import jax.numpy as jnp
from jax import lax
from jax.experimental import pallas as pl
from jax.experimental.pallas import tpu as pltpu

F32 = jnp.float32
BF16 = jnp.bfloat16

N_DEV = 8
N_META = 16
CHUNK = 64
PAD = CHUNK - N_META
SUB = 16
HEAD = 128
CONV_WIDTH = 31
HALO = 32
POOL_WINDOWS = (2, 4, 8, 16)
EPS = 1e-6
NEG = -1e30
ADAM_LR, ADAM_B1, ADAM_B2, ADAM_EPS, ADAM_WD, ADAM_STEP = 0.001, 0.9, 0.999, 1e-08, 0.01, 10
VMEM_LIMIT = 56 * 1024 * 1024
EV_ROWS = 416
HGRN_ROWS = 832
MM_ROWS_BIG = 2080
MM_ROWS_MID = 1040
MM_ROWS_K4 = 416
SMALL_F32_ROWS = 8
HGRN_HEADS_FWD = 8
HGRN_HEADS_BWD = 2
MESH = pl.DeviceIdType.MESH
AXES = ("x", "y", "c")
ANY = pl.BlockSpec(memory_space=pl.ANY)


def _cp(*sem):
    return pltpu.CompilerParams(dimension_semantics=sem, vmem_limit_bytes=VMEM_LIMIT)


def _tile(n, cap, mult):
    best = None
    for d in range(mult, min(n, cap) + 1, mult):
        if n % d == 0:
            best = d
    assert best is not None, (n, cap, mult)
    return best


def _nt(a, b):
    return lax.dot_general(a, b, (((1,), (1,)), ((), ())), preferred_element_type=F32)


def _tn(a, b):
    return lax.dot_general(a, b, (((0,), (0,)), ((), ())), preferred_element_type=F32)


def _nn(a, b):
    return jnp.dot(a, b, preferred_element_type=F32)


def _r16(x):
    return x.astype(BF16).astype(F32)


def _row_ids(base, n):
    return base + lax.broadcasted_iota(jnp.int32, (n, 1), 0)


def _dsilu(x, s):
    return s * (1.0 + x * (1.0 - s))


def _loss_head(h, g, tgt):
    T, D = h.shape
    tm = _tile(T, MM_ROWS_MID, 16)
    first_x = PAD + N_META

    def body(h_ref, g_ref, t_ref, loss_ref, dh_ref, dhb_ref, dg_ref):
        i = pl.program_id(0)
        x = h_ref[...]
        r = lax.rsqrt(jnp.mean(x * x, axis=-1, keepdims=True) + EPS)
        xh = x * r
        gv = g_ref[...]
        out = xh * gv
        valid = _row_ids(i * tm, tm) >= first_x
        e = jnp.where(valid, out - t_ref[...], 0.0)
        dout = e * (1.0 / D)
        dxh = dout * gv
        dx = r * (dxh - xh * jnp.mean(dxh * xh, axis=-1, keepdims=True))
        dh_ref[...] = dx
        dhb_ref[...] = dx.astype(BF16)

        @pl.when(i == 0)
        def _():
            dg_ref[...] = jnp.zeros_like(dg_ref)
            loss_ref[...] = jnp.zeros_like(loss_ref)

        dg_ref[...] += jnp.sum(dout * xh, axis=0, keepdims=True)
        loss_ref[...] += 0.5 * jnp.sum(jnp.mean(e * e, axis=-1, keepdims=True))

    row = pl.BlockSpec((tm, D), lambda i: (i, 0))
    vec = pl.BlockSpec((1, D), lambda i: (0, 0))
    return pl.pallas_call(
        body, grid=(T // tm,),
        in_specs=[row, vec, row],
        out_specs=[pl.BlockSpec((8, 128), lambda i: (0, 0)), row, row, vec],
        out_shape=[jax.ShapeDtypeStruct((8, 128), F32), jax.ShapeDtypeStruct((T, D), F32),
                   jax.ShapeDtypeStruct((T, D), BF16), jax.ShapeDtypeStruct((1, D), F32)],
        compiler_params=_cp("arbitrary"), name="loss_head")(h, g, tgt)


def _mm_nn(name, a, w, w_spec, M, N, K, tm, tn, tk, mode, extra=None, a_spec=None):
    nk = K // tk
    if a_spec is None:
        a_spec = pl.BlockSpec((tm, tk), lambda i, j, k: (i, k))
    o_spec = pl.BlockSpec((tm, tn), lambda i, j, k: (i, j))

    def body(*refs):
        if mode == "resid":
            a_ref, w_ref, e_ref = refs[:3]
            outs = refs[3:]
        else:
            a_ref, w_ref = refs[:2]
            outs = refs[2:]
        acc_ref = outs[-1] if nk > 1 else None
        part = _nn(a_ref[...], w_ref[...])

        def finish(acc):
            if mode == "f32":
                outs[0][...] = acc
            elif mode == "relu2":
                r = jnp.maximum(acc, 0.0)
                outs[0][...] = r.astype(BF16)
                outs[1][...] = (r * r).astype(BF16)
            else:
                keep = _row_ids(pl.program_id(0) * tm, tm) >= PAD
                outs[0][...] = jnp.where(keep, e_ref[...] + acc, 0.0)

        if nk == 1:
            finish(part)
        else:
            k = pl.program_id(2)

            @pl.when(k == 0)
            def _():
                acc_ref[...] = part

            @pl.when(k > 0)
            def _():
                acc_ref[...] += part

            @pl.when(k == nk - 1)
            def _():
                finish(acc_ref[...])

    in_specs = [a_spec, w_spec(tk, tn)]
    args = [a, w]
    if mode == "resid":
        in_specs.append(o_spec)
        args.append(extra)
    if mode == "relu2":
        out_specs = [o_spec, o_spec]
        out_shape = [jax.ShapeDtypeStruct((M, N), BF16)] * 2
    else:
        out_specs = [o_spec]
        out_shape = [jax.ShapeDtypeStruct((M, N), F32)]
    scratch = [pltpu.VMEM((tm, tn), F32)] if nk > 1 else []
    res = pl.pallas_call(
        body, grid=(M // tm, N // tn, nk), in_specs=in_specs, out_specs=out_specs, out_shape=out_shape,
        scratch_shapes=scratch, compiler_params=_cp("parallel", "parallel", "arbitrary"), name=name)(*args)
    return res if mode == "relu2" else res[0]


def _mm_rms_nn(name, h, g, w, tm, tn, mode):
    M, K = h.shape
    N = w.shape[1]

    def body(h_ref, g_ref, w_ref, n_ref, *outs):
        @pl.when(pl.program_id(1) == 0)
        def _():
            ch = _tile(tm, 256, 16)

            def chunk(c, carry):
                rows = pl.ds(pl.multiple_of(c * ch, ch), ch)
                x = h_ref[rows, :]
                r = lax.rsqrt(jnp.mean(x * x, axis=-1, keepdims=True) + EPS)
                n_ref[rows, :] = ((x * r) * g_ref[...]).astype(BF16)
                return carry

            lax.fori_loop(0, tm // ch, chunk, 0)

        acc = _nn(n_ref[...], w_ref[...])
        if mode == "f32":
            outs[0][...] = acc
        else:
            r = jnp.maximum(acc, 0.0)
            outs[0][...] = r.astype(BF16)
            outs[1][...] = (r * r).astype(BF16)

    row = pl.BlockSpec((tm, K), lambda i, j: (i, 0))
    o_spec = pl.BlockSpec((tm, tn), lambda i, j: (i, j))
    n_out = 1 if mode == "f32" else 2
    return pl.pallas_call(
        body, grid=(M // tm, N // tn),
        in_specs=[row, pl.BlockSpec((1, K), lambda i, j: (0, 0)), pl.BlockSpec((K, tn), lambda i, j: (0, j))],
        out_specs=[row] + [o_spec] * n_out,
        out_shape=[jax.ShapeDtypeStruct((M, K), BF16)] + [jax.ShapeDtypeStruct((M, N), F32 if mode == "f32" else BF16)] * n_out,
        compiler_params=_cp("parallel", "arbitrary"), name=name)(h, g, w)


def _mm_nt(name, dy, w, dy_spec, w_spec, M, J, N, tm, tj, tn, mode, extra=None, parts=1):
    nk = N // tn
    o_spec = pl.BlockSpec((tm, tj), lambda i, j, k: (i, j))
    n_extra = {"f32": 0, "dact": 1, "rms": 3}[mode]
    if mode == "rms":
        assert nk == 1 and tj == J

    def body(*refs):
        dy_ref, w_ref = refs[:2]
        ex = refs[2:2 + n_extra]
        outs = refs[2 + n_extra:]
        acc_ref = outs[-1] if nk > 1 else None
        if parts == 1:
            part = _nt(dy_ref[...], w_ref[...])
        else:
            wq = tn // parts
            part = _nt(dy_ref[0], w_ref[:, 0:wq])
            for q in range(1, parts):
                part = part + _nt(dy_ref[q], w_ref[:, q * wq:(q + 1) * wq])

        def finish(acc):
            if mode == "f32":
                outs[0][...] = acc
            elif mode == "dact":
                outs[0][...] = (acc * (2.0 * ex[0][...].astype(F32))).astype(BF16)
            else:
                h_ref, g_ref, dres_ref = ex
                dh_ref, dhb_ref, dg_ref = outs[:3]
                i = pl.program_id(0)

                @pl.when(i == 0)
                def _():
                    dg_ref[...] = jnp.zeros_like(dg_ref)

                ch = _tile(tm, 256, 16)
                for c0 in range(0, tm, ch):
                    a_c = acc[c0:c0 + ch]
                    x = h_ref[c0:c0 + ch, :]
                    r = lax.rsqrt(jnp.mean(x * x, axis=-1, keepdims=True) + EPS)
                    xh = x * r
                    dxh = a_c * g_ref[...]
                    dx = r * (dxh - xh * jnp.mean(dxh * xh, axis=-1, keepdims=True))
                    keep = _row_ids(i * tm + c0, ch) >= PAD
                    dh = jnp.where(keep, dres_ref[c0:c0 + ch, :] + dx, 0.0)
                    dh_ref[c0:c0 + ch, :] = dh
                    dhb_ref[c0:c0 + ch, :] = dh.astype(BF16)
                    dg_ref[...] += jnp.sum(a_c * xh, axis=0, keepdims=True)

        if nk == 1:
            finish(part)
        else:
            k = pl.program_id(2)

            @pl.when(k == 0)
            def _():
                acc_ref[...] = part

            @pl.when(k > 0)
            def _():
                acc_ref[...] += part

            @pl.when(k == nk - 1)
            def _():
                finish(acc_ref[...])

    in_specs = [dy_spec(tm, tn), w_spec(tj, tn)]
    args = [dy, w]
    scratch = [pltpu.VMEM((tm, tj), F32)] if nk > 1 else []
    if mode == "rms":
        vec = pl.BlockSpec((1, J), lambda i, j, k: (0, 0))
        h, g, dres = extra
        res = pl.pallas_call(
            body, grid=(M // tm, 1, 1), in_specs=in_specs + [o_spec, vec, o_spec], out_specs=[o_spec, o_spec, vec],
            out_shape=[jax.ShapeDtypeStruct((M, J), F32), jax.ShapeDtypeStruct((M, J), BF16),
                       jax.ShapeDtypeStruct((1, J), F32)],
            compiler_params=_cp("arbitrary", "arbitrary", "arbitrary"), name=name)(*args, h, g, dres)
        return res
    if mode == "dact":
        in_specs.append(o_spec)
        args.append(extra)
    return pl.pallas_call(
        body, grid=(M // tm, J // tj, nk), in_specs=in_specs, out_specs=[o_spec],
        out_shape=[jax.ShapeDtypeStruct((M, J), BF16 if mode == "dact" else F32)],
        scratch_shapes=scratch, compiler_params=_cp("parallel", "parallel", "arbitrary"), name=name)(*args)[0]


def _mm_tn(name, x, dy, x_spec, dy_spec, o_spec, o_shape, T, K, N, tt, tk, tn):
    nt = T // tt

    def body(x_ref, dy_ref, o_ref, *acc):
        part = _tn(x_ref[...], dy_ref[...])
        if nt == 1:
            o_ref[...] = part.astype(BF16)
            return
        acc_ref = acc[0]
        t = pl.program_id(2)

        @pl.when(t == 0)
        def _():
            acc_ref[...] = part

        @pl.when(t > 0)
        def _():
            acc_ref[...] += part

        @pl.when(t == nt - 1)
        def _():
            o_ref[...] = acc_ref[...].astype(BF16)

    return pl.pallas_call(
        body, grid=(K // tk, N // tn, nt), in_specs=[x_spec(tt, tk), dy_spec(tt, tn)], out_specs=o_spec(tk, tn),
        out_shape=jax.ShapeDtypeStruct(o_shape, BF16), scratch_shapes=[pltpu.VMEM((tk, tn), F32)] if nt > 1 else [],
        compiler_params=_cp("parallel", "parallel", "arbitrary"), name=name)(x, dy)


def _pool_counts(base, n, w):
    pos = _row_ids(base, n) - PAD
    return jnp.clip(pos + 1, 1, w).astype(F32)


def _shifted_copies(buf, rows):
    buf[0, rows:rows + 8, :] = jnp.zeros((8, buf.shape[2]), F32)

    def blk(s, carry):
        b = pl.multiple_of(s * HALO, HALO)
        win = buf[0, pl.ds(b, HALO + 8), :]
        for r in range(1, 8):
            buf[r, pl.ds(b, HALO), :] = win[r:r + HALO]
        return carry

    lax.fori_loop(0, rows // HALO, blk, 0)


def _ev_fwd(u, cw, cb, lg, lb, pw, pb, ps):
    T = u.shape[0]
    C = 512
    tm = _tile(T, EV_ROWS, HALO)
    nsub = tm // HALO
    hb = tm // HALO

    def body(val_ref, gate_ref, pin_ref, valh_ref, gateh_ref, pinh_ref, cw_ref, cb_ref, lg_ref, lb_ref, pw_ref,
             pb_ref, ps_ref, yab_ref, yc_ref, a_ext, p_ext, d_buf):
        i = pl.program_id(0)
        nf = (i > 0).astype(F32)
        a_ext[0, 0:HALO, :] = valh_ref[...] * jax.nn.sigmoid(gateh_ref[...]) * nf
        a_ext[0, HALO:HALO + tm, :] = val_ref[...] * jax.nn.sigmoid(gate_ref[...])
        p_ext[0:HALO, :] = pinh_ref[...] * nf
        p_ext[HALO:, :] = pin_ref[...]
        _shifted_copies(a_ext, tm + HALO)

        def sub(s, carry):
            base = pl.multiple_of(s * HALO, HALO)
            acc = jnp.zeros((HALO, C), F32) + cb_ref[...]
            for j in range(CONV_WIDTH):
                off = 2 + j
                acc = acc + cw_ref[pl.ds(j, 1), :] * a_ext[off % 8, pl.ds(pl.multiple_of(base + off // 8 * 8, 8), HALO), :]
            yc_ref[pl.ds(base, HALO), :] = acc
            mu = jnp.mean(acc, axis=-1, keepdims=True)
            yc = acc - mu
            rstd = lax.rsqrt(jnp.mean(yc * yc, axis=-1, keepdims=True) + EPS)
            z = (yc * rstd) * lg_ref[...] + lb_ref[...]
            yab_ref[pl.ds(base, HALO), 0:C] = (z * jax.nn.sigmoid(z)).astype(BF16)
            pwin = p_ext[pl.ds(base, 2 * HALO), :]
            for gi, w in enumerate(POOL_WINDOWS):
                lo, hi = gi * HEAD, (gi + 1) * HEAD
                x = pwin[HALO:, lo:hi]
                tot = x
                for k in range(1, w):
                    tot = tot + pwin[HALO - k:2 * HALO - k, lo:hi]
                cnt = _pool_counts(i * tm + base, HALO, w)
                d_buf[pl.ds(base, HALO), lo:hi] = (tot / cnt - x).astype(BF16)
            return carry

        lax.fori_loop(0, nsub, sub, 0, unroll=2)
        for gi in range(len(POOL_WINDOWS)):
            lo, hi = gi * HEAD, (gi + 1) * HEAD
            y = _nn(d_buf[:, lo:hi], pw_ref[gi]) + pb_ref[:, lo:hi]
            yab_ref[:, C + lo:C + hi] = (y * ps_ref[:, lo:hi]).astype(BF16)

    def main(c):
        return pl.BlockSpec((tm, C), lambda i: (i, c))

    def halo(c):
        return pl.BlockSpec((HALO, C), lambda i: (jnp.maximum(i * hb - 1, 0), c))

    vec = pl.BlockSpec((1, C), lambda i: (0, 0))
    return pl.pallas_call(
        body, grid=(T // tm,),
        in_specs=[main(0), main(1), main(2), halo(0), halo(1), halo(2),
                  pl.BlockSpec((32, C), lambda i: (0, 0)), vec, vec, vec,
                  pl.BlockSpec((4, HEAD, HEAD), lambda i: (0, 0, 0)), vec, vec],
        out_specs=[pl.BlockSpec((tm, 2 * C), lambda i: (i, 0)), pl.BlockSpec((tm, C), lambda i: (i, 0))],
        out_shape=[jax.ShapeDtypeStruct((T, 2 * C), BF16), jax.ShapeDtypeStruct((T, C), F32)],
        scratch_shapes=[pltpu.VMEM((8, tm + HALO + 8, C), F32), pltpu.VMEM((tm + HALO, C), F32),
                        pltpu.VMEM((tm, C), BF16)],
        compiler_params=_cp("parallel"), name="ev_fwd")(u, u, u, u, u, u, cw, cb, lg, lb, pw, pb, ps)


def _ev_bwd(dyab, yc, u, cw, lg, lb, pw, pwt, pb, ps):
    T = u.shape[0]
    C = 512
    tm = _tile(T, EV_ROWS, HALO)
    nsub = tm // HALO
    hb = tm // HALO
    nblk = T // tm
    E = tm + HALO

    def body(dya_ref, dyb_ref, dyah_ref, dybh_ref, yc_ref, ych_ref, val_ref, gate_ref, pin_ref, valh_ref, gateh_ref,
             pinh_ref, cw_ref, lg_ref, lb_ref, pw_ref, pwt_ref, pb_ref, ps_ref,
             du_ref, dcw_ref, dvec_ref, dpw_ref,
             dy_ext, a_ext, p_ext, ddc_ext, dd_buf, d_buf, dpre_buf, dcw_acc, vec_acc):
        i = pl.program_id(0)
        nf = (i > 0).astype(F32)
        nl = (i < nblk - 1).astype(F32)

        @pl.when(i == 0)
        def _():
            dcw_ref[...] = jnp.zeros_like(dcw_ref)
            dvec_ref[...] = jnp.zeros_like(dvec_ref)
            dpw_ref[...] = jnp.zeros_like(dpw_ref)

        dcw_acc[...] = jnp.zeros_like(dcw_acc)
        vec_acc[...] = jnp.zeros_like(vec_acc)
        a_ext[0, 0:HALO, :] = valh_ref[...] * jax.nn.sigmoid(gateh_ref[...]) * nf
        a_ext[0, HALO:E, :] = val_ref[...] * jax.nn.sigmoid(gate_ref[...])
        p_ext[0:HALO, :] = pinh_ref[...] * nf
        p_ext[HALO:, :] = pin_ref[...]
        _shifted_copies(a_ext, E)

        def ln_bwd(y, dya, main):
            mu = jnp.mean(y, axis=-1, keepdims=True)
            ycen = y - mu
            rstd = lax.rsqrt(jnp.mean(ycen * ycen, axis=-1, keepdims=True) + EPS)
            yh = ycen * rstd
            z = yh * lg_ref[...] + lb_ref[...]
            sz = jax.nn.sigmoid(z)
            dz = dya * _dsilu(z, sz)
            dyh = dz * lg_ref[...]
            dy = rstd * (dyh - jnp.mean(dyh, axis=-1, keepdims=True) - yh * jnp.mean(dyh * yh, axis=-1, keepdims=True))
            if main:
                vec_acc[1] += jnp.sum((dz * yh).reshape(HALO // 8, 8, C), axis=0)
                vec_acc[2] += jnp.sum(dz.reshape(HALO // 8, 8, C), axis=0)
                vec_acc[0] += jnp.sum(dy.reshape(HALO // 8, 8, C), axis=0)
            return dy

        def pool_dd(dyb, base, main):
            dpre = dyb * ps_ref[...]
            for gi, w in enumerate(POOL_WINDOWS):
                lo, hi = gi * HEAD, (gi + 1) * HEAD
                dd = _nn(dpre[:, lo:hi].astype(BF16), pwt_ref[gi])
                cnt = _pool_counts(i * tm + base, HALO, w)
                ddc_ext[pl.ds(base, HALO), lo:hi] = dd / cnt
                if main:
                    dd_buf[pl.ds(base, HALO), lo:hi] = dd
            if main:
                dpre_buf[pl.ds(base, HALO), :] = dpre.astype(BF16)
                vec_acc[4] += jnp.sum(dpre.reshape(HALO // 8, 8, C), axis=0)

        def p1(s, carry):
            base = pl.multiple_of(s * HALO, HALO)
            dy_ext[0, pl.ds(base, HALO), :] = ln_bwd(yc_ref[pl.ds(base, HALO), :], dya_ref[pl.ds(base, HALO), :], True)
            pool_dd(dyb_ref[pl.ds(base, HALO), :], base, True)
            return carry

        lax.fori_loop(0, nsub, p1, 0, unroll=2)
        dy_ext[0, tm:E, :] = ln_bwd(ych_ref[...], dyah_ref[...], False) * nl
        _shifted_copies(dy_ext, E)
        dpre_h = dybh_ref[...] * ps_ref[...] * nl
        for gi, w in enumerate(POOL_WINDOWS):
            lo, hi = gi * HEAD, (gi + 1) * HEAD
            dd = _nn(dpre_h[:, lo:hi].astype(BF16), pwt_ref[gi])
            ddc_ext[tm:, lo:hi] = dd / _pool_counts(i * tm + tm, HALO, w)

        def p2(s, carry):
            base = pl.multiple_of(s * HALO, HALO)
            dy_m = dy_ext[0, pl.ds(base, HALO), :]
            da = jnp.zeros((HALO, C), F32)
            for j in range(CONV_WIDTH):
                sh = CONV_WIDTH - 1 - j
                off = 2 + j
                da = da + cw_ref[pl.ds(j, 1), :] * dy_ext[sh % 8, pl.ds(pl.multiple_of(base + sh // 8 * 8, 8), HALO), :]
                a_j = a_ext[off % 8, pl.ds(pl.multiple_of(base + off // 8 * 8, 8), HALO), :]
                dcw_acc[j] += jnp.sum((dy_m * a_j).reshape(HALO // 8, 8, C), axis=0)
            v = val_ref[pl.ds(base, HALO), :]
            g = gate_ref[pl.ds(base, HALO), :]
            sg = jax.nn.sigmoid(g)
            du_ref[pl.ds(base, HALO), 0:C] = (da * sg).astype(BF16)
            du_ref[pl.ds(base, HALO), C:2 * C] = (da * v * sg * (1.0 - sg)).astype(BF16)
            pwin = p_ext[pl.ds(base, 2 * HALO), :]
            cwin = ddc_ext[pl.ds(base, 2 * HALO), :]
            for gi, w in enumerate(POOL_WINDOWS):
                lo, hi = gi * HEAD, (gi + 1) * HEAD
                x = pwin[HALO:, lo:hi]
                tot = x
                back = cwin[0:HALO, lo:hi]
                for k in range(1, w):
                    tot = tot + pwin[HALO - k:2 * HALO - k, lo:hi]
                    back = back + cwin[k:k + HALO, lo:hi]
                cnt = _pool_counts(i * tm + base, HALO, w)
                d_buf[pl.ds(base, HALO), lo:hi] = (tot / cnt - x).astype(BF16)
                du_ref[pl.ds(base, HALO), 2 * C + lo:2 * C + hi] = (back - dd_buf[pl.ds(base, HALO), lo:hi]).astype(BF16)
            return carry

        lax.fori_loop(0, nsub, p2, 0)
        for gi in range(len(POOL_WINDOWS)):
            lo, hi = gi * HEAD, (gi + 1) * HEAD
            pre = _nn(d_buf[:, lo:hi], pw_ref[gi]) + pb_ref[:, lo:hi]
            vec_acc[3, :, lo:hi] += jnp.sum((dyb_ref[:, lo:hi] * pre).reshape(tm // 8, 8, HEAD), axis=0)
            dpw_ref[gi] += _tn(d_buf[:, lo:hi], dpre_buf[:, lo:hi])
        for j in range(CONV_WIDTH):
            dcw_ref[pl.ds(j, 1), :] += jnp.sum(dcw_acc[j], axis=0, keepdims=True)
        for r in range(5):
            dvec_ref[pl.ds(r, 1), :] += jnp.sum(vec_acc[r], axis=0, keepdims=True)

    def main(c, width=C):
        return pl.BlockSpec((tm, width), lambda i: (i, c))

    def prev(c):
        return pl.BlockSpec((HALO, C), lambda i: (jnp.maximum(i * hb - 1, 0), c))

    def nxt(c):
        return pl.BlockSpec((HALO, C), lambda i: (jnp.minimum((i + 1) * hb, T // HALO - 1), c))

    vec = pl.BlockSpec((1, C), lambda i: (0, 0))
    mat = pl.BlockSpec((4, HEAD, HEAD), lambda i: (0, 0, 0))
    return pl.pallas_call(
        body, grid=(nblk,),
        in_specs=[main(0), main(1), nxt(0), nxt(1), main(0), nxt(0), main(0), main(1), main(2), prev(0), prev(1),
                  prev(2), pl.BlockSpec((32, C), lambda i: (0, 0)), vec, vec, mat, mat, vec, vec],
        out_specs=[pl.BlockSpec((tm, 3 * C), lambda i: (i, 0)), pl.BlockSpec((32, C), lambda i: (0, 0)),
                   pl.BlockSpec((8, C), lambda i: (0, 0)), mat],
        out_shape=[jax.ShapeDtypeStruct((T, 3 * C), BF16), jax.ShapeDtypeStruct((32, C), F32),
                   jax.ShapeDtypeStruct((8, C), F32), jax.ShapeDtypeStruct((4, HEAD, HEAD), F32)],
        scratch_shapes=[pltpu.VMEM((8, E + 8, C), F32), pltpu.VMEM((8, E + 8, C), F32), pltpu.VMEM((E, C), F32),
                        pltpu.VMEM((E, C), F32), pltpu.VMEM((tm, C), F32), pltpu.VMEM((tm, C), BF16),
                        pltpu.VMEM((tm, C), BF16), pltpu.VMEM((32, 8, C), F32), pltpu.VMEM((8, 8, C), F32)],
        compiler_params=_cp("arbitrary"), name="ev_bwd")(
            dyab, dyab, dyab, dyab, yc, yc, u, u, u, u, u, u, cw, lg, lb, pw, pwt, pb, ps)


def _cumsum_rows(x, reverse=False):
    n = x.shape[0]
    rid = lax.broadcasted_iota(jnp.int32, (n, 1), 0)
    k = 1
    while k < n:
        if reverse:
            sh = jnp.where(rid < n - k, pltpu.roll(x, n - k, 0), 0.0)
        else:
            sh = jnp.where(rid >= k, pltpu.roll(x, k, 0), 0.0)
        x = x + sh
        k *= 2
    return x


def _hgrn_gates(qr, fr, lbv):
    sq = jax.nn.sigmoid(qr)
    sg = jax.nn.sigmoid(fr)
    fg = lbv + (1.0 - lbv) * sg
    return qr * sq, sq, sg, fg, 1.0 - fg, jnp.log(fg)


def _hgrn_fwd(u, lbv, gn):
    T = u.shape[0]
    H = 8
    RB = _tile(T, HGRN_ROWS, CHUNK)
    NC = RB // CHUNK
    NS = CHUNK // SUB

    HP = HGRN_HEADS_FWD
    W = HP * HEAD

    def body(q_ref, f_ref, i_ref, g_ref, lb_ref, gn_ref, y_ref, o_ref, s0_ref, st, qs, ks, bs, vs, os_):
        rb = pl.program_id(1)

        @pl.when(rb == 0)
        def _():
            st[...] = jnp.zeros_like(st)

        t8 = lax.broadcasted_iota(jnp.int32, (8, 1), 0)

        def head(hh, c, rows):
            sl = slice(hh * HEAD, (hh + 1) * HEAD)
            q, _, _, _, kk, lf = _hgrn_gates(q_ref[rows, sl], f_ref[rows, sl], lb_ref[:, sl])
            v = i_ref[rows, sl]
            b = _cumsum_rows(lf)
            qs[hh] = q
            ks[hh] = kk
            bs[hh] = b
            vs[hh] = v
            st0 = st[hh]
            s0_ref[hh, c] = st0
            os_[hh] = _nt((q * jnp.exp(b)).astype(BF16), st0.astype(BF16))
            for I in range(NS):
                lo = I * SUB
                qI = qs[hh, lo:lo + SUB, :]
                bI = bs[hh, lo:lo + SUB, :]
                oI = jnp.zeros((SUB, HEAD), F32)
                if I > 0:
                    bprev = bs[hh, pl.ds(lo - 1, 1), :]
                    qt = _r16(qI * jnp.exp(bI - bprev))
                    kt = _r16(ks[hh, 0:lo, :] * jnp.exp(bprev - bs[hh, 0:lo, :]))
                    A = _nt(qt, kt)
                    oI = oI + _nn(_r16(A), _r16(vs[hh, 0:lo, :]))
                od = [jnp.zeros((8, HEAD), F32) for _ in range(SUB // 8)]
                for s in range(SUB):
                    row = pl.ds(lo + s, 1)
                    brow, krow, vrow = bs[hh, row, :], ks[hh, row, :], vs[hh, row, :]
                    for ti in range(SUB // 8):
                        o8 = 8 * ti
                        if s > o8 + 7:
                            continue
                        d = bI[o8:o8 + 8] - brow
                        if s > o8:
                            d = jnp.where(t8 >= s - o8, d, NEG)
                        col = jnp.sum(qI[o8:o8 + 8] * jnp.exp(d) * krow, axis=1, keepdims=True)
                        od[ti] = od[ti] + col * vrow
                os_[hh, lo:lo + SUB, :] += oI + jnp.concatenate(od, axis=0)
            blast = bs[hh, pl.ds(CHUNK - 1, 1), :]
            kh = kk * jnp.exp(blast - b)
            st[hh] = st0 * jnp.exp(blast) + _tn(v.astype(BF16), kh.astype(BF16))
            o = os_[hh]
            o_ref[rows, sl] = o
            rr = lax.rsqrt(jnp.mean(o * o, axis=-1, keepdims=True) + EPS)
            gr = g_ref[rows, sl]
            y_ref[rows, sl] = (((o * rr) * gn_ref[...]) * (gr * jax.nn.sigmoid(gr))).astype(BF16)

        def chunk(c, carry):
            rows = pl.ds(pl.multiple_of(c * CHUNK, CHUNK), CHUNK)
            for hh in range(HP):
                head(hh, c, rows)
            return carry

        lax.fori_loop(0, NC, chunk, 0)

    def blk(q):
        return pl.BlockSpec((RB, W), lambda h, r: (r, q * (H // HP) + h))

    sc = lambda: pltpu.VMEM((HP, CHUNK, HEAD), F32)
    return pl.pallas_call(
        body, grid=(H // HP, T // RB),
        in_specs=[blk(0), blk(1), blk(2), blk(3), pl.BlockSpec((1, W), lambda h, r: (0, h)),
                  pl.BlockSpec((1, HEAD), lambda h, r: (0, 0))],
        out_specs=[pl.BlockSpec((RB, W), lambda h, r: (r, h)), pl.BlockSpec((RB, W), lambda h, r: (r, h)),
                   pl.BlockSpec((HP, NC, HEAD, HEAD), lambda h, r: (h, r, 0, 0))],
        out_shape=[jax.ShapeDtypeStruct((T, H * HEAD), BF16), jax.ShapeDtypeStruct((T, H * HEAD), F32),
                   jax.ShapeDtypeStruct((H, T // CHUNK, HEAD, HEAD), F32)],
        scratch_shapes=[pltpu.VMEM((HP, HEAD, HEAD), F32), sc(), sc(), sc(), sc(), sc()],
        compiler_params=_cp("parallel", "arbitrary"), name="hgrn_fwd")(u, u, u, u, lbv, gn)


def _hgrn_bwd(dy, o, s0, u, lbv, gn):
    T = u.shape[0]
    H = 8
    RB = _tile(T, HGRN_ROWS, CHUNK)
    NB = T // RB
    NC = RB // CHUNK
    NS = CHUNK // SUB

    def body(q_ref, f_ref, i_ref, g_ref, lb_ref, gn_ref, o_ref, dy_ref, s0_ref, du_ref, dlb_ref, dgn_ref,
             dst, qs, ks, bs, vs, dos, dqs, dks, dki, dvs, dbs):
        rb = pl.program_id(1)

        @pl.when(rb == 0)
        def _():
            dst[...] = jnp.zeros_like(dst)
            dlb_ref[...] = jnp.zeros_like(dlb_ref)
            dgn_ref[...] = jnp.zeros_like(dgn_ref)

        t8 = lax.broadcasted_iota(jnp.int32, (8, 1), 0)
        lane = lax.broadcasted_iota(jnp.int32, (8, HEAD), 1)
        gnv = gn_ref[...]

        def head(hh, c, rows):
            sl = slice(hh * HEAD, (hh + 1) * HEAD)
            lbv_ = lb_ref[:, sl]
            qr = q_ref[rows, sl]
            q, sq, sg, fg, kk, lf = _hgrn_gates(qr, f_ref[rows, sl], lbv_)
            v = i_ref[rows, sl]
            gr = g_ref[rows, sl]
            b = _cumsum_rows(lf)
            eb = jnp.exp(b)
            ov = o_ref[rows, sl]
            dyv = dy_ref[rows, sl]
            rr = lax.rsqrt(jnp.mean(ov * ov, axis=-1, keepdims=True) + EPS)
            oh = ov * rr
            gs = jax.nn.sigmoid(gr)
            dgr = dyv * (oh * gnv) * _dsilu(gr, gs)
            dnrm = dyv * (gr * gs)
            dgn_ref[hh] += jnp.sum(dnrm * oh, axis=0, keepdims=True)
            t1 = dnrm * gnv
            do = rr * (t1 - oh * jnp.mean(t1 * oh, axis=-1, keepdims=True))
            qs[hh] = q
            ks[hh] = kk
            bs[hh] = b
            vs[hh] = v
            dos[hh] = do
            st0 = s0_ref[hh, c]
            dS = dst[hh]
            do_b = do.astype(BF16)
            blast = bs[hh, pl.ds(CHUNK - 1, 1), :]
            elast = jnp.exp(blast - b)
            dq_inter = _nn(do_b, st0.astype(BF16)) * eb
            dqs[hh] = dq_inter
            dbs[hh] = q * dq_inter
            kh = kk * elast
            dvs[hh] = _nt(kh.astype(BF16), dS.astype(BF16))
            dk_inter = _nn(v.astype(BF16), dS.astype(BF16)) * elast
            dki[hh] = dk_inter
            dks[hh] = jnp.zeros((CHUNK, HEAD), F32)
            for I in range(NS):
                lo = I * SUB
                qI = qs[hh, lo:lo + SUB, :]
                bI = bs[hh, lo:lo + SUB, :]
                doI = dos[hh, lo:lo + SUB, :]
                dqI = jnp.zeros((SUB, HEAD), F32)
                dbI = jnp.zeros((SUB, HEAD), F32)
                if I > 0:
                    bprev = bs[hh, pl.ds(lo - 1, 1), :]
                    eq = jnp.exp(bI - bprev)
                    ek = jnp.exp(bprev - bs[hh, 0:lo, :])
                    qt = _r16(qI * eq)
                    kt = _r16(ks[hh, 0:lo, :] * ek)
                    A = _r16(_nt(qt, kt))
                    doI_b = _r16(doI)
                    dA = _r16(_nt(doI_b, _r16(vs[hh, 0:lo, :])))
                    dvs[hh, 0:lo, :] += _tn(A, doI_b)
                    dqt = _nn(dA, kt)
                    dkt = _tn(dA, qt)
                    dqI = dqI + dqt * eq
                    dbI = dbI + qt.astype(F32) * dqt
                    dks[hh, 0:lo, :] += dkt * ek
                    dbs[hh, 0:lo, :] -= kt.astype(F32) * dkt
                dq_t = [jnp.zeros((8, HEAD), F32) for _ in range(SUB // 8)]
                a_t = [jnp.zeros((8, HEAD), F32) for _ in range(SUB // 8)]
                for s in range(SUB):
                    row = pl.ds(lo + s, 1)
                    brow, krow, vrow = bs[hh, row, :], ks[hh, row, :], vs[hh, row, :]
                    dk_s = None
                    for ti in range(SUB // 8):
                        o8 = 8 * ti
                        if s > o8 + 7:
                            continue
                        d = bI[o8:o8 + 8] - brow
                        if s > o8:
                            d = jnp.where(t8 >= s - o8, d, NEG)
                        Es = jnp.exp(d)
                        qE = qI[o8:o8 + 8] * Es
                        col = jnp.sum(qE * krow, axis=1, keepdims=True)
                        a_t[ti] = jnp.where(lane == s, col, a_t[ti])
                        dcol = jnp.sum(doI[o8:o8 + 8] * vrow, axis=1, keepdims=True)
                        dq_t[ti] = dq_t[ti] + (dcol * Es) * krow
                        part = jnp.sum(dcol * qE, axis=0, keepdims=True)
                        dk_s = part if dk_s is None else dk_s + part
                    dks[hh, row, :] += dk_s
                    dbs[hh, row, :] -= krow * dk_s
                a_d = jnp.concatenate(a_t, axis=0)
                dq_d = jnp.concatenate(dq_t, axis=0)
                dvs[hh, lo:lo + SUB, :] += _tn(a_d, doI)[0:SUB]
                dqI = dqI + dq_d
                dbI = dbI + qI * dq_d
                dqs[hh, lo:lo + SUB, :] += dqI
                dbs[hh, lo:lo + SUB, :] += dbI
            kdk = kk * dki[hh]
            excl = _cumsum_rows(kdk) - kdk
            suff = _cumsum_rows(dbs[hh], reverse=True)
            gdec = jnp.sum(dS * st0, axis=0, keepdims=True) * jnp.exp(blast)
            dlf = suff + excl + gdec
            dk = dks[hh] + dki[hh]
            dfg = dlf / fg - dk
            dlb_ref[:, sl] += jnp.sum(dfg * (1.0 - sg), axis=0, keepdims=True)
            du_ref[0, rows, sl] = (dqs[hh] * _dsilu(qr, sq)).astype(BF16)
            du_ref[1, rows, sl] = (dfg * (1.0 - lbv_) * sg * (1.0 - sg)).astype(BF16)
            du_ref[2, rows, sl] = dvs[hh].astype(BF16)
            du_ref[3, rows, sl] = dgr.astype(BF16)
            dst[hh] = dS * jnp.exp(blast) + _tn(do_b, (q * eb).astype(BF16))

        def chunk(cc, carry):
            c = NC - 1 - cc
            rows = pl.ds(pl.multiple_of(c * CHUNK, CHUNK), CHUNK)
            for hh in range(HP):
                head(hh, c, rows)
            return carry

        lax.fori_loop(0, NC, chunk, 0)

    HP = HGRN_HEADS_BWD
    W = HP * HEAD

    def blk(qd):
        return pl.BlockSpec((RB, W), lambda h, r: (NB - 1 - r, qd * (H // HP) + h))

    hblk = pl.BlockSpec((RB, W), lambda h, r: (NB - 1 - r, h))
    sc = lambda: pltpu.VMEM((HP, CHUNK, HEAD), F32)
    return pl.pallas_call(
        body, grid=(H // HP, NB),
        in_specs=[blk(0), blk(1), blk(2), blk(3), pl.BlockSpec((1, W), lambda h, r: (0, h)),
                  pl.BlockSpec((1, HEAD), lambda h, r: (0, 0)), hblk, hblk,
                  pl.BlockSpec((HP, NC, HEAD, HEAD), lambda h, r: (h, NB - 1 - r, 0, 0))],
        out_specs=[pl.BlockSpec((4, RB, W), lambda h, r: (0, NB - 1 - r, h)),
                   pl.BlockSpec((1, W), lambda h, r: (0, h)), pl.BlockSpec((HP, 1, HEAD), lambda h, r: (h, 0, 0))],
        out_shape=[jax.ShapeDtypeStruct((4, T, H * HEAD), BF16), jax.ShapeDtypeStruct((1, H * HEAD), F32),
                   jax.ShapeDtypeStruct((H, 1, HEAD), F32)],
        scratch_shapes=[pltpu.VMEM((HP, HEAD, HEAD), F32)] + [sc() for _ in range(10)],
        compiler_params=_cp("parallel", "arbitrary"), name="hgrn_bwd")(u, u, u, u, lbv, gn, o, dy, s0)


def _softmax_rows(p_ref, L):
    rows = [p_ref[pl.ds(l, 1), :] for l in range(L)]
    m = rows[0]
    for r in rows[1:]:
        m = jnp.maximum(m, r)
    e = [jnp.exp(r - m) for r in rows]
    tot = e[0]
    for t in e[1:]:
        tot = tot + t
    return [t / tot for t in e]


def _lb_fwd(lbp):
    L, D = lbp.shape

    def body(p_ref, o_ref):
        sm = _softmax_rows(p_ref, L)
        acc = jnp.zeros((1, D), F32)
        o_ref[pl.ds(0, 1), :] = acc
        for l in range(1, L):
            acc = acc + sm[l]
            o_ref[pl.ds(l, 1), :] = acc

    return pl.pallas_call(body, out_shape=jax.ShapeDtypeStruct((L, D), F32), name="lb_fwd")(lbp)


def _lb_bwd(lbp, dlb):
    L, D = lbp.shape

    def body(p_ref, d_ref, o_ref):
        sm = _softmax_rows(p_ref, L)
        dsm = [jnp.zeros((1, D), F32)]
        for i in range(1, L):
            t = jnp.zeros((1, D), F32)
            for l in range(i, L):
                t = t + d_ref[pl.ds(l, 1), :]
            dsm.append(t)
        dot = jnp.zeros((1, D), F32)
        for i in range(L):
            dot = dot + dsm[i] * sm[i]
        for i in range(L):
            o_ref[pl.ds(i, 1), :] = sm[i] * (dsm[i] - dot)

    return pl.pallas_call(body, out_shape=jax.ShapeDtypeStruct((L, D), F32), name="lb_bwd")(lbp, dlb)


def _my_pos():
    return lax.axis_index("x"), lax.axis_index("y"), lax.axis_index("c")


def _peer(mask):
    x, y, c = _my_pos()
    mx, my, mc = (mask >> 2) & 1, (mask >> 1) & 1, mask & 1
    px = (1 - x) if mx else x
    py = (1 - y) if my else y
    pc = (1 - c) if mc else c
    return (px, py, pc), 4 * px + 2 * py + pc


HBM_SPEC =pl.BlockSpec(memory_space=pltpu.HBM)
SEM_SPEC = pl.BlockSpec(memory_space=pltpu.SEMAPHORE)
EFFECT = pltpu.SideEffectType.DATAFLOW_SIDE_EFFECTING


def _hbm(a):
    return pltpu.with_memory_space_constraint(a, pltpu.HBM)


def _landing(block_shape, dtype, axis=0):
    if axis == 0:
        return lax.empty((N_DEV,) + tuple(block_shape), dtype)
    rows, n = block_shape
    return lax.empty((rows, N_DEV * n), dtype)


def _slot(ref, i):
    if len(ref.shape) == 2:
        n = ref.shape[1] // N_DEV
        return ref.at[:, pl.ds(i * n, n)]
    return ref.at[i]


def _push_start(name, srcs, lands, whole, groups):
    n = len(srcs)
    ng = 1 + max(groups)
    cnt = [groups.count(g) for g in range(ng)]
    idx = [groups[:a].count(groups[a]) for a in range(n)]

    def body(*refs):
        src_refs, land_refs = refs[:n], refs[n:2 * n]
        sems = refs[2 * n:2 * n + 3 * ng]
        token = refs[-1]
        x, y, c = _my_pos()
        me = 4 * x + 2 * y + c
        for a in range(n):
            g = groups[a]
            for m in range(1, N_DEV):
                peer, pid = _peer(m)
                pltpu.make_async_remote_copy(
                    src_ref=src_refs[a] if whole else src_refs[a].at[pid], dst_ref=_slot(land_refs[a], me),
                    send_sem=sems[3 * g].at[idx[a] * (N_DEV - 1) + m - 1],
                    recv_sem=sems[3 * g + 1].at[idx[a] * (N_DEV - 1) + m - 1],
                    device_id=peer, device_id_type=MESH).start()
            pltpu.make_async_copy(src_refs[a] if whole else src_refs[a].at[me], _slot(land_refs[a], me),
                                  sems[3 * g + 2].at[idx[a]]).start()
        token[...] = jnp.zeros_like(token)

    sem_shapes = []
    for g in range(ng):
        sem_shapes += [pltpu.SemaphoreType.DMA((cnt[g] * (N_DEV - 1),))] * 2 + [pltpu.SemaphoreType.DMA((cnt[g],))]
    thru = [pltpu.HBM(s.shape, s.dtype) for s in list(srcs) + list(lands)]
    res = pl.pallas_call(
        body, name=name,
        out_shape=tuple(sem_shapes + thru + [jax.ShapeDtypeStruct((8, 128), F32)]),
        in_specs=tuple([HBM_SPEC] * (2 * n)),
        out_specs=tuple([SEM_SPEC] * (3 * ng) + [HBM_SPEC] * (2 * n) + [pl.BlockSpec(memory_space=pltpu.VMEM)]),
        input_output_aliases={i: 3 * ng + i for i in range(2 * n)},
        compiler_params=pltpu.CompilerParams(has_side_effects=EFFECT),
    )(*[_hbm(s) for s in srcs], *[_hbm(z) for z in lands])
    sems = [(res[3 * g], res[3 * g + 1], res[3 * g + 2]) for g in range(ng)]
    srcs_thru = list(res[3 * ng:3 * ng + n])
    lands_thru = list(res[3 * ng + n:3 * ng + 2 * n])
    return sems, srcs_thru, lands_thru, res[-1]


def _push_wait(name, srcs_thru, lands_thru, sems, after, whole):
    n = len(srcs_thru)

    def body(*refs):
        src_refs, land_refs = refs[:n], refs[n:2 * n]
        send_sems, recv_sems, own_sems = refs[2 * n], refs[2 * n + 1], refs[2 * n + 2]
        x, y, c = _my_pos()
        me = 4 * x + 2 * y + c
        for a in range(n):
            pltpu.make_async_copy(src_refs[a] if whole else src_refs[a].at[me], _slot(land_refs[a], me),
                                  own_sems.at[a]).wait()
            for m in range(1, N_DEV):
                peer, pid = _peer(m)
                cp = pltpu.make_async_remote_copy(
                    src_ref=src_refs[a] if whole else src_refs[a].at[pid], dst_ref=_slot(land_refs[a], pid),
                    send_sem=send_sems.at[a * (N_DEV - 1) + m - 1], recv_sem=recv_sems.at[a * (N_DEV - 1) + m - 1],
                    device_id=peer, device_id_type=MESH)
                cp.wait_send()
                cp.wait_recv()

    thru = [pltpu.HBM(s.shape, s.dtype) for s in list(srcs_thru) + list(lands_thru)]
    res = pl.pallas_call(
        body, name=name, out_shape=tuple(thru),
        in_specs=tuple([HBM_SPEC] * (2 * n) + [SEM_SPEC, SEM_SPEC, SEM_SPEC, ANY]),
        out_specs=tuple([HBM_SPEC] * (2 * n)),
        input_output_aliases={i: i for i in range(2 * n)},
        compiler_params=pltpu.CompilerParams(has_side_effects=EFFECT),
    )(*srcs_thru, *lands_thru, sems[0], sems[1], sems[2], after)
    return list(res[n:])


def _adamw(recv, w, m, v, layer=0, prev=None):
    L, R, C = w.shape
    tr = _tile(R, max(8, (1 << 18) // C), 8) if R % 8 == 0 else R
    bc1 = 1.0 - ADAM_B1 ** ADAM_STEP
    bc2 = 1.0 - ADAM_B2 ** ADAM_STEP
    if prev is None:
        prev = [lax.empty((L, R, C), F32) for _ in range(4)]

    def body(r_ref, w_ref, m_ref, v_ref, p0, p1, p2, p3, g_ref, d_ref, nm_ref, nv_ref):
        g = r_ref[0].astype(F32)
        for s in range(1, N_DEV):
            g = g + r_ref[s].astype(F32)
        nm = ADAM_B1 * m_ref[...] + (1.0 - ADAM_B1) * g
        nv = ADAM_B2 * v_ref[...] + (1.0 - ADAM_B2) * (g * g)
        mh = nm / bc1
        vh = nv / bc2
        g_ref[...] = g
        d_ref[...] = -ADAM_LR * (mh / (jnp.sqrt(vh) + ADAM_EPS) + ADAM_WD * w_ref[...])
        nm_ref[...] = nm
        nv_ref[...] = nv

    row = pl.BlockSpec((None, tr, C), lambda i: (layer, i, 0))
    return pl.pallas_call(
        body, grid=(R // tr,),
        in_specs=[pl.BlockSpec((N_DEV, tr, C), lambda i: (0, i, 0)), row, row, row] + [ANY] * 4,
        out_specs=[row] * 4, out_shape=[jax.ShapeDtypeStruct((L, R, C), F32)] * 4,
        input_output_aliases={4: 0, 5: 1, 6: 2, 7: 3},
        compiler_params=_cp("parallel"), name="adamw")(recv, w, m, v, *prev)


def _full_w_spec(tk, tn):
    return pl.BlockSpec((tk, tn), lambda i, j, k: (k, j))


def kernel(x, meta_tokens, mix_norm_g, mlp_norm_g, final_norm_g, ev_w_in, ev_conv_w, ev_conv_b, ev_ln_g, ev_ln_b, ev_pool_w, ev_pool_b, ev_pool_scale, ev_w_out, od_w_in, od_gnorm_g, od_w_out, lb_param, mlp_w1, mlp_w2, loss_target, m_meta_tokens, m_mix_norm_g, m_mlp_norm_g, m_final_norm_g, m_ev_w_in, m_ev_conv_w, m_ev_conv_b, m_ev_ln_g, m_ev_ln_b, m_ev_pool_w, m_ev_pool_b, m_ev_pool_scale, m_ev_w_out, m_od_w_in, m_od_gnorm_g, m_od_w_out, m_lb_param, m_mlp_w1, m_mlp_w2, v_meta_tokens, v_mix_norm_g, v_mlp_norm_g, v_final_norm_g, v_ev_w_in, v_ev_conv_w, v_ev_conv_b, v_ev_ln_g, v_ev_ln_b, v_ev_pool_w, v_ev_pool_b, v_ev_pool_scale, v_ev_w_out, v_od_w_in, v_od_gnorm_g, v_od_w_out, v_lb_param, v_mlp_w1, v_mlp_w2):
    S, D = x.shape[1], x.shape[2]
    T = PAD + N_META + S
    DEPTH = mix_norm_g.shape[0]
    DFF = mlp_w1.shape[2] * N_DEV
    dev = 4 * lax.axis_index("x") + 2 * lax.axis_index("y") + lax.axis_index("c")

    n_ev = ev_w_in.shape[0]
    n_od = od_w_in.shape[0]
    n_in_od = od_w_in.shape[2]
    n_w1 = mlp_w1.shape[2]

    ag_src, ag_grp, ag_axis, ag_at = [], [], [], {}
    for key, arr in (("meta", meta_tokens), ("cw", ev_conv_w)):
        ag_at["small", key] = len(ag_src)
        ag_src.append(arr)
        ag_grp.append(len(ag_grp))
        ag_axis.append(0)
    for layer in range(DEPTH):
        j = layer // 2
        mixer = [("in", ev_w_in[j]), ("out", ev_w_out[j])] if layer % 2 == 0 else [("in", od_w_in[j]), ("out", od_w_out[j])]
        for pos, (key, arr) in enumerate(mixer + [("w1", mlp_w1[layer]), ("w2", mlp_w2[layer])]):
            ag_at[layer, key] = len(ag_src)
            ag_src.append(arr.astype(BF16))
            ag_grp.append(len(ag_grp))
            ag_axis.append(1 if key in ("in", "w1") and arr.shape[1] % 128 == 0 else 0)
    ag_sems, ag_s, ag_l, ag_tok = _push_start(
        "ag_start", ag_src, [_landing(s_.shape, s_.dtype, ax) for s_, ax in zip(ag_src, ag_axis)], True, ag_grp)

    def ag_wait(layer, key, after):
        a = ag_at[layer, key]
        return _push_wait(f"ag_wait_{a}", [ag_s[a]], [ag_l[a]], ag_sems[a], after, True)[0]

    g_meta = ag_wait("small", "meta", ag_tok)
    g_cw = ag_wait("small", "cw", ag_tok)
    meta_full = jnp.transpose(g_meta, (1, 0, 2)).reshape(N_META, D)
    cw_full = jnp.transpose(g_cw, (1, 2, 0, 3)).reshape(n_ev, CONV_WIDTH, -1)
    cw_pad = jnp.pad(cw_full, ((0, 0), (0, 32 - CONV_WIDTH), (0, 0)))

    h = jnp.concatenate([jnp.zeros((PAD, D), F32), meta_full, x[0]], axis=0) + ag_tok[0, 0]
    tgt = jnp.pad(loss_target[0], ((PAD + N_META, 0), (0, 0)))
    lb_all = _lb_fwd(lb_param)
    h, tgt, lb_all = lax.optimization_barrier((h, tgt, lb_all))

    tm_big = _tile(T, MM_ROWS_BIG, 16)
    tm_mid = _tile(T, MM_ROWS_MID, 16)
    tm_k4 = _tile(T, MM_ROWS_K4, 16)

    saved = []
    for layer in range(DEPTH):
        j = layer // 2
        sv = {"h0": h}
        g_in = ag_wait(layer, "in", h)
        w_in = g_in if g_in.ndim == 2 else jnp.transpose(g_in, (1, 0, 2)).reshape(D, -1)
        if layer % 2 == 0:
            sv["n"], u = _mm_rms_nn("ev_in", h, mix_norm_g[layer][None], w_in, tm_big, 512, "f32")
            yab, yc = _ev_fwd(u, cw_pad[j], ev_conv_b[j][None], ev_ln_g[j][None], ev_ln_b[j][None],
                              ev_pool_w[j].astype(BF16), ev_pool_b[j].reshape(1, -1), ev_pool_scale[j][None])
            sv.update(u=u, y=yab, yc=yc)
            w_out = ag_wait(layer, "out", yab).reshape(-1, D)
            h = _mm_nn("ev_out", yab, w_out, _full_w_spec, T, D, D, tm_mid, D, D, "resid", extra=h)
        else:
            sv["n"], u = _mm_rms_nn("od_in", h, mix_norm_g[layer][None], w_in, tm_big, 512, "f32")
            y, o, s0 = _hgrn_fwd(u, lb_all[layer][None], od_gnorm_g[j][None])
            sv.update(u=u, y=y, o=o, s0=s0)
            w_out = ag_wait(layer, "out", y).reshape(-1, D)
            h = _mm_nn("od_out", y, w_out, _full_w_spec, T, D, D, tm_mid, D, D, "resid", extra=h)
        sv["h1"] = h
        w_w1 = ag_wait(layer, "w1", h)
        n2, r, act = _mm_rms_nn("mlp_w1", h, mlp_norm_g[layer][None], w_w1, tm_big, 512, "relu2")
        w_w2 = ag_wait(layer, "w2", act).reshape(DFF, D)
        sv.update(w_in=w_in, w_out=w_out, w_w1=w_w1, w_w2=w_w2)
        sv.update(n2=n2, r=r, act=act)
        h = _mm_nn("mlp_w2", act, w_w2, _full_w_spec, T, D, DFF, tm_k4, D, DFF, "resid", extra=h)
        saved.append(sv)

    loss_blk, dh, dhb, dg_final = _loss_head(h, final_norm_g[None], tgt)
    loss = lax.psum(loss_blk[0, 0], AXES)

    tt = T
    g_mix, g_mlp = [None] * DEPTH, [None] * DEPTH
    small ={"cw": [None] * n_ev, "vec": [None] * n_ev, "pw": [None] * n_ev, "gn": [None] * n_od}
    dlb_rows = [jnp.zeros((1, D), F32) for _ in range(DEPTH)]

    def xs2(tt_, tk):
        return pl.BlockSpec((tt_, tk), lambda a, b, t: (t, a))

    def ys2(tt_, tn):
        return pl.BlockSpec((tt_, tn), lambda a, b, t: (t, b))

    def os2(tk, tn):
        return pl.BlockSpec((tk, tn), lambda a, b, t: (a, b))

    def os3(tk, tn):
        return pl.BlockSpec((None, tk, tn), lambda a, b, t: (b, a, 0))

    def dy2(tm, tn):
        return pl.BlockSpec((tm, tn), lambda i, jj, k: (i, k))

    def w_rows(tj, tn):
        return pl.BlockSpec((tj, tn), lambda i, jj, k: (jj, k))

    def w_whole(tj, tn):
        return pl.BlockSpec((tj, tn), lambda i, jj, k: (0, 0), pipeline_mode=pl.Buffered(1))

    rs_pending = []

    def rs_start(tag, mats):
        blocks = [m_ if m_.ndim == 3 else m_.reshape(N_DEV, m_.shape[0] // N_DEV, m_.shape[1]) for m_ in mats]
        lands = [_landing(b_.shape[1:], b_.dtype) for b_ in blocks]
        sems, s_thru, l_thru, tok = _push_start(f"rs_start_{tag}", blocks, lands, False, [0] * len(blocks))
        rs_pending.append((tag, s_thru, l_thru, sems[0]))
        return tok[0, 0]

    for layer in reversed(range(DEPTH)):
        j = layer // 2
        sv = saved[layer]
        da1 = _mm_nt("mlp_w2_t", dhb, sv["w_w2"], dy2, w_rows, T, DFF, D, tm_mid, 1024, D, "dact", extra=sv["r"])
        dw2 = _mm_tn("mlp_dw2", sv["act"], dhb, xs2, ys2, os2, (DFF, D), T, DFF, D, tt, 512, D)
        dw1 = _mm_tn("mlp_dw1", sv["n2"], da1, xs2, ys2, os3, (N_DEV, D, n_w1), T, D, DFF, tt, D, n_w1)
        tok = rs_start(f"mlp{layer}", [dw1, dw2])
        dh, dhb, g_mlp[layer] = _mm_nt("mlp_w1_t", da1, sv["w_w1"], dy2, w_whole, T, D, DFF, tm_k4, D, DFF, "rms",
                                       extra=(sv["h1"], mlp_norm_g[layer][None] + tok, dh))
        if layer % 2 == 0:
            dyab = _mm_nt("ev_out_t", dhb, sv["w_out"], dy2, w_rows, T, D, D, tm_mid, D, D, "f32")
            dwout = _mm_tn("ev_dwout", sv["y"], dhb, xs2, ys2, os2, (D, D), T, D, D, tt, 512, D)
            du, small["cw"][j], small["vec"][j], small["pw"][j] = _ev_bwd(
                dyab, sv["yc"], sv["u"], cw_pad[j], ev_ln_g[j][None], ev_ln_b[j][None], ev_pool_w[j].astype(BF16),
                jnp.transpose(ev_pool_w[j], (0, 2, 1)).astype(BF16), ev_pool_b[j].reshape(1, -1),
                ev_pool_scale[j][None])
            nin = du.shape[1]
            dwin = _mm_tn("ev_dwin", sv["n"], du, xs2, ys2, os2, (D, nin), T, D, nin, tt, D, 512)
            dwin = jnp.transpose(dwin.reshape(D, N_DEV, nin // N_DEV), (1, 0, 2))
            tok = rs_start(f"mix{layer}", [dwin, dwout])
            dh, dhb, g_mix[layer] = _mm_nt("ev_in_t", du, sv["w_in"], dy2, w_whole, T, D, nin, tm_k4, D, nin, "rms",
                                           extra=(sv["h0"], mix_norm_g[layer][None] + tok, dh))
        else:
            dy = _mm_nt("od_out_t", dhb, sv["w_out"], dy2, w_rows, T, D, D, tm_mid, D, D, "f32")
            dwout = _mm_tn("od_dwout", sv["y"], dhb, xs2, ys2, os2, (D, D), T, D, D, tt, 512, D)
            du3, dlb_rows[layer], small["gn"][j] = _hgrn_bwd(dy, sv["o"], sv["s0"], sv["u"], lb_all[layer][None],
                                                              od_gnorm_g[j][None])
            per = D // n_in_od

            def du_t(tt_, tn):
                return pl.BlockSpec((None, tt_, tn), lambda a, b, t: (b // per, t, b % per))

            dwin = _mm_tn("od_dwin", sv["n"], du3, xs2, du_t, os3, (N_DEV, D, n_in_od), T, D, 4 * D, tt, D, n_in_od)
            tok = rs_start(f"mix{layer}", [dwin, dwout])
            dh, dhb, g_mix[layer] = _mm_nt(
                "od_in_t", du3, sv["w_in"], lambda tm, tn: pl.BlockSpec((4, tm, tn // 4), lambda i, jj, k: (0, i, 0)),
                w_whole, T, D, 4 * D, tm_k4, D, 4 * D, "rms", extra=(sv["h0"], mix_norm_g[layer][None] + tok, dh),
                parts=4)

    dmeta = dh[PAD:PAD + N_META]
    grad_x = dh[PAD + N_META:][None]
    dlb_param = _lb_bwd(lb_param, jnp.concatenate(dlb_rows, axis=0))

    pieces = [
        ("final", dg_final), ("pad", jnp.zeros((SMALL_F32_ROWS - 1, D), F32)),
        ("meta", dmeta), ("mix", jnp.concatenate(g_mix, 0)), ("mlp", jnp.concatenate(g_mlp, 0)),
        ("cw", jnp.stack([c[:CONV_WIDTH] for c in small["cw"]])), ("cb", jnp.stack([v_[0] for v_ in small["vec"]])),
        ("lng", jnp.stack([v_[1] for v_ in small["vec"]])), ("lnb", jnp.stack([v_[2] for v_ in small["vec"]])),
        ("pw", jnp.stack(small["pw"])), ("pb", jnp.stack([v_[4] for v_ in small["vec"]])),
        ("ps", jnp.stack([v_[3] for v_ in small["vec"]])), ("gn", jnp.stack([jnp.sum(g_, axis=0)[0] for g_ in small["gn"]])),
        ("lb", dlb_param),
    ]
    flat = jnp.concatenate([p.reshape(-1) for _, p in pieces])
    n_small = flat.shape[0]
    rows_small = SMALL_F32_ROWS + -(-(n_small // 1024 + 1 - SMALL_F32_ROWS) // 16) * 16
    flat = jnp.pad(flat, (0, rows_small * 1024 - n_small)).reshape(rows_small, 1024)

    sm_src = [flat[:SMALL_F32_ROWS], flat[SMALL_F32_ROWS:].astype(BF16)]
    sm_sems, sm_s, sm_l, sm_tok = _push_start("small_start", sm_src, [_landing(a_.shape, a_.dtype) for a_ in sm_src],
                                              True, [0, 0])
    recv = {}
    for tag, s_thru, l_thru, sems in rs_pending:
        got = _push_wait(f"rs_wait_{tag}", s_thru, l_thru, sems, sm_tok, False)
        layer = int(tag[3:])
        if tag.startswith("mlp"):
            recv["w1", layer], recv["w2", layer] = got
        else:
            key = "ev" if layer % 2 == 0 else "od"
            recv[key + "_in", layer // 2], recv[key + "_out", layer // 2] = got

    outs = {}
    big = {"ev_in": ("ev_w_in", ev_w_in, m_ev_w_in, v_ev_w_in), "ev_out": ("ev_w_out", ev_w_out, m_ev_w_out, v_ev_w_out),
           "od_in": ("od_w_in", od_w_in, m_od_w_in, v_od_w_in), "od_out": ("od_w_out", od_w_out, m_od_w_out, v_od_w_out),
           "w1": ("mlp_w1", mlp_w1, m_mlp_w1, v_mlp_w1), "w2": ("mlp_w2", mlp_w2, m_mlp_w2, v_mlp_w2)}
    for key, (name, w, m, v) in big.items():
        res = None
        for l in range(w.shape[0]):
            res = _adamw(recv[key, l], w, m, v, layer=l, prev=res)
        outs[name] = res

    small_params = {
        "meta": ("meta_tokens", None), "mix": ("mix_norm_g", mix_norm_g, m_mix_norm_g, v_mix_norm_g),
        "mlp": ("mlp_norm_g", mlp_norm_g, m_mlp_norm_g, v_mlp_norm_g),
        "final": ("final_norm_g", final_norm_g, m_final_norm_g, v_final_norm_g),
        "cw": ("ev_conv_w", None), "cb": ("ev_conv_b", ev_conv_b, m_ev_conv_b, v_ev_conv_b),
        "lng": ("ev_ln_g", ev_ln_g, m_ev_ln_g, v_ev_ln_g), "lnb": ("ev_ln_b", ev_ln_b, m_ev_ln_b, v_ev_ln_b),
        "pw": ("ev_pool_w", ev_pool_w, m_ev_pool_w, v_ev_pool_w), "pb": ("ev_pool_b", ev_pool_b, m_ev_pool_b, v_ev_pool_b),
        "ps": ("ev_pool_scale", ev_pool_scale, m_ev_pool_scale, v_ev_pool_scale),
        "gn": ("od_gnorm_g", od_gnorm_g, m_od_gnorm_g, v_od_gnorm_g), "lb": ("lb_param", lb_param, m_lb_param, v_lb_param),
    }
    csh = ev_conv_w.shape[2]
    msh = meta_tokens.shape[1]

    def packed(which):
        parts = []
        for key, g_ in pieces:
            ent = small_params.get(key)
            if key == "pad":
                full = g_
            elif key == "meta":
                src = (meta_tokens, m_meta_tokens, v_meta_tokens)[which]
                full = lax.dynamic_update_slice(jnp.zeros((N_META, D), F32), src, (0, dev * msh))
            elif key == "cw":
                src = (ev_conv_w, m_ev_conv_w, v_ev_conv_w)[which]
                full = lax.dynamic_update_slice(jnp.zeros(g_.shape, F32), src, (0, 0, dev * csh))
            else:
                full = ent[1 + which]
            parts.append(full.reshape(-1))
        f = jnp.concatenate(parts)
        return jnp.pad(f, (0, rows_small * 1024 - n_small)).reshape(rows_small, 1024)

    got_f32, got_bf16 = _push_wait("small_wait", sm_s, sm_l, sm_sems[0], outs["mlp_w2"][0], True)
    recv_small = jnp.concatenate([got_f32, got_bf16.astype(F32)], axis=1)
    sres = [r_[0] for r_ in _adamw(recv_small, packed(0)[None], packed(1)[None], packed(2)[None])]
    off = 0
    for key, g_ in pieces:
        size = g_.size
        vals = [r_.reshape(-1)[off:off + size].reshape(g_.shape) for r_ in sres]
        off += size
        if key == "pad":
            continue
        name = small_params[key][0]
        if key == "meta":
            vals = [lax.dynamic_slice(v_, (0, dev * msh), (N_META, msh)) for v_ in vals]
        elif key == "cw":
            vals = [lax.dynamic_slice(v_, (0, 0, dev * csh), v_.shape[:2] + (csh,)) for v_ in vals]
        else:
            vals = [v_.reshape(small_params[key][1].shape) for v_ in vals]
        outs[name] = vals

    names = ["meta_tokens", "mix_norm_g", "mlp_norm_g", "final_norm_g", "ev_w_in", "ev_conv_w", "ev_conv_b", "ev_ln_g",
             "ev_ln_b", "ev_pool_w", "ev_pool_b", "ev_pool_scale", "ev_w_out", "od_w_in", "od_gnorm_g", "od_w_out",
             "lb_param", "mlp_w1", "mlp_w2"]
    result = [loss, grad_x]
    for k in range(4):
        result += [outs[nm][k] for nm in names]
    return tuple(result)
```

```python
import functools

import jax
import jax.numpy as jnp
from jax import lax
from jax.experimental import pallas as pl
from jax.experimental.pallas import tpu as pltpu

F32 = jnp.float32
BF16 = jnp.bfloat16

N_DEV = 8
N_META = 16
CHUNK = 64
PAD = CHUNK - N_META
SUB = 16
HEAD = 128
CONV_WIDTH = 31
HALO = 32
POOL_WINDOWS = (2, 4, 8, 16)
EPS = 1e-6
NEG = -1e30
ADAM_LR, ADAM_B1, ADAM_B2, ADAM_EPS, ADAM_WD, ADAM_STEP = 0.001, 0.9, 0.999, 1e-08, 0.01, 10
VMEM_LIMIT = 56 * 1024 * 1024
EV_ROWS = 416
EV_ROWS_FWD = 832
HGRN_ROWS = 832
MM_ROWS_BIG = 2080
MM_ROWS_MID = 1040
MM_ROWS_K4 = 416
SMALL_F32_ROWS = 8
HGRN_HEADS_FWD = 8
HGRN_HEADS_BWD = 2
MESH = pl.DeviceIdType.MESH
AXES = ("x", "y", "c")
ANY = pl.BlockSpec(memory_space=pl.ANY)


def _cp(*sem):
    return pltpu.CompilerParams(dimension_semantics=sem, vmem_limit_bytes=VMEM_LIMIT)


def _tile(n, cap, mult):
    best = None
    for d in range(mult, min(n, cap) + 1, mult):
        if n % d == 0:
            best = d
    assert best is not None, (n, cap, mult)
    return best


def _nt(a, b):
    return lax.dot_general(a, b, (((1,), (1,)), ((), ())), preferred_element_type=F32)


def _tn(a, b):
    return lax.dot_general(a, b, (((0,), (0,)), ((), ())), preferred_element_type=F32)


def _nn(a, b):
    return jnp.dot(a, b, preferred_element_type=F32)


def _r16(x):
    return x.astype(BF16).astype(F32)


def _row_ids(base, n):
    return base + lax.broadcasted_iota(jnp.int32, (n, 1), 0)


def _dsilu(x, s):
    return s * (1.0 + x * (1.0 - s))


def _loss_head(h, g, tgt):
    T, D = h.shape
    tm = _tile(T, MM_ROWS_MID, 16)
    first_x = PAD + N_META

    def body(h_ref, g_ref, t_ref, loss_ref, dh_ref, dhb_ref, dg_ref):
        i = pl.program_id(0)
        x = h_ref[...]
        r = lax.rsqrt(jnp.mean(x * x, axis=-1, keepdims=True) + EPS)
        xh = x * r
        gv = g_ref[...]
        out = xh * gv
        valid = _row_ids(i * tm, tm) >= first_x
        e = jnp.where(valid, out - t_ref[...], 0.0)
        dout = e * (1.0 / D)
        dxh = dout * gv
        dx = r * (dxh - xh * jnp.mean(dxh * xh, axis=-1, keepdims=True))
        dh_ref[...] = dx
        dhb_ref[...] = dx.astype(BF16)

        @pl.when(i == 0)
        def _():
            dg_ref[...] = jnp.zeros_like(dg_ref)
            loss_ref[...] = jnp.zeros_like(loss_ref)

        dg_ref[...] += jnp.sum(dout * xh, axis=0, keepdims=True)
        loss_ref[...] += 0.5 * jnp.sum(jnp.mean(e * e, axis=-1, keepdims=True))

    row = pl.BlockSpec((tm, D), lambda i: (i, 0))
    vec = pl.BlockSpec((1, D), lambda i: (0, 0))
    return pl.pallas_call(
        body, grid=(T // tm,),
        in_specs=[row, vec, row],
        out_specs=[pl.BlockSpec((8, 128), lambda i: (0, 0)), row, row, vec],
        out_shape=[jax.ShapeDtypeStruct((8, 128), F32), jax.ShapeDtypeStruct((T, D), F32),
                   jax.ShapeDtypeStruct((T, D), BF16), jax.ShapeDtypeStruct((1, D), F32)],
        compiler_params=_cp("arbitrary"), name="loss_head")(h, g, tgt)


def _mm_nn(name, a, w, w_spec, M, N, K, tm, tn, tk, mode, extra=None, a_spec=None):
    nk = K // tk
    if a_spec is None:
        a_spec = pl.BlockSpec((tm, tk), lambda i, j, k: (i, k))
    o_spec = pl.BlockSpec((tm, tn), lambda i, j, k: (i, j))

    def body(*refs):
        if mode == "resid":
            a_ref, w_ref, e_ref = refs[:3]
            outs = refs[3:]
        else:
            a_ref, w_ref = refs[:2]
            outs = refs[2:]
        acc_ref = outs[-1] if nk > 1 else None
        part = _nn(a_ref[...], w_ref[...])

        def finish(acc):
            if mode == "f32":
                outs[0][...] = acc
            elif mode == "relu2":
                r = jnp.maximum(acc, 0.0)
                outs[0][...] = r.astype(BF16)
                outs[1][...] = (r * r).astype(BF16)
            else:
                keep = _row_ids(pl.program_id(0) * tm, tm) >= PAD
                outs[0][...] = jnp.where(keep, e_ref[...] + acc, 0.0)

        if nk == 1:
            finish(part)
        else:
            k = pl.program_id(2)

            @pl.when(k == 0)
            def _():
                acc_ref[...] = part

            @pl.when(k > 0)
            def _():
                acc_ref[...] += part

            @pl.when(k == nk - 1)
            def _():
                finish(acc_ref[...])

    in_specs = [a_spec, w_spec(tk, tn)]
    args = [a, w]
    if mode == "resid":
        in_specs.append(o_spec)
        args.append(extra)
    if mode == "relu2":
        out_specs = [o_spec, o_spec]
        out_shape = [jax.ShapeDtypeStruct((M, N), BF16)] * 2
    else:
        out_specs = [o_spec]
        out_shape = [jax.ShapeDtypeStruct((M, N), F32)]
    scratch = [pltpu.VMEM((tm, tn), F32)] if nk > 1 else []
    res = pl.pallas_call(
        body, grid=(M // tm, N // tn, nk), in_specs=in_specs, out_specs=out_specs, out_shape=out_shape,
        scratch_shapes=scratch, compiler_params=_cp("parallel", "parallel", "arbitrary"), name=name)(*args)
    return res if mode == "relu2" else res[0]


def _mm_rms_nn(name, h, g, w, tm, tn, mode):
    M, K = h.shape
    N = w.shape[1]

    def body(h_ref, g_ref, w_ref, n_ref, *outs):
        @pl.when(pl.program_id(1) == 0)
        def _():
            ch = _tile(tm, 256, 16)

            def chunk(c, carry):
                rows = pl.ds(pl.multiple_of(c * ch, ch), ch)
                x = h_ref[rows, :]
                r = lax.rsqrt(jnp.mean(x * x, axis=-1, keepdims=True) + EPS)
                n_ref[rows, :] = ((x * r) * g_ref[...]).astype(BF16)
                return carry

            lax.fori_loop(0, tm // ch, chunk, 0)

        acc = _nn(n_ref[...], w_ref[...])
        if mode == "f32":
            outs[0][...] = acc
        else:
            r = jnp.maximum(acc, 0.0)
            outs[0][...] = r.astype(BF16)
            outs[1][...] = (r * r).astype(BF16)

    row = pl.BlockSpec((tm, K), lambda i, j: (i, 0))
    o_spec = pl.BlockSpec((tm, tn), lambda i, j: (i, j))
    n_out = 1 if mode == "f32" else 2
    return pl.pallas_call(
        body, grid=(M // tm, N // tn),
        in_specs=[row, pl.BlockSpec((1, K), lambda i, j: (0, 0)), pl.BlockSpec((K, tn), lambda i, j: (0, j))],
        out_specs=[row] + [o_spec] * n_out,
        out_shape=[jax.ShapeDtypeStruct((M, K), BF16)] + [jax.ShapeDtypeStruct((M, N), F32 if mode == "f32" else BF16)] * n_out,
        compiler_params=_cp("parallel", "arbitrary"), name=name)(h, g, w)


def _mm_nt(name, dy, w, dy_spec, w_spec, M, J, N, tm, tj, tn, mode, extra=None, parts=1):
    nk = N // tn
    o_spec = pl.BlockSpec((tm, tj), lambda i, j, k: (i, j))
    n_extra = {"f32": 0, "dact": 1, "rms": 3}[mode]
    if mode == "rms":
        assert nk == 1 and tj == J

    def body(*refs):
        dy_ref, w_ref = refs[:2]
        ex = refs[2:2 + n_extra]
        outs = refs[2 + n_extra:]
        acc_ref = outs[-1] if nk > 1 else None
        if parts == 1:
            part = _nt(dy_ref[...], w_ref[...])
        else:
            wq = tn // parts
            part = _nt(dy_ref[0], w_ref[:, 0:wq])
            for q in range(1, parts):
                part = part + _nt(dy_ref[q], w_ref[:, q * wq:(q + 1) * wq])

        def finish(acc):
            if mode == "f32":
                outs[0][...] = acc
            elif mode == "dact":
                outs[0][...] = (acc * (2.0 * ex[0][...].astype(F32))).astype(BF16)
            else:
                h_ref, g_ref, dres_ref = ex
                dh_ref, dhb_ref, dg_ref = outs[:3]
                i = pl.program_id(0)

                @pl.when(i == 0)
                def _():
                    dg_ref[...] = jnp.zeros_like(dg_ref)

                ch = _tile(tm, 256, 16)
                for c0 in range(0, tm, ch):
                    a_c = acc[c0:c0 + ch]
                    x = h_ref[c0:c0 + ch, :]
                    r = lax.rsqrt(jnp.mean(x * x, axis=-1, keepdims=True) + EPS)
                    xh = x * r
                    dxh = a_c * g_ref[...]
                    dx = r * (dxh - xh * jnp.mean(dxh * xh, axis=-1, keepdims=True))
                    keep = _row_ids(i * tm + c0, ch) >= PAD
                    dh = jnp.where(keep, dres_ref[c0:c0 + ch, :] + dx, 0.0)
                    dh_ref[c0:c0 + ch, :] = dh
                    dhb_ref[c0:c0 + ch, :] = dh.astype(BF16)
                    dg_ref[...] += jnp.sum(a_c * xh, axis=0, keepdims=True)

        if nk == 1:
            finish(part)
        else:
            k = pl.program_id(2)

            @pl.when(k == 0)
            def _():
                acc_ref[...] = part

            @pl.when(k > 0)
            def _():
                acc_ref[...] += part

            @pl.when(k == nk - 1)
            def _():
                finish(acc_ref[...])

    in_specs = [dy_spec(tm, tn), w_spec(tj, tn)]
    args = [dy, w]
    scratch = [pltpu.VMEM((tm, tj), F32)] if nk > 1 else []
    if mode == "rms":
        vec = pl.BlockSpec((1, J), lambda i, j, k: (0, 0))
        h, g, dres = extra
        res = pl.pallas_call(
            body, grid=(M // tm, 1, 1), in_specs=in_specs + [o_spec, vec, o_spec], out_specs=[o_spec, o_spec, vec],
            out_shape=[jax.ShapeDtypeStruct((M, J), F32), jax.ShapeDtypeStruct((M, J), BF16),
                       jax.ShapeDtypeStruct((1, J), F32)],
            compiler_params=_cp("arbitrary", "arbitrary", "arbitrary"), name=name)(*args, h, g, dres)
        return res
    if mode == "dact":
        in_specs.append(o_spec)
        args.append(extra)
    return pl.pallas_call(
        body, grid=(M // tm, J // tj, nk), in_specs=in_specs, out_specs=[o_spec],
        out_shape=[jax.ShapeDtypeStruct((M, J), BF16 if mode == "dact" else F32)],
        scratch_shapes=scratch, compiler_params=_cp("parallel", "parallel", "arbitrary"), name=name)(*args)[0]


def _mm_tn(name, x, dy, x_spec, dy_spec, o_spec, o_shape, T, K, N, tt, tk, tn):
    nt = T // tt

    def body(x_ref, dy_ref, o_ref, *acc):
        part = _tn(x_ref[...], dy_ref[...])
        if nt == 1:
            o_ref[...] = part.astype(BF16)
            return
        acc_ref = acc[0]
        t = pl.program_id(2)

        @pl.when(t == 0)
        def _():
            acc_ref[...] = part

        @pl.when(t > 0)
        def _():
            acc_ref[...] += part

        @pl.when(t == nt - 1)
        def _():
            o_ref[...] = acc_ref[...].astype(BF16)

    return pl.pallas_call(
        body, grid=(K // tk, N // tn, nt), in_specs=[x_spec(tt, tk), dy_spec(tt, tn)], out_specs=o_spec(tk, tn),
        out_shape=jax.ShapeDtypeStruct(o_shape, BF16), scratch_shapes=[pltpu.VMEM((tk, tn), F32)] if nt > 1 else [],
        compiler_params=_cp("parallel", "parallel", "arbitrary"), name=name)(x, dy)


def _pool_counts(base, n, w):
    pos = _row_ids(base, n) - PAD
    return jnp.clip(pos + 1, 1, w).astype(F32)


def _shifted_copies(buf, rows):
    buf[0, rows:rows + 8, :] = jnp.zeros((8, buf.shape[2]), F32)

    def blk(s, carry):
        b = pl.multiple_of(s * HALO, HALO)
        win = buf[0, pl.ds(b, HALO + 8), :]
        for r in range(1, 8):
            buf[r, pl.ds(b, HALO), :] = win[r:r + HALO]
        return carry

    lax.fori_loop(0, rows // HALO, blk, 0)


def _ev_fwd(u, cw, cb, lg, lb, pw, pb, ps):
    T = u.shape[0]
    C = 512
    tm = _tile(T, EV_ROWS_FWD, HALO)
    nsub = tm // HALO
    hb = tm // HALO

    def body(val_ref, gate_ref, pin_ref, valh_ref, gateh_ref, pinh_ref, cw_ref, cb_ref, lg_ref, lb_ref, pw_ref,
             pb_ref, ps_ref, yab_ref, yc_ref, a_ext, p_ext, d_buf):
        i = pl.program_id(0)
        nf = (i > 0).astype(F32)
        a_ext[0, 0:HALO, :] = valh_ref[...] * jax.nn.sigmoid(gateh_ref[...]) * nf
        a_ext[0, HALO:HALO + tm, :] = val_ref[...] * jax.nn.sigmoid(gate_ref[...])
        p_ext[0:HALO, :] = pinh_ref[...] * nf
        p_ext[HALO:, :] = pin_ref[...]
        _shifted_copies(a_ext, tm + HALO)

        def sub(s, carry):
            base = pl.multiple_of(s * HALO, HALO)
            acc = jnp.zeros((HALO, C), F32) + cb_ref[...]
            for j in range(CONV_WIDTH):
                off = 2 + j
                acc = acc + cw_ref[pl.ds(j, 1), :] * a_ext[off % 8, pl.ds(pl.multiple_of(base + off // 8 * 8, 8), HALO), :]
            yc_ref[pl.ds(base, HALO), :] = acc
            mu = jnp.mean(acc, axis=-1, keepdims=True)
            yc = acc - mu
            rstd = lax.rsqrt(jnp.mean(yc * yc, axis=-1, keepdims=True) + EPS)
            z = (yc * rstd) * lg_ref[...] + lb_ref[...]
            yab_ref[pl.ds(base, HALO), 0:C] = (z * jax.nn.sigmoid(z)).astype(BF16)
            pwin = p_ext[pl.ds(base, 2 * HALO), :]
            for gi, w in enumerate(POOL_WINDOWS):
                lo, hi = gi * HEAD, (gi + 1) * HEAD
                x = pwin[HALO:, lo:hi]
                tot = x
                for k in range(1, w):
                    tot = tot + pwin[HALO - k:2 * HALO - k, lo:hi]
                cnt = _pool_counts(i * tm + base, HALO, w)
                d_buf[pl.ds(base, HALO), lo:hi] = (tot / cnt - x).astype(BF16)
            return carry

        lax.fori_loop(0, nsub, sub, 0, unroll=2)
        for gi in range(len(POOL_WINDOWS)):
            lo, hi = gi * HEAD, (gi + 1) * HEAD
            y = _nn(d_buf[:, lo:hi], pw_ref[gi]) + pb_ref[:, lo:hi]
            yab_ref[:, C + lo:C + hi] = (y * ps_ref[:, lo:hi]).astype(BF16)

    def main(c):
        return pl.BlockSpec((tm, C), lambda i: (i, c))

    def halo(c):
        return pl.BlockSpec((HALO, C), lambda i: (jnp.maximum(i * hb - 1, 0), c))

    vec = pl.BlockSpec((1, C), lambda i: (0, 0))
    return pl.pallas_call(
        body, grid=(T // tm,),
        in_specs=[main(0), main(1), main(2), halo(0), halo(1), halo(2),
                  pl.BlockSpec((32, C), lambda i: (0, 0)), vec, vec, vec,
                  pl.BlockSpec((4, HEAD, HEAD), lambda i: (0, 0, 0)), vec, vec],
        out_specs=[pl.BlockSpec((tm, 2 * C), lambda i: (i, 0)), pl.BlockSpec((tm, C), lambda i: (i, 0))],
        out_shape=[jax.ShapeDtypeStruct((T, 2 * C), BF16), jax.ShapeDtypeStruct((T, C), F32)],
        scratch_shapes=[pltpu.VMEM((8, tm + HALO + 8, C), F32), pltpu.VMEM((tm + HALO, C), F32),
                        pltpu.VMEM((tm, C), BF16)],
        compiler_params=_cp("parallel"), name="ev_fwd")(u, u, u, u, u, u, cw, cb, lg, lb, pw, pb, ps)


def _ev_bwd(dyab, yc, u, cw, lg, lb, pw, pwt, pb, ps):
    T = u.shape[0]
    C = 512
    tm = _tile(T, EV_ROWS, HALO)
    nsub = tm // HALO
    hb = tm // HALO
    nblk = T // tm
    E = tm + HALO

    def body(dya_ref, dyb_ref, dyah_ref, dybh_ref, yc_ref, ych_ref, val_ref, gate_ref, pin_ref, valh_ref, gateh_ref,
             pinh_ref, cw_ref, lg_ref, lb_ref, pw_ref, pwt_ref, pb_ref, ps_ref,
             du_ref, dcw_ref, dvec_ref, dpw_ref,
             dy_ext, a_ext, p_ext, ddc_ext, dd_buf, d_buf, dpre_buf, dcw_acc, vec_acc):
        i = pl.program_id(0)
        nf = (i > 0).astype(F32)
        nl = (i < nblk - 1).astype(F32)

        @pl.when(i == 0)
        def _():
            dcw_ref[...] = jnp.zeros_like(dcw_ref)
            dvec_ref[...] = jnp.zeros_like(dvec_ref)
            dpw_ref[...] = jnp.zeros_like(dpw_ref)

        dcw_acc[...] = jnp.zeros_like(dcw_acc)
        vec_acc[...] = jnp.zeros_like(vec_acc)
        a_ext[0, 0:HALO, :] = valh_ref[...] * jax.nn.sigmoid(gateh_ref[...]) * nf
        a_ext[0, HALO:E, :] = val_ref[...] * jax.nn.sigmoid(gate_ref[...])
        p_ext[0:HALO, :] = pinh_ref[...] * nf
        p_ext[HALO:, :] = pin_ref[...]
        _shifted_copies(a_ext, E)

        def ln_bwd(y, dya, main):
            mu = jnp.mean(y, axis=-1, keepdims=True)
            ycen = y - mu
            rstd = lax.rsqrt(jnp.mean(ycen * ycen, axis=-1, keepdims=True) + EPS)
            yh = ycen * rstd
            z = yh * lg_ref[...] + lb_ref[...]
            sz = jax.nn.sigmoid(z)
            dz = dya * _dsilu(z, sz)
            dyh = dz * lg_ref[...]
            dy = rstd * (dyh - jnp.mean(dyh, axis=-1, keepdims=True) - yh * jnp.mean(dyh * yh, axis=-1, keepdims=True))
            if main:
                vec_acc[1] += jnp.sum((dz * yh).reshape(HALO // 8, 8, C), axis=0)
                vec_acc[2] += jnp.sum(dz.reshape(HALO // 8, 8, C), axis=0)
                vec_acc[0] += jnp.sum(dy.reshape(HALO // 8, 8, C), axis=0)
            return dy

        def pool_dd(dyb, base, main):
            dpre = dyb * ps_ref[...]
            for gi, w in enumerate(POOL_WINDOWS):
                lo, hi = gi * HEAD, (gi + 1) * HEAD
                dd = _nn(dpre[:, lo:hi].astype(BF16), pwt_ref[gi])
                cnt = _pool_counts(i * tm + base, HALO, w)
                ddc_ext[pl.ds(base, HALO), lo:hi] = dd / cnt
                if main:
                    dd_buf[pl.ds(base, HALO), lo:hi] = dd
            if main:
                dpre_buf[pl.ds(base, HALO), :] = dpre.astype(BF16)
                vec_acc[4] += jnp.sum(dpre.reshape(HALO // 8, 8, C), axis=0)

        def p1(s, carry):
            base = pl.multiple_of(s * HALO, HALO)
            dy_ext[0, pl.ds(base, HALO), :] = ln_bwd(yc_ref[pl.ds(base, HALO), :], dya_ref[pl.ds(base, HALO), :], True)
            pool_dd(dyb_ref[pl.ds(base, HALO), :], base, True)
            return carry

        lax.fori_loop(0, nsub, p1, 0, unroll=2)
        dy_ext[0, tm:E, :] = ln_bwd(ych_ref[...], dyah_ref[...], False) * nl
        _shifted_copies(dy_ext, E)
        dpre_h = dybh_ref[...] * ps_ref[...] * nl
        for gi, w in enumerate(POOL_WINDOWS):
            lo, hi = gi * HEAD, (gi + 1) * HEAD
            dd = _nn(dpre_h[:, lo:hi].astype(BF16), pwt_ref[gi])
            ddc_ext[tm:, lo:hi] = dd / _pool_counts(i * tm + tm, HALO, w)

        def p2(s, carry):
            base = pl.multiple_of(s * HALO, HALO)
            dy_m = dy_ext[0, pl.ds(base, HALO), :]
            da = jnp.zeros((HALO, C), F32)
            for j in range(CONV_WIDTH):
                sh = CONV_WIDTH - 1 - j
                off = 2 + j
                da = da + cw_ref[pl.ds(j, 1), :] * dy_ext[sh % 8, pl.ds(pl.multiple_of(base + sh // 8 * 8, 8), HALO), :]
                a_j = a_ext[off % 8, pl.ds(pl.multiple_of(base + off // 8 * 8, 8), HALO), :]
                dcw_acc[j] += jnp.sum((dy_m * a_j).reshape(HALO // 8, 8, C), axis=0)
            v = val_ref[pl.ds(base, HALO), :]
            g = gate_ref[pl.ds(base, HALO), :]
            sg = jax.nn.sigmoid(g)
            du_ref[pl.ds(base, HALO), 0:C] = (da * sg).astype(BF16)
            du_ref[pl.ds(base, HALO), C:2 * C] = (da * v * sg * (1.0 - sg)).astype(BF16)
            pwin = p_ext[pl.ds(base, 2 * HALO), :]
            cwin = ddc_ext[pl.ds(base, 2 * HALO), :]
            for gi, w in enumerate(POOL_WINDOWS):
                lo, hi = gi * HEAD, (gi + 1) * HEAD
                x = pwin[HALO:, lo:hi]
                tot = x
                back = cwin[0:HALO, lo:hi]
                for k in range(1, w):
                    tot = tot + pwin[HALO - k:2 * HALO - k, lo:hi]
                    back = back + cwin[k:k + HALO, lo:hi]
                cnt = _pool_counts(i * tm + base, HALO, w)
                d_buf[pl.ds(base, HALO), lo:hi] = (tot / cnt - x).astype(BF16)
                du_ref[pl.ds(base, HALO), 2 * C + lo:2 * C + hi] = (back - dd_buf[pl.ds(base, HALO), lo:hi]).astype(BF16)
            return carry

        lax.fori_loop(0, nsub, p2, 0)
        for gi in range(len(POOL_WINDOWS)):
            lo, hi = gi * HEAD, (gi + 1) * HEAD
            pre = _nn(d_buf[:, lo:hi], pw_ref[gi]) + pb_ref[:, lo:hi]
            vec_acc[3, :, lo:hi] += jnp.sum((dyb_ref[:, lo:hi] * pre).reshape(tm // 8, 8, HEAD), axis=0)
            dpw_ref[gi] += _tn(d_buf[:, lo:hi], dpre_buf[:, lo:hi])
        for j in range(CONV_WIDTH):
            dcw_ref[pl.ds(j, 1), :] += jnp.sum(dcw_acc[j], axis=0, keepdims=True)
        for r in range(5):
            dvec_ref[pl.ds(r, 1), :] += jnp.sum(vec_acc[r], axis=0, keepdims=True)

    def main(c, width=C):
        return pl.BlockSpec((tm, width), lambda i: (i, c))

    def prev(c):
        return pl.BlockSpec((HALO, C), lambda i: (jnp.maximum(i * hb - 1, 0), c))

    def nxt(c):
        return pl.BlockSpec((HALO, C), lambda i: (jnp.minimum((i + 1) * hb, T // HALO - 1), c))

    vec = pl.BlockSpec((1, C), lambda i: (0, 0))
    mat = pl.BlockSpec((4, HEAD, HEAD), lambda i: (0, 0, 0))
    return pl.pallas_call(
        body, grid=(nblk,),
        in_specs=[main(0), main(1), nxt(0), nxt(1), main(0), nxt(0), main(0), main(1), main(2), prev(0), prev(1),
                  prev(2), pl.BlockSpec((32, C), lambda i: (0, 0)), vec, vec, mat, mat, vec, vec],
        out_specs=[pl.BlockSpec((tm, 3 * C), lambda i: (i, 0)), pl.BlockSpec((32, C), lambda i: (0, 0)),
                   pl.BlockSpec((8, C), lambda i: (0, 0)), mat],
        out_shape=[jax.ShapeDtypeStruct((T, 3 * C), BF16), jax.ShapeDtypeStruct((32, C), F32),
                   jax.ShapeDtypeStruct((8, C), F32), jax.ShapeDtypeStruct((4, HEAD, HEAD), F32)],
        scratch_shapes=[pltpu.VMEM((8, E + 8, C), F32), pltpu.VMEM((8, E + 8, C), F32), pltpu.VMEM((E, C), F32),
                        pltpu.VMEM((E, C), F32), pltpu.VMEM((tm, C), F32), pltpu.VMEM((tm, C), BF16),
                        pltpu.VMEM((tm, C), BF16), pltpu.VMEM((32, 8, C), F32), pltpu.VMEM((8, 8, C), F32)],
        compiler_params=_cp("arbitrary"), name="ev_bwd")(
            dyab, dyab, dyab, dyab, yc, yc, u, u, u, u, u, u, cw, lg, lb, pw, pwt, pb, ps)


def _cumsum_rows(x, reverse=False):
    n = x.shape[0]
    rid = lax.broadcasted_iota(jnp.int32, (n, 1), 0)
    k = 1
    while k < n:
        if reverse:
            sh = jnp.where(rid < n - k, pltpu.roll(x, n - k, 0), 0.0)
        else:
            sh = jnp.where(rid >= k, pltpu.roll(x, k, 0), 0.0)
        x = x + sh
        k *= 2
    return x


def _hgrn_gates(qr, fr, lbv):
    sq = jax.nn.sigmoid(qr)
    sg = jax.nn.sigmoid(fr)
    fg = lbv + (1.0 - lbv) * sg
    return qr * sq, sq, sg, fg, 1.0 - fg, jnp.log(fg)


def _hgrn_fwd(u, lbv, gn):
    T = u.shape[0]
    H = 8
    RB = _tile(T, HGRN_ROWS, CHUNK)
    NC = RB // CHUNK
    NS = CHUNK // SUB

    HP = HGRN_HEADS_FWD
    W = HP * HEAD

    def body(q_ref, f_ref, i_ref, g_ref, lb_ref, gn_ref, y_ref, o_ref, s0_ref, st, qs, ks, bs, vs, os_):
        rb = pl.program_id(1)

        @pl.when(rb == 0)
        def _():
            st[...] = jnp.zeros_like(st)

        t8 = lax.broadcasted_iota(jnp.int32, (8, 1), 0)

        def head(hh, c, rows):
            sl = slice(hh * HEAD, (hh + 1) * HEAD)
            q, _, _, _, kk, lf = _hgrn_gates(q_ref[rows, sl], f_ref[rows, sl], lb_ref[:, sl])
            v = i_ref[rows, sl]
            b = _cumsum_rows(lf)
            qs[hh] = q
            ks[hh] = kk
            bs[hh] = b
            vs[hh] = v
            st0 = st[hh]
            s0_ref[hh, c] = st0
            os_[hh] = _nt((q * jnp.exp(b)).astype(BF16), st0.astype(BF16))
            for I in range(NS):
                lo = I * SUB
                qI = qs[hh, lo:lo + SUB, :]
                bI = bs[hh, lo:lo + SUB, :]
                oI = jnp.zeros((SUB, HEAD), F32)
                if I > 0:
                    bprev = bs[hh, pl.ds(lo - 1, 1), :]
                    qt = _r16(qI * jnp.exp(bI - bprev))
                    kt = _r16(ks[hh, 0:lo, :] * jnp.exp(bprev - bs[hh, 0:lo, :]))
                    A = _nt(qt, kt)
                    oI = oI + _nn(_r16(A), _r16(vs[hh, 0:lo, :]))
                od = [jnp.zeros((8, HEAD), F32) for _ in range(SUB // 8)]
                for s in range(SUB):
                    row = pl.ds(lo + s, 1)
                    brow, krow, vrow = bs[hh, row, :], ks[hh, row, :], vs[hh, row, :]
                    for ti in range(SUB // 8):
                        o8 = 8 * ti
                        if s > o8 + 7:
                            continue
                        d = bI[o8:o8 + 8] - brow
                        if s > o8:
                            d = jnp.where(t8 >= s - o8, d, NEG)
                        col = jnp.sum(qI[o8:o8 + 8] * jnp.exp(d) * krow, axis=1, keepdims=True)
                        od[ti] = od[ti] + col * vrow
                os_[hh, lo:lo + SUB, :] += oI + jnp.concatenate(od, axis=0)
            blast = bs[hh, pl.ds(CHUNK - 1, 1), :]
            kh = kk * jnp.exp(blast - b)
            st[hh] = st0 * jnp.exp(blast) + _tn(v.astype(BF16), kh.astype(BF16))
            o = os_[hh]
            o_ref[rows, sl] = o
            rr = lax.rsqrt(jnp.mean(o * o, axis=-1, keepdims=True) + EPS)
            gr = g_ref[rows, sl]
            y_ref[rows, sl] = (((o * rr) * gn_ref[...]) * (gr * jax.nn.sigmoid(gr))).astype(BF16)

        def chunk(c, carry):
            rows = pl.ds(pl.multiple_of(c * CHUNK, CHUNK), CHUNK)
            for hh in range(HP):
                head(hh, c, rows)
            return carry

        lax.fori_loop(0, NC, chunk, 0)

    def blk(q):
        return pl.BlockSpec((RB, W), lambda h, r: (r, q * (H // HP) + h))

    sc = lambda: pltpu.VMEM((HP, CHUNK, HEAD), F32)
    return pl.pallas_call(
        body, grid=(H // HP, T // RB),
        in_specs=[blk(0), blk(1), blk(2), blk(3), pl.BlockSpec((1, W), lambda h, r: (0, h)),
                  pl.BlockSpec((1, HEAD), lambda h, r: (0, 0))],
        out_specs=[pl.BlockSpec((RB, W), lambda h, r: (r, h)), pl.BlockSpec((RB, W), lambda h, r: (r, h)),
                   pl.BlockSpec((HP, NC, HEAD, HEAD), lambda h, r: (h, r, 0, 0))],
        out_shape=[jax.ShapeDtypeStruct((T, H * HEAD), BF16), jax.ShapeDtypeStruct((T, H * HEAD), F32),
                   jax.ShapeDtypeStruct((H, T // CHUNK, HEAD, HEAD), F32)],
        scratch_shapes=[pltpu.VMEM((HP, HEAD, HEAD), F32), sc(), sc(), sc(), sc(), sc()],
        compiler_params=_cp("parallel", "arbitrary"), name="hgrn_fwd")(u, u, u, u, lbv, gn)


def _hgrn_bwd(dy, o, s0, u, lbv, gn):
    T = u.shape[0]
    H = 8
    RB = _tile(T, HGRN_ROWS, CHUNK)
    NB = T // RB
    NC = RB // CHUNK
    NS = CHUNK // SUB

    def body(q_ref, f_ref, i_ref, g_ref, lb_ref, gn_ref, o_ref, dy_ref, s0_ref, du_ref, dlb_ref, dgn_ref,
             dst, qs, ks, bs, vs, dos, dqs, dks, dki, dvs, dbs):
        rb = pl.program_id(1)

        @pl.when(rb == 0)
        def _():
            dst[...] = jnp.zeros_like(dst)
            dlb_ref[...] = jnp.zeros_like(dlb_ref)
            dgn_ref[...] = jnp.zeros_like(dgn_ref)

        t8 = lax.broadcasted_iota(jnp.int32, (8, 1), 0)
        lane = lax.broadcasted_iota(jnp.int32, (8, HEAD), 1)
        gnv = gn_ref[...]

        def head(hh, c, rows):
            sl = slice(hh * HEAD, (hh + 1) * HEAD)
            lbv_ = lb_ref[:, sl]
            qr = q_ref[rows, sl]
            q, sq, sg, fg, kk, lf = _hgrn_gates(qr, f_ref[rows, sl], lbv_)
            v = i_ref[rows, sl]
            gr = g_ref[rows, sl]
            b = _cumsum_rows(lf)
            eb = jnp.exp(b)
            ov = o_ref[rows, sl]
            dyv = dy_ref[rows, sl]
            rr = lax.rsqrt(jnp.mean(ov * ov, axis=-1, keepdims=True) + EPS)
            oh = ov * rr
            gs = jax.nn.sigmoid(gr)
            dgr = dyv * (oh * gnv) * _dsilu(gr, gs)
            dnrm = dyv * (gr * gs)
            dgn_ref[hh] += jnp.sum(dnrm * oh, axis=0, keepdims=True)
            t1 = dnrm * gnv
            do = rr * (t1 - oh * jnp.mean(t1 * oh, axis=-1, keepdims=True))
            qs[hh] = q
            ks[hh] = kk
            bs[hh] = b
            vs[hh] = v
            dos[hh] = do
            st0 = s0_ref[hh, c]
            dS = dst[hh]
            do_b = do.astype(BF16)
            blast = bs[hh, pl.ds(CHUNK - 1, 1), :]
            elast = jnp.exp(blast - b)
            dq_inter = _nn(do_b, st0.astype(BF16)) * eb
            dqs[hh] = dq_inter
            dbs[hh] = q * dq_inter
            kh = kk * elast
            dvs[hh] = _nt(kh.astype(BF16), dS.astype(BF16))
            dk_inter = _nn(v.astype(BF16), dS.astype(BF16)) * elast
            dki[hh] = dk_inter
            dks[hh] = jnp.zeros((CHUNK, HEAD), F32)
            for I in range(NS):
                lo = I * SUB
                qI = qs[hh, lo:lo + SUB, :]
                bI = bs[hh, lo:lo + SUB, :]
                doI = dos[hh, lo:lo + SUB, :]
                dqI = jnp.zeros((SUB, HEAD), F32)
                dbI = jnp.zeros((SUB, HEAD), F32)
                if I > 0:
                    bprev = bs[hh, pl.ds(lo - 1, 1), :]
                    eq = jnp.exp(bI - bprev)
                    ek = jnp.exp(bprev - bs[hh, 0:lo, :])
                    qt = _r16(qI * eq)
                    kt = _r16(ks[hh, 0:lo, :] * ek)
                    A = _r16(_nt(qt, kt))
                    doI_b = _r16(doI)
                    dA = _r16(_nt(doI_b, _r16(vs[hh, 0:lo, :])))
                    dvs[hh, 0:lo, :] += _tn(A, doI_b)
                    dqt = _nn(dA, kt)
                    dkt = _tn(dA, qt)
                    dqI = dqI + dqt * eq
                    dbI = dbI + qt.astype(F32) * dqt
                    dks[hh, 0:lo, :] += dkt * ek
                    dbs[hh, 0:lo, :] -= kt.astype(F32) * dkt
                dq_t = [jnp.zeros((8, HEAD), F32) for _ in range(SUB // 8)]
                a_t = [jnp.zeros((8, HEAD), F32) for _ in range(SUB // 8)]
                for s in range(SUB):
                    row = pl.ds(lo + s, 1)
                    brow, krow, vrow = bs[hh, row, :], ks[hh, row, :], vs[hh, row, :]
                    dk_s = None
                    for ti in range(SUB // 8):
                        o8 = 8 * ti
                        if s > o8 + 7:
                            continue
                        d = bI[o8:o8 + 8] - brow
                        if s > o8:
                            d = jnp.where(t8 >= s - o8, d, NEG)
                        Es = jnp.exp(d)
                        qE = qI[o8:o8 + 8] * Es
                        col = jnp.sum(qE * krow, axis=1, keepdims=True)
                        a_t[ti] = jnp.where(lane == s, col, a_t[ti])
                        dcol = jnp.sum(doI[o8:o8 + 8] * vrow, axis=1, keepdims=True)
                        dq_t[ti] = dq_t[ti] + (dcol * Es) * krow
                        part = jnp.sum(dcol * qE, axis=0, keepdims=True)
                        dk_s = part if dk_s is None else dk_s + part
                    dks[hh, row, :] += dk_s
                    dbs[hh, row, :] -= krow * dk_s
                a_d = jnp.concatenate(a_t, axis=0)
                dq_d = jnp.concatenate(dq_t, axis=0)
                dvs[hh, lo:lo + SUB, :] += _tn(a_d, doI)[0:SUB]
                dqI = dqI + dq_d
                dbI = dbI + qI * dq_d
                dqs[hh, lo:lo + SUB, :] += dqI
                dbs[hh, lo:lo + SUB, :] += dbI
            kdk = kk * dki[hh]
            excl = _cumsum_rows(kdk) - kdk
            suff = _cumsum_rows(dbs[hh], reverse=True)
            gdec = jnp.sum(dS * st0, axis=0, keepdims=True) * jnp.exp(blast)
            dlf = suff + excl + gdec
            dk = dks[hh] + dki[hh]
            dfg = dlf / fg - dk
            dlb_ref[:, sl] += jnp.sum(dfg * (1.0 - sg), axis=0, keepdims=True)
            du_ref[0, rows, sl] = (dqs[hh] * _dsilu(qr, sq)).astype(BF16)
            du_ref[1, rows, sl] = (dfg * (1.0 - lbv_) * sg * (1.0 - sg)).astype(BF16)
            du_ref[2, rows, sl] = dvs[hh].astype(BF16)
            du_ref[3, rows, sl] = dgr.astype(BF16)
            dst[hh] = dS * jnp.exp(blast) + _tn(do_b, (q * eb).astype(BF16))

        def chunk(cc, carry):
            c = NC - 1 - cc
            rows = pl.ds(pl.multiple_of(c * CHUNK, CHUNK), CHUNK)
            for hh in range(HP):
                head(hh, c, rows)
            return carry

        lax.fori_loop(0, NC, chunk, 0)

    HP = HGRN_HEADS_BWD
    W = HP * HEAD

    def blk(qd):
        return pl.BlockSpec((RB, W), lambda h, r: (NB - 1 - r, qd * (H // HP) + h))

    hblk = pl.BlockSpec((RB, W), lambda h, r: (NB - 1 - r, h))
    sc = lambda: pltpu.VMEM((HP, CHUNK, HEAD), F32)
    return pl.pallas_call(
        body, grid=(H // HP, NB),
        in_specs=[blk(0), blk(1), blk(2), blk(3), pl.BlockSpec((1, W), lambda h, r: (0, h)),
                  pl.BlockSpec((1, HEAD), lambda h, r: (0, 0)), hblk, hblk,
                  pl.BlockSpec((HP, NC, HEAD, HEAD), lambda h, r: (h, NB - 1 - r, 0, 0))],
        out_specs=[pl.BlockSpec((4, RB, W), lambda h, r: (0, NB - 1 - r, h)),
                   pl.BlockSpec((1, W), lambda h, r: (0, h)), pl.BlockSpec((HP, 1, HEAD), lambda h, r: (h, 0, 0))],
        out_shape=[jax.ShapeDtypeStruct((4, T, H * HEAD), BF16), jax.ShapeDtypeStruct((1, H * HEAD), F32),
                   jax.ShapeDtypeStruct((H, 1, HEAD), F32)],
        scratch_shapes=[pltpu.VMEM((HP, HEAD, HEAD), F32)] + [sc() for _ in range(10)],
        compiler_params=_cp("parallel", "arbitrary"), name="hgrn_bwd")(u, u, u, u, lbv, gn, o, dy, s0)


def _softmax_rows(p_ref, L):
    rows = [p_ref[pl.ds(l, 1), :] for l in range(L)]
    m = rows[0]
    for r in rows[1:]:
        m = jnp.maximum(m, r)
    e = [jnp.exp(r - m) for r in rows]
    tot = e[0]
    for t in e[1:]:
        tot = tot + t
    return [t / tot for t in e]


def _lb_fwd(lbp):
    L, D = lbp.shape

    def body(p_ref, o_ref):
        sm = _softmax_rows(p_ref, L)
        acc = jnp.zeros((1, D), F32)
        o_ref[pl.ds(0, 1), :] = acc
        for l in range(1, L):
            acc = acc + sm[l]
            o_ref[pl.ds(l, 1), :] = acc

    return pl.pallas_call(body, out_shape=jax.ShapeDtypeStruct((L, D), F32), name="lb_fwd")(lbp)


def _lb_bwd(lbp, dlb):
    L, D = lbp.shape

    def body(p_ref, d_ref, o_ref):
        sm = _softmax_rows(p_ref, L)
        dsm = [jnp.zeros((1, D), F32)]
        for i in range(1, L):
            t = jnp.zeros((1, D), F32)
            for l in range(i, L):
                t = t + d_ref[pl.ds(l, 1), :]
            dsm.append(t)
        dot = jnp.zeros((1, D), F32)
        for i in range(L):
            dot = dot + dsm[i] * sm[i]
        for i in range(L):
            o_ref[pl.ds(i, 1), :] = sm[i] * (dsm[i] - dot)

    return pl.pallas_call(body, out_shape=jax.ShapeDtypeStruct((L, D), F32), name="lb_bwd")(lbp, dlb)


def _my_pos():
    return lax.axis_index("x"), lax.axis_index("y"), lax.axis_index("c")


def _peer(mask):
    x, y, c = _my_pos()
    mx, my, mc = (mask >> 2) & 1, (mask >> 1) & 1, mask & 1
    px = (1 - x) if mx else x
    py = (1 - y) if my else y
    pc = (1 - c) if mc else c
    return (px, py, pc), 4 * px + 2 * py + pc


HBM_SPEC =pl.BlockSpec(memory_space=pltpu.HBM)
SEM_SPEC = pl.BlockSpec(memory_space=pltpu.SEMAPHORE)
EFFECT = pltpu.SideEffectType.DATAFLOW_SIDE_EFFECTING


def _hbm(a):
    return pltpu.with_memory_space_constraint(a, pltpu.HBM)


def _landing(block_shape, dtype, axis=0):
    if axis == 0:
        return lax.empty((N_DEV,) + tuple(block_shape), dtype)
    rows, n = block_shape
    return lax.empty((rows, N_DEV * n), dtype)


def _slot(ref, i):
    if len(ref.shape) == 2:
        n = ref.shape[1] // N_DEV
        return ref.at[:, pl.ds(i * n, n)]
    return ref.at[i]


def _push_start(name, srcs, lands, whole, groups):
    n = len(srcs)
    ng = 1 + max(groups)
    cnt = [groups.count(g) for g in range(ng)]
    idx = [groups[:a].count(groups[a]) for a in range(n)]

    def body(*refs):
        src_refs, land_refs = refs[:n], refs[n:2 * n]
        sems = refs[2 * n:2 * n + 3 * ng]
        token = refs[-1]
        x, y, c = _my_pos()
        me = 4 * x + 2 * y + c
        for a in range(n):
            g = groups[a]
            for m in range(1, N_DEV):
                peer, pid = _peer(m)
                pltpu.make_async_remote_copy(
                    src_ref=src_refs[a] if whole else src_refs[a].at[pid], dst_ref=_slot(land_refs[a], me),
                    send_sem=sems[3 * g].at[idx[a] * (N_DEV - 1) + m - 1],
                    recv_sem=sems[3 * g + 1].at[idx[a] * (N_DEV - 1) + m - 1],
                    device_id=peer, device_id_type=MESH).start()
            pltpu.make_async_copy(src_refs[a] if whole else src_refs[a].at[me], _slot(land_refs[a], me),
                                  sems[3 * g + 2].at[idx[a]]).start()
        token[...] = jnp.zeros_like(token)

    sem_shapes = []
    for g in range(ng):
        sem_shapes += [pltpu.SemaphoreType.DMA((cnt[g] * (N_DEV - 1),))] * 2 + [pltpu.SemaphoreType.DMA((cnt[g],))]
    thru = [pltpu.HBM(s.shape, s.dtype) for s in list(srcs) + list(lands)]
    res = pl.pallas_call(
        body, name=name,
        out_shape=tuple(sem_shapes + thru + [jax.ShapeDtypeStruct((8, 128), F32)]),
        in_specs=tuple([HBM_SPEC] * (2 * n)),
        out_specs=tuple([SEM_SPEC] * (3 * ng) + [HBM_SPEC] * (2 * n) + [pl.BlockSpec(memory_space=pltpu.VMEM)]),
        input_output_aliases={i: 3 * ng + i for i in range(2 * n)},
        compiler_params=pltpu.CompilerParams(has_side_effects=EFFECT),
    )(*[_hbm(s) for s in srcs], *[_hbm(z) for z in lands])
    sems = [(res[3 * g], res[3 * g + 1], res[3 * g + 2]) for g in range(ng)]
    srcs_thru = list(res[3 * ng:3 * ng + n])
    lands_thru = list(res[3 * ng + n:3 * ng + 2 * n])
    return sems, srcs_thru, lands_thru, res[-1]


def _push_wait(name, srcs_thru, lands_thru, sems, after, whole):
    n = len(srcs_thru)

    def body(*refs):
        src_refs, land_refs = refs[:n], refs[n:2 * n]
        send_sems, recv_sems, own_sems = refs[2 * n], refs[2 * n + 1], refs[2 * n + 2]
        x, y, c = _my_pos()
        me = 4 * x + 2 * y + c
        for a in range(n):
            pltpu.make_async_copy(src_refs[a] if whole else src_refs[a].at[me], _slot(land_refs[a], me),
                                  own_sems.at[a]).wait()
            for m in range(1, N_DEV):
                peer, pid = _peer(m)
                cp = pltpu.make_async_remote_copy(
                    src_ref=src_refs[a] if whole else src_refs[a].at[pid], dst_ref=_slot(land_refs[a], pid),
                    send_sem=send_sems.at[a * (N_DEV - 1) + m - 1], recv_sem=recv_sems.at[a * (N_DEV - 1) + m - 1],
                    device_id=peer, device_id_type=MESH)
                cp.wait_send()
                cp.wait_recv()

    thru = [pltpu.HBM(s.shape, s.dtype) for s in list(srcs_thru) + list(lands_thru)]
    res = pl.pallas_call(
        body, name=name, out_shape=tuple(thru),
        in_specs=tuple([HBM_SPEC] * (2 * n) + [SEM_SPEC, SEM_SPEC, SEM_SPEC, ANY]),
        out_specs=tuple([HBM_SPEC] * (2 * n)),
        input_output_aliases={i: i for i in range(2 * n)},
        compiler_params=pltpu.CompilerParams(has_side_effects=EFFECT),
    )(*srcs_thru, *lands_thru, sems[0], sems[1], sems[2], after)
    return list(res[n:])


def _adamw(recv, w, m, v, layer=0, prev=None):
    L, R, C = w.shape
    tr = _tile(R, max(8, (1 << 18) // C), 8) if R % 8 == 0 else R
    bc1 = 1.0 - ADAM_B1 ** ADAM_STEP
    bc2 = 1.0 - ADAM_B2 ** ADAM_STEP
    if prev is None:
        prev = [lax.empty((L, R, C), F32) for _ in range(4)]

    def body(r_ref, w_ref, m_ref, v_ref, p0, p1, p2, p3, g_ref, d_ref, nm_ref, nv_ref):
        g = r_ref[0].astype(F32)
        for s in range(1, N_DEV):
            g = g + r_ref[s].astype(F32)
        nm = ADAM_B1 * m_ref[...] + (1.0 - ADAM_B1) * g
        nv = ADAM_B2 * v_ref[...] + (1.0 - ADAM_B2) * (g * g)
        mh = nm / bc1
        vh = nv / bc2
        g_ref[...] = g
        d_ref[...] = -ADAM_LR * (mh / (jnp.sqrt(vh) + ADAM_EPS) + ADAM_WD * w_ref[...])
        nm_ref[...] = nm
        nv_ref[...] = nv

    row = pl.BlockSpec((None, tr, C), lambda i: (layer, i, 0))
    return pl.pallas_call(
        body, grid=(R // tr,),
        in_specs=[pl.BlockSpec((N_DEV, tr, C), lambda i: (0, i, 0)), row, row, row] + [ANY] * 4,
        out_specs=[row] * 4, out_shape=[jax.ShapeDtypeStruct((L, R, C), F32)] * 4,
        input_output_aliases={4: 0, 5: 1, 6: 2, 7: 3},
        compiler_params=_cp("parallel"), name="adamw")(recv, w, m, v, *prev)


def _full_w_spec(tk, tn):
    return pl.BlockSpec((tk, tn), lambda i, j, k: (k, j))


def kernel(x, meta_tokens, mix_norm_g, mlp_norm_g, final_norm_g, ev_w_in, ev_conv_w, ev_conv_b, ev_ln_g, ev_ln_b, ev_pool_w, ev_pool_b, ev_pool_scale, ev_w_out, od_w_in, od_gnorm_g, od_w_out, lb_param, mlp_w1, mlp_w2, loss_target, m_meta_tokens, m_mix_norm_g, m_mlp_norm_g, m_final_norm_g, m_ev_w_in, m_ev_conv_w, m_ev_conv_b, m_ev_ln_g, m_ev_ln_b, m_ev_pool_w, m_ev_pool_b, m_ev_pool_scale, m_ev_w_out, m_od_w_in, m_od_gnorm_g, m_od_w_out, m_lb_param, m_mlp_w1, m_mlp_w2, v_meta_tokens, v_mix_norm_g, v_mlp_norm_g, v_final_norm_g, v_ev_w_in, v_ev_conv_w, v_ev_conv_b, v_ev_ln_g, v_ev_ln_b, v_ev_pool_w, v_ev_pool_b, v_ev_pool_scale, v_ev_w_out, v_od_w_in, v_od_gnorm_g, v_od_w_out, v_lb_param, v_mlp_w1, v_mlp_w2):
    S, D = x.shape[1], x.shape[2]
    T = PAD + N_META + S
    DEPTH = mix_norm_g.shape[0]
    DFF = mlp_w1.shape[2] * N_DEV
    dev = 4 * lax.axis_index("x") + 2 * lax.axis_index("y") + lax.axis_index("c")

    n_ev = ev_w_in.shape[0]
    n_od = od_w_in.shape[0]
    n_in_od = od_w_in.shape[2]
    n_w1 = mlp_w1.shape[2]

    ag_src, ag_grp, ag_axis, ag_at = [], [], [], {}
    for key, arr in (("meta", meta_tokens), ("cw", ev_conv_w)):
        ag_at["small", key] = len(ag_src)
        ag_src.append(arr)
        ag_grp.append(len(ag_grp))
        ag_axis.append(0)
    for layer in range(DEPTH):
        j = layer // 2
        mixer = [("in", ev_w_in[j]), ("out", ev_w_out[j])] if layer % 2 == 0 else [("in", od_w_in[j]), ("out", od_w_out[j])]
        for pos, (key, arr) in enumerate(mixer + [("w1", mlp_w1[layer]), ("w2", mlp_w2[layer])]):
            ag_at[layer, key] = len(ag_src)
            ag_src.append(arr.astype(BF16))
            ag_grp.append(len(ag_grp))
            ag_axis.append(1 if key in ("in", "w1") and arr.shape[1] % 128 == 0 else 0)
    ag_sems, ag_s, ag_l, ag_tok = _push_start(
        "ag_start", ag_src, [_landing(s_.shape, s_.dtype, ax) for s_, ax in zip(ag_src, ag_axis)], True, ag_grp)

    def ag_wait(layer, key, after):
        a = ag_at[layer, key]
        return _push_wait(f"ag_wait_{a}", [ag_s[a]], [ag_l[a]], ag_sems[a], after, True)[0]

    g_meta = ag_wait("small", "meta", ag_tok)
    g_cw = ag_wait("small", "cw", ag_tok)
    meta_full = jnp.transpose(g_meta, (1, 0, 2)).reshape(N_META, D)
    cw_full = jnp.transpose(g_cw, (1, 2, 0, 3)).reshape(n_ev, CONV_WIDTH, -1)
    cw_pad = jnp.pad(cw_full, ((0, 0), (0, 32 - CONV_WIDTH), (0, 0)))

    h = jnp.concatenate([jnp.zeros((PAD, D), F32), meta_full, x[0]], axis=0) + ag_tok[0, 0]
    tgt = jnp.pad(loss_target[0], ((PAD + N_META, 0), (0, 0)))
    lb_all = _lb_fwd(lb_param)
    h, tgt, lb_all = lax.optimization_barrier((h, tgt, lb_all))

    tm_big = _tile(T, MM_ROWS_BIG, 16)
    tm_mid = _tile(T, MM_ROWS_MID, 16)
    tm_k4 = _tile(T, MM_ROWS_K4, 16)

    saved = []
    for layer in range(DEPTH):
        j = layer // 2
        sv = {"h0": h}
        g_in = ag_wait(layer, "in", h)
        w_in = g_in if g_in.ndim == 2 else jnp.transpose(g_in, (1, 0, 2)).reshape(D, -1)
        if layer % 2 == 0:
            sv["n"], u = _mm_rms_nn("ev_in", h, mix_norm_g[layer][None], w_in, tm_big, 512, "f32")
            yab, yc = _ev_fwd(u, cw_pad[j], ev_conv_b[j][None], ev_ln_g[j][None], ev_ln_b[j][None],
                              ev_pool_w[j].astype(BF16), ev_pool_b[j].reshape(1, -1), ev_pool_scale[j][None])
            sv.update(u=u, y=yab, yc=yc)
            w_out = ag_wait(layer, "out", yab).reshape(-1, D)
            h = _mm_nn("ev_out", yab, w_out, _full_w_spec, T, D, D, tm_k4, D, D, "resid", extra=h)
        else:
            sv["n"], u = _mm_rms_nn("od_in", h, mix_norm_g[layer][None], w_in, tm_big, 512, "f32")
            y, o, s0 = _hgrn_fwd(u, lb_all[layer][None], od_gnorm_g[j][None])
            sv.update(u=u, y=y, o=o, s0=s0)
            w_out = ag_wait(layer, "out", y).reshape(-1, D)
            h = _mm_nn("od_out", y, w_out, _full_w_spec, T, D, D, tm_k4, D, D, "resid", extra=h)
        sv["h1"] = h
        w_w1 = ag_wait(layer, "w1", h)
        n2, r, act = _mm_rms_nn("mlp_w1", h, mlp_norm_g[layer][None], w_w1, tm_big, 512, "relu2")
        w_w2 = ag_wait(layer, "w2", act).reshape(DFF, D)
        sv.update(w_in=w_in, w_out=w_out, w_w1=w_w1, w_w2=w_w2)
        sv.update(n2=n2, r=r, act=act)
        h = _mm_nn("mlp_w2", act, w_w2, _full_w_spec, T, D, DFF, tm_k4, D, DFF, "resid", extra=h)
        saved.append(sv)

    loss_blk, dh, dhb, dg_final = _loss_head(h, final_norm_g[None], tgt)
    loss = lax.psum(loss_blk[0, 0], AXES)

    tt = T
    g_mix, g_mlp = [None] * DEPTH, [None] * DEPTH
    small ={"cw": [None] * n_ev, "vec": [None] * n_ev, "pw": [None] * n_ev, "gn": [None] * n_od}
    dlb_rows = [jnp.zeros((1, D), F32) for _ in range(DEPTH)]

    def xs2(tt_, tk):
        return pl.BlockSpec((tt_, tk), lambda a, b, t: (t, a))

    def ys2(tt_, tn):
        return pl.BlockSpec((tt_, tn), lambda a, b, t: (t, b))

    def os2(tk, tn):
        return pl.BlockSpec((tk, tn), lambda a, b, t: (a, b))

    def os3(tk, tn):
        return pl.BlockSpec((None, tk, tn), lambda a, b, t: (b, a, 0))

    def dy2(tm, tn):
        return pl.BlockSpec((tm, tn), lambda i, jj, k: (i, k))

    def w_rows(tj, tn):
        return pl.BlockSpec((tj, tn), lambda i, jj, k: (jj, k))

    def w_whole(tj, tn):
        return pl.BlockSpec((tj, tn), lambda i, jj, k: (0, 0), pipeline_mode=pl.Buffered(1))

    rs_pending = []

    def rs_start(tag, mats):
        blocks = [m_ if m_.ndim == 3 else m_.reshape(N_DEV, m_.shape[0] // N_DEV, m_.shape[1]) for m_ in mats]
        lands = [_landing(b_.shape[1:], b_.dtype) for b_ in blocks]
        sems, s_thru, l_thru, tok = _push_start(f"rs_start_{tag}", blocks, lands, False, [0] * len(blocks))
        rs_pending.append((tag, s_thru, l_thru, sems[0]))
        return tok[0, 0]

    for layer in reversed(range(DEPTH)):
        j = layer // 2
        sv = saved[layer]
        da1 = _mm_nt("mlp_w2_t", dhb, sv["w_w2"], dy2, w_rows, T, DFF, D, tm_mid, 1024, D, "dact", extra=sv["r"])
        dw2 = _mm_tn("mlp_dw2", sv["act"], dhb, xs2, ys2, os2, (DFF, D), T, DFF, D, tt, 512, D)
        dw1 = _mm_tn("mlp_dw1", sv["n2"], da1, xs2, ys2, os3, (N_DEV, D, n_w1), T, D, DFF, tt, D, n_w1)
        tok = rs_start(f"mlp{layer}", [dw1, dw2])
        dh, dhb, g_mlp[layer] = _mm_nt("mlp_w1_t", da1, sv["w_w1"], dy2, w_whole, T, D, DFF, tm_k4, D, DFF, "rms",
                                       extra=(sv["h1"], mlp_norm_g[layer][None] + tok, dh))
        if layer % 2 == 0:
            dyab = _mm_nt("ev_out_t", dhb, sv["w_out"], dy2, w_rows, T, D, D, tm_mid, D, D, "f32")
            dwout = _mm_tn("ev_dwout", sv["y"], dhb, xs2, ys2, os2, (D, D), T, D, D, tt, 512, D)
            du, small["cw"][j], small["vec"][j], small["pw"][j] = _ev_bwd(
                dyab, sv["yc"], sv["u"], cw_pad[j], ev_ln_g[j][None], ev_ln_b[j][None], ev_pool_w[j].astype(BF16),
                jnp.transpose(ev_pool_w[j], (0, 2, 1)).astype(BF16), ev_pool_b[j].reshape(1, -1),
                ev_pool_scale[j][None])
            nin = du.shape[1]
            dwin = _mm_tn("ev_dwin", sv["n"], du, xs2, ys2, os2, (D, nin), T, D, nin, tt, D, 512)
            dwin = jnp.transpose(dwin.reshape(D, N_DEV, nin // N_DEV), (1, 0, 2))
            tok = rs_start(f"mix{layer}", [dwin, dwout])
            dh, dhb, g_mix[layer] = _mm_nt("ev_in_t", du, sv["w_in"], dy2, w_whole, T, D, nin, tm_k4, D, nin, "rms",
                                           extra=(sv["h0"], mix_norm_g[layer][None] + tok, dh))
        else:
            dy = _mm_nt("od_out_t", dhb, sv["w_out"], dy2, w_rows, T, D, D, tm_mid, D, D, "f32")
            dwout = _mm_tn("od_dwout", sv["y"], dhb, xs2, ys2, os2, (D, D), T, D, D, tt, 512, D)
            du3, dlb_rows[layer], small["gn"][j] = _hgrn_bwd(dy, sv["o"], sv["s0"], sv["u"], lb_all[layer][None],
                                                              od_gnorm_g[j][None])
            per = D // n_in_od

            def du_t(tt_, tn):
                return pl.BlockSpec((None, tt_, tn), lambda a, b, t: (b // per, t, b % per))

            dwin = _mm_tn("od_dwin", sv["n"], du3, xs2, du_t, os3, (N_DEV, D, n_in_od), T, D, 4 * D, tt, D, n_in_od)
            tok = rs_start(f"mix{layer}", [dwin, dwout])
            dh, dhb, g_mix[layer] = _mm_nt(
                "od_in_t", du3, sv["w_in"], lambda tm, tn: pl.BlockSpec((4, tm, tn // 4), lambda i, jj, k: (0, i, 0)),
                w_whole, T, D, 4 * D, tm_k4, D, 4 * D, "rms", extra=(sv["h0"], mix_norm_g[layer][None] + tok, dh),
                parts=4)

    dmeta = dh[PAD:PAD + N_META]
    grad_x = dh[PAD + N_META:][None]
    dlb_param = _lb_bwd(lb_param, jnp.concatenate(dlb_rows, axis=0))

    pieces = [
        ("final", dg_final), ("pad", jnp.zeros((SMALL_F32_ROWS - 1, D), F32)),
        ("meta", dmeta), ("mix", jnp.concatenate(g_mix, 0)), ("mlp", jnp.concatenate(g_mlp, 0)),
        ("cw", jnp.stack([c[:CONV_WIDTH] for c in small["cw"]])), ("cb", jnp.stack([v_[0] for v_ in small["vec"]])),
        ("lng", jnp.stack([v_[1] for v_ in small["vec"]])), ("lnb", jnp.stack([v_[2] for v_ in small["vec"]])),
        ("pw", jnp.stack(small["pw"])), ("pb", jnp.stack([v_[4] for v_ in small["vec"]])),
        ("ps", jnp.stack([v_[3] for v_ in small["vec"]])), ("gn", jnp.stack([jnp.sum(g_, axis=0)[0] for g_ in small["gn"]])),
        ("lb", dlb_param),
    ]
    flat = jnp.concatenate([p.reshape(-1) for _, p in pieces])
    n_small = flat.shape[0]
    rows_small = SMALL_F32_ROWS + -(-(n_small // 1024 + 1 - SMALL_F32_ROWS) // 16) * 16
    flat = jnp.pad(flat, (0, rows_small * 1024 - n_small)).reshape(rows_small, 1024)

    sm_src = [flat[:SMALL_F32_ROWS], flat[SMALL_F32_ROWS:].astype(BF16)]
    sm_sems, sm_s, sm_l, sm_tok = _push_start("small_start", sm_src, [_landing(a_.shape, a_.dtype) for a_ in sm_src],
                                              True, [0, 0])
    recv = {}
    for tag, s_thru, l_thru, sems in rs_pending:
        got = _push_wait(f"rs_wait_{tag}", s_thru, l_thru, sems, sm_tok, False)
        layer = int(tag[3:])
        if tag.startswith("mlp"):
            recv["w1", layer], recv["w2", layer] = got
        else:
            key = "ev" if layer % 2 == 0 else "od"
            recv[key + "_in", layer // 2], recv[key + "_out", layer // 2] = got

    outs = {}
    big = {"ev_in": ("ev_w_in", ev_w_in, m_ev_w_in, v_ev_w_in), "ev_out": ("ev_w_out", ev_w_out, m_ev_w_out, v_ev_w_out),
           "od_in": ("od_w_in", od_w_in, m_od_w_in, v_od_w_in), "od_out": ("od_w_out", od_w_out, m_od_w_out, v_od_w_out),
           "w1": ("mlp_w1", mlp_w1, m_mlp_w1, v_mlp_w1), "w2": ("mlp_w2", mlp_w2, m_mlp_w2, v_mlp_w2)}
    for key, (name, w, m, v) in big.items():
        res = None
        for l in range(w.shape[0]):
            res = _adamw(recv[key, l], w, m, v, layer=l, prev=res)
        outs[name] = res

    small_params = {
        "meta": ("meta_tokens", None), "mix": ("mix_norm_g", mix_norm_g, m_mix_norm_g, v_mix_norm_g),
        "mlp": ("mlp_norm_g", mlp_norm_g, m_mlp_norm_g, v_mlp_norm_g),
        "final": ("final_norm_g", final_norm_g, m_final_norm_g, v_final_norm_g),
        "cw": ("ev_conv_w", None), "cb": ("ev_conv_b", ev_conv_b, m_ev_conv_b, v_ev_conv_b),
        "lng": ("ev_ln_g", ev_ln_g, m_ev_ln_g, v_ev_ln_g), "lnb": ("ev_ln_b", ev_ln_b, m_ev_ln_b, v_ev_ln_b),
        "pw": ("ev_pool_w", ev_pool_w, m_ev_pool_w, v_ev_pool_w), "pb": ("ev_pool_b", ev_pool_b, m_ev_pool_b, v_ev_pool_b),
        "ps": ("ev_pool_scale", ev_pool_scale, m_ev_pool_scale, v_ev_pool_scale),
        "gn": ("od_gnorm_g", od_gnorm_g, m_od_gnorm_g, v_od_gnorm_g), "lb": ("lb_param", lb_param, m_lb_param, v_lb_param),
    }
    csh = ev_conv_w.shape[2]
    msh = meta_tokens.shape[1]

    def packed(which):
        parts = []
        for key, g_ in pieces:
            ent = small_params.get(key)
            if key == "pad":
                full = g_
            elif key == "meta":
                src = (meta_tokens, m_meta_tokens, v_meta_tokens)[which]
                full = lax.dynamic_update_slice(jnp.zeros((N_META, D), F32), src, (0, dev * msh))
            elif key == "cw":
                src = (ev_conv_w, m_ev_conv_w, v_ev_conv_w)[which]
                full = lax.dynamic_update_slice(jnp.zeros(g_.shape, F32), src, (0, 0, dev * csh))
            else:
                full = ent[1 + which]
            parts.append(full.reshape(-1))
        f = jnp.concatenate(parts)
        return jnp.pad(f, (0, rows_small * 1024 - n_small)).reshape(rows_small, 1024)

    got_f32, got_bf16 = _push_wait("small_wait", sm_s, sm_l, sm_sems[0], outs["mlp_w2"][0], True)
    recv_small = jnp.concatenate([got_f32, got_bf16.astype(F32)], axis=1)
    sres = [r_[0] for r_ in _adamw(recv_small, packed(0)[None], packed(1)[None], packed(2)[None])]
    off = 0
    for key, g_ in pieces:
        size = g_.size
        vals = [r_.reshape(-1)[off:off + size].reshape(g_.shape) for r_ in sres]
        off += size
        if key == "pad":
            continue
        name = small_params[key][0]
        if key == "meta":
            vals = [lax.dynamic_slice(v_, (0, dev * msh), (N_META, msh)) for v_ in vals]
        elif key == "cw":
            vals = [lax.dynamic_slice(v_, (0, 0, dev * csh), v_.shape[:2] + (csh,)) for v_ in vals]
        else:
            vals = [v_.reshape(small_params[key][1].shape) for v_ in vals]
        outs[name] = vals

    names = ["meta_tokens", "mix_norm_g", "mlp_norm_g", "final_norm_g", "ev_w_in", "ev_conv_w", "ev_conv_b", "ev_ln_g",
             "ev_ln_b", "ev_pool_w", "ev_pool_b", "ev_pool_scale", "ev_w_out", "od_w_in", "od_gnorm_g", "od_w_out",
             "lb_param", "mlp_w1", "mlp_w2"]
    result = [loss, grad_x]
    for k in range(4):
        result += [outs[nm][k] for nm in names]
    return tuple(result)
```

```python
import functools

import jax
import jax.numpy as jnp
from jax import lax
from jax.experimental import pallas as pl
from jax.experimental.pallas import tpu as pltpu

F32 = jnp.float32
BF16 = jnp.bfloat16

N_DEV = 8
N_META = 16
CHUNK = 64
PAD = CHUNK - N_META
SUB = 16
HEAD = 128
CONV_WIDTH = 31
HALO = 32
POOL_WINDOWS = (2, 4, 8, 16)
EPS = 1e-6
NEG = -1e30
ADAM_LR, ADAM_B1, ADAM_B2, ADAM_EPS, ADAM_WD, ADAM_STEP = 0.001, 0.9, 0.999, 1e-08, 0.01, 10
VMEM_LIMIT = 56 * 1024 * 1024
EV_ROWS = 416
HGRN_ROWS = 832
MM_ROWS_BIG = 2080
MM_ROWS_MID = 1040
MM_ROWS_K4 = 416
SMALL_F32_ROWS = 8
HGRN_HEADS_FWD = 8
HGRN_HEADS_BWD = 2
MESH = pl.DeviceIdType.MESH
AXES = ("x", "y", "c")
ANY = pl.BlockSpec(memory_space=pl.ANY)


def _cp(*sem):
    return pltpu.CompilerParams(dimension_semantics=sem, vmem_limit_bytes=VMEM_LIMIT)


def _tile(n, cap, mult):
    best = None
    for d in range(mult, min(n, cap) + 1, mult):
        if n % d == 0:
            best = d
    assert best is not None, (n, cap, mult)
    return best


def _nt(a, b):
    return lax.dot_general(a, b, (((1,), (1,)), ((), ())), preferred_element_type=F32)


def _tn(a, b):
    return lax.dot_general(a, b, (((0,), (0,)), ((), ())), preferred_element_type=F32)


def _nn(a, b):
    return jnp.dot(a, b, preferred_element_type=F32)


def _r16(x):
    return x.astype(BF16).astype(F32)


def _row_ids(base, n):
    return base + lax.broadcasted_iota(jnp.int32, (n, 1), 0)


def _dsilu(x, s):
    return s * (1.0 + x * (1.0 - s))


def _loss_head(h, g, tgt):
    T, D = h.shape
    tm = _tile(T, MM_ROWS_MID, 16)
    first_x = PAD + N_META

    def body(h_ref, g_ref, t_ref, loss_ref, dh_ref, dhb_ref, dg_ref):
        i = pl.program_id(0)
        x = h_ref[...]
        r = lax.rsqrt(jnp.mean(x * x, axis=-1, keepdims=True) + EPS)
        xh = x * r
        gv = g_ref[...]
        out = xh * gv
        valid = _row_ids(i * tm, tm) >= first_x
        e = jnp.where(valid, out - t_ref[...], 0.0)
        dout = e * (1.0 / D)
        dxh = dout * gv
        dx = r * (dxh - xh * jnp.mean(dxh * xh, axis=-1, keepdims=True))
        dh_ref[...] = dx
        dhb_ref[...] = dx.astype(BF16)

        @pl.when(i == 0)
        def _():
            dg_ref[...] = jnp.zeros_like(dg_ref)
            loss_ref[...] = jnp.zeros_like(loss_ref)

        dg_ref[...] += jnp.sum(dout * xh, axis=0, keepdims=True)
        loss_ref[...] += 0.5 * jnp.sum(jnp.mean(e * e, axis=-1, keepdims=True))

    row = pl.BlockSpec((tm, D), lambda i: (i, 0))
    vec = pl.BlockSpec((1, D), lambda i: (0, 0))
    return pl.pallas_call(
        body, grid=(T // tm,),
        in_specs=[row, vec, row],
        out_specs=[pl.BlockSpec((8, 128), lambda i: (0, 0)), row, row, vec],
        out_shape=[jax.ShapeDtypeStruct((8, 128), F32), jax.ShapeDtypeStruct((T, D), F32),
                   jax.ShapeDtypeStruct((T, D), BF16), jax.ShapeDtypeStruct((1, D), F32)],
        compiler_params=_cp("arbitrary"), name="loss_head")(h, g, tgt)


def _mm_nn(name, a, w, w_spec, M, N, K, tm, tn, tk, mode, extra=None, a_spec=None):
    nk = K // tk
    if a_spec is None:
        a_spec = pl.BlockSpec((tm, tk), lambda i, j, k: (i, k))
    o_spec = pl.BlockSpec((tm, tn), lambda i, j, k: (i, j))

    def body(*refs):
        if mode == "resid":
            a_ref, w_ref, e_ref = refs[:3]
            outs = refs[3:]
        else:
            a_ref, w_ref = refs[:2]
            outs = refs[2:]
        acc_ref = outs[-1] if nk > 1 else None
        part = _nn(a_ref[...], w_ref[...])

        def finish(acc):
            if mode == "f32":
                outs[0][...] = acc
            elif mode == "relu2":
                r = jnp.maximum(acc, 0.0)
                outs[0][...] = r.astype(BF16)
                outs[1][...] = (r * r).astype(BF16)
            else:
                keep = _row_ids(pl.program_id(0) * tm, tm) >= PAD
                outs[0][...] = jnp.where(keep, e_ref[...] + acc, 0.0)

        if nk == 1:
            finish(part)
        else:
            k = pl.program_id(2)

            @pl.when(k == 0)
            def _():
                acc_ref[...] = part

            @pl.when(k > 0)
            def _():
                acc_ref[...] += part

            @pl.when(k == nk - 1)
            def _():
                finish(acc_ref[...])

    in_specs = [a_spec, w_spec(tk, tn)]
    args = [a, w]
    if mode == "resid":
        in_specs.append(o_spec)
        args.append(extra)
    if mode == "relu2":
        out_specs = [o_spec, o_spec]
        out_shape = [jax.ShapeDtypeStruct((M, N), BF16)] * 2
    else:
        out_specs = [o_spec]
        out_shape = [jax.ShapeDtypeStruct((M, N), F32)]
    scratch = [pltpu.VMEM((tm, tn), F32)] if nk > 1 else []
    res = pl.pallas_call(
        body, grid=(M // tm, N // tn, nk), in_specs=in_specs, out_specs=out_specs, out_shape=out_shape,
        scratch_shapes=scratch, compiler_params=_cp("parallel", "parallel", "arbitrary"), name=name)(*args)
    return res if mode == "relu2" else res[0]


def _mm_rms_nn(name, h, g, w, tm, tn, mode):
    M, K = h.shape
    N = w.shape[1]

    def body(h_ref, g_ref, w_ref, n_ref, *outs):
        @pl.when(pl.program_id(1) == 0)
        def _():
            ch = _tile(tm, 256, 16)

            def chunk(c, carry):
                rows = pl.ds(pl.multiple_of(c * ch, ch), ch)
                x = h_ref[rows, :]
                r = lax.rsqrt(jnp.mean(x * x, axis=-1, keepdims=True) + EPS)
                n_ref[rows, :] = ((x * r) * g_ref[...]).astype(BF16)
                return carry

            lax.fori_loop(0, tm // ch, chunk, 0)

        acc = _nn(n_ref[...], w_ref[...])
        if mode == "f32":
            outs[0][...] = acc
        else:
            r = jnp.maximum(acc, 0.0)
            outs[0][...] = r.astype(BF16)
            outs[1][...] = (r * r).astype(BF16)

    row = pl.BlockSpec((tm, K), lambda i, j: (i, 0))
    o_spec = pl.BlockSpec((tm, tn), lambda i, j: (i, j))
    n_out = 1 if mode == "f32" else 2
    return pl.pallas_call(
        body, grid=(M // tm, N // tn),
        in_specs=[row, pl.BlockSpec((1, K), lambda i, j: (0, 0)), pl.BlockSpec((K, tn), lambda i, j: (0, j))],
        out_specs=[row] + [o_spec] * n_out,
        out_shape=[jax.ShapeDtypeStruct((M, K), BF16)] + [jax.ShapeDtypeStruct((M, N), F32 if mode == "f32" else BF16)] * n_out,
        compiler_params=_cp("parallel", "arbitrary"), name=name)(h, g, w)


def _mm_nt(name, dy, w, dy_spec, w_spec, M, J, N, tm, tj, tn, mode, extra=None, parts=1):
    nk = N // tn
    o_spec = pl.BlockSpec((tm, tj), lambda i, j, k: (i, j))
    n_extra = {"f32": 0, "dact": 1, "rms": 3}[mode]
    if mode == "rms":
        assert nk == 1 and tj == J

    def body(*refs):
        dy_ref, w_ref = refs[:2]
        ex = refs[2:2 + n_extra]
        outs = refs[2 + n_extra:]
        acc_ref = outs[-1] if nk > 1 else None
        if parts == 1:
            part = _nt(dy_ref[...], w_ref[...])
        else:
            wq = tn // parts
            part = _nt(dy_ref[0], w_ref[:, 0:wq])
            for q in range(1, parts):
                part = part + _nt(dy_ref[q], w_ref[:, q * wq:(q + 1) * wq])

        def finish(acc):
            if mode == "f32":
                outs[0][...] = acc
            elif mode == "dact":
                outs[0][...] = (acc * (2.0 * ex[0][...].astype(F32))).astype(BF16)
            else:
                h_ref, g_ref, dres_ref = ex
                dh_ref, dhb_ref, dg_ref = outs[:3]
                i = pl.program_id(0)

                @pl.when(i == 0)
                def _():
                    dg_ref[...] = jnp.zeros_like(dg_ref)

                ch = _tile(tm, 256, 16)
                for c0 in range(0, tm, ch):
                    a_c = acc[c0:c0 + ch]
                    x = h_ref[c0:c0 + ch, :]
                    r = lax.rsqrt(jnp.mean(x * x, axis=-1, keepdims=True) + EPS)
                    xh = x * r
                    dxh = a_c * g_ref[...]
                    dx = r * (dxh - xh * jnp.mean(dxh * xh, axis=-1, keepdims=True))
                    keep = _row_ids(i * tm + c0, ch) >= PAD
                    dh = jnp.where(keep, dres_ref[c0:c0 + ch, :] + dx, 0.0)
                    dh_ref[c0:c0 + ch, :] = dh
                    dhb_ref[c0:c0 + ch, :] = dh.astype(BF16)
                    dg_ref[...] += jnp.sum(a_c * xh, axis=0, keepdims=True)

        if nk == 1:
            finish(part)
        else:
            k = pl.program_id(2)

            @pl.when(k == 0)
            def _():
                acc_ref[...] = part

            @pl.when(k > 0)
            def _():
                acc_ref[...] += part

            @pl.when(k == nk - 1)
            def _():
                finish(acc_ref[...])

    in_specs = [dy_spec(tm, tn), w_spec(tj, tn)]
    args = [dy, w]
    scratch = [pltpu.VMEM((tm, tj), F32)] if nk > 1 else []
    if mode == "rms":
        vec = pl.BlockSpec((1, J), lambda i, j, k: (0, 0))
        h, g, dres = extra
        res = pl.pallas_call(
            body, grid=(M // tm, 1, 1), in_specs=in_specs + [o_spec, vec, o_spec], out_specs=[o_spec, o_spec, vec],
            out_shape=[jax.ShapeDtypeStruct((M, J), F32), jax.ShapeDtypeStruct((M, J), BF16),
                       jax.ShapeDtypeStruct((1, J), F32)],
            compiler_params=_cp("arbitrary", "arbitrary", "arbitrary"), name=name)(*args, h, g, dres)
        return res
    if mode == "dact":
        in_specs.append(o_spec)
        args.append(extra)
    return pl.pallas_call(
        body, grid=(M // tm, J // tj, nk), in_specs=in_specs, out_specs=[o_spec],
        out_shape=[jax.ShapeDtypeStruct((M, J), BF16 if mode == "dact" else F32)],
        scratch_shapes=scratch, compiler_params=_cp("parallel", "parallel", "arbitrary"), name=name)(*args)[0]


def _mm_tn(name, x, dy, x_spec, dy_spec, o_spec, o_shape, T, K, N, tt, tk, tn):
    nt = T // tt

    def body(x_ref, dy_ref, o_ref, *acc):
        part = _tn(x_ref[...], dy_ref[...])
        if nt == 1:
            o_ref[...] = part.astype(BF16)
            return
        acc_ref = acc[0]
        t = pl.program_id(2)

        @pl.when(t == 0)
        def _():
            acc_ref[...] = part

        @pl.when(t > 0)
        def _():
            acc_ref[...] += part

        @pl.when(t == nt - 1)
        def _():
            o_ref[...] = acc_ref[...].astype(BF16)

    return pl.pallas_call(
        body, grid=(K // tk, N // tn, nt), in_specs=[x_spec(tt, tk), dy_spec(tt, tn)], out_specs=o_spec(tk, tn),
        out_shape=jax.ShapeDtypeStruct(o_shape, BF16), scratch_shapes=[pltpu.VMEM((tk, tn), F32)] if nt > 1 else [],
        compiler_params=_cp("parallel", "parallel", "arbitrary"), name=name)(x, dy)


def _pool_counts(base, n, w):
    pos = _row_ids(base, n) - PAD
    return jnp.clip(pos + 1, 1, w).astype(F32)


def _shifted_copies(buf, rows):
    buf[0, rows:rows + 8, :] = jnp.zeros((8, buf.shape[2]), F32)

    def blk(s, carry):
        b = pl.multiple_of(s * HALO, HALO)
        win = buf[0, pl.ds(b, HALO + 8), :]
        for r in range(1, 8):
            buf[r, pl.ds(b, HALO), :] = win[r:r + HALO]
        return carry

    lax.fori_loop(0, rows // HALO, blk, 0)


def _ev_fwd(u, cw, cb, lg, lb, pw, pb, ps):
    T = u.shape[0]
    C = 512
    tm = _tile(T, EV_ROWS, HALO)
    nsub = tm // HALO
    hb = tm // HALO

    def body(val_ref, gate_ref, pin_ref, valh_ref, gateh_ref, pinh_ref, cw_ref, cb_ref, lg_ref, lb_ref, pw_ref,
             pb_ref, ps_ref, yab_ref, yc_ref, a_ext, p_ext, d_buf):
        i = pl.program_id(0)
        nf = (i > 0).astype(F32)
        a_ext[0, 0:HALO, :] = valh_ref[...] * jax.nn.sigmoid(gateh_ref[...]) * nf
        a_ext[0, HALO:HALO + tm, :] = val_ref[...] * jax.nn.sigmoid(gate_ref[...])
        p_ext[0:HALO, :] = pinh_ref[...] * nf
        p_ext[HALO:, :] = pin_ref[...]
        _shifted_copies(a_ext, tm + HALO)

        def sub(s, carry):
            base = pl.multiple_of(s * HALO, HALO)
            acc = jnp.zeros((HALO, C), F32) + cb_ref[...]
            for j in range(CONV_WIDTH):
                off = 2 + j
                acc = acc + cw_ref[pl.ds(j, 1), :] * a_ext[off % 8, pl.ds(pl.multiple_of(base + off // 8 * 8, 8), HALO), :]
            yc_ref[pl.ds(base, HALO), :] = acc
            mu = jnp.mean(acc, axis=-1, keepdims=True)
            yc = acc - mu
            rstd = lax.rsqrt(jnp.mean(yc * yc, axis=-1, keepdims=True) + EPS)
            z = (yc * rstd) * lg_ref[...] + lb_ref[...]
            yab_ref[pl.ds(base, HALO), 0:C] = (z * jax.nn.sigmoid(z)).astype(BF16)
            pwin = p_ext[pl.ds(base, 2 * HALO), :]
            for gi, w in enumerate(POOL_WINDOWS):
                lo, hi = gi * HEAD, (gi + 1) * HEAD
                x = pwin[HALO:, lo:hi]
                tot = x
                for k in range(1, w):
                    tot = tot + pwin[HALO - k:2 * HALO - k, lo:hi]
                cnt = _pool_counts(i * tm + base, HALO, w)
                d_buf[pl.ds(base, HALO), lo:hi] = (tot / cnt - x).astype(BF16)
            return carry

        lax.fori_loop(0, nsub, sub, 0, unroll=2)
        for gi in range(len(POOL_WINDOWS)):
            lo, hi = gi * HEAD, (gi + 1) * HEAD
            y = _nn(d_buf[:, lo:hi], pw_ref[gi]) + pb_ref[:, lo:hi]
            yab_ref[:, C + lo:C + hi] = (y * ps_ref[:, lo:hi]).astype(BF16)

    def main(c):
        return pl.BlockSpec((tm, C), lambda i: (i, c))

    def halo(c):
        return pl.BlockSpec((HALO, C), lambda i: (jnp.maximum(i * hb - 1, 0), c))

    vec = pl.BlockSpec((1, C), lambda i: (0, 0))
    return pl.pallas_call(
        body, grid=(T // tm,),
        in_specs=[main(0), main(1), main(2), halo(0), halo(1), halo(2),
                  pl.BlockSpec((32, C), lambda i: (0, 0)), vec, vec, vec,
                  pl.BlockSpec((4, HEAD, HEAD), lambda i: (0, 0, 0)), vec, vec],
        out_specs=[pl.BlockSpec((tm, 2 * C), lambda i: (i, 0)), pl.BlockSpec((tm, C), lambda i: (i, 0))],
        out_shape=[jax.ShapeDtypeStruct((T, 2 * C), BF16), jax.ShapeDtypeStruct((T, C), F32)],
        scratch_shapes=[pltpu.VMEM((8, tm + HALO + 8, C), F32), pltpu.VMEM((tm + HALO, C), F32),
                        pltpu.VMEM((tm, C), BF16)],
        compiler_params=_cp("parallel"), name="ev_fwd")(u, u, u, u, u, u, cw, cb, lg, lb, pw, pb, ps)


def _ev_bwd(dyab, yc, u, cw, lg, lb, pw, pwt, pb, ps):
    T = u.shape[0]
    C = 512
    tm = _tile(T, EV_ROWS, HALO)
    nsub = tm // HALO
    hb = tm // HALO
    nblk = T // tm
    E = tm + HALO

    def body(dya_ref, dyb_ref, dyah_ref, dybh_ref, yc_ref, ych_ref, val_ref, gate_ref, pin_ref, valh_ref, gateh_ref,
             pinh_ref, cw_ref, lg_ref, lb_ref, pw_ref, pwt_ref, pb_ref, ps_ref,
             du_ref, dcw_ref, dvec_ref, dpw_ref,
             dy_ext, a_ext, p_ext, ddc_ext, dd_buf, d_buf, dpre_buf, dcw_acc, vec_acc):
        i = pl.program_id(0)
        nf = (i > 0).astype(F32)
        nl = (i < nblk - 1).astype(F32)

        @pl.when(i == 0)
        def _():
            dcw_ref[...] = jnp.zeros_like(dcw_ref)
            dvec_ref[...] = jnp.zeros_like(dvec_ref)
            dpw_ref[...] = jnp.zeros_like(dpw_ref)

        dcw_acc[...] = jnp.zeros_like(dcw_acc)
        vec_acc[...] = jnp.zeros_like(vec_acc)
        a_ext[0, 0:HALO, :] = valh_ref[...] * jax.nn.sigmoid(gateh_ref[...]) * nf
        a_ext[0, HALO:E, :] = val_ref[...] * jax.nn.sigmoid(gate_ref[...])
        p_ext[0:HALO, :] = pinh_ref[...] * nf
        p_ext[HALO:, :] = pin_ref[...]
        _shifted_copies(a_ext, E)

        def ln_bwd(y, dya, main):
            mu = jnp.mean(y, axis=-1, keepdims=True)
            ycen = y - mu
            rstd = lax.rsqrt(jnp.mean(ycen * ycen, axis=-1, keepdims=True) + EPS)
            yh = ycen * rstd
            z = yh * lg_ref[...] + lb_ref[...]
            sz = jax.nn.sigmoid(z)
            dz = dya * _dsilu(z, sz)
            dyh = dz * lg_ref[...]
            dy = rstd * (dyh - jnp.mean(dyh, axis=-1, keepdims=True) - yh * jnp.mean(dyh * yh, axis=-1, keepdims=True))
            if main:
                vec_acc[1] += jnp.sum((dz * yh).reshape(HALO // 8, 8, C), axis=0)
                vec_acc[2] += jnp.sum(dz.reshape(HALO // 8, 8, C), axis=0)
                vec_acc[0] += jnp.sum(dy.reshape(HALO // 8, 8, C), axis=0)
            return dy

        def pool_dd(dyb, base, main):
            dpre = dyb * ps_ref[...]
            for gi, w in enumerate(POOL_WINDOWS):
                lo, hi = gi * HEAD, (gi + 1) * HEAD
                dd = _nn(dpre[:, lo:hi].astype(BF16), pwt_ref[gi])
                cnt = _pool_counts(i * tm + base, HALO, w)
                ddc_ext[pl.ds(base, HALO), lo:hi] = dd / cnt
                if main:
                    dd_buf[pl.ds(base, HALO), lo:hi] = dd
            if main:
                dpre_buf[pl.ds(base, HALO), :] = dpre.astype(BF16)
                vec_acc[4] += jnp.sum(dpre.reshape(HALO // 8, 8, C), axis=0)

        def p1(s, carry):
            base = pl.multiple_of(s * HALO, HALO)
            dy_ext[0, pl.ds(base, HALO), :] = ln_bwd(yc_ref[pl.ds(base, HALO), :], dya_ref[pl.ds(base, HALO), :], True)
            pool_dd(dyb_ref[pl.ds(base, HALO), :], base, True)
            return carry

        lax.fori_loop(0, nsub, p1, 0, unroll=2)
        dy_ext[0, tm:E, :] = ln_bwd(ych_ref[...], dyah_ref[...], False) * nl
        _shifted_copies(dy_ext, E)
        dpre_h = dybh_ref[...] * ps_ref[...] * nl
        for gi, w in enumerate(POOL_WINDOWS):
            lo, hi = gi * HEAD, (gi + 1) * HEAD
            dd = _nn(dpre_h[:, lo:hi].astype(BF16), pwt_ref[gi])
            ddc_ext[tm:, lo:hi] = dd / _pool_counts(i * tm + tm, HALO, w)

        def p2(s, carry):
            base = pl.multiple_of(s * HALO, HALO)
            dy_m = dy_ext[0, pl.ds(base, HALO), :]
            da = jnp.zeros((HALO, C), F32)
            for j in range(CONV_WIDTH):
                sh = CONV_WIDTH - 1 - j
                off = 2 + j
                da = da + cw_ref[pl.ds(j, 1), :] * dy_ext[sh % 8, pl.ds(pl.multiple_of(base + sh // 8 * 8, 8), HALO), :]
                a_j = a_ext[off % 8, pl.ds(pl.multiple_of(base + off // 8 * 8, 8), HALO), :]
                dcw_acc[j] += jnp.sum((dy_m * a_j).reshape(HALO // 8, 8, C), axis=0)
            v = val_ref[pl.ds(base, HALO), :]
            g = gate_ref[pl.ds(base, HALO), :]
            sg = jax.nn.sigmoid(g)
            du_ref[pl.ds(base, HALO), 0:C] = (da * sg).astype(BF16)
            du_ref[pl.ds(base, HALO), C:2 * C] = (da * v * sg * (1.0 - sg)).astype(BF16)
            pwin = p_ext[pl.ds(base, 2 * HALO), :]
            cwin = ddc_ext[pl.ds(base, 2 * HALO), :]
            for gi, w in enumerate(POOL_WINDOWS):
                lo, hi = gi * HEAD, (gi + 1) * HEAD
                x = pwin[HALO:, lo:hi]
                tot = x
                back = cwin[0:HALO, lo:hi]
                for k in range(1, w):
                    tot = tot + pwin[HALO - k:2 * HALO - k, lo:hi]
                    back = back + cwin[k:k + HALO, lo:hi]
                cnt = _pool_counts(i * tm + base, HALO, w)
                d_buf[pl.ds(base, HALO), lo:hi] = (tot / cnt - x).astype(BF16)
                du_ref[pl.ds(base, HALO), 2 * C + lo:2 * C + hi] = (back - dd_buf[pl.ds(base, HALO), lo:hi]).astype(BF16)
            return carry

        lax.fori_loop(0, nsub, p2, 0)
        for gi in range(len(POOL_WINDOWS)):
            lo, hi = gi * HEAD, (gi + 1) * HEAD
            pre = _nn(d_buf[:, lo:hi], pw_ref[gi]) + pb_ref[:, lo:hi]
            vec_acc[3, :, lo:hi] += jnp.sum((dyb_ref[:, lo:hi] * pre).reshape(tm // 8, 8, HEAD), axis=0)
            dpw_ref[gi] += _tn(d_buf[:, lo:hi], dpre_buf[:, lo:hi])
        for j in range(CONV_WIDTH):
            dcw_ref[pl.ds(j, 1), :] += jnp.sum(dcw_acc[j], axis=0, keepdims=True)
        for r in range(5):
            dvec_ref[pl.ds(r, 1), :] += jnp.sum(vec_acc[r], axis=0, keepdims=True)

    def main(c, width=C):
        return pl.BlockSpec((tm, width), lambda i: (i, c))

    def prev(c):
        return pl.BlockSpec((HALO, C), lambda i: (jnp.maximum(i * hb - 1, 0), c))

    def nxt(c):
        return pl.BlockSpec((HALO, C), lambda i: (jnp.minimum((i + 1) * hb, T // HALO - 1), c))

    vec = pl.BlockSpec((1, C), lambda i: (0, 0))
    mat = pl.BlockSpec((4, HEAD, HEAD), lambda i: (0, 0, 0))
    return pl.pallas_call(
        body, grid=(nblk,),
        in_specs=[main(0), main(1), nxt(0), nxt(1), main(0), nxt(0), main(0), main(1), main(2), prev(0), prev(1),
                  prev(2), pl.BlockSpec((32, C), lambda i: (0, 0)), vec, vec, mat, mat, vec, vec],
        out_specs=[pl.BlockSpec((tm, 3 * C), lambda i: (i, 0)), pl.BlockSpec((32, C), lambda i: (0, 0)),
                   pl.BlockSpec((8, C), lambda i: (0, 0)), mat],
        out_shape=[jax.ShapeDtypeStruct((T, 3 * C), BF16), jax.ShapeDtypeStruct((32, C), F32),
                   jax.ShapeDtypeStruct((8, C), F32), jax.ShapeDtypeStruct((4, HEAD, HEAD), F32)],
        scratch_shapes=[pltpu.VMEM((8, E + 8, C), F32), pltpu.VMEM((8, E + 8, C), F32), pltpu.VMEM((E, C), F32),
                        pltpu.VMEM((E, C), F32), pltpu.VMEM((tm, C), F32), pltpu.VMEM((tm, C), BF16),
                        pltpu.VMEM((tm, C), BF16), pltpu.VMEM((32, 8, C), F32), pltpu.VMEM((8, 8, C), F32)],
        compiler_params=_cp("arbitrary"), name="ev_bwd")(
            dyab, dyab, dyab, dyab, yc, yc, u, u, u, u, u, u, cw, lg, lb, pw, pwt, pb, ps)


def _cumsum_rows(x, reverse=False):
    n = x.shape[0]
    rid = lax.broadcasted_iota(jnp.int32, (n, 1), 0)
    k = 1
    while k < n:
        if reverse:
            sh = jnp.where(rid < n - k, pltpu.roll(x, n - k, 0), 0.0)
        else:
            sh = jnp.where(rid >= k, pltpu.roll(x, k, 0), 0.0)
        x = x + sh
        k *= 2
    return x


def _hgrn_gates(qr, fr, lbv):
    sq = jax.nn.sigmoid(qr)
    sg = jax.nn.sigmoid(fr)
    fg = lbv + (1.0 - lbv) * sg
    return qr * sq, sq, sg, fg, 1.0 - fg, jnp.log(fg)


def _hgrn_fwd(u, lbv, gn):
    T = u.shape[0]
    H = 8
    RB = _tile(T, HGRN_ROWS, CHUNK)
    NC = RB // CHUNK
    NS = CHUNK // SUB

    HP = HGRN_HEADS_FWD
    W = HP * HEAD

    def body(q_ref, f_ref, i_ref, g_ref, lb_ref, gn_ref, y_ref, o_ref, s0_ref, st, qs, ks, bs, vs, os_):
        rb = pl.program_id(1)

        @pl.when(rb == 0)
        def _():
            st[...] = jnp.zeros_like(st)

        t8 = lax.broadcasted_iota(jnp.int32, (8, 1), 0)

        def head(hh, c, rows):
            sl = slice(hh * HEAD, (hh + 1) * HEAD)
            q, _, _, _, kk, lf = _hgrn_gates(q_ref[rows, sl], f_ref[rows, sl], lb_ref[:, sl])
            v = i_ref[rows, sl]
            b = _cumsum_rows(lf)
            qs[hh] = q
            ks[hh] = kk
            bs[hh] = b
            vs[hh] = v
            st0 = st[hh]
            s0_ref[hh, c] = st0
            os_[hh] = _nt((q * jnp.exp(b)).astype(BF16), st0.astype(BF16))
            for I in range(NS):
                lo = I * SUB
                qI = qs[hh, lo:lo + SUB, :]
                bI = bs[hh, lo:lo + SUB, :]
                oI = jnp.zeros((SUB, HEAD), F32)
                if I > 0:
                    bprev = bs[hh, pl.ds(lo - 1, 1), :]
                    qt = _r16(qI * jnp.exp(bI - bprev))
                    kt = _r16(ks[hh, 0:lo, :] * jnp.exp(bprev - bs[hh, 0:lo, :]))
                    A = _nt(qt, kt)
                    oI = oI + _nn(_r16(A), _r16(vs[hh, 0:lo, :]))
                od = [jnp.zeros((8, HEAD), F32) for _ in range(SUB // 8)]
                for s in range(SUB):
                    row = pl.ds(lo + s, 1)
                    brow, krow, vrow = bs[hh, row, :], ks[hh, row, :], vs[hh, row, :]
                    for ti in range(SUB // 8):
                        o8 = 8 * ti
                        if s > o8 + 7:
                            continue
                        d = bI[o8:o8 + 8] - brow
                        if s > o8:
                            d = jnp.where(t8 >= s - o8, d, NEG)
                        col = jnp.sum(qI[o8:o8 + 8] * jnp.exp(d) * krow, axis=1, keepdims=True)
                        od[ti] = od[ti] + col * vrow
                os_[hh, lo:lo + SUB, :] += oI + jnp.concatenate(od, axis=0)
            blast = bs[hh, pl.ds(CHUNK - 1, 1), :]
            kh = kk * jnp.exp(blast - b)
            st[hh] = st0 * jnp.exp(blast) + _tn(v.astype(BF16), kh.astype(BF16))
            o = os_[hh]
            o_ref[rows, sl] = o
            rr = lax.rsqrt(jnp.mean(o * o, axis=-1, keepdims=True) + EPS)
            gr = g_ref[rows, sl]
            y_ref[rows, sl] = (((o * rr) * gn_ref[...]) * (gr * jax.nn.sigmoid(gr))).astype(BF16)

        def chunk(c, carry):
            rows = pl.ds(pl.multiple_of(c * CHUNK, CHUNK), CHUNK)
            for hh in range(HP):
                head(hh, c, rows)
            return carry

        lax.fori_loop(0, NC, chunk, 0)

    def blk(q):
        return pl.BlockSpec((RB, W), lambda h, r: (r, q * (H // HP) + h))

    sc = lambda: pltpu.VMEM((HP, CHUNK, HEAD), F32)
    return pl.pallas_call(
        body, grid=(H // HP, T // RB),
        in_specs=[blk(0), blk(1), blk(2), blk(3), pl.BlockSpec((1, W), lambda h, r: (0, h)),
                  pl.BlockSpec((1, HEAD), lambda h, r: (0, 0))],
        out_specs=[pl.BlockSpec((RB, W), lambda h, r: (r, h)), pl.BlockSpec((RB, W), lambda h, r: (r, h)),
                   pl.BlockSpec((HP, NC, HEAD, HEAD), lambda h, r: (h, r, 0, 0))],
        out_shape=[jax.ShapeDtypeStruct((T, H * HEAD), BF16), jax.ShapeDtypeStruct((T, H * HEAD), F32),
                   jax.ShapeDtypeStruct((H, T // CHUNK, HEAD, HEAD), F32)],
        scratch_shapes=[pltpu.VMEM((HP, HEAD, HEAD), F32), sc(), sc(), sc(), sc(), sc()],
        compiler_params=_cp("parallel", "arbitrary"), name="hgrn_fwd")(u, u, u, u, lbv, gn)


def _hgrn_bwd(dy, o, s0, u, lbv, gn):
    T = u.shape[0]
    H = 8
    RB = _tile(T, HGRN_ROWS, CHUNK)
    NB = T // RB
    NC = RB // CHUNK
    NS = CHUNK // SUB

    def body(q_ref, f_ref, i_ref, g_ref, lb_ref, gn_ref, o_ref, dy_ref, s0_ref, du_ref, dlb_ref, dgn_ref,
             dst, qs, ks, bs, vs, dos, dqs, dks, dki, dvs, dbs):
        rb = pl.program_id(1)

        @pl.when(rb == 0)
        def _():
            dst[...] = jnp.zeros_like(dst)
            dlb_ref[...] = jnp.zeros_like(dlb_ref)
            dgn_ref[...] = jnp.zeros_like(dgn_ref)

        t8 = lax.broadcasted_iota(jnp.int32, (8, 1), 0)
        lane = lax.broadcasted_iota(jnp.int32, (8, HEAD), 1)
        gnv = gn_ref[...]

        def head(hh, c, rows):
            sl = slice(hh * HEAD, (hh + 1) * HEAD)
            lbv_ = lb_ref[:, sl]
            qr = q_ref[rows, sl]
            q, sq, sg, fg, kk, lf = _hgrn_gates(qr, f_ref[rows, sl], lbv_)
            v = i_ref[rows, sl]
            gr = g_ref[rows, sl]
            b = _cumsum_rows(lf)
            eb = jnp.exp(b)
            ov = o_ref[rows, sl]
            dyv = dy_ref[rows, sl]
            rr = lax.rsqrt(jnp.mean(ov * ov, axis=-1, keepdims=True) + EPS)
            oh = ov * rr
            gs = jax.nn.sigmoid(gr)
            dgr = dyv * (oh * gnv) * _dsilu(gr, gs)
            dnrm = dyv * (gr * gs)
            dgn_ref[hh] += jnp.sum(dnrm * oh, axis=0, keepdims=True)
            t1 = dnrm * gnv
            do = rr * (t1 - oh * jnp.mean(t1 * oh, axis=-1, keepdims=True))
            qs[hh] = q
            ks[hh] = kk
            bs[hh] = b
            vs[hh] = v
            dos[hh] = do
            st0 = s0_ref[hh, c]
            dS = dst[hh]
            do_b = do.astype(BF16)
            blast = bs[hh, pl.ds(CHUNK - 1, 1), :]
            elast = jnp.exp(blast - b)
            dq_inter = _nn(do_b, st0.astype(BF16)) * eb
            dqs[hh] = dq_inter
            dbs[hh] = q * dq_inter
            kh = kk * elast
            dvs[hh] = _nt(kh.astype(BF16), dS.astype(BF16))
            dk_inter = _nn(v.astype(BF16), dS.astype(BF16)) * elast
            dki[hh] = dk_inter
            dks[hh] = jnp.zeros((CHUNK, HEAD), F32)
            for I in range(NS):
                lo = I * SUB
                qI = qs[hh, lo:lo + SUB, :]
                bI = bs[hh, lo:lo + SUB, :]
                doI = dos[hh, lo:lo + SUB, :]
                dqI = jnp.zeros((SUB, HEAD), F32)
                dbI = jnp.zeros((SUB, HEAD), F32)
                if I > 0:
                    bprev = bs[hh, pl.ds(lo - 1, 1), :]
                    eq = jnp.exp(bI - bprev)
                    ek = jnp.exp(bprev - bs[hh, 0:lo, :])
                    qt = _r16(qI * eq)
                    kt = _r16(ks[hh, 0:lo, :] * ek)
                    A = _r16(_nt(qt, kt))
                    doI_b = _r16(doI)
                    dA = _r16(_nt(doI_b, _r16(vs[hh, 0:lo, :])))
                    dvs[hh, 0:lo, :] += _tn(A, doI_b)
                    dqt = _nn(dA, kt)
                    dkt = _tn(dA, qt)
                    dqI = dqI + dqt * eq
                    dbI = dbI + qt.astype(F32) * dqt
                    dks[hh, 0:lo, :] += dkt * ek
                    dbs[hh, 0:lo, :] -= kt.astype(F32) * dkt
                dq_t = [jnp.zeros((8, HEAD), F32) for _ in range(SUB // 8)]
                a_t = [jnp.zeros((8, HEAD), F32) for _ in range(SUB // 8)]
                for s in range(SUB):
                    row = pl.ds(lo + s, 1)
                    brow, krow, vrow = bs[hh, row, :], ks[hh, row, :], vs[hh, row, :]
                    dk_s = None
                    for ti in range(SUB // 8):
                        o8 = 8 * ti
                        if s > o8 + 7:
                            continue
                        d = bI[o8:o8 + 8] - brow
                        if s > o8:
                            d = jnp.where(t8 >= s - o8, d, NEG)
                        Es = jnp.exp(d)
                        qE = qI[o8:o8 + 8] * Es
                        col = jnp.sum(qE * krow, axis=1, keepdims=True)
                        a_t[ti] = jnp.where(lane == s, col, a_t[ti])
                        dcol = jnp.sum(doI[o8:o8 + 8] * vrow, axis=1, keepdims=True)
                        dq_t[ti] = dq_t[ti] + (dcol * Es) * krow
                        part = jnp.sum(dcol * qE, axis=0, keepdims=True)
                        dk_s = part if dk_s is None else dk_s + part
                    dks[hh, row, :] += dk_s
                    dbs[hh, row, :] -= krow * dk_s
                a_d = jnp.concatenate(a_t, axis=0)
                dq_d = jnp.concatenate(dq_t, axis=0)
                dvs[hh, lo:lo + SUB, :] += _tn(a_d, doI)[0:SUB]
                dqI = dqI + dq_d
                dbI = dbI + qI * dq_d
                dqs[hh, lo:lo + SUB, :] += dqI
                dbs[hh, lo:lo + SUB, :] += dbI
            kdk = kk * dki[hh]
            excl = _cumsum_rows(kdk) - kdk
            suff = _cumsum_rows(dbs[hh], reverse=True)
            gdec = jnp.sum(dS * st0, axis=0, keepdims=True) * jnp.exp(blast)
            dlf = suff + excl + gdec
            dk = dks[hh] + dki[hh]
            dfg = dlf / fg - dk
            dlb_ref[:, sl] += jnp.sum(dfg * (1.0 - sg), axis=0, keepdims=True)
            du_ref[0, rows, sl] = (dqs[hh] * _dsilu(qr, sq)).astype(BF16)
            du_ref[1, rows, sl] = (dfg * (1.0 - lbv_) * sg * (1.0 - sg)).astype(BF16)
            du_ref[2, rows, sl] = dvs[hh].astype(BF16)
            du_ref[3, rows, sl] = dgr.astype(BF16)
            dst[hh] = dS * jnp.exp(blast) + _tn(do_b, (q * eb).astype(BF16))

        def chunk(cc, carry):
            c = NC - 1 - cc
            rows = pl.ds(pl.multiple_of(c * CHUNK, CHUNK), CHUNK)
            for hh in range(HP):
                head(hh, c, rows)
            return carry

        lax.fori_loop(0, NC, chunk, 0)

    HP = HGRN_HEADS_BWD
    W = HP * HEAD

    def blk(qd):
        return pl.BlockSpec((RB, W), lambda h, r: (NB - 1 - r, qd * (H // HP) + h))

    hblk = pl.BlockSpec((RB, W), lambda h, r: (NB - 1 - r, h))
    sc = lambda: pltpu.VMEM((HP, CHUNK, HEAD), F32)
    return pl.pallas_call(
        body, grid=(H // HP, NB),
        in_specs=[blk(0), blk(1), blk(2), blk(3), pl.BlockSpec((1, W), lambda h, r: (0, h)),
                  pl.BlockSpec((1, HEAD), lambda h, r: (0, 0)), hblk, hblk,
                  pl.BlockSpec((HP, NC, HEAD, HEAD), lambda h, r: (h, NB - 1 - r, 0, 0))],
        out_specs=[pl.BlockSpec((4, RB, W), lambda h, r: (0, NB - 1 - r, h)),
                   pl.BlockSpec((1, W), lambda h, r: (0, h)), pl.BlockSpec((HP, 1, HEAD), lambda h, r: (h, 0, 0))],
        out_shape=[jax.ShapeDtypeStruct((4, T, H * HEAD), BF16), jax.ShapeDtypeStruct((1, H * HEAD), F32),
                   jax.ShapeDtypeStruct((H, 1, HEAD), F32)],
        scratch_shapes=[pltpu.VMEM((HP, HEAD, HEAD), F32)] + [sc() for _ in range(10)],
        compiler_params=_cp("parallel", "arbitrary"), name="hgrn_bwd")(u, u, u, u, lbv, gn, o, dy, s0)


def _softmax_rows(p_ref, L):
    rows = [p_ref[pl.ds(l, 1), :] for l in range(L)]
    m = rows[0]
    for r in rows[1:]:
        m = jnp.maximum(m, r)
    e = [jnp.exp(r - m) for r in rows]
    tot = e[0]
    for t in e[1:]:
        tot = tot + t
    return [t / tot for t in e]


def _lb_fwd(lbp):
    L, D = lbp.shape

    def body(p_ref, o_ref):
        sm = _softmax_rows(p_ref, L)
        acc = jnp.zeros((1, D), F32)
        o_ref[pl.ds(0, 1), :] = acc
        for l in range(1, L):
            acc = acc + sm[l]
            o_ref[pl.ds(l, 1), :] = acc

    return pl.pallas_call(body, out_shape=jax.ShapeDtypeStruct((L, D), F32), name="lb_fwd")(lbp)


def _lb_bwd(lbp, dlb):
    L, D = lbp.shape

    def body(p_ref, d_ref, o_ref):
        sm = _softmax_rows(p_ref, L)
        dsm = [jnp.zeros((1, D), F32)]
        for i in range(1, L):
            t = jnp.zeros((1, D), F32)
            for l in range(i, L):
                t = t + d_ref[pl.ds(l, 1), :]
            dsm.append(t)
        dot = jnp.zeros((1, D), F32)
        for i in range(L):
            dot = dot + dsm[i] * sm[i]
        for i in range(L):
            o_ref[pl.ds(i, 1), :] = sm[i] * (dsm[i] - dot)

    return pl.pallas_call(body, out_shape=jax.ShapeDtypeStruct((L, D), F32), name="lb_bwd")(lbp, dlb)


def _my_pos():
    return lax.axis_index("x"), lax.axis_index("y"), lax.axis_index("c")


def _peer(mask):
    x, y, c = _my_pos()
    mx, my, mc = (mask >> 2) & 1, (mask >> 1) & 1, mask & 1
    px = (1 - x) if mx else x
    py = (1 - y) if my else y
    pc = (1 - c) if mc else c
    return (px, py, pc), 4 * px + 2 * py + pc


HBM_SPEC =pl.BlockSpec(memory_space=pltpu.HBM)
SEM_SPEC = pl.BlockSpec(memory_space=pltpu.SEMAPHORE)
EFFECT = pltpu.SideEffectType.DATAFLOW_SIDE_EFFECTING


def _hbm(a):
    return pltpu.with_memory_space_constraint(a, pltpu.HBM)


def _landing(block_shape, dtype, axis=0):
    if axis == 0:
        return lax.empty((N_DEV,) + tuple(block_shape), dtype)
    rows, n = block_shape
    return lax.empty((rows, N_DEV * n), dtype)


def _slot(ref, i):
    if len(ref.shape) == 2:
        n = ref.shape[1] // N_DEV
        return ref.at[:, pl.ds(i * n, n)]
    return ref.at[i]


def _push_start(name, srcs, lands, whole, groups):
    n = len(srcs)
    ng = 1 + max(groups)
    cnt = [groups.count(g) for g in range(ng)]
    idx = [groups[:a].count(groups[a]) for a in range(n)]

    def body(*refs):
        src_refs, land_refs = refs[:n], refs[n:2 * n]
        sems = refs[2 * n:2 * n + 3 * ng]
        token = refs[-1]
        x, y, c = _my_pos()
        me = 4 * x + 2 * y + c
        for a in range(n):
            g = groups[a]
            for m in range(1, N_DEV):
                peer, pid = _peer(m)
                pltpu.make_async_remote_copy(
                    src_ref=src_refs[a] if whole else src_refs[a].at[pid], dst_ref=_slot(land_refs[a], me),
                    send_sem=sems[3 * g].at[idx[a] * (N_DEV - 1) + m - 1],
                    recv_sem=sems[3 * g + 1].at[idx[a] * (N_DEV - 1) + m - 1],
                    device_id=peer, device_id_type=MESH).start()
            pltpu.make_async_copy(src_refs[a] if whole else src_refs[a].at[me], _slot(land_refs[a], me),
                                  sems[3 * g + 2].at[idx[a]]).start()
        token[...] = jnp.zeros_like(token)

    sem_shapes = []
    for g in range(ng):
        sem_shapes += [pltpu.SemaphoreType.DMA((cnt[g] * (N_DEV - 1),))] * 2 + [pltpu.SemaphoreType.DMA((cnt[g],))]
    thru = [pltpu.HBM(s.shape, s.dtype) for s in list(srcs) + list(lands)]
    res = pl.pallas_call(
        body, name=name,
        out_shape=tuple(sem_shapes + thru + [jax.ShapeDtypeStruct((8, 128), F32)]),
        in_specs=tuple([HBM_SPEC] * (2 * n)),
        out_specs=tuple([SEM_SPEC] * (3 * ng) + [HBM_SPEC] * (2 * n) + [pl.BlockSpec(memory_space=pltpu.VMEM)]),
        input_output_aliases={i: 3 * ng + i for i in range(2 * n)},
        compiler_params=pltpu.CompilerParams(has_side_effects=EFFECT),
    )(*[_hbm(s) for s in srcs], *[_hbm(z) for z in lands])
    sems = [(res[3 * g], res[3 * g + 1], res[3 * g + 2]) for g in range(ng)]
    srcs_thru = list(res[3 * ng:3 * ng + n])
    lands_thru = list(res[3 * ng + n:3 * ng + 2 * n])
    return sems, srcs_thru, lands_thru, res[-1]


def _push_wait(name, srcs_thru, lands_thru, sems, after, whole):
    n = len(srcs_thru)

    def body(*refs):
        src_refs, land_refs = refs[:n], refs[n:2 * n]
        send_sems, recv_sems, own_sems = refs[2 * n], refs[2 * n + 1], refs[2 * n + 2]
        x, y, c = _my_pos()
        me = 4 * x + 2 * y + c
        for a in range(n):
            pltpu.make_async_copy(src_refs[a] if whole else src_refs[a].at[me], _slot(land_refs[a], me),
                                  own_sems.at[a]).wait()
            for m in range(1, N_DEV):
                peer, pid = _peer(m)
                cp = pltpu.make_async_remote_copy(
                    src_ref=src_refs[a] if whole else src_refs[a].at[pid], dst_ref=_slot(land_refs[a], pid),
                    send_sem=send_sems.at[a * (N_DEV - 1) + m - 1], recv_sem=recv_sems.at[a * (N_DEV - 1) + m - 1],
                    device_id=peer, device_id_type=MESH)
                cp.wait_send()
                cp.wait_recv()

    thru = [pltpu.HBM(s.shape, s.dtype) for s in list(srcs_thru) + list(lands_thru)]
    res = pl.pallas_call(
        body, name=name, out_shape=tuple(thru),
        in_specs=tuple([HBM_SPEC] * (2 * n) + [SEM_SPEC, SEM_SPEC, SEM_SPEC, ANY]),
        out_specs=tuple([HBM_SPEC] * (2 * n)),
        input_output_aliases={i: i for i in range(2 * n)},
        compiler_params=pltpu.CompilerParams(has_side_effects=EFFECT),
    )(*srcs_thru, *lands_thru, sems[0], sems[1], sems[2], after)
    return list(res[n:])


def _adamw(recv, w, m, v, layer=0, prev=None):
    L, R, C = w.shape
    tr = _tile(R, max(8, (1 << 19) // C), 8) if R % 8 == 0 else R
    bc1 = 1.0 - ADAM_B1 ** ADAM_STEP
    bc2 = 1.0 - ADAM_B2 ** ADAM_STEP
    if prev is None:
        prev = [lax.empty((L, R, C), F32) for _ in range(4)]

    def body(r_ref, w_ref, m_ref, v_ref, p0, p1, p2, p3, g_ref, d_ref, nm_ref, nv_ref):
        g = r_ref[0].astype(F32)
        for s in range(1, N_DEV):
            g = g + r_ref[s].astype(F32)
        nm = ADAM_B1 * m_ref[...] + (1.0 - ADAM_B1) * g
        nv = ADAM_B2 * v_ref[...] + (1.0 - ADAM_B2) * (g * g)
        mh = nm / bc1
        vh = nv / bc2
        g_ref[...] = g
        d_ref[...] = -ADAM_LR * (mh / (jnp.sqrt(vh) + ADAM_EPS) + ADAM_WD * w_ref[...])
        nm_ref[...] = nm
        nv_ref[...] = nv

    row = pl.BlockSpec((None, tr, C), lambda i: (layer, i, 0))
    return pl.pallas_call(
        body, grid=(R // tr,),
        in_specs=[pl.BlockSpec((N_DEV, tr, C), lambda i: (0, i, 0)), row, row, row] + [ANY] * 4,
        out_specs=[row] * 4, out_shape=[jax.ShapeDtypeStruct((L, R, C), F32)] * 4,
        input_output_aliases={4: 0, 5: 1, 6: 2, 7: 3},
        compiler_params=_cp("parallel"), name="adamw")(recv, w, m, v, *prev)


def _full_w_spec(tk, tn):
    return pl.BlockSpec((tk, tn), lambda i, j, k: (k, j))


def kernel(x, meta_tokens, mix_norm_g, mlp_norm_g, final_norm_g, ev_w_in, ev_conv_w, ev_conv_b, ev_ln_g, ev_ln_b, ev_pool_w, ev_pool_b, ev_pool_scale, ev_w_out, od_w_in, od_gnorm_g, od_w_out, lb_param, mlp_w1, mlp_w2, loss_target, m_meta_tokens, m_mix_norm_g, m_mlp_norm_g, m_final_norm_g, m_ev_w_in, m_ev_conv_w, m_ev_conv_b, m_ev_ln_g, m_ev_ln_b, m_ev_pool_w, m_ev_pool_b, m_ev_pool_scale, m_ev_w_out, m_od_w_in, m_od_gnorm_g, m_od_w_out, m_lb_param, m_mlp_w1, m_mlp_w2, v_meta_tokens, v_mix_norm_g, v_mlp_norm_g, v_final_norm_g, v_ev_w_in, v_ev_conv_w, v_ev_conv_b, v_ev_ln_g, v_ev_ln_b, v_ev_pool_w, v_ev_pool_b, v_ev_pool_scale, v_ev_w_out, v_od_w_in, v_od_gnorm_g, v_od_w_out, v_lb_param, v_mlp_w1, v_mlp_w2):
    S, D = x.shape[1], x.shape[2]
    T = PAD + N_META + S
    DEPTH = mix_norm_g.shape[0]
    DFF = mlp_w1.shape[2] * N_DEV
    dev = 4 * lax.axis_index("x") + 2 * lax.axis_index("y") + lax.axis_index("c")

    n_ev = ev_w_in.shape[0]
    n_od = od_w_in.shape[0]
    n_in_od = od_w_in.shape[2]
    n_w1 = mlp_w1.shape[2]

    ag_src, ag_grp, ag_axis, ag_at = [], [], [], {}
    for key, arr in (("meta", meta_tokens), ("cw", ev_conv_w)):
        ag_at["small", key] = len(ag_src)
        ag_src.append(arr)
        ag_grp.append(len(ag_grp))
        ag_axis.append(0)
    for layer in range(DEPTH):
        j = layer // 2
        mixer = [("in", ev_w_in[j]), ("out", ev_w_out[j])] if layer % 2 == 0 else [("in", od_w_in[j]), ("out", od_w_out[j])]
        for pos, (key, arr) in enumerate(mixer + [("w1", mlp_w1[layer]), ("w2", mlp_w2[layer])]):
            ag_at[layer, key] = len(ag_src)
            ag_src.append(arr.astype(BF16))
            ag_grp.append(len(ag_grp))
            ag_axis.append(1 if key in ("in", "w1") and arr.shape[1] % 128 == 0 else 0)
    ag_sems, ag_s, ag_l, ag_tok = _push_start(
        "ag_start", ag_src, [_landing(s_.shape, s_.dtype, ax) for s_, ax in zip(ag_src, ag_axis)], True, ag_grp)

    def ag_wait(layer, key, after):
        a = ag_at[layer, key]
        return _push_wait(f"ag_wait_{a}", [ag_s[a]], [ag_l[a]], ag_sems[a], after, True)[0]

    g_meta = ag_wait("small", "meta", ag_tok)
    g_cw = ag_wait("small", "cw", ag_tok)
    meta_full = jnp.transpose(g_meta, (1, 0, 2)).reshape(N_META, D)
    cw_full = jnp.transpose(g_cw, (1, 2, 0, 3)).reshape(n_ev, CONV_WIDTH, -1)
    cw_pad = jnp.pad(cw_full, ((0, 0), (0, 32 - CONV_WIDTH), (0, 0)))

    h = jnp.concatenate([jnp.zeros((PAD, D), F32), meta_full, x[0]], axis=0) + ag_tok[0, 0]
    tgt = jnp.pad(loss_target[0], ((PAD + N_META, 0), (0, 0)))
    lb_all = _lb_fwd(lb_param)
    h, tgt, lb_all = lax.optimization_barrier((h, tgt, lb_all))

    tm_big = _tile(T, MM_ROWS_BIG, 16)
    tm_mid = _tile(T, MM_ROWS_MID, 16)
    tm_k4 = _tile(T, MM_ROWS_K4, 16)

    saved = []
    for layer in range(DEPTH):
        j = layer // 2
        sv = {"h0": h}
        g_in = ag_wait(layer, "in", h)
        w_in = g_in if g_in.ndim == 2 else jnp.transpose(g_in, (1, 0, 2)).reshape(D, -1)
        if layer % 2 == 0:
            sv["n"], u = _mm_rms_nn("ev_in", h, mix_norm_g[layer][None], w_in, tm_big, 512, "f32")
            yab, yc = _ev_fwd(u, cw_pad[j], ev_conv_b[j][None], ev_ln_g[j][None], ev_ln_b[j][None],
                              ev_pool_w[j].astype(BF16), ev_pool_b[j].reshape(1, -1), ev_pool_scale[j][None])
            sv.update(u=u, y=yab, yc=yc)
            w_out = ag_wait(layer, "out", yab).reshape(-1, D)
            h = _mm_nn("ev_out", yab, w_out, _full_w_spec, T, D, D, tm_mid, D, D, "resid", extra=h)
        else:
            sv["n"], u = _mm_rms_nn("od_in", h, mix_norm_g[layer][None], w_in, tm_big, 512, "f32")
            y, o, s0 = _hgrn_fwd(u, lb_all[layer][None], od_gnorm_g[j][None])
            sv.update(u=u, y=y, o=o, s0=s0)
            w_out = ag_wait(layer, "out", y).reshape(-1, D)
            h = _mm_nn("od_out", y, w_out, _full_w_spec, T, D, D, tm_mid, D, D, "resid", extra=h)
        sv["h1"] = h
        w_w1 = ag_wait(layer, "w1", h)
        n2, r, act = _mm_rms_nn("mlp_w1", h, mlp_norm_g[layer][None], w_w1, tm_big, 1024, "relu2")
        w_w2 = ag_wait(layer, "w2", act).reshape(DFF, D)
        sv.update(w_in=w_in, w_out=w_out, w_w1=w_w1, w_w2=w_w2)
        sv.update(n2=n2, r=r, act=act)
        h = _mm_nn("mlp_w2", act, w_w2, _full_w_spec, T, D, DFF, tm_k4, D, DFF, "resid", extra=h)
        saved.append(sv)

    loss_blk, dh, dhb, dg_final = _loss_head(h, final_norm_g[None], tgt)
    loss = lax.psum(loss_blk[0, 0], AXES)

    tt = T
    g_mix, g_mlp = [None] * DEPTH, [None] * DEPTH
    small ={"cw": [None] * n_ev, "vec": [None] * n_ev, "pw": [None] * n_ev, "gn": [None] * n_od}
    dlb_rows = [jnp.zeros((1, D), F32) for _ in range(DEPTH)]

    def xs2(tt_, tk):
        return pl.BlockSpec((tt_, tk), lambda a, b, t: (t, a))

    def ys2(tt_, tn):
        return pl.BlockSpec((tt_, tn), lambda a, b, t: (t, b))

    def os2(tk, tn):
        return pl.BlockSpec((tk, tn), lambda a, b, t: (a, b))

    def os3(tk, tn):
        return pl.BlockSpec((None, tk, tn), lambda a, b, t: (b, a, 0))

    def dy2(tm, tn):
        return pl.BlockSpec((tm, tn), lambda i, jj, k: (i, k))

    def w_rows(tj, tn):
        return pl.BlockSpec((tj, tn), lambda i, jj, k: (jj, k))

    def w_whole(tj, tn):
        return pl.BlockSpec((tj, tn), lambda i, jj, k: (0, 0), pipeline_mode=pl.Buffered(1))

    rs_pending = []

    def rs_start(tag, mats):
        blocks = [m_ if m_.ndim == 3 else m_.reshape(N_DEV, m_.shape[0] // N_DEV, m_.shape[1]) for m_ in mats]
        lands = [_landing(b_.shape[1:], b_.dtype) for b_ in blocks]
        sems, s_thru, l_thru, tok = _push_start(f"rs_start_{tag}", blocks, lands, False, [0] * len(blocks))
        rs_pending.append((tag, s_thru, l_thru, sems[0]))
        return tok[0, 0]

    for layer in reversed(range(DEPTH)):
        j = layer // 2
        sv = saved[layer]
        da1 = _mm_nt("mlp_w2_t", dhb, sv["w_w2"], dy2, w_rows, T, DFF, D, tm_mid, 1024, D, "dact", extra=sv["r"])
        dw2 = _mm_tn("mlp_dw2", sv["act"], dhb, xs2, ys2, os2, (DFF, D), T, DFF, D, tt, 512, D)
        dw1 = _mm_tn("mlp_dw1", sv["n2"], da1, xs2, ys2, os3, (N_DEV, D, n_w1), T, D, DFF, tt, D, n_w1)
        tok = rs_start(f"mlp{layer}", [dw1, dw2])
        dh, dhb, g_mlp[layer] = _mm_nt("mlp_w1_t", da1, sv["w_w1"], dy2, w_whole, T, D, DFF, tm_k4, D, DFF, "rms",
                                       extra=(sv["h1"], mlp_norm_g[layer][None] + tok, dh))
        if layer % 2 == 0:
            dyab = _mm_nt("ev_out_t", dhb, sv["w_out"], dy2, w_rows, T, D, D, tm_mid, D, D, "f32")
            dwout = _mm_tn("ev_dwout", sv["y"], dhb, xs2, ys2, os2, (D, D), T, D, D, tt, 512, D)
            du, small["cw"][j], small["vec"][j], small["pw"][j] = _ev_bwd(
                dyab, sv["yc"], sv["u"], cw_pad[j], ev_ln_g[j][None], ev_ln_b[j][None], ev_pool_w[j].astype(BF16),
                jnp.transpose(ev_pool_w[j], (0, 2, 1)).astype(BF16), ev_pool_b[j].reshape(1, -1),
                ev_pool_scale[j][None])
            nin = du.shape[1]
            dwin = _mm_tn("ev_dwin", sv["n"], du, xs2, ys2, os2, (D, nin), T, D, nin, tt, D, 512)
            dwin = jnp.transpose(dwin.reshape(D, N_DEV, nin // N_DEV), (1, 0, 2))
            tok = rs_start(f"mix{layer}", [dwin, dwout])
            dh, dhb, g_mix[layer] = _mm_nt("ev_in_t", du, sv["w_in"], dy2, w_whole, T, D, nin, tm_k4, D, nin, "rms",
                                           extra=(sv["h0"], mix_norm_g[layer][None] + tok, dh))
        else:
            dy = _mm_nt("od_out_t", dhb, sv["w_out"], dy2, w_rows, T, D, D, tm_mid, D, D, "f32")
            dwout = _mm_tn("od_dwout", sv["y"], dhb, xs2, ys2, os2, (D, D), T, D, D, tt, 512, D)
            du3, dlb_rows[layer], small["gn"][j] = _hgrn_bwd(dy, sv["o"], sv["s0"], sv["u"], lb_all[layer][None],
                                                              od_gnorm_g[j][None])
            per = D // n_in_od

            def du_t(tt_, tn):
                return pl.BlockSpec((None, tt_, tn), lambda a, b, t: (b // per, t, b % per))

            dwin = _mm_tn("od_dwin", sv["n"], du3, xs2, du_t, os3, (N_DEV, D, n_in_od), T, D, 4 * D, tt, D, n_in_od)
            tok = rs_start(f"mix{layer}", [dwin, dwout])
            dh, dhb, g_mix[layer] = _mm_nt(
                "od_in_t", du3, sv["w_in"], lambda tm, tn: pl.BlockSpec((4, tm, tn // 4), lambda i, jj, k: (0, i, 0)),
                w_whole, T, D, 4 * D, tm_k4, D, 4 * D, "rms", extra=(sv["h0"], mix_norm_g[layer][None] + tok, dh),
                parts=4)

    dmeta = dh[PAD:PAD + N_META]
    grad_x = dh[PAD + N_META:][None]
    dlb_param = _lb_bwd(lb_param, jnp.concatenate(dlb_rows, axis=0))

    pieces = [
        ("final", dg_final), ("pad", jnp.zeros((SMALL_F32_ROWS - 1, D), F32)),
        ("meta", dmeta), ("mix", jnp.concatenate(g_mix, 0)), ("mlp", jnp.concatenate(g_mlp, 0)),
        ("cw", jnp.stack([c[:CONV_WIDTH] for c in small["cw"]])), ("cb", jnp.stack([v_[0] for v_ in small["vec"]])),
        ("lng", jnp.stack([v_[1] for v_ in small["vec"]])), ("lnb", jnp.stack([v_[2] for v_ in small["vec"]])),
        ("pw", jnp.stack(small["pw"])), ("pb", jnp.stack([v_[4] for v_ in small["vec"]])),
        ("ps", jnp.stack([v_[3] for v_ in small["vec"]])), ("gn", jnp.stack([jnp.sum(g_, axis=0)[0] for g_ in small["gn"]])),
        ("lb", dlb_param),
    ]
    flat = jnp.concatenate([p.reshape(-1) for _, p in pieces])
    n_small = flat.shape[0]
    rows_small = SMALL_F32_ROWS + -(-(n_small // 1024 + 1 - SMALL_F32_ROWS) // 16) * 16
    flat = jnp.pad(flat, (0, rows_small * 1024 - n_small)).reshape(rows_small, 1024)

    sm_src = [flat[:SMALL_F32_ROWS], flat[SMALL_F32_ROWS:].astype(BF16)]
    sm_sems, sm_s, sm_l, sm_tok = _push_start("small_start", sm_src, [_landing(a_.shape, a_.dtype) for a_ in sm_src],
                                              True, [0, 0])
    recv = {}
    for tag, s_thru, l_thru, sems in rs_pending:
        got = _push_wait(f"rs_wait_{tag}", s_thru, l_thru, sems, sm_tok, False)
        layer = int(tag[3:])
        if tag.startswith("mlp"):
            recv["w1", layer], recv["w2", layer] = got
        else:
            key = "ev" if layer % 2 == 0 else "od"
            recv[key + "_in", layer // 2], recv[key + "_out", layer // 2] = got

    outs = {}
    big = {"ev_in": ("ev_w_in", ev_w_in, m_ev_w_in, v_ev_w_in), "ev_out": ("ev_w_out", ev_w_out, m_ev_w_out, v_ev_w_out),
           "od_in": ("od_w_in", od_w_in, m_od_w_in, v_od_w_in), "od_out": ("od_w_out", od_w_out, m_od_w_out, v_od_w_out),
           "w1": ("mlp_w1", mlp_w1, m_mlp_w1, v_mlp_w1), "w2": ("mlp_w2", mlp_w2, m_mlp_w2, v_mlp_w2)}
    for key, (name, w, m, v) in big.items():
        res = None
        for l in range(w.shape[0]):
            res = _adamw(recv[key, l], w, m, v, layer=l, prev=res)
        outs[name] = res

    small_params = {
        "meta": ("meta_tokens", None), "mix": ("mix_norm_g", mix_norm_g, m_mix_norm_g, v_mix_norm_g),
        "mlp": ("mlp_norm_g", mlp_norm_g, m_mlp_norm_g, v_mlp_norm_g),
        "final": ("final_norm_g", final_norm_g, m_final_norm_g, v_final_norm_g),
        "cw": ("ev_conv_w", None), "cb": ("ev_conv_b", ev_conv_b, m_ev_conv_b, v_ev_conv_b),
        "lng": ("ev_ln_g", ev_ln_g, m_ev_ln_g, v_ev_ln_g), "lnb": ("ev_ln_b", ev_ln_b, m_ev_ln_b, v_ev_ln_b),
        "pw": ("ev_pool_w", ev_pool_w, m_ev_pool_w, v_ev_pool_w), "pb": ("ev_pool_b", ev_pool_b, m_ev_pool_b, v_ev_pool_b),
        "ps": ("ev_pool_scale", ev_pool_scale, m_ev_pool_scale, v_ev_pool_scale),
        "gn": ("od_gnorm_g", od_gnorm_g, m_od_gnorm_g, v_od_gnorm_g), "lb": ("lb_param", lb_param, m_lb_param, v_lb_param),
    }
    csh = ev_conv_w.shape[2]
    msh = meta_tokens.shape[1]

    def packed(which):
        parts = []
        for key, g_ in pieces:
            ent = small_params.get(key)
            if key == "pad":
                full = g_
            elif key == "meta":
                src = (meta_tokens, m_meta_tokens, v_meta_tokens)[which]
                full = lax.dynamic_update_slice(jnp.zeros((N_META, D), F32), src, (0, dev * msh))
            elif key == "cw":
                src = (ev_conv_w, m_ev_conv_w, v_ev_conv_w)[which]
                full = lax.dynamic_update_slice(jnp.zeros(g_.shape, F32), src, (0, 0, dev * csh))
            else:
                full = ent[1 + which]
            parts.append(full.reshape(-1))
        f = jnp.concatenate(parts)
        return jnp.pad(f, (0, rows_small * 1024 - n_small)).reshape(rows_small, 1024)

    got_f32, got_bf16 = _push_wait("small_wait", sm_s, sm_l, sm_sems[0], outs["mlp_w2"][0], True)
    recv_small = jnp.concatenate([got_f32, got_bf16.astype(F32)], axis=1)
    sres = [r_[0] for r_ in _adamw(recv_small, packed(0)[None], packed(1)[None], packed(2)[None])]
    off = 0
    for key, g_ in pieces:
        size = g_.size
        vals = [r_.reshape(-1)[off:off + size].reshape(g_.shape) for r_ in sres]
        off += size
        if key == "pad":
            continue
        name = small_params[key][0]
        if key == "meta":
            vals = [lax.dynamic_slice(v_, (0, dev * msh), (N_META, msh)) for v_ in vals]
        elif key == "cw":
            vals = [lax.dynamic_slice(v_, (0, 0, dev * csh), v_.shape[:2] + (csh,)) for v_ in vals]
        else:
            vals = [v_.reshape(small_params[key][1].shape) for v_ in vals]
        outs[name] = vals

    names = ["meta_tokens", "mix_norm_g", "mlp_norm_g", "final_norm_g", "ev_w_in", "ev_conv_w", "ev_conv_b", "ev_ln_g",
             "ev_ln_b", "ev_pool_w", "ev_pool_b", "ev_pool_scale", "ev_w_out", "od_w_in", "od_gnorm_g", "od_w_out",
             "lb_param", "mlp_w1", "mlp_w2"]
    result = [loss, grad_x]
    for k in range(4):
        result += [outs[nm][k] for nm in names]
    return tuple(result)
```

```python
import functools

import jax
import jax.numpy as jnp
from jax import lax
from jax.experimental import pallas as pl
from jax.experimental.pallas import tpu as pltpu

F32 = jnp.float32
BF16 = jnp.bfloat16

N_DEV = 8
N_META = 16
CHUNK = 64
PAD = CHUNK - N_META
SUB = 16
HEAD = 128
CONV_WIDTH = 31
HALO = 32
POOL_WINDOWS = (2, 4, 8, 16)
EPS = 1e-6
NEG = -1e30
ADAM_LR, ADAM_B1, ADAM_B2, ADAM_EPS, ADAM_WD, ADAM_STEP = 0.001, 0.9, 0.999, 1e-08, 0.01, 10
VMEM_LIMIT = 56 * 1024 * 1024
EV_ROWS = 416
HGRN_ROWS = 832
MM_ROWS_BIG = 2080
MM_ROWS_MID = 1040
MM_ROWS_K4 = 416
SMALL_F32_ROWS = 8
HGRN_HEADS_FWD = 8
HGRN_HEADS_BWD = 2
MESH = pl.DeviceIdType.MESH
AXES = ("x", "y", "c")
ANY = pl.BlockSpec(memory_space=pl.ANY)


def _cp(*sem):
    return pltpu.CompilerParams(dimension_semantics=sem, vmem_limit_bytes=VMEM_LIMIT)


def _tile(n, cap, mult):
    best = None
    for d in range(mult, min(n, cap) + 1, mult):
        if n % d == 0:
            best = d
    assert best is not None, (n, cap, mult)
    return best


def _nt(a, b):
    return lax.dot_general(a, b, (((1,), (1,)), ((), ())), preferred_element_type=F32)


def _tn(a, b):
    return lax.dot_general(a, b, (((0,), (0,)), ((), ())), preferred_element_type=F32)


def _nn(a, b):
    return jnp.dot(a, b, preferred_element_type=F32)


def _r16(x):
    return x.astype(BF16).astype(F32)


def _row_ids(base, n):
    return base + lax.broadcasted_iota(jnp.int32, (n, 1), 0)


def _dsilu(x, s):
    return s * (1.0 + x * (1.0 - s))


def _loss_head(h, g, tgt):
    T, D = h.shape
    tm = _tile(T, MM_ROWS_MID, 16)
    first_x = PAD + N_META

    def body(h_ref, g_ref, t_ref, loss_ref, dh_ref, dhb_ref, dg_ref):
        i = pl.program_id(0)
        x = h_ref[...]
        r = lax.rsqrt(jnp.mean(x * x, axis=-1, keepdims=True) + EPS)
        xh = x * r
        gv = g_ref[...]
        out = xh * gv
        valid = _row_ids(i * tm, tm) >= first_x
        e = jnp.where(valid, out - t_ref[...], 0.0)
        dout = e * (1.0 / D)
        dxh = dout * gv
        dx = r * (dxh - xh * jnp.mean(dxh * xh, axis=-1, keepdims=True))
        dh_ref[...] = dx
        dhb_ref[...] = dx.astype(BF16)

        @pl.when(i == 0)
        def _():
            dg_ref[...] = jnp.zeros_like(dg_ref)
            loss_ref[...] = jnp.zeros_like(loss_ref)

        dg_ref[...] += jnp.sum(dout * xh, axis=0, keepdims=True)
        loss_ref[...] += 0.5 * jnp.sum(jnp.mean(e * e, axis=-1, keepdims=True))

    row = pl.BlockSpec((tm, D), lambda i: (i, 0))
    vec = pl.BlockSpec((1, D), lambda i: (0, 0))
    return pl.pallas_call(
        body, grid=(T // tm,),
        in_specs=[row, vec, row],
        out_specs=[pl.BlockSpec((8, 128), lambda i: (0, 0)), row, row, vec],
        out_shape=[jax.ShapeDtypeStruct((8, 128), F32), jax.ShapeDtypeStruct((T, D), F32),
                   jax.ShapeDtypeStruct((T, D), BF16), jax.ShapeDtypeStruct((1, D), F32)],
        compiler_params=_cp("arbitrary"), name="loss_head")(h, g, tgt)


def _mm_nn(name, a, w, w_spec, M, N, K, tm, tn, tk, mode, extra=None, a_spec=None):
    nk = K // tk
    if a_spec is None:
        a_spec = pl.BlockSpec((tm, tk), lambda i, j, k: (i, k))
    o_spec = pl.BlockSpec((tm, tn), lambda i, j, k: (i, j))

    def body(*refs):
        if mode == "resid":
            a_ref, w_ref, e_ref = refs[:3]
            outs = refs[3:]
        else:
            a_ref, w_ref = refs[:2]
            outs = refs[2:]
        acc_ref = outs[-1] if nk > 1 else None
        part = _nn(a_ref[...], w_ref[...])

        def finish(acc):
            if mode == "f32":
                outs[0][...] = acc
            elif mode == "relu2":
                r = jnp.maximum(acc, 0.0)
                outs[0][...] = r.astype(BF16)
                outs[1][...] = (r * r).astype(BF16)
            else:
                keep = _row_ids(pl.program_id(0) * tm, tm) >= PAD
                outs[0][...] = jnp.where(keep, e_ref[...] + acc, 0.0)

        if nk == 1:
            finish(part)
        else:
            k = pl.program_id(2)

            @pl.when(k == 0)
            def _():
                acc_ref[...] = part

            @pl.when(k > 0)
            def _():
                acc_ref[...] += part

            @pl.when(k == nk - 1)
            def _():
                finish(acc_ref[...])

    in_specs = [a_spec, w_spec(tk, tn)]
    args = [a, w]
    if mode == "resid":
        in_specs.append(o_spec)
        args.append(extra)
    if mode == "relu2":
        out_specs = [o_spec, o_spec]
        out_shape = [jax.ShapeDtypeStruct((M, N), BF16)] * 2
    else:
        out_specs = [o_spec]
        out_shape = [jax.ShapeDtypeStruct((M, N), F32)]
    scratch = [pltpu.VMEM((tm, tn), F32)] if nk > 1 else []
    res = pl.pallas_call(
        body, grid=(M // tm, N // tn, nk), in_specs=in_specs, out_specs=out_specs, out_shape=out_shape,
        scratch_shapes=scratch, compiler_params=_cp("parallel", "parallel", "arbitrary"), name=name)(*args)
    return res if mode == "relu2" else res[0]


def _mm_rms_nn(name, h, g, w, tm, tn, mode):
    M, K = h.shape
    N = w.shape[1]

    def body(h_ref, g_ref, w_ref, n_ref, *outs):
        @pl.when(pl.program_id(1) == 0)
        def _():
            ch = _tile(tm, 256, 16)

            def chunk(c, carry):
                rows = pl.ds(pl.multiple_of(c * ch, ch), ch)
                x = h_ref[rows, :]
                r = lax.rsqrt(jnp.mean(x * x, axis=-1, keepdims=True) + EPS)
                n_ref[rows, :] = ((x * r) * g_ref[...]).astype(BF16)
                return carry

            lax.fori_loop(0, tm // ch, chunk, 0)

        acc = _nn(n_ref[...], w_ref[...])
        if mode == "f32":
            outs[0][...] = acc
        else:
            r = jnp.maximum(acc, 0.0)
            outs[0][...] = r.astype(BF16)
            outs[1][...] = (r * r).astype(BF16)

    row = pl.BlockSpec((tm, K), lambda i, j: (i, 0))
    o_spec = pl.BlockSpec((tm, tn), lambda i, j: (i, j))
    n_out = 1 if mode == "f32" else 2
    return pl.pallas_call(
        body, grid=(M // tm, N // tn),
        in_specs=[row, pl.BlockSpec((1, K), lambda i, j: (0, 0)), pl.BlockSpec((K, tn), lambda i, j: (0, j))],
        out_specs=[row] + [o_spec] * n_out,
        out_shape=[jax.ShapeDtypeStruct((M, K), BF16)] + [jax.ShapeDtypeStruct((M, N), F32 if mode == "f32" else BF16)] * n_out,
        compiler_params=_cp("parallel", "arbitrary"), name=name)(h, g, w)


def _mm_nt(name, dy, w, dy_spec, w_spec, M, J, N, tm, tj, tn, mode, extra=None, parts=1):
    nk = N // tn
    o_spec = pl.BlockSpec((tm, tj), lambda i, j, k: (i, j))
    n_extra = {"f32": 0, "dact": 1, "rms": 3}[mode]
    if mode == "rms":
        assert nk == 1 and tj == J

    def body(*refs):
        dy_ref, w_ref = refs[:2]
        ex = refs[2:2 + n_extra]
        outs = refs[2 + n_extra:]
        acc_ref = outs[-1] if nk > 1 else None
        if parts == 1:
            part = _nt(dy_ref[...], w_ref[...])
        else:
            wq = tn // parts
            part = _nt(dy_ref[0], w_ref[:, 0:wq])
            for q in range(1, parts):
                part = part + _nt(dy_ref[q], w_ref[:, q * wq:(q + 1) * wq])

        def finish(acc):
            if mode == "f32":
                outs[0][...] = acc
            elif mode == "dact":
                outs[0][...] = (acc * (2.0 * ex[0][...].astype(F32))).astype(BF16)
            else:
                h_ref, g_ref, dres_ref = ex
                dh_ref, dhb_ref, dg_ref = outs[:3]
                i = pl.program_id(0)

                @pl.when(i == 0)
                def _():
                    dg_ref[...] = jnp.zeros_like(dg_ref)

                ch = _tile(tm, 256, 16)
                for c0 in range(0, tm, ch):
                    a_c = acc[c0:c0 + ch]
                    x = h_ref[c0:c0 + ch, :]
                    r = lax.rsqrt(jnp.mean(x * x, axis=-1, keepdims=True) + EPS)
                    xh = x * r
                    dxh = a_c * g_ref[...]
                    dx = r * (dxh - xh * jnp.mean(dxh * xh, axis=-1, keepdims=True))
                    keep = _row_ids(i * tm + c0, ch) >= PAD
                    dh = jnp.where(keep, dres_ref[c0:c0 + ch, :] + dx, 0.0)
                    dh_ref[c0:c0 + ch, :] = dh
                    dhb_ref[c0:c0 + ch, :] = dh.astype(BF16)
                    dg_ref[...] += jnp.sum(a_c * xh, axis=0, keepdims=True)

        if nk == 1:
            finish(part)
        else:
            k = pl.program_id(2)

            @pl.when(k == 0)
            def _():
                acc_ref[...] = part

            @pl.when(k > 0)
            def _():
                acc_ref[...] += part

            @pl.when(k == nk - 1)
            def _():
                finish(acc_ref[...])

    in_specs = [dy_spec(tm, tn), w_spec(tj, tn)]
    args = [dy, w]
    scratch = [pltpu.VMEM((tm, tj), F32)] if nk > 1 else []
    if mode == "rms":
        vec = pl.BlockSpec((1, J), lambda i, j, k: (0, 0))
        h, g, dres = extra
        res = pl.pallas_call(
            body, grid=(M // tm, 1, 1), in_specs=in_specs + [o_spec, vec, o_spec], out_specs=[o_spec, o_spec, vec],
            out_shape=[jax.ShapeDtypeStruct((M, J), F32), jax.ShapeDtypeStruct((M, J), BF16),
                       jax.ShapeDtypeStruct((1, J), F32)],
            compiler_params=_cp("arbitrary", "arbitrary", "arbitrary"), name=name)(*args, h, g, dres)
        return res
    if mode == "dact":
        in_specs.append(o_spec)
        args.append(extra)
    return pl.pallas_call(
        body, grid=(M // tm, J // tj, nk), in_specs=in_specs, out_specs=[o_spec],
        out_shape=[jax.ShapeDtypeStruct((M, J), BF16 if mode == "dact" else F32)],
        scratch_shapes=scratch, compiler_params=_cp("parallel", "parallel", "arbitrary"), name=name)(*args)[0]


def _mm_tn(name, x, dy, x_spec, dy_spec, o_spec, o_shape, T, K, N, tt, tk, tn):
    nt = T // tt

    def body(x_ref, dy_ref, o_ref, *acc):
        part = _tn(x_ref[...], dy_ref[...])
        if nt == 1:
            o_ref[...] = part.astype(BF16)
            return
        acc_ref = acc[0]
        t = pl.program_id(2)

        @pl.when(t == 0)
        def _():
            acc_ref[...] = part

        @pl.when(t > 0)
        def _():
            acc_ref[...] += part

        @pl.when(t == nt - 1)
        def _():
            o_ref[...] = acc_ref[...].astype(BF16)

    return pl.pallas_call(
        body, grid=(K // tk, N // tn, nt), in_specs=[x_spec(tt, tk), dy_spec(tt, tn)], out_specs=o_spec(tk, tn),
        out_shape=jax.ShapeDtypeStruct(o_shape, BF16), scratch_shapes=[pltpu.VMEM((tk, tn), F32)] if nt > 1 else [],
        compiler_params=_cp("parallel", "parallel", "arbitrary"), name=name)(x, dy)


def _pool_counts(base, n, w):
    pos = _row_ids(base, n) - PAD
    return jnp.clip(pos + 1, 1, w).astype(F32)


def _shifted_copies(buf, rows):
    buf[0, rows:rows + 8, :] = jnp.zeros((8, buf.shape[2]), F32)

    def blk(s, carry):
        b = pl.multiple_of(s * HALO, HALO)
        win = buf[0, pl.ds(b, HALO + 8), :]
        for r in range(1, 8):
            buf[r, pl.ds(b, HALO), :] = win[r:r + HALO]
        return carry

    lax.fori_loop(0, rows // HALO, blk, 0)


def _ev_fwd(u, cw, cb, lg, lb, pw, pb, ps):
    T = u.shape[0]
    C = 512
    tm = _tile(T, EV_ROWS, HALO)
    nsub = tm // HALO
    hb = tm // HALO

    def body(val_ref, gate_ref, pin_ref, valh_ref, gateh_ref, pinh_ref, cw_ref, cb_ref, lg_ref, lb_ref, pw_ref,
             pb_ref, ps_ref, yab_ref, yc_ref, a_ext, p_ext, d_buf):
        i = pl.program_id(0)
        nf = (i > 0).astype(F32)
        a_ext[0, 0:HALO, :] = valh_ref[...] * jax.nn.sigmoid(gateh_ref[...]) * nf
        a_ext[0, HALO:HALO + tm, :] = val_ref[...] * jax.nn.sigmoid(gate_ref[...])
        p_ext[0:HALO, :] = pinh_ref[...] * nf
        p_ext[HALO:, :] = pin_ref[...]
        _shifted_copies(a_ext, tm + HALO)

        def sub(s, carry):
            base = pl.multiple_of(s * HALO, HALO)
            acc = jnp.zeros((HALO, C), F32) + cb_ref[...]
            for j in range(CONV_WIDTH):
                off = 2 + j
                acc = acc + cw_ref[pl.ds(j, 1), :] * a_ext[off % 8, pl.ds(pl.multiple_of(base + off // 8 * 8, 8), HALO), :]
            yc_ref[pl.ds(base, HALO), :] = acc
            mu = jnp.mean(acc, axis=-1, keepdims=True)
            yc = acc - mu
            rstd = lax.rsqrt(jnp.mean(yc * yc, axis=-1, keepdims=True) + EPS)
            z = (yc * rstd) * lg_ref[...] + lb_ref[...]
            yab_ref[pl.ds(base, HALO), 0:C] = (z * jax.nn.sigmoid(z)).astype(BF16)
            pwin = p_ext[pl.ds(base, 2 * HALO), :]
            for gi, w in enumerate(POOL_WINDOWS):
                lo, hi = gi * HEAD, (gi + 1) * HEAD
                x = pwin[HALO:, lo:hi]
                tot = x
                for k in range(1, w):
                    tot = tot + pwin[HALO - k:2 * HALO - k, lo:hi]
                cnt = _pool_counts(i * tm + base, HALO, w)
                d_buf[pl.ds(base, HALO), lo:hi] = (tot / cnt - x).astype(BF16)
            return carry

        lax.fori_loop(0, nsub, sub, 0, unroll=2)
        for gi in range(len(POOL_WINDOWS)):
            lo, hi = gi * HEAD, (gi + 1) * HEAD
            y = _nn(d_buf[:, lo:hi], pw_ref[gi]) + pb_ref[:, lo:hi]
            yab_ref[:, C + lo:C + hi] = (y * ps_ref[:, lo:hi]).astype(BF16)

    def main(c):
        return pl.BlockSpec((tm, C), lambda i: (i, c))

    def halo(c):
        return pl.BlockSpec((HALO, C), lambda i: (jnp.maximum(i * hb - 1, 0), c))

    vec = pl.BlockSpec((1, C), lambda i: (0, 0))
    return pl.pallas_call(
        body, grid=(T // tm,),
        in_specs=[main(0), main(1), main(2), halo(0), halo(1), halo(2),
                  pl.BlockSpec((32, C), lambda i: (0, 0)), vec, vec, vec,
                  pl.BlockSpec((4, HEAD, HEAD), lambda i: (0, 0, 0)), vec, vec],
        out_specs=[pl.BlockSpec((tm, 2 * C), lambda i: (i, 0)), pl.BlockSpec((tm, C), lambda i: (i, 0))],
        out_shape=[jax.ShapeDtypeStruct((T, 2 * C), BF16), jax.ShapeDtypeStruct((T, C), F32)],
        scratch_shapes=[pltpu.VMEM((8, tm + HALO + 8, C), F32), pltpu.VMEM((tm + HALO, C), F32),
                        pltpu.VMEM((tm, C), BF16)],
        compiler_params=_cp("parallel"), name="ev_fwd")(u, u, u, u, u, u, cw, cb, lg, lb, pw, pb, ps)


def _ev_bwd(dyab, yc, u, cw, lg, lb, pw, pwt, pb, ps):
    T = u.shape[0]
    C = 512
    tm = _tile(T, EV_ROWS, HALO)
    nsub = tm // HALO
    hb = tm // HALO
    nblk = T // tm
    E = tm + HALO

    def body(dya_ref, dyb_ref, dyah_ref, dybh_ref, yc_ref, ych_ref, val_ref, gate_ref, pin_ref, valh_ref, gateh_ref,
             pinh_ref, cw_ref, lg_ref, lb_ref, pw_ref, pwt_ref, pb_ref, ps_ref,
             du_ref, dcw_ref, dvec_ref, dpw_ref,
             dy_ext, a_ext, p_ext, ddc_ext, dd_buf, d_buf, dpre_buf, dcw_acc, vec_acc):
        i = pl.program_id(0)
        nf = (i > 0).astype(F32)
        nl = (i < nblk - 1).astype(F32)

        @pl.when(i == 0)
        def _():
            dcw_ref[...] = jnp.zeros_like(dcw_ref)
            dvec_ref[...] = jnp.zeros_like(dvec_ref)
            dpw_ref[...] = jnp.zeros_like(dpw_ref)

        dcw_acc[...] = jnp.zeros_like(dcw_acc)
        vec_acc[...] = jnp.zeros_like(vec_acc)
        a_ext[0, 0:HALO, :] = valh_ref[...] * jax.nn.sigmoid(gateh_ref[...]) * nf
        a_ext[0, HALO:E, :] = val_ref[...] * jax.nn.sigmoid(gate_ref[...])
        p_ext[0:HALO, :] = pinh_ref[...] * nf
        p_ext[HALO:, :] = pin_ref[...]
        _shifted_copies(a_ext, E)

        def ln_bwd(y, dya, main):
            mu = jnp.mean(y, axis=-1, keepdims=True)
            ycen = y - mu
            rstd = lax.rsqrt(jnp.mean(ycen * ycen, axis=-1, keepdims=True) + EPS)
            yh = ycen * rstd
            z = yh * lg_ref[...] + lb_ref[...]
            sz = jax.nn.sigmoid(z)
            dz = dya * _dsilu(z, sz)
            dyh = dz * lg_ref[...]
            dy = rstd * (dyh - jnp.mean(dyh, axis=-1, keepdims=True) - yh * jnp.mean(dyh * yh, axis=-1, keepdims=True))
            if main:
                vec_acc[1] += jnp.sum((dz * yh).reshape(HALO // 8, 8, C), axis=0)
                vec_acc[2] += jnp.sum(dz.reshape(HALO // 8, 8, C), axis=0)
                vec_acc[0] += jnp.sum(dy.reshape(HALO // 8, 8, C), axis=0)
            return dy

        def pool_dd(dyb, base, main):
            dpre = dyb * ps_ref[...]
            for gi, w in enumerate(POOL_WINDOWS):
                lo, hi = gi * HEAD, (gi + 1) * HEAD
                dd = _nn(dpre[:, lo:hi].astype(BF16), pwt_ref[gi])
                cnt = _pool_counts(i * tm + base, HALO, w)
                ddc_ext[pl.ds(base, HALO), lo:hi] = dd / cnt
                if main:
                    dd_buf[pl.ds(base, HALO), lo:hi] = dd
            if main:
                dpre_buf[pl.ds(base, HALO), :] = dpre.astype(BF16)
                vec_acc[4] += jnp.sum(dpre.reshape(HALO // 8, 8, C), axis=0)

        def p1(s, carry):
            base = pl.multiple_of(s * HALO, HALO)
            dy_ext[0, pl.ds(base, HALO), :] = ln_bwd(yc_ref[pl.ds(base, HALO), :], dya_ref[pl.ds(base, HALO), :], True)
            pool_dd(dyb_ref[pl.ds(base, HALO), :], base, True)
            return carry

        lax.fori_loop(0, nsub, p1, 0, unroll=2)
        dy_ext[0, tm:E, :] = ln_bwd(ych_ref[...], dyah_ref[...], False) * nl
        _shifted_copies(dy_ext, E)
        dpre_h = dybh_ref[...] * ps_ref[...] * nl
        for gi, w in enumerate(POOL_WINDOWS):
            lo, hi = gi * HEAD, (gi + 1) * HEAD
            dd = _nn(dpre_h[:, lo:hi].astype(BF16), pwt_ref[gi])
            ddc_ext[tm:, lo:hi] = dd / _pool_counts(i * tm + tm, HALO, w)

        def p2(s, carry):
            base = pl.multiple_of(s * HALO, HALO)
            dy_m = dy_ext[0, pl.ds(base, HALO), :]
            da = jnp.zeros((HALO, C), F32)
            for j in range(CONV_WIDTH):
                sh = CONV_WIDTH - 1 - j
                off = 2 + j
                da = da + cw_ref[pl.ds(j, 1), :] * dy_ext[sh % 8, pl.ds(pl.multiple_of(base + sh // 8 * 8, 8), HALO), :]
                a_j = a_ext[off % 8, pl.ds(pl.multiple_of(base + off // 8 * 8, 8), HALO), :]
                dcw_acc[j] += jnp.sum((dy_m * a_j).reshape(HALO // 8, 8, C), axis=0)
            v = val_ref[pl.ds(base, HALO), :]
            g = gate_ref[pl.ds(base, HALO), :]
            sg = jax.nn.sigmoid(g)
            du_ref[pl.ds(base, HALO), 0:C] = (da * sg).astype(BF16)
            du_ref[pl.ds(base, HALO), C:2 * C] = (da * v * sg * (1.0 - sg)).astype(BF16)
            pwin = p_ext[pl.ds(base, 2 * HALO), :]
            cwin = ddc_ext[pl.ds(base, 2 * HALO), :]
            for gi, w in enumerate(POOL_WINDOWS):
                lo, hi = gi * HEAD, (gi + 1) * HEAD
                x = pwin[HALO:, lo:hi]
                tot = x
                back = cwin[0:HALO, lo:hi]
                for k in range(1, w):
                    tot = tot + pwin[HALO - k:2 * HALO - k, lo:hi]
                    back = back + cwin[k:k + HALO, lo:hi]
                cnt = _pool_counts(i * tm + base, HALO, w)
                d_buf[pl.ds(base, HALO), lo:hi] = (tot / cnt - x).astype(BF16)
                du_ref[pl.ds(base, HALO), 2 * C + lo:2 * C + hi] = (back - dd_buf[pl.ds(base, HALO), lo:hi]).astype(BF16)
            return carry

        lax.fori_loop(0, nsub, p2, 0)
        for gi in range(len(POOL_WINDOWS)):
            lo, hi = gi * HEAD, (gi + 1) * HEAD
            pre = _nn(d_buf[:, lo:hi], pw_ref[gi]) + pb_ref[:, lo:hi]
            vec_acc[3, :, lo:hi] += jnp.sum((dyb_ref[:, lo:hi] * pre).reshape(tm // 8, 8, HEAD), axis=0)
            dpw_ref[gi] += _tn(d_buf[:, lo:hi], dpre_buf[:, lo:hi])
        for j in range(CONV_WIDTH):
            dcw_ref[pl.ds(j, 1), :] += jnp.sum(dcw_acc[j], axis=0, keepdims=True)
        for r in range(5):
            dvec_ref[pl.ds(r, 1), :] += jnp.sum(vec_acc[r], axis=0, keepdims=True)

    def main(c, width=C):
        return pl.BlockSpec((tm, width), lambda i: (i, c))

    def prev(c):
        return pl.BlockSpec((HALO, C), lambda i: (jnp.maximum(i * hb - 1, 0), c))

    def nxt(c):
        return pl.BlockSpec((HALO, C), lambda i: (jnp.minimum((i + 1) * hb, T // HALO - 1), c))

    vec = pl.BlockSpec((1, C), lambda i: (0, 0))
    mat = pl.BlockSpec((4, HEAD, HEAD), lambda i: (0, 0, 0))
    return pl.pallas_call(
        body, grid=(nblk,),
        in_specs=[main(0), main(1), nxt(0), nxt(1), main(0), nxt(0), main(0), main(1), main(2), prev(0), prev(1),
                  prev(2), pl.BlockSpec((32, C), lambda i: (0, 0)), vec, vec, mat, mat, vec, vec],
        out_specs=[pl.BlockSpec((tm, 3 * C), lambda i: (i, 0)), pl.BlockSpec((32, C), lambda i: (0, 0)),
                   pl.BlockSpec((8, C), lambda i: (0, 0)), mat],
        out_shape=[jax.ShapeDtypeStruct((T, 3 * C), BF16), jax.ShapeDtypeStruct((32, C), F32),
                   jax.ShapeDtypeStruct((8, C), F32), jax.ShapeDtypeStruct((4, HEAD, HEAD), F32)],
        scratch_shapes=[pltpu.VMEM((8, E + 8, C), F32), pltpu.VMEM((8, E + 8, C), F32), pltpu.VMEM((E, C), F32),
                        pltpu.VMEM((E, C), F32), pltpu.VMEM((tm, C), F32), pltpu.VMEM((tm, C), BF16),
                        pltpu.VMEM((tm, C), BF16), pltpu.VMEM((32, 8, C), F32), pltpu.VMEM((8, 8, C), F32)],
        compiler_params=_cp("arbitrary"), name="ev_bwd")(
            dyab, dyab, dyab, dyab, yc, yc, u, u, u, u, u, u, cw, lg, lb, pw, pwt, pb, ps)


def _cumsum_rows(x, reverse=False):
    n = x.shape[0]
    rid = lax.broadcasted_iota(jnp.int32, (n, 1), 0)
    k = 1
    while k < n:
        if reverse:
            sh = jnp.where(rid < n - k, pltpu.roll(x, n - k, 0), 0.0)
        else:
            sh = jnp.where(rid >= k, pltpu.roll(x, k, 0), 0.0)
        x = x + sh
        k *= 2
    return x


def _hgrn_gates(qr, fr, lbv):
    sq = jax.nn.sigmoid(qr)
    sg = jax.nn.sigmoid(fr)
    fg = lbv + (1.0 - lbv) * sg
    return qr * sq, sq, sg, fg, 1.0 - fg, jnp.log(fg)


def _hgrn_fwd(u, lbv, gn):
    T = u.shape[0]
    H = 8
    RB = _tile(T, HGRN_ROWS, CHUNK)
    NC = RB // CHUNK
    NS = CHUNK // SUB

    HP = HGRN_HEADS_FWD
    W = HP * HEAD

    def body(q_ref, f_ref, i_ref, g_ref, lb_ref, gn_ref, y_ref, o_ref, s0_ref, st, qs, ks, bs, vs, os_):
        rb = pl.program_id(1)

        @pl.when(rb == 0)
        def _():
            st[...] = jnp.zeros_like(st)

        t8 = lax.broadcasted_iota(jnp.int32, (8, 1), 0)

        def head(hh, c, rows):
            sl = slice(hh * HEAD, (hh + 1) * HEAD)
            q, _, _, _, kk, lf = _hgrn_gates(q_ref[rows, sl], f_ref[rows, sl], lb_ref[:, sl])
            v = i_ref[rows, sl]
            b = _cumsum_rows(lf)
            qs[hh] = q
            ks[hh] = kk
            bs[hh] = b
            vs[hh] = v
            st0 = st[hh]
            s0_ref[hh, c] = st0
            os_[hh] = _nt((q * jnp.exp(b)).astype(BF16), st0.astype(BF16))
            for I in range(NS):
                lo = I * SUB
                qI = qs[hh, lo:lo + SUB, :]
                bI = bs[hh, lo:lo + SUB, :]
                oI = jnp.zeros((SUB, HEAD), F32)
                if I > 0:
                    bprev = bs[hh, pl.ds(lo - 1, 1), :]
                    qt = _r16(qI * jnp.exp(bI - bprev))
                    kt = _r16(ks[hh, 0:lo, :] * jnp.exp(bprev - bs[hh, 0:lo, :]))
                    A = _nt(qt, kt)
                    oI = oI + _nn(_r16(A), _r16(vs[hh, 0:lo, :]))
                od = [jnp.zeros((8, HEAD), F32) for _ in range(SUB // 8)]
                for s in range(SUB):
                    row = pl.ds(lo + s, 1)
                    brow, krow, vrow = bs[hh, row, :], ks[hh, row, :], vs[hh, row, :]
                    for ti in range(SUB // 8):
                        o8 = 8 * ti
                        if s > o8 + 7:
                            continue
                        d = bI[o8:o8 + 8] - brow
                        if s > o8:
                            d = jnp.where(t8 >= s - o8, d, NEG)
                        col = jnp.sum(qI[o8:o8 + 8] * jnp.exp(d) * krow, axis=1, keepdims=True)
                        od[ti] = od[ti] + col * vrow
                os_[hh, lo:lo + SUB, :] += oI + jnp.concatenate(od, axis=0)
            blast = bs[hh, pl.ds(CHUNK - 1, 1), :]
            kh = kk * jnp.exp(blast - b)
            st[hh] = st0 * jnp.exp(blast) + _tn(v.astype(BF16), kh.astype(BF16))
            o = os_[hh]
            o_ref[rows, sl] = o
            rr = lax.rsqrt(jnp.mean(o * o, axis=-1, keepdims=True) + EPS)
            gr = g_ref[rows, sl]
            y_ref[rows, sl] = (((o * rr) * gn_ref[...]) * (gr * jax.nn.sigmoid(gr))).astype(BF16)

        def chunk(c, carry):
            rows = pl.ds(pl.multiple_of(c * CHUNK, CHUNK), CHUNK)
            for hh in range(HP):
                head(hh, c, rows)
            return carry

        lax.fori_loop(0, NC, chunk, 0)

    def blk(q):
        return pl.BlockSpec((RB, W), lambda h, r: (r, q * (H // HP) + h))

    sc = lambda: pltpu.VMEM((HP, CHUNK, HEAD), F32)
    return pl.pallas_call(
        body, grid=(H // HP, T // RB),
        in_specs=[blk(0), blk(1), blk(2), blk(3), pl.BlockSpec((1, W), lambda h, r: (0, h)),
                  pl.BlockSpec((1, HEAD), lambda h, r: (0, 0))],
        out_specs=[pl.BlockSpec((RB, W), lambda h, r: (r, h)), pl.BlockSpec((RB, W), lambda h, r: (r, h)),
                   pl.BlockSpec((HP, NC, HEAD, HEAD), lambda h, r: (h, r, 0, 0))],
        out_shape=[jax.ShapeDtypeStruct((T, H * HEAD), BF16), jax.ShapeDtypeStruct((T, H * HEAD), F32),
                   jax.ShapeDtypeStruct((H, T // CHUNK, HEAD, HEAD), F32)],
        scratch_shapes=[pltpu.VMEM((HP, HEAD, HEAD), F32), sc(), sc(), sc(), sc(), sc()],
        compiler_params=_cp("parallel", "arbitrary"), name="hgrn_fwd")(u, u, u, u, lbv, gn)


def _hgrn_bwd(dy, o, s0, u, lbv, gn):
    T = u.shape[0]
    H = 8
    RB = _tile(T, HGRN_ROWS, CHUNK)
    NB = T // RB
    NC = RB // CHUNK
    NS = CHUNK // SUB

    def body(q_ref, f_ref, i_ref, g_ref, lb_ref, gn_ref, o_ref, dy_ref, s0_ref, du_ref, dlb_ref, dgn_ref,
             dst, qs, ks, bs, vs, dos, dqs, dks, dki, dvs, dbs):
        rb = pl.program_id(1)

        @pl.when(rb == 0)
        def _():
            dst[...] = jnp.zeros_like(dst)
            dlb_ref[...] = jnp.zeros_like(dlb_ref)
            dgn_ref[...] = jnp.zeros_like(dgn_ref)

        t8 = lax.broadcasted_iota(jnp.int32, (8, 1), 0)
        lane = lax.broadcasted_iota(jnp.int32, (8, HEAD), 1)
        gnv = gn_ref[...]

        def head(hh, c, rows):
            sl = slice(hh * HEAD, (hh + 1) * HEAD)
            lbv_ = lb_ref[:, sl]
            qr = q_ref[rows, sl]
            q, sq, sg, fg, kk, lf = _hgrn_gates(qr, f_ref[rows, sl], lbv_)
            v = i_ref[rows, sl]
            gr = g_ref[rows, sl]
            b = _cumsum_rows(lf)
            eb = jnp.exp(b)
            ov = o_ref[rows, sl]
            dyv = dy_ref[rows, sl]
            rr = lax.rsqrt(jnp.mean(ov * ov, axis=-1, keepdims=True) + EPS)
            oh = ov * rr
            gs = jax.nn.sigmoid(gr)
            dgr = dyv * (oh * gnv) * _dsilu(gr, gs)
            dnrm = dyv * (gr * gs)
            dgn_ref[hh] += jnp.sum(dnrm * oh, axis=0, keepdims=True)
            t1 = dnrm * gnv
            do = rr * (t1 - oh * jnp.mean(t1 * oh, axis=-1, keepdims=True))
            qs[hh] = q
            ks[hh] = kk
            bs[hh] = b
            vs[hh] = v
            dos[hh] = do
            st0 = s0_ref[hh, c]
            dS = dst[hh]
            do_b = do.astype(BF16)
            blast = bs[hh, pl.ds(CHUNK - 1, 1), :]
            elast = jnp.exp(blast - b)
            dq_inter = _nn(do_b, st0.astype(BF16)) * eb
            dqs[hh] = dq_inter
            dbs[hh] = q * dq_inter
            kh = kk * elast
            dvs[hh] = _nt(kh.astype(BF16), dS.astype(BF16))
            dk_inter = _nn(v.astype(BF16), dS.astype(BF16)) * elast
            dki[hh] = dk_inter
            dks[hh] = jnp.zeros((CHUNK, HEAD), F32)
            for I in range(NS):
                lo = I * SUB
                qI = qs[hh, lo:lo + SUB, :]
                bI = bs[hh, lo:lo + SUB, :]
                doI = dos[hh, lo:lo + SUB, :]
                dqI = jnp.zeros((SUB, HEAD), F32)
                dbI = jnp.zeros((SUB, HEAD), F32)
                if I > 0:
                    bprev = bs[hh, pl.ds(lo - 1, 1), :]
                    eq = jnp.exp(bI - bprev)
                    ek = jnp.exp(bprev - bs[hh, 0:lo, :])
                    qt = _r16(qI * eq)
                    kt = _r16(ks[hh, 0:lo, :] * ek)
                    A = _r16(_nt(qt, kt))
                    doI_b = _r16(doI)
                    dA = _r16(_nt(doI_b, _r16(vs[hh, 0:lo, :])))
                    dvs[hh, 0:lo, :] += _tn(A, doI_b)
                    dqt = _nn(dA, kt)
                    dkt = _tn(dA, qt)
                    dqI = dqI + dqt * eq
                    dbI = dbI + qt.astype(F32) * dqt
                    dks[hh, 0:lo, :] += dkt * ek
                    dbs[hh, 0:lo, :] -= kt.astype(F32) * dkt
                dq_t = [jnp.zeros((8, HEAD), F32) for _ in range(SUB // 8)]
                a_t = [jnp.zeros((8, HEAD), F32) for _ in range(SUB // 8)]
                for s in range(SUB):
                    row = pl.ds(lo + s, 1)
                    brow, krow, vrow = bs[hh, row, :], ks[hh, row, :], vs[hh, row, :]
                    dk_s = None
                    for ti in range(SUB // 8):
                        o8 = 8 * ti
                        if s > o8 + 7:
                            continue
                        d = bI[o8:o8 + 8] - brow
                        if s > o8:
                            d = jnp.where(t8 >= s - o8, d, NEG)
                        Es = jnp.exp(d)
                        qE = qI[o8:o8 + 8] * Es
                        col = jnp.sum(qE * krow, axis=1, keepdims=True)
                        a_t[ti] = jnp.where(lane == s, col, a_t[ti])
                        dcol = jnp.sum(doI[o8:o8 + 8] * vrow, axis=1, keepdims=True)
                        dq_t[ti] = dq_t[ti] + (dcol * Es) * krow
                        part = jnp.sum(dcol * qE, axis=0, keepdims=True)
                        dk_s = part if dk_s is None else dk_s + part
                    dks[hh, row, :] += dk_s
                    dbs[hh, row, :] -= krow * dk_s
                a_d = jnp.concatenate(a_t, axis=0)
                dq_d = jnp.concatenate(dq_t, axis=0)
                dvs[hh, lo:lo + SUB, :] += _tn(a_d, doI)[0:SUB]
                dqI = dqI + dq_d
                dbI = dbI + qI * dq_d
                dqs[hh, lo:lo + SUB, :] += dqI
                dbs[hh, lo:lo + SUB, :] += dbI
            kdk = kk * dki[hh]
            excl = _cumsum_rows(kdk) - kdk
            suff = _cumsum_rows(dbs[hh], reverse=True)
            gdec = jnp.sum(dS * st0, axis=0, keepdims=True) * jnp.exp(blast)
            dlf = suff + excl + gdec
            dk = dks[hh] + dki[hh]
            dfg = dlf / fg - dk
            dlb_ref[:, sl] += jnp.sum(dfg * (1.0 - sg), axis=0, keepdims=True)
            du_ref[0, rows, sl] = (dqs[hh] * _dsilu(qr, sq)).astype(BF16)
            du_ref[1, rows, sl] = (dfg * (1.0 - lbv_) * sg * (1.0 - sg)).astype(BF16)
            du_ref[2, rows, sl] = dvs[hh].astype(BF16)
            du_ref[3, rows, sl] = dgr.astype(BF16)
            dst[hh] = dS * jnp.exp(blast) + _tn(do_b, (q * eb).astype(BF16))

        def chunk(cc, carry):
            c = NC - 1 - cc
            rows = pl.ds(pl.multiple_of(c * CHUNK, CHUNK), CHUNK)
            for hh in range(HP):
                head(hh, c, rows)
            return carry

        lax.fori_loop(0, NC, chunk, 0)

    HP = HGRN_HEADS_BWD
    W = HP * HEAD

    def blk(qd):
        return pl.BlockSpec((RB, W), lambda h, r: (NB - 1 - r, qd * (H // HP) + h))

    hblk = pl.BlockSpec((RB, W), lambda h, r: (NB - 1 - r, h))
    sc = lambda: pltpu.VMEM((HP, CHUNK, HEAD), F32)
    return pl.pallas_call(
        body, grid=(H // HP, NB),
        in_specs=[blk(0), blk(1), blk(2), blk(3), pl.BlockSpec((1, W), lambda h, r: (0, h)),
                  pl.BlockSpec((1, HEAD), lambda h, r: (0, 0)), hblk, hblk,
                  pl.BlockSpec((HP, NC, HEAD, HEAD), lambda h, r: (h, NB - 1 - r, 0, 0))],
        out_specs=[pl.BlockSpec((4, RB, W), lambda h, r: (0, NB - 1 - r, h)),
                   pl.BlockSpec((1, W), lambda h, r: (0, h)), pl.BlockSpec((HP, 1, HEAD), lambda h, r: (h, 0, 0))],
        out_shape=[jax.ShapeDtypeStruct((4, T, H * HEAD), BF16), jax.ShapeDtypeStruct((1, H * HEAD), F32),
                   jax.ShapeDtypeStruct((H, 1, HEAD), F32)],
        scratch_shapes=[pltpu.VMEM((HP, HEAD, HEAD), F32)] + [sc() for _ in range(10)],
        compiler_params=_cp("parallel", "arbitrary"), name="hgrn_bwd")(u, u, u, u, lbv, gn, o, dy, s0)


def _softmax_rows(p_ref, L):
    rows = [p_ref[pl.ds(l, 1), :] for l in range(L)]
    m = rows[0]
    for r in rows[1:]:
        m = jnp.maximum(m, r)
    e = [jnp.exp(r - m) for r in rows]
    tot = e[0]
    for t in e[1:]:
        tot = tot + t
    return [t / tot for t in e]


def _lb_fwd(lbp):
    L, D = lbp.shape

    def body(p_ref, o_ref):
        sm = _softmax_rows(p_ref, L)
        acc = jnp.zeros((1, D), F32)
        o_ref[pl.ds(0, 1), :] = acc
        for l in range(1, L):
            acc = acc + sm[l]
            o_ref[pl.ds(l, 1), :] = acc

    return pl.pallas_call(body, out_shape=jax.ShapeDtypeStruct((L, D), F32), name="lb_fwd")(lbp)


def _lb_bwd(lbp, dlb):
    L, D = lbp.shape

    def body(p_ref, d_ref, o_ref):
        sm = _softmax_rows(p_ref, L)
        dsm = [jnp.zeros((1, D), F32)]
        for i in range(1, L):
            t = jnp.zeros((1, D), F32)
            for l in range(i, L):
                t = t + d_ref[pl.ds(l, 1), :]
            dsm.append(t)
        dot = jnp.zeros((1, D), F32)
        for i in range(L):
            dot = dot + dsm[i] * sm[i]
        for i in range(L):
            o_ref[pl.ds(i, 1), :] = sm[i] * (dsm[i] - dot)

    return pl.pallas_call(body, out_shape=jax.ShapeDtypeStruct((L, D), F32), name="lb_bwd")(lbp, dlb)


def _my_pos():
    return lax.axis_index("x"), lax.axis_index("y"), lax.axis_index("c")


def _peer(mask):
    x, y, c = _my_pos()
    mx, my, mc = (mask >> 2) & 1, (mask >> 1) & 1, mask & 1
    px = (1 - x) if mx else x
    py = (1 - y) if my else y
    pc = (1 - c) if mc else c
    return (px, py, pc), 4 * px + 2 * py + pc


HBM_SPEC =pl.BlockSpec(memory_space=pltpu.HBM)
SEM_SPEC = pl.BlockSpec(memory_space=pltpu.SEMAPHORE)
EFFECT = pltpu.SideEffectType.DATAFLOW_SIDE_EFFECTING


def _hbm(a):
    return pltpu.with_memory_space_constraint(a, pltpu.HBM)


def _landing(block_shape, dtype, axis=0):
    if axis == 0:
        return lax.empty((N_DEV,) + tuple(block_shape), dtype)
    rows, n = block_shape
    return lax.empty((rows, N_DEV * n), dtype)


def _slot(ref, i):
    if len(ref.shape) == 2:
        n = ref.shape[1] // N_DEV
        return ref.at[:, pl.ds(i * n, n)]
    return ref.at[i]


def _push_start(name, srcs, lands, whole, groups):
    n = len(srcs)
    ng = 1 + max(groups)
    cnt = [groups.count(g) for g in range(ng)]
    idx = [groups[:a].count(groups[a]) for a in range(n)]

    def body(*refs):
        src_refs, land_refs = refs[:n], refs[n:2 * n]
        sems = refs[2 * n:2 * n + 3 * ng]
        token = refs[-1]
        x, y, c = _my_pos()
        me = 4 * x + 2 * y + c
        for a in range(n):
            g = groups[a]
            for m in range(1, N_DEV):
                peer, pid = _peer(m)
                pltpu.make_async_remote_copy(
                    src_ref=src_refs[a] if whole else src_refs[a].at[pid], dst_ref=_slot(land_refs[a], me),
                    send_sem=sems[3 * g].at[idx[a] * (N_DEV - 1) + m - 1],
                    recv_sem=sems[3 * g + 1].at[idx[a] * (N_DEV - 1) + m - 1],
                    device_id=peer, device_id_type=MESH).start()
            pltpu.make_async_copy(src_refs[a] if whole else src_refs[a].at[me], _slot(land_refs[a], me),
                                  sems[3 * g + 2].at[idx[a]]).start()
        token[...] = jnp.zeros_like(token)

    sem_shapes = []
    for g in range(ng):
        sem_shapes += [pltpu.SemaphoreType.DMA((cnt[g] * (N_DEV - 1),))] * 2 + [pltpu.SemaphoreType.DMA((cnt[g],))]
    thru = [pltpu.HBM(s.shape, s.dtype) for s in list(srcs) + list(lands)]
    res = pl.pallas_call(
        body, name=name,
        out_shape=tuple(sem_shapes + thru + [jax.ShapeDtypeStruct((8, 128), F32)]),
        in_specs=tuple([HBM_SPEC] * (2 * n)),
        out_specs=tuple([SEM_SPEC] * (3 * ng) + [HBM_SPEC] * (2 * n) + [pl.BlockSpec(memory_space=pltpu.VMEM)]),
        input_output_aliases={i: 3 * ng + i for i in range(2 * n)},
        compiler_params=pltpu.CompilerParams(has_side_effects=EFFECT),
    )(*[_hbm(s) for s in srcs], *[_hbm(z) for z in lands])
    sems = [(res[3 * g], res[3 * g + 1], res[3 * g + 2]) for g in range(ng)]
    srcs_thru = list(res[3 * ng:3 * ng + n])
    lands_thru = list(res[3 * ng + n:3 * ng + 2 * n])
    return sems, srcs_thru, lands_thru, res[-1]


def _push_wait(name, srcs_thru, lands_thru, sems, after, whole):
    n = len(srcs_thru)

    def body(*refs):
        src_refs, land_refs = refs[:n], refs[n:2 * n]
        send_sems, recv_sems, own_sems = refs[2 * n], refs[2 * n + 1], refs[2 * n + 2]
        x, y, c = _my_pos()
        me = 4 * x + 2 * y + c
        for a in range(n):
            pltpu.make_async_copy(src_refs[a] if whole else src_refs[a].at[me], _slot(land_refs[a], me),
                                  own_sems.at[a]).wait()
            for m in range(1, N_DEV):
                peer, pid = _peer(m)
                cp = pltpu.make_async_remote_copy(
                    src_ref=src_refs[a] if whole else src_refs[a].at[pid], dst_ref=_slot(land_refs[a], pid),
                    send_sem=send_sems.at[a * (N_DEV - 1) + m - 1], recv_sem=recv_sems.at[a * (N_DEV - 1) + m - 1],
                    device_id=peer, device_id_type=MESH)
                cp.wait_send()
                cp.wait_recv()

    thru = [pltpu.HBM(s.shape, s.dtype) for s in list(srcs_thru) + list(lands_thru)]
    res = pl.pallas_call(
        body, name=name, out_shape=tuple(thru),
        in_specs=tuple([HBM_SPEC] * (2 * n) + [SEM_SPEC, SEM_SPEC, SEM_SPEC, ANY]),
        out_specs=tuple([HBM_SPEC] * (2 * n)),
        input_output_aliases={i: i for i in range(2 * n)},
        compiler_params=pltpu.CompilerParams(has_side_effects=EFFECT),
    )(*srcs_thru, *lands_thru, sems[0], sems[1], sems[2], after)
    return list(res[n:])


def _adamw(recv, w, m, v, layer=0, prev=None):
    L, R, C = w.shape
    tr = _tile(R, max(8, (1 << 18) // C), 8) if R % 8 == 0 else R
    bc1 = 1.0 - ADAM_B1 ** ADAM_STEP
    bc2 = 1.0 - ADAM_B2 ** ADAM_STEP
    if prev is None:
        prev = [lax.empty((L, R, C), F32) for _ in range(4)]

    def body(r_ref, w_ref, m_ref, v_ref, p0, p1, p2, p3, g_ref, d_ref, nm_ref, nv_ref):
        g = r_ref[0].astype(F32)
        for s in range(1, N_DEV):
            g = g + r_ref[s].astype(F32)
        nm = ADAM_B1 * m_ref[...] + (1.0 - ADAM_B1) * g
        nv = ADAM_B2 * v_ref[...] + (1.0 - ADAM_B2) * (g * g)
        mh = nm / bc1
        vh = nv / bc2
        g_ref[...] = g
        d_ref[...] = -ADAM_LR * (mh / (jnp.sqrt(vh) + ADAM_EPS) + ADAM_WD * w_ref[...])
        nm_ref[...] = nm
        nv_ref[...] = nv

    row = pl.BlockSpec((None, tr, C), lambda i: (layer, i, 0))
    return pl.pallas_call(
        body, grid=(R // tr,),
        in_specs=[pl.BlockSpec((N_DEV, tr, C), lambda i: (0, i, 0)), row, row, row] + [ANY] * 4,
        out_specs=[row] * 4, out_shape=[jax.ShapeDtypeStruct((L, R, C), F32)] * 4,
        input_output_aliases={4: 0, 5: 1, 6: 2, 7: 3},
        compiler_params=_cp("parallel"), name="adamw")(recv, w, m, v, *prev)


def _full_w_spec(tk, tn):
    return pl.BlockSpec((tk, tn), lambda i, j, k: (k, j))


def kernel(x, meta_tokens, mix_norm_g, mlp_norm_g, final_norm_g, ev_w_in, ev_conv_w, ev_conv_b, ev_ln_g, ev_ln_b, ev_pool_w, ev_pool_b, ev_pool_scale, ev_w_out, od_w_in, od_gnorm_g, od_w_out, lb_param, mlp_w1, mlp_w2, loss_target, m_meta_tokens, m_mix_norm_g, m_mlp_norm_g, m_final_norm_g, m_ev_w_in, m_ev_conv_w, m_ev_conv_b, m_ev_ln_g, m_ev_ln_b, m_ev_pool_w, m_ev_pool_b, m_ev_pool_scale, m_ev_w_out, m_od_w_in, m_od_gnorm_g, m_od_w_out, m_lb_param, m_mlp_w1, m_mlp_w2, v_meta_tokens, v_mix_norm_g, v_mlp_norm_g, v_final_norm_g, v_ev_w_in, v_ev_conv_w, v_ev_conv_b, v_ev_ln_g, v_ev_ln_b, v_ev_pool_w, v_ev_pool_b, v_ev_pool_scale, v_ev_w_out, v_od_w_in, v_od_gnorm_g, v_od_w_out, v_lb_param, v_mlp_w1, v_mlp_w2):
    S, D = x.shape[1], x.shape[2]
    T = PAD + N_META + S
    DEPTH = mix_norm_g.shape[0]
    DFF = mlp_w1.shape[2] * N_DEV
    dev = 4 * lax.axis_index("x") + 2 * lax.axis_index("y") + lax.axis_index("c")

    n_ev = ev_w_in.shape[0]
    n_od = od_w_in.shape[0]
    n_in_od = od_w_in.shape[2]
    n_w1 = mlp_w1.shape[2]

    ag_src, ag_grp, ag_axis, ag_at = [], [], [], {}
    for key, arr in (("meta", meta_tokens), ("cw", ev_conv_w)):
        ag_at["small", key] = len(ag_src)
        ag_src.append(arr)
        ag_grp.append(len(ag_grp))
        ag_axis.append(0)
    for layer in range(DEPTH):
        j = layer // 2
        mixer = [("in", ev_w_in[j]), ("out", ev_w_out[j])] if layer % 2 == 0 else [("in", od_w_in[j]), ("out", od_w_out[j])]
        for pos, (key, arr) in enumerate(mixer + [("w1", mlp_w1[layer]), ("w2", mlp_w2[layer])]):
            ag_at[layer, key] = len(ag_src)
            ag_src.append(arr.astype(BF16))
            ag_grp.append(len(ag_grp))
            ag_axis.append(1 if key in ("in", "w1") and arr.shape[1] % 128 == 0 else 0)
    ag_sems, ag_s, ag_l, ag_tok = _push_start(
        "ag_start", ag_src, [_landing(s_.shape, s_.dtype, ax) for s_, ax in zip(ag_src, ag_axis)], True, ag_grp)

    def ag_wait(layer, key, after):
        a = ag_at[layer, key]
        return _push_wait(f"ag_wait_{a}", [ag_s[a]], [ag_l[a]], ag_sems[a], after, True)[0]

    g_meta = ag_wait("small", "meta", ag_tok)
    g_cw = ag_wait("small", "cw", ag_tok)
    meta_full = jnp.transpose(g_meta, (1, 0, 2)).reshape(N_META, D)
    cw_full = jnp.transpose(g_cw, (1, 2, 0, 3)).reshape(n_ev, CONV_WIDTH, -1)
    cw_pad = jnp.pad(cw_full, ((0, 0), (0, 32 - CONV_WIDTH), (0, 0)))

    h = jnp.concatenate([jnp.zeros((PAD, D), F32), meta_full, x[0]], axis=0) + ag_tok[0, 0]
    tgt = jnp.pad(loss_target[0], ((PAD + N_META, 0), (0, 0)))
    lb_all = _lb_fwd(lb_param)
    h, tgt, lb_all = lax.optimization_barrier((h, tgt, lb_all))

    tm_big = _tile(T, MM_ROWS_BIG, 16)
    tm_mid = _tile(T, MM_ROWS_MID, 16)
    tm_k4 = _tile(T, MM_ROWS_K4, 16)

    saved = []
    for layer in range(DEPTH):
        j = layer // 2
        sv = {"h0": h}
        g_in = ag_wait(layer, "in", h)
        w_in = g_in if g_in.ndim == 2 else jnp.transpose(g_in, (1, 0, 2)).reshape(D, -1)
        if layer % 2 == 0:
            sv["n"], u = _mm_rms_nn("ev_in", h, mix_norm_g[layer][None], w_in, tm_big, 512, "f32")
            yab, yc = _ev_fwd(u, cw_pad[j], ev_conv_b[j][None], ev_ln_g[j][None], ev_ln_b[j][None],
                              ev_pool_w[j].astype(BF16), ev_pool_b[j].reshape(1, -1), ev_pool_scale[j][None])
            sv.update(u=u, y=yab, yc=yc)
            w_out = ag_wait(layer, "out", yab).reshape(-1, D)
            h = _mm_nn("ev_out", yab, w_out, _full_w_spec, T, D, D, tm_mid, D, D, "resid", extra=h)
        else:
            sv["n"], u = _mm_rms_nn("od_in", h, mix_norm_g[layer][None], w_in, tm_big, 512, "f32")
            y, o, s0 = _hgrn_fwd(u, lb_all[layer][None], od_gnorm_g[j][None])
            sv.update(u=u, y=y, o=o, s0=s0)
            w_out = ag_wait(layer, "out", y).reshape(-1, D)
            h = _mm_nn("od_out", y, w_out, _full_w_spec, T, D, D, tm_mid, D, D, "resid", extra=h)
        sv["h1"] = h
        w_w1 = ag_wait(layer, "w1", h)
        n2, r, act = _mm_rms_nn("mlp_w1", h, mlp_norm_g[layer][None], w_w1, tm_big, 512, "relu2")
        w_w2 = ag_wait(layer, "w2", act).reshape(DFF, D)
        sv.update(w_in=w_in, w_out=w_out, w_w1=w_w1, w_w2=w_w2)
        sv.update(n2=n2, r=r, act=act)
        h = _mm_nn("mlp_w2", act, w_w2, _full_w_spec, T, D, DFF, tm_k4, D, DFF, "resid", extra=h)
        saved.append(sv)

    loss_blk, dh, dhb, dg_final = _loss_head(h, final_norm_g[None], tgt)
    loss = lax.psum(loss_blk[0, 0], AXES)

    tt = T
    g_mix, g_mlp = [None] * DEPTH, [None] * DEPTH
    small ={"cw": [None] * n_ev, "vec": [None] * n_ev, "pw": [None] * n_ev, "gn": [None] * n_od}
    dlb_rows = [jnp.zeros((1, D), F32) for _ in range(DEPTH)]

    def xs2(tt_, tk):
        return pl.BlockSpec((tt_, tk), lambda a, b, t: (t, a))

    def ys2(tt_, tn):
        return pl.BlockSpec((tt_, tn), lambda a, b, t: (t, b))

    def os2(tk, tn):
        return pl.BlockSpec((tk, tn), lambda a, b, t: (a, b))

    def os3(tk, tn):
        return pl.BlockSpec((None, tk, tn), lambda a, b, t: (b, a, 0))

    def dy2(tm, tn):
        return pl.BlockSpec((tm, tn), lambda i, jj, k: (i, k))

    def w_rows(tj, tn):
        return pl.BlockSpec((tj, tn), lambda i, jj, k: (jj, k))

    def w_whole(tj, tn):
        return pl.BlockSpec((tj, tn), lambda i, jj, k: (0, 0), pipeline_mode=pl.Buffered(1))

    rs_pending = []

    def rs_start(tag, mats):
        blocks = [m_ if m_.ndim == 3 else m_.reshape(N_DEV, m_.shape[0] // N_DEV, m_.shape[1]) for m_ in mats]
        lands = [_landing(b_.shape[1:], b_.dtype) for b_ in blocks]
        sems, s_thru, l_thru, tok = _push_start(f"rs_start_{tag}", blocks, lands, False, [0] * len(blocks))
        rs_pending.append((tag, s_thru, l_thru, sems[0]))
        return tok[0, 0]

    for layer in reversed(range(DEPTH)):
        j = layer // 2
        sv = saved[layer]
        da1 = _mm_nt("mlp_w2_t", dhb, sv["w_w2"], dy2, w_rows, T, DFF, D, tm_big, 512, D, "dact", extra=sv["r"])
        dw2 = _mm_tn("mlp_dw2", sv["act"], dhb, xs2, ys2, os2, (DFF, D), T, DFF, D, tt, 512, D)
        dw1 = _mm_tn("mlp_dw1", sv["n2"], da1, xs2, ys2, os3, (N_DEV, D, n_w1), T, D, DFF, tt, D, n_w1)
        tok = rs_start(f"mlp{layer}", [dw1, dw2])
        dh, dhb, g_mlp[layer] = _mm_nt("mlp_w1_t", da1, sv["w_w1"], dy2, w_whole, T, D, DFF, tm_k4, D, DFF, "rms",
                                       extra=(sv["h1"], mlp_norm_g[layer][None] + tok, dh))
        if layer % 2 == 0:
            dyab = _mm_nt("ev_out_t", dhb, sv["w_out"], dy2, w_rows, T, D, D, tm_mid, D, D, "f32")
            dwout = _mm_tn("ev_dwout", sv["y"], dhb, xs2, ys2, os2, (D, D), T, D, D, tt, 512, D)
            du, small["cw"][j], small["vec"][j], small["pw"][j] = _ev_bwd(
                dyab, sv["yc"], sv["u"], cw_pad[j], ev_ln_g[j][None], ev_ln_b[j][None], ev_pool_w[j].astype(BF16),
                jnp.transpose(ev_pool_w[j], (0, 2, 1)).astype(BF16), ev_pool_b[j].reshape(1, -1),
                ev_pool_scale[j][None])
            nin = du.shape[1]
            dwin = _mm_tn("ev_dwin", sv["n"], du, xs2, ys2, os2, (D, nin), T, D, nin, tt, D, 512)
            dwin = jnp.transpose(dwin.reshape(D, N_DEV, nin // N_DEV), (1, 0, 2))
            tok = rs_start(f"mix{layer}", [dwin, dwout])
            dh, dhb, g_mix[layer] = _mm_nt("ev_in_t", du, sv["w_in"], dy2, w_whole, T, D, nin, tm_k4, D, nin, "rms",
                                           extra=(sv["h0"], mix_norm_g[layer][None] + tok, dh))
        else:
            dy = _mm_nt("od_out_t", dhb, sv["w_out"], dy2, w_rows, T, D, D, tm_mid, D, D, "f32")
            dwout = _mm_tn("od_dwout", sv["y"], dhb, xs2, ys2, os2, (D, D), T, D, D, tt, 512, D)
            du3, dlb_rows[layer], small["gn"][j] = _hgrn_bwd(dy, sv["o"], sv["s0"], sv["u"], lb_all[layer][None],
                                                              od_gnorm_g[j][None])
            per = D // n_in_od

            def du_t(tt_, tn):
                return pl.BlockSpec((None, tt_, tn), lambda a, b, t: (b // per, t, b % per))

            dwin = _mm_tn("od_dwin", sv["n"], du3, xs2, du_t, os3, (N_DEV, D, n_in_od), T, D, 4 * D, tt, D, n_in_od)
            tok = rs_start(f"mix{layer}", [dwin, dwout])
            dh, dhb, g_mix[layer] = _mm_nt(
                "od_in_t", du3, sv["w_in"], lambda tm, tn: pl.BlockSpec((4, tm, tn // 4), lambda i, jj, k: (0, i, 0)),
                w_whole, T, D, 4 * D, tm_k4, D, 4 * D, "rms", extra=(sv["h0"], mix_norm_g[layer][None] + tok, dh),
                parts=4)

    dmeta = dh[PAD:PAD + N_META]
    grad_x = dh[PAD + N_META:][None]
    dlb_param = _lb_bwd(lb_param, jnp.concatenate(dlb_rows, axis=0))

    pieces = [
        ("final", dg_final), ("pad", jnp.zeros((SMALL_F32_ROWS - 1, D), F32)),
        ("meta", dmeta), ("mix", jnp.concatenate(g_mix, 0)), ("mlp", jnp.concatenate(g_mlp, 0)),
        ("cw", jnp.stack([c[:CONV_WIDTH] for c in small["cw"]])), ("cb", jnp.stack([v_[0] for v_ in small["vec"]])),
        ("lng", jnp.stack([v_[1] for v_ in small["vec"]])), ("lnb", jnp.stack([v_[2] for v_ in small["vec"]])),
        ("pw", jnp.stack(small["pw"])), ("pb", jnp.stack([v_[4] for v_ in small["vec"]])),
        ("ps", jnp.stack([v_[3] for v_ in small["vec"]])), ("gn", jnp.stack([jnp.sum(g_, axis=0)[0] for g_ in small["gn"]])),
        ("lb", dlb_param),
    ]
    flat = jnp.concatenate([p.reshape(-1) for _, p in pieces])
    n_small = flat.shape[0]
    rows_small = SMALL_F32_ROWS + -(-(n_small // 1024 + 1 - SMALL_F32_ROWS) // 16) * 16
    flat = jnp.pad(flat, (0, rows_small * 1024 - n_small)).reshape(rows_small, 1024)

    sm_src = [flat[:SMALL_F32_ROWS], flat[SMALL_F32_ROWS:].astype(BF16)]
    sm_sems, sm_s, sm_l, sm_tok = _push_start("small_start", sm_src, [_landing(a_.shape, a_.dtype) for a_ in sm_src],
                                              True, [0, 0])
    recv = {}
    for tag, s_thru, l_thru, sems in rs_pending:
        got = _push_wait(f"rs_wait_{tag}", s_thru, l_thru, sems, sm_tok, False)
        layer = int(tag[3:])
        if tag.startswith("mlp"):
            recv["w1", layer], recv["w2", layer] = got
        else:
            key = "ev" if layer % 2 == 0 else "od"
            recv[key + "_in", layer // 2], recv[key + "_out", layer // 2] = got

    outs = {}
    big = {"ev_in": ("ev_w_in", ev_w_in, m_ev_w_in, v_ev_w_in), "ev_out": ("ev_w_out", ev_w_out, m_ev_w_out, v_ev_w_out),
           "od_in": ("od_w_in", od_w_in, m_od_w_in, v_od_w_in), "od_out": ("od_w_out", od_w_out, m_od_w_out, v_od_w_out),
           "w1": ("mlp_w1", mlp_w1, m_mlp_w1, v_mlp_w1), "w2": ("mlp_w2", mlp_w2, m_mlp_w2, v_mlp_w2)}
    for key, (name, w, m, v) in big.items():
        res = None
        for l in range(w.shape[0]):
            res = _adamw(recv[key, l], w, m, v, layer=l, prev=res)
        outs[name] = res

    small_params = {
        "meta": ("meta_tokens", None), "mix": ("mix_norm_g", mix_norm_g, m_mix_norm_g, v_mix_norm_g),
        "mlp": ("mlp_norm_g", mlp_norm_g, m_mlp_norm_g, v_mlp_norm_g),
        "final": ("final_norm_g", final_norm_g, m_final_norm_g, v_final_norm_g),
        "cw": ("ev_conv_w", None), "cb": ("ev_conv_b", ev_conv_b, m_ev_conv_b, v_ev_conv_b),
        "lng": ("ev_ln_g", ev_ln_g, m_ev_ln_g, v_ev_ln_g), "lnb": ("ev_ln_b", ev_ln_b, m_ev_ln_b, v_ev_ln_b),
        "pw": ("ev_pool_w", ev_pool_w, m_ev_pool_w, v_ev_pool_w), "pb": ("ev_pool_b", ev_pool_b, m_ev_pool_b, v_ev_pool_b),
        "ps": ("ev_pool_scale", ev_pool_scale, m_ev_pool_scale, v_ev_pool_scale),
        "gn": ("od_gnorm_g", od_gnorm_g, m_od_gnorm_g, v_od_gnorm_g), "lb": ("lb_param", lb_param, m_lb_param, v_lb_param),
    }
    csh = ev_conv_w.shape[2]
    msh = meta_tokens.shape[1]

    def packed(which):
        parts = []
        for key, g_ in pieces:
            ent = small_params.get(key)
            if key == "pad":
                full = g_
            elif key == "meta":
                src = (meta_tokens, m_meta_tokens, v_meta_tokens)[which]
                full = lax.dynamic_update_slice(jnp.zeros((N_META, D), F32), src, (0, dev * msh))
            elif key == "cw":
                src = (ev_conv_w, m_ev_conv_w, v_ev_conv_w)[which]
                full = lax.dynamic_update_slice(jnp.zeros(g_.shape, F32), src, (0, 0, dev * csh))
            else:
                full = ent[1 + which]
            parts.append(full.reshape(-1))
        f = jnp.concatenate(parts)
        return jnp.pad(f, (0, rows_small * 1024 - n_small)).reshape(rows_small, 1024)

    got_f32, got_bf16 = _push_wait("small_wait", sm_s, sm_l, sm_sems[0], outs["mlp_w2"][0], True)
    recv_small = jnp.concatenate([got_f32, got_bf16.astype(F32)], axis=1)
    sres = [r_[0] for r_ in _adamw(recv_small, packed(0)[None], packed(1)[None], packed(2)[None])]
    off = 0
    for key, g_ in pieces:
        size = g_.size
        vals = [r_.reshape(-1)[off:off + size].reshape(g_.shape) for r_ in sres]
        off += size
        if key == "pad":
            continue
        name = small_params[key][0]
        if key == "meta":
            vals = [lax.dynamic_slice(v_, (0, dev * msh), (N_META, msh)) for v_ in vals]
        elif key == "cw":
            vals = [lax.dynamic_slice(v_, (0, 0, dev * csh), v_.shape[:2] + (csh,)) for v_ in vals]
        else:
            vals = [v_.reshape(small_params[key][1].shape) for v_ in vals]
        outs[name] = vals

    names = ["meta_tokens", "mix_norm_g", "mlp_norm_g", "final_norm_g", "ev_w_in", "ev_conv_w", "ev_conv_b", "ev_ln_g",
             "ev_ln_b", "ev_pool_w", "ev_pool_b", "ev_pool_scale", "ev_w_out", "od_w_in", "od_gnorm_g", "od_w_out",
             "lb_param", "mlp_w1", "mlp_w2"]
    result = [loss, grad_x]
    for k in range(4):
        result += [outs[nm][k] for nm in names]
    return tuple(result)
```

```python
import functools

import jax
import jax.numpy as jnp
from jax import lax
from jax.experimental import pallas as pl
from jax.experimental.pallas import tpu as pltpu

F32 = jnp.float32
BF16 = jnp.bfloat16

N_DEV = 8
N_META = 16
CHUNK = 64
PAD = CHUNK - N_META
SUB = 16
HEAD = 128
CONV_WIDTH = 31
HALO = 32
POOL_WINDOWS = (2, 4, 8, 16)
EPS = 1e-6
NEG = -1e30
ADAM_LR, ADAM_B1, ADAM_B2, ADAM_EPS, ADAM_WD, ADAM_STEP = 0.001, 0.9, 0.999, 1e-08, 0.01, 10
VMEM_LIMIT = 56 * 1024 * 1024
EV_ROWS = 416
HGRN_ROWS = 832
MM_ROWS_BIG = 2080
MM_ROWS_MID = 1040
MM_ROWS_K4 = 416
SMALL_F32_ROWS = 8
HGRN_HEADS_FWD = 8
HGRN_HEADS_BWD = 2
MESH = pl.DeviceIdType.MESH
AXES = ("x", "y", "c")
ANY = pl.BlockSpec(memory_space=pl.ANY)


def _cp(*sem):
    return pltpu.CompilerParams(dimension_semantics=sem, vmem_limit_bytes=VMEM_LIMIT)


def _tile(n, cap, mult):
    best = None
    for d in range(mult, min(n, cap) + 1, mult):
        if n % d == 0:
            best = d
    assert best is not None, (n, cap, mult)
    return best


def _nt(a, b):
    return lax.dot_general(a, b, (((1,), (1,)), ((), ())), preferred_element_type=F32)


def _tn(a, b):
    return lax.dot_general(a, b, (((0,), (0,)), ((), ())), preferred_element_type=F32)


def _nn(a, b):
    return jnp.dot(a, b, preferred_element_type=F32)


def _r16(x):
    return x.astype(BF16).astype(F32)


def _row_ids(base, n):
    return base + lax.broadcasted_iota(jnp.int32, (n, 1), 0)


def _dsilu(x, s):
    return s * (1.0 + x * (1.0 - s))


def _loss_head(h, g, tgt):
    T, D = h.shape
    tm = _tile(T, MM_ROWS_MID, 16)
    first_x = PAD + N_META

    def body(h_ref, g_ref, t_ref, loss_ref, dh_ref, dhb_ref, dg_ref):
        i = pl.program_id(0)
        x = h_ref[...]
        r = lax.rsqrt(jnp.mean(x * x, axis=-1, keepdims=True) + EPS)
        xh = x * r
        gv = g_ref[...]
        out = xh * gv
        valid = _row_ids(i * tm, tm) >= first_x
        e = jnp.where(valid, out - t_ref[...], 0.0)
        dout = e * (1.0 / D)
        dxh = dout * gv
        dx = r * (dxh - xh * jnp.mean(dxh * xh, axis=-1, keepdims=True))
        dh_ref[...] = dx
        dhb_ref[...] = dx.astype(BF16)

        @pl.when(i == 0)
        def _():
            dg_ref[...] = jnp.zeros_like(dg_ref)
            loss_ref[...] = jnp.zeros_like(loss_ref)

        dg_ref[...] += jnp.sum(dout * xh, axis=0, keepdims=True)
        loss_ref[...] += 0.5 * jnp.sum(jnp.mean(e * e, axis=-1, keepdims=True))

    row = pl.BlockSpec((tm, D), lambda i: (i, 0))
    vec = pl.BlockSpec((1, D), lambda i: (0, 0))
    return pl.pallas_call(
        body, grid=(T // tm,),
        in_specs=[row, vec, row],
        out_specs=[pl.BlockSpec((8, 128), lambda i: (0, 0)), row, row, vec],
        out_shape=[jax.ShapeDtypeStruct((8, 128), F32), jax.ShapeDtypeStruct((T, D), F32),
                   jax.ShapeDtypeStruct((T, D), BF16), jax.ShapeDtypeStruct((1, D), F32)],
        compiler_params=_cp("arbitrary"), name="loss_head")(h, g, tgt)


def _mm_nn(name, a, w, w_spec, M, N, K, tm, tn, tk, mode, extra=None, a_spec=None):
    nk = K // tk
    if a_spec is None:
        a_spec = pl.BlockSpec((tm, tk), lambda i, j, k: (i, k))
    o_spec = pl.BlockSpec((tm, tn), lambda i, j, k: (i, j))

    def body(*refs):
        if mode == "resid":
            a_ref, w_ref, e_ref = refs[:3]
            outs = refs[3:]
        else:
            a_ref, w_ref = refs[:2]
            outs = refs[2:]
        acc_ref = outs[-1] if nk > 1 else None
        part = _nn(a_ref[...], w_ref[...])

        def finish(acc):
            if mode == "f32":
                outs[0][...] = acc
            elif mode == "relu2":
                r = jnp.maximum(acc, 0.0)
                outs[0][...] = r.astype(BF16)
                outs[1][...] = (r * r).astype(BF16)
            else:
                keep = _row_ids(pl.program_id(0) * tm, tm) >= PAD
                outs[0][...] = jnp.where(keep, e_ref[...] + acc, 0.0)

        if nk == 1:
            finish(part)
        else:
            k = pl.program_id(2)

            @pl.when(k == 0)
            def _():
                acc_ref[...] = part

            @pl.when(k > 0)
            def _():
                acc_ref[...] += part

            @pl.when(k == nk - 1)
            def _():
                finish(acc_ref[...])

    in_specs = [a_spec, w_spec(tk, tn)]
    args = [a, w]
    if mode == "resid":
        in_specs.append(o_spec)
        args.append(extra)
    if mode == "relu2":
        out_specs = [o_spec, o_spec]
        out_shape = [jax.ShapeDtypeStruct((M, N), BF16)] * 2
    else:
        out_specs = [o_spec]
        out_shape = [jax.ShapeDtypeStruct((M, N), F32)]
    scratch = [pltpu.VMEM((tm, tn), F32)] if nk > 1 else []
    res = pl.pallas_call(
        body, grid=(M // tm, N // tn, nk), in_specs=in_specs, out_specs=out_specs, out_shape=out_shape,
        scratch_shapes=scratch, compiler_params=_cp("parallel", "parallel", "arbitrary"), name=name)(*args)
    return res if mode == "relu2" else res[0]


def _mm_rms_nn(name, h, g, w, tm, tn, mode):
    M, K = h.shape
    N = w.shape[1]

    def body(h_ref, g_ref, w_ref, n_ref, *outs):
        @pl.when(pl.program_id(1) == 0)
        def _():
            ch = _tile(tm, 256, 16)

            def chunk(c, carry):
                rows = pl.ds(pl.multiple_of(c * ch, ch), ch)
                x = h_ref[rows, :]
                r = lax.rsqrt(jnp.mean(x * x, axis=-1, keepdims=True) + EPS)
                n_ref[rows, :] = ((x * r) * g_ref[...]).astype(BF16)
                return carry

            lax.fori_loop(0, tm // ch, chunk, 0)

        acc = _nn(n_ref[...], w_ref[...])
        if mode == "f32":
            outs[0][...] = acc
        else:
            r = jnp.maximum(acc, 0.0)
            outs[0][...] = r.astype(BF16)
            outs[1][...] = (r * r).astype(BF16)

    row = pl.BlockSpec((tm, K), lambda i, j: (i, 0))
    o_spec = pl.BlockSpec((tm, tn), lambda i, j: (i, j))
    n_out = 1 if mode == "f32" else 2
    return pl.pallas_call(
        body, grid=(M // tm, N // tn),
        in_specs=[row, pl.BlockSpec((1, K), lambda i, j: (0, 0)), pl.BlockSpec((K, tn), lambda i, j: (0, j))],
        out_specs=[row] + [o_spec] * n_out,
        out_shape=[jax.ShapeDtypeStruct((M, K), BF16)] + [jax.ShapeDtypeStruct((M, N), F32 if mode == "f32" else BF16)] * n_out,
        compiler_params=_cp("parallel", "arbitrary"), name=name)(h, g, w)


def _mm_nt(name, dy, w, dy_spec, w_spec, M, J, N, tm, tj, tn, mode, extra=None, parts=1):
    nk = N // tn
    o_spec = pl.BlockSpec((tm, tj), lambda i, j, k: (i, j))
    n_extra = {"f32": 0, "dact": 1, "rms": 3}[mode]
    if mode == "rms":
        assert nk == 1 and tj == J

    def body(*refs):
        dy_ref, w_ref = refs[:2]
        ex = refs[2:2 + n_extra]
        outs = refs[2 + n_extra:]
        acc_ref = outs[-1] if nk > 1 else None
        if parts == 1:
            part = _nt(dy_ref[...], w_ref[...])
        else:
            wq = tn // parts
            part = _nt(dy_ref[0], w_ref[:, 0:wq])
            for q in range(1, parts):
                part = part + _nt(dy_ref[q], w_ref[:, q * wq:(q + 1) * wq])

        def finish(acc):
            if mode == "f32":
                outs[0][...] = acc
            elif mode == "dact":
                outs[0][...] = (acc * (2.0 * ex[0][...].astype(F32))).astype(BF16)
            else:
                h_ref, g_ref, dres_ref = ex
                dh_ref, dhb_ref, dg_ref = outs[:3]
                i = pl.program_id(0)

                @pl.when(i == 0)
                def _():
                    dg_ref[...] = jnp.zeros_like(dg_ref)

                ch = _tile(tm, 256, 16)
                for c0 in range(0, tm, ch):
                    a_c = acc[c0:c0 + ch]
                    x = h_ref[c0:c0 + ch, :]
                    r = lax.rsqrt(jnp.mean(x * x, axis=-1, keepdims=True) + EPS)
                    xh = x * r
                    dxh = a_c * g_ref[...]
                    dx = r * (dxh - xh * jnp.mean(dxh * xh, axis=-1, keepdims=True))
                    keep = _row_ids(i * tm + c0, ch) >= PAD
                    dh = jnp.where(keep, dres_ref[c0:c0 + ch, :] + dx, 0.0)
                    dh_ref[c0:c0 + ch, :] = dh
                    dhb_ref[c0:c0 + ch, :] = dh.astype(BF16)
                    dg_ref[...] += jnp.sum(a_c * xh, axis=0, keepdims=True)

        if nk == 1:
            finish(part)
        else:
            k = pl.program_id(2)

            @pl.when(k == 0)
            def _():
                acc_ref[...] = part

            @pl.when(k > 0)
            def _():
                acc_ref[...] += part

            @pl.when(k == nk - 1)
            def _():
                finish(acc_ref[...])

    in_specs = [dy_spec(tm, tn), w_spec(tj, tn)]
    args = [dy, w]
    scratch = [pltpu.VMEM((tm, tj), F32)] if nk > 1 else []
    if mode == "rms":
        vec = pl.BlockSpec((1, J), lambda i, j, k: (0, 0))
        h, g, dres = extra
        res = pl.pallas_call(
            body, grid=(M // tm, 1, 1), in_specs=in_specs + [o_spec, vec, o_spec], out_specs=[o_spec, o_spec, vec],
            out_shape=[jax.ShapeDtypeStruct((M, J), F32), jax.ShapeDtypeStruct((M, J), BF16),
                       jax.ShapeDtypeStruct((1, J), F32)],
            compiler_params=_cp("arbitrary", "arbitrary", "arbitrary"), name=name)(*args, h, g, dres)
        return res
    if mode == "dact":
        in_specs.append(o_spec)
        args.append(extra)
    return pl.pallas_call(
        body, grid=(M // tm, J // tj, nk), in_specs=in_specs, out_specs=[o_spec],
        out_shape=[jax.ShapeDtypeStruct((M, J), BF16 if mode == "dact" else F32)],
        scratch_shapes=scratch, compiler_params=_cp("parallel", "parallel", "arbitrary"), name=name)(*args)[0]


def _mm_tn(name, x, dy, x_spec, dy_spec, o_spec, o_shape, T, K, N, tt, tk, tn):
    nt = T // tt

    def body(x_ref, dy_ref, o_ref, *acc):
        part = _tn(x_ref[...], dy_ref[...])
        if nt == 1:
            o_ref[...] = part.astype(BF16)
            return
        acc_ref = acc[0]
        t = pl.program_id(2)

        @pl.when(t == 0)
        def _():
            acc_ref[...] = part

        @pl.when(t > 0)
        def _():
            acc_ref[...] += part

        @pl.when(t == nt - 1)
        def _():
            o_ref[...] = acc_ref[...].astype(BF16)

    return pl.pallas_call(
        body, grid=(K // tk, N // tn, nt), in_specs=[x_spec(tt, tk), dy_spec(tt, tn)], out_specs=o_spec(tk, tn),
        out_shape=jax.ShapeDtypeStruct(o_shape, BF16), scratch_shapes=[pltpu.VMEM((tk, tn), F32)] if nt > 1 else [],
        compiler_params=_cp("parallel", "parallel", "arbitrary"), name=name)(x, dy)


def _pool_counts(base, n, w):
    pos = _row_ids(base, n) - PAD
    return jnp.clip(pos + 1, 1, w).astype(F32)


def _shifted_copies(buf, rows):
    buf[0, rows:rows + 8, :] = jnp.zeros((8, buf.shape[2]), F32)

    def blk(s, carry):
        b = pl.multiple_of(s * HALO, HALO)
        win = buf[0, pl.ds(b, HALO + 8), :]
        for r in range(1, 8):
            buf[r, pl.ds(b, HALO), :] = win[r:r + HALO]
        return carry

    lax.fori_loop(0, rows // HALO, blk, 0)


def _ev_fwd(u, cw, cb, lg, lb, pw, pb, ps):
    T = u.shape[0]
    C = 512
    tm = _tile(T, EV_ROWS, HALO)
    nsub = tm // HALO
    hb = tm // HALO

    def body(val_ref, gate_ref, pin_ref, valh_ref, gateh_ref, pinh_ref, cw_ref, cb_ref, lg_ref, lb_ref, pw_ref,
             pb_ref, ps_ref, yab_ref, yc_ref, a_ext, p_ext, d_buf):
        i = pl.program_id(0)
        nf = (i > 0).astype(F32)
        a_ext[0, 0:HALO, :] = valh_ref[...] * jax.nn.sigmoid(gateh_ref[...]) * nf
        a_ext[0, HALO:HALO + tm, :] = val_ref[...] * jax.nn.sigmoid(gate_ref[...])
        p_ext[0:HALO, :] = pinh_ref[...] * nf
        p_ext[HALO:, :] = pin_ref[...]
        _shifted_copies(a_ext, tm + HALO)

        def sub(s, carry):
            base = pl.multiple_of(s * HALO, HALO)
            acc = jnp.zeros((HALO, C), F32) + cb_ref[...]
            for j in range(CONV_WIDTH):
                off = 2 + j
                acc = acc + cw_ref[pl.ds(j, 1), :] * a_ext[off % 8, pl.ds(pl.multiple_of(base + off // 8 * 8, 8), HALO), :]
            yc_ref[pl.ds(base, HALO), :] = acc
            mu = jnp.mean(acc, axis=-1, keepdims=True)
            yc = acc - mu
            rstd = lax.rsqrt(jnp.mean(yc * yc, axis=-1, keepdims=True) + EPS)
            z = (yc * rstd) * lg_ref[...] + lb_ref[...]
            yab_ref[pl.ds(base, HALO), 0:C] = (z * jax.nn.sigmoid(z)).astype(BF16)
            pwin = p_ext[pl.ds(base, 2 * HALO), :]
            for gi, w in enumerate(POOL_WINDOWS):
                lo, hi = gi * HEAD, (gi + 1) * HEAD
                x = pwin[HALO:, lo:hi]
                tot = x
                for k in range(1, w):
                    tot = tot + pwin[HALO - k:2 * HALO - k, lo:hi]
                cnt = _pool_counts(i * tm + base, HALO, w)
                d_buf[pl.ds(base, HALO), lo:hi] = (tot / cnt - x).astype(BF16)
            return carry

        lax.fori_loop(0, nsub, sub, 0, unroll=2)
        for gi in range(len(POOL_WINDOWS)):
            lo, hi = gi * HEAD, (gi + 1) * HEAD
            y = _nn(d_buf[:, lo:hi], pw_ref[gi]) + pb_ref[:, lo:hi]
            yab_ref[:, C + lo:C + hi] = (y * ps_ref[:, lo:hi]).astype(BF16)

    def main(c):
        return pl.BlockSpec((tm, C), lambda i: (i, c))

    def halo(c):
        return pl.BlockSpec((HALO, C), lambda i: (jnp.maximum(i * hb - 1, 0), c))

    vec = pl.BlockSpec((1, C), lambda i: (0, 0))
    return pl.pallas_call(
        body, grid=(T // tm,),
        in_specs=[main(0), main(1), main(2), halo(0), halo(1), halo(2),
                  pl.BlockSpec((32, C), lambda i: (0, 0)), vec, vec, vec,
                  pl.BlockSpec((4, HEAD, HEAD), lambda i: (0, 0, 0)), vec, vec],
        out_specs=[pl.BlockSpec((tm, 2 * C), lambda i: (i, 0)), pl.BlockSpec((tm, C), lambda i: (i, 0))],
        out_shape=[jax.ShapeDtypeStruct((T, 2 * C), BF16), jax.ShapeDtypeStruct((T, C), F32)],
        scratch_shapes=[pltpu.VMEM((8, tm + HALO + 8, C), F32), pltpu.VMEM((tm + HALO, C), F32),
                        pltpu.VMEM((tm, C), BF16)],
        compiler_params=_cp("parallel"), name="ev_fwd")(u, u, u, u, u, u, cw, cb, lg, lb, pw, pb, ps)


def _ev_bwd(dyab, yc, u, cw, lg, lb, pw, pwt, pb, ps):
    T = u.shape[0]
    C = 512
    tm = _tile(T, EV_ROWS, HALO)
    nsub = tm // HALO
    hb = tm // HALO
    nblk = T // tm
    E = tm + HALO

    def body(dya_ref, dyb_ref, dyah_ref, dybh_ref, yc_ref, ych_ref, val_ref, gate_ref, pin_ref, valh_ref, gateh_ref,
             pinh_ref, cw_ref, lg_ref, lb_ref, pw_ref, pwt_ref, pb_ref, ps_ref,
             du_ref, dcw_ref, dvec_ref, dpw_ref,
             dy_ext, a_ext, p_ext, ddc_ext, dd_buf, d_buf, dpre_buf, dcw_acc, vec_acc):
        i = pl.program_id(0)
        nf = (i > 0).astype(F32)
        nl = (i < nblk - 1).astype(F32)

        @pl.when(i == 0)
        def _():
            dcw_ref[...] = jnp.zeros_like(dcw_ref)
            dvec_ref[...] = jnp.zeros_like(dvec_ref)
            dpw_ref[...] = jnp.zeros_like(dpw_ref)

        dcw_acc[...] = jnp.zeros_like(dcw_acc)
        vec_acc[...] = jnp.zeros_like(vec_acc)
        a_ext[0, 0:HALO, :] = valh_ref[...] * jax.nn.sigmoid(gateh_ref[...]) * nf
        a_ext[0, HALO:E, :] = val_ref[...] * jax.nn.sigmoid(gate_ref[...])
        p_ext[0:HALO, :] = pinh_ref[...] * nf
        p_ext[HALO:, :] = pin_ref[...]
        _shifted_copies(a_ext, E)

        def ln_bwd(y, dya, main):
            mu = jnp.mean(y, axis=-1, keepdims=True)
            ycen = y - mu
            rstd = lax.rsqrt(jnp.mean(ycen * ycen, axis=-1, keepdims=True) + EPS)
            yh = ycen * rstd
            z = yh * lg_ref[...] + lb_ref[...]
            sz = jax.nn.sigmoid(z)
            dz = dya * _dsilu(z, sz)
            dyh = dz * lg_ref[...]
            dy = rstd * (dyh - jnp.mean(dyh, axis=-1, keepdims=True) - yh * jnp.mean(dyh * yh, axis=-1, keepdims=True))
            if main:
                vec_acc[1] += jnp.sum((dz * yh).reshape(HALO // 8, 8, C), axis=0)
                vec_acc[2] += jnp.sum(dz.reshape(HALO // 8, 8, C), axis=0)
                vec_acc[0] += jnp.sum(dy.reshape(HALO // 8, 8, C), axis=0)
            return dy

        def pool_dd(dyb, base, main):
            dpre = dyb * ps_ref[...]
            for gi, w in enumerate(POOL_WINDOWS):
                lo, hi = gi * HEAD, (gi + 1) * HEAD
                dd = _nn(dpre[:, lo:hi].astype(BF16), pwt_ref[gi])
                cnt = _pool_counts(i * tm + base, HALO, w)
                ddc_ext[pl.ds(base, HALO), lo:hi] = dd / cnt
                if main:
                    dd_buf[pl.ds(base, HALO), lo:hi] = dd
            if main:
                dpre_buf[pl.ds(base, HALO), :] = dpre.astype(BF16)
                vec_acc[4] += jnp.sum(dpre.reshape(HALO // 8, 8, C), axis=0)

        def p1(s, carry):
            base = pl.multiple_of(s * HALO, HALO)
            dy_ext[0, pl.ds(base, HALO), :] = ln_bwd(yc_ref[pl.ds(base, HALO), :], dya_ref[pl.ds(base, HALO), :], True)
            pool_dd(dyb_ref[pl.ds(base, HALO), :], base, True)
            return carry

        lax.fori_loop(0, nsub, p1, 0, unroll=2)
        dy_ext[0, tm:E, :] = ln_bwd(ych_ref[...], dyah_ref[...], False) * nl
        _shifted_copies(dy_ext, E)
        dpre_h = dybh_ref[...] * ps_ref[...] * nl
        for gi, w in enumerate(POOL_WINDOWS):
            lo, hi = gi * HEAD, (gi + 1) * HEAD
            dd = _nn(dpre_h[:, lo:hi].astype(BF16), pwt_ref[gi])
            ddc_ext[tm:, lo:hi] = dd / _pool_counts(i * tm + tm, HALO, w)

        def p2(s, carry):
            base = pl.multiple_of(s * HALO, HALO)
            dy_m = dy_ext[0, pl.ds(base, HALO), :]
            da = jnp.zeros((HALO, C), F32)
            for j in range(CONV_WIDTH):
                sh = CONV_WIDTH - 1 - j
                off = 2 + j
                da = da + cw_ref[pl.ds(j, 1), :] * dy_ext[sh % 8, pl.ds(pl.multiple_of(base + sh // 8 * 8, 8), HALO), :]
                a_j = a_ext[off % 8, pl.ds(pl.multiple_of(base + off // 8 * 8, 8), HALO), :]
                dcw_acc[j] += jnp.sum((dy_m * a_j).reshape(HALO // 8, 8, C), axis=0)
            v = val_ref[pl.ds(base, HALO), :]
            g = gate_ref[pl.ds(base, HALO), :]
            sg = jax.nn.sigmoid(g)
            du_ref[pl.ds(base, HALO), 0:C] = (da * sg).astype(BF16)
            du_ref[pl.ds(base, HALO), C:2 * C] = (da * v * sg * (1.0 - sg)).astype(BF16)
            pwin = p_ext[pl.ds(base, 2 * HALO), :]
            cwin = ddc_ext[pl.ds(base, 2 * HALO), :]
            for gi, w in enumerate(POOL_WINDOWS):
                lo, hi = gi * HEAD, (gi + 1) * HEAD
                x = pwin[HALO:, lo:hi]
                tot = x
                back = cwin[0:HALO, lo:hi]
                for k in range(1, w):
                    tot = tot + pwin[HALO - k:2 * HALO - k, lo:hi]
                    back = back + cwin[k:k + HALO, lo:hi]
                cnt = _pool_counts(i * tm + base, HALO, w)
                d_buf[pl.ds(base, HALO), lo:hi] = (tot / cnt - x).astype(BF16)
                du_ref[pl.ds(base, HALO), 2 * C + lo:2 * C + hi] = (back - dd_buf[pl.ds(base, HALO), lo:hi]).astype(BF16)
            return carry

        lax.fori_loop(0, nsub, p2, 0)
        for gi in range(len(POOL_WINDOWS)):
            lo, hi = gi * HEAD, (gi + 1) * HEAD
            pre = _nn(d_buf[:, lo:hi], pw_ref[gi]) + pb_ref[:, lo:hi]
            vec_acc[3, :, lo:hi] += jnp.sum((dyb_ref[:, lo:hi] * pre).reshape(tm // 8, 8, HEAD), axis=0)
            dpw_ref[gi] += _tn(d_buf[:, lo:hi], dpre_buf[:, lo:hi])
        for j in range(CONV_WIDTH):
            dcw_ref[pl.ds(j, 1), :] += jnp.sum(dcw_acc[j], axis=0, keepdims=True)
        for r in range(5):
            dvec_ref[pl.ds(r, 1), :] += jnp.sum(vec_acc[r], axis=0, keepdims=True)

    def main(c, width=C):
        return pl.BlockSpec((tm, width), lambda i: (i, c))

    def prev(c):
        return pl.BlockSpec((HALO, C), lambda i: (jnp.maximum(i * hb - 1, 0), c))

    def nxt(c):
        return pl.BlockSpec((HALO, C), lambda i: (jnp.minimum((i + 1) * hb, T // HALO - 1), c))

    vec = pl.BlockSpec((1, C), lambda i: (0, 0))
    mat = pl.BlockSpec((4, HEAD, HEAD), lambda i: (0, 0, 0))
    return pl.pallas_call(
        body, grid=(nblk,),
        in_specs=[main(0), main(1), nxt(0), nxt(1), main(0), nxt(0), main(0), main(1), main(2), prev(0), prev(1),
                  prev(2), pl.BlockSpec((32, C), lambda i: (0, 0)), vec, vec, mat, mat, vec, vec],
        out_specs=[pl.BlockSpec((tm, 3 * C), lambda i: (i, 0)), pl.BlockSpec((32, C), lambda i: (0, 0)),
                   pl.BlockSpec((8, C), lambda i: (0, 0)), mat],
        out_shape=[jax.ShapeDtypeStruct((T, 3 * C), BF16), jax.ShapeDtypeStruct((32, C), F32),
                   jax.ShapeDtypeStruct((8, C), F32), jax.ShapeDtypeStruct((4, HEAD, HEAD), F32)],
        scratch_shapes=[pltpu.VMEM((8, E + 8, C), F32), pltpu.VMEM((8, E + 8, C), F32), pltpu.VMEM((E, C), F32),
                        pltpu.VMEM((E, C), F32), pltpu.VMEM((tm, C), F32), pltpu.VMEM((tm, C), BF16),
                        pltpu.VMEM((tm, C), BF16), pltpu.VMEM((32, 8, C), F32), pltpu.VMEM((8, 8, C), F32)],
        compiler_params=_cp("arbitrary"), name="ev_bwd")(
            dyab, dyab, dyab, dyab, yc, yc, u, u, u, u, u, u, cw, lg, lb, pw, pwt, pb, ps)


def _cumsum_rows(x, reverse=False):
    n = x.shape[0]
    rid = lax.broadcasted_iota(jnp.int32, (n, 1), 0)
    k = 1
    while k < n:
        if reverse:
            sh = jnp.where(rid < n - k, pltpu.roll(x, n - k, 0), 0.0)
        else:
            sh = jnp.where(rid >= k, pltpu.roll(x, k, 0), 0.0)
        x = x + sh
        k *= 2
    return x


def _hgrn_gates(qr, fr, lbv):
    sq = jax.nn.sigmoid(qr)
    sg = jax.nn.sigmoid(fr)
    fg = lbv + (1.0 - lbv) * sg
    return qr * sq, sq, sg, fg, 1.0 - fg, jnp.log(fg)


def _hgrn_fwd(u, lbv, gn):
    T = u.shape[0]
    H = 8
    RB = _tile(T, HGRN_ROWS, CHUNK)
    NC = RB // CHUNK
    NS = CHUNK // SUB

    HP = HGRN_HEADS_FWD
    W = HP * HEAD

    def body(q_ref, f_ref, i_ref, g_ref, lb_ref, gn_ref, y_ref, o_ref, s0_ref, st, qs, ks, bs, vs, os_):
        rb = pl.program_id(1)

        @pl.when(rb == 0)
        def _():
            st[...] = jnp.zeros_like(st)

        t8 = lax.broadcasted_iota(jnp.int32, (8, 1), 0)

        def head(hh, c, rows):
            sl = slice(hh * HEAD, (hh + 1) * HEAD)
            q, _, _, _, kk, lf = _hgrn_gates(q_ref[rows, sl], f_ref[rows, sl], lb_ref[:, sl])
            v = i_ref[rows, sl]
            b = _cumsum_rows(lf)
            qs[hh] = q
            ks[hh] = kk
            bs[hh] = b
            vs[hh] = v
            st0 = st[hh]
            s0_ref[hh, c] = st0
            os_[hh] = _nt((q * jnp.exp(b)).astype(BF16), st0.astype(BF16))
            for I in range(NS):
                lo = I * SUB
                qI = qs[hh, lo:lo + SUB, :]
                bI = bs[hh, lo:lo + SUB, :]
                oI = jnp.zeros((SUB, HEAD), F32)
                if I > 0:
                    bprev = bs[hh, pl.ds(lo - 1, 1), :]
                    qt = _r16(qI * jnp.exp(bI - bprev))
                    kt = _r16(ks[hh, 0:lo, :] * jnp.exp(bprev - bs[hh, 0:lo, :]))
                    A = _nt(qt, kt)
                    oI = oI + _nn(_r16(A), _r16(vs[hh, 0:lo, :]))
                od = [jnp.zeros((8, HEAD), F32) for _ in range(SUB // 8)]
                for s in range(SUB):
                    row = pl.ds(lo + s, 1)
                    brow, krow, vrow = bs[hh, row, :], ks[hh, row, :], vs[hh, row, :]
                    for ti in range(SUB // 8):
                        o8 = 8 * ti
                        if s > o8 + 7:
                            continue
                        d = bI[o8:o8 + 8] - brow
                        if s > o8:
                            d = jnp.where(t8 >= s - o8, d, NEG)
                        col = jnp.sum(qI[o8:o8 + 8] * jnp.exp(d) * krow, axis=1, keepdims=True)
                        od[ti] = od[ti] + col * vrow
                os_[hh, lo:lo + SUB, :] += oI + jnp.concatenate(od, axis=0)
            blast = bs[hh, pl.ds(CHUNK - 1, 1), :]
            kh = kk * jnp.exp(blast - b)
            st[hh] = st0 * jnp.exp(blast) + _tn(v.astype(BF16), kh.astype(BF16))
            o = os_[hh]
            o_ref[rows, sl] = o
            rr = lax.rsqrt(jnp.mean(o * o, axis=-1, keepdims=True) + EPS)
            gr = g_ref[rows, sl]
            y_ref[rows, sl] = (((o * rr) * gn_ref[...]) * (gr * jax.nn.sigmoid(gr))).astype(BF16)

        def chunk(c, carry):
            rows = pl.ds(pl.multiple_of(c * CHUNK, CHUNK), CHUNK)
            for hh in range(HP):
                head(hh, c, rows)
            return carry

        lax.fori_loop(0, NC, chunk, 0)

    def blk(q):
        return pl.BlockSpec((RB, W), lambda h, r: (r, q * (H // HP) + h))

    sc = lambda: pltpu.VMEM((HP, CHUNK, HEAD), F32)
    return pl.pallas_call(
        body, grid=(H // HP, T // RB),
        in_specs=[blk(0), blk(1), blk(2), blk(3), pl.BlockSpec((1, W), lambda h, r: (0, h)),
                  pl.BlockSpec((1, HEAD), lambda h, r: (0, 0))],
        out_specs=[pl.BlockSpec((RB, W), lambda h, r: (r, h)), pl.BlockSpec((RB, W), lambda h, r: (r, h)),
                   pl.BlockSpec((HP, NC, HEAD, HEAD), lambda h, r: (h, r, 0, 0))],
        out_shape=[jax.ShapeDtypeStruct((T, H * HEAD), BF16), jax.ShapeDtypeStruct((T, H * HEAD), F32),
                   jax.ShapeDtypeStruct((H, T // CHUNK, HEAD, HEAD), F32)],
        scratch_shapes=[pltpu.VMEM((HP, HEAD, HEAD), F32), sc(), sc(), sc(), sc(), sc()],
        compiler_params=_cp("parallel", "arbitrary"), name="hgrn_fwd")(u, u, u, u, lbv, gn)


def _hgrn_bwd(dy, o, s0, u, lbv, gn):
    T = u.shape[0]
    H = 8
    RB = _tile(T, HGRN_ROWS, CHUNK)
    NB = T // RB
    NC = RB // CHUNK
    NS = CHUNK // SUB

    def body(q_ref, f_ref, i_ref, g_ref, lb_ref, gn_ref, o_ref, dy_ref, s0_ref, du_ref, dlb_ref, dgn_ref,
             dst, qs, ks, bs, vs, dos, dqs, dks, dki, dvs, dbs):
        rb = pl.program_id(1)

        @pl.when(rb == 0)
        def _():
            dst[...] = jnp.zeros_like(dst)
            dlb_ref[...] = jnp.zeros_like(dlb_ref)
            dgn_ref[...] = jnp.zeros_like(dgn_ref)

        t8 = lax.broadcasted_iota(jnp.int32, (8, 1), 0)
        lane = lax.broadcasted_iota(jnp.int32, (8, HEAD), 1)
        gnv = gn_ref[...]

        def head(hh, c, rows):
            sl = slice(hh * HEAD, (hh + 1) * HEAD)
            lbv_ = lb_ref[:, sl]
            qr = q_ref[rows, sl]
            q, sq, sg, fg, kk, lf = _hgrn_gates(qr, f_ref[rows, sl], lbv_)
            v = i_ref[rows, sl]
            gr = g_ref[rows, sl]
            b = _cumsum_rows(lf)
            eb = jnp.exp(b)
            ov = o_ref[rows, sl]
            dyv = dy_ref[rows, sl]
            rr = lax.rsqrt(jnp.mean(ov * ov, axis=-1, keepdims=True) + EPS)
            oh = ov * rr
            gs = jax.nn.sigmoid(gr)
            dgr = dyv * (oh * gnv) * _dsilu(gr, gs)
            dnrm = dyv * (gr * gs)
            dgn_ref[hh] += jnp.sum(dnrm * oh, axis=0, keepdims=True)
            t1 = dnrm * gnv
            do = rr * (t1 - oh * jnp.mean(t1 * oh, axis=-1, keepdims=True))
            qs[hh] = q
            ks[hh] = kk
            bs[hh] = b
            vs[hh] = v
            dos[hh] = do
            st0 = s0_ref[hh, c]
            dS = dst[hh]
            do_b = do.astype(BF16)
            blast = bs[hh, pl.ds(CHUNK - 1, 1), :]
            elast = jnp.exp(blast - b)
            dq_inter = _nn(do_b, st0.astype(BF16)) * eb
            dqs[hh] = dq_inter
            dbs[hh] = q * dq_inter
            kh = kk * elast
            dvs[hh] = _nt(kh.astype(BF16), dS.astype(BF16))
            dk_inter = _nn(v.astype(BF16), dS.astype(BF16)) * elast
            dki[hh] = dk_inter
            dks[hh] = jnp.zeros((CHUNK, HEAD), F32)
            for I in range(NS):
                lo = I * SUB
                qI = qs[hh, lo:lo + SUB, :]
                bI = bs[hh, lo:lo + SUB, :]
                doI = dos[hh, lo:lo + SUB, :]
                dqI = jnp.zeros((SUB, HEAD), F32)
                dbI = jnp.zeros((SUB, HEAD), F32)
                if I > 0:
                    bprev = bs[hh, pl.ds(lo - 1, 1), :]
                    eq = jnp.exp(bI - bprev)
                    ek = jnp.exp(bprev - bs[hh, 0:lo, :])
                    qt = _r16(qI * eq)
                    kt = _r16(ks[hh, 0:lo, :] * ek)
                    A = _r16(_nt(qt, kt))
                    doI_b = _r16(doI)
                    dA = _r16(_nt(doI_b, _r16(vs[hh, 0:lo, :])))
                    dvs[hh, 0:lo, :] += _tn(A, doI_b)
                    dqt = _nn(dA, kt)
                    dkt = _tn(dA, qt)
                    dqI = dqI + dqt * eq
                    dbI = dbI + qt.astype(F32) * dqt
                    dks[hh, 0:lo, :] += dkt * ek
                    dbs[hh, 0:lo, :] -= kt.astype(F32) * dkt
                dq_t = [jnp.zeros((8, HEAD), F32) for _ in range(SUB // 8)]
                a_t = [jnp.zeros((8, HEAD), F32) for _ in range(SUB // 8)]
                for s in range(SUB):
                    row = pl.ds(lo + s, 1)
                    brow, krow, vrow = bs[hh, row, :], ks[hh, row, :], vs[hh, row, :]
                    dk_s = None
                    for ti in range(SUB // 8):
                        o8 = 8 * ti
                        if s > o8 + 7:
                            continue
                        d = bI[o8:o8 + 8] - brow
                        if s > o8:
                            d = jnp.where(t8 >= s - o8, d, NEG)
                        Es = jnp.exp(d)
                        qE = qI[o8:o8 + 8] * Es
                        col = jnp.sum(qE * krow, axis=1, keepdims=True)
                        a_t[ti] = jnp.where(lane == s, col, a_t[ti])
                        dcol = jnp.sum(doI[o8:o8 + 8] * vrow, axis=1, keepdims=True)
                        dq_t[ti] = dq_t[ti] + (dcol * Es) * krow
                        part = jnp.sum(dcol * qE, axis=0, keepdims=True)
                        dk_s = part if dk_s is None else dk_s + part
                    dks[hh, row, :] += dk_s
                    dbs[hh, row, :] -= krow * dk_s
                a_d = jnp.concatenate(a_t, axis=0)
                dq_d = jnp.concatenate(dq_t, axis=0)
                dvs[hh, lo:lo + SUB, :] += _tn(a_d, doI)[0:SUB]
                dqI = dqI + dq_d
                dbI = dbI + qI * dq_d
                dqs[hh, lo:lo + SUB, :] += dqI
                dbs[hh, lo:lo + SUB, :] += dbI
            kdk = kk * dki[hh]
            excl = _cumsum_rows(kdk) - kdk
            suff = _cumsum_rows(dbs[hh], reverse=True)
            gdec = jnp.sum(dS * st0, axis=0, keepdims=True) * jnp.exp(blast)
            dlf = suff + excl + gdec
            dk = dks[hh] + dki[hh]
            dfg = dlf / fg - dk
            dlb_ref[:, sl] += jnp.sum(dfg * (1.0 - sg), axis=0, keepdims=True)
            du_ref[0, rows, sl] = (dqs[hh] * _dsilu(qr, sq)).astype(BF16)
            du_ref[1, rows, sl] = (dfg * (1.0 - lbv_) * sg * (1.0 - sg)).astype(BF16)
            du_ref[2, rows, sl] = dvs[hh].astype(BF16)
            du_ref[3, rows, sl] = dgr.astype(BF16)
            dst[hh] = dS * jnp.exp(blast) + _tn(do_b, (q * eb).astype(BF16))

        def chunk(cc, carry):
            c = NC - 1 - cc
            rows = pl.ds(pl.multiple_of(c * CHUNK, CHUNK), CHUNK)
            for hh in range(HP):
                head(hh, c, rows)
            return carry

        lax.fori_loop(0, NC, chunk, 0)

    HP = HGRN_HEADS_BWD
    W = HP * HEAD

    def blk(qd):
        return pl.BlockSpec((RB, W), lambda h, r: (NB - 1 - r, qd * (H // HP) + h))

    hblk = pl.BlockSpec((RB, W), lambda h, r: (NB - 1 - r, h))
    sc = lambda: pltpu.VMEM((HP, CHUNK, HEAD), F32)
    return pl.pallas_call(
        body, grid=(H // HP, NB),
        in_specs=[blk(0), blk(1), blk(2), blk(3), pl.BlockSpec((1, W), lambda h, r: (0, h)),
                  pl.BlockSpec((1, HEAD), lambda h, r: (0, 0)), hblk, hblk,
                  pl.BlockSpec((HP, NC, HEAD, HEAD), lambda h, r: (h, NB - 1 - r, 0, 0))],
        out_specs=[pl.BlockSpec((4, RB, W), lambda h, r: (0, NB - 1 - r, h)),
                   pl.BlockSpec((1, W), lambda h, r: (0, h)), pl.BlockSpec((HP, 1, HEAD), lambda h, r: (h, 0, 0))],
        out_shape=[jax.ShapeDtypeStruct((4, T, H * HEAD), BF16), jax.ShapeDtypeStruct((1, H * HEAD), F32),
                   jax.ShapeDtypeStruct((H, 1, HEAD), F32)],
        scratch_shapes=[pltpu.VMEM((HP, HEAD, HEAD), F32)] + [sc() for _ in range(10)],
        compiler_params=_cp("parallel", "arbitrary"), name="hgrn_bwd")(u, u, u, u, lbv, gn, o, dy, s0)


def _softmax_rows(p_ref, L):
    rows = [p_ref[pl.ds(l, 1), :] for l in range(L)]
    m = rows[0]
    for r in rows[1:]:
        m = jnp.maximum(m, r)
    e = [jnp.exp(r - m) for r in rows]
    tot = e[0]
    for t in e[1:]:
        tot = tot + t
    return [t / tot for t in e]


def _lb_fwd(lbp):
    L, D = lbp.shape

    def body(p_ref, o_ref):
        sm = _softmax_rows(p_ref, L)
        acc = jnp.zeros((1, D), F32)
        o_ref[pl.ds(0, 1), :] = acc
        for l in range(1, L):
            acc = acc + sm[l]
            o_ref[pl.ds(l, 1), :] = acc

    return pl.pallas_call(body, out_shape=jax.ShapeDtypeStruct((L, D), F32), name="lb_fwd")(lbp)


def _lb_bwd(lbp, dlb):
    L, D = lbp.shape

    def body(p_ref, d_ref, o_ref):
        sm = _softmax_rows(p_ref, L)
        dsm = [jnp.zeros((1, D), F32)]
        for i in range(1, L):
            t = jnp.zeros((1, D), F32)
            for l in range(i, L):
                t = t + d_ref[pl.ds(l, 1), :]
            dsm.append(t)
        dot = jnp.zeros((1, D), F32)
        for i in range(L):
            dot = dot + dsm[i] * sm[i]
        for i in range(L):
            o_ref[pl.ds(i, 1), :] = sm[i] * (dsm[i] - dot)

    return pl.pallas_call(body, out_shape=jax.ShapeDtypeStruct((L, D), F32), name="lb_bwd")(lbp, dlb)


def _my_pos():
    return lax.axis_index("x"), lax.axis_index("y"), lax.axis_index("c")


def _peer(mask):
    x, y, c = _my_pos()
    mx, my, mc = (mask >> 2) & 1, (mask >> 1) & 1, mask & 1
    px = (1 - x) if mx else x
    py = (1 - y) if my else y
    pc = (1 - c) if mc else c
    return (px, py, pc), 4 * px + 2 * py + pc


HBM_SPEC =pl.BlockSpec(memory_space=pltpu.HBM)
SEM_SPEC = pl.BlockSpec(memory_space=pltpu.SEMAPHORE)
EFFECT = pltpu.SideEffectType.DATAFLOW_SIDE_EFFECTING


def _hbm(a):
    return pltpu.with_memory_space_constraint(a, pltpu.HBM)


def _landing(block_shape, dtype, axis=0):
    if axis == 0:
        return lax.empty((N_DEV,) + tuple(block_shape), dtype)
    rows, n = block_shape
    return lax.empty((rows, N_DEV * n), dtype)


def _slot(ref, i):
    if len(ref.shape) == 2:
        n = ref.shape[1] // N_DEV
        return ref.at[:, pl.ds(i * n, n)]
    return ref.at[i]


def _push_start(name, srcs, lands, whole, groups):
    n = len(srcs)
    ng = 1 + max(groups)
    cnt = [groups.count(g) for g in range(ng)]
    idx = [groups[:a].count(groups[a]) for a in range(n)]

    def body(*refs):
        src_refs, land_refs = refs[:n], refs[n:2 * n]
        sems = refs[2 * n:2 * n + 3 * ng]
        token = refs[-1]
        x, y, c = _my_pos()
        me = 4 * x + 2 * y + c
        for a in range(n):
            g = groups[a]
            for m in range(1, N_DEV):
                peer, pid = _peer(m)
                pltpu.make_async_remote_copy(
                    src_ref=src_refs[a] if whole else src_refs[a].at[pid], dst_ref=_slot(land_refs[a], me),
                    send_sem=sems[3 * g].at[idx[a] * (N_DEV - 1) + m - 1],
                    recv_sem=sems[3 * g + 1].at[idx[a] * (N_DEV - 1) + m - 1],
                    device_id=peer, device_id_type=MESH).start()
            pltpu.make_async_copy(src_refs[a] if whole else src_refs[a].at[me], _slot(land_refs[a], me),
                                  sems[3 * g + 2].at[idx[a]]).start()
        token[...] = jnp.zeros_like(token)

    sem_shapes = []
    for g in range(ng):
        sem_shapes += [pltpu.SemaphoreType.DMA((cnt[g] * (N_DEV - 1),))] * 2 + [pltpu.SemaphoreType.DMA((cnt[g],))]
    thru = [pltpu.HBM(s.shape, s.dtype) for s in list(srcs) + list(lands)]
    res = pl.pallas_call(
        body, name=name,
        out_shape=tuple(sem_shapes + thru + [jax.ShapeDtypeStruct((8, 128), F32)]),
        in_specs=tuple([HBM_SPEC] * (2 * n)),
        out_specs=tuple([SEM_SPEC] * (3 * ng) + [HBM_SPEC] * (2 * n) + [pl.BlockSpec(memory_space=pltpu.VMEM)]),
        input_output_aliases={i: 3 * ng + i for i in range(2 * n)},
        compiler_params=pltpu.CompilerParams(has_side_effects=EFFECT),
    )(*[_hbm(s) for s in srcs], *[_hbm(z) for z in lands])
    sems = [(res[3 * g], res[3 * g + 1], res[3 * g + 2]) for g in range(ng)]
    srcs_thru = list(res[3 * ng:3 * ng + n])
    lands_thru = list(res[3 * ng + n:3 * ng + 2 * n])
    return sems, srcs_thru, lands_thru, res[-1]


def _push_wait(name, srcs_thru, lands_thru, sems, after, whole):
    n = len(srcs_thru)

    def body(*refs):
        src_refs, land_refs = refs[:n], refs[n:2 * n]
        send_sems, recv_sems, own_sems = refs[2 * n], refs[2 * n + 1], refs[2 * n + 2]
        x, y, c = _my_pos()
        me = 4 * x + 2 * y + c
        for a in range(n):
            pltpu.make_async_copy(src_refs[a] if whole else src_refs[a].at[me], _slot(land_refs[a], me),
                                  own_sems.at[a]).wait()
            for m in range(1, N_DEV):
                peer, pid = _peer(m)
                cp = pltpu.make_async_remote_copy(
                    src_ref=src_refs[a] if whole else src_refs[a].at[pid], dst_ref=_slot(land_refs[a], pid),
                    send_sem=send_sems.at[a * (N_DEV - 1) + m - 1], recv_sem=recv_sems.at[a * (N_DEV - 1) + m - 1],
                    device_id=peer, device_id_type=MESH)
                cp.wait_send()
                cp.wait_recv()

    thru = [pltpu.HBM(s.shape, s.dtype) for s in list(srcs_thru) + list(lands_thru)]
    res = pl.pallas_call(
        body, name=name, out_shape=tuple(thru),
        in_specs=tuple([HBM_SPEC] * (2 * n) + [SEM_SPEC, SEM_SPEC, SEM_SPEC, ANY]),
        out_specs=tuple([HBM_SPEC] * (2 * n)),
        input_output_aliases={i: i for i in range(2 * n)},
        compiler_params=pltpu.CompilerParams(has_side_effects=EFFECT),
    )(*srcs_thru, *lands_thru, sems[0], sems[1], sems[2], after)
    return list(res[n:])


def _adamw(recv, w, m, v, layer=0, prev=None):
    L, R, C = w.shape
    tr = _tile(R, max(8, (1 << 18) // C), 8) if R % 8 == 0 else R
    bc1 = 1.0 - ADAM_B1 ** ADAM_STEP
    bc2 = 1.0 - ADAM_B2 ** ADAM_STEP
    if prev is None:
        prev = [lax.empty((L, R, C), F32) for _ in range(4)]

    def body(r_ref, w_ref, m_ref, v_ref, p0, p1, p2, p3, g_ref, d_ref, nm_ref, nv_ref):
        g = r_ref[0].astype(F32)
        for s in range(1, N_DEV):
            g = g + r_ref[s].astype(F32)
        nm = ADAM_B1 * m_ref[...] + (1.0 - ADAM_B1) * g
        nv = ADAM_B2 * v_ref[...] + (1.0 - ADAM_B2) * (g * g)
        mh = nm / bc1
        vh = nv / bc2
        g_ref[...] = g
        d_ref[...] = -ADAM_LR * (mh / (jnp.sqrt(vh) + ADAM_EPS) + ADAM_WD * w_ref[...])
        nm_ref[...] = nm
        nv_ref[...] = nv

    row = pl.BlockSpec((None, tr, C), lambda i: (layer, i, 0))
    return pl.pallas_call(
        body, grid=(R // tr,),
        in_specs=[pl.BlockSpec((N_DEV, tr, C), lambda i: (0, i, 0)), row, row, row] + [ANY] * 4,
        out_specs=[row] * 4, out_shape=[jax.ShapeDtypeStruct((L, R, C), F32)] * 4,
        input_output_aliases={4: 0, 5: 1, 6: 2, 7: 3},
        compiler_params=_cp("parallel"), name="adamw")(recv, w, m, v, *prev)


def _full_w_spec(tk, tn):
    return pl.BlockSpec((tk, tn), lambda i, j, k: (k, j))


def kernel(x, meta_tokens, mix_norm_g, mlp_norm_g, final_norm_g, ev_w_in, ev_conv_w, ev_conv_b, ev_ln_g, ev_ln_b, ev_pool_w, ev_pool_b, ev_pool_scale, ev_w_out, od_w_in, od_gnorm_g, od_w_out, lb_param, mlp_w1, mlp_w2, loss_target, m_meta_tokens, m_mix_norm_g, m_mlp_norm_g, m_final_norm_g, m_ev_w_in, m_ev_conv_w, m_ev_conv_b, m_ev_ln_g, m_ev_ln_b, m_ev_pool_w, m_ev_pool_b, m_ev_pool_scale, m_ev_w_out, m_od_w_in, m_od_gnorm_g, m_od_w_out, m_lb_param, m_mlp_w1, m_mlp_w2, v_meta_tokens, v_mix_norm_g, v_mlp_norm_g, v_final_norm_g, v_ev_w_in, v_ev_conv_w, v_ev_conv_b, v_ev_ln_g, v_ev_ln_b, v_ev_pool_w, v_ev_pool_b, v_ev_pool_scale, v_ev_w_out, v_od_w_in, v_od_gnorm_g, v_od_w_out, v_lb_param, v_mlp_w1, v_mlp_w2):
    S, D = x.shape[1], x.shape[2]
    T = PAD + N_META + S
    DEPTH = mix_norm_g.shape[0]
    DFF = mlp_w1.shape[2] * N_DEV
    dev = 4 * lax.axis_index("x") + 2 * lax.axis_index("y") + lax.axis_index("c")

    n_ev = ev_w_in.shape[0]
    n_od = od_w_in.shape[0]
    n_in_od = od_w_in.shape[2]
    n_w1 = mlp_w1.shape[2]

    ag_src, ag_grp, ag_axis, ag_at = [], [], [], {}
    for key, arr in (("meta", meta_tokens), ("cw", ev_conv_w)):
        ag_at["small", key] = len(ag_src)
        ag_src.append(arr)
        ag_grp.append(len(ag_grp))
        ag_axis.append(0)
    for layer in range(DEPTH):
        j = layer // 2
        mixer = [("in", ev_w_in[j]), ("out", ev_w_out[j])] if layer % 2 == 0 else [("in", od_w_in[j]), ("out", od_w_out[j])]
        for pos, (key, arr) in enumerate(mixer + [("w1", mlp_w1[layer]), ("w2", mlp_w2[layer])]):
            ag_at[layer, key] = len(ag_src)
            ag_src.append(arr.astype(BF16))
            ag_grp.append(len(ag_grp))
            ag_axis.append(1 if key in ("in", "w1") and arr.shape[1] % 128 == 0 else 0)
    ag_sems, ag_s, ag_l, ag_tok = _push_start(
        "ag_start", ag_src, [_landing(s_.shape, s_.dtype, ax) for s_, ax in zip(ag_src, ag_axis)], True, ag_grp)

    def ag_wait(layer, key, after):
        a = ag_at[layer, key]
        return _push_wait(f"ag_wait_{a}", [ag_s[a]], [ag_l[a]], ag_sems[a], after, True)[0]

    g_meta = ag_wait("small", "meta", ag_tok)
    g_cw = ag_wait("small", "cw", ag_tok)
    meta_full = jnp.transpose(g_meta, (1, 0, 2)).reshape(N_META, D)
    cw_full = jnp.transpose(g_cw, (1, 2, 0, 3)).reshape(n_ev, CONV_WIDTH, -1)
    cw_pad = jnp.pad(cw_full, ((0, 0), (0, 32 - CONV_WIDTH), (0, 0)))

    h = jnp.concatenate([jnp.zeros((PAD, D), F32), meta_full, x[0]], axis=0) + ag_tok[0, 0]
    tgt = jnp.pad(loss_target[0], ((PAD + N_META, 0), (0, 0)))
    lb_all = _lb_fwd(lb_param)
    h, tgt, lb_all = lax.optimization_barrier((h, tgt, lb_all))

    tm_big = _tile(T, MM_ROWS_BIG, 16)
    tm_mid = _tile(T, MM_ROWS_MID, 16)
    tm_k4 = _tile(T, MM_ROWS_K4, 16)

    saved = []
    for layer in range(DEPTH):
        j = layer // 2
        sv = {"h0": h}
        g_in = ag_wait(layer, "in", h)
        w_in = g_in if g_in.ndim == 2 else jnp.transpose(g_in, (1, 0, 2)).reshape(D, -1)
        if layer % 2 == 0:
            sv["n"], u = _mm_rms_nn("ev_in", h, mix_norm_g[layer][None], w_in, tm_big, 512, "f32")
            yab, yc = _ev_fwd(u, cw_pad[j], ev_conv_b[j][None], ev_ln_g[j][None], ev_ln_b[j][None],
                              ev_pool_w[j].astype(BF16), ev_pool_b[j].reshape(1, -1), ev_pool_scale[j][None])
            sv.update(u=u, y=yab, yc=yc)
            w_out = ag_wait(layer, "out", yab).reshape(-1, D)
            h = _mm_nn("ev_out", yab, w_out, _full_w_spec, T, D, D, tm_mid, D, D, "resid", extra=h)
        else:
            sv["n"], u = _mm_rms_nn("od_in", h, mix_norm_g[layer][None], w_in, tm_big, 512, "f32")
            y, o, s0 = _hgrn_fwd(u, lb_all[layer][None], od_gnorm_g[j][None])
            sv.update(u=u, y=y, o=o, s0=s0)
            w_out = ag_wait(layer, "out", y).reshape(-1, D)
            h = _mm_nn("od_out", y, w_out, _full_w_spec, T, D, D, tm_mid, D, D, "resid", extra=h)
        sv["h1"] = h
        w_w1 = ag_wait(layer, "w1", h)
        n2, r, act = _mm_rms_nn("mlp_w1", h, mlp_norm_g[layer][None], w_w1, tm_big, 512, "relu2")
        w_w2 = ag_wait(layer, "w2", act).reshape(DFF, D)
        sv.update(w_in=w_in, w_out=w_out, w_w1=w_w1, w_w2=w_w2)
        sv.update(n2=n2, r=r, act=act)
        h = _mm_nn("mlp_w2", act, w_w2, _full_w_spec, T, D, DFF, tm_k4, D, DFF, "resid", extra=h)
        saved.append(sv)

    loss_blk, dh, dhb, dg_final = _loss_head(h, final_norm_g[None], tgt)
    loss = lax.psum(loss_blk[0, 0], AXES)

    tt = T
    g_mix, g_mlp = [None] * DEPTH, [None] * DEPTH
    small ={"cw": [None] * n_ev, "vec": [None] * n_ev, "pw": [None] * n_ev, "gn": [None] * n_od}
    dlb_rows = [jnp.zeros((1, D), F32) for _ in range(DEPTH)]

    def xs2(tt_, tk):
        return pl.BlockSpec((tt_, tk), lambda a, b, t: (t, a))

    def ys2(tt_, tn):
        return pl.BlockSpec((tt_, tn), lambda a, b, t: (t, b))

    def os2(tk, tn):
        return pl.BlockSpec((tk, tn), lambda a, b, t: (a, b))

    def os3(tk, tn):
        return pl.BlockSpec((None, tk, tn), lambda a, b, t: (b, a, 0))

    def dy2(tm, tn):
        return pl.BlockSpec((tm, tn), lambda i, jj, k: (i, k))

    def w_rows(tj, tn):
        return pl.BlockSpec((tj, tn), lambda i, jj, k: (jj, k))

    rs_pending = []

    def rs_start(tag, mats):
        blocks = [m_ if m_.ndim == 3 else m_.reshape(N_DEV, m_.shape[0] // N_DEV, m_.shape[1]) for m_ in mats]
        lands = [_landing(b_.shape[1:], b_.dtype) for b_ in blocks]
        sems, s_thru, l_thru, tok = _push_start(f"rs_start_{tag}", blocks, lands, False, [0] * len(blocks))
        rs_pending.append((tag, s_thru, l_thru, sems[0]))
        return tok[0, 0]

    for layer in reversed(range(DEPTH)):
        j = layer // 2
        sv = saved[layer]
        da1 = _mm_nt("mlp_w2_t", dhb, sv["w_w2"], dy2, w_rows, T, DFF, D, tm_big, 512, D, "dact", extra=sv["r"])
        dw2 = _mm_tn("mlp_dw2", sv["act"], dhb, xs2, ys2, os2, (DFF, D), T, DFF, D, tt, 512, D)
        dw1 = _mm_tn("mlp_dw1", sv["n2"], da1, xs2, ys2, os3, (N_DEV, D, n_w1), T, D, DFF, tt, D, n_w1)
        tok = rs_start(f"mlp{layer}", [dw1, dw2])
        dh, dhb, g_mlp[layer] = _mm_nt("mlp_w1_t", da1, sv["w_w1"], dy2, w_rows, T, D, DFF, tm_k4, D, DFF, "rms",
                                       extra=(sv["h1"], mlp_norm_g[layer][None] + tok, dh))
        if layer % 2 == 0:
            dyab = _mm_nt("ev_out_t", dhb, sv["w_out"], dy2, w_rows, T, D, D, tm_mid, D, D, "f32")
            dwout = _mm_tn("ev_dwout", sv["y"], dhb, xs2, ys2, os2, (D, D), T, D, D, tt, 512, D)
            du, small["cw"][j], small["vec"][j], small["pw"][j] = _ev_bwd(
                dyab, sv["yc"], sv["u"], cw_pad[j], ev_ln_g[j][None], ev_ln_b[j][None], ev_pool_w[j].astype(BF16),
                jnp.transpose(ev_pool_w[j], (0, 2, 1)).astype(BF16), ev_pool_b[j].reshape(1, -1),
                ev_pool_scale[j][None])
            nin = du.shape[1]
            dwin = _mm_tn("ev_dwin", sv["n"], du, xs2, ys2, os2, (D, nin), T, D, nin, tt, D, 512)
            dwin = jnp.transpose(dwin.reshape(D, N_DEV, nin // N_DEV), (1, 0, 2))
            tok = rs_start(f"mix{layer}", [dwin, dwout])
            dh, dhb, g_mix[layer] = _mm_nt("ev_in_t", du, sv["w_in"], dy2, w_rows, T, D, nin, tm_k4, D, nin, "rms",
                                           extra=(sv["h0"], mix_norm_g[layer][None] + tok, dh))
        else:
            dy = _mm_nt("od_out_t", dhb, sv["w_out"], dy2, w_rows, T, D, D, tm_mid, D, D, "f32")
            dwout = _mm_tn("od_dwout", sv["y"], dhb, xs2, ys2, os2, (D, D), T, D, D, tt, 512, D)
            du3, dlb_rows[layer], small["gn"][j] = _hgrn_bwd(dy, sv["o"], sv["s0"], sv["u"], lb_all[layer][None],
                                                              od_gnorm_g[j][None])
            per = D // n_in_od

            def du_t(tt_, tn):
                return pl.BlockSpec((None, tt_, tn), lambda a, b, t: (b // per, t, b % per))

            dwin = _mm_tn("od_dwin", sv["n"], du3, xs2, du_t, os3, (N_DEV, D, n_in_od), T, D, 4 * D, tt, D, n_in_od)
            tok = rs_start(f"mix{layer}", [dwin, dwout])
            dh, dhb, g_mix[layer] = _mm_nt(
                "od_in_t", du3, sv["w_in"], lambda tm, tn: pl.BlockSpec((4, tm, tn // 4), lambda i, jj, k: (0, i, 0)),
                w_rows, T, D, 4 * D, tm_k4, D, 4 * D, "rms", extra=(sv["h0"], mix_norm_g[layer][None] + tok, dh),
                parts=4)

    dmeta = dh[PAD:PAD + N_META]
    grad_x = dh[PAD + N_META:][None]
    dlb_param = _lb_bwd(lb_param, jnp.concatenate(dlb_rows, axis=0))

    pieces = [
        ("final", dg_final), ("pad", jnp.zeros((SMALL_F32_ROWS - 1, D), F32)),
        ("meta", dmeta), ("mix", jnp.concatenate(g_mix, 0)), ("mlp", jnp.concatenate(g_mlp, 0)),
        ("cw", jnp.stack([c[:CONV_WIDTH] for c in small["cw"]])), ("cb", jnp.stack([v_[0] for v_ in small["vec"]])),
        ("lng", jnp.stack([v_[1] for v_ in small["vec"]])), ("lnb", jnp.stack([v_[2] for v_ in small["vec"]])),
        ("pw", jnp.stack(small["pw"])), ("pb", jnp.stack([v_[4] for v_ in small["vec"]])),
        ("ps", jnp.stack([v_[3] for v_ in small["vec"]])), ("gn", jnp.stack([jnp.sum(g_, axis=0)[0] for g_ in small["gn"]])),
        ("lb", dlb_param),
    ]
    flat = jnp.concatenate([p.reshape(-1) for _, p in pieces])
    n_small = flat.shape[0]
    rows_small = SMALL_F32_ROWS + -(-(n_small // 1024 + 1 - SMALL_F32_ROWS) // 16) * 16
    flat = jnp.pad(flat, (0, rows_small * 1024 - n_small)).reshape(rows_small, 1024)

    sm_src = [flat[:SMALL_F32_ROWS], flat[SMALL_F32_ROWS:].astype(BF16)]
    sm_sems, sm_s, sm_l, sm_tok = _push_start("small_start", sm_src, [_landing(a_.shape, a_.dtype) for a_ in sm_src],
                                              True, [0, 0])
    recv = {}
    for tag, s_thru, l_thru, sems in rs_pending:
        got = _push_wait(f"rs_wait_{tag}", s_thru, l_thru, sems, sm_tok, False)
        layer = int(tag[3:])
        if tag.startswith("mlp"):
            recv["w1", layer], recv["w2", layer] = got
        else:
            key = "ev" if layer % 2 == 0 else "od"
            recv[key + "_in", layer // 2], recv[key + "_out", layer // 2] = got

    outs = {}
    big = {"ev_in": ("ev_w_in", ev_w_in, m_ev_w_in, v_ev_w_in), "ev_out": ("ev_w_out", ev_w_out, m_ev_w_out, v_ev_w_out),
           "od_in": ("od_w_in", od_w_in, m_od_w_in, v_od_w_in), "od_out": ("od_w_out", od_w_out, m_od_w_out, v_od_w_out),
           "w1": ("mlp_w1", mlp_w1, m_mlp_w1, v_mlp_w1), "w2": ("mlp_w2", mlp_w2, m_mlp_w2, v_mlp_w2)}
    for key, (name, w, m, v) in big.items():
        res = None
        for l in range(w.shape[0]):
            res = _adamw(recv[key, l], w, m, v, layer=l, prev=res)
        outs[name] = res

    small_params = {
        "meta": ("meta_tokens", None), "mix": ("mix_norm_g", mix_norm_g, m_mix_norm_g, v_mix_norm_g),
        "mlp": ("mlp_norm_g", mlp_norm_g, m_mlp_norm_g, v_mlp_norm_g),
        "final": ("final_norm_g", final_norm_g, m_final_norm_g, v_final_norm_g),
        "cw": ("ev_conv_w", None), "cb": ("ev_conv_b", ev_conv_b, m_ev_conv_b, v_ev_conv_b),
        "lng": ("ev_ln_g", ev_ln_g, m_ev_ln_g, v_ev_ln_g), "lnb": ("ev_ln_b", ev_ln_b, m_ev_ln_b, v_ev_ln_b),
        "pw": ("ev_pool_w", ev_pool_w, m_ev_pool_w, v_ev_pool_w), "pb": ("ev_pool_b", ev_pool_b, m_ev_pool_b, v_ev_pool_b),
        "ps": ("ev_pool_scale", ev_pool_scale, m_ev_pool_scale, v_ev_pool_scale),
        "gn": ("od_gnorm_g", od_gnorm_g, m_od_gnorm_g, v_od_gnorm_g), "lb": ("lb_param", lb_param, m_lb_param, v_lb_param),
    }
    csh = ev_conv_w.shape[2]
    msh = meta_tokens.shape[1]

    def packed(which):
        parts = []
        for key, g_ in pieces:
            ent = small_params.get(key)
            if key == "pad":
                full = g_
            elif key == "meta":
                src = (meta_tokens, m_meta_tokens, v_meta_tokens)[which]
                full = lax.dynamic_update_slice(jnp.zeros((N_META, D), F32), src, (0, dev * msh))
            elif key == "cw":
                src = (ev_conv_w, m_ev_conv_w, v_ev_conv_w)[which]
                full = lax.dynamic_update_slice(jnp.zeros(g_.shape, F32), src, (0, 0, dev * csh))
            else:
                full = ent[1 + which]
            parts.append(full.reshape(-1))
        f = jnp.concatenate(parts)
        return jnp.pad(f, (0, rows_small * 1024 - n_small)).reshape(rows_small, 1024)

    got_f32, got_bf16 = _push_wait("small_wait", sm_s, sm_l, sm_sems[0], outs["mlp_w2"][0], True)
    recv_small = jnp.concatenate([got_f32, got_bf16.astype(F32)], axis=1)
    sres = [r_[0] for r_ in _adamw(recv_small, packed(0)[None], packed(1)[None], packed(2)[None])]
    off = 0
    for key, g_ in pieces:
        size = g_.size
        vals = [r_.reshape(-1)[off:off + size].reshape(g_.shape) for r_ in sres]
        off += size
        if key == "pad":
            continue
        name = small_params[key][0]
        if key == "meta":
            vals = [lax.dynamic_slice(v_, (0, dev * msh), (N_META, msh)) for v_ in vals]
        elif key == "cw":
            vals = [lax.dynamic_slice(v_, (0, 0, dev * csh), v_.shape[:2] + (csh,)) for v_ in vals]
        else:
            vals = [v_.reshape(small_params[key][1].shape) for v_ in vals]
        outs[name] = vals

    names = ["meta_tokens", "mix_norm_g", "mlp_norm_g", "final_norm_g", "ev_w_in", "ev_conv_w", "ev_conv_b", "ev_ln_g",
             "ev_ln_b", "ev_pool_w", "ev_pool_b", "ev_pool_scale", "ev_w_out", "od_w_in", "od_gnorm_g", "od_w_out",
             "lb_param", "mlp_w1", "mlp_w2"]
    result = [loss, grad_x]
    for k in range(4):
        result += [outs[nm][k] for nm in names]
    return tuple(result)
```

```python
import functools

import jax
import jax.numpy as jnp
from jax import lax
from jax.experimental import pallas as pl
from jax.experimental.pallas import tpu as pltpu

F32 = jnp.float32
BF16 = jnp.bfloat16

N_DEV = 8
N_META = 16
CHUNK = 64
PAD = CHUNK - N_META
SUB = 16
HEAD = 128
CONV_WIDTH = 31
HALO = 32
POOL_WINDOWS = (2, 4, 8, 16)
EPS = 1e-6
NEG = -1e30
ADAM_LR, ADAM_B1, ADAM_B2, ADAM_EPS, ADAM_WD, ADAM_STEP = 0.001, 0.9, 0.999, 1e-08, 0.01, 10
VMEM_LIMIT = 56 * 1024 * 1024
EV_ROWS = 416
HGRN_ROWS = 832
MM_ROWS_BIG = 2080
MM_ROWS_MID = 1040
MM_ROWS_K4 = 416
SMALL_F32_ROWS = 8
HGRN_HEADS_FWD = 8
HGRN_HEADS_BWD = 2
MESH = pl.DeviceIdType.MESH
AXES = ("x", "y", "c")
ANY = pl.BlockSpec(memory_space=pl.ANY)


def _cp(*sem):
    return pltpu.CompilerParams(dimension_semantics=sem, vmem_limit_bytes=VMEM_LIMIT)


def _tile(n, cap, mult):
    best = None
    for d in range(mult, min(n, cap) + 1, mult):
        if n % d == 0:
            best = d
    assert best is not None, (n, cap, mult)
    return best


def _nt(a, b):
    return lax.dot_general(a, b, (((1,), (1,)), ((), ())), preferred_element_type=F32)


def _tn(a, b):
    return lax.dot_general(a, b, (((0,), (0,)), ((), ())), preferred_element_type=F32)


def _nn(a, b):
    return jnp.dot(a, b, preferred_element_type=F32)


def _r16(x):
    return x.astype(BF16).astype(F32)


def _row_ids(base, n):
    return base + lax.broadcasted_iota(jnp.int32, (n, 1), 0)


def _dsilu(x, s):
    return s * (1.0 + x * (1.0 - s))


def _loss_head(h, g, tgt):
    T, D = h.shape
    tm = _tile(T, MM_ROWS_MID, 16)
    first_x = PAD + N_META

    def body(h_ref, g_ref, t_ref, loss_ref, dh_ref, dhb_ref, dg_ref):
        i = pl.program_id(0)
        x = h_ref[...]
        r = lax.rsqrt(jnp.mean(x * x, axis=-1, keepdims=True) + EPS)
        xh = x * r
        gv = g_ref[...]
        out = xh * gv
        valid = _row_ids(i * tm, tm) >= first_x
        e = jnp.where(valid, out - t_ref[...], 0.0)
        dout = e * (1.0 / D)
        dxh = dout * gv
        dx = r * (dxh - xh * jnp.mean(dxh * xh, axis=-1, keepdims=True))
        dh_ref[...] = dx
        dhb_ref[...] = dx.astype(BF16)

        @pl.when(i == 0)
        def _():
            dg_ref[...] = jnp.zeros_like(dg_ref)
            loss_ref[...] = jnp.zeros_like(loss_ref)

        dg_ref[...] += jnp.sum(dout * xh, axis=0, keepdims=True)
        loss_ref[...] += 0.5 * jnp.sum(jnp.mean(e * e, axis=-1, keepdims=True))

    row = pl.BlockSpec((tm, D), lambda i: (i, 0))
    vec = pl.BlockSpec((1, D), lambda i: (0, 0))
    return pl.pallas_call(
        body, grid=(T // tm,),
        in_specs=[row, vec, row],
        out_specs=[pl.BlockSpec((8, 128), lambda i: (0, 0)), row, row, vec],
        out_shape=[jax.ShapeDtypeStruct((8, 128), F32), jax.ShapeDtypeStruct((T, D), F32),
                   jax.ShapeDtypeStruct((T, D), BF16), jax.ShapeDtypeStruct((1, D), F32)],
        compiler_params=_cp("arbitrary"), name="loss_head")(h, g, tgt)


def _mm_nn(name, a, w, w_spec, M, N, K, tm, tn, tk, mode, extra=None, a_spec=None):
    nk = K // tk
    if a_spec is None:
        a_spec = pl.BlockSpec((tm, tk), lambda i, j, k: (i, k))
    o_spec = pl.BlockSpec((tm, tn), lambda i, j, k: (i, j))

    def body(*refs):
        if mode == "resid":
            a_ref, w_ref, e_ref = refs[:3]
            outs = refs[3:]
        else:
            a_ref, w_ref = refs[:2]
            outs = refs[2:]
        acc_ref = outs[-1] if nk > 1 else None
        part = _nn(a_ref[...], w_ref[...])

        def finish(acc):
            if mode == "f32":
                outs[0][...] = acc
            elif mode == "relu2":
                r = jnp.maximum(acc, 0.0)
                outs[0][...] = r.astype(BF16)
                outs[1][...] = (r * r).astype(BF16)
            else:
                keep = _row_ids(pl.program_id(0) * tm, tm) >= PAD
                outs[0][...] = jnp.where(keep, e_ref[...] + acc, 0.0)

        if nk == 1:
            finish(part)
        else:
            k = pl.program_id(2)

            @pl.when(k == 0)
            def _():
                acc_ref[...] = part

            @pl.when(k > 0)
            def _():
                acc_ref[...] += part

            @pl.when(k == nk - 1)
            def _():
                finish(acc_ref[...])

    in_specs = [a_spec, w_spec(tk, tn)]
    args = [a, w]
    if mode == "resid":
        in_specs.append(o_spec)
        args.append(extra)
    if mode == "relu2":
        out_specs = [o_spec, o_spec]
        out_shape = [jax.ShapeDtypeStruct((M, N), BF16)] * 2
    else:
        out_specs = [o_spec]
        out_shape = [jax.ShapeDtypeStruct((M, N), F32)]
    scratch = [pltpu.VMEM((tm, tn), F32)] if nk > 1 else []
    res = pl.pallas_call(
        body, grid=(M // tm, N // tn, nk), in_specs=in_specs, out_specs=out_specs, out_shape=out_shape,
        scratch_shapes=scratch, compiler_params=_cp("parallel", "parallel", "arbitrary"), name=name)(*args)
    return res if mode == "relu2" else res[0]


def _mm_rms_nn(name, h, g, w, tm, tn, mode):
    M, K = h.shape
    N = w.shape[1]

    def body(h_ref, g_ref, w_ref, n_ref, *outs):
        @pl.when(pl.program_id(1) == 0)
        def _():
            ch = _tile(tm, 256, 16)

            def chunk(c, carry):
                rows = pl.ds(pl.multiple_of(c * ch, ch), ch)
                x = h_ref[rows, :]
                r = lax.rsqrt(jnp.mean(x * x, axis=-1, keepdims=True) + EPS)
                n_ref[rows, :] = ((x * r) * g_ref[...]).astype(BF16)
                return carry

            lax.fori_loop(0, tm // ch, chunk, 0)

        acc = _nn(n_ref[...], w_ref[...])
        if mode == "f32":
            outs[0][...] = acc
        else:
            r = jnp.maximum(acc, 0.0)
            outs[0][...] = r.astype(BF16)
            outs[1][...] = (r * r).astype(BF16)

    row = pl.BlockSpec((tm, K), lambda i, j: (i, 0))
    o_spec = pl.BlockSpec((tm, tn), lambda i, j: (i, j))
    n_out = 1 if mode == "f32" else 2
    return pl.pallas_call(
        body, grid=(M // tm, N // tn),
        in_specs=[row, pl.BlockSpec((1, K), lambda i, j: (0, 0)), pl.BlockSpec((K, tn), lambda i, j: (0, j))],
        out_specs=[row] + [o_spec] * n_out,
        out_shape=[jax.ShapeDtypeStruct((M, K), BF16)] + [jax.ShapeDtypeStruct((M, N), F32 if mode == "f32" else BF16)] * n_out,
        compiler_params=_cp("parallel", "arbitrary"), name=name)(h, g, w)


def _mm_nt(name, dy, w, dy_spec, w_spec, M, J, N, tm, tj, tn, mode, extra=None, parts=1):
    nk = N // tn
    o_spec = pl.BlockSpec((tm, tj), lambda i, j, k: (i, j))
    n_extra = {"f32": 0, "dact": 1, "rms": 3}[mode]
    if mode == "rms":
        assert nk == 1 and tj == J

    def body(*refs):
        dy_ref, w_ref = refs[:2]
        ex = refs[2:2 + n_extra]
        outs = refs[2 + n_extra:]
        acc_ref = outs[-1] if nk > 1 else None
        if parts == 1:
            part = _nt(dy_ref[...], w_ref[...])
        else:
            wq = tn // parts
            part = _nt(dy_ref[0], w_ref[:, 0:wq])
            for q in range(1, parts):
                part = part + _nt(dy_ref[q], w_ref[:, q * wq:(q + 1) * wq])

        def finish(acc):
            if mode == "f32":
                outs[0][...] = acc
            elif mode == "dact":
                outs[0][...] = (acc * (2.0 * ex[0][...].astype(F32))).astype(BF16)
            else:
                h_ref, g_ref, dres_ref = ex
                dh_ref, dhb_ref, dg_ref = outs[:3]
                i = pl.program_id(0)

                @pl.when(i == 0)
                def _():
                    dg_ref[...] = jnp.zeros_like(dg_ref)

                ch = _tile(tm, 256, 16)
                for c0 in range(0, tm, ch):
                    a_c = acc[c0:c0 + ch]
                    x = h_ref[c0:c0 + ch, :]
                    r = lax.rsqrt(jnp.mean(x * x, axis=-1, keepdims=True) + EPS)
                    xh = x * r
                    dxh = a_c * g_ref[...]
                    dx = r * (dxh - xh * jnp.mean(dxh * xh, axis=-1, keepdims=True))
                    keep = _row_ids(i * tm + c0, ch) >= PAD
                    dh = jnp.where(keep, dres_ref[c0:c0 + ch, :] + dx, 0.0)
                    dh_ref[c0:c0 + ch, :] = dh
                    dhb_ref[c0:c0 + ch, :] = dh.astype(BF16)
                    dg_ref[...] += jnp.sum(a_c * xh, axis=0, keepdims=True)

        if nk == 1:
            finish(part)
        else:
            k = pl.program_id(2)

            @pl.when(k == 0)
            def _():
                acc_ref[...] = part

            @pl.when(k > 0)
            def _():
                acc_ref[...] += part

            @pl.when(k == nk - 1)
            def _():
                finish(acc_ref[...])

    in_specs = [dy_spec(tm, tn), w_spec(tj, tn)]
    args = [dy, w]
    scratch = [pltpu.VMEM((tm, tj), F32)] if nk > 1 else []
    if mode == "rms":
        vec = pl.BlockSpec((1, J), lambda i, j, k: (0, 0))
        h, g, dres = extra
        res = pl.pallas_call(
            body, grid=(M // tm, 1, 1), in_specs=in_specs + [o_spec, vec, o_spec], out_specs=[o_spec, o_spec, vec],
            out_shape=[jax.ShapeDtypeStruct((M, J), F32), jax.ShapeDtypeStruct((M, J), BF16),
                       jax.ShapeDtypeStruct((1, J), F32)],
            compiler_params=_cp("arbitrary", "arbitrary", "arbitrary"), name=name)(*args, h, g, dres)
        return res
    if mode == "dact":
        in_specs.append(o_spec)
        args.append(extra)
    return pl.pallas_call(
        body, grid=(M // tm, J // tj, nk), in_specs=in_specs, out_specs=[o_spec],
        out_shape=[jax.ShapeDtypeStruct((M, J), BF16 if mode == "dact" else F32)],
        scratch_shapes=scratch, compiler_params=_cp("parallel", "parallel", "arbitrary"), name=name)(*args)[0]


def _mm_tn(name, x, dy, x_spec, dy_spec, o_spec, o_shape, T, K, N, tt, tk, tn):
    nt = T // tt

    def body(x_ref, dy_ref, o_ref, *acc):
        part = _tn(x_ref[...], dy_ref[...])
        if nt == 1:
            o_ref[...] = part.astype(BF16)
            return
        acc_ref = acc[0]
        t = pl.program_id(2)

        @pl.when(t == 0)
        def _():
            acc_ref[...] = part

        @pl.when(t > 0)
        def _():
            acc_ref[...] += part

        @pl.when(t == nt - 1)
        def _():
            o_ref[...] = acc_ref[...].astype(BF16)

    return pl.pallas_call(
        body, grid=(K // tk, N // tn, nt), in_specs=[x_spec(tt, tk), dy_spec(tt, tn)], out_specs=o_spec(tk, tn),
        out_shape=jax.ShapeDtypeStruct(o_shape, BF16), scratch_shapes=[pltpu.VMEM((tk, tn), F32)] if nt > 1 else [],
        compiler_params=_cp("parallel", "parallel", "arbitrary"), name=name)(x, dy)


def _pool_counts(base, n, w):
    pos = _row_ids(base, n) - PAD
    return jnp.clip(pos + 1, 1, w).astype(F32)


def _shifted_copies(buf, rows):
    buf[0, rows:rows + 8, :] = jnp.zeros((8, buf.shape[2]), F32)

    def blk(s, carry):
        b = pl.multiple_of(s * HALO, HALO)
        win = buf[0, pl.ds(b, HALO + 8), :]
        for r in range(1, 8):
            buf[r, pl.ds(b, HALO), :] = win[r:r + HALO]
        return carry

    lax.fori_loop(0, rows // HALO, blk, 0)


def _ev_fwd(u, cw, cb, lg, lb, pw, pb, ps):
    T = u.shape[0]
    C = 512
    tm = _tile(T, EV_ROWS, HALO)
    nsub = tm // HALO
    hb = tm // HALO

    def body(val_ref, gate_ref, pin_ref, valh_ref, gateh_ref, pinh_ref, cw_ref, cb_ref, lg_ref, lb_ref, pw_ref,
             pb_ref, ps_ref, yab_ref, yc_ref, a_ext, p_ext, d_buf):
        i = pl.program_id(0)
        nf = (i > 0).astype(F32)
        a_ext[0, 0:HALO, :] = valh_ref[...] * jax.nn.sigmoid(gateh_ref[...]) * nf
        a_ext[0, HALO:HALO + tm, :] = val_ref[...] * jax.nn.sigmoid(gate_ref[...])
        p_ext[0:HALO, :] = pinh_ref[...] * nf
        p_ext[HALO:, :] = pin_ref[...]
        _shifted_copies(a_ext, tm + HALO)

        def sub(s, carry):
            base = pl.multiple_of(s * HALO, HALO)
            acc = jnp.zeros((HALO, C), F32) + cb_ref[...]
            for j in range(CONV_WIDTH):
                off = 2 + j
                acc = acc + cw_ref[pl.ds(j, 1), :] * a_ext[off % 8, pl.ds(pl.multiple_of(base + off // 8 * 8, 8), HALO), :]
            yc_ref[pl.ds(base, HALO), :] = acc
            mu = jnp.mean(acc, axis=-1, keepdims=True)
            yc = acc - mu
            rstd = lax.rsqrt(jnp.mean(yc * yc, axis=-1, keepdims=True) + EPS)
            z = (yc * rstd) * lg_ref[...] + lb_ref[...]
            yab_ref[pl.ds(base, HALO), 0:C] = (z * jax.nn.sigmoid(z)).astype(BF16)
            pwin = p_ext[pl.ds(base, 2 * HALO), :]
            for gi, w in enumerate(POOL_WINDOWS):
                lo, hi = gi * HEAD, (gi + 1) * HEAD
                x = pwin[HALO:, lo:hi]
                tot = x
                for k in range(1, w):
                    tot = tot + pwin[HALO - k:2 * HALO - k, lo:hi]
                cnt = _pool_counts(i * tm + base, HALO, w)
                d_buf[pl.ds(base, HALO), lo:hi] = (tot / cnt - x).astype(BF16)
            return carry

        lax.fori_loop(0, nsub, sub, 0, unroll=2)
        for gi in range(len(POOL_WINDOWS)):
            lo, hi = gi * HEAD, (gi + 1) * HEAD
            y = _nn(d_buf[:, lo:hi], pw_ref[gi]) + pb_ref[:, lo:hi]
            yab_ref[:, C + lo:C + hi] = (y * ps_ref[:, lo:hi]).astype(BF16)

    def main(c):
        return pl.BlockSpec((tm, C), lambda i: (i, c))

    def halo(c):
        return pl.BlockSpec((HALO, C), lambda i: (jnp.maximum(i * hb - 1, 0), c))

    vec = pl.BlockSpec((1, C), lambda i: (0, 0))
    return pl.pallas_call(
        body, grid=(T // tm,),
        in_specs=[main(0), main(1), main(2), halo(0), halo(1), halo(2),
                  pl.BlockSpec((32, C), lambda i: (0, 0)), vec, vec, vec,
                  pl.BlockSpec((4, HEAD, HEAD), lambda i: (0, 0, 0)), vec, vec],
        out_specs=[pl.BlockSpec((tm, 2 * C), lambda i: (i, 0)), pl.BlockSpec((tm, C), lambda i: (i, 0))],
        out_shape=[jax.ShapeDtypeStruct((T, 2 * C), BF16), jax.ShapeDtypeStruct((T, C), F32)],
        scratch_shapes=[pltpu.VMEM((8, tm + HALO + 8, C), F32), pltpu.VMEM((tm + HALO, C), F32),
                        pltpu.VMEM((tm, C), BF16)],
        compiler_params=_cp("parallel"), name="ev_fwd")(u, u, u, u, u, u, cw, cb, lg, lb, pw, pb, ps)


def _ev_bwd(dyab, yc, u, cw, lg, lb, pw, pwt, pb, ps):
    T = u.shape[0]
    C = 512
    tm = _tile(T, EV_ROWS, HALO)
    nsub = tm // HALO
    hb = tm // HALO
    nblk = T // tm
    E = tm + HALO

    def body(dya_ref, dyb_ref, dyah_ref, dybh_ref, yc_ref, ych_ref, val_ref, gate_ref, pin_ref, valh_ref, gateh_ref,
             pinh_ref, cw_ref, lg_ref, lb_ref, pw_ref, pwt_ref, pb_ref, ps_ref,
             du_ref, dcw_ref, dvec_ref, dpw_ref,
             dy_ext, a_ext, p_ext, ddc_ext, dd_buf, d_buf, dpre_buf, dcw_acc, vec_acc):
        i = pl.program_id(0)
        nf = (i > 0).astype(F32)
        nl = (i < nblk - 1).astype(F32)

        @pl.when(i == 0)
        def _():
            dcw_ref[...] = jnp.zeros_like(dcw_ref)
            dvec_ref[...] = jnp.zeros_like(dvec_ref)
            dpw_ref[...] = jnp.zeros_like(dpw_ref)

        dcw_acc[...] = jnp.zeros_like(dcw_acc)
        vec_acc[...] = jnp.zeros_like(vec_acc)
        a_ext[0, 0:HALO, :] = valh_ref[...] * jax.nn.sigmoid(gateh_ref[...]) * nf
        a_ext[0, HALO:E, :] = val_ref[...] * jax.nn.sigmoid(gate_ref[...])
        p_ext[0:HALO, :] = pinh_ref[...] * nf
        p_ext[HALO:, :] = pin_ref[...]
        _shifted_copies(a_ext, E)

        def ln_bwd(y, dya, main):
            mu = jnp.mean(y, axis=-1, keepdims=True)
            ycen = y - mu
            rstd = lax.rsqrt(jnp.mean(ycen * ycen, axis=-1, keepdims=True) + EPS)
            yh = ycen * rstd
            z = yh * lg_ref[...] + lb_ref[...]
            sz = jax.nn.sigmoid(z)
            dz = dya * _dsilu(z, sz)
            dyh = dz * lg_ref[...]
            dy = rstd * (dyh - jnp.mean(dyh, axis=-1, keepdims=True) - yh * jnp.mean(dyh * yh, axis=-1, keepdims=True))
            if main:
                vec_acc[1] += jnp.sum((dz * yh).reshape(HALO // 8, 8, C), axis=0)
                vec_acc[2] += jnp.sum(dz.reshape(HALO // 8, 8, C), axis=0)
                vec_acc[0] += jnp.sum(dy.reshape(HALO // 8, 8, C), axis=0)
            return dy

        def pool_dd(dyb, base, main):
            dpre = dyb * ps_ref[...]
            for gi, w in enumerate(POOL_WINDOWS):
                lo, hi = gi * HEAD, (gi + 1) * HEAD
                dd = _nn(dpre[:, lo:hi].astype(BF16), pwt_ref[gi])
                cnt = _pool_counts(i * tm + base, HALO, w)
                ddc_ext[pl.ds(base, HALO), lo:hi] = dd / cnt
                if main:
                    dd_buf[pl.ds(base, HALO), lo:hi] = dd
            if main:
                dpre_buf[pl.ds(base, HALO), :] = dpre.astype(BF16)
                vec_acc[4] += jnp.sum(dpre.reshape(HALO // 8, 8, C), axis=0)

        def p1(s, carry):
            base = pl.multiple_of(s * HALO, HALO)
            dy_ext[0, pl.ds(base, HALO), :] = ln_bwd(yc_ref[pl.ds(base, HALO), :], dya_ref[pl.ds(base, HALO), :], True)
            pool_dd(dyb_ref[pl.ds(base, HALO), :], base, True)
            return carry

        lax.fori_loop(0, nsub, p1, 0, unroll=2)
        dy_ext[0, tm:E, :] = ln_bwd(ych_ref[...], dyah_ref[...], False) * nl
        _shifted_copies(dy_ext, E)
        dpre_h = dybh_ref[...] * ps_ref[...] * nl
        for gi, w in enumerate(POOL_WINDOWS):
            lo, hi = gi * HEAD, (gi + 1) * HEAD
            dd = _nn(dpre_h[:, lo:hi].astype(BF16), pwt_ref[gi])
            ddc_ext[tm:, lo:hi] = dd / _pool_counts(i * tm + tm, HALO, w)

        def p2(s, carry):
            base = pl.multiple_of(s * HALO, HALO)
            dy_m = dy_ext[0, pl.ds(base, HALO), :]
            da = jnp.zeros((HALO, C), F32)
            for j in range(CONV_WIDTH):
                sh = CONV_WIDTH - 1 - j
                off = 2 + j
                da = da + cw_ref[pl.ds(j, 1), :] * dy_ext[sh % 8, pl.ds(pl.multiple_of(base + sh // 8 * 8, 8), HALO), :]
                a_j = a_ext[off % 8, pl.ds(pl.multiple_of(base + off // 8 * 8, 8), HALO), :]
                dcw_acc[j] += jnp.sum((dy_m * a_j).reshape(HALO // 8, 8, C), axis=0)
            v = val_ref[pl.ds(base, HALO), :]
            g = gate_ref[pl.ds(base, HALO), :]
            sg = jax.nn.sigmoid(g)
            du_ref[pl.ds(base, HALO), 0:C] = (da * sg).astype(BF16)
            du_ref[pl.ds(base, HALO), C:2 * C] = (da * v * sg * (1.0 - sg)).astype(BF16)
            pwin = p_ext[pl.ds(base, 2 * HALO), :]
            cwin = ddc_ext[pl.ds(base, 2 * HALO), :]
            for gi, w in enumerate(POOL_WINDOWS):
                lo, hi = gi * HEAD, (gi + 1) * HEAD
                x = pwin[HALO:, lo:hi]
                tot = x
                back = cwin[0:HALO, lo:hi]
                for k in range(1, w):
                    tot = tot + pwin[HALO - k:2 * HALO - k, lo:hi]
                    back = back + cwin[k:k + HALO, lo:hi]
                cnt = _pool_counts(i * tm + base, HALO, w)
                d_buf[pl.ds(base, HALO), lo:hi] = (tot / cnt - x).astype(BF16)
                du_ref[pl.ds(base, HALO), 2 * C + lo:2 * C + hi] = (back - dd_buf[pl.ds(base, HALO), lo:hi]).astype(BF16)
            return carry

        lax.fori_loop(0, nsub, p2, 0)
        for gi in range(len(POOL_WINDOWS)):
            lo, hi = gi * HEAD, (gi + 1) * HEAD
            pre = _nn(d_buf[:, lo:hi], pw_ref[gi]) + pb_ref[:, lo:hi]
            vec_acc[3, :, lo:hi] += jnp.sum((dyb_ref[:, lo:hi] * pre).reshape(tm // 8, 8, HEAD), axis=0)
            dpw_ref[gi] += _tn(d_buf[:, lo:hi], dpre_buf[:, lo:hi])
        for j in range(CONV_WIDTH):
            dcw_ref[pl.ds(j, 1), :] += jnp.sum(dcw_acc[j], axis=0, keepdims=True)
        for r in range(5):
            dvec_ref[pl.ds(r, 1), :] += jnp.sum(vec_acc[r], axis=0, keepdims=True)

    def main(c, width=C):
        return pl.BlockSpec((tm, width), lambda i: (i, c))

    def prev(c):
        return pl.BlockSpec((HALO, C), lambda i: (jnp.maximum(i * hb - 1, 0), c))

    def nxt(c):
        return pl.BlockSpec((HALO, C), lambda i: (jnp.minimum((i + 1) * hb, T // HALO - 1), c))

    vec = pl.BlockSpec((1, C), lambda i: (0, 0))
    mat = pl.BlockSpec((4, HEAD, HEAD), lambda i: (0, 0, 0))
    return pl.pallas_call(
        body, grid=(nblk,),
        in_specs=[main(0), main(1), nxt(0), nxt(1), main(0), nxt(0), main(0), main(1), main(2), prev(0), prev(1),
                  prev(2), pl.BlockSpec((32, C), lambda i: (0, 0)), vec, vec, mat, mat, vec, vec],
        out_specs=[pl.BlockSpec((tm, 3 * C), lambda i: (i, 0)), pl.BlockSpec((32, C), lambda i: (0, 0)),
                   pl.BlockSpec((8, C), lambda i: (0, 0)), mat],
        out_shape=[jax.ShapeDtypeStruct((T, 3 * C), BF16), jax.ShapeDtypeStruct((32, C), F32),
                   jax.ShapeDtypeStruct((8, C), F32), jax.ShapeDtypeStruct((4, HEAD, HEAD), F32)],
        scratch_shapes=[pltpu.VMEM((8, E + 8, C), F32), pltpu.VMEM((8, E + 8, C), F32), pltpu.VMEM((E, C), F32),
                        pltpu.VMEM((E, C), F32), pltpu.VMEM((tm, C), F32), pltpu.VMEM((tm, C), BF16),
                        pltpu.VMEM((tm, C), BF16), pltpu.VMEM((32, 8, C), F32), pltpu.VMEM((8, 8, C), F32)],
        compiler_params=_cp("arbitrary"), name="ev_bwd")(
            dyab, dyab, dyab, dyab, yc, yc, u, u, u, u, u, u, cw, lg, lb, pw, pwt, pb, ps)


def _cumsum_rows(x, reverse=False):
    n = x.shape[0]
    rid = lax.broadcasted_iota(jnp.int32, (n, 1), 0)
    k = 1
    while k < n:
        if reverse:
            sh = jnp.where(rid < n - k, pltpu.roll(x, n - k, 0), 0.0)
        else:
            sh = jnp.where(rid >= k, pltpu.roll(x, k, 0), 0.0)
        x = x + sh
        k *= 2
    return x


def _hgrn_gates(qr, fr, lbv):
    sq = jax.nn.sigmoid(qr)
    sg = jax.nn.sigmoid(fr)
    fg = lbv + (1.0 - lbv) * sg
    return qr * sq, sq, sg, fg, 1.0 - fg, jnp.log(fg)


def _hgrn_fwd(u, lbv, gn):
    T = u.shape[0]
    H = 8
    RB = _tile(T, HGRN_ROWS, CHUNK)
    NC = RB // CHUNK
    NS = CHUNK // SUB

    HP = HGRN_HEADS_FWD
    W = HP * HEAD

    def body(q_ref, f_ref, i_ref, g_ref, lb_ref, gn_ref, y_ref, o_ref, s0_ref, st, qs, ks, bs, vs, os_):
        rb = pl.program_id(1)

        @pl.when(rb == 0)
        def _():
            st[...] = jnp.zeros_like(st)

        t8 = lax.broadcasted_iota(jnp.int32, (8, 1), 0)

        def head(hh, c, rows):
            sl = slice(hh * HEAD, (hh + 1) * HEAD)
            q, _, _, _, kk, lf = _hgrn_gates(q_ref[rows, sl], f_ref[rows, sl], lb_ref[:, sl])
            v = i_ref[rows, sl]
            b = _cumsum_rows(lf)
            qs[hh] = q
            ks[hh] = kk
            bs[hh] = b
            vs[hh] = v
            st0 = st[hh]
            s0_ref[hh, c] = st0
            os_[hh] = _nt((q * jnp.exp(b)).astype(BF16), st0.astype(BF16))
            for I in range(NS):
                lo = I * SUB
                qI = qs[hh, lo:lo + SUB, :]
                bI = bs[hh, lo:lo + SUB, :]
                oI = jnp.zeros((SUB, HEAD), F32)
                if I > 0:
                    bprev = bs[hh, pl.ds(lo - 1, 1), :]
                    qt = _r16(qI * jnp.exp(bI - bprev))
                    kt = _r16(ks[hh, 0:lo, :] * jnp.exp(bprev - bs[hh, 0:lo, :]))
                    A = _nt(qt, kt)
                    oI = oI + _nn(_r16(A), _r16(vs[hh, 0:lo, :]))
                od = [jnp.zeros((8, HEAD), F32) for _ in range(SUB // 8)]
                for s in range(SUB):
                    row = pl.ds(lo + s, 1)
                    brow, krow, vrow = bs[hh, row, :], ks[hh, row, :], vs[hh, row, :]
                    for ti in range(SUB // 8):
                        o8 = 8 * ti
                        if s > o8 + 7:
                            continue
                        d = bI[o8:o8 + 8] - brow
                        if s > o8:
                            d = jnp.where(t8 >= s - o8, d, NEG)
                        col = jnp.sum(qI[o8:o8 + 8] * jnp.exp(d) * krow, axis=1, keepdims=True)
                        od[ti] = od[ti] + col * vrow
                os_[hh, lo:lo + SUB, :] += oI + jnp.concatenate(od, axis=0)
            blast = bs[hh, pl.ds(CHUNK - 1, 1), :]
            kh = kk * jnp.exp(blast - b)
            st[hh] = st0 * jnp.exp(blast) + _tn(v.astype(BF16), kh.astype(BF16))
            o = os_[hh]
            o_ref[rows, sl] = o
            rr = lax.rsqrt(jnp.mean(o * o, axis=-1, keepdims=True) + EPS)
            gr = g_ref[rows, sl]
            y_ref[rows, sl] = (((o * rr) * gn_ref[...]) * (gr * jax.nn.sigmoid(gr))).astype(BF16)

        def chunk(c, carry):
            rows = pl.ds(pl.multiple_of(c * CHUNK, CHUNK), CHUNK)
            for hh in range(HP):
                head(hh, c, rows)
            return carry

        lax.fori_loop(0, NC, chunk, 0)

    def blk(q):
        return pl.BlockSpec((RB, W), lambda h, r: (r, q * (H // HP) + h))

    sc = lambda: pltpu.VMEM((HP, CHUNK, HEAD), F32)
    return pl.pallas_call(
        body, grid=(H // HP, T // RB),
        in_specs=[blk(0), blk(1), blk(2), blk(3), pl.BlockSpec((1, W), lambda h, r: (0, h)),
                  pl.BlockSpec((1, HEAD), lambda h, r: (0, 0))],
        out_specs=[pl.BlockSpec((RB, W), lambda h, r: (r, h)), pl.BlockSpec((RB, W), lambda h, r: (r, h)),
                   pl.BlockSpec((HP, NC, HEAD, HEAD), lambda h, r: (h, r, 0, 0))],
        out_shape=[jax.ShapeDtypeStruct((T, H * HEAD), BF16), jax.ShapeDtypeStruct((T, H * HEAD), F32),
                   jax.ShapeDtypeStruct((H, T // CHUNK, HEAD, HEAD), F32)],
        scratch_shapes=[pltpu.VMEM((HP, HEAD, HEAD), F32), sc(), sc(), sc(), sc(), sc()],
        compiler_params=_cp("parallel", "arbitrary"), name="hgrn_fwd")(u, u, u, u, lbv, gn)


def _hgrn_bwd(dy, o, s0, u, lbv, gn):
    T = u.shape[0]
    H = 8
    RB = _tile(T, HGRN_ROWS, CHUNK)
    NB = T // RB
    NC = RB // CHUNK
    NS = CHUNK // SUB

    def body(q_ref, f_ref, i_ref, g_ref, lb_ref, gn_ref, o_ref, dy_ref, s0_ref, du_ref, dlb_ref, dgn_ref,
             dst, qs, ks, bs, vs, dos, dqs, dks, dki, dvs, dbs):
        rb = pl.program_id(1)

        @pl.when(rb == 0)
        def _():
            dst[...] = jnp.zeros_like(dst)
            dlb_ref[...] = jnp.zeros_like(dlb_ref)
            dgn_ref[...] = jnp.zeros_like(dgn_ref)

        t8 = lax.broadcasted_iota(jnp.int32, (8, 1), 0)
        lane = lax.broadcasted_iota(jnp.int32, (8, HEAD), 1)
        gnv = gn_ref[...]

        def head(hh, c, rows):
            sl = slice(hh * HEAD, (hh + 1) * HEAD)
            lbv_ = lb_ref[:, sl]
            qr = q_ref[rows, sl]
            q, sq, sg, fg, kk, lf = _hgrn_gates(qr, f_ref[rows, sl], lbv_)
            v = i_ref[rows, sl]
            gr = g_ref[rows, sl]
            b = _cumsum_rows(lf)
            eb = jnp.exp(b)
            ov = o_ref[rows, sl]
            dyv = dy_ref[rows, sl]
            rr = lax.rsqrt(jnp.mean(ov * ov, axis=-1, keepdims=True) + EPS)
            oh = ov * rr
            gs = jax.nn.sigmoid(gr)
            dgr = dyv * (oh * gnv) * _dsilu(gr, gs)
            dnrm = dyv * (gr * gs)
            dgn_ref[hh] += jnp.sum(dnrm * oh, axis=0, keepdims=True)
            t1 = dnrm * gnv
            do = rr * (t1 - oh * jnp.mean(t1 * oh, axis=-1, keepdims=True))
            qs[hh] = q
            ks[hh] = kk
            bs[hh] = b
            vs[hh] = v
            dos[hh] = do
            st0 = s0_ref[hh, c]
            dS = dst[hh]
            do_b = do.astype(BF16)
            blast = bs[hh, pl.ds(CHUNK - 1, 1), :]
            elast = jnp.exp(blast - b)
            dq_inter = _nn(do_b, st0.astype(BF16)) * eb
            dqs[hh] = dq_inter
            dbs[hh] = q * dq_inter
            kh = kk * elast
            dvs[hh] = _nt(kh.astype(BF16), dS.astype(BF16))
            dk_inter = _nn(v.astype(BF16), dS.astype(BF16)) * elast
            dki[hh] = dk_inter
            dks[hh] = jnp.zeros((CHUNK, HEAD), F32)
            for I in range(NS):
                lo = I * SUB
                qI = qs[hh, lo:lo + SUB, :]
                bI = bs[hh, lo:lo + SUB, :]
                doI = dos[hh, lo:lo + SUB, :]
                dqI = jnp.zeros((SUB, HEAD), F32)
                dbI = jnp.zeros((SUB, HEAD), F32)
                if I > 0:
                    bprev = bs[hh, pl.ds(lo - 1, 1), :]
                    eq = jnp.exp(bI - bprev)
                    ek = jnp.exp(bprev - bs[hh, 0:lo, :])
                    qt = _r16(qI * eq)
                    kt = _r16(ks[hh, 0:lo, :] * ek)
                    A = _r16(_nt(qt, kt))
                    doI_b = _r16(doI)
                    dA = _r16(_nt(doI_b, _r16(vs[hh, 0:lo, :])))
                    dvs[hh, 0:lo, :] += _tn(A, doI_b)
                    dqt = _nn(dA, kt)
                    dkt = _tn(dA, qt)
                    dqI = dqI + dqt * eq
                    dbI = dbI + qt.astype(F32) * dqt
                    dks[hh, 0:lo, :] += dkt * ek
                    dbs[hh, 0:lo, :] -= kt.astype(F32) * dkt
                dq_t = [jnp.zeros((8, HEAD), F32) for _ in range(SUB // 8)]
                a_t = [jnp.zeros((8, HEAD), F32) for _ in range(SUB // 8)]
                for s in range(SUB):
                    row = pl.ds(lo + s, 1)
                    brow, krow, vrow = bs[hh, row, :], ks[hh, row, :], vs[hh, row, :]
                    dk_s = None
                    for ti in range(SUB // 8):
                        o8 = 8 * ti
                        if s > o8 + 7:
                            continue
                        d = bI[o8:o8 + 8] - brow
                        if s > o8:
                            d = jnp.where(t8 >= s - o8, d, NEG)
                        Es = jnp.exp(d)
                        qE = qI[o8:o8 + 8] * Es
                        col = jnp.sum(qE * krow, axis=1, keepdims=True)
                        a_t[ti] = jnp.where(lane == s, col, a_t[ti])
                        dcol = jnp.sum(doI[o8:o8 + 8] * vrow, axis=1, keepdims=True)
                        dq_t[ti] = dq_t[ti] + (dcol * Es) * krow
                        part = jnp.sum(dcol * qE, axis=0, keepdims=True)
                        dk_s = part if dk_s is None else dk_s + part
                    dks[hh, row, :] += dk_s
                    dbs[hh, row, :] -= krow * dk_s
                a_d = jnp.concatenate(a_t, axis=0)
                dq_d = jnp.concatenate(dq_t, axis=0)
                dvs[hh, lo:lo + SUB, :] += _tn(a_d, doI)[0:SUB]
                dqI = dqI + dq_d
                dbI = dbI + qI * dq_d
                dqs[hh, lo:lo + SUB, :] += dqI
                dbs[hh, lo:lo + SUB, :] += dbI
            kdk = kk * dki[hh]
            excl = _cumsum_rows(kdk) - kdk
            suff = _cumsum_rows(dbs[hh], reverse=True)
            gdec = jnp.sum(dS * st0, axis=0, keepdims=True) * jnp.exp(blast)
            dlf = suff + excl + gdec
            dk = dks[hh] + dki[hh]
            dfg = dlf / fg - dk
            dlb_ref[:, sl] += jnp.sum(dfg * (1.0 - sg), axis=0, keepdims=True)
            du_ref[0, rows, sl] = (dqs[hh] * _dsilu(qr, sq)).astype(BF16)
            du_ref[1, rows, sl] = (dfg * (1.0 - lbv_) * sg * (1.0 - sg)).astype(BF16)
            du_ref[2, rows, sl] = dvs[hh].astype(BF16)
            du_ref[3, rows, sl] = dgr.astype(BF16)
            dst[hh] = dS * jnp.exp(blast) + _tn(do_b, (q * eb).astype(BF16))

        def chunk(cc, carry):
            c = NC - 1 - cc
            rows = pl.ds(pl.multiple_of(c * CHUNK, CHUNK), CHUNK)
            for hh in range(HP):
                head(hh, c, rows)
            return carry

        lax.fori_loop(0, NC, chunk, 0)

    HP = HGRN_HEADS_BWD
    W = HP * HEAD

    def blk(qd):
        return pl.BlockSpec((RB, W), lambda h, r: (NB - 1 - r, qd * (H // HP) + h))

    hblk = pl.BlockSpec((RB, W), lambda h, r: (NB - 1 - r, h))
    sc = lambda: pltpu.VMEM((HP, CHUNK, HEAD), F32)
    return pl.pallas_call(
        body, grid=(H // HP, NB),
        in_specs=[blk(0), blk(1), blk(2), blk(3), pl.BlockSpec((1, W), lambda h, r: (0, h)),
                  pl.BlockSpec((1, HEAD), lambda h, r: (0, 0)), hblk, hblk,
                  pl.BlockSpec((HP, NC, HEAD, HEAD), lambda h, r: (h, NB - 1 - r, 0, 0))],
        out_specs=[pl.BlockSpec((4, RB, W), lambda h, r: (0, NB - 1 - r, h)),
                   pl.BlockSpec((1, W), lambda h, r: (0, h)), pl.BlockSpec((HP, 1, HEAD), lambda h, r: (h, 0, 0))],
        out_shape=[jax.ShapeDtypeStruct((4, T, H * HEAD), BF16), jax.ShapeDtypeStruct((1, H * HEAD), F32),
                   jax.ShapeDtypeStruct((H, 1, HEAD), F32)],
        scratch_shapes=[pltpu.VMEM((HP, HEAD, HEAD), F32)] + [sc() for _ in range(10)],
        compiler_params=_cp("parallel", "arbitrary"), name="hgrn_bwd")(u, u, u, u, lbv, gn, o, dy, s0)


def _softmax_rows(p_ref, L):
    rows = [p_ref[pl.ds(l, 1), :] for l in range(L)]
    m = rows[0]
    for r in rows[1:]:
        m = jnp.maximum(m, r)
    e = [jnp.exp(r - m) for r in rows]
    tot = e[0]
    for t in e[1:]:
        tot = tot + t
    return [t / tot for t in e]


def _lb_fwd(lbp):
    L, D = lbp.shape

    def body(p_ref, o_ref):
        sm = _softmax_rows(p_ref, L)
        acc = jnp.zeros((1, D), F32)
        o_ref[pl.ds(0, 1), :] = acc
        for l in range(1, L):
            acc = acc + sm[l]
            o_ref[pl.ds(l, 1), :] = acc

    return pl.pallas_call(body, out_shape=jax.ShapeDtypeStruct((L, D), F32), name="lb_fwd")(lbp)


def _lb_bwd(lbp, dlb):
    L, D = lbp.shape

    def body(p_ref, d_ref, o_ref):
        sm = _softmax_rows(p_ref, L)
        dsm = [jnp.zeros((1, D), F32)]
        for i in range(1, L):
            t = jnp.zeros((1, D), F32)
            for l in range(i, L):
                t = t + d_ref[pl.ds(l, 1), :]
            dsm.append(t)
        dot = jnp.zeros((1, D), F32)
        for i in range(L):
            dot = dot + dsm[i] * sm[i]
        for i in range(L):
            o_ref[pl.ds(i, 1), :] = sm[i] * (dsm[i] - dot)

    return pl.pallas_call(body, out_shape=jax.ShapeDtypeStruct((L, D), F32), name="lb_bwd")(lbp, dlb)


def _my_pos():
    return lax.axis_index("x"), lax.axis_index("y"), lax.axis_index("c")


def _peer(mask):
    x, y, c = _my_pos()
    mx, my, mc = (mask >> 2) & 1, (mask >> 1) & 1, mask & 1
    px = (1 - x) if mx else x
    py = (1 - y) if my else y
    pc = (1 - c) if mc else c
    return (px, py, pc), 4 * px + 2 * py + pc


HBM_SPEC =pl.BlockSpec(memory_space=pltpu.HBM)
SEM_SPEC = pl.BlockSpec(memory_space=pltpu.SEMAPHORE)
EFFECT = pltpu.SideEffectType.DATAFLOW_SIDE_EFFECTING


def _hbm(a):
    return pltpu.with_memory_space_constraint(a, pltpu.HBM)


def _landing(block_shape, dtype, axis=0):
    if axis == 0:
        return lax.empty((N_DEV,) + tuple(block_shape), dtype)
    rows, n = block_shape
    return lax.empty((rows, N_DEV * n), dtype)


def _slot(ref, i):
    if len(ref.shape) == 2:
        n = ref.shape[1] // N_DEV
        return ref.at[:, pl.ds(i * n, n)]
    return ref.at[i]


def _push_start(name, srcs, lands, whole, groups):
    n = len(srcs)
    ng = 1 + max(groups)
    cnt = [groups.count(g) for g in range(ng)]
    idx = [groups[:a].count(groups[a]) for a in range(n)]

    def body(*refs):
        src_refs, land_refs = refs[:n], refs[n:2 * n]
        sems = refs[2 * n:2 * n + 3 * ng]
        token = refs[-1]
        x, y, c = _my_pos()
        me = 4 * x + 2 * y + c
        for a in range(n):
            g = groups[a]
            for m in range(1, N_DEV):
                peer, pid = _peer(m)
                pltpu.make_async_remote_copy(
                    src_ref=src_refs[a] if whole else src_refs[a].at[pid], dst_ref=_slot(land_refs[a], me),
                    send_sem=sems[3 * g].at[idx[a] * (N_DEV - 1) + m - 1],
                    recv_sem=sems[3 * g + 1].at[idx[a] * (N_DEV - 1) + m - 1],
                    device_id=peer, device_id_type=MESH).start()
            pltpu.make_async_copy(src_refs[a] if whole else src_refs[a].at[me], _slot(land_refs[a], me),
                                  sems[3 * g + 2].at[idx[a]]).start()
        token[...] = jnp.zeros_like(token)

    sem_shapes = []
    for g in range(ng):
        sem_shapes += [pltpu.SemaphoreType.DMA((cnt[g] * (N_DEV - 1),))] * 2 + [pltpu.SemaphoreType.DMA((cnt[g],))]
    thru = [pltpu.HBM(s.shape, s.dtype) for s in list(srcs) + list(lands)]
    res = pl.pallas_call(
        body, name=name,
        out_shape=tuple(sem_shapes + thru + [jax.ShapeDtypeStruct((8, 128), F32)]),
        in_specs=tuple([HBM_SPEC] * (2 * n)),
        out_specs=tuple([SEM_SPEC] * (3 * ng) + [HBM_SPEC] * (2 * n) + [pl.BlockSpec(memory_space=pltpu.VMEM)]),
        input_output_aliases={i: 3 * ng + i for i in range(2 * n)},
        compiler_params=pltpu.CompilerParams(has_side_effects=EFFECT),
    )(*[_hbm(s) for s in srcs], *[_hbm(z) for z in lands])
    sems = [(res[3 * g], res[3 * g + 1], res[3 * g + 2]) for g in range(ng)]
    srcs_thru = list(res[3 * ng:3 * ng + n])
    lands_thru = list(res[3 * ng + n:3 * ng + 2 * n])
    return sems, srcs_thru, lands_thru, res[-1]


def _push_wait(name, srcs_thru, lands_thru, sems, after, whole):
    n = len(srcs_thru)

    def body(*refs):
        src_refs, land_refs = refs[:n], refs[n:2 * n]
        send_sems, recv_sems, own_sems = refs[2 * n], refs[2 * n + 1], refs[2 * n + 2]
        x, y, c = _my_pos()
        me = 4 * x + 2 * y + c
        for a in range(n):
            pltpu.make_async_copy(src_refs[a] if whole else src_refs[a].at[me], _slot(land_refs[a], me),
                                  own_sems.at[a]).wait()
            for m in range(1, N_DEV):
                peer, pid = _peer(m)
                cp = pltpu.make_async_remote_copy(
                    src_ref=src_refs[a] if whole else src_refs[a].at[pid], dst_ref=_slot(land_refs[a], pid),
                    send_sem=send_sems.at[a * (N_DEV - 1) + m - 1], recv_sem=recv_sems.at[a * (N_DEV - 1) + m - 1],
                    device_id=peer, device_id_type=MESH)
                cp.wait_send()
                cp.wait_recv()

    thru = [pltpu.HBM(s.shape, s.dtype) for s in list(srcs_thru) + list(lands_thru)]
    res = pl.pallas_call(
        body, name=name, out_shape=tuple(thru),
        in_specs=tuple([HBM_SPEC] * (2 * n) + [SEM_SPEC, SEM_SPEC, SEM_SPEC, ANY]),
        out_specs=tuple([HBM_SPEC] * (2 * n)),
        input_output_aliases={i: i for i in range(2 * n)},
        compiler_params=pltpu.CompilerParams(has_side_effects=EFFECT),
    )(*srcs_thru, *lands_thru, sems[0], sems[1], sems[2], after)
    return list(res[n:])


def _adamw(recv, w, m, v, layer=0, prev=None):
    L, R, C = w.shape
    tr = _tile(R, max(8, (1 << 18) // C), 8) if R % 8 == 0 else R
    bc1 = 1.0 - ADAM_B1 ** ADAM_STEP
    bc2 = 1.0 - ADAM_B2 ** ADAM_STEP
    if prev is None:
        prev = [lax.empty((L, R, C), F32) for _ in range(4)]

    def body(r_ref, w_ref, m_ref, v_ref, p0, p1, p2, p3, g_ref, d_ref, nm_ref, nv_ref):
        g = r_ref[0].astype(F32)
        for s in range(1, N_DEV):
            g = g + r_ref[s].astype(F32)
        nm = ADAM_B1 * m_ref[...] + (1.0 - ADAM_B1) * g
        nv = ADAM_B2 * v_ref[...] + (1.0 - ADAM_B2) * (g * g)
        mh = nm / bc1
        vh = nv / bc2
        g_ref[...] = g
        d_ref[...] = -ADAM_LR * (mh / (jnp.sqrt(vh) + ADAM_EPS) + ADAM_WD * w_ref[...])
        nm_ref[...] = nm
        nv_ref[...] = nv

    row = pl.BlockSpec((None, tr, C), lambda i: (layer, i, 0))
    return pl.pallas_call(
        body, grid=(R // tr,),
        in_specs=[pl.BlockSpec((N_DEV, tr, C), lambda i: (0, i, 0)), row, row, row] + [ANY] * 4,
        out_specs=[row] * 4, out_shape=[jax.ShapeDtypeStruct((L, R, C), F32)] * 4,
        input_output_aliases={4: 0, 5: 1, 6: 2, 7: 3},
        compiler_params=_cp("parallel"), name="adamw")(recv, w, m, v, *prev)


def _full_w_spec(tk, tn):
    return pl.BlockSpec((tk, tn), lambda i, j, k: (k, j))


def kernel(x, meta_tokens, mix_norm_g, mlp_norm_g, final_norm_g, ev_w_in, ev_conv_w, ev_conv_b, ev_ln_g, ev_ln_b, ev_pool_w, ev_pool_b, ev_pool_scale, ev_w_out, od_w_in, od_gnorm_g, od_w_out, lb_param, mlp_w1, mlp_w2, loss_target, m_meta_tokens, m_mix_norm_g, m_mlp_norm_g, m_final_norm_g, m_ev_w_in, m_ev_conv_w, m_ev_conv_b, m_ev_ln_g, m_ev_ln_b, m_ev_pool_w, m_ev_pool_b, m_ev_pool_scale, m_ev_w_out, m_od_w_in, m_od_gnorm_g, m_od_w_out, m_lb_param, m_mlp_w1, m_mlp_w2, v_meta_tokens, v_mix_norm_g, v_mlp_norm_g, v_final_norm_g, v_ev_w_in, v_ev_conv_w, v_ev_conv_b, v_ev_ln_g, v_ev_ln_b, v_ev_pool_w, v_ev_pool_b, v_ev_pool_scale, v_ev_w_out, v_od_w_in, v_od_gnorm_g, v_od_w_out, v_lb_param, v_mlp_w1, v_mlp_w2):
    S, D = x.shape[1], x.shape[2]
    T = PAD + N_META + S
    DEPTH = mix_norm_g.shape[0]
    DFF = mlp_w1.shape[2] * N_DEV
    dev = 4 * lax.axis_index("x") + 2 * lax.axis_index("y") + lax.axis_index("c")

    n_ev = ev_w_in.shape[0]
    n_od = od_w_in.shape[0]
    n_in_od = od_w_in.shape[2]
    n_w1 = mlp_w1.shape[2]

    ag_src, ag_grp, ag_axis, ag_at = [], [], [], {}
    for key, arr in (("meta", meta_tokens), ("cw", ev_conv_w)):
        ag_at["small", key] = len(ag_src)
        ag_src.append(arr)
        ag_grp.append(len(ag_grp))
        ag_axis.append(0)
    for layer in range(DEPTH):
        j = layer // 2
        mixer = [("in", ev_w_in[j]), ("out", ev_w_out[j])] if layer % 2 == 0 else [("in", od_w_in[j]), ("out", od_w_out[j])]
        for pos, (key, arr) in enumerate(mixer + [("w1", mlp_w1[layer]), ("w2", mlp_w2[layer])]):
            ag_at[layer, key] = len(ag_src)
            ag_src.append(arr.astype(BF16))
            ag_grp.append(len(ag_grp))
            ag_axis.append(1 if key in ("in", "w1") and arr.shape[1] % 128 == 0 else 0)
    ag_sems, ag_s, ag_l, ag_tok = _push_start(
        "ag_start", ag_src, [_landing(s_.shape, s_.dtype, ax) for s_, ax in zip(ag_src, ag_axis)], True, ag_grp)

    def ag_wait(layer, key, after):
        a = ag_at[layer, key]
        return _push_wait(f"ag_wait_{a}", [ag_s[a]], [ag_l[a]], ag_sems[a], after, True)[0]

    g_meta = ag_wait("small", "meta", ag_tok)
    g_cw = ag_wait("small", "cw", ag_tok)
    meta_full = jnp.transpose(g_meta, (1, 0, 2)).reshape(N_META, D)
    cw_full = jnp.transpose(g_cw, (1, 2, 0, 3)).reshape(n_ev, CONV_WIDTH, -1)
    cw_pad = jnp.pad(cw_full, ((0, 0), (0, 32 - CONV_WIDTH), (0, 0)))

    h = jnp.concatenate([jnp.zeros((PAD, D), F32), meta_full, x[0]], axis=0) + ag_tok[0, 0]
    tgt = jnp.pad(loss_target[0], ((PAD + N_META, 0), (0, 0)))
    lb_all = _lb_fwd(lb_param)
    h, tgt, lb_all = lax.optimization_barrier((h, tgt, lb_all))

    tm_big = _tile(T, MM_ROWS_BIG, 16)
    tm_mid = _tile(T, MM_ROWS_MID, 16)
    tm_k4 = _tile(T, MM_ROWS_K4, 16)

    saved = []
    for layer in range(DEPTH):
        j = layer // 2
        sv = {"h0": h}
        g_in = ag_wait(layer, "in", h)
        w_in = g_in if g_in.ndim == 2 else jnp.transpose(g_in, (1, 0, 2)).reshape(D, -1)
        if layer % 2 == 0:
            sv["n"], u = _mm_rms_nn("ev_in", h, mix_norm_g[layer][None], w_in, tm_big, 512, "f32")
            yab, yc = _ev_fwd(u, cw_pad[j], ev_conv_b[j][None], ev_ln_g[j][None], ev_ln_b[j][None],
                              ev_pool_w[j].astype(BF16), ev_pool_b[j].reshape(1, -1), ev_pool_scale[j][None])
            sv.update(u=u, y=yab, yc=yc)
            w_out = ag_wait(layer, "out", yab).reshape(-1, D)
            h = _mm_nn("ev_out", yab, w_out, _full_w_spec, T, D, D, tm_mid, D, D, "resid", extra=h)
        else:
            sv["n"], u = _mm_rms_nn("od_in", h, mix_norm_g[layer][None], w_in, tm_big, 512, "f32")
            y, o, s0 = _hgrn_fwd(u, lb_all[layer][None], od_gnorm_g[j][None])
            sv.update(u=u, y=y, o=o, s0=s0)
            w_out = ag_wait(layer, "out", y).reshape(-1, D)
            h = _mm_nn("od_out", y, w_out, _full_w_spec, T, D, D, tm_mid, D, D, "resid", extra=h)
        sv["h1"] = h
        w_w1 = ag_wait(layer, "w1", h)
        n2, r, act = _mm_rms_nn("mlp_w1", h, mlp_norm_g[layer][None], w_w1, tm_big, 512, "relu2")
        w_w2 = ag_wait(layer, "w2", act).reshape(DFF, D)
        sv.update(w_in=w_in, w_out=w_out, w_w1=w_w1, w_w2=w_w2)
        sv.update(n2=n2, r=r, act=act)
        h = _mm_nn("mlp_w2", act, w_w2, _full_w_spec, T, D, DFF, tm_k4, D, DFF, "resid", extra=h)
        saved.append(sv)

    loss_blk, dh, dhb, dg_final = _loss_head(h, final_norm_g[None], tgt)
    loss = lax.psum(loss_blk[0, 0], AXES)

    tt = T
    g_mix, g_mlp = [None] * DEPTH, [None] * DEPTH
    small ={"cw": [None] * n_ev, "vec": [None] * n_ev, "pw": [None] * n_ev, "gn": [None] * n_od}
    dlb_rows = [jnp.zeros((1, D), F32) for _ in range(DEPTH)]

    def xs2(tt_, tk):
        return pl.BlockSpec((tt_, tk), lambda a, b, t: (t, a))

    def ys2(tt_, tn):
        return pl.BlockSpec((tt_, tn), lambda a, b, t: (t, b))

    def os2(tk, tn):
        return pl.BlockSpec((tk, tn), lambda a, b, t: (a, b))

    def os3(tk, tn):
        return pl.BlockSpec((None, tk, tn), lambda a, b, t: (b, a, 0))

    def dy2(tm, tn):
        return pl.BlockSpec((tm, tn), lambda i, jj, k: (i, k))

    def w_rows(tj, tn):
        return pl.BlockSpec((tj, tn), lambda i, jj, k: (jj, k))

    def w_whole(tj, tn):
        return pl.BlockSpec((tj, tn), lambda i, jj, k: (0, 0), pipeline_mode=pl.Buffered(1))

    rs_pending = []

    def rs_start(tag, mats):
        blocks = [m_ if m_.ndim == 3 else m_.reshape(N_DEV, m_.shape[0] // N_DEV, m_.shape[1]) for m_ in mats]
        lands = [_landing(b_.shape[1:], b_.dtype) for b_ in blocks]
        sems, s_thru, l_thru, tok = _push_start(f"rs_start_{tag}", blocks, lands, False, [0] * len(blocks))
        rs_pending.append((tag, s_thru, l_thru, sems[0]))
        return tok[0, 0]

    for layer in reversed(range(DEPTH)):
        j = layer // 2
        sv = saved[layer]
        da1 = _mm_nt("mlp_w2_t", dhb, sv["w_w2"], dy2, w_rows, T, DFF, D, tm_big, 512, D, "dact", extra=sv["r"])
        dw2 = _mm_tn("mlp_dw2", sv["act"], dhb, xs2, ys2, os2, (DFF, D), T, DFF, D, tt, 1024, D)
        dw1 = _mm_tn("mlp_dw1", sv["n2"], da1, xs2, ys2, os3, (N_DEV, D, n_w1), T, D, DFF, tt, D, n_w1)
        tok = rs_start(f"mlp{layer}", [dw1, dw2])
        dh, dhb, g_mlp[layer] = _mm_nt("mlp_w1_t", da1, sv["w_w1"], dy2, w_whole, T, D, DFF, tm_k4, D, DFF, "rms",
                                       extra=(sv["h1"], mlp_norm_g[layer][None] + tok, dh))
        if layer % 2 == 0:
            dyab = _mm_nt("ev_out_t", dhb, sv["w_out"], dy2, w_rows, T, D, D, tm_mid, D, D, "f32")
            dwout = _mm_tn("ev_dwout", sv["y"], dhb, xs2, ys2, os2, (D, D), T, D, D, tt, 512, D)
            du, small["cw"][j], small["vec"][j], small["pw"][j] = _ev_bwd(
                dyab, sv["yc"], sv["u"], cw_pad[j], ev_ln_g[j][None], ev_ln_b[j][None], ev_pool_w[j].astype(BF16),
                jnp.transpose(ev_pool_w[j], (0, 2, 1)).astype(BF16), ev_pool_b[j].reshape(1, -1),
                ev_pool_scale[j][None])
            nin = du.shape[1]
            dwin = _mm_tn("ev_dwin", sv["n"], du, xs2, ys2, os2, (D, nin), T, D, nin, tt, D, 512)
            dwin = jnp.transpose(dwin.reshape(D, N_DEV, nin // N_DEV), (1, 0, 2))
            tok = rs_start(f"mix{layer}", [dwin, dwout])
            dh, dhb, g_mix[layer] = _mm_nt("ev_in_t", du, sv["w_in"], dy2, w_whole, T, D, nin, tm_k4, D, nin, "rms",
                                           extra=(sv["h0"], mix_norm_g[layer][None] + tok, dh))
        else:
            dy = _mm_nt("od_out_t", dhb, sv["w_out"], dy2, w_rows, T, D, D, tm_mid, D, D, "f32")
            dwout = _mm_tn("od_dwout", sv["y"], dhb, xs2, ys2, os2, (D, D), T, D, D, tt, 512, D)
            du3, dlb_rows[layer], small["gn"][j] = _hgrn_bwd(dy, sv["o"], sv["s0"], sv["u"], lb_all[layer][None],
                                                              od_gnorm_g[j][None])
            per = D // n_in_od

            def du_t(tt_, tn):
                return pl.BlockSpec((None, tt_, tn), lambda a, b, t: (b // per, t, b % per))

            dwin = _mm_tn("od_dwin", sv["n"], du3, xs2, du_t, os3, (N_DEV, D, n_in_od), T, D, 4 * D, tt, D, n_in_od)
            tok = rs_start(f"mix{layer}", [dwin, dwout])
            dh, dhb, g_mix[layer] = _mm_nt(
                "od_in_t", du3, sv["w_in"], lambda tm, tn: pl.BlockSpec((4, tm, tn // 4), lambda i, jj, k: (0, i, 0)),
                w_whole, T, D, 4 * D, tm_k4, D, 4 * D, "rms", extra=(sv["h0"], mix_norm_g[layer][None] + tok, dh),
                parts=4)

    dmeta = dh[PAD:PAD + N_META]
    grad_x = dh[PAD + N_META:][None]
    dlb_param = _lb_bwd(lb_param, jnp.concatenate(dlb_rows, axis=0))

    pieces = [
        ("final", dg_final), ("pad", jnp.zeros((SMALL_F32_ROWS - 1, D), F32)),
        ("meta", dmeta), ("mix", jnp.concatenate(g_mix, 0)), ("mlp", jnp.concatenate(g_mlp, 0)),
        ("cw", jnp.stack([c[:CONV_WIDTH] for c in small["cw"]])), ("cb", jnp.stack([v_[0] for v_ in small["vec"]])),
        ("lng", jnp.stack([v_[1] for v_ in small["vec"]])), ("lnb", jnp.stack([v_[2] for v_ in small["vec"]])),
        ("pw", jnp.stack(small["pw"])), ("pb", jnp.stack([v_[4] for v_ in small["vec"]])),
        ("ps", jnp.stack([v_[3] for v_ in small["vec"]])), ("gn", jnp.stack([jnp.sum(g_, axis=0)[0] for g_ in small["gn"]])),
        ("lb", dlb_param),
    ]
    flat = jnp.concatenate([p.reshape(-1) for _, p in pieces])
    n_small = flat.shape[0]
    rows_small = SMALL_F32_ROWS + -(-(n_small // 1024 + 1 - SMALL_F32_ROWS) // 16) * 16
    flat = jnp.pad(flat, (0, rows_small * 1024 - n_small)).reshape(rows_small, 1024)

    sm_src = [flat[:SMALL_F32_ROWS], flat[SMALL_F32_ROWS:].astype(BF16)]
    sm_sems, sm_s, sm_l, sm_tok = _push_start("small_start", sm_src, [_landing(a_.shape, a_.dtype) for a_ in sm_src],
                                              True, [0, 0])
    recv = {}
    for tag, s_thru, l_thru, sems in rs_pending:
        got = _push_wait(f"rs_wait_{tag}", s_thru, l_thru, sems, sm_tok, False)
        layer = int(tag[3:])
        if tag.startswith("mlp"):
            recv["w1", layer], recv["w2", layer] = got
        else:
            key = "ev" if layer % 2 == 0 else "od"
            recv[key + "_in", layer // 2], recv[key + "_out", layer // 2] = got

    outs = {}
    big = {"ev_in": ("ev_w_in", ev_w_in, m_ev_w_in, v_ev_w_in), "ev_out": ("ev_w_out", ev_w_out, m_ev_w_out, v_ev_w_out),
           "od_in": ("od_w_in", od_w_in, m_od_w_in, v_od_w_in), "od_out": ("od_w_out", od_w_out, m_od_w_out, v_od_w_out),
           "w1": ("mlp_w1", mlp_w1, m_mlp_w1, v_mlp_w1), "w2": ("mlp_w2", mlp_w2, m_mlp_w2, v_mlp_w2)}
    for key, (name, w, m, v) in big.items():
        res = None
        for l in range(w.shape[0]):
            res = _adamw(recv[key, l], w, m, v, layer=l, prev=res)
        outs[name] = res

    small_params = {
        "meta": ("meta_tokens", None), "mix": ("mix_norm_g", mix_norm_g, m_mix_norm_g, v_mix_norm_g),
        "mlp": ("mlp_norm_g", mlp_norm_g, m_mlp_norm_g, v_mlp_norm_g),
        "final": ("final_norm_g", final_norm_g, m_final_norm_g, v_final_norm_g),
        "cw": ("ev_conv_w", None), "cb": ("ev_conv_b", ev_conv_b, m_ev_conv_b, v_ev_conv_b),
        "lng": ("ev_ln_g", ev_ln_g, m_ev_ln_g, v_ev_ln_g), "lnb": ("ev_ln_b", ev_ln_b, m_ev_ln_b, v_ev_ln_b),
        "pw": ("ev_pool_w", ev_pool_w, m_ev_pool_w, v_ev_pool_w), "pb": ("ev_pool_b", ev_pool_b, m_ev_pool_b, v_ev_pool_b),
        "ps": ("ev_pool_scale", ev_pool_scale, m_ev_pool_scale, v_ev_pool_scale),
        "gn": ("od_gnorm_g", od_gnorm_g, m_od_gnorm_g, v_od_gnorm_g), "lb": ("lb_param", lb_param, m_lb_param, v_lb_param),
    }
    csh = ev_conv_w.shape[2]
    msh = meta_tokens.shape[1]

    def packed(which):
        parts = []
        for key, g_ in pieces:
            ent = small_params.get(key)
            if key == "pad":
                full = g_
            elif key == "meta":
                src = (meta_tokens, m_meta_tokens, v_meta_tokens)[which]
                full = lax.dynamic_update_slice(jnp.zeros((N_META, D), F32), src, (0, dev * msh))
            elif key == "cw":
                src = (ev_conv_w, m_ev_conv_w, v_ev_conv_w)[which]
                full = lax.dynamic_update_slice(jnp.zeros(g_.shape, F32), src, (0, 0, dev * csh))
            else:
                full = ent[1 + which]
            parts.append(full.reshape(-1))
        f = jnp.concatenate(parts)
        return jnp.pad(f, (0, rows_small * 1024 - n_small)).reshape(rows_small, 1024)

    got_f32, got_bf16 = _push_wait("small_wait", sm_s, sm_l, sm_sems[0], outs["mlp_w2"][0], True)
    recv_small = jnp.concatenate([got_f32, got_bf16.astype(F32)], axis=1)
    sres = [r_[0] for r_ in _adamw(recv_small, packed(0)[None], packed(1)[None], packed(2)[None])]
    off = 0
    for key, g_ in pieces:
        size = g_.size
        vals = [r_.reshape(-1)[off:off + size].reshape(g_.shape) for r_ in sres]
        off += size
        if key == "pad":
            continue
        name = small_params[key][0]
        if key == "meta":
            vals = [lax.dynamic_slice(v_, (0, dev * msh), (N_META, msh)) for v_ in vals]
        elif key == "cw":
            vals = [lax.dynamic_slice(v_, (0, 0, dev * csh), v_.shape[:2] + (csh,)) for v_ in vals]
        else:
            vals = [v_.reshape(small_params[key][1].shape) for v_ in vals]
        outs[name] = vals

    names = ["meta_tokens", "mix_norm_g", "mlp_norm_g", "final_norm_g", "ev_w_in", "ev_conv_w", "ev_conv_b", "ev_ln_g",
             "ev_ln_b", "ev_pool_w", "ev_pool_b", "ev_pool_scale", "ev_w_out", "od_w_in", "od_gnorm_g", "od_w_out",
             "lb_param", "mlp_w1", "mlp_w2"]
    result = [loss, grad_x]
    for k in range(4):
        result += [outs[nm][k] for nm in names]
    return tuple(result)
```

```python
import functools

import jax
import jax.numpy as jnp
from jax import lax
from jax.experimental import pallas as pl
from jax.experimental.pallas import tpu as pltpu

F32 = jnp.float32
BF16 = jnp.bfloat16

N_DEV = 8
N_META = 16
CHUNK = 64
PAD = CHUNK - N_META
SUB = 16
HEAD = 128
CONV_WIDTH = 31
HALO = 32
POOL_WINDOWS = (2, 4, 8, 16)
EPS = 1e-6
NEG = -1e30
ADAM_LR, ADAM_B1, ADAM_B2, ADAM_EPS, ADAM_WD, ADAM_STEP = 0.001, 0.9, 0.999, 1e-08, 0.01, 10
VMEM_LIMIT = 56 * 1024 * 1024
EV_ROWS = 416
HGRN_ROWS = 832
MM_ROWS_BIG = 2080
MM_ROWS_MID = 1040
MM_ROWS_K4 = 416
SMALL_F32_ROWS = 8
HGRN_HEADS_FWD = 8
HGRN_HEADS_BWD = 2
MESH = pl.DeviceIdType.MESH
AXES = ("x", "y", "c")
ANY = pl.BlockSpec(memory_space=pl.ANY)


def _cp(*sem):
    return pltpu.CompilerParams(dimension_semantics=sem, vmem_limit_bytes=VMEM_LIMIT)


def _tile(n, cap, mult):
    best = None
    for d in range(mult, min(n, cap) + 1, mult):
        if n % d == 0:
            best = d
    assert best is not None, (n, cap, mult)
    return best


def _nt(a, b):
    return lax.dot_general(a, b, (((1,), (1,)), ((), ())), preferred_element_type=F32)


def _tn(a, b):
    return lax.dot_general(a, b, (((0,), (0,)), ((), ())), preferred_element_type=F32)


def _nn(a, b):
    return jnp.dot(a, b, preferred_element_type=F32)


def _r16(x):
    return x.astype(BF16).astype(F32)


def _row_ids(base, n):
    return base + lax.broadcasted_iota(jnp.int32, (n, 1), 0)


def _dsilu(x, s):
    return s * (1.0 + x * (1.0 - s))


def _loss_head(h, g, tgt):
    T, D = h.shape
    tm = _tile(T, MM_ROWS_MID, 16)
    first_x = PAD + N_META

    def body(h_ref, g_ref, t_ref, loss_ref, dh_ref, dhb_ref, dg_ref):
        i = pl.program_id(0)
        x = h_ref[...]
        r = lax.rsqrt(jnp.mean(x * x, axis=-1, keepdims=True) + EPS)
        xh = x * r
        gv = g_ref[...]
        out = xh * gv
        valid = _row_ids(i * tm, tm) >= first_x
        e = jnp.where(valid, out - t_ref[...], 0.0)
        dout = e * (1.0 / D)
        dxh = dout * gv
        dx = r * (dxh - xh * jnp.mean(dxh * xh, axis=-1, keepdims=True))
        dh_ref[...] = dx
        dhb_ref[...] = dx.astype(BF16)

        @pl.when(i == 0)
        def _():
            dg_ref[...] = jnp.zeros_like(dg_ref)
            loss_ref[...] = jnp.zeros_like(loss_ref)

        dg_ref[...] += jnp.sum(dout * xh, axis=0, keepdims=True)
        loss_ref[...] += 0.5 * jnp.sum(jnp.mean(e * e, axis=-1, keepdims=True))

    row = pl.BlockSpec((tm, D), lambda i: (i, 0))
    vec = pl.BlockSpec((1, D), lambda i: (0, 0))
    return pl.pallas_call(
        body, grid=(T // tm,),
        in_specs=[row, vec, row],
        out_specs=[pl.BlockSpec((8, 128), lambda i: (0, 0)), row, row, vec],
        out_shape=[jax.ShapeDtypeStruct((8, 128), F32), jax.ShapeDtypeStruct((T, D), F32),
                   jax.ShapeDtypeStruct((T, D), BF16), jax.ShapeDtypeStruct((1, D), F32)],
        compiler_params=_cp("arbitrary"), name="loss_head")(h, g, tgt)


def _mm_nn(name, a, w, w_spec, M, N, K, tm, tn, tk, mode, extra=None, a_spec=None):
    nk = K // tk
    if a_spec is None:
        a_spec = pl.BlockSpec((tm, tk), lambda i, j, k: (i, k))
    o_spec = pl.BlockSpec((tm, tn), lambda i, j, k: (i, j))

    def body(*refs):
        if mode == "resid":
            a_ref, w_ref, e_ref = refs[:3]
            outs = refs[3:]
        else:
            a_ref, w_ref = refs[:2]
            outs = refs[2:]
        acc_ref = outs[-1] if nk > 1 else None
        part = _nn(a_ref[...], w_ref[...])

        def finish(acc):
            if mode == "f32":
                outs[0][...] = acc
            elif mode == "relu2":
                r = jnp.maximum(acc, 0.0)
                outs[0][...] = r.astype(BF16)
                outs[1][...] = (r * r).astype(BF16)
            else:
                keep = _row_ids(pl.program_id(0) * tm, tm) >= PAD
                outs[0][...] = jnp.where(keep, e_ref[...] + acc, 0.0)

        if nk == 1:
            finish(part)
        else:
            k = pl.program_id(2)

            @pl.when(k == 0)
            def _():
                acc_ref[...] = part

            @pl.when(k > 0)
            def _():
                acc_ref[...] += part

            @pl.when(k == nk - 1)
            def _():
                finish(acc_ref[...])

    in_specs = [a_spec, w_spec(tk, tn)]
    args = [a, w]
    if mode == "resid":
        in_specs.append(o_spec)
        args.append(extra)
    if mode == "relu2":
        out_specs = [o_spec, o_spec]
        out_shape = [jax.ShapeDtypeStruct((M, N), BF16)] * 2
    else:
        out_specs = [o_spec]
        out_shape = [jax.ShapeDtypeStruct((M, N), F32)]
    scratch = [pltpu.VMEM((tm, tn), F32)] if nk > 1 else []
    res = pl.pallas_call(
        body, grid=(M // tm, N // tn, nk), in_specs=in_specs, out_specs=out_specs, out_shape=out_shape,
        scratch_shapes=scratch, compiler_params=_cp("parallel", "parallel", "arbitrary"), name=name)(*args)
    return res if mode == "relu2" else res[0]


def _mm_rms_nn(name, h, g, w, tm, tn, mode):
    M, K = h.shape
    N = w.shape[1]

    def body(h_ref, g_ref, w_ref, n_ref, *outs):
        @pl.when(pl.program_id(1) == 0)
        def _():
            ch = _tile(tm, 256, 16)

            def chunk(c, carry):
                rows = pl.ds(pl.multiple_of(c * ch, ch), ch)
                x = h_ref[rows, :]
                r = lax.rsqrt(jnp.mean(x * x, axis=-1, keepdims=True) + EPS)
                n_ref[rows, :] = ((x * r) * g_ref[...]).astype(BF16)
                return carry

            lax.fori_loop(0, tm // ch, chunk, 0)

        acc = _nn(n_ref[...], w_ref[...])
        if mode == "f32":
            outs[0][...] = acc
        else:
            r = jnp.maximum(acc, 0.0)
            outs[0][...] = r.astype(BF16)
            outs[1][...] = (r * r).astype(BF16)

    row = pl.BlockSpec((tm, K), lambda i, j: (i, 0))
    o_spec = pl.BlockSpec((tm, tn), lambda i, j: (i, j))
    n_out = 1 if mode == "f32" else 2
    return pl.pallas_call(
        body, grid=(M // tm, N // tn),
        in_specs=[row, pl.BlockSpec((1, K), lambda i, j: (0, 0)), pl.BlockSpec((K, tn), lambda i, j: (0, j))],
        out_specs=[row] + [o_spec] * n_out,
        out_shape=[jax.ShapeDtypeStruct((M, K), BF16)] + [jax.ShapeDtypeStruct((M, N), F32 if mode == "f32" else BF16)] * n_out,
        compiler_params=_cp("parallel", "arbitrary"), name=name)(h, g, w)


def _mm_nt(name, dy, w, dy_spec, w_spec, M, J, N, tm, tj, tn, mode, extra=None, parts=1):
    nk = N // tn
    o_spec = pl.BlockSpec((tm, tj), lambda i, j, k: (i, j))
    n_extra = {"f32": 0, "dact": 1, "rms": 3}[mode]
    if mode == "rms":
        assert nk == 1 and tj == J

    def body(*refs):
        dy_ref, w_ref = refs[:2]
        ex = refs[2:2 + n_extra]
        outs = refs[2 + n_extra:]
        acc_ref = outs[-1] if nk > 1 else None
        if parts == 1:
            part = _nt(dy_ref[...], w_ref[...])
        else:
            wq = tn // parts
            part = _nt(dy_ref[0], w_ref[:, 0:wq])
            for q in range(1, parts):
                part = part + _nt(dy_ref[q], w_ref[:, q * wq:(q + 1) * wq])

        def finish(acc):
            if mode == "f32":
                outs[0][...] = acc
            elif mode == "dact":
                outs[0][...] = (acc * (2.0 * ex[0][...].astype(F32))).astype(BF16)
            else:
                h_ref, g_ref, dres_ref = ex
                dh_ref, dhb_ref, dg_ref = outs[:3]
                i = pl.program_id(0)

                @pl.when(i == 0)
                def _():
                    dg_ref[...] = jnp.zeros_like(dg_ref)

                ch = _tile(tm, 256, 16)
                for c0 in range(0, tm, ch):
                    a_c = acc[c0:c0 + ch]
                    x = h_ref[c0:c0 + ch, :]
                    r = lax.rsqrt(jnp.mean(x * x, axis=-1, keepdims=True) + EPS)
                    xh = x * r
                    dxh = a_c * g_ref[...]
                    dx = r * (dxh - xh * jnp.mean(dxh * xh, axis=-1, keepdims=True))
                    keep = _row_ids(i * tm + c0, ch) >= PAD
                    dh = jnp.where(keep, dres_ref[c0:c0 + ch, :] + dx, 0.0)
                    dh_ref[c0:c0 + ch, :] = dh
                    dhb_ref[c0:c0 + ch, :] = dh.astype(BF16)
                    dg_ref[...] += jnp.sum(a_c * xh, axis=0, keepdims=True)

        if nk == 1:
            finish(part)
        else:
            k = pl.program_id(2)

            @pl.when(k == 0)
            def _():
                acc_ref[...] = part

            @pl.when(k > 0)
            def _():
                acc_ref[...] += part

            @pl.when(k == nk - 1)
            def _():
                finish(acc_ref[...])

    in_specs = [dy_spec(tm, tn), w_spec(tj, tn)]
    args = [dy, w]
    scratch = [pltpu.VMEM((tm, tj), F32)] if nk > 1 else []
    if mode == "rms":
        vec = pl.BlockSpec((1, J), lambda i, j, k: (0, 0))
        h, g, dres = extra
        res = pl.pallas_call(
            body, grid=(M // tm, 1, 1), in_specs=in_specs + [o_spec, vec, o_spec], out_specs=[o_spec, o_spec, vec],
            out_shape=[jax.ShapeDtypeStruct((M, J), F32), jax.ShapeDtypeStruct((M, J), BF16),
                       jax.ShapeDtypeStruct((1, J), F32)],
            compiler_params=_cp("arbitrary", "arbitrary", "arbitrary"), name=name)(*args, h, g, dres)
        return res
    if mode == "dact":
        in_specs.append(o_spec)
        args.append(extra)
    return pl.pallas_call(
        body, grid=(M // tm, J // tj, nk), in_specs=in_specs, out_specs=[o_spec],
        out_shape=[jax.ShapeDtypeStruct((M, J), BF16 if mode == "dact" else F32)],
        scratch_shapes=scratch, compiler_params=_cp("parallel", "parallel", "arbitrary"), name=name)(*args)[0]


def _mm_tn(name, x, dy, x_spec, dy_spec, o_spec, o_shape, T, K, N, tt, tk, tn):
    nt = T // tt

    def body(x_ref, dy_ref, o_ref, *acc):
        part = _tn(x_ref[...], dy_ref[...])
        if nt == 1:
            o_ref[...] = part.astype(BF16)
            return
        acc_ref = acc[0]
        t = pl.program_id(2)

        @pl.when(t == 0)
        def _():
            acc_ref[...] = part

        @pl.when(t > 0)
        def _():
            acc_ref[...] += part

        @pl.when(t == nt - 1)
        def _():
            o_ref[...] = acc_ref[...].astype(BF16)

    return pl.pallas_call(
        body, grid=(K // tk, N // tn, nt), in_specs=[x_spec(tt, tk), dy_spec(tt, tn)], out_specs=o_spec(tk, tn),
        out_shape=jax.ShapeDtypeStruct(o_shape, BF16), scratch_shapes=[pltpu.VMEM((tk, tn), F32)] if nt > 1 else [],
        compiler_params=_cp("parallel", "parallel", "arbitrary"), name=name)(x, dy)


def _pool_counts(base, n, w):
    pos = _row_ids(base, n) - PAD
    return jnp.clip(pos + 1, 1, w).astype(F32)


def _shifted_copies(buf, rows):
    buf[0, rows:rows + 8, :] = jnp.zeros((8, buf.shape[2]), F32)

    def blk(s, carry):
        b = pl.multiple_of(s * HALO, HALO)
        win = buf[0, pl.ds(b, HALO + 8), :]
        for r in range(1, 8):
            buf[r, pl.ds(b, HALO), :] = win[r:r + HALO]
        return carry

    lax.fori_loop(0, rows // HALO, blk, 0)


def _ev_fwd(u, cw, cb, lg, lb, pw, pb, ps):
    T = u.shape[0]
    C = 512
    tm = _tile(T, EV_ROWS, HALO)
    nsub = tm // HALO
    hb = tm // HALO

    def body(val_ref, gate_ref, pin_ref, valh_ref, gateh_ref, pinh_ref, cw_ref, cb_ref, lg_ref, lb_ref, pw_ref,
             pb_ref, ps_ref, yab_ref, yc_ref, a_ext, p_ext, d_buf):
        i = pl.program_id(0)
        nf = (i > 0).astype(F32)
        a_ext[0, 0:HALO, :] = valh_ref[...] * jax.nn.sigmoid(gateh_ref[...]) * nf
        a_ext[0, HALO:HALO + tm, :] = val_ref[...] * jax.nn.sigmoid(gate_ref[...])
        p_ext[0:HALO, :] = pinh_ref[...] * nf
        p_ext[HALO:, :] = pin_ref[...]
        _shifted_copies(a_ext, tm + HALO)

        def sub(s, carry):
            base = pl.multiple_of(s * HALO, HALO)
            acc = jnp.zeros((HALO, C), F32) + cb_ref[...]
            for j in range(CONV_WIDTH):
                off = 2 + j
                acc = acc + cw_ref[pl.ds(j, 1), :] * a_ext[off % 8, pl.ds(pl.multiple_of(base + off // 8 * 8, 8), HALO), :]
            yc_ref[pl.ds(base, HALO), :] = acc
            mu = jnp.mean(acc, axis=-1, keepdims=True)
            yc = acc - mu
            rstd = lax.rsqrt(jnp.mean(yc * yc, axis=-1, keepdims=True) + EPS)
            z = (yc * rstd) * lg_ref[...] + lb_ref[...]
            yab_ref[pl.ds(base, HALO), 0:C] = (z * jax.nn.sigmoid(z)).astype(BF16)
            pwin = p_ext[pl.ds(base, 2 * HALO), :]
            for gi, w in enumerate(POOL_WINDOWS):
                lo, hi = gi * HEAD, (gi + 1) * HEAD
                x = pwin[HALO:, lo:hi]
                tot = x
                for k in range(1, w):
                    tot = tot + pwin[HALO - k:2 * HALO - k, lo:hi]
                cnt = _pool_counts(i * tm + base, HALO, w)
                d_buf[pl.ds(base, HALO), lo:hi] = (tot / cnt - x).astype(BF16)
            return carry

        lax.fori_loop(0, nsub, sub, 0, unroll=2)
        for gi in range(len(POOL_WINDOWS)):
            lo, hi = gi * HEAD, (gi + 1) * HEAD
            y = _nn(d_buf[:, lo:hi], pw_ref[gi]) + pb_ref[:, lo:hi]
            yab_ref[:, C + lo:C + hi] = (y * ps_ref[:, lo:hi]).astype(BF16)

    def main(c):
        return pl.BlockSpec((tm, C), lambda i: (i, c))

    def halo(c):
        return pl.BlockSpec((HALO, C), lambda i: (jnp.maximum(i * hb - 1, 0), c))

    vec = pl.BlockSpec((1, C), lambda i: (0, 0))
    return pl.pallas_call(
        body, grid=(T // tm,),
        in_specs=[main(0), main(1), main(2), halo(0), halo(1), halo(2),
                  pl.BlockSpec((32, C), lambda i: (0, 0)), vec, vec, vec,
                  pl.BlockSpec((4, HEAD, HEAD), lambda i: (0, 0, 0)), vec, vec],
        out_specs=[pl.BlockSpec((tm, 2 * C), lambda i: (i, 0)), pl.BlockSpec((tm, C), lambda i: (i, 0))],
        out_shape=[jax.ShapeDtypeStruct((T, 2 * C), BF16), jax.ShapeDtypeStruct((T, C), F32)],
        scratch_shapes=[pltpu.VMEM((8, tm + HALO + 8, C), F32), pltpu.VMEM((tm + HALO, C), F32),
                        pltpu.VMEM((tm, C), BF16)],
        compiler_params=_cp("parallel"), name="ev_fwd")(u, u, u, u, u, u, cw, cb, lg, lb, pw, pb, ps)


def _ev_bwd(dyab, yc, u, cw, lg, lb, pw, pwt, pb, ps):
    T = u.shape[0]
    C = 512
    tm = _tile(T, EV_ROWS, HALO)
    nsub = tm // HALO
    hb = tm // HALO
    nblk = T // tm
    E = tm + HALO

    def body(dya_ref, dyb_ref, dyah_ref, dybh_ref, yc_ref, ych_ref, val_ref, gate_ref, pin_ref, valh_ref, gateh_ref,
             pinh_ref, cw_ref, lg_ref, lb_ref, pw_ref, pwt_ref, pb_ref, ps_ref,
             du_ref, dcw_ref, dvec_ref, dpw_ref,
             dy_ext, a_ext, p_ext, ddc_ext, dd_buf, d_buf, dpre_buf, dcw_acc, vec_acc):
        i = pl.program_id(0)
        nf = (i > 0).astype(F32)
        nl = (i < nblk - 1).astype(F32)

        @pl.when(i == 0)
        def _():
            dcw_ref[...] = jnp.zeros_like(dcw_ref)
            dvec_ref[...] = jnp.zeros_like(dvec_ref)
            dpw_ref[...] = jnp.zeros_like(dpw_ref)

        dcw_acc[...] = jnp.zeros_like(dcw_acc)
        vec_acc[...] = jnp.zeros_like(vec_acc)
        a_ext[0, 0:HALO, :] = valh_ref[...] * jax.nn.sigmoid(gateh_ref[...]) * nf
        a_ext[0, HALO:E, :] = val_ref[...] * jax.nn.sigmoid(gate_ref[...])
        p_ext[0:HALO, :] = pinh_ref[...] * nf
        p_ext[HALO:, :] = pin_ref[...]
        _shifted_copies(a_ext, E)

        def ln_bwd(y, dya, main):
            mu = jnp.mean(y, axis=-1, keepdims=True)
            ycen = y - mu
            rstd = lax.rsqrt(jnp.mean(ycen * ycen, axis=-1, keepdims=True) + EPS)
            yh = ycen * rstd
            z = yh * lg_ref[...] + lb_ref[...]
            sz = jax.nn.sigmoid(z)
            dz = dya * _dsilu(z, sz)
            dyh = dz * lg_ref[...]
            dy = rstd * (dyh - jnp.mean(dyh, axis=-1, keepdims=True) - yh * jnp.mean(dyh * yh, axis=-1, keepdims=True))
            if main:
                vec_acc[1] += jnp.sum((dz * yh).reshape(HALO // 8, 8, C), axis=0)
                vec_acc[2] += jnp.sum(dz.reshape(HALO // 8, 8, C), axis=0)
                vec_acc[0] += jnp.sum(dy.reshape(HALO // 8, 8, C), axis=0)
            return dy

        def pool_dd(dyb, base, main):
            dpre = dyb * ps_ref[...]
            for gi, w in enumerate(POOL_WINDOWS):
                lo, hi = gi * HEAD, (gi + 1) * HEAD
                dd = _nn(dpre[:, lo:hi].astype(BF16), pwt_ref[gi])
                cnt = _pool_counts(i * tm + base, HALO, w)
                ddc_ext[pl.ds(base, HALO), lo:hi] = dd / cnt
                if main:
                    dd_buf[pl.ds(base, HALO), lo:hi] = dd
            if main:
                dpre_buf[pl.ds(base, HALO), :] = dpre.astype(BF16)
                vec_acc[4] += jnp.sum(dpre.reshape(HALO // 8, 8, C), axis=0)

        def p1(s, carry):
            base = pl.multiple_of(s * HALO, HALO)
            dy_ext[0, pl.ds(base, HALO), :] = ln_bwd(yc_ref[pl.ds(base, HALO), :], dya_ref[pl.ds(base, HALO), :], True)
            pool_dd(dyb_ref[pl.ds(base, HALO), :], base, True)
            return carry

        lax.fori_loop(0, nsub, p1, 0, unroll=2)
        dy_ext[0, tm:E, :] = ln_bwd(ych_ref[...], dyah_ref[...], False) * nl
        _shifted_copies(dy_ext, E)
        dpre_h = dybh_ref[...] * ps_ref[...] * nl
        for gi, w in enumerate(POOL_WINDOWS):
            lo, hi = gi * HEAD, (gi + 1) * HEAD
            dd = _nn(dpre_h[:, lo:hi].astype(BF16), pwt_ref[gi])
            ddc_ext[tm:, lo:hi] = dd / _pool_counts(i * tm + tm, HALO, w)

        def p2(s, carry):
            base = pl.multiple_of(s * HALO, HALO)
            dy_m = dy_ext[0, pl.ds(base, HALO), :]
            da = jnp.zeros((HALO, C), F32)
            for j in range(CONV_WIDTH):
                sh = CONV_WIDTH - 1 - j
                off = 2 + j
                da = da + cw_ref[pl.ds(j, 1), :] * dy_ext[sh % 8, pl.ds(pl.multiple_of(base + sh // 8 * 8, 8), HALO), :]
                a_j = a_ext[off % 8, pl.ds(pl.multiple_of(base + off // 8 * 8, 8), HALO), :]
                dcw_acc[j] += jnp.sum((dy_m * a_j).reshape(HALO // 8, 8, C), axis=0)
            v = val_ref[pl.ds(base, HALO), :]
            g = gate_ref[pl.ds(base, HALO), :]
            sg = jax.nn.sigmoid(g)
            du_ref[pl.ds(base, HALO), 0:C] = (da * sg).astype(BF16)
            du_ref[pl.ds(base, HALO), C:2 * C] = (da * v * sg * (1.0 - sg)).astype(BF16)
            pwin = p_ext[pl.ds(base, 2 * HALO), :]
            cwin = ddc_ext[pl.ds(base, 2 * HALO), :]
            for gi, w in enumerate(POOL_WINDOWS):
                lo, hi = gi * HEAD, (gi + 1) * HEAD
                x = pwin[HALO:, lo:hi]
                tot = x
                back = cwin[0:HALO, lo:hi]
                for k in range(1, w):
                    tot = tot + pwin[HALO - k:2 * HALO - k, lo:hi]
                    back = back + cwin[k:k + HALO, lo:hi]
                cnt = _pool_counts(i * tm + base, HALO, w)
                d_buf[pl.ds(base, HALO), lo:hi] = (tot / cnt - x).astype(BF16)
                du_ref[pl.ds(base, HALO), 2 * C + lo:2 * C + hi] = (back - dd_buf[pl.ds(base, HALO), lo:hi]).astype(BF16)
            return carry

        lax.fori_loop(0, nsub, p2, 0)
        for gi in range(len(POOL_WINDOWS)):
            lo, hi = gi * HEAD, (gi + 1) * HEAD
            pre = _nn(d_buf[:, lo:hi], pw_ref[gi]) + pb_ref[:, lo:hi]
            vec_acc[3, :, lo:hi] += jnp.sum((dyb_ref[:, lo:hi] * pre).reshape(tm // 8, 8, HEAD), axis=0)
            dpw_ref[gi] += _tn(d_buf[:, lo:hi], dpre_buf[:, lo:hi])
        for j in range(CONV_WIDTH):
            dcw_ref[pl.ds(j, 1), :] += jnp.sum(dcw_acc[j], axis=0, keepdims=True)
        for r in range(5):
            dvec_ref[pl.ds(r, 1), :] += jnp.sum(vec_acc[r], axis=0, keepdims=True)

    def main(c, width=C):
        return pl.BlockSpec((tm, width), lambda i: (i, c))

    def prev(c):
        return pl.BlockSpec((HALO, C), lambda i: (jnp.maximum(i * hb - 1, 0), c))

    def nxt(c):
        return pl.BlockSpec((HALO, C), lambda i: (jnp.minimum((i + 1) * hb, T // HALO - 1), c))

    vec = pl.BlockSpec((1, C), lambda i: (0, 0))
    mat = pl.BlockSpec((4, HEAD, HEAD), lambda i: (0, 0, 0))
    return pl.pallas_call(
        body, grid=(nblk,),
        in_specs=[main(0), main(1), nxt(0), nxt(1), main(0), nxt(0), main(0), main(1), main(2), prev(0), prev(1),
                  prev(2), pl.BlockSpec((32, C), lambda i: (0, 0)), vec, vec, mat, mat, vec, vec],
        out_specs=[pl.BlockSpec((tm, 3 * C), lambda i: (i, 0)), pl.BlockSpec((32, C), lambda i: (0, 0)),
                   pl.BlockSpec((8, C), lambda i: (0, 0)), mat],
        out_shape=[jax.ShapeDtypeStruct((T, 3 * C), BF16), jax.ShapeDtypeStruct((32, C), F32),
                   jax.ShapeDtypeStruct((8, C), F32), jax.ShapeDtypeStruct((4, HEAD, HEAD), F32)],
        scratch_shapes=[pltpu.VMEM((8, E + 8, C), F32), pltpu.VMEM((8, E + 8, C), F32), pltpu.VMEM((E, C), F32),
                        pltpu.VMEM((E, C), F32), pltpu.VMEM((tm, C), F32), pltpu.VMEM((tm, C), BF16),
                        pltpu.VMEM((tm, C), BF16), pltpu.VMEM((32, 8, C), F32), pltpu.VMEM((8, 8, C), F32)],
        compiler_params=_cp("arbitrary"), name="ev_bwd")(
            dyab, dyab, dyab, dyab, yc, yc, u, u, u, u, u, u, cw, lg, lb, pw, pwt, pb, ps)


def _cumsum_rows(x, reverse=False):
    n = x.shape[0]
    rid = lax.broadcasted_iota(jnp.int32, (n, 1), 0)
    k = 1
    while k < n:
        if reverse:
            sh = jnp.where(rid < n - k, pltpu.roll(x, n - k, 0), 0.0)
        else:
            sh = jnp.where(rid >= k, pltpu.roll(x, k, 0), 0.0)
        x = x + sh
        k *= 2
    return x


def _hgrn_gates(qr, fr, lbv):
    sq = jax.nn.sigmoid(qr)
    sg = jax.nn.sigmoid(fr)
    fg = lbv + (1.0 - lbv) * sg
    return qr * sq, sq, sg, fg, 1.0 - fg, jnp.log(fg)


def _hgrn_fwd(u, lbv, gn):
    T = u.shape[0]
    H = 8
    RB = _tile(T, HGRN_ROWS, CHUNK)
    NC = RB // CHUNK
    NS = CHUNK // SUB

    HP = HGRN_HEADS_FWD
    W = HP * HEAD

    def body(q_ref, f_ref, i_ref, g_ref, lb_ref, gn_ref, y_ref, o_ref, s0_ref, st, qs, ks, bs, vs, os_):
        rb = pl.program_id(1)

        @pl.when(rb == 0)
        def _():
            st[...] = jnp.zeros_like(st)

        t8 = lax.broadcasted_iota(jnp.int32, (8, 1), 0)

        def head(hh, c, rows):
            sl = slice(hh * HEAD, (hh + 1) * HEAD)
            q, _, _, _, kk, lf = _hgrn_gates(q_ref[rows, sl], f_ref[rows, sl], lb_ref[:, sl])
            v = i_ref[rows, sl]
            b = _cumsum_rows(lf)
            qs[hh] = q
            ks[hh] = kk
            bs[hh] = b
            vs[hh] = v
            st0 = st[hh]
            s0_ref[hh, c] = st0
            os_[hh] = _nt((q * jnp.exp(b)).astype(BF16), st0.astype(BF16))
            for I in range(NS):
                lo = I * SUB
                qI = qs[hh, lo:lo + SUB, :]
                bI = bs[hh, lo:lo + SUB, :]
                oI = jnp.zeros((SUB, HEAD), F32)
                if I > 0:
                    bprev = bs[hh, pl.ds(lo - 1, 1), :]
                    qt = _r16(qI * jnp.exp(bI - bprev))
                    kt = _r16(ks[hh, 0:lo, :] * jnp.exp(bprev - bs[hh, 0:lo, :]))
                    A = _nt(qt, kt)
                    oI = oI + _nn(_r16(A), _r16(vs[hh, 0:lo, :]))
                od = [jnp.zeros((8, HEAD), F32) for _ in range(SUB // 8)]
                for s in range(SUB):
                    row = pl.ds(lo + s, 1)
                    brow, krow, vrow = bs[hh, row, :], ks[hh, row, :], vs[hh, row, :]
                    for ti in range(SUB // 8):
                        o8 = 8 * ti
                        if s > o8 + 7:
                            continue
                        d = bI[o8:o8 + 8] - brow
                        if s > o8:
                            d = jnp.where(t8 >= s - o8, d, NEG)
                        col = jnp.sum(qI[o8:o8 + 8] * jnp.exp(d) * krow, axis=1, keepdims=True)
                        od[ti] = od[ti] + col * vrow
                os_[hh, lo:lo + SUB, :] += oI + jnp.concatenate(od, axis=0)
            blast = bs[hh, pl.ds(CHUNK - 1, 1), :]
            kh = kk * jnp.exp(blast - b)
            st[hh] = st0 * jnp.exp(blast) + _tn(v.astype(BF16), kh.astype(BF16))
            o = os_[hh]
            o_ref[rows, sl] = o
            rr = lax.rsqrt(jnp.mean(o * o, axis=-1, keepdims=True) + EPS)
            gr = g_ref[rows, sl]
            y_ref[rows, sl] = (((o * rr) * gn_ref[...]) * (gr * jax.nn.sigmoid(gr))).astype(BF16)

        def chunk(c, carry):
            rows = pl.ds(pl.multiple_of(c * CHUNK, CHUNK), CHUNK)
            for hh in range(HP):
                head(hh, c, rows)
            return carry

        lax.fori_loop(0, NC, chunk, 0)

    def blk(q):
        return pl.BlockSpec((RB, W), lambda h, r: (r, q * (H // HP) + h))

    sc = lambda: pltpu.VMEM((HP, CHUNK, HEAD), F32)
    return pl.pallas_call(
        body, grid=(H // HP, T // RB),
        in_specs=[blk(0), blk(1), blk(2), blk(3), pl.BlockSpec((1, W), lambda h, r: (0, h)),
                  pl.BlockSpec((1, HEAD), lambda h, r: (0, 0))],
        out_specs=[pl.BlockSpec((RB, W), lambda h, r: (r, h)), pl.BlockSpec((RB, W), lambda h, r: (r, h)),
                   pl.BlockSpec((HP, NC, HEAD, HEAD), lambda h, r: (h, r, 0, 0))],
        out_shape=[jax.ShapeDtypeStruct((T, H * HEAD), BF16), jax.ShapeDtypeStruct((T, H * HEAD), F32),
                   jax.ShapeDtypeStruct((H, T // CHUNK, HEAD, HEAD), F32)],
        scratch_shapes=[pltpu.VMEM((HP, HEAD, HEAD), F32), sc(), sc(), sc(), sc(), sc()],
        compiler_params=_cp("parallel", "arbitrary"), name="hgrn_fwd")(u, u, u, u, lbv, gn)


def _hgrn_bwd(dy, o, s0, u, lbv, gn):
    T = u.shape[0]
    H = 8
    RB = _tile(T, HGRN_ROWS, CHUNK)
    NB = T // RB
    NC = RB // CHUNK
    NS = CHUNK // SUB

    def body(q_ref, f_ref, i_ref, g_ref, lb_ref, gn_ref, o_ref, dy_ref, s0_ref, du_ref, dlb_ref, dgn_ref,
             dst, qs, ks, bs, vs, dos, dqs, dks, dki, dvs, dbs):
        rb = pl.program_id(1)

        @pl.when(rb == 0)
        def _():
            dst[...] = jnp.zeros_like(dst)
            dlb_ref[...] = jnp.zeros_like(dlb_ref)
            dgn_ref[...] = jnp.zeros_like(dgn_ref)

        t8 = lax.broadcasted_iota(jnp.int32, (8, 1), 0)
        lane = lax.broadcasted_iota(jnp.int32, (8, HEAD), 1)
        gnv = gn_ref[...]

        def head(hh, c, rows):
            sl = slice(hh * HEAD, (hh + 1) * HEAD)
            lbv_ = lb_ref[:, sl]
            qr = q_ref[rows, sl]
            q, sq, sg, fg, kk, lf = _hgrn_gates(qr, f_ref[rows, sl], lbv_)
            v = i_ref[rows, sl]
            gr = g_ref[rows, sl]
            b = _cumsum_rows(lf)
            eb = jnp.exp(b)
            ov = o_ref[rows, sl]
            dyv = dy_ref[rows, sl]
            rr = lax.rsqrt(jnp.mean(ov * ov, axis=-1, keepdims=True) + EPS)
            oh = ov * rr
            gs = jax.nn.sigmoid(gr)
            dgr = dyv * (oh * gnv) * _dsilu(gr, gs)
            dnrm = dyv * (gr * gs)
            dgn_ref[hh] += jnp.sum(dnrm * oh, axis=0, keepdims=True)
            t1 = dnrm * gnv
            do = rr * (t1 - oh * jnp.mean(t1 * oh, axis=-1, keepdims=True))
            qs[hh] = q
            ks[hh] = kk
            bs[hh] = b
            vs[hh] = v
            dos[hh] = do
            st0 = s0_ref[hh, c]
            dS = dst[hh]
            do_b = do.astype(BF16)
            blast = bs[hh, pl.ds(CHUNK - 1, 1), :]
            elast = jnp.exp(blast - b)
            dq_inter = _nn(do_b, st0.astype(BF16)) * eb
            dqs[hh] = dq_inter
            dbs[hh] = q * dq_inter
            kh = kk * elast
            dvs[hh] = _nt(kh.astype(BF16), dS.astype(BF16))
            dk_inter = _nn(v.astype(BF16), dS.astype(BF16)) * elast
            dki[hh] = dk_inter
            dks[hh] = jnp.zeros((CHUNK, HEAD), F32)
            for I in range(NS):
                lo = I * SUB
                qI = qs[hh, lo:lo + SUB, :]
                bI = bs[hh, lo:lo + SUB, :]
                doI = dos[hh, lo:lo + SUB, :]
                dqI = jnp.zeros((SUB, HEAD), F32)
                dbI = jnp.zeros((SUB, HEAD), F32)
                if I > 0:
                    bprev = bs[hh, pl.ds(lo - 1, 1), :]
                    eq = jnp.exp(bI - bprev)
                    ek = jnp.exp(bprev - bs[hh, 0:lo, :])
                    qt = _r16(qI * eq)
                    kt = _r16(ks[hh, 0:lo, :] * ek)
                    A = _r16(_nt(qt, kt))
                    doI_b = _r16(doI)
                    dA = _r16(_nt(doI_b, _r16(vs[hh, 0:lo, :])))
                    dvs[hh, 0:lo, :] += _tn(A, doI_b)
                    dqt = _nn(dA, kt)
                    dkt = _tn(dA, qt)
                    dqI = dqI + dqt * eq
                    dbI = dbI + qt.astype(F32) * dqt
                    dks[hh, 0:lo, :] += dkt * ek
                    dbs[hh, 0:lo, :] -= kt.astype(F32) * dkt
                dq_t = [jnp.zeros((8, HEAD), F32) for _ in range(SUB // 8)]
                a_t = [jnp.zeros((8, HEAD), F32) for _ in range(SUB // 8)]
                for s in range(SUB):
                    row = pl.ds(lo + s, 1)
                    brow, krow, vrow = bs[hh, row, :], ks[hh, row, :], vs[hh, row, :]
                    dk_s = None
                    for ti in range(SUB // 8):
                        o8 = 8 * ti
                        if s > o8 + 7:
                            continue
                        d = bI[o8:o8 + 8] - brow
                        if s > o8:
                            d = jnp.where(t8 >= s - o8, d, NEG)
                        Es = jnp.exp(d)
                        qE = qI[o8:o8 + 8] * Es
                        col = jnp.sum(qE * krow, axis=1, keepdims=True)
                        a_t[ti] = jnp.where(lane == s, col, a_t[ti])
                        dcol = jnp.sum(doI[o8:o8 + 8] * vrow, axis=1, keepdims=True)
                        dq_t[ti] = dq_t[ti] + (dcol * Es) * krow
                        part = jnp.sum(dcol * qE, axis=0, keepdims=True)
                        dk_s = part if dk_s is None else dk_s + part
                    dks[hh, row, :] += dk_s
                    dbs[hh, row, :] -= krow * dk_s
                a_d = jnp.concatenate(a_t, axis=0)
                dq_d = jnp.concatenate(dq_t, axis=0)
                dvs[hh, lo:lo + SUB, :] += _tn(a_d, doI)[0:SUB]
                dqI = dqI + dq_d
                dbI = dbI + qI * dq_d
                dqs[hh, lo:lo + SUB, :] += dqI
                dbs[hh, lo:lo + SUB, :] += dbI
            kdk = kk * dki[hh]
            excl = _cumsum_rows(kdk) - kdk
            suff = _cumsum_rows(dbs[hh], reverse=True)
            gdec = jnp.sum(dS * st0, axis=0, keepdims=True) * jnp.exp(blast)
            dlf = suff + excl + gdec
            dk = dks[hh] + dki[hh]
            dfg = dlf / fg - dk
            dlb_ref[:, sl] += jnp.sum(dfg * (1.0 - sg), axis=0, keepdims=True)
            du_ref[0, rows, sl] = (dqs[hh] * _dsilu(qr, sq)).astype(BF16)
            du_ref[1, rows, sl] = (dfg * (1.0 - lbv_) * sg * (1.0 - sg)).astype(BF16)
            du_ref[2, rows, sl] = dvs[hh].astype(BF16)
            du_ref[3, rows, sl] = dgr.astype(BF16)
            dst[hh] = dS * jnp.exp(blast) + _tn(do_b, (q * eb).astype(BF16))

        def chunk(cc, carry):
            c = NC - 1 - cc
            rows = pl.ds(pl.multiple_of(c * CHUNK, CHUNK), CHUNK)
            for hh in range(HP):
                head(hh, c, rows)
            return carry

        lax.fori_loop(0, NC, chunk, 0)

    HP = HGRN_HEADS_BWD
    W = HP * HEAD

    def blk(qd):
        return pl.BlockSpec((RB, W), lambda h, r: (NB - 1 - r, qd * (H // HP) + h))

    hblk = pl.BlockSpec((RB, W), lambda h, r: (NB - 1 - r, h))
    sc = lambda: pltpu.VMEM((HP, CHUNK, HEAD), F32)
    return pl.pallas_call(
        body, grid=(H // HP, NB),
        in_specs=[blk(0), blk(1), blk(2), blk(3), pl.BlockSpec((1, W), lambda h, r: (0, h)),
                  pl.BlockSpec((1, HEAD), lambda h, r: (0, 0)), hblk, hblk,
                  pl.BlockSpec((HP, NC, HEAD, HEAD), lambda h, r: (h, NB - 1 - r, 0, 0))],
        out_specs=[pl.BlockSpec((4, RB, W), lambda h, r: (0, NB - 1 - r, h)),
                   pl.BlockSpec((1, W), lambda h, r: (0, h)), pl.BlockSpec((HP, 1, HEAD), lambda h, r: (h, 0, 0))],
        out_shape=[jax.ShapeDtypeStruct((4, T, H * HEAD), BF16), jax.ShapeDtypeStruct((1, H * HEAD), F32),
                   jax.ShapeDtypeStruct((H, 1, HEAD), F32)],
        scratch_shapes=[pltpu.VMEM((HP, HEAD, HEAD), F32)] + [sc() for _ in range(10)],
        compiler_params=_cp("parallel", "arbitrary"), name="hgrn_bwd")(u, u, u, u, lbv, gn, o, dy, s0)


def _softmax_rows(p_ref, L):
    rows = [p_ref[pl.ds(l, 1), :] for l in range(L)]
    m = rows[0]
    for r in rows[1:]:
        m = jnp.maximum(m, r)
    e = [jnp.exp(r - m) for r in rows]
    tot = e[0]
    for t in e[1:]:
        tot = tot + t
    return [t / tot for t in e]


def _lb_fwd(lbp):
    L, D = lbp.shape

    def body(p_ref, o_ref):
        sm = _softmax_rows(p_ref, L)
        acc = jnp.zeros((1, D), F32)
        o_ref[pl.ds(0, 1), :] = acc
        for l in range(1, L):
            acc = acc + sm[l]
            o_ref[pl.ds(l, 1), :] = acc

    return pl.pallas_call(body, out_shape=jax.ShapeDtypeStruct((L, D), F32), name="lb_fwd")(lbp)


def _lb_bwd(lbp, dlb):
    L, D = lbp.shape

    def body(p_ref, d_ref, o_ref):
        sm = _softmax_rows(p_ref, L)
        dsm = [jnp.zeros((1, D), F32)]
        for i in range(1, L):
            t = jnp.zeros((1, D), F32)
            for l in range(i, L):
                t = t + d_ref[pl.ds(l, 1), :]
            dsm.append(t)
        dot = jnp.zeros((1, D), F32)
        for i in range(L):
            dot = dot + dsm[i] * sm[i]
        for i in range(L):
            o_ref[pl.ds(i, 1), :] = sm[i] * (dsm[i] - dot)

    return pl.pallas_call(body, out_shape=jax.ShapeDtypeStruct((L, D), F32), name="lb_bwd")(lbp, dlb)


def _my_pos():
    return lax.axis_index("x"), lax.axis_index("y"), lax.axis_index("c")


def _peer(mask):
    x, y, c = _my_pos()
    mx, my, mc = (mask >> 2) & 1, (mask >> 1) & 1, mask & 1
    px = (1 - x) if mx else x
    py = (1 - y) if my else y
    pc = (1 - c) if mc else c
    return (px, py, pc), 4 * px + 2 * py + pc


HBM_SPEC =pl.BlockSpec(memory_space=pltpu.HBM)
SEM_SPEC = pl.BlockSpec(memory_space=pltpu.SEMAPHORE)
EFFECT = pltpu.SideEffectType.DATAFLOW_SIDE_EFFECTING


def _hbm(a):
    return pltpu.with_memory_space_constraint(a, pltpu.HBM)


def _landing(block_shape, dtype, axis=0):
    if axis == 0:
        return lax.empty((N_DEV,) + tuple(block_shape), dtype)
    rows, n = block_shape
    return lax.empty((rows, N_DEV * n), dtype)


def _slot(ref, i):
    if len(ref.shape) == 2:
        n = ref.shape[1] // N_DEV
        return ref.at[:, pl.ds(i * n, n)]
    return ref.at[i]


def _push_start(name, srcs, lands, whole, groups):
    n = len(srcs)
    ng = 1 + max(groups)
    cnt = [groups.count(g) for g in range(ng)]
    idx = [groups[:a].count(groups[a]) for a in range(n)]

    def body(*refs):
        src_refs, land_refs = refs[:n], refs[n:2 * n]
        sems = refs[2 * n:2 * n + 3 * ng]
        token = refs[-1]
        x, y, c = _my_pos()
        me = 4 * x + 2 * y + c
        for a in range(n):
            g = groups[a]
            for m in range(1, N_DEV):
                peer, pid = _peer(m)
                pltpu.make_async_remote_copy(
                    src_ref=src_refs[a] if whole else src_refs[a].at[pid], dst_ref=_slot(land_refs[a], me),
                    send_sem=sems[3 * g].at[idx[a] * (N_DEV - 1) + m - 1],
                    recv_sem=sems[3 * g + 1].at[idx[a] * (N_DEV - 1) + m - 1],
                    device_id=peer, device_id_type=MESH).start()
            pltpu.make_async_copy(src_refs[a] if whole else src_refs[a].at[me], _slot(land_refs[a], me),
                                  sems[3 * g + 2].at[idx[a]]).start()
        token[...] = jnp.zeros_like(token)

    sem_shapes = []
    for g in range(ng):
        sem_shapes += [pltpu.SemaphoreType.DMA((cnt[g] * (N_DEV - 1),))] * 2 + [pltpu.SemaphoreType.DMA((cnt[g],))]
    thru = [pltpu.HBM(s.shape, s.dtype) for s in list(srcs) + list(lands)]
    res = pl.pallas_call(
        body, name=name,
        out_shape=tuple(sem_shapes + thru + [jax.ShapeDtypeStruct((8, 128), F32)]),
        in_specs=tuple([HBM_SPEC] * (2 * n)),
        out_specs=tuple([SEM_SPEC] * (3 * ng) + [HBM_SPEC] * (2 * n) + [pl.BlockSpec(memory_space=pltpu.VMEM)]),
        input_output_aliases={i: 3 * ng + i for i in range(2 * n)},
        compiler_params=pltpu.CompilerParams(has_side_effects=EFFECT),
    )(*[_hbm(s) for s in srcs], *[_hbm(z) for z in lands])
    sems = [(res[3 * g], res[3 * g + 1], res[3 * g + 2]) for g in range(ng)]
    srcs_thru = list(res[3 * ng:3 * ng + n])
    lands_thru = list(res[3 * ng + n:3 * ng + 2 * n])
    return sems, srcs_thru, lands_thru, res[-1]


def _push_wait(name, srcs_thru, lands_thru, sems, after, whole):
    n = len(srcs_thru)

    def body(*refs):
        src_refs, land_refs = refs[:n], refs[n:2 * n]
        send_sems, recv_sems, own_sems = refs[2 * n], refs[2 * n + 1], refs[2 * n + 2]
        x, y, c = _my_pos()
        me = 4 * x + 2 * y + c
        for a in range(n):
            pltpu.make_async_copy(src_refs[a] if whole else src_refs[a].at[me], _slot(land_refs[a], me),
                                  own_sems.at[a]).wait()
            for m in range(1, N_DEV):
                peer, pid = _peer(m)
                cp = pltpu.make_async_remote_copy(
                    src_ref=src_refs[a] if whole else src_refs[a].at[pid], dst_ref=_slot(land_refs[a], pid),
                    send_sem=send_sems.at[a * (N_DEV - 1) + m - 1], recv_sem=recv_sems.at[a * (N_DEV - 1) + m - 1],
                    device_id=peer, device_id_type=MESH)
                cp.wait_send()
                cp.wait_recv()

    thru = [pltpu.HBM(s.shape, s.dtype) for s in list(srcs_thru) + list(lands_thru)]
    res = pl.pallas_call(
        body, name=name, out_shape=tuple(thru),
        in_specs=tuple([HBM_SPEC] * (2 * n) + [SEM_SPEC, SEM_SPEC, SEM_SPEC, ANY]),
        out_specs=tuple([HBM_SPEC] * (2 * n)),
        input_output_aliases={i: i for i in range(2 * n)},
        compiler_params=pltpu.CompilerParams(has_side_effects=EFFECT),
    )(*srcs_thru, *lands_thru, sems[0], sems[1], sems[2], after)
    return list(res[n:])


def _adamw(recv, w, m, v, layer=0, prev=None):
    L, R, C = w.shape
    tr = _tile(R, max(8, (1 << 18) // C), 8) if R % 8 == 0 else R
    bc1 = 1.0 - ADAM_B1 ** ADAM_STEP
    bc2 = 1.0 - ADAM_B2 ** ADAM_STEP
    if prev is None:
        prev = [lax.empty((L, R, C), F32) for _ in range(4)]

    def body(r_ref, w_ref, m_ref, v_ref, p0, p1, p2, p3, g_ref, d_ref, nm_ref, nv_ref):
        g = r_ref[0].astype(F32)
        for s in range(1, N_DEV):
            g = g + r_ref[s].astype(F32)
        nm = ADAM_B1 * m_ref[...] + (1.0 - ADAM_B1) * g
        nv = ADAM_B2 * v_ref[...] + (1.0 - ADAM_B2) * (g * g)
        mh = nm / bc1
        vh = nv / bc2
        g_ref[...] = g
        d_ref[...] = -ADAM_LR * (mh / (jnp.sqrt(vh) + ADAM_EPS) + ADAM_WD * w_ref[...])
        nm_ref[...] = nm
        nv_ref[...] = nv

    row = pl.BlockSpec((None, tr, C), lambda i: (layer, i, 0))
    return pl.pallas_call(
        body, grid=(R // tr,),
        in_specs=[pl.BlockSpec((N_DEV, tr, C), lambda i: (0, i, 0)), row, row, row] + [ANY] * 4,
        out_specs=[row] * 4, out_shape=[jax.ShapeDtypeStruct((L, R, C), F32)] * 4,
        input_output_aliases={4: 0, 5: 1, 6: 2, 7: 3},
        compiler_params=_cp("parallel"), name="adamw")(recv, w, m, v, *prev)


def _full_w_spec(tk, tn):
    return pl.BlockSpec((tk, tn), lambda i, j, k: (k, j))


def kernel(x, meta_tokens, mix_norm_g, mlp_norm_g, final_norm_g, ev_w_in, ev_conv_w, ev_conv_b, ev_ln_g, ev_ln_b, ev_pool_w, ev_pool_b, ev_pool_scale, ev_w_out, od_w_in, od_gnorm_g, od_w_out, lb_param, mlp_w1, mlp_w2, loss_target, m_meta_tokens, m_mix_norm_g, m_mlp_norm_g, m_final_norm_g, m_ev_w_in, m_ev_conv_w, m_ev_conv_b, m_ev_ln_g, m_ev_ln_b, m_ev_pool_w, m_ev_pool_b, m_ev_pool_scale, m_ev_w_out, m_od_w_in, m_od_gnorm_g, m_od_w_out, m_lb_param, m_mlp_w1, m_mlp_w2, v_meta_tokens, v_mix_norm_g, v_mlp_norm_g, v_final_norm_g, v_ev_w_in, v_ev_conv_w, v_ev_conv_b, v_ev_ln_g, v_ev_ln_b, v_ev_pool_w, v_ev_pool_b, v_ev_pool_scale, v_ev_w_out, v_od_w_in, v_od_gnorm_g, v_od_w_out, v_lb_param, v_mlp_w1, v_mlp_w2):
    S, D = x.shape[1], x.shape[2]
    T = PAD + N_META + S
    DEPTH = mix_norm_g.shape[0]
    DFF = mlp_w1.shape[2] * N_DEV
    dev = 4 * lax.axis_index("x") + 2 * lax.axis_index("y") + lax.axis_index("c")

    n_ev = ev_w_in.shape[0]
    n_od = od_w_in.shape[0]
    n_in_od = od_w_in.shape[2]
    n_w1 = mlp_w1.shape[2]

    ag_src, ag_grp, ag_axis, ag_at = [], [], [], {}
    for key, arr in (("meta", meta_tokens), ("cw", ev_conv_w)):
        ag_at["small", key] = len(ag_src)
        ag_src.append(arr)
        ag_grp.append(len(ag_grp))
        ag_axis.append(0)
    for layer in range(DEPTH):
        j = layer // 2
        mixer = [("in", ev_w_in[j]), ("out", ev_w_out[j])] if layer % 2 == 0 else [("in", od_w_in[j]), ("out", od_w_out[j])]
        for pos, (key, arr) in enumerate(mixer + [("w1", mlp_w1[layer]), ("w2", mlp_w2[layer])]):
            ag_at[layer, key] = len(ag_src)
            ag_src.append(arr.astype(BF16))
            ag_grp.append(len(ag_grp))
            ag_axis.append(1 if key in ("in", "w1") and arr.shape[1] % 128 == 0 else 0)
    ag_sems, ag_s, ag_l, ag_tok = _push_start(
        "ag_start", ag_src, [_landing(s_.shape, s_.dtype, ax) for s_, ax in zip(ag_src, ag_axis)], True, ag_grp)

    def ag_wait(layer, key, after):
        a = ag_at[layer, key]
        return _push_wait(f"ag_wait_{a}", [ag_s[a]], [ag_l[a]], ag_sems[a], after, True)[0]

    g_meta = ag_wait("small", "meta", ag_tok)
    g_cw = ag_wait("small", "cw", ag_tok)
    meta_full = jnp.transpose(g_meta, (1, 0, 2)).reshape(N_META, D)
    cw_full = jnp.transpose(g_cw, (1, 2, 0, 3)).reshape(n_ev, CONV_WIDTH, -1)
    cw_pad = jnp.pad(cw_full, ((0, 0), (0, 32 - CONV_WIDTH), (0, 0)))

    h = jnp.concatenate([jnp.zeros((PAD, D), F32), meta_full, x[0]], axis=0) + ag_tok[0, 0]
    tgt = jnp.pad(loss_target[0], ((PAD + N_META, 0), (0, 0)))
    lb_all = _lb_fwd(lb_param)
    h, tgt, lb_all = lax.optimization_barrier((h, tgt, lb_all))

    tm_big = _tile(T, MM_ROWS_BIG, 16)
    tm_mid = _tile(T, MM_ROWS_MID, 16)
    tm_k4 = _tile(T, MM_ROWS_K4, 16)

    saved = []
    for layer in range(DEPTH):
        j = layer // 2
        sv = {"h0": h}
        g_in = ag_wait(layer, "in", h)
        w_in = g_in if g_in.ndim == 2 else jnp.transpose(g_in, (1, 0, 2)).reshape(D, -1)
        if layer % 2 == 0:
            sv["n"], u = _mm_rms_nn("ev_in", h, mix_norm_g[layer][None], w_in, tm_big, 512, "f32")
            yab, yc = _ev_fwd(u, cw_pad[j], ev_conv_b[j][None], ev_ln_g[j][None], ev_ln_b[j][None],
                              ev_pool_w[j].astype(BF16), ev_pool_b[j].reshape(1, -1), ev_pool_scale[j][None])
            sv.update(u=u, y=yab, yc=yc)
            w_out = ag_wait(layer, "out", yab).reshape(-1, D)
            h = _mm_nn("ev_out", yab, w_out, _full_w_spec, T, D, D, tm_mid, D, D, "resid", extra=h)
        else:
            sv["n"], u = _mm_rms_nn("od_in", h, mix_norm_g[layer][None], w_in, tm_big, 512, "f32")
            y, o, s0 = _hgrn_fwd(u, lb_all[layer][None], od_gnorm_g[j][None])
            sv.update(u=u, y=y, o=o, s0=s0)
            w_out = ag_wait(layer, "out", y).reshape(-1, D)
            h = _mm_nn("od_out", y, w_out, _full_w_spec, T, D, D, tm_mid, D, D, "resid", extra=h)
        sv["h1"] = h
        w_w1 = ag_wait(layer, "w1", h)
        n2, r, act = _mm_rms_nn("mlp_w1", h, mlp_norm_g[layer][None], w_w1, tm_big, 512, "relu2")
        w_w2 = ag_wait(layer, "w2", act).reshape(DFF, D)
        sv.update(w_in=w_in, w_out=w_out, w_w1=w_w1, w_w2=w_w2)
        sv.update(n2=n2, r=r, act=act)
        h = _mm_nn("mlp_w2", act, w_w2, _full_w_spec, T, D, DFF, tm_k4, D, DFF, "resid", extra=h)
        saved.append(sv)

    loss_blk, dh, dhb, dg_final = _loss_head(h, final_norm_g[None], tgt)
    loss = lax.psum(loss_blk[0, 0], AXES)

    tt = T
    g_mix, g_mlp = [None] * DEPTH, [None] * DEPTH
    small ={"cw": [None] * n_ev, "vec": [None] * n_ev, "pw": [None] * n_ev, "gn": [None] * n_od}
    dlb_rows = [jnp.zeros((1, D), F32) for _ in range(DEPTH)]

    def xs2(tt_, tk):
        return pl.BlockSpec((tt_, tk), lambda a, b, t: (t, a))

    def ys2(tt_, tn):
        return pl.BlockSpec((tt_, tn), lambda a, b, t: (t, b))

    def os2(tk, tn):
        return pl.BlockSpec((tk, tn), lambda a, b, t: (a, b))

    def os3(tk, tn):
        return pl.BlockSpec((None, tk, tn), lambda a, b, t: (b, a, 0))

    def dy2(tm, tn):
        return pl.BlockSpec((tm, tn), lambda i, jj, k: (i, k))

    def w_rows(tj, tn):
        return pl.BlockSpec((tj, tn), lambda i, jj, k: (jj, k))

    def w_whole(tj, tn):
        return pl.BlockSpec((tj, tn), lambda i, jj, k: (0, 0), pipeline_mode=pl.Buffered(1))

    rs_pending = []

    def rs_start(tag, mats):
        blocks = [m_ if m_.ndim == 3 else m_.reshape(N_DEV, m_.shape[0] // N_DEV, m_.shape[1]) for m_ in mats]
        lands = [_landing(b_.shape[1:], b_.dtype) for b_ in blocks]
        sems, s_thru, l_thru, tok = _push_start(f"rs_start_{tag}", blocks, lands, False, [0] * len(blocks))
        rs_pending.append((tag, s_thru, l_thru, sems[0]))
        return tok[0, 0]

    for layer in reversed(range(DEPTH)):
        j = layer // 2
        sv = saved[layer]
        da1 = _mm_nt("mlp_w2_t", dhb, sv["w_w2"], dy2, w_rows, T, DFF, D, tm_big, 512, D, "dact", extra=sv["r"])
        dw2 = _mm_tn("mlp_dw2", sv["act"], dhb, xs2, ys2, os2, (DFF, D), T, DFF, D, tt, 512, D)
        dw1 = _mm_tn("mlp_dw1", sv["n2"], da1, xs2, ys2, os3, (N_DEV, D, n_w1), T, D, DFF, tt, D, n_w1)
        held = [dw1, dw2] if layer > 0 else []
        tok = 0.0 if layer > 0 else rs_start(f"mlp{layer}", [dw1, dw2])
        dh, dhb, g_mlp[layer] = _mm_nt("mlp_w1_t", da1, sv["w_w1"], dy2, w_whole, T, D, DFF, tm_k4, D, DFF, "rms",
                                       extra=(sv["h1"], mlp_norm_g[layer][None] + tok, dh))
        if layer % 2 == 0:
            dyab = _mm_nt("ev_out_t", dhb, sv["w_out"], dy2, w_rows, T, D, D, tm_mid, D, D, "f32")
            dwout = _mm_tn("ev_dwout", sv["y"], dhb, xs2, ys2, os2, (D, D), T, D, D, tt, 512, D)
            du, small["cw"][j], small["vec"][j], small["pw"][j] = _ev_bwd(
                dyab, sv["yc"], sv["u"], cw_pad[j], ev_ln_g[j][None], ev_ln_b[j][None], ev_pool_w[j].astype(BF16),
                jnp.transpose(ev_pool_w[j], (0, 2, 1)).astype(BF16), ev_pool_b[j].reshape(1, -1),
                ev_pool_scale[j][None])
            nin = du.shape[1]
            dwin = _mm_tn("ev_dwin", sv["n"], du, xs2, ys2, os2, (D, nin), T, D, nin, tt, D, 512)
            dwin = jnp.transpose(dwin.reshape(D, N_DEV, nin // N_DEV), (1, 0, 2))
            tok = rs_start(f"mix{layer}", [dwin, dwout] + held)
            dh, dhb, g_mix[layer] = _mm_nt("ev_in_t", du, sv["w_in"], dy2, w_whole, T, D, nin, tm_k4, D, nin, "rms",
                                           extra=(sv["h0"], mix_norm_g[layer][None] + tok, dh))
        else:
            dy = _mm_nt("od_out_t", dhb, sv["w_out"], dy2, w_rows, T, D, D, tm_mid, D, D, "f32")
            dwout = _mm_tn("od_dwout", sv["y"], dhb, xs2, ys2, os2, (D, D), T, D, D, tt, 512, D)
            du3, dlb_rows[layer], small["gn"][j] = _hgrn_bwd(dy, sv["o"], sv["s0"], sv["u"], lb_all[layer][None],
                                                              od_gnorm_g[j][None])
            per = D // n_in_od

            def du_t(tt_, tn):
                return pl.BlockSpec((None, tt_, tn), lambda a, b, t: (b // per, t, b % per))

            dwin = _mm_tn("od_dwin", sv["n"], du3, xs2, du_t, os3, (N_DEV, D, n_in_od), T, D, 4 * D, tt, D, n_in_od)
            tok = rs_start(f"mix{layer}", [dwin, dwout] + held)
            dh, dhb, g_mix[layer] = _mm_nt(
                "od_in_t", du3, sv["w_in"], lambda tm, tn: pl.BlockSpec((4, tm, tn // 4), lambda i, jj, k: (0, i, 0)),
                w_whole, T, D, 4 * D, tm_k4, D, 4 * D, "rms", extra=(sv["h0"], mix_norm_g[layer][None] + tok, dh),
                parts=4)

    dmeta = dh[PAD:PAD + N_META]
    grad_x = dh[PAD + N_META:][None]
    dlb_param = _lb_bwd(lb_param, jnp.concatenate(dlb_rows, axis=0))

    pieces = [
        ("final", dg_final), ("pad", jnp.zeros((SMALL_F32_ROWS - 1, D), F32)),
        ("meta", dmeta), ("mix", jnp.concatenate(g_mix, 0)), ("mlp", jnp.concatenate(g_mlp, 0)),
        ("cw", jnp.stack([c[:CONV_WIDTH] for c in small["cw"]])), ("cb", jnp.stack([v_[0] for v_ in small["vec"]])),
        ("lng", jnp.stack([v_[1] for v_ in small["vec"]])), ("lnb", jnp.stack([v_[2] for v_ in small["vec"]])),
        ("pw", jnp.stack(small["pw"])), ("pb", jnp.stack([v_[4] for v_ in small["vec"]])),
        ("ps", jnp.stack([v_[3] for v_ in small["vec"]])), ("gn", jnp.stack([jnp.sum(g_, axis=0)[0] for g_ in small["gn"]])),
        ("lb", dlb_param),
    ]
    flat = jnp.concatenate([p.reshape(-1) for _, p in pieces])
    n_small = flat.shape[0]
    rows_small = SMALL_F32_ROWS + -(-(n_small // 1024 + 1 - SMALL_F32_ROWS) // 16) * 16
    flat = jnp.pad(flat, (0, rows_small * 1024 - n_small)).reshape(rows_small, 1024)

    sm_src = [flat[:SMALL_F32_ROWS], flat[SMALL_F32_ROWS:].astype(BF16)]
    sm_sems, sm_s, sm_l, sm_tok = _push_start("small_start", sm_src, [_landing(a_.shape, a_.dtype) for a_ in sm_src],
                                              True, [0, 0])
    recv = {}
    for tag, s_thru, l_thru, sems in rs_pending:
        got = _push_wait(f"rs_wait_{tag}", s_thru, l_thru, sems, sm_tok, False)
        layer = int(tag[3:])
        if tag.startswith("mlp"):
            recv["w1", layer], recv["w2", layer] = got
        else:
            key = "ev" if layer % 2 == 0 else "od"
            recv[key + "_in", layer // 2], recv[key + "_out", layer // 2] = got[:2]
            if len(got) > 2:
                recv["w1", layer], recv["w2", layer] = got[2:]

    outs = {}
    big = {"ev_in": ("ev_w_in", ev_w_in, m_ev_w_in, v_ev_w_in), "ev_out": ("ev_w_out", ev_w_out, m_ev_w_out, v_ev_w_out),
           "od_in": ("od_w_in", od_w_in, m_od_w_in, v_od_w_in), "od_out": ("od_w_out", od_w_out, m_od_w_out, v_od_w_out),
           "w1": ("mlp_w1", mlp_w1, m_mlp_w1, v_mlp_w1), "w2": ("mlp_w2", mlp_w2, m_mlp_w2, v_mlp_w2)}
    for key, (name, w, m, v) in big.items():
        res = None
        for l in range(w.shape[0]):
            res = _adamw(recv[key, l], w, m, v, layer=l, prev=res)
        outs[name] = res

    small_params = {
        "meta": ("meta_tokens", None), "mix": ("mix_norm_g", mix_norm_g, m_mix_norm_g, v_mix_norm_g),
        "mlp": ("mlp_norm_g", mlp_norm_g, m_mlp_norm_g, v_mlp_norm_g),
        "final": ("final_norm_g", final_norm_g, m_final_norm_g, v_final_norm_g),
        "cw": ("ev_conv_w", None), "cb": ("ev_conv_b", ev_conv_b, m_ev_conv_b, v_ev_conv_b),
        "lng": ("ev_ln_g", ev_ln_g, m_ev_ln_g, v_ev_ln_g), "lnb": ("ev_ln_b", ev_ln_b, m_ev_ln_b, v_ev_ln_b),
        "pw": ("ev_pool_w", ev_pool_w, m_ev_pool_w, v_ev_pool_w), "pb": ("ev_pool_b", ev_pool_b, m_ev_pool_b, v_ev_pool_b),
        "ps": ("ev_pool_scale", ev_pool_scale, m_ev_pool_scale, v_ev_pool_scale),
        "gn": ("od_gnorm_g", od_gnorm_g, m_od_gnorm_g, v_od_gnorm_g), "lb": ("lb_param", lb_param, m_lb_param, v_lb_param),
    }
    csh = ev_conv_w.shape[2]
    msh = meta_tokens.shape[1]

    def packed(which):
        parts = []
        for key, g_ in pieces:
            ent = small_params.get(key)
            if key == "pad":
                full = g_
            elif key == "meta":
                src = (meta_tokens, m_meta_tokens, v_meta_tokens)[which]
                full = lax.dynamic_update_slice(jnp.zeros((N_META, D), F32), src, (0, dev * msh))
            elif key == "cw":
                src = (ev_conv_w, m_ev_conv_w, v_ev_conv_w)[which]
                full = lax.dynamic_update_slice(jnp.zeros(g_.shape, F32), src, (0, 0, dev * csh))
            else:
                full = ent[1 + which]
            parts.append(full.reshape(-1))
        f = jnp.concatenate(parts)
        return jnp.pad(f, (0, rows_small * 1024 - n_small)).reshape(rows_small, 1024)

    got_f32, got_bf16 = _push_wait("small_wait", sm_s, sm_l, sm_sems[0], outs["mlp_w2"][0], True)
    recv_small = jnp.concatenate([got_f32, got_bf16.astype(F32)], axis=1)
    sres = [r_[0] for r_ in _adamw(recv_small, packed(0)[None], packed(1)[None], packed(2)[None])]
    off = 0
    for key, g_ in pieces:
        size = g_.size
        vals = [r_.reshape(-1)[off:off + size].reshape(g_.shape) for r_ in sres]
        off += size
        if key == "pad":
            continue
        name = small_params[key][0]
        if key == "meta":
            vals = [lax.dynamic_slice(v_, (0, dev * msh), (N_META, msh)) for v_ in vals]
        elif key == "cw":
            vals = [lax.dynamic_slice(v_, (0, 0, dev * csh), v_.shape[:2] + (csh,)) for v_ in vals]
        else:
            vals = [v_.reshape(small_params[key][1].shape) for v_ in vals]
        outs[name] = vals

    names = ["meta_tokens", "mix_norm_g", "mlp_norm_g", "final_norm_g", "ev_w_in", "ev_conv_w", "ev_conv_b", "ev_ln_g",
             "ev_ln_b", "ev_pool_w", "ev_pool_b", "ev_pool_scale", "ev_w_out", "od_w_in", "od_gnorm_g", "od_w_out",
             "lb_param", "mlp_w1", "mlp_w2"]
    result = [loss, grad_x]
    for k in range(4):
        result += [outs[nm][k] for nm in names]
    return tuple(result)
```

```python
import functools

import jax
import jax.numpy as jnp
from jax import lax
from jax.experimental import pallas as pl
from jax.experimental.pallas import tpu as pltpu

F32 = jnp.float32
BF16 = jnp.bfloat16

N_DEV = 8
N_META = 16
CHUNK = 64
PAD = CHUNK - N_META
SUB = 16
HEAD = 128
CONV_WIDTH = 31
HALO = 32
POOL_WINDOWS = (2, 4, 8, 16)
EPS = 1e-6
NEG = -1e30
ADAM_LR, ADAM_B1, ADAM_B2, ADAM_EPS, ADAM_WD, ADAM_STEP = 0.001, 0.9, 0.999, 1e-08, 0.01, 10
VMEM_LIMIT = 56 * 1024 * 1024
EV_ROWS = 416
HGRN_ROWS = 832
MM_ROWS_BIG = 2080
MM_ROWS_MID = 1040
MM_ROWS_K4 = 416
SMALL_F32_ROWS = 8
HGRN_HEADS_FWD = 8
HGRN_HEADS_BWD = 2
MESH = pl.DeviceIdType.MESH
AXES = ("x", "y", "c")
ANY = pl.BlockSpec(memory_space=pl.ANY)


def _cp(*sem):
    return pltpu.CompilerParams(dimension_semantics=sem, vmem_limit_bytes=VMEM_LIMIT)


def _tile(n, cap, mult):
    best = None
    for d in range(mult, min(n, cap) + 1, mult):
        if n % d == 0:
            best = d
    assert best is not None, (n, cap, mult)
    return best


def _nt(a, b):
    return lax.dot_general(a, b, (((1,), (1,)), ((), ())), preferred_element_type=F32)


def _tn(a, b):
    return lax.dot_general(a, b, (((0,), (0,)), ((), ())), preferred_element_type=F32)


def _nn(a, b):
    return jnp.dot(a, b, preferred_element_type=F32)


def _r16(x):
    return x.astype(BF16).astype(F32)


def _row_ids(base, n):
    return base + lax.broadcasted_iota(jnp.int32, (n, 1), 0)


def _dsilu(x, s):
    return s * (1.0 + x * (1.0 - s))


def _loss_head(h, g, tgt):
    T, D = h.shape
    tm = _tile(T, MM_ROWS_MID, 16)
    first_x = PAD + N_META

    def body(h_ref, g_ref, t_ref, loss_ref, dh_ref, dhb_ref, dg_ref):
        i = pl.program_id(0)
        x = h_ref[...]
        r = lax.rsqrt(jnp.mean(x * x, axis=-1, keepdims=True) + EPS)
        xh = x * r
        gv = g_ref[...]
        out = xh * gv
        valid = _row_ids(i * tm, tm) >= first_x
        e = jnp.where(valid, out - t_ref[...], 0.0)
        dout = e * (1.0 / D)
        dxh = dout * gv
        dx = r * (dxh - xh * jnp.mean(dxh * xh, axis=-1, keepdims=True))
        dh_ref[...] = dx
        dhb_ref[...] = dx.astype(BF16)

        @pl.when(i == 0)
        def _():
            dg_ref[...] = jnp.zeros_like(dg_ref)
            loss_ref[...] = jnp.zeros_like(loss_ref)

        dg_ref[...] += jnp.sum(dout * xh, axis=0, keepdims=True)
        loss_ref[...] += 0.5 * jnp.sum(jnp.mean(e * e, axis=-1, keepdims=True))

    row = pl.BlockSpec((tm, D), lambda i: (i, 0))
    vec = pl.BlockSpec((1, D), lambda i: (0, 0))
    return pl.pallas_call(
        body, grid=(T // tm,),
        in_specs=[row, vec, row],
        out_specs=[pl.BlockSpec((8, 128), lambda i: (0, 0)), row, row, vec],
        out_shape=[jax.ShapeDtypeStruct((8, 128), F32), jax.ShapeDtypeStruct((T, D), F32),
                   jax.ShapeDtypeStruct((T, D), BF16), jax.ShapeDtypeStruct((1, D), F32)],
        compiler_params=_cp("arbitrary"), name="loss_head")(h, g, tgt)


def _mm_nn(name, a, w, w_spec, M, N, K, tm, tn, tk, mode, extra=None, a_spec=None):
    nk = K // tk
    if a_spec is None:
        a_spec = pl.BlockSpec((tm, tk), lambda i, j, k: (i, k))
    o_spec = pl.BlockSpec((tm, tn), lambda i, j, k: (i, j))

    def body(*refs):
        if mode == "resid":
            a_ref, w_ref, e_ref = refs[:3]
            outs = refs[3:]
        else:
            a_ref, w_ref = refs[:2]
            outs = refs[2:]
        acc_ref = outs[-1] if nk > 1 else None
        part = _nn(a_ref[...], w_ref[...])

        def finish(acc):
            if mode == "f32":
                outs[0][...] = acc
            elif mode == "relu2":
                r = jnp.maximum(acc, 0.0)
                outs[0][...] = r.astype(BF16)
                outs[1][...] = (r * r).astype(BF16)
            else:
                keep = _row_ids(pl.program_id(0) * tm, tm) >= PAD
                outs[0][...] = jnp.where(keep, e_ref[...] + acc, 0.0)

        if nk == 1:
            finish(part)
        else:
            k = pl.program_id(2)

            @pl.when(k == 0)
            def _():
                acc_ref[...] = part

            @pl.when(k > 0)
            def _():
                acc_ref[...] += part

            @pl.when(k == nk - 1)
            def _():
                finish(acc_ref[...])

    in_specs = [a_spec, w_spec(tk, tn)]
    args = [a, w]
    if mode == "resid":
        in_specs.append(o_spec)
        args.append(extra)
    if mode == "relu2":
        out_specs = [o_spec, o_spec]
        out_shape = [jax.ShapeDtypeStruct((M, N), BF16)] * 2
    else:
        out_specs = [o_spec]
        out_shape = [jax.ShapeDtypeStruct((M, N), F32)]
    scratch = [pltpu.VMEM((tm, tn), F32)] if nk > 1 else []
    res = pl.pallas_call(
        body, grid=(M // tm, N // tn, nk), in_specs=in_specs, out_specs=out_specs, out_shape=out_shape,
        scratch_shapes=scratch, compiler_params=_cp("parallel", "parallel", "arbitrary"), name=name)(*args)
    return res if mode == "relu2" else res[0]


def _mm_rms_nn(name, h, g, w, tm, tn, mode):
    M, K = h.shape
    N = w.shape[1]

    def body(h_ref, g_ref, w_ref, n_ref, *outs):
        @pl.when(pl.program_id(1) == 0)
        def _():
            ch = _tile(tm, 256, 16)

            def chunk(c, carry):
                rows = pl.ds(pl.multiple_of(c * ch, ch), ch)
                x = h_ref[rows, :]
                r = lax.rsqrt(jnp.mean(x * x, axis=-1, keepdims=True) + EPS)
                n_ref[rows, :] = ((x * r) * g_ref[...]).astype(BF16)
                return carry

            lax.fori_loop(0, tm // ch, chunk, 0)

        acc = _nn(n_ref[...], w_ref[...])
        if mode == "f32":
            outs[0][...] = acc
        else:
            r = jnp.maximum(acc, 0.0)
            outs[0][...] = r.astype(BF16)
            outs[1][...] = (r * r).astype(BF16)

    row = pl.BlockSpec((tm, K), lambda i, j: (i, 0))
    o_spec = pl.BlockSpec((tm, tn), lambda i, j: (i, j))
    n_out = 1 if mode == "f32" else 2
    return pl.pallas_call(
        body, grid=(M // tm, N // tn),
        in_specs=[row, pl.BlockSpec((1, K), lambda i, j: (0, 0)), pl.BlockSpec((K, tn), lambda i, j: (0, j))],
        out_specs=[row] + [o_spec] * n_out,
        out_shape=[jax.ShapeDtypeStruct((M, K), BF16)] + [jax.ShapeDtypeStruct((M, N), F32 if mode == "f32" else BF16)] * n_out,
        compiler_params=_cp("parallel", "arbitrary"), name=name)(h, g, w)


def _mm_nt(name, dy, w, dy_spec, w_spec, M, J, N, tm, tj, tn, mode, extra=None, parts=1):
    nk = N // tn
    o_spec = pl.BlockSpec((tm, tj), lambda i, j, k: (i, j))
    n_extra = {"f32": 0, "dact": 1, "rms": 3}[mode]
    if mode == "rms":
        assert nk == 1 and tj == J

    def body(*refs):
        dy_ref, w_ref = refs[:2]
        ex = refs[2:2 + n_extra]
        outs = refs[2 + n_extra:]
        acc_ref = outs[-1] if nk > 1 else None
        if parts == 1:
            part = _nt(dy_ref[...], w_ref[...])
        else:
            wq = tn // parts
            part = _nt(dy_ref[0], w_ref[:, 0:wq])
            for q in range(1, parts):
                part = part + _nt(dy_ref[q], w_ref[:, q * wq:(q + 1) * wq])

        def finish(acc):
            if mode == "f32":
                outs[0][...] = acc
            elif mode == "dact":
                outs[0][...] = (acc * (2.0 * ex[0][...].astype(F32))).astype(BF16)
            else:
                h_ref, g_ref, dres_ref = ex
                dh_ref, dhb_ref, dg_ref = outs[:3]
                i = pl.program_id(0)

                @pl.when(i == 0)
                def _():
                    dg_ref[...] = jnp.zeros_like(dg_ref)

                ch = _tile(tm, 256, 16)
                for c0 in range(0, tm, ch):
                    a_c = acc[c0:c0 + ch]
                    x = h_ref[c0:c0 + ch, :]
                    r = lax.rsqrt(jnp.mean(x * x, axis=-1, keepdims=True) + EPS)
                    xh = x * r
                    dxh = a_c * g_ref[...]
                    dx = r * (dxh - xh * jnp.mean(dxh * xh, axis=-1, keepdims=True))
                    keep = _row_ids(i * tm + c0, ch) >= PAD
                    dh = jnp.where(keep, dres_ref[c0:c0 + ch, :] + dx, 0.0)
                    dh_ref[c0:c0 + ch, :] = dh
                    dhb_ref[c0:c0 + ch, :] = dh.astype(BF16)
                    dg_ref[...] += jnp.sum(a_c * xh, axis=0, keepdims=True)

        if nk == 1:
            finish(part)
        else:
            k = pl.program_id(2)

            @pl.when(k == 0)
            def _():
                acc_ref[...] = part

            @pl.when(k > 0)
            def _():
                acc_ref[...] += part

            @pl.when(k == nk - 1)
            def _():
                finish(acc_ref[...])

    in_specs = [dy_spec(tm, tn), w_spec(tj, tn)]
    args = [dy, w]
    scratch = [pltpu.VMEM((tm, tj), F32)] if nk > 1 else []
    if mode == "rms":
        vec = pl.BlockSpec((1, J), lambda i, j, k: (0, 0))
        h, g, dres = extra
        res = pl.pallas_call(
            body, grid=(M // tm, 1, 1), in_specs=in_specs + [o_spec, vec, o_spec], out_specs=[o_spec, o_spec, vec],
            out_shape=[jax.ShapeDtypeStruct((M, J), F32), jax.ShapeDtypeStruct((M, J), BF16),
                       jax.ShapeDtypeStruct((1, J), F32)],
            compiler_params=_cp("arbitrary", "arbitrary", "arbitrary"), name=name)(*args, h, g, dres)
        return res
    if mode == "dact":
        in_specs.append(o_spec)
        args.append(extra)
    return pl.pallas_call(
        body, grid=(M // tm, J // tj, nk), in_specs=in_specs, out_specs=[o_spec],
        out_shape=[jax.ShapeDtypeStruct((M, J), BF16 if mode == "dact" else F32)],
        scratch_shapes=scratch, compiler_params=_cp("parallel", "parallel", "arbitrary"), name=name)(*args)[0]


def _mm_tn(name, x, dy, x_spec, dy_spec, o_spec, o_shape, T, K, N, tt, tk, tn):
    nt = T // tt

    def body(x_ref, dy_ref, o_ref, *acc):
        part = _tn(x_ref[...], dy_ref[...])
        if nt == 1:
            o_ref[...] = part.astype(BF16)
            return
        acc_ref = acc[0]
        t = pl.program_id(2)

        @pl.when(t == 0)
        def _():
            acc_ref[...] = part

        @pl.when(t > 0)
        def _():
            acc_ref[...] += part

        @pl.when(t == nt - 1)
        def _():
            o_ref[...] = acc_ref[...].astype(BF16)

    return pl.pallas_call(
        body, grid=(K // tk, N // tn, nt), in_specs=[x_spec(tt, tk), dy_spec(tt, tn)], out_specs=o_spec(tk, tn),
        out_shape=jax.ShapeDtypeStruct(o_shape, BF16), scratch_shapes=[pltpu.VMEM((tk, tn), F32)] if nt > 1 else [],
        compiler_params=_cp("parallel", "parallel", "arbitrary"), name=name)(x, dy)


def _pool_counts(base, n, w):
    pos = _row_ids(base, n) - PAD
    return jnp.clip(pos + 1, 1, w).astype(F32)


def _shifted_copies(buf, rows):
    buf[0, rows:rows + 8, :] = jnp.zeros((8, buf.shape[2]), F32)

    def blk(s, carry):
        b = pl.multiple_of(s * HALO, HALO)
        win = buf[0, pl.ds(b, HALO + 8), :]
        for r in range(1, 8):
            buf[r, pl.ds(b, HALO), :] = win[r:r + HALO]
        return carry

    lax.fori_loop(0, rows // HALO, blk, 0)


def _ev_fwd(u, cw, cb, lg, lb, pw, pb, ps):
    T = u.shape[0]
    C = 512
    tm = _tile(T, EV_ROWS, HALO)
    nsub = tm // HALO
    hb = tm // HALO

    def body(val_ref, gate_ref, pin_ref, valh_ref, gateh_ref, pinh_ref, cw_ref, cb_ref, lg_ref, lb_ref, pw_ref,
             pb_ref, ps_ref, yab_ref, yc_ref, a_ext, p_ext, d_buf):
        i = pl.program_id(0)
        nf = (i > 0).astype(F32)
        a_ext[0, 0:HALO, :] = valh_ref[...] * jax.nn.sigmoid(gateh_ref[...]) * nf
        a_ext[0, HALO:HALO + tm, :] = val_ref[...] * jax.nn.sigmoid(gate_ref[...])
        p_ext[0:HALO, :] = pinh_ref[...] * nf
        p_ext[HALO:, :] = pin_ref[...]
        _shifted_copies(a_ext, tm + HALO)

        def sub(s, carry):
            base = pl.multiple_of(s * HALO, HALO)
            acc = jnp.zeros((HALO, C), F32) + cb_ref[...]
            for j in range(CONV_WIDTH):
                off = 2 + j
                acc = acc + cw_ref[pl.ds(j, 1), :] * a_ext[off % 8, pl.ds(pl.multiple_of(base + off // 8 * 8, 8), HALO), :]
            yc_ref[pl.ds(base, HALO), :] = acc
            mu = jnp.mean(acc, axis=-1, keepdims=True)
            yc = acc - mu
            rstd = lax.rsqrt(jnp.mean(yc * yc, axis=-1, keepdims=True) + EPS)
            z = (yc * rstd) * lg_ref[...] + lb_ref[...]
            yab_ref[pl.ds(base, HALO), 0:C] = (z * jax.nn.sigmoid(z)).astype(BF16)
            pwin = p_ext[pl.ds(base, 2 * HALO), :]
            for gi, w in enumerate(POOL_WINDOWS):
                lo, hi = gi * HEAD, (gi + 1) * HEAD
                x = pwin[HALO:, lo:hi]
                tot = x
                for k in range(1, w):
                    tot = tot + pwin[HALO - k:2 * HALO - k, lo:hi]
                cnt = _pool_counts(i * tm + base, HALO, w)
                d_buf[pl.ds(base, HALO), lo:hi] = (tot / cnt - x).astype(BF16)
            return carry

        lax.fori_loop(0, nsub, sub, 0, unroll=2)
        for gi in range(len(POOL_WINDOWS)):
            lo, hi = gi * HEAD, (gi + 1) * HEAD
            y = _nn(d_buf[:, lo:hi], pw_ref[gi]) + pb_ref[:, lo:hi]
            yab_ref[:, C + lo:C + hi] = (y * ps_ref[:, lo:hi]).astype(BF16)

    def main(c):
        return pl.BlockSpec((tm, C), lambda i: (i, c))

    def halo(c):
        return pl.BlockSpec((HALO, C), lambda i: (jnp.maximum(i * hb - 1, 0), c))

    vec = pl.BlockSpec((1, C), lambda i: (0, 0))
    return pl.pallas_call(
        body, grid=(T // tm,),
        in_specs=[main(0), main(1), main(2), halo(0), halo(1), halo(2),
                  pl.BlockSpec((32, C), lambda i: (0, 0)), vec, vec, vec,
                  pl.BlockSpec((4, HEAD, HEAD), lambda i: (0, 0, 0)), vec, vec],
        out_specs=[pl.BlockSpec((tm, 2 * C), lambda i: (i, 0)), pl.BlockSpec((tm, C), lambda i: (i, 0))],
        out_shape=[jax.ShapeDtypeStruct((T, 2 * C), BF16), jax.ShapeDtypeStruct((T, C), F32)],
        scratch_shapes=[pltpu.VMEM((8, tm + HALO + 8, C), F32), pltpu.VMEM((tm + HALO, C), F32),
                        pltpu.VMEM((tm, C), BF16)],
        compiler_params=_cp("parallel"), name="ev_fwd")(u, u, u, u, u, u, cw, cb, lg, lb, pw, pb, ps)


def _ev_bwd(dyab, yc, u, cw, lg, lb, pw, pwt, pb, ps):
    T = u.shape[0]
    C = 512
    tm = _tile(T, EV_ROWS, HALO)
    nsub = tm // HALO
    hb = tm // HALO
    nblk = T // tm
    E = tm + HALO

    def body(dya_ref, dyb_ref, dyah_ref, dybh_ref, yc_ref, ych_ref, val_ref, gate_ref, pin_ref, valh_ref, gateh_ref,
             pinh_ref, cw_ref, lg_ref, lb_ref, pw_ref, pwt_ref, pb_ref, ps_ref,
             du_ref, dcw_ref, dvec_ref, dpw_ref,
             dy_ext, a_ext, p_ext, ddc_ext, dd_buf, d_buf, dpre_buf, dcw_acc, vec_acc):
        i = pl.program_id(0)
        nf = (i > 0).astype(F32)
        nl = (i < nblk - 1).astype(F32)

        @pl.when(i == 0)
        def _():
            dcw_ref[...] = jnp.zeros_like(dcw_ref)
            dvec_ref[...] = jnp.zeros_like(dvec_ref)
            dpw_ref[...] = jnp.zeros_like(dpw_ref)

        dcw_acc[...] = jnp.zeros_like(dcw_acc)
        vec_acc[...] = jnp.zeros_like(vec_acc)
        a_ext[0, 0:HALO, :] = valh_ref[...] * jax.nn.sigmoid(gateh_ref[...]) * nf
        a_ext[0, HALO:E, :] = val_ref[...] * jax.nn.sigmoid(gate_ref[...])
        p_ext[0:HALO, :] = pinh_ref[...] * nf
        p_ext[HALO:, :] = pin_ref[...]
        _shifted_copies(a_ext, E)

        def ln_bwd(y, dya, main):
            mu = jnp.mean(y, axis=-1, keepdims=True)
            ycen = y - mu
            rstd = lax.rsqrt(jnp.mean(ycen * ycen, axis=-1, keepdims=True) + EPS)
            yh = ycen * rstd
            z = yh * lg_ref[...] + lb_ref[...]
            sz = jax.nn.sigmoid(z)
            dz = dya * _dsilu(z, sz)
            dyh = dz * lg_ref[...]
            dy = rstd * (dyh - jnp.mean(dyh, axis=-1, keepdims=True) - yh * jnp.mean(dyh * yh, axis=-1, keepdims=True))
            if main:
                vec_acc[1] += jnp.sum((dz * yh).reshape(HALO // 8, 8, C), axis=0)
                vec_acc[2] += jnp.sum(dz.reshape(HALO // 8, 8, C), axis=0)
                vec_acc[0] += jnp.sum(dy.reshape(HALO // 8, 8, C), axis=0)
            return dy

        def pool_dd(dyb, base, main):
            dpre = dyb * ps_ref[...]
            for gi, w in enumerate(POOL_WINDOWS):
                lo, hi = gi * HEAD, (gi + 1) * HEAD
                dd = _nn(dpre[:, lo:hi].astype(BF16), pwt_ref[gi])
                cnt = _pool_counts(i * tm + base, HALO, w)
                ddc_ext[pl.ds(base, HALO), lo:hi] = dd / cnt
                if main:
                    dd_buf[pl.ds(base, HALO), lo:hi] = dd
            if main:
                dpre_buf[pl.ds(base, HALO), :] = dpre.astype(BF16)
                vec_acc[4] += jnp.sum(dpre.reshape(HALO // 8, 8, C), axis=0)

        def p1(s, carry):
            base = pl.multiple_of(s * HALO, HALO)
            dy_ext[0, pl.ds(base, HALO), :] = ln_bwd(yc_ref[pl.ds(base, HALO), :], dya_ref[pl.ds(base, HALO), :], True)
            pool_dd(dyb_ref[pl.ds(base, HALO), :], base, True)
            return carry

        lax.fori_loop(0, nsub, p1, 0, unroll=2)
        dy_ext[0, tm:E, :] = ln_bwd(ych_ref[...], dyah_ref[...], False) * nl
        _shifted_copies(dy_ext, E)
        dpre_h = dybh_ref[...] * ps_ref[...] * nl
        for gi, w in enumerate(POOL_WINDOWS):
            lo, hi = gi * HEAD, (gi + 1) * HEAD
            dd = _nn(dpre_h[:, lo:hi].astype(BF16), pwt_ref[gi])
            ddc_ext[tm:, lo:hi] = dd / _pool_counts(i * tm + tm, HALO, w)

        def p2(s, carry):
            base = pl.multiple_of(s * HALO, HALO)
            dy_m = dy_ext[0, pl.ds(base, HALO), :]
            da = jnp.zeros((HALO, C), F32)
            for j in range(CONV_WIDTH):
                sh = CONV_WIDTH - 1 - j
                off = 2 + j
                da = da + cw_ref[pl.ds(j, 1), :] * dy_ext[sh % 8, pl.ds(pl.multiple_of(base + sh // 8 * 8, 8), HALO), :]
                a_j = a_ext[off % 8, pl.ds(pl.multiple_of(base + off // 8 * 8, 8), HALO), :]
                dcw_acc[j] += jnp.sum((dy_m * a_j).reshape(HALO // 8, 8, C), axis=0)
            v = val_ref[pl.ds(base, HALO), :]
            g = gate_ref[pl.ds(base, HALO), :]
            sg = jax.nn.sigmoid(g)
            du_ref[pl.ds(base, HALO), 0:C] = (da * sg).astype(BF16)
            du_ref[pl.ds(base, HALO), C:2 * C] = (da * v * sg * (1.0 - sg)).astype(BF16)
            pwin = p_ext[pl.ds(base, 2 * HALO), :]
            cwin = ddc_ext[pl.ds(base, 2 * HALO), :]
            for gi, w in enumerate(POOL_WINDOWS):
                lo, hi = gi * HEAD, (gi + 1) * HEAD
                x = pwin[HALO:, lo:hi]
                tot = x
                back = cwin[0:HALO, lo:hi]
                for k in range(1, w):
                    tot = tot + pwin[HALO - k:2 * HALO - k, lo:hi]
                    back = back + cwin[k:k + HALO, lo:hi]
                cnt = _pool_counts(i * tm + base, HALO, w)
                d_buf[pl.ds(base, HALO), lo:hi] = (tot / cnt - x).astype(BF16)
                du_ref[pl.ds(base, HALO), 2 * C + lo:2 * C + hi] = (back - dd_buf[pl.ds(base, HALO), lo:hi]).astype(BF16)
            return carry

        lax.fori_loop(0, nsub, p2, 0)
        for gi in range(len(POOL_WINDOWS)):
            lo, hi = gi * HEAD, (gi + 1) * HEAD
            pre = _nn(d_buf[:, lo:hi], pw_ref[gi]) + pb_ref[:, lo:hi]
            vec_acc[3, :, lo:hi] += jnp.sum((dyb_ref[:, lo:hi] * pre).reshape(tm // 8, 8, HEAD), axis=0)
            dpw_ref[gi] += _tn(d_buf[:, lo:hi], dpre_buf[:, lo:hi])
        for j in range(CONV_WIDTH):
            dcw_ref[pl.ds(j, 1), :] += jnp.sum(dcw_acc[j], axis=0, keepdims=True)
        for r in range(5):
            dvec_ref[pl.ds(r, 1), :] += jnp.sum(vec_acc[r], axis=0, keepdims=True)

    def main(c, width=C):
        return pl.BlockSpec((tm, width), lambda i: (i, c))

    def prev(c):
        return pl.BlockSpec((HALO, C), lambda i: (jnp.maximum(i * hb - 1, 0), c))

    def nxt(c):
        return pl.BlockSpec((HALO, C), lambda i: (jnp.minimum((i + 1) * hb, T // HALO - 1), c))

    vec = pl.BlockSpec((1, C), lambda i: (0, 0))
    mat = pl.BlockSpec((4, HEAD, HEAD), lambda i: (0, 0, 0))
    return pl.pallas_call(
        body, grid=(nblk,),
        in_specs=[main(0), main(1), nxt(0), nxt(1), main(0), nxt(0), main(0), main(1), main(2), prev(0), prev(1),
                  prev(2), pl.BlockSpec((32, C), lambda i: (0, 0)), vec, vec, mat, mat, vec, vec],
        out_specs=[pl.BlockSpec((tm, 3 * C), lambda i: (i, 0)), pl.BlockSpec((32, C), lambda i: (0, 0)),
                   pl.BlockSpec((8, C), lambda i: (0, 0)), mat],
        out_shape=[jax.ShapeDtypeStruct((T, 3 * C), BF16), jax.ShapeDtypeStruct((32, C), F32),
                   jax.ShapeDtypeStruct((8, C), F32), jax.ShapeDtypeStruct((4, HEAD, HEAD), F32)],
        scratch_shapes=[pltpu.VMEM((8, E + 8, C), F32), pltpu.VMEM((8, E + 8, C), F32), pltpu.VMEM((E, C), F32),
                        pltpu.VMEM((E, C), F32), pltpu.VMEM((tm, C), F32), pltpu.VMEM((tm, C), BF16),
                        pltpu.VMEM((tm, C), BF16), pltpu.VMEM((32, 8, C), F32), pltpu.VMEM((8, 8, C), F32)],
        compiler_params=_cp("arbitrary"), name="ev_bwd")(
            dyab, dyab, dyab, dyab, yc, yc, u, u, u, u, u, u, cw, lg, lb, pw, pwt, pb, ps)


def _cumsum_rows(x, reverse=False):
    n = x.shape[0]
    rid = lax.broadcasted_iota(jnp.int32, (n, 1), 0)
    k = 1
    while k < n:
        if reverse:
            sh = jnp.where(rid < n - k, pltpu.roll(x, n - k, 0), 0.0)
        else:
            sh = jnp.where(rid >= k, pltpu.roll(x, k, 0), 0.0)
        x = x + sh
        k *= 2
    return x


def _hgrn_gates(qr, fr, lbv):
    sq = jax.nn.sigmoid(qr)
    sg = jax.nn.sigmoid(fr)
    fg = lbv + (1.0 - lbv) * sg
    return qr * sq, sq, sg, fg, 1.0 - fg, jnp.log(fg)


def _hgrn_fwd(u, lbv, gn):
    T = u.shape[0]
    H = 8
    RB = _tile(T, HGRN_ROWS, CHUNK)
    NC = RB // CHUNK
    NS = CHUNK // SUB

    HP = HGRN_HEADS_FWD
    W = HP * HEAD

    def body(q_ref, f_ref, i_ref, g_ref, lb_ref, gn_ref, y_ref, o_ref, s0_ref, st, qs, ks, bs, vs, os_):
        rb = pl.program_id(1)

        @pl.when(rb == 0)
        def _():
            st[...] = jnp.zeros_like(st)

        t8 = lax.broadcasted_iota(jnp.int32, (8, 1), 0)

        def head(hh, c, rows):
            sl = slice(hh * HEAD, (hh + 1) * HEAD)
            q, _, _, _, kk, lf = _hgrn_gates(q_ref[rows, sl], f_ref[rows, sl], lb_ref[:, sl])
            v = i_ref[rows, sl]
            b = _cumsum_rows(lf)
            qs[hh] = q
            ks[hh] = kk
            bs[hh] = b
            vs[hh] = v
            st0 = st[hh]
            s0_ref[hh, c] = st0
            os_[hh] = _nt((q * jnp.exp(b)).astype(BF16), st0.astype(BF16))
            for I in range(NS):
                lo = I * SUB
                qI = qs[hh, lo:lo + SUB, :]
                bI = bs[hh, lo:lo + SUB, :]
                oI = jnp.zeros((SUB, HEAD), F32)
                if I > 0:
                    bprev = bs[hh, pl.ds(lo - 1, 1), :]
                    qt = _r16(qI * jnp.exp(bI - bprev))
                    kt = _r16(ks[hh, 0:lo, :] * jnp.exp(bprev - bs[hh, 0:lo, :]))
                    A = _nt(qt, kt)
                    oI = oI + _nn(_r16(A), _r16(vs[hh, 0:lo, :]))
                od = [jnp.zeros((8, HEAD), F32) for _ in range(SUB // 8)]
                for s in range(SUB):
                    row = pl.ds(lo + s, 1)
                    brow, krow, vrow = bs[hh, row, :], ks[hh, row, :], vs[hh, row, :]
                    for ti in range(SUB // 8):
                        o8 = 8 * ti
                        if s > o8 + 7:
                            continue
                        d = bI[o8:o8 + 8] - brow
                        if s > o8:
                            d = jnp.where(t8 >= s - o8, d, NEG)
                        col = jnp.sum(qI[o8:o8 + 8] * jnp.exp(d) * krow, axis=1, keepdims=True)
                        od[ti] = od[ti] + col * vrow
                os_[hh, lo:lo + SUB, :] += oI + jnp.concatenate(od, axis=0)
            blast = bs[hh, pl.ds(CHUNK - 1, 1), :]
            kh = kk * jnp.exp(blast - b)
            st[hh] = st0 * jnp.exp(blast) + _tn(v.astype(BF16), kh.astype(BF16))
            o = os_[hh]
            o_ref[rows, sl] = o
            rr = lax.rsqrt(jnp.mean(o * o, axis=-1, keepdims=True) + EPS)
            gr = g_ref[rows, sl]
            y_ref[rows, sl] = (((o * rr) * gn_ref[...]) * (gr * jax.nn.sigmoid(gr))).astype(BF16)

        def chunk(c, carry):
            rows = pl.ds(pl.multiple_of(c * CHUNK, CHUNK), CHUNK)
            for hh in range(HP):
                head(hh, c, rows)
            return carry

        lax.fori_loop(0, NC, chunk, 0)

    def blk(q):
        return pl.BlockSpec((RB, W), lambda h, r: (r, q * (H // HP) + h))

    sc = lambda: pltpu.VMEM((HP, CHUNK, HEAD), F32)
    return pl.pallas_call(
        body, grid=(H // HP, T // RB),
        in_specs=[blk(0), blk(1), blk(2), blk(3), pl.BlockSpec((1, W), lambda h, r: (0, h)),
                  pl.BlockSpec((1, HEAD), lambda h, r: (0, 0))],
        out_specs=[pl.BlockSpec((RB, W), lambda h, r: (r, h)), pl.BlockSpec((RB, W), lambda h, r: (r, h)),
                   pl.BlockSpec((HP, NC, HEAD, HEAD), lambda h, r: (h, r, 0, 0))],
        out_shape=[jax.ShapeDtypeStruct((T, H * HEAD), BF16), jax.ShapeDtypeStruct((T, H * HEAD), F32),
                   jax.ShapeDtypeStruct((H, T // CHUNK, HEAD, HEAD), F32)],
        scratch_shapes=[pltpu.VMEM((HP, HEAD, HEAD), F32), sc(), sc(), sc(), sc(), sc()],
        compiler_params=_cp("parallel", "arbitrary"), name="hgrn_fwd")(u, u, u, u, lbv, gn)


def _hgrn_bwd(dy, o, s0, u, lbv, gn):
    T = u.shape[0]
    H = 8
    RB = _tile(T, HGRN_ROWS, CHUNK)
    NB = T // RB
    NC = RB // CHUNK
    NS = CHUNK // SUB

    def body(q_ref, f_ref, i_ref, g_ref, lb_ref, gn_ref, o_ref, dy_ref, s0_ref, du_ref, dlb_ref, dgn_ref,
             dst, qs, ks, bs, vs, dos, dqs, dks, dki, dvs, dbs):
        rb = pl.program_id(1)

        @pl.when(rb == 0)
        def _():
            dst[...] = jnp.zeros_like(dst)
            dlb_ref[...] = jnp.zeros_like(dlb_ref)
            dgn_ref[...] = jnp.zeros_like(dgn_ref)

        t8 = lax.broadcasted_iota(jnp.int32, (8, 1), 0)
        lane = lax.broadcasted_iota(jnp.int32, (8, HEAD), 1)
        gnv = gn_ref[...]

        def head(hh, c, rows):
            sl = slice(hh * HEAD, (hh + 1) * HEAD)
            lbv_ = lb_ref[:, sl]
            qr = q_ref[rows, sl]
            q, sq, sg, fg, kk, lf = _hgrn_gates(qr, f_ref[rows, sl], lbv_)
            v = i_ref[rows, sl]
            gr = g_ref[rows, sl]
            b = _cumsum_rows(lf)
            eb = jnp.exp(b)
            ov = o_ref[rows, sl]
            dyv = dy_ref[rows, sl]
            rr = lax.rsqrt(jnp.mean(ov * ov, axis=-1, keepdims=True) + EPS)
            oh = ov * rr
            gs = jax.nn.sigmoid(gr)
            dgr = dyv * (oh * gnv) * _dsilu(gr, gs)
            dnrm = dyv * (gr * gs)
            dgn_ref[hh] += jnp.sum(dnrm * oh, axis=0, keepdims=True)
            t1 = dnrm * gnv
            do = rr * (t1 - oh * jnp.mean(t1 * oh, axis=-1, keepdims=True))
            qs[hh] = q
            ks[hh] = kk
            bs[hh] = b
            vs[hh] = v
            dos[hh] = do
            st0 = s0_ref[hh, c]
            dS = dst[hh]
            do_b = do.astype(BF16)
            blast = bs[hh, pl.ds(CHUNK - 1, 1), :]
            elast = jnp.exp(blast - b)
            dq_inter = _nn(do_b, st0.astype(BF16)) * eb
            dqs[hh] = dq_inter
            dbs[hh] = q * dq_inter
            kh = kk * elast
            dvs[hh] = _nt(kh.astype(BF16), dS.astype(BF16))
            dk_inter = _nn(v.astype(BF16), dS.astype(BF16)) * elast
            dki[hh] = dk_inter
            dks[hh] = jnp.zeros((CHUNK, HEAD), F32)
            for I in range(NS):
                lo = I * SUB
                qI = qs[hh, lo:lo + SUB, :]
                bI = bs[hh, lo:lo + SUB, :]
                doI = dos[hh, lo:lo + SUB, :]
                dqI = jnp.zeros((SUB, HEAD), F32)
                dbI = jnp.zeros((SUB, HEAD), F32)
                if I > 0:
                    bprev = bs[hh, pl.ds(lo - 1, 1), :]
                    eq = jnp.exp(bI - bprev)
                    ek = jnp.exp(bprev - bs[hh, 0:lo, :])
                    qt = _r16(qI * eq)
                    kt = _r16(ks[hh, 0:lo, :] * ek)
                    A = _r16(_nt(qt, kt))
                    doI_b = _r16(doI)
                    dA = _r16(_nt(doI_b, _r16(vs[hh, 0:lo, :])))
                    dvs[hh, 0:lo, :] += _tn(A, doI_b)
                    dqt = _nn(dA, kt)
                    dkt = _tn(dA, qt)
                    dqI = dqI + dqt * eq
                    dbI = dbI + qt.astype(F32) * dqt
                    dks[hh, 0:lo, :] += dkt * ek
                    dbs[hh, 0:lo, :] -= kt.astype(F32) * dkt
                dq_t = [jnp.zeros((8, HEAD), F32) for _ in range(SUB // 8)]
                a_t = [jnp.zeros((8, HEAD), F32) for _ in range(SUB // 8)]
                for s in range(SUB):
                    row = pl.ds(lo + s, 1)
                    brow, krow, vrow = bs[hh, row, :], ks[hh, row, :], vs[hh, row, :]
                    dk_s = None
                    for ti in range(SUB // 8):
                        o8 = 8 * ti
                        if s > o8 + 7:
                            continue
                        d = bI[o8:o8 + 8] - brow
                        if s > o8:
                            d = jnp.where(t8 >= s - o8, d, NEG)
                        Es = jnp.exp(d)
                        qE = qI[o8:o8 + 8] * Es
                        col = jnp.sum(qE * krow, axis=1, keepdims=True)
                        a_t[ti] = jnp.where(lane == s, col, a_t[ti])
                        dcol = jnp.sum(doI[o8:o8 + 8] * vrow, axis=1, keepdims=True)
                        dq_t[ti] = dq_t[ti] + (dcol * Es) * krow
                        part = jnp.sum(dcol * qE, axis=0, keepdims=True)
                        dk_s = part if dk_s is None else dk_s + part
                    dks[hh, row, :] += dk_s
                    dbs[hh, row, :] -= krow * dk_s
                a_d = jnp.concatenate(a_t, axis=0)
                dq_d = jnp.concatenate(dq_t, axis=0)
                dvs[hh, lo:lo + SUB, :] += _tn(a_d, doI)[0:SUB]
                dqI = dqI + dq_d
                dbI = dbI + qI * dq_d
                dqs[hh, lo:lo + SUB, :] += dqI
                dbs[hh, lo:lo + SUB, :] += dbI
            kdk = kk * dki[hh]
            excl = _cumsum_rows(kdk) - kdk
            suff = _cumsum_rows(dbs[hh], reverse=True)
            gdec = jnp.sum(dS * st0, axis=0, keepdims=True) * jnp.exp(blast)
            dlf = suff + excl + gdec
            dk = dks[hh] + dki[hh]
            dfg = dlf / fg - dk
            dlb_ref[:, sl] += jnp.sum(dfg * (1.0 - sg), axis=0, keepdims=True)
            du_ref[0, rows, sl] = (dqs[hh] * _dsilu(qr, sq)).astype(BF16)
            du_ref[1, rows, sl] = (dfg * (1.0 - lbv_) * sg * (1.0 - sg)).astype(BF16)
            du_ref[2, rows, sl] = dvs[hh].astype(BF16)
            du_ref[3, rows, sl] = dgr.astype(BF16)
            dst[hh] = dS * jnp.exp(blast) + _tn(do_b, (q * eb).astype(BF16))

        def chunk(cc, carry):
            c = NC - 1 - cc
            rows = pl.ds(pl.multiple_of(c * CHUNK, CHUNK), CHUNK)
            for hh in range(HP):
                head(hh, c, rows)
            return carry

        lax.fori_loop(0, NC, chunk, 0)

    HP = HGRN_HEADS_BWD
    W = HP * HEAD

    def blk(qd):
        return pl.BlockSpec((RB, W), lambda h, r: (NB - 1 - r, qd * (H // HP) + h))

    hblk = pl.BlockSpec((RB, W), lambda h, r: (NB - 1 - r, h))
    sc = lambda: pltpu.VMEM((HP, CHUNK, HEAD), F32)
    return pl.pallas_call(
        body, grid=(H // HP, NB),
        in_specs=[blk(0), blk(1), blk(2), blk(3), pl.BlockSpec((1, W), lambda h, r: (0, h)),
                  pl.BlockSpec((1, HEAD), lambda h, r: (0, 0)), hblk, hblk,
                  pl.BlockSpec((HP, NC, HEAD, HEAD), lambda h, r: (h, NB - 1 - r, 0, 0))],
        out_specs=[pl.BlockSpec((4, RB, W), lambda h, r: (0, NB - 1 - r, h)),
                   pl.BlockSpec((1, W), lambda h, r: (0, h)), pl.BlockSpec((HP, 1, HEAD), lambda h, r: (h, 0, 0))],
        out_shape=[jax.ShapeDtypeStruct((4, T, H * HEAD), BF16), jax.ShapeDtypeStruct((1, H * HEAD), F32),
                   jax.ShapeDtypeStruct((H, 1, HEAD), F32)],
        scratch_shapes=[pltpu.VMEM((HP, HEAD, HEAD), F32)] + [sc() for _ in range(10)],
        compiler_params=_cp("parallel", "arbitrary"), name="hgrn_bwd")(u, u, u, u, lbv, gn, o, dy, s0)


def _softmax_rows(p_ref, L):
    rows = [p_ref[pl.ds(l, 1), :] for l in range(L)]
    m = rows[0]
    for r in rows[1:]:
        m = jnp.maximum(m, r)
    e = [jnp.exp(r - m) for r in rows]
    tot = e[0]
    for t in e[1:]:
        tot = tot + t
    return [t / tot for t in e]


def _lb_fwd(lbp):
    L, D = lbp.shape

    def body(p_ref, o_ref):
        sm = _softmax_rows(p_ref, L)
        acc = jnp.zeros((1, D), F32)
        o_ref[pl.ds(0, 1), :] = acc
        for l in range(1, L):
            acc = acc + sm[l]
            o_ref[pl.ds(l, 1), :] = acc

    return pl.pallas_call(body, out_shape=jax.ShapeDtypeStruct((L, D), F32), name="lb_fwd")(lbp)


def _lb_bwd(lbp, dlb):
    L, D = lbp.shape

    def body(p_ref, d_ref, o_ref):
        sm = _softmax_rows(p_ref, L)
        dsm = [jnp.zeros((1, D), F32)]
        for i in range(1, L):
            t = jnp.zeros((1, D), F32)
            for l in range(i, L):
                t = t + d_ref[pl.ds(l, 1), :]
            dsm.append(t)
        dot = jnp.zeros((1, D), F32)
        for i in range(L):
            dot = dot + dsm[i] * sm[i]
        for i in range(L):
            o_ref[pl.ds(i, 1), :] = sm[i] * (dsm[i] - dot)

    return pl.pallas_call(body, out_shape=jax.ShapeDtypeStruct((L, D), F32), name="lb_bwd")(lbp, dlb)


def _my_pos():
    return lax.axis_index("x"), lax.axis_index("y"), lax.axis_index("c")


def _peer(mask):
    x, y, c = _my_pos()
    mx, my, mc = (mask >> 2) & 1, (mask >> 1) & 1, mask & 1
    px = (1 - x) if mx else x
    py = (1 - y) if my else y
    pc = (1 - c) if mc else c
    return (px, py, pc), 4 * px + 2 * py + pc


HBM_SPEC =pl.BlockSpec(memory_space=pltpu.HBM)
SEM_SPEC = pl.BlockSpec(memory_space=pltpu.SEMAPHORE)
EFFECT = pltpu.SideEffectType.DATAFLOW_SIDE_EFFECTING


def _hbm(a):
    return pltpu.with_memory_space_constraint(a, pltpu.HBM)


def _landing(block_shape, dtype, axis=0):
    if axis == 0:
        return lax.empty((N_DEV,) + tuple(block_shape), dtype)
    rows, n = block_shape
    return lax.empty((rows, N_DEV * n), dtype)


def _slot(ref, i):
    if len(ref.shape) == 2:
        n = ref.shape[1] // N_DEV
        return ref.at[:, pl.ds(i * n, n)]
    return ref.at[i]


def _push_start(name, srcs, lands, whole, groups):
    n = len(srcs)
    ng = 1 + max(groups)
    cnt = [groups.count(g) for g in range(ng)]
    idx = [groups[:a].count(groups[a]) for a in range(n)]

    def body(*refs):
        src_refs, land_refs = refs[:n], refs[n:2 * n]
        sems = refs[2 * n:2 * n + 3 * ng]
        token = refs[-1]
        x, y, c = _my_pos()
        me = 4 * x + 2 * y + c
        for a in range(n):
            g = groups[a]
            for m in range(1, N_DEV):
                peer, pid = _peer(m)
                pltpu.make_async_remote_copy(
                    src_ref=src_refs[a] if whole else src_refs[a].at[pid], dst_ref=_slot(land_refs[a], me),
                    send_sem=sems[3 * g].at[idx[a] * (N_DEV - 1) + m - 1],
                    recv_sem=sems[3 * g + 1].at[idx[a] * (N_DEV - 1) + m - 1],
                    device_id=peer, device_id_type=MESH).start()
            pltpu.make_async_copy(src_refs[a] if whole else src_refs[a].at[me], _slot(land_refs[a], me),
                                  sems[3 * g + 2].at[idx[a]]).start()
        token[...] = jnp.zeros_like(token)

    sem_shapes = []
    for g in range(ng):
        sem_shapes += [pltpu.SemaphoreType.DMA((cnt[g] * (N_DEV - 1),))] * 2 + [pltpu.SemaphoreType.DMA((cnt[g],))]
    thru = [pltpu.HBM(s.shape, s.dtype) for s in list(srcs) + list(lands)]
    res = pl.pallas_call(
        body, name=name,
        out_shape=tuple(sem_shapes + thru + [jax.ShapeDtypeStruct((8, 128), F32)]),
        in_specs=tuple([HBM_SPEC] * (2 * n)),
        out_specs=tuple([SEM_SPEC] * (3 * ng) + [HBM_SPEC] * (2 * n) + [pl.BlockSpec(memory_space=pltpu.VMEM)]),
        input_output_aliases={i: 3 * ng + i for i in range(2 * n)},
        compiler_params=pltpu.CompilerParams(has_side_effects=EFFECT),
    )(*[_hbm(s) for s in srcs], *[_hbm(z) for z in lands])
    sems = [(res[3 * g], res[3 * g + 1], res[3 * g + 2]) for g in range(ng)]
    srcs_thru = list(res[3 * ng:3 * ng + n])
    lands_thru = list(res[3 * ng + n:3 * ng + 2 * n])
    return sems, srcs_thru, lands_thru, res[-1]


def _push_wait(name, srcs_thru, lands_thru, sems, after, whole):
    n = len(srcs_thru)

    def body(*refs):
        src_refs, land_refs = refs[:n], refs[n:2 * n]
        send_sems, recv_sems, own_sems = refs[2 * n], refs[2 * n + 1], refs[2 * n + 2]
        x, y, c = _my_pos()
        me = 4 * x + 2 * y + c
        for a in range(n):
            pltpu.make_async_copy(src_refs[a] if whole else src_refs[a].at[me], _slot(land_refs[a], me),
                                  own_sems.at[a]).wait()
            for m in range(1, N_DEV):
                peer, pid = _peer(m)
                cp = pltpu.make_async_remote_copy(
                    src_ref=src_refs[a] if whole else src_refs[a].at[pid], dst_ref=_slot(land_refs[a], pid),
                    send_sem=send_sems.at[a * (N_DEV - 1) + m - 1], recv_sem=recv_sems.at[a * (N_DEV - 1) + m - 1],
                    device_id=peer, device_id_type=MESH)
                cp.wait_send()
                cp.wait_recv()

    thru = [pltpu.HBM(s.shape, s.dtype) for s in list(srcs_thru) + list(lands_thru)]
    res = pl.pallas_call(
        body, name=name, out_shape=tuple(thru),
        in_specs=tuple([HBM_SPEC] * (2 * n) + [SEM_SPEC, SEM_SPEC, SEM_SPEC, ANY]),
        out_specs=tuple([HBM_SPEC] * (2 * n)),
        input_output_aliases={i: i for i in range(2 * n)},
        compiler_params=pltpu.CompilerParams(has_side_effects=EFFECT),
    )(*srcs_thru, *lands_thru, sems[0], sems[1], sems[2], after)
    return list(res[n:])


def _adamw(recv, w, m, v, layer=0, prev=None):
    L, R, C = w.shape
    tr = _tile(R, max(8, (1 << 18) // C), 8) if R % 8 == 0 else R
    bc1 = 1.0 - ADAM_B1 ** ADAM_STEP
    bc2 = 1.0 - ADAM_B2 ** ADAM_STEP
    if prev is None:
        prev = [lax.empty((L, R, C), F32) for _ in range(4)]

    def body(r_ref, w_ref, m_ref, v_ref, g_ref, d_ref, nm_ref, nv_ref):
        g = r_ref[0].astype(F32)
        for s in range(1, N_DEV):
            g = g + r_ref[s].astype(F32)
        nm = ADAM_B1 * m_ref[...] + (1.0 - ADAM_B1) * g
        nv = ADAM_B2 * v_ref[...] + (1.0 - ADAM_B2) * (g * g)
        mh = nm / bc1
        vh = nv / bc2
        g_ref[...] = g
        d_ref[...] = -ADAM_LR * (mh / (jnp.sqrt(vh) + ADAM_EPS) + ADAM_WD * w_ref[...])
        nm_ref[...] = nm
        nv_ref[...] = nv

    row = pl.BlockSpec((None, tr, C), lambda i: (layer, i, 0))
    row_in = pl.BlockSpec((None, tr, C), lambda i: (layer, i, 0), pipeline_mode=pl.Buffered(3))
    r_in = pl.BlockSpec((N_DEV, tr, C), lambda i: (0, i, 0), pipeline_mode=pl.Buffered(3))

    def outer(r_hbm, w_hbm, m_hbm, v_hbm, p0, p1, p2, p3, g_hbm, d_hbm, nm_hbm, nv_hbm):
        pltpu.emit_pipeline(body, grid=(R // tr,), in_specs=[r_in, row_in, row_in, row_in], out_specs=[row] * 4)(
            r_hbm, w_hbm, m_hbm, v_hbm, g_hbm, d_hbm, nm_hbm, nv_hbm)

    return pl.pallas_call(
        outer, in_specs=[ANY] * 8, out_specs=[ANY] * 4, out_shape=[jax.ShapeDtypeStruct((L, R, C), F32)] * 4,
        input_output_aliases={4: 0, 5: 1, 6: 2, 7: 3},
        compiler_params=pltpu.CompilerParams(vmem_limit_bytes=VMEM_LIMIT), name="adamw")(recv, w, m, v, *prev)


def _full_w_spec(tk, tn):
    return pl.BlockSpec((tk, tn), lambda i, j, k: (k, j))


def kernel(x, meta_tokens, mix_norm_g, mlp_norm_g, final_norm_g, ev_w_in, ev_conv_w, ev_conv_b, ev_ln_g, ev_ln_b, ev_pool_w, ev_pool_b, ev_pool_scale, ev_w_out, od_w_in, od_gnorm_g, od_w_out, lb_param, mlp_w1, mlp_w2, loss_target, m_meta_tokens, m_mix_norm_g, m_mlp_norm_g, m_final_norm_g, m_ev_w_in, m_ev_conv_w, m_ev_conv_b, m_ev_ln_g, m_ev_ln_b, m_ev_pool_w, m_ev_pool_b, m_ev_pool_scale, m_ev_w_out, m_od_w_in, m_od_gnorm_g, m_od_w_out, m_lb_param, m_mlp_w1, m_mlp_w2, v_meta_tokens, v_mix_norm_g, v_mlp_norm_g, v_final_norm_g, v_ev_w_in, v_ev_conv_w, v_ev_conv_b, v_ev_ln_g, v_ev_ln_b, v_ev_pool_w, v_ev_pool_b, v_ev_pool_scale, v_ev_w_out, v_od_w_in, v_od_gnorm_g, v_od_w_out, v_lb_param, v_mlp_w1, v_mlp_w2):
    S, D = x.shape[1], x.shape[2]
    T = PAD + N_META + S
    DEPTH = mix_norm_g.shape[0]
    DFF = mlp_w1.shape[2] * N_DEV
    dev = 4 * lax.axis_index("x") + 2 * lax.axis_index("y") + lax.axis_index("c")

    n_ev = ev_w_in.shape[0]
    n_od = od_w_in.shape[0]
    n_in_od = od_w_in.shape[2]
    n_w1 = mlp_w1.shape[2]

    ag_src, ag_grp, ag_axis, ag_at = [], [], [], {}
    for key, arr in (("meta", meta_tokens), ("cw", ev_conv_w)):
        ag_at["small", key] = len(ag_src)
        ag_src.append(arr)
        ag_grp.append(len(ag_grp))
        ag_axis.append(0)
    for layer in range(DEPTH):
        j = layer // 2
        mixer = [("in", ev_w_in[j]), ("out", ev_w_out[j])] if layer % 2 == 0 else [("in", od_w_in[j]), ("out", od_w_out[j])]
        for pos, (key, arr) in enumerate(mixer + [("w1", mlp_w1[layer]), ("w2", mlp_w2[layer])]):
            ag_at[layer, key] = len(ag_src)
            ag_src.append(arr.astype(BF16))
            ag_grp.append(len(ag_grp))
            ag_axis.append(1 if key in ("in", "w1") and arr.shape[1] % 128 == 0 else 0)
    ag_sems, ag_s, ag_l, ag_tok = _push_start(
        "ag_start", ag_src, [_landing(s_.shape, s_.dtype, ax) for s_, ax in zip(ag_src, ag_axis)], True, ag_grp)

    def ag_wait(layer, key, after):
        a = ag_at[layer, key]
        return _push_wait(f"ag_wait_{a}", [ag_s[a]], [ag_l[a]], ag_sems[a], after, True)[0]

    g_meta = ag_wait("small", "meta", ag_tok)
    g_cw = ag_wait("small", "cw", ag_tok)
    meta_full = jnp.transpose(g_meta, (1, 0, 2)).reshape(N_META, D)
    cw_full = jnp.transpose(g_cw, (1, 2, 0, 3)).reshape(n_ev, CONV_WIDTH, -1)
    cw_pad = jnp.pad(cw_full, ((0, 0), (0, 32 - CONV_WIDTH), (0, 0)))

    h = jnp.concatenate([jnp.zeros((PAD, D), F32), meta_full, x[0]], axis=0) + ag_tok[0, 0]
    tgt = jnp.pad(loss_target[0], ((PAD + N_META, 0), (0, 0)))
    lb_all = _lb_fwd(lb_param)
    h, tgt, lb_all = lax.optimization_barrier((h, tgt, lb_all))

    tm_big = _tile(T, MM_ROWS_BIG, 16)
    tm_mid = _tile(T, MM_ROWS_MID, 16)
    tm_k4 = _tile(T, MM_ROWS_K4, 16)

    saved = []
    for layer in range(DEPTH):
        j = layer // 2
        sv = {"h0": h}
        g_in = ag_wait(layer, "in", h)
        w_in = g_in if g_in.ndim == 2 else jnp.transpose(g_in, (1, 0, 2)).reshape(D, -1)
        if layer % 2 == 0:
            sv["n"], u = _mm_rms_nn("ev_in", h, mix_norm_g[layer][None], w_in, tm_big, 512, "f32")
            yab, yc = _ev_fwd(u, cw_pad[j], ev_conv_b[j][None], ev_ln_g[j][None], ev_ln_b[j][None],
                              ev_pool_w[j].astype(BF16), ev_pool_b[j].reshape(1, -1), ev_pool_scale[j][None])
            sv.update(u=u, y=yab, yc=yc)
            w_out = ag_wait(layer, "out", yab).reshape(-1, D)
            h = _mm_nn("ev_out", yab, w_out, _full_w_spec, T, D, D, tm_mid, D, D, "resid", extra=h)
        else:
            sv["n"], u = _mm_rms_nn("od_in", h, mix_norm_g[layer][None], w_in, tm_big, 512, "f32")
            y, o, s0 = _hgrn_fwd(u, lb_all[layer][None], od_gnorm_g[j][None])
            sv.update(u=u, y=y, o=o, s0=s0)
            w_out = ag_wait(layer, "out", y).reshape(-1, D)
            h = _mm_nn("od_out", y, w_out, _full_w_spec, T, D, D, tm_mid, D, D, "resid", extra=h)
        sv["h1"] = h
        w_w1 = ag_wait(layer, "w1", h)
        n2, r, act = _mm_rms_nn("mlp_w1", h, mlp_norm_g[layer][None], w_w1, tm_big, 512, "relu2")
        w_w2 = ag_wait(layer, "w2", act).reshape(DFF, D)
        sv.update(w_in=w_in, w_out=w_out, w_w1=w_w1, w_w2=w_w2)
        sv.update(n2=n2, r=r, act=act)
        h = _mm_nn("mlp_w2", act, w_w2, _full_w_spec, T, D, DFF, tm_k4, D, DFF, "resid", extra=h)
        saved.append(sv)

    loss_blk, dh, dhb, dg_final = _loss_head(h, final_norm_g[None], tgt)
    loss = lax.psum(loss_blk[0, 0], AXES)

    tt = T
    g_mix, g_mlp = [None] * DEPTH, [None] * DEPTH
    small ={"cw": [None] * n_ev, "vec": [None] * n_ev, "pw": [None] * n_ev, "gn": [None] * n_od}
    dlb_rows = [jnp.zeros((1, D), F32) for _ in range(DEPTH)]

    def xs2(tt_, tk):
        return pl.BlockSpec((tt_, tk), lambda a, b, t: (t, a))

    def ys2(tt_, tn):
        return pl.BlockSpec((tt_, tn), lambda a, b, t: (t, b))

    def os2(tk, tn):
        return pl.BlockSpec((tk, tn), lambda a, b, t: (a, b))

    def os3(tk, tn):
        return pl.BlockSpec((None, tk, tn), lambda a, b, t: (b, a, 0))

    def dy2(tm, tn):
        return pl.BlockSpec((tm, tn), lambda i, jj, k: (i, k))

    def w_rows(tj, tn):
        return pl.BlockSpec((tj, tn), lambda i, jj, k: (jj, k))

    def w_whole(tj, tn):
        return pl.BlockSpec((tj, tn), lambda i, jj, k: (0, 0), pipeline_mode=pl.Buffered(1))

    rs_pending = []

    def rs_start(tag, mats):
        blocks = [m_ if m_.ndim == 3 else m_.reshape(N_DEV, m_.shape[0] // N_DEV, m_.shape[1]) for m_ in mats]
        lands = [_landing(b_.shape[1:], b_.dtype) for b_ in blocks]
        sems, s_thru, l_thru, tok = _push_start(f"rs_start_{tag}", blocks, lands, False, [0] * len(blocks))
        rs_pending.append((tag, s_thru, l_thru, sems[0]))
        return tok[0, 0]

    for layer in reversed(range(DEPTH)):
        j = layer // 2
        sv = saved[layer]
        da1 = _mm_nt("mlp_w2_t", dhb, sv["w_w2"], dy2, w_rows, T, DFF, D, tm_big, 512, D, "dact", extra=sv["r"])
        dw2 = _mm_tn("mlp_dw2", sv["act"], dhb, xs2, ys2, os2, (DFF, D), T, DFF, D, tt, 512, D)
        dw1 = _mm_tn("mlp_dw1", sv["n2"], da1, xs2, ys2, os3, (N_DEV, D, n_w1), T, D, DFF, tt, D, n_w1)
        held = [dw1, dw2] if layer > 0 else []
        tok = 0.0 if layer > 0 else rs_start(f"mlp{layer}", [dw1, dw2])
        dh, dhb, g_mlp[layer] = _mm_nt("mlp_w1_t", da1, sv["w_w1"], dy2, w_whole, T, D, DFF, tm_k4, D, DFF, "rms",
                                       extra=(sv["h1"], mlp_norm_g[layer][None] + tok, dh))
        if layer % 2 == 0:
            dyab = _mm_nt("ev_out_t", dhb, sv["w_out"], dy2, w_rows, T, D, D, tm_mid, D, D, "f32")
            dwout = _mm_tn("ev_dwout", sv["y"], dhb, xs2, ys2, os2, (D, D), T, D, D, tt, 512, D)
            du, small["cw"][j], small["vec"][j], small["pw"][j] = _ev_bwd(
                dyab, sv["yc"], sv["u"], cw_pad[j], ev_ln_g[j][None], ev_ln_b[j][None], ev_pool_w[j].astype(BF16),
                jnp.transpose(ev_pool_w[j], (0, 2, 1)).astype(BF16), ev_pool_b[j].reshape(1, -1),
                ev_pool_scale[j][None])
            nin = du.shape[1]
            dwin = _mm_tn("ev_dwin", sv["n"], du, xs2, ys2, os2, (D, nin), T, D, nin, tt, D, 512)
            dwin = jnp.transpose(dwin.reshape(D, N_DEV, nin // N_DEV), (1, 0, 2))
            tok = rs_start(f"mix{layer}", [dwin, dwout] + held)
            dh, dhb, g_mix[layer] = _mm_nt("ev_in_t", du, sv["w_in"], dy2, w_whole, T, D, nin, tm_k4, D, nin, "rms",
                                           extra=(sv["h0"], mix_norm_g[layer][None] + tok, dh))
        else:
            dy = _mm_nt("od_out_t", dhb, sv["w_out"], dy2, w_rows, T, D, D, tm_mid, D, D, "f32")
            dwout = _mm_tn("od_dwout", sv["y"], dhb, xs2, ys2, os2, (D, D), T, D, D, tt, 512, D)
            du3, dlb_rows[layer], small["gn"][j] = _hgrn_bwd(dy, sv["o"], sv["s0"], sv["u"], lb_all[layer][None],
                                                              od_gnorm_g[j][None])
            per = D // n_in_od

            def du_t(tt_, tn):
                return pl.BlockSpec((None, tt_, tn), lambda a, b, t: (b // per, t, b % per))

            dwin = _mm_tn("od_dwin", sv["n"], du3, xs2, du_t, os3, (N_DEV, D, n_in_od), T, D, 4 * D, tt, D, n_in_od)
            tok = rs_start(f"mix{layer}", [dwin, dwout] + held)
            dh, dhb, g_mix[layer] = _mm_nt(
                "od_in_t", du3, sv["w_in"], lambda tm, tn: pl.BlockSpec((4, tm, tn // 4), lambda i, jj, k: (0, i, 0)),
                w_whole, T, D, 4 * D, tm_k4, D, 4 * D, "rms", extra=(sv["h0"], mix_norm_g[layer][None] + tok, dh),
                parts=4)

    dmeta = dh[PAD:PAD + N_META]
    grad_x = dh[PAD + N_META:][None]
    dlb_param = _lb_bwd(lb_param, jnp.concatenate(dlb_rows, axis=0))

    pieces = [
        ("final", dg_final), ("pad", jnp.zeros((SMALL_F32_ROWS - 1, D), F32)),
        ("meta", dmeta), ("mix", jnp.concatenate(g_mix, 0)), ("mlp", jnp.concatenate(g_mlp, 0)),
        ("cw", jnp.stack([c[:CONV_WIDTH] for c in small["cw"]])), ("cb", jnp.stack([v_[0] for v_ in small["vec"]])),
        ("lng", jnp.stack([v_[1] for v_ in small["vec"]])), ("lnb", jnp.stack([v_[2] for v_ in small["vec"]])),
        ("pw", jnp.stack(small["pw"])), ("pb", jnp.stack([v_[4] for v_ in small["vec"]])),
        ("ps", jnp.stack([v_[3] for v_ in small["vec"]])), ("gn", jnp.stack([jnp.sum(g_, axis=0)[0] for g_ in small["gn"]])),
        ("lb", dlb_param),
    ]
    flat = jnp.concatenate([p.reshape(-1) for _, p in pieces])
    n_small = flat.shape[0]
    rows_small = SMALL_F32_ROWS + -(-(n_small // 1024 + 1 - SMALL_F32_ROWS) // 16) * 16
    flat = jnp.pad(flat, (0, rows_small * 1024 - n_small)).reshape(rows_small, 1024)

    sm_src = [flat[:SMALL_F32_ROWS], flat[SMALL_F32_ROWS:].astype(BF16)]
    sm_sems, sm_s, sm_l, sm_tok = _push_start("small_start", sm_src, [_landing(a_.shape, a_.dtype) for a_ in sm_src],
                                              True, [0, 0])
    recv = {}
    for tag, s_thru, l_thru, sems in rs_pending:
        got = _push_wait(f"rs_wait_{tag}", s_thru, l_thru, sems, sm_tok, False)
        layer = int(tag[3:])
        if tag.startswith("mlp"):
            recv["w1", layer], recv["w2", layer] = got
        else:
            key = "ev" if layer % 2 == 0 else "od"
            recv[key + "_in", layer // 2], recv[key + "_out", layer // 2] = got[:2]
            if len(got) > 2:
                recv["w1", layer], recv["w2", layer] = got[2:]

    outs = {}
    big = {"ev_in": ("ev_w_in", ev_w_in, m_ev_w_in, v_ev_w_in), "ev_out": ("ev_w_out", ev_w_out, m_ev_w_out, v_ev_w_out),
           "od_in": ("od_w_in", od_w_in, m_od_w_in, v_od_w_in), "od_out": ("od_w_out", od_w_out, m_od_w_out, v_od_w_out),
           "w1": ("mlp_w1", mlp_w1, m_mlp_w1, v_mlp_w1), "w2": ("mlp_w2", mlp_w2, m_mlp_w2, v_mlp_w2)}
    for key, (name, w, m, v) in big.items():
        res = None
        for l in range(w.shape[0]):
            res = _adamw(recv[key, l], w, m, v, layer=l, prev=res)
        outs[name] = res

    small_params = {
        "meta": ("meta_tokens", None), "mix": ("mix_norm_g", mix_norm_g, m_mix_norm_g, v_mix_norm_g),
        "mlp": ("mlp_norm_g", mlp_norm_g, m_mlp_norm_g, v_mlp_norm_g),
        "final": ("final_norm_g", final_norm_g, m_final_norm_g, v_final_norm_g),
        "cw": ("ev_conv_w", None), "cb": ("ev_conv_b", ev_conv_b, m_ev_conv_b, v_ev_conv_b),
        "lng": ("ev_ln_g", ev_ln_g, m_ev_ln_g, v_ev_ln_g), "lnb": ("ev_ln_b", ev_ln_b, m_ev_ln_b, v_ev_ln_b),
        "pw": ("ev_pool_w", ev_pool_w, m_ev_pool_w, v_ev_pool_w), "pb": ("ev_pool_b", ev_pool_b, m_ev_pool_b, v_ev_pool_b),
        "ps": ("ev_pool_scale", ev_pool_scale, m_ev_pool_scale, v_ev_pool_scale),
        "gn": ("od_gnorm_g", od_gnorm_g, m_od_gnorm_g, v_od_gnorm_g), "lb": ("lb_param", lb_param, m_lb_param, v_lb_param),
    }
    csh = ev_conv_w.shape[2]
    msh = meta_tokens.shape[1]

    def packed(which):
        parts = []
        for key, g_ in pieces:
            ent = small_params.get(key)
            if key == "pad":
                full = g_
            elif key == "meta":
                src = (meta_tokens, m_meta_tokens, v_meta_tokens)[which]
                full = lax.dynamic_update_slice(jnp.zeros((N_META, D), F32), src, (0, dev * msh))
            elif key == "cw":
                src = (ev_conv_w, m_ev_conv_w, v_ev_conv_w)[which]
                full = lax.dynamic_update_slice(jnp.zeros(g_.shape, F32), src, (0, 0, dev * csh))
            else:
                full = ent[1 + which]
            parts.append(full.reshape(-1))
        f = jnp.concatenate(parts)
        return jnp.pad(f, (0, rows_small * 1024 - n_small)).reshape(rows_small, 1024)

    got_f32, got_bf16 = _push_wait("small_wait", sm_s, sm_l, sm_sems[0], outs["mlp_w2"][0], True)
    recv_small = jnp.concatenate([got_f32, got_bf16.astype(F32)], axis=1)
    sres = [r_[0] for r_ in _adamw(recv_small, packed(0)[None], packed(1)[None], packed(2)[None])]
    off = 0
    for key, g_ in pieces:
        size = g_.size
        vals = [r_.reshape(-1)[off:off + size].reshape(g_.shape) for r_ in sres]
        off += size
        if key == "pad":
            continue
        name = small_params[key][0]
        if key == "meta":
            vals = [lax.dynamic_slice(v_, (0, dev * msh), (N_META, msh)) for v_ in vals]
        elif key == "cw":
            vals = [lax.dynamic_slice(v_, (0, 0, dev * csh), v_.shape[:2] + (csh,)) for v_ in vals]
        else:
            vals = [v_.reshape(small_params[key][1].shape) for v_ in vals]
        outs[name] = vals

    names = ["meta_tokens", "mix_norm_g", "mlp_norm_g", "final_norm_g", "ev_w_in", "ev_conv_w", "ev_conv_b", "ev_ln_g",
             "ev_ln_b", "ev_pool_w", "ev_pool_b", "ev_pool_scale", "ev_w_out", "od_w_in", "od_gnorm_g", "od_w_out",
             "lb_param", "mlp_w1", "mlp_w2"]
    result = [loss, grad_x]
    for k in range(4):
        result += [outs[nm][k] for nm in names]
    return tuple(result)
```
